```python
import jax, jax.numpy as jnp
from jax import lax
import numpy as np

D_MODEL = 1024
BATCH = 8
SEQ = 2048
DEPTH = 1

MIX_WIDTH = D_MODEL
HEAD_DIM = 64
ATTN_WIDTH = MIX_WIDTH // 2
ATTN_HEADS = ATTN_WIDTH // HEAD_DIM
SGU_WIDTH = MIX_WIDTH - ATTN_WIDTH
SGU_GROUP_DIM = 64
SGU_GROUPS = SGU_WIDTH // SGU_GROUP_DIM
SGU_CHUNK = 128
DILATED_PATTERNS = ((128, 1), (512, 4), (2048, 16))
BLOCK_Q = 128
ROPE_THETA = 500000.0
ROT_DIM = HEAD_DIM // 4
ROT_HALF = ROT_DIM // 2
D_FF = ((8 * D_MODEL // 3 + 255) // 256) * 256
IN_PROJ_WIDTH = 3 * ATTN_WIDTH + 2 * SGU_WIDTH
RMS_EPS = 1e-6
LN_EPS = 1e-5

kernel_name = "hymba_dilated_attn_gmlp_sandwich_block"


def rms_norm(x, gain):
    xf = x.astype(jnp.float32)
    y = xf * lax.rsqrt(jnp.mean(xf * xf, axis=-1, keepdims=True) + RMS_EPS)
    return (y * gain.astype(jnp.float32)).astype(x.dtype)


def layer_norm(x, gain, bias):
    xf = x.astype(jnp.float32)
    mu = jnp.mean(xf, axis=-1, keepdims=True)
    xc = xf - mu
    y = xc * lax.rsqrt(jnp.mean(xc * xc, axis=-1, keepdims=True) + LN_EPS)
    return (y * gain.astype(jnp.float32) + bias.astype(jnp.float32)).astype(x.dtype)


def partial_rotary(x, positions):
    inv_freq = ROPE_THETA ** (-jnp.arange(0, ROT_DIM, 2, dtype=jnp.float32) / ROT_DIM)
    ang = positions.astype(jnp.float32)[:, :, None, None] * inv_freq
    cos, sin = jnp.cos(ang), jnp.sin(ang)
    xf = x.astype(jnp.float32)
    x1 = xf[..., :ROT_HALF]
    x2 = xf[..., ROT_HALF:ROT_DIM]
    out = jnp.concatenate([x1 * cos - x2 * sin, x2 * cos + x1 * sin, xf[..., ROT_DIM:]], axis=-1)
    return out.astype(x.dtype)


def dilated_window_attention(q, k, v, window, dilation):
    B, H, S, Dh = q.shape
    L = S // dilation
    W = window // dilation
    nblk = -(-L // BLOCK_Q)
    Lp = nblk * BLOCK_Q

    def strided(t):
        return t.reshape(B, H, L, dilation, Dh).transpose(0, 1, 3, 2, 4)

    qs, ks, vs = strided(q), strided(k), strided(v)
    qs = jnp.pad(qs, ((0, 0), (0, 0), (0, 0), (0, Lp - L), (0, 0)))
    kv_pad = ((0, 0), (0, 0), (0, 0), (W, Lp - L), (0, 0))
    kp = jnp.pad(ks, kv_pad)
    vp = jnp.pad(vs, kv_pad)
    blk = jnp.arange(nblk)[:, None]
    col = jnp.arange(BLOCK_Q + W)[None, :]
    idx = blk * BLOCK_Q + col
    kb = kp[:, :, :, idx]
    vb = vp[:, :, :, idx]
    qb = qs.reshape(B, H, dilation, nblk, BLOCK_Q, Dh)

    scale = 1.0 / np.sqrt(HEAD_DIM)
    s = jnp.einsum('bhrnqd,bhrnkd->bhrnqk', qb, kb,
                   preferred_element_type=jnp.float32) * scale
    row = jnp.arange(BLOCK_Q)[:, None]
    dist = row + W - jnp.arange(BLOCK_Q + W)[None, :]
    key_pos = jnp.arange(nblk)[:, None, None] * BLOCK_Q + jnp.arange(BLOCK_Q + W)[None, None, :] - W
    mask = (dist >= 0)[None] & (dist <= W)[None] & (key_pos >= 0)
    s = jnp.where(mask, s, jnp.finfo(jnp.float32).min)
    m = jnp.max(s, axis=-1, keepdims=True)
    p = jnp.exp(s - m)
    denom = jnp.sum(p, axis=-1, keepdims=True)
    o = jnp.einsum('bhrnqk,bhrnkd->bhrnqd', p, vb.astype(jnp.float32)) / denom
    lse = (m + jnp.log(denom))[..., 0]
    o = o.reshape(B, H, dilation, Lp, Dh)[:, :, :, :L]
    lse = lse.reshape(B, H, dilation, Lp)[:, :, :, :L]
    o = o.transpose(0, 1, 3, 2, 4).reshape(B, H, S, Dh)
    lse = lse.transpose(0, 1, 3, 2).reshape(B, H, S)
    return o, lse


def dilated_mixture_attention(q, k, v):
    outs, lses = [], []
    for window, dilation in DILATED_PATTERNS:
        o, lse = dilated_window_attention(q, k, v, window, dilation)
        outs.append(o)
        lses.append(lse)
    wts = jax.nn.softmax(jnp.stack(lses, axis=0), axis=0)
    o = jnp.sum(wts[..., None] * jnp.stack(outs, axis=0), axis=0)
    return o.astype(q.dtype)


def spatial_gating(u, v, ln_gain, ln_bias, w_spatial, b_spatial):
    B, S, _ = u.shape
    u = jax.nn.gelu(u, approximate=False)
    v = layer_norm(jax.nn.gelu(v, approximate=False), ln_gain, ln_bias)
    vc = v.reshape(B, S // SGU_CHUNK, SGU_CHUNK, SGU_GROUPS, SGU_GROUP_DIM)
    causal = jnp.tril(jnp.ones((SGU_CHUNK, SGU_CHUNK), dtype=bool))
    w = jnp.where(causal[None], w_spatial, 0.0).astype(v.dtype)
    mixed = jnp.einsum('gij,bnjgc->bnigc', w, vc) + b_spatial.T[:, :, None].astype(v.dtype)
    return u * mixed.reshape(B, S, SGU_WIDTH)


def _fwd_setup_inputs(seed: int = 0) -> dict:
    key = jax.random.key(seed)
    ks = jax.random.split(key, 20)
    f32 = jnp.float32

    def nrm(k, shape, scale):
        return jax.random.normal(k, shape, f32) * scale

    def gain(k, width):
        return 1.0 + 0.05 * jax.random.normal(k, (DEPTH, width), f32)

    x = jax.random.normal(ks[0], (BATCH, SEQ, D_MODEL), f32)
    offsets = jax.random.randint(ks[1], (BATCH, 1), 0, 4096, dtype=jnp.int32)
    positions = (jnp.arange(SEQ, dtype=jnp.int32)[None, :] + offsets).astype(jnp.int32)
    return {
        "x": x,
        "positions": positions,
        "pre_mix_norm": gain(ks[2], D_MODEL),
        "w_in": nrm(ks[3], (DEPTH, D_MODEL, IN_PROJ_WIDTH), D_MODEL ** -0.5),
        "sgu_ln_gain": gain(ks[4], SGU_WIDTH),
        "sgu_ln_bias": nrm(ks[5], (DEPTH, SGU_WIDTH), 0.02),
        "sgu_w_spatial": nrm(ks[6], (DEPTH, SGU_GROUPS, SGU_CHUNK, SGU_CHUNK), 0.5 * SGU_CHUNK ** -0.5),
        "sgu_b_spatial": 1.0 + nrm(ks[7], (DEPTH, SGU_GROUPS, SGU_CHUNK), 0.1),
        "attn_out_norm": gain(ks[8], ATTN_WIDTH),
        "sgu_out_norm": gain(ks[9], SGU_WIDTH),
        "w_out": nrm(ks[10], (DEPTH, MIX_WIDTH, D_MODEL), MIX_WIDTH ** -0.5),
        "post_mix_norm": gain(ks[11], D_MODEL),
        "pre_ffn_norm": gain(ks[12], D_MODEL),
        "w_gate": nrm(ks[13], (DEPTH, D_MODEL, D_FF), D_MODEL ** -0.5),
        "w_up": nrm(ks[14], (DEPTH, D_MODEL, D_FF), D_MODEL ** -0.5),
        "w_down": nrm(ks[15], (DEPTH, D_FF, D_MODEL), D_FF ** -0.5),
        "post_ffn_norm": gain(ks[16], D_MODEL),
    }


def _fwd_reference(x, positions, pre_mix_norm, w_in, sgu_ln_gain, sgu_ln_bias, sgu_w_spatial,
              sgu_b_spatial, attn_out_norm, sgu_out_norm, w_out, post_mix_norm,
              pre_ffn_norm, w_gate, w_up, w_down, post_ffn_norm):
    B, S, _ = x.shape
    for l in range(DEPTH):
        h = rms_norm(x, pre_mix_norm[l])
        proj = h @ w_in[l]
        q, k, v_a, u, v_s = jnp.split(
            proj, [ATTN_WIDTH, 2 * ATTN_WIDTH, 3 * ATTN_WIDTH, 3 * ATTN_WIDTH + SGU_WIDTH], axis=-1)
        q = partial_rotary(q.reshape(B, S, ATTN_HEADS, HEAD_DIM), positions).transpose(0, 2, 1, 3)
        k = partial_rotary(k.reshape(B, S, ATTN_HEADS, HEAD_DIM), positions).transpose(0, 2, 1, 3)
        v_a = v_a.reshape(B, S, ATTN_HEADS, HEAD_DIM).transpose(0, 2, 1, 3)
        attn = dilated_mixture_attention(q, k, v_a)
        attn = attn.transpose(0, 2, 1, 3).reshape(B, S, ATTN_WIDTH)
        sgu = spatial_gating(u, v_s, sgu_ln_gain[l], sgu_ln_bias[l],
                             sgu_w_spatial[l], sgu_b_spatial[l])
        mixed = jnp.concatenate([rms_norm(attn, attn_out_norm[l]),
                                 rms_norm(sgu, sgu_out_norm[l])], axis=-1)
        y = mixed @ w_out[l]
        x = x + rms_norm(y, post_mix_norm[l])
        h = rms_norm(x, pre_ffn_norm[l])
        f = (jax.nn.silu(h @ w_gate[l]) * (h @ w_up[l])) @ w_down[l]
        x = x + rms_norm(f, post_ffn_norm[l])
    return x


import jax as _jax
import jax.numpy as _jnp

TWIN_FORMAT = 'train_step'
FWD_PARAMS = ['x', 'positions', 'pre_mix_norm', 'w_in', 'sgu_ln_gain', 'sgu_ln_bias', 'sgu_w_spatial', 'sgu_b_spatial', 'attn_out_norm', 'sgu_out_norm', 'w_out', 'post_mix_norm', 'pre_ffn_norm', 'w_gate', 'w_up', 'w_down', 'post_ffn_norm']
TWIN_WEIGHTS = ['pre_mix_norm', 'w_in', 'sgu_ln_gain', 'sgu_ln_bias', 'sgu_w_spatial', 'sgu_b_spatial', 'attn_out_norm', 'sgu_out_norm', 'w_out', 'post_mix_norm', 'pre_ffn_norm', 'w_gate', 'w_up', 'w_down', 'post_ffn_norm']
TWIN_DIFF_INPUT = 'x'
TWIN_INPUTS = ['x', 'positions', 'pre_mix_norm', 'w_in', 'sgu_ln_gain', 'sgu_ln_bias', 'sgu_w_spatial', 'sgu_b_spatial', 'attn_out_norm', 'sgu_out_norm', 'w_out', 'post_mix_norm', 'pre_ffn_norm', 'w_gate', 'w_up', 'w_down', 'post_ffn_norm', 'loss_target', 'm_pre_mix_norm', 'm_w_in', 'm_sgu_ln_gain', 'm_sgu_ln_bias', 'm_sgu_w_spatial', 'm_sgu_b_spatial', 'm_attn_out_norm', 'm_sgu_out_norm', 'm_w_out', 'm_post_mix_norm', 'm_pre_ffn_norm', 'm_w_gate', 'm_w_up', 'm_w_down', 'm_post_ffn_norm', 'v_pre_mix_norm', 'v_w_in', 'v_sgu_ln_gain', 'v_sgu_ln_bias', 'v_sgu_w_spatial', 'v_sgu_b_spatial', 'v_attn_out_norm', 'v_sgu_out_norm', 'v_w_out', 'v_post_mix_norm', 'v_pre_ffn_norm', 'v_w_gate', 'v_w_up', 'v_w_down', 'v_post_ffn_norm']
TWIN_OUTPUTS = ['loss', 'grad_x', 'grad_pre_mix_norm', 'grad_w_in', 'grad_sgu_ln_gain', 'grad_sgu_ln_bias', 'grad_sgu_w_spatial', 'grad_sgu_b_spatial', 'grad_attn_out_norm', 'grad_sgu_out_norm', 'grad_w_out', 'grad_post_mix_norm', 'grad_pre_ffn_norm', 'grad_w_gate', 'grad_w_up', 'grad_w_down', 'grad_post_ffn_norm', 'delta_pre_mix_norm', 'delta_w_in', 'delta_sgu_ln_gain', 'delta_sgu_ln_bias', 'delta_sgu_w_spatial', 'delta_sgu_b_spatial', 'delta_attn_out_norm', 'delta_sgu_out_norm', 'delta_w_out', 'delta_post_mix_norm', 'delta_pre_ffn_norm', 'delta_w_gate', 'delta_w_up', 'delta_w_down', 'delta_post_ffn_norm', 'new_m_pre_mix_norm', 'new_m_w_in', 'new_m_sgu_ln_gain', 'new_m_sgu_ln_bias', 'new_m_sgu_w_spatial', 'new_m_sgu_b_spatial', 'new_m_attn_out_norm', 'new_m_sgu_out_norm', 'new_m_w_out', 'new_m_post_mix_norm', 'new_m_pre_ffn_norm', 'new_m_w_gate', 'new_m_w_up', 'new_m_w_down', 'new_m_post_ffn_norm', 'new_v_pre_mix_norm', 'new_v_w_in', 'new_v_sgu_ln_gain', 'new_v_sgu_ln_bias', 'new_v_sgu_w_spatial', 'new_v_sgu_b_spatial', 'new_v_attn_out_norm', 'new_v_sgu_out_norm', 'new_v_w_out', 'new_v_post_mix_norm', 'new_v_pre_ffn_norm', 'new_v_w_gate', 'new_v_w_up', 'new_v_w_down', 'new_v_post_ffn_norm']
TWIN_LEAF_KINDS = {'loss': 'loss', 'grad_x': 'grad_x', 'grad_pre_mix_norm': 'grad_w', 'grad_w_in': 'grad_w', 'grad_sgu_ln_gain': 'grad_w', 'grad_sgu_ln_bias': 'grad_w', 'grad_sgu_w_spatial': 'grad_w', 'grad_sgu_b_spatial': 'grad_w', 'grad_attn_out_norm': 'grad_w', 'grad_sgu_out_norm': 'grad_w', 'grad_w_out': 'grad_w', 'grad_post_mix_norm': 'grad_w', 'grad_pre_ffn_norm': 'grad_w', 'grad_w_gate': 'grad_w', 'grad_w_up': 'grad_w', 'grad_w_down': 'grad_w', 'grad_post_ffn_norm': 'grad_w', 'delta_pre_mix_norm': 'delta_w', 'delta_w_in': 'delta_w', 'delta_sgu_ln_gain': 'delta_w', 'delta_sgu_ln_bias': 'delta_w', 'delta_sgu_w_spatial': 'delta_w', 'delta_sgu_b_spatial': 'delta_w', 'delta_attn_out_norm': 'delta_w', 'delta_sgu_out_norm': 'delta_w', 'delta_w_out': 'delta_w', 'delta_post_mix_norm': 'delta_w', 'delta_pre_ffn_norm': 'delta_w', 'delta_w_gate': 'delta_w', 'delta_w_up': 'delta_w', 'delta_w_down': 'delta_w', 'delta_post_ffn_norm': 'delta_w', 'new_m_pre_mix_norm': 'new_m', 'new_m_w_in': 'new_m', 'new_m_sgu_ln_gain': 'new_m', 'new_m_sgu_ln_bias': 'new_m', 'new_m_sgu_w_spatial': 'new_m', 'new_m_sgu_b_spatial': 'new_m', 'new_m_attn_out_norm': 'new_m', 'new_m_sgu_out_norm': 'new_m', 'new_m_w_out': 'new_m', 'new_m_post_mix_norm': 'new_m', 'new_m_pre_ffn_norm': 'new_m', 'new_m_w_gate': 'new_m', 'new_m_w_up': 'new_m', 'new_m_w_down': 'new_m', 'new_m_post_ffn_norm': 'new_m', 'new_v_pre_mix_norm': 'new_v', 'new_v_w_in': 'new_v', 'new_v_sgu_ln_gain': 'new_v', 'new_v_sgu_ln_bias': 'new_v', 'new_v_sgu_w_spatial': 'new_v', 'new_v_sgu_b_spatial': 'new_v', 'new_v_attn_out_norm': 'new_v', 'new_v_sgu_out_norm': 'new_v', 'new_v_w_out': 'new_v', 'new_v_post_mix_norm': 'new_v', 'new_v_pre_ffn_norm': 'new_v', 'new_v_w_gate': 'new_v', 'new_v_w_up': 'new_v', 'new_v_w_down': 'new_v', 'new_v_post_ffn_norm': 'new_v'}


def _forward(args):
    return _fwd_reference(*[args[k] for k in FWD_PARAMS])


def _output_shape():
    out = _jax.eval_shape(lambda: _forward(_fwd_setup_inputs(0)))
    return out.shape, out.dtype

N_MICROBATCH = 1
ADAM_LR = 0.001
ADAM_B1 = 0.9
ADAM_B2 = 0.999
ADAM_EPS = 1e-08
ADAM_WD = 0.01
ADAM_STEP = 10
PER_EXAMPLE_BATCH_AXIS = {'x': 0, 'positions': 0, 'loss_target': 0}
SHARED_INPUTS = []
_WEIGHT_DTYPES = {'pre_mix_norm': _jnp.float32, 'w_in': _jnp.float32, 'sgu_ln_gain': _jnp.float32, 'sgu_ln_bias': _jnp.float32, 'sgu_w_spatial': _jnp.float32, 'sgu_b_spatial': _jnp.float32, 'attn_out_norm': _jnp.float32, 'sgu_out_norm': _jnp.float32, 'w_out': _jnp.float32, 'post_mix_norm': _jnp.float32, 'pre_ffn_norm': _jnp.float32, 'w_gate': _jnp.float32, 'w_up': _jnp.float32, 'w_down': _jnp.float32, 'post_ffn_norm': _jnp.float32}
MOMENT_SCALE = {'pre_mix_norm': 5.254795e-01, 'w_in': 3.369406e-01, 'sgu_ln_gain': 1.048227e-01, 'sgu_ln_bias': 1.074485e-01, 'sgu_w_spatial': 1.436799e-01, 'sgu_b_spatial': 2.032700e-01, 'attn_out_norm': 4.005210e-01, 'sgu_out_norm': 9.142550e-01, 'w_out': 6.381396e-01, 'post_mix_norm': 1.614975e+01, 'pre_ffn_norm': 5.573288e-01, 'w_gate': 1.926196e-01, 'w_up': 2.751596e-01, 'w_down': 4.662861e-01, 'post_ffn_norm': 1.603073e+01}


def _to_microbatches(a, axis):
    t = _jnp.moveaxis(a, axis, 0)
    t = t.reshape((N_MICROBATCH, t.shape[0] // N_MICROBATCH) + t.shape[1:])
    return _jnp.moveaxis(t, 1, axis + 1)


def setup_inputs(seed: int = 0) -> dict:
    inp = _fwd_setup_inputs(seed)
    key = _jax.random.fold_in(_jax.random.key(seed), 7919)
    shape, _ = _output_shape()
    out = dict(inp)
    out["loss_target"] = _jax.random.normal(_jax.random.fold_in(key, 0), shape, _jnp.float32)
    for i, name in enumerate(TWIN_WEIGHTS):
        w = inp[name].astype(_jnp.float32)
        if MOMENT_SCALE is None:
            s = _jnp.sqrt(_jnp.mean(_jnp.square(w)) + 1e-30)
        else:
            s = MOMENT_SCALE[name]
        km, kv = _jax.random.split(_jax.random.fold_in(key, i + 1))
        out[name] = w
        out["m_" + name] = s * _jax.random.normal(km, w.shape, _jnp.float32)
        out["v_" + name] = (s * s) * _jax.random.uniform(kv, w.shape, _jnp.float32, 0.5, 1.5)
    if N_MICROBATCH > 1:
        for name, axis in PER_EXAMPLE_BATCH_AXIS.items():
            out[name] = _to_microbatches(out[name], axis)
    return {'x': out['x'], 'positions': out['positions'], 'pre_mix_norm': out['pre_mix_norm'], 'w_in': out['w_in'], 'sgu_ln_gain': out['sgu_ln_gain'], 'sgu_ln_bias': out['sgu_ln_bias'], 'sgu_w_spatial': out['sgu_w_spatial'], 'sgu_b_spatial': out['sgu_b_spatial'], 'attn_out_norm': out['attn_out_norm'], 'sgu_out_norm': out['sgu_out_norm'], 'w_out': out['w_out'], 'post_mix_norm': out['post_mix_norm'], 'pre_ffn_norm': out['pre_ffn_norm'], 'w_gate': out['w_gate'], 'w_up': out['w_up'], 'w_down': out['w_down'], 'post_ffn_norm': out['post_ffn_norm'], 'loss_target': out['loss_target'], 'm_pre_mix_norm': out['m_pre_mix_norm'], 'm_w_in': out['m_w_in'], 'm_sgu_ln_gain': out['m_sgu_ln_gain'], 'm_sgu_ln_bias': out['m_sgu_ln_bias'], 'm_sgu_w_spatial': out['m_sgu_w_spatial'], 'm_sgu_b_spatial': out['m_sgu_b_spatial'], 'm_attn_out_norm': out['m_attn_out_norm'], 'm_sgu_out_norm': out['m_sgu_out_norm'], 'm_w_out': out['m_w_out'], 'm_post_mix_norm': out['m_post_mix_norm'], 'm_pre_ffn_norm': out['m_pre_ffn_norm'], 'm_w_gate': out['m_w_gate'], 'm_w_up': out['m_w_up'], 'm_w_down': out['m_w_down'], 'm_post_ffn_norm': out['m_post_ffn_norm'], 'v_pre_mix_norm': out['v_pre_mix_norm'], 'v_w_in': out['v_w_in'], 'v_sgu_ln_gain': out['v_sgu_ln_gain'], 'v_sgu_ln_bias': out['v_sgu_ln_bias'], 'v_sgu_w_spatial': out['v_sgu_w_spatial'], 'v_sgu_b_spatial': out['v_sgu_b_spatial'], 'v_attn_out_norm': out['v_attn_out_norm'], 'v_sgu_out_norm': out['v_sgu_out_norm'], 'v_w_out': out['v_w_out'], 'v_post_mix_norm': out['v_post_mix_norm'], 'v_pre_ffn_norm': out['v_pre_ffn_norm'], 'v_w_gate': out['v_w_gate'], 'v_w_up': out['v_w_up'], 'v_w_down': out['v_w_down'], 'v_post_ffn_norm': out['v_post_ffn_norm']}


def _loss(weights, diff, rest, loss_target):
    with _jax.named_scope("forward"):
        args = {**rest, TWIN_DIFF_INPUT: diff, **{k: w.astype(_WEIGHT_DTYPES[k]) for k, w in weights.items()}}
        y = _forward(args)
    with _jax.named_scope("loss_head"):
        err = _jnp.square(y.astype(_jnp.float32) - loss_target)
        return 0.5 * _jnp.sum(_jnp.mean(err, axis=-1)) if err.ndim else 0.5 * err


def _adamw(w, g, m, v):
    m = ADAM_B1 * m + (1.0 - ADAM_B1) * g
    v = ADAM_B2 * v + (1.0 - ADAM_B2) * _jnp.square(g)
    m_hat = m / (1.0 - ADAM_B1 ** ADAM_STEP)
    v_hat = v / (1.0 - ADAM_B2 ** ADAM_STEP)
    delta = -ADAM_LR * (m_hat / (_jnp.sqrt(v_hat) + ADAM_EPS) + ADAM_WD * w)
    return delta, m, v


def reference(x, positions, pre_mix_norm, w_in, sgu_ln_gain, sgu_ln_bias, sgu_w_spatial, sgu_b_spatial, attn_out_norm, sgu_out_norm, w_out, post_mix_norm, pre_ffn_norm, w_gate, w_up, w_down, post_ffn_norm, loss_target, m_pre_mix_norm, m_w_in, m_sgu_ln_gain, m_sgu_ln_bias, m_sgu_w_spatial, m_sgu_b_spatial, m_attn_out_norm, m_sgu_out_norm, m_w_out, m_post_mix_norm, m_pre_ffn_norm, m_w_gate, m_w_up, m_w_down, m_post_ffn_norm, v_pre_mix_norm, v_w_in, v_sgu_ln_gain, v_sgu_ln_bias, v_sgu_w_spatial, v_sgu_b_spatial, v_attn_out_norm, v_sgu_out_norm, v_w_out, v_post_mix_norm, v_pre_ffn_norm, v_w_gate, v_w_up, v_w_down, v_post_ffn_norm):
    given = dict(x=x, positions=positions, pre_mix_norm=pre_mix_norm, w_in=w_in, sgu_ln_gain=sgu_ln_gain, sgu_ln_bias=sgu_ln_bias, sgu_w_spatial=sgu_w_spatial, sgu_b_spatial=sgu_b_spatial, attn_out_norm=attn_out_norm, sgu_out_norm=sgu_out_norm, w_out=w_out, post_mix_norm=post_mix_norm, pre_ffn_norm=pre_ffn_norm, w_gate=w_gate, w_up=w_up, w_down=w_down, post_ffn_norm=post_ffn_norm, loss_target=loss_target, m_pre_mix_norm=m_pre_mix_norm, m_w_in=m_w_in, m_sgu_ln_gain=m_sgu_ln_gain, m_sgu_ln_bias=m_sgu_ln_bias, m_sgu_w_spatial=m_sgu_w_spatial, m_sgu_b_spatial=m_sgu_b_spatial, m_attn_out_norm=m_attn_out_norm, m_sgu_out_norm=m_sgu_out_norm, m_w_out=m_w_out, m_post_mix_norm=m_post_mix_norm, m_pre_ffn_norm=m_pre_ffn_norm, m_w_gate=m_w_gate, m_w_up=m_w_up, m_w_down=m_w_down, m_post_ffn_norm=m_post_ffn_norm, v_pre_mix_norm=v_pre_mix_norm, v_w_in=v_w_in, v_sgu_ln_gain=v_sgu_ln_gain, v_sgu_ln_bias=v_sgu_ln_bias, v_sgu_w_spatial=v_sgu_w_spatial, v_sgu_b_spatial=v_sgu_b_spatial, v_attn_out_norm=v_attn_out_norm, v_sgu_out_norm=v_sgu_out_norm, v_w_out=v_w_out, v_post_mix_norm=v_post_mix_norm, v_pre_ffn_norm=v_pre_ffn_norm, v_w_gate=v_w_gate, v_w_up=v_w_up, v_w_down=v_w_down, v_post_ffn_norm=v_post_ffn_norm)
    weights = {n: given[n] for n in TWIN_WEIGHTS}
    shared = {n: given[n] for n in SHARED_INPUTS}
    per_example = {n: given[n] for n in ['x', 'positions']}
    grad_fn = _jax.value_and_grad(_loss, argnums=(0, 1))

    def one_microbatch(ex, loss_target):
        ex = dict(ex)
        diff = ex.pop(TWIN_DIFF_INPUT)
        return grad_fn(weights, diff, {**shared, **ex}, loss_target)

    if N_MICROBATCH == 1:
        loss, (grad_w, grad_x) = one_microbatch(per_example, given["loss_target"])
    else:
        def body(carry, xs):
            loss_sum, grad_sum = carry
            l_k, (gw_k, gx_k) = one_microbatch(xs[0], xs[1])
            with _jax.named_scope("update"):
                return (loss_sum + l_k, _jax.tree.map(_jnp.add, grad_sum, gw_k)), gx_k

        init = (_jnp.zeros((), _jnp.float32), _jax.tree.map(_jnp.zeros_like, weights))
        (loss, grad_w), grad_x = _jax.lax.scan(body, init, (per_example, given["loss_target"]))
    with _jax.named_scope("update"):
        delta_w, new_m, new_v = {}, {}, {}
        for n in TWIN_WEIGHTS:
            delta_w[n], new_m[n], new_v[n] = _adamw(weights[n], grad_w[n], given["m_" + n], given["v_" + n])
    return (loss, grad_x, *[grad_w[n] for n in TWIN_WEIGHTS], *[delta_w[n] for n in TWIN_WEIGHTS],
            *[new_m[n] for n in TWIN_WEIGHTS], *[new_v[n] for n in TWIN_WEIGHTS])
```

```python
import numpy as np
import jax
import jax.numpy as jnp
from jax import lax
from jax.experimental import pallas as pl
from jax.experimental.pallas import tpu as pltpu

F32 = jnp.float32
BF16 = jnp.bfloat16

SEQ = 2048
D_MODEL = 1024
HEAD_DIM = 64
ATTN_W = 512
SGU_W = 512
SGU_GROUPS = 8
CHUNK = 128
DILATIONS = (1, 4, 16)
N_SHARD = 4
IN_S = 640
OUT_S = 256
FF_S = 704
PROJ_W = N_SHARD * IN_S
RMS_EPS = 1e-6
LN_EPS = 1e-5
ROPE_THETA = 500000.0
ATTN_SCALE = 1.0 / np.sqrt(HEAD_DIM)
NEG = -1e30
TM = 256
VMEM_LIMIT = 56 * 1024 * 1024
SMALL_ROWS = 136

ADAM_LR = 0.001
ADAM_B1 = 0.9
ADAM_B2 = 0.999
ADAM_EPS = 1e-08
ADAM_WD = 0.01
ADAM_STEP = 10

MESH = pl.DeviceIdType.MESH
ANY = pl.BlockSpec(memory_space=pl.ANY)


def _dot(a, b):
    return jnp.dot(a, b, preferred_element_type=F32)


def _dot_nt(a, b):
    return lax.dot_general(a, b, (((1,), (1,)), ((), ())), preferred_element_type=F32)


def _dot_tn(a, b):
    return lax.dot_general(a, b, (((0,), (0,)), ((), ())), preferred_element_type=F32)


def _dot_exact(a, b):
    return jnp.dot(a, b, preferred_element_type=F32, precision=lax.Precision.HIGHEST)


def _rms_stats(x):
    r = lax.rsqrt(jnp.mean(x * x, axis=-1, keepdims=True) + RMS_EPS)
    return x * r, r


def _rms_bwd(xh, r, gain, dy):
    dxh = dy * gain
    dx = r * (dxh - xh * jnp.mean(dxh * xh, axis=-1, keepdims=True))
    return dx, jnp.sum(dy * xh, axis=0, keepdims=True)


_ERF_ALPHA = (-2.72614225801306e-10, 2.77068142495902e-08, -2.10102402082508e-06, -5.69250639462346e-05,
              -7.34990630326855e-04, -2.95459980854025e-03, -1.60960333262415e-02)
_ERF_BETA = (-1.45660718464996e-05, -2.13374055278905e-04, -1.68282697438203e-03, -7.37332916720468e-03,
             -1.42647390514189e-02)


def _erf(x):
    x = jnp.clip(x, -4.0, 4.0)
    x2 = x * x
    p = jnp.full_like(x, _ERF_ALPHA[0])
    for a in _ERF_ALPHA[1:]:
        p = p * x2 + a
    q = jnp.full_like(x, _ERF_BETA[0])
    for b in _ERF_BETA[1:]:
        q = q * x2 + b
    return x * p / q


def _gelu(x):
    return 0.5 * x * (1.0 + _erf(x * np.float32(1.0 / np.sqrt(2.0))))


def _gelu_grad(x):
    cdf = 0.5 * (1.0 + _erf(x * np.float32(1.0 / np.sqrt(2.0))))
    pdf = jnp.exp(-0.5 * x * x) * np.float32(1.0 / np.sqrt(2.0 * np.pi))
    return cdf + x * pdf


def _sigmoid(x):
    return 1.0 / (1.0 + jnp.exp(-x))


_INV_FREQ = tuple(float(np.float32(ROPE_THETA ** (-2.0 * j / 16.0))) for j in range(8))


def _rot_tables(pos):
    lane = lax.broadcasted_iota(jnp.int32, (1, 128), 1)
    d = lane & 63
    j = d & 7
    inv = jnp.zeros((1, 128), F32)
    for jj in range(8):
        inv = jnp.where(j == jj, _INV_FREQ[jj], inv)
    ang = pos.astype(F32) * inv
    c = jnp.cos(ang)
    s = jnp.sin(ang)
    cos_t = jnp.where(d < 16, c, 1.0)
    sin_a = jnp.where(d < 8, -s, 0.0)
    sin_b = jnp.where((d >= 8) & (d < 16), s, 0.0)
    return tuple(jnp.tile(t, (1, 4)) for t in (cos_t, sin_a, sin_b))


def _rope(x, tabs):
    cos_t, sin_a, sin_b = tabs
    return x * cos_t + pltpu.roll(x, 504, 1) * sin_a + pltpu.roll(x, 8, 1) * sin_b


def _rope_bwd(dy, tabs):
    cos_t, sin_a, sin_b = tabs
    return dy * cos_t + pltpu.roll(dy * sin_a, 8, 1) + pltpu.roll(dy * sin_b, 504, 1)


def _left_half():
    return lax.broadcasted_iota(jnp.int32, (CHUNK, CHUNK), 1) < HEAD_DIM


def _group_ones():
    lane = lax.broadcasted_iota(jnp.int32, (SGU_GROUPS, SGU_W), 1)
    row = lax.broadcasted_iota(jnp.int32, (SGU_GROUPS, SGU_W), 0)
    return ((lane >> 6) == row).astype(F32)


def _masked_spatial(w_ref):
    row = lax.broadcasted_iota(jnp.int32, (CHUNK, CHUNK), 0)
    col = lax.broadcasted_iota(jnp.int32, (CHUNK, CHUNK), 1)
    return [jnp.where(col <= row, w_ref[g], 0.0).astype(BF16) for g in range(SGU_GROUPS)]


def _sgu_core(u, vs, lg, lb, wm, bias_full):
    tm = u.shape[0]
    gu = _gelu(u)
    gv = _gelu(vs)
    mu = jnp.mean(gv, axis=-1, keepdims=True)
    xc = gv - mu
    rstd = lax.rsqrt(jnp.mean(xc * xc, axis=-1, keepdims=True) + LN_EPS)
    xh = xc * rstd
    vnb = (xh * lg + lb).astype(BF16)
    left = _left_half()
    rows = []
    for c in range(tm // CHUNK):
        pieces = []
        for p in range(4):
            vp = vnb[c * CHUNK:(c + 1) * CHUNK, p * 128:(p + 1) * 128]
            pieces.append(jnp.where(left, _dot(wm[2 * p], vp), _dot(wm[2 * p + 1], vp)))
        rows.append(jnp.concatenate(pieces, axis=1) + bias_full)
    mixed = jnp.concatenate(rows, axis=0)
    return gu, xh, rstd, vnb, mixed


def _resident(shape):
    n = len(shape)
    return pl.BlockSpec(shape, lambda *_: (0,) * n, pipeline_mode=pl.Buffered(1))


def _rows(ncol, tm=TM):
    return pl.BlockSpec((tm, ncol), lambda i: (i, 0))


def _rows3(nlead, ncol, tm=TM):
    return pl.BlockSpec((nlead, tm, ncol), lambda i: (0, i, 0))


def _acc(ncol, nrow=1):
    return pl.BlockSpec((nrow, ncol), lambda i: (0, 0))


def _seq_params():
    return pltpu.CompilerParams(dimension_semantics=("arbitrary",), vmem_limit_bytes=VMEM_LIMIT)


def _sds(shape, dtype):
    return jax.ShapeDtypeStruct(shape, dtype)


def _inproj_fwd(x, pos, g_pre, w_in):
    def body(x_ref, pos_ref, g_ref, w_ref, h_ref, q_ref, k_ref, v_ref, u_ref, vs_ref):
        xh, _ = _rms_stats(x_ref[...])
        h = (xh * g_ref[...]).astype(BF16)
        h_ref[...] = h
        proj = jnp.concatenate([_dot(h, w_ref[s]) for s in range(N_SHARD)], axis=1)
        tabs = _rot_tables(pos_ref[...])
        q_ref[...] = _rope(proj[:, 0:512], tabs).astype(BF16)
        k_ref[...] = _rope(proj[:, 512:1024], tabs).astype(BF16)
        v_ref[...] = proj[:, 1024:1536].astype(BF16)
        u_ref[...] = proj[:, 1536:2048]
        vs_ref[...] = proj[:, 2048:2560]

    return pl.pallas_call(
        body, name="inproj_fwd", grid=(SEQ // TM,),
        in_specs=[_rows(D_MODEL), _rows(1), _resident((1, D_MODEL)), _resident((N_SHARD, D_MODEL, IN_S))],
        out_specs=[_rows(D_MODEL), _rows(512), _rows(512), _rows(512), _rows(512), _rows(512)],
        out_shape=[_sds((SEQ, D_MODEL), BF16), _sds((SEQ, 512), BF16), _sds((SEQ, 512), BF16),
                   _sds((SEQ, 512), BF16), _sds((SEQ, 512), F32), _sds((SEQ, 512), F32)],
        compiler_params=_seq_params(),
    )(x, pos, g_pre, w_in)


def _sgu_fwd(u, vs, lg, lb, w_sp, b_t):
    def body(u_ref, vs_ref, lg_ref, lb_ref, w_ref, bt_ref, out_ref):
        wm = _masked_spatial(w_ref)
        bias_full = _dot_exact(bt_ref[...], _group_ones())
        gu, _, _, _, mixed = _sgu_core(u_ref[...], vs_ref[...], lg_ref[...], lb_ref[...], wm, bias_full)
        out_ref[...] = gu * mixed

    return pl.pallas_call(
        body, name="sgu_fwd", grid=(SEQ // TM,),
        in_specs=[_rows(SGU_W), _rows(SGU_W), _resident((1, SGU_W)), _resident((1, SGU_W)),
                  _resident((SGU_GROUPS, CHUNK, CHUNK)), _resident((CHUNK, SGU_GROUPS))],
        out_specs=_rows(SGU_W),
        out_shape=_sds((SEQ, SGU_W), F32),
        compiler_params=_seq_params(),
    )(u, vs, lg, lb, w_sp, b_t)


def _block_masks():
    row = lax.broadcasted_iota(jnp.int32, (CHUNK, CHUNK), 0)
    col = lax.broadcasted_iota(jnp.int32, (CHUNK, CHUNK), 1)
    return col <= row, col >= row


def _attn_fwd(qv, kv, vv, dil):
    seg = SEQ // dil
    nblk = seg // CHUNK

    def body(q_ref, k_ref, v_ref, o_ref, l_ref):
        left = _left_half()
        m_cur, m_prev = _block_masks()

        def blk(b, carry):
            r0 = pl.multiple_of(b * CHUNK, CHUNK)
            rp = pl.multiple_of(jnp.maximum(b - 1, 0) * CHUNK, CHUNK)
            prev_ok = m_prev & (b > 0)
            for hp in range(4):
                ls = slice(hp * 128, (hp + 1) * 128)
                qp = q_ref[pl.ds(r0, CHUNK), ls]
                kc = k_ref[pl.ds(r0, CHUNK), ls]
                vc = v_ref[pl.ds(r0, CHUNK), ls]
                if nblk > 1:
                    kp = k_ref[pl.ds(rp, CHUNK), ls]
                    vp = v_ref[pl.ds(rp, CHUNK), ls]
                outs, lses = [], []
                for hm in (left, ~left):
                    qh = jnp.where(hm, qp, jnp.zeros_like(qp))
                    sc = jnp.where(m_cur, _dot_nt(qh, kc) * ATTN_SCALE, NEG)
                    m = jnp.max(sc, axis=-1, keepdims=True)
                    if nblk > 1:
                        sp = jnp.where(prev_ok, _dot_nt(qh, kp) * ATTN_SCALE, NEG)
                        m = jnp.maximum(m, jnp.max(sp, axis=-1, keepdims=True))
                    pc = jnp.exp(sc - m)
                    den = jnp.sum(pc, axis=-1, keepdims=True)
                    acc = _dot(pc.astype(BF16), vc)
                    if nblk > 1:
                        pp = jnp.exp(sp - m)
                        den = den + jnp.sum(pp, axis=-1, keepdims=True)
                        acc = acc + _dot(pp.astype(BF16), vp)
                    outs.append(acc / den)
                    lses.append(m + jnp.log(den))
                o_ref[pl.ds(r0, CHUNK), ls] = jnp.where(left, outs[0], outs[1])
                l_ref[pl.ds(r0, CHUNK), ls] = jnp.where(left, lses[0], lses[1])
            return carry

        lax.fori_loop(0, nblk, blk, 0)

    spec = pl.BlockSpec((seg, ATTN_W), lambda r: (0, r))
    return pl.pallas_call(
        body, name=f"attn_fwd_d{dil}", grid=(dil,),
        in_specs=[spec, spec, spec], out_specs=[spec, spec],
        out_shape=[_sds((seg, dil * ATTN_W), F32), _sds((seg, dil * ATTN_W), F32)],
        compiler_params=_seq_params(),
    )(qv, kv, vv)


def _mix_out_fwd(o_list, l_list, sgu, x, w_out, g_attn, g_sgu, g_post):
    def body(o1, o2, o3, l1, l2, l3, sgu_ref, x_ref, w_ref, ga_ref, gs_ref, gp_ref,
             attn_ref, lse_ref, mixed_ref, y_ref, x1_ref):
        ls = [l1[...], l2[...], l3[...]]
        m = jnp.maximum(jnp.maximum(ls[0], ls[1]), ls[2])
        es = [jnp.exp(l - m) for l in ls]
        den = es[0] + es[1] + es[2]
        attn = (es[0] * o1[...] + es[1] * o2[...] + es[2] * o3[...]) / den
        attn_ref[...] = attn
        lse_ref[...] = m + jnp.log(den)
        ah, _ = _rms_stats(attn)
        sh, _ = _rms_stats(sgu_ref[...])
        mixed = jnp.concatenate([ah * ga_ref[...], sh * gs_ref[...]], axis=1).astype(BF16)
        mixed_ref[...] = mixed
        y = _dot(mixed[:, 0:OUT_S], w_ref[0])
        for s in range(1, N_SHARD):
            y = y + _dot(mixed[:, s * OUT_S:(s + 1) * OUT_S], w_ref[s])
        y_ref[...] = y
        yh, _ = _rms_stats(y)
        x1_ref[...] = x_ref[...] + yh * gp_ref[...]

    return pl.pallas_call(
        body, name="mix_out_fwd", grid=(SEQ // TM,),
        in_specs=[_rows(512)] * 7 + [_rows(D_MODEL), _resident((N_SHARD, OUT_S, D_MODEL)),
                                    _resident((1, 512)), _resident((1, 512)), _resident((1, D_MODEL))],
        out_specs=[_rows(512), _rows(512), _rows(D_MODEL), _rows(D_MODEL), _rows(D_MODEL)],
        out_shape=[_sds((SEQ, 512), F32), _sds((SEQ, 512), F32), _sds((SEQ, D_MODEL), BF16),
                   _sds((SEQ, D_MODEL), F32), _sds((SEQ, D_MODEL), F32)],
        compiler_params=_seq_params(),
    )(*o_list, *l_list, sgu, x, w_out, g_attn, g_sgu, g_post)


def _ffn_fwd_bwd(x1, target, w_gate, w_up, w_down, g_pre, g_post):
    def body(x1_ref, t_ref, wg_ref, wu_ref, wd_ref, gpf_ref, gpo_ref,
             h2_ref, a_ref, dg_ref, dup_ref, df_ref, dx1_ref, loss_ref, dgpf_ref, dgpo_ref, g_scr, up_scr):
        @pl.when(pl.program_id(0) == 0)
        def _():
            loss_ref[...] = jnp.zeros_like(loss_ref)
            dgpf_ref[...] = jnp.zeros_like(dgpf_ref)
            dgpo_ref[...] = jnp.zeros_like(dgpo_ref)

        x1 = x1_ref[...]
        gpf = gpf_ref[...]
        gpo = gpo_ref[...]
        xh, r = _rms_stats(x1)
        h2 = (xh * gpf).astype(BF16)
        h2_ref[...] = h2
        f = jnp.zeros((TM, D_MODEL), F32)
        for s in range(N_SHARD):
            g = _dot(h2, wg_ref[s])
            up = _dot(h2, wu_ref[s])
            g_scr[s] = g
            up_scr[s] = up
            a = (g * _sigmoid(g) * up).astype(BF16)
            a_ref[s] = a
            f = f + _dot(a, wd_ref[s])
        fh, rf = _rms_stats(f)
        diff = x1 + fh * gpo - t_ref[...]
        loss_ref[...] += jnp.sum(diff * diff, axis=0, keepdims=True)
        dout = diff * np.float32(1.0 / D_MODEL)
        df, dgpo = _rms_bwd(fh, rf, gpo, dout)
        dgpo_ref[...] += dgpo
        dfb = df.astype(BF16)
        df_ref[...] = dfb
        dh2 = jnp.zeros((TM, D_MODEL), F32)
        for s in range(N_SHARD):
            da = _dot_nt(dfb, wd_ref[s])
            g = g_scr[s]
            up = up_scr[s]
            sg = _sigmoid(g)
            dup = (da * (g * sg)).astype(BF16)
            dg = (da * up * (sg * (1.0 + g * (1.0 - sg)))).astype(BF16)
            dg_ref[s] = dg
            dup_ref[s] = dup
            dh2 = dh2 + _dot_nt(dg, wg_ref[s]) + _dot_nt(dup, wu_ref[s])
        dx, dgpf = _rms_bwd(xh, r, gpf, dh2)
        dgpf_ref[...] += dgpf
        dx1_ref[...] = dout + dx

    return pl.pallas_call(
        body, name="ffn_fwd_bwd", grid=(SEQ // TM,),
        in_specs=[_rows(D_MODEL), _rows(D_MODEL), _resident((N_SHARD, D_MODEL, FF_S)),
                  _resident((N_SHARD, D_MODEL, FF_S)), _resident((N_SHARD, FF_S, D_MODEL)),
                  _resident((1, D_MODEL)), _resident((1, D_MODEL))],
        out_specs=[_rows(D_MODEL), _rows3(N_SHARD, FF_S), _rows3(N_SHARD, FF_S), _rows3(N_SHARD, FF_S),
                   _rows(D_MODEL), _rows(D_MODEL), _acc(D_MODEL), _acc(D_MODEL), _acc(D_MODEL)],
        out_shape=[_sds((SEQ, D_MODEL), BF16), _sds((N_SHARD, SEQ, FF_S), BF16), _sds((N_SHARD, SEQ, FF_S), BF16),
                   _sds((N_SHARD, SEQ, FF_S), BF16), _sds((SEQ, D_MODEL), BF16), _sds((SEQ, D_MODEL), F32),
                   _sds((1, D_MODEL), F32), _sds((1, D_MODEL), F32), _sds((1, D_MODEL), F32)],
        scratch_shapes=[pltpu.VMEM((N_SHARD, TM, FF_S), F32), pltpu.VMEM((N_SHARD, TM, FF_S), F32)],
        compiler_params=_seq_params(),
    )(x1, target, w_gate, w_up, w_down, g_pre, g_post)


def _wgrad(a, b, a_spec, b_spec, out_block, name):
    def body(a_ref, b_ref, o_ref):
        av = a_ref[0] if len(a_ref.shape) == 3 else a_ref[...]
        bv = b_ref[0] if len(b_ref.shape) == 3 else b_ref[...]
        o_ref[0] = _dot_tn(av, bv)

    return pl.pallas_call(
        body, name=name, grid=(N_SHARD,),
        in_specs=[a_spec, b_spec],
        out_specs=pl.BlockSpec((1,) + out_block, lambda s: (s, 0, 0)),
        out_shape=_sds((N_SHARD,) + out_block, F32),
        compiler_params=_seq_params(),
    )(a, b)


def _outproj_bwd(dx1, y, attn, sgu, w_out, g_post, g_attn, g_sgu):
    def body(dx1_ref, y_ref, attn_ref, sgu_ref, w_ref, gp_ref, ga_ref, gs_ref,
             dy_ref, dattn_ref, delta_ref, dsgu_ref, dgp_ref, dga_ref, dgs_ref):
        @pl.when(pl.program_id(0) == 0)
        def _():
            dgp_ref[...] = jnp.zeros_like(dgp_ref)
            dga_ref[...] = jnp.zeros_like(dga_ref)
            dgs_ref[...] = jnp.zeros_like(dgs_ref)

        yh, ry = _rms_stats(y_ref[...])
        dy, dgp = _rms_bwd(yh, ry, gp_ref[...], dx1_ref[...])
        dgp_ref[...] += dgp
        dyb = dy.astype(BF16)
        dy_ref[...] = dyb
        dmixed = jnp.concatenate([_dot_nt(dyb, w_ref[s]) for s in range(N_SHARD)], axis=1)
        attn = attn_ref[...]
        ah, ra = _rms_stats(attn)
        dattn, dga = _rms_bwd(ah, ra, ga_ref[...], dmixed[:, 0:512])
        dga_ref[...] += dga
        sh, rs = _rms_stats(sgu_ref[...])
        dsgu, dgs = _rms_bwd(sh, rs, gs_ref[...], dmixed[:, 512:1024])
        dgs_ref[...] += dgs
        dattn_ref[...] = dattn.astype(BF16)
        dsgu_ref[...] = dsgu
        la = lax.broadcasted_iota(jnp.int32, (ATTN_W, ATTN_W), 0) >> 6
        lb = lax.broadcasted_iota(jnp.int32, (ATTN_W, ATTN_W), 1) >> 6
        delta_ref[...] = _dot_exact(dattn * attn, (la == lb).astype(F32))

    return pl.pallas_call(
        body, name="outproj_bwd", grid=(SEQ // TM,),
        in_specs=[_rows(D_MODEL), _rows(D_MODEL), _rows(512), _rows(512), _resident((N_SHARD, OUT_S, D_MODEL)),
                  _resident((1, D_MODEL)), _resident((1, 512)), _resident((1, 512))],
        out_specs=[_rows(D_MODEL), _rows(512), _rows(512), _rows(512), _acc(D_MODEL), _acc(512), _acc(512)],
        out_shape=[_sds((SEQ, D_MODEL), BF16), _sds((SEQ, 512), BF16), _sds((SEQ, 512), F32), _sds((SEQ, 512), F32),
                   _sds((1, D_MODEL), F32), _sds((1, 512), F32), _sds((1, 512), F32)],
        compiler_params=_seq_params(),
    )(dx1, y, attn, sgu, w_out, g_post, g_attn, g_sgu)


def _sgu_bwd(u, vs, dsgu, lg, lb, w_sp, b_t):
    nsteps = SEQ // TM

    def body(u_ref, vs_ref, ds_ref, lg_ref, lb_ref, w_ref, bt_ref,
             du_ref, dvs_ref, dw_ref, db_ref, dlg_ref, dlb_ref, dbias_scr):
        i = pl.program_id(0)

        @pl.when(i == 0)
        def _():
            dw_ref[...] = jnp.zeros_like(dw_ref)
            dlg_ref[...] = jnp.zeros_like(dlg_ref)
            dlb_ref[...] = jnp.zeros_like(dlb_ref)
            dbias_scr[...] = jnp.zeros_like(dbias_scr)

        wm = _masked_spatial(w_ref)
        ones_g = _group_ones()
        bias_full = _dot_exact(bt_ref[...], ones_g)
        u = u_ref[...]
        vs = vs_ref[...]
        lg = lg_ref[...]
        gu, xh, rstd, vnb, mixed = _sgu_core(u, vs, lg, lb_ref[...], wm, bias_full)
        dsgu = ds_ref[...]
        du_ref[...] = (dsgu * mixed * _gelu_grad(u)).astype(BF16)
        dmixed = dsgu * gu
        left = _left_half()
        dvn_rows = []
        for c in range(TM // CHUNK):
            rs = slice(c * CHUNK, (c + 1) * CHUNK)
            dm_c = dmixed[rs, :]
            dbias_scr[...] += dm_c
            pieces = []
            for p in range(4):
                ls = slice(p * 128, (p + 1) * 128)
                dmp = dm_c[:, ls]
                vp = vnb[rs, ls]
                dmb = dmp.astype(BF16)
                zero = jnp.zeros_like(dmb)
                dw_ref[2 * p] += _dot_nt(jnp.where(left, dmb, zero), vp)
                dw_ref[2 * p + 1] += _dot_nt(jnp.where(left, zero, dmb), vp)
                pieces.append(jnp.where(left, _dot_tn(wm[2 * p], dmb), _dot_tn(wm[2 * p + 1], dmb)))
            dvn_rows.append(jnp.concatenate(pieces, axis=1))
        dvn = jnp.concatenate(dvn_rows, axis=0)
        dlg_ref[...] += jnp.sum(dvn * xh, axis=0, keepdims=True)
        dlb_ref[...] += jnp.sum(dvn, axis=0, keepdims=True)
        dxh = dvn * lg
        dgv = rstd * (dxh - jnp.mean(dxh, axis=-1, keepdims=True) - xh * jnp.mean(dxh * xh, axis=-1, keepdims=True))
        dvs_ref[...] = (dgv * _gelu_grad(vs)).astype(BF16)

        @pl.when(i == nsteps - 1)
        def _():
            row = lax.broadcasted_iota(jnp.int32, (CHUNK, CHUNK), 0)
            col = lax.broadcasted_iota(jnp.int32, (CHUNK, CHUNK), 1)
            for g in range(SGU_GROUPS):
                dw_ref[g] = jnp.where(col <= row, dw_ref[g], 0.0)
            db_ref[...] = lax.dot_general(ones_g, dbias_scr[...], (((1,), (1,)), ((), ())),
                                          preferred_element_type=F32, precision=lax.Precision.HIGHEST)

    return pl.pallas_call(
        body, name="sgu_bwd", grid=(nsteps,),
        in_specs=[_rows(SGU_W), _rows(SGU_W), _rows(SGU_W), _resident((1, SGU_W)), _resident((1, SGU_W)),
                  _resident((SGU_GROUPS, CHUNK, CHUNK)), _resident((CHUNK, SGU_GROUPS))],
        out_specs=[_rows(SGU_W), _rows(SGU_W), pl.BlockSpec((SGU_GROUPS, CHUNK, CHUNK), lambda i: (0, 0, 0)),
                   _acc(CHUNK, SGU_GROUPS), _acc(SGU_W), _acc(SGU_W)],
        out_shape=[_sds((SEQ, SGU_W), BF16), _sds((SEQ, SGU_W), BF16), _sds((SGU_GROUPS, CHUNK, CHUNK), F32),
                   _sds((SGU_GROUPS, CHUNK), F32), _sds((1, SGU_W), F32), _sds((1, SGU_W), F32)],
        scratch_shapes=[pltpu.VMEM((CHUNK, SGU_W), F32)],
        compiler_params=_seq_params(),
    )(u, vs, dsgu, lg, lb, w_sp, b_t)


def _attn_bwd(qv, kv, vv, dov, deltav, lsev, dil):
    seg = SEQ // dil
    nblk = seg // CHUNK

    def body(q_ref, k_ref, v_ref, do_ref, dl_ref, lse_ref, dq_ref, dk_ref, dv_ref):
        left = _left_half()
        m_cur, m_prev = _block_masks()

        def blk(b, carry):
            r0 = pl.multiple_of(b * CHUNK, CHUNK)
            rp = pl.multiple_of(jnp.maximum(b - 1, 0) * CHUNK, CHUNK)
            prev_ok = m_prev & (b > 0)
            for hp in range(4):
                ls = slice(hp * 128, (hp + 1) * 128)
                qp = q_ref[pl.ds(r0, CHUNK), ls]
                kc = k_ref[pl.ds(r0, CHUNK), ls]
                vc = v_ref[pl.ds(r0, CHUNK), ls]
                dop = do_ref[pl.ds(r0, CHUNK), ls]
                lsep = lse_ref[pl.ds(r0, CHUNK), ls]
                dlp = dl_ref[pl.ds(r0, CHUNK), ls]
                if nblk > 1:
                    kp = k_ref[pl.ds(rp, CHUNK), ls]
                    vp = v_ref[pl.ds(rp, CHUNK), ls]
                dqs = []
                dkc = jnp.zeros((CHUNK, 128), F32)
                dvc = jnp.zeros((CHUNK, 128), F32)
                dkp = jnp.zeros((CHUNK, 128), F32)
                dvp = jnp.zeros((CHUNK, 128), F32)
                for side, hm in enumerate((left, ~left)):
                    c0 = side * HEAD_DIM
                    qh = jnp.where(hm, qp, jnp.zeros_like(qp))
                    doh = jnp.where(hm, dop, jnp.zeros_like(dop))
                    lse_h = lsep[:, c0:c0 + 1]
                    dl_h = dlp[:, c0:c0 + 1]
                    sc = _dot_nt(qh, kc) * ATTN_SCALE
                    pc = jnp.exp(jnp.where(m_cur, sc - lse_h, NEG))
                    dsc = (pc * (_dot_nt(doh, vc) - dl_h) * ATTN_SCALE).astype(BF16)
                    dq = _dot(dsc, kc)
                    dkc = dkc + _dot_tn(dsc, qh)
                    dvc = dvc + _dot_tn(pc.astype(BF16), doh)
                    if nblk > 1:
                        sp = _dot_nt(qh, kp) * ATTN_SCALE
                        pp = jnp.exp(jnp.where(prev_ok, sp - lse_h, NEG))
                        dsp = (pp * (_dot_nt(doh, vp) - dl_h) * ATTN_SCALE).astype(BF16)
                        dq = dq + _dot(dsp, kp)
                        dkp = dkp + _dot_tn(dsp, qh)
                        dvp = dvp + _dot_tn(pp.astype(BF16), doh)
                    dqs.append(dq)
                dq_ref[pl.ds(r0, CHUNK), ls] = jnp.where(left, dqs[0], dqs[1])
                dk_ref[pl.ds(r0, CHUNK), ls] = dkc
                dv_ref[pl.ds(r0, CHUNK), ls] = dvc
                if nblk > 1:
                    @pl.when(b > 0)
                    def _():
                        dk_ref[pl.ds(rp, CHUNK), ls] += dkp
                        dv_ref[pl.ds(rp, CHUNK), ls] += dvp
            return carry

        lax.fori_loop(0, nblk, blk, 0)

    spec = pl.BlockSpec((seg, ATTN_W), lambda r: (0, r))
    return pl.pallas_call(
        body, name=f"attn_bwd_d{dil}", grid=(dil,),
        in_specs=[spec] * 6, out_specs=[spec] * 3,
        out_shape=[_sds((seg, dil * ATTN_W), F32)] * 3,
        compiler_params=_seq_params(),
    )(qv, kv, vv, dov, deltav, lsev)


def _inproj_bwd(dqs, dks, dvs, du, dvs_sgu, pos, x, dx1, w_in, g_pre):
    def body(dq1, dq2, dq3, dk1, dk2, dk3, dv1, dv2, dv3, du_ref, dvs_ref, pos_ref, x_ref, dx1_ref, w_ref, g_ref,
             dproj_ref, gx_ref, dg_ref):
        @pl.when(pl.program_id(0) == 0)
        def _():
            dg_ref[...] = jnp.zeros_like(dg_ref)

        tabs = _rot_tables(pos_ref[...])
        dproj_ref[:, 0:512] = _rope_bwd(dq1[...] + dq2[...] + dq3[...], tabs).astype(BF16)
        dproj_ref[:, 512:1024] = _rope_bwd(dk1[...] + dk2[...] + dk3[...], tabs).astype(BF16)
        dproj_ref[:, 1024:1536] = (dv1[...] + dv2[...] + dv3[...]).astype(BF16)
        dproj_ref[:, 1536:2048] = du_ref[...]
        dproj_ref[:, 2048:2560] = dvs_ref[...]
        dh = jnp.zeros((TM, D_MODEL), F32)
        for s in range(N_SHARD):
            dh = dh + _dot_nt(dproj_ref[:, s * IN_S:(s + 1) * IN_S], w_ref[s])
        g = g_ref[...]
        xh, r = _rms_stats(x_ref[...])
        dx, dg = _rms_bwd(xh, r, g, dh)
        dg_ref[...] += dg
        gx_ref[...] = dx1_ref[...] + dx

    return pl.pallas_call(
        body, name="inproj_bwd", grid=(SEQ // TM,),
        in_specs=[_rows(512)] * 11 + [_rows(1), _rows(D_MODEL), _rows(D_MODEL),
                                     _resident((N_SHARD, D_MODEL, IN_S)), _resident((1, D_MODEL))],
        out_specs=[_rows(PROJ_W), _rows(D_MODEL), _acc(D_MODEL)],
        out_shape=[_sds((SEQ, PROJ_W), BF16), _sds((SEQ, D_MODEL), F32), _sds((1, D_MODEL), F32)],
        compiler_params=_seq_params(),
    )(*dqs, *dks, *dvs, du, dvs_sgu, pos, x, dx1, w_in, g_pre)


def _to_view(a, dil):
    return a if dil == 1 else a.reshape(SEQ // dil, dil * a.shape[1])


def _from_view(a, dil):
    return a if dil == 1 else a.reshape(SEQ, a.shape[1] // dil)


def _local_step(x, pos, target, w_in, w_out, w_gate, w_up, w_down, small):
    b_t = small["sgu_b_spatial"].T
    h, q, k, v, u, vs = _inproj_fwd(x, pos, small["pre_mix_norm"], w_in)
    sgu = _sgu_fwd(u, vs, small["sgu_ln_gain"], small["sgu_ln_bias"], small["sgu_w_spatial"], b_t)
    views = [tuple(_to_view(t, dil) for t in (q, k, v)) for dil in DILATIONS]
    o_list, l_list = [], []
    for dil, (qv, kv, vv) in zip(DILATIONS, views):
        o, l = _attn_fwd(qv, kv, vv, dil)
        o_list.append(_from_view(o, dil))
        l_list.append(_from_view(l, dil))
    attn, lse, mixed, y, x1 = _mix_out_fwd(o_list, l_list, sgu, x, w_out, small["attn_out_norm"],
                                           small["sgu_out_norm"], small["post_mix_norm"])
    h2, a, dg, dup, df, dx1, loss_cols, d_pre_ffn, d_post_ffn = _ffn_fwd_bwd(
        x1, target, w_gate, w_up, w_down, small["pre_ffn_norm"], small["post_ffn_norm"])

    full_tok = pl.BlockSpec((SEQ, D_MODEL), lambda s: (0, 0), pipeline_mode=pl.Buffered(1))
    ff_tok = pl.BlockSpec((1, SEQ, FF_S), lambda s: (s, 0, 0))
    gw_gate = _wgrad(h2, dg, full_tok, ff_tok, (D_MODEL, FF_S), "wgrad_gate")
    gw_up = _wgrad(h2, dup, full_tok, ff_tok, (D_MODEL, FF_S), "wgrad_up")
    gw_down = _wgrad(a, df, ff_tok, full_tok, (FF_S, D_MODEL), "wgrad_down")

    dy, dattn, delta, dsgu, d_post_mix, d_attn_norm, d_sgu_norm = _outproj_bwd(
        dx1, y, attn, sgu, w_out, small["post_mix_norm"], small["attn_out_norm"], small["sgu_out_norm"])
    gw_out = _wgrad(mixed, dy, pl.BlockSpec((SEQ, OUT_S), lambda s: (0, s)), full_tok, (OUT_S, D_MODEL), "wgrad_out")
    du, dvs_sgu, d_w_sp, d_b_sp, d_ln_gain, d_ln_bias = _sgu_bwd(
        u, vs, dsgu, small["sgu_ln_gain"], small["sgu_ln_bias"], small["sgu_w_spatial"], b_t)

    dqs, dks, dvs = [], [], []
    for dil, (qv, kv, vv) in zip(DILATIONS, views):
        dq, dk, dv = _attn_bwd(qv, kv, vv, _to_view(dattn, dil), _to_view(delta, dil), _to_view(lse, dil), dil)
        dqs.append(_from_view(dq, dil))
        dks.append(_from_view(dk, dil))
        dvs.append(_from_view(dv, dil))
    dproj, grad_x, d_pre_mix = _inproj_bwd(dqs, dks, dvs, du, dvs_sgu, pos, x, dx1, w_in, small["pre_mix_norm"])
    gw_in = _wgrad(h, dproj, full_tok, pl.BlockSpec((SEQ, IN_S), lambda s: (0, s)), (D_MODEL, IN_S), "wgrad_in")

    small_grads = {
        "pre_mix_norm": d_pre_mix, "sgu_ln_gain": d_ln_gain, "sgu_ln_bias": d_ln_bias, "sgu_w_spatial": d_w_sp,
        "sgu_b_spatial": d_b_sp, "attn_out_norm": d_attn_norm, "sgu_out_norm": d_sgu_norm,
        "post_mix_norm": d_post_mix, "pre_ffn_norm": d_pre_ffn, "post_ffn_norm": d_post_ffn,
    }
    return loss_cols, grad_x, (gw_in, gw_out, gw_gate, gw_up, gw_down), small_grads


def _coords():
    return lax.axis_index("x"), lax.axis_index("y"), lax.axis_index("c")


def _other_chips(x, y):
    return [(1 - x, y), (x, 1 - y), (1 - x, 1 - y)]


def _comm_call(body, name, n_in, out_shape, scratch_shapes):
    return pl.pallas_call(
        body, name=name, in_specs=[ANY] * n_in, out_specs=[ANY] * len(out_shape), out_shape=out_shape,
        scratch_shapes=scratch_shapes,
        compiler_params=pltpu.CompilerParams(has_side_effects=True),
    )


def _gather_weights(shards):
    n = len(shards)

    def body(*refs):
        ins, outs = refs[:n], refs[n:2 * n]
        send_sems, recv_sems, local_sems = refs[2 * n:]
        x, y, c = _coords()
        s_me = 2 * x + y
        chips = _other_chips(x, y)
        sibling = (x, y, 1 - c)

        def half(w, cc):
            hw = shards[w].shape[0] // 2
            return pl.ds(cc * hw, hw)

        def copy(k, w, shard, cc, to):
            src = ins[w].at[half(w, cc)] if shard is None else outs[w].at[shard, half(w, cc)]
            dst = outs[w].at[s_me if shard is None else shard, half(w, cc)]
            return pltpu.make_async_remote_copy(src_ref=src, dst_ref=dst, send_sem=send_sems.at[k],
                                                recv_sem=recv_sems.at[k], device_id=to, device_id_type=MESH)

        local = [pltpu.make_async_copy(ins[w], outs[w].at[s_me], local_sems.at[w]) for w in range(n)]
        for cp in local:
            cp.start()
        first = [copy(j * n + w, w, None, c, (cx, cy, c)) for j, (cx, cy) in enumerate(chips) for w in range(n)]
        for cp in first:
            cp.start()
        passed = []
        for j, (cx, cy) in enumerate(chips):
            for w in range(n):
                copy(j * n + w, w, 2 * cx + cy, c, (x, y, c)).wait_recv()
                fw = copy((3 + j) * n + w, w, 2 * cx + cy, c, sibling)
                fw.start()
                passed.append(fw)
        for j, (cx, cy) in enumerate(chips):
            for w in range(n):
                copy((3 + j) * n + w, w, 2 * cx + cy, 1 - c, (x, y, c)).wait_recv()
        for cp in first + passed:
            cp.wait_send()
        for cp in local:
            cp.wait()

    out_shape = [_sds((N_SHARD,) + s.shape, s.dtype) for s in shards]
    scratch = [pltpu.SemaphoreType.DMA((6 * n,)), pltpu.SemaphoreType.DMA((6 * n,)), pltpu.SemaphoreType.DMA((n,))]
    return _comm_call(body, "comm_gather_weights", n, out_shape, scratch)(*shards)


def _rs_to_sibling(gws):
    n = len(gws)

    def body(*refs):
        ins, outs = refs[:n], refs[n:2 * n]
        send_sems, recv_sems = refs[2 * n:]
        x, y, c = _coords()
        copies = []
        for w in range(n):
            hw = gws[w].shape[1] // 2
            copies.append(pltpu.make_async_remote_copy(
                src_ref=ins[w].at[:, pl.ds((1 - c) * hw, hw), :], dst_ref=outs[w], send_sem=send_sems.at[w],
                recv_sem=recv_sems.at[w], device_id=(x, y, 1 - c), device_id_type=MESH))
        for cp in copies:
            cp.start()
        for cp in copies:
            cp.wait()

    out_shape = [_sds((N_SHARD, g.shape[1] // 2, g.shape[2]), g.dtype) for g in gws]
    scratch = [pltpu.SemaphoreType.DMA((n,)), pltpu.SemaphoreType.DMA((n,))]
    return _comm_call(body, "comm_rs_sibling", n, out_shape, scratch)(*gws)


def _rs_chip_sum(gw, recv, core):
    _, rows, cols = gw.shape
    hw = rows // 2

    def body(c_ref, g_ref, r_ref, o_ref):
        o_ref[...] = (g_ref[...] + r_ref[...]).astype(BF16)

    return pl.pallas_call(
        body, name="rs_chip_sum",
        grid_spec=pltpu.PrefetchScalarGridSpec(
            num_scalar_prefetch=1, grid=(N_SHARD,),
            in_specs=[pl.BlockSpec((1, hw, cols), lambda s, c_ref: (s, c_ref[0], 0)),
                      pl.BlockSpec((1, hw, cols), lambda s, c_ref: (s, 0, 0))],
            out_specs=pl.BlockSpec((1, hw, cols), lambda s, c_ref: (s, 0, 0))),
        out_shape=_sds((N_SHARD, hw, cols), BF16),
        compiler_params=_seq_params(),
    )(core, gw, recv)


def _rs_between_chips(pbs):
    n = len(pbs)

    def body(*refs):
        ins, outs = refs[:n], refs[n:2 * n]
        send_sems, recv_sems = refs[2 * n:]
        x, y, c = _coords()
        copies = []
        for j, (cx, cy) in enumerate(_other_chips(x, y)):
            for w in range(n):
                copies.append(pltpu.make_async_remote_copy(
                    src_ref=ins[w].at[2 * cx + cy], dst_ref=outs[w].at[j], send_sem=send_sems.at[j * n + w],
                    recv_sem=recv_sems.at[j * n + w], device_id=(cx, cy, c), device_id_type=MESH))
        for cp in copies:
            cp.start()
        for cp in copies:
            cp.wait()

    out_shape = [_sds((3,) + p.shape[1:], p.dtype) for p in pbs]
    scratch = [pltpu.SemaphoreType.DMA((3 * n,)), pltpu.SemaphoreType.DMA((3 * n,))]
    return _comm_call(body, "comm_rs_chips", n, out_shape, scratch)(*pbs)


def _rs_final_sum(gw, recv_sib, recv_chips, shard_core):
    _, rows, cols = gw.shape
    hw = rows // 2

    def body(sc_ref, g_ref, r_ref, rc_ref, o_ref):
        acc = g_ref[0] + r_ref[0]
        for j in range(3):
            acc = acc + rc_ref[j].astype(F32)
        o_ref[...] = acc

    return pl.pallas_call(
        body, name="rs_final_sum",
        grid_spec=pltpu.PrefetchScalarGridSpec(
            num_scalar_prefetch=1, grid=(1,),
            in_specs=[pl.BlockSpec((1, hw, cols), lambda i, sc: (sc[0], sc[1], 0)),
                      pl.BlockSpec((1, hw, cols), lambda i, sc: (sc[0], 0, 0)),
                      pl.BlockSpec((3, hw, cols), lambda i, sc: (0, 0, 0))],
            out_specs=pl.BlockSpec((hw, cols), lambda i, sc: (0, 0))),
        out_shape=_sds((hw, cols), F32),
        compiler_params=_seq_params(),
    )(shard_core, gw, recv_sib, recv_chips)


def _rs_join_halves(halves):
    n = len(halves)

    def body(*refs):
        ins, outs = refs[:n], refs[n:2 * n]
        send_sems, recv_sems, local_sems = refs[2 * n:]
        x, y, c = _coords()
        local, remote = [], []
        for w in range(n):
            hw = halves[w].shape[0]
            local.append(pltpu.make_async_copy(ins[w], outs[w].at[pl.ds(c * hw, hw)], local_sems.at[w]))
            remote.append(pltpu.make_async_remote_copy(
                src_ref=ins[w], dst_ref=outs[w].at[pl.ds(c * hw, hw)], send_sem=send_sems.at[w],
                recv_sem=recv_sems.at[w], device_id=(x, y, 1 - c), device_id_type=MESH))
        for cp in local + remote:
            cp.start()
        for w in range(n):
            hw = halves[w].shape[0]
            remote[w].wait_send()
            pltpu.make_async_remote_copy(
                src_ref=ins[w], dst_ref=outs[w].at[pl.ds((1 - c) * hw, hw)], send_sem=send_sems.at[w],
                recv_sem=recv_sems.at[w], device_id=(x, y, c), device_id_type=MESH).wait_recv()
            local[w].wait()

    out_shape = [_sds((2 * h.shape[0], h.shape[1]), h.dtype) for h in halves]
    scratch = [pltpu.SemaphoreType.DMA((n,)), pltpu.SemaphoreType.DMA((n,)), pltpu.SemaphoreType.DMA((n,))]
    return _comm_call(body, "comm_rs_join", n, out_shape, scratch)(*halves)


def _allreduce_small(buf):
    rows, cols = buf.shape

    def body(in_ref, out_ref, slots, send_sems, recv_sems):
        x, y, c = _coords()
        me = 4 * x + 2 * y + c
        copies, peers = [], []
        for k in range(1, 8):
            px = 1 - x if (k >> 2) & 1 else x
            py = 1 - y if (k >> 1) & 1 else y
            pc = 1 - c if k & 1 else c
            peers.append(4 * px + 2 * py + pc)
            copies.append(pltpu.make_async_remote_copy(
                src_ref=in_ref, dst_ref=slots.at[me], send_sem=send_sems.at[k - 1], recv_sem=recv_sems.at[k - 1],
                device_id=(px, py, pc), device_id_type=MESH))
        for cp in copies:
            cp.start()
        slots[me] = in_ref[...]
        for k in range(7):
            pltpu.make_async_remote_copy(
                src_ref=in_ref, dst_ref=slots.at[peers[k]], send_sem=send_sems.at[k], recv_sem=recv_sems.at[k],
                device_id=(x, y, c), device_id_type=MESH).wait_recv()
        for cp in copies:
            cp.wait_send()
        acc = slots[0]
        for i in range(1, 8):
            acc = acc + slots[i]
        out_ref[...] = acc

    vmem = pl.BlockSpec(memory_space=pltpu.VMEM)
    return pl.pallas_call(
        body, name="comm_allreduce_small", in_specs=[vmem], out_specs=vmem, out_shape=_sds((rows, cols), F32),
        scratch_shapes=[pltpu.VMEM((8, rows, cols), F32), pltpu.SemaphoreType.DMA((7,)), pltpu.SemaphoreType.DMA((7,))],
        compiler_params=pltpu.CompilerParams(has_side_effects=True, vmem_limit_bytes=VMEM_LIMIT),
    )(buf)


def _adamw(w, g, m, v, block_rows, name):
    rows, cols = w.shape

    def body(w_ref, g_ref, m_ref, v_ref, d_ref, nm_ref, nv_ref):
        g = g_ref[...]
        m = ADAM_B1 * m_ref[...] + (1.0 - ADAM_B1) * g
        v = ADAM_B2 * v_ref[...] + (1.0 - ADAM_B2) * (g * g)
        m_hat = m / (1.0 - ADAM_B1 ** ADAM_STEP)
        v_hat = v / (1.0 - ADAM_B2 ** ADAM_STEP)
        d_ref[...] = -ADAM_LR * (m_hat / (jnp.sqrt(v_hat) + ADAM_EPS) + ADAM_WD * w_ref[...])
        nm_ref[...] = m
        nv_ref[...] = v

    spec = pl.BlockSpec((block_rows, cols), lambda i: (i, 0))
    return pl.pallas_call(
        body, name=name, grid=(rows // block_rows,), in_specs=[spec] * 4, out_specs=[spec] * 3,
        out_shape=[_sds((rows, cols), F32)] * 3,
        compiler_params=_seq_params(),
    )(w, g, m, v)


WEIGHTS = ("pre_mix_norm", "w_in", "sgu_ln_gain", "sgu_ln_bias", "sgu_w_spatial", "sgu_b_spatial", "attn_out_norm",
           "sgu_out_norm", "w_out", "post_mix_norm", "pre_ffn_norm", "w_gate", "w_up", "w_down", "post_ffn_norm")
BIG = ("w_in", "w_out", "w_gate", "w_up", "w_down")
BIG_ADAM_ROWS = {"w_in": 256, "w_out": 128, "w_gate": 256, "w_up": 256, "w_down": 352}
SMALL = ("pre_mix_norm", "post_mix_norm", "pre_ffn_norm", "post_ffn_norm", "sgu_ln_gain", "sgu_ln_bias",
         "attn_out_norm", "sgu_out_norm", "sgu_w_spatial", "sgu_b_spatial")


def _pack_small(d):
    flat = [d[n].reshape(-1) for n in SMALL]
    used = sum(f.shape[0] for f in flat)
    flat.append(jnp.zeros((SMALL_ROWS * 1024 - used,), F32))
    return jnp.concatenate(flat).reshape(SMALL_ROWS, 1024)


def _unpack_small(buf, shapes):
    flat = buf.reshape(-1)
    out, off = {}, 0
    for n in SMALL:
        size = int(np.prod(shapes[n]))
        out[n] = flat[off:off + size].reshape(shapes[n])
        off += size
    return out


def kernel(x, positions, pre_mix_norm, w_in, sgu_ln_gain, sgu_ln_bias, sgu_w_spatial, sgu_b_spatial, attn_out_norm, sgu_out_norm, w_out, post_mix_norm, pre_ffn_norm, w_gate, w_up, w_down, post_ffn_norm, loss_target, m_pre_mix_norm, m_w_in, m_sgu_ln_gain, m_sgu_ln_bias, m_sgu_w_spatial, m_sgu_b_spatial, m_attn_out_norm, m_sgu_out_norm, m_w_out, m_post_mix_norm, m_pre_ffn_norm, m_w_gate, m_w_up, m_w_down, m_post_ffn_norm, v_pre_mix_norm, v_w_in, v_sgu_ln_gain, v_sgu_ln_bias, v_sgu_w_spatial, v_sgu_b_spatial, v_attn_out_norm, v_sgu_out_norm, v_w_out, v_post_mix_norm, v_pre_ffn_norm, v_w_gate, v_w_up, v_w_down, v_post_ffn_norm):
    a = dict(locals())
    cx, cy, cc = _coords()
    core = jnp.stack([cc]).astype(jnp.int32)
    shard_core = jnp.stack([2 * cx + cy, cc]).astype(jnp.int32)

    full = _gather_weights([a[n][0].astype(BF16) for n in BIG])
    small = {n: (a[n][0] if a[n].ndim > 2 else a[n]) for n in SMALL}
    loss_cols, grad_x, gws, small_grads = _local_step(
        x[0], positions.reshape(SEQ, 1), loss_target[0], *full, small)
    loss = lax.psum(jnp.sum(loss_cols) * np.float32(0.5 / D_MODEL), ("x", "y", "c"))

    recv_sib = _rs_to_sibling(list(gws))
    chip_part = [_rs_chip_sum(g, r, core) for g, r in zip(gws, recv_sib)]
    recv_chips = _rs_between_chips(chip_part)
    halves = [_rs_final_sum(g, r, rc, shard_core) for g, r, rc in zip(gws, recv_sib, recv_chips)]
    big_grads = dict(zip(BIG, _rs_join_halves(halves)))

    shapes = {n: a[n].shape for n in SMALL}
    small_sum = _allreduce_small(_pack_small(small_grads))

    grads, deltas, new_m, new_v = {}, {}, {}, {}
    for n in BIG:
        grads[n] = big_grads[n][None]
        d, nm, nv = _adamw(a[n][0], big_grads[n], a["m_" + n][0], a["v_" + n][0], BIG_ADAM_ROWS[n], "adamw_" + n)
        deltas[n], new_m[n], new_v[n] = d[None], nm[None], nv[None]
    d, nm, nv = _adamw(_pack_small({n: a[n] for n in SMALL}), small_sum, _pack_small({n: a["m_" + n] for n in SMALL}),
                       _pack_small({n: a["v_" + n] for n in SMALL}), SMALL_ROWS, "adamw_small")
    grads.update(_unpack_small(small_sum, shapes))
    deltas.update(_unpack_small(d, shapes))
    new_m.update(_unpack_small(nm, shapes))
    new_v.update(_unpack_small(nv, shapes))
    return (loss, grad_x[None], *[grads[n] for n in WEIGHTS], *[deltas[n] for n in WEIGHTS],
            *[new_m[n] for n in WEIGHTS], *[new_v[n] for n in WEIGHTS])
```

```python
import numpy as np
import jax
import jax.numpy as jnp
from jax import lax
from jax.experimental import pallas as pl
from jax.experimental.pallas import tpu as pltpu

F32 = jnp.float32
BF16 = jnp.bfloat16

SEQ = 2048
D_MODEL = 1024
HEAD_DIM = 64
ATTN_W = 512
SGU_W = 512
SGU_GROUPS = 8
CHUNK = 128
DILATIONS = (1, 4, 16)
N_SHARD = 4
IN_S = 640
OUT_S = 256
FF_S = 704
PROJ_W = N_SHARD * IN_S
RMS_EPS = 1e-6
LN_EPS = 1e-5
ROPE_THETA = 500000.0
ATTN_SCALE = 1.0 / np.sqrt(HEAD_DIM)
NEG = -1e30
TM = 256
VMEM_LIMIT = 56 * 1024 * 1024
SMALL_ROWS = 136

ADAM_LR = 0.001
ADAM_B1 = 0.9
ADAM_B2 = 0.999
ADAM_EPS = 1e-08
ADAM_WD = 0.01
ADAM_STEP = 10

MESH = pl.DeviceIdType.MESH
ANY = pl.BlockSpec(memory_space=pl.ANY)


def _dot(a, b):
    return jnp.dot(a, b, preferred_element_type=F32)


def _dot_nt(a, b):
    return lax.dot_general(a, b, (((1,), (1,)), ((), ())), preferred_element_type=F32)


def _dot_tn(a, b):
    return lax.dot_general(a, b, (((0,), (0,)), ((), ())), preferred_element_type=F32)


def _dot_exact(a, b):
    return jnp.dot(a, b, preferred_element_type=F32, precision=lax.Precision.HIGHEST)


def _rms_stats(x):
    r = lax.rsqrt(jnp.mean(x * x, axis=-1, keepdims=True) + RMS_EPS)
    return x * r, r


def _rms_bwd(xh, r, gain, dy):
    dxh = dy * gain
    dx = r * (dxh - xh * jnp.mean(dxh * xh, axis=-1, keepdims=True))
    return dx, jnp.sum(dy * xh, axis=0, keepdims=True)


_ERF_ALPHA = (-2.72614225801306e-10, 2.77068142495902e-08, -2.10102402082508e-06, -5.69250639462346e-05,
              -7.34990630326855e-04, -2.95459980854025e-03, -1.60960333262415e-02)
_ERF_BETA = (-1.45660718464996e-05, -2.13374055278905e-04, -1.68282697438203e-03, -7.37332916720468e-03,
             -1.42647390514189e-02)


def _erf(x):
    x = jnp.clip(x, -4.0, 4.0)
    x2 = x * x
    p = jnp.full_like(x, _ERF_ALPHA[0])
    for a in _ERF_ALPHA[1:]:
        p = p * x2 + a
    q = jnp.full_like(x, _ERF_BETA[0])
    for b in _ERF_BETA[1:]:
        q = q * x2 + b
    return x * p / q


def _gelu(x):
    return 0.5 * x * (1.0 + _erf(x * np.float32(1.0 / np.sqrt(2.0))))


def _gelu_grad(x):
    cdf = 0.5 * (1.0 + _erf(x * np.float32(1.0 / np.sqrt(2.0))))
    pdf = jnp.exp(-0.5 * x * x) * np.float32(1.0 / np.sqrt(2.0 * np.pi))
    return cdf + x * pdf


def _sigmoid(x):
    return 1.0 / (1.0 + jnp.exp(-x))


_INV_FREQ = tuple(float(np.float32(ROPE_THETA ** (-2.0 * j / 16.0))) for j in range(8))


def _rot_tables(pos):
    lane = lax.broadcasted_iota(jnp.int32, (1, 128), 1)
    d = lane & 63
    j = d & 7
    inv = jnp.zeros((1, 128), F32)
    for jj in range(8):
        inv = jnp.where(j == jj, _INV_FREQ[jj], inv)
    ang = pos.astype(F32) * inv
    c = jnp.cos(ang)
    s = jnp.sin(ang)
    cos_t = jnp.where(d < 16, c, 1.0)
    sin_a = jnp.where(d < 8, -s, 0.0)
    sin_b = jnp.where((d >= 8) & (d < 16), s, 0.0)
    return tuple(jnp.tile(t, (1, 4)) for t in (cos_t, sin_a, sin_b))


def _rope(x, tabs):
    cos_t, sin_a, sin_b = tabs
    return x * cos_t + pltpu.roll(x, 504, 1) * sin_a + pltpu.roll(x, 8, 1) * sin_b


def _rope_bwd(dy, tabs):
    cos_t, sin_a, sin_b = tabs
    return dy * cos_t + pltpu.roll(dy * sin_a, 8, 1) + pltpu.roll(dy * sin_b, 504, 1)


def _left_half():
    return lax.broadcasted_iota(jnp.int32, (CHUNK, CHUNK), 1) < HEAD_DIM


def _group_ones():
    lane = lax.broadcasted_iota(jnp.int32, (SGU_GROUPS, SGU_W), 1)
    row = lax.broadcasted_iota(jnp.int32, (SGU_GROUPS, SGU_W), 0)
    return ((lane >> 6) == row).astype(F32)


def _masked_spatial(w_ref):
    row = lax.broadcasted_iota(jnp.int32, (CHUNK, CHUNK), 0)
    col = lax.broadcasted_iota(jnp.int32, (CHUNK, CHUNK), 1)
    return [jnp.where(col <= row, w_ref[g], 0.0).astype(BF16) for g in range(SGU_GROUPS)]


def _sgu_core(u, vs, lg, lb, wm, bias_full):
    tm = u.shape[0]
    gu = _gelu(u)
    gv = _gelu(vs)
    mu = jnp.mean(gv, axis=-1, keepdims=True)
    xc = gv - mu
    rstd = lax.rsqrt(jnp.mean(xc * xc, axis=-1, keepdims=True) + LN_EPS)
    xh = xc * rstd
    vnb = (xh * lg + lb).astype(BF16)
    left = _left_half()
    rows = []
    for c in range(tm // CHUNK):
        pieces = []
        for p in range(4):
            vp = vnb[c * CHUNK:(c + 1) * CHUNK, p * 128:(p + 1) * 128]
            pieces.append(jnp.where(left, _dot(wm[2 * p], vp), _dot(wm[2 * p + 1], vp)))
        rows.append(jnp.concatenate(pieces, axis=1) + bias_full)
    mixed = jnp.concatenate(rows, axis=0)
    return gu, xh, rstd, vnb, mixed


def _resident(shape):
    n = len(shape)
    return pl.BlockSpec(shape, lambda *_: (0,) * n, pipeline_mode=pl.Buffered(1))


def _rows(ncol, tm=TM):
    return pl.BlockSpec((tm, ncol), lambda i: (i, 0))


def _rows3(nlead, ncol, tm=TM):
    return pl.BlockSpec((nlead, tm, ncol), lambda i: (0, i, 0))


def _acc(ncol, nrow=1):
    return pl.BlockSpec((nrow, ncol), lambda i: (0, 0))


def _seq_params():
    return pltpu.CompilerParams(dimension_semantics=("arbitrary",), vmem_limit_bytes=VMEM_LIMIT)


def _sds(shape, dtype):
    return jax.ShapeDtypeStruct(shape, dtype)


def _inproj_fwd(x, pos, g_pre, w_in):
    def body(x_ref, pos_ref, g_ref, w_ref, h_ref, q_ref, k_ref, v_ref, u_ref, vs_ref):
        xh, _ = _rms_stats(x_ref[...])
        h = (xh * g_ref[...]).astype(BF16)
        h_ref[...] = h
        proj = jnp.concatenate([_dot(h, w_ref[s]) for s in range(N_SHARD)], axis=1)
        tabs = _rot_tables(pos_ref[...])
        q_ref[...] = _rope(proj[:, 0:512], tabs).astype(BF16)
        k_ref[...] = _rope(proj[:, 512:1024], tabs).astype(BF16)
        v_ref[...] = proj[:, 1024:1536].astype(BF16)
        u_ref[...] = proj[:, 1536:2048]
        vs_ref[...] = proj[:, 2048:2560]

    return pl.pallas_call(
        body, name="inproj_fwd", grid=(SEQ // TM,),
        in_specs=[_rows(D_MODEL), _rows(1), _resident((1, D_MODEL)), _resident((N_SHARD, D_MODEL, IN_S))],
        out_specs=[_rows(D_MODEL), _rows(512), _rows(512), _rows(512), _rows(512), _rows(512)],
        out_shape=[_sds((SEQ, D_MODEL), BF16), _sds((SEQ, 512), BF16), _sds((SEQ, 512), BF16),
                   _sds((SEQ, 512), BF16), _sds((SEQ, 512), F32), _sds((SEQ, 512), F32)],
        compiler_params=_seq_params(),
    )(x, pos, g_pre, w_in)


def _sgu_fwd(u, vs, lg, lb, w_sp, b_t):
    def body(u_ref, vs_ref, lg_ref, lb_ref, w_ref, bt_ref, out_ref):
        wm = _masked_spatial(w_ref)
        bias_full = _dot_exact(bt_ref[...], _group_ones())
        gu, _, _, _, mixed = _sgu_core(u_ref[...], vs_ref[...], lg_ref[...], lb_ref[...], wm, bias_full)
        out_ref[...] = gu * mixed

    return pl.pallas_call(
        body, name="sgu_fwd", grid=(SEQ // TM,),
        in_specs=[_rows(SGU_W), _rows(SGU_W), _resident((1, SGU_W)), _resident((1, SGU_W)),
                  _resident((SGU_GROUPS, CHUNK, CHUNK)), _resident((CHUNK, SGU_GROUPS))],
        out_specs=_rows(SGU_W),
        out_shape=_sds((SEQ, SGU_W), F32),
        compiler_params=_seq_params(),
    )(u, vs, lg, lb, w_sp, b_t)


def _block_masks():
    row = lax.broadcasted_iota(jnp.int32, (CHUNK, CHUNK), 0)
    col = lax.broadcasted_iota(jnp.int32, (CHUNK, CHUNK), 1)
    return col <= row, col >= row


def _attn_fwd(qv, kv, vv, dil):
    seg = SEQ // dil
    nblk = seg // CHUNK

    def body(q_ref, k_ref, v_ref, o_ref, l_ref):
        left = _left_half()
        m_cur, m_prev = _block_masks()

        def blk(b, carry):
            r0 = pl.multiple_of(b * CHUNK, CHUNK)
            rp = pl.multiple_of(jnp.maximum(b - 1, 0) * CHUNK, CHUNK)
            prev_ok = m_prev & (b > 0)
            for hp in range(4):
                ls = slice(hp * 128, (hp + 1) * 128)
                qp = q_ref[pl.ds(r0, CHUNK), ls]
                kc = k_ref[pl.ds(r0, CHUNK), ls]
                vc = v_ref[pl.ds(r0, CHUNK), ls]
                if nblk > 1:
                    kp = k_ref[pl.ds(rp, CHUNK), ls]
                    vp = v_ref[pl.ds(rp, CHUNK), ls]
                outs, lses = [], []
                for hm in (left, ~left):
                    qh = jnp.where(hm, qp, jnp.zeros_like(qp))
                    sc = jnp.where(m_cur, _dot_nt(qh, kc) * ATTN_SCALE, NEG)
                    m = jnp.max(sc, axis=-1, keepdims=True)
                    if nblk > 1:
                        sp = jnp.where(prev_ok, _dot_nt(qh, kp) * ATTN_SCALE, NEG)
                        m = jnp.maximum(m, jnp.max(sp, axis=-1, keepdims=True))
                    pc = jnp.exp(sc - m)
                    den = jnp.sum(pc, axis=-1, keepdims=True)
                    acc = _dot(pc.astype(BF16), vc)
                    if nblk > 1:
                        pp = jnp.exp(sp - m)
                        den = den + jnp.sum(pp, axis=-1, keepdims=True)
                        acc = acc + _dot(pp.astype(BF16), vp)
                    outs.append(acc / den)
                    lses.append(m + jnp.log(den))
                o_ref[pl.ds(r0, CHUNK), ls] = jnp.where(left, outs[0], outs[1])
                l_ref[pl.ds(r0, CHUNK), ls] = jnp.where(left, lses[0], lses[1])
            return carry

        lax.fori_loop(0, nblk, blk, 0)

    spec = pl.BlockSpec((seg, ATTN_W), lambda r: (0, r))
    return pl.pallas_call(
        body, name=f"attn_fwd_d{dil}", grid=(dil,),
        in_specs=[spec, spec, spec], out_specs=[spec, spec],
        out_shape=[_sds((seg, dil * ATTN_W), F32), _sds((seg, dil * ATTN_W), F32)],
        compiler_params=_seq_params(),
    )(qv, kv, vv)


def _mix_out_fwd(o_list, l_list, sgu, x, w_out, g_attn, g_sgu, g_post):
    def body(o1, o2, o3, l1, l2, l3, sgu_ref, x_ref, w_ref, ga_ref, gs_ref, gp_ref,
             attn_ref, lse_ref, mixed_ref, y_ref, x1_ref):
        ls = [l1[...], l2[...], l3[...]]
        m = jnp.maximum(jnp.maximum(ls[0], ls[1]), ls[2])
        es = [jnp.exp(l - m) for l in ls]
        den = es[0] + es[1] + es[2]
        attn = (es[0] * o1[...] + es[1] * o2[...] + es[2] * o3[...]) / den
        attn_ref[...] = attn
        lse_ref[...] = m + jnp.log(den)
        ah, _ = _rms_stats(attn)
        sh, _ = _rms_stats(sgu_ref[...])
        mixed = jnp.concatenate([ah * ga_ref[...], sh * gs_ref[...]], axis=1).astype(BF16)
        mixed_ref[...] = mixed
        y = _dot(mixed[:, 0:OUT_S], w_ref[0])
        for s in range(1, N_SHARD):
            y = y + _dot(mixed[:, s * OUT_S:(s + 1) * OUT_S], w_ref[s])
        y_ref[...] = y
        yh, _ = _rms_stats(y)
        x1_ref[...] = x_ref[...] + yh * gp_ref[...]

    return pl.pallas_call(
        body, name="mix_out_fwd", grid=(SEQ // TM,),
        in_specs=[_rows(512)] * 7 + [_rows(D_MODEL), _resident((N_SHARD, OUT_S, D_MODEL)),
                                    _resident((1, 512)), _resident((1, 512)), _resident((1, D_MODEL))],
        out_specs=[_rows(512), _rows(512), _rows(D_MODEL), _rows(D_MODEL), _rows(D_MODEL)],
        out_shape=[_sds((SEQ, 512), F32), _sds((SEQ, 512), F32), _sds((SEQ, D_MODEL), BF16),
                   _sds((SEQ, D_MODEL), F32), _sds((SEQ, D_MODEL), F32)],
        compiler_params=_seq_params(),
    )(*o_list, *l_list, sgu, x, w_out, g_attn, g_sgu, g_post)


def _ffn_fwd_bwd(x1, target, w_gate, w_up, w_down, g_pre, g_post):
    def body(x1_ref, t_ref, wg_ref, wu_ref, wd_ref, gpf_ref, gpo_ref,
             h2_ref, a_ref, dg_ref, dup_ref, df_ref, dx1_ref, loss_ref, dgpf_ref, dgpo_ref, g_scr, up_scr):
        @pl.when(pl.program_id(0) == 0)
        def _():
            loss_ref[...] = jnp.zeros_like(loss_ref)
            dgpf_ref[...] = jnp.zeros_like(dgpf_ref)
            dgpo_ref[...] = jnp.zeros_like(dgpo_ref)

        x1 = x1_ref[...]
        gpf = gpf_ref[...]
        gpo = gpo_ref[...]
        xh, r = _rms_stats(x1)
        h2 = (xh * gpf).astype(BF16)
        h2_ref[...] = h2
        f = jnp.zeros((TM, D_MODEL), F32)
        for s in range(N_SHARD):
            g = _dot(h2, wg_ref[s])
            up = _dot(h2, wu_ref[s])
            g_scr[s] = g
            up_scr[s] = up
            a = (g * _sigmoid(g) * up).astype(BF16)
            a_ref[s] = a
            f = f + _dot(a, wd_ref[s])
        fh, rf = _rms_stats(f)
        diff = x1 + fh * gpo - t_ref[...]
        loss_ref[...] += jnp.sum(diff * diff, axis=0, keepdims=True)
        dout = diff * np.float32(1.0 / D_MODEL)
        df, dgpo = _rms_bwd(fh, rf, gpo, dout)
        dgpo_ref[...] += dgpo
        dfb = df.astype(BF16)
        df_ref[...] = dfb
        dh2 = jnp.zeros((TM, D_MODEL), F32)
        for s in range(N_SHARD):
            da = _dot_nt(dfb, wd_ref[s])
            g = g_scr[s]
            up = up_scr[s]
            sg = _sigmoid(g)
            dup = (da * (g * sg)).astype(BF16)
            dg = (da * up * (sg * (1.0 + g * (1.0 - sg)))).astype(BF16)
            dg_ref[s] = dg
            dup_ref[s] = dup
            dh2 = dh2 + _dot_nt(dg, wg_ref[s]) + _dot_nt(dup, wu_ref[s])
        dx, dgpf = _rms_bwd(xh, r, gpf, dh2)
        dgpf_ref[...] += dgpf
        dx1_ref[...] = dout + dx

    return pl.pallas_call(
        body, name="ffn_fwd_bwd", grid=(SEQ // TM,),
        in_specs=[_rows(D_MODEL), _rows(D_MODEL), _resident((N_SHARD, D_MODEL, FF_S)),
                  _resident((N_SHARD, D_MODEL, FF_S)), _resident((N_SHARD, FF_S, D_MODEL)),
                  _resident((1, D_MODEL)), _resident((1, D_MODEL))],
        out_specs=[_rows(D_MODEL), _rows3(N_SHARD, FF_S), _rows3(N_SHARD, FF_S), _rows3(N_SHARD, FF_S),
                   _rows(D_MODEL), _rows(D_MODEL), _acc(D_MODEL), _acc(D_MODEL), _acc(D_MODEL)],
        out_shape=[_sds((SEQ, D_MODEL), BF16), _sds((N_SHARD, SEQ, FF_S), BF16), _sds((N_SHARD, SEQ, FF_S), BF16),
                   _sds((N_SHARD, SEQ, FF_S), BF16), _sds((SEQ, D_MODEL), BF16), _sds((SEQ, D_MODEL), F32),
                   _sds((1, D_MODEL), F32), _sds((1, D_MODEL), F32), _sds((1, D_MODEL), F32)],
        scratch_shapes=[pltpu.VMEM((N_SHARD, TM, FF_S), F32), pltpu.VMEM((N_SHARD, TM, FF_S), F32)],
        compiler_params=_seq_params(),
    )(x1, target, w_gate, w_up, w_down, g_pre, g_post)


def _wgrad(a, b, a_spec, b_spec, out_block, name):
    def body(a_ref, b_ref, o_ref):
        av = a_ref[0] if len(a_ref.shape) == 3 else a_ref[...]
        bv = b_ref[0] if len(b_ref.shape) == 3 else b_ref[...]
        o_ref[0] = _dot_tn(av, bv)

    return pl.pallas_call(
        body, name=name, grid=(N_SHARD,),
        in_specs=[a_spec, b_spec],
        out_specs=pl.BlockSpec((1,) + out_block, lambda s: (s, 0, 0)),
        out_shape=_sds((N_SHARD,) + out_block, F32),
        compiler_params=_seq_params(),
    )(a, b)


def _outproj_bwd(dx1, y, attn, sgu, w_out, g_post, g_attn, g_sgu):
    def body(dx1_ref, y_ref, attn_ref, sgu_ref, w_ref, gp_ref, ga_ref, gs_ref,
             dy_ref, dattn_ref, delta_ref, dsgu_ref, dgp_ref, dga_ref, dgs_ref):
        @pl.when(pl.program_id(0) == 0)
        def _():
            dgp_ref[...] = jnp.zeros_like(dgp_ref)
            dga_ref[...] = jnp.zeros_like(dga_ref)
            dgs_ref[...] = jnp.zeros_like(dgs_ref)

        yh, ry = _rms_stats(y_ref[...])
        dy, dgp = _rms_bwd(yh, ry, gp_ref[...], dx1_ref[...])
        dgp_ref[...] += dgp
        dyb = dy.astype(BF16)
        dy_ref[...] = dyb
        dmixed = jnp.concatenate([_dot_nt(dyb, w_ref[s]) for s in range(N_SHARD)], axis=1)
        attn = attn_ref[...]
        ah, ra = _rms_stats(attn)
        dattn, dga = _rms_bwd(ah, ra, ga_ref[...], dmixed[:, 0:512])
        dga_ref[...] += dga
        sh, rs = _rms_stats(sgu_ref[...])
        dsgu, dgs = _rms_bwd(sh, rs, gs_ref[...], dmixed[:, 512:1024])
        dgs_ref[...] += dgs
        dattn_ref[...] = dattn.astype(BF16)
        dsgu_ref[...] = dsgu
        la = lax.broadcasted_iota(jnp.int32, (ATTN_W, ATTN_W), 0) >> 6
        lb = lax.broadcasted_iota(jnp.int32, (ATTN_W, ATTN_W), 1) >> 6
        delta_ref[...] = _dot_exact(dattn * attn, (la == lb).astype(F32))

    return pl.pallas_call(
        body, name="outproj_bwd", grid=(SEQ // TM,),
        in_specs=[_rows(D_MODEL), _rows(D_MODEL), _rows(512), _rows(512), _resident((N_SHARD, OUT_S, D_MODEL)),
                  _resident((1, D_MODEL)), _resident((1, 512)), _resident((1, 512))],
        out_specs=[_rows(D_MODEL), _rows(512), _rows(512), _rows(512), _acc(D_MODEL), _acc(512), _acc(512)],
        out_shape=[_sds((SEQ, D_MODEL), BF16), _sds((SEQ, 512), BF16), _sds((SEQ, 512), F32), _sds((SEQ, 512), F32),
                   _sds((1, D_MODEL), F32), _sds((1, 512), F32), _sds((1, 512), F32)],
        compiler_params=_seq_params(),
    )(dx1, y, attn, sgu, w_out, g_post, g_attn, g_sgu)


def _sgu_bwd(u, vs, dsgu, lg, lb, w_sp, b_t):
    nsteps = SEQ // TM

    def body(u_ref, vs_ref, ds_ref, lg_ref, lb_ref, w_ref, bt_ref,
             du_ref, dvs_ref, dw_ref, db_ref, dlg_ref, dlb_ref, dbias_scr):
        i = pl.program_id(0)

        @pl.when(i == 0)
        def _():
            dw_ref[...] = jnp.zeros_like(dw_ref)
            dlg_ref[...] = jnp.zeros_like(dlg_ref)
            dlb_ref[...] = jnp.zeros_like(dlb_ref)
            dbias_scr[...] = jnp.zeros_like(dbias_scr)

        wm = _masked_spatial(w_ref)
        ones_g = _group_ones()
        bias_full = _dot_exact(bt_ref[...], ones_g)
        u = u_ref[...]
        vs = vs_ref[...]
        lg = lg_ref[...]
        gu, xh, rstd, vnb, mixed = _sgu_core(u, vs, lg, lb_ref[...], wm, bias_full)
        dsgu = ds_ref[...]
        du_ref[...] = (dsgu * mixed * _gelu_grad(u)).astype(BF16)
        dmixed = dsgu * gu
        left = _left_half()
        dvn_rows = []
        for c in range(TM // CHUNK):
            rs = slice(c * CHUNK, (c + 1) * CHUNK)
            dm_c = dmixed[rs, :]
            dbias_scr[...] += dm_c
            pieces = []
            for p in range(4):
                ls = slice(p * 128, (p + 1) * 128)
                dmp = dm_c[:, ls]
                vp = vnb[rs, ls]
                dmb = dmp.astype(BF16)
                zero = jnp.zeros_like(dmb)
                dw_ref[2 * p] += _dot_nt(jnp.where(left, dmb, zero), vp)
                dw_ref[2 * p + 1] += _dot_nt(jnp.where(left, zero, dmb), vp)
                pieces.append(jnp.where(left, _dot_tn(wm[2 * p], dmb), _dot_tn(wm[2 * p + 1], dmb)))
            dvn_rows.append(jnp.concatenate(pieces, axis=1))
        dvn = jnp.concatenate(dvn_rows, axis=0)
        dlg_ref[...] += jnp.sum(dvn * xh, axis=0, keepdims=True)
        dlb_ref[...] += jnp.sum(dvn, axis=0, keepdims=True)
        dxh = dvn * lg
        dgv = rstd * (dxh - jnp.mean(dxh, axis=-1, keepdims=True) - xh * jnp.mean(dxh * xh, axis=-1, keepdims=True))
        dvs_ref[...] = (dgv * _gelu_grad(vs)).astype(BF16)

        @pl.when(i == nsteps - 1)
        def _():
            row = lax.broadcasted_iota(jnp.int32, (CHUNK, CHUNK), 0)
            col = lax.broadcasted_iota(jnp.int32, (CHUNK, CHUNK), 1)
            for g in range(SGU_GROUPS):
                dw_ref[g] = jnp.where(col <= row, dw_ref[g], 0.0)
            db_ref[...] = lax.dot_general(ones_g, dbias_scr[...], (((1,), (1,)), ((), ())),
                                          preferred_element_type=F32, precision=lax.Precision.HIGHEST)

    return pl.pallas_call(
        body, name="sgu_bwd", grid=(nsteps,),
        in_specs=[_rows(SGU_W), _rows(SGU_W), _rows(SGU_W), _resident((1, SGU_W)), _resident((1, SGU_W)),
                  _resident((SGU_GROUPS, CHUNK, CHUNK)), _resident((CHUNK, SGU_GROUPS))],
        out_specs=[_rows(SGU_W), _rows(SGU_W), pl.BlockSpec((SGU_GROUPS, CHUNK, CHUNK), lambda i: (0, 0, 0)),
                   _acc(CHUNK, SGU_GROUPS), _acc(SGU_W), _acc(SGU_W)],
        out_shape=[_sds((SEQ, SGU_W), BF16), _sds((SEQ, SGU_W), BF16), _sds((SGU_GROUPS, CHUNK, CHUNK), F32),
                   _sds((SGU_GROUPS, CHUNK), F32), _sds((1, SGU_W), F32), _sds((1, SGU_W), F32)],
        scratch_shapes=[pltpu.VMEM((CHUNK, SGU_W), F32)],
        compiler_params=_seq_params(),
    )(u, vs, dsgu, lg, lb, w_sp, b_t)


def _attn_bwd(qv, kv, vv, dov, deltav, lsev, dil):
    seg = SEQ // dil
    nblk = seg // CHUNK

    def body(q_ref, k_ref, v_ref, do_ref, dl_ref, lse_ref, dq_ref, dk_ref, dv_ref):
        left = _left_half()
        m_cur, m_prev = _block_masks()

        def blk(b, carry):
            r0 = pl.multiple_of(b * CHUNK, CHUNK)
            rp = pl.multiple_of(jnp.maximum(b - 1, 0) * CHUNK, CHUNK)
            prev_ok = m_prev & (b > 0)
            for hp in range(4):
                ls = slice(hp * 128, (hp + 1) * 128)
                qp = q_ref[pl.ds(r0, CHUNK), ls]
                kc = k_ref[pl.ds(r0, CHUNK), ls]
                vc = v_ref[pl.ds(r0, CHUNK), ls]
                dop = do_ref[pl.ds(r0, CHUNK), ls]
                lsep = lse_ref[pl.ds(r0, CHUNK), ls]
                dlp = dl_ref[pl.ds(r0, CHUNK), ls]
                if nblk > 1:
                    kp = k_ref[pl.ds(rp, CHUNK), ls]
                    vp = v_ref[pl.ds(rp, CHUNK), ls]
                dqs = []
                dkc = jnp.zeros((CHUNK, 128), F32)
                dvc = jnp.zeros((CHUNK, 128), F32)
                dkp = jnp.zeros((CHUNK, 128), F32)
                dvp = jnp.zeros((CHUNK, 128), F32)
                for side, hm in enumerate((left, ~left)):
                    c0 = side * HEAD_DIM
                    qh = jnp.where(hm, qp, jnp.zeros_like(qp))
                    doh = jnp.where(hm, dop, jnp.zeros_like(dop))
                    lse_h = lsep[:, c0:c0 + 1]
                    dl_h = dlp[:, c0:c0 + 1]
                    sc = _dot_nt(qh, kc) * ATTN_SCALE
                    pc = jnp.exp(jnp.where(m_cur, sc - lse_h, NEG))
                    dsc = (pc * (_dot_nt(doh, vc) - dl_h) * ATTN_SCALE).astype(BF16)
                    dq = _dot(dsc, kc)
                    dkc = dkc + _dot_tn(dsc, qh)
                    dvc = dvc + _dot_tn(pc.astype(BF16), doh)
                    if nblk > 1:
                        sp = _dot_nt(qh, kp) * ATTN_SCALE
                        pp = jnp.exp(jnp.where(prev_ok, sp - lse_h, NEG))
                        dsp = (pp * (_dot_nt(doh, vp) - dl_h) * ATTN_SCALE).astype(BF16)
                        dq = dq + _dot(dsp, kp)
                        dkp = dkp + _dot_tn(dsp, qh)
                        dvp = dvp + _dot_tn(pp.astype(BF16), doh)
                    dqs.append(dq)
                dq_ref[pl.ds(r0, CHUNK), ls] = jnp.where(left, dqs[0], dqs[1])
                dk_ref[pl.ds(r0, CHUNK), ls] = dkc
                dv_ref[pl.ds(r0, CHUNK), ls] = dvc
                if nblk > 1:
                    @pl.when(b > 0)
                    def _():
                        dk_ref[pl.ds(rp, CHUNK), ls] += dkp
                        dv_ref[pl.ds(rp, CHUNK), ls] += dvp
            return carry

        lax.fori_loop(0, nblk, blk, 0)

    spec = pl.BlockSpec((seg, ATTN_W), lambda r: (0, r))
    return pl.pallas_call(
        body, name=f"attn_bwd_d{dil}", grid=(dil,),
        in_specs=[spec] * 6, out_specs=[spec] * 3,
        out_shape=[_sds((seg, dil * ATTN_W), F32)] * 3,
        compiler_params=_seq_params(),
    )(qv, kv, vv, dov, deltav, lsev)


def _inproj_bwd(dqs, dks, dvs, du, dvs_sgu, pos, x, dx1, w_in, g_pre):
    def body(dq1, dq2, dq3, dk1, dk2, dk3, dv1, dv2, dv3, du_ref, dvs_ref, pos_ref, x_ref, dx1_ref, w_ref, g_ref,
             dproj_ref, gx_ref, dg_ref):
        @pl.when(pl.program_id(0) == 0)
        def _():
            dg_ref[...] = jnp.zeros_like(dg_ref)

        tabs = _rot_tables(pos_ref[...])
        dproj_ref[:, 0:512] = _rope_bwd(dq1[...] + dq2[...] + dq3[...], tabs).astype(BF16)
        dproj_ref[:, 512:1024] = _rope_bwd(dk1[...] + dk2[...] + dk3[...], tabs).astype(BF16)
        dproj_ref[:, 1024:1536] = (dv1[...] + dv2[...] + dv3[...]).astype(BF16)
        dproj_ref[:, 1536:2048] = du_ref[...]
        dproj_ref[:, 2048:2560] = dvs_ref[...]
        dh = jnp.zeros((TM, D_MODEL), F32)
        for s in range(N_SHARD):
            dh = dh + _dot_nt(dproj_ref[:, s * IN_S:(s + 1) * IN_S], w_ref[s])
        g = g_ref[...]
        xh, r = _rms_stats(x_ref[...])
        dx, dg = _rms_bwd(xh, r, g, dh)
        dg_ref[...] += dg
        gx_ref[...] = dx1_ref[...] + dx

    return pl.pallas_call(
        body, name="inproj_bwd", grid=(SEQ // TM,),
        in_specs=[_rows(512)] * 11 + [_rows(1), _rows(D_MODEL), _rows(D_MODEL),
                                     _resident((N_SHARD, D_MODEL, IN_S)), _resident((1, D_MODEL))],
        out_specs=[_rows(PROJ_W), _rows(D_MODEL), _acc(D_MODEL)],
        out_shape=[_sds((SEQ, PROJ_W), BF16), _sds((SEQ, D_MODEL), F32), _sds((1, D_MODEL), F32)],
        compiler_params=_seq_params(),
    )(*dqs, *dks, *dvs, du, dvs_sgu, pos, x, dx1, w_in, g_pre)


def _to_view(a, dil):
    return a if dil == 1 else a.reshape(SEQ // dil, dil * a.shape[1])


def _from_view(a, dil):
    return a if dil == 1 else a.reshape(SEQ, a.shape[1] // dil)


def _local_step(x, pos, target, w_in, w_out, w_gate, w_up, w_down, small):
    b_t = small["sgu_b_spatial"].T
    h, q, k, v, u, vs = _inproj_fwd(x, pos, small["pre_mix_norm"], w_in)
    sgu = _sgu_fwd(u, vs, small["sgu_ln_gain"], small["sgu_ln_bias"], small["sgu_w_spatial"], b_t)
    views = [tuple(_to_view(t, dil) for t in (q, k, v)) for dil in DILATIONS]
    o_list, l_list = [], []
    for dil, (qv, kv, vv) in zip(DILATIONS, views):
        o, l = _attn_fwd(qv, kv, vv, dil)
        o_list.append(_from_view(o, dil))
        l_list.append(_from_view(l, dil))
    attn, lse, mixed, y, x1 = _mix_out_fwd(o_list, l_list, sgu, x, w_out, small["attn_out_norm"],
                                           small["sgu_out_norm"], small["post_mix_norm"])
    h2, a, dg, dup, df, dx1, loss_cols, d_pre_ffn, d_post_ffn = _ffn_fwd_bwd(
        x1, target, w_gate, w_up, w_down, small["pre_ffn_norm"], small["post_ffn_norm"])

    full_tok = pl.BlockSpec((SEQ, D_MODEL), lambda s: (0, 0), pipeline_mode=pl.Buffered(1))
    ff_tok = pl.BlockSpec((1, SEQ, FF_S), lambda s: (s, 0, 0))
    gw_gate = _wgrad(h2, dg, full_tok, ff_tok, (D_MODEL, FF_S), "wgrad_gate")
    gw_up = _wgrad(h2, dup, full_tok, ff_tok, (D_MODEL, FF_S), "wgrad_up")
    gw_down = _wgrad(a, df, ff_tok, full_tok, (FF_S, D_MODEL), "wgrad_down")

    dy, dattn, delta, dsgu, d_post_mix, d_attn_norm, d_sgu_norm = _outproj_bwd(
        dx1, y, attn, sgu, w_out, small["post_mix_norm"], small["attn_out_norm"], small["sgu_out_norm"])
    gw_out = _wgrad(mixed, dy, pl.BlockSpec((SEQ, OUT_S), lambda s: (0, s)), full_tok, (OUT_S, D_MODEL), "wgrad_out")
    du, dvs_sgu, d_w_sp, d_b_sp, d_ln_gain, d_ln_bias = _sgu_bwd(
        u, vs, dsgu, small["sgu_ln_gain"], small["sgu_ln_bias"], small["sgu_w_spatial"], b_t)

    dqs, dks, dvs = [], [], []
    for dil, (qv, kv, vv) in zip(DILATIONS, views):
        dq, dk, dv = _attn_bwd(qv, kv, vv, _to_view(dattn, dil), _to_view(delta, dil), _to_view(lse, dil), dil)
        dqs.append(_from_view(dq, dil))
        dks.append(_from_view(dk, dil))
        dvs.append(_from_view(dv, dil))
    dproj, grad_x, d_pre_mix = _inproj_bwd(dqs, dks, dvs, du, dvs_sgu, pos, x, dx1, w_in, small["pre_mix_norm"])
    gw_in = _wgrad(h, dproj, full_tok, pl.BlockSpec((SEQ, IN_S), lambda s: (0, s)), (D_MODEL, IN_S), "wgrad_in")

    small_grads = {
        "pre_mix_norm": d_pre_mix, "sgu_ln_gain": d_ln_gain, "sgu_ln_bias": d_ln_bias, "sgu_w_spatial": d_w_sp,
        "sgu_b_spatial": d_b_sp, "attn_out_norm": d_attn_norm, "sgu_out_norm": d_sgu_norm,
        "post_mix_norm": d_post_mix, "pre_ffn_norm": d_pre_ffn, "post_ffn_norm": d_post_ffn,
    }
    return loss_cols, grad_x, (gw_in, gw_out, gw_gate, gw_up, gw_down), small_grads


def _coords():
    return lax.axis_index("x"), lax.axis_index("y"), lax.axis_index("c")


def _other_chips(x, y):
    return [(1 - x, y), (x, 1 - y), (1 - x, 1 - y)]


def _comm_call(body, name, n_in, out_shape, scratch_shapes):
    return pl.pallas_call(
        body, name=name, in_specs=[ANY] * n_in, out_specs=[ANY] * len(out_shape), out_shape=out_shape,
        scratch_shapes=scratch_shapes,
        compiler_params=pltpu.CompilerParams(has_side_effects=True),
    )


def _gather_weights(shards):
    n = len(shards)
    halves = [s.reshape(2, s.shape[0] // 2, s.shape[1]) for s in shards]

    def body(*refs):
        ins, outs = refs[:n], refs[n:2 * n]
        send_sems, recv_sems, local_sems = refs[2 * n:]
        x, y, c = _coords()
        s_me = 2 * x + y
        chips = _other_chips(x, y)
        sibling = (x, y, 1 - c)

        def copy(k, w, shard, cc, to):
            src = ins[w].at[cc] if shard is None else outs[w].at[shard, cc]
            dst = outs[w].at[s_me if shard is None else shard, cc]
            return pltpu.make_async_remote_copy(src_ref=src, dst_ref=dst, send_sem=send_sems.at[k],
                                                recv_sem=recv_sems.at[k], device_id=to, device_id_type=MESH)

        local = [pltpu.make_async_copy(ins[w], outs[w].at[s_me], local_sems.at[w]) for w in range(n)]
        for cp in local:
            cp.start()
        first = [copy(j * n + w, w, None, c, (cx, cy, c)) for j, (cx, cy) in enumerate(chips) for w in range(n)]
        for cp in first:
            cp.start()
        passed = []
        for j, (cx, cy) in enumerate(chips):
            for w in range(n):
                copy(j * n + w, w, 2 * cx + cy, c, (x, y, c)).wait_recv()
                fw = copy((3 + j) * n + w, w, 2 * cx + cy, c, sibling)
                fw.start()
                passed.append(fw)
        for j, (cx, cy) in enumerate(chips):
            for w in range(n):
                copy((3 + j) * n + w, w, 2 * cx + cy, 1 - c, (x, y, c)).wait_recv()
        for cp in first + passed:
            cp.wait_send()
        for cp in local:
            cp.wait()

    out_shape = [_sds((N_SHARD,) + h.shape, h.dtype) for h in halves]
    scratch = [pltpu.SemaphoreType.DMA((6 * n,)), pltpu.SemaphoreType.DMA((6 * n,)), pltpu.SemaphoreType.DMA((n,))]
    full = _comm_call(body, "comm_gather_weights", n, out_shape, scratch)(*halves)
    return [f.reshape((N_SHARD,) + s.shape) for f, s in zip(full, shards)]


def _rs_to_sibling(gws):
    n = len(gws)

    def body(*refs):
        ins, outs = refs[:n], refs[n:2 * n]
        send_sems, recv_sems = refs[2 * n:]
        x, y, c = _coords()
        copies = []
        for w in range(n):
            hw = gws[w].shape[1] // 2
            copies.append(pltpu.make_async_remote_copy(
                src_ref=ins[w].at[:, pl.ds((1 - c) * hw, hw), :], dst_ref=outs[w], send_sem=send_sems.at[w],
                recv_sem=recv_sems.at[w], device_id=(x, y, 1 - c), device_id_type=MESH))
        for cp in copies:
            cp.start()
        for cp in copies:
            cp.wait()

    out_shape = [_sds((N_SHARD, g.shape[1] // 2, g.shape[2]), g.dtype) for g in gws]
    scratch = [pltpu.SemaphoreType.DMA((n,)), pltpu.SemaphoreType.DMA((n,))]
    return _comm_call(body, "comm_rs_sibling", n, out_shape, scratch)(*gws)


def _rs_chip_sum(gw, recv, core):
    _, rows, cols = gw.shape
    hw = rows // 2

    def body(c_ref, g_ref, r_ref, o_ref):
        o_ref[...] = (g_ref[...] + r_ref[...]).astype(BF16)

    return pl.pallas_call(
        body, name="rs_chip_sum",
        grid_spec=pltpu.PrefetchScalarGridSpec(
            num_scalar_prefetch=1, grid=(N_SHARD,),
            in_specs=[pl.BlockSpec((1, hw, cols), lambda s, c_ref: (s, c_ref[0], 0)),
                      pl.BlockSpec((1, hw, cols), lambda s, c_ref: (s, 0, 0))],
            out_specs=pl.BlockSpec((1, hw, cols), lambda s, c_ref: (s, 0, 0))),
        out_shape=_sds((N_SHARD, hw, cols), BF16),
        compiler_params=_seq_params(),
    )(core, gw, recv)


def _rs_between_chips(pbs):
    n = len(pbs)

    def body(*refs):
        ins, outs = refs[:n], refs[n:2 * n]
        send_sems, recv_sems = refs[2 * n:]
        x, y, c = _coords()
        copies = []
        for j, (cx, cy) in enumerate(_other_chips(x, y)):
            for w in range(n):
                copies.append(pltpu.make_async_remote_copy(
                    src_ref=ins[w].at[2 * cx + cy], dst_ref=outs[w].at[j], send_sem=send_sems.at[j * n + w],
                    recv_sem=recv_sems.at[j * n + w], device_id=(cx, cy, c), device_id_type=MESH))
        for cp in copies:
            cp.start()
        for cp in copies:
            cp.wait()

    out_shape = [_sds((3,) + p.shape[1:], p.dtype) for p in pbs]
    scratch = [pltpu.SemaphoreType.DMA((3 * n,)), pltpu.SemaphoreType.DMA((3 * n,))]
    return _comm_call(body, "comm_rs_chips", n, out_shape, scratch)(*pbs)


def _rs_final_sum(gw, recv_sib, recv_chips, shard_core):
    _, rows, cols = gw.shape
    hw = rows // 2

    def body(sc_ref, g_ref, r_ref, rc_ref, o_ref):
        acc = g_ref[0] + r_ref[0]
        for j in range(3):
            acc = acc + rc_ref[j].astype(F32)
        o_ref[...] = acc

    return pl.pallas_call(
        body, name="rs_final_sum",
        grid_spec=pltpu.PrefetchScalarGridSpec(
            num_scalar_prefetch=1, grid=(1,),
            in_specs=[pl.BlockSpec((1, hw, cols), lambda i, sc: (sc[0], sc[1], 0)),
                      pl.BlockSpec((1, hw, cols), lambda i, sc: (sc[0], 0, 0)),
                      pl.BlockSpec((3, hw, cols), lambda i, sc: (0, 0, 0))],
            out_specs=pl.BlockSpec((hw, cols), lambda i, sc: (0, 0))),
        out_shape=_sds((hw, cols), F32),
        compiler_params=_seq_params(),
    )(shard_core, gw, recv_sib, recv_chips)


def _rs_join_halves(halves):
    n = len(halves)

    def body(*refs):
        ins, outs = refs[:n], refs[n:2 * n]
        send_sems, recv_sems, local_sems = refs[2 * n:]
        x, y, c = _coords()
        local, remote = [], []
        for w in range(n):
            local.append(pltpu.make_async_copy(ins[w], outs[w].at[c], local_sems.at[w]))
            remote.append(pltpu.make_async_remote_copy(
                src_ref=ins[w], dst_ref=outs[w].at[c], send_sem=send_sems.at[w],
                recv_sem=recv_sems.at[w], device_id=(x, y, 1 - c), device_id_type=MESH))
        for cp in local + remote:
            cp.start()
        for w in range(n):
            remote[w].wait_send()
            pltpu.make_async_remote_copy(
                src_ref=ins[w], dst_ref=outs[w].at[1 - c], send_sem=send_sems.at[w],
                recv_sem=recv_sems.at[w], device_id=(x, y, c), device_id_type=MESH).wait_recv()
            local[w].wait()

    out_shape = [_sds((2,) + h.shape, h.dtype) for h in halves]
    scratch = [pltpu.SemaphoreType.DMA((n,)), pltpu.SemaphoreType.DMA((n,)), pltpu.SemaphoreType.DMA((n,))]
    joined = _comm_call(body, "comm_rs_join", n, out_shape, scratch)(*halves)
    return [j.reshape(2 * h.shape[0], h.shape[1]) for j, h in zip(joined, halves)]


def _allreduce_small(buf):
    rows, cols = buf.shape

    def body(in_ref, out_ref, slots, send_sems, recv_sems):
        x, y, c = _coords()
        me = 4 * x + 2 * y + c
        copies, peers = [], []
        for k in range(1, 8):
            px = 1 - x if (k >> 2) & 1 else x
            py = 1 - y if (k >> 1) & 1 else y
            pc = 1 - c if k & 1 else c
            peers.append(4 * px + 2 * py + pc)
            copies.append(pltpu.make_async_remote_copy(
                src_ref=in_ref, dst_ref=slots.at[me], send_sem=send_sems.at[k - 1], recv_sem=recv_sems.at[k - 1],
                device_id=(px, py, pc), device_id_type=MESH))
        for cp in copies:
            cp.start()
        slots[me] = in_ref[...]
        for k in range(7):
            pltpu.make_async_remote_copy(
                src_ref=in_ref, dst_ref=slots.at[peers[k]], send_sem=send_sems.at[k], recv_sem=recv_sems.at[k],
                device_id=(x, y, c), device_id_type=MESH).wait_recv()
        for cp in copies:
            cp.wait_send()
        acc = slots[0]
        for i in range(1, 8):
            acc = acc + slots[i]
        out_ref[...] = acc

    vmem = pl.BlockSpec(memory_space=pltpu.VMEM)
    return pl.pallas_call(
        body, name="comm_allreduce_small", in_specs=[vmem], out_specs=vmem, out_shape=_sds((rows, cols), F32),
        scratch_shapes=[pltpu.VMEM((8, rows, cols), F32), pltpu.SemaphoreType.DMA((7,)), pltpu.SemaphoreType.DMA((7,))],
        compiler_params=pltpu.CompilerParams(has_side_effects=True, vmem_limit_bytes=VMEM_LIMIT),
    )(buf)


def _adamw(w, g, m, v, block_rows, name):
    rows, cols = w.shape

    def body(w_ref, g_ref, m_ref, v_ref, d_ref, nm_ref, nv_ref):
        g = g_ref[...]
        m = ADAM_B1 * m_ref[...] + (1.0 - ADAM_B1) * g
        v = ADAM_B2 * v_ref[...] + (1.0 - ADAM_B2) * (g * g)
        m_hat = m / (1.0 - ADAM_B1 ** ADAM_STEP)
        v_hat = v / (1.0 - ADAM_B2 ** ADAM_STEP)
        d_ref[...] = -ADAM_LR * (m_hat / (jnp.sqrt(v_hat) + ADAM_EPS) + ADAM_WD * w_ref[...])
        nm_ref[...] = m
        nv_ref[...] = v

    spec = pl.BlockSpec((block_rows, cols), lambda i: (i, 0))
    return pl.pallas_call(
        body, name=name, grid=(rows // block_rows,), in_specs=[spec] * 4, out_specs=[spec] * 3,
        out_shape=[_sds((rows, cols), F32)] * 3,
        compiler_params=_seq_params(),
    )(w, g, m, v)


WEIGHTS = ("pre_mix_norm", "w_in", "sgu_ln_gain", "sgu_ln_bias", "sgu_w_spatial", "sgu_b_spatial", "attn_out_norm",
           "sgu_out_norm", "w_out", "post_mix_norm", "pre_ffn_norm", "w_gate", "w_up", "w_down", "post_ffn_norm")
BIG = ("w_in", "w_out", "w_gate", "w_up", "w_down")
BIG_ADAM_ROWS = {"w_in": 256, "w_out": 128, "w_gate": 256, "w_up": 256, "w_down": 352}
SMALL = ("pre_mix_norm", "post_mix_norm", "pre_ffn_norm", "post_ffn_norm", "sgu_ln_gain", "sgu_ln_bias",
         "attn_out_norm", "sgu_out_norm", "sgu_w_spatial", "sgu_b_spatial")


def _pack_small(d):
    flat = [d[n].reshape(-1) for n in SMALL]
    used = sum(f.shape[0] for f in flat)
    flat.append(jnp.zeros((SMALL_ROWS * 1024 - used,), F32))
    return jnp.concatenate(flat).reshape(SMALL_ROWS, 1024)


def _unpack_small(buf, shapes):
    flat = buf.reshape(-1)
    out, off = {}, 0
    for n in SMALL:
        size = int(np.prod(shapes[n]))
        out[n] = flat[off:off + size].reshape(shapes[n])
        off += size
    return out


def kernel(x, positions, pre_mix_norm, w_in, sgu_ln_gain, sgu_ln_bias, sgu_w_spatial, sgu_b_spatial, attn_out_norm, sgu_out_norm, w_out, post_mix_norm, pre_ffn_norm, w_gate, w_up, w_down, post_ffn_norm, loss_target, m_pre_mix_norm, m_w_in, m_sgu_ln_gain, m_sgu_ln_bias, m_sgu_w_spatial, m_sgu_b_spatial, m_attn_out_norm, m_sgu_out_norm, m_w_out, m_post_mix_norm, m_pre_ffn_norm, m_w_gate, m_w_up, m_w_down, m_post_ffn_norm, v_pre_mix_norm, v_w_in, v_sgu_ln_gain, v_sgu_ln_bias, v_sgu_w_spatial, v_sgu_b_spatial, v_attn_out_norm, v_sgu_out_norm, v_w_out, v_post_mix_norm, v_pre_ffn_norm, v_w_gate, v_w_up, v_w_down, v_post_ffn_norm):
    a = dict(locals())
    cx, cy, cc = _coords()
    core = jnp.stack([cc]).astype(jnp.int32)
    shard_core = jnp.stack([2 * cx + cy, cc]).astype(jnp.int32)

    full = _gather_weights([a[n][0].astype(BF16) for n in BIG])
    small = {n: (a[n][0] if a[n].ndim > 2 else a[n]) for n in SMALL}
    loss_cols, grad_x, gws, small_grads = _local_step(
        x[0], positions.reshape(SEQ, 1), loss_target[0], *full, small)
    loss = lax.psum(jnp.sum(loss_cols) * np.float32(0.5 / D_MODEL), ("x", "y", "c"))

    recv_sib = _rs_to_sibling(list(gws))
    chip_part = [_rs_chip_sum(g, r, core) for g, r in zip(gws, recv_sib)]
    recv_chips = _rs_between_chips(chip_part)
    halves = [_rs_final_sum(g, r, rc, shard_core) for g, r, rc in zip(gws, recv_sib, recv_chips)]
    big_grads = dict(zip(BIG, _rs_join_halves(halves)))

    shapes = {n: a[n].shape for n in SMALL}
    small_sum = _allreduce_small(_pack_small(small_grads))

    grads, deltas, new_m, new_v = {}, {}, {}, {}
    for n in BIG:
        grads[n] = big_grads[n][None]
        d, nm, nv = _adamw(a[n][0], big_grads[n], a["m_" + n][0], a["v_" + n][0], BIG_ADAM_ROWS[n], "adamw_" + n)
        deltas[n], new_m[n], new_v[n] = d[None], nm[None], nv[None]
    d, nm, nv = _adamw(_pack_small({n: a[n] for n in SMALL}), small_sum, _pack_small({n: a["m_" + n] for n in SMALL}),
                       _pack_small({n: a["v_" + n] for n in SMALL}), SMALL_ROWS, "adamw_small")
    grads.update(_unpack_small(small_sum, shapes))
    deltas.update(_unpack_small(d, shapes))
    new_m.update(_unpack_small(nm, shapes))
    new_v.update(_unpack_small(nv, shapes))
    return (loss, grad_x[None], *[grads[n] for n in WEIGHTS], *[deltas[n] for n in WEIGHTS],
            *[new_m[n] for n in WEIGHTS], *[new_v[n] for n in WEIGHTS])
```

```python
import numpy as np
import jax
import jax.numpy as jnp
from jax import lax
from jax.experimental import pallas as pl
from jax.experimental.pallas import tpu as pltpu

F32 = jnp.float32
BF16 = jnp.bfloat16

SEQ = 2048
D_MODEL = 1024
HEAD_DIM = 64
ATTN_W = 512
SGU_W = 512
SGU_GROUPS = 8
CHUNK = 128
DILATIONS = (1, 4, 16)
N_SHARD = 4
IN_S = 640
OUT_S = 256
FF_S = 704
PROJ_W = N_SHARD * IN_S
RMS_EPS = 1e-6
LN_EPS = 1e-5
ROPE_THETA = 500000.0
ATTN_SCALE = 1.0 / np.sqrt(HEAD_DIM)
NEG = -1e30
TM = 256
VMEM_LIMIT = 56 * 1024 * 1024
SMALL_ROWS = 136

ADAM_LR = 0.001
ADAM_B1 = 0.9
ADAM_B2 = 0.999
ADAM_EPS = 1e-08
ADAM_WD = 0.01
ADAM_STEP = 10

MESH = pl.DeviceIdType.MESH
ANY = pl.BlockSpec(memory_space=pl.ANY)


def _dot(a, b):
    return jnp.dot(a, b, preferred_element_type=F32)


def _dot_nt(a, b):
    return lax.dot_general(a, b, (((1,), (1,)), ((), ())), preferred_element_type=F32)


def _dot_tn(a, b):
    return lax.dot_general(a, b, (((0,), (0,)), ((), ())), preferred_element_type=F32)


def _dot_exact(a, b):
    return jnp.dot(a, b, preferred_element_type=F32, precision=lax.Precision.HIGHEST)


def _rms_stats(x):
    r = lax.rsqrt(jnp.mean(x * x, axis=-1, keepdims=True) + RMS_EPS)
    return x * r, r


def _rms_bwd(xh, r, gain, dy):
    dxh = dy * gain
    dx = r * (dxh - xh * jnp.mean(dxh * xh, axis=-1, keepdims=True))
    return dx, jnp.sum(dy * xh, axis=0, keepdims=True)


_ERF_ALPHA = (-2.72614225801306e-10, 2.77068142495902e-08, -2.10102402082508e-06, -5.69250639462346e-05,
              -7.34990630326855e-04, -2.95459980854025e-03, -1.60960333262415e-02)
_ERF_BETA = (-1.45660718464996e-05, -2.13374055278905e-04, -1.68282697438203e-03, -7.37332916720468e-03,
             -1.42647390514189e-02)


def _erf(x):
    x = jnp.clip(x, -4.0, 4.0)
    x2 = x * x
    p = jnp.full_like(x, _ERF_ALPHA[0])
    for a in _ERF_ALPHA[1:]:
        p = p * x2 + a
    q = jnp.full_like(x, _ERF_BETA[0])
    for b in _ERF_BETA[1:]:
        q = q * x2 + b
    return x * p / q


def _gelu(x):
    return 0.5 * x * (1.0 + _erf(x * np.float32(1.0 / np.sqrt(2.0))))


def _gelu_grad(x):
    cdf = 0.5 * (1.0 + _erf(x * np.float32(1.0 / np.sqrt(2.0))))
    pdf = jnp.exp(-0.5 * x * x) * np.float32(1.0 / np.sqrt(2.0 * np.pi))
    return cdf + x * pdf


def _sigmoid(x):
    return 1.0 / (1.0 + jnp.exp(-x))


_INV_FREQ = tuple(float(np.float32(ROPE_THETA ** (-2.0 * j / 16.0))) for j in range(8))


def _rot_tables(pos):
    lane = lax.broadcasted_iota(jnp.int32, (1, 128), 1)
    d = lane & 63
    j = d & 7
    inv = jnp.zeros((1, 128), F32)
    for jj in range(8):
        inv = jnp.where(j == jj, _INV_FREQ[jj], inv)
    ang = pos.astype(F32) * inv
    c = jnp.cos(ang)
    s = jnp.sin(ang)
    cos_t = jnp.where(d < 16, c, 1.0)
    sin_a = jnp.where(d < 8, -s, 0.0)
    sin_b = jnp.where((d >= 8) & (d < 16), s, 0.0)
    return tuple(jnp.tile(t, (1, 4)) for t in (cos_t, sin_a, sin_b))


def _rope(x, tabs):
    cos_t, sin_a, sin_b = tabs
    return x * cos_t + pltpu.roll(x, 504, 1) * sin_a + pltpu.roll(x, 8, 1) * sin_b


def _rope_bwd(dy, tabs):
    cos_t, sin_a, sin_b = tabs
    return dy * cos_t + pltpu.roll(dy * sin_a, 8, 1) + pltpu.roll(dy * sin_b, 504, 1)


def _left_half():
    return lax.broadcasted_iota(jnp.int32, (CHUNK, CHUNK), 1) < HEAD_DIM


def _group_ones():
    lane = lax.broadcasted_iota(jnp.int32, (SGU_GROUPS, SGU_W), 1)
    row = lax.broadcasted_iota(jnp.int32, (SGU_GROUPS, SGU_W), 0)
    return ((lane >> 6) == row).astype(F32)


def _masked_spatial(w_ref):
    row = lax.broadcasted_iota(jnp.int32, (CHUNK, CHUNK), 0)
    col = lax.broadcasted_iota(jnp.int32, (CHUNK, CHUNK), 1)
    return [jnp.where(col <= row, w_ref[g], 0.0).astype(BF16) for g in range(SGU_GROUPS)]


def _sgu_core(u, vs, lg, lb, wm, bias_full):
    tm = u.shape[0]
    gu = _gelu(u)
    gv = _gelu(vs)
    mu = jnp.mean(gv, axis=-1, keepdims=True)
    xc = gv - mu
    rstd = lax.rsqrt(jnp.mean(xc * xc, axis=-1, keepdims=True) + LN_EPS)
    xh = xc * rstd
    vnb = (xh * lg + lb).astype(BF16)
    left = _left_half()
    rows = []
    for c in range(tm // CHUNK):
        pieces = []
        for p in range(4):
            vp = vnb[c * CHUNK:(c + 1) * CHUNK, p * 128:(p + 1) * 128]
            pieces.append(jnp.where(left, _dot(wm[2 * p], vp), _dot(wm[2 * p + 1], vp)))
        rows.append(jnp.concatenate(pieces, axis=1) + bias_full)
    mixed = jnp.concatenate(rows, axis=0)
    return gu, xh, rstd, vnb, mixed


def _resident(shape):
    n = len(shape)
    return pl.BlockSpec(shape, lambda *_: (0,) * n, pipeline_mode=pl.Buffered(1))


def _rows(ncol, tm=TM):
    return pl.BlockSpec((tm, ncol), lambda i: (i, 0))


def _rows3(nlead, ncol, tm=TM):
    return pl.BlockSpec((nlead, tm, ncol), lambda i: (0, i, 0))


def _acc(ncol, nrow=1):
    return pl.BlockSpec((nrow, ncol), lambda i: (0, 0))


def _seq_params():
    return pltpu.CompilerParams(dimension_semantics=("arbitrary",), vmem_limit_bytes=VMEM_LIMIT)


def _sds(shape, dtype):
    return jax.ShapeDtypeStruct(shape, dtype)


def _inproj_fwd(x, pos, g_pre, w_in):
    def body(x_ref, pos_ref, g_ref, w_ref, h_ref, q_ref, k_ref, v_ref, u_ref, vs_ref):
        xh, _ = _rms_stats(x_ref[...])
        h = (xh * g_ref[...]).astype(BF16)
        h_ref[...] = h
        proj = jnp.concatenate([_dot(h, w_ref[s]) for s in range(N_SHARD)], axis=1)
        tabs = _rot_tables(pos_ref[...])
        q_ref[...] = _rope(proj[:, 0:512], tabs).astype(BF16)
        k_ref[...] = _rope(proj[:, 512:1024], tabs).astype(BF16)
        v_ref[...] = proj[:, 1024:1536].astype(BF16)
        u_ref[...] = proj[:, 1536:2048]
        vs_ref[...] = proj[:, 2048:2560]

    return pl.pallas_call(
        body, name="inproj_fwd", grid=(SEQ // TM,),
        in_specs=[_rows(D_MODEL), _rows(1), _resident((1, D_MODEL)), _resident((N_SHARD, D_MODEL, IN_S))],
        out_specs=[_rows(D_MODEL), _rows(512), _rows(512), _rows(512), _rows(512), _rows(512)],
        out_shape=[_sds((SEQ, D_MODEL), BF16), _sds((SEQ, 512), BF16), _sds((SEQ, 512), BF16),
                   _sds((SEQ, 512), BF16), _sds((SEQ, 512), F32), _sds((SEQ, 512), F32)],
        compiler_params=_seq_params(),
    )(x, pos, g_pre, w_in)


def _sgu_fwd(u, vs, lg, lb, w_sp, b_t):
    def body(u_ref, vs_ref, lg_ref, lb_ref, w_ref, bt_ref, out_ref):
        wm = _masked_spatial(w_ref)
        bias_full = _dot_exact(bt_ref[...], _group_ones())
        gu, _, _, _, mixed = _sgu_core(u_ref[...], vs_ref[...], lg_ref[...], lb_ref[...], wm, bias_full)
        out_ref[...] = gu * mixed

    return pl.pallas_call(
        body, name="sgu_fwd", grid=(SEQ // TM,),
        in_specs=[_rows(SGU_W), _rows(SGU_W), _resident((1, SGU_W)), _resident((1, SGU_W)),
                  _resident((SGU_GROUPS, CHUNK, CHUNK)), _resident((CHUNK, SGU_GROUPS))],
        out_specs=_rows(SGU_W),
        out_shape=_sds((SEQ, SGU_W), F32),
        compiler_params=_seq_params(),
    )(u, vs, lg, lb, w_sp, b_t)


def _block_masks():
    row = lax.broadcasted_iota(jnp.int32, (CHUNK, CHUNK), 0)
    col = lax.broadcasted_iota(jnp.int32, (CHUNK, CHUNK), 1)
    return col <= row, col >= row


def _attn_fwd(qv, kv, vv, dil):
    seg = SEQ // dil
    nblk = seg // CHUNK

    def body(q_ref, k_ref, v_ref, o_ref, l_ref):
        left = _left_half()
        m_cur, m_prev = _block_masks()

        def blk(b, carry):
            r0 = pl.multiple_of(b * CHUNK, CHUNK)
            rp = pl.multiple_of(jnp.maximum(b - 1, 0) * CHUNK, CHUNK)
            prev_ok = m_prev & (b > 0)
            for hp in range(4):
                ls = slice(hp * 128, (hp + 1) * 128)
                qp = q_ref[pl.ds(r0, CHUNK), ls]
                kc = k_ref[pl.ds(r0, CHUNK), ls]
                vc = v_ref[pl.ds(r0, CHUNK), ls]
                if nblk > 1:
                    kp = k_ref[pl.ds(rp, CHUNK), ls]
                    vp = v_ref[pl.ds(rp, CHUNK), ls]
                outs, lses = [], []
                for hm in (left, ~left):
                    qh = jnp.where(hm, qp, jnp.zeros_like(qp))
                    sc = jnp.where(m_cur, _dot_nt(qh, kc) * ATTN_SCALE, NEG)
                    m = jnp.max(sc, axis=-1, keepdims=True)
                    if nblk > 1:
                        sp = jnp.where(prev_ok, _dot_nt(qh, kp) * ATTN_SCALE, NEG)
                        m = jnp.maximum(m, jnp.max(sp, axis=-1, keepdims=True))
                    pc = jnp.exp(sc - m)
                    den = jnp.sum(pc, axis=-1, keepdims=True)
                    acc = _dot(pc.astype(BF16), vc)
                    if nblk > 1:
                        pp = jnp.exp(sp - m)
                        den = den + jnp.sum(pp, axis=-1, keepdims=True)
                        acc = acc + _dot(pp.astype(BF16), vp)
                    outs.append(acc / den)
                    lses.append(m + jnp.log(den))
                o_ref[pl.ds(r0, CHUNK), ls] = jnp.where(left, outs[0], outs[1])
                l_ref[pl.ds(r0, CHUNK), ls] = jnp.where(left, lses[0], lses[1])
            return carry

        lax.fori_loop(0, nblk, blk, 0)

    spec = pl.BlockSpec((seg, ATTN_W), lambda r: (0, r))
    return pl.pallas_call(
        body, name=f"attn_fwd_d{dil}", grid=(dil,),
        in_specs=[spec, spec, spec], out_specs=[spec, spec],
        out_shape=[_sds((seg, dil * ATTN_W), F32), _sds((seg, dil * ATTN_W), F32)],
        compiler_params=_seq_params(),
    )(qv, kv, vv)


def _mix_out_fwd(o_list, l_list, sgu, x, w_out, g_attn, g_sgu, g_post):
    def body(o1, o2, o3, l1, l2, l3, sgu_ref, x_ref, w_ref, ga_ref, gs_ref, gp_ref,
             attn_ref, lse_ref, mixed_ref, y_ref, x1_ref):
        ls = [l1[...], l2[...], l3[...]]
        m = jnp.maximum(jnp.maximum(ls[0], ls[1]), ls[2])
        es = [jnp.exp(l - m) for l in ls]
        den = es[0] + es[1] + es[2]
        attn = (es[0] * o1[...] + es[1] * o2[...] + es[2] * o3[...]) / den
        attn_ref[...] = attn
        lse_ref[...] = m + jnp.log(den)
        ah, _ = _rms_stats(attn)
        sh, _ = _rms_stats(sgu_ref[...])
        mixed = jnp.concatenate([ah * ga_ref[...], sh * gs_ref[...]], axis=1).astype(BF16)
        mixed_ref[...] = mixed
        y = _dot(mixed[:, 0:OUT_S], w_ref[0])
        for s in range(1, N_SHARD):
            y = y + _dot(mixed[:, s * OUT_S:(s + 1) * OUT_S], w_ref[s])
        y_ref[...] = y
        yh, _ = _rms_stats(y)
        x1_ref[...] = x_ref[...] + yh * gp_ref[...]

    return pl.pallas_call(
        body, name="mix_out_fwd", grid=(SEQ // TM,),
        in_specs=[_rows(512)] * 7 + [_rows(D_MODEL), _resident((N_SHARD, OUT_S, D_MODEL)),
                                    _resident((1, 512)), _resident((1, 512)), _resident((1, D_MODEL))],
        out_specs=[_rows(512), _rows(512), _rows(D_MODEL), _rows(D_MODEL), _rows(D_MODEL)],
        out_shape=[_sds((SEQ, 512), F32), _sds((SEQ, 512), F32), _sds((SEQ, D_MODEL), BF16),
                   _sds((SEQ, D_MODEL), F32), _sds((SEQ, D_MODEL), F32)],
        compiler_params=_seq_params(),
    )(*o_list, *l_list, sgu, x, w_out, g_attn, g_sgu, g_post)


def _ffn_fwd_bwd(x1, target, w_gate, w_up, w_down, g_pre, g_post):
    def body(x1_ref, t_ref, wg_ref, wu_ref, wd_ref, gpf_ref, gpo_ref,
             h2_ref, a_ref, dg_ref, dup_ref, df_ref, dx1_ref, loss_ref, dgpf_ref, dgpo_ref, g_scr, up_scr):
        @pl.when(pl.program_id(0) == 0)
        def _():
            loss_ref[...] = jnp.zeros_like(loss_ref)
            dgpf_ref[...] = jnp.zeros_like(dgpf_ref)
            dgpo_ref[...] = jnp.zeros_like(dgpo_ref)

        x1 = x1_ref[...]
        gpf = gpf_ref[...]
        gpo = gpo_ref[...]
        xh, r = _rms_stats(x1)
        h2 = (xh * gpf).astype(BF16)
        h2_ref[...] = h2
        f = jnp.zeros((TM, D_MODEL), F32)
        for s in range(N_SHARD):
            g = _dot(h2, wg_ref[s])
            up = _dot(h2, wu_ref[s])
            g_scr[s] = g
            up_scr[s] = up
            a = (g * _sigmoid(g) * up).astype(BF16)
            a_ref[s] = a
            f = f + _dot(a, wd_ref[s])
        fh, rf = _rms_stats(f)
        diff = x1 + fh * gpo - t_ref[...]
        loss_ref[...] += jnp.sum(diff * diff, axis=0, keepdims=True)
        dout = diff * np.float32(1.0 / D_MODEL)
        df, dgpo = _rms_bwd(fh, rf, gpo, dout)
        dgpo_ref[...] += dgpo
        dfb = df.astype(BF16)
        df_ref[...] = dfb
        dh2 = jnp.zeros((TM, D_MODEL), F32)
        for s in range(N_SHARD):
            da = _dot_nt(dfb, wd_ref[s])
            g = g_scr[s]
            up = up_scr[s]
            sg = _sigmoid(g)
            dup = (da * (g * sg)).astype(BF16)
            dg = (da * up * (sg * (1.0 + g * (1.0 - sg)))).astype(BF16)
            dg_ref[s] = dg
            dup_ref[s] = dup
            dh2 = dh2 + _dot_nt(dg, wg_ref[s]) + _dot_nt(dup, wu_ref[s])
        dx, dgpf = _rms_bwd(xh, r, gpf, dh2)
        dgpf_ref[...] += dgpf
        dx1_ref[...] = dout + dx

    return pl.pallas_call(
        body, name="ffn_fwd_bwd", grid=(SEQ // TM,),
        in_specs=[_rows(D_MODEL), _rows(D_MODEL), _resident((N_SHARD, D_MODEL, FF_S)),
                  _resident((N_SHARD, D_MODEL, FF_S)), _resident((N_SHARD, FF_S, D_MODEL)),
                  _resident((1, D_MODEL)), _resident((1, D_MODEL))],
        out_specs=[_rows(D_MODEL), _rows3(N_SHARD, FF_S), _rows3(N_SHARD, FF_S), _rows3(N_SHARD, FF_S),
                   _rows(D_MODEL), _rows(D_MODEL), _acc(D_MODEL), _acc(D_MODEL), _acc(D_MODEL)],
        out_shape=[_sds((SEQ, D_MODEL), BF16), _sds((N_SHARD, SEQ, FF_S), BF16), _sds((N_SHARD, SEQ, FF_S), BF16),
                   _sds((N_SHARD, SEQ, FF_S), BF16), _sds((SEQ, D_MODEL), BF16), _sds((SEQ, D_MODEL), F32),
                   _sds((1, D_MODEL), F32), _sds((1, D_MODEL), F32), _sds((1, D_MODEL), F32)],
        scratch_shapes=[pltpu.VMEM((N_SHARD, TM, FF_S), F32), pltpu.VMEM((N_SHARD, TM, FF_S), F32)],
        compiler_params=_seq_params(),
    )(x1, target, w_gate, w_up, w_down, g_pre, g_post)


def _wgrad(a, b, a_spec, b_spec, out_block, name):
    def body(a_ref, b_ref, o_ref):
        av = a_ref[0] if len(a_ref.shape) == 3 else a_ref[...]
        bv = b_ref[0] if len(b_ref.shape) == 3 else b_ref[...]
        o_ref[0] = _dot_tn(av, bv)

    return pl.pallas_call(
        body, name=name, grid=(N_SHARD,),
        in_specs=[a_spec, b_spec],
        out_specs=pl.BlockSpec((1,) + out_block, lambda s: (s, 0, 0)),
        out_shape=_sds((N_SHARD,) + out_block, F32),
        compiler_params=_seq_params(),
    )(a, b)


def _outproj_bwd(dx1, y, attn, sgu, w_out, g_post, g_attn, g_sgu):
    def body(dx1_ref, y_ref, attn_ref, sgu_ref, w_ref, gp_ref, ga_ref, gs_ref,
             dy_ref, dattn_ref, delta_ref, dsgu_ref, dgp_ref, dga_ref, dgs_ref):
        @pl.when(pl.program_id(0) == 0)
        def _():
            dgp_ref[...] = jnp.zeros_like(dgp_ref)
            dga_ref[...] = jnp.zeros_like(dga_ref)
            dgs_ref[...] = jnp.zeros_like(dgs_ref)

        yh, ry = _rms_stats(y_ref[...])
        dy, dgp = _rms_bwd(yh, ry, gp_ref[...], dx1_ref[...])
        dgp_ref[...] += dgp
        dyb = dy.astype(BF16)
        dy_ref[...] = dyb
        dmixed = jnp.concatenate([_dot_nt(dyb, w_ref[s]) for s in range(N_SHARD)], axis=1)
        attn = attn_ref[...]
        ah, ra = _rms_stats(attn)
        dattn, dga = _rms_bwd(ah, ra, ga_ref[...], dmixed[:, 0:512])
        dga_ref[...] += dga
        sh, rs = _rms_stats(sgu_ref[...])
        dsgu, dgs = _rms_bwd(sh, rs, gs_ref[...], dmixed[:, 512:1024])
        dgs_ref[...] += dgs
        dattn_ref[...] = dattn.astype(BF16)
        dsgu_ref[...] = dsgu
        la = lax.broadcasted_iota(jnp.int32, (ATTN_W, ATTN_W), 0) >> 6
        lb = lax.broadcasted_iota(jnp.int32, (ATTN_W, ATTN_W), 1) >> 6
        delta_ref[...] = _dot_exact(dattn * attn, (la == lb).astype(F32))

    return pl.pallas_call(
        body, name="outproj_bwd", grid=(SEQ // TM,),
        in_specs=[_rows(D_MODEL), _rows(D_MODEL), _rows(512), _rows(512), _resident((N_SHARD, OUT_S, D_MODEL)),
                  _resident((1, D_MODEL)), _resident((1, 512)), _resident((1, 512))],
        out_specs=[_rows(D_MODEL), _rows(512), _rows(512), _rows(512), _acc(D_MODEL), _acc(512), _acc(512)],
        out_shape=[_sds((SEQ, D_MODEL), BF16), _sds((SEQ, 512), BF16), _sds((SEQ, 512), F32), _sds((SEQ, 512), F32),
                   _sds((1, D_MODEL), F32), _sds((1, 512), F32), _sds((1, 512), F32)],
        compiler_params=_seq_params(),
    )(dx1, y, attn, sgu, w_out, g_post, g_attn, g_sgu)


def _sgu_bwd(u, vs, dsgu, lg, lb, w_sp, b_t):
    nsteps = SEQ // TM

    def body(u_ref, vs_ref, ds_ref, lg_ref, lb_ref, w_ref, bt_ref,
             du_ref, dvs_ref, dw_ref, db_ref, dlg_ref, dlb_ref, dbias_scr):
        i = pl.program_id(0)

        @pl.when(i == 0)
        def _():
            dw_ref[...] = jnp.zeros_like(dw_ref)
            dlg_ref[...] = jnp.zeros_like(dlg_ref)
            dlb_ref[...] = jnp.zeros_like(dlb_ref)
            dbias_scr[...] = jnp.zeros_like(dbias_scr)

        wm = _masked_spatial(w_ref)
        ones_g = _group_ones()
        bias_full = _dot_exact(bt_ref[...], ones_g)
        u = u_ref[...]
        vs = vs_ref[...]
        lg = lg_ref[...]
        gu, xh, rstd, vnb, mixed = _sgu_core(u, vs, lg, lb_ref[...], wm, bias_full)
        dsgu = ds_ref[...]
        du_ref[...] = (dsgu * mixed * _gelu_grad(u)).astype(BF16)
        dmixed = dsgu * gu
        left = _left_half()
        dvn_rows = []
        for c in range(TM // CHUNK):
            rs = slice(c * CHUNK, (c + 1) * CHUNK)
            dm_c = dmixed[rs, :]
            dbias_scr[...] += dm_c
            pieces = []
            for p in range(4):
                ls = slice(p * 128, (p + 1) * 128)
                dmp = dm_c[:, ls]
                vp = vnb[rs, ls]
                dmb = dmp.astype(BF16)
                zero = jnp.zeros_like(dmb)
                dw_ref[2 * p] += _dot_nt(jnp.where(left, dmb, zero), vp)
                dw_ref[2 * p + 1] += _dot_nt(jnp.where(left, zero, dmb), vp)
                pieces.append(jnp.where(left, _dot_tn(wm[2 * p], dmb), _dot_tn(wm[2 * p + 1], dmb)))
            dvn_rows.append(jnp.concatenate(pieces, axis=1))
        dvn = jnp.concatenate(dvn_rows, axis=0)
        dlg_ref[...] += jnp.sum(dvn * xh, axis=0, keepdims=True)
        dlb_ref[...] += jnp.sum(dvn, axis=0, keepdims=True)
        dxh = dvn * lg
        dgv = rstd * (dxh - jnp.mean(dxh, axis=-1, keepdims=True) - xh * jnp.mean(dxh * xh, axis=-1, keepdims=True))
        dvs_ref[...] = (dgv * _gelu_grad(vs)).astype(BF16)

        @pl.when(i == nsteps - 1)
        def _():
            row = lax.broadcasted_iota(jnp.int32, (CHUNK, CHUNK), 0)
            col = lax.broadcasted_iota(jnp.int32, (CHUNK, CHUNK), 1)
            for g in range(SGU_GROUPS):
                dw_ref[g] = jnp.where(col <= row, dw_ref[g], 0.0)
            db_ref[...] = lax.dot_general(ones_g, dbias_scr[...], (((1,), (1,)), ((), ())),
                                          preferred_element_type=F32, precision=lax.Precision.HIGHEST)

    return pl.pallas_call(
        body, name="sgu_bwd", grid=(nsteps,),
        in_specs=[_rows(SGU_W), _rows(SGU_W), _rows(SGU_W), _resident((1, SGU_W)), _resident((1, SGU_W)),
                  _resident((SGU_GROUPS, CHUNK, CHUNK)), _resident((CHUNK, SGU_GROUPS))],
        out_specs=[_rows(SGU_W), _rows(SGU_W), pl.BlockSpec((SGU_GROUPS, CHUNK, CHUNK), lambda i: (0, 0, 0)),
                   _acc(CHUNK, SGU_GROUPS), _acc(SGU_W), _acc(SGU_W)],
        out_shape=[_sds((SEQ, SGU_W), BF16), _sds((SEQ, SGU_W), BF16), _sds((SGU_GROUPS, CHUNK, CHUNK), F32),
                   _sds((SGU_GROUPS, CHUNK), F32), _sds((1, SGU_W), F32), _sds((1, SGU_W), F32)],
        scratch_shapes=[pltpu.VMEM((CHUNK, SGU_W), F32)],
        compiler_params=_seq_params(),
    )(u, vs, dsgu, lg, lb, w_sp, b_t)


def _attn_bwd(qv, kv, vv, dov, deltav, lsev, dil):
    seg = SEQ // dil
    nblk = seg // CHUNK

    def body(q_ref, k_ref, v_ref, do_ref, dl_ref, lse_ref, dq_ref, dk_ref, dv_ref):
        left = _left_half()
        m_cur, m_prev = _block_masks()

        def blk(b, carry):
            r0 = pl.multiple_of(b * CHUNK, CHUNK)
            rp = pl.multiple_of(jnp.maximum(b - 1, 0) * CHUNK, CHUNK)
            prev_ok = m_prev & (b > 0)
            for hp in range(4):
                ls = slice(hp * 128, (hp + 1) * 128)
                qp = q_ref[pl.ds(r0, CHUNK), ls]
                kc = k_ref[pl.ds(r0, CHUNK), ls]
                vc = v_ref[pl.ds(r0, CHUNK), ls]
                dop = do_ref[pl.ds(r0, CHUNK), ls]
                lsep = lse_ref[pl.ds(r0, CHUNK), ls]
                dlp = dl_ref[pl.ds(r0, CHUNK), ls]
                if nblk > 1:
                    kp = k_ref[pl.ds(rp, CHUNK), ls]
                    vp = v_ref[pl.ds(rp, CHUNK), ls]
                dqs = []
                dkc = jnp.zeros((CHUNK, 128), F32)
                dvc = jnp.zeros((CHUNK, 128), F32)
                dkp = jnp.zeros((CHUNK, 128), F32)
                dvp = jnp.zeros((CHUNK, 128), F32)
                for side, hm in enumerate((left, ~left)):
                    c0 = side * HEAD_DIM
                    qh = jnp.where(hm, qp, jnp.zeros_like(qp))
                    doh = jnp.where(hm, dop, jnp.zeros_like(dop))
                    lse_h = lsep[:, c0:c0 + 1]
                    dl_h = dlp[:, c0:c0 + 1]
                    sc = _dot_nt(qh, kc) * ATTN_SCALE
                    pc = jnp.exp(jnp.where(m_cur, sc - lse_h, NEG))
                    dsc = (pc * (_dot_nt(doh, vc) - dl_h) * ATTN_SCALE).astype(BF16)
                    dq = _dot(dsc, kc)
                    dkc = dkc + _dot_tn(dsc, qh)
                    dvc = dvc + _dot_tn(pc.astype(BF16), doh)
                    if nblk > 1:
                        sp = _dot_nt(qh, kp) * ATTN_SCALE
                        pp = jnp.exp(jnp.where(prev_ok, sp - lse_h, NEG))
                        dsp = (pp * (_dot_nt(doh, vp) - dl_h) * ATTN_SCALE).astype(BF16)
                        dq = dq + _dot(dsp, kp)
                        dkp = dkp + _dot_tn(dsp, qh)
                        dvp = dvp + _dot_tn(pp.astype(BF16), doh)
                    dqs.append(dq)
                dq_ref[pl.ds(r0, CHUNK), ls] = jnp.where(left, dqs[0], dqs[1])
                dk_ref[pl.ds(r0, CHUNK), ls] = dkc
                dv_ref[pl.ds(r0, CHUNK), ls] = dvc
                if nblk > 1:
                    @pl.when(b > 0)
                    def _():
                        dk_ref[pl.ds(rp, CHUNK), ls] += dkp
                        dv_ref[pl.ds(rp, CHUNK), ls] += dvp
            return carry

        lax.fori_loop(0, nblk, blk, 0)

    spec = pl.BlockSpec((seg, ATTN_W), lambda r: (0, r))
    return pl.pallas_call(
        body, name=f"attn_bwd_d{dil}", grid=(dil,),
        in_specs=[spec] * 6, out_specs=[spec] * 3,
        out_shape=[_sds((seg, dil * ATTN_W), F32)] * 3,
        compiler_params=_seq_params(),
    )(qv, kv, vv, dov, deltav, lsev)


def _inproj_bwd(dqs, dks, dvs, du, dvs_sgu, pos, x, dx1, w_in, g_pre):
    def body(dq1, dq2, dq3, dk1, dk2, dk3, dv1, dv2, dv3, du_ref, dvs_ref, pos_ref, x_ref, dx1_ref, w_ref, g_ref,
             dproj_ref, gx_ref, dg_ref):
        @pl.when(pl.program_id(0) == 0)
        def _():
            dg_ref[...] = jnp.zeros_like(dg_ref)

        tabs = _rot_tables(pos_ref[...])
        dproj_ref[:, 0:512] = _rope_bwd(dq1[...] + dq2[...] + dq3[...], tabs).astype(BF16)
        dproj_ref[:, 512:1024] = _rope_bwd(dk1[...] + dk2[...] + dk3[...], tabs).astype(BF16)
        dproj_ref[:, 1024:1536] = (dv1[...] + dv2[...] + dv3[...]).astype(BF16)
        dproj_ref[:, 1536:2048] = du_ref[...]
        dproj_ref[:, 2048:2560] = dvs_ref[...]
        dh = jnp.zeros((TM, D_MODEL), F32)
        for s in range(N_SHARD):
            dh = dh + _dot_nt(dproj_ref[:, s * IN_S:(s + 1) * IN_S], w_ref[s])
        g = g_ref[...]
        xh, r = _rms_stats(x_ref[...])
        dx, dg = _rms_bwd(xh, r, g, dh)
        dg_ref[...] += dg
        gx_ref[...] = dx1_ref[...] + dx

    return pl.pallas_call(
        body, name="inproj_bwd", grid=(SEQ // TM,),
        in_specs=[_rows(512)] * 11 + [_rows(1), _rows(D_MODEL), _rows(D_MODEL),
                                     _resident((N_SHARD, D_MODEL, IN_S)), _resident((1, D_MODEL))],
        out_specs=[_rows(PROJ_W), _rows(D_MODEL), _acc(D_MODEL)],
        out_shape=[_sds((SEQ, PROJ_W), BF16), _sds((SEQ, D_MODEL), F32), _sds((1, D_MODEL), F32)],
        compiler_params=_seq_params(),
    )(*dqs, *dks, *dvs, du, dvs_sgu, pos, x, dx1, w_in, g_pre)


def _to_view(a, dil):
    return a if dil == 1 else a.reshape(SEQ // dil, dil * a.shape[1])


def _from_view(a, dil):
    return a if dil == 1 else a.reshape(SEQ, a.shape[1] // dil)


def _local_step(x, pos, target, w_in, w_out, w_gate, w_up, w_down, small):
    b_t = small["sgu_b_spatial"].T
    h, q, k, v, u, vs = _inproj_fwd(x, pos, small["pre_mix_norm"], w_in)
    sgu = _sgu_fwd(u, vs, small["sgu_ln_gain"], small["sgu_ln_bias"], small["sgu_w_spatial"], b_t)
    views = [tuple(_to_view(t, dil) for t in (q, k, v)) for dil in DILATIONS]
    o_list, l_list = [], []
    for dil, (qv, kv, vv) in zip(DILATIONS, views):
        o, l = _attn_fwd(qv, kv, vv, dil)
        o_list.append(_from_view(o, dil))
        l_list.append(_from_view(l, dil))
    attn, lse, mixed, y, x1 = _mix_out_fwd(o_list, l_list, sgu, x, w_out, small["attn_out_norm"],
                                           small["sgu_out_norm"], small["post_mix_norm"])
    h2, a, dg, dup, df, dx1, loss_cols, d_pre_ffn, d_post_ffn = _ffn_fwd_bwd(
        x1, target, w_gate, w_up, w_down, small["pre_ffn_norm"], small["post_ffn_norm"])

    full_tok = pl.BlockSpec((SEQ, D_MODEL), lambda s: (0, 0), pipeline_mode=pl.Buffered(1))
    ff_tok = pl.BlockSpec((1, SEQ, FF_S), lambda s: (s, 0, 0))
    gw_gate = _wgrad(h2, dg, full_tok, ff_tok, (D_MODEL, FF_S), "wgrad_gate")
    gw_up = _wgrad(h2, dup, full_tok, ff_tok, (D_MODEL, FF_S), "wgrad_up")
    gw_down = _wgrad(a, df, ff_tok, full_tok, (FF_S, D_MODEL), "wgrad_down")

    dy, dattn, delta, dsgu, d_post_mix, d_attn_norm, d_sgu_norm = _outproj_bwd(
        dx1, y, attn, sgu, w_out, small["post_mix_norm"], small["attn_out_norm"], small["sgu_out_norm"])
    gw_out = _wgrad(mixed, dy, pl.BlockSpec((SEQ, OUT_S), lambda s: (0, s)), full_tok, (OUT_S, D_MODEL), "wgrad_out")
    du, dvs_sgu, d_w_sp, d_b_sp, d_ln_gain, d_ln_bias = _sgu_bwd(
        u, vs, dsgu, small["sgu_ln_gain"], small["sgu_ln_bias"], small["sgu_w_spatial"], b_t)

    dqs, dks, dvs = [], [], []
    for dil, (qv, kv, vv) in zip(DILATIONS, views):
        dq, dk, dv = _attn_bwd(qv, kv, vv, _to_view(dattn, dil), _to_view(delta, dil), _to_view(lse, dil), dil)
        dqs.append(_from_view(dq, dil))
        dks.append(_from_view(dk, dil))
        dvs.append(_from_view(dv, dil))
    dproj, grad_x, d_pre_mix = _inproj_bwd(dqs, dks, dvs, du, dvs_sgu, pos, x, dx1, w_in, small["pre_mix_norm"])
    gw_in = _wgrad(h, dproj, full_tok, pl.BlockSpec((SEQ, IN_S), lambda s: (0, s)), (D_MODEL, IN_S), "wgrad_in")

    small_grads = {
        "pre_mix_norm": d_pre_mix, "sgu_ln_gain": d_ln_gain, "sgu_ln_bias": d_ln_bias, "sgu_w_spatial": d_w_sp,
        "sgu_b_spatial": d_b_sp, "attn_out_norm": d_attn_norm, "sgu_out_norm": d_sgu_norm,
        "post_mix_norm": d_post_mix, "pre_ffn_norm": d_pre_ffn, "post_ffn_norm": d_post_ffn,
    }
    return loss_cols, grad_x, (gw_in, gw_out, gw_gate, gw_up, gw_down), small_grads


def _coords():
    return lax.axis_index("x"), lax.axis_index("y"), lax.axis_index("c")


def _other_chips(x, y):
    return [(1 - x, y), (x, 1 - y), (1 - x, 1 - y)]


def _comm_call(body, name, n_in, out_shape, scratch_shapes):
    return pl.pallas_call(
        body, name=name, in_specs=[ANY] * n_in, out_specs=[ANY] * len(out_shape), out_shape=out_shape,
        scratch_shapes=scratch_shapes,
        compiler_params=pltpu.CompilerParams(has_side_effects=True),
    )


def _gather_weights(shards):
    n = len(shards)
    halves = [s.reshape(2, s.shape[0] // 2, s.shape[1]) for s in shards]

    def body(*refs):
        ins, outs = refs[:n], refs[n:2 * n]
        send_sems, recv_sems = refs[2 * n:]
        x, y, c = _coords()
        s_me = 2 * x + y
        chips = _other_chips(x, y)
        sibling = (x, y, 1 - c)

        def copy(k, w, shard, cc, to):
            src = ins[w].at[cc] if shard is None else outs[w].at[shard, cc]
            dst = outs[w].at[s_me if shard is None else shard, cc]
            return pltpu.make_async_remote_copy(src_ref=src, dst_ref=dst, send_sem=send_sems.at[k],
                                                recv_sem=recv_sems.at[k], device_id=to, device_id_type=MESH)

        first = [copy(j * n + w, w, None, c, (cx, cy, c)) for j, (cx, cy) in enumerate(chips) for w in range(n)]
        for cp in first:
            cp.start()
        passed = []
        for j, (cx, cy) in enumerate(chips):
            for w in range(n):
                copy(j * n + w, w, 2 * cx + cy, c, (x, y, c)).wait_recv()
                fw = copy((3 + j) * n + w, w, 2 * cx + cy, c, sibling)
                fw.start()
                passed.append(fw)
        for j, (cx, cy) in enumerate(chips):
            for w in range(n):
                copy((3 + j) * n + w, w, 2 * cx + cy, 1 - c, (x, y, c)).wait_recv()
        for cp in first + passed:
            cp.wait_send()

    out_shape = [_sds((N_SHARD,) + h.shape, h.dtype) for h in halves]
    scratch = [pltpu.SemaphoreType.DMA((6 * n,)), pltpu.SemaphoreType.DMA((6 * n,))]
    full = _comm_call(body, "comm_gather_weights", n, out_shape, scratch)(*halves)
    s_me = 2 * lax.axis_index("x") + lax.axis_index("y")
    full = [lax.dynamic_update_slice(f, h[None], (s_me, 0, 0, 0)) for f, h in zip(full, halves)]
    return [f.reshape((N_SHARD,) + s.shape) for f, s in zip(full, shards)]


def _rs_to_sibling(gws):
    n = len(gws)

    def body(*refs):
        ins, outs = refs[:n], refs[n:2 * n]
        send_sems, recv_sems = refs[2 * n:]
        x, y, c = _coords()
        copies = []
        for w in range(n):
            hw = gws[w].shape[1] // 2
            copies.append(pltpu.make_async_remote_copy(
                src_ref=ins[w].at[:, pl.ds((1 - c) * hw, hw), :], dst_ref=outs[w], send_sem=send_sems.at[w],
                recv_sem=recv_sems.at[w], device_id=(x, y, 1 - c), device_id_type=MESH))
        for cp in copies:
            cp.start()
        for cp in copies:
            cp.wait()

    out_shape = [_sds((N_SHARD, g.shape[1] // 2, g.shape[2]), g.dtype) for g in gws]
    scratch = [pltpu.SemaphoreType.DMA((n,)), pltpu.SemaphoreType.DMA((n,))]
    return _comm_call(body, "comm_rs_sibling", n, out_shape, scratch)(*gws)


def _rs_chip_sum(gw, recv, core):
    _, rows, cols = gw.shape
    hw = rows // 2

    def body(c_ref, g_ref, r_ref, o_ref):
        o_ref[...] = (g_ref[...] + r_ref[...]).astype(BF16)

    return pl.pallas_call(
        body, name="rs_chip_sum",
        grid_spec=pltpu.PrefetchScalarGridSpec(
            num_scalar_prefetch=1, grid=(N_SHARD,),
            in_specs=[pl.BlockSpec((1, hw, cols), lambda s, c_ref: (s, c_ref[0], 0)),
                      pl.BlockSpec((1, hw, cols), lambda s, c_ref: (s, 0, 0))],
            out_specs=pl.BlockSpec((1, hw, cols), lambda s, c_ref: (s, 0, 0))),
        out_shape=_sds((N_SHARD, hw, cols), BF16),
        compiler_params=_seq_params(),
    )(core, gw, recv)


def _rs_between_chips(pbs):
    n = len(pbs)

    def body(*refs):
        ins, outs = refs[:n], refs[n:2 * n]
        send_sems, recv_sems = refs[2 * n:]
        x, y, c = _coords()
        copies = []
        for j, (cx, cy) in enumerate(_other_chips(x, y)):
            for w in range(n):
                copies.append(pltpu.make_async_remote_copy(
                    src_ref=ins[w].at[2 * cx + cy], dst_ref=outs[w].at[j], send_sem=send_sems.at[j * n + w],
                    recv_sem=recv_sems.at[j * n + w], device_id=(cx, cy, c), device_id_type=MESH))
        for cp in copies:
            cp.start()
        for cp in copies:
            cp.wait()

    out_shape = [_sds((3,) + p.shape[1:], p.dtype) for p in pbs]
    scratch = [pltpu.SemaphoreType.DMA((3 * n,)), pltpu.SemaphoreType.DMA((3 * n,))]
    return _comm_call(body, "comm_rs_chips", n, out_shape, scratch)(*pbs)


def _rs_final_sum(gw, recv_sib, recv_chips, shard_core):
    _, rows, cols = gw.shape
    hw = rows // 2

    def body(sc_ref, g_ref, r_ref, rc_ref, o_ref):
        acc = g_ref[0] + r_ref[0]
        for j in range(3):
            acc = acc + rc_ref[j].astype(F32)
        o_ref[0] = acc

    return pl.pallas_call(
        body, name="rs_final_sum",
        grid_spec=pltpu.PrefetchScalarGridSpec(
            num_scalar_prefetch=1, grid=(1,),
            in_specs=[pl.BlockSpec((1, hw, cols), lambda i, sc: (sc[0], sc[1], 0)),
                      pl.BlockSpec((1, hw, cols), lambda i, sc: (sc[0], 0, 0)),
                      pl.BlockSpec((3, hw, cols), lambda i, sc: (0, 0, 0))],
            out_specs=pl.BlockSpec((1, hw, cols), lambda i, sc: (sc[1], 0, 0))),
        out_shape=_sds((2, hw, cols), F32),
        compiler_params=_seq_params(),
    )(shard_core, gw, recv_sib, recv_chips)


def _rs_join_halves(halves):
    n = len(halves)

    def body(*refs):
        bufs = refs[n:2 * n]
        send_sems, recv_sems = refs[2 * n:]
        x, y, c = _coords()
        remote = [pltpu.make_async_remote_copy(
            src_ref=bufs[w].at[c], dst_ref=bufs[w].at[c], send_sem=send_sems.at[w], recv_sem=recv_sems.at[w],
            device_id=(x, y, 1 - c), device_id_type=MESH) for w in range(n)]
        for cp in remote:
            cp.start()
        for w in range(n):
            remote[w].wait_send()
            pltpu.make_async_remote_copy(
                src_ref=bufs[w].at[c], dst_ref=bufs[w].at[1 - c], send_sem=send_sems.at[w],
                recv_sem=recv_sems.at[w], device_id=(x, y, c), device_id_type=MESH).wait_recv()

    joined = pl.pallas_call(
        body, name="comm_rs_join", in_specs=[ANY] * n, out_specs=[ANY] * n,
        out_shape=[_sds(h.shape, h.dtype) for h in halves], input_output_aliases={w: w for w in range(n)},
        scratch_shapes=[pltpu.SemaphoreType.DMA((n,)), pltpu.SemaphoreType.DMA((n,))],
        compiler_params=pltpu.CompilerParams(has_side_effects=True),
    )(*halves)
    return [j.reshape(2 * h.shape[1], h.shape[2]) for j, h in zip(joined, halves)]


def _allreduce_small(buf):
    rows, cols = buf.shape

    def body(in_ref, out_ref, slots, send_sems, recv_sems):
        x, y, c = _coords()
        me = 4 * x + 2 * y + c
        copies, peers = [], []
        for k in range(1, 8):
            px = 1 - x if (k >> 2) & 1 else x
            py = 1 - y if (k >> 1) & 1 else y
            pc = 1 - c if k & 1 else c
            peers.append(4 * px + 2 * py + pc)
            copies.append(pltpu.make_async_remote_copy(
                src_ref=in_ref, dst_ref=slots.at[me], send_sem=send_sems.at[k - 1], recv_sem=recv_sems.at[k - 1],
                device_id=(px, py, pc), device_id_type=MESH))
        for cp in copies:
            cp.start()
        slots[me] = in_ref[...]
        for k in range(7):
            pltpu.make_async_remote_copy(
                src_ref=in_ref, dst_ref=slots.at[peers[k]], send_sem=send_sems.at[k], recv_sem=recv_sems.at[k],
                device_id=(x, y, c), device_id_type=MESH).wait_recv()
        for cp in copies:
            cp.wait_send()
        acc = slots[0]
        for i in range(1, 8):
            acc = acc + slots[i]
        out_ref[...] = acc

    vmem = pl.BlockSpec(memory_space=pltpu.VMEM)
    return pl.pallas_call(
        body, name="comm_allreduce_small", in_specs=[vmem], out_specs=vmem, out_shape=_sds((rows, cols), F32),
        scratch_shapes=[pltpu.VMEM((8, rows, cols), F32), pltpu.SemaphoreType.DMA((7,)), pltpu.SemaphoreType.DMA((7,))],
        compiler_params=pltpu.CompilerParams(has_side_effects=True, vmem_limit_bytes=VMEM_LIMIT),
    )(buf)


def _adamw(w, g, m, v, block_rows, name):
    rows, cols = w.shape

    def body(w_ref, g_ref, m_ref, v_ref, d_ref, nm_ref, nv_ref):
        g = g_ref[...]
        m = ADAM_B1 * m_ref[...] + (1.0 - ADAM_B1) * g
        v = ADAM_B2 * v_ref[...] + (1.0 - ADAM_B2) * (g * g)
        m_hat = m / (1.0 - ADAM_B1 ** ADAM_STEP)
        v_hat = v / (1.0 - ADAM_B2 ** ADAM_STEP)
        d_ref[...] = -ADAM_LR * (m_hat / (jnp.sqrt(v_hat) + ADAM_EPS) + ADAM_WD * w_ref[...])
        nm_ref[...] = m
        nv_ref[...] = v

    spec = pl.BlockSpec((block_rows, cols), lambda i: (i, 0))
    return pl.pallas_call(
        body, name=name, grid=(rows // block_rows,), in_specs=[spec] * 4, out_specs=[spec] * 3,
        out_shape=[_sds((rows, cols), F32)] * 3,
        compiler_params=_seq_params(),
    )(w, g, m, v)


WEIGHTS = ("pre_mix_norm", "w_in", "sgu_ln_gain", "sgu_ln_bias", "sgu_w_spatial", "sgu_b_spatial", "attn_out_norm",
           "sgu_out_norm", "w_out", "post_mix_norm", "pre_ffn_norm", "w_gate", "w_up", "w_down", "post_ffn_norm")
BIG = ("w_in", "w_out", "w_gate", "w_up", "w_down")
BIG_ADAM_ROWS = {"w_in": 256, "w_out": 128, "w_gate": 256, "w_up": 256, "w_down": 352}
SMALL = ("pre_mix_norm", "post_mix_norm", "pre_ffn_norm", "post_ffn_norm", "sgu_ln_gain", "sgu_ln_bias",
         "attn_out_norm", "sgu_out_norm", "sgu_w_spatial", "sgu_b_spatial")


def _pack_small(d):
    flat = [d[n].reshape(-1) for n in SMALL]
    used = sum(f.shape[0] for f in flat)
    flat.append(jnp.zeros((SMALL_ROWS * 1024 - used,), F32))
    return jnp.concatenate(flat).reshape(SMALL_ROWS, 1024)


def _unpack_small(buf, shapes):
    flat = buf.reshape(-1)
    out, off = {}, 0
    for n in SMALL:
        size = int(np.prod(shapes[n]))
        out[n] = flat[off:off + size].reshape(shapes[n])
        off += size
    return out


def kernel(x, positions, pre_mix_norm, w_in, sgu_ln_gain, sgu_ln_bias, sgu_w_spatial, sgu_b_spatial, attn_out_norm, sgu_out_norm, w_out, post_mix_norm, pre_ffn_norm, w_gate, w_up, w_down, post_ffn_norm, loss_target, m_pre_mix_norm, m_w_in, m_sgu_ln_gain, m_sgu_ln_bias, m_sgu_w_spatial, m_sgu_b_spatial, m_attn_out_norm, m_sgu_out_norm, m_w_out, m_post_mix_norm, m_pre_ffn_norm, m_w_gate, m_w_up, m_w_down, m_post_ffn_norm, v_pre_mix_norm, v_w_in, v_sgu_ln_gain, v_sgu_ln_bias, v_sgu_w_spatial, v_sgu_b_spatial, v_attn_out_norm, v_sgu_out_norm, v_w_out, v_post_mix_norm, v_pre_ffn_norm, v_w_gate, v_w_up, v_w_down, v_post_ffn_norm):
    a = dict(locals())
    cx, cy, cc = _coords()
    core = jnp.stack([cc]).astype(jnp.int32)
    shard_core = jnp.stack([2 * cx + cy, cc]).astype(jnp.int32)

    full = _gather_weights([a[n][0].astype(BF16) for n in BIG])
    small = {n: (a[n][0] if a[n].ndim > 2 else a[n]) for n in SMALL}
    loss_cols, grad_x, gws, small_grads = _local_step(
        x[0], positions.reshape(SEQ, 1), loss_target[0], *full, small)
    loss = lax.psum(jnp.sum(loss_cols) * np.float32(0.5 / D_MODEL), ("x", "y", "c"))

    recv_sib = _rs_to_sibling(list(gws))
    chip_part = [_rs_chip_sum(g, r, core) for g, r in zip(gws, recv_sib)]
    recv_chips = _rs_between_chips(chip_part)
    halves = [_rs_final_sum(g, r, rc, shard_core) for g, r, rc in zip(gws, recv_sib, recv_chips)]
    big_grads = dict(zip(BIG, _rs_join_halves(halves)))

    shapes = {n: a[n].shape for n in SMALL}
    small_sum = _allreduce_small(_pack_small(small_grads))

    grads, deltas, new_m, new_v = {}, {}, {}, {}
    for n in BIG:
        grads[n] = big_grads[n][None]
        d, nm, nv = _adamw(a[n][0], big_grads[n], a["m_" + n][0], a["v_" + n][0], BIG_ADAM_ROWS[n], "adamw_" + n)
        deltas[n], new_m[n], new_v[n] = d[None], nm[None], nv[None]
    d, nm, nv = _adamw(_pack_small({n: a[n] for n in SMALL}), small_sum, _pack_small({n: a["m_" + n] for n in SMALL}),
                       _pack_small({n: a["v_" + n] for n in SMALL}), SMALL_ROWS, "adamw_small")
    grads.update(_unpack_small(small_sum, shapes))
    deltas.update(_unpack_small(d, shapes))
    new_m.update(_unpack_small(nm, shapes))
    new_v.update(_unpack_small(nv, shapes))
    return (loss, grad_x[None], *[grads[n] for n in WEIGHTS], *[deltas[n] for n in WEIGHTS],
            *[new_m[n] for n in WEIGHTS], *[new_v[n] for n in WEIGHTS])
```

```python
import numpy as np
import jax
import jax.numpy as jnp
from jax import lax
from jax.experimental import pallas as pl
from jax.experimental.pallas import tpu as pltpu

F32 = jnp.float32
BF16 = jnp.bfloat16

SEQ = 2048
D_MODEL = 1024
HEAD_DIM = 64
ATTN_W = 512
SGU_W = 512
SGU_GROUPS = 8
CHUNK = 128
DILATIONS = (1, 4, 16)
N_SHARD = 4
IN_S = 640
OUT_S = 256
FF_S = 704
PROJ_W = N_SHARD * IN_S
RMS_EPS = 1e-6
LN_EPS = 1e-5
ROPE_THETA = 500000.0
ATTN_SCALE = 1.0 / np.sqrt(HEAD_DIM)
NEG = -1e30
TM = 256
VMEM_LIMIT = 56 * 1024 * 1024
SMALL_ROWS = 136

ADAM_LR = 0.001
ADAM_B1 = 0.9
ADAM_B2 = 0.999
ADAM_EPS = 1e-08
ADAM_WD = 0.01
ADAM_STEP = 10

MESH = pl.DeviceIdType.MESH
ANY = pl.BlockSpec(memory_space=pl.ANY)


def _dot(a, b):
    return jnp.dot(a, b, preferred_element_type=F32)


def _dot_nt(a, b):
    return lax.dot_general(a, b, (((1,), (1,)), ((), ())), preferred_element_type=F32)


def _dot_tn(a, b):
    return lax.dot_general(a, b, (((0,), (0,)), ((), ())), preferred_element_type=F32)


def _dot_exact(a, b):
    return jnp.dot(a, b, preferred_element_type=F32, precision=lax.Precision.HIGHEST)


def _rms_stats(x):
    r = lax.rsqrt(jnp.mean(x * x, axis=-1, keepdims=True) + RMS_EPS)
    return x * r, r


def _rms_bwd(xh, r, gain, dy):
    dxh = dy * gain
    dx = r * (dxh - xh * jnp.mean(dxh * xh, axis=-1, keepdims=True))
    return dx, jnp.sum(dy * xh, axis=0, keepdims=True)


_ERF_ALPHA = (-2.72614225801306e-10, 2.77068142495902e-08, -2.10102402082508e-06, -5.69250639462346e-05,
              -7.34990630326855e-04, -2.95459980854025e-03, -1.60960333262415e-02)
_ERF_BETA = (-1.45660718464996e-05, -2.13374055278905e-04, -1.68282697438203e-03, -7.37332916720468e-03,
             -1.42647390514189e-02)


def _erf(x):
    x = jnp.clip(x, -4.0, 4.0)
    x2 = x * x
    p = jnp.full_like(x, _ERF_ALPHA[0])
    for a in _ERF_ALPHA[1:]:
        p = p * x2 + a
    q = jnp.full_like(x, _ERF_BETA[0])
    for b in _ERF_BETA[1:]:
        q = q * x2 + b
    return x * p / q


def _gelu(x):
    return 0.5 * x * (1.0 + _erf(x * np.float32(1.0 / np.sqrt(2.0))))


def _gelu_grad(x):
    cdf = 0.5 * (1.0 + _erf(x * np.float32(1.0 / np.sqrt(2.0))))
    pdf = jnp.exp(-0.5 * x * x) * np.float32(1.0 / np.sqrt(2.0 * np.pi))
    return cdf + x * pdf


def _sigmoid(x):
    return 1.0 / (1.0 + jnp.exp(-x))


_INV_FREQ = tuple(float(np.float32(ROPE_THETA ** (-2.0 * j / 16.0))) for j in range(8))


def _rot_tables(pos):
    lane = lax.broadcasted_iota(jnp.int32, (1, 128), 1)
    d = lane & 63
    j = d & 7
    inv = jnp.zeros((1, 128), F32)
    for jj in range(8):
        inv = jnp.where(j == jj, _INV_FREQ[jj], inv)
    ang = pos.astype(F32) * inv
    c = jnp.cos(ang)
    s = jnp.sin(ang)
    cos_t = jnp.where(d < 16, c, 1.0)
    sin_a = jnp.where(d < 8, -s, 0.0)
    sin_b = jnp.where((d >= 8) & (d < 16), s, 0.0)
    return tuple(jnp.tile(t, (1, 4)) for t in (cos_t, sin_a, sin_b))


def _rope(x, tabs):
    cos_t, sin_a, sin_b = tabs
    return x * cos_t + pltpu.roll(x, 504, 1) * sin_a + pltpu.roll(x, 8, 1) * sin_b


def _rope_bwd(dy, tabs):
    cos_t, sin_a, sin_b = tabs
    return dy * cos_t + pltpu.roll(dy * sin_a, 8, 1) + pltpu.roll(dy * sin_b, 504, 1)


def _left_half():
    return lax.broadcasted_iota(jnp.int32, (CHUNK, CHUNK), 1) < HEAD_DIM


def _group_ones():
    lane = lax.broadcasted_iota(jnp.int32, (SGU_GROUPS, SGU_W), 1)
    row = lax.broadcasted_iota(jnp.int32, (SGU_GROUPS, SGU_W), 0)
    return ((lane >> 6) == row).astype(F32)


def _masked_spatial(w_ref):
    row = lax.broadcasted_iota(jnp.int32, (CHUNK, CHUNK), 0)
    col = lax.broadcasted_iota(jnp.int32, (CHUNK, CHUNK), 1)
    return [jnp.where(col <= row, w_ref[g], 0.0).astype(BF16) for g in range(SGU_GROUPS)]


def _sgu_core(u, vs, lg, lb, wm, bias_full):
    tm = u.shape[0]
    gu = _gelu(u)
    gv = _gelu(vs)
    mu = jnp.mean(gv, axis=-1, keepdims=True)
    xc = gv - mu
    rstd = lax.rsqrt(jnp.mean(xc * xc, axis=-1, keepdims=True) + LN_EPS)
    xh = xc * rstd
    vnb = (xh * lg + lb).astype(BF16)
    left = _left_half()
    rows = []
    for c in range(tm // CHUNK):
        pieces = []
        for p in range(4):
            vp = vnb[c * CHUNK:(c + 1) * CHUNK, p * 128:(p + 1) * 128]
            pieces.append(jnp.where(left, _dot(wm[2 * p], vp), _dot(wm[2 * p + 1], vp)))
        rows.append(jnp.concatenate(pieces, axis=1) + bias_full)
    mixed = jnp.concatenate(rows, axis=0)
    return gu, xh, rstd, vnb, mixed


def _resident(shape):
    n = len(shape)
    return pl.BlockSpec(shape, lambda *_: (0,) * n, pipeline_mode=pl.Buffered(1))


def _rows(ncol, tm=TM):
    return pl.BlockSpec((tm, ncol), lambda i: (i, 0))


def _rows3(nlead, ncol, tm=TM):
    return pl.BlockSpec((nlead, tm, ncol), lambda i: (0, i, 0))


def _acc(ncol, nrow=1):
    return pl.BlockSpec((nrow, ncol), lambda i: (0, 0))


def _seq_params():
    return pltpu.CompilerParams(dimension_semantics=("arbitrary",), vmem_limit_bytes=VMEM_LIMIT)


def _sds(shape, dtype):
    return jax.ShapeDtypeStruct(shape, dtype)


class _Comm:
    def __init__(self, args, out_shape, n_sems, start, finish, aliased=False):
        self.args, self.out_shape, self.n_sems = list(args), list(out_shape), n_sems
        self.start, self.finish, self.aliased = start, finish, aliased


def _pcall(body, *, name, grid, in_specs, out_specs, out_shape, args, scratch_shapes=(), comms=()):
    single = not isinstance(out_shape, (list, tuple))
    out_specs = [out_specs] if single else list(out_specs)
    out_shape = [out_shape] if single else list(out_shape)
    n_in, n_out, n_scr = len(in_specs), len(out_shape), len(scratch_shapes)
    c_args = [a for c in comms for a in c.args]
    c_outs = [o for c in comms for o in c.out_shape]
    aliases, ai, ao = {}, n_in, n_out
    for c in comms:
        if c.aliased:
            aliases.update({ai + k: ao + k for k in range(len(c.args))})
        ai += len(c.args)
        ao += len(c.out_shape)
    sems = [pltpu.SemaphoreType.DMA((c.n_sems,)) for c in comms for _ in range(2)]
    steps = grid[0]

    def wrapped(*refs):
        o0 = n_in + len(c_args)
        s0 = o0 + n_out + len(c_outs)
        m_in, m_out, m_sem = refs[n_in:o0], refs[o0 + n_out:s0], refs[s0 + n_scr:]

        def each(phase):
            ii = oi = 0
            for k, c in enumerate(comms):
                getattr(c, phase)(m_in[ii:ii + len(c.args)], m_out[oi:oi + len(c.out_shape)],
                                  m_sem[2 * k], m_sem[2 * k + 1])
                ii += len(c.args)
                oi += len(c.out_shape)

        if comms:
            @pl.when(pl.program_id(0) == 0)
            def _():
                each("start")

        body(*refs[:n_in], *refs[o0:o0 + n_out], *refs[s0:s0 + n_scr])

        if comms:
            @pl.when(pl.program_id(0) == steps - 1)
            def _():
                each("finish")

    res = pl.pallas_call(
        wrapped, name=name, grid=grid,
        in_specs=list(in_specs) + [ANY] * len(c_args), out_specs=out_specs + [ANY] * len(c_outs),
        out_shape=out_shape + c_outs, scratch_shapes=list(scratch_shapes) + sems,
        input_output_aliases=aliases, compiler_params=_seq_params(),
    )(*args, *c_args)
    mine = res[0] if single else list(res[:n_out])
    if not comms:
        return mine
    theirs, oi = [], n_out
    for c in comms:
        theirs.append(list(res[oi:oi + len(c.out_shape)]))
        oi += len(c.out_shape)
    return mine, theirs


def _inproj_fwd(x, pos, g_pre, w_in, comms=()):
    def body(x_ref, pos_ref, g_ref, w_ref, h_ref, q_ref, k_ref, v_ref, u_ref, vs_ref):
        xh, _ = _rms_stats(x_ref[...])
        h = (xh * g_ref[...]).astype(BF16)
        h_ref[...] = h
        proj = jnp.concatenate([_dot(h, w_ref[s]) for s in range(N_SHARD)], axis=1)
        tabs = _rot_tables(pos_ref[...])
        q_ref[...] = _rope(proj[:, 0:512], tabs).astype(BF16)
        k_ref[...] = _rope(proj[:, 512:1024], tabs).astype(BF16)
        v_ref[...] = proj[:, 1024:1536].astype(BF16)
        u_ref[...] = proj[:, 1536:2048]
        vs_ref[...] = proj[:, 2048:2560]

    return _pcall(
        body, name="inproj_fwd", grid=(SEQ // TM,),
        in_specs=[_rows(D_MODEL), _rows(1), _resident((1, D_MODEL)), _resident((N_SHARD, D_MODEL, IN_S))],
        out_specs=[_rows(D_MODEL), _rows(512), _rows(512), _rows(512), _rows(512), _rows(512)],
        out_shape=[_sds((SEQ, D_MODEL), BF16), _sds((SEQ, 512), BF16), _sds((SEQ, 512), BF16),
                   _sds((SEQ, 512), BF16), _sds((SEQ, 512), F32), _sds((SEQ, 512), F32)],
        args=(x, pos, g_pre, w_in), comms=comms)


def _sgu_fwd(u, vs, lg, lb, w_sp, b_t, comms=()):
    def body(u_ref, vs_ref, lg_ref, lb_ref, w_ref, bt_ref, out_ref):
        wm = _masked_spatial(w_ref)
        bias_full = _dot_exact(bt_ref[...], _group_ones())
        gu, _, _, _, mixed = _sgu_core(u_ref[...], vs_ref[...], lg_ref[...], lb_ref[...], wm, bias_full)
        out_ref[...] = gu * mixed

    return _pcall(
        body, name="sgu_fwd", grid=(SEQ // TM,),
        in_specs=[_rows(SGU_W), _rows(SGU_W), _resident((1, SGU_W)), _resident((1, SGU_W)),
                  _resident((SGU_GROUPS, CHUNK, CHUNK)), _resident((CHUNK, SGU_GROUPS))],
        out_specs=_rows(SGU_W),
        out_shape=_sds((SEQ, SGU_W), F32),
        args=(u, vs, lg, lb, w_sp, b_t), comms=comms)


def _block_masks():
    row = lax.broadcasted_iota(jnp.int32, (CHUNK, CHUNK), 0)
    col = lax.broadcasted_iota(jnp.int32, (CHUNK, CHUNK), 1)
    return col <= row, col >= row


def _attn_fwd(qv, kv, vv, dil, comms=()):
    seg = SEQ // dil
    nblk = seg // CHUNK

    def body(q_ref, k_ref, v_ref, o_ref, l_ref):
        left = _left_half()
        m_cur, m_prev = _block_masks()

        def blk(b, carry):
            r0 = pl.multiple_of(b * CHUNK, CHUNK)
            rp = pl.multiple_of(jnp.maximum(b - 1, 0) * CHUNK, CHUNK)
            prev_ok = m_prev & (b > 0)
            for hp in range(4):
                ls = slice(hp * 128, (hp + 1) * 128)
                qp = q_ref[pl.ds(r0, CHUNK), ls]
                kc = k_ref[pl.ds(r0, CHUNK), ls]
                vc = v_ref[pl.ds(r0, CHUNK), ls]
                if nblk > 1:
                    kp = k_ref[pl.ds(rp, CHUNK), ls]
                    vp = v_ref[pl.ds(rp, CHUNK), ls]
                outs, lses = [], []
                for hm in (left, ~left):
                    qh = jnp.where(hm, qp, jnp.zeros_like(qp))
                    sc = jnp.where(m_cur, _dot_nt(qh, kc) * ATTN_SCALE, NEG)
                    m = jnp.max(sc, axis=-1, keepdims=True)
                    if nblk > 1:
                        sp = jnp.where(prev_ok, _dot_nt(qh, kp) * ATTN_SCALE, NEG)
                        m = jnp.maximum(m, jnp.max(sp, axis=-1, keepdims=True))
                    pc = jnp.exp(sc - m)
                    den = jnp.sum(pc, axis=-1, keepdims=True)
                    acc = _dot(pc.astype(BF16), vc)
                    if nblk > 1:
                        pp = jnp.exp(sp - m)
                        den = den + jnp.sum(pp, axis=-1, keepdims=True)
                        acc = acc + _dot(pp.astype(BF16), vp)
                    outs.append(acc / den)
                    lses.append(m + jnp.log(den))
                o_ref[pl.ds(r0, CHUNK), ls] = jnp.where(left, outs[0], outs[1])
                l_ref[pl.ds(r0, CHUNK), ls] = jnp.where(left, lses[0], lses[1])
            return carry

        lax.fori_loop(0, nblk, blk, 0)

    spec = pl.BlockSpec((seg, ATTN_W), lambda r: (0, r))
    return _pcall(
        body, name=f"attn_fwd_d{dil}", grid=(dil,),
        in_specs=[spec, spec, spec], out_specs=[spec, spec],
        out_shape=[_sds((seg, dil * ATTN_W), F32), _sds((seg, dil * ATTN_W), F32)],
        args=(qv, kv, vv), comms=comms)


def _mix_out_fwd(o_list, l_list, sgu, x, w_out, g_attn, g_sgu, g_post, comms=()):
    def body(o1, o2, o3, l1, l2, l3, sgu_ref, x_ref, w_ref, ga_ref, gs_ref, gp_ref,
             attn_ref, lse_ref, mixed_ref, y_ref, x1_ref):
        ls = [l1[...], l2[...], l3[...]]
        m = jnp.maximum(jnp.maximum(ls[0], ls[1]), ls[2])
        es = [jnp.exp(l - m) for l in ls]
        den = es[0] + es[1] + es[2]
        attn = (es[0] * o1[...] + es[1] * o2[...] + es[2] * o3[...]) / den
        attn_ref[...] = attn
        lse_ref[...] = m + jnp.log(den)
        ah, _ = _rms_stats(attn)
        sh, _ = _rms_stats(sgu_ref[...])
        mixed = jnp.concatenate([ah * ga_ref[...], sh * gs_ref[...]], axis=1).astype(BF16)
        mixed_ref[...] = mixed
        y = _dot(mixed[:, 0:OUT_S], w_ref[0])
        for s in range(1, N_SHARD):
            y = y + _dot(mixed[:, s * OUT_S:(s + 1) * OUT_S], w_ref[s])
        y_ref[...] = y
        yh, _ = _rms_stats(y)
        x1_ref[...] = x_ref[...] + yh * gp_ref[...]

    return _pcall(
        body, name="mix_out_fwd", grid=(SEQ // TM,),
        in_specs=[_rows(512)] * 7 + [_rows(D_MODEL), _resident((N_SHARD, OUT_S, D_MODEL)),
                                    _resident((1, 512)), _resident((1, 512)), _resident((1, D_MODEL))],
        out_specs=[_rows(512), _rows(512), _rows(D_MODEL), _rows(D_MODEL), _rows(D_MODEL)],
        out_shape=[_sds((SEQ, 512), F32), _sds((SEQ, 512), F32), _sds((SEQ, D_MODEL), BF16),
                   _sds((SEQ, D_MODEL), F32), _sds((SEQ, D_MODEL), F32)],
        args=(*o_list, *l_list, sgu, x, w_out, g_attn, g_sgu, g_post), comms=comms)


def _ffn_fwd_bwd(x1, target, w_gate, w_up, w_down, g_pre, g_post, comms=()):
    def body(x1_ref, t_ref, wg_ref, wu_ref, wd_ref, gpf_ref, gpo_ref,
             h2_ref, a_ref, dg_ref, dup_ref, df_ref, dx1_ref, loss_ref, dgpf_ref, dgpo_ref, g_scr, up_scr):
        @pl.when(pl.program_id(0) == 0)
        def _():
            loss_ref[...] = jnp.zeros_like(loss_ref)
            dgpf_ref[...] = jnp.zeros_like(dgpf_ref)
            dgpo_ref[...] = jnp.zeros_like(dgpo_ref)

        x1 = x1_ref[...]
        gpf = gpf_ref[...]
        gpo = gpo_ref[...]
        xh, r = _rms_stats(x1)
        h2 = (xh * gpf).astype(BF16)
        h2_ref[...] = h2
        f = jnp.zeros((TM, D_MODEL), F32)
        for s in range(N_SHARD):
            g = _dot(h2, wg_ref[s])
            up = _dot(h2, wu_ref[s])
            g_scr[s] = g
            up_scr[s] = up
            a = (g * _sigmoid(g) * up).astype(BF16)
            a_ref[s] = a
            f = f + _dot(a, wd_ref[s])
        fh, rf = _rms_stats(f)
        diff = x1 + fh * gpo - t_ref[...]
        loss_ref[...] += jnp.sum(diff * diff, axis=0, keepdims=True)
        dout = diff * np.float32(1.0 / D_MODEL)
        df, dgpo = _rms_bwd(fh, rf, gpo, dout)
        dgpo_ref[...] += dgpo
        dfb = df.astype(BF16)
        df_ref[...] = dfb
        dh2 = jnp.zeros((TM, D_MODEL), F32)
        for s in range(N_SHARD):
            da = _dot_nt(dfb, wd_ref[s])
            g = g_scr[s]
            up = up_scr[s]
            sg = _sigmoid(g)
            dup = (da * (g * sg)).astype(BF16)
            dg = (da * up * (sg * (1.0 + g * (1.0 - sg)))).astype(BF16)
            dg_ref[s] = dg
            dup_ref[s] = dup
            dh2 = dh2 + _dot_nt(dg, wg_ref[s]) + _dot_nt(dup, wu_ref[s])
        dx, dgpf = _rms_bwd(xh, r, gpf, dh2)
        dgpf_ref[...] += dgpf
        dx1_ref[...] = dout + dx

    return _pcall(
        body, name="ffn_fwd_bwd", grid=(SEQ // TM,),
        in_specs=[_rows(D_MODEL), _rows(D_MODEL), _resident((N_SHARD, D_MODEL, FF_S)),
                  _resident((N_SHARD, D_MODEL, FF_S)), _resident((N_SHARD, FF_S, D_MODEL)),
                  _resident((1, D_MODEL)), _resident((1, D_MODEL))],
        out_specs=[_rows(D_MODEL), _rows3(N_SHARD, FF_S), _rows3(N_SHARD, FF_S), _rows3(N_SHARD, FF_S),
                   _rows(D_MODEL), _rows(D_MODEL), _acc(D_MODEL), _acc(D_MODEL), _acc(D_MODEL)],
        out_shape=[_sds((SEQ, D_MODEL), BF16), _sds((N_SHARD, SEQ, FF_S), BF16), _sds((N_SHARD, SEQ, FF_S), BF16),
                   _sds((N_SHARD, SEQ, FF_S), BF16), _sds((SEQ, D_MODEL), BF16), _sds((SEQ, D_MODEL), F32),
                   _sds((1, D_MODEL), F32), _sds((1, D_MODEL), F32), _sds((1, D_MODEL), F32)],
        scratch_shapes=[pltpu.VMEM((N_SHARD, TM, FF_S), F32), pltpu.VMEM((N_SHARD, TM, FF_S), F32)],
        args=(x1, target, w_gate, w_up, w_down, g_pre, g_post), comms=comms)


def _wgrad(a, b, a_spec, b_spec, out_block, name, comms=()):
    def body(a_ref, b_ref, o_ref):
        av = a_ref[0] if len(a_ref.shape) == 3 else a_ref[...]
        bv = b_ref[0] if len(b_ref.shape) == 3 else b_ref[...]
        o_ref[0] = _dot_tn(av, bv)

    return _pcall(
        body, name=name, grid=(N_SHARD,),
        in_specs=[a_spec, b_spec],
        out_specs=pl.BlockSpec((1,) + out_block, lambda s: (s, 0, 0)),
        out_shape=_sds((N_SHARD,) + out_block, F32),
        args=(a, b), comms=comms)


def _outproj_bwd(dx1, y, attn, sgu, w_out, g_post, g_attn, g_sgu, comms=()):
    def body(dx1_ref, y_ref, attn_ref, sgu_ref, w_ref, gp_ref, ga_ref, gs_ref,
             dy_ref, dattn_ref, delta_ref, dsgu_ref, dgp_ref, dga_ref, dgs_ref):
        @pl.when(pl.program_id(0) == 0)
        def _():
            dgp_ref[...] = jnp.zeros_like(dgp_ref)
            dga_ref[...] = jnp.zeros_like(dga_ref)
            dgs_ref[...] = jnp.zeros_like(dgs_ref)

        yh, ry = _rms_stats(y_ref[...])
        dy, dgp = _rms_bwd(yh, ry, gp_ref[...], dx1_ref[...])
        dgp_ref[...] += dgp
        dyb = dy.astype(BF16)
        dy_ref[...] = dyb
        dmixed = jnp.concatenate([_dot_nt(dyb, w_ref[s]) for s in range(N_SHARD)], axis=1)
        attn = attn_ref[...]
        ah, ra = _rms_stats(attn)
        dattn, dga = _rms_bwd(ah, ra, ga_ref[...], dmixed[:, 0:512])
        dga_ref[...] += dga
        sh, rs = _rms_stats(sgu_ref[...])
        dsgu, dgs = _rms_bwd(sh, rs, gs_ref[...], dmixed[:, 512:1024])
        dgs_ref[...] += dgs
        dattn_ref[...] = dattn.astype(BF16)
        dsgu_ref[...] = dsgu
        la = lax.broadcasted_iota(jnp.int32, (ATTN_W, ATTN_W), 0) >> 6
        lb = lax.broadcasted_iota(jnp.int32, (ATTN_W, ATTN_W), 1) >> 6
        delta_ref[...] = _dot_exact(dattn * attn, (la == lb).astype(F32))

    return _pcall(
        body, name="outproj_bwd", grid=(SEQ // TM,),
        in_specs=[_rows(D_MODEL), _rows(D_MODEL), _rows(512), _rows(512), _resident((N_SHARD, OUT_S, D_MODEL)),
                  _resident((1, D_MODEL)), _resident((1, 512)), _resident((1, 512))],
        out_specs=[_rows(D_MODEL), _rows(512), _rows(512), _rows(512), _acc(D_MODEL), _acc(512), _acc(512)],
        out_shape=[_sds((SEQ, D_MODEL), BF16), _sds((SEQ, 512), BF16), _sds((SEQ, 512), F32), _sds((SEQ, 512), F32),
                   _sds((1, D_MODEL), F32), _sds((1, 512), F32), _sds((1, 512), F32)],
        args=(dx1, y, attn, sgu, w_out, g_post, g_attn, g_sgu), comms=comms)


def _sgu_bwd(u, vs, dsgu, lg, lb, w_sp, b_t, comms=()):
    nsteps = SEQ // TM

    def body(u_ref, vs_ref, ds_ref, lg_ref, lb_ref, w_ref, bt_ref,
             du_ref, dvs_ref, dw_ref, db_ref, dlg_ref, dlb_ref, dbias_scr):
        i = pl.program_id(0)

        @pl.when(i == 0)
        def _():
            dw_ref[...] = jnp.zeros_like(dw_ref)
            dlg_ref[...] = jnp.zeros_like(dlg_ref)
            dlb_ref[...] = jnp.zeros_like(dlb_ref)
            dbias_scr[...] = jnp.zeros_like(dbias_scr)

        wm = _masked_spatial(w_ref)
        ones_g = _group_ones()
        bias_full = _dot_exact(bt_ref[...], ones_g)
        u = u_ref[...]
        vs = vs_ref[...]
        lg = lg_ref[...]
        gu, xh, rstd, vnb, mixed = _sgu_core(u, vs, lg, lb_ref[...], wm, bias_full)
        dsgu = ds_ref[...]
        du_ref[...] = (dsgu * mixed * _gelu_grad(u)).astype(BF16)
        dmixed = dsgu * gu
        left = _left_half()
        dvn_rows = []
        for c in range(TM // CHUNK):
            rs = slice(c * CHUNK, (c + 1) * CHUNK)
            dm_c = dmixed[rs, :]
            dbias_scr[...] += dm_c
            pieces = []
            for p in range(4):
                ls = slice(p * 128, (p + 1) * 128)
                dmp = dm_c[:, ls]
                vp = vnb[rs, ls]
                dmb = dmp.astype(BF16)
                zero = jnp.zeros_like(dmb)
                dw_ref[2 * p] += _dot_nt(jnp.where(left, dmb, zero), vp)
                dw_ref[2 * p + 1] += _dot_nt(jnp.where(left, zero, dmb), vp)
                pieces.append(jnp.where(left, _dot_tn(wm[2 * p], dmb), _dot_tn(wm[2 * p + 1], dmb)))
            dvn_rows.append(jnp.concatenate(pieces, axis=1))
        dvn = jnp.concatenate(dvn_rows, axis=0)
        dlg_ref[...] += jnp.sum(dvn * xh, axis=0, keepdims=True)
        dlb_ref[...] += jnp.sum(dvn, axis=0, keepdims=True)
        dxh = dvn * lg
        dgv = rstd * (dxh - jnp.mean(dxh, axis=-1, keepdims=True) - xh * jnp.mean(dxh * xh, axis=-1, keepdims=True))
        dvs_ref[...] = (dgv * _gelu_grad(vs)).astype(BF16)

        @pl.when(i == nsteps - 1)
        def _():
            row = lax.broadcasted_iota(jnp.int32, (CHUNK, CHUNK), 0)
            col = lax.broadcasted_iota(jnp.int32, (CHUNK, CHUNK), 1)
            for g in range(SGU_GROUPS):
                dw_ref[g] = jnp.where(col <= row, dw_ref[g], 0.0)
            db_ref[...] = lax.dot_general(ones_g, dbias_scr[...], (((1,), (1,)), ((), ())),
                                          preferred_element_type=F32, precision=lax.Precision.HIGHEST)

    return _pcall(
        body, name="sgu_bwd", grid=(nsteps,),
        in_specs=[_rows(SGU_W), _rows(SGU_W), _rows(SGU_W), _resident((1, SGU_W)), _resident((1, SGU_W)),
                  _resident((SGU_GROUPS, CHUNK, CHUNK)), _resident((CHUNK, SGU_GROUPS))],
        out_specs=[_rows(SGU_W), _rows(SGU_W), pl.BlockSpec((SGU_GROUPS, CHUNK, CHUNK), lambda i: (0, 0, 0)),
                   _acc(CHUNK, SGU_GROUPS), _acc(SGU_W), _acc(SGU_W)],
        out_shape=[_sds((SEQ, SGU_W), BF16), _sds((SEQ, SGU_W), BF16), _sds((SGU_GROUPS, CHUNK, CHUNK), F32),
                   _sds((SGU_GROUPS, CHUNK), F32), _sds((1, SGU_W), F32), _sds((1, SGU_W), F32)],
        scratch_shapes=[pltpu.VMEM((CHUNK, SGU_W), F32)],
        args=(u, vs, dsgu, lg, lb, w_sp, b_t), comms=comms)


def _attn_bwd(qv, kv, vv, dov, deltav, lsev, dil, comms=()):
    seg = SEQ // dil
    nblk = seg // CHUNK

    def body(q_ref, k_ref, v_ref, do_ref, dl_ref, lse_ref, dq_ref, dk_ref, dv_ref):
        left = _left_half()
        m_cur, m_prev = _block_masks()

        def blk(b, carry):
            r0 = pl.multiple_of(b * CHUNK, CHUNK)
            rp = pl.multiple_of(jnp.maximum(b - 1, 0) * CHUNK, CHUNK)
            prev_ok = m_prev & (b > 0)
            for hp in range(4):
                ls = slice(hp * 128, (hp + 1) * 128)
                qp = q_ref[pl.ds(r0, CHUNK), ls]
                kc = k_ref[pl.ds(r0, CHUNK), ls]
                vc = v_ref[pl.ds(r0, CHUNK), ls]
                dop = do_ref[pl.ds(r0, CHUNK), ls]
                lsep = lse_ref[pl.ds(r0, CHUNK), ls]
                dlp = dl_ref[pl.ds(r0, CHUNK), ls]
                if nblk > 1:
                    kp = k_ref[pl.ds(rp, CHUNK), ls]
                    vp = v_ref[pl.ds(rp, CHUNK), ls]
                dqs = []
                dkc = jnp.zeros((CHUNK, 128), F32)
                dvc = jnp.zeros((CHUNK, 128), F32)
                dkp = jnp.zeros((CHUNK, 128), F32)
                dvp = jnp.zeros((CHUNK, 128), F32)
                for side, hm in enumerate((left, ~left)):
                    c0 = side * HEAD_DIM
                    qh = jnp.where(hm, qp, jnp.zeros_like(qp))
                    doh = jnp.where(hm, dop, jnp.zeros_like(dop))
                    lse_h = lsep[:, c0:c0 + 1]
                    dl_h = dlp[:, c0:c0 + 1]
                    sc = _dot_nt(qh, kc) * ATTN_SCALE
                    pc = jnp.exp(jnp.where(m_cur, sc - lse_h, NEG))
                    dsc = (pc * (_dot_nt(doh, vc) - dl_h) * ATTN_SCALE).astype(BF16)
                    dq = _dot(dsc, kc)
                    dkc = dkc + _dot_tn(dsc, qh)
                    dvc = dvc + _dot_tn(pc.astype(BF16), doh)
                    if nblk > 1:
                        sp = _dot_nt(qh, kp) * ATTN_SCALE
                        pp = jnp.exp(jnp.where(prev_ok, sp - lse_h, NEG))
                        dsp = (pp * (_dot_nt(doh, vp) - dl_h) * ATTN_SCALE).astype(BF16)
                        dq = dq + _dot(dsp, kp)
                        dkp = dkp + _dot_tn(dsp, qh)
                        dvp = dvp + _dot_tn(pp.astype(BF16), doh)
                    dqs.append(dq)
                dq_ref[pl.ds(r0, CHUNK), ls] = jnp.where(left, dqs[0], dqs[1])
                dk_ref[pl.ds(r0, CHUNK), ls] = dkc
                dv_ref[pl.ds(r0, CHUNK), ls] = dvc
                if nblk > 1:
                    @pl.when(b > 0)
                    def _():
                        dk_ref[pl.ds(rp, CHUNK), ls] += dkp
                        dv_ref[pl.ds(rp, CHUNK), ls] += dvp
            return carry

        lax.fori_loop(0, nblk, blk, 0)

    spec = pl.BlockSpec((seg, ATTN_W), lambda r: (0, r))
    return _pcall(
        body, name=f"attn_bwd_d{dil}", grid=(dil,),
        in_specs=[spec] * 6, out_specs=[spec] * 3,
        out_shape=[_sds((seg, dil * ATTN_W), F32)] * 3,
        args=(qv, kv, vv, dov, deltav, lsev), comms=comms)


def _inproj_bwd(dqs, dks, dvs, du, dvs_sgu, pos, x, dx1, w_in, g_pre, comms=()):
    def body(dq1, dq2, dq3, dk1, dk2, dk3, dv1, dv2, dv3, du_ref, dvs_ref, pos_ref, x_ref, dx1_ref, w_ref, g_ref,
             dproj_ref, gx_ref, dg_ref):
        @pl.when(pl.program_id(0) == 0)
        def _():
            dg_ref[...] = jnp.zeros_like(dg_ref)

        tabs = _rot_tables(pos_ref[...])
        dproj_ref[:, 0:512] = _rope_bwd(dq1[...] + dq2[...] + dq3[...], tabs).astype(BF16)
        dproj_ref[:, 512:1024] = _rope_bwd(dk1[...] + dk2[...] + dk3[...], tabs).astype(BF16)
        dproj_ref[:, 1024:1536] = (dv1[...] + dv2[...] + dv3[...]).astype(BF16)
        dproj_ref[:, 1536:2048] = du_ref[...]
        dproj_ref[:, 2048:2560] = dvs_ref[...]
        dh = jnp.zeros((TM, D_MODEL), F32)
        for s in range(N_SHARD):
            dh = dh + _dot_nt(dproj_ref[:, s * IN_S:(s + 1) * IN_S], w_ref[s])
        g = g_ref[...]
        xh, r = _rms_stats(x_ref[...])
        dx, dg = _rms_bwd(xh, r, g, dh)
        dg_ref[...] += dg
        gx_ref[...] = dx1_ref[...] + dx

    return _pcall(
        body, name="inproj_bwd", grid=(SEQ // TM,),
        in_specs=[_rows(512)] * 11 + [_rows(1), _rows(D_MODEL), _rows(D_MODEL),
                                     _resident((N_SHARD, D_MODEL, IN_S)), _resident((1, D_MODEL))],
        out_specs=[_rows(PROJ_W), _rows(D_MODEL), _acc(D_MODEL)],
        out_shape=[_sds((SEQ, PROJ_W), BF16), _sds((SEQ, D_MODEL), F32), _sds((1, D_MODEL), F32)],
        args=(*dqs, *dks, *dvs, du, dvs_sgu, pos, x, dx1, w_in, g_pre), comms=comms)


def _to_view(a, dil):
    return a if dil == 1 else a.reshape(SEQ // dil, dil * a.shape[1])


def _from_view(a, dil):
    return a if dil == 1 else a.reshape(SEQ, a.shape[1] // dil)


def _local_step(x, pos, target, w_in, w_out, w_gate, w_up, w_down, small):
    b_t = small["sgu_b_spatial"].T
    h, q, k, v, u, vs = _inproj_fwd(x, pos, small["pre_mix_norm"], w_in)
    sgu = _sgu_fwd(u, vs, small["sgu_ln_gain"], small["sgu_ln_bias"], small["sgu_w_spatial"], b_t)
    views = [tuple(_to_view(t, dil) for t in (q, k, v)) for dil in DILATIONS]
    o_list, l_list = [], []
    for dil, (qv, kv, vv) in zip(DILATIONS, views):
        o, l = _attn_fwd(qv, kv, vv, dil)
        o_list.append(_from_view(o, dil))
        l_list.append(_from_view(l, dil))
    attn, lse, mixed, y, x1 = _mix_out_fwd(o_list, l_list, sgu, x, w_out, small["attn_out_norm"],
                                           small["sgu_out_norm"], small["post_mix_norm"])
    h2, a, dg, dup, df, dx1, loss_cols, d_pre_ffn, d_post_ffn = _ffn_fwd_bwd(
        x1, target, w_gate, w_up, w_down, small["pre_ffn_norm"], small["post_ffn_norm"])

    full_tok = pl.BlockSpec((SEQ, D_MODEL), lambda s: (0, 0), pipeline_mode=pl.Buffered(1))
    ff_tok = pl.BlockSpec((1, SEQ, FF_S), lambda s: (s, 0, 0))
    gw_gate = _wgrad(h2, dg, full_tok, ff_tok, (D_MODEL, FF_S), "wgrad_gate")
    gw_up = _wgrad(h2, dup, full_tok, ff_tok, (D_MODEL, FF_S), "wgrad_up")
    gw_down = _wgrad(a, df, ff_tok, full_tok, (FF_S, D_MODEL), "wgrad_down")

    dy, dattn, delta, dsgu, d_post_mix, d_attn_norm, d_sgu_norm = _outproj_bwd(
        dx1, y, attn, sgu, w_out, small["post_mix_norm"], small["attn_out_norm"], small["sgu_out_norm"])
    gw_out = _wgrad(mixed, dy, pl.BlockSpec((SEQ, OUT_S), lambda s: (0, s)), full_tok, (OUT_S, D_MODEL), "wgrad_out")
    du, dvs_sgu, d_w_sp, d_b_sp, d_ln_gain, d_ln_bias = _sgu_bwd(
        u, vs, dsgu, small["sgu_ln_gain"], small["sgu_ln_bias"], small["sgu_w_spatial"], b_t)

    dqs, dks, dvs = [], [], []
    for dil, (qv, kv, vv) in zip(DILATIONS, views):
        dq, dk, dv = _attn_bwd(qv, kv, vv, _to_view(dattn, dil), _to_view(delta, dil), _to_view(lse, dil), dil)
        dqs.append(_from_view(dq, dil))
        dks.append(_from_view(dk, dil))
        dvs.append(_from_view(dv, dil))
    dproj, grad_x, d_pre_mix = _inproj_bwd(dqs, dks, dvs, du, dvs_sgu, pos, x, dx1, w_in, small["pre_mix_norm"])
    gw_in = _wgrad(h, dproj, full_tok, pl.BlockSpec((SEQ, IN_S), lambda s: (0, s)), (D_MODEL, IN_S), "wgrad_in")

    small_grads = {
        "pre_mix_norm": d_pre_mix, "sgu_ln_gain": d_ln_gain, "sgu_ln_bias": d_ln_bias, "sgu_w_spatial": d_w_sp,
        "sgu_b_spatial": d_b_sp, "attn_out_norm": d_attn_norm, "sgu_out_norm": d_sgu_norm,
        "post_mix_norm": d_post_mix, "pre_ffn_norm": d_pre_ffn, "post_ffn_norm": d_post_ffn,
    }
    return loss_cols, grad_x, (gw_in, gw_out, gw_gate, gw_up, gw_down), small_grads


def _coords():
    return lax.axis_index("x"), lax.axis_index("y"), lax.axis_index("c")


def _other_chips(x, y):
    return [(1 - x, y), (x, 1 - y), (1 - x, 1 - y)]


def _comm_call(body, name, n_in, out_shape, scratch_shapes):
    return pl.pallas_call(
        body, name=name, in_specs=[ANY] * n_in, out_specs=[ANY] * len(out_shape), out_shape=out_shape,
        scratch_shapes=scratch_shapes,
        compiler_params=pltpu.CompilerParams(has_side_effects=True),
    )


def _gather_weights(shards):
    n = len(shards)
    halves = [s.reshape(2, s.shape[0] // 2, s.shape[1]) for s in shards]

    def body(*refs):
        ins, outs = refs[:n], refs[n:2 * n]
        send_sems, recv_sems = refs[2 * n:]
        x, y, c = _coords()
        s_me = 2 * x + y
        chips = _other_chips(x, y)
        sibling = (x, y, 1 - c)

        def copy(k, w, shard, cc, to):
            src = ins[w].at[cc] if shard is None else outs[w].at[shard, cc]
            dst = outs[w].at[s_me if shard is None else shard, cc]
            return pltpu.make_async_remote_copy(src_ref=src, dst_ref=dst, send_sem=send_sems.at[k],
                                                recv_sem=recv_sems.at[k], device_id=to, device_id_type=MESH)

        first = [copy(j * n + w, w, None, c, (cx, cy, c)) for j, (cx, cy) in enumerate(chips) for w in range(n)]
        for cp in first:
            cp.start()
        passed = []
        for j, (cx, cy) in enumerate(chips):
            for w in range(n):
                copy(j * n + w, w, 2 * cx + cy, c, (x, y, c)).wait_recv()
                fw = copy((3 + j) * n + w, w, 2 * cx + cy, c, sibling)
                fw.start()
                passed.append(fw)
        for j, (cx, cy) in enumerate(chips):
            for w in range(n):
                copy((3 + j) * n + w, w, 2 * cx + cy, 1 - c, (x, y, c)).wait_recv()
        for cp in first + passed:
            cp.wait_send()

    out_shape = [_sds((N_SHARD,) + h.shape, h.dtype) for h in halves]
    scratch = [pltpu.SemaphoreType.DMA((6 * n,)), pltpu.SemaphoreType.DMA((6 * n,))]
    full = _comm_call(body, "comm_gather_weights", n, out_shape, scratch)(*halves)
    s_me = 2 * lax.axis_index("x") + lax.axis_index("y")
    full = [lax.dynamic_update_slice(f, h[None], (s_me, 0, 0, 0)) for f, h in zip(full, halves)]
    return [f.reshape((N_SHARD,) + s.shape) for f, s in zip(full, shards)]


def _rs_to_sibling(gws):
    n = len(gws)

    def body(*refs):
        ins, outs = refs[:n], refs[n:2 * n]
        send_sems, recv_sems = refs[2 * n:]
        x, y, c = _coords()
        copies = []
        for w in range(n):
            hw = gws[w].shape[1] // 2
            copies.append(pltpu.make_async_remote_copy(
                src_ref=ins[w].at[:, pl.ds((1 - c) * hw, hw), :], dst_ref=outs[w], send_sem=send_sems.at[w],
                recv_sem=recv_sems.at[w], device_id=(x, y, 1 - c), device_id_type=MESH))
        for cp in copies:
            cp.start()
        for cp in copies:
            cp.wait()

    out_shape = [_sds((N_SHARD, g.shape[1] // 2, g.shape[2]), g.dtype) for g in gws]
    scratch = [pltpu.SemaphoreType.DMA((n,)), pltpu.SemaphoreType.DMA((n,))]
    return _comm_call(body, "comm_rs_sibling", n, out_shape, scratch)(*gws)


def _rs_chip_sum(gw, recv, core):
    _, rows, cols = gw.shape
    hw = rows // 2

    def body(c_ref, g_ref, r_ref, o_ref):
        o_ref[...] = (g_ref[...] + r_ref[...]).astype(BF16)

    return pl.pallas_call(
        body, name="rs_chip_sum",
        grid_spec=pltpu.PrefetchScalarGridSpec(
            num_scalar_prefetch=1, grid=(N_SHARD,),
            in_specs=[pl.BlockSpec((1, hw, cols), lambda s, c_ref: (s, c_ref[0], 0)),
                      pl.BlockSpec((1, hw, cols), lambda s, c_ref: (s, 0, 0))],
            out_specs=pl.BlockSpec((1, hw, cols), lambda s, c_ref: (s, 0, 0))),
        out_shape=_sds((N_SHARD, hw, cols), BF16),
        compiler_params=_seq_params(),
    )(core, gw, recv)


def _rs_between_chips(pbs):
    n = len(pbs)

    def body(*refs):
        ins, outs = refs[:n], refs[n:2 * n]
        send_sems, recv_sems = refs[2 * n:]
        x, y, c = _coords()
        copies = []
        for j, (cx, cy) in enumerate(_other_chips(x, y)):
            for w in range(n):
                copies.append(pltpu.make_async_remote_copy(
                    src_ref=ins[w].at[2 * cx + cy], dst_ref=outs[w].at[j], send_sem=send_sems.at[j * n + w],
                    recv_sem=recv_sems.at[j * n + w], device_id=(cx, cy, c), device_id_type=MESH))
        for cp in copies:
            cp.start()
        for cp in copies:
            cp.wait()

    out_shape = [_sds((3,) + p.shape[1:], p.dtype) for p in pbs]
    scratch = [pltpu.SemaphoreType.DMA((3 * n,)), pltpu.SemaphoreType.DMA((3 * n,))]
    return _comm_call(body, "comm_rs_chips", n, out_shape, scratch)(*pbs)


def _rs_final_sum(gw, recv_sib, recv_chips, shard_core):
    _, rows, cols = gw.shape
    hw = rows // 2

    def body(sc_ref, g_ref, r_ref, rc_ref, o_ref):
        acc = g_ref[0] + r_ref[0]
        for j in range(3):
            acc = acc + rc_ref[j].astype(F32)
        o_ref[0] = acc

    return pl.pallas_call(
        body, name="rs_final_sum",
        grid_spec=pltpu.PrefetchScalarGridSpec(
            num_scalar_prefetch=1, grid=(1,),
            in_specs=[pl.BlockSpec((1, hw, cols), lambda i, sc: (sc[0], sc[1], 0)),
                      pl.BlockSpec((1, hw, cols), lambda i, sc: (sc[0], 0, 0)),
                      pl.BlockSpec((3, hw, cols), lambda i, sc: (0, 0, 0))],
            out_specs=pl.BlockSpec((1, hw, cols), lambda i, sc: (sc[1], 0, 0))),
        out_shape=_sds((2, hw, cols), F32),
        compiler_params=_seq_params(),
    )(shard_core, gw, recv_sib, recv_chips)


def _rs_join_halves(halves):
    n = len(halves)

    def body(*refs):
        bufs = refs[n:2 * n]
        send_sems, recv_sems = refs[2 * n:]
        x, y, c = _coords()
        remote = [pltpu.make_async_remote_copy(
            src_ref=bufs[w].at[c], dst_ref=bufs[w].at[c], send_sem=send_sems.at[w], recv_sem=recv_sems.at[w],
            device_id=(x, y, 1 - c), device_id_type=MESH) for w in range(n)]
        for cp in remote:
            cp.start()
        for w in range(n):
            remote[w].wait_send()
            pltpu.make_async_remote_copy(
                src_ref=bufs[w].at[c], dst_ref=bufs[w].at[1 - c], send_sem=send_sems.at[w],
                recv_sem=recv_sems.at[w], device_id=(x, y, c), device_id_type=MESH).wait_recv()

    joined = pl.pallas_call(
        body, name="comm_rs_join", in_specs=[ANY] * n, out_specs=[ANY] * n,
        out_shape=[_sds(h.shape, h.dtype) for h in halves], input_output_aliases={w: w for w in range(n)},
        scratch_shapes=[pltpu.SemaphoreType.DMA((n,)), pltpu.SemaphoreType.DMA((n,))],
        compiler_params=pltpu.CompilerParams(has_side_effects=True),
    )(*halves)
    return [j.reshape(2 * h.shape[1], h.shape[2]) for j, h in zip(joined, halves)]


def _allreduce_small(buf):
    rows, cols = buf.shape

    def body(in_ref, out_ref, slots, send_sems, recv_sems):
        x, y, c = _coords()
        me = 4 * x + 2 * y + c
        copies, peers = [], []
        for k in range(1, 8):
            px = 1 - x if (k >> 2) & 1 else x
            py = 1 - y if (k >> 1) & 1 else y
            pc = 1 - c if k & 1 else c
            peers.append(4 * px + 2 * py + pc)
            copies.append(pltpu.make_async_remote_copy(
                src_ref=in_ref, dst_ref=slots.at[me], send_sem=send_sems.at[k - 1], recv_sem=recv_sems.at[k - 1],
                device_id=(px, py, pc), device_id_type=MESH))
        for cp in copies:
            cp.start()
        slots[me] = in_ref[...]
        for k in range(7):
            pltpu.make_async_remote_copy(
                src_ref=in_ref, dst_ref=slots.at[peers[k]], send_sem=send_sems.at[k], recv_sem=recv_sems.at[k],
                device_id=(x, y, c), device_id_type=MESH).wait_recv()
        for cp in copies:
            cp.wait_send()
        acc = slots[0]
        for i in range(1, 8):
            acc = acc + slots[i]
        out_ref[...] = acc

    vmem = pl.BlockSpec(memory_space=pltpu.VMEM)
    return pl.pallas_call(
        body, name="comm_allreduce_small", in_specs=[vmem], out_specs=vmem, out_shape=_sds((rows, cols), F32),
        scratch_shapes=[pltpu.VMEM((8, rows, cols), F32), pltpu.SemaphoreType.DMA((7,)), pltpu.SemaphoreType.DMA((7,))],
        compiler_params=pltpu.CompilerParams(has_side_effects=True, vmem_limit_bytes=VMEM_LIMIT),
    )(buf)


def _adamw(w, g, m, v, block_rows, name):
    rows, cols = w.shape

    def body(w_ref, g_ref, m_ref, v_ref, d_ref, nm_ref, nv_ref):
        g = g_ref[...]
        m = ADAM_B1 * m_ref[...] + (1.0 - ADAM_B1) * g
        v = ADAM_B2 * v_ref[...] + (1.0 - ADAM_B2) * (g * g)
        m_hat = m / (1.0 - ADAM_B1 ** ADAM_STEP)
        v_hat = v / (1.0 - ADAM_B2 ** ADAM_STEP)
        d_ref[...] = -ADAM_LR * (m_hat / (jnp.sqrt(v_hat) + ADAM_EPS) + ADAM_WD * w_ref[...])
        nm_ref[...] = m
        nv_ref[...] = v

    spec = pl.BlockSpec((block_rows, cols), lambda i: (i, 0))
    return pl.pallas_call(
        body, name=name, grid=(rows // block_rows,), in_specs=[spec] * 4, out_specs=[spec] * 3,
        out_shape=[_sds((rows, cols), F32)] * 3,
        compiler_params=_seq_params(),
    )(w, g, m, v)


WEIGHTS = ("pre_mix_norm", "w_in", "sgu_ln_gain", "sgu_ln_bias", "sgu_w_spatial", "sgu_b_spatial", "attn_out_norm",
           "sgu_out_norm", "w_out", "post_mix_norm", "pre_ffn_norm", "w_gate", "w_up", "w_down", "post_ffn_norm")
BIG = ("w_in", "w_out", "w_gate", "w_up", "w_down")
BIG_ADAM_ROWS = {"w_in": 256, "w_out": 128, "w_gate": 256, "w_up": 256, "w_down": 352}
SMALL = ("pre_mix_norm", "post_mix_norm", "pre_ffn_norm", "post_ffn_norm", "sgu_ln_gain", "sgu_ln_bias",
         "attn_out_norm", "sgu_out_norm", "sgu_w_spatial", "sgu_b_spatial")


def _pack_small(d):
    flat = [d[n].reshape(-1) for n in SMALL]
    used = sum(f.shape[0] for f in flat)
    flat.append(jnp.zeros((SMALL_ROWS * 1024 - used,), F32))
    return jnp.concatenate(flat).reshape(SMALL_ROWS, 1024)


def _unpack_small(buf, shapes):
    flat = buf.reshape(-1)
    out, off = {}, 0
    for n in SMALL:
        size = int(np.prod(shapes[n]))
        out[n] = flat[off:off + size].reshape(shapes[n])
        off += size
    return out


def _kernel_unoverlapped(x, positions, pre_mix_norm, w_in, sgu_ln_gain, sgu_ln_bias, sgu_w_spatial, sgu_b_spatial, attn_out_norm, sgu_out_norm, w_out, post_mix_norm, pre_ffn_norm, w_gate, w_up, w_down, post_ffn_norm, loss_target, m_pre_mix_norm, m_w_in, m_sgu_ln_gain, m_sgu_ln_bias, m_sgu_w_spatial, m_sgu_b_spatial, m_attn_out_norm, m_sgu_out_norm, m_w_out, m_post_mix_norm, m_pre_ffn_norm, m_w_gate, m_w_up, m_w_down, m_post_ffn_norm, v_pre_mix_norm, v_w_in, v_sgu_ln_gain, v_sgu_ln_bias, v_sgu_w_spatial, v_sgu_b_spatial, v_attn_out_norm, v_sgu_out_norm, v_w_out, v_post_mix_norm, v_pre_ffn_norm, v_w_gate, v_w_up, v_w_down, v_post_ffn_norm):
    a = dict(locals())
    cx, cy, cc = _coords()
    core = jnp.stack([cc]).astype(jnp.int32)
    shard_core = jnp.stack([2 * cx + cy, cc]).astype(jnp.int32)

    full = _gather_weights([a[n][0].astype(BF16) for n in BIG])
    small = {n: (a[n][0] if a[n].ndim > 2 else a[n]) for n in SMALL}
    loss_cols, grad_x, gws, small_grads = _local_step(
        x[0], positions.reshape(SEQ, 1), loss_target[0], *full, small)
    loss = lax.psum(jnp.sum(loss_cols) * np.float32(0.5 / D_MODEL), ("x", "y", "c"))

    recv_sib = _rs_to_sibling(list(gws))
    chip_part = [_rs_chip_sum(g, r, core) for g, r in zip(gws, recv_sib)]
    recv_chips = _rs_between_chips(chip_part)
    halves = [_rs_final_sum(g, r, rc, shard_core) for g, r, rc in zip(gws, recv_sib, recv_chips)]
    big_grads = dict(zip(BIG, _rs_join_halves(halves)))

    shapes = {n: a[n].shape for n in SMALL}
    small_sum = _allreduce_small(_pack_small(small_grads))

    grads, deltas, new_m, new_v = {}, {}, {}, {}
    for n in BIG:
        grads[n] = big_grads[n][None]
        d, nm, nv = _adamw(a[n][0], big_grads[n], a["m_" + n][0], a["v_" + n][0], BIG_ADAM_ROWS[n], "adamw_" + n)
        deltas[n], new_m[n], new_v[n] = d[None], nm[None], nv[None]
    d, nm, nv = _adamw(_pack_small({n: a[n] for n in SMALL}), small_sum, _pack_small({n: a["m_" + n] for n in SMALL}),
                       _pack_small({n: a["v_" + n] for n in SMALL}), SMALL_ROWS, "adamw_small")
    grads.update(_unpack_small(small_sum, shapes))
    deltas.update(_unpack_small(d, shapes))
    new_m.update(_unpack_small(nm, shapes))
    new_v.update(_unpack_small(nv, shapes))
    return (loss, grad_x[None], *[grads[n] for n in WEIGHTS], *[deltas[n] for n in WEIGHTS],
            *[new_m[n] for n in WEIGHTS], *[new_v[n] for n in WEIGHTS])


def _remote(src, dst, send_sem, recv_sem, to):
    return pltpu.make_async_remote_copy(src_ref=src, dst_ref=dst, send_sem=send_sem, recv_sem=recv_sem,
                                        device_id=to, device_id_type=MESH)


def _halves(a):
    *lead, rows, cols = a.shape
    return a.reshape(*lead, 2, rows // 2, cols)


def _gather_ici(shards):
    n = len(shards)

    def desc(ins, outs, ss, rs, j, w, landed):
        x, y, c = _coords()
        cx, cy = _other_chips(x, y)[j]
        shard = 2 * cx + cy if landed else 2 * x + y
        return _remote(ins[w].at[c], outs[w].at[shard, c], ss.at[j * n + w], rs.at[j * n + w], (cx, cy, c))

    def start(ins, outs, ss, rs):
        for j in range(3):
            for w in range(n):
                desc(ins, outs, ss, rs, j, w, False).start()

    def finish(ins, outs, ss, rs):
        for j in range(3):
            for w in range(n):
                desc(ins, outs, ss, rs, j, w, True).wait_recv()
                desc(ins, outs, ss, rs, j, w, False).wait_send()

    return _Comm(shards, [_sds((N_SHARD,) + s.shape, s.dtype) for s in shards], 3 * n, start, finish)


def _gather_pass(fulls):
    n = len(fulls)

    def desc(bufs, ss, rs, j, w, landed):
        x, y, c = _coords()
        cx, cy = _other_chips(x, y)[j]
        shard = 2 * cx + cy
        return _remote(bufs[w].at[shard, c], bufs[w].at[shard, 1 - c if landed else c],
                       ss.at[j * n + w], rs.at[j * n + w], (x, y, 1 - c))

    def start(ins, outs, ss, rs):
        for j in range(3):
            for w in range(n):
                desc(outs, ss, rs, j, w, False).start()

    def finish(ins, outs, ss, rs):
        for j in range(3):
            for w in range(n):
                desc(outs, ss, rs, j, w, True).wait_recv()
                desc(outs, ss, rs, j, w, False).wait_send()

    return _Comm(fulls, [_sds(f.shape, f.dtype) for f in fulls], 3 * n, start, finish, aliased=True)


def _rs_sibling(gws):
    n = len(gws)

    def desc(ins, outs, ss, rs, w):
        x, y, c = _coords()
        return _remote(ins[w].at[:, 1 - c], outs[w], ss.at[w], rs.at[w], (x, y, 1 - c))

    def start(ins, outs, ss, rs):
        for w in range(n):
            desc(ins, outs, ss, rs, w).start()

    def finish(ins, outs, ss, rs):
        for w in range(n):
            desc(ins, outs, ss, rs, w).wait()

    out_shape = [_sds((N_SHARD, g.shape[1] // 2, g.shape[2]), g.dtype) for g in gws]
    return _Comm([_halves(g) for g in gws], out_shape, n, start, finish)


def _rs_chips(pbs):
    n = len(pbs)

    def desc(ins, outs, ss, rs, j, w):
        x, y, c = _coords()
        cx, cy = _other_chips(x, y)[j]
        return _remote(ins[w].at[2 * cx + cy], outs[w].at[j], ss.at[j * n + w], rs.at[j * n + w], (cx, cy, c))

    def start(ins, outs, ss, rs):
        for j in range(3):
            for w in range(n):
                desc(ins, outs, ss, rs, j, w).start()

    def finish(ins, outs, ss, rs):
        for j in range(3):
            for w in range(n):
                desc(ins, outs, ss, rs, j, w).wait()

    return _Comm(pbs, [_sds((3,) + p.shape[1:], p.dtype) for p in pbs], 3 * n, start, finish)


def _rs_join(halves):
    n = len(halves)

    def desc(bufs, ss, rs, w, landed):
        x, y, c = _coords()
        return _remote(bufs[w].at[c], bufs[w].at[1 - c if landed else c], ss.at[w], rs.at[w], (x, y, 1 - c))

    def start(ins, outs, ss, rs):
        for w in range(n):
            desc(outs, ss, rs, w, False).start()

    def finish(ins, outs, ss, rs):
        for w in range(n):
            desc(outs, ss, rs, w, True).wait_recv()
            desc(outs, ss, rs, w, False).wait_send()

    return _Comm(halves, [_sds(h.shape, h.dtype) for h in halves], n, start, finish, aliased=True)


def _small_exchange(buf):
    def desc(ins, outs, ss, rs, k, landed):
        x, y, c = _coords()
        px = 1 - x if (k >> 2) & 1 else x
        py = 1 - y if (k >> 1) & 1 else y
        pc = 1 - c if k & 1 else c
        slot = 4 * px + 2 * py + pc if landed else 4 * x + 2 * y + c
        return _remote(ins[0], outs[0].at[slot], ss.at[k - 1], rs.at[k - 1], (px, py, pc))

    def start(ins, outs, ss, rs):
        for k in range(1, 8):
            desc(ins, outs, ss, rs, k, False).start()

    def finish(ins, outs, ss, rs):
        for k in range(1, 8):
            desc(ins, outs, ss, rs, k, True).wait_recv()
            desc(ins, outs, ss, rs, k, False).wait_send()

    return _Comm([buf], [_sds((8,) + buf.shape, buf.dtype)], 7, start, finish)


def _comm_only(name, comms):
    return _pcall(lambda: None, name=name, grid=(1,), in_specs=[], out_specs=[], out_shape=[], args=(),
                  comms=comms)[1]


def _adam_math(w, g, m, v):
    m = ADAM_B1 * m + (1.0 - ADAM_B1) * g
    v = ADAM_B2 * v + (1.0 - ADAM_B2) * (g * g)
    m_hat = m / (1.0 - ADAM_B1 ** ADAM_STEP)
    v_hat = v / (1.0 - ADAM_B2 ** ADAM_STEP)
    return -ADAM_LR * (m_hat / (jnp.sqrt(v_hat) + ADAM_EPS) + ADAM_WD * w), m, v


def _adamw_small(own, slots, w, m, v, me):
    rows, cols = own.shape

    def body(me_ref, own_ref, slots_ref, w_ref, m_ref, v_ref, g_ref, d_ref, nm_ref, nv_ref):
        own_v = own_ref[...]
        g = jnp.where(me_ref[0] == 0, own_v, slots_ref[0])
        for i in range(1, 8):
            g = g + jnp.where(me_ref[0] == i, own_v, slots_ref[i])
        g_ref[...] = g
        d_ref[...], nm_ref[...], nv_ref[...] = _adam_math(w_ref[...], g, m_ref[...], v_ref[...])

    flat = pl.BlockSpec((rows, cols), lambda i, me_ref: (0, 0))
    return pl.pallas_call(
        body, name="adamw_small",
        grid_spec=pltpu.PrefetchScalarGridSpec(
            num_scalar_prefetch=1, grid=(1,),
            in_specs=[flat, pl.BlockSpec((8, rows, cols), lambda i, me_ref: (0, 0, 0)), flat, flat, flat],
            out_specs=[flat] * 4),
        out_shape=[_sds((rows, cols), F32)] * 4,
        compiler_params=_seq_params(),
    )(me, own, slots, w, m, v)


def kernel(x, positions, pre_mix_norm, w_in, sgu_ln_gain, sgu_ln_bias, sgu_w_spatial, sgu_b_spatial, attn_out_norm, sgu_out_norm, w_out, post_mix_norm, pre_ffn_norm, w_gate, w_up, w_down, post_ffn_norm, loss_target, m_pre_mix_norm, m_w_in, m_sgu_ln_gain, m_sgu_ln_bias, m_sgu_w_spatial, m_sgu_b_spatial, m_attn_out_norm, m_sgu_out_norm, m_w_out, m_post_mix_norm, m_pre_ffn_norm, m_w_gate, m_w_up, m_w_down, m_post_ffn_norm, v_pre_mix_norm, v_w_in, v_sgu_ln_gain, v_sgu_ln_bias, v_sgu_w_spatial, v_sgu_b_spatial, v_attn_out_norm, v_sgu_out_norm, v_w_out, v_post_mix_norm, v_pre_ffn_norm, v_w_gate, v_w_up, v_w_down, v_post_ffn_norm):
    a = dict(locals())
    cx, cy, cc = _coords()
    s_me = 2 * cx + cy
    core = jnp.stack([cc]).astype(jnp.int32)
    shard_core = jnp.stack([s_me, cc]).astype(jnp.int32)
    me = jnp.stack([4 * cx + 2 * cy + cc]).astype(jnp.int32)
    small = {n: (a[n][0] if a[n].ndim > 2 else a[n]) for n in SMALL}
    b_t = small["sgu_b_spatial"].T
    xs, pos, target = x[0], positions.reshape(SEQ, 1), loss_target[0]
    own = {n: _halves(a[n][0].astype(BF16)) for n in BIG}

    def with_own(full, n):
        full = lax.dynamic_update_slice(full, own[n][None], (s_me, 0, 0, 0))
        return full.reshape((N_SHARD,) + a[n].shape[1:])

    (w_in_f,) = _gather_weights([a["w_in"][0].astype(BF16)])
    (h, q, k, v, u, vs), ((l_out,),) = _inproj_fwd(xs, pos, small["pre_mix_norm"], w_in_f,
                                                   comms=[_gather_ici([own["w_out"]])])
    views = [tuple(_to_view(t, dil) for t in (q, k, v)) for dil in DILATIONS]
    rider = {1: "w_gate", 4: "w_up", 16: "w_down"}
    landed, o_list, l_list = {}, [], []
    for dil, (qv, kv, vv) in zip(DILATIONS, views):
        (o, l), ((landed[rider[dil]],),) = _attn_fwd(qv, kv, vv, dil, comms=[_gather_ici([own[rider[dil]]])])
        o_list.append(_from_view(o, dil))
        l_list.append(_from_view(l, dil))
    sgu, ((l_out,),) = _sgu_fwd(u, vs, small["sgu_ln_gain"], small["sgu_ln_bias"], small["sgu_w_spatial"], b_t,
                                comms=[_gather_pass([l_out])])
    w_out_f = with_own(l_out, "w_out")
    (attn, lse, mixed, y, x1), (ffn_w,) = _mix_out_fwd(
        o_list, l_list, sgu, xs, w_out_f, small["attn_out_norm"], small["sgu_out_norm"],
        small["post_mix_norm"], comms=[_gather_pass([landed["w_gate"], landed["w_up"], landed["w_down"]])])
    w_gate_f, w_up_f, w_down_f = (with_own(f, n) for f, n in zip(ffn_w, ("w_gate", "w_up", "w_down")))
    h2, act, dg, dup, df, dx1, loss_cols, d_pre_ffn, d_post_ffn = _ffn_fwd_bwd(
        x1, target, w_gate_f, w_up_f, w_down_f, small["pre_ffn_norm"], small["post_ffn_norm"])
    loss = lax.psum(jnp.sum(loss_cols) * np.float32(0.5 / D_MODEL), ("x", "y", "c"))

    full_tok = pl.BlockSpec((SEQ, D_MODEL), lambda s: (0, 0), pipeline_mode=pl.Buffered(1))
    ff_tok = pl.BlockSpec((1, SEQ, FF_S), lambda s: (s, 0, 0))
    gw = {}
    gw["w_gate"] = _wgrad(h2, dg, full_tok, ff_tok, (D_MODEL, FF_S), "wgrad_gate")
    gw["w_up"], ((sib_gate,),) = _wgrad(h2, dup, full_tok, ff_tok, (D_MODEL, FF_S), "wgrad_up",
                                        comms=[_rs_sibling([gw["w_gate"]])])
    gw["w_down"], ((sib_up,),) = _wgrad(act, df, ff_tok, full_tok, (FF_S, D_MODEL), "wgrad_down",
                                        comms=[_rs_sibling([gw["w_up"]])])
    (dy, dattn, delta, dsgu, d_post_mix, d_attn_norm, d_sgu_norm), ((sib_down,),) = _outproj_bwd(
        dx1, y, attn, sgu, w_out_f, small["post_mix_norm"], small["attn_out_norm"],
        small["sgu_out_norm"], comms=[_rs_sibling([gw["w_down"]])])
    sib = {"w_gate": sib_gate, "w_up": sib_up, "w_down": sib_down}
    part = {n: _rs_chip_sum(gw[n], sib[n], core) for n in ("w_gate", "w_up", "w_down")}
    gw["w_out"] = _wgrad(mixed, dy, pl.BlockSpec((SEQ, OUT_S), lambda s: (0, s)), full_tok, (OUT_S, D_MODEL),
                         "wgrad_out")
    (du, dvs_sgu, d_w_sp, d_b_sp, d_ln_gain, d_ln_bias), ((far_gate,), (sib["w_out"],)) = _sgu_bwd(
        u, vs, dsgu, small["sgu_ln_gain"], small["sgu_ln_bias"], small["sgu_w_spatial"], b_t,
        comms=[_rs_chips([part["w_gate"]]), _rs_sibling([gw["w_out"]])])
    part["w_out"] = _rs_chip_sum(gw["w_out"], sib["w_out"], core)
    half, far, joined = {}, {"w_gate": far_gate}, {}
    half["w_gate"] = _rs_final_sum(gw["w_gate"], sib["w_gate"], far["w_gate"], shard_core)

    dqs, dks, dvs = [], [], []
    for dil, (qv, kv, vv) in zip(DILATIONS, views):
        if dil == 1:
            riders = [_rs_chips([part["w_up"], part["w_down"]])]
        elif dil == 4:
            riders = [_rs_chips([part["w_out"]]), _rs_join([half["w_gate"]])]
        else:
            riders = [_rs_join([half["w_up"], half["w_down"]])]
        (dq, dk, dv), got = _attn_bwd(qv, kv, vv, _to_view(dattn, dil), _to_view(delta, dil), _to_view(lse, dil),
                                      dil, comms=riders)
        if dil == 1:
            far["w_up"], far["w_down"] = got[0]
            for n in ("w_up", "w_down"):
                half[n] = _rs_final_sum(gw[n], sib[n], far[n], shard_core)
        elif dil == 4:
            (far["w_out"],), (joined["w_gate"],) = got
            half["w_out"] = _rs_final_sum(gw["w_out"], sib["w_out"], far["w_out"], shard_core)
        else:
            joined["w_up"], joined["w_down"] = got[0]
        dqs.append(_from_view(dq, dil))
        dks.append(_from_view(dk, dil))
        dvs.append(_from_view(dv, dil))
    (dproj, grad_x, d_pre_mix), ((joined["w_out"],),) = _inproj_bwd(
        dqs, dks, dvs, du, dvs_sgu, pos, xs, dx1, w_in_f, small["pre_mix_norm"], comms=[_rs_join([half["w_out"]])])
    small_grads = {
        "pre_mix_norm": d_pre_mix, "sgu_ln_gain": d_ln_gain, "sgu_ln_bias": d_ln_bias, "sgu_w_spatial": d_w_sp,
        "sgu_b_spatial": d_b_sp, "attn_out_norm": d_attn_norm, "sgu_out_norm": d_sgu_norm,
        "post_mix_norm": d_post_mix, "pre_ffn_norm": d_pre_ffn, "post_ffn_norm": d_post_ffn,
    }
    packed = _pack_small(small_grads)
    gw["w_in"], ((slots,),) = _wgrad(h, dproj, full_tok, pl.BlockSpec((SEQ, IN_S), lambda s: (0, s)),
                                     (D_MODEL, IN_S), "wgrad_in", comms=[_small_exchange(packed)])
    ((sib["w_in"],),) = _comm_only("comm_rs_sibling_in", [_rs_sibling([gw["w_in"]])])
    part["w_in"] = _rs_chip_sum(gw["w_in"], sib["w_in"], core)
    ((far["w_in"],),) = _comm_only("comm_rs_chips_in", [_rs_chips([part["w_in"]])])
    half["w_in"] = _rs_final_sum(gw["w_in"], sib["w_in"], far["w_in"], shard_core)
    ((joined["w_in"],),) = _comm_only("comm_rs_join_in", [_rs_join([half["w_in"]])])

    grads, deltas, new_m, new_v = {}, {}, {}, {}
    for n in BIG:
        g = joined[n].reshape(a[n].shape[1:])
        grads[n] = g[None]
        d, nm, nv = _adamw(a[n][0], g, a["m_" + n][0], a["v_" + n][0], BIG_ADAM_ROWS[n], "adamw_" + n)
        deltas[n], new_m[n], new_v[n] = d[None], nm[None], nv[None]
    shapes = {n: a[n].shape for n in SMALL}
    g, d, nm, nv = _adamw_small(packed, slots, _pack_small({n: a[n] for n in SMALL}),
                                _pack_small({n: a["m_" + n] for n in SMALL}),
                                _pack_small({n: a["v_" + n] for n in SMALL}), me)
    grads.update(_unpack_small(g, shapes))
    deltas.update(_unpack_small(d, shapes))
    new_m.update(_unpack_small(nm, shapes))
    new_v.update(_unpack_small(nv, shapes))
    return (loss, grad_x[None], *[grads[n] for n in WEIGHTS], *[deltas[n] for n in WEIGHTS],
            *[new_m[n] for n in WEIGHTS], *[new_v[n] for n in WEIGHTS])
```

```python
import numpy as np
import jax
import jax.numpy as jnp
from jax import lax
from jax.experimental import pallas as pl
from jax.experimental.pallas import tpu as pltpu

F32 = jnp.float32
BF16 = jnp.bfloat16

SEQ = 2048
D_MODEL = 1024
HEAD_DIM = 64
ATTN_W = 512
SGU_W = 512
SGU_GROUPS = 8
CHUNK = 128
DILATIONS = (1, 4, 16)
N_SHARD = 4
IN_S = 640
OUT_S = 256
FF_S = 704
PROJ_W = N_SHARD * IN_S
RMS_EPS = 1e-6
LN_EPS = 1e-5
ROPE_THETA = 500000.0
ATTN_SCALE = 1.0 / np.sqrt(HEAD_DIM)
NEG = -1e30
TM = 256
VMEM_LIMIT = 56 * 1024 * 1024
SMALL_ROWS = 136

ADAM_LR = 0.001
ADAM_B1 = 0.9
ADAM_B2 = 0.999
ADAM_EPS = 1e-08
ADAM_WD = 0.01
ADAM_STEP = 10

MESH = pl.DeviceIdType.MESH
ANY = pl.BlockSpec(memory_space=pl.ANY)


def _dot(a, b):
    return jnp.dot(a, b, preferred_element_type=F32)


def _dot_nt(a, b):
    return lax.dot_general(a, b, (((1,), (1,)), ((), ())), preferred_element_type=F32)


def _dot_tn(a, b):
    return lax.dot_general(a, b, (((0,), (0,)), ((), ())), preferred_element_type=F32)


def _dot_exact(a, b):
    return jnp.dot(a, b, preferred_element_type=F32, precision=lax.Precision.HIGHEST)


def _rms_stats(x):
    r = lax.rsqrt(jnp.mean(x * x, axis=-1, keepdims=True) + RMS_EPS)
    return x * r, r


def _rms_bwd(xh, r, gain, dy):
    dxh = dy * gain
    dx = r * (dxh - xh * jnp.mean(dxh * xh, axis=-1, keepdims=True))
    return dx, jnp.sum(dy * xh, axis=0, keepdims=True)


_ERF_ALPHA = (-2.72614225801306e-10, 2.77068142495902e-08, -2.10102402082508e-06, -5.69250639462346e-05,
              -7.34990630326855e-04, -2.95459980854025e-03, -1.60960333262415e-02)
_ERF_BETA = (-1.45660718464996e-05, -2.13374055278905e-04, -1.68282697438203e-03, -7.37332916720468e-03,
             -1.42647390514189e-02)


def _erf(x):
    x = jnp.clip(x, -4.0, 4.0)
    x2 = x * x
    p = jnp.full_like(x, _ERF_ALPHA[0])
    for a in _ERF_ALPHA[1:]:
        p = p * x2 + a
    q = jnp.full_like(x, _ERF_BETA[0])
    for b in _ERF_BETA[1:]:
        q = q * x2 + b
    return x * p / q


def _gelu(x):
    return 0.5 * x * (1.0 + _erf(x * np.float32(1.0 / np.sqrt(2.0))))


def _gelu_grad(x):
    cdf = 0.5 * (1.0 + _erf(x * np.float32(1.0 / np.sqrt(2.0))))
    pdf = jnp.exp(-0.5 * x * x) * np.float32(1.0 / np.sqrt(2.0 * np.pi))
    return cdf + x * pdf


def _sigmoid(x):
    return 1.0 / (1.0 + jnp.exp(-x))


_INV_FREQ = tuple(float(np.float32(ROPE_THETA ** (-2.0 * j / 16.0))) for j in range(8))


def _rot_tables(pos):
    lane = lax.broadcasted_iota(jnp.int32, (1, 128), 1)
    d = lane & 63
    j = d & 7
    inv = jnp.zeros((1, 128), F32)
    for jj in range(8):
        inv = jnp.where(j == jj, _INV_FREQ[jj], inv)
    ang = pos.astype(F32) * inv
    c = jnp.cos(ang)
    s = jnp.sin(ang)
    cos_t = jnp.where(d < 16, c, 1.0)
    sin_a = jnp.where(d < 8, -s, 0.0)
    sin_b = jnp.where((d >= 8) & (d < 16), s, 0.0)
    return tuple(jnp.tile(t, (1, 4)) for t in (cos_t, sin_a, sin_b))


def _rope(x, tabs):
    cos_t, sin_a, sin_b = tabs
    return x * cos_t + pltpu.roll(x, 504, 1) * sin_a + pltpu.roll(x, 8, 1) * sin_b


def _rope_bwd(dy, tabs):
    cos_t, sin_a, sin_b = tabs
    return dy * cos_t + pltpu.roll(dy * sin_a, 8, 1) + pltpu.roll(dy * sin_b, 504, 1)


def _left_half():
    return lax.broadcasted_iota(jnp.int32, (CHUNK, CHUNK), 1) < HEAD_DIM


def _group_ones():
    lane = lax.broadcasted_iota(jnp.int32, (SGU_GROUPS, SGU_W), 1)
    row = lax.broadcasted_iota(jnp.int32, (SGU_GROUPS, SGU_W), 0)
    return ((lane >> 6) == row).astype(F32)


def _masked_spatial(w_ref):
    row = lax.broadcasted_iota(jnp.int32, (CHUNK, CHUNK), 0)
    col = lax.broadcasted_iota(jnp.int32, (CHUNK, CHUNK), 1)
    return [jnp.where(col <= row, w_ref[g], 0.0).astype(BF16) for g in range(SGU_GROUPS)]


def _sgu_core(u, vs, lg, lb, wm, bias_full):
    tm = u.shape[0]
    gu = _gelu(u)
    gv = _gelu(vs)
    mu = jnp.mean(gv, axis=-1, keepdims=True)
    xc = gv - mu
    rstd = lax.rsqrt(jnp.mean(xc * xc, axis=-1, keepdims=True) + LN_EPS)
    xh = xc * rstd
    vnb = (xh * lg + lb).astype(BF16)
    left = _left_half()
    rows = []
    for c in range(tm // CHUNK):
        pieces = []
        for p in range(4):
            vp = vnb[c * CHUNK:(c + 1) * CHUNK, p * 128:(p + 1) * 128]
            pieces.append(jnp.where(left, _dot(wm[2 * p], vp), _dot(wm[2 * p + 1], vp)))
        rows.append(jnp.concatenate(pieces, axis=1) + bias_full)
    mixed = jnp.concatenate(rows, axis=0)
    return gu, xh, rstd, vnb, mixed


def _resident(shape):
    n = len(shape)
    return pl.BlockSpec(shape, lambda *_: (0,) * n, pipeline_mode=pl.Buffered(1))


def _rows(ncol, tm=TM):
    return pl.BlockSpec((tm, ncol), lambda i: (i, 0))


def _rows3(nlead, ncol, tm=TM):
    return pl.BlockSpec((nlead, tm, ncol), lambda i: (0, i, 0))


def _acc(ncol, nrow=1):
    return pl.BlockSpec((nrow, ncol), lambda i: (0, 0))


def _seq_params():
    return pltpu.CompilerParams(dimension_semantics=("arbitrary",), vmem_limit_bytes=VMEM_LIMIT)


def _sds(shape, dtype):
    return jax.ShapeDtypeStruct(shape, dtype)


class _Comm:
    def __init__(self, args, out_shape, n_sems, start, finish, aliased=False):
        self.args, self.out_shape, self.n_sems = list(args), list(out_shape), n_sems
        self.start, self.finish, self.aliased = start, finish, aliased


def _pcall(body, *, name, grid, in_specs, out_specs, out_shape, args, scratch_shapes=(), comms=()):
    single = not isinstance(out_shape, (list, tuple))
    out_specs = [out_specs] if single else list(out_specs)
    out_shape = [out_shape] if single else list(out_shape)
    n_in, n_out, n_scr = len(in_specs), len(out_shape), len(scratch_shapes)
    c_args = [a for c in comms for a in c.args]
    c_outs = [o for c in comms for o in c.out_shape]
    aliases, ai, ao = {}, n_in, n_out
    for c in comms:
        if c.aliased:
            aliases.update({ai + k: ao + k for k in range(len(c.args))})
        ai += len(c.args)
        ao += len(c.out_shape)
    sems = [pltpu.SemaphoreType.DMA((c.n_sems,)) for c in comms for _ in range(2)]
    steps = grid[0]

    def wrapped(*refs):
        o0 = n_in + len(c_args)
        s0 = o0 + n_out + len(c_outs)
        m_in, m_out, m_sem = refs[n_in:o0], refs[o0 + n_out:s0], refs[s0 + n_scr:]

        def each(phase):
            ii = oi = 0
            for k, c in enumerate(comms):
                getattr(c, phase)(m_in[ii:ii + len(c.args)], m_out[oi:oi + len(c.out_shape)],
                                  m_sem[2 * k], m_sem[2 * k + 1])
                ii += len(c.args)
                oi += len(c.out_shape)

        if comms:
            @pl.when(pl.program_id(0) == 0)
            def _():
                each("start")

        body(*refs[:n_in], *refs[o0:o0 + n_out], *refs[s0:s0 + n_scr])

        if comms:
            @pl.when(pl.program_id(0) == steps - 1)
            def _():
                each("finish")

    res = pl.pallas_call(
        wrapped, name=name, grid=grid,
        in_specs=list(in_specs) + [ANY] * len(c_args), out_specs=out_specs + [ANY] * len(c_outs),
        out_shape=out_shape + c_outs, scratch_shapes=list(scratch_shapes) + sems,
        input_output_aliases=aliases, compiler_params=_seq_params(),
    )(*args, *c_args)
    mine = res[0] if single else list(res[:n_out])
    if not comms:
        return mine
    theirs, oi = [], n_out
    for c in comms:
        theirs.append(list(res[oi:oi + len(c.out_shape)]))
        oi += len(c.out_shape)
    return mine, theirs


def _inproj_fwd(x, pos, g_pre, w_in, comms=()):
    def body(x_ref, pos_ref, g_ref, w_ref, h_ref, q_ref, k_ref, v_ref, u_ref, vs_ref):
        xh, _ = _rms_stats(x_ref[...])
        h = (xh * g_ref[...]).astype(BF16)
        h_ref[...] = h
        proj = jnp.concatenate([_dot(h, w_ref[s]) for s in range(N_SHARD)], axis=1)
        tabs = _rot_tables(pos_ref[...])
        q_ref[...] = (_rope(proj[:, 0:512], tabs) * np.float32(ATTN_SCALE)).astype(BF16)
        k_ref[...] = _rope(proj[:, 512:1024], tabs).astype(BF16)
        v_ref[...] = proj[:, 1024:1536].astype(BF16)
        u_ref[...] = proj[:, 1536:2048]
        vs_ref[...] = proj[:, 2048:2560]

    return _pcall(
        body, name="inproj_fwd", grid=(SEQ // TM,),
        in_specs=[_rows(D_MODEL), _rows(1), _resident((1, D_MODEL)), _resident((N_SHARD, D_MODEL, IN_S))],
        out_specs=[_rows(D_MODEL), _rows(512), _rows(512), _rows(512), _rows(512), _rows(512)],
        out_shape=[_sds((SEQ, D_MODEL), BF16), _sds((SEQ, 512), BF16), _sds((SEQ, 512), BF16),
                   _sds((SEQ, 512), BF16), _sds((SEQ, 512), F32), _sds((SEQ, 512), F32)],
        args=(x, pos, g_pre, w_in), comms=comms)


def _sgu_fwd(u, vs, lg, lb, w_sp, b_t, comms=()):
    def body(u_ref, vs_ref, lg_ref, lb_ref, w_ref, bt_ref, out_ref):
        wm = _masked_spatial(w_ref)
        bias_full = _dot_exact(bt_ref[...], _group_ones())
        gu, _, _, _, mixed = _sgu_core(u_ref[...], vs_ref[...], lg_ref[...], lb_ref[...], wm, bias_full)
        out_ref[...] = gu * mixed

    return _pcall(
        body, name="sgu_fwd", grid=(SEQ // TM,),
        in_specs=[_rows(SGU_W), _rows(SGU_W), _resident((1, SGU_W)), _resident((1, SGU_W)),
                  _resident((SGU_GROUPS, CHUNK, CHUNK)), _resident((CHUNK, SGU_GROUPS))],
        out_specs=_rows(SGU_W),
        out_shape=_sds((SEQ, SGU_W), F32),
        args=(u, vs, lg, lb, w_sp, b_t), comms=comms)


def _block_masks():
    row = lax.broadcasted_iota(jnp.int32, (CHUNK, CHUNK), 0)
    col = lax.broadcasted_iota(jnp.int32, (CHUNK, CHUNK), 1)
    return col <= row, col >= row


def _attn_fwd(qv, kv, vv, dil, comms=()):
    seg = SEQ // dil
    nblk = seg // CHUNK

    def body(q_ref, k_ref, v_ref, o_ref, l_ref):
        left = _left_half()
        m_cur, m_prev = _block_masks()
        zero = jnp.zeros((CHUNK, CHUNK), BF16)
        ones = (jnp.where(left, 1.0, 0.0).astype(BF16), jnp.where(left, 0.0, 1.0).astype(BF16))

        def blk(b, carry):
            r0 = pl.multiple_of(b * CHUNK, CHUNK)
            rp = pl.multiple_of(jnp.maximum(b - 1, 0) * CHUNK, CHUNK)
            prev_ok = m_prev & (b > 0)
            for hp in range(4):
                ls = slice(hp * 128, (hp + 1) * 128)
                qp = q_ref[pl.ds(r0, CHUNK), ls]
                kc = k_ref[pl.ds(r0, CHUNK), ls]
                vc = v_ref[pl.ds(r0, CHUNK), ls]
                if nblk > 1:
                    kp = k_ref[pl.ds(rp, CHUNK), ls]
                    vp = v_ref[pl.ds(rp, CHUNK), ls]
                acc = jnp.zeros((CHUNK, CHUNK), F32)
                den = jnp.zeros((CHUNK, CHUNK), F32)
                maxes = []
                for side, hm in enumerate((left, ~left)):
                    qh = jnp.where(hm, qp, zero)
                    sc = jnp.where(m_cur, _dot_nt(qh, kc), NEG)
                    if nblk > 1:
                        sp = jnp.where(prev_ok, _dot_nt(qh, kp), NEG)
                        m = jnp.max(jnp.maximum(sc, sp), axis=-1, keepdims=True)
                        pc = jnp.exp(sc - m)
                        pp = jnp.exp(sp - m)
                        acc = acc + _dot(pc.astype(BF16), jnp.where(hm, vc, zero))
                        acc = acc + _dot(pp.astype(BF16), jnp.where(hm, vp, zero))
                        den = den + _dot((pc + pp).astype(BF16), ones[side])
                    else:
                        m = jnp.max(sc, axis=-1, keepdims=True)
                        pc = jnp.exp(sc - m).astype(BF16)
                        acc = acc + _dot(pc, jnp.where(hm, vc, zero))
                        den = den + _dot(pc, ones[side])
                    maxes.append(m)
                o_ref[pl.ds(r0, CHUNK), ls] = acc / den
                l_ref[pl.ds(r0, CHUNK), ls] = jnp.where(left, maxes[0], maxes[1]) + jnp.log(den)
            return carry

        lax.fori_loop(0, nblk, blk, 0)

    spec = pl.BlockSpec((seg, ATTN_W), lambda r: (0, r))
    return _pcall(
        body, name=f"attn_fwd_d{dil}", grid=(dil,),
        in_specs=[spec, spec, spec], out_specs=[spec, spec],
        out_shape=[_sds((seg, dil * ATTN_W), F32), _sds((seg, dil * ATTN_W), F32)],
        args=(qv, kv, vv), comms=comms)


def _mix_out_fwd(o_list, l_list, sgu, x, w_out, g_attn, g_sgu, g_post, comms=()):
    def body(o1, o2, o3, l1, l2, l3, sgu_ref, x_ref, w_ref, ga_ref, gs_ref, gp_ref,
             attn_ref, lse_ref, mixed_ref, y_ref, x1_ref):
        ls = [l1[...], l2[...], l3[...]]
        m = jnp.maximum(jnp.maximum(ls[0], ls[1]), ls[2])
        es = [jnp.exp(l - m) for l in ls]
        den = es[0] + es[1] + es[2]
        attn = (es[0] * o1[...] + es[1] * o2[...] + es[2] * o3[...]) / den
        attn_ref[...] = attn
        lse_ref[...] = m + jnp.log(den)
        ah, _ = _rms_stats(attn)
        sh, _ = _rms_stats(sgu_ref[...])
        mixed = jnp.concatenate([ah * ga_ref[...], sh * gs_ref[...]], axis=1).astype(BF16)
        mixed_ref[...] = mixed
        y = _dot(mixed[:, 0:OUT_S], w_ref[0])
        for s in range(1, N_SHARD):
            y = y + _dot(mixed[:, s * OUT_S:(s + 1) * OUT_S], w_ref[s])
        y_ref[...] = y
        yh, _ = _rms_stats(y)
        x1_ref[...] = x_ref[...] + yh * gp_ref[...]

    return _pcall(
        body, name="mix_out_fwd", grid=(SEQ // TM,),
        in_specs=[_rows(512)] * 7 + [_rows(D_MODEL), _resident((N_SHARD, OUT_S, D_MODEL)),
                                    _resident((1, 512)), _resident((1, 512)), _resident((1, D_MODEL))],
        out_specs=[_rows(512), _rows(512), _rows(D_MODEL), _rows(D_MODEL), _rows(D_MODEL)],
        out_shape=[_sds((SEQ, 512), F32), _sds((SEQ, 512), F32), _sds((SEQ, D_MODEL), BF16),
                   _sds((SEQ, D_MODEL), F32), _sds((SEQ, D_MODEL), F32)],
        args=(*o_list, *l_list, sgu, x, w_out, g_attn, g_sgu, g_post), comms=comms)


def _ffn_fwd_bwd(x1, target, w_gate, w_up, w_down, g_pre, g_post, comms=()):
    def body(x1_ref, t_ref, wg_ref, wu_ref, wd_ref, gpf_ref, gpo_ref,
             h2_ref, a_ref, dg_ref, dup_ref, df_ref, dx1_ref, loss_ref, dgpf_ref, dgpo_ref, g_scr, up_scr):
        @pl.when(pl.program_id(0) == 0)
        def _():
            loss_ref[...] = jnp.zeros_like(loss_ref)
            dgpf_ref[...] = jnp.zeros_like(dgpf_ref)
            dgpo_ref[...] = jnp.zeros_like(dgpo_ref)

        x1 = x1_ref[...]
        gpf = gpf_ref[...]
        gpo = gpo_ref[...]
        xh, r = _rms_stats(x1)
        h2 = (xh * gpf).astype(BF16)
        h2_ref[...] = h2
        f = jnp.zeros((TM, D_MODEL), F32)
        for s in range(N_SHARD):
            g = _dot(h2, wg_ref[s])
            up = _dot(h2, wu_ref[s])
            g_scr[s] = g
            up_scr[s] = up
            a = (g * _sigmoid(g) * up).astype(BF16)
            a_ref[s] = a
            f = f + _dot(a, wd_ref[s])
        fh, rf = _rms_stats(f)
        diff = x1 + fh * gpo - t_ref[...]
        loss_ref[...] += jnp.sum(diff * diff, axis=0, keepdims=True)
        dout = diff * np.float32(1.0 / D_MODEL)
        df, dgpo = _rms_bwd(fh, rf, gpo, dout)
        dgpo_ref[...] += dgpo
        dfb = df.astype(BF16)
        df_ref[...] = dfb
        dh2 = jnp.zeros((TM, D_MODEL), F32)
        for s in range(N_SHARD):
            da = _dot_nt(dfb, wd_ref[s])
            g = g_scr[s]
            up = up_scr[s]
            sg = _sigmoid(g)
            dup = (da * (g * sg)).astype(BF16)
            dg = (da * up * (sg * (1.0 + g * (1.0 - sg)))).astype(BF16)
            dg_ref[s] = dg
            dup_ref[s] = dup
            dh2 = dh2 + _dot_nt(dg, wg_ref[s]) + _dot_nt(dup, wu_ref[s])
        dx, dgpf = _rms_bwd(xh, r, gpf, dh2)
        dgpf_ref[...] += dgpf
        dx1_ref[...] = dout + dx

    return _pcall(
        body, name="ffn_fwd_bwd", grid=(SEQ // TM,),
        in_specs=[_rows(D_MODEL), _rows(D_MODEL), _resident((N_SHARD, D_MODEL, FF_S)),
                  _resident((N_SHARD, D_MODEL, FF_S)), _resident((N_SHARD, FF_S, D_MODEL)),
                  _resident((1, D_MODEL)), _resident((1, D_MODEL))],
        out_specs=[_rows(D_MODEL), _rows3(N_SHARD, FF_S), _rows3(N_SHARD, FF_S), _rows3(N_SHARD, FF_S),
                   _rows(D_MODEL), _rows(D_MODEL), _acc(D_MODEL), _acc(D_MODEL), _acc(D_MODEL)],
        out_shape=[_sds((SEQ, D_MODEL), BF16), _sds((N_SHARD, SEQ, FF_S), BF16), _sds((N_SHARD, SEQ, FF_S), BF16),
                   _sds((N_SHARD, SEQ, FF_S), BF16), _sds((SEQ, D_MODEL), BF16), _sds((SEQ, D_MODEL), F32),
                   _sds((1, D_MODEL), F32), _sds((1, D_MODEL), F32), _sds((1, D_MODEL), F32)],
        scratch_shapes=[pltpu.VMEM((N_SHARD, TM, FF_S), F32), pltpu.VMEM((N_SHARD, TM, FF_S), F32)],
        args=(x1, target, w_gate, w_up, w_down, g_pre, g_post), comms=comms)


def _wgrad(a, b, a_spec, b_spec, out_block, name, comms=()):
    def body(a_ref, b_ref, o_ref):
        av = a_ref[0] if len(a_ref.shape) == 3 else a_ref[...]
        bv = b_ref[0] if len(b_ref.shape) == 3 else b_ref[...]
        o_ref[0] = _dot_tn(av, bv)

    return _pcall(
        body, name=name, grid=(N_SHARD,),
        in_specs=[a_spec, b_spec],
        out_specs=pl.BlockSpec((1,) + out_block, lambda s: (s, 0, 0)),
        out_shape=_sds((N_SHARD,) + out_block, F32),
        args=(a, b), comms=comms)


def _outproj_bwd(dx1, y, attn, sgu, w_out, g_post, g_attn, g_sgu, comms=()):
    def body(dx1_ref, y_ref, attn_ref, sgu_ref, w_ref, gp_ref, ga_ref, gs_ref,
             dy_ref, dattn_ref, delta_ref, dsgu_ref, dgp_ref, dga_ref, dgs_ref):
        @pl.when(pl.program_id(0) == 0)
        def _():
            dgp_ref[...] = jnp.zeros_like(dgp_ref)
            dga_ref[...] = jnp.zeros_like(dga_ref)
            dgs_ref[...] = jnp.zeros_like(dgs_ref)

        yh, ry = _rms_stats(y_ref[...])
        dy, dgp = _rms_bwd(yh, ry, gp_ref[...], dx1_ref[...])
        dgp_ref[...] += dgp
        dyb = dy.astype(BF16)
        dy_ref[...] = dyb
        dmixed = jnp.concatenate([_dot_nt(dyb, w_ref[s]) for s in range(N_SHARD)], axis=1)
        attn = attn_ref[...]
        ah, ra = _rms_stats(attn)
        dattn, dga = _rms_bwd(ah, ra, ga_ref[...], dmixed[:, 0:512])
        dga_ref[...] += dga
        sh, rs = _rms_stats(sgu_ref[...])
        dsgu, dgs = _rms_bwd(sh, rs, gs_ref[...], dmixed[:, 512:1024])
        dgs_ref[...] += dgs
        dattn_ref[...] = dattn.astype(BF16)
        dsgu_ref[...] = dsgu
        la = lax.broadcasted_iota(jnp.int32, (ATTN_W, ATTN_W), 0) >> 6
        lb = lax.broadcasted_iota(jnp.int32, (ATTN_W, ATTN_W), 1) >> 6
        delta_ref[...] = _dot_exact(dattn * attn, (la == lb).astype(F32))

    return _pcall(
        body, name="outproj_bwd", grid=(SEQ // TM,),
        in_specs=[_rows(D_MODEL), _rows(D_MODEL), _rows(512), _rows(512), _resident((N_SHARD, OUT_S, D_MODEL)),
                  _resident((1, D_MODEL)), _resident((1, 512)), _resident((1, 512))],
        out_specs=[_rows(D_MODEL), _rows(512), _rows(512), _rows(512), _acc(D_MODEL), _acc(512), _acc(512)],
        out_shape=[_sds((SEQ, D_MODEL), BF16), _sds((SEQ, 512), BF16), _sds((SEQ, 512), F32), _sds((SEQ, 512), F32),
                   _sds((1, D_MODEL), F32), _sds((1, 512), F32), _sds((1, 512), F32)],
        args=(dx1, y, attn, sgu, w_out, g_post, g_attn, g_sgu), comms=comms)


def _sgu_bwd(u, vs, dsgu, lg, lb, w_sp, b_t, comms=()):
    nsteps = SEQ // TM

    def body(u_ref, vs_ref, ds_ref, lg_ref, lb_ref, w_ref, bt_ref,
             du_ref, dvs_ref, dw_ref, db_ref, dlg_ref, dlb_ref, dbias_scr):
        i = pl.program_id(0)

        @pl.when(i == 0)
        def _():
            dw_ref[...] = jnp.zeros_like(dw_ref)
            dlg_ref[...] = jnp.zeros_like(dlg_ref)
            dlb_ref[...] = jnp.zeros_like(dlb_ref)
            dbias_scr[...] = jnp.zeros_like(dbias_scr)

        wm = _masked_spatial(w_ref)
        ones_g = _group_ones()
        bias_full = _dot_exact(bt_ref[...], ones_g)
        u = u_ref[...]
        vs = vs_ref[...]
        lg = lg_ref[...]
        gu, xh, rstd, vnb, mixed = _sgu_core(u, vs, lg, lb_ref[...], wm, bias_full)
        dsgu = ds_ref[...]
        du_ref[...] = (dsgu * mixed * _gelu_grad(u)).astype(BF16)
        dmixed = dsgu * gu
        left = _left_half()
        dvn_rows = []
        for c in range(TM // CHUNK):
            rs = slice(c * CHUNK, (c + 1) * CHUNK)
            dm_c = dmixed[rs, :]
            dbias_scr[...] += dm_c
            pieces = []
            for p in range(4):
                ls = slice(p * 128, (p + 1) * 128)
                dmp = dm_c[:, ls]
                vp = vnb[rs, ls]
                dmb = dmp.astype(BF16)
                zero = jnp.zeros_like(dmb)
                dw_ref[2 * p] += _dot_nt(jnp.where(left, dmb, zero), vp)
                dw_ref[2 * p + 1] += _dot_nt(jnp.where(left, zero, dmb), vp)
                pieces.append(jnp.where(left, _dot_tn(wm[2 * p], dmb), _dot_tn(wm[2 * p + 1], dmb)))
            dvn_rows.append(jnp.concatenate(pieces, axis=1))
        dvn = jnp.concatenate(dvn_rows, axis=0)
        dlg_ref[...] += jnp.sum(dvn * xh, axis=0, keepdims=True)
        dlb_ref[...] += jnp.sum(dvn, axis=0, keepdims=True)
        dxh = dvn * lg
        dgv = rstd * (dxh - jnp.mean(dxh, axis=-1, keepdims=True) - xh * jnp.mean(dxh * xh, axis=-1, keepdims=True))
        dvs_ref[...] = (dgv * _gelu_grad(vs)).astype(BF16)

        @pl.when(i == nsteps - 1)
        def _():
            row = lax.broadcasted_iota(jnp.int32, (CHUNK, CHUNK), 0)
            col = lax.broadcasted_iota(jnp.int32, (CHUNK, CHUNK), 1)
            for g in range(SGU_GROUPS):
                dw_ref[g] = jnp.where(col <= row, dw_ref[g], 0.0)
            db_ref[...] = lax.dot_general(ones_g, dbias_scr[...], (((1,), (1,)), ((), ())),
                                          preferred_element_type=F32, precision=lax.Precision.HIGHEST)

    return _pcall(
        body, name="sgu_bwd", grid=(nsteps,),
        in_specs=[_rows(SGU_W), _rows(SGU_W), _rows(SGU_W), _resident((1, SGU_W)), _resident((1, SGU_W)),
                  _resident((SGU_GROUPS, CHUNK, CHUNK)), _resident((CHUNK, SGU_GROUPS))],
        out_specs=[_rows(SGU_W), _rows(SGU_W), pl.BlockSpec((SGU_GROUPS, CHUNK, CHUNK), lambda i: (0, 0, 0)),
                   _acc(CHUNK, SGU_GROUPS), _acc(SGU_W), _acc(SGU_W)],
        out_shape=[_sds((SEQ, SGU_W), BF16), _sds((SEQ, SGU_W), BF16), _sds((SGU_GROUPS, CHUNK, CHUNK), F32),
                   _sds((SGU_GROUPS, CHUNK), F32), _sds((1, SGU_W), F32), _sds((1, SGU_W), F32)],
        scratch_shapes=[pltpu.VMEM((CHUNK, SGU_W), F32)],
        args=(u, vs, dsgu, lg, lb, w_sp, b_t), comms=comms)


def _attn_bwd(qv, kv, vv, dov, deltav, lsev, dil, comms=()):
    seg = SEQ // dil
    nblk = seg // CHUNK

    def body(q_ref, k_ref, v_ref, do_ref, dl_ref, lse_ref, dq_ref, dk_ref, dv_ref):
        left = _left_half()
        m_cur, m_prev = _block_masks()

        def blk(b, carry):
            r0 = pl.multiple_of(b * CHUNK, CHUNK)
            rp = pl.multiple_of(jnp.maximum(b - 1, 0) * CHUNK, CHUNK)
            prev_ok = m_prev & (b > 0)
            for hp in range(4):
                ls = slice(hp * 128, (hp + 1) * 128)
                qp = q_ref[pl.ds(r0, CHUNK), ls]
                kc = k_ref[pl.ds(r0, CHUNK), ls]
                vc = v_ref[pl.ds(r0, CHUNK), ls]
                dop = do_ref[pl.ds(r0, CHUNK), ls]
                lsep = lse_ref[pl.ds(r0, CHUNK), ls]
                dlp = dl_ref[pl.ds(r0, CHUNK), ls]
                if nblk > 1:
                    kp = k_ref[pl.ds(rp, CHUNK), ls]
                    vp = v_ref[pl.ds(rp, CHUNK), ls]
                dqs = []
                dkc = jnp.zeros((CHUNK, 128), F32)
                dvc = jnp.zeros((CHUNK, 128), F32)
                dkp = jnp.zeros((CHUNK, 128), F32)
                dvp = jnp.zeros((CHUNK, 128), F32)
                for side, hm in enumerate((left, ~left)):
                    c0 = side * HEAD_DIM
                    qh = jnp.where(hm, qp, jnp.zeros_like(qp))
                    doh = jnp.where(hm, dop, jnp.zeros_like(dop))
                    lse_h = lsep[:, c0:c0 + 1]
                    dl_h = dlp[:, c0:c0 + 1]
                    sc = _dot_nt(qh, kc)
                    pc = jnp.exp(jnp.where(m_cur, sc - lse_h, NEG))
                    dsc = (pc * (_dot_nt(doh, vc) - dl_h)).astype(BF16)
                    dq = _dot(dsc, kc)
                    dkc = dkc + _dot_tn(dsc, qh)
                    dvc = dvc + _dot_tn(pc.astype(BF16), doh)
                    if nblk > 1:
                        sp = _dot_nt(qh, kp)
                        pp = jnp.exp(jnp.where(prev_ok, sp - lse_h, NEG))
                        dsp = (pp * (_dot_nt(doh, vp) - dl_h)).astype(BF16)
                        dq = dq + _dot(dsp, kp)
                        dkp = dkp + _dot_tn(dsp, qh)
                        dvp = dvp + _dot_tn(pp.astype(BF16), doh)
                    dqs.append(dq)
                dq_ref[pl.ds(r0, CHUNK), ls] = jnp.where(left, dqs[0], dqs[1])
                dk_ref[pl.ds(r0, CHUNK), ls] = dkc
                dv_ref[pl.ds(r0, CHUNK), ls] = dvc
                if nblk > 1:
                    @pl.when(b > 0)
                    def _():
                        dk_ref[pl.ds(rp, CHUNK), ls] += dkp
                        dv_ref[pl.ds(rp, CHUNK), ls] += dvp
            return carry

        lax.fori_loop(0, nblk, blk, 0)

    spec = pl.BlockSpec((seg, ATTN_W), lambda r: (0, r))
    return _pcall(
        body, name=f"attn_bwd_d{dil}", grid=(dil,),
        in_specs=[spec] * 6, out_specs=[spec] * 3,
        out_shape=[_sds((seg, dil * ATTN_W), F32)] * 3,
        args=(qv, kv, vv, dov, deltav, lsev), comms=comms)


def _inproj_bwd(dqs, dks, dvs, du, dvs_sgu, pos, x, dx1, w_in, g_pre, comms=()):
    def body(dq1, dq2, dq3, dk1, dk2, dk3, dv1, dv2, dv3, du_ref, dvs_ref, pos_ref, x_ref, dx1_ref, w_ref, g_ref,
             dproj_ref, gx_ref, dg_ref):
        @pl.when(pl.program_id(0) == 0)
        def _():
            dg_ref[...] = jnp.zeros_like(dg_ref)

        tabs = _rot_tables(pos_ref[...])
        dproj_ref[:, 0:512] = _rope_bwd((dq1[...] + dq2[...] + dq3[...]) * np.float32(ATTN_SCALE), tabs).astype(BF16)
        dproj_ref[:, 512:1024] = _rope_bwd(dk1[...] + dk2[...] + dk3[...], tabs).astype(BF16)
        dproj_ref[:, 1024:1536] = (dv1[...] + dv2[...] + dv3[...]).astype(BF16)
        dproj_ref[:, 1536:2048] = du_ref[...]
        dproj_ref[:, 2048:2560] = dvs_ref[...]
        dh = jnp.zeros((TM, D_MODEL), F32)
        for s in range(N_SHARD):
            dh = dh + _dot_nt(dproj_ref[:, s * IN_S:(s + 1) * IN_S], w_ref[s])
        g = g_ref[...]
        xh, r = _rms_stats(x_ref[...])
        dx, dg = _rms_bwd(xh, r, g, dh)
        dg_ref[...] += dg
        gx_ref[...] = dx1_ref[...] + dx

    return _pcall(
        body, name="inproj_bwd", grid=(SEQ // TM,),
        in_specs=[_rows(512)] * 11 + [_rows(1), _rows(D_MODEL), _rows(D_MODEL),
                                     _resident((N_SHARD, D_MODEL, IN_S)), _resident((1, D_MODEL))],
        out_specs=[_rows(PROJ_W), _rows(D_MODEL), _acc(D_MODEL)],
        out_shape=[_sds((SEQ, PROJ_W), BF16), _sds((SEQ, D_MODEL), F32), _sds((1, D_MODEL), F32)],
        args=(*dqs, *dks, *dvs, du, dvs_sgu, pos, x, dx1, w_in, g_pre), comms=comms)


def _to_view(a, dil):
    return a if dil == 1 else a.reshape(SEQ // dil, dil * a.shape[1])


def _from_view(a, dil):
    return a if dil == 1 else a.reshape(SEQ, a.shape[1] // dil)


def _local_step(x, pos, target, w_in, w_out, w_gate, w_up, w_down, small):
    b_t = small["sgu_b_spatial"].T
    h, q, k, v, u, vs = _inproj_fwd(x, pos, small["pre_mix_norm"], w_in)
    sgu = _sgu_fwd(u, vs, small["sgu_ln_gain"], small["sgu_ln_bias"], small["sgu_w_spatial"], b_t)
    views = [tuple(_to_view(t, dil) for t in (q, k, v)) for dil in DILATIONS]
    o_list, l_list = [], []
    for dil, (qv, kv, vv) in zip(DILATIONS, views):
        o, l = _attn_fwd(qv, kv, vv, dil)
        o_list.append(_from_view(o, dil))
        l_list.append(_from_view(l, dil))
    attn, lse, mixed, y, x1 = _mix_out_fwd(o_list, l_list, sgu, x, w_out, small["attn_out_norm"],
                                           small["sgu_out_norm"], small["post_mix_norm"])
    h2, a, dg, dup, df, dx1, loss_cols, d_pre_ffn, d_post_ffn = _ffn_fwd_bwd(
        x1, target, w_gate, w_up, w_down, small["pre_ffn_norm"], small["post_ffn_norm"])

    full_tok = pl.BlockSpec((SEQ, D_MODEL), lambda s: (0, 0), pipeline_mode=pl.Buffered(1))
    ff_tok = pl.BlockSpec((1, SEQ, FF_S), lambda s: (s, 0, 0))
    gw_gate = _wgrad(h2, dg, full_tok, ff_tok, (D_MODEL, FF_S), "wgrad_gate")
    gw_up = _wgrad(h2, dup, full_tok, ff_tok, (D_MODEL, FF_S), "wgrad_up")
    gw_down = _wgrad(a, df, ff_tok, full_tok, (FF_S, D_MODEL), "wgrad_down")

    dy, dattn, delta, dsgu, d_post_mix, d_attn_norm, d_sgu_norm = _outproj_bwd(
        dx1, y, attn, sgu, w_out, small["post_mix_norm"], small["attn_out_norm"], small["sgu_out_norm"])
    gw_out = _wgrad(mixed, dy, pl.BlockSpec((SEQ, OUT_S), lambda s: (0, s)), full_tok, (OUT_S, D_MODEL), "wgrad_out")
    du, dvs_sgu, d_w_sp, d_b_sp, d_ln_gain, d_ln_bias = _sgu_bwd(
        u, vs, dsgu, small["sgu_ln_gain"], small["sgu_ln_bias"], small["sgu_w_spatial"], b_t)

    dqs, dks, dvs = [], [], []
    for dil, (qv, kv, vv) in zip(DILATIONS, views):
        dq, dk, dv = _attn_bwd(qv, kv, vv, _to_view(dattn, dil), _to_view(delta, dil), _to_view(lse, dil), dil)
        dqs.append(_from_view(dq, dil))
        dks.append(_from_view(dk, dil))
        dvs.append(_from_view(dv, dil))
    dproj, grad_x, d_pre_mix = _inproj_bwd(dqs, dks, dvs, du, dvs_sgu, pos, x, dx1, w_in, small["pre_mix_norm"])
    gw_in = _wgrad(h, dproj, full_tok, pl.BlockSpec((SEQ, IN_S), lambda s: (0, s)), (D_MODEL, IN_S), "wgrad_in")

    small_grads = {
        "pre_mix_norm": d_pre_mix, "sgu_ln_gain": d_ln_gain, "sgu_ln_bias": d_ln_bias, "sgu_w_spatial": d_w_sp,
        "sgu_b_spatial": d_b_sp, "attn_out_norm": d_attn_norm, "sgu_out_norm": d_sgu_norm,
        "post_mix_norm": d_post_mix, "pre_ffn_norm": d_pre_ffn, "post_ffn_norm": d_post_ffn,
    }
    return loss_cols, grad_x, (gw_in, gw_out, gw_gate, gw_up, gw_down), small_grads


def _coords():
    return lax.axis_index("x"), lax.axis_index("y"), lax.axis_index("c")


def _other_chips(x, y):
    return [(1 - x, y), (x, 1 - y), (1 - x, 1 - y)]


def _comm_call(body, name, n_in, out_shape, scratch_shapes):
    return pl.pallas_call(
        body, name=name, in_specs=[ANY] * n_in, out_specs=[ANY] * len(out_shape), out_shape=out_shape,
        scratch_shapes=scratch_shapes,
        compiler_params=pltpu.CompilerParams(has_side_effects=True),
    )


def _gather_weights(shards):
    n = len(shards)
    halves = [s.reshape(2, s.shape[0] // 2, s.shape[1]) for s in shards]

    def body(*refs):
        ins, outs = refs[:n], refs[n:2 * n]
        send_sems, recv_sems = refs[2 * n:]
        x, y, c = _coords()
        s_me = 2 * x + y
        chips = _other_chips(x, y)
        sibling = (x, y, 1 - c)

        def copy(k, w, shard, cc, to):
            src = ins[w].at[cc] if shard is None else outs[w].at[shard, cc]
            dst = outs[w].at[s_me if shard is None else shard, cc]
            return pltpu.make_async_remote_copy(src_ref=src, dst_ref=dst, send_sem=send_sems.at[k],
                                                recv_sem=recv_sems.at[k], device_id=to, device_id_type=MESH)

        first = [copy(j * n + w, w, None, c, (cx, cy, c)) for j, (cx, cy) in enumerate(chips) for w in range(n)]
        for cp in first:
            cp.start()
        passed = []
        for j, (cx, cy) in enumerate(chips):
            for w in range(n):
                copy(j * n + w, w, 2 * cx + cy, c, (x, y, c)).wait_recv()
                fw = copy((3 + j) * n + w, w, 2 * cx + cy, c, sibling)
                fw.start()
                passed.append(fw)
        for j, (cx, cy) in enumerate(chips):
            for w in range(n):
                copy((3 + j) * n + w, w, 2 * cx + cy, 1 - c, (x, y, c)).wait_recv()
        for cp in first + passed:
            cp.wait_send()

    out_shape = [_sds((N_SHARD,) + h.shape, h.dtype) for h in halves]
    scratch = [pltpu.SemaphoreType.DMA((6 * n,)), pltpu.SemaphoreType.DMA((6 * n,))]
    full = _comm_call(body, "comm_gather_weights", n, out_shape, scratch)(*halves)
    s_me = 2 * lax.axis_index("x") + lax.axis_index("y")
    full = [lax.dynamic_update_slice(f, h[None], (s_me, 0, 0, 0)) for f, h in zip(full, halves)]
    return [f.reshape((N_SHARD,) + s.shape) for f, s in zip(full, shards)]


def _rs_to_sibling(gws):
    n = len(gws)

    def body(*refs):
        ins, outs = refs[:n], refs[n:2 * n]
        send_sems, recv_sems = refs[2 * n:]
        x, y, c = _coords()
        copies = []
        for w in range(n):
            hw = gws[w].shape[1] // 2
            copies.append(pltpu.make_async_remote_copy(
                src_ref=ins[w].at[:, pl.ds((1 - c) * hw, hw), :], dst_ref=outs[w], send_sem=send_sems.at[w],
                recv_sem=recv_sems.at[w], device_id=(x, y, 1 - c), device_id_type=MESH))
        for cp in copies:
            cp.start()
        for cp in copies:
            cp.wait()

    out_shape = [_sds((N_SHARD, g.shape[1] // 2, g.shape[2]), g.dtype) for g in gws]
    scratch = [pltpu.SemaphoreType.DMA((n,)), pltpu.SemaphoreType.DMA((n,))]
    return _comm_call(body, "comm_rs_sibling", n, out_shape, scratch)(*gws)


def _rs_chip_sum(gw, recv, core):
    _, rows, cols = gw.shape
    hw = rows // 2

    def body(c_ref, g_ref, r_ref, o_ref):
        o_ref[...] = (g_ref[...] + r_ref[...]).astype(BF16)

    return pl.pallas_call(
        body, name="rs_chip_sum",
        grid_spec=pltpu.PrefetchScalarGridSpec(
            num_scalar_prefetch=1, grid=(N_SHARD,),
            in_specs=[pl.BlockSpec((1, hw, cols), lambda s, c_ref: (s, c_ref[0], 0)),
                      pl.BlockSpec((1, hw, cols), lambda s, c_ref: (s, 0, 0))],
            out_specs=pl.BlockSpec((1, hw, cols), lambda s, c_ref: (s, 0, 0))),
        out_shape=_sds((N_SHARD, hw, cols), BF16),
        compiler_params=_seq_params(),
    )(core, gw, recv)


def _rs_between_chips(pbs):
    n = len(pbs)

    def body(*refs):
        ins, outs = refs[:n], refs[n:2 * n]
        send_sems, recv_sems = refs[2 * n:]
        x, y, c = _coords()
        copies = []
        for j, (cx, cy) in enumerate(_other_chips(x, y)):
            for w in range(n):
                copies.append(pltpu.make_async_remote_copy(
                    src_ref=ins[w].at[2 * cx + cy], dst_ref=outs[w].at[j], send_sem=send_sems.at[j * n + w],
                    recv_sem=recv_sems.at[j * n + w], device_id=(cx, cy, c), device_id_type=MESH))
        for cp in copies:
            cp.start()
        for cp in copies:
            cp.wait()

    out_shape = [_sds((3,) + p.shape[1:], p.dtype) for p in pbs]
    scratch = [pltpu.SemaphoreType.DMA((3 * n,)), pltpu.SemaphoreType.DMA((3 * n,))]
    return _comm_call(body, "comm_rs_chips", n, out_shape, scratch)(*pbs)


def _rs_final_sum(gw, recv_sib, recv_chips, shard_core):
    _, rows, cols = gw.shape
    hw = rows // 2

    def body(sc_ref, g_ref, r_ref, rc_ref, o_ref):
        acc = g_ref[0] + r_ref[0]
        for j in range(3):
            acc = acc + rc_ref[j].astype(F32)
        o_ref[0] = acc

    return pl.pallas_call(
        body, name="rs_final_sum",
        grid_spec=pltpu.PrefetchScalarGridSpec(
            num_scalar_prefetch=1, grid=(1,),
            in_specs=[pl.BlockSpec((1, hw, cols), lambda i, sc: (sc[0], sc[1], 0)),
                      pl.BlockSpec((1, hw, cols), lambda i, sc: (sc[0], 0, 0)),
                      pl.BlockSpec((3, hw, cols), lambda i, sc: (0, 0, 0))],
            out_specs=pl.BlockSpec((1, hw, cols), lambda i, sc: (sc[1], 0, 0))),
        out_shape=_sds((2, hw, cols), F32),
        compiler_params=_seq_params(),
    )(shard_core, gw, recv_sib, recv_chips)


def _rs_join_halves(halves):
    n = len(halves)

    def body(*refs):
        bufs = refs[n:2 * n]
        send_sems, recv_sems = refs[2 * n:]
        x, y, c = _coords()
        remote = [pltpu.make_async_remote_copy(
            src_ref=bufs[w].at[c], dst_ref=bufs[w].at[c], send_sem=send_sems.at[w], recv_sem=recv_sems.at[w],
            device_id=(x, y, 1 - c), device_id_type=MESH) for w in range(n)]
        for cp in remote:
            cp.start()
        for w in range(n):
            remote[w].wait_send()
            pltpu.make_async_remote_copy(
                src_ref=bufs[w].at[c], dst_ref=bufs[w].at[1 - c], send_sem=send_sems.at[w],
                recv_sem=recv_sems.at[w], device_id=(x, y, c), device_id_type=MESH).wait_recv()

    joined = pl.pallas_call(
        body, name="comm_rs_join", in_specs=[ANY] * n, out_specs=[ANY] * n,
        out_shape=[_sds(h.shape, h.dtype) for h in halves], input_output_aliases={w: w for w in range(n)},
        scratch_shapes=[pltpu.SemaphoreType.DMA((n,)), pltpu.SemaphoreType.DMA((n,))],
        compiler_params=pltpu.CompilerParams(has_side_effects=True),
    )(*halves)
    return [j.reshape(2 * h.shape[1], h.shape[2]) for j, h in zip(joined, halves)]


def _allreduce_small(buf):
    rows, cols = buf.shape

    def body(in_ref, out_ref, slots, send_sems, recv_sems):
        x, y, c = _coords()
        me = 4 * x + 2 * y + c
        copies, peers = [], []
        for k in range(1, 8):
            px = 1 - x if (k >> 2) & 1 else x
            py = 1 - y if (k >> 1) & 1 else y
            pc = 1 - c if k & 1 else c
            peers.append(4 * px + 2 * py + pc)
            copies.append(pltpu.make_async_remote_copy(
                src_ref=in_ref, dst_ref=slots.at[me], send_sem=send_sems.at[k - 1], recv_sem=recv_sems.at[k - 1],
                device_id=(px, py, pc), device_id_type=MESH))
        for cp in copies:
            cp.start()
        slots[me] = in_ref[...]
        for k in range(7):
            pltpu.make_async_remote_copy(
                src_ref=in_ref, dst_ref=slots.at[peers[k]], send_sem=send_sems.at[k], recv_sem=recv_sems.at[k],
                device_id=(x, y, c), device_id_type=MESH).wait_recv()
        for cp in copies:
            cp.wait_send()
        acc = slots[0]
        for i in range(1, 8):
            acc = acc + slots[i]
        out_ref[...] = acc

    vmem = pl.BlockSpec(memory_space=pltpu.VMEM)
    return pl.pallas_call(
        body, name="comm_allreduce_small", in_specs=[vmem], out_specs=vmem, out_shape=_sds((rows, cols), F32),
        scratch_shapes=[pltpu.VMEM((8, rows, cols), F32), pltpu.SemaphoreType.DMA((7,)), pltpu.SemaphoreType.DMA((7,))],
        compiler_params=pltpu.CompilerParams(has_side_effects=True, vmem_limit_bytes=VMEM_LIMIT),
    )(buf)


def _adamw(w, g, m, v, block_rows, name):
    rows, cols = w.shape

    def body(w_ref, g_ref, m_ref, v_ref, d_ref, nm_ref, nv_ref):
        g = g_ref[...]
        m = ADAM_B1 * m_ref[...] + (1.0 - ADAM_B1) * g
        v = ADAM_B2 * v_ref[...] + (1.0 - ADAM_B2) * (g * g)
        m_hat = m / (1.0 - ADAM_B1 ** ADAM_STEP)
        v_hat = v / (1.0 - ADAM_B2 ** ADAM_STEP)
        d_ref[...] = -ADAM_LR * (m_hat / (jnp.sqrt(v_hat) + ADAM_EPS) + ADAM_WD * w_ref[...])
        nm_ref[...] = m
        nv_ref[...] = v

    spec = pl.BlockSpec((block_rows, cols), lambda i: (i, 0))
    return pl.pallas_call(
        body, name=name, grid=(rows // block_rows,), in_specs=[spec] * 4, out_specs=[spec] * 3,
        out_shape=[_sds((rows, cols), F32)] * 3,
        compiler_params=_seq_params(),
    )(w, g, m, v)


WEIGHTS = ("pre_mix_norm", "w_in", "sgu_ln_gain", "sgu_ln_bias", "sgu_w_spatial", "sgu_b_spatial", "attn_out_norm",
           "sgu_out_norm", "w_out", "post_mix_norm", "pre_ffn_norm", "w_gate", "w_up", "w_down", "post_ffn_norm")
BIG = ("w_in", "w_out", "w_gate", "w_up", "w_down")
BIG_ADAM_ROWS = {"w_in": 256, "w_out": 128, "w_gate": 256, "w_up": 256, "w_down": 352}
SMALL = ("pre_mix_norm", "post_mix_norm", "pre_ffn_norm", "post_ffn_norm", "sgu_ln_gain", "sgu_ln_bias",
         "attn_out_norm", "sgu_out_norm", "sgu_w_spatial", "sgu_b_spatial")


def _pack_small(d):
    flat = [d[n].reshape(-1) for n in SMALL]
    used = sum(f.shape[0] for f in flat)
    flat.append(jnp.zeros((SMALL_ROWS * 1024 - used,), F32))
    return jnp.concatenate(flat).reshape(SMALL_ROWS, 1024)


def _unpack_small(buf, shapes):
    flat = buf.reshape(-1)
    out, off = {}, 0
    for n in SMALL:
        size = int(np.prod(shapes[n]))
        out[n] = flat[off:off + size].reshape(shapes[n])
        off += size
    return out


def _kernel_unoverlapped(x, positions, pre_mix_norm, w_in, sgu_ln_gain, sgu_ln_bias, sgu_w_spatial, sgu_b_spatial, attn_out_norm, sgu_out_norm, w_out, post_mix_norm, pre_ffn_norm, w_gate, w_up, w_down, post_ffn_norm, loss_target, m_pre_mix_norm, m_w_in, m_sgu_ln_gain, m_sgu_ln_bias, m_sgu_w_spatial, m_sgu_b_spatial, m_attn_out_norm, m_sgu_out_norm, m_w_out, m_post_mix_norm, m_pre_ffn_norm, m_w_gate, m_w_up, m_w_down, m_post_ffn_norm, v_pre_mix_norm, v_w_in, v_sgu_ln_gain, v_sgu_ln_bias, v_sgu_w_spatial, v_sgu_b_spatial, v_attn_out_norm, v_sgu_out_norm, v_w_out, v_post_mix_norm, v_pre_ffn_norm, v_w_gate, v_w_up, v_w_down, v_post_ffn_norm):
    a = dict(locals())
    cx, cy, cc = _coords()
    core = jnp.stack([cc]).astype(jnp.int32)
    shard_core = jnp.stack([2 * cx + cy, cc]).astype(jnp.int32)

    full = _gather_weights([a[n][0].astype(BF16) for n in BIG])
    small = {n: (a[n][0] if a[n].ndim > 2 else a[n]) for n in SMALL}
    loss_cols, grad_x, gws, small_grads = _local_step(
        x[0], positions.reshape(SEQ, 1), loss_target[0], *full, small)
    loss = lax.psum(jnp.sum(loss_cols) * np.float32(0.5 / D_MODEL), ("x", "y", "c"))

    recv_sib = _rs_to_sibling(list(gws))
    chip_part = [_rs_chip_sum(g, r, core) for g, r in zip(gws, recv_sib)]
    recv_chips = _rs_between_chips(chip_part)
    halves = [_rs_final_sum(g, r, rc, shard_core) for g, r, rc in zip(gws, recv_sib, recv_chips)]
    big_grads = dict(zip(BIG, _rs_join_halves(halves)))

    shapes = {n: a[n].shape for n in SMALL}
    small_sum = _allreduce_small(_pack_small(small_grads))

    grads, deltas, new_m, new_v = {}, {}, {}, {}
    for n in BIG:
        grads[n] = big_grads[n][None]
        d, nm, nv = _adamw(a[n][0], big_grads[n], a["m_" + n][0], a["v_" + n][0], BIG_ADAM_ROWS[n], "adamw_" + n)
        deltas[n], new_m[n], new_v[n] = d[None], nm[None], nv[None]
    d, nm, nv = _adamw(_pack_small({n: a[n] for n in SMALL}), small_sum, _pack_small({n: a["m_" + n] for n in SMALL}),
                       _pack_small({n: a["v_" + n] for n in SMALL}), SMALL_ROWS, "adamw_small")
    grads.update(_unpack_small(small_sum, shapes))
    deltas.update(_unpack_small(d, shapes))
    new_m.update(_unpack_small(nm, shapes))
    new_v.update(_unpack_small(nv, shapes))
    return (loss, grad_x[None], *[grads[n] for n in WEIGHTS], *[deltas[n] for n in WEIGHTS],
            *[new_m[n] for n in WEIGHTS], *[new_v[n] for n in WEIGHTS])


def _remote(src, dst, send_sem, recv_sem, to):
    return pltpu.make_async_remote_copy(src_ref=src, dst_ref=dst, send_sem=send_sem, recv_sem=recv_sem,
                                        device_id=to, device_id_type=MESH)


def _halves(a):
    *lead, rows, cols = a.shape
    return a.reshape(*lead, 2, rows // 2, cols)


def _gather_ici(shards):
    n = len(shards)

    def desc(ins, outs, ss, rs, j, w, landed):
        x, y, c = _coords()
        cx, cy = _other_chips(x, y)[j]
        shard = 2 * cx + cy if landed else 2 * x + y
        return _remote(ins[w].at[c], outs[w].at[shard, c], ss.at[j * n + w], rs.at[j * n + w], (cx, cy, c))

    def start(ins, outs, ss, rs):
        for j in range(3):
            for w in range(n):
                desc(ins, outs, ss, rs, j, w, False).start()

    def finish(ins, outs, ss, rs):
        for j in range(3):
            for w in range(n):
                desc(ins, outs, ss, rs, j, w, True).wait_recv()
                desc(ins, outs, ss, rs, j, w, False).wait_send()

    return _Comm(shards, [_sds((N_SHARD,) + s.shape, s.dtype) for s in shards], 3 * n, start, finish)


def _gather_pass(fulls):
    n = len(fulls)

    def desc(bufs, ss, rs, j, w, landed):
        x, y, c = _coords()
        cx, cy = _other_chips(x, y)[j]
        shard = 2 * cx + cy
        return _remote(bufs[w].at[shard, c], bufs[w].at[shard, 1 - c if landed else c],
                       ss.at[j * n + w], rs.at[j * n + w], (x, y, 1 - c))

    def start(ins, outs, ss, rs):
        for j in range(3):
            for w in range(n):
                desc(outs, ss, rs, j, w, False).start()

    def finish(ins, outs, ss, rs):
        for j in range(3):
            for w in range(n):
                desc(outs, ss, rs, j, w, True).wait_recv()
                desc(outs, ss, rs, j, w, False).wait_send()

    return _Comm(fulls, [_sds(f.shape, f.dtype) for f in fulls], 3 * n, start, finish, aliased=True)


def _rs_sibling(gws):
    n = len(gws)

    def desc(ins, outs, ss, rs, w):
        x, y, c = _coords()
        return _remote(ins[w].at[:, 1 - c], outs[w], ss.at[w], rs.at[w], (x, y, 1 - c))

    def start(ins, outs, ss, rs):
        for w in range(n):
            desc(ins, outs, ss, rs, w).start()

    def finish(ins, outs, ss, rs):
        for w in range(n):
            desc(ins, outs, ss, rs, w).wait()

    out_shape = [_sds((N_SHARD, g.shape[1] // 2, g.shape[2]), g.dtype) for g in gws]
    return _Comm([_halves(g) for g in gws], out_shape, n, start, finish)


def _rs_chips(pbs):
    n = len(pbs)

    def desc(ins, outs, ss, rs, j, w):
        x, y, c = _coords()
        cx, cy = _other_chips(x, y)[j]
        return _remote(ins[w].at[2 * cx + cy], outs[w].at[j], ss.at[j * n + w], rs.at[j * n + w], (cx, cy, c))

    def start(ins, outs, ss, rs):
        for j in range(3):
            for w in range(n):
                desc(ins, outs, ss, rs, j, w).start()

    def finish(ins, outs, ss, rs):
        for j in range(3):
            for w in range(n):
                desc(ins, outs, ss, rs, j, w).wait()

    return _Comm(pbs, [_sds((3,) + p.shape[1:], p.dtype) for p in pbs], 3 * n, start, finish)


def _rs_join(halves):
    n = len(halves)

    def desc(bufs, ss, rs, w, landed):
        x, y, c = _coords()
        return _remote(bufs[w].at[c], bufs[w].at[1 - c if landed else c], ss.at[w], rs.at[w], (x, y, 1 - c))

    def start(ins, outs, ss, rs):
        for w in range(n):
            desc(outs, ss, rs, w, False).start()

    def finish(ins, outs, ss, rs):
        for w in range(n):
            desc(outs, ss, rs, w, True).wait_recv()
            desc(outs, ss, rs, w, False).wait_send()

    return _Comm(halves, [_sds(h.shape, h.dtype) for h in halves], n, start, finish, aliased=True)


def _small_exchange(buf):
    def desc(ins, outs, ss, rs, k, landed):
        x, y, c = _coords()
        px = 1 - x if (k >> 2) & 1 else x
        py = 1 - y if (k >> 1) & 1 else y
        pc = 1 - c if k & 1 else c
        slot = 4 * px + 2 * py + pc if landed else 4 * x + 2 * y + c
        return _remote(ins[0], outs[0].at[slot], ss.at[k - 1], rs.at[k - 1], (px, py, pc))

    def start(ins, outs, ss, rs):
        for k in range(1, 8):
            desc(ins, outs, ss, rs, k, False).start()

    def finish(ins, outs, ss, rs):
        for k in range(1, 8):
            desc(ins, outs, ss, rs, k, True).wait_recv()
            desc(ins, outs, ss, rs, k, False).wait_send()

    return _Comm([buf], [_sds((8,) + buf.shape, buf.dtype)], 7, start, finish)


SMALL_EARLY = ("post_mix_norm", "pre_ffn_norm", "post_ffn_norm", "sgu_ln_gain", "sgu_ln_bias", "attn_out_norm",
               "sgu_out_norm", "sgu_w_spatial", "sgu_b_spatial")
SMALL_LATE = ("pre_mix_norm",)


def _pack(d, names, rows):
    flat = [d[n].reshape(-1) for n in names]
    used = sum(f.shape[0] for f in flat)
    flat.append(jnp.zeros((rows * 1024 - used,), F32))
    return jnp.concatenate(flat).reshape(rows, 1024)


def _unpack(buf, names, shapes):
    flat = buf.reshape(-1)
    out, off = {}, 0
    for n in names:
        size = int(np.prod(shapes[n]))
        out[n] = flat[off:off + size].reshape(shapes[n])
        off += size
    return out


def _comm_only(name, comms):
    return _pcall(lambda: None, name=name, grid=(1,), in_specs=[], out_specs=[], out_shape=[], args=(),
                  comms=comms)[1]


def _adam_math(w, g, m, v):
    m = ADAM_B1 * m + (1.0 - ADAM_B1) * g
    v = ADAM_B2 * v + (1.0 - ADAM_B2) * (g * g)
    m_hat = m / (1.0 - ADAM_B1 ** ADAM_STEP)
    v_hat = v / (1.0 - ADAM_B2 ** ADAM_STEP)
    return -ADAM_LR * (m_hat / (jnp.sqrt(v_hat) + ADAM_EPS) + ADAM_WD * w), m, v


def _adamw_small(own, slots, w, m, v, me):
    rows, cols = own.shape

    def body(me_ref, own_ref, slots_ref, w_ref, m_ref, v_ref, g_ref, d_ref, nm_ref, nv_ref):
        own_v = own_ref[...]
        g = jnp.where(me_ref[0] == 0, own_v, slots_ref[0])
        for i in range(1, 8):
            g = g + jnp.where(me_ref[0] == i, own_v, slots_ref[i])
        g_ref[...] = g
        d_ref[...], nm_ref[...], nv_ref[...] = _adam_math(w_ref[...], g, m_ref[...], v_ref[...])

    flat = pl.BlockSpec((rows, cols), lambda i, me_ref: (0, 0))
    return pl.pallas_call(
        body, name="adamw_small",
        grid_spec=pltpu.PrefetchScalarGridSpec(
            num_scalar_prefetch=1, grid=(1,),
            in_specs=[flat, pl.BlockSpec((8, rows, cols), lambda i, me_ref: (0, 0, 0)), flat, flat, flat],
            out_specs=[flat] * 4),
        out_shape=[_sds((rows, cols), F32)] * 4,
        compiler_params=_seq_params(),
    )(me, own, slots, w, m, v)


def kernel(x, positions, pre_mix_norm, w_in, sgu_ln_gain, sgu_ln_bias, sgu_w_spatial, sgu_b_spatial, attn_out_norm, sgu_out_norm, w_out, post_mix_norm, pre_ffn_norm, w_gate, w_up, w_down, post_ffn_norm, loss_target, m_pre_mix_norm, m_w_in, m_sgu_ln_gain, m_sgu_ln_bias, m_sgu_w_spatial, m_sgu_b_spatial, m_attn_out_norm, m_sgu_out_norm, m_w_out, m_post_mix_norm, m_pre_ffn_norm, m_w_gate, m_w_up, m_w_down, m_post_ffn_norm, v_pre_mix_norm, v_w_in, v_sgu_ln_gain, v_sgu_ln_bias, v_sgu_w_spatial, v_sgu_b_spatial, v_attn_out_norm, v_sgu_out_norm, v_w_out, v_post_mix_norm, v_pre_ffn_norm, v_w_gate, v_w_up, v_w_down, v_post_ffn_norm):
    a = dict(locals())
    cx, cy, cc = _coords()
    s_me = 2 * cx + cy
    core = jnp.stack([cc]).astype(jnp.int32)
    shard_core = jnp.stack([s_me, cc]).astype(jnp.int32)
    me = jnp.stack([4 * cx + 2 * cy + cc]).astype(jnp.int32)
    small = {n: (a[n][0] if a[n].ndim > 2 else a[n]) for n in SMALL}
    b_t = small["sgu_b_spatial"].T
    xs, pos, target = x[0], positions.reshape(SEQ, 1), loss_target[0]
    own = {n: _halves(a[n][0].astype(BF16)) for n in BIG}

    def with_own(full, n):
        full = lax.dynamic_update_slice(full, own[n][None], (s_me, 0, 0, 0))
        return full.reshape((N_SHARD,) + a[n].shape[1:])

    (w_in_f,) = _gather_weights([a["w_in"][0].astype(BF16)])
    (h, q, k, v, u, vs), ((l_out,),) = _inproj_fwd(xs, pos, small["pre_mix_norm"], w_in_f,
                                                   comms=[_gather_ici([own["w_out"]])])
    views = [tuple(_to_view(t, dil) for t in (q, k, v)) for dil in DILATIONS]
    rider = {1: "w_gate", 4: "w_up", 16: "w_down"}
    landed, o_list, l_list = {}, [], []
    for dil, (qv, kv, vv) in zip(DILATIONS, views):
        (o, l), ((landed[rider[dil]],),) = _attn_fwd(qv, kv, vv, dil, comms=[_gather_ici([own[rider[dil]]])])
        o_list.append(_from_view(o, dil))
        l_list.append(_from_view(l, dil))
    sgu, ((l_out,),) = _sgu_fwd(u, vs, small["sgu_ln_gain"], small["sgu_ln_bias"], small["sgu_w_spatial"], b_t,
                                comms=[_gather_pass([l_out])])
    w_out_f = with_own(l_out, "w_out")
    (attn, lse, mixed, y, x1), (ffn_w,) = _mix_out_fwd(
        o_list, l_list, sgu, xs, w_out_f, small["attn_out_norm"], small["sgu_out_norm"],
        small["post_mix_norm"], comms=[_gather_pass([landed["w_gate"], landed["w_up"], landed["w_down"]])])
    w_gate_f, w_up_f, w_down_f = (with_own(f, n) for f, n in zip(ffn_w, ("w_gate", "w_up", "w_down")))
    h2, act, dg, dup, df, dx1, loss_cols, d_pre_ffn, d_post_ffn = _ffn_fwd_bwd(
        x1, target, w_gate_f, w_up_f, w_down_f, small["pre_ffn_norm"], small["post_ffn_norm"])
    loss = lax.psum(jnp.sum(loss_cols) * np.float32(0.5 / D_MODEL), ("x", "y", "c"))

    full_tok = pl.BlockSpec((SEQ, D_MODEL), lambda s: (0, 0), pipeline_mode=pl.Buffered(1))
    ff_tok = pl.BlockSpec((1, SEQ, FF_S), lambda s: (s, 0, 0))
    gw = {}
    gw["w_gate"] = _wgrad(h2, dg, full_tok, ff_tok, (D_MODEL, FF_S), "wgrad_gate")
    gw["w_up"], ((sib_gate,),) = _wgrad(h2, dup, full_tok, ff_tok, (D_MODEL, FF_S), "wgrad_up",
                                        comms=[_rs_sibling([gw["w_gate"]])])
    gw["w_down"], ((sib_up,),) = _wgrad(act, df, ff_tok, full_tok, (FF_S, D_MODEL), "wgrad_down",
                                        comms=[_rs_sibling([gw["w_up"]])])
    (dy, dattn, delta, dsgu, d_post_mix, d_attn_norm, d_sgu_norm), ((sib_down,),) = _outproj_bwd(
        dx1, y, attn, sgu, w_out_f, small["post_mix_norm"], small["attn_out_norm"],
        small["sgu_out_norm"], comms=[_rs_sibling([gw["w_down"]])])
    sib = {"w_gate": sib_gate, "w_up": sib_up, "w_down": sib_down}
    part = {n: _rs_chip_sum(gw[n], sib[n], core) for n in ("w_gate", "w_up", "w_down")}
    gw["w_out"] = _wgrad(mixed, dy, pl.BlockSpec((SEQ, OUT_S), lambda s: (0, s)), full_tok, (OUT_S, D_MODEL),
                         "wgrad_out")
    (du, dvs_sgu, d_w_sp, d_b_sp, d_ln_gain, d_ln_bias), ((far_gate,), (sib["w_out"],)) = _sgu_bwd(
        u, vs, dsgu, small["sgu_ln_gain"], small["sgu_ln_bias"], small["sgu_w_spatial"], b_t,
        comms=[_rs_chips([part["w_gate"]]), _rs_sibling([gw["w_out"]])])
    part["w_out"] = _rs_chip_sum(gw["w_out"], sib["w_out"], core)
    half, far, joined = {}, {"w_gate": far_gate}, {}
    half["w_gate"] = _rs_final_sum(gw["w_gate"], sib["w_gate"], far["w_gate"], shard_core)
    packed_early = _pack({
        "sgu_ln_gain": d_ln_gain, "sgu_ln_bias": d_ln_bias, "sgu_w_spatial": d_w_sp, "sgu_b_spatial": d_b_sp,
        "attn_out_norm": d_attn_norm, "sgu_out_norm": d_sgu_norm, "post_mix_norm": d_post_mix,
        "pre_ffn_norm": d_pre_ffn, "post_ffn_norm": d_post_ffn}, SMALL_EARLY, SMALL_ROWS)

    dqs, dks, dvs = [], [], []
    for dil, (qv, kv, vv) in zip(DILATIONS, views):
        if dil == 1:
            riders = [_rs_chips([part["w_up"], part["w_down"]])]
        elif dil == 4:
            riders = [_rs_chips([part["w_out"]]), _rs_join([half["w_gate"]]), _small_exchange(packed_early)]
        else:
            riders = [_rs_join([half["w_up"], half["w_down"]])]
        (dq, dk, dv), got = _attn_bwd(qv, kv, vv, _to_view(dattn, dil), _to_view(delta, dil), _to_view(lse, dil),
                                      dil, comms=riders)
        if dil == 1:
            far["w_up"], far["w_down"] = got[0]
            for n in ("w_up", "w_down"):
                half[n] = _rs_final_sum(gw[n], sib[n], far[n], shard_core)
        elif dil == 4:
            (far["w_out"],), (joined["w_gate"],), (slots_early,) = got
            half["w_out"] = _rs_final_sum(gw["w_out"], sib["w_out"], far["w_out"], shard_core)
        else:
            joined["w_up"], joined["w_down"] = got[0]
        dqs.append(_from_view(dq, dil))
        dks.append(_from_view(dk, dil))
        dvs.append(_from_view(dv, dil))
    (dproj, grad_x, d_pre_mix), ((joined["w_out"],),) = _inproj_bwd(
        dqs, dks, dvs, du, dvs_sgu, pos, xs, dx1, w_in_f, small["pre_mix_norm"], comms=[_rs_join([half["w_out"]])])
    packed_late = _pack({"pre_mix_norm": d_pre_mix}, SMALL_LATE, 8)
    gw["w_in"], ((slots_late,),) = _wgrad(h, dproj, full_tok, pl.BlockSpec((SEQ, IN_S), lambda s: (0, s)),
                                          (D_MODEL, IN_S), "wgrad_in", comms=[_small_exchange(packed_late)])
    ((sib["w_in"],),) = _comm_only("comm_rs_sibling_in", [_rs_sibling([gw["w_in"]])])
    part["w_in"] = _rs_chip_sum(gw["w_in"], sib["w_in"], core)
    ((far["w_in"],),) = _comm_only("comm_rs_chips_in", [_rs_chips([part["w_in"]])])
    half["w_in"] = _rs_final_sum(gw["w_in"], sib["w_in"], far["w_in"], shard_core)
    ((joined["w_in"],),) = _comm_only("comm_rs_join_in", [_rs_join([half["w_in"]])])

    grads, deltas, new_m, new_v = {}, {}, {}, {}
    for n in BIG:
        g = joined[n].reshape(a[n].shape[1:])
        grads[n] = g[None]
        d, nm, nv = _adamw(a[n][0], g, a["m_" + n][0], a["v_" + n][0], BIG_ADAM_ROWS[n], "adamw_" + n)
        deltas[n], new_m[n], new_v[n] = d[None], nm[None], nv[None]
    for names, rows, packed, slots in ((SMALL_EARLY, SMALL_ROWS, packed_early, slots_early),
                                       (SMALL_LATE, 8, packed_late, slots_late)):
        outs = _adamw_small(packed, slots, _pack(a, names, rows), _pack({n: a["m_" + n] for n in names}, names, rows),
                            _pack({n: a["v_" + n] for n in names}, names, rows), me)
        for dst, buf in zip((grads, deltas, new_m, new_v), outs):
            dst.update(_unpack(buf, names, {n: a[n].shape for n in names}))
    return (loss, grad_x[None], *[grads[n] for n in WEIGHTS], *[deltas[n] for n in WEIGHTS],
            *[new_m[n] for n in WEIGHTS], *[new_v[n] for n in WEIGHTS])
```

```python
import numpy as np
import jax
import jax.numpy as jnp
from jax import lax
from jax.experimental import pallas as pl
from jax.experimental.pallas import tpu as pltpu

F32 = jnp.float32
BF16 = jnp.bfloat16

SEQ = 2048
D_MODEL = 1024
HEAD_DIM = 64
ATTN_W = 512
SGU_W = 512
SGU_GROUPS = 8
CHUNK = 128
DILATIONS = (1, 4, 16)
N_SHARD = 4
IN_S = 640
OUT_S = 256
FF_S = 704
PROJ_W = N_SHARD * IN_S
RMS_EPS = 1e-6
LN_EPS = 1e-5
ROPE_THETA = 500000.0
ATTN_SCALE = 1.0 / np.sqrt(HEAD_DIM)
NEG = -1e30
TM = 256
VMEM_LIMIT = 56 * 1024 * 1024
SMALL_ROWS = 136

ADAM_LR = 0.001
ADAM_B1 = 0.9
ADAM_B2 = 0.999
ADAM_EPS = 1e-08
ADAM_WD = 0.01
ADAM_STEP = 10

MESH = pl.DeviceIdType.MESH
ANY = pl.BlockSpec(memory_space=pl.ANY)


def _dot(a, b):
    return jnp.dot(a, b, preferred_element_type=F32)


def _dot_nt(a, b):
    return lax.dot_general(a, b, (((1,), (1,)), ((), ())), preferred_element_type=F32)


def _dot_tn(a, b):
    return lax.dot_general(a, b, (((0,), (0,)), ((), ())), preferred_element_type=F32)


def _dot_exact(a, b):
    return jnp.dot(a, b, preferred_element_type=F32, precision=lax.Precision.HIGHEST)


def _rms_stats(x):
    r = lax.rsqrt(jnp.mean(x * x, axis=-1, keepdims=True) + RMS_EPS)
    return x * r, r


def _rms_bwd(xh, r, gain, dy):
    dxh = dy * gain
    dx = r * (dxh - xh * jnp.mean(dxh * xh, axis=-1, keepdims=True))
    return dx, jnp.sum(dy * xh, axis=0, keepdims=True)


_ERF_ALPHA = (-2.72614225801306e-10, 2.77068142495902e-08, -2.10102402082508e-06, -5.69250639462346e-05,
              -7.34990630326855e-04, -2.95459980854025e-03, -1.60960333262415e-02)
_ERF_BETA = (-1.45660718464996e-05, -2.13374055278905e-04, -1.68282697438203e-03, -7.37332916720468e-03,
             -1.42647390514189e-02)


def _erf(x):
    x = jnp.clip(x, -4.0, 4.0)
    x2 = x * x
    p = jnp.full_like(x, _ERF_ALPHA[0])
    for a in _ERF_ALPHA[1:]:
        p = p * x2 + a
    q = jnp.full_like(x, _ERF_BETA[0])
    for b in _ERF_BETA[1:]:
        q = q * x2 + b
    return x * p / q


def _gelu(x):
    return 0.5 * x * (1.0 + _erf(x * np.float32(1.0 / np.sqrt(2.0))))


def _gelu_grad(x):
    cdf = 0.5 * (1.0 + _erf(x * np.float32(1.0 / np.sqrt(2.0))))
    pdf = jnp.exp(-0.5 * x * x) * np.float32(1.0 / np.sqrt(2.0 * np.pi))
    return cdf + x * pdf


def _sigmoid(x):
    return 1.0 / (1.0 + jnp.exp(-x))


_INV_FREQ = tuple(float(np.float32(ROPE_THETA ** (-2.0 * j / 16.0))) for j in range(8))


def _rot_tables(pos):
    lane = lax.broadcasted_iota(jnp.int32, (1, 128), 1)
    d = lane & 63
    j = d & 7
    inv = jnp.zeros((1, 128), F32)
    for jj in range(8):
        inv = jnp.where(j == jj, _INV_FREQ[jj], inv)
    ang = pos.astype(F32) * inv
    c = jnp.cos(ang)
    s = jnp.sin(ang)
    cos_t = jnp.where(d < 16, c, 1.0)
    sin_a = jnp.where(d < 8, -s, 0.0)
    sin_b = jnp.where((d >= 8) & (d < 16), s, 0.0)
    return tuple(jnp.tile(t, (1, 4)) for t in (cos_t, sin_a, sin_b))


def _rope(x, tabs):
    cos_t, sin_a, sin_b = tabs
    return x * cos_t + pltpu.roll(x, 504, 1) * sin_a + pltpu.roll(x, 8, 1) * sin_b


def _rope_bwd(dy, tabs):
    cos_t, sin_a, sin_b = tabs
    return dy * cos_t + pltpu.roll(dy * sin_a, 8, 1) + pltpu.roll(dy * sin_b, 504, 1)


def _left_half():
    return lax.broadcasted_iota(jnp.int32, (CHUNK, CHUNK), 1) < HEAD_DIM


def _group_ones():
    lane = lax.broadcasted_iota(jnp.int32, (SGU_GROUPS, SGU_W), 1)
    row = lax.broadcasted_iota(jnp.int32, (SGU_GROUPS, SGU_W), 0)
    return ((lane >> 6) == row).astype(F32)


def _masked_spatial(w_ref):
    row = lax.broadcasted_iota(jnp.int32, (CHUNK, CHUNK), 0)
    col = lax.broadcasted_iota(jnp.int32, (CHUNK, CHUNK), 1)
    return [jnp.where(col <= row, w_ref[g], 0.0).astype(BF16) for g in range(SGU_GROUPS)]


def _sgu_core(u, vs, lg, lb, wm, bias_full):
    tm = u.shape[0]
    gu = _gelu(u)
    gv = _gelu(vs)
    mu = jnp.mean(gv, axis=-1, keepdims=True)
    xc = gv - mu
    rstd = lax.rsqrt(jnp.mean(xc * xc, axis=-1, keepdims=True) + LN_EPS)
    xh = xc * rstd
    vnb = (xh * lg + lb).astype(BF16)
    left = _left_half()
    rows = []
    for c in range(tm // CHUNK):
        pieces = []
        for p in range(4):
            vp = vnb[c * CHUNK:(c + 1) * CHUNK, p * 128:(p + 1) * 128]
            pieces.append(jnp.where(left, _dot(wm[2 * p], vp), _dot(wm[2 * p + 1], vp)))
        rows.append(jnp.concatenate(pieces, axis=1) + bias_full)
    mixed = jnp.concatenate(rows, axis=0)
    return gu, xh, rstd, vnb, mixed


def _resident(shape):
    n = len(shape)
    return pl.BlockSpec(shape, lambda *_: (0,) * n, pipeline_mode=pl.Buffered(1))


def _rows(ncol, tm=TM):
    return pl.BlockSpec((tm, ncol), lambda i: (i, 0))


def _rows3(nlead, ncol, tm=TM):
    return pl.BlockSpec((nlead, tm, ncol), lambda i: (0, i, 0))


def _acc(ncol, nrow=1):
    return pl.BlockSpec((nrow, ncol), lambda i: (0, 0))


def _seq_params():
    return pltpu.CompilerParams(dimension_semantics=("arbitrary",), vmem_limit_bytes=VMEM_LIMIT)


def _sds(shape, dtype):
    return jax.ShapeDtypeStruct(shape, dtype)


class _Comm:
    def __init__(self, args, out_shape, n_sems, start, finish, aliased=False):
        self.args, self.out_shape, self.n_sems = list(args), list(out_shape), n_sems
        self.start, self.finish, self.aliased = start, finish, aliased


def _pcall(body, *, name, grid, in_specs, out_specs, out_shape, args, scratch_shapes=(), comms=()):
    single = not isinstance(out_shape, (list, tuple))
    out_specs = [out_specs] if single else list(out_specs)
    out_shape = [out_shape] if single else list(out_shape)
    n_in, n_out, n_scr = len(in_specs), len(out_shape), len(scratch_shapes)
    c_args = [a for c in comms for a in c.args]
    c_outs = [o for c in comms for o in c.out_shape]
    aliases, ai, ao = {}, n_in, n_out
    for c in comms:
        if c.aliased:
            aliases.update({ai + k: ao + k for k in range(len(c.args))})
        ai += len(c.args)
        ao += len(c.out_shape)
    sems = [pltpu.SemaphoreType.DMA((c.n_sems,)) for c in comms for _ in range(2)]
    steps = grid[0]

    def wrapped(*refs):
        o0 = n_in + len(c_args)
        s0 = o0 + n_out + len(c_outs)
        m_in, m_out, m_sem = refs[n_in:o0], refs[o0 + n_out:s0], refs[s0 + n_scr:]

        def each(phase):
            ii = oi = 0
            for k, c in enumerate(comms):
                getattr(c, phase)(m_in[ii:ii + len(c.args)], m_out[oi:oi + len(c.out_shape)],
                                  m_sem[2 * k], m_sem[2 * k + 1])
                ii += len(c.args)
                oi += len(c.out_shape)

        if comms:
            @pl.when(pl.program_id(0) == 0)
            def _():
                each("start")

        body(*refs[:n_in], *refs[o0:o0 + n_out], *refs[s0:s0 + n_scr])

        if comms:
            @pl.when(pl.program_id(0) == steps - 1)
            def _():
                each("finish")

    res = pl.pallas_call(
        wrapped, name=name, grid=grid,
        in_specs=list(in_specs) + [ANY] * len(c_args), out_specs=out_specs + [ANY] * len(c_outs),
        out_shape=out_shape + c_outs, scratch_shapes=list(scratch_shapes) + sems,
        input_output_aliases=aliases, compiler_params=_seq_params(),
    )(*args, *c_args)
    mine = res[0] if single else list(res[:n_out])
    if not comms:
        return mine
    theirs, oi = [], n_out
    for c in comms:
        theirs.append(list(res[oi:oi + len(c.out_shape)]))
        oi += len(c.out_shape)
    return mine, theirs


def _inproj_fwd(x, pos, g_pre, w_in, comms=()):
    def body(x_ref, pos_ref, g_ref, w_ref, h_ref, q_ref, k_ref, v_ref, u_ref, vs_ref):
        xh, _ = _rms_stats(x_ref[...])
        h = (xh * g_ref[...]).astype(BF16)
        h_ref[...] = h
        proj = jnp.concatenate([_dot(h, w_ref[s]) for s in range(N_SHARD)], axis=1)
        tabs = _rot_tables(pos_ref[...])
        q_ref[...] = (_rope(proj[:, 0:512], tabs) * np.float32(ATTN_SCALE)).astype(BF16)
        k_ref[...] = _rope(proj[:, 512:1024], tabs).astype(BF16)
        v_ref[...] = proj[:, 1024:1536].astype(BF16)
        u_ref[...] = proj[:, 1536:2048]
        vs_ref[...] = proj[:, 2048:2560]

    return _pcall(
        body, name="inproj_fwd", grid=(SEQ // TM,),
        in_specs=[_rows(D_MODEL), _rows(1), _resident((1, D_MODEL)), _resident((N_SHARD, D_MODEL, IN_S))],
        out_specs=[_rows(D_MODEL), _rows(512), _rows(512), _rows(512), _rows(512), _rows(512)],
        out_shape=[_sds((SEQ, D_MODEL), BF16), _sds((SEQ, 512), BF16), _sds((SEQ, 512), BF16),
                   _sds((SEQ, 512), BF16), _sds((SEQ, 512), F32), _sds((SEQ, 512), F32)],
        args=(x, pos, g_pre, w_in), comms=comms)


def _sgu_fwd(u, vs, lg, lb, w_sp, b_t, comms=()):
    def body(u_ref, vs_ref, lg_ref, lb_ref, w_ref, bt_ref, out_ref):
        wm = _masked_spatial(w_ref)
        bias_full = _dot_exact(bt_ref[...], _group_ones())
        gu, _, _, _, mixed = _sgu_core(u_ref[...], vs_ref[...], lg_ref[...], lb_ref[...], wm, bias_full)
        out_ref[...] = gu * mixed

    return _pcall(
        body, name="sgu_fwd", grid=(SEQ // TM,),
        in_specs=[_rows(SGU_W), _rows(SGU_W), _resident((1, SGU_W)), _resident((1, SGU_W)),
                  _resident((SGU_GROUPS, CHUNK, CHUNK)), _resident((CHUNK, SGU_GROUPS))],
        out_specs=_rows(SGU_W),
        out_shape=_sds((SEQ, SGU_W), F32),
        args=(u, vs, lg, lb, w_sp, b_t), comms=comms)


def _block_masks():
    row = lax.broadcasted_iota(jnp.int32, (CHUNK, CHUNK), 0)
    col = lax.broadcasted_iota(jnp.int32, (CHUNK, CHUNK), 1)
    return col <= row, col >= row


def _attn_fwd(qv, kv, vv, dil, comms=()):
    seg = SEQ // dil
    nblk = seg // CHUNK

    def body(q_ref, k_ref, v_ref, o_ref, l_ref):
        left = _left_half()
        m_cur, m_prev = _block_masks()
        zero = jnp.zeros((CHUNK, CHUNK), BF16)
        ones = (jnp.where(left, 1.0, 0.0).astype(BF16), jnp.where(left, 0.0, 1.0).astype(BF16))

        def blk(b, carry):
            r0 = pl.multiple_of(b * CHUNK, CHUNK)
            rp = pl.multiple_of(jnp.maximum(b - 1, 0) * CHUNK, CHUNK)
            prev_ok = m_prev & (b > 0)
            sides = tuple(enumerate((left, ~left)))
            tiles, scores = [], []
            for hp in range(4):
                ls = slice(hp * 128, (hp + 1) * 128)
                qp = q_ref[pl.ds(r0, CHUNK), ls]
                kc = k_ref[pl.ds(r0, CHUNK), ls]
                kp = k_ref[pl.ds(rp, CHUNK), ls] if nblk > 1 else None
                tiles.append((ls, v_ref[pl.ds(r0, CHUNK), ls], v_ref[pl.ds(rp, CHUNK), ls] if nblk > 1 else None))
                for _, hm in sides:
                    qh = jnp.where(hm, qp, zero)
                    sc = jnp.where(m_cur, _dot_nt(qh, kc), NEG)
                    sp = jnp.where(prev_ok, _dot_nt(qh, kp), NEG) if nblk > 1 else None
                    scores.append((sc, sp))
            probs = []
            for sc, sp in scores:
                if nblk > 1:
                    m = jnp.max(jnp.maximum(sc, sp), axis=-1, keepdims=True)
                    pc = jnp.exp(sc - m)
                    pp = jnp.exp(sp - m)
                    probs.append((m, pc.astype(BF16), pp.astype(BF16), (pc + pp).astype(BF16)))
                else:
                    m = jnp.max(sc, axis=-1, keepdims=True)
                    pc = jnp.exp(sc - m).astype(BF16)
                    probs.append((m, pc, None, pc))
            for hp, (ls, vc, vp) in enumerate(tiles):
                acc = jnp.zeros((CHUNK, CHUNK), F32)
                den = jnp.zeros((CHUNK, CHUNK), F32)
                for side, hm in sides:
                    _, pc, pp, psum = probs[2 * hp + side]
                    acc = acc + _dot(pc, jnp.where(hm, vc, zero))
                    if nblk > 1:
                        acc = acc + _dot(pp, jnp.where(hm, vp, zero))
                    den = den + _dot(psum, ones[side])
                o_ref[pl.ds(r0, CHUNK), ls] = acc / den
                l_ref[pl.ds(r0, CHUNK), ls] = jnp.where(left, probs[2 * hp][0], probs[2 * hp + 1][0]) + jnp.log(den)
            return carry

        lax.fori_loop(0, nblk, blk, 0)

    spec = pl.BlockSpec((seg, ATTN_W), lambda r: (0, r))
    return _pcall(
        body, name=f"attn_fwd_d{dil}", grid=(dil,),
        in_specs=[spec, spec, spec], out_specs=[spec, spec],
        out_shape=[_sds((seg, dil * ATTN_W), F32), _sds((seg, dil * ATTN_W), F32)],
        args=(qv, kv, vv), comms=comms)


def _mix_out_fwd(o_list, l_list, sgu, x, w_out, g_attn, g_sgu, g_post, comms=()):
    def body(o1, o2, o3, l1, l2, l3, sgu_ref, x_ref, w_ref, ga_ref, gs_ref, gp_ref,
             attn_ref, lse_ref, mixed_ref, y_ref, x1_ref):
        ls = [l1[...], l2[...], l3[...]]
        m = jnp.maximum(jnp.maximum(ls[0], ls[1]), ls[2])
        es = [jnp.exp(l - m) for l in ls]
        den = es[0] + es[1] + es[2]
        attn = (es[0] * o1[...] + es[1] * o2[...] + es[2] * o3[...]) / den
        attn_ref[...] = attn
        lse_ref[...] = m + jnp.log(den)
        ah, _ = _rms_stats(attn)
        sh, _ = _rms_stats(sgu_ref[...])
        mixed = jnp.concatenate([ah * ga_ref[...], sh * gs_ref[...]], axis=1).astype(BF16)
        mixed_ref[...] = mixed
        y = _dot(mixed[:, 0:OUT_S], w_ref[0])
        for s in range(1, N_SHARD):
            y = y + _dot(mixed[:, s * OUT_S:(s + 1) * OUT_S], w_ref[s])
        y_ref[...] = y
        yh, _ = _rms_stats(y)
        x1_ref[...] = x_ref[...] + yh * gp_ref[...]

    return _pcall(
        body, name="mix_out_fwd", grid=(SEQ // TM,),
        in_specs=[_rows(512)] * 7 + [_rows(D_MODEL), _resident((N_SHARD, OUT_S, D_MODEL)),
                                    _resident((1, 512)), _resident((1, 512)), _resident((1, D_MODEL))],
        out_specs=[_rows(512), _rows(512), _rows(D_MODEL), _rows(D_MODEL), _rows(D_MODEL)],
        out_shape=[_sds((SEQ, 512), F32), _sds((SEQ, 512), F32), _sds((SEQ, D_MODEL), BF16),
                   _sds((SEQ, D_MODEL), F32), _sds((SEQ, D_MODEL), F32)],
        args=(*o_list, *l_list, sgu, x, w_out, g_attn, g_sgu, g_post), comms=comms)


def _ffn_fwd_bwd(x1, target, w_gate, w_up, w_down, g_pre, g_post, comms=()):
    def body(x1_ref, t_ref, wg_ref, wu_ref, wd_ref, gpf_ref, gpo_ref,
             h2_ref, a_ref, dg_ref, dup_ref, df_ref, dx1_ref, loss_ref, dgpf_ref, dgpo_ref, g_scr, up_scr):
        @pl.when(pl.program_id(0) == 0)
        def _():
            loss_ref[...] = jnp.zeros_like(loss_ref)
            dgpf_ref[...] = jnp.zeros_like(dgpf_ref)
            dgpo_ref[...] = jnp.zeros_like(dgpo_ref)

        x1 = x1_ref[...]
        gpf = gpf_ref[...]
        gpo = gpo_ref[...]
        xh, r = _rms_stats(x1)
        h2 = (xh * gpf).astype(BF16)
        h2_ref[...] = h2
        f = jnp.zeros((TM, D_MODEL), F32)
        for s in range(N_SHARD):
            g = _dot(h2, wg_ref[s])
            up = _dot(h2, wu_ref[s])
            g_scr[s] = g
            up_scr[s] = up
            a = (g * _sigmoid(g) * up).astype(BF16)
            a_ref[s] = a
            f = f + _dot(a, wd_ref[s])
        fh, rf = _rms_stats(f)
        diff = x1 + fh * gpo - t_ref[...]
        loss_ref[...] += jnp.sum(diff * diff, axis=0, keepdims=True)
        dout = diff * np.float32(1.0 / D_MODEL)
        df, dgpo = _rms_bwd(fh, rf, gpo, dout)
        dgpo_ref[...] += dgpo
        dfb = df.astype(BF16)
        df_ref[...] = dfb
        dh2 = jnp.zeros((TM, D_MODEL), F32)
        for s in range(N_SHARD):
            da = _dot_nt(dfb, wd_ref[s])
            g = g_scr[s]
            up = up_scr[s]
            sg = _sigmoid(g)
            dup = (da * (g * sg)).astype(BF16)
            dg = (da * up * (sg * (1.0 + g * (1.0 - sg)))).astype(BF16)
            dg_ref[s] = dg
            dup_ref[s] = dup
            dh2 = dh2 + _dot_nt(dg, wg_ref[s]) + _dot_nt(dup, wu_ref[s])
        dx, dgpf = _rms_bwd(xh, r, gpf, dh2)
        dgpf_ref[...] += dgpf
        dx1_ref[...] = dout + dx

    return _pcall(
        body, name="ffn_fwd_bwd", grid=(SEQ // TM,),
        in_specs=[_rows(D_MODEL), _rows(D_MODEL), _resident((N_SHARD, D_MODEL, FF_S)),
                  _resident((N_SHARD, D_MODEL, FF_S)), _resident((N_SHARD, FF_S, D_MODEL)),
                  _resident((1, D_MODEL)), _resident((1, D_MODEL))],
        out_specs=[_rows(D_MODEL), _rows3(N_SHARD, FF_S), _rows3(N_SHARD, FF_S), _rows3(N_SHARD, FF_S),
                   _rows(D_MODEL), _rows(D_MODEL), _acc(D_MODEL), _acc(D_MODEL), _acc(D_MODEL)],
        out_shape=[_sds((SEQ, D_MODEL), BF16), _sds((N_SHARD, SEQ, FF_S), BF16), _sds((N_SHARD, SEQ, FF_S), BF16),
                   _sds((N_SHARD, SEQ, FF_S), BF16), _sds((SEQ, D_MODEL), BF16), _sds((SEQ, D_MODEL), F32),
                   _sds((1, D_MODEL), F32), _sds((1, D_MODEL), F32), _sds((1, D_MODEL), F32)],
        scratch_shapes=[pltpu.VMEM((N_SHARD, TM, FF_S), F32), pltpu.VMEM((N_SHARD, TM, FF_S), F32)],
        args=(x1, target, w_gate, w_up, w_down, g_pre, g_post), comms=comms)


def _wgrad(a, b, a_spec, b_spec, out_block, name, comms=()):
    def body(a_ref, b_ref, o_ref):
        av = a_ref[0] if len(a_ref.shape) == 3 else a_ref[...]
        bv = b_ref[0] if len(b_ref.shape) == 3 else b_ref[...]
        o_ref[0] = _dot_tn(av, bv)

    return _pcall(
        body, name=name, grid=(N_SHARD,),
        in_specs=[a_spec, b_spec],
        out_specs=pl.BlockSpec((1,) + out_block, lambda s: (s, 0, 0)),
        out_shape=_sds((N_SHARD,) + out_block, F32),
        args=(a, b), comms=comms)


def _outproj_bwd(dx1, y, attn, sgu, w_out, g_post, g_attn, g_sgu, comms=()):
    def body(dx1_ref, y_ref, attn_ref, sgu_ref, w_ref, gp_ref, ga_ref, gs_ref,
             dy_ref, dattn_ref, delta_ref, dsgu_ref, dgp_ref, dga_ref, dgs_ref):
        @pl.when(pl.program_id(0) == 0)
        def _():
            dgp_ref[...] = jnp.zeros_like(dgp_ref)
            dga_ref[...] = jnp.zeros_like(dga_ref)
            dgs_ref[...] = jnp.zeros_like(dgs_ref)

        yh, ry = _rms_stats(y_ref[...])
        dy, dgp = _rms_bwd(yh, ry, gp_ref[...], dx1_ref[...])
        dgp_ref[...] += dgp
        dyb = dy.astype(BF16)
        dy_ref[...] = dyb
        dmixed = jnp.concatenate([_dot_nt(dyb, w_ref[s]) for s in range(N_SHARD)], axis=1)
        attn = attn_ref[...]
        ah, ra = _rms_stats(attn)
        dattn, dga = _rms_bwd(ah, ra, ga_ref[...], dmixed[:, 0:512])
        dga_ref[...] += dga
        sh, rs = _rms_stats(sgu_ref[...])
        dsgu, dgs = _rms_bwd(sh, rs, gs_ref[...], dmixed[:, 512:1024])
        dgs_ref[...] += dgs
        dattn_ref[...] = dattn.astype(BF16)
        dsgu_ref[...] = dsgu
        la = lax.broadcasted_iota(jnp.int32, (ATTN_W, ATTN_W), 0) >> 6
        lb = lax.broadcasted_iota(jnp.int32, (ATTN_W, ATTN_W), 1) >> 6
        delta_ref[...] = _dot_exact(dattn * attn, (la == lb).astype(F32))

    return _pcall(
        body, name="outproj_bwd", grid=(SEQ // TM,),
        in_specs=[_rows(D_MODEL), _rows(D_MODEL), _rows(512), _rows(512), _resident((N_SHARD, OUT_S, D_MODEL)),
                  _resident((1, D_MODEL)), _resident((1, 512)), _resident((1, 512))],
        out_specs=[_rows(D_MODEL), _rows(512), _rows(512), _rows(512), _acc(D_MODEL), _acc(512), _acc(512)],
        out_shape=[_sds((SEQ, D_MODEL), BF16), _sds((SEQ, 512), BF16), _sds((SEQ, 512), F32), _sds((SEQ, 512), F32),
                   _sds((1, D_MODEL), F32), _sds((1, 512), F32), _sds((1, 512), F32)],
        args=(dx1, y, attn, sgu, w_out, g_post, g_attn, g_sgu), comms=comms)


def _sgu_bwd(u, vs, dsgu, lg, lb, w_sp, b_t, comms=()):
    nsteps = SEQ // TM

    def body(u_ref, vs_ref, ds_ref, lg_ref, lb_ref, w_ref, bt_ref,
             du_ref, dvs_ref, dw_ref, db_ref, dlg_ref, dlb_ref, dbias_scr):
        i = pl.program_id(0)

        @pl.when(i == 0)
        def _():
            dw_ref[...] = jnp.zeros_like(dw_ref)
            dlg_ref[...] = jnp.zeros_like(dlg_ref)
            dlb_ref[...] = jnp.zeros_like(dlb_ref)
            dbias_scr[...] = jnp.zeros_like(dbias_scr)

        wm = _masked_spatial(w_ref)
        ones_g = _group_ones()
        bias_full = _dot_exact(bt_ref[...], ones_g)
        u = u_ref[...]
        vs = vs_ref[...]
        lg = lg_ref[...]
        gu, xh, rstd, vnb, mixed = _sgu_core(u, vs, lg, lb_ref[...], wm, bias_full)
        dsgu = ds_ref[...]
        du_ref[...] = (dsgu * mixed * _gelu_grad(u)).astype(BF16)
        dmixed = dsgu * gu
        left = _left_half()
        dvn_rows = []
        for c in range(TM // CHUNK):
            rs = slice(c * CHUNK, (c + 1) * CHUNK)
            dm_c = dmixed[rs, :]
            dbias_scr[...] += dm_c
            pieces = []
            for p in range(4):
                ls = slice(p * 128, (p + 1) * 128)
                dmp = dm_c[:, ls]
                vp = vnb[rs, ls]
                dmb = dmp.astype(BF16)
                zero = jnp.zeros_like(dmb)
                dw_ref[2 * p] += _dot_nt(jnp.where(left, dmb, zero), vp)
                dw_ref[2 * p + 1] += _dot_nt(jnp.where(left, zero, dmb), vp)
                pieces.append(jnp.where(left, _dot_tn(wm[2 * p], dmb), _dot_tn(wm[2 * p + 1], dmb)))
            dvn_rows.append(jnp.concatenate(pieces, axis=1))
        dvn = jnp.concatenate(dvn_rows, axis=0)
        dlg_ref[...] += jnp.sum(dvn * xh, axis=0, keepdims=True)
        dlb_ref[...] += jnp.sum(dvn, axis=0, keepdims=True)
        dxh = dvn * lg
        dgv = rstd * (dxh - jnp.mean(dxh, axis=-1, keepdims=True) - xh * jnp.mean(dxh * xh, axis=-1, keepdims=True))
        dvs_ref[...] = (dgv * _gelu_grad(vs)).astype(BF16)

        @pl.when(i == nsteps - 1)
        def _():
            row = lax.broadcasted_iota(jnp.int32, (CHUNK, CHUNK), 0)
            col = lax.broadcasted_iota(jnp.int32, (CHUNK, CHUNK), 1)
            for g in range(SGU_GROUPS):
                dw_ref[g] = jnp.where(col <= row, dw_ref[g], 0.0)
            db_ref[...] = lax.dot_general(ones_g, dbias_scr[...], (((1,), (1,)), ((), ())),
                                          preferred_element_type=F32, precision=lax.Precision.HIGHEST)

    return _pcall(
        body, name="sgu_bwd", grid=(nsteps,),
        in_specs=[_rows(SGU_W), _rows(SGU_W), _rows(SGU_W), _resident((1, SGU_W)), _resident((1, SGU_W)),
                  _resident((SGU_GROUPS, CHUNK, CHUNK)), _resident((CHUNK, SGU_GROUPS))],
        out_specs=[_rows(SGU_W), _rows(SGU_W), pl.BlockSpec((SGU_GROUPS, CHUNK, CHUNK), lambda i: (0, 0, 0)),
                   _acc(CHUNK, SGU_GROUPS), _acc(SGU_W), _acc(SGU_W)],
        out_shape=[_sds((SEQ, SGU_W), BF16), _sds((SEQ, SGU_W), BF16), _sds((SGU_GROUPS, CHUNK, CHUNK), F32),
                   _sds((SGU_GROUPS, CHUNK), F32), _sds((1, SGU_W), F32), _sds((1, SGU_W), F32)],
        scratch_shapes=[pltpu.VMEM((CHUNK, SGU_W), F32)],
        args=(u, vs, dsgu, lg, lb, w_sp, b_t), comms=comms)


def _attn_bwd(qv, kv, vv, dov, deltav, lsev, dil, comms=()):
    seg = SEQ // dil
    nblk = seg // CHUNK

    def body(q_ref, k_ref, v_ref, do_ref, dl_ref, lse_ref, dq_ref, dk_ref, dv_ref):
        left = _left_half()
        m_cur, m_prev = _block_masks()

        def blk(b, carry):
            r0 = pl.multiple_of(b * CHUNK, CHUNK)
            rp = pl.multiple_of(jnp.maximum(b - 1, 0) * CHUNK, CHUNK)
            prev_ok = m_prev & (b > 0)
            sides = tuple(enumerate((left, ~left)))
            zero = jnp.zeros((CHUNK, CHUNK), BF16)
            tiles, firsts = [], []
            for hp in range(4):
                ls = slice(hp * 128, (hp + 1) * 128)
                qp = q_ref[pl.ds(r0, CHUNK), ls]
                kc = k_ref[pl.ds(r0, CHUNK), ls]
                vc = v_ref[pl.ds(r0, CHUNK), ls]
                dop = do_ref[pl.ds(r0, CHUNK), ls]
                kp = k_ref[pl.ds(rp, CHUNK), ls] if nblk > 1 else None
                vp = v_ref[pl.ds(rp, CHUNK), ls] if nblk > 1 else None
                tiles.append((ls, kc, kp))
                for side, hm in sides:
                    qh = jnp.where(hm, qp, zero)
                    doh = jnp.where(hm, dop, zero)
                    cur = (_dot_nt(qh, kc), _dot_nt(doh, vc))
                    prev = (_dot_nt(qh, kp), _dot_nt(doh, vp)) if nblk > 1 else None
                    firsts.append((qh, doh, cur, prev))
            seconds = []
            for i, (qh, doh, cur, prev) in enumerate(firsts):
                hp, c0 = i // 2, (i % 2) * HEAD_DIM
                ls = slice(hp * 128, (hp + 1) * 128)
                lse_h = lse_ref[pl.ds(r0, CHUNK), ls][:, c0:c0 + 1]
                dl_h = dl_ref[pl.ds(r0, CHUNK), ls][:, c0:c0 + 1]
                pc = jnp.exp(jnp.where(m_cur, cur[0] - lse_h, NEG))
                out = [pc.astype(BF16), (pc * (cur[1] - dl_h)).astype(BF16), None, None]
                if nblk > 1:
                    pp = jnp.exp(jnp.where(prev_ok, prev[0] - lse_h, NEG))
                    out[2:] = [pp.astype(BF16), (pp * (prev[1] - dl_h)).astype(BF16)]
                seconds.append(out)
            for hp, (ls, kc, kp) in enumerate(tiles):
                dq = jnp.zeros((CHUNK, CHUNK), F32)
                dkc = jnp.zeros((CHUNK, CHUNK), F32)
                dvc = jnp.zeros((CHUNK, CHUNK), F32)
                dkp = jnp.zeros((CHUNK, CHUNK), F32)
                dvp = jnp.zeros((CHUNK, CHUNK), F32)
                for side, hm in sides:
                    qh, doh, _, _ = firsts[2 * hp + side]
                    pcb, dsc, ppb, dsp = seconds[2 * hp + side]
                    dq = dq + _dot(dsc, jnp.where(hm, kc, zero))
                    dkc = dkc + _dot_tn(dsc, qh)
                    dvc = dvc + _dot_tn(pcb, doh)
                    if nblk > 1:
                        dq = dq + _dot(dsp, jnp.where(hm, kp, zero))
                        dkp = dkp + _dot_tn(dsp, qh)
                        dvp = dvp + _dot_tn(ppb, doh)
                dq_ref[pl.ds(r0, CHUNK), ls] = dq
                dk_ref[pl.ds(r0, CHUNK), ls] = dkc
                dv_ref[pl.ds(r0, CHUNK), ls] = dvc
                if nblk > 1:
                    @pl.when(b > 0)
                    def _():
                        dk_ref[pl.ds(rp, CHUNK), ls] += dkp
                        dv_ref[pl.ds(rp, CHUNK), ls] += dvp
            return carry

        lax.fori_loop(0, nblk, blk, 0)

    spec = pl.BlockSpec((seg, ATTN_W), lambda r: (0, r))
    return _pcall(
        body, name=f"attn_bwd_d{dil}", grid=(dil,),
        in_specs=[spec] * 6, out_specs=[spec] * 3,
        out_shape=[_sds((seg, dil * ATTN_W), F32)] * 3,
        args=(qv, kv, vv, dov, deltav, lsev), comms=comms)


def _inproj_bwd(dqs, dks, dvs, du, dvs_sgu, pos, x, dx1, w_in, g_pre, comms=()):
    def body(dq1, dq2, dq3, dk1, dk2, dk3, dv1, dv2, dv3, du_ref, dvs_ref, pos_ref, x_ref, dx1_ref, w_ref, g_ref,
             dproj_ref, gx_ref, dg_ref):
        @pl.when(pl.program_id(0) == 0)
        def _():
            dg_ref[...] = jnp.zeros_like(dg_ref)

        tabs = _rot_tables(pos_ref[...])
        dproj_ref[:, 0:512] = _rope_bwd((dq1[...] + dq2[...] + dq3[...]) * np.float32(ATTN_SCALE), tabs).astype(BF16)
        dproj_ref[:, 512:1024] = _rope_bwd(dk1[...] + dk2[...] + dk3[...], tabs).astype(BF16)
        dproj_ref[:, 1024:1536] = (dv1[...] + dv2[...] + dv3[...]).astype(BF16)
        dproj_ref[:, 1536:2048] = du_ref[...]
        dproj_ref[:, 2048:2560] = dvs_ref[...]
        dh = jnp.zeros((TM, D_MODEL), F32)
        for s in range(N_SHARD):
            dh = dh + _dot_nt(dproj_ref[:, s * IN_S:(s + 1) * IN_S], w_ref[s])
        g = g_ref[...]
        xh, r = _rms_stats(x_ref[...])
        dx, dg = _rms_bwd(xh, r, g, dh)
        dg_ref[...] += dg
        gx_ref[...] = dx1_ref[...] + dx

    return _pcall(
        body, name="inproj_bwd", grid=(SEQ // TM,),
        in_specs=[_rows(512)] * 11 + [_rows(1), _rows(D_MODEL), _rows(D_MODEL),
                                     _resident((N_SHARD, D_MODEL, IN_S)), _resident((1, D_MODEL))],
        out_specs=[_rows(PROJ_W), _rows(D_MODEL), _acc(D_MODEL)],
        out_shape=[_sds((SEQ, PROJ_W), BF16), _sds((SEQ, D_MODEL), F32), _sds((1, D_MODEL), F32)],
        args=(*dqs, *dks, *dvs, du, dvs_sgu, pos, x, dx1, w_in, g_pre), comms=comms)


def _to_view(a, dil):
    return a if dil == 1 else a.reshape(SEQ // dil, dil * a.shape[1])


def _from_view(a, dil):
    return a if dil == 1 else a.reshape(SEQ, a.shape[1] // dil)


def _local_step(x, pos, target, w_in, w_out, w_gate, w_up, w_down, small):
    b_t = small["sgu_b_spatial"].T
    h, q, k, v, u, vs = _inproj_fwd(x, pos, small["pre_mix_norm"], w_in)
    sgu = _sgu_fwd(u, vs, small["sgu_ln_gain"], small["sgu_ln_bias"], small["sgu_w_spatial"], b_t)
    views = [tuple(_to_view(t, dil) for t in (q, k, v)) for dil in DILATIONS]
    o_list, l_list = [], []
    for dil, (qv, kv, vv) in zip(DILATIONS, views):
        o, l = _attn_fwd(qv, kv, vv, dil)
        o_list.append(_from_view(o, dil))
        l_list.append(_from_view(l, dil))
    attn, lse, mixed, y, x1 = _mix_out_fwd(o_list, l_list, sgu, x, w_out, small["attn_out_norm"],
                                           small["sgu_out_norm"], small["post_mix_norm"])
    h2, a, dg, dup, df, dx1, loss_cols, d_pre_ffn, d_post_ffn = _ffn_fwd_bwd(
        x1, target, w_gate, w_up, w_down, small["pre_ffn_norm"], small["post_ffn_norm"])

    full_tok = pl.BlockSpec((SEQ, D_MODEL), lambda s: (0, 0), pipeline_mode=pl.Buffered(1))
    ff_tok = pl.BlockSpec((1, SEQ, FF_S), lambda s: (s, 0, 0))
    gw_gate = _wgrad(h2, dg, full_tok, ff_tok, (D_MODEL, FF_S), "wgrad_gate")
    gw_up = _wgrad(h2, dup, full_tok, ff_tok, (D_MODEL, FF_S), "wgrad_up")
    gw_down = _wgrad(a, df, ff_tok, full_tok, (FF_S, D_MODEL), "wgrad_down")

    dy, dattn, delta, dsgu, d_post_mix, d_attn_norm, d_sgu_norm = _outproj_bwd(
        dx1, y, attn, sgu, w_out, small["post_mix_norm"], small["attn_out_norm"], small["sgu_out_norm"])
    gw_out = _wgrad(mixed, dy, pl.BlockSpec((SEQ, OUT_S), lambda s: (0, s)), full_tok, (OUT_S, D_MODEL), "wgrad_out")
    du, dvs_sgu, d_w_sp, d_b_sp, d_ln_gain, d_ln_bias = _sgu_bwd(
        u, vs, dsgu, small["sgu_ln_gain"], small["sgu_ln_bias"], small["sgu_w_spatial"], b_t)

    dqs, dks, dvs = [], [], []
    for dil, (qv, kv, vv) in zip(DILATIONS, views):
        dq, dk, dv = _attn_bwd(qv, kv, vv, _to_view(dattn, dil), _to_view(delta, dil), _to_view(lse, dil), dil)
        dqs.append(_from_view(dq, dil))
        dks.append(_from_view(dk, dil))
        dvs.append(_from_view(dv, dil))
    dproj, grad_x, d_pre_mix = _inproj_bwd(dqs, dks, dvs, du, dvs_sgu, pos, x, dx1, w_in, small["pre_mix_norm"])
    gw_in = _wgrad(h, dproj, full_tok, pl.BlockSpec((SEQ, IN_S), lambda s: (0, s)), (D_MODEL, IN_S), "wgrad_in")

    small_grads = {
        "pre_mix_norm": d_pre_mix, "sgu_ln_gain": d_ln_gain, "sgu_ln_bias": d_ln_bias, "sgu_w_spatial": d_w_sp,
        "sgu_b_spatial": d_b_sp, "attn_out_norm": d_attn_norm, "sgu_out_norm": d_sgu_norm,
        "post_mix_norm": d_post_mix, "pre_ffn_norm": d_pre_ffn, "post_ffn_norm": d_post_ffn,
    }
    return loss_cols, grad_x, (gw_in, gw_out, gw_gate, gw_up, gw_down), small_grads


def _coords():
    return lax.axis_index("x"), lax.axis_index("y"), lax.axis_index("c")


def _other_chips(x, y):
    return [(1 - x, y), (x, 1 - y), (1 - x, 1 - y)]


def _comm_call(body, name, n_in, out_shape, scratch_shapes):
    return pl.pallas_call(
        body, name=name, in_specs=[ANY] * n_in, out_specs=[ANY] * len(out_shape), out_shape=out_shape,
        scratch_shapes=scratch_shapes,
        compiler_params=pltpu.CompilerParams(has_side_effects=True),
    )


def _gather_weights(shards):
    n = len(shards)
    halves = [s.reshape(2, s.shape[0] // 2, s.shape[1]) for s in shards]

    def body(*refs):
        ins, outs = refs[:n], refs[n:2 * n]
        send_sems, recv_sems = refs[2 * n:]
        x, y, c = _coords()
        s_me = 2 * x + y
        chips = _other_chips(x, y)
        sibling = (x, y, 1 - c)

        def copy(k, w, shard, cc, to):
            src = ins[w].at[cc] if shard is None else outs[w].at[shard, cc]
            dst = outs[w].at[s_me if shard is None else shard, cc]
            return pltpu.make_async_remote_copy(src_ref=src, dst_ref=dst, send_sem=send_sems.at[k],
                                                recv_sem=recv_sems.at[k], device_id=to, device_id_type=MESH)

        first = [copy(j * n + w, w, None, c, (cx, cy, c)) for j, (cx, cy) in enumerate(chips) for w in range(n)]
        for cp in first:
            cp.start()
        passed = []
        for j, (cx, cy) in enumerate(chips):
            for w in range(n):
                copy(j * n + w, w, 2 * cx + cy, c, (x, y, c)).wait_recv()
                fw = copy((3 + j) * n + w, w, 2 * cx + cy, c, sibling)
                fw.start()
                passed.append(fw)
        for j, (cx, cy) in enumerate(chips):
            for w in range(n):
                copy((3 + j) * n + w, w, 2 * cx + cy, 1 - c, (x, y, c)).wait_recv()
        for cp in first + passed:
            cp.wait_send()

    out_shape = [_sds((N_SHARD,) + h.shape, h.dtype) for h in halves]
    scratch = [pltpu.SemaphoreType.DMA((6 * n,)), pltpu.SemaphoreType.DMA((6 * n,))]
    full = _comm_call(body, "comm_gather_weights", n, out_shape, scratch)(*halves)
    s_me = 2 * lax.axis_index("x") + lax.axis_index("y")
    full = [lax.dynamic_update_slice(f, h[None], (s_me, 0, 0, 0)) for f, h in zip(full, halves)]
    return [f.reshape((N_SHARD,) + s.shape) for f, s in zip(full, shards)]


def _rs_to_sibling(gws):
    n = len(gws)

    def body(*refs):
        ins, outs = refs[:n], refs[n:2 * n]
        send_sems, recv_sems = refs[2 * n:]
        x, y, c = _coords()
        copies = []
        for w in range(n):
            hw = gws[w].shape[1] // 2
            copies.append(pltpu.make_async_remote_copy(
                src_ref=ins[w].at[:, pl.ds((1 - c) * hw, hw), :], dst_ref=outs[w], send_sem=send_sems.at[w],
                recv_sem=recv_sems.at[w], device_id=(x, y, 1 - c), device_id_type=MESH))
        for cp in copies:
            cp.start()
        for cp in copies:
            cp.wait()

    out_shape = [_sds((N_SHARD, g.shape[1] // 2, g.shape[2]), g.dtype) for g in gws]
    scratch = [pltpu.SemaphoreType.DMA((n,)), pltpu.SemaphoreType.DMA((n,))]
    return _comm_call(body, "comm_rs_sibling", n, out_shape, scratch)(*gws)


def _rs_chip_sum(gw, recv, core):
    _, rows, cols = gw.shape
    hw = rows // 2

    def body(c_ref, g_ref, r_ref, o_ref):
        o_ref[...] = (g_ref[...] + r_ref[...]).astype(BF16)

    return pl.pallas_call(
        body, name="rs_chip_sum",
        grid_spec=pltpu.PrefetchScalarGridSpec(
            num_scalar_prefetch=1, grid=(N_SHARD,),
            in_specs=[pl.BlockSpec((1, hw, cols), lambda s, c_ref: (s, c_ref[0], 0)),
                      pl.BlockSpec((1, hw, cols), lambda s, c_ref: (s, 0, 0))],
            out_specs=pl.BlockSpec((1, hw, cols), lambda s, c_ref: (s, 0, 0))),
        out_shape=_sds((N_SHARD, hw, cols), BF16),
        compiler_params=_seq_params(),
    )(core, gw, recv)


def _rs_between_chips(pbs):
    n = len(pbs)

    def body(*refs):
        ins, outs = refs[:n], refs[n:2 * n]
        send_sems, recv_sems = refs[2 * n:]
        x, y, c = _coords()
        copies = []
        for j, (cx, cy) in enumerate(_other_chips(x, y)):
            for w in range(n):
                copies.append(pltpu.make_async_remote_copy(
                    src_ref=ins[w].at[2 * cx + cy], dst_ref=outs[w].at[j], send_sem=send_sems.at[j * n + w],
                    recv_sem=recv_sems.at[j * n + w], device_id=(cx, cy, c), device_id_type=MESH))
        for cp in copies:
            cp.start()
        for cp in copies:
            cp.wait()

    out_shape = [_sds((3,) + p.shape[1:], p.dtype) for p in pbs]
    scratch = [pltpu.SemaphoreType.DMA((3 * n,)), pltpu.SemaphoreType.DMA((3 * n,))]
    return _comm_call(body, "comm_rs_chips", n, out_shape, scratch)(*pbs)


def _rs_final_sum(gw, recv_sib, recv_chips, shard_core):
    _, rows, cols = gw.shape
    hw = rows // 2

    def body(sc_ref, g_ref, r_ref, rc_ref, o_ref):
        acc = g_ref[0] + r_ref[0]
        for j in range(3):
            acc = acc + rc_ref[j].astype(F32)
        o_ref[0] = acc

    return pl.pallas_call(
        body, name="rs_final_sum",
        grid_spec=pltpu.PrefetchScalarGridSpec(
            num_scalar_prefetch=1, grid=(1,),
            in_specs=[pl.BlockSpec((1, hw, cols), lambda i, sc: (sc[0], sc[1], 0)),
                      pl.BlockSpec((1, hw, cols), lambda i, sc: (sc[0], 0, 0)),
                      pl.BlockSpec((3, hw, cols), lambda i, sc: (0, 0, 0))],
            out_specs=pl.BlockSpec((1, hw, cols), lambda i, sc: (sc[1], 0, 0))),
        out_shape=_sds((2, hw, cols), F32),
        compiler_params=_seq_params(),
    )(shard_core, gw, recv_sib, recv_chips)


def _rs_join_halves(halves):
    n = len(halves)

    def body(*refs):
        bufs = refs[n:2 * n]
        send_sems, recv_sems = refs[2 * n:]
        x, y, c = _coords()
        remote = [pltpu.make_async_remote_copy(
            src_ref=bufs[w].at[c], dst_ref=bufs[w].at[c], send_sem=send_sems.at[w], recv_sem=recv_sems.at[w],
            device_id=(x, y, 1 - c), device_id_type=MESH) for w in range(n)]
        for cp in remote:
            cp.start()
        for w in range(n):
            remote[w].wait_send()
            pltpu.make_async_remote_copy(
                src_ref=bufs[w].at[c], dst_ref=bufs[w].at[1 - c], send_sem=send_sems.at[w],
                recv_sem=recv_sems.at[w], device_id=(x, y, c), device_id_type=MESH).wait_recv()

    joined = pl.pallas_call(
        body, name="comm_rs_join", in_specs=[ANY] * n, out_specs=[ANY] * n,
        out_shape=[_sds(h.shape, h.dtype) for h in halves], input_output_aliases={w: w for w in range(n)},
        scratch_shapes=[pltpu.SemaphoreType.DMA((n,)), pltpu.SemaphoreType.DMA((n,))],
        compiler_params=pltpu.CompilerParams(has_side_effects=True),
    )(*halves)
    return [j.reshape(2 * h.shape[1], h.shape[2]) for j, h in zip(joined, halves)]


def _allreduce_small(buf):
    rows, cols = buf.shape

    def body(in_ref, out_ref, slots, send_sems, recv_sems):
        x, y, c = _coords()
        me = 4 * x + 2 * y + c
        copies, peers = [], []
        for k in range(1, 8):
            px = 1 - x if (k >> 2) & 1 else x
            py = 1 - y if (k >> 1) & 1 else y
            pc = 1 - c if k & 1 else c
            peers.append(4 * px + 2 * py + pc)
            copies.append(pltpu.make_async_remote_copy(
                src_ref=in_ref, dst_ref=slots.at[me], send_sem=send_sems.at[k - 1], recv_sem=recv_sems.at[k - 1],
                device_id=(px, py, pc), device_id_type=MESH))
        for cp in copies:
            cp.start()
        slots[me] = in_ref[...]
        for k in range(7):
            pltpu.make_async_remote_copy(
                src_ref=in_ref, dst_ref=slots.at[peers[k]], send_sem=send_sems.at[k], recv_sem=recv_sems.at[k],
                device_id=(x, y, c), device_id_type=MESH).wait_recv()
        for cp in copies:
            cp.wait_send()
        acc = slots[0]
        for i in range(1, 8):
            acc = acc + slots[i]
        out_ref[...] = acc

    vmem = pl.BlockSpec(memory_space=pltpu.VMEM)
    return pl.pallas_call(
        body, name="comm_allreduce_small", in_specs=[vmem], out_specs=vmem, out_shape=_sds((rows, cols), F32),
        scratch_shapes=[pltpu.VMEM((8, rows, cols), F32), pltpu.SemaphoreType.DMA((7,)), pltpu.SemaphoreType.DMA((7,))],
        compiler_params=pltpu.CompilerParams(has_side_effects=True, vmem_limit_bytes=VMEM_LIMIT),
    )(buf)


def _adamw(w, g, m, v, block_rows, name):
    rows, cols = w.shape

    def body(w_ref, g_ref, m_ref, v_ref, d_ref, nm_ref, nv_ref):
        g = g_ref[...]
        m = ADAM_B1 * m_ref[...] + (1.0 - ADAM_B1) * g
        v = ADAM_B2 * v_ref[...] + (1.0 - ADAM_B2) * (g * g)
        m_hat = m / (1.0 - ADAM_B1 ** ADAM_STEP)
        v_hat = v / (1.0 - ADAM_B2 ** ADAM_STEP)
        d_ref[...] = -ADAM_LR * (m_hat / (jnp.sqrt(v_hat) + ADAM_EPS) + ADAM_WD * w_ref[...])
        nm_ref[...] = m
        nv_ref[...] = v

    spec = pl.BlockSpec((block_rows, cols), lambda i: (i, 0))
    return pl.pallas_call(
        body, name=name, grid=(rows // block_rows,), in_specs=[spec] * 4, out_specs=[spec] * 3,
        out_shape=[_sds((rows, cols), F32)] * 3,
        compiler_params=_seq_params(),
    )(w, g, m, v)


WEIGHTS = ("pre_mix_norm", "w_in", "sgu_ln_gain", "sgu_ln_bias", "sgu_w_spatial", "sgu_b_spatial", "attn_out_norm",
           "sgu_out_norm", "w_out", "post_mix_norm", "pre_ffn_norm", "w_gate", "w_up", "w_down", "post_ffn_norm")
BIG = ("w_in", "w_out", "w_gate", "w_up", "w_down")
BIG_ADAM_ROWS = {"w_in": 256, "w_out": 128, "w_gate": 256, "w_up": 256, "w_down": 352}
SMALL = ("pre_mix_norm", "post_mix_norm", "pre_ffn_norm", "post_ffn_norm", "sgu_ln_gain", "sgu_ln_bias",
         "attn_out_norm", "sgu_out_norm", "sgu_w_spatial", "sgu_b_spatial")


def _pack_small(d):
    flat = [d[n].reshape(-1) for n in SMALL]
    used = sum(f.shape[0] for f in flat)
    flat.append(jnp.zeros((SMALL_ROWS * 1024 - used,), F32))
    return jnp.concatenate(flat).reshape(SMALL_ROWS, 1024)


def _unpack_small(buf, shapes):
    flat = buf.reshape(-1)
    out, off = {}, 0
    for n in SMALL:
        size = int(np.prod(shapes[n]))
        out[n] = flat[off:off + size].reshape(shapes[n])
        off += size
    return out


def _kernel_unoverlapped(x, positions, pre_mix_norm, w_in, sgu_ln_gain, sgu_ln_bias, sgu_w_spatial, sgu_b_spatial, attn_out_norm, sgu_out_norm, w_out, post_mix_norm, pre_ffn_norm, w_gate, w_up, w_down, post_ffn_norm, loss_target, m_pre_mix_norm, m_w_in, m_sgu_ln_gain, m_sgu_ln_bias, m_sgu_w_spatial, m_sgu_b_spatial, m_attn_out_norm, m_sgu_out_norm, m_w_out, m_post_mix_norm, m_pre_ffn_norm, m_w_gate, m_w_up, m_w_down, m_post_ffn_norm, v_pre_mix_norm, v_w_in, v_sgu_ln_gain, v_sgu_ln_bias, v_sgu_w_spatial, v_sgu_b_spatial, v_attn_out_norm, v_sgu_out_norm, v_w_out, v_post_mix_norm, v_pre_ffn_norm, v_w_gate, v_w_up, v_w_down, v_post_ffn_norm):
    a = dict(locals())
    cx, cy, cc = _coords()
    core = jnp.stack([cc]).astype(jnp.int32)
    shard_core = jnp.stack([2 * cx + cy, cc]).astype(jnp.int32)

    full = _gather_weights([a[n][0].astype(BF16) for n in BIG])
    small = {n: (a[n][0] if a[n].ndim > 2 else a[n]) for n in SMALL}
    loss_cols, grad_x, gws, small_grads = _local_step(
        x[0], positions.reshape(SEQ, 1), loss_target[0], *full, small)
    loss = lax.psum(jnp.sum(loss_cols) * np.float32(0.5 / D_MODEL), ("x", "y", "c"))

    recv_sib = _rs_to_sibling(list(gws))
    chip_part = [_rs_chip_sum(g, r, core) for g, r in zip(gws, recv_sib)]
    recv_chips = _rs_between_chips(chip_part)
    halves = [_rs_final_sum(g, r, rc, shard_core) for g, r, rc in zip(gws, recv_sib, recv_chips)]
    big_grads = dict(zip(BIG, _rs_join_halves(halves)))

    shapes = {n: a[n].shape for n in SMALL}
    small_sum = _allreduce_small(_pack_small(small_grads))

    grads, deltas, new_m, new_v = {}, {}, {}, {}
    for n in BIG:
        grads[n] = big_grads[n][None]
        d, nm, nv = _adamw(a[n][0], big_grads[n], a["m_" + n][0], a["v_" + n][0], BIG_ADAM_ROWS[n], "adamw_" + n)
        deltas[n], new_m[n], new_v[n] = d[None], nm[None], nv[None]
    d, nm, nv = _adamw(_pack_small({n: a[n] for n in SMALL}), small_sum, _pack_small({n: a["m_" + n] for n in SMALL}),
                       _pack_small({n: a["v_" + n] for n in SMALL}), SMALL_ROWS, "adamw_small")
    grads.update(_unpack_small(small_sum, shapes))
    deltas.update(_unpack_small(d, shapes))
    new_m.update(_unpack_small(nm, shapes))
    new_v.update(_unpack_small(nv, shapes))
    return (loss, grad_x[None], *[grads[n] for n in WEIGHTS], *[deltas[n] for n in WEIGHTS],
            *[new_m[n] for n in WEIGHTS], *[new_v[n] for n in WEIGHTS])


def _remote(src, dst, send_sem, recv_sem, to):
    return pltpu.make_async_remote_copy(src_ref=src, dst_ref=dst, send_sem=send_sem, recv_sem=recv_sem,
                                        device_id=to, device_id_type=MESH)


def _halves(a):
    *lead, rows, cols = a.shape
    return a.reshape(*lead, 2, rows // 2, cols)


def _gather_ici(shards):
    n = len(shards)

    def desc(ins, outs, ss, rs, j, w, landed):
        x, y, c = _coords()
        cx, cy = _other_chips(x, y)[j]
        shard = 2 * cx + cy if landed else 2 * x + y
        return _remote(ins[w].at[c], outs[w].at[shard, c], ss.at[j * n + w], rs.at[j * n + w], (cx, cy, c))

    def start(ins, outs, ss, rs):
        for j in range(3):
            for w in range(n):
                desc(ins, outs, ss, rs, j, w, False).start()

    def finish(ins, outs, ss, rs):
        for j in range(3):
            for w in range(n):
                desc(ins, outs, ss, rs, j, w, True).wait_recv()
                desc(ins, outs, ss, rs, j, w, False).wait_send()

    return _Comm(shards, [_sds((N_SHARD,) + s.shape, s.dtype) for s in shards], 3 * n, start, finish)


def _gather_pass(fulls):
    n = len(fulls)

    def desc(bufs, ss, rs, j, w, landed):
        x, y, c = _coords()
        cx, cy = _other_chips(x, y)[j]
        shard = 2 * cx + cy
        return _remote(bufs[w].at[shard, c], bufs[w].at[shard, 1 - c if landed else c],
                       ss.at[j * n + w], rs.at[j * n + w], (x, y, 1 - c))

    def start(ins, outs, ss, rs):
        for j in range(3):
            for w in range(n):
                desc(outs, ss, rs, j, w, False).start()

    def finish(ins, outs, ss, rs):
        for j in range(3):
            for w in range(n):
                desc(outs, ss, rs, j, w, True).wait_recv()
                desc(outs, ss, rs, j, w, False).wait_send()

    return _Comm(fulls, [_sds(f.shape, f.dtype) for f in fulls], 3 * n, start, finish, aliased=True)


def _rs_sibling(gws):
    n = len(gws)

    def desc(ins, outs, ss, rs, w):
        x, y, c = _coords()
        return _remote(ins[w].at[:, 1 - c], outs[w], ss.at[w], rs.at[w], (x, y, 1 - c))

    def start(ins, outs, ss, rs):
        for w in range(n):
            desc(ins, outs, ss, rs, w).start()

    def finish(ins, outs, ss, rs):
        for w in range(n):
            desc(ins, outs, ss, rs, w).wait()

    out_shape = [_sds((N_SHARD, g.shape[1] // 2, g.shape[2]), g.dtype) for g in gws]
    return _Comm([_halves(g) for g in gws], out_shape, n, start, finish)


def _rs_chips(pbs):
    n = len(pbs)

    def desc(ins, outs, ss, rs, j, w):
        x, y, c = _coords()
        cx, cy = _other_chips(x, y)[j]
        return _remote(ins[w].at[2 * cx + cy], outs[w].at[j], ss.at[j * n + w], rs.at[j * n + w], (cx, cy, c))

    def start(ins, outs, ss, rs):
        for j in range(3):
            for w in range(n):
                desc(ins, outs, ss, rs, j, w).start()

    def finish(ins, outs, ss, rs):
        for j in range(3):
            for w in range(n):
                desc(ins, outs, ss, rs, j, w).wait()

    return _Comm(pbs, [_sds((3,) + p.shape[1:], p.dtype) for p in pbs], 3 * n, start, finish)


def _rs_join(halves):
    n = len(halves)

    def desc(bufs, ss, rs, w, landed):
        x, y, c = _coords()
        return _remote(bufs[w].at[c], bufs[w].at[1 - c if landed else c], ss.at[w], rs.at[w], (x, y, 1 - c))

    def start(ins, outs, ss, rs):
        for w in range(n):
            desc(outs, ss, rs, w, False).start()

    def finish(ins, outs, ss, rs):
        for w in range(n):
            desc(outs, ss, rs, w, True).wait_recv()
            desc(outs, ss, rs, w, False).wait_send()

    return _Comm(halves, [_sds(h.shape, h.dtype) for h in halves], n, start, finish, aliased=True)


def _small_exchange(buf):
    def desc(ins, outs, ss, rs, k, landed):
        x, y, c = _coords()
        px = 1 - x if (k >> 2) & 1 else x
        py = 1 - y if (k >> 1) & 1 else y
        pc = 1 - c if k & 1 else c
        slot = 4 * px + 2 * py + pc if landed else 4 * x + 2 * y + c
        return _remote(ins[0], outs[0].at[slot], ss.at[k - 1], rs.at[k - 1], (px, py, pc))

    def start(ins, outs, ss, rs):
        for k in range(1, 8):
            desc(ins, outs, ss, rs, k, False).start()

    def finish(ins, outs, ss, rs):
        for k in range(1, 8):
            desc(ins, outs, ss, rs, k, True).wait_recv()
            desc(ins, outs, ss, rs, k, False).wait_send()

    return _Comm([buf], [_sds((8,) + buf.shape, buf.dtype)], 7, start, finish)


SMALL_EARLY = ("post_mix_norm", "pre_ffn_norm", "post_ffn_norm", "sgu_ln_gain", "sgu_ln_bias", "attn_out_norm",
               "sgu_out_norm", "sgu_w_spatial", "sgu_b_spatial")
SMALL_LATE = ("pre_mix_norm",)


def _pack(d, names, rows):
    flat = [d[n].reshape(-1) for n in names]
    used = sum(f.shape[0] for f in flat)
    flat.append(jnp.zeros((rows * 1024 - used,), F32))
    return jnp.concatenate(flat).reshape(rows, 1024)


def _unpack(buf, names, shapes):
    flat = buf.reshape(-1)
    out, off = {}, 0
    for n in names:
        size = int(np.prod(shapes[n]))
        out[n] = flat[off:off + size].reshape(shapes[n])
        off += size
    return out


def _comm_only(name, comms):
    return _pcall(lambda: None, name=name, grid=(1,), in_specs=[], out_specs=[], out_shape=[], args=(),
                  comms=comms)[1]


def _adam_math(w, g, m, v):
    m = ADAM_B1 * m + (1.0 - ADAM_B1) * g
    v = ADAM_B2 * v + (1.0 - ADAM_B2) * (g * g)
    m_hat = m / (1.0 - ADAM_B1 ** ADAM_STEP)
    v_hat = v / (1.0 - ADAM_B2 ** ADAM_STEP)
    return -ADAM_LR * (m_hat / (jnp.sqrt(v_hat) + ADAM_EPS) + ADAM_WD * w), m, v


def _adamw_small(own, slots, w, m, v, me):
    rows, cols = own.shape

    def body(me_ref, own_ref, slots_ref, w_ref, m_ref, v_ref, g_ref, d_ref, nm_ref, nv_ref):
        own_v = own_ref[...]
        g = jnp.where(me_ref[0] == 0, own_v, slots_ref[0])
        for i in range(1, 8):
            g = g + jnp.where(me_ref[0] == i, own_v, slots_ref[i])
        g_ref[...] = g
        d_ref[...], nm_ref[...], nv_ref[...] = _adam_math(w_ref[...], g, m_ref[...], v_ref[...])

    flat = pl.BlockSpec((rows, cols), lambda i, me_ref: (0, 0))
    return pl.pallas_call(
        body, name="adamw_small",
        grid_spec=pltpu.PrefetchScalarGridSpec(
            num_scalar_prefetch=1, grid=(1,),
            in_specs=[flat, pl.BlockSpec((8, rows, cols), lambda i, me_ref: (0, 0, 0)), flat, flat, flat],
            out_specs=[flat] * 4),
        out_shape=[_sds((rows, cols), F32)] * 4,
        compiler_params=_seq_params(),
    )(me, own, slots, w, m, v)


def kernel(x, positions, pre_mix_norm, w_in, sgu_ln_gain, sgu_ln_bias, sgu_w_spatial, sgu_b_spatial, attn_out_norm, sgu_out_norm, w_out, post_mix_norm, pre_ffn_norm, w_gate, w_up, w_down, post_ffn_norm, loss_target, m_pre_mix_norm, m_w_in, m_sgu_ln_gain, m_sgu_ln_bias, m_sgu_w_spatial, m_sgu_b_spatial, m_attn_out_norm, m_sgu_out_norm, m_w_out, m_post_mix_norm, m_pre_ffn_norm, m_w_gate, m_w_up, m_w_down, m_post_ffn_norm, v_pre_mix_norm, v_w_in, v_sgu_ln_gain, v_sgu_ln_bias, v_sgu_w_spatial, v_sgu_b_spatial, v_attn_out_norm, v_sgu_out_norm, v_w_out, v_post_mix_norm, v_pre_ffn_norm, v_w_gate, v_w_up, v_w_down, v_post_ffn_norm):
    a = dict(locals())
    cx, cy, cc = _coords()
    s_me = 2 * cx + cy
    core = jnp.stack([cc]).astype(jnp.int32)
    shard_core = jnp.stack([s_me, cc]).astype(jnp.int32)
    me = jnp.stack([4 * cx + 2 * cy + cc]).astype(jnp.int32)
    small = {n: (a[n][0] if a[n].ndim > 2 else a[n]) for n in SMALL}
    b_t = small["sgu_b_spatial"].T
    xs, pos, target = x[0], positions.reshape(SEQ, 1), loss_target[0]
    own = {n: _halves(a[n][0].astype(BF16)) for n in BIG}

    def with_own(full, n):
        full = lax.dynamic_update_slice(full, own[n][None], (s_me, 0, 0, 0))
        return full.reshape((N_SHARD,) + a[n].shape[1:])

    (w_in_f,) = _gather_weights([a["w_in"][0].astype(BF16)])
    (h, q, k, v, u, vs), ((l_out,),) = _inproj_fwd(xs, pos, small["pre_mix_norm"], w_in_f,
                                                   comms=[_gather_ici([own["w_out"]])])
    views = [tuple(_to_view(t, dil) for t in (q, k, v)) for dil in DILATIONS]
    rider = {1: "w_gate", 4: "w_up", 16: "w_down"}
    landed, o_list, l_list = {}, [], []
    for dil, (qv, kv, vv) in zip(DILATIONS, views):
        (o, l), ((landed[rider[dil]],),) = _attn_fwd(qv, kv, vv, dil, comms=[_gather_ici([own[rider[dil]]])])
        o_list.append(_from_view(o, dil))
        l_list.append(_from_view(l, dil))
    sgu, ((l_out,),) = _sgu_fwd(u, vs, small["sgu_ln_gain"], small["sgu_ln_bias"], small["sgu_w_spatial"], b_t,
                                comms=[_gather_pass([l_out])])
    w_out_f = with_own(l_out, "w_out")
    (attn, lse, mixed, y, x1), (ffn_w,) = _mix_out_fwd(
        o_list, l_list, sgu, xs, w_out_f, small["attn_out_norm"], small["sgu_out_norm"],
        small["post_mix_norm"], comms=[_gather_pass([landed["w_gate"], landed["w_up"], landed["w_down"]])])
    w_gate_f, w_up_f, w_down_f = (with_own(f, n) for f, n in zip(ffn_w, ("w_gate", "w_up", "w_down")))
    h2, act, dg, dup, df, dx1, loss_cols, d_pre_ffn, d_post_ffn = _ffn_fwd_bwd(
        x1, target, w_gate_f, w_up_f, w_down_f, small["pre_ffn_norm"], small["post_ffn_norm"])
    loss = lax.psum(jnp.sum(loss_cols) * np.float32(0.5 / D_MODEL), ("x", "y", "c"))

    full_tok = pl.BlockSpec((SEQ, D_MODEL), lambda s: (0, 0), pipeline_mode=pl.Buffered(1))
    ff_tok = pl.BlockSpec((1, SEQ, FF_S), lambda s: (s, 0, 0))
    gw = {}
    gw["w_gate"] = _wgrad(h2, dg, full_tok, ff_tok, (D_MODEL, FF_S), "wgrad_gate")
    gw["w_up"], ((sib_gate,),) = _wgrad(h2, dup, full_tok, ff_tok, (D_MODEL, FF_S), "wgrad_up",
                                        comms=[_rs_sibling([gw["w_gate"]])])
    gw["w_down"], ((sib_up,),) = _wgrad(act, df, ff_tok, full_tok, (FF_S, D_MODEL), "wgrad_down",
                                        comms=[_rs_sibling([gw["w_up"]])])
    (dy, dattn, delta, dsgu, d_post_mix, d_attn_norm, d_sgu_norm), ((sib_down,),) = _outproj_bwd(
        dx1, y, attn, sgu, w_out_f, small["post_mix_norm"], small["attn_out_norm"],
        small["sgu_out_norm"], comms=[_rs_sibling([gw["w_down"]])])
    sib = {"w_gate": sib_gate, "w_up": sib_up, "w_down": sib_down}
    part = {n: _rs_chip_sum(gw[n], sib[n], core) for n in ("w_gate", "w_up", "w_down")}
    gw["w_out"] = _wgrad(mixed, dy, pl.BlockSpec((SEQ, OUT_S), lambda s: (0, s)), full_tok, (OUT_S, D_MODEL),
                         "wgrad_out")
    (du, dvs_sgu, d_w_sp, d_b_sp, d_ln_gain, d_ln_bias), ((far_gate,), (sib["w_out"],)) = _sgu_bwd(
        u, vs, dsgu, small["sgu_ln_gain"], small["sgu_ln_bias"], small["sgu_w_spatial"], b_t,
        comms=[_rs_chips([part["w_gate"]]), _rs_sibling([gw["w_out"]])])
    part["w_out"] = _rs_chip_sum(gw["w_out"], sib["w_out"], core)
    half, far, joined = {}, {"w_gate": far_gate}, {}
    half["w_gate"] = _rs_final_sum(gw["w_gate"], sib["w_gate"], far["w_gate"], shard_core)
    packed_early = _pack({
        "sgu_ln_gain": d_ln_gain, "sgu_ln_bias": d_ln_bias, "sgu_w_spatial": d_w_sp, "sgu_b_spatial": d_b_sp,
        "attn_out_norm": d_attn_norm, "sgu_out_norm": d_sgu_norm, "post_mix_norm": d_post_mix,
        "pre_ffn_norm": d_pre_ffn, "post_ffn_norm": d_post_ffn}, SMALL_EARLY, SMALL_ROWS)

    dqs, dks, dvs = [], [], []
    for dil, (qv, kv, vv) in zip(DILATIONS, views):
        if dil == 1:
            riders = [_rs_chips([part["w_up"], part["w_down"]])]
        elif dil == 4:
            riders = [_rs_chips([part["w_out"]]), _rs_join([half["w_gate"]]), _small_exchange(packed_early)]
        else:
            riders = [_rs_join([half["w_up"], half["w_down"]])]
        (dq, dk, dv), got = _attn_bwd(qv, kv, vv, _to_view(dattn, dil), _to_view(delta, dil), _to_view(lse, dil),
                                      dil, comms=riders)
        if dil == 1:
            far["w_up"], far["w_down"] = got[0]
            for n in ("w_up", "w_down"):
                half[n] = _rs_final_sum(gw[n], sib[n], far[n], shard_core)
        elif dil == 4:
            (far["w_out"],), (joined["w_gate"],), (slots_early,) = got
            half["w_out"] = _rs_final_sum(gw["w_out"], sib["w_out"], far["w_out"], shard_core)
        else:
            joined["w_up"], joined["w_down"] = got[0]
        dqs.append(_from_view(dq, dil))
        dks.append(_from_view(dk, dil))
        dvs.append(_from_view(dv, dil))
    (dproj, grad_x, d_pre_mix), ((joined["w_out"],),) = _inproj_bwd(
        dqs, dks, dvs, du, dvs_sgu, pos, xs, dx1, w_in_f, small["pre_mix_norm"], comms=[_rs_join([half["w_out"]])])
    packed_late = _pack({"pre_mix_norm": d_pre_mix}, SMALL_LATE, 8)
    gw["w_in"], ((slots_late,),) = _wgrad(h, dproj, full_tok, pl.BlockSpec((SEQ, IN_S), lambda s: (0, s)),
                                          (D_MODEL, IN_S), "wgrad_in", comms=[_small_exchange(packed_late)])
    ((sib["w_in"],),) = _comm_only("comm_rs_sibling_in", [_rs_sibling([gw["w_in"]])])
    part["w_in"] = _rs_chip_sum(gw["w_in"], sib["w_in"], core)
    ((far["w_in"],),) = _comm_only("comm_rs_chips_in", [_rs_chips([part["w_in"]])])
    half["w_in"] = _rs_final_sum(gw["w_in"], sib["w_in"], far["w_in"], shard_core)
    ((joined["w_in"],),) = _comm_only("comm_rs_join_in", [_rs_join([half["w_in"]])])

    grads, deltas, new_m, new_v = {}, {}, {}, {}
    for n in BIG:
        g = joined[n].reshape(a[n].shape[1:])
        grads[n] = g[None]
        d, nm, nv = _adamw(a[n][0], g, a["m_" + n][0], a["v_" + n][0], BIG_ADAM_ROWS[n], "adamw_" + n)
        deltas[n], new_m[n], new_v[n] = d[None], nm[None], nv[None]
    for names, rows, packed, slots in ((SMALL_EARLY, SMALL_ROWS, packed_early, slots_early),
                                       (SMALL_LATE, 8, packed_late, slots_late)):
        outs = _adamw_small(packed, slots, _pack(a, names, rows), _pack({n: a["m_" + n] for n in names}, names, rows),
                            _pack({n: a["v_" + n] for n in names}, names, rows), me)
        for dst, buf in zip((grads, deltas, new_m, new_v), outs):
            dst.update(_unpack(buf, names, {n: a[n].shape for n in names}))
    return (loss, grad_x[None], *[grads[n] for n in WEIGHTS], *[deltas[n] for n in WEIGHTS],
            *[new_m[n] for n in WEIGHTS], *[new_v[n] for n in WEIGHTS])
```

```python
import numpy as np
import jax
import jax.numpy as jnp
from jax import lax
from jax.experimental import pallas as pl
from jax.experimental.pallas import tpu as pltpu

F32 = jnp.float32
BF16 = jnp.bfloat16

SEQ = 2048
D_MODEL = 1024
HEAD_DIM = 64
ATTN_W = 512
SGU_W = 512
SGU_GROUPS = 8
CHUNK = 128
DILATIONS = (1, 4, 16)
N_SHARD = 4
IN_S = 640
OUT_S = 256
FF_S = 704
PROJ_W = N_SHARD * IN_S
RMS_EPS = 1e-6
LN_EPS = 1e-5
ROPE_THETA = 500000.0
ATTN_SCALE = 1.0 / np.sqrt(HEAD_DIM)
NEG = -1e30
TM = 256
VMEM_LIMIT = 56 * 1024 * 1024
SMALL_ROWS = 136

ADAM_LR = 0.001
ADAM_B1 = 0.9
ADAM_B2 = 0.999
ADAM_EPS = 1e-08
ADAM_WD = 0.01
ADAM_STEP = 10

MESH = pl.DeviceIdType.MESH
ANY = pl.BlockSpec(memory_space=pl.ANY)


def _dot(a, b):
    return jnp.dot(a, b, preferred_element_type=F32)


def _dot_nt(a, b):
    return lax.dot_general(a, b, (((1,), (1,)), ((), ())), preferred_element_type=F32)


def _dot_tn(a, b):
    return lax.dot_general(a, b, (((0,), (0,)), ((), ())), preferred_element_type=F32)


def _dot_exact(a, b):
    return jnp.dot(a, b, preferred_element_type=F32, precision=lax.Precision.HIGHEST)


def _rms_stats(x):
    r = lax.rsqrt(jnp.mean(x * x, axis=-1, keepdims=True) + RMS_EPS)
    return x * r, r


def _rms_bwd(xh, r, gain, dy):
    dxh = dy * gain
    dx = r * (dxh - xh * jnp.mean(dxh * xh, axis=-1, keepdims=True))
    return dx, jnp.sum(dy * xh, axis=0, keepdims=True)


_ERF_ALPHA = (-2.72614225801306e-10, 2.77068142495902e-08, -2.10102402082508e-06, -5.69250639462346e-05,
              -7.34990630326855e-04, -2.95459980854025e-03, -1.60960333262415e-02)
_ERF_BETA = (-1.45660718464996e-05, -2.13374055278905e-04, -1.68282697438203e-03, -7.37332916720468e-03,
             -1.42647390514189e-02)


def _erf(x):
    x = jnp.clip(x, -4.0, 4.0)
    x2 = x * x
    p = jnp.full_like(x, _ERF_ALPHA[0])
    for a in _ERF_ALPHA[1:]:
        p = p * x2 + a
    q = jnp.full_like(x, _ERF_BETA[0])
    for b in _ERF_BETA[1:]:
        q = q * x2 + b
    return x * p / q


def _gelu(x):
    return 0.5 * x * (1.0 + _erf(x * np.float32(1.0 / np.sqrt(2.0))))


def _gelu_grad(x):
    cdf = 0.5 * (1.0 + _erf(x * np.float32(1.0 / np.sqrt(2.0))))
    pdf = jnp.exp(-0.5 * x * x) * np.float32(1.0 / np.sqrt(2.0 * np.pi))
    return cdf + x * pdf


def _sigmoid(x):
    return 1.0 / (1.0 + jnp.exp(-x))


_INV_FREQ = tuple(float(np.float32(ROPE_THETA ** (-2.0 * j / 16.0))) for j in range(8))


def _rot_tables(pos):
    lane = lax.broadcasted_iota(jnp.int32, (1, 128), 1)
    d = lane & 63
    j = d & 7
    inv = jnp.zeros((1, 128), F32)
    for jj in range(8):
        inv = jnp.where(j == jj, _INV_FREQ[jj], inv)
    ang = pos.astype(F32) * inv
    c = jnp.cos(ang)
    s = jnp.sin(ang)
    cos_t = jnp.where(d < 16, c, 1.0)
    sin_a = jnp.where(d < 8, -s, 0.0)
    sin_b = jnp.where((d >= 8) & (d < 16), s, 0.0)
    return tuple(jnp.tile(t, (1, 4)) for t in (cos_t, sin_a, sin_b))


def _rope(x, tabs):
    cos_t, sin_a, sin_b = tabs
    return x * cos_t + pltpu.roll(x, 504, 1) * sin_a + pltpu.roll(x, 8, 1) * sin_b


def _rope_bwd(dy, tabs):
    cos_t, sin_a, sin_b = tabs
    return dy * cos_t + pltpu.roll(dy * sin_a, 8, 1) + pltpu.roll(dy * sin_b, 504, 1)


def _left_half():
    return lax.broadcasted_iota(jnp.int32, (CHUNK, CHUNK), 1) < HEAD_DIM


def _group_ones():
    lane = lax.broadcasted_iota(jnp.int32, (SGU_GROUPS, SGU_W), 1)
    row = lax.broadcasted_iota(jnp.int32, (SGU_GROUPS, SGU_W), 0)
    return ((lane >> 6) == row).astype(F32)


def _masked_spatial(w_ref):
    row = lax.broadcasted_iota(jnp.int32, (CHUNK, CHUNK), 0)
    col = lax.broadcasted_iota(jnp.int32, (CHUNK, CHUNK), 1)
    return [jnp.where(col <= row, w_ref[g], 0.0).astype(BF16) for g in range(SGU_GROUPS)]


def _sgu_core(u, vs, lg, lb, wm, bias_full):
    tm = u.shape[0]
    gu = _gelu(u)
    gv = _gelu(vs)
    mu = jnp.mean(gv, axis=-1, keepdims=True)
    xc = gv - mu
    rstd = lax.rsqrt(jnp.mean(xc * xc, axis=-1, keepdims=True) + LN_EPS)
    xh = xc * rstd
    vnb = (xh * lg + lb).astype(BF16)
    left = _left_half()
    rows = []
    for c in range(tm // CHUNK):
        pieces = []
        for p in range(4):
            vp = vnb[c * CHUNK:(c + 1) * CHUNK, p * 128:(p + 1) * 128]
            pieces.append(jnp.where(left, _dot(wm[2 * p], vp), _dot(wm[2 * p + 1], vp)))
        rows.append(jnp.concatenate(pieces, axis=1) + bias_full)
    mixed = jnp.concatenate(rows, axis=0)
    return gu, xh, rstd, vnb, mixed


def _resident(shape):
    n = len(shape)
    return pl.BlockSpec(shape, lambda *_: (0,) * n, pipeline_mode=pl.Buffered(1))


def _rows(ncol, tm=TM):
    return pl.BlockSpec((tm, ncol), lambda i: (i, 0))


def _rows3(nlead, ncol, tm=TM):
    return pl.BlockSpec((nlead, tm, ncol), lambda i: (0, i, 0))


def _acc(ncol, nrow=1):
    return pl.BlockSpec((nrow, ncol), lambda i: (0, 0))


def _seq_params():
    return pltpu.CompilerParams(dimension_semantics=("arbitrary",), vmem_limit_bytes=VMEM_LIMIT)


def _sds(shape, dtype):
    return jax.ShapeDtypeStruct(shape, dtype)


class _Comm:
    def __init__(self, args, out_shape, n_sems, start, finish, aliased=False):
        self.args, self.out_shape, self.n_sems = list(args), list(out_shape), n_sems
        self.start, self.finish, self.aliased = start, finish, aliased


def _pcall(body, *, name, grid, in_specs, out_specs, out_shape, args, scratch_shapes=(), comms=()):
    single = not isinstance(out_shape, (list, tuple))
    out_specs = [out_specs] if single else list(out_specs)
    out_shape = [out_shape] if single else list(out_shape)
    n_in, n_out, n_scr = len(in_specs), len(out_shape), len(scratch_shapes)
    c_args = [a for c in comms for a in c.args]
    c_outs = [o for c in comms for o in c.out_shape]
    aliases, ai, ao = {}, n_in, n_out
    for c in comms:
        if c.aliased:
            aliases.update({ai + k: ao + k for k in range(len(c.args))})
        ai += len(c.args)
        ao += len(c.out_shape)
    sems = [pltpu.SemaphoreType.DMA((c.n_sems,)) for c in comms for _ in range(2)]
    steps = grid[0]

    def wrapped(*refs):
        o0 = n_in + len(c_args)
        s0 = o0 + n_out + len(c_outs)
        m_in, m_out, m_sem = refs[n_in:o0], refs[o0 + n_out:s0], refs[s0 + n_scr:]

        def each(phase):
            ii = oi = 0
            for k, c in enumerate(comms):
                getattr(c, phase)(m_in[ii:ii + len(c.args)], m_out[oi:oi + len(c.out_shape)],
                                  m_sem[2 * k], m_sem[2 * k + 1])
                ii += len(c.args)
                oi += len(c.out_shape)

        if comms:
            @pl.when(pl.program_id(0) == 0)
            def _():
                each("start")

        body(*refs[:n_in], *refs[o0:o0 + n_out], *refs[s0:s0 + n_scr])

        if comms:
            @pl.when(pl.program_id(0) == steps - 1)
            def _():
                each("finish")

    res = pl.pallas_call(
        wrapped, name=name, grid=grid,
        in_specs=list(in_specs) + [ANY] * len(c_args), out_specs=out_specs + [ANY] * len(c_outs),
        out_shape=out_shape + c_outs, scratch_shapes=list(scratch_shapes) + sems,
        input_output_aliases=aliases, compiler_params=_seq_params(),
    )(*args, *c_args)
    mine = res[0] if single else list(res[:n_out])
    if not comms:
        return mine
    theirs, oi = [], n_out
    for c in comms:
        theirs.append(list(res[oi:oi + len(c.out_shape)]))
        oi += len(c.out_shape)
    return mine, theirs


def _inproj_fwd(x, pos, g_pre, w_in, comms=()):
    def body(x_ref, pos_ref, g_ref, w_ref, h_ref, q_ref, k_ref, v_ref, u_ref, vs_ref):
        xh, _ = _rms_stats(x_ref[...])
        h = (xh * g_ref[...]).astype(BF16)
        h_ref[...] = h
        proj = jnp.concatenate([_dot(h, w_ref[s]) for s in range(N_SHARD)], axis=1)
        tabs = _rot_tables(pos_ref[...])
        q_ref[...] = (_rope(proj[:, 0:512], tabs) * np.float32(ATTN_SCALE)).astype(BF16)
        k_ref[...] = _rope(proj[:, 512:1024], tabs).astype(BF16)
        v_ref[...] = proj[:, 1024:1536].astype(BF16)
        u_ref[...] = proj[:, 1536:2048]
        vs_ref[...] = proj[:, 2048:2560]

    return _pcall(
        body, name="inproj_fwd", grid=(SEQ // TM,),
        in_specs=[_rows(D_MODEL), _rows(1), _resident((1, D_MODEL)), _resident((N_SHARD, D_MODEL, IN_S))],
        out_specs=[_rows(D_MODEL), _rows(512), _rows(512), _rows(512), _rows(512), _rows(512)],
        out_shape=[_sds((SEQ, D_MODEL), BF16), _sds((SEQ, 512), BF16), _sds((SEQ, 512), BF16),
                   _sds((SEQ, 512), BF16), _sds((SEQ, 512), F32), _sds((SEQ, 512), F32)],
        args=(x, pos, g_pre, w_in), comms=comms)


def _sgu_fwd(u, vs, lg, lb, w_sp, b_t, comms=()):
    def body(u_ref, vs_ref, lg_ref, lb_ref, w_ref, bt_ref, out_ref):
        wm = _masked_spatial(w_ref)
        bias_full = _dot_exact(bt_ref[...], _group_ones())
        gu, _, _, _, mixed = _sgu_core(u_ref[...], vs_ref[...], lg_ref[...], lb_ref[...], wm, bias_full)
        out_ref[...] = gu * mixed

    return _pcall(
        body, name="sgu_fwd", grid=(SEQ // TM,),
        in_specs=[_rows(SGU_W), _rows(SGU_W), _resident((1, SGU_W)), _resident((1, SGU_W)),
                  _resident((SGU_GROUPS, CHUNK, CHUNK)), _resident((CHUNK, SGU_GROUPS))],
        out_specs=_rows(SGU_W),
        out_shape=_sds((SEQ, SGU_W), F32),
        args=(u, vs, lg, lb, w_sp, b_t), comms=comms)


def _block_masks():
    row = lax.broadcasted_iota(jnp.int32, (CHUNK, CHUNK), 0)
    col = lax.broadcasted_iota(jnp.int32, (CHUNK, CHUNK), 1)
    return col <= row, col >= row


def _attn_fwd(qv, kv, vv, dil, comms=()):
    seg = SEQ // dil
    nblk = seg // CHUNK

    def body(q_ref, k_ref, v_ref, o_ref, l_ref):
        left = _left_half()
        m_cur, m_prev = _block_masks()
        zero = jnp.zeros((CHUNK, CHUNK), BF16)
        ones = (jnp.where(left, 1.0, 0.0).astype(BF16), jnp.where(left, 0.0, 1.0).astype(BF16))

        def blk(b, carry):
            r0 = pl.multiple_of(b * CHUNK, CHUNK)
            rp = pl.multiple_of(jnp.maximum(b - 1, 0) * CHUNK, CHUNK)
            prev_ok = m_prev & (b > 0)
            sides = tuple(enumerate((left, ~left)))
            tiles, scores = [], []
            for hp in range(4):
                ls = slice(hp * 128, (hp + 1) * 128)
                qp = q_ref[pl.ds(r0, CHUNK), ls]
                kc = k_ref[pl.ds(r0, CHUNK), ls]
                kp = k_ref[pl.ds(rp, CHUNK), ls] if nblk > 1 else None
                tiles.append((ls, v_ref[pl.ds(r0, CHUNK), ls], v_ref[pl.ds(rp, CHUNK), ls] if nblk > 1 else None))
                for _, hm in sides:
                    qh = jnp.where(hm, qp, zero)
                    sc = jnp.where(m_cur, _dot_nt(qh, kc), NEG)
                    sp = jnp.where(prev_ok, _dot_nt(qh, kp), NEG) if nblk > 1 else None
                    scores.append((sc, sp))
            probs = []
            for sc, sp in scores:
                if nblk > 1:
                    m = jnp.max(jnp.maximum(sc, sp), axis=-1, keepdims=True)
                    pc = jnp.exp(sc - m)
                    pp = jnp.exp(sp - m)
                    probs.append((m, pc.astype(BF16), pp.astype(BF16), (pc + pp).astype(BF16)))
                else:
                    m = jnp.max(sc, axis=-1, keepdims=True)
                    pc = jnp.exp(sc - m).astype(BF16)
                    probs.append((m, pc, None, pc))
            for hp, (ls, vc, vp) in enumerate(tiles):
                acc = jnp.zeros((CHUNK, CHUNK), F32)
                den = jnp.zeros((CHUNK, CHUNK), F32)
                for side, hm in sides:
                    _, pc, pp, psum = probs[2 * hp + side]
                    acc = acc + _dot(pc, jnp.where(hm, vc, zero))
                    if nblk > 1:
                        acc = acc + _dot(pp, jnp.where(hm, vp, zero))
                    den = den + _dot(psum, ones[side])
                o_ref[pl.ds(r0, CHUNK), ls] = acc / den
                l_ref[pl.ds(r0, CHUNK), ls] = jnp.where(left, probs[2 * hp][0], probs[2 * hp + 1][0]) + jnp.log(den)
            return carry

        lax.fori_loop(0, nblk, blk, 0)

    spec = pl.BlockSpec((seg, ATTN_W), lambda r: (0, r))
    return _pcall(
        body, name=f"attn_fwd_d{dil}", grid=(dil,),
        in_specs=[spec, spec, spec], out_specs=[spec, spec],
        out_shape=[_sds((seg, dil * ATTN_W), F32), _sds((seg, dil * ATTN_W), F32)],
        args=(qv, kv, vv), comms=comms)


def _mix_out_fwd(o_list, l_list, sgu, x, w_out, g_attn, g_sgu, g_post, comms=()):
    def body(o1, o2, o3, l1, l2, l3, sgu_ref, x_ref, w_ref, ga_ref, gs_ref, gp_ref,
             attn_ref, lse_ref, mixed_ref, y_ref, x1_ref):
        ls = [l1[...], l2[...], l3[...]]
        m = jnp.maximum(jnp.maximum(ls[0], ls[1]), ls[2])
        es = [jnp.exp(l - m) for l in ls]
        den = es[0] + es[1] + es[2]
        attn = (es[0] * o1[...] + es[1] * o2[...] + es[2] * o3[...]) / den
        attn_ref[...] = attn
        lse_ref[...] = m + jnp.log(den)
        ah, _ = _rms_stats(attn)
        sh, _ = _rms_stats(sgu_ref[...])
        mixed = jnp.concatenate([ah * ga_ref[...], sh * gs_ref[...]], axis=1).astype(BF16)
        mixed_ref[...] = mixed
        y = _dot(mixed[:, 0:OUT_S], w_ref[0])
        for s in range(1, N_SHARD):
            y = y + _dot(mixed[:, s * OUT_S:(s + 1) * OUT_S], w_ref[s])
        y_ref[...] = y
        yh, _ = _rms_stats(y)
        x1_ref[...] = x_ref[...] + yh * gp_ref[...]

    return _pcall(
        body, name="mix_out_fwd", grid=(SEQ // TM,),
        in_specs=[_rows(512)] * 7 + [_rows(D_MODEL), _resident((N_SHARD, OUT_S, D_MODEL)),
                                    _resident((1, 512)), _resident((1, 512)), _resident((1, D_MODEL))],
        out_specs=[_rows(512), _rows(512), _rows(D_MODEL), _rows(D_MODEL), _rows(D_MODEL)],
        out_shape=[_sds((SEQ, 512), F32), _sds((SEQ, 512), F32), _sds((SEQ, D_MODEL), BF16),
                   _sds((SEQ, D_MODEL), F32), _sds((SEQ, D_MODEL), F32)],
        args=(*o_list, *l_list, sgu, x, w_out, g_attn, g_sgu, g_post), comms=comms)


def _ffn_fwd_bwd(x1, target, w_gate, w_up, w_down, g_pre, g_post, comms=()):
    def body(x1_ref, t_ref, wg_ref, wu_ref, wd_ref, gpf_ref, gpo_ref,
             h2_ref, a_ref, dg_ref, dup_ref, df_ref, dx1_ref, loss_ref, dgpf_ref, dgpo_ref, g_scr, up_scr):
        @pl.when(pl.program_id(0) == 0)
        def _():
            loss_ref[...] = jnp.zeros_like(loss_ref)
            dgpf_ref[...] = jnp.zeros_like(dgpf_ref)
            dgpo_ref[...] = jnp.zeros_like(dgpo_ref)

        x1 = x1_ref[...]
        gpf = gpf_ref[...]
        gpo = gpo_ref[...]
        xh, r = _rms_stats(x1)
        h2 = (xh * gpf).astype(BF16)
        h2_ref[...] = h2
        f = jnp.zeros((TM, D_MODEL), F32)
        for s in range(N_SHARD):
            g = _dot(h2, wg_ref[s])
            up = _dot(h2, wu_ref[s])
            g_scr[s] = g
            up_scr[s] = up
            a = (g * _sigmoid(g) * up).astype(BF16)
            a_ref[s] = a
            f = f + _dot(a, wd_ref[s])
        fh, rf = _rms_stats(f)
        diff = x1 + fh * gpo - t_ref[...]
        loss_ref[...] += jnp.sum(diff * diff, axis=0, keepdims=True)
        dout = diff * np.float32(1.0 / D_MODEL)
        df, dgpo = _rms_bwd(fh, rf, gpo, dout)
        dgpo_ref[...] += dgpo
        dfb = df.astype(BF16)
        df_ref[...] = dfb
        dh2 = jnp.zeros((TM, D_MODEL), F32)
        for s in range(N_SHARD):
            da = _dot_nt(dfb, wd_ref[s])
            g = g_scr[s]
            up = up_scr[s]
            sg = _sigmoid(g)
            dup = (da * (g * sg)).astype(BF16)
            dg = (da * up * (sg * (1.0 + g * (1.0 - sg)))).astype(BF16)
            dg_ref[s] = dg
            dup_ref[s] = dup
            dh2 = dh2 + _dot_nt(dg, wg_ref[s]) + _dot_nt(dup, wu_ref[s])
        dx, dgpf = _rms_bwd(xh, r, gpf, dh2)
        dgpf_ref[...] += dgpf
        dx1_ref[...] = dout + dx

    return _pcall(
        body, name="ffn_fwd_bwd", grid=(SEQ // TM,),
        in_specs=[_rows(D_MODEL), _rows(D_MODEL), _resident((N_SHARD, D_MODEL, FF_S)),
                  _resident((N_SHARD, D_MODEL, FF_S)), _resident((N_SHARD, FF_S, D_MODEL)),
                  _resident((1, D_MODEL)), _resident((1, D_MODEL))],
        out_specs=[_rows(D_MODEL), _rows3(N_SHARD, FF_S), _rows3(N_SHARD, FF_S), _rows3(N_SHARD, FF_S),
                   _rows(D_MODEL), _rows(D_MODEL), _acc(D_MODEL), _acc(D_MODEL), _acc(D_MODEL)],
        out_shape=[_sds((SEQ, D_MODEL), BF16), _sds((N_SHARD, SEQ, FF_S), BF16), _sds((N_SHARD, SEQ, FF_S), BF16),
                   _sds((N_SHARD, SEQ, FF_S), BF16), _sds((SEQ, D_MODEL), BF16), _sds((SEQ, D_MODEL), F32),
                   _sds((1, D_MODEL), F32), _sds((1, D_MODEL), F32), _sds((1, D_MODEL), F32)],
        scratch_shapes=[pltpu.VMEM((N_SHARD, TM, FF_S), F32), pltpu.VMEM((N_SHARD, TM, FF_S), F32)],
        args=(x1, target, w_gate, w_up, w_down, g_pre, g_post), comms=comms)


def _wgrad(a, b, a_spec, b_spec, out_block, name, comms=()):
    def body(a_ref, b_ref, o_ref):
        av = a_ref[0] if len(a_ref.shape) == 3 else a_ref[...]
        bv = b_ref[0] if len(b_ref.shape) == 3 else b_ref[...]
        o_ref[0] = _dot_tn(av, bv)

    return _pcall(
        body, name=name, grid=(N_SHARD,),
        in_specs=[a_spec, b_spec],
        out_specs=pl.BlockSpec((1,) + out_block, lambda s: (s, 0, 0)),
        out_shape=_sds((N_SHARD,) + out_block, F32),
        args=(a, b), comms=comms)


def _outproj_bwd(dx1, y, attn, sgu, w_out, g_post, g_attn, g_sgu, comms=()):
    def body(dx1_ref, y_ref, attn_ref, sgu_ref, w_ref, gp_ref, ga_ref, gs_ref,
             dy_ref, dattn_ref, delta_ref, dsgu_ref, dgp_ref, dga_ref, dgs_ref):
        @pl.when(pl.program_id(0) == 0)
        def _():
            dgp_ref[...] = jnp.zeros_like(dgp_ref)
            dga_ref[...] = jnp.zeros_like(dga_ref)
            dgs_ref[...] = jnp.zeros_like(dgs_ref)

        yh, ry = _rms_stats(y_ref[...])
        dy, dgp = _rms_bwd(yh, ry, gp_ref[...], dx1_ref[...])
        dgp_ref[...] += dgp
        dyb = dy.astype(BF16)
        dy_ref[...] = dyb
        dmixed = jnp.concatenate([_dot_nt(dyb, w_ref[s]) for s in range(N_SHARD)], axis=1)
        attn = attn_ref[...]
        ah, ra = _rms_stats(attn)
        dattn, dga = _rms_bwd(ah, ra, ga_ref[...], dmixed[:, 0:512])
        dga_ref[...] += dga
        sh, rs = _rms_stats(sgu_ref[...])
        dsgu, dgs = _rms_bwd(sh, rs, gs_ref[...], dmixed[:, 512:1024])
        dgs_ref[...] += dgs
        dattn_ref[...] = dattn.astype(BF16)
        dsgu_ref[...] = dsgu
        la = lax.broadcasted_iota(jnp.int32, (ATTN_W, ATTN_W), 0) >> 6
        lb = lax.broadcasted_iota(jnp.int32, (ATTN_W, ATTN_W), 1) >> 6
        delta_ref[...] = _dot_exact(dattn * attn, (la == lb).astype(F32))

    return _pcall(
        body, name="outproj_bwd", grid=(SEQ // TM,),
        in_specs=[_rows(D_MODEL), _rows(D_MODEL), _rows(512), _rows(512), _resident((N_SHARD, OUT_S, D_MODEL)),
                  _resident((1, D_MODEL)), _resident((1, 512)), _resident((1, 512))],
        out_specs=[_rows(D_MODEL), _rows(512), _rows(512), _rows(512), _acc(D_MODEL), _acc(512), _acc(512)],
        out_shape=[_sds((SEQ, D_MODEL), BF16), _sds((SEQ, 512), BF16), _sds((SEQ, 512), F32), _sds((SEQ, 512), F32),
                   _sds((1, D_MODEL), F32), _sds((1, 512), F32), _sds((1, 512), F32)],
        args=(dx1, y, attn, sgu, w_out, g_post, g_attn, g_sgu), comms=comms)


def _sgu_bwd(u, vs, dsgu, lg, lb, w_sp, b_t, comms=()):
    nsteps = SEQ // TM

    def body(u_ref, vs_ref, ds_ref, lg_ref, lb_ref, w_ref, bt_ref,
             du_ref, dvs_ref, dw_ref, db_ref, dlg_ref, dlb_ref, dbias_scr):
        i = pl.program_id(0)

        @pl.when(i == 0)
        def _():
            dw_ref[...] = jnp.zeros_like(dw_ref)
            dlg_ref[...] = jnp.zeros_like(dlg_ref)
            dlb_ref[...] = jnp.zeros_like(dlb_ref)
            dbias_scr[...] = jnp.zeros_like(dbias_scr)

        wm = _masked_spatial(w_ref)
        ones_g = _group_ones()
        bias_full = _dot_exact(bt_ref[...], ones_g)
        u = u_ref[...]
        vs = vs_ref[...]
        lg = lg_ref[...]
        gu, xh, rstd, vnb, mixed = _sgu_core(u, vs, lg, lb_ref[...], wm, bias_full)
        dsgu = ds_ref[...]
        du_ref[...] = (dsgu * mixed * _gelu_grad(u)).astype(BF16)
        dmixed = dsgu * gu
        left = _left_half()
        dvn_rows = []
        for c in range(TM // CHUNK):
            rs = slice(c * CHUNK, (c + 1) * CHUNK)
            dm_c = dmixed[rs, :]
            dbias_scr[...] += dm_c
            pieces = []
            for p in range(4):
                ls = slice(p * 128, (p + 1) * 128)
                dmp = dm_c[:, ls]
                vp = vnb[rs, ls]
                dmb = dmp.astype(BF16)
                zero = jnp.zeros_like(dmb)
                dw_ref[2 * p] += _dot_nt(jnp.where(left, dmb, zero), vp)
                dw_ref[2 * p + 1] += _dot_nt(jnp.where(left, zero, dmb), vp)
                pieces.append(jnp.where(left, _dot_tn(wm[2 * p], dmb), _dot_tn(wm[2 * p + 1], dmb)))
            dvn_rows.append(jnp.concatenate(pieces, axis=1))
        dvn = jnp.concatenate(dvn_rows, axis=0)
        dlg_ref[...] += jnp.sum(dvn * xh, axis=0, keepdims=True)
        dlb_ref[...] += jnp.sum(dvn, axis=0, keepdims=True)
        dxh = dvn * lg
        dgv = rstd * (dxh - jnp.mean(dxh, axis=-1, keepdims=True) - xh * jnp.mean(dxh * xh, axis=-1, keepdims=True))
        dvs_ref[...] = (dgv * _gelu_grad(vs)).astype(BF16)

        @pl.when(i == nsteps - 1)
        def _():
            row = lax.broadcasted_iota(jnp.int32, (CHUNK, CHUNK), 0)
            col = lax.broadcasted_iota(jnp.int32, (CHUNK, CHUNK), 1)
            for g in range(SGU_GROUPS):
                dw_ref[g] = jnp.where(col <= row, dw_ref[g], 0.0)
            db_ref[...] = lax.dot_general(ones_g, dbias_scr[...], (((1,), (1,)), ((), ())),
                                          preferred_element_type=F32, precision=lax.Precision.HIGHEST)

    return _pcall(
        body, name="sgu_bwd", grid=(nsteps,),
        in_specs=[_rows(SGU_W), _rows(SGU_W), _rows(SGU_W), _resident((1, SGU_W)), _resident((1, SGU_W)),
                  _resident((SGU_GROUPS, CHUNK, CHUNK)), _resident((CHUNK, SGU_GROUPS))],
        out_specs=[_rows(SGU_W), _rows(SGU_W), pl.BlockSpec((SGU_GROUPS, CHUNK, CHUNK), lambda i: (0, 0, 0)),
                   _acc(CHUNK, SGU_GROUPS), _acc(SGU_W), _acc(SGU_W)],
        out_shape=[_sds((SEQ, SGU_W), BF16), _sds((SEQ, SGU_W), BF16), _sds((SGU_GROUPS, CHUNK, CHUNK), F32),
                   _sds((SGU_GROUPS, CHUNK), F32), _sds((1, SGU_W), F32), _sds((1, SGU_W), F32)],
        scratch_shapes=[pltpu.VMEM((CHUNK, SGU_W), F32)],
        args=(u, vs, dsgu, lg, lb, w_sp, b_t), comms=comms)


def _attn_bwd(qv, kv, vv, dov, deltav, lsev, dil, comms=()):
    seg = SEQ // dil
    nblk = seg // CHUNK

    def body(q_ref, k_ref, v_ref, do_ref, dl_ref, lse_ref, dq_ref, dk_ref, dv_ref):
        left = _left_half()
        m_cur, m_prev = _block_masks()

        def blk(b, carry):
            r0 = pl.multiple_of(b * CHUNK, CHUNK)
            rp = pl.multiple_of(jnp.maximum(b - 1, 0) * CHUNK, CHUNK)
            prev_ok = m_prev & (b > 0)
            sides = tuple(enumerate((left, ~left)))
            zero = jnp.zeros((CHUNK, CHUNK), BF16)
            tiles, firsts = [], []
            for hp in range(4):
                ls = slice(hp * 128, (hp + 1) * 128)
                qp = q_ref[pl.ds(r0, CHUNK), ls]
                kc = k_ref[pl.ds(r0, CHUNK), ls]
                vc = v_ref[pl.ds(r0, CHUNK), ls]
                dop = do_ref[pl.ds(r0, CHUNK), ls]
                kp = k_ref[pl.ds(rp, CHUNK), ls] if nblk > 1 else None
                vp = v_ref[pl.ds(rp, CHUNK), ls] if nblk > 1 else None
                tiles.append((ls, kc, kp))
                for side, hm in sides:
                    qh = jnp.where(hm, qp, zero)
                    doh = jnp.where(hm, dop, zero)
                    cur = (_dot_nt(qh, kc), _dot_nt(doh, vc))
                    prev = (_dot_nt(qh, kp), _dot_nt(doh, vp)) if nblk > 1 else None
                    firsts.append((qh, doh, cur, prev))
            seconds = []
            for i, (qh, doh, cur, prev) in enumerate(firsts):
                hp, c0 = i // 2, (i % 2) * HEAD_DIM
                ls = slice(hp * 128, (hp + 1) * 128)
                lse_h = lse_ref[pl.ds(r0, CHUNK), ls][:, c0:c0 + 1]
                dl_h = dl_ref[pl.ds(r0, CHUNK), ls][:, c0:c0 + 1]
                pc = jnp.exp(jnp.where(m_cur, cur[0] - lse_h, NEG))
                out = [pc.astype(BF16), (pc * (cur[1] - dl_h)).astype(BF16), None, None]
                if nblk > 1:
                    pp = jnp.exp(jnp.where(prev_ok, prev[0] - lse_h, NEG))
                    out[2:] = [pp.astype(BF16), (pp * (prev[1] - dl_h)).astype(BF16)]
                seconds.append(out)
            for hp, (ls, kc, kp) in enumerate(tiles):
                dq = jnp.zeros((CHUNK, CHUNK), F32)
                dkc = jnp.zeros((CHUNK, CHUNK), F32)
                dvc = jnp.zeros((CHUNK, CHUNK), F32)
                dkp = jnp.zeros((CHUNK, CHUNK), F32)
                dvp = jnp.zeros((CHUNK, CHUNK), F32)
                for side, hm in sides:
                    qh, doh, _, _ = firsts[2 * hp + side]
                    pcb, dsc, ppb, dsp = seconds[2 * hp + side]
                    dq = dq + _dot(dsc, jnp.where(hm, kc, zero))
                    dkc = dkc + _dot_tn(dsc, qh)
                    dvc = dvc + _dot_tn(pcb, doh)
                    if nblk > 1:
                        dq = dq + _dot(dsp, jnp.where(hm, kp, zero))
                        dkp = dkp + _dot_tn(dsp, qh)
                        dvp = dvp + _dot_tn(ppb, doh)
                dq_ref[pl.ds(r0, CHUNK), ls] = dq
                dk_ref[pl.ds(r0, CHUNK), ls] = dkc
                dv_ref[pl.ds(r0, CHUNK), ls] = dvc
                if nblk > 1:
                    @pl.when(b > 0)
                    def _():
                        dk_ref[pl.ds(rp, CHUNK), ls] += dkp
                        dv_ref[pl.ds(rp, CHUNK), ls] += dvp
            return carry

        lax.fori_loop(0, nblk, blk, 0)

    spec = pl.BlockSpec((seg, ATTN_W), lambda r: (0, r))
    return _pcall(
        body, name=f"attn_bwd_d{dil}", grid=(dil,),
        in_specs=[spec] * 6, out_specs=[spec] * 3,
        out_shape=[_sds((seg, dil * ATTN_W), F32)] * 3,
        args=(qv, kv, vv, dov, deltav, lsev), comms=comms)


def _inproj_bwd(dqs, dks, dvs, du, dvs_sgu, pos, x, dx1, w_in, g_pre, comms=()):
    def body(dq1, dq2, dq3, dk1, dk2, dk3, dv1, dv2, dv3, du_ref, dvs_ref, pos_ref, x_ref, dx1_ref, w_ref, g_ref,
             dproj_ref, gx_ref, dg_ref):
        @pl.when(pl.program_id(0) == 0)
        def _():
            dg_ref[...] = jnp.zeros_like(dg_ref)

        tabs = _rot_tables(pos_ref[...])
        dproj_ref[:, 0:512] = _rope_bwd((dq1[...] + dq2[...] + dq3[...]) * np.float32(ATTN_SCALE), tabs).astype(BF16)
        dproj_ref[:, 512:1024] = _rope_bwd(dk1[...] + dk2[...] + dk3[...], tabs).astype(BF16)
        dproj_ref[:, 1024:1536] = (dv1[...] + dv2[...] + dv3[...]).astype(BF16)
        dproj_ref[:, 1536:2048] = du_ref[...]
        dproj_ref[:, 2048:2560] = dvs_ref[...]
        dh = jnp.zeros((TM, D_MODEL), F32)
        for s in range(N_SHARD):
            dh = dh + _dot_nt(dproj_ref[:, s * IN_S:(s + 1) * IN_S], w_ref[s])
        g = g_ref[...]
        xh, r = _rms_stats(x_ref[...])
        dx, dg = _rms_bwd(xh, r, g, dh)
        dg_ref[...] += dg
        gx_ref[...] = dx1_ref[...] + dx

    return _pcall(
        body, name="inproj_bwd", grid=(SEQ // TM,),
        in_specs=[_rows(512)] * 11 + [_rows(1), _rows(D_MODEL), _rows(D_MODEL),
                                     _resident((N_SHARD, D_MODEL, IN_S)), _resident((1, D_MODEL))],
        out_specs=[_rows(PROJ_W), _rows(D_MODEL), _acc(D_MODEL)],
        out_shape=[_sds((SEQ, PROJ_W), BF16), _sds((SEQ, D_MODEL), F32), _sds((1, D_MODEL), F32)],
        args=(*dqs, *dks, *dvs, du, dvs_sgu, pos, x, dx1, w_in, g_pre), comms=comms)


def _to_view(a, dil):
    return a if dil == 1 else a.reshape(SEQ // dil, dil * a.shape[1])


def _from_view(a, dil):
    return a if dil == 1 else a.reshape(SEQ, a.shape[1] // dil)


def _local_step(x, pos, target, w_in, w_out, w_gate, w_up, w_down, small):
    b_t = small["sgu_b_spatial"].T
    h, q, k, v, u, vs = _inproj_fwd(x, pos, small["pre_mix_norm"], w_in)
    sgu = _sgu_fwd(u, vs, small["sgu_ln_gain"], small["sgu_ln_bias"], small["sgu_w_spatial"], b_t)
    views = [tuple(_to_view(t, dil) for t in (q, k, v)) for dil in DILATIONS]
    o_list, l_list = [], []
    for dil, (qv, kv, vv) in zip(DILATIONS, views):
        o, l = _attn_fwd(qv, kv, vv, dil)
        o_list.append(_from_view(o, dil))
        l_list.append(_from_view(l, dil))
    attn, lse, mixed, y, x1 = _mix_out_fwd(o_list, l_list, sgu, x, w_out, small["attn_out_norm"],
                                           small["sgu_out_norm"], small["post_mix_norm"])
    h2, a, dg, dup, df, dx1, loss_cols, d_pre_ffn, d_post_ffn = _ffn_fwd_bwd(
        x1, target, w_gate, w_up, w_down, small["pre_ffn_norm"], small["post_ffn_norm"])

    full_tok = pl.BlockSpec((SEQ, D_MODEL), lambda s: (0, 0), pipeline_mode=pl.Buffered(1))
    ff_tok = pl.BlockSpec((1, SEQ, FF_S), lambda s: (s, 0, 0))
    gw_gate = _wgrad(h2, dg, full_tok, ff_tok, (D_MODEL, FF_S), "wgrad_gate")
    gw_up = _wgrad(h2, dup, full_tok, ff_tok, (D_MODEL, FF_S), "wgrad_up")
    gw_down = _wgrad(a, df, ff_tok, full_tok, (FF_S, D_MODEL), "wgrad_down")

    dy, dattn, delta, dsgu, d_post_mix, d_attn_norm, d_sgu_norm = _outproj_bwd(
        dx1, y, attn, sgu, w_out, small["post_mix_norm"], small["attn_out_norm"], small["sgu_out_norm"])
    gw_out = _wgrad(mixed, dy, pl.BlockSpec((SEQ, OUT_S), lambda s: (0, s)), full_tok, (OUT_S, D_MODEL), "wgrad_out")
    du, dvs_sgu, d_w_sp, d_b_sp, d_ln_gain, d_ln_bias = _sgu_bwd(
        u, vs, dsgu, small["sgu_ln_gain"], small["sgu_ln_bias"], small["sgu_w_spatial"], b_t)

    dqs, dks, dvs = [], [], []
    for dil, (qv, kv, vv) in zip(DILATIONS, views):
        dq, dk, dv = _attn_bwd(qv, kv, vv, _to_view(dattn, dil), _to_view(delta, dil), _to_view(lse, dil), dil)
        dqs.append(_from_view(dq, dil))
        dks.append(_from_view(dk, dil))
        dvs.append(_from_view(dv, dil))
    dproj, grad_x, d_pre_mix = _inproj_bwd(dqs, dks, dvs, du, dvs_sgu, pos, x, dx1, w_in, small["pre_mix_norm"])
    gw_in = _wgrad(h, dproj, full_tok, pl.BlockSpec((SEQ, IN_S), lambda s: (0, s)), (D_MODEL, IN_S), "wgrad_in")

    small_grads = {
        "pre_mix_norm": d_pre_mix, "sgu_ln_gain": d_ln_gain, "sgu_ln_bias": d_ln_bias, "sgu_w_spatial": d_w_sp,
        "sgu_b_spatial": d_b_sp, "attn_out_norm": d_attn_norm, "sgu_out_norm": d_sgu_norm,
        "post_mix_norm": d_post_mix, "pre_ffn_norm": d_pre_ffn, "post_ffn_norm": d_post_ffn,
    }
    return loss_cols, grad_x, (gw_in, gw_out, gw_gate, gw_up, gw_down), small_grads


def _coords():
    return lax.axis_index("x"), lax.axis_index("y"), lax.axis_index("c")


def _other_chips(x, y):
    return [(1 - x, y), (x, 1 - y), (1 - x, 1 - y)]


def _comm_call(body, name, n_in, out_shape, scratch_shapes):
    return pl.pallas_call(
        body, name=name, in_specs=[ANY] * n_in, out_specs=[ANY] * len(out_shape), out_shape=out_shape,
        scratch_shapes=scratch_shapes,
        compiler_params=pltpu.CompilerParams(has_side_effects=True),
    )


def _gather_weights(shards):
    n = len(shards)
    halves = [s.reshape(2, s.shape[0] // 2, s.shape[1]) for s in shards]

    def body(*refs):
        ins, outs = refs[:n], refs[n:2 * n]
        send_sems, recv_sems = refs[2 * n:]
        x, y, c = _coords()
        s_me = 2 * x + y
        chips = _other_chips(x, y)
        sibling = (x, y, 1 - c)

        def copy(k, w, shard, cc, to):
            src = ins[w].at[cc] if shard is None else outs[w].at[shard, cc]
            dst = outs[w].at[s_me if shard is None else shard, cc]
            return pltpu.make_async_remote_copy(src_ref=src, dst_ref=dst, send_sem=send_sems.at[k],
                                                recv_sem=recv_sems.at[k], device_id=to, device_id_type=MESH)

        first = [copy(j * n + w, w, None, c, (cx, cy, c)) for j, (cx, cy) in enumerate(chips) for w in range(n)]
        for cp in first:
            cp.start()
        passed = []
        for j, (cx, cy) in enumerate(chips):
            for w in range(n):
                copy(j * n + w, w, 2 * cx + cy, c, (x, y, c)).wait_recv()
                fw = copy((3 + j) * n + w, w, 2 * cx + cy, c, sibling)
                fw.start()
                passed.append(fw)
        for j, (cx, cy) in enumerate(chips):
            for w in range(n):
                copy((3 + j) * n + w, w, 2 * cx + cy, 1 - c, (x, y, c)).wait_recv()
        for cp in first + passed:
            cp.wait_send()

    out_shape = [_sds((N_SHARD,) + h.shape, h.dtype) for h in halves]
    scratch = [pltpu.SemaphoreType.DMA((6 * n,)), pltpu.SemaphoreType.DMA((6 * n,))]
    full = _comm_call(body, "comm_gather_weights", n, out_shape, scratch)(*halves)
    s_me = 2 * lax.axis_index("x") + lax.axis_index("y")
    full = [lax.dynamic_update_slice(f, h[None], (s_me, 0, 0, 0)) for f, h in zip(full, halves)]
    return [f.reshape((N_SHARD,) + s.shape) for f, s in zip(full, shards)]


def _rs_to_sibling(gws):
    n = len(gws)

    def body(*refs):
        ins, outs = refs[:n], refs[n:2 * n]
        send_sems, recv_sems = refs[2 * n:]
        x, y, c = _coords()
        copies = []
        for w in range(n):
            hw = gws[w].shape[1] // 2
            copies.append(pltpu.make_async_remote_copy(
                src_ref=ins[w].at[:, pl.ds((1 - c) * hw, hw), :], dst_ref=outs[w], send_sem=send_sems.at[w],
                recv_sem=recv_sems.at[w], device_id=(x, y, 1 - c), device_id_type=MESH))
        for cp in copies:
            cp.start()
        for cp in copies:
            cp.wait()

    out_shape = [_sds((N_SHARD, g.shape[1] // 2, g.shape[2]), g.dtype) for g in gws]
    scratch = [pltpu.SemaphoreType.DMA((n,)), pltpu.SemaphoreType.DMA((n,))]
    return _comm_call(body, "comm_rs_sibling", n, out_shape, scratch)(*gws)


def _rs_chip_sum(gw, recv, core):
    _, rows, cols = gw.shape
    hw = rows // 2

    def body(c_ref, g_ref, r_ref, o_ref):
        o_ref[...] = (g_ref[...] + r_ref[...]).astype(BF16)

    return pl.pallas_call(
        body, name="rs_chip_sum",
        grid_spec=pltpu.PrefetchScalarGridSpec(
            num_scalar_prefetch=1, grid=(N_SHARD,),
            in_specs=[pl.BlockSpec((1, hw, cols), lambda s, c_ref: (s, c_ref[0], 0)),
                      pl.BlockSpec((1, hw, cols), lambda s, c_ref: (s, 0, 0))],
            out_specs=pl.BlockSpec((1, hw, cols), lambda s, c_ref: (s, 0, 0))),
        out_shape=_sds((N_SHARD, hw, cols), BF16),
        compiler_params=_seq_params(),
    )(core, gw, recv)


def _rs_between_chips(pbs):
    n = len(pbs)

    def body(*refs):
        ins, outs = refs[:n], refs[n:2 * n]
        send_sems, recv_sems = refs[2 * n:]
        x, y, c = _coords()
        copies = []
        for j, (cx, cy) in enumerate(_other_chips(x, y)):
            for w in range(n):
                copies.append(pltpu.make_async_remote_copy(
                    src_ref=ins[w].at[2 * cx + cy], dst_ref=outs[w].at[j], send_sem=send_sems.at[j * n + w],
                    recv_sem=recv_sems.at[j * n + w], device_id=(cx, cy, c), device_id_type=MESH))
        for cp in copies:
            cp.start()
        for cp in copies:
            cp.wait()

    out_shape = [_sds((3,) + p.shape[1:], p.dtype) for p in pbs]
    scratch = [pltpu.SemaphoreType.DMA((3 * n,)), pltpu.SemaphoreType.DMA((3 * n,))]
    return _comm_call(body, "comm_rs_chips", n, out_shape, scratch)(*pbs)


def _rs_final_sum(gw, recv_sib, recv_chips, shard_core):
    _, rows, cols = gw.shape
    hw = rows // 2

    def body(sc_ref, g_ref, r_ref, rc_ref, o_ref):
        acc = g_ref[0] + r_ref[0]
        for j in range(3):
            acc = acc + rc_ref[j].astype(F32)
        o_ref[0] = acc

    return pl.pallas_call(
        body, name="rs_final_sum",
        grid_spec=pltpu.PrefetchScalarGridSpec(
            num_scalar_prefetch=1, grid=(1,),
            in_specs=[pl.BlockSpec((1, hw, cols), lambda i, sc: (sc[0], sc[1], 0)),
                      pl.BlockSpec((1, hw, cols), lambda i, sc: (sc[0], 0, 0)),
                      pl.BlockSpec((3, hw, cols), lambda i, sc: (0, 0, 0))],
            out_specs=pl.BlockSpec((1, hw, cols), lambda i, sc: (sc[1], 0, 0))),
        out_shape=_sds((2, hw, cols), F32),
        compiler_params=_seq_params(),
    )(shard_core, gw, recv_sib, recv_chips)


def _rs_join_halves(halves):
    n = len(halves)

    def body(*refs):
        bufs = refs[n:2 * n]
        send_sems, recv_sems = refs[2 * n:]
        x, y, c = _coords()
        remote = [pltpu.make_async_remote_copy(
            src_ref=bufs[w].at[c], dst_ref=bufs[w].at[c], send_sem=send_sems.at[w], recv_sem=recv_sems.at[w],
            device_id=(x, y, 1 - c), device_id_type=MESH) for w in range(n)]
        for cp in remote:
            cp.start()
        for w in range(n):
            remote[w].wait_send()
            pltpu.make_async_remote_copy(
                src_ref=bufs[w].at[c], dst_ref=bufs[w].at[1 - c], send_sem=send_sems.at[w],
                recv_sem=recv_sems.at[w], device_id=(x, y, c), device_id_type=MESH).wait_recv()

    joined = pl.pallas_call(
        body, name="comm_rs_join", in_specs=[ANY] * n, out_specs=[ANY] * n,
        out_shape=[_sds(h.shape, h.dtype) for h in halves], input_output_aliases={w: w for w in range(n)},
        scratch_shapes=[pltpu.SemaphoreType.DMA((n,)), pltpu.SemaphoreType.DMA((n,))],
        compiler_params=pltpu.CompilerParams(has_side_effects=True),
    )(*halves)
    return [j.reshape(2 * h.shape[1], h.shape[2]) for j, h in zip(joined, halves)]


def _allreduce_small(buf):
    rows, cols = buf.shape

    def body(in_ref, out_ref, slots, send_sems, recv_sems):
        x, y, c = _coords()
        me = 4 * x + 2 * y + c
        copies, peers = [], []
        for k in range(1, 8):
            px = 1 - x if (k >> 2) & 1 else x
            py = 1 - y if (k >> 1) & 1 else y
            pc = 1 - c if k & 1 else c
            peers.append(4 * px + 2 * py + pc)
            copies.append(pltpu.make_async_remote_copy(
                src_ref=in_ref, dst_ref=slots.at[me], send_sem=send_sems.at[k - 1], recv_sem=recv_sems.at[k - 1],
                device_id=(px, py, pc), device_id_type=MESH))
        for cp in copies:
            cp.start()
        slots[me] = in_ref[...]
        for k in range(7):
            pltpu.make_async_remote_copy(
                src_ref=in_ref, dst_ref=slots.at[peers[k]], send_sem=send_sems.at[k], recv_sem=recv_sems.at[k],
                device_id=(x, y, c), device_id_type=MESH).wait_recv()
        for cp in copies:
            cp.wait_send()
        acc = slots[0]
        for i in range(1, 8):
            acc = acc + slots[i]
        out_ref[...] = acc

    vmem = pl.BlockSpec(memory_space=pltpu.VMEM)
    return pl.pallas_call(
        body, name="comm_allreduce_small", in_specs=[vmem], out_specs=vmem, out_shape=_sds((rows, cols), F32),
        scratch_shapes=[pltpu.VMEM((8, rows, cols), F32), pltpu.SemaphoreType.DMA((7,)), pltpu.SemaphoreType.DMA((7,))],
        compiler_params=pltpu.CompilerParams(has_side_effects=True, vmem_limit_bytes=VMEM_LIMIT),
    )(buf)


def _adamw(w, g, m, v, block_rows, name):
    rows, cols = w.shape

    def body(w_ref, g_ref, m_ref, v_ref, d_ref, nm_ref, nv_ref):
        g = g_ref[...]
        m = ADAM_B1 * m_ref[...] + (1.0 - ADAM_B1) * g
        v = ADAM_B2 * v_ref[...] + (1.0 - ADAM_B2) * (g * g)
        m_hat = m / (1.0 - ADAM_B1 ** ADAM_STEP)
        v_hat = v / (1.0 - ADAM_B2 ** ADAM_STEP)
        d_ref[...] = -ADAM_LR * (m_hat / (jnp.sqrt(v_hat) + ADAM_EPS) + ADAM_WD * w_ref[...])
        nm_ref[...] = m
        nv_ref[...] = v

    spec = pl.BlockSpec((block_rows, cols), lambda i: (i, 0))
    return pl.pallas_call(
        body, name=name, grid=(rows // block_rows,), in_specs=[spec] * 4, out_specs=[spec] * 3,
        out_shape=[_sds((rows, cols), F32)] * 3,
        compiler_params=_seq_params(),
    )(w, g, m, v)


WEIGHTS = ("pre_mix_norm", "w_in", "sgu_ln_gain", "sgu_ln_bias", "sgu_w_spatial", "sgu_b_spatial", "attn_out_norm",
           "sgu_out_norm", "w_out", "post_mix_norm", "pre_ffn_norm", "w_gate", "w_up", "w_down", "post_ffn_norm")
BIG = ("w_in", "w_out", "w_gate", "w_up", "w_down")
BIG_ADAM_ROWS = {"w_in": 256, "w_out": 128, "w_gate": 256, "w_up": 256, "w_down": 352}
SMALL = ("pre_mix_norm", "post_mix_norm", "pre_ffn_norm", "post_ffn_norm", "sgu_ln_gain", "sgu_ln_bias",
         "attn_out_norm", "sgu_out_norm", "sgu_w_spatial", "sgu_b_spatial")


def _pack_small(d):
    flat = [d[n].reshape(-1) for n in SMALL]
    used = sum(f.shape[0] for f in flat)
    flat.append(jnp.zeros((SMALL_ROWS * 1024 - used,), F32))
    return jnp.concatenate(flat).reshape(SMALL_ROWS, 1024)


def _unpack_small(buf, shapes):
    flat = buf.reshape(-1)
    out, off = {}, 0
    for n in SMALL:
        size = int(np.prod(shapes[n]))
        out[n] = flat[off:off + size].reshape(shapes[n])
        off += size
    return out


def _kernel_unoverlapped(x, positions, pre_mix_norm, w_in, sgu_ln_gain, sgu_ln_bias, sgu_w_spatial, sgu_b_spatial, attn_out_norm, sgu_out_norm, w_out, post_mix_norm, pre_ffn_norm, w_gate, w_up, w_down, post_ffn_norm, loss_target, m_pre_mix_norm, m_w_in, m_sgu_ln_gain, m_sgu_ln_bias, m_sgu_w_spatial, m_sgu_b_spatial, m_attn_out_norm, m_sgu_out_norm, m_w_out, m_post_mix_norm, m_pre_ffn_norm, m_w_gate, m_w_up, m_w_down, m_post_ffn_norm, v_pre_mix_norm, v_w_in, v_sgu_ln_gain, v_sgu_ln_bias, v_sgu_w_spatial, v_sgu_b_spatial, v_attn_out_norm, v_sgu_out_norm, v_w_out, v_post_mix_norm, v_pre_ffn_norm, v_w_gate, v_w_up, v_w_down, v_post_ffn_norm):
    a = dict(locals())
    cx, cy, cc = _coords()
    core = jnp.stack([cc]).astype(jnp.int32)
    shard_core = jnp.stack([2 * cx + cy, cc]).astype(jnp.int32)

    full = _gather_weights([a[n][0].astype(BF16) for n in BIG])
    small = {n: (a[n][0] if a[n].ndim > 2 else a[n]) for n in SMALL}
    loss_cols, grad_x, gws, small_grads = _local_step(
        x[0], positions.reshape(SEQ, 1), loss_target[0], *full, small)
    loss = lax.psum(jnp.sum(loss_cols) * np.float32(0.5 / D_MODEL), ("x", "y", "c"))

    recv_sib = _rs_to_sibling(list(gws))
    chip_part = [_rs_chip_sum(g, r, core) for g, r in zip(gws, recv_sib)]
    recv_chips = _rs_between_chips(chip_part)
    halves = [_rs_final_sum(g, r, rc, shard_core) for g, r, rc in zip(gws, recv_sib, recv_chips)]
    big_grads = dict(zip(BIG, _rs_join_halves(halves)))

    shapes = {n: a[n].shape for n in SMALL}
    small_sum = _allreduce_small(_pack_small(small_grads))

    grads, deltas, new_m, new_v = {}, {}, {}, {}
    for n in BIG:
        grads[n] = big_grads[n][None]
        d, nm, nv = _adamw(a[n][0], big_grads[n], a["m_" + n][0], a["v_" + n][0], BIG_ADAM_ROWS[n], "adamw_" + n)
        deltas[n], new_m[n], new_v[n] = d[None], nm[None], nv[None]
    d, nm, nv = _adamw(_pack_small({n: a[n] for n in SMALL}), small_sum, _pack_small({n: a["m_" + n] for n in SMALL}),
                       _pack_small({n: a["v_" + n] for n in SMALL}), SMALL_ROWS, "adamw_small")
    grads.update(_unpack_small(small_sum, shapes))
    deltas.update(_unpack_small(d, shapes))
    new_m.update(_unpack_small(nm, shapes))
    new_v.update(_unpack_small(nv, shapes))
    return (loss, grad_x[None], *[grads[n] for n in WEIGHTS], *[deltas[n] for n in WEIGHTS],
            *[new_m[n] for n in WEIGHTS], *[new_v[n] for n in WEIGHTS])


def _remote(src, dst, send_sem, recv_sem, to):
    return pltpu.make_async_remote_copy(src_ref=src, dst_ref=dst, send_sem=send_sem, recv_sem=recv_sem,
                                        device_id=to, device_id_type=MESH)


def _halves(a):
    *lead, rows, cols = a.shape
    return a.reshape(*lead, 2, rows // 2, cols)


def _gather_ici(shards):
    n = len(shards)

    def desc(ins, outs, ss, rs, j, w, landed):
        x, y, c = _coords()
        cx, cy = _other_chips(x, y)[j]
        shard = 2 * cx + cy if landed else 2 * x + y
        return _remote(ins[w].at[c], outs[w].at[shard, c], ss.at[j * n + w], rs.at[j * n + w], (cx, cy, c))

    def start(ins, outs, ss, rs):
        for j in range(3):
            for w in range(n):
                desc(ins, outs, ss, rs, j, w, False).start()

    def finish(ins, outs, ss, rs):
        for j in range(3):
            for w in range(n):
                desc(ins, outs, ss, rs, j, w, True).wait_recv()
                desc(ins, outs, ss, rs, j, w, False).wait_send()

    return _Comm(shards, [_sds((N_SHARD,) + s.shape, s.dtype) for s in shards], 3 * n, start, finish)


def _gather_pass(fulls):
    n = len(fulls)

    def desc(bufs, ss, rs, j, w, landed):
        x, y, c = _coords()
        cx, cy = _other_chips(x, y)[j]
        shard = 2 * cx + cy
        return _remote(bufs[w].at[shard, c], bufs[w].at[shard, 1 - c if landed else c],
                       ss.at[j * n + w], rs.at[j * n + w], (x, y, 1 - c))

    def start(ins, outs, ss, rs):
        for j in range(3):
            for w in range(n):
                desc(outs, ss, rs, j, w, False).start()

    def finish(ins, outs, ss, rs):
        for j in range(3):
            for w in range(n):
                desc(outs, ss, rs, j, w, True).wait_recv()
                desc(outs, ss, rs, j, w, False).wait_send()

    return _Comm(fulls, [_sds(f.shape, f.dtype) for f in fulls], 3 * n, start, finish, aliased=True)


def _rs_sibling(gws):
    n = len(gws)

    def desc(ins, outs, ss, rs, w):
        x, y, c = _coords()
        return _remote(ins[w].at[:, 1 - c], outs[w], ss.at[w], rs.at[w], (x, y, 1 - c))

    def start(ins, outs, ss, rs):
        for w in range(n):
            desc(ins, outs, ss, rs, w).start()

    def finish(ins, outs, ss, rs):
        for w in range(n):
            desc(ins, outs, ss, rs, w).wait()

    out_shape = [_sds((N_SHARD, g.shape[1] // 2, g.shape[2]), g.dtype) for g in gws]
    return _Comm([_halves(g) for g in gws], out_shape, n, start, finish)


def _rs_chips(pbs):
    n = len(pbs)

    def desc(ins, outs, ss, rs, j, w):
        x, y, c = _coords()
        cx, cy = _other_chips(x, y)[j]
        return _remote(ins[w].at[2 * cx + cy], outs[w].at[j], ss.at[j * n + w], rs.at[j * n + w], (cx, cy, c))

    def start(ins, outs, ss, rs):
        for j in range(3):
            for w in range(n):
                desc(ins, outs, ss, rs, j, w).start()

    def finish(ins, outs, ss, rs):
        for j in range(3):
            for w in range(n):
                desc(ins, outs, ss, rs, j, w).wait()

    return _Comm(pbs, [_sds((3,) + p.shape[1:], p.dtype) for p in pbs], 3 * n, start, finish)


def _rs_join(halves):
    n = len(halves)

    def desc(bufs, ss, rs, w, landed):
        x, y, c = _coords()
        return _remote(bufs[w].at[c], bufs[w].at[1 - c if landed else c], ss.at[w], rs.at[w], (x, y, 1 - c))

    def start(ins, outs, ss, rs):
        for w in range(n):
            desc(outs, ss, rs, w, False).start()

    def finish(ins, outs, ss, rs):
        for w in range(n):
            desc(outs, ss, rs, w, True).wait_recv()
            desc(outs, ss, rs, w, False).wait_send()

    return _Comm(halves, [_sds(h.shape, h.dtype) for h in halves], n, start, finish, aliased=True)


def _small_exchange(buf):
    def desc(ins, outs, ss, rs, k, landed):
        x, y, c = _coords()
        px = 1 - x if (k >> 2) & 1 else x
        py = 1 - y if (k >> 1) & 1 else y
        pc = 1 - c if k & 1 else c
        slot = 4 * px + 2 * py + pc if landed else 4 * x + 2 * y + c
        return _remote(ins[0], outs[0].at[slot], ss.at[k - 1], rs.at[k - 1], (px, py, pc))

    def start(ins, outs, ss, rs):
        for k in range(1, 8):
            desc(ins, outs, ss, rs, k, False).start()

    def finish(ins, outs, ss, rs):
        for k in range(1, 8):
            desc(ins, outs, ss, rs, k, True).wait_recv()
            desc(ins, outs, ss, rs, k, False).wait_send()

    return _Comm([buf], [_sds((8,) + buf.shape, buf.dtype)], 7, start, finish)


HBM = pl.BlockSpec(memory_space=pltpu.HBM)
SEM = pl.BlockSpec(memory_space=pltpu.SEMAPHORE)
DATAFLOW = pltpu.SideEffectType.DATAFLOW_SIDE_EFFECTING


def _split_start(name, comm, lands, after):
    srcs = [pltpu.with_memory_space_constraint(s, pltpu.HBM) for s in comm.args]
    lands = [pltpu.with_memory_space_constraint(b, pltpu.HBM) for b in lands]
    ns, nb = len(srcs), len(lands)

    def body(*refs):
        send_sems, recv_sems = refs[ns + nb + 1], refs[ns + nb + 2]
        comm.start(refs[:ns], refs[ns:ns + nb], send_sems, recv_sems)
        refs[-1][...] = jnp.zeros_like(refs[-1])

    res = pl.pallas_call(
        body, name=name,
        out_shape=(pltpu.SemaphoreType.DMA((comm.n_sems,)), pltpu.SemaphoreType.DMA((comm.n_sems,)),
                   *[pltpu.HBM(b.shape, b.dtype) for b in srcs + lands], _sds((8, 128), F32)),
        in_specs=[HBM] * (ns + nb) + [ANY],
        out_specs=(SEM, SEM, *[HBM] * (ns + nb), pl.BlockSpec(memory_space=pltpu.VMEM)),
        input_output_aliases={i: 2 + i for i in range(ns + nb)},
        compiler_params=pltpu.CompilerParams(has_side_effects=DATAFLOW),
    )(*srcs, *lands, after)
    return res[0], res[1], list(res[2:2 + ns]), list(res[2 + ns:2 + ns + nb]), res[-1]


def _split_wait(name, comm, send_sems, recv_sems, srcs, lands, after):
    ns, nb = len(srcs), len(lands)

    def body(*refs):
        comm.finish(refs[:ns], refs[ns:ns + nb], refs[ns + nb], refs[ns + nb + 1])

    res = pl.pallas_call(
        body, name=name,
        out_shape=tuple(pltpu.HBM(b.shape, b.dtype) for b in srcs + lands),
        in_specs=[HBM] * (ns + nb) + [SEM, SEM, ANY], out_specs=tuple([HBM] * (ns + nb)),
        input_output_aliases={i: i for i in range(ns + nb)},
        compiler_params=pltpu.CompilerParams(has_side_effects=DATAFLOW),
    )(*srcs, *lands, send_sems, recv_sems, after)
    return list(res[ns:])


SMALL_EARLY = ("post_mix_norm", "pre_ffn_norm", "post_ffn_norm", "sgu_ln_gain", "sgu_ln_bias", "attn_out_norm",
               "sgu_out_norm", "sgu_w_spatial", "sgu_b_spatial")
SMALL_LATE = ("pre_mix_norm",)


def _pack(d, names, rows):
    flat = [d[n].reshape(-1) for n in names]
    used = sum(f.shape[0] for f in flat)
    flat.append(jnp.zeros((rows * 1024 - used,), F32))
    return jnp.concatenate(flat).reshape(rows, 1024)


def _unpack(buf, names, shapes):
    flat = buf.reshape(-1)
    out, off = {}, 0
    for n in names:
        size = int(np.prod(shapes[n]))
        out[n] = flat[off:off + size].reshape(shapes[n])
        off += size
    return out


def _comm_only(name, comms):
    return _pcall(lambda: None, name=name, grid=(1,), in_specs=[], out_specs=[], out_shape=[], args=(),
                  comms=comms)[1]


def _adam_math(w, g, m, v):
    m = ADAM_B1 * m + (1.0 - ADAM_B1) * g
    v = ADAM_B2 * v + (1.0 - ADAM_B2) * (g * g)
    m_hat = m / (1.0 - ADAM_B1 ** ADAM_STEP)
    v_hat = v / (1.0 - ADAM_B2 ** ADAM_STEP)
    return -ADAM_LR * (m_hat / (jnp.sqrt(v_hat) + ADAM_EPS) + ADAM_WD * w), m, v


def _adamw_small(own, slots, w, m, v, me):
    rows, cols = own.shape

    def body(me_ref, own_ref, slots_ref, w_ref, m_ref, v_ref, g_ref, d_ref, nm_ref, nv_ref):
        own_v = own_ref[...]
        g = jnp.where(me_ref[0] == 0, own_v, slots_ref[0])
        for i in range(1, 8):
            g = g + jnp.where(me_ref[0] == i, own_v, slots_ref[i])
        g_ref[...] = g
        d_ref[...], nm_ref[...], nv_ref[...] = _adam_math(w_ref[...], g, m_ref[...], v_ref[...])

    flat = pl.BlockSpec((rows, cols), lambda i, me_ref: (0, 0))
    return pl.pallas_call(
        body, name="adamw_small",
        grid_spec=pltpu.PrefetchScalarGridSpec(
            num_scalar_prefetch=1, grid=(1,),
            in_specs=[flat, pl.BlockSpec((8, rows, cols), lambda i, me_ref: (0, 0, 0)), flat, flat, flat],
            out_specs=[flat] * 4),
        out_shape=[_sds((rows, cols), F32)] * 4,
        compiler_params=_seq_params(),
    )(me, own, slots, w, m, v)


def kernel(x, positions, pre_mix_norm, w_in, sgu_ln_gain, sgu_ln_bias, sgu_w_spatial, sgu_b_spatial, attn_out_norm, sgu_out_norm, w_out, post_mix_norm, pre_ffn_norm, w_gate, w_up, w_down, post_ffn_norm, loss_target, m_pre_mix_norm, m_w_in, m_sgu_ln_gain, m_sgu_ln_bias, m_sgu_w_spatial, m_sgu_b_spatial, m_attn_out_norm, m_sgu_out_norm, m_w_out, m_post_mix_norm, m_pre_ffn_norm, m_w_gate, m_w_up, m_w_down, m_post_ffn_norm, v_pre_mix_norm, v_w_in, v_sgu_ln_gain, v_sgu_ln_bias, v_sgu_w_spatial, v_sgu_b_spatial, v_attn_out_norm, v_sgu_out_norm, v_w_out, v_post_mix_norm, v_pre_ffn_norm, v_w_gate, v_w_up, v_w_down, v_post_ffn_norm):
    a = dict(locals())
    cx, cy, cc = _coords()
    s_me = 2 * cx + cy
    core = jnp.stack([cc]).astype(jnp.int32)
    shard_core = jnp.stack([s_me, cc]).astype(jnp.int32)
    me = jnp.stack([4 * cx + 2 * cy + cc]).astype(jnp.int32)
    small = {n: (a[n][0] if a[n].ndim > 2 else a[n]) for n in SMALL}
    b_t = small["sgu_b_spatial"].T
    xs, pos, target = x[0], positions.reshape(SEQ, 1), loss_target[0]
    own = {n: _halves(a[n][0].astype(BF16)) for n in BIG}

    def with_own(full, n):
        full = lax.dynamic_update_slice(full, own[n][None], (s_me, 0, 0, 0))
        return full.reshape((N_SHARD,) + a[n].shape[1:])

    (w_in_f,) = _gather_weights([a["w_in"][0].astype(BF16)])
    ffn_names = ("w_gate", "w_up", "w_down")
    ffn_gather = _gather_ici([own[n] for n in ffn_names])
    ffn_lands = [lax.empty(o.shape, o.dtype) for o in ffn_gather.out_shape]
    f_send, f_recv, f_srcs, ffn_lands, token = _split_start("gather_ffn_start", ffn_gather, ffn_lands, w_in_f)
    g_pre_mix = small["pre_mix_norm"] + token[0:1, 0:1]
    (h, q, k, v, u, vs), ((l_out,),) = _inproj_fwd(xs, pos, g_pre_mix, w_in_f,
                                                   comms=[_gather_ici([own["w_out"]])])
    views = [tuple(_to_view(t, dil) for t in (q, k, v)) for dil in DILATIONS]
    o_list, l_list = [], []
    for dil, (qv, kv, vv) in zip(DILATIONS, views):
        o, l = _attn_fwd(qv, kv, vv, dil)
        o_list.append(_from_view(o, dil))
        l_list.append(_from_view(l, dil))
    sgu, ((l_out,),) = _sgu_fwd(u, vs, small["sgu_ln_gain"], small["sgu_ln_bias"], small["sgu_w_spatial"], b_t,
                                comms=[_gather_pass([l_out])])
    w_out_f = with_own(l_out, "w_out")
    ffn_lands = _split_wait("gather_ffn_wait", ffn_gather, f_send, f_recv, f_srcs, ffn_lands, sgu)
    (attn, lse, mixed, y, x1), (ffn_w,) = _mix_out_fwd(
        o_list, l_list, sgu, xs, w_out_f, small["attn_out_norm"], small["sgu_out_norm"],
        small["post_mix_norm"], comms=[_gather_pass(ffn_lands)])
    w_gate_f, w_up_f, w_down_f = (with_own(f, n) for f, n in zip(ffn_w, ("w_gate", "w_up", "w_down")))
    h2, act, dg, dup, df, dx1, loss_cols, d_pre_ffn, d_post_ffn = _ffn_fwd_bwd(
        x1, target, w_gate_f, w_up_f, w_down_f, small["pre_ffn_norm"], small["post_ffn_norm"])
    loss = lax.psum(jnp.sum(loss_cols) * np.float32(0.5 / D_MODEL), ("x", "y", "c"))

    full_tok = pl.BlockSpec((SEQ, D_MODEL), lambda s: (0, 0), pipeline_mode=pl.Buffered(1))
    ff_tok = pl.BlockSpec((1, SEQ, FF_S), lambda s: (s, 0, 0))
    gw = {}
    gw["w_gate"] = _wgrad(h2, dg, full_tok, ff_tok, (D_MODEL, FF_S), "wgrad_gate")
    gw["w_up"], ((sib_gate,),) = _wgrad(h2, dup, full_tok, ff_tok, (D_MODEL, FF_S), "wgrad_up",
                                        comms=[_rs_sibling([gw["w_gate"]])])
    gw["w_down"], ((sib_up,),) = _wgrad(act, df, ff_tok, full_tok, (FF_S, D_MODEL), "wgrad_down",
                                        comms=[_rs_sibling([gw["w_up"]])])
    (dy, dattn, delta, dsgu, d_post_mix, d_attn_norm, d_sgu_norm), ((sib_down,),) = _outproj_bwd(
        dx1, y, attn, sgu, w_out_f, small["post_mix_norm"], small["attn_out_norm"],
        small["sgu_out_norm"], comms=[_rs_sibling([gw["w_down"]])])
    sib = {"w_gate": sib_gate, "w_up": sib_up, "w_down": sib_down}
    part = {n: _rs_chip_sum(gw[n], sib[n], core) for n in ("w_gate", "w_up", "w_down")}
    gw["w_out"] = _wgrad(mixed, dy, pl.BlockSpec((SEQ, OUT_S), lambda s: (0, s)), full_tok, (OUT_S, D_MODEL),
                         "wgrad_out")
    (du, dvs_sgu, d_w_sp, d_b_sp, d_ln_gain, d_ln_bias), ((far_gate,), (sib["w_out"],)) = _sgu_bwd(
        u, vs, dsgu, small["sgu_ln_gain"], small["sgu_ln_bias"], small["sgu_w_spatial"], b_t,
        comms=[_rs_chips([part["w_gate"]]), _rs_sibling([gw["w_out"]])])
    part["w_out"] = _rs_chip_sum(gw["w_out"], sib["w_out"], core)
    half, far, joined = {}, {"w_gate": far_gate}, {}
    half["w_gate"] = _rs_final_sum(gw["w_gate"], sib["w_gate"], far["w_gate"], shard_core)
    packed_early = _pack({
        "sgu_ln_gain": d_ln_gain, "sgu_ln_bias": d_ln_bias, "sgu_w_spatial": d_w_sp, "sgu_b_spatial": d_b_sp,
        "attn_out_norm": d_attn_norm, "sgu_out_norm": d_sgu_norm, "post_mix_norm": d_post_mix,
        "pre_ffn_norm": d_pre_ffn, "post_ffn_norm": d_post_ffn}, SMALL_EARLY, SMALL_ROWS)

    dqs, dks, dvs = [], [], []
    for dil, (qv, kv, vv) in zip(DILATIONS, views):
        if dil == 1:
            riders = [_rs_chips([part["w_up"], part["w_down"]])]
        elif dil == 4:
            riders = [_rs_chips([part["w_out"]]), _rs_join([half["w_gate"]]), _small_exchange(packed_early)]
        else:
            riders = [_rs_join([half["w_up"], half["w_down"]])]
        (dq, dk, dv), got = _attn_bwd(qv, kv, vv, _to_view(dattn, dil), _to_view(delta, dil), _to_view(lse, dil),
                                      dil, comms=riders)
        if dil == 1:
            far["w_up"], far["w_down"] = got[0]
            for n in ("w_up", "w_down"):
                half[n] = _rs_final_sum(gw[n], sib[n], far[n], shard_core)
        elif dil == 4:
            (far["w_out"],), (joined["w_gate"],), (slots_early,) = got
            half["w_out"] = _rs_final_sum(gw["w_out"], sib["w_out"], far["w_out"], shard_core)
        else:
            joined["w_up"], joined["w_down"] = got[0]
        dqs.append(_from_view(dq, dil))
        dks.append(_from_view(dk, dil))
        dvs.append(_from_view(dv, dil))
    (dproj, grad_x, d_pre_mix), ((joined["w_out"],),) = _inproj_bwd(
        dqs, dks, dvs, du, dvs_sgu, pos, xs, dx1, w_in_f, small["pre_mix_norm"], comms=[_rs_join([half["w_out"]])])
    packed_late = _pack({"pre_mix_norm": d_pre_mix}, SMALL_LATE, 8)
    gw["w_in"], ((slots_late,),) = _wgrad(h, dproj, full_tok, pl.BlockSpec((SEQ, IN_S), lambda s: (0, s)),
                                          (D_MODEL, IN_S), "wgrad_in", comms=[_small_exchange(packed_late)])
    ((sib["w_in"],),) = _comm_only("comm_rs_sibling_in", [_rs_sibling([gw["w_in"]])])
    part["w_in"] = _rs_chip_sum(gw["w_in"], sib["w_in"], core)
    ((far["w_in"],),) = _comm_only("comm_rs_chips_in", [_rs_chips([part["w_in"]])])
    half["w_in"] = _rs_final_sum(gw["w_in"], sib["w_in"], far["w_in"], shard_core)
    ((joined["w_in"],),) = _comm_only("comm_rs_join_in", [_rs_join([half["w_in"]])])

    grads, deltas, new_m, new_v = {}, {}, {}, {}
    for n in BIG:
        g = joined[n].reshape(a[n].shape[1:])
        grads[n] = g[None]
        d, nm, nv = _adamw(a[n][0], g, a["m_" + n][0], a["v_" + n][0], BIG_ADAM_ROWS[n], "adamw_" + n)
        deltas[n], new_m[n], new_v[n] = d[None], nm[None], nv[None]
    for names, rows, packed, slots in ((SMALL_EARLY, SMALL_ROWS, packed_early, slots_early),
                                       (SMALL_LATE, 8, packed_late, slots_late)):
        outs = _adamw_small(packed, slots, _pack(a, names, rows), _pack({n: a["m_" + n] for n in names}, names, rows),
                            _pack({n: a["v_" + n] for n in names}, names, rows), me)
        for dst, buf in zip((grads, deltas, new_m, new_v), outs):
            dst.update(_unpack(buf, names, {n: a[n].shape for n in names}))
    return (loss, grad_x[None], *[grads[n] for n in WEIGHTS], *[deltas[n] for n in WEIGHTS],
            *[new_m[n] for n in WEIGHTS], *[new_v[n] for n in WEIGHTS])
```

```python
import numpy as np
import jax
import jax.numpy as jnp
from jax import lax
from jax.experimental import pallas as pl
from jax.experimental.pallas import tpu as pltpu

F32 = jnp.float32
BF16 = jnp.bfloat16

SEQ = 2048
D_MODEL = 1024
HEAD_DIM = 64
ATTN_W = 512
SGU_W = 512
SGU_GROUPS = 8
CHUNK = 128
DILATIONS = (1, 4, 16)
N_SHARD = 4
IN_S = 640
OUT_S = 256
FF_S = 704
PROJ_W = N_SHARD * IN_S
RMS_EPS = 1e-6
LN_EPS = 1e-5
ROPE_THETA = 500000.0
ATTN_SCALE = 1.0 / np.sqrt(HEAD_DIM)
NEG = -1e30
TM = 256
VMEM_LIMIT = 56 * 1024 * 1024
SMALL_ROWS = 136

ADAM_LR = 0.001
ADAM_B1 = 0.9
ADAM_B2 = 0.999
ADAM_EPS = 1e-08
ADAM_WD = 0.01
ADAM_STEP = 10

MESH = pl.DeviceIdType.MESH
ANY = pl.BlockSpec(memory_space=pl.ANY)


def _dot(a, b):
    return jnp.dot(a, b, preferred_element_type=F32)


def _dot_nt(a, b):
    return lax.dot_general(a, b, (((1,), (1,)), ((), ())), preferred_element_type=F32)


def _dot_tn(a, b):
    return lax.dot_general(a, b, (((0,), (0,)), ((), ())), preferred_element_type=F32)


def _dot_exact(a, b):
    return jnp.dot(a, b, preferred_element_type=F32, precision=lax.Precision.HIGHEST)


def _rms_stats(x):
    r = lax.rsqrt(jnp.mean(x * x, axis=-1, keepdims=True) + RMS_EPS)
    return x * r, r


def _rms_bwd(xh, r, gain, dy):
    dxh = dy * gain
    dx = r * (dxh - xh * jnp.mean(dxh * xh, axis=-1, keepdims=True))
    return dx, jnp.sum(dy * xh, axis=0, keepdims=True)


_ERF_ALPHA = (-2.72614225801306e-10, 2.77068142495902e-08, -2.10102402082508e-06, -5.69250639462346e-05,
              -7.34990630326855e-04, -2.95459980854025e-03, -1.60960333262415e-02)
_ERF_BETA = (-1.45660718464996e-05, -2.13374055278905e-04, -1.68282697438203e-03, -7.37332916720468e-03,
             -1.42647390514189e-02)


def _erf(x):
    x = jnp.clip(x, -4.0, 4.0)
    x2 = x * x
    p = jnp.full_like(x, _ERF_ALPHA[0])
    for a in _ERF_ALPHA[1:]:
        p = p * x2 + a
    q = jnp.full_like(x, _ERF_BETA[0])
    for b in _ERF_BETA[1:]:
        q = q * x2 + b
    return x * p / q


def _gelu(x):
    return 0.5 * x * (1.0 + _erf(x * np.float32(1.0 / np.sqrt(2.0))))


def _gelu_grad(x):
    cdf = 0.5 * (1.0 + _erf(x * np.float32(1.0 / np.sqrt(2.0))))
    pdf = jnp.exp(-0.5 * x * x) * np.float32(1.0 / np.sqrt(2.0 * np.pi))
    return cdf + x * pdf


def _sigmoid(x):
    return 1.0 / (1.0 + jnp.exp(-x))


_INV_FREQ = tuple(float(np.float32(ROPE_THETA ** (-2.0 * j / 16.0))) for j in range(8))


def _rot_tables(pos):
    lane = lax.broadcasted_iota(jnp.int32, (1, 128), 1)
    d = lane & 63
    j = d & 7
    inv = jnp.zeros((1, 128), F32)
    for jj in range(8):
        inv = jnp.where(j == jj, _INV_FREQ[jj], inv)
    ang = pos.astype(F32) * inv
    c = jnp.cos(ang)
    s = jnp.sin(ang)
    cos_t = jnp.where(d < 16, c, 1.0)
    sin_a = jnp.where(d < 8, -s, 0.0)
    sin_b = jnp.where((d >= 8) & (d < 16), s, 0.0)
    return tuple(jnp.tile(t, (1, 4)) for t in (cos_t, sin_a, sin_b))


def _rope(x, tabs):
    cos_t, sin_a, sin_b = tabs
    return x * cos_t + pltpu.roll(x, 504, 1) * sin_a + pltpu.roll(x, 8, 1) * sin_b


def _rope_bwd(dy, tabs):
    cos_t, sin_a, sin_b = tabs
    return dy * cos_t + pltpu.roll(dy * sin_a, 8, 1) + pltpu.roll(dy * sin_b, 504, 1)


def _left_half():
    return lax.broadcasted_iota(jnp.int32, (CHUNK, CHUNK), 1) < HEAD_DIM


def _group_ones():
    lane = lax.broadcasted_iota(jnp.int32, (SGU_GROUPS, SGU_W), 1)
    row = lax.broadcasted_iota(jnp.int32, (SGU_GROUPS, SGU_W), 0)
    return ((lane >> 6) == row).astype(F32)


def _masked_spatial(w_ref):
    row = lax.broadcasted_iota(jnp.int32, (CHUNK, CHUNK), 0)
    col = lax.broadcasted_iota(jnp.int32, (CHUNK, CHUNK), 1)
    return [jnp.where(col <= row, w_ref[g], 0.0).astype(BF16) for g in range(SGU_GROUPS)]


def _sgu_core(u, vs, lg, lb, wm, bias_full):
    tm = u.shape[0]
    gu = _gelu(u)
    gv = _gelu(vs)
    mu = jnp.mean(gv, axis=-1, keepdims=True)
    xc = gv - mu
    rstd = lax.rsqrt(jnp.mean(xc * xc, axis=-1, keepdims=True) + LN_EPS)
    xh = xc * rstd
    vnb = (xh * lg + lb).astype(BF16)
    left = _left_half()
    rows = []
    for c in range(tm // CHUNK):
        pieces = []
        for p in range(4):
            vp = vnb[c * CHUNK:(c + 1) * CHUNK, p * 128:(p + 1) * 128]
            pieces.append(jnp.where(left, _dot(wm[2 * p], vp), _dot(wm[2 * p + 1], vp)))
        rows.append(jnp.concatenate(pieces, axis=1) + bias_full)
    mixed = jnp.concatenate(rows, axis=0)
    return gu, xh, rstd, vnb, mixed


def _resident(shape):
    n = len(shape)
    return pl.BlockSpec(shape, lambda *_: (0,) * n, pipeline_mode=pl.Buffered(1))


def _rows(ncol, tm=TM):
    return pl.BlockSpec((tm, ncol), lambda i: (i, 0))


def _rows3(nlead, ncol, tm=TM):
    return pl.BlockSpec((nlead, tm, ncol), lambda i: (0, i, 0))


def _acc(ncol, nrow=1):
    return pl.BlockSpec((nrow, ncol), lambda i: (0, 0))


def _seq_params():
    return pltpu.CompilerParams(dimension_semantics=("arbitrary",), vmem_limit_bytes=VMEM_LIMIT)


def _sds(shape, dtype):
    return jax.ShapeDtypeStruct(shape, dtype)


class _Comm:
    def __init__(self, args, out_shape, n_sems, start, finish, aliased=False):
        self.args, self.out_shape, self.n_sems = list(args), list(out_shape), n_sems
        self.start, self.finish, self.aliased = start, finish, aliased


def _pcall(body, *, name, grid, in_specs, out_specs, out_shape, args, scratch_shapes=(), comms=(), after=()):
    single = not isinstance(out_shape, (list, tuple))
    out_specs = [out_specs] if single else list(out_specs)
    out_shape = [out_shape] if single else list(out_shape)
    n_in, n_out, n_scr = len(in_specs), len(out_shape), len(scratch_shapes)
    c_args = [a for c in comms for a in c.args]
    c_outs = [o for c in comms for o in c.out_shape]
    aliases, ai, ao = {}, n_in, n_out
    for c in comms:
        if c.aliased:
            aliases.update({ai + k: ao + k for k in range(len(c.args))})
        ai += len(c.args)
        ao += len(c.out_shape)
    sems = [pltpu.SemaphoreType.DMA((c.n_sems,)) for c in comms for _ in range(2)]
    steps = grid[0]

    def wrapped(*refs):
        o0 = n_in + len(c_args) + len(after)
        s0 = o0 + n_out + len(c_outs)
        m_in, m_out, m_sem = refs[n_in:n_in + len(c_args)], refs[o0 + n_out:s0], refs[s0 + n_scr:]

        def each(phase):
            ii = oi = 0
            for k, c in enumerate(comms):
                getattr(c, phase)(m_in[ii:ii + len(c.args)], m_out[oi:oi + len(c.out_shape)],
                                  m_sem[2 * k], m_sem[2 * k + 1])
                ii += len(c.args)
                oi += len(c.out_shape)

        if comms:
            @pl.when(pl.program_id(0) == 0)
            def _():
                each("start")

        body(*refs[:n_in], *refs[o0:o0 + n_out], *refs[s0:s0 + n_scr])

        if comms:
            @pl.when(pl.program_id(0) == steps - 1)
            def _():
                each("finish")

    res = pl.pallas_call(
        wrapped, name=name, grid=grid,
        in_specs=list(in_specs) + [ANY] * (len(c_args) + len(after)), out_specs=out_specs + [ANY] * len(c_outs),
        out_shape=out_shape + c_outs, scratch_shapes=list(scratch_shapes) + sems,
        input_output_aliases=aliases, compiler_params=_seq_params(),
    )(*args, *c_args, *after)
    mine = res[0] if single else list(res[:n_out])
    if not comms:
        return mine
    theirs, oi = [], n_out
    for c in comms:
        theirs.append(list(res[oi:oi + len(c.out_shape)]))
        oi += len(c.out_shape)
    return mine, theirs


def _inproj_fwd(x, pos, g_pre, w_in, comms=()):
    def body(x_ref, pos_ref, g_ref, w_ref, h_ref, q_ref, k_ref, v_ref, u_ref, vs_ref):
        xh, _ = _rms_stats(x_ref[...])
        h = (xh * g_ref[...]).astype(BF16)
        h_ref[...] = h
        proj = jnp.concatenate([_dot(h, w_ref[s]) for s in range(N_SHARD)], axis=1)
        tabs = _rot_tables(pos_ref[...])
        q_ref[...] = (_rope(proj[:, 0:512], tabs) * np.float32(ATTN_SCALE)).astype(BF16)
        k_ref[...] = _rope(proj[:, 512:1024], tabs).astype(BF16)
        v_ref[...] = proj[:, 1024:1536].astype(BF16)
        u_ref[...] = proj[:, 1536:2048]
        vs_ref[...] = proj[:, 2048:2560]

    return _pcall(
        body, name="inproj_fwd", grid=(SEQ // TM,),
        in_specs=[_rows(D_MODEL), _rows(1), _resident((1, D_MODEL)), _resident((N_SHARD, D_MODEL, IN_S))],
        out_specs=[_rows(D_MODEL), _rows(512), _rows(512), _rows(512), _rows(512), _rows(512)],
        out_shape=[_sds((SEQ, D_MODEL), BF16), _sds((SEQ, 512), BF16), _sds((SEQ, 512), BF16),
                   _sds((SEQ, 512), BF16), _sds((SEQ, 512), F32), _sds((SEQ, 512), F32)],
        args=(x, pos, g_pre, w_in), comms=comms)


def _sgu_fwd(u, vs, lg, lb, w_sp, b_t, comms=()):
    def body(u_ref, vs_ref, lg_ref, lb_ref, w_ref, bt_ref, out_ref):
        wm = _masked_spatial(w_ref)
        bias_full = _dot_exact(bt_ref[...], _group_ones())
        gu, _, _, _, mixed = _sgu_core(u_ref[...], vs_ref[...], lg_ref[...], lb_ref[...], wm, bias_full)
        out_ref[...] = gu * mixed

    return _pcall(
        body, name="sgu_fwd", grid=(SEQ // TM,),
        in_specs=[_rows(SGU_W), _rows(SGU_W), _resident((1, SGU_W)), _resident((1, SGU_W)),
                  _resident((SGU_GROUPS, CHUNK, CHUNK)), _resident((CHUNK, SGU_GROUPS))],
        out_specs=_rows(SGU_W),
        out_shape=_sds((SEQ, SGU_W), F32),
        args=(u, vs, lg, lb, w_sp, b_t), comms=comms)


def _block_masks():
    row = lax.broadcasted_iota(jnp.int32, (CHUNK, CHUNK), 0)
    col = lax.broadcasted_iota(jnp.int32, (CHUNK, CHUNK), 1)
    return col <= row, col >= row


def _attn_fwd(qv, kv, vv, dil, comms=()):
    seg = SEQ // dil
    nblk = seg // CHUNK

    def body(q_ref, k_ref, v_ref, o_ref, l_ref):
        left = _left_half()
        m_cur, m_prev = _block_masks()
        zero = jnp.zeros((CHUNK, CHUNK), BF16)
        ones = (jnp.where(left, 1.0, 0.0).astype(BF16), jnp.where(left, 0.0, 1.0).astype(BF16))

        def blk(b, carry):
            r0 = pl.multiple_of(b * CHUNK, CHUNK)
            rp = pl.multiple_of(jnp.maximum(b - 1, 0) * CHUNK, CHUNK)
            prev_ok = m_prev & (b > 0)
            sides = tuple(enumerate((left, ~left)))
            tiles, scores = [], []
            for hp in range(4):
                ls = slice(hp * 128, (hp + 1) * 128)
                qp = q_ref[pl.ds(r0, CHUNK), ls]
                kc = k_ref[pl.ds(r0, CHUNK), ls]
                kp = k_ref[pl.ds(rp, CHUNK), ls] if nblk > 1 else None
                tiles.append((ls, v_ref[pl.ds(r0, CHUNK), ls], v_ref[pl.ds(rp, CHUNK), ls] if nblk > 1 else None))
                for _, hm in sides:
                    qh = jnp.where(hm, qp, zero)
                    sc = jnp.where(m_cur, _dot_nt(qh, kc), NEG)
                    sp = jnp.where(prev_ok, _dot_nt(qh, kp), NEG) if nblk > 1 else None
                    scores.append((sc, sp))
            probs = []
            for sc, sp in scores:
                if nblk > 1:
                    m = jnp.max(jnp.maximum(sc, sp), axis=-1, keepdims=True)
                    pc = jnp.exp(sc - m)
                    pp = jnp.exp(sp - m)
                    probs.append((m, pc.astype(BF16), pp.astype(BF16), (pc + pp).astype(BF16)))
                else:
                    m = jnp.max(sc, axis=-1, keepdims=True)
                    pc = jnp.exp(sc - m).astype(BF16)
                    probs.append((m, pc, None, pc))
            for hp, (ls, vc, vp) in enumerate(tiles):
                acc = jnp.zeros((CHUNK, CHUNK), F32)
                den = jnp.zeros((CHUNK, CHUNK), F32)
                for side, hm in sides:
                    _, pc, pp, psum = probs[2 * hp + side]
                    acc = acc + _dot(pc, jnp.where(hm, vc, zero))
                    if nblk > 1:
                        acc = acc + _dot(pp, jnp.where(hm, vp, zero))
                    den = den + _dot(psum, ones[side])
                o_ref[pl.ds(r0, CHUNK), ls] = acc / den
                l_ref[pl.ds(r0, CHUNK), ls] = jnp.where(left, probs[2 * hp][0], probs[2 * hp + 1][0]) + jnp.log(den)
            return carry

        lax.fori_loop(0, nblk, blk, 0)

    spec = pl.BlockSpec((seg, ATTN_W), lambda r: (0, r))
    return _pcall(
        body, name=f"attn_fwd_d{dil}", grid=(dil,),
        in_specs=[spec, spec, spec], out_specs=[spec, spec],
        out_shape=[_sds((seg, dil * ATTN_W), F32), _sds((seg, dil * ATTN_W), F32)],
        args=(qv, kv, vv), comms=comms)


def _mix_out_fwd(o_list, l_list, sgu, x, w_out, g_attn, g_sgu, g_post, comms=()):
    def body(o1, o2, o3, l1, l2, l3, sgu_ref, x_ref, w_ref, ga_ref, gs_ref, gp_ref,
             attn_ref, lse_ref, mixed_ref, y_ref, x1_ref):
        ls = [l1[...], l2[...], l3[...]]
        m = jnp.maximum(jnp.maximum(ls[0], ls[1]), ls[2])
        es = [jnp.exp(l - m) for l in ls]
        den = es[0] + es[1] + es[2]
        attn = (es[0] * o1[...] + es[1] * o2[...] + es[2] * o3[...]) / den
        attn_ref[...] = attn
        lse_ref[...] = m + jnp.log(den)
        ah, _ = _rms_stats(attn)
        sh, _ = _rms_stats(sgu_ref[...])
        mixed = jnp.concatenate([ah * ga_ref[...], sh * gs_ref[...]], axis=1).astype(BF16)
        mixed_ref[...] = mixed
        y = _dot(mixed[:, 0:OUT_S], w_ref[0])
        for s in range(1, N_SHARD):
            y = y + _dot(mixed[:, s * OUT_S:(s + 1) * OUT_S], w_ref[s])
        y_ref[...] = y
        yh, _ = _rms_stats(y)
        x1_ref[...] = x_ref[...] + yh * gp_ref[...]

    return _pcall(
        body, name="mix_out_fwd", grid=(SEQ // TM,),
        in_specs=[_rows(512)] * 7 + [_rows(D_MODEL), _resident((N_SHARD, OUT_S, D_MODEL)),
                                    _resident((1, 512)), _resident((1, 512)), _resident((1, D_MODEL))],
        out_specs=[_rows(512), _rows(512), _rows(D_MODEL), _rows(D_MODEL), _rows(D_MODEL)],
        out_shape=[_sds((SEQ, 512), F32), _sds((SEQ, 512), F32), _sds((SEQ, D_MODEL), BF16),
                   _sds((SEQ, D_MODEL), F32), _sds((SEQ, D_MODEL), F32)],
        args=(*o_list, *l_list, sgu, x, w_out, g_attn, g_sgu, g_post), comms=comms)


def _ffn_fwd_bwd(x1, target, w_gate, w_up, w_down, g_pre, g_post, comms=()):
    def body(x1_ref, t_ref, wg_ref, wu_ref, wd_ref, gpf_ref, gpo_ref,
             h2_ref, a_ref, dg_ref, dup_ref, df_ref, dx1_ref, loss_ref, dgpf_ref, dgpo_ref, g_scr, up_scr):
        @pl.when(pl.program_id(0) == 0)
        def _():
            loss_ref[...] = jnp.zeros_like(loss_ref)
            dgpf_ref[...] = jnp.zeros_like(dgpf_ref)
            dgpo_ref[...] = jnp.zeros_like(dgpo_ref)

        x1 = x1_ref[...]
        gpf = gpf_ref[...]
        gpo = gpo_ref[...]
        xh, r = _rms_stats(x1)
        h2 = (xh * gpf).astype(BF16)
        h2_ref[...] = h2
        f = jnp.zeros((TM, D_MODEL), F32)
        for s in range(N_SHARD):
            g = _dot(h2, wg_ref[s])
            up = _dot(h2, wu_ref[s])
            g_scr[s] = g
            up_scr[s] = up
            a = (g * _sigmoid(g) * up).astype(BF16)
            a_ref[s] = a
            f = f + _dot(a, wd_ref[s])
        fh, rf = _rms_stats(f)
        diff = x1 + fh * gpo - t_ref[...]
        loss_ref[...] += jnp.sum(diff * diff, axis=0, keepdims=True)
        dout = diff * np.float32(1.0 / D_MODEL)
        df, dgpo = _rms_bwd(fh, rf, gpo, dout)
        dgpo_ref[...] += dgpo
        dfb = df.astype(BF16)
        df_ref[...] = dfb
        dh2 = jnp.zeros((TM, D_MODEL), F32)
        for s in range(N_SHARD):
            da = _dot_nt(dfb, wd_ref[s])
            g = g_scr[s]
            up = up_scr[s]
            sg = _sigmoid(g)
            dup = (da * (g * sg)).astype(BF16)
            dg = (da * up * (sg * (1.0 + g * (1.0 - sg)))).astype(BF16)
            dg_ref[s] = dg
            dup_ref[s] = dup
            dh2 = dh2 + _dot_nt(dg, wg_ref[s]) + _dot_nt(dup, wu_ref[s])
        dx, dgpf = _rms_bwd(xh, r, gpf, dh2)
        dgpf_ref[...] += dgpf
        dx1_ref[...] = dout + dx

    return _pcall(
        body, name="ffn_fwd_bwd", grid=(SEQ // TM,),
        in_specs=[_rows(D_MODEL), _rows(D_MODEL), _resident((N_SHARD, D_MODEL, FF_S)),
                  _resident((N_SHARD, D_MODEL, FF_S)), _resident((N_SHARD, FF_S, D_MODEL)),
                  _resident((1, D_MODEL)), _resident((1, D_MODEL))],
        out_specs=[_rows(D_MODEL), _rows3(N_SHARD, FF_S), _rows3(N_SHARD, FF_S), _rows3(N_SHARD, FF_S),
                   _rows(D_MODEL), _rows(D_MODEL), _acc(D_MODEL), _acc(D_MODEL), _acc(D_MODEL)],
        out_shape=[_sds((SEQ, D_MODEL), BF16), _sds((N_SHARD, SEQ, FF_S), BF16), _sds((N_SHARD, SEQ, FF_S), BF16),
                   _sds((N_SHARD, SEQ, FF_S), BF16), _sds((SEQ, D_MODEL), BF16), _sds((SEQ, D_MODEL), F32),
                   _sds((1, D_MODEL), F32), _sds((1, D_MODEL), F32), _sds((1, D_MODEL), F32)],
        scratch_shapes=[pltpu.VMEM((N_SHARD, TM, FF_S), F32), pltpu.VMEM((N_SHARD, TM, FF_S), F32)],
        args=(x1, target, w_gate, w_up, w_down, g_pre, g_post), comms=comms)


def _wgrad(a, b, a_spec, b_spec, out_block, name, comms=()):
    def body(a_ref, b_ref, o_ref):
        av = a_ref[0] if len(a_ref.shape) == 3 else a_ref[...]
        bv = b_ref[0] if len(b_ref.shape) == 3 else b_ref[...]
        o_ref[0] = _dot_tn(av, bv)

    return _pcall(
        body, name=name, grid=(N_SHARD,),
        in_specs=[a_spec, b_spec],
        out_specs=pl.BlockSpec((1,) + out_block, lambda s: (s, 0, 0)),
        out_shape=_sds((N_SHARD,) + out_block, F32),
        args=(a, b), comms=comms)


def _outproj_bwd(dx1, y, attn, sgu, w_out, g_post, g_attn, g_sgu, comms=()):
    def body(dx1_ref, y_ref, attn_ref, sgu_ref, w_ref, gp_ref, ga_ref, gs_ref,
             dy_ref, dattn_ref, delta_ref, dsgu_ref, dgp_ref, dga_ref, dgs_ref):
        @pl.when(pl.program_id(0) == 0)
        def _():
            dgp_ref[...] = jnp.zeros_like(dgp_ref)
            dga_ref[...] = jnp.zeros_like(dga_ref)
            dgs_ref[...] = jnp.zeros_like(dgs_ref)

        yh, ry = _rms_stats(y_ref[...])
        dy, dgp = _rms_bwd(yh, ry, gp_ref[...], dx1_ref[...])
        dgp_ref[...] += dgp
        dyb = dy.astype(BF16)
        dy_ref[...] = dyb
        dmixed = jnp.concatenate([_dot_nt(dyb, w_ref[s]) for s in range(N_SHARD)], axis=1)
        attn = attn_ref[...]
        ah, ra = _rms_stats(attn)
        dattn, dga = _rms_bwd(ah, ra, ga_ref[...], dmixed[:, 0:512])
        dga_ref[...] += dga
        sh, rs = _rms_stats(sgu_ref[...])
        dsgu, dgs = _rms_bwd(sh, rs, gs_ref[...], dmixed[:, 512:1024])
        dgs_ref[...] += dgs
        dattn_ref[...] = dattn.astype(BF16)
        dsgu_ref[...] = dsgu
        la = lax.broadcasted_iota(jnp.int32, (ATTN_W, ATTN_W), 0) >> 6
        lb = lax.broadcasted_iota(jnp.int32, (ATTN_W, ATTN_W), 1) >> 6
        delta_ref[...] = _dot_exact(dattn * attn, (la == lb).astype(F32))

    return _pcall(
        body, name="outproj_bwd", grid=(SEQ // TM,),
        in_specs=[_rows(D_MODEL), _rows(D_MODEL), _rows(512), _rows(512), _resident((N_SHARD, OUT_S, D_MODEL)),
                  _resident((1, D_MODEL)), _resident((1, 512)), _resident((1, 512))],
        out_specs=[_rows(D_MODEL), _rows(512), _rows(512), _rows(512), _acc(D_MODEL), _acc(512), _acc(512)],
        out_shape=[_sds((SEQ, D_MODEL), BF16), _sds((SEQ, 512), BF16), _sds((SEQ, 512), F32), _sds((SEQ, 512), F32),
                   _sds((1, D_MODEL), F32), _sds((1, 512), F32), _sds((1, 512), F32)],
        args=(dx1, y, attn, sgu, w_out, g_post, g_attn, g_sgu), comms=comms)


def _sgu_bwd(u, vs, dsgu, lg, lb, w_sp, b_t, comms=(), after=()):
    nsteps = SEQ // TM

    def body(u_ref, vs_ref, ds_ref, lg_ref, lb_ref, w_ref, bt_ref,
             du_ref, dvs_ref, dw_ref, db_ref, dlg_ref, dlb_ref, dbias_scr):
        i = pl.program_id(0)

        @pl.when(i == 0)
        def _():
            dw_ref[...] = jnp.zeros_like(dw_ref)
            dlg_ref[...] = jnp.zeros_like(dlg_ref)
            dlb_ref[...] = jnp.zeros_like(dlb_ref)
            dbias_scr[...] = jnp.zeros_like(dbias_scr)

        wm = _masked_spatial(w_ref)
        ones_g = _group_ones()
        bias_full = _dot_exact(bt_ref[...], ones_g)
        u = u_ref[...]
        vs = vs_ref[...]
        lg = lg_ref[...]
        gu, xh, rstd, vnb, mixed = _sgu_core(u, vs, lg, lb_ref[...], wm, bias_full)
        dsgu = ds_ref[...]
        du_ref[...] = (dsgu * mixed * _gelu_grad(u)).astype(BF16)
        dmixed = dsgu * gu
        left = _left_half()
        dvn_rows = []
        for c in range(TM // CHUNK):
            rs = slice(c * CHUNK, (c + 1) * CHUNK)
            dm_c = dmixed[rs, :]
            dbias_scr[...] += dm_c
            pieces = []
            for p in range(4):
                ls = slice(p * 128, (p + 1) * 128)
                dmp = dm_c[:, ls]
                vp = vnb[rs, ls]
                dmb = dmp.astype(BF16)
                zero = jnp.zeros_like(dmb)
                dw_ref[2 * p] += _dot_nt(jnp.where(left, dmb, zero), vp)
                dw_ref[2 * p + 1] += _dot_nt(jnp.where(left, zero, dmb), vp)
                pieces.append(jnp.where(left, _dot_tn(wm[2 * p], dmb), _dot_tn(wm[2 * p + 1], dmb)))
            dvn_rows.append(jnp.concatenate(pieces, axis=1))
        dvn = jnp.concatenate(dvn_rows, axis=0)
        dlg_ref[...] += jnp.sum(dvn * xh, axis=0, keepdims=True)
        dlb_ref[...] += jnp.sum(dvn, axis=0, keepdims=True)
        dxh = dvn * lg
        dgv = rstd * (dxh - jnp.mean(dxh, axis=-1, keepdims=True) - xh * jnp.mean(dxh * xh, axis=-1, keepdims=True))
        dvs_ref[...] = (dgv * _gelu_grad(vs)).astype(BF16)

        @pl.when(i == nsteps - 1)
        def _():
            row = lax.broadcasted_iota(jnp.int32, (CHUNK, CHUNK), 0)
            col = lax.broadcasted_iota(jnp.int32, (CHUNK, CHUNK), 1)
            for g in range(SGU_GROUPS):
                dw_ref[g] = jnp.where(col <= row, dw_ref[g], 0.0)
            db_ref[...] = lax.dot_general(ones_g, dbias_scr[...], (((1,), (1,)), ((), ())),
                                          preferred_element_type=F32, precision=lax.Precision.HIGHEST)

    return _pcall(
        body, name="sgu_bwd", grid=(nsteps,),
        in_specs=[_rows(SGU_W), _rows(SGU_W), _rows(SGU_W), _resident((1, SGU_W)), _resident((1, SGU_W)),
                  _resident((SGU_GROUPS, CHUNK, CHUNK)), _resident((CHUNK, SGU_GROUPS))],
        out_specs=[_rows(SGU_W), _rows(SGU_W), pl.BlockSpec((SGU_GROUPS, CHUNK, CHUNK), lambda i: (0, 0, 0)),
                   _acc(CHUNK, SGU_GROUPS), _acc(SGU_W), _acc(SGU_W)],
        out_shape=[_sds((SEQ, SGU_W), BF16), _sds((SEQ, SGU_W), BF16), _sds((SGU_GROUPS, CHUNK, CHUNK), F32),
                   _sds((SGU_GROUPS, CHUNK), F32), _sds((1, SGU_W), F32), _sds((1, SGU_W), F32)],
        scratch_shapes=[pltpu.VMEM((CHUNK, SGU_W), F32)],
        args=(u, vs, dsgu, lg, lb, w_sp, b_t), comms=comms, after=after)


def _attn_bwd(qv, kv, vv, dov, deltav, lsev, dil, comms=(), after=()):
    seg = SEQ // dil
    nblk = seg // CHUNK

    def body(q_ref, k_ref, v_ref, do_ref, dl_ref, lse_ref, dq_ref, dk_ref, dv_ref):
        left = _left_half()
        m_cur, m_prev = _block_masks()

        def blk(b, carry):
            r0 = pl.multiple_of(b * CHUNK, CHUNK)
            rp = pl.multiple_of(jnp.maximum(b - 1, 0) * CHUNK, CHUNK)
            prev_ok = m_prev & (b > 0)
            sides = tuple(enumerate((left, ~left)))
            zero = jnp.zeros((CHUNK, CHUNK), BF16)
            tiles, firsts = [], []
            for hp in range(4):
                ls = slice(hp * 128, (hp + 1) * 128)
                qp = q_ref[pl.ds(r0, CHUNK), ls]
                kc = k_ref[pl.ds(r0, CHUNK), ls]
                vc = v_ref[pl.ds(r0, CHUNK), ls]
                dop = do_ref[pl.ds(r0, CHUNK), ls]
                kp = k_ref[pl.ds(rp, CHUNK), ls] if nblk > 1 else None
                vp = v_ref[pl.ds(rp, CHUNK), ls] if nblk > 1 else None
                tiles.append((ls, kc, kp))
                for side, hm in sides:
                    qh = jnp.where(hm, qp, zero)
                    doh = jnp.where(hm, dop, zero)
                    cur = (_dot_nt(qh, kc), _dot_nt(doh, vc))
                    prev = (_dot_nt(qh, kp), _dot_nt(doh, vp)) if nblk > 1 else None
                    firsts.append((qh, doh, cur, prev))
            seconds = []
            for i, (qh, doh, cur, prev) in enumerate(firsts):
                hp, c0 = i // 2, (i % 2) * HEAD_DIM
                ls = slice(hp * 128, (hp + 1) * 128)
                lse_h = lse_ref[pl.ds(r0, CHUNK), ls][:, c0:c0 + 1]
                dl_h = dl_ref[pl.ds(r0, CHUNK), ls][:, c0:c0 + 1]
                pc = jnp.exp(jnp.where(m_cur, cur[0] - lse_h, NEG))
                out = [pc.astype(BF16), (pc * (cur[1] - dl_h)).astype(BF16), None, None]
                if nblk > 1:
                    pp = jnp.exp(jnp.where(prev_ok, prev[0] - lse_h, NEG))
                    out[2:] = [pp.astype(BF16), (pp * (prev[1] - dl_h)).astype(BF16)]
                seconds.append(out)
            for hp, (ls, kc, kp) in enumerate(tiles):
                dq = jnp.zeros((CHUNK, CHUNK), F32)
                dkc = jnp.zeros((CHUNK, CHUNK), F32)
                dvc = jnp.zeros((CHUNK, CHUNK), F32)
                dkp = jnp.zeros((CHUNK, CHUNK), F32)
                dvp = jnp.zeros((CHUNK, CHUNK), F32)
                for side, hm in sides:
                    qh, doh, _, _ = firsts[2 * hp + side]
                    pcb, dsc, ppb, dsp = seconds[2 * hp + side]
                    dq = dq + _dot(dsc, jnp.where(hm, kc, zero))
                    dkc = dkc + _dot_tn(dsc, qh)
                    dvc = dvc + _dot_tn(pcb, doh)
                    if nblk > 1:
                        dq = dq + _dot(dsp, jnp.where(hm, kp, zero))
                        dkp = dkp + _dot_tn(dsp, qh)
                        dvp = dvp + _dot_tn(ppb, doh)
                dq_ref[pl.ds(r0, CHUNK), ls] = dq
                dk_ref[pl.ds(r0, CHUNK), ls] = dkc
                dv_ref[pl.ds(r0, CHUNK), ls] = dvc
                if nblk > 1:
                    @pl.when(b > 0)
                    def _():
                        dk_ref[pl.ds(rp, CHUNK), ls] += dkp
                        dv_ref[pl.ds(rp, CHUNK), ls] += dvp
            return carry

        lax.fori_loop(0, nblk, blk, 0)

    spec = pl.BlockSpec((seg, ATTN_W), lambda r: (0, r))
    return _pcall(
        body, name=f"attn_bwd_d{dil}", grid=(dil,),
        in_specs=[spec] * 6, out_specs=[spec] * 3,
        out_shape=[_sds((seg, dil * ATTN_W), F32)] * 3,
        args=(qv, kv, vv, dov, deltav, lsev), comms=comms, after=after)


def _inproj_bwd(dqs, dks, dvs, du, dvs_sgu, pos, x, dx1, w_in, g_pre, comms=()):
    def body(dq1, dq2, dq3, dk1, dk2, dk3, dv1, dv2, dv3, du_ref, dvs_ref, pos_ref, x_ref, dx1_ref, w_ref, g_ref,
             dproj_ref, gx_ref, dg_ref):
        @pl.when(pl.program_id(0) == 0)
        def _():
            dg_ref[...] = jnp.zeros_like(dg_ref)

        tabs = _rot_tables(pos_ref[...])
        dproj_ref[:, 0:512] = _rope_bwd((dq1[...] + dq2[...] + dq3[...]) * np.float32(ATTN_SCALE), tabs).astype(BF16)
        dproj_ref[:, 512:1024] = _rope_bwd(dk1[...] + dk2[...] + dk3[...], tabs).astype(BF16)
        dproj_ref[:, 1024:1536] = (dv1[...] + dv2[...] + dv3[...]).astype(BF16)
        dproj_ref[:, 1536:2048] = du_ref[...]
        dproj_ref[:, 2048:2560] = dvs_ref[...]
        dh = jnp.zeros((TM, D_MODEL), F32)
        for s in range(N_SHARD):
            dh = dh + _dot_nt(dproj_ref[:, s * IN_S:(s + 1) * IN_S], w_ref[s])
        g = g_ref[...]
        xh, r = _rms_stats(x_ref[...])
        dx, dg = _rms_bwd(xh, r, g, dh)
        dg_ref[...] += dg
        gx_ref[...] = dx1_ref[...] + dx

    return _pcall(
        body, name="inproj_bwd", grid=(SEQ // TM,),
        in_specs=[_rows(512)] * 11 + [_rows(1), _rows(D_MODEL), _rows(D_MODEL),
                                     _resident((N_SHARD, D_MODEL, IN_S)), _resident((1, D_MODEL))],
        out_specs=[_rows(PROJ_W), _rows(D_MODEL), _acc(D_MODEL)],
        out_shape=[_sds((SEQ, PROJ_W), BF16), _sds((SEQ, D_MODEL), F32), _sds((1, D_MODEL), F32)],
        args=(*dqs, *dks, *dvs, du, dvs_sgu, pos, x, dx1, w_in, g_pre), comms=comms)


def _to_view(a, dil):
    return a if dil == 1 else a.reshape(SEQ // dil, dil * a.shape[1])


def _from_view(a, dil):
    return a if dil == 1 else a.reshape(SEQ, a.shape[1] // dil)


def _local_step(x, pos, target, w_in, w_out, w_gate, w_up, w_down, small):
    b_t = small["sgu_b_spatial"].T
    h, q, k, v, u, vs = _inproj_fwd(x, pos, small["pre_mix_norm"], w_in)
    sgu = _sgu_fwd(u, vs, small["sgu_ln_gain"], small["sgu_ln_bias"], small["sgu_w_spatial"], b_t)
    views = [tuple(_to_view(t, dil) for t in (q, k, v)) for dil in DILATIONS]
    o_list, l_list = [], []
    for dil, (qv, kv, vv) in zip(DILATIONS, views):
        o, l = _attn_fwd(qv, kv, vv, dil)
        o_list.append(_from_view(o, dil))
        l_list.append(_from_view(l, dil))
    attn, lse, mixed, y, x1 = _mix_out_fwd(o_list, l_list, sgu, x, w_out, small["attn_out_norm"],
                                           small["sgu_out_norm"], small["post_mix_norm"])
    h2, a, dg, dup, df, dx1, loss_cols, d_pre_ffn, d_post_ffn = _ffn_fwd_bwd(
        x1, target, w_gate, w_up, w_down, small["pre_ffn_norm"], small["post_ffn_norm"])

    full_tok = pl.BlockSpec((SEQ, D_MODEL), lambda s: (0, 0), pipeline_mode=pl.Buffered(1))
    ff_tok = pl.BlockSpec((1, SEQ, FF_S), lambda s: (s, 0, 0))
    gw_gate = _wgrad(h2, dg, full_tok, ff_tok, (D_MODEL, FF_S), "wgrad_gate")
    gw_up = _wgrad(h2, dup, full_tok, ff_tok, (D_MODEL, FF_S), "wgrad_up")
    gw_down = _wgrad(a, df, ff_tok, full_tok, (FF_S, D_MODEL), "wgrad_down")

    dy, dattn, delta, dsgu, d_post_mix, d_attn_norm, d_sgu_norm = _outproj_bwd(
        dx1, y, attn, sgu, w_out, small["post_mix_norm"], small["attn_out_norm"], small["sgu_out_norm"])
    gw_out = _wgrad(mixed, dy, pl.BlockSpec((SEQ, OUT_S), lambda s: (0, s)), full_tok, (OUT_S, D_MODEL), "wgrad_out")
    du, dvs_sgu, d_w_sp, d_b_sp, d_ln_gain, d_ln_bias = _sgu_bwd(
        u, vs, dsgu, small["sgu_ln_gain"], small["sgu_ln_bias"], small["sgu_w_spatial"], b_t)

    dqs, dks, dvs = [], [], []
    for dil, (qv, kv, vv) in zip(DILATIONS, views):
        dq, dk, dv = _attn_bwd(qv, kv, vv, _to_view(dattn, dil), _to_view(delta, dil), _to_view(lse, dil), dil)
        dqs.append(_from_view(dq, dil))
        dks.append(_from_view(dk, dil))
        dvs.append(_from_view(dv, dil))
    dproj, grad_x, d_pre_mix = _inproj_bwd(dqs, dks, dvs, du, dvs_sgu, pos, x, dx1, w_in, small["pre_mix_norm"])
    gw_in = _wgrad(h, dproj, full_tok, pl.BlockSpec((SEQ, IN_S), lambda s: (0, s)), (D_MODEL, IN_S), "wgrad_in")

    small_grads = {
        "pre_mix_norm": d_pre_mix, "sgu_ln_gain": d_ln_gain, "sgu_ln_bias": d_ln_bias, "sgu_w_spatial": d_w_sp,
        "sgu_b_spatial": d_b_sp, "attn_out_norm": d_attn_norm, "sgu_out_norm": d_sgu_norm,
        "post_mix_norm": d_post_mix, "pre_ffn_norm": d_pre_ffn, "post_ffn_norm": d_post_ffn,
    }
    return loss_cols, grad_x, (gw_in, gw_out, gw_gate, gw_up, gw_down), small_grads


def _coords():
    return lax.axis_index("x"), lax.axis_index("y"), lax.axis_index("c")


def _other_chips(x, y):
    return [(1 - x, y), (x, 1 - y), (1 - x, 1 - y)]


def _comm_call(body, name, n_in, out_shape, scratch_shapes):
    return pl.pallas_call(
        body, name=name, in_specs=[ANY] * n_in, out_specs=[ANY] * len(out_shape), out_shape=out_shape,
        scratch_shapes=scratch_shapes,
        compiler_params=pltpu.CompilerParams(has_side_effects=True),
    )


def _gather_weights(shards):
    n = len(shards)
    halves = [s.reshape(2, s.shape[0] // 2, s.shape[1]) for s in shards]

    def body(*refs):
        ins, outs = refs[:n], refs[n:2 * n]
        send_sems, recv_sems = refs[2 * n:]
        x, y, c = _coords()
        s_me = 2 * x + y
        chips = _other_chips(x, y)
        sibling = (x, y, 1 - c)

        def copy(k, w, shard, cc, to):
            src = ins[w].at[cc] if shard is None else outs[w].at[shard, cc]
            dst = outs[w].at[s_me if shard is None else shard, cc]
            return pltpu.make_async_remote_copy(src_ref=src, dst_ref=dst, send_sem=send_sems.at[k],
                                                recv_sem=recv_sems.at[k], device_id=to, device_id_type=MESH)

        first = [copy(j * n + w, w, None, c, (cx, cy, c)) for j, (cx, cy) in enumerate(chips) for w in range(n)]
        for cp in first:
            cp.start()
        passed = []
        for j, (cx, cy) in enumerate(chips):
            for w in range(n):
                copy(j * n + w, w, 2 * cx + cy, c, (x, y, c)).wait_recv()
                fw = copy((3 + j) * n + w, w, 2 * cx + cy, c, sibling)
                fw.start()
                passed.append(fw)
        for j, (cx, cy) in enumerate(chips):
            for w in range(n):
                copy((3 + j) * n + w, w, 2 * cx + cy, 1 - c, (x, y, c)).wait_recv()
        for cp in first + passed:
            cp.wait_send()

    out_shape = [_sds((N_SHARD,) + h.shape, h.dtype) for h in halves]
    scratch = [pltpu.SemaphoreType.DMA((6 * n,)), pltpu.SemaphoreType.DMA((6 * n,))]
    full = _comm_call(body, "comm_gather_weights", n, out_shape, scratch)(*halves)
    s_me = 2 * lax.axis_index("x") + lax.axis_index("y")
    full = [lax.dynamic_update_slice(f, h[None], (s_me, 0, 0, 0)) for f, h in zip(full, halves)]
    return [f.reshape((N_SHARD,) + s.shape) for f, s in zip(full, shards)]


def _rs_to_sibling(gws):
    n = len(gws)

    def body(*refs):
        ins, outs = refs[:n], refs[n:2 * n]
        send_sems, recv_sems = refs[2 * n:]
        x, y, c = _coords()
        copies = []
        for w in range(n):
            hw = gws[w].shape[1] // 2
            copies.append(pltpu.make_async_remote_copy(
                src_ref=ins[w].at[:, pl.ds((1 - c) * hw, hw), :], dst_ref=outs[w], send_sem=send_sems.at[w],
                recv_sem=recv_sems.at[w], device_id=(x, y, 1 - c), device_id_type=MESH))
        for cp in copies:
            cp.start()
        for cp in copies:
            cp.wait()

    out_shape = [_sds((N_SHARD, g.shape[1] // 2, g.shape[2]), g.dtype) for g in gws]
    scratch = [pltpu.SemaphoreType.DMA((n,)), pltpu.SemaphoreType.DMA((n,))]
    return _comm_call(body, "comm_rs_sibling", n, out_shape, scratch)(*gws)


def _rs_chip_sum(gw, recv, core):
    _, rows, cols = gw.shape
    hw = rows // 2

    def body(c_ref, g_ref, r_ref, o_ref):
        o_ref[...] = (g_ref[...] + r_ref[...]).astype(BF16)

    return pl.pallas_call(
        body, name="rs_chip_sum",
        grid_spec=pltpu.PrefetchScalarGridSpec(
            num_scalar_prefetch=1, grid=(N_SHARD,),
            in_specs=[pl.BlockSpec((1, hw, cols), lambda s, c_ref: (s, c_ref[0], 0)),
                      pl.BlockSpec((1, hw, cols), lambda s, c_ref: (s, 0, 0))],
            out_specs=pl.BlockSpec((1, hw, cols), lambda s, c_ref: (s, 0, 0))),
        out_shape=_sds((N_SHARD, hw, cols), BF16),
        compiler_params=_seq_params(),
    )(core, gw, recv)


def _rs_between_chips(pbs):
    n = len(pbs)

    def body(*refs):
        ins, outs = refs[:n], refs[n:2 * n]
        send_sems, recv_sems = refs[2 * n:]
        x, y, c = _coords()
        copies = []
        for j, (cx, cy) in enumerate(_other_chips(x, y)):
            for w in range(n):
                copies.append(pltpu.make_async_remote_copy(
                    src_ref=ins[w].at[2 * cx + cy], dst_ref=outs[w].at[j], send_sem=send_sems.at[j * n + w],
                    recv_sem=recv_sems.at[j * n + w], device_id=(cx, cy, c), device_id_type=MESH))
        for cp in copies:
            cp.start()
        for cp in copies:
            cp.wait()

    out_shape = [_sds((3,) + p.shape[1:], p.dtype) for p in pbs]
    scratch = [pltpu.SemaphoreType.DMA((3 * n,)), pltpu.SemaphoreType.DMA((3 * n,))]
    return _comm_call(body, "comm_rs_chips", n, out_shape, scratch)(*pbs)


def _rs_final_sum(gw, recv_sib, recv_chips, shard_core):
    _, rows, cols = gw.shape
    hw = rows // 2

    def body(sc_ref, g_ref, r_ref, rc_ref, o_ref):
        acc = g_ref[0] + r_ref[0]
        for j in range(3):
            acc = acc + rc_ref[j].astype(F32)
        o_ref[0] = acc

    return pl.pallas_call(
        body, name="rs_final_sum",
        grid_spec=pltpu.PrefetchScalarGridSpec(
            num_scalar_prefetch=1, grid=(1,),
            in_specs=[pl.BlockSpec((1, hw, cols), lambda i, sc: (sc[0], sc[1], 0)),
                      pl.BlockSpec((1, hw, cols), lambda i, sc: (sc[0], 0, 0)),
                      pl.BlockSpec((3, hw, cols), lambda i, sc: (0, 0, 0))],
            out_specs=pl.BlockSpec((1, hw, cols), lambda i, sc: (sc[1], 0, 0))),
        out_shape=_sds((2, hw, cols), F32),
        compiler_params=_seq_params(),
    )(shard_core, gw, recv_sib, recv_chips)


def _rs_join_halves(halves):
    n = len(halves)

    def body(*refs):
        bufs = refs[n:2 * n]
        send_sems, recv_sems = refs[2 * n:]
        x, y, c = _coords()
        remote = [pltpu.make_async_remote_copy(
            src_ref=bufs[w].at[c], dst_ref=bufs[w].at[c], send_sem=send_sems.at[w], recv_sem=recv_sems.at[w],
            device_id=(x, y, 1 - c), device_id_type=MESH) for w in range(n)]
        for cp in remote:
            cp.start()
        for w in range(n):
            remote[w].wait_send()
            pltpu.make_async_remote_copy(
                src_ref=bufs[w].at[c], dst_ref=bufs[w].at[1 - c], send_sem=send_sems.at[w],
                recv_sem=recv_sems.at[w], device_id=(x, y, c), device_id_type=MESH).wait_recv()

    joined = pl.pallas_call(
        body, name="comm_rs_join", in_specs=[ANY] * n, out_specs=[ANY] * n,
        out_shape=[_sds(h.shape, h.dtype) for h in halves], input_output_aliases={w: w for w in range(n)},
        scratch_shapes=[pltpu.SemaphoreType.DMA((n,)), pltpu.SemaphoreType.DMA((n,))],
        compiler_params=pltpu.CompilerParams(has_side_effects=True),
    )(*halves)
    return [j.reshape(2 * h.shape[1], h.shape[2]) for j, h in zip(joined, halves)]


def _allreduce_small(buf):
    rows, cols = buf.shape

    def body(in_ref, out_ref, slots, send_sems, recv_sems):
        x, y, c = _coords()
        me = 4 * x + 2 * y + c
        copies, peers = [], []
        for k in range(1, 8):
            px = 1 - x if (k >> 2) & 1 else x
            py = 1 - y if (k >> 1) & 1 else y
            pc = 1 - c if k & 1 else c
            peers.append(4 * px + 2 * py + pc)
            copies.append(pltpu.make_async_remote_copy(
                src_ref=in_ref, dst_ref=slots.at[me], send_sem=send_sems.at[k - 1], recv_sem=recv_sems.at[k - 1],
                device_id=(px, py, pc), device_id_type=MESH))
        for cp in copies:
            cp.start()
        slots[me] = in_ref[...]
        for k in range(7):
            pltpu.make_async_remote_copy(
                src_ref=in_ref, dst_ref=slots.at[peers[k]], send_sem=send_sems.at[k], recv_sem=recv_sems.at[k],
                device_id=(x, y, c), device_id_type=MESH).wait_recv()
        for cp in copies:
            cp.wait_send()
        acc = slots[0]
        for i in range(1, 8):
            acc = acc + slots[i]
        out_ref[...] = acc

    vmem = pl.BlockSpec(memory_space=pltpu.VMEM)
    return pl.pallas_call(
        body, name="comm_allreduce_small", in_specs=[vmem], out_specs=vmem, out_shape=_sds((rows, cols), F32),
        scratch_shapes=[pltpu.VMEM((8, rows, cols), F32), pltpu.SemaphoreType.DMA((7,)), pltpu.SemaphoreType.DMA((7,))],
        compiler_params=pltpu.CompilerParams(has_side_effects=True, vmem_limit_bytes=VMEM_LIMIT),
    )(buf)


def _adamw(w, g, m, v, block_rows, name):
    rows, cols = w.shape

    def body(w_ref, g_ref, m_ref, v_ref, d_ref, nm_ref, nv_ref):
        g = g_ref[...]
        m = ADAM_B1 * m_ref[...] + (1.0 - ADAM_B1) * g
        v = ADAM_B2 * v_ref[...] + (1.0 - ADAM_B2) * (g * g)
        m_hat = m / (1.0 - ADAM_B1 ** ADAM_STEP)
        v_hat = v / (1.0 - ADAM_B2 ** ADAM_STEP)
        d_ref[...] = -ADAM_LR * (m_hat / (jnp.sqrt(v_hat) + ADAM_EPS) + ADAM_WD * w_ref[...])
        nm_ref[...] = m
        nv_ref[...] = v

    spec = pl.BlockSpec((block_rows, cols), lambda i: (i, 0))
    return pl.pallas_call(
        body, name=name, grid=(rows // block_rows,), in_specs=[spec] * 4, out_specs=[spec] * 3,
        out_shape=[_sds((rows, cols), F32)] * 3,
        compiler_params=_seq_params(),
    )(w, g, m, v)


WEIGHTS = ("pre_mix_norm", "w_in", "sgu_ln_gain", "sgu_ln_bias", "sgu_w_spatial", "sgu_b_spatial", "attn_out_norm",
           "sgu_out_norm", "w_out", "post_mix_norm", "pre_ffn_norm", "w_gate", "w_up", "w_down", "post_ffn_norm")
BIG = ("w_in", "w_out", "w_gate", "w_up", "w_down")
BIG_ADAM_ROWS = {"w_in": 256, "w_out": 128, "w_gate": 256, "w_up": 256, "w_down": 352}
SMALL = ("pre_mix_norm", "post_mix_norm", "pre_ffn_norm", "post_ffn_norm", "sgu_ln_gain", "sgu_ln_bias",
         "attn_out_norm", "sgu_out_norm", "sgu_w_spatial", "sgu_b_spatial")


def _pack_small(d):
    flat = [d[n].reshape(-1) for n in SMALL]
    used = sum(f.shape[0] for f in flat)
    flat.append(jnp.zeros((SMALL_ROWS * 1024 - used,), F32))
    return jnp.concatenate(flat).reshape(SMALL_ROWS, 1024)


def _unpack_small(buf, shapes):
    flat = buf.reshape(-1)
    out, off = {}, 0
    for n in SMALL:
        size = int(np.prod(shapes[n]))
        out[n] = flat[off:off + size].reshape(shapes[n])
        off += size
    return out


def _kernel_unoverlapped(x, positions, pre_mix_norm, w_in, sgu_ln_gain, sgu_ln_bias, sgu_w_spatial, sgu_b_spatial, attn_out_norm, sgu_out_norm, w_out, post_mix_norm, pre_ffn_norm, w_gate, w_up, w_down, post_ffn_norm, loss_target, m_pre_mix_norm, m_w_in, m_sgu_ln_gain, m_sgu_ln_bias, m_sgu_w_spatial, m_sgu_b_spatial, m_attn_out_norm, m_sgu_out_norm, m_w_out, m_post_mix_norm, m_pre_ffn_norm, m_w_gate, m_w_up, m_w_down, m_post_ffn_norm, v_pre_mix_norm, v_w_in, v_sgu_ln_gain, v_sgu_ln_bias, v_sgu_w_spatial, v_sgu_b_spatial, v_attn_out_norm, v_sgu_out_norm, v_w_out, v_post_mix_norm, v_pre_ffn_norm, v_w_gate, v_w_up, v_w_down, v_post_ffn_norm):
    a = dict(locals())
    cx, cy, cc = _coords()
    core = jnp.stack([cc]).astype(jnp.int32)
    shard_core = jnp.stack([2 * cx + cy, cc]).astype(jnp.int32)

    full = _gather_weights([a[n][0].astype(BF16) for n in BIG])
    small = {n: (a[n][0] if a[n].ndim > 2 else a[n]) for n in SMALL}
    loss_cols, grad_x, gws, small_grads = _local_step(
        x[0], positions.reshape(SEQ, 1), loss_target[0], *full, small)
    loss = lax.psum(jnp.sum(loss_cols) * np.float32(0.5 / D_MODEL), ("x", "y", "c"))

    recv_sib = _rs_to_sibling(list(gws))
    chip_part = [_rs_chip_sum(g, r, core) for g, r in zip(gws, recv_sib)]
    recv_chips = _rs_between_chips(chip_part)
    halves = [_rs_final_sum(g, r, rc, shard_core) for g, r, rc in zip(gws, recv_sib, recv_chips)]
    big_grads = dict(zip(BIG, _rs_join_halves(halves)))

    shapes = {n: a[n].shape for n in SMALL}
    small_sum = _allreduce_small(_pack_small(small_grads))

    grads, deltas, new_m, new_v = {}, {}, {}, {}
    for n in BIG:
        grads[n] = big_grads[n][None]
        d, nm, nv = _adamw(a[n][0], big_grads[n], a["m_" + n][0], a["v_" + n][0], BIG_ADAM_ROWS[n], "adamw_" + n)
        deltas[n], new_m[n], new_v[n] = d[None], nm[None], nv[None]
    d, nm, nv = _adamw(_pack_small({n: a[n] for n in SMALL}), small_sum, _pack_small({n: a["m_" + n] for n in SMALL}),
                       _pack_small({n: a["v_" + n] for n in SMALL}), SMALL_ROWS, "adamw_small")
    grads.update(_unpack_small(small_sum, shapes))
    deltas.update(_unpack_small(d, shapes))
    new_m.update(_unpack_small(nm, shapes))
    new_v.update(_unpack_small(nv, shapes))
    return (loss, grad_x[None], *[grads[n] for n in WEIGHTS], *[deltas[n] for n in WEIGHTS],
            *[new_m[n] for n in WEIGHTS], *[new_v[n] for n in WEIGHTS])


def _remote(src, dst, send_sem, recv_sem, to):
    return pltpu.make_async_remote_copy(src_ref=src, dst_ref=dst, send_sem=send_sem, recv_sem=recv_sem,
                                        device_id=to, device_id_type=MESH)


def _halves(a):
    *lead, rows, cols = a.shape
    return a.reshape(*lead, 2, rows // 2, cols)


def _gather_ici(shards):
    n = len(shards)

    def desc(ins, outs, ss, rs, j, w, landed):
        x, y, c = _coords()
        cx, cy = _other_chips(x, y)[j]
        shard = 2 * cx + cy if landed else 2 * x + y
        return _remote(ins[w].at[c], outs[w].at[shard, c], ss.at[j * n + w], rs.at[j * n + w], (cx, cy, c))

    def start(ins, outs, ss, rs):
        for j in range(3):
            for w in range(n):
                desc(ins, outs, ss, rs, j, w, False).start()

    def finish(ins, outs, ss, rs):
        for j in range(3):
            for w in range(n):
                desc(ins, outs, ss, rs, j, w, True).wait_recv()
                desc(ins, outs, ss, rs, j, w, False).wait_send()

    return _Comm(shards, [_sds((N_SHARD,) + s.shape, s.dtype) for s in shards], 3 * n, start, finish)


def _gather_pass(fulls):
    n = len(fulls)

    def desc(bufs, ss, rs, j, w, landed):
        x, y, c = _coords()
        cx, cy = _other_chips(x, y)[j]
        shard = 2 * cx + cy
        return _remote(bufs[w].at[shard, c], bufs[w].at[shard, 1 - c if landed else c],
                       ss.at[j * n + w], rs.at[j * n + w], (x, y, 1 - c))

    def start(ins, outs, ss, rs):
        for j in range(3):
            for w in range(n):
                desc(outs, ss, rs, j, w, False).start()

    def finish(ins, outs, ss, rs):
        for j in range(3):
            for w in range(n):
                desc(outs, ss, rs, j, w, True).wait_recv()
                desc(outs, ss, rs, j, w, False).wait_send()

    return _Comm(fulls, [_sds(f.shape, f.dtype) for f in fulls], 3 * n, start, finish, aliased=True)


def _rs_sibling(gws):
    n = len(gws)

    def desc(ins, outs, ss, rs, w):
        x, y, c = _coords()
        return _remote(ins[w].at[:, 1 - c], outs[w], ss.at[w], rs.at[w], (x, y, 1 - c))

    def start(ins, outs, ss, rs):
        for w in range(n):
            desc(ins, outs, ss, rs, w).start()

    def finish(ins, outs, ss, rs):
        for w in range(n):
            desc(ins, outs, ss, rs, w).wait()

    out_shape = [_sds((N_SHARD, g.shape[1] // 2, g.shape[2]), g.dtype) for g in gws]
    return _Comm([_halves(g) for g in gws], out_shape, n, start, finish)


def _rs_chips(pbs):
    n = len(pbs)

    def desc(ins, outs, ss, rs, j, w):
        x, y, c = _coords()
        cx, cy = _other_chips(x, y)[j]
        return _remote(ins[w].at[2 * cx + cy], outs[w].at[j], ss.at[j * n + w], rs.at[j * n + w], (cx, cy, c))

    def start(ins, outs, ss, rs):
        for j in range(3):
            for w in range(n):
                desc(ins, outs, ss, rs, j, w).start()

    def finish(ins, outs, ss, rs):
        for j in range(3):
            for w in range(n):
                desc(ins, outs, ss, rs, j, w).wait()

    return _Comm(pbs, [_sds((3,) + p.shape[1:], p.dtype) for p in pbs], 3 * n, start, finish)


def _rs_join(halves):
    n = len(halves)

    def desc(bufs, ss, rs, w, landed):
        x, y, c = _coords()
        return _remote(bufs[w].at[c], bufs[w].at[1 - c if landed else c], ss.at[w], rs.at[w], (x, y, 1 - c))

    def start(ins, outs, ss, rs):
        for w in range(n):
            desc(outs, ss, rs, w, False).start()

    def finish(ins, outs, ss, rs):
        for w in range(n):
            desc(outs, ss, rs, w, True).wait_recv()
            desc(outs, ss, rs, w, False).wait_send()

    return _Comm(halves, [_sds(h.shape, h.dtype) for h in halves], n, start, finish, aliased=True)


def _small_exchange(buf):
    def desc(ins, outs, ss, rs, k, landed):
        x, y, c = _coords()
        px = 1 - x if (k >> 2) & 1 else x
        py = 1 - y if (k >> 1) & 1 else y
        pc = 1 - c if k & 1 else c
        slot = 4 * px + 2 * py + pc if landed else 4 * x + 2 * y + c
        return _remote(ins[0], outs[0].at[slot], ss.at[k - 1], rs.at[k - 1], (px, py, pc))

    def start(ins, outs, ss, rs):
        for k in range(1, 8):
            desc(ins, outs, ss, rs, k, False).start()

    def finish(ins, outs, ss, rs):
        for k in range(1, 8):
            desc(ins, outs, ss, rs, k, True).wait_recv()
            desc(ins, outs, ss, rs, k, False).wait_send()

    return _Comm([buf], [_sds((8,) + buf.shape, buf.dtype)], 7, start, finish)


HBM = pl.BlockSpec(memory_space=pltpu.HBM)
SEM = pl.BlockSpec(memory_space=pltpu.SEMAPHORE)
DATAFLOW = pltpu.SideEffectType.DATAFLOW_SIDE_EFFECTING


def _split_start(name, comm, lands, after):
    srcs = [pltpu.with_memory_space_constraint(s, pltpu.HBM) for s in comm.args]
    lands = [pltpu.with_memory_space_constraint(b, pltpu.HBM) for b in lands]
    ns, nb = len(srcs), len(lands)

    def body(*refs):
        send_sems, recv_sems = refs[ns + nb + 1], refs[ns + nb + 2]
        comm.start(refs[:ns], refs[ns:ns + nb], send_sems, recv_sems)
        refs[-1][...] = jnp.zeros_like(refs[-1])

    res = pl.pallas_call(
        body, name=name,
        out_shape=(pltpu.SemaphoreType.DMA((comm.n_sems,)), pltpu.SemaphoreType.DMA((comm.n_sems,)),
                   *[pltpu.HBM(b.shape, b.dtype) for b in srcs + lands], _sds((8, 128), F32)),
        in_specs=[HBM] * (ns + nb) + [ANY],
        out_specs=(SEM, SEM, *[HBM] * (ns + nb), pl.BlockSpec(memory_space=pltpu.VMEM)),
        input_output_aliases={i: 2 + i for i in range(ns + nb)},
        compiler_params=pltpu.CompilerParams(has_side_effects=DATAFLOW),
    )(*srcs, *lands, after)
    return res[0], res[1], list(res[2:2 + ns]), list(res[2 + ns:2 + ns + nb]), res[-1]


def _split_wait(name, comm, send_sems, recv_sems, srcs, lands, after):
    ns, nb = len(srcs), len(lands)

    def body(*refs):
        comm.finish(refs[:ns], refs[ns:ns + nb], refs[ns + nb], refs[ns + nb + 1])

    res = pl.pallas_call(
        body, name=name,
        out_shape=tuple(pltpu.HBM(b.shape, b.dtype) for b in srcs + lands),
        in_specs=[HBM] * (ns + nb) + [SEM, SEM, ANY], out_specs=tuple([HBM] * (ns + nb)),
        input_output_aliases={i: i for i in range(ns + nb)},
        compiler_params=pltpu.CompilerParams(has_side_effects=DATAFLOW),
    )(*srcs, *lands, send_sems, recv_sems, after)
    return list(res[ns:])


SMALL_EARLY = ("post_mix_norm", "pre_ffn_norm", "post_ffn_norm", "sgu_ln_gain", "sgu_ln_bias", "attn_out_norm",
               "sgu_out_norm", "sgu_w_spatial", "sgu_b_spatial")
SMALL_LATE = ("pre_mix_norm",)


def _pack(d, names, rows):
    flat = [d[n].reshape(-1) for n in names]
    used = sum(f.shape[0] for f in flat)
    flat.append(jnp.zeros((rows * 1024 - used,), F32))
    return jnp.concatenate(flat).reshape(rows, 1024)


def _unpack(buf, names, shapes):
    flat = buf.reshape(-1)
    out, off = {}, 0
    for n in names:
        size = int(np.prod(shapes[n]))
        out[n] = flat[off:off + size].reshape(shapes[n])
        off += size
    return out


def _comm_only(name, comms):
    return _pcall(lambda: None, name=name, grid=(1,), in_specs=[], out_specs=[], out_shape=[], args=(),
                  comms=comms)[1]


def _adam_math(w, g, m, v):
    m = ADAM_B1 * m + (1.0 - ADAM_B1) * g
    v = ADAM_B2 * v + (1.0 - ADAM_B2) * (g * g)
    m_hat = m / (1.0 - ADAM_B1 ** ADAM_STEP)
    v_hat = v / (1.0 - ADAM_B2 ** ADAM_STEP)
    return -ADAM_LR * (m_hat / (jnp.sqrt(v_hat) + ADAM_EPS) + ADAM_WD * w), m, v


def _adamw_small(own, slots, w, m, v, me):
    rows, cols = own.shape

    def body(me_ref, own_ref, slots_ref, w_ref, m_ref, v_ref, g_ref, d_ref, nm_ref, nv_ref):
        own_v = own_ref[...]
        g = jnp.where(me_ref[0] == 0, own_v, slots_ref[0])
        for i in range(1, 8):
            g = g + jnp.where(me_ref[0] == i, own_v, slots_ref[i])
        g_ref[...] = g
        d_ref[...], nm_ref[...], nv_ref[...] = _adam_math(w_ref[...], g, m_ref[...], v_ref[...])

    flat = pl.BlockSpec((rows, cols), lambda i, me_ref: (0, 0))
    return pl.pallas_call(
        body, name="adamw_small",
        grid_spec=pltpu.PrefetchScalarGridSpec(
            num_scalar_prefetch=1, grid=(1,),
            in_specs=[flat, pl.BlockSpec((8, rows, cols), lambda i, me_ref: (0, 0, 0)), flat, flat, flat],
            out_specs=[flat] * 4),
        out_shape=[_sds((rows, cols), F32)] * 4,
        compiler_params=_seq_params(),
    )(me, own, slots, w, m, v)


def kernel(x, positions, pre_mix_norm, w_in, sgu_ln_gain, sgu_ln_bias, sgu_w_spatial, sgu_b_spatial, attn_out_norm, sgu_out_norm, w_out, post_mix_norm, pre_ffn_norm, w_gate, w_up, w_down, post_ffn_norm, loss_target, m_pre_mix_norm, m_w_in, m_sgu_ln_gain, m_sgu_ln_bias, m_sgu_w_spatial, m_sgu_b_spatial, m_attn_out_norm, m_sgu_out_norm, m_w_out, m_post_mix_norm, m_pre_ffn_norm, m_w_gate, m_w_up, m_w_down, m_post_ffn_norm, v_pre_mix_norm, v_w_in, v_sgu_ln_gain, v_sgu_ln_bias, v_sgu_w_spatial, v_sgu_b_spatial, v_attn_out_norm, v_sgu_out_norm, v_w_out, v_post_mix_norm, v_pre_ffn_norm, v_w_gate, v_w_up, v_w_down, v_post_ffn_norm):
    a = dict(locals())
    cx, cy, cc = _coords()
    s_me = 2 * cx + cy
    core = jnp.stack([cc]).astype(jnp.int32)
    shard_core = jnp.stack([s_me, cc]).astype(jnp.int32)
    me = jnp.stack([4 * cx + 2 * cy + cc]).astype(jnp.int32)
    small = {n: (a[n][0] if a[n].ndim > 2 else a[n]) for n in SMALL}
    b_t = small["sgu_b_spatial"].T
    xs, pos, target = x[0], positions.reshape(SEQ, 1), loss_target[0]
    own = {n: _halves(a[n][0].astype(BF16)) for n in BIG}

    def with_own(full, n):
        full = lax.dynamic_update_slice(full, own[n][None], (s_me, 0, 0, 0))
        return full.reshape((N_SHARD,) + a[n].shape[1:])

    rest = ("w_out", "w_gate", "w_up", "w_down")
    g_in, g_rest = _gather_ici([own["w_in"]]), _gather_ici([own[n] for n in rest])
    in_send, in_recv, in_srcs, in_lands, token = _split_start(
        "gather_in_start", g_in, [lax.empty(o.shape, o.dtype) for o in g_in.out_shape], small["pre_mix_norm"])
    r_send, r_recv, r_srcs, r_lands, token = _split_start(
        "gather_rest_start", g_rest, [lax.empty(o.shape, o.dtype) for o in g_rest.out_shape], token)
    in_lands = _split_wait("gather_in_wait", g_in, in_send, in_recv, in_srcs, in_lands, token)
    ((in_lands,),) = _comm_only("comm_pass_in", [_gather_pass(in_lands)])
    w_in_f = with_own(in_lands, "w_in")
    h, q, k, v, u, vs = _inproj_fwd(xs, pos, small["pre_mix_norm"], w_in_f)
    views = [tuple(_to_view(t, dil) for t in (q, k, v)) for dil in DILATIONS]
    o_list, l_list = [], []
    for dil, (qv, kv, vv) in zip(DILATIONS, views):
        o, l = _attn_fwd(qv, kv, vv, dil)
        o_list.append(_from_view(o, dil))
        l_list.append(_from_view(l, dil))
    r_lands = _split_wait("gather_rest_wait", g_rest, r_send, r_recv, r_srcs, r_lands, l_list[-1])
    sgu, (r_lands,) = _sgu_fwd(u, vs, small["sgu_ln_gain"], small["sgu_ln_bias"], small["sgu_w_spatial"], b_t,
                               comms=[_gather_pass(r_lands)])
    w_out_f, w_gate_f, w_up_f, w_down_f = (with_own(f, n) for f, n in zip(r_lands, rest))
    attn, lse, mixed, y, x1 = _mix_out_fwd(
        o_list, l_list, sgu, xs, w_out_f, small["attn_out_norm"], small["sgu_out_norm"], small["post_mix_norm"])
    h2, act, dg, dup, df, dx1, loss_cols, d_pre_ffn, d_post_ffn = _ffn_fwd_bwd(
        x1, target, w_gate_f, w_up_f, w_down_f, small["pre_ffn_norm"], small["post_ffn_norm"])
    loss = lax.psum(jnp.sum(loss_cols) * np.float32(0.5 / D_MODEL), ("x", "y", "c"))

    full_tok = pl.BlockSpec((SEQ, D_MODEL), lambda s: (0, 0), pipeline_mode=pl.Buffered(1))
    ff_tok = pl.BlockSpec((1, SEQ, FF_S), lambda s: (s, 0, 0))
    gw = {}
    gw["w_gate"] = _wgrad(h2, dg, full_tok, ff_tok, (D_MODEL, FF_S), "wgrad_gate")
    gw["w_up"], ((sib_gate,),) = _wgrad(h2, dup, full_tok, ff_tok, (D_MODEL, FF_S), "wgrad_up",
                                        comms=[_rs_sibling([gw["w_gate"]])])
    gw["w_down"], ((sib_up,),) = _wgrad(act, df, ff_tok, full_tok, (FF_S, D_MODEL), "wgrad_down",
                                        comms=[_rs_sibling([gw["w_up"]])])
    (dy, dattn, delta, dsgu, d_post_mix, d_attn_norm, d_sgu_norm), ((sib_down,),) = _outproj_bwd(
        dx1, y, attn, sgu, w_out_f, small["post_mix_norm"], small["attn_out_norm"],
        small["sgu_out_norm"], comms=[_rs_sibling([gw["w_down"]])])
    sib = {"w_gate": sib_gate, "w_up": sib_up, "w_down": sib_down}
    part = {n: _rs_chip_sum(gw[n], sib[n], core) for n in ("w_gate", "w_up", "w_down")}
    gw["w_out"] = _wgrad(mixed, dy, pl.BlockSpec((SEQ, OUT_S), lambda s: (0, s)), full_tok, (OUT_S, D_MODEL),
                         "wgrad_out")
    ffn = ("w_gate", "w_up", "w_down")
    x_ffn = _rs_chips([part[n] for n in ffn])
    s_ffn = _split_start("rs_ffn_start", x_ffn, [lax.empty(o.shape, o.dtype) for o in x_ffn.out_shape],
                         part["w_down"])
    (du, dvs_sgu, d_w_sp, d_b_sp, d_ln_gain, d_ln_bias), ((sib["w_out"],),) = _sgu_bwd(
        u, vs, dsgu, small["sgu_ln_gain"], small["sgu_ln_bias"], small["sgu_w_spatial"], b_t,
        comms=[_rs_sibling([gw["w_out"]])], after=[s_ffn[4]])
    part["w_out"] = _rs_chip_sum(gw["w_out"], sib["w_out"], core)
    packed_early = _pack({
        "sgu_ln_gain": d_ln_gain, "sgu_ln_bias": d_ln_bias, "sgu_w_spatial": d_w_sp, "sgu_b_spatial": d_b_sp,
        "attn_out_norm": d_attn_norm, "sgu_out_norm": d_sgu_norm, "post_mix_norm": d_post_mix,
        "pre_ffn_norm": d_pre_ffn, "post_ffn_norm": d_post_ffn}, SMALL_EARLY, SMALL_ROWS)
    x_out, x_small = _rs_chips([part["w_out"]]), _small_exchange(packed_early)
    s_out = _split_start("rs_out_start", x_out, [lax.empty(o.shape, o.dtype) for o in x_out.out_shape], s_ffn[4])
    s_small = _split_start("small_early_start", x_small,
                           [lax.empty(o.shape, o.dtype) for o in x_small.out_shape], s_out[4])

    dqs, dks, dvs = [], [], []
    for dil, (qv, kv, vv) in zip(DILATIONS, views):
        dq, dk, dv = _attn_bwd(qv, kv, vv, _to_view(dattn, dil), _to_view(delta, dil), _to_view(lse, dil), dil,
                               after=[s_small[4]])
        dqs.append(_from_view(dq, dil))
        dks.append(_from_view(dk, dil))
        dvs.append(_from_view(dv, dil))
    half, far, joined = {}, {}, {}
    got = _split_wait("rs_ffn_wait", x_ffn, *s_ffn[:4], dvs[-1])
    for n, f in zip(ffn, got):
        far[n] = f
        half[n] = _rs_final_sum(gw[n], sib[n], f, shard_core)
    (dproj, grad_x, d_pre_mix), (got,) = _inproj_bwd(
        dqs, dks, dvs, du, dvs_sgu, pos, xs, dx1, w_in_f, small["pre_mix_norm"],
        comms=[_rs_join([half[n] for n in ffn])])
    joined.update(zip(ffn, got))
    (far["w_out"],) = _split_wait("rs_out_wait", x_out, *s_out[:4], grad_x)
    (slots_early,) = _split_wait("small_early_wait", x_small, *s_small[:4], far["w_out"])
    half["w_out"] = _rs_final_sum(gw["w_out"], sib["w_out"], far["w_out"], shard_core)
    packed_late = _pack({"pre_mix_norm": d_pre_mix}, SMALL_LATE, 8)
    gw["w_in"], ((joined["w_out"],), (slots_late,)) = _wgrad(
        h, dproj, full_tok, pl.BlockSpec((SEQ, IN_S), lambda s: (0, s)), (D_MODEL, IN_S), "wgrad_in",
        comms=[_rs_join([half["w_out"]]), _small_exchange(packed_late)])
    ((sib["w_in"],),) = _comm_only("comm_rs_sibling_in", [_rs_sibling([gw["w_in"]])])
    part["w_in"] = _rs_chip_sum(gw["w_in"], sib["w_in"], core)
    ((far["w_in"],),) = _comm_only("comm_rs_chips_in", [_rs_chips([part["w_in"]])])
    half["w_in"] = _rs_final_sum(gw["w_in"], sib["w_in"], far["w_in"], shard_core)
    ((joined["w_in"],),) = _comm_only("comm_rs_join_in", [_rs_join([half["w_in"]])])

    grads, deltas, new_m, new_v = {}, {}, {}, {}
    for n in BIG:
        g = joined[n].reshape(a[n].shape[1:])
        grads[n] = g[None]
        d, nm, nv = _adamw(a[n][0], g, a["m_" + n][0], a["v_" + n][0], BIG_ADAM_ROWS[n], "adamw_" + n)
        deltas[n], new_m[n], new_v[n] = d[None], nm[None], nv[None]
    for names, rows, packed, slots in ((SMALL_EARLY, SMALL_ROWS, packed_early, slots_early),
                                       (SMALL_LATE, 8, packed_late, slots_late)):
        outs = _adamw_small(packed, slots, _pack(a, names, rows), _pack({n: a["m_" + n] for n in names}, names, rows),
                            _pack({n: a["v_" + n] for n in names}, names, rows), me)
        for dst, buf in zip((grads, deltas, new_m, new_v), outs):
            dst.update(_unpack(buf, names, {n: a[n].shape for n in names}))
    return (loss, grad_x[None], *[grads[n] for n in WEIGHTS], *[deltas[n] for n in WEIGHTS],
            *[new_m[n] for n in WEIGHTS], *[new_v[n] for n in WEIGHTS])
```

```python
import numpy as np
import jax
import jax.numpy as jnp
from jax import lax
from jax.experimental import pallas as pl
from jax.experimental.pallas import tpu as pltpu

F32 = jnp.float32
BF16 = jnp.bfloat16

SEQ = 2048
D_MODEL = 1024
HEAD_DIM = 64
ATTN_W = 512
SGU_W = 512
SGU_GROUPS = 8
CHUNK = 128
DILATIONS = (1, 4, 16)
N_SHARD = 4
IN_S = 640
OUT_S = 256
FF_S = 704
PROJ_W = N_SHARD * IN_S
RMS_EPS = 1e-6
LN_EPS = 1e-5
ROPE_THETA = 500000.0
ATTN_SCALE = 1.0 / np.sqrt(HEAD_DIM)
NEG = -1e30
TM = 256
VMEM_LIMIT = 56 * 1024 * 1024
SMALL_ROWS = 136

ADAM_LR = 0.001
ADAM_B1 = 0.9
ADAM_B2 = 0.999
ADAM_EPS = 1e-08
ADAM_WD = 0.01
ADAM_STEP = 10

MESH = pl.DeviceIdType.MESH
ANY = pl.BlockSpec(memory_space=pl.ANY)


def _dot(a, b):
    return jnp.dot(a, b, preferred_element_type=F32)


def _dot_nt(a, b):
    return lax.dot_general(a, b, (((1,), (1,)), ((), ())), preferred_element_type=F32)


def _dot_tn(a, b):
    return lax.dot_general(a, b, (((0,), (0,)), ((), ())), preferred_element_type=F32)


def _dot_exact(a, b):
    return jnp.dot(a, b, preferred_element_type=F32, precision=lax.Precision.HIGHEST)


def _rms_stats(x):
    r = lax.rsqrt(jnp.mean(x * x, axis=-1, keepdims=True) + RMS_EPS)
    return x * r, r


def _rms_bwd(xh, r, gain, dy):
    dxh = dy * gain
    dx = r * (dxh - xh * jnp.mean(dxh * xh, axis=-1, keepdims=True))
    return dx, jnp.sum(dy * xh, axis=0, keepdims=True)


_ERF_ALPHA = (-2.72614225801306e-10, 2.77068142495902e-08, -2.10102402082508e-06, -5.69250639462346e-05,
              -7.34990630326855e-04, -2.95459980854025e-03, -1.60960333262415e-02)
_ERF_BETA = (-1.45660718464996e-05, -2.13374055278905e-04, -1.68282697438203e-03, -7.37332916720468e-03,
             -1.42647390514189e-02)


def _erf(x):
    x = jnp.clip(x, -4.0, 4.0)
    x2 = x * x
    p = jnp.full_like(x, _ERF_ALPHA[0])
    for a in _ERF_ALPHA[1:]:
        p = p * x2 + a
    q = jnp.full_like(x, _ERF_BETA[0])
    for b in _ERF_BETA[1:]:
        q = q * x2 + b
    return x * p / q


def _gelu(x):
    return 0.5 * x * (1.0 + _erf(x * np.float32(1.0 / np.sqrt(2.0))))


def _gelu_grad(x):
    cdf = 0.5 * (1.0 + _erf(x * np.float32(1.0 / np.sqrt(2.0))))
    pdf = jnp.exp(-0.5 * x * x) * np.float32(1.0 / np.sqrt(2.0 * np.pi))
    return cdf + x * pdf


def _sigmoid(x):
    return 1.0 / (1.0 + jnp.exp(-x))


_INV_FREQ = tuple(float(np.float32(ROPE_THETA ** (-2.0 * j / 16.0))) for j in range(8))


def _rot_tables(pos):
    lane = lax.broadcasted_iota(jnp.int32, (1, 128), 1)
    d = lane & 63
    j = d & 7
    inv = jnp.zeros((1, 128), F32)
    for jj in range(8):
        inv = jnp.where(j == jj, _INV_FREQ[jj], inv)
    ang = pos.astype(F32) * inv
    c = jnp.cos(ang)
    s = jnp.sin(ang)
    cos_t = jnp.where(d < 16, c, 1.0)
    sin_a = jnp.where(d < 8, -s, 0.0)
    sin_b = jnp.where((d >= 8) & (d < 16), s, 0.0)
    return tuple(jnp.tile(t, (1, 4)) for t in (cos_t, sin_a, sin_b))


def _rope(x, tabs):
    cos_t, sin_a, sin_b = tabs
    return x * cos_t + pltpu.roll(x, 504, 1) * sin_a + pltpu.roll(x, 8, 1) * sin_b


def _rope_bwd(dy, tabs):
    cos_t, sin_a, sin_b = tabs
    return dy * cos_t + pltpu.roll(dy * sin_a, 8, 1) + pltpu.roll(dy * sin_b, 504, 1)


def _left_half():
    return lax.broadcasted_iota(jnp.int32, (CHUNK, CHUNK), 1) < HEAD_DIM


def _group_ones():
    lane = lax.broadcasted_iota(jnp.int32, (SGU_GROUPS, SGU_W), 1)
    row = lax.broadcasted_iota(jnp.int32, (SGU_GROUPS, SGU_W), 0)
    return ((lane >> 6) == row).astype(F32)


def _masked_spatial(w_ref):
    row = lax.broadcasted_iota(jnp.int32, (CHUNK, CHUNK), 0)
    col = lax.broadcasted_iota(jnp.int32, (CHUNK, CHUNK), 1)
    return [jnp.where(col <= row, w_ref[g], 0.0).astype(BF16) for g in range(SGU_GROUPS)]


def _sgu_core(u, vs, lg, lb, wm, bias_full):
    tm = u.shape[0]
    gu = _gelu(u)
    gv = _gelu(vs)
    mu = jnp.mean(gv, axis=-1, keepdims=True)
    xc = gv - mu
    rstd = lax.rsqrt(jnp.mean(xc * xc, axis=-1, keepdims=True) + LN_EPS)
    xh = xc * rstd
    vnb = (xh * lg + lb).astype(BF16)
    left = _left_half()
    rows = []
    for c in range(tm // CHUNK):
        pieces = []
        for p in range(4):
            vp = vnb[c * CHUNK:(c + 1) * CHUNK, p * 128:(p + 1) * 128]
            pieces.append(jnp.where(left, _dot(wm[2 * p], vp), _dot(wm[2 * p + 1], vp)))
        rows.append(jnp.concatenate(pieces, axis=1) + bias_full)
    mixed = jnp.concatenate(rows, axis=0)
    return gu, xh, rstd, vnb, mixed


def _resident(shape):
    n = len(shape)
    return pl.BlockSpec(shape, lambda *_: (0,) * n, pipeline_mode=pl.Buffered(1))


def _rows(ncol, tm=TM):
    return pl.BlockSpec((tm, ncol), lambda i: (i, 0))


def _rows3(nlead, ncol, tm=TM):
    return pl.BlockSpec((nlead, tm, ncol), lambda i: (0, i, 0))


def _acc(ncol, nrow=1):
    return pl.BlockSpec((nrow, ncol), lambda i: (0, 0))


def _seq_params():
    return pltpu.CompilerParams(dimension_semantics=("arbitrary",), vmem_limit_bytes=VMEM_LIMIT)


def _sds(shape, dtype):
    return jax.ShapeDtypeStruct(shape, dtype)


class _Comm:
    def __init__(self, args, out_shape, n_sems, start, finish, aliased=False):
        self.args, self.out_shape, self.n_sems = list(args), list(out_shape), n_sems
        self.start, self.finish, self.aliased = start, finish, aliased


def _pcall(body, *, name, grid, in_specs, out_specs, out_shape, args, scratch_shapes=(), comms=(), after=()):
    single = not isinstance(out_shape, (list, tuple))
    out_specs = [out_specs] if single else list(out_specs)
    out_shape = [out_shape] if single else list(out_shape)
    n_in, n_out, n_scr = len(in_specs), len(out_shape), len(scratch_shapes)
    c_args = [a for c in comms for a in c.args]
    c_outs = [o for c in comms for o in c.out_shape]
    aliases, ai, ao = {}, n_in, n_out
    for c in comms:
        if c.aliased:
            aliases.update({ai + k: ao + k for k in range(len(c.args))})
        ai += len(c.args)
        ao += len(c.out_shape)
    sems = [pltpu.SemaphoreType.DMA((c.n_sems,)) for c in comms for _ in range(2)]
    steps = grid[0]

    def wrapped(*refs):
        o0 = n_in + len(c_args) + len(after)
        s0 = o0 + n_out + len(c_outs)
        m_in, m_out, m_sem = refs[n_in:n_in + len(c_args)], refs[o0 + n_out:s0], refs[s0 + n_scr:]

        def each(phase):
            ii = oi = 0
            for k, c in enumerate(comms):
                getattr(c, phase)(m_in[ii:ii + len(c.args)], m_out[oi:oi + len(c.out_shape)],
                                  m_sem[2 * k], m_sem[2 * k + 1])
                ii += len(c.args)
                oi += len(c.out_shape)

        if comms:
            @pl.when(pl.program_id(0) == 0)
            def _():
                each("start")

        body(*refs[:n_in], *refs[o0:o0 + n_out], *refs[s0:s0 + n_scr])

        if comms:
            @pl.when(pl.program_id(0) == steps - 1)
            def _():
                each("finish")

    res = pl.pallas_call(
        wrapped, name=name, grid=grid,
        in_specs=list(in_specs) + [ANY] * (len(c_args) + len(after)), out_specs=out_specs + [ANY] * len(c_outs),
        out_shape=out_shape + c_outs, scratch_shapes=list(scratch_shapes) + sems,
        input_output_aliases=aliases, compiler_params=_seq_params(),
    )(*args, *c_args, *after)
    mine = res[0] if single else list(res[:n_out])
    if not comms:
        return mine
    theirs, oi = [], n_out
    for c in comms:
        theirs.append(list(res[oi:oi + len(c.out_shape)]))
        oi += len(c.out_shape)
    return mine, theirs


def _inproj_fwd(x, pos, g_pre, w_in, comms=()):
    def body(x_ref, pos_ref, g_ref, w_ref, h_ref, q_ref, k_ref, v_ref, u_ref, vs_ref):
        xh, _ = _rms_stats(x_ref[...])
        h = (xh * g_ref[...]).astype(BF16)
        h_ref[...] = h
        proj = jnp.concatenate([_dot(h, w_ref[s]) for s in range(N_SHARD)], axis=1)
        tabs = _rot_tables(pos_ref[...])
        q_ref[...] = (_rope(proj[:, 0:512], tabs) * np.float32(ATTN_SCALE)).astype(BF16)
        k_ref[...] = _rope(proj[:, 512:1024], tabs).astype(BF16)
        v_ref[...] = proj[:, 1024:1536].astype(BF16)
        u_ref[...] = proj[:, 1536:2048]
        vs_ref[...] = proj[:, 2048:2560]

    return _pcall(
        body, name="inproj_fwd", grid=(SEQ // TM,),
        in_specs=[_rows(D_MODEL), _rows(1), _resident((1, D_MODEL)), _resident((N_SHARD, D_MODEL, IN_S))],
        out_specs=[_rows(D_MODEL), _rows(512), _rows(512), _rows(512), _rows(512), _rows(512)],
        out_shape=[_sds((SEQ, D_MODEL), BF16), _sds((SEQ, 512), BF16), _sds((SEQ, 512), BF16),
                   _sds((SEQ, 512), BF16), _sds((SEQ, 512), F32), _sds((SEQ, 512), F32)],
        args=(x, pos, g_pre, w_in), comms=comms)


def _sgu_fwd(u, vs, lg, lb, w_sp, b_t, comms=()):
    def body(u_ref, vs_ref, lg_ref, lb_ref, w_ref, bt_ref, out_ref):
        wm = _masked_spatial(w_ref)
        bias_full = _dot_exact(bt_ref[...], _group_ones())
        gu, _, _, _, mixed = _sgu_core(u_ref[...], vs_ref[...], lg_ref[...], lb_ref[...], wm, bias_full)
        out_ref[...] = gu * mixed

    return _pcall(
        body, name="sgu_fwd", grid=(SEQ // TM,),
        in_specs=[_rows(SGU_W), _rows(SGU_W), _resident((1, SGU_W)), _resident((1, SGU_W)),
                  _resident((SGU_GROUPS, CHUNK, CHUNK)), _resident((CHUNK, SGU_GROUPS))],
        out_specs=_rows(SGU_W),
        out_shape=_sds((SEQ, SGU_W), F32),
        args=(u, vs, lg, lb, w_sp, b_t), comms=comms)


def _block_masks():
    row = lax.broadcasted_iota(jnp.int32, (CHUNK, CHUNK), 0)
    col = lax.broadcasted_iota(jnp.int32, (CHUNK, CHUNK), 1)
    return col <= row, col >= row


def _attn_fwd(qv, kv, vv, dil, comms=()):
    seg = SEQ // dil
    nblk = seg // CHUNK

    def body(q_ref, k_ref, v_ref, o_ref, l_ref):
        left = _left_half()
        m_cur, m_prev = _block_masks()
        zero = jnp.zeros((CHUNK, CHUNK), BF16)
        ones = (jnp.where(left, 1.0, 0.0).astype(BF16), jnp.where(left, 0.0, 1.0).astype(BF16))

        def blk(b, carry):
            r0 = pl.multiple_of(b * CHUNK, CHUNK)
            rp = pl.multiple_of(jnp.maximum(b - 1, 0) * CHUNK, CHUNK)
            prev_ok = m_prev & (b > 0)
            sides = tuple(enumerate((left, ~left)))
            tiles, scores = [], []
            for hp in range(4):
                ls = slice(hp * 128, (hp + 1) * 128)
                qp = q_ref[pl.ds(r0, CHUNK), ls]
                kc = k_ref[pl.ds(r0, CHUNK), ls]
                kp = k_ref[pl.ds(rp, CHUNK), ls] if nblk > 1 else None
                tiles.append((ls, v_ref[pl.ds(r0, CHUNK), ls], v_ref[pl.ds(rp, CHUNK), ls] if nblk > 1 else None))
                for _, hm in sides:
                    qh = jnp.where(hm, qp, zero)
                    sc = jnp.where(m_cur, _dot_nt(qh, kc), NEG)
                    sp = jnp.where(prev_ok, _dot_nt(qh, kp), NEG) if nblk > 1 else None
                    scores.append((sc, sp))
            probs = []
            for sc, sp in scores:
                if nblk > 1:
                    m = jnp.max(jnp.maximum(sc, sp), axis=-1, keepdims=True)
                    pc = jnp.exp(sc - m)
                    pp = jnp.exp(sp - m)
                    probs.append((m, pc.astype(BF16), pp.astype(BF16), (pc + pp).astype(BF16)))
                else:
                    m = jnp.max(sc, axis=-1, keepdims=True)
                    pc = jnp.exp(sc - m).astype(BF16)
                    probs.append((m, pc, None, pc))
            for hp, (ls, vc, vp) in enumerate(tiles):
                acc = jnp.zeros((CHUNK, CHUNK), F32)
                den = jnp.zeros((CHUNK, CHUNK), F32)
                for side, hm in sides:
                    _, pc, pp, psum = probs[2 * hp + side]
                    acc = acc + _dot(pc, jnp.where(hm, vc, zero))
                    if nblk > 1:
                        acc = acc + _dot(pp, jnp.where(hm, vp, zero))
                    den = den + _dot(psum, ones[side])
                o_ref[pl.ds(r0, CHUNK), ls] = acc / den
                l_ref[pl.ds(r0, CHUNK), ls] = jnp.where(left, probs[2 * hp][0], probs[2 * hp + 1][0]) + jnp.log(den)
            return carry

        lax.fori_loop(0, nblk, blk, 0)

    spec = pl.BlockSpec((seg, ATTN_W), lambda r: (0, r))
    return _pcall(
        body, name=f"attn_fwd_d{dil}", grid=(dil,),
        in_specs=[spec, spec, spec], out_specs=[spec, spec],
        out_shape=[_sds((seg, dil * ATTN_W), F32), _sds((seg, dil * ATTN_W), F32)],
        args=(qv, kv, vv), comms=comms)


def _mix_out_fwd(o_list, l_list, sgu, x, w_out, g_attn, g_sgu, g_post, comms=()):
    def body(o1, o2, o3, l1, l2, l3, sgu_ref, x_ref, w_ref, ga_ref, gs_ref, gp_ref,
             attn_ref, lse_ref, mixed_ref, y_ref, x1_ref):
        ls = [l1[...], l2[...], l3[...]]
        m = jnp.maximum(jnp.maximum(ls[0], ls[1]), ls[2])
        es = [jnp.exp(l - m) for l in ls]
        den = es[0] + es[1] + es[2]
        attn = (es[0] * o1[...] + es[1] * o2[...] + es[2] * o3[...]) / den
        attn_ref[...] = attn
        lse_ref[...] = m + jnp.log(den)
        ah, _ = _rms_stats(attn)
        sh, _ = _rms_stats(sgu_ref[...])
        mixed = jnp.concatenate([ah * ga_ref[...], sh * gs_ref[...]], axis=1).astype(BF16)
        mixed_ref[...] = mixed
        y = _dot(mixed[:, 0:OUT_S], w_ref[0])
        for s in range(1, N_SHARD):
            y = y + _dot(mixed[:, s * OUT_S:(s + 1) * OUT_S], w_ref[s])
        y_ref[...] = y
        yh, _ = _rms_stats(y)
        x1_ref[...] = x_ref[...] + yh * gp_ref[...]

    return _pcall(
        body, name="mix_out_fwd", grid=(SEQ // TM,),
        in_specs=[_rows(512)] * 7 + [_rows(D_MODEL), _resident((N_SHARD, OUT_S, D_MODEL)),
                                    _resident((1, 512)), _resident((1, 512)), _resident((1, D_MODEL))],
        out_specs=[_rows(512), _rows(512), _rows(D_MODEL), _rows(D_MODEL), _rows(D_MODEL)],
        out_shape=[_sds((SEQ, 512), F32), _sds((SEQ, 512), F32), _sds((SEQ, D_MODEL), BF16),
                   _sds((SEQ, D_MODEL), F32), _sds((SEQ, D_MODEL), F32)],
        args=(*o_list, *l_list, sgu, x, w_out, g_attn, g_sgu, g_post), comms=comms)


def _ffn_fwd_bwd(x1, target, w_gate, w_up, w_down, g_pre, g_post, comms=()):
    def body(x1_ref, t_ref, wg_ref, wu_ref, wd_ref, gpf_ref, gpo_ref,
             h2_ref, a_ref, dg_ref, dup_ref, df_ref, dx1_ref, loss_ref, dgpf_ref, dgpo_ref, g_scr, up_scr):
        @pl.when(pl.program_id(0) == 0)
        def _():
            loss_ref[...] = jnp.zeros_like(loss_ref)
            dgpf_ref[...] = jnp.zeros_like(dgpf_ref)
            dgpo_ref[...] = jnp.zeros_like(dgpo_ref)

        x1 = x1_ref[...]
        gpf = gpf_ref[...]
        gpo = gpo_ref[...]
        xh, r = _rms_stats(x1)
        h2 = (xh * gpf).astype(BF16)
        h2_ref[...] = h2
        f = jnp.zeros((TM, D_MODEL), F32)
        for s in range(N_SHARD):
            g = _dot(h2, wg_ref[s])
            up = _dot(h2, wu_ref[s])
            g_scr[s] = g
            up_scr[s] = up
            a = (g * _sigmoid(g) * up).astype(BF16)
            a_ref[s] = a
            f = f + _dot(a, wd_ref[s])
        fh, rf = _rms_stats(f)
        diff = x1 + fh * gpo - t_ref[...]
        loss_ref[...] += jnp.sum(diff * diff, axis=0, keepdims=True)
        dout = diff * np.float32(1.0 / D_MODEL)
        df, dgpo = _rms_bwd(fh, rf, gpo, dout)
        dgpo_ref[...] += dgpo
        dfb = df.astype(BF16)
        df_ref[...] = dfb
        dh2 = jnp.zeros((TM, D_MODEL), F32)
        for s in range(N_SHARD):
            da = _dot_nt(dfb, wd_ref[s])
            g = g_scr[s]
            up = up_scr[s]
            sg = _sigmoid(g)
            dup = (da * (g * sg)).astype(BF16)
            dg = (da * up * (sg * (1.0 + g * (1.0 - sg)))).astype(BF16)
            dg_ref[s] = dg
            dup_ref[s] = dup
            dh2 = dh2 + _dot_nt(dg, wg_ref[s]) + _dot_nt(dup, wu_ref[s])
        dx, dgpf = _rms_bwd(xh, r, gpf, dh2)
        dgpf_ref[...] += dgpf
        dx1_ref[...] = dout + dx

    return _pcall(
        body, name="ffn_fwd_bwd", grid=(SEQ // TM,),
        in_specs=[_rows(D_MODEL), _rows(D_MODEL), _resident((N_SHARD, D_MODEL, FF_S)),
                  _resident((N_SHARD, D_MODEL, FF_S)), _resident((N_SHARD, FF_S, D_MODEL)),
                  _resident((1, D_MODEL)), _resident((1, D_MODEL))],
        out_specs=[_rows(D_MODEL), _rows3(N_SHARD, FF_S), _rows3(N_SHARD, FF_S), _rows3(N_SHARD, FF_S),
                   _rows(D_MODEL), _rows(D_MODEL), _acc(D_MODEL), _acc(D_MODEL), _acc(D_MODEL)],
        out_shape=[_sds((SEQ, D_MODEL), BF16), _sds((N_SHARD, SEQ, FF_S), BF16), _sds((N_SHARD, SEQ, FF_S), BF16),
                   _sds((N_SHARD, SEQ, FF_S), BF16), _sds((SEQ, D_MODEL), BF16), _sds((SEQ, D_MODEL), F32),
                   _sds((1, D_MODEL), F32), _sds((1, D_MODEL), F32), _sds((1, D_MODEL), F32)],
        scratch_shapes=[pltpu.VMEM((N_SHARD, TM, FF_S), F32), pltpu.VMEM((N_SHARD, TM, FF_S), F32)],
        args=(x1, target, w_gate, w_up, w_down, g_pre, g_post), comms=comms)


def _wgrad(a, b, a_spec, b_spec, out_block, name, comms=()):
    def body(a_ref, b_ref, o_ref):
        av = a_ref[0] if len(a_ref.shape) == 3 else a_ref[...]
        bv = b_ref[0] if len(b_ref.shape) == 3 else b_ref[...]
        o_ref[0] = _dot_tn(av, bv)

    return _pcall(
        body, name=name, grid=(N_SHARD,),
        in_specs=[a_spec, b_spec],
        out_specs=pl.BlockSpec((1,) + out_block, lambda s: (s, 0, 0)),
        out_shape=_sds((N_SHARD,) + out_block, F32),
        args=(a, b), comms=comms)


def _outproj_bwd(dx1, y, attn, sgu, w_out, g_post, g_attn, g_sgu, comms=()):
    def body(dx1_ref, y_ref, attn_ref, sgu_ref, w_ref, gp_ref, ga_ref, gs_ref,
             dy_ref, dattn_ref, delta_ref, dsgu_ref, dgp_ref, dga_ref, dgs_ref):
        @pl.when(pl.program_id(0) == 0)
        def _():
            dgp_ref[...] = jnp.zeros_like(dgp_ref)
            dga_ref[...] = jnp.zeros_like(dga_ref)
            dgs_ref[...] = jnp.zeros_like(dgs_ref)

        yh, ry = _rms_stats(y_ref[...])
        dy, dgp = _rms_bwd(yh, ry, gp_ref[...], dx1_ref[...])
        dgp_ref[...] += dgp
        dyb = dy.astype(BF16)
        dy_ref[...] = dyb
        dmixed = jnp.concatenate([_dot_nt(dyb, w_ref[s]) for s in range(N_SHARD)], axis=1)
        attn = attn_ref[...]
        ah, ra = _rms_stats(attn)
        dattn, dga = _rms_bwd(ah, ra, ga_ref[...], dmixed[:, 0:512])
        dga_ref[...] += dga
        sh, rs = _rms_stats(sgu_ref[...])
        dsgu, dgs = _rms_bwd(sh, rs, gs_ref[...], dmixed[:, 512:1024])
        dgs_ref[...] += dgs
        dattn_ref[...] = dattn.astype(BF16)
        dsgu_ref[...] = dsgu
        la = lax.broadcasted_iota(jnp.int32, (ATTN_W, ATTN_W), 0) >> 6
        lb = lax.broadcasted_iota(jnp.int32, (ATTN_W, ATTN_W), 1) >> 6
        delta_ref[...] = _dot_exact(dattn * attn, (la == lb).astype(F32))

    return _pcall(
        body, name="outproj_bwd", grid=(SEQ // TM,),
        in_specs=[_rows(D_MODEL), _rows(D_MODEL), _rows(512), _rows(512), _resident((N_SHARD, OUT_S, D_MODEL)),
                  _resident((1, D_MODEL)), _resident((1, 512)), _resident((1, 512))],
        out_specs=[_rows(D_MODEL), _rows(512), _rows(512), _rows(512), _acc(D_MODEL), _acc(512), _acc(512)],
        out_shape=[_sds((SEQ, D_MODEL), BF16), _sds((SEQ, 512), BF16), _sds((SEQ, 512), F32), _sds((SEQ, 512), F32),
                   _sds((1, D_MODEL), F32), _sds((1, 512), F32), _sds((1, 512), F32)],
        args=(dx1, y, attn, sgu, w_out, g_post, g_attn, g_sgu), comms=comms)


def _sgu_bwd(u, vs, dsgu, lg, lb, w_sp, b_t, comms=(), after=()):
    nsteps = SEQ // TM

    def body(u_ref, vs_ref, ds_ref, lg_ref, lb_ref, w_ref, bt_ref,
             du_ref, dvs_ref, dw_ref, db_ref, dlg_ref, dlb_ref, dbias_scr):
        i = pl.program_id(0)

        @pl.when(i == 0)
        def _():
            dw_ref[...] = jnp.zeros_like(dw_ref)
            dlg_ref[...] = jnp.zeros_like(dlg_ref)
            dlb_ref[...] = jnp.zeros_like(dlb_ref)
            dbias_scr[...] = jnp.zeros_like(dbias_scr)

        wm = _masked_spatial(w_ref)
        ones_g = _group_ones()
        bias_full = _dot_exact(bt_ref[...], ones_g)
        u = u_ref[...]
        vs = vs_ref[...]
        lg = lg_ref[...]
        gu, xh, rstd, vnb, mixed = _sgu_core(u, vs, lg, lb_ref[...], wm, bias_full)
        dsgu = ds_ref[...]
        du_ref[...] = (dsgu * mixed * _gelu_grad(u)).astype(BF16)
        dmixed = dsgu * gu
        left = _left_half()
        dvn_rows = []
        for c in range(TM // CHUNK):
            rs = slice(c * CHUNK, (c + 1) * CHUNK)
            dm_c = dmixed[rs, :]
            dbias_scr[...] += dm_c
            pieces = []
            for p in range(4):
                ls = slice(p * 128, (p + 1) * 128)
                dmp = dm_c[:, ls]
                vp = vnb[rs, ls]
                dmb = dmp.astype(BF16)
                zero = jnp.zeros_like(dmb)
                dw_ref[2 * p] += _dot_nt(jnp.where(left, dmb, zero), vp)
                dw_ref[2 * p + 1] += _dot_nt(jnp.where(left, zero, dmb), vp)
                pieces.append(jnp.where(left, _dot_tn(wm[2 * p], dmb), _dot_tn(wm[2 * p + 1], dmb)))
            dvn_rows.append(jnp.concatenate(pieces, axis=1))
        dvn = jnp.concatenate(dvn_rows, axis=0)
        dlg_ref[...] += jnp.sum(dvn * xh, axis=0, keepdims=True)
        dlb_ref[...] += jnp.sum(dvn, axis=0, keepdims=True)
        dxh = dvn * lg
        dgv = rstd * (dxh - jnp.mean(dxh, axis=-1, keepdims=True) - xh * jnp.mean(dxh * xh, axis=-1, keepdims=True))
        dvs_ref[...] = (dgv * _gelu_grad(vs)).astype(BF16)

        @pl.when(i == nsteps - 1)
        def _():
            row = lax.broadcasted_iota(jnp.int32, (CHUNK, CHUNK), 0)
            col = lax.broadcasted_iota(jnp.int32, (CHUNK, CHUNK), 1)
            for g in range(SGU_GROUPS):
                dw_ref[g] = jnp.where(col <= row, dw_ref[g], 0.0)
            db_ref[...] = lax.dot_general(ones_g, dbias_scr[...], (((1,), (1,)), ((), ())),
                                          preferred_element_type=F32, precision=lax.Precision.HIGHEST)

    return _pcall(
        body, name="sgu_bwd", grid=(nsteps,),
        in_specs=[_rows(SGU_W), _rows(SGU_W), _rows(SGU_W), _resident((1, SGU_W)), _resident((1, SGU_W)),
                  _resident((SGU_GROUPS, CHUNK, CHUNK)), _resident((CHUNK, SGU_GROUPS))],
        out_specs=[_rows(SGU_W), _rows(SGU_W), pl.BlockSpec((SGU_GROUPS, CHUNK, CHUNK), lambda i: (0, 0, 0)),
                   _acc(CHUNK, SGU_GROUPS), _acc(SGU_W), _acc(SGU_W)],
        out_shape=[_sds((SEQ, SGU_W), BF16), _sds((SEQ, SGU_W), BF16), _sds((SGU_GROUPS, CHUNK, CHUNK), F32),
                   _sds((SGU_GROUPS, CHUNK), F32), _sds((1, SGU_W), F32), _sds((1, SGU_W), F32)],
        scratch_shapes=[pltpu.VMEM((CHUNK, SGU_W), F32)],
        args=(u, vs, dsgu, lg, lb, w_sp, b_t), comms=comms, after=after)


def _attn_bwd(qv, kv, vv, dov, deltav, lsev, dil, comms=(), after=()):
    seg = SEQ // dil
    nblk = seg // CHUNK

    def body(q_ref, k_ref, v_ref, do_ref, dl_ref, lse_ref, dq_ref, dk_ref, dv_ref):
        left = _left_half()
        m_cur, m_prev = _block_masks()

        def blk(b, carry):
            r0 = pl.multiple_of(b * CHUNK, CHUNK)
            rp = pl.multiple_of(jnp.maximum(b - 1, 0) * CHUNK, CHUNK)
            prev_ok = m_prev & (b > 0)
            sides = tuple(enumerate((left, ~left)))
            zero = jnp.zeros((CHUNK, CHUNK), BF16)
            tiles, firsts = [], []
            for hp in range(4):
                ls = slice(hp * 128, (hp + 1) * 128)
                qp = q_ref[pl.ds(r0, CHUNK), ls]
                kc = k_ref[pl.ds(r0, CHUNK), ls]
                vc = v_ref[pl.ds(r0, CHUNK), ls]
                dop = do_ref[pl.ds(r0, CHUNK), ls]
                kp = k_ref[pl.ds(rp, CHUNK), ls] if nblk > 1 else None
                vp = v_ref[pl.ds(rp, CHUNK), ls] if nblk > 1 else None
                tiles.append((ls, kc, kp))
                for side, hm in sides:
                    qh = jnp.where(hm, qp, zero)
                    doh = jnp.where(hm, dop, zero)
                    cur = (_dot_nt(qh, kc), _dot_nt(doh, vc))
                    prev = (_dot_nt(qh, kp), _dot_nt(doh, vp)) if nblk > 1 else None
                    firsts.append((qh, doh, cur, prev))
            seconds = []
            for i, (qh, doh, cur, prev) in enumerate(firsts):
                hp, c0 = i // 2, (i % 2) * HEAD_DIM
                ls = slice(hp * 128, (hp + 1) * 128)
                lse_h = lse_ref[pl.ds(r0, CHUNK), ls][:, c0:c0 + 1]
                dl_h = dl_ref[pl.ds(r0, CHUNK), ls][:, c0:c0 + 1]
                pc = jnp.exp(jnp.where(m_cur, cur[0] - lse_h, NEG))
                out = [pc.astype(BF16), (pc * (cur[1] - dl_h)).astype(BF16), None, None]
                if nblk > 1:
                    pp = jnp.exp(jnp.where(prev_ok, prev[0] - lse_h, NEG))
                    out[2:] = [pp.astype(BF16), (pp * (prev[1] - dl_h)).astype(BF16)]
                seconds.append(out)
            for hp, (ls, kc, kp) in enumerate(tiles):
                dq = jnp.zeros((CHUNK, CHUNK), F32)
                dkc = jnp.zeros((CHUNK, CHUNK), F32)
                dvc = jnp.zeros((CHUNK, CHUNK), F32)
                dkp = jnp.zeros((CHUNK, CHUNK), F32)
                dvp = jnp.zeros((CHUNK, CHUNK), F32)
                for side, hm in sides:
                    qh, doh, _, _ = firsts[2 * hp + side]
                    pcb, dsc, ppb, dsp = seconds[2 * hp + side]
                    dq = dq + _dot(dsc, jnp.where(hm, kc, zero))
                    dkc = dkc + _dot_tn(dsc, qh)
                    dvc = dvc + _dot_tn(pcb, doh)
                    if nblk > 1:
                        dq = dq + _dot(dsp, jnp.where(hm, kp, zero))
                        dkp = dkp + _dot_tn(dsp, qh)
                        dvp = dvp + _dot_tn(ppb, doh)
                dq_ref[pl.ds(r0, CHUNK), ls] = dq
                dk_ref[pl.ds(r0, CHUNK), ls] = dkc
                dv_ref[pl.ds(r0, CHUNK), ls] = dvc
                if nblk > 1:
                    @pl.when(b > 0)
                    def _():
                        dk_ref[pl.ds(rp, CHUNK), ls] += dkp
                        dv_ref[pl.ds(rp, CHUNK), ls] += dvp
            return carry

        lax.fori_loop(0, nblk, blk, 0)

    spec = pl.BlockSpec((seg, ATTN_W), lambda r: (0, r))
    return _pcall(
        body, name=f"attn_bwd_d{dil}", grid=(dil,),
        in_specs=[spec] * 6, out_specs=[spec] * 3,
        out_shape=[_sds((seg, dil * ATTN_W), F32)] * 3,
        args=(qv, kv, vv, dov, deltav, lsev), comms=comms, after=after)


def _inproj_bwd(dqs, dks, dvs, du, dvs_sgu, pos, x, dx1, w_in, g_pre, comms=()):
    def body(dq1, dq2, dq3, dk1, dk2, dk3, dv1, dv2, dv3, du_ref, dvs_ref, pos_ref, x_ref, dx1_ref, w_ref, g_ref,
             dproj_ref, gx_ref, dg_ref):
        @pl.when(pl.program_id(0) == 0)
        def _():
            dg_ref[...] = jnp.zeros_like(dg_ref)

        tabs = _rot_tables(pos_ref[...])
        dproj_ref[:, 0:512] = _rope_bwd((dq1[...] + dq2[...] + dq3[...]) * np.float32(ATTN_SCALE), tabs).astype(BF16)
        dproj_ref[:, 512:1024] = _rope_bwd(dk1[...] + dk2[...] + dk3[...], tabs).astype(BF16)
        dproj_ref[:, 1024:1536] = (dv1[...] + dv2[...] + dv3[...]).astype(BF16)
        dproj_ref[:, 1536:2048] = du_ref[...]
        dproj_ref[:, 2048:2560] = dvs_ref[...]
        dh = jnp.zeros((TM, D_MODEL), F32)
        for s in range(N_SHARD):
            dh = dh + _dot_nt(dproj_ref[:, s * IN_S:(s + 1) * IN_S], w_ref[s])
        g = g_ref[...]
        xh, r = _rms_stats(x_ref[...])
        dx, dg = _rms_bwd(xh, r, g, dh)
        dg_ref[...] += dg
        gx_ref[...] = dx1_ref[...] + dx

    return _pcall(
        body, name="inproj_bwd", grid=(SEQ // TM,),
        in_specs=[_rows(512)] * 11 + [_rows(1), _rows(D_MODEL), _rows(D_MODEL),
                                     _resident((N_SHARD, D_MODEL, IN_S)), _resident((1, D_MODEL))],
        out_specs=[_rows(PROJ_W), _rows(D_MODEL), _acc(D_MODEL)],
        out_shape=[_sds((SEQ, PROJ_W), BF16), _sds((SEQ, D_MODEL), F32), _sds((1, D_MODEL), F32)],
        args=(*dqs, *dks, *dvs, du, dvs_sgu, pos, x, dx1, w_in, g_pre), comms=comms)


def _to_view(a, dil):
    return a if dil == 1 else a.reshape(SEQ // dil, dil * a.shape[1])


def _from_view(a, dil):
    return a if dil == 1 else a.reshape(SEQ, a.shape[1] // dil)


def _local_step(x, pos, target, w_in, w_out, w_gate, w_up, w_down, small):
    b_t = small["sgu_b_spatial"].T
    h, q, k, v, u, vs = _inproj_fwd(x, pos, small["pre_mix_norm"], w_in)
    sgu = _sgu_fwd(u, vs, small["sgu_ln_gain"], small["sgu_ln_bias"], small["sgu_w_spatial"], b_t)
    views = [tuple(_to_view(t, dil) for t in (q, k, v)) for dil in DILATIONS]
    o_list, l_list = [], []
    for dil, (qv, kv, vv) in zip(DILATIONS, views):
        o, l = _attn_fwd(qv, kv, vv, dil)
        o_list.append(_from_view(o, dil))
        l_list.append(_from_view(l, dil))
    attn, lse, mixed, y, x1 = _mix_out_fwd(o_list, l_list, sgu, x, w_out, small["attn_out_norm"],
                                           small["sgu_out_norm"], small["post_mix_norm"])
    h2, a, dg, dup, df, dx1, loss_cols, d_pre_ffn, d_post_ffn = _ffn_fwd_bwd(
        x1, target, w_gate, w_up, w_down, small["pre_ffn_norm"], small["post_ffn_norm"])

    full_tok = pl.BlockSpec((SEQ, D_MODEL), lambda s: (0, 0), pipeline_mode=pl.Buffered(1))
    ff_tok = pl.BlockSpec((1, SEQ, FF_S), lambda s: (s, 0, 0))
    gw_gate = _wgrad(h2, dg, full_tok, ff_tok, (D_MODEL, FF_S), "wgrad_gate")
    gw_up = _wgrad(h2, dup, full_tok, ff_tok, (D_MODEL, FF_S), "wgrad_up")
    gw_down = _wgrad(a, df, ff_tok, full_tok, (FF_S, D_MODEL), "wgrad_down")

    dy, dattn, delta, dsgu, d_post_mix, d_attn_norm, d_sgu_norm = _outproj_bwd(
        dx1, y, attn, sgu, w_out, small["post_mix_norm"], small["attn_out_norm"], small["sgu_out_norm"])
    gw_out = _wgrad(mixed, dy, pl.BlockSpec((SEQ, OUT_S), lambda s: (0, s)), full_tok, (OUT_S, D_MODEL), "wgrad_out")
    du, dvs_sgu, d_w_sp, d_b_sp, d_ln_gain, d_ln_bias = _sgu_bwd(
        u, vs, dsgu, small["sgu_ln_gain"], small["sgu_ln_bias"], small["sgu_w_spatial"], b_t)

    dqs, dks, dvs = [], [], []
    for dil, (qv, kv, vv) in zip(DILATIONS, views):
        dq, dk, dv = _attn_bwd(qv, kv, vv, _to_view(dattn, dil), _to_view(delta, dil), _to_view(lse, dil), dil)
        dqs.append(_from_view(dq, dil))
        dks.append(_from_view(dk, dil))
        dvs.append(_from_view(dv, dil))
    dproj, grad_x, d_pre_mix = _inproj_bwd(dqs, dks, dvs, du, dvs_sgu, pos, x, dx1, w_in, small["pre_mix_norm"])
    gw_in = _wgrad(h, dproj, full_tok, pl.BlockSpec((SEQ, IN_S), lambda s: (0, s)), (D_MODEL, IN_S), "wgrad_in")

    small_grads = {
        "pre_mix_norm": d_pre_mix, "sgu_ln_gain": d_ln_gain, "sgu_ln_bias": d_ln_bias, "sgu_w_spatial": d_w_sp,
        "sgu_b_spatial": d_b_sp, "attn_out_norm": d_attn_norm, "sgu_out_norm": d_sgu_norm,
        "post_mix_norm": d_post_mix, "pre_ffn_norm": d_pre_ffn, "post_ffn_norm": d_post_ffn,
    }
    return loss_cols, grad_x, (gw_in, gw_out, gw_gate, gw_up, gw_down), small_grads


def _coords():
    return lax.axis_index("x"), lax.axis_index("y"), lax.axis_index("c")


def _other_chips(x, y):
    return [(1 - x, y), (x, 1 - y), (1 - x, 1 - y)]


def _comm_call(body, name, n_in, out_shape, scratch_shapes):
    return pl.pallas_call(
        body, name=name, in_specs=[ANY] * n_in, out_specs=[ANY] * len(out_shape), out_shape=out_shape,
        scratch_shapes=scratch_shapes,
        compiler_params=pltpu.CompilerParams(has_side_effects=True),
    )


def _gather_weights(shards):
    n = len(shards)
    halves = [s.reshape(2, s.shape[0] // 2, s.shape[1]) for s in shards]

    def body(*refs):
        ins, outs = refs[:n], refs[n:2 * n]
        send_sems, recv_sems = refs[2 * n:]
        x, y, c = _coords()
        s_me = 2 * x + y
        chips = _other_chips(x, y)
        sibling = (x, y, 1 - c)

        def copy(k, w, shard, cc, to):
            src = ins[w].at[cc] if shard is None else outs[w].at[shard, cc]
            dst = outs[w].at[s_me if shard is None else shard, cc]
            return pltpu.make_async_remote_copy(src_ref=src, dst_ref=dst, send_sem=send_sems.at[k],
                                                recv_sem=recv_sems.at[k], device_id=to, device_id_type=MESH)

        first = [copy(j * n + w, w, None, c, (cx, cy, c)) for j, (cx, cy) in enumerate(chips) for w in range(n)]
        for cp in first:
            cp.start()
        passed = []
        for j, (cx, cy) in enumerate(chips):
            for w in range(n):
                copy(j * n + w, w, 2 * cx + cy, c, (x, y, c)).wait_recv()
                fw = copy((3 + j) * n + w, w, 2 * cx + cy, c, sibling)
                fw.start()
                passed.append(fw)
        for j, (cx, cy) in enumerate(chips):
            for w in range(n):
                copy((3 + j) * n + w, w, 2 * cx + cy, 1 - c, (x, y, c)).wait_recv()
        for cp in first + passed:
            cp.wait_send()

    out_shape = [_sds((N_SHARD,) + h.shape, h.dtype) for h in halves]
    scratch = [pltpu.SemaphoreType.DMA((6 * n,)), pltpu.SemaphoreType.DMA((6 * n,))]
    full = _comm_call(body, "comm_gather_weights", n, out_shape, scratch)(*halves)
    s_me = 2 * lax.axis_index("x") + lax.axis_index("y")
    full = [lax.dynamic_update_slice(f, h[None], (s_me, 0, 0, 0)) for f, h in zip(full, halves)]
    return [f.reshape((N_SHARD,) + s.shape) for f, s in zip(full, shards)]


def _rs_to_sibling(gws):
    n = len(gws)

    def body(*refs):
        ins, outs = refs[:n], refs[n:2 * n]
        send_sems, recv_sems = refs[2 * n:]
        x, y, c = _coords()
        copies = []
        for w in range(n):
            hw = gws[w].shape[1] // 2
            copies.append(pltpu.make_async_remote_copy(
                src_ref=ins[w].at[:, pl.ds((1 - c) * hw, hw), :], dst_ref=outs[w], send_sem=send_sems.at[w],
                recv_sem=recv_sems.at[w], device_id=(x, y, 1 - c), device_id_type=MESH))
        for cp in copies:
            cp.start()
        for cp in copies:
            cp.wait()

    out_shape = [_sds((N_SHARD, g.shape[1] // 2, g.shape[2]), g.dtype) for g in gws]
    scratch = [pltpu.SemaphoreType.DMA((n,)), pltpu.SemaphoreType.DMA((n,))]
    return _comm_call(body, "comm_rs_sibling", n, out_shape, scratch)(*gws)


def _rs_chip_sum(gw, recv, core):
    _, rows, cols = gw.shape
    hw = rows // 2

    def body(c_ref, g_ref, r_ref, o_ref):
        o_ref[...] = (g_ref[...] + r_ref[...]).astype(BF16)

    return pl.pallas_call(
        body, name="rs_chip_sum",
        grid_spec=pltpu.PrefetchScalarGridSpec(
            num_scalar_prefetch=1, grid=(N_SHARD,),
            in_specs=[pl.BlockSpec((1, hw, cols), lambda s, c_ref: (s, c_ref[0], 0)),
                      pl.BlockSpec((1, hw, cols), lambda s, c_ref: (s, 0, 0))],
            out_specs=pl.BlockSpec((1, hw, cols), lambda s, c_ref: (s, 0, 0))),
        out_shape=_sds((N_SHARD, hw, cols), BF16),
        compiler_params=_seq_params(),
    )(core, gw, recv)


def _rs_between_chips(pbs):
    n = len(pbs)

    def body(*refs):
        ins, outs = refs[:n], refs[n:2 * n]
        send_sems, recv_sems = refs[2 * n:]
        x, y, c = _coords()
        copies = []
        for j, (cx, cy) in enumerate(_other_chips(x, y)):
            for w in range(n):
                copies.append(pltpu.make_async_remote_copy(
                    src_ref=ins[w].at[2 * cx + cy], dst_ref=outs[w].at[j], send_sem=send_sems.at[j * n + w],
                    recv_sem=recv_sems.at[j * n + w], device_id=(cx, cy, c), device_id_type=MESH))
        for cp in copies:
            cp.start()
        for cp in copies:
            cp.wait()

    out_shape = [_sds((3,) + p.shape[1:], p.dtype) for p in pbs]
    scratch = [pltpu.SemaphoreType.DMA((3 * n,)), pltpu.SemaphoreType.DMA((3 * n,))]
    return _comm_call(body, "comm_rs_chips", n, out_shape, scratch)(*pbs)


def _rs_final_sum(gw, recv_sib, recv_chips, shard_core):
    _, rows, cols = gw.shape
    hw = rows // 2

    def body(sc_ref, g_ref, r_ref, rc_ref, o_ref):
        acc = g_ref[0] + r_ref[0]
        for j in range(3):
            acc = acc + rc_ref[j].astype(F32)
        o_ref[0] = acc

    return pl.pallas_call(
        body, name="rs_final_sum",
        grid_spec=pltpu.PrefetchScalarGridSpec(
            num_scalar_prefetch=1, grid=(1,),
            in_specs=[pl.BlockSpec((1, hw, cols), lambda i, sc: (sc[0], sc[1], 0)),
                      pl.BlockSpec((1, hw, cols), lambda i, sc: (sc[0], 0, 0)),
                      pl.BlockSpec((3, hw, cols), lambda i, sc: (0, 0, 0))],
            out_specs=pl.BlockSpec((1, hw, cols), lambda i, sc: (sc[1], 0, 0))),
        out_shape=_sds((2, hw, cols), F32),
        compiler_params=_seq_params(),
    )(shard_core, gw, recv_sib, recv_chips)


def _rs_join_halves(halves):
    n = len(halves)

    def body(*refs):
        bufs = refs[n:2 * n]
        send_sems, recv_sems = refs[2 * n:]
        x, y, c = _coords()
        remote = [pltpu.make_async_remote_copy(
            src_ref=bufs[w].at[c], dst_ref=bufs[w].at[c], send_sem=send_sems.at[w], recv_sem=recv_sems.at[w],
            device_id=(x, y, 1 - c), device_id_type=MESH) for w in range(n)]
        for cp in remote:
            cp.start()
        for w in range(n):
            remote[w].wait_send()
            pltpu.make_async_remote_copy(
                src_ref=bufs[w].at[c], dst_ref=bufs[w].at[1 - c], send_sem=send_sems.at[w],
                recv_sem=recv_sems.at[w], device_id=(x, y, c), device_id_type=MESH).wait_recv()

    joined = pl.pallas_call(
        body, name="comm_rs_join", in_specs=[ANY] * n, out_specs=[ANY] * n,
        out_shape=[_sds(h.shape, h.dtype) for h in halves], input_output_aliases={w: w for w in range(n)},
        scratch_shapes=[pltpu.SemaphoreType.DMA((n,)), pltpu.SemaphoreType.DMA((n,))],
        compiler_params=pltpu.CompilerParams(has_side_effects=True),
    )(*halves)
    return [j.reshape(2 * h.shape[1], h.shape[2]) for j, h in zip(joined, halves)]


def _allreduce_small(buf):
    rows, cols = buf.shape

    def body(in_ref, out_ref, slots, send_sems, recv_sems):
        x, y, c = _coords()
        me = 4 * x + 2 * y + c
        copies, peers = [], []
        for k in range(1, 8):
            px = 1 - x if (k >> 2) & 1 else x
            py = 1 - y if (k >> 1) & 1 else y
            pc = 1 - c if k & 1 else c
            peers.append(4 * px + 2 * py + pc)
            copies.append(pltpu.make_async_remote_copy(
                src_ref=in_ref, dst_ref=slots.at[me], send_sem=send_sems.at[k - 1], recv_sem=recv_sems.at[k - 1],
                device_id=(px, py, pc), device_id_type=MESH))
        for cp in copies:
            cp.start()
        slots[me] = in_ref[...]
        for k in range(7):
            pltpu.make_async_remote_copy(
                src_ref=in_ref, dst_ref=slots.at[peers[k]], send_sem=send_sems.at[k], recv_sem=recv_sems.at[k],
                device_id=(x, y, c), device_id_type=MESH).wait_recv()
        for cp in copies:
            cp.wait_send()
        acc = slots[0]
        for i in range(1, 8):
            acc = acc + slots[i]
        out_ref[...] = acc

    vmem = pl.BlockSpec(memory_space=pltpu.VMEM)
    return pl.pallas_call(
        body, name="comm_allreduce_small", in_specs=[vmem], out_specs=vmem, out_shape=_sds((rows, cols), F32),
        scratch_shapes=[pltpu.VMEM((8, rows, cols), F32), pltpu.SemaphoreType.DMA((7,)), pltpu.SemaphoreType.DMA((7,))],
        compiler_params=pltpu.CompilerParams(has_side_effects=True, vmem_limit_bytes=VMEM_LIMIT),
    )(buf)


def _adamw(w, g, m, v, block_rows, name, after=()):
    rows, cols = w.shape

    def body(w_ref, g_ref, m_ref, v_ref, *rest):
        d_ref, nm_ref, nv_ref = rest[len(after):]
        g = g_ref[...]
        m = ADAM_B1 * m_ref[...] + (1.0 - ADAM_B1) * g
        v = ADAM_B2 * v_ref[...] + (1.0 - ADAM_B2) * (g * g)
        m_hat = m / (1.0 - ADAM_B1 ** ADAM_STEP)
        v_hat = v / (1.0 - ADAM_B2 ** ADAM_STEP)
        d_ref[...] = -ADAM_LR * (m_hat / (jnp.sqrt(v_hat) + ADAM_EPS) + ADAM_WD * w_ref[...])
        nm_ref[...] = m
        nv_ref[...] = v

    spec = pl.BlockSpec((block_rows, cols), lambda i: (i, 0))
    return pl.pallas_call(
        body, name=name, grid=(rows // block_rows,), in_specs=[spec] * 4 + [ANY] * len(after), out_specs=[spec] * 3,
        out_shape=[_sds((rows, cols), F32)] * 3,
        compiler_params=_seq_params(),
    )(w, g, m, v, *after)


WEIGHTS = ("pre_mix_norm", "w_in", "sgu_ln_gain", "sgu_ln_bias", "sgu_w_spatial", "sgu_b_spatial", "attn_out_norm",
           "sgu_out_norm", "w_out", "post_mix_norm", "pre_ffn_norm", "w_gate", "w_up", "w_down", "post_ffn_norm")
BIG = ("w_in", "w_out", "w_gate", "w_up", "w_down")
BIG_ADAM_ROWS = {"w_in": 256, "w_out": 128, "w_gate": 256, "w_up": 256, "w_down": 352}
SMALL = ("pre_mix_norm", "post_mix_norm", "pre_ffn_norm", "post_ffn_norm", "sgu_ln_gain", "sgu_ln_bias",
         "attn_out_norm", "sgu_out_norm", "sgu_w_spatial", "sgu_b_spatial")


def _pack_small(d):
    flat = [d[n].reshape(-1) for n in SMALL]
    used = sum(f.shape[0] for f in flat)
    flat.append(jnp.zeros((SMALL_ROWS * 1024 - used,), F32))
    return jnp.concatenate(flat).reshape(SMALL_ROWS, 1024)


def _unpack_small(buf, shapes):
    flat = buf.reshape(-1)
    out, off = {}, 0
    for n in SMALL:
        size = int(np.prod(shapes[n]))
        out[n] = flat[off:off + size].reshape(shapes[n])
        off += size
    return out


def _kernel_unoverlapped(x, positions, pre_mix_norm, w_in, sgu_ln_gain, sgu_ln_bias, sgu_w_spatial, sgu_b_spatial, attn_out_norm, sgu_out_norm, w_out, post_mix_norm, pre_ffn_norm, w_gate, w_up, w_down, post_ffn_norm, loss_target, m_pre_mix_norm, m_w_in, m_sgu_ln_gain, m_sgu_ln_bias, m_sgu_w_spatial, m_sgu_b_spatial, m_attn_out_norm, m_sgu_out_norm, m_w_out, m_post_mix_norm, m_pre_ffn_norm, m_w_gate, m_w_up, m_w_down, m_post_ffn_norm, v_pre_mix_norm, v_w_in, v_sgu_ln_gain, v_sgu_ln_bias, v_sgu_w_spatial, v_sgu_b_spatial, v_attn_out_norm, v_sgu_out_norm, v_w_out, v_post_mix_norm, v_pre_ffn_norm, v_w_gate, v_w_up, v_w_down, v_post_ffn_norm):
    a = dict(locals())
    cx, cy, cc = _coords()
    core = jnp.stack([cc]).astype(jnp.int32)
    shard_core = jnp.stack([2 * cx + cy, cc]).astype(jnp.int32)

    full = _gather_weights([a[n][0].astype(BF16) for n in BIG])
    small = {n: (a[n][0] if a[n].ndim > 2 else a[n]) for n in SMALL}
    loss_cols, grad_x, gws, small_grads = _local_step(
        x[0], positions.reshape(SEQ, 1), loss_target[0], *full, small)
    loss = lax.psum(jnp.sum(loss_cols) * np.float32(0.5 / D_MODEL), ("x", "y", "c"))

    recv_sib = _rs_to_sibling(list(gws))
    chip_part = [_rs_chip_sum(g, r, core) for g, r in zip(gws, recv_sib)]
    recv_chips = _rs_between_chips(chip_part)
    halves = [_rs_final_sum(g, r, rc, shard_core) for g, r, rc in zip(gws, recv_sib, recv_chips)]
    big_grads = dict(zip(BIG, _rs_join_halves(halves)))

    shapes = {n: a[n].shape for n in SMALL}
    small_sum = _allreduce_small(_pack_small(small_grads))

    grads, deltas, new_m, new_v = {}, {}, {}, {}
    for n in BIG:
        grads[n] = big_grads[n][None]
        d, nm, nv = _adamw(a[n][0], big_grads[n], a["m_" + n][0], a["v_" + n][0], BIG_ADAM_ROWS[n], "adamw_" + n)
        deltas[n], new_m[n], new_v[n] = d[None], nm[None], nv[None]
    d, nm, nv = _adamw(_pack_small({n: a[n] for n in SMALL}), small_sum, _pack_small({n: a["m_" + n] for n in SMALL}),
                       _pack_small({n: a["v_" + n] for n in SMALL}), SMALL_ROWS, "adamw_small")
    grads.update(_unpack_small(small_sum, shapes))
    deltas.update(_unpack_small(d, shapes))
    new_m.update(_unpack_small(nm, shapes))
    new_v.update(_unpack_small(nv, shapes))
    return (loss, grad_x[None], *[grads[n] for n in WEIGHTS], *[deltas[n] for n in WEIGHTS],
            *[new_m[n] for n in WEIGHTS], *[new_v[n] for n in WEIGHTS])


def _remote(src, dst, send_sem, recv_sem, to):
    return pltpu.make_async_remote_copy(src_ref=src, dst_ref=dst, send_sem=send_sem, recv_sem=recv_sem,
                                        device_id=to, device_id_type=MESH)


def _halves(a):
    *lead, rows, cols = a.shape
    return a.reshape(*lead, 2, rows // 2, cols)


def _gather_ici(shards):
    n = len(shards)

    def desc(ins, outs, ss, rs, j, w, landed):
        x, y, c = _coords()
        cx, cy = _other_chips(x, y)[j]
        shard = 2 * cx + cy if landed else 2 * x + y
        return _remote(ins[w].at[c], outs[w].at[shard, c], ss.at[j * n + w], rs.at[j * n + w], (cx, cy, c))

    def start(ins, outs, ss, rs):
        for j in range(3):
            for w in range(n):
                desc(ins, outs, ss, rs, j, w, False).start()

    def finish(ins, outs, ss, rs):
        for j in range(3):
            for w in range(n):
                desc(ins, outs, ss, rs, j, w, True).wait_recv()
                desc(ins, outs, ss, rs, j, w, False).wait_send()

    return _Comm(shards, [_sds((N_SHARD,) + s.shape, s.dtype) for s in shards], 3 * n, start, finish)


def _gather_pass(fulls):
    n = len(fulls)

    def desc(bufs, ss, rs, j, w, landed):
        x, y, c = _coords()
        cx, cy = _other_chips(x, y)[j]
        shard = 2 * cx + cy
        return _remote(bufs[w].at[shard, c], bufs[w].at[shard, 1 - c if landed else c],
                       ss.at[j * n + w], rs.at[j * n + w], (x, y, 1 - c))

    def start(ins, outs, ss, rs):
        for j in range(3):
            for w in range(n):
                desc(outs, ss, rs, j, w, False).start()

    def finish(ins, outs, ss, rs):
        for j in range(3):
            for w in range(n):
                desc(outs, ss, rs, j, w, True).wait_recv()
                desc(outs, ss, rs, j, w, False).wait_send()

    return _Comm(fulls, [_sds(f.shape, f.dtype) for f in fulls], 3 * n, start, finish, aliased=True)


def _rs_sibling(gws):
    n = len(gws)

    def desc(ins, outs, ss, rs, w):
        x, y, c = _coords()
        return _remote(ins[w].at[:, 1 - c], outs[w], ss.at[w], rs.at[w], (x, y, 1 - c))

    def start(ins, outs, ss, rs):
        for w in range(n):
            desc(ins, outs, ss, rs, w).start()

    def finish(ins, outs, ss, rs):
        for w in range(n):
            desc(ins, outs, ss, rs, w).wait()

    out_shape = [_sds((N_SHARD, g.shape[1] // 2, g.shape[2]), g.dtype) for g in gws]
    return _Comm([_halves(g) for g in gws], out_shape, n, start, finish)


def _rs_chips(pbs):
    n = len(pbs)

    def desc(ins, outs, ss, rs, j, w):
        x, y, c = _coords()
        cx, cy = _other_chips(x, y)[j]
        return _remote(ins[w].at[2 * cx + cy], outs[w].at[j], ss.at[j * n + w], rs.at[j * n + w], (cx, cy, c))

    def start(ins, outs, ss, rs):
        for j in range(3):
            for w in range(n):
                desc(ins, outs, ss, rs, j, w).start()

    def finish(ins, outs, ss, rs):
        for j in range(3):
            for w in range(n):
                desc(ins, outs, ss, rs, j, w).wait()

    return _Comm(pbs, [_sds((3,) + p.shape[1:], p.dtype) for p in pbs], 3 * n, start, finish)


def _rs_join(halves):
    n = len(halves)

    def desc(bufs, ss, rs, w, landed):
        x, y, c = _coords()
        return _remote(bufs[w].at[c], bufs[w].at[1 - c if landed else c], ss.at[w], rs.at[w], (x, y, 1 - c))

    def start(ins, outs, ss, rs):
        for w in range(n):
            desc(outs, ss, rs, w, False).start()

    def finish(ins, outs, ss, rs):
        for w in range(n):
            desc(outs, ss, rs, w, True).wait_recv()
            desc(outs, ss, rs, w, False).wait_send()

    return _Comm(halves, [_sds(h.shape, h.dtype) for h in halves], n, start, finish, aliased=True)


def _small_exchange(buf):
    def desc(ins, outs, ss, rs, k, landed):
        x, y, c = _coords()
        px = 1 - x if (k >> 2) & 1 else x
        py = 1 - y if (k >> 1) & 1 else y
        pc = 1 - c if k & 1 else c
        slot = 4 * px + 2 * py + pc if landed else 4 * x + 2 * y + c
        return _remote(ins[0], outs[0].at[slot], ss.at[k - 1], rs.at[k - 1], (px, py, pc))

    def start(ins, outs, ss, rs):
        for k in range(1, 8):
            desc(ins, outs, ss, rs, k, False).start()

    def finish(ins, outs, ss, rs):
        for k in range(1, 8):
            desc(ins, outs, ss, rs, k, True).wait_recv()
            desc(ins, outs, ss, rs, k, False).wait_send()

    return _Comm([buf], [_sds((8,) + buf.shape, buf.dtype)], 7, start, finish)


HBM = pl.BlockSpec(memory_space=pltpu.HBM)
SEM = pl.BlockSpec(memory_space=pltpu.SEMAPHORE)
DATAFLOW = pltpu.SideEffectType.DATAFLOW_SIDE_EFFECTING


def _split_start(name, comm, lands, after):
    srcs = [pltpu.with_memory_space_constraint(s, pltpu.HBM) for s in comm.args]
    lands = [pltpu.with_memory_space_constraint(b, pltpu.HBM) for b in lands]
    ns, nb = len(srcs), len(lands)

    def body(*refs):
        send_sems, recv_sems = refs[ns + nb + 1], refs[ns + nb + 2]
        comm.start(refs[:ns], refs[ns:ns + nb], send_sems, recv_sems)
        refs[-1][...] = jnp.zeros_like(refs[-1])

    res = pl.pallas_call(
        body, name=name,
        out_shape=(pltpu.SemaphoreType.DMA((comm.n_sems,)), pltpu.SemaphoreType.DMA((comm.n_sems,)),
                   *[pltpu.HBM(b.shape, b.dtype) for b in srcs + lands], _sds((8, 128), F32)),
        in_specs=[HBM] * (ns + nb) + [ANY],
        out_specs=(SEM, SEM, *[HBM] * (ns + nb), pl.BlockSpec(memory_space=pltpu.VMEM)),
        input_output_aliases={i: 2 + i for i in range(ns + nb)},
        compiler_params=pltpu.CompilerParams(has_side_effects=DATAFLOW),
    )(*srcs, *lands, after)
    return res[0], res[1], list(res[2:2 + ns]), list(res[2 + ns:2 + ns + nb]), res[-1]


def _split_wait(name, comm, send_sems, recv_sems, srcs, lands, after):
    ns, nb = len(srcs), len(lands)

    def body(*refs):
        comm.finish(refs[:ns], refs[ns:ns + nb], refs[ns + nb], refs[ns + nb + 1])

    res = pl.pallas_call(
        body, name=name,
        out_shape=tuple(pltpu.HBM(b.shape, b.dtype) for b in srcs + lands),
        in_specs=[HBM] * (ns + nb) + [SEM, SEM, ANY], out_specs=tuple([HBM] * (ns + nb)),
        input_output_aliases={i: i for i in range(ns + nb)},
        compiler_params=pltpu.CompilerParams(has_side_effects=DATAFLOW),
    )(*srcs, *lands, send_sems, recv_sems, after)
    return list(res[ns:])


SMALL_EARLY = ("post_mix_norm", "pre_ffn_norm", "post_ffn_norm", "sgu_ln_gain", "sgu_ln_bias", "attn_out_norm",
               "sgu_out_norm", "sgu_w_spatial", "sgu_b_spatial")
SMALL_LATE = ("pre_mix_norm",)


def _pack(d, names, rows):
    flat = [d[n].reshape(-1) for n in names]
    used = sum(f.shape[0] for f in flat)
    flat.append(jnp.zeros((rows * 1024 - used,), F32))
    return jnp.concatenate(flat).reshape(rows, 1024)


def _unpack(buf, names, shapes):
    flat = buf.reshape(-1)
    out, off = {}, 0
    for n in names:
        size = int(np.prod(shapes[n]))
        out[n] = flat[off:off + size].reshape(shapes[n])
        off += size
    return out


def _comm_only(name, comms):
    return _pcall(lambda: None, name=name, grid=(1,), in_specs=[], out_specs=[], out_shape=[], args=(),
                  comms=comms)[1]


def _adam_math(w, g, m, v):
    m = ADAM_B1 * m + (1.0 - ADAM_B1) * g
    v = ADAM_B2 * v + (1.0 - ADAM_B2) * (g * g)
    m_hat = m / (1.0 - ADAM_B1 ** ADAM_STEP)
    v_hat = v / (1.0 - ADAM_B2 ** ADAM_STEP)
    return -ADAM_LR * (m_hat / (jnp.sqrt(v_hat) + ADAM_EPS) + ADAM_WD * w), m, v


def _adamw_small(own, slots, w, m, v, me):
    rows, cols = own.shape

    def body(me_ref, own_ref, slots_ref, w_ref, m_ref, v_ref, g_ref, d_ref, nm_ref, nv_ref):
        own_v = own_ref[...]
        g = jnp.where(me_ref[0] == 0, own_v, slots_ref[0])
        for i in range(1, 8):
            g = g + jnp.where(me_ref[0] == i, own_v, slots_ref[i])
        g_ref[...] = g
        d_ref[...], nm_ref[...], nv_ref[...] = _adam_math(w_ref[...], g, m_ref[...], v_ref[...])

    flat = pl.BlockSpec((rows, cols), lambda i, me_ref: (0, 0))
    return pl.pallas_call(
        body, name="adamw_small",
        grid_spec=pltpu.PrefetchScalarGridSpec(
            num_scalar_prefetch=1, grid=(1,),
            in_specs=[flat, pl.BlockSpec((8, rows, cols), lambda i, me_ref: (0, 0, 0)), flat, flat, flat],
            out_specs=[flat] * 4),
        out_shape=[_sds((rows, cols), F32)] * 4,
        compiler_params=_seq_params(),
    )(me, own, slots, w, m, v)


def kernel(x, positions, pre_mix_norm, w_in, sgu_ln_gain, sgu_ln_bias, sgu_w_spatial, sgu_b_spatial, attn_out_norm, sgu_out_norm, w_out, post_mix_norm, pre_ffn_norm, w_gate, w_up, w_down, post_ffn_norm, loss_target, m_pre_mix_norm, m_w_in, m_sgu_ln_gain, m_sgu_ln_bias, m_sgu_w_spatial, m_sgu_b_spatial, m_attn_out_norm, m_sgu_out_norm, m_w_out, m_post_mix_norm, m_pre_ffn_norm, m_w_gate, m_w_up, m_w_down, m_post_ffn_norm, v_pre_mix_norm, v_w_in, v_sgu_ln_gain, v_sgu_ln_bias, v_sgu_w_spatial, v_sgu_b_spatial, v_attn_out_norm, v_sgu_out_norm, v_w_out, v_post_mix_norm, v_pre_ffn_norm, v_w_gate, v_w_up, v_w_down, v_post_ffn_norm):
    a = dict(locals())
    cx, cy, cc = _coords()
    s_me = 2 * cx + cy
    core = jnp.stack([cc]).astype(jnp.int32)
    shard_core = jnp.stack([s_me, cc]).astype(jnp.int32)
    me = jnp.stack([4 * cx + 2 * cy + cc]).astype(jnp.int32)
    small = {n: (a[n][0] if a[n].ndim > 2 else a[n]) for n in SMALL}
    b_t = small["sgu_b_spatial"].T
    xs, pos, target = x[0], positions.reshape(SEQ, 1), loss_target[0]
    own = {n: _halves(a[n][0].astype(BF16)) for n in BIG}

    def with_own(full, n):
        full = lax.dynamic_update_slice(full, own[n][None], (s_me, 0, 0, 0))
        return full.reshape((N_SHARD,) + a[n].shape[1:])

    ffn = ("w_gate", "w_up", "w_down")
    g_in, g_out, g_ffn = _gather_ici([own["w_in"]]), _gather_ici([own["w_out"]]), _gather_ici([own[n] for n in ffn])
    s_in = _split_start("gather_in_start", g_in, [lax.empty(o.shape, o.dtype) for o in g_in.out_shape],
                        small["pre_mix_norm"])
    s_out = _split_start("gather_out_start", g_out, [lax.empty(o.shape, o.dtype) for o in g_out.out_shape], s_in[4])
    s_ffn = _split_start("gather_ffn_start", g_ffn, [lax.empty(o.shape, o.dtype) for o in g_ffn.out_shape], s_out[4])
    in_lands = _split_wait("gather_in_wait", g_in, *s_in[:4], s_ffn[4])
    ((in_lands,),) = _comm_only("comm_pass_in", [_gather_pass(in_lands)])
    w_in_f = with_own(in_lands, "w_in")
    h, q, k, v, u, vs = _inproj_fwd(xs, pos, small["pre_mix_norm"], w_in_f)
    views = [tuple(_to_view(t, dil) for t in (q, k, v)) for dil in DILATIONS]
    o_list, l_list = [], []
    for dil, (qv, kv, vv) in zip(DILATIONS, views):
        o, l = _attn_fwd(qv, kv, vv, dil)
        o_list.append(_from_view(o, dil))
        l_list.append(_from_view(l, dil))
    out_lands = _split_wait("gather_out_wait", g_out, *s_out[:4], l_list[-1])
    sgu, ((out_lands,),) = _sgu_fwd(u, vs, small["sgu_ln_gain"], small["sgu_ln_bias"], small["sgu_w_spatial"], b_t,
                                    comms=[_gather_pass(out_lands)])
    w_out_f = with_own(out_lands, "w_out")
    ffn_lands = _split_wait("gather_ffn_wait", g_ffn, *s_ffn[:4], sgu)
    (attn, lse, mixed, y, x1), (ffn_lands,) = _mix_out_fwd(
        o_list, l_list, sgu, xs, w_out_f, small["attn_out_norm"], small["sgu_out_norm"], small["post_mix_norm"],
        comms=[_gather_pass(ffn_lands)])
    w_gate_f, w_up_f, w_down_f = (with_own(f, n) for f, n in zip(ffn_lands, ffn))
    h2, act, dg, dup, df, dx1, loss_cols, d_pre_ffn, d_post_ffn = _ffn_fwd_bwd(
        x1, target, w_gate_f, w_up_f, w_down_f, small["pre_ffn_norm"], small["post_ffn_norm"])
    loss = lax.psum(jnp.sum(loss_cols) * np.float32(0.5 / D_MODEL), ("x", "y", "c"))

    full_tok = pl.BlockSpec((SEQ, D_MODEL), lambda s: (0, 0), pipeline_mode=pl.Buffered(1))
    ff_tok = pl.BlockSpec((1, SEQ, FF_S), lambda s: (s, 0, 0))
    gw = {}
    gw["w_gate"] = _wgrad(h2, dg, full_tok, ff_tok, (D_MODEL, FF_S), "wgrad_gate")
    gw["w_up"], ((sib_gate,),) = _wgrad(h2, dup, full_tok, ff_tok, (D_MODEL, FF_S), "wgrad_up",
                                        comms=[_rs_sibling([gw["w_gate"]])])
    gw["w_down"], ((sib_up,),) = _wgrad(act, df, ff_tok, full_tok, (FF_S, D_MODEL), "wgrad_down",
                                        comms=[_rs_sibling([gw["w_up"]])])
    (dy, dattn, delta, dsgu, d_post_mix, d_attn_norm, d_sgu_norm), ((sib_down,),) = _outproj_bwd(
        dx1, y, attn, sgu, w_out_f, small["post_mix_norm"], small["attn_out_norm"],
        small["sgu_out_norm"], comms=[_rs_sibling([gw["w_down"]])])
    sib = {"w_gate": sib_gate, "w_up": sib_up, "w_down": sib_down}
    part = {n: _rs_chip_sum(gw[n], sib[n], core) for n in ("w_gate", "w_up", "w_down")}
    gw["w_out"] = _wgrad(mixed, dy, pl.BlockSpec((SEQ, OUT_S), lambda s: (0, s)), full_tok, (OUT_S, D_MODEL),
                         "wgrad_out")
    ffn = ("w_gate", "w_up", "w_down")
    x_ffn = _rs_chips([part[n] for n in ffn])
    s_ffn = _split_start("rs_ffn_start", x_ffn, [lax.empty(o.shape, o.dtype) for o in x_ffn.out_shape],
                         part["w_down"])
    (du, dvs_sgu, d_w_sp, d_b_sp, d_ln_gain, d_ln_bias), ((sib["w_out"],),) = _sgu_bwd(
        u, vs, dsgu, small["sgu_ln_gain"], small["sgu_ln_bias"], small["sgu_w_spatial"], b_t,
        comms=[_rs_sibling([gw["w_out"]])], after=[s_ffn[4]])
    part["w_out"] = _rs_chip_sum(gw["w_out"], sib["w_out"], core)
    packed_early = _pack({
        "sgu_ln_gain": d_ln_gain, "sgu_ln_bias": d_ln_bias, "sgu_w_spatial": d_w_sp, "sgu_b_spatial": d_b_sp,
        "attn_out_norm": d_attn_norm, "sgu_out_norm": d_sgu_norm, "post_mix_norm": d_post_mix,
        "pre_ffn_norm": d_pre_ffn, "post_ffn_norm": d_post_ffn}, SMALL_EARLY, SMALL_ROWS)
    x_out, x_small = _rs_chips([part["w_out"]]), _small_exchange(packed_early)
    s_out = _split_start("rs_out_start", x_out, [lax.empty(o.shape, o.dtype) for o in x_out.out_shape], s_ffn[4])
    s_small = _split_start("small_early_start", x_small,
                           [lax.empty(o.shape, o.dtype) for o in x_small.out_shape], s_out[4])

    dqs, dks, dvs = [], [], []
    for dil, (qv, kv, vv) in zip(DILATIONS, views):
        dq, dk, dv = _attn_bwd(qv, kv, vv, _to_view(dattn, dil), _to_view(delta, dil), _to_view(lse, dil), dil,
                               after=[s_small[4]])
        dqs.append(_from_view(dq, dil))
        dks.append(_from_view(dk, dil))
        dvs.append(_from_view(dv, dil))
    half, far, joined = {}, {}, {}
    got = _split_wait("rs_ffn_wait", x_ffn, *s_ffn[:4], dvs[-1])
    for n, f in zip(ffn, got):
        far[n] = f
        half[n] = _rs_final_sum(gw[n], sib[n], f, shard_core)
    (dproj, grad_x, d_pre_mix), (got,) = _inproj_bwd(
        dqs, dks, dvs, du, dvs_sgu, pos, xs, dx1, w_in_f, small["pre_mix_norm"],
        comms=[_rs_join([half[n] for n in ffn])])
    joined.update(zip(ffn, got))
    (far["w_out"],) = _split_wait("rs_out_wait", x_out, *s_out[:4], grad_x)
    (slots_early,) = _split_wait("small_early_wait", x_small, *s_small[:4], far["w_out"])
    half["w_out"] = _rs_final_sum(gw["w_out"], sib["w_out"], far["w_out"], shard_core)
    packed_late = _pack({"pre_mix_norm": d_pre_mix}, SMALL_LATE, 8)
    gw["w_in"], ((joined["w_out"],), (slots_late,)) = _wgrad(
        h, dproj, full_tok, pl.BlockSpec((SEQ, IN_S), lambda s: (0, s)), (D_MODEL, IN_S), "wgrad_in",
        comms=[_rs_join([half["w_out"]]), _small_exchange(packed_late)])
    ((sib["w_in"],),) = _comm_only("comm_rs_sibling_in", [_rs_sibling([gw["w_in"]])])
    part["w_in"] = _rs_chip_sum(gw["w_in"], sib["w_in"], core)
    x_in = _rs_chips([part["w_in"]])
    s_in = _split_start("rs_in_start", x_in, [lax.empty(o.shape, o.dtype) for o in x_in.out_shape], part["w_in"])

    grads, deltas, new_m, new_v = {}, {}, {}, {}

    def update(n, after):
        g = joined[n].reshape(a[n].shape[1:])
        grads[n] = g[None]
        d, nm, nv = _adamw(a[n][0], g, a["m_" + n][0], a["v_" + n][0], BIG_ADAM_ROWS[n], "adamw_" + n, after)
        deltas[n], new_m[n], new_v[n] = d[None], nm[None], nv[None]

    for n in ("w_gate", "w_up", "w_down", "w_out"):
        update(n, [s_in[4]])
    (far["w_in"],) = _split_wait("rs_in_wait", x_in, *s_in[:4], new_v["w_out"])
    half["w_in"] = _rs_final_sum(gw["w_in"], sib["w_in"], far["w_in"], shard_core)
    ((joined["w_in"],),) = _comm_only("comm_rs_join_in", [_rs_join([half["w_in"]])])
    update("w_in", [])
    for names, rows, packed, slots in ((SMALL_EARLY, SMALL_ROWS, packed_early, slots_early),
                                       (SMALL_LATE, 8, packed_late, slots_late)):
        outs = _adamw_small(packed, slots, _pack(a, names, rows), _pack({n: a["m_" + n] for n in names}, names, rows),
                            _pack({n: a["v_" + n] for n in names}, names, rows), me)
        for dst, buf in zip((grads, deltas, new_m, new_v), outs):
            dst.update(_unpack(buf, names, {n: a[n].shape for n in names}))
    return (loss, grad_x[None], *[grads[n] for n in WEIGHTS], *[deltas[n] for n in WEIGHTS],
            *[new_m[n] for n in WEIGHTS], *[new_v[n] for n in WEIGHTS])
```

```python
import numpy as np
import jax
import jax.numpy as jnp
from jax import lax
from jax.experimental import pallas as pl
from jax.experimental.pallas import tpu as pltpu

F32 = jnp.float32
BF16 = jnp.bfloat16

SEQ = 2048
D_MODEL = 1024
HEAD_DIM = 64
ATTN_W = 512
SGU_W = 512
SGU_GROUPS = 8
CHUNK = 128
DILATIONS = (1, 4, 16)
N_SHARD = 4
IN_S = 640
OUT_S = 256
FF_S = 704
PROJ_W = N_SHARD * IN_S
RMS_EPS = 1e-6
LN_EPS = 1e-5
ROPE_THETA = 500000.0
ATTN_SCALE = 1.0 / np.sqrt(HEAD_DIM)
NEG = -1e30
TM = 256
VMEM_LIMIT = 56 * 1024 * 1024
SMALL_ROWS = 136

ADAM_LR = 0.001
ADAM_B1 = 0.9
ADAM_B2 = 0.999
ADAM_EPS = 1e-08
ADAM_WD = 0.01
ADAM_STEP = 10

MESH = pl.DeviceIdType.MESH
ANY = pl.BlockSpec(memory_space=pl.ANY)


def _dot(a, b):
    return jnp.dot(a, b, preferred_element_type=F32)


def _dot_nt(a, b):
    return lax.dot_general(a, b, (((1,), (1,)), ((), ())), preferred_element_type=F32)


def _dot_tn(a, b):
    return lax.dot_general(a, b, (((0,), (0,)), ((), ())), preferred_element_type=F32)


def _dot_exact(a, b):
    return jnp.dot(a, b, preferred_element_type=F32, precision=lax.Precision.HIGHEST)


def _rms_stats(x):
    r = lax.rsqrt(jnp.mean(x * x, axis=-1, keepdims=True) + RMS_EPS)
    return x * r, r


def _rms_bwd(xh, r, gain, dy):
    dxh = dy * gain
    dx = r * (dxh - xh * jnp.mean(dxh * xh, axis=-1, keepdims=True))
    return dx, jnp.sum(dy * xh, axis=0, keepdims=True)


_ERF_ALPHA = (-2.72614225801306e-10, 2.77068142495902e-08, -2.10102402082508e-06, -5.69250639462346e-05,
              -7.34990630326855e-04, -2.95459980854025e-03, -1.60960333262415e-02)
_ERF_BETA = (-1.45660718464996e-05, -2.13374055278905e-04, -1.68282697438203e-03, -7.37332916720468e-03,
             -1.42647390514189e-02)


def _erf(x):
    x = jnp.clip(x, -4.0, 4.0)
    x2 = x * x
    p = jnp.full_like(x, _ERF_ALPHA[0])
    for a in _ERF_ALPHA[1:]:
        p = p * x2 + a
    q = jnp.full_like(x, _ERF_BETA[0])
    for b in _ERF_BETA[1:]:
        q = q * x2 + b
    return x * p / q


def _gelu(x):
    return 0.5 * x * (1.0 + _erf(x * np.float32(1.0 / np.sqrt(2.0))))


def _gelu_grad(x):
    cdf = 0.5 * (1.0 + _erf(x * np.float32(1.0 / np.sqrt(2.0))))
    pdf = jnp.exp(-0.5 * x * x) * np.float32(1.0 / np.sqrt(2.0 * np.pi))
    return cdf + x * pdf


def _sigmoid(x):
    return 1.0 / (1.0 + jnp.exp(-x))


_INV_FREQ = tuple(float(np.float32(ROPE_THETA ** (-2.0 * j / 16.0))) for j in range(8))


def _rot_tables(pos):
    lane = lax.broadcasted_iota(jnp.int32, (1, 128), 1)
    d = lane & 63
    j = d & 7
    inv = jnp.zeros((1, 128), F32)
    for jj in range(8):
        inv = jnp.where(j == jj, _INV_FREQ[jj], inv)
    ang = pos.astype(F32) * inv
    c = jnp.cos(ang)
    s = jnp.sin(ang)
    cos_t = jnp.where(d < 16, c, 1.0)
    sin_a = jnp.where(d < 8, -s, 0.0)
    sin_b = jnp.where((d >= 8) & (d < 16), s, 0.0)
    return tuple(jnp.tile(t, (1, 4)) for t in (cos_t, sin_a, sin_b))


def _rope(x, tabs):
    cos_t, sin_a, sin_b = tabs
    return x * cos_t + pltpu.roll(x, 504, 1) * sin_a + pltpu.roll(x, 8, 1) * sin_b


def _rope_bwd(dy, tabs):
    cos_t, sin_a, sin_b = tabs
    return dy * cos_t + pltpu.roll(dy * sin_a, 8, 1) + pltpu.roll(dy * sin_b, 504, 1)


def _left_half():
    return lax.broadcasted_iota(jnp.int32, (CHUNK, CHUNK), 1) < HEAD_DIM


def _group_ones():
    lane = lax.broadcasted_iota(jnp.int32, (SGU_GROUPS, SGU_W), 1)
    row = lax.broadcasted_iota(jnp.int32, (SGU_GROUPS, SGU_W), 0)
    return ((lane >> 6) == row).astype(F32)


def _masked_spatial(w_ref):
    row = lax.broadcasted_iota(jnp.int32, (CHUNK, CHUNK), 0)
    col = lax.broadcasted_iota(jnp.int32, (CHUNK, CHUNK), 1)
    return [jnp.where(col <= row, w_ref[g], 0.0).astype(BF16) for g in range(SGU_GROUPS)]


def _sgu_core(u, vs, lg, lb, wm, bias_full):
    tm = u.shape[0]
    gu = _gelu(u)
    gv = _gelu(vs)
    mu = jnp.mean(gv, axis=-1, keepdims=True)
    xc = gv - mu
    rstd = lax.rsqrt(jnp.mean(xc * xc, axis=-1, keepdims=True) + LN_EPS)
    xh = xc * rstd
    vnb = (xh * lg + lb).astype(BF16)
    left = _left_half()
    rows = []
    for c in range(tm // CHUNK):
        pieces = []
        for p in range(4):
            vp = vnb[c * CHUNK:(c + 1) * CHUNK, p * 128:(p + 1) * 128]
            pieces.append(jnp.where(left, _dot(wm[2 * p], vp), _dot(wm[2 * p + 1], vp)))
        rows.append(jnp.concatenate(pieces, axis=1) + bias_full)
    mixed = jnp.concatenate(rows, axis=0)
    return gu, xh, rstd, vnb, mixed


def _resident(shape):
    n = len(shape)
    return pl.BlockSpec(shape, lambda *_: (0,) * n, pipeline_mode=pl.Buffered(1))


def _rows(ncol, tm=TM):
    return pl.BlockSpec((tm, ncol), lambda i: (i, 0))


def _rows3(nlead, ncol, tm=TM):
    return pl.BlockSpec((nlead, tm, ncol), lambda i: (0, i, 0))


def _acc(ncol, nrow=1):
    return pl.BlockSpec((nrow, ncol), lambda i: (0, 0))


def _seq_params():
    return pltpu.CompilerParams(dimension_semantics=("arbitrary",), vmem_limit_bytes=VMEM_LIMIT)


def _sds(shape, dtype):
    return jax.ShapeDtypeStruct(shape, dtype)


class _Comm:
    def __init__(self, args, out_shape, n_sems, start, finish, aliased=False):
        self.args, self.out_shape, self.n_sems = list(args), list(out_shape), n_sems
        self.start, self.finish, self.aliased = start, finish, aliased


def _pcall(body, *, name, grid, in_specs, out_specs, out_shape, args, scratch_shapes=(), comms=(), after=()):
    single = not isinstance(out_shape, (list, tuple))
    out_specs = [out_specs] if single else list(out_specs)
    out_shape = [out_shape] if single else list(out_shape)
    n_in, n_out, n_scr = len(in_specs), len(out_shape), len(scratch_shapes)
    c_args = [a for c in comms for a in c.args]
    c_outs = [o for c in comms for o in c.out_shape]
    aliases, ai, ao = {}, n_in, n_out
    for c in comms:
        if c.aliased:
            aliases.update({ai + k: ao + k for k in range(len(c.args))})
        ai += len(c.args)
        ao += len(c.out_shape)
    sems = [pltpu.SemaphoreType.DMA((c.n_sems,)) for c in comms for _ in range(2)]
    steps = grid[0]

    def wrapped(*refs):
        o0 = n_in + len(c_args) + len(after)
        s0 = o0 + n_out + len(c_outs)
        m_in, m_out, m_sem = refs[n_in:n_in + len(c_args)], refs[o0 + n_out:s0], refs[s0 + n_scr:]

        def each(phase):
            ii = oi = 0
            for k, c in enumerate(comms):
                getattr(c, phase)(m_in[ii:ii + len(c.args)], m_out[oi:oi + len(c.out_shape)],
                                  m_sem[2 * k], m_sem[2 * k + 1])
                ii += len(c.args)
                oi += len(c.out_shape)

        if comms:
            @pl.when(pl.program_id(0) == 0)
            def _():
                each("start")

        body(*refs[:n_in], *refs[o0:o0 + n_out], *refs[s0:s0 + n_scr])

        if comms:
            @pl.when(pl.program_id(0) == steps - 1)
            def _():
                each("finish")

    res = pl.pallas_call(
        wrapped, name=name, grid=grid,
        in_specs=list(in_specs) + [ANY] * (len(c_args) + len(after)), out_specs=out_specs + [ANY] * len(c_outs),
        out_shape=out_shape + c_outs, scratch_shapes=list(scratch_shapes) + sems,
        input_output_aliases=aliases, compiler_params=_seq_params(),
    )(*args, *c_args, *after)
    mine = res[0] if single else list(res[:n_out])
    if not comms:
        return mine
    theirs, oi = [], n_out
    for c in comms:
        theirs.append(list(res[oi:oi + len(c.out_shape)]))
        oi += len(c.out_shape)
    return mine, theirs


def _inproj_fwd(x, pos, g_pre, w_in, comms=()):
    def body(x_ref, pos_ref, g_ref, w_ref, h_ref, q_ref, k_ref, v_ref, u_ref, vs_ref):
        xh, _ = _rms_stats(x_ref[...])
        h = (xh * g_ref[...]).astype(BF16)
        h_ref[...] = h
        proj = jnp.concatenate([_dot(h, w_ref[s]) for s in range(N_SHARD)], axis=1)
        tabs = _rot_tables(pos_ref[...])
        q_ref[...] = (_rope(proj[:, 0:512], tabs) * np.float32(ATTN_SCALE)).astype(BF16)
        k_ref[...] = _rope(proj[:, 512:1024], tabs).astype(BF16)
        v_ref[...] = proj[:, 1024:1536].astype(BF16)
        u_ref[...] = proj[:, 1536:2048]
        vs_ref[...] = proj[:, 2048:2560]

    return _pcall(
        body, name="inproj_fwd", grid=(SEQ // TM,),
        in_specs=[_rows(D_MODEL), _rows(1), _resident((1, D_MODEL)), _resident((N_SHARD, D_MODEL, IN_S))],
        out_specs=[_rows(D_MODEL), _rows(512), _rows(512), _rows(512), _rows(512), _rows(512)],
        out_shape=[_sds((SEQ, D_MODEL), BF16), _sds((SEQ, 512), BF16), _sds((SEQ, 512), BF16),
                   _sds((SEQ, 512), BF16), _sds((SEQ, 512), F32), _sds((SEQ, 512), F32)],
        args=(x, pos, g_pre, w_in), comms=comms)


def _sgu_fwd(u, vs, lg, lb, w_sp, b_t, comms=()):
    def body(u_ref, vs_ref, lg_ref, lb_ref, w_ref, bt_ref, out_ref):
        wm = _masked_spatial(w_ref)
        bias_full = _dot_exact(bt_ref[...], _group_ones())
        gu, _, _, _, mixed = _sgu_core(u_ref[...], vs_ref[...], lg_ref[...], lb_ref[...], wm, bias_full)
        out_ref[...] = gu * mixed

    return _pcall(
        body, name="sgu_fwd", grid=(SEQ // TM,),
        in_specs=[_rows(SGU_W), _rows(SGU_W), _resident((1, SGU_W)), _resident((1, SGU_W)),
                  _resident((SGU_GROUPS, CHUNK, CHUNK)), _resident((CHUNK, SGU_GROUPS))],
        out_specs=_rows(SGU_W),
        out_shape=_sds((SEQ, SGU_W), F32),
        args=(u, vs, lg, lb, w_sp, b_t), comms=comms)


def _block_masks():
    row = lax.broadcasted_iota(jnp.int32, (CHUNK, CHUNK), 0)
    col = lax.broadcasted_iota(jnp.int32, (CHUNK, CHUNK), 1)
    return col <= row, col >= row


def _attn_fwd(qv, kv, vv, dil, comms=()):
    seg = SEQ // dil
    nblk = seg // CHUNK

    def body(q_ref, k_ref, v_ref, o_ref, l_ref):
        left = _left_half()
        m_cur, m_prev = _block_masks()
        zero = jnp.zeros((CHUNK, CHUNK), BF16)
        ones = (jnp.where(left, 1.0, 0.0).astype(BF16), jnp.where(left, 0.0, 1.0).astype(BF16))

        def blk(b, carry):
            r0 = pl.multiple_of(b * CHUNK, CHUNK)
            rp = pl.multiple_of(jnp.maximum(b - 1, 0) * CHUNK, CHUNK)
            prev_ok = m_prev & (b > 0)
            sides = tuple(enumerate((left, ~left)))
            tiles, scores = [], []
            for hp in range(4):
                ls = slice(hp * 128, (hp + 1) * 128)
                qp = q_ref[pl.ds(r0, CHUNK), ls]
                kc = k_ref[pl.ds(r0, CHUNK), ls]
                kp = k_ref[pl.ds(rp, CHUNK), ls] if nblk > 1 else None
                tiles.append((ls, v_ref[pl.ds(r0, CHUNK), ls], v_ref[pl.ds(rp, CHUNK), ls] if nblk > 1 else None))
                for _, hm in sides:
                    qh = jnp.where(hm, qp, zero)
                    sc = jnp.where(m_cur, _dot_nt(qh, kc), NEG)
                    sp = jnp.where(prev_ok, _dot_nt(qh, kp), NEG) if nblk > 1 else None
                    scores.append((sc, sp))
            probs = []
            for sc, sp in scores:
                if nblk > 1:
                    m = jnp.max(jnp.maximum(sc, sp), axis=-1, keepdims=True)
                    pc = jnp.exp(sc - m)
                    pp = jnp.exp(sp - m)
                    probs.append((m, pc.astype(BF16), pp.astype(BF16), (pc + pp).astype(BF16)))
                else:
                    m = jnp.max(sc, axis=-1, keepdims=True)
                    pc = jnp.exp(sc - m).astype(BF16)
                    probs.append((m, pc, None, pc))
            for hp, (ls, vc, vp) in enumerate(tiles):
                acc = jnp.zeros((CHUNK, CHUNK), F32)
                den = jnp.zeros((CHUNK, CHUNK), F32)
                for side, hm in sides:
                    _, pc, pp, psum = probs[2 * hp + side]
                    acc = acc + _dot(pc, jnp.where(hm, vc, zero))
                    if nblk > 1:
                        acc = acc + _dot(pp, jnp.where(hm, vp, zero))
                    den = den + _dot(psum, ones[side])
                o_ref[pl.ds(r0, CHUNK), ls] = acc / den
                l_ref[pl.ds(r0, CHUNK), ls] = jnp.where(left, probs[2 * hp][0], probs[2 * hp + 1][0]) + jnp.log(den)
            return carry

        lax.fori_loop(0, nblk, blk, 0)

    spec = pl.BlockSpec((seg, ATTN_W), lambda r: (0, r))
    return _pcall(
        body, name=f"attn_fwd_d{dil}", grid=(dil,),
        in_specs=[spec, spec, spec], out_specs=[spec, spec],
        out_shape=[_sds((seg, dil * ATTN_W), F32), _sds((seg, dil * ATTN_W), F32)],
        args=(qv, kv, vv), comms=comms)


def _mix_out_fwd(o_list, l_list, sgu, x, w_out, g_attn, g_sgu, g_post, comms=()):
    def body(o1, o2, o3, l1, l2, l3, sgu_ref, x_ref, w_ref, ga_ref, gs_ref, gp_ref,
             attn_ref, lse_ref, mixed_ref, y_ref, x1_ref):
        ls = [l1[...], l2[...], l3[...]]
        m = jnp.maximum(jnp.maximum(ls[0], ls[1]), ls[2])
        es = [jnp.exp(l - m) for l in ls]
        den = es[0] + es[1] + es[2]
        attn = (es[0] * o1[...] + es[1] * o2[...] + es[2] * o3[...]) / den
        attn_ref[...] = attn
        lse_ref[...] = m + jnp.log(den)
        ah, _ = _rms_stats(attn)
        sh, _ = _rms_stats(sgu_ref[...])
        mixed = jnp.concatenate([ah * ga_ref[...], sh * gs_ref[...]], axis=1).astype(BF16)
        mixed_ref[...] = mixed
        y = _dot(mixed[:, 0:OUT_S], w_ref[0])
        for s in range(1, N_SHARD):
            y = y + _dot(mixed[:, s * OUT_S:(s + 1) * OUT_S], w_ref[s])
        y_ref[...] = y
        yh, _ = _rms_stats(y)
        x1_ref[...] = x_ref[...] + yh * gp_ref[...]

    return _pcall(
        body, name="mix_out_fwd", grid=(SEQ // TM,),
        in_specs=[_rows(512)] * 7 + [_rows(D_MODEL), _resident((N_SHARD, OUT_S, D_MODEL)),
                                    _resident((1, 512)), _resident((1, 512)), _resident((1, D_MODEL))],
        out_specs=[_rows(512), _rows(512), _rows(D_MODEL), _rows(D_MODEL), _rows(D_MODEL)],
        out_shape=[_sds((SEQ, 512), F32), _sds((SEQ, 512), F32), _sds((SEQ, D_MODEL), BF16),
                   _sds((SEQ, D_MODEL), F32), _sds((SEQ, D_MODEL), F32)],
        args=(*o_list, *l_list, sgu, x, w_out, g_attn, g_sgu, g_post), comms=comms)


def _ffn_fwd_bwd(x1, target, w_gate, w_up, w_down, g_pre, g_post, comms=()):
    def body(x1_ref, t_ref, wg_ref, wu_ref, wd_ref, gpf_ref, gpo_ref,
             h2_ref, a_ref, dg_ref, dup_ref, df_ref, dx1_ref, loss_ref, dgpf_ref, dgpo_ref, g_scr, up_scr):
        @pl.when(pl.program_id(0) == 0)
        def _():
            loss_ref[...] = jnp.zeros_like(loss_ref)
            dgpf_ref[...] = jnp.zeros_like(dgpf_ref)
            dgpo_ref[...] = jnp.zeros_like(dgpo_ref)

        x1 = x1_ref[...]
        gpf = gpf_ref[...]
        gpo = gpo_ref[...]
        xh, r = _rms_stats(x1)
        h2 = (xh * gpf).astype(BF16)
        h2_ref[...] = h2
        f = jnp.zeros((TM, D_MODEL), F32)
        for s in range(N_SHARD):
            g = _dot_nt(h2, wg_ref[s])
            up = _dot_nt(h2, wu_ref[s])
            g_scr[s] = g
            up_scr[s] = up
            a = (g * _sigmoid(g) * up).astype(BF16)
            a_ref[s] = a
            f = f + _dot(a, wd_ref[s])
        fh, rf = _rms_stats(f)
        diff = x1 + fh * gpo - t_ref[...]
        loss_ref[...] += jnp.sum(diff * diff, axis=0, keepdims=True)
        dout = diff * np.float32(1.0 / D_MODEL)
        df, dgpo = _rms_bwd(fh, rf, gpo, dout)
        dgpo_ref[...] += dgpo
        dfb = df.astype(BF16)
        df_ref[...] = dfb
        dh2 = jnp.zeros((TM, D_MODEL), F32)
        for s in range(N_SHARD):
            da = _dot_nt(dfb, wd_ref[s])
            g = g_scr[s]
            up = up_scr[s]
            sg = _sigmoid(g)
            dup = (da * (g * sg)).astype(BF16)
            dg = (da * up * (sg * (1.0 + g * (1.0 - sg)))).astype(BF16)
            dg_ref[s] = dg
            dup_ref[s] = dup
            dh2 = dh2 + _dot(dg, wg_ref[s]) + _dot(dup, wu_ref[s])
        dx, dgpf = _rms_bwd(xh, r, gpf, dh2)
        dgpf_ref[...] += dgpf
        dx1_ref[...] = dout + dx

    return _pcall(
        body, name="ffn_fwd_bwd", grid=(SEQ // TM,),
        in_specs=[_rows(D_MODEL), _rows(D_MODEL), _resident((N_SHARD, FF_S, D_MODEL)),
                  _resident((N_SHARD, FF_S, D_MODEL)), _resident((N_SHARD, FF_S, D_MODEL)),
                  _resident((1, D_MODEL)), _resident((1, D_MODEL))],
        out_specs=[_rows(D_MODEL), _rows3(N_SHARD, FF_S), _rows3(N_SHARD, FF_S), _rows3(N_SHARD, FF_S),
                   _rows(D_MODEL), _rows(D_MODEL), _acc(D_MODEL), _acc(D_MODEL), _acc(D_MODEL)],
        out_shape=[_sds((SEQ, D_MODEL), BF16), _sds((N_SHARD, SEQ, FF_S), BF16), _sds((N_SHARD, SEQ, FF_S), BF16),
                   _sds((N_SHARD, SEQ, FF_S), BF16), _sds((SEQ, D_MODEL), BF16), _sds((SEQ, D_MODEL), F32),
                   _sds((1, D_MODEL), F32), _sds((1, D_MODEL), F32), _sds((1, D_MODEL), F32)],
        scratch_shapes=[pltpu.VMEM((N_SHARD, TM, FF_S), F32), pltpu.VMEM((N_SHARD, TM, FF_S), F32)],
        args=(x1, target, w_gate, w_up, w_down, g_pre, g_post), comms=comms)


def _wgrad(a, b, a_spec, b_spec, out_block, name, comms=()):
    def body(a_ref, b_ref, o_ref):
        av = a_ref[0] if len(a_ref.shape) == 3 else a_ref[...]
        bv = b_ref[0] if len(b_ref.shape) == 3 else b_ref[...]
        o_ref[0] = _dot_tn(av, bv)

    return _pcall(
        body, name=name, grid=(N_SHARD,),
        in_specs=[a_spec, b_spec],
        out_specs=pl.BlockSpec((1,) + out_block, lambda s: (s, 0, 0)),
        out_shape=_sds((N_SHARD,) + out_block, F32),
        args=(a, b), comms=comms)


def _outproj_bwd(dx1, y, attn, sgu, w_out, g_post, g_attn, g_sgu, comms=()):
    def body(dx1_ref, y_ref, attn_ref, sgu_ref, w_ref, gp_ref, ga_ref, gs_ref,
             dy_ref, dattn_ref, delta_ref, dsgu_ref, dgp_ref, dga_ref, dgs_ref):
        @pl.when(pl.program_id(0) == 0)
        def _():
            dgp_ref[...] = jnp.zeros_like(dgp_ref)
            dga_ref[...] = jnp.zeros_like(dga_ref)
            dgs_ref[...] = jnp.zeros_like(dgs_ref)

        yh, ry = _rms_stats(y_ref[...])
        dy, dgp = _rms_bwd(yh, ry, gp_ref[...], dx1_ref[...])
        dgp_ref[...] += dgp
        dyb = dy.astype(BF16)
        dy_ref[...] = dyb
        dmixed = jnp.concatenate([_dot_nt(dyb, w_ref[s]) for s in range(N_SHARD)], axis=1)
        attn = attn_ref[...]
        ah, ra = _rms_stats(attn)
        dattn, dga = _rms_bwd(ah, ra, ga_ref[...], dmixed[:, 0:512])
        dga_ref[...] += dga
        sh, rs = _rms_stats(sgu_ref[...])
        dsgu, dgs = _rms_bwd(sh, rs, gs_ref[...], dmixed[:, 512:1024])
        dgs_ref[...] += dgs
        dattn_ref[...] = dattn.astype(BF16)
        dsgu_ref[...] = dsgu
        la = lax.broadcasted_iota(jnp.int32, (ATTN_W, ATTN_W), 0) >> 6
        lb = lax.broadcasted_iota(jnp.int32, (ATTN_W, ATTN_W), 1) >> 6
        delta_ref[...] = _dot_exact(dattn * attn, (la == lb).astype(F32))

    return _pcall(
        body, name="outproj_bwd", grid=(SEQ // TM,),
        in_specs=[_rows(D_MODEL), _rows(D_MODEL), _rows(512), _rows(512), _resident((N_SHARD, OUT_S, D_MODEL)),
                  _resident((1, D_MODEL)), _resident((1, 512)), _resident((1, 512))],
        out_specs=[_rows(D_MODEL), _rows(512), _rows(512), _rows(512), _acc(D_MODEL), _acc(512), _acc(512)],
        out_shape=[_sds((SEQ, D_MODEL), BF16), _sds((SEQ, 512), BF16), _sds((SEQ, 512), F32), _sds((SEQ, 512), F32),
                   _sds((1, D_MODEL), F32), _sds((1, 512), F32), _sds((1, 512), F32)],
        args=(dx1, y, attn, sgu, w_out, g_post, g_attn, g_sgu), comms=comms)


def _sgu_bwd(u, vs, dsgu, lg, lb, w_sp, b_t, comms=(), after=()):
    nsteps = SEQ // TM

    def body(u_ref, vs_ref, ds_ref, lg_ref, lb_ref, w_ref, bt_ref,
             du_ref, dvs_ref, dw_ref, db_ref, dlg_ref, dlb_ref, dbias_scr):
        i = pl.program_id(0)

        @pl.when(i == 0)
        def _():
            dw_ref[...] = jnp.zeros_like(dw_ref)
            dlg_ref[...] = jnp.zeros_like(dlg_ref)
            dlb_ref[...] = jnp.zeros_like(dlb_ref)
            dbias_scr[...] = jnp.zeros_like(dbias_scr)

        wm = _masked_spatial(w_ref)
        ones_g = _group_ones()
        bias_full = _dot_exact(bt_ref[...], ones_g)
        u = u_ref[...]
        vs = vs_ref[...]
        lg = lg_ref[...]
        gu, xh, rstd, vnb, mixed = _sgu_core(u, vs, lg, lb_ref[...], wm, bias_full)
        dsgu = ds_ref[...]
        du_ref[...] = (dsgu * mixed * _gelu_grad(u)).astype(BF16)
        dmixed = dsgu * gu
        left = _left_half()
        dvn_rows = []
        for c in range(TM // CHUNK):
            rs = slice(c * CHUNK, (c + 1) * CHUNK)
            dm_c = dmixed[rs, :]
            dbias_scr[...] += dm_c
            pieces = []
            for p in range(4):
                ls = slice(p * 128, (p + 1) * 128)
                dmp = dm_c[:, ls]
                vp = vnb[rs, ls]
                dmb = dmp.astype(BF16)
                zero = jnp.zeros_like(dmb)
                dw_ref[2 * p] += _dot_nt(jnp.where(left, dmb, zero), vp)
                dw_ref[2 * p + 1] += _dot_nt(jnp.where(left, zero, dmb), vp)
                pieces.append(jnp.where(left, _dot_tn(wm[2 * p], dmb), _dot_tn(wm[2 * p + 1], dmb)))
            dvn_rows.append(jnp.concatenate(pieces, axis=1))
        dvn = jnp.concatenate(dvn_rows, axis=0)
        dlg_ref[...] += jnp.sum(dvn * xh, axis=0, keepdims=True)
        dlb_ref[...] += jnp.sum(dvn, axis=0, keepdims=True)
        dxh = dvn * lg
        dgv = rstd * (dxh - jnp.mean(dxh, axis=-1, keepdims=True) - xh * jnp.mean(dxh * xh, axis=-1, keepdims=True))
        dvs_ref[...] = (dgv * _gelu_grad(vs)).astype(BF16)

        @pl.when(i == nsteps - 1)
        def _():
            row = lax.broadcasted_iota(jnp.int32, (CHUNK, CHUNK), 0)
            col = lax.broadcasted_iota(jnp.int32, (CHUNK, CHUNK), 1)
            for g in range(SGU_GROUPS):
                dw_ref[g] = jnp.where(col <= row, dw_ref[g], 0.0)
            db_ref[...] = lax.dot_general(ones_g, dbias_scr[...], (((1,), (1,)), ((), ())),
                                          preferred_element_type=F32, precision=lax.Precision.HIGHEST)

    return _pcall(
        body, name="sgu_bwd", grid=(nsteps,),
        in_specs=[_rows(SGU_W), _rows(SGU_W), _rows(SGU_W), _resident((1, SGU_W)), _resident((1, SGU_W)),
                  _resident((SGU_GROUPS, CHUNK, CHUNK)), _resident((CHUNK, SGU_GROUPS))],
        out_specs=[_rows(SGU_W), _rows(SGU_W), pl.BlockSpec((SGU_GROUPS, CHUNK, CHUNK), lambda i: (0, 0, 0)),
                   _acc(CHUNK, SGU_GROUPS), _acc(SGU_W), _acc(SGU_W)],
        out_shape=[_sds((SEQ, SGU_W), BF16), _sds((SEQ, SGU_W), BF16), _sds((SGU_GROUPS, CHUNK, CHUNK), F32),
                   _sds((SGU_GROUPS, CHUNK), F32), _sds((1, SGU_W), F32), _sds((1, SGU_W), F32)],
        scratch_shapes=[pltpu.VMEM((CHUNK, SGU_W), F32)],
        args=(u, vs, dsgu, lg, lb, w_sp, b_t), comms=comms, after=after)


def _attn_bwd(qv, kv, vv, dov, deltav, lsev, dil, comms=(), after=()):
    seg = SEQ // dil
    nblk = seg // CHUNK

    def body(q_ref, k_ref, v_ref, do_ref, dl_ref, lse_ref, dq_ref, dk_ref, dv_ref):
        left = _left_half()
        m_cur, m_prev = _block_masks()

        def blk(b, carry):
            r0 = pl.multiple_of(b * CHUNK, CHUNK)
            rp = pl.multiple_of(jnp.maximum(b - 1, 0) * CHUNK, CHUNK)
            prev_ok = m_prev & (b > 0)
            sides = tuple(enumerate((left, ~left)))
            zero = jnp.zeros((CHUNK, CHUNK), BF16)
            tiles, firsts = [], []
            for hp in range(4):
                ls = slice(hp * 128, (hp + 1) * 128)
                qp = q_ref[pl.ds(r0, CHUNK), ls]
                kc = k_ref[pl.ds(r0, CHUNK), ls]
                vc = v_ref[pl.ds(r0, CHUNK), ls]
                dop = do_ref[pl.ds(r0, CHUNK), ls]
                kp = k_ref[pl.ds(rp, CHUNK), ls] if nblk > 1 else None
                vp = v_ref[pl.ds(rp, CHUNK), ls] if nblk > 1 else None
                tiles.append((ls, kc, kp))
                for side, hm in sides:
                    qh = jnp.where(hm, qp, zero)
                    doh = jnp.where(hm, dop, zero)
                    cur = (_dot_nt(qh, kc), _dot_nt(doh, vc))
                    prev = (_dot_nt(qh, kp), _dot_nt(doh, vp)) if nblk > 1 else None
                    firsts.append((qh, doh, cur, prev))
            seconds = []
            for i, (qh, doh, cur, prev) in enumerate(firsts):
                hp, c0 = i // 2, (i % 2) * HEAD_DIM
                ls = slice(hp * 128, (hp + 1) * 128)
                lse_h = lse_ref[pl.ds(r0, CHUNK), ls][:, c0:c0 + 1]
                dl_h = dl_ref[pl.ds(r0, CHUNK), ls][:, c0:c0 + 1]
                pc = jnp.exp(jnp.where(m_cur, cur[0] - lse_h, NEG))
                out = [pc.astype(BF16), (pc * (cur[1] - dl_h)).astype(BF16), None, None]
                if nblk > 1:
                    pp = jnp.exp(jnp.where(prev_ok, prev[0] - lse_h, NEG))
                    out[2:] = [pp.astype(BF16), (pp * (prev[1] - dl_h)).astype(BF16)]
                seconds.append(out)
            for hp, (ls, kc, kp) in enumerate(tiles):
                dq = jnp.zeros((CHUNK, CHUNK), F32)
                dkc = jnp.zeros((CHUNK, CHUNK), F32)
                dvc = jnp.zeros((CHUNK, CHUNK), F32)
                dkp = jnp.zeros((CHUNK, CHUNK), F32)
                dvp = jnp.zeros((CHUNK, CHUNK), F32)
                for side, hm in sides:
                    qh, doh, _, _ = firsts[2 * hp + side]
                    pcb, dsc, ppb, dsp = seconds[2 * hp + side]
                    dq = dq + _dot(dsc, jnp.where(hm, kc, zero))
                    dkc = dkc + _dot_tn(dsc, qh)
                    dvc = dvc + _dot_tn(pcb, doh)
                    if nblk > 1:
                        dq = dq + _dot(dsp, jnp.where(hm, kp, zero))
                        dkp = dkp + _dot_tn(dsp, qh)
                        dvp = dvp + _dot_tn(ppb, doh)
                dq_ref[pl.ds(r0, CHUNK), ls] = dq
                dk_ref[pl.ds(r0, CHUNK), ls] = dkc
                dv_ref[pl.ds(r0, CHUNK), ls] = dvc
                if nblk > 1:
                    @pl.when(b > 0)
                    def _():
                        dk_ref[pl.ds(rp, CHUNK), ls] += dkp
                        dv_ref[pl.ds(rp, CHUNK), ls] += dvp
            return carry

        lax.fori_loop(0, nblk, blk, 0)

    spec = pl.BlockSpec((seg, ATTN_W), lambda r: (0, r))
    return _pcall(
        body, name=f"attn_bwd_d{dil}", grid=(dil,),
        in_specs=[spec] * 6, out_specs=[spec] * 3,
        out_shape=[_sds((seg, dil * ATTN_W), F32)] * 3,
        args=(qv, kv, vv, dov, deltav, lsev), comms=comms, after=after)


def _inproj_bwd(dqs, dks, dvs, du, dvs_sgu, pos, x, dx1, w_in, g_pre, comms=()):
    def body(dq1, dq2, dq3, dk1, dk2, dk3, dv1, dv2, dv3, du_ref, dvs_ref, pos_ref, x_ref, dx1_ref, w_ref, g_ref,
             dproj_ref, gx_ref, dg_ref):
        @pl.when(pl.program_id(0) == 0)
        def _():
            dg_ref[...] = jnp.zeros_like(dg_ref)

        tabs = _rot_tables(pos_ref[...])
        dproj_ref[:, 0:512] = _rope_bwd((dq1[...] + dq2[...] + dq3[...]) * np.float32(ATTN_SCALE), tabs).astype(BF16)
        dproj_ref[:, 512:1024] = _rope_bwd(dk1[...] + dk2[...] + dk3[...], tabs).astype(BF16)
        dproj_ref[:, 1024:1536] = (dv1[...] + dv2[...] + dv3[...]).astype(BF16)
        dproj_ref[:, 1536:2048] = du_ref[...]
        dproj_ref[:, 2048:2560] = dvs_ref[...]
        dh = jnp.zeros((TM, D_MODEL), F32)
        for s in range(N_SHARD):
            dh = dh + _dot_nt(dproj_ref[:, s * IN_S:(s + 1) * IN_S], w_ref[s])
        g = g_ref[...]
        xh, r = _rms_stats(x_ref[...])
        dx, dg = _rms_bwd(xh, r, g, dh)
        dg_ref[...] += dg
        gx_ref[...] = dx1_ref[...] + dx

    return _pcall(
        body, name="inproj_bwd", grid=(SEQ // TM,),
        in_specs=[_rows(512)] * 11 + [_rows(1), _rows(D_MODEL), _rows(D_MODEL),
                                     _resident((N_SHARD, D_MODEL, IN_S)), _resident((1, D_MODEL))],
        out_specs=[_rows(PROJ_W), _rows(D_MODEL), _acc(D_MODEL)],
        out_shape=[_sds((SEQ, PROJ_W), BF16), _sds((SEQ, D_MODEL), F32), _sds((1, D_MODEL), F32)],
        args=(*dqs, *dks, *dvs, du, dvs_sgu, pos, x, dx1, w_in, g_pre), comms=comms)


def _to_view(a, dil):
    return a if dil == 1 else a.reshape(SEQ // dil, dil * a.shape[1])


def _from_view(a, dil):
    return a if dil == 1 else a.reshape(SEQ, a.shape[1] // dil)


def _local_step(x, pos, target, w_in, w_out, w_gate, w_up, w_down, small):
    b_t = small["sgu_b_spatial"].T
    h, q, k, v, u, vs = _inproj_fwd(x, pos, small["pre_mix_norm"], w_in)
    sgu = _sgu_fwd(u, vs, small["sgu_ln_gain"], small["sgu_ln_bias"], small["sgu_w_spatial"], b_t)
    views = [tuple(_to_view(t, dil) for t in (q, k, v)) for dil in DILATIONS]
    o_list, l_list = [], []
    for dil, (qv, kv, vv) in zip(DILATIONS, views):
        o, l = _attn_fwd(qv, kv, vv, dil)
        o_list.append(_from_view(o, dil))
        l_list.append(_from_view(l, dil))
    attn, lse, mixed, y, x1 = _mix_out_fwd(o_list, l_list, sgu, x, w_out, small["attn_out_norm"],
                                           small["sgu_out_norm"], small["post_mix_norm"])
    h2, a, dg, dup, df, dx1, loss_cols, d_pre_ffn, d_post_ffn = _ffn_fwd_bwd(
        x1, target, w_gate, w_up, w_down, small["pre_ffn_norm"], small["post_ffn_norm"])

    full_tok = pl.BlockSpec((SEQ, D_MODEL), lambda s: (0, 0), pipeline_mode=pl.Buffered(1))
    ff_tok = pl.BlockSpec((1, SEQ, FF_S), lambda s: (s, 0, 0))
    gw_gate = _wgrad(dg, h2, ff_tok, full_tok, (FF_S, D_MODEL), "wgrad_gate")
    gw_up = _wgrad(dup, h2, ff_tok, full_tok, (FF_S, D_MODEL), "wgrad_up")
    gw_down = _wgrad(a, df, ff_tok, full_tok, (FF_S, D_MODEL), "wgrad_down")

    dy, dattn, delta, dsgu, d_post_mix, d_attn_norm, d_sgu_norm = _outproj_bwd(
        dx1, y, attn, sgu, w_out, small["post_mix_norm"], small["attn_out_norm"], small["sgu_out_norm"])
    gw_out = _wgrad(mixed, dy, pl.BlockSpec((SEQ, OUT_S), lambda s: (0, s)), full_tok, (OUT_S, D_MODEL), "wgrad_out")
    du, dvs_sgu, d_w_sp, d_b_sp, d_ln_gain, d_ln_bias = _sgu_bwd(
        u, vs, dsgu, small["sgu_ln_gain"], small["sgu_ln_bias"], small["sgu_w_spatial"], b_t)

    dqs, dks, dvs = [], [], []
    for dil, (qv, kv, vv) in zip(DILATIONS, views):
        dq, dk, dv = _attn_bwd(qv, kv, vv, _to_view(dattn, dil), _to_view(delta, dil), _to_view(lse, dil), dil)
        dqs.append(_from_view(dq, dil))
        dks.append(_from_view(dk, dil))
        dvs.append(_from_view(dv, dil))
    dproj, grad_x, d_pre_mix = _inproj_bwd(dqs, dks, dvs, du, dvs_sgu, pos, x, dx1, w_in, small["pre_mix_norm"])
    gw_in = _wgrad(h, dproj, full_tok, pl.BlockSpec((SEQ, IN_S), lambda s: (0, s)), (D_MODEL, IN_S), "wgrad_in")

    small_grads = {
        "pre_mix_norm": d_pre_mix, "sgu_ln_gain": d_ln_gain, "sgu_ln_bias": d_ln_bias, "sgu_w_spatial": d_w_sp,
        "sgu_b_spatial": d_b_sp, "attn_out_norm": d_attn_norm, "sgu_out_norm": d_sgu_norm,
        "post_mix_norm": d_post_mix, "pre_ffn_norm": d_pre_ffn, "post_ffn_norm": d_post_ffn,
    }
    return loss_cols, grad_x, (gw_in, gw_out, gw_gate, gw_up, gw_down), small_grads


def _coords():
    return lax.axis_index("x"), lax.axis_index("y"), lax.axis_index("c")


def _other_chips(x, y):
    return [(1 - x, y), (x, 1 - y), (1 - x, 1 - y)]


def _comm_call(body, name, n_in, out_shape, scratch_shapes):
    return pl.pallas_call(
        body, name=name, in_specs=[ANY] * n_in, out_specs=[ANY] * len(out_shape), out_shape=out_shape,
        scratch_shapes=scratch_shapes,
        compiler_params=pltpu.CompilerParams(has_side_effects=True),
    )


def _gather_weights(shards):
    n = len(shards)
    halves = [s.reshape(2, s.shape[0] // 2, s.shape[1]) for s in shards]

    def body(*refs):
        ins, outs = refs[:n], refs[n:2 * n]
        send_sems, recv_sems = refs[2 * n:]
        x, y, c = _coords()
        s_me = 2 * x + y
        chips = _other_chips(x, y)
        sibling = (x, y, 1 - c)

        def copy(k, w, shard, cc, to):
            src = ins[w].at[cc] if shard is None else outs[w].at[shard, cc]
            dst = outs[w].at[s_me if shard is None else shard, cc]
            return pltpu.make_async_remote_copy(src_ref=src, dst_ref=dst, send_sem=send_sems.at[k],
                                                recv_sem=recv_sems.at[k], device_id=to, device_id_type=MESH)

        first = [copy(j * n + w, w, None, c, (cx, cy, c)) for j, (cx, cy) in enumerate(chips) for w in range(n)]
        for cp in first:
            cp.start()
        passed = []
        for j, (cx, cy) in enumerate(chips):
            for w in range(n):
                copy(j * n + w, w, 2 * cx + cy, c, (x, y, c)).wait_recv()
                fw = copy((3 + j) * n + w, w, 2 * cx + cy, c, sibling)
                fw.start()
                passed.append(fw)
        for j, (cx, cy) in enumerate(chips):
            for w in range(n):
                copy((3 + j) * n + w, w, 2 * cx + cy, 1 - c, (x, y, c)).wait_recv()
        for cp in first + passed:
            cp.wait_send()

    out_shape = [_sds((N_SHARD,) + h.shape, h.dtype) for h in halves]
    scratch = [pltpu.SemaphoreType.DMA((6 * n,)), pltpu.SemaphoreType.DMA((6 * n,))]
    full = _comm_call(body, "comm_gather_weights", n, out_shape, scratch)(*halves)
    s_me = 2 * lax.axis_index("x") + lax.axis_index("y")
    full = [lax.dynamic_update_slice(f, h[None], (s_me, 0, 0, 0)) for f, h in zip(full, halves)]
    return [f.reshape((N_SHARD,) + s.shape) for f, s in zip(full, shards)]


def _rs_to_sibling(gws):
    n = len(gws)

    def body(*refs):
        ins, outs = refs[:n], refs[n:2 * n]
        send_sems, recv_sems = refs[2 * n:]
        x, y, c = _coords()
        copies = []
        for w in range(n):
            hw = gws[w].shape[1] // 2
            copies.append(pltpu.make_async_remote_copy(
                src_ref=ins[w].at[:, pl.ds((1 - c) * hw, hw), :], dst_ref=outs[w], send_sem=send_sems.at[w],
                recv_sem=recv_sems.at[w], device_id=(x, y, 1 - c), device_id_type=MESH))
        for cp in copies:
            cp.start()
        for cp in copies:
            cp.wait()

    out_shape = [_sds((N_SHARD, g.shape[1] // 2, g.shape[2]), g.dtype) for g in gws]
    scratch = [pltpu.SemaphoreType.DMA((n,)), pltpu.SemaphoreType.DMA((n,))]
    return _comm_call(body, "comm_rs_sibling", n, out_shape, scratch)(*gws)


def _rs_chip_sum(gw, recv, core):
    _, rows, cols = gw.shape
    hw = rows // 2

    def body(c_ref, g_ref, r_ref, o_ref):
        o_ref[...] = (g_ref[...] + r_ref[...]).astype(BF16)

    return pl.pallas_call(
        body, name="rs_chip_sum",
        grid_spec=pltpu.PrefetchScalarGridSpec(
            num_scalar_prefetch=1, grid=(N_SHARD,),
            in_specs=[pl.BlockSpec((1, hw, cols), lambda s, c_ref: (s, c_ref[0], 0)),
                      pl.BlockSpec((1, hw, cols), lambda s, c_ref: (s, 0, 0))],
            out_specs=pl.BlockSpec((1, hw, cols), lambda s, c_ref: (s, 0, 0))),
        out_shape=_sds((N_SHARD, hw, cols), BF16),
        compiler_params=_seq_params(),
    )(core, gw, recv)


def _rs_between_chips(pbs):
    n = len(pbs)

    def body(*refs):
        ins, outs = refs[:n], refs[n:2 * n]
        send_sems, recv_sems = refs[2 * n:]
        x, y, c = _coords()
        copies = []
        for j, (cx, cy) in enumerate(_other_chips(x, y)):
            for w in range(n):
                copies.append(pltpu.make_async_remote_copy(
                    src_ref=ins[w].at[2 * cx + cy], dst_ref=outs[w].at[j], send_sem=send_sems.at[j * n + w],
                    recv_sem=recv_sems.at[j * n + w], device_id=(cx, cy, c), device_id_type=MESH))
        for cp in copies:
            cp.start()
        for cp in copies:
            cp.wait()

    out_shape = [_sds((3,) + p.shape[1:], p.dtype) for p in pbs]
    scratch = [pltpu.SemaphoreType.DMA((3 * n,)), pltpu.SemaphoreType.DMA((3 * n,))]
    return _comm_call(body, "comm_rs_chips", n, out_shape, scratch)(*pbs)


def _rs_final_sum(gw, recv_sib, recv_chips, shard_core):
    _, rows, cols = gw.shape
    hw = rows // 2

    def body(sc_ref, g_ref, r_ref, rc_ref, o_ref):
        acc = g_ref[0] + r_ref[0]
        for j in range(3):
            acc = acc + rc_ref[j].astype(F32)
        o_ref[0] = acc

    return pl.pallas_call(
        body, name="rs_final_sum",
        grid_spec=pltpu.PrefetchScalarGridSpec(
            num_scalar_prefetch=1, grid=(1,),
            in_specs=[pl.BlockSpec((1, hw, cols), lambda i, sc: (sc[0], sc[1], 0)),
                      pl.BlockSpec((1, hw, cols), lambda i, sc: (sc[0], 0, 0)),
                      pl.BlockSpec((3, hw, cols), lambda i, sc: (0, 0, 0))],
            out_specs=pl.BlockSpec((1, hw, cols), lambda i, sc: (sc[1], 0, 0))),
        out_shape=_sds((2, hw, cols), F32),
        compiler_params=_seq_params(),
    )(shard_core, gw, recv_sib, recv_chips)


def _rs_join_halves(halves):
    n = len(halves)

    def body(*refs):
        bufs = refs[n:2 * n]
        send_sems, recv_sems = refs[2 * n:]
        x, y, c = _coords()
        remote = [pltpu.make_async_remote_copy(
            src_ref=bufs[w].at[c], dst_ref=bufs[w].at[c], send_sem=send_sems.at[w], recv_sem=recv_sems.at[w],
            device_id=(x, y, 1 - c), device_id_type=MESH) for w in range(n)]
        for cp in remote:
            cp.start()
        for w in range(n):
            remote[w].wait_send()
            pltpu.make_async_remote_copy(
                src_ref=bufs[w].at[c], dst_ref=bufs[w].at[1 - c], send_sem=send_sems.at[w],
                recv_sem=recv_sems.at[w], device_id=(x, y, c), device_id_type=MESH).wait_recv()

    joined = pl.pallas_call(
        body, name="comm_rs_join", in_specs=[ANY] * n, out_specs=[ANY] * n,
        out_shape=[_sds(h.shape, h.dtype) for h in halves], input_output_aliases={w: w for w in range(n)},
        scratch_shapes=[pltpu.SemaphoreType.DMA((n,)), pltpu.SemaphoreType.DMA((n,))],
        compiler_params=pltpu.CompilerParams(has_side_effects=True),
    )(*halves)
    return [j.reshape(2 * h.shape[1], h.shape[2]) for j, h in zip(joined, halves)]


def _allreduce_small(buf):
    rows, cols = buf.shape

    def body(in_ref, out_ref, slots, send_sems, recv_sems):
        x, y, c = _coords()
        me = 4 * x + 2 * y + c
        copies, peers = [], []
        for k in range(1, 8):
            px = 1 - x if (k >> 2) & 1 else x
            py = 1 - y if (k >> 1) & 1 else y
            pc = 1 - c if k & 1 else c
            peers.append(4 * px + 2 * py + pc)
            copies.append(pltpu.make_async_remote_copy(
                src_ref=in_ref, dst_ref=slots.at[me], send_sem=send_sems.at[k - 1], recv_sem=recv_sems.at[k - 1],
                device_id=(px, py, pc), device_id_type=MESH))
        for cp in copies:
            cp.start()
        slots[me] = in_ref[...]
        for k in range(7):
            pltpu.make_async_remote_copy(
                src_ref=in_ref, dst_ref=slots.at[peers[k]], send_sem=send_sems.at[k], recv_sem=recv_sems.at[k],
                device_id=(x, y, c), device_id_type=MESH).wait_recv()
        for cp in copies:
            cp.wait_send()
        acc = slots[0]
        for i in range(1, 8):
            acc = acc + slots[i]
        out_ref[...] = acc

    vmem = pl.BlockSpec(memory_space=pltpu.VMEM)
    return pl.pallas_call(
        body, name="comm_allreduce_small", in_specs=[vmem], out_specs=vmem, out_shape=_sds((rows, cols), F32),
        scratch_shapes=[pltpu.VMEM((8, rows, cols), F32), pltpu.SemaphoreType.DMA((7,)), pltpu.SemaphoreType.DMA((7,))],
        compiler_params=pltpu.CompilerParams(has_side_effects=True, vmem_limit_bytes=VMEM_LIMIT),
    )(buf)


def _adamw(w, g, m, v, block_rows, name, after=()):
    rows, cols = w.shape

    def body(w_ref, g_ref, m_ref, v_ref, *rest):
        d_ref, nm_ref, nv_ref = rest[len(after):]
        g = g_ref[...]
        m = ADAM_B1 * m_ref[...] + (1.0 - ADAM_B1) * g
        v = ADAM_B2 * v_ref[...] + (1.0 - ADAM_B2) * (g * g)
        m_hat = m / (1.0 - ADAM_B1 ** ADAM_STEP)
        v_hat = v / (1.0 - ADAM_B2 ** ADAM_STEP)
        d_ref[...] = -ADAM_LR * (m_hat / (jnp.sqrt(v_hat) + ADAM_EPS) + ADAM_WD * w_ref[...])
        nm_ref[...] = m
        nv_ref[...] = v

    spec = pl.BlockSpec((block_rows, cols), lambda i: (i, 0))
    return pl.pallas_call(
        body, name=name, grid=(rows // block_rows,), in_specs=[spec] * 4 + [ANY] * len(after), out_specs=[spec] * 3,
        out_shape=[_sds((rows, cols), F32)] * 3,
        compiler_params=_seq_params(),
    )(w, g, m, v, *after)


WEIGHTS = ("pre_mix_norm", "w_in", "sgu_ln_gain", "sgu_ln_bias", "sgu_w_spatial", "sgu_b_spatial", "attn_out_norm",
           "sgu_out_norm", "w_out", "post_mix_norm", "pre_ffn_norm", "w_gate", "w_up", "w_down", "post_ffn_norm")
BIG = ("w_in", "w_out", "w_gate", "w_up", "w_down")
BIG_ADAM_ROWS = {"w_in": 256, "w_out": 128, "w_gate": 352, "w_up": 352, "w_down": 352}
SMALL = ("pre_mix_norm", "post_mix_norm", "pre_ffn_norm", "post_ffn_norm", "sgu_ln_gain", "sgu_ln_bias",
         "attn_out_norm", "sgu_out_norm", "sgu_w_spatial", "sgu_b_spatial")


def _pack_small(d):
    flat = [d[n].reshape(-1) for n in SMALL]
    used = sum(f.shape[0] for f in flat)
    flat.append(jnp.zeros((SMALL_ROWS * 1024 - used,), F32))
    return jnp.concatenate(flat).reshape(SMALL_ROWS, 1024)


def _unpack_small(buf, shapes):
    flat = buf.reshape(-1)
    out, off = {}, 0
    for n in SMALL:
        size = int(np.prod(shapes[n]))
        out[n] = flat[off:off + size].reshape(shapes[n])
        off += size
    return out


def _kernel_unoverlapped(x, positions, pre_mix_norm, w_in, sgu_ln_gain, sgu_ln_bias, sgu_w_spatial, sgu_b_spatial, attn_out_norm, sgu_out_norm, w_out, post_mix_norm, pre_ffn_norm, w_gate, w_up, w_down, post_ffn_norm, loss_target, m_pre_mix_norm, m_w_in, m_sgu_ln_gain, m_sgu_ln_bias, m_sgu_w_spatial, m_sgu_b_spatial, m_attn_out_norm, m_sgu_out_norm, m_w_out, m_post_mix_norm, m_pre_ffn_norm, m_w_gate, m_w_up, m_w_down, m_post_ffn_norm, v_pre_mix_norm, v_w_in, v_sgu_ln_gain, v_sgu_ln_bias, v_sgu_w_spatial, v_sgu_b_spatial, v_attn_out_norm, v_sgu_out_norm, v_w_out, v_post_mix_norm, v_pre_ffn_norm, v_w_gate, v_w_up, v_w_down, v_post_ffn_norm):
    a = dict(locals())
    cx, cy, cc = _coords()
    core = jnp.stack([cc]).astype(jnp.int32)
    shard_core = jnp.stack([2 * cx + cy, cc]).astype(jnp.int32)

    full = _gather_weights([a[n][0].astype(BF16) for n in BIG])
    small = {n: (a[n][0] if a[n].ndim > 2 else a[n]) for n in SMALL}
    loss_cols, grad_x, gws, small_grads = _local_step(
        x[0], positions.reshape(SEQ, 1), loss_target[0], *full, small)
    loss = lax.psum(jnp.sum(loss_cols) * np.float32(0.5 / D_MODEL), ("x", "y", "c"))

    recv_sib = _rs_to_sibling(list(gws))
    chip_part = [_rs_chip_sum(g, r, core) for g, r in zip(gws, recv_sib)]
    recv_chips = _rs_between_chips(chip_part)
    halves = [_rs_final_sum(g, r, rc, shard_core) for g, r, rc in zip(gws, recv_sib, recv_chips)]
    big_grads = dict(zip(BIG, _rs_join_halves(halves)))

    shapes = {n: a[n].shape for n in SMALL}
    small_sum = _allreduce_small(_pack_small(small_grads))

    grads, deltas, new_m, new_v = {}, {}, {}, {}
    for n in BIG:
        grads[n] = big_grads[n][None]
        d, nm, nv = _adamw(a[n][0], big_grads[n], a["m_" + n][0], a["v_" + n][0], BIG_ADAM_ROWS[n], "adamw_" + n)
        deltas[n], new_m[n], new_v[n] = d[None], nm[None], nv[None]
    d, nm, nv = _adamw(_pack_small({n: a[n] for n in SMALL}), small_sum, _pack_small({n: a["m_" + n] for n in SMALL}),
                       _pack_small({n: a["v_" + n] for n in SMALL}), SMALL_ROWS, "adamw_small")
    grads.update(_unpack_small(small_sum, shapes))
    deltas.update(_unpack_small(d, shapes))
    new_m.update(_unpack_small(nm, shapes))
    new_v.update(_unpack_small(nv, shapes))
    return (loss, grad_x[None], *[grads[n] for n in WEIGHTS], *[deltas[n] for n in WEIGHTS],
            *[new_m[n] for n in WEIGHTS], *[new_v[n] for n in WEIGHTS])


def _remote(src, dst, send_sem, recv_sem, to):
    return pltpu.make_async_remote_copy(src_ref=src, dst_ref=dst, send_sem=send_sem, recv_sem=recv_sem,
                                        device_id=to, device_id_type=MESH)


def _halves(a):
    *lead, rows, cols = a.shape
    return a.reshape(*lead, 2, rows // 2, cols)


def _gather_ici(shards):
    n = len(shards)

    def desc(ins, outs, ss, rs, j, w, landed):
        x, y, c = _coords()
        cx, cy = _other_chips(x, y)[j]
        shard = 2 * cx + cy if landed else 2 * x + y
        return _remote(ins[w].at[c], outs[w].at[shard, c], ss.at[j * n + w], rs.at[j * n + w], (cx, cy, c))

    def start(ins, outs, ss, rs):
        for j in range(3):
            for w in range(n):
                desc(ins, outs, ss, rs, j, w, False).start()

    def finish(ins, outs, ss, rs):
        for j in range(3):
            for w in range(n):
                desc(ins, outs, ss, rs, j, w, True).wait_recv()
                desc(ins, outs, ss, rs, j, w, False).wait_send()

    return _Comm(shards, [_sds((N_SHARD,) + s.shape, s.dtype) for s in shards], 3 * n, start, finish)


def _gather_pass(fulls):
    n = len(fulls)

    def desc(bufs, ss, rs, j, w, landed):
        x, y, c = _coords()
        cx, cy = _other_chips(x, y)[j]
        shard = 2 * cx + cy
        return _remote(bufs[w].at[shard, c], bufs[w].at[shard, 1 - c if landed else c],
                       ss.at[j * n + w], rs.at[j * n + w], (x, y, 1 - c))

    def start(ins, outs, ss, rs):
        for j in range(3):
            for w in range(n):
                desc(outs, ss, rs, j, w, False).start()

    def finish(ins, outs, ss, rs):
        for j in range(3):
            for w in range(n):
                desc(outs, ss, rs, j, w, True).wait_recv()
                desc(outs, ss, rs, j, w, False).wait_send()

    return _Comm(fulls, [_sds(f.shape, f.dtype) for f in fulls], 3 * n, start, finish, aliased=True)


def _rs_sibling(gws):
    n = len(gws)

    def desc(ins, outs, ss, rs, w):
        x, y, c = _coords()
        return _remote(ins[w].at[:, 1 - c], outs[w], ss.at[w], rs.at[w], (x, y, 1 - c))

    def start(ins, outs, ss, rs):
        for w in range(n):
            desc(ins, outs, ss, rs, w).start()

    def finish(ins, outs, ss, rs):
        for w in range(n):
            desc(ins, outs, ss, rs, w).wait()

    out_shape = [_sds((N_SHARD, g.shape[1] // 2, g.shape[2]), g.dtype) for g in gws]
    return _Comm([_halves(g) for g in gws], out_shape, n, start, finish)


def _rs_chips(pbs):
    n = len(pbs)

    def desc(ins, outs, ss, rs, j, w):
        x, y, c = _coords()
        cx, cy = _other_chips(x, y)[j]
        return _remote(ins[w].at[2 * cx + cy], outs[w].at[j], ss.at[j * n + w], rs.at[j * n + w], (cx, cy, c))

    def start(ins, outs, ss, rs):
        for j in range(3):
            for w in range(n):
                desc(ins, outs, ss, rs, j, w).start()

    def finish(ins, outs, ss, rs):
        for j in range(3):
            for w in range(n):
                desc(ins, outs, ss, rs, j, w).wait()

    return _Comm(pbs, [_sds((3,) + p.shape[1:], p.dtype) for p in pbs], 3 * n, start, finish)


def _rs_join(halves):
    n = len(halves)

    def desc(bufs, ss, rs, w, landed):
        x, y, c = _coords()
        return _remote(bufs[w].at[c], bufs[w].at[1 - c if landed else c], ss.at[w], rs.at[w], (x, y, 1 - c))

    def start(ins, outs, ss, rs):
        for w in range(n):
            desc(outs, ss, rs, w, False).start()

    def finish(ins, outs, ss, rs):
        for w in range(n):
            desc(outs, ss, rs, w, True).wait_recv()
            desc(outs, ss, rs, w, False).wait_send()

    return _Comm(halves, [_sds(h.shape, h.dtype) for h in halves], n, start, finish, aliased=True)


def _small_exchange(buf):
    def desc(ins, outs, ss, rs, k, landed):
        x, y, c = _coords()
        px = 1 - x if (k >> 2) & 1 else x
        py = 1 - y if (k >> 1) & 1 else y
        pc = 1 - c if k & 1 else c
        slot = 4 * px + 2 * py + pc if landed else 4 * x + 2 * y + c
        return _remote(ins[0], outs[0].at[slot], ss.at[k - 1], rs.at[k - 1], (px, py, pc))

    def start(ins, outs, ss, rs):
        for k in range(1, 8):
            desc(ins, outs, ss, rs, k, False).start()

    def finish(ins, outs, ss, rs):
        for k in range(1, 8):
            desc(ins, outs, ss, rs, k, True).wait_recv()
            desc(ins, outs, ss, rs, k, False).wait_send()

    return _Comm([buf], [_sds((8,) + buf.shape, buf.dtype)], 7, start, finish)


HBM = pl.BlockSpec(memory_space=pltpu.HBM)
SEM = pl.BlockSpec(memory_space=pltpu.SEMAPHORE)
DATAFLOW = pltpu.SideEffectType.DATAFLOW_SIDE_EFFECTING


def _split_start(name, comm, lands, after):
    srcs = [pltpu.with_memory_space_constraint(s, pltpu.HBM) for s in comm.args]
    lands = [pltpu.with_memory_space_constraint(b, pltpu.HBM) for b in lands]
    ns, nb = len(srcs), len(lands)

    def body(*refs):
        send_sems, recv_sems = refs[ns + nb + 1], refs[ns + nb + 2]
        comm.start(refs[:ns], refs[ns:ns + nb], send_sems, recv_sems)
        refs[-1][...] = jnp.zeros_like(refs[-1])

    res = pl.pallas_call(
        body, name=name,
        out_shape=(pltpu.SemaphoreType.DMA((comm.n_sems,)), pltpu.SemaphoreType.DMA((comm.n_sems,)),
                   *[pltpu.HBM(b.shape, b.dtype) for b in srcs + lands], _sds((8, 128), F32)),
        in_specs=[HBM] * (ns + nb) + [ANY],
        out_specs=(SEM, SEM, *[HBM] * (ns + nb), pl.BlockSpec(memory_space=pltpu.VMEM)),
        input_output_aliases={i: 2 + i for i in range(ns + nb)},
        compiler_params=pltpu.CompilerParams(has_side_effects=DATAFLOW),
    )(*srcs, *lands, after)
    return res[0], res[1], list(res[2:2 + ns]), list(res[2 + ns:2 + ns + nb]), res[-1]


def _split_wait(name, comm, send_sems, recv_sems, srcs, lands, after):
    ns, nb = len(srcs), len(lands)

    def body(*refs):
        comm.finish(refs[:ns], refs[ns:ns + nb], refs[ns + nb], refs[ns + nb + 1])

    res = pl.pallas_call(
        body, name=name,
        out_shape=tuple(pltpu.HBM(b.shape, b.dtype) for b in srcs + lands),
        in_specs=[HBM] * (ns + nb) + [SEM, SEM, ANY], out_specs=tuple([HBM] * (ns + nb)),
        input_output_aliases={i: i for i in range(ns + nb)},
        compiler_params=pltpu.CompilerParams(has_side_effects=DATAFLOW),
    )(*srcs, *lands, send_sems, recv_sems, after)
    return list(res[ns:])


LOSS_ROW = "loss_cols"
SMALL_EARLY = ("post_mix_norm", "pre_ffn_norm", "post_ffn_norm", "sgu_ln_gain", "sgu_ln_bias", "attn_out_norm",
               "sgu_out_norm", "sgu_w_spatial", "sgu_b_spatial", LOSS_ROW)
SMALL_LATE = ("pre_mix_norm",)


def _pack(d, names, rows):
    flat = [d[n].reshape(-1) for n in names]
    used = sum(f.shape[0] for f in flat)
    flat.append(jnp.zeros((rows * 1024 - used,), F32))
    return jnp.concatenate(flat).reshape(rows, 1024)


def _unpack(buf, names, shapes):
    flat = buf.reshape(-1)
    out, off = {}, 0
    for n in names:
        size = int(np.prod(shapes[n]))
        out[n] = flat[off:off + size].reshape(shapes[n])
        off += size
    return out


def _comm_only(name, comms):
    return _pcall(lambda: None, name=name, grid=(1,), in_specs=[], out_specs=[], out_shape=[], args=(),
                  comms=comms)[1]


def _adam_math(w, g, m, v):
    m = ADAM_B1 * m + (1.0 - ADAM_B1) * g
    v = ADAM_B2 * v + (1.0 - ADAM_B2) * (g * g)
    m_hat = m / (1.0 - ADAM_B1 ** ADAM_STEP)
    v_hat = v / (1.0 - ADAM_B2 ** ADAM_STEP)
    return -ADAM_LR * (m_hat / (jnp.sqrt(v_hat) + ADAM_EPS) + ADAM_WD * w), m, v


def _adamw_small(own, slots, w, m, v, me):
    rows, cols = own.shape

    def body(me_ref, own_ref, slots_ref, w_ref, m_ref, v_ref, g_ref, d_ref, nm_ref, nv_ref):
        own_v = own_ref[...]
        g = jnp.where(me_ref[0] == 0, own_v, slots_ref[0])
        for i in range(1, 8):
            g = g + jnp.where(me_ref[0] == i, own_v, slots_ref[i])
        g_ref[...] = g
        d_ref[...], nm_ref[...], nv_ref[...] = _adam_math(w_ref[...], g, m_ref[...], v_ref[...])

    flat = pl.BlockSpec((rows, cols), lambda i, me_ref: (0, 0))
    return pl.pallas_call(
        body, name="adamw_small",
        grid_spec=pltpu.PrefetchScalarGridSpec(
            num_scalar_prefetch=1, grid=(1,),
            in_specs=[flat, pl.BlockSpec((8, rows, cols), lambda i, me_ref: (0, 0, 0)), flat, flat, flat],
            out_specs=[flat] * 4),
        out_shape=[_sds((rows, cols), F32)] * 4,
        compiler_params=_seq_params(),
    )(me, own, slots, w, m, v)


def kernel(x, positions, pre_mix_norm, w_in, sgu_ln_gain, sgu_ln_bias, sgu_w_spatial, sgu_b_spatial, attn_out_norm, sgu_out_norm, w_out, post_mix_norm, pre_ffn_norm, w_gate, w_up, w_down, post_ffn_norm, loss_target, m_pre_mix_norm, m_w_in, m_sgu_ln_gain, m_sgu_ln_bias, m_sgu_w_spatial, m_sgu_b_spatial, m_attn_out_norm, m_sgu_out_norm, m_w_out, m_post_mix_norm, m_pre_ffn_norm, m_w_gate, m_w_up, m_w_down, m_post_ffn_norm, v_pre_mix_norm, v_w_in, v_sgu_ln_gain, v_sgu_ln_bias, v_sgu_w_spatial, v_sgu_b_spatial, v_attn_out_norm, v_sgu_out_norm, v_w_out, v_post_mix_norm, v_pre_ffn_norm, v_w_gate, v_w_up, v_w_down, v_post_ffn_norm):
    a = dict(locals())
    cx, cy, cc = _coords()
    s_me = 2 * cx + cy
    core = jnp.stack([cc]).astype(jnp.int32)
    shard_core = jnp.stack([s_me, cc]).astype(jnp.int32)
    me = jnp.stack([4 * cx + 2 * cy + cc]).astype(jnp.int32)
    small = {n: (a[n][0] if a[n].ndim > 2 else a[n]) for n in SMALL}
    b_t = small["sgu_b_spatial"].T
    xs, pos, target = x[0], positions.reshape(SEQ, 1), loss_target[0]
    flipped = ("w_gate", "w_up")

    def big(name, n):
        return jnp.swapaxes(a[name], 1, 2)[0] if n in flipped else a[name][0]

    own = {n: _halves(big(n, n).astype(BF16)) for n in BIG}

    def with_own(full, n):
        full = lax.dynamic_update_slice(full, own[n][None], (s_me, 0, 0, 0))
        return full.reshape((N_SHARD,) + big(n, n).shape)

    ffn = ("w_gate", "w_up", "w_down")
    g_in, g_out, g_ffn = _gather_ici([own["w_in"]]), _gather_ici([own["w_out"]]), _gather_ici([own[n] for n in ffn])
    s_in = _split_start("gather_in_start", g_in, [lax.empty(o.shape, o.dtype) for o in g_in.out_shape],
                        small["pre_mix_norm"])
    s_out = _split_start("gather_out_start", g_out, [lax.empty(o.shape, o.dtype) for o in g_out.out_shape], s_in[4])
    s_ffn = _split_start("gather_ffn_start", g_ffn, [lax.empty(o.shape, o.dtype) for o in g_ffn.out_shape], s_out[4])
    in_lands = _split_wait("gather_in_wait", g_in, *s_in[:4], s_ffn[4])
    ((in_lands,),) = _comm_only("comm_pass_in", [_gather_pass(in_lands)])
    w_in_f = with_own(in_lands, "w_in")
    h, q, k, v, u, vs = _inproj_fwd(xs, pos, small["pre_mix_norm"], w_in_f)
    views = [tuple(_to_view(t, dil) for t in (q, k, v)) for dil in DILATIONS]
    o_list, l_list = [], []
    for dil, (qv, kv, vv) in zip(DILATIONS, views):
        o, l = _attn_fwd(qv, kv, vv, dil)
        o_list.append(_from_view(o, dil))
        l_list.append(_from_view(l, dil))
    out_lands = _split_wait("gather_out_wait", g_out, *s_out[:4], l_list[-1])
    sgu, ((out_lands,),) = _sgu_fwd(u, vs, small["sgu_ln_gain"], small["sgu_ln_bias"], small["sgu_w_spatial"], b_t,
                                    comms=[_gather_pass(out_lands)])
    w_out_f = with_own(out_lands, "w_out")
    ffn_lands = _split_wait("gather_ffn_wait", g_ffn, *s_ffn[:4], sgu)
    (attn, lse, mixed, y, x1), (ffn_lands,) = _mix_out_fwd(
        o_list, l_list, sgu, xs, w_out_f, small["attn_out_norm"], small["sgu_out_norm"], small["post_mix_norm"],
        comms=[_gather_pass(ffn_lands)])
    w_gate_f, w_up_f, w_down_f = (with_own(f, n) for f, n in zip(ffn_lands, ffn))
    h2, act, dg, dup, df, dx1, loss_cols, d_pre_ffn, d_post_ffn = _ffn_fwd_bwd(
        x1, target, w_gate_f, w_up_f, w_down_f, small["pre_ffn_norm"], small["post_ffn_norm"])

    full_tok = pl.BlockSpec((SEQ, D_MODEL), lambda s: (0, 0), pipeline_mode=pl.Buffered(1))
    ff_tok = pl.BlockSpec((1, SEQ, FF_S), lambda s: (s, 0, 0))
    gw = {}
    gw["w_gate"] = _wgrad(dg, h2, ff_tok, full_tok, (FF_S, D_MODEL), "wgrad_gate")
    gw["w_up"], ((sib_gate,),) = _wgrad(dup, h2, ff_tok, full_tok, (FF_S, D_MODEL), "wgrad_up",
                                        comms=[_rs_sibling([gw["w_gate"]])])
    gw["w_down"], ((sib_up,),) = _wgrad(act, df, ff_tok, full_tok, (FF_S, D_MODEL), "wgrad_down",
                                        comms=[_rs_sibling([gw["w_up"]])])
    (dy, dattn, delta, dsgu, d_post_mix, d_attn_norm, d_sgu_norm), ((sib_down,),) = _outproj_bwd(
        dx1, y, attn, sgu, w_out_f, small["post_mix_norm"], small["attn_out_norm"],
        small["sgu_out_norm"], comms=[_rs_sibling([gw["w_down"]])])
    sib = {"w_gate": sib_gate, "w_up": sib_up, "w_down": sib_down}
    part = {n: _rs_chip_sum(gw[n], sib[n], core) for n in ("w_gate", "w_up", "w_down")}
    gw["w_out"] = _wgrad(mixed, dy, pl.BlockSpec((SEQ, OUT_S), lambda s: (0, s)), full_tok, (OUT_S, D_MODEL),
                         "wgrad_out")
    ffn = ("w_gate", "w_up", "w_down")
    x_ffn = _rs_chips([part[n] for n in ffn])
    s_ffn = _split_start("rs_ffn_start", x_ffn, [lax.empty(o.shape, o.dtype) for o in x_ffn.out_shape],
                         part["w_down"])
    (du, dvs_sgu, d_w_sp, d_b_sp, d_ln_gain, d_ln_bias), ((sib["w_out"],),) = _sgu_bwd(
        u, vs, dsgu, small["sgu_ln_gain"], small["sgu_ln_bias"], small["sgu_w_spatial"], b_t,
        comms=[_rs_sibling([gw["w_out"]])], after=[s_ffn[4]])
    part["w_out"] = _rs_chip_sum(gw["w_out"], sib["w_out"], core)
    packed_early = _pack({
        "sgu_ln_gain": d_ln_gain, "sgu_ln_bias": d_ln_bias, "sgu_w_spatial": d_w_sp, "sgu_b_spatial": d_b_sp,
        "attn_out_norm": d_attn_norm, "sgu_out_norm": d_sgu_norm, "post_mix_norm": d_post_mix,
        "pre_ffn_norm": d_pre_ffn, "post_ffn_norm": d_post_ffn, LOSS_ROW: loss_cols}, SMALL_EARLY, SMALL_ROWS)
    x_out, x_small = _rs_chips([part["w_out"]]), _small_exchange(packed_early)
    s_out = _split_start("rs_out_start", x_out, [lax.empty(o.shape, o.dtype) for o in x_out.out_shape], s_ffn[4])
    s_small = _split_start("small_early_start", x_small,
                           [lax.empty(o.shape, o.dtype) for o in x_small.out_shape], s_out[4])

    dqs, dks, dvs = [], [], []
    for dil, (qv, kv, vv) in zip(DILATIONS, views):
        dq, dk, dv = _attn_bwd(qv, kv, vv, _to_view(dattn, dil), _to_view(delta, dil), _to_view(lse, dil), dil,
                               after=[s_small[4]])
        dqs.append(_from_view(dq, dil))
        dks.append(_from_view(dk, dil))
        dvs.append(_from_view(dv, dil))
    half, far, joined = {}, {}, {}
    got = _split_wait("rs_ffn_wait", x_ffn, *s_ffn[:4], dvs[-1])
    for n, f in zip(ffn, got):
        far[n] = f
        half[n] = _rs_final_sum(gw[n], sib[n], f, shard_core)
    (dproj, grad_x, d_pre_mix), (got,) = _inproj_bwd(
        dqs, dks, dvs, du, dvs_sgu, pos, xs, dx1, w_in_f, small["pre_mix_norm"],
        comms=[_rs_join([half[n] for n in ffn])])
    joined.update(zip(ffn, got))
    (far["w_out"],) = _split_wait("rs_out_wait", x_out, *s_out[:4], grad_x)
    (slots_early,) = _split_wait("small_early_wait", x_small, *s_small[:4], far["w_out"])
    half["w_out"] = _rs_final_sum(gw["w_out"], sib["w_out"], far["w_out"], shard_core)
    packed_late = _pack({"pre_mix_norm": d_pre_mix}, SMALL_LATE, 8)
    gw["w_in"], ((joined["w_out"],), (slots_late,)) = _wgrad(
        h, dproj, full_tok, pl.BlockSpec((SEQ, IN_S), lambda s: (0, s)), (D_MODEL, IN_S), "wgrad_in",
        comms=[_rs_join([half["w_out"]]), _small_exchange(packed_late)])
    ((sib["w_in"],),) = _comm_only("comm_rs_sibling_in", [_rs_sibling([gw["w_in"]])])
    part["w_in"] = _rs_chip_sum(gw["w_in"], sib["w_in"], core)
    x_in = _rs_chips([part["w_in"]])
    s_in = _split_start("rs_in_start", x_in, [lax.empty(o.shape, o.dtype) for o in x_in.out_shape], part["w_in"])

    grads, deltas, new_m, new_v = {}, {}, {}, {}

    def update(n, after):
        g = joined[n].reshape(big(n, n).shape)
        outs = (g, *_adamw(big(n, n), g, big("m_" + n, n), big("v_" + n, n), BIG_ADAM_ROWS[n], "adamw_" + n, after))
        grads[n], deltas[n], new_m[n], new_v[n] = (
            jnp.swapaxes(o[None], 1, 2) if n in flipped else o[None] for o in outs)

    for n in ("w_gate", "w_up", "w_down", "w_out"):
        update(n, [s_in[4]])
    (far["w_in"],) = _split_wait("rs_in_wait", x_in, *s_in[:4], new_v["w_out"])
    half["w_in"] = _rs_final_sum(gw["w_in"], sib["w_in"], far["w_in"], shard_core)
    ((joined["w_in"],),) = _comm_only("comm_rs_join_in", [_rs_join([half["w_in"]])])
    update("w_in", [])
    a[LOSS_ROW] = a["m_" + LOSS_ROW] = a["v_" + LOSS_ROW] = jnp.zeros((1, D_MODEL), F32)
    for names, rows, packed, slots in ((SMALL_EARLY, SMALL_ROWS, packed_early, slots_early),
                                       (SMALL_LATE, 8, packed_late, slots_late)):
        outs = _adamw_small(packed, slots, _pack(a, names, rows), _pack({n: a["m_" + n] for n in names}, names, rows),
                            _pack({n: a["v_" + n] for n in names}, names, rows), me)
        for dst, buf in zip((grads, deltas, new_m, new_v), outs):
            dst.update(_unpack(buf, names, {n: a[n].shape for n in names}))
    loss = jnp.sum(grads[LOSS_ROW]) * np.float32(0.5 / D_MODEL)
    return (loss, grad_x[None], *[grads[n] for n in WEIGHTS], *[deltas[n] for n in WEIGHTS],
            *[new_m[n] for n in WEIGHTS], *[new_v[n] for n in WEIGHTS])
```

```python
import numpy as np
import jax
import jax.numpy as jnp
from jax import lax
from jax.experimental import pallas as pl
from jax.experimental.pallas import tpu as pltpu

F32 = jnp.float32
BF16 = jnp.bfloat16

SEQ = 2048
D_MODEL = 1024
HEAD_DIM = 64
ATTN_W = 512
SGU_W = 512
SGU_GROUPS = 8
CHUNK = 128
DILATIONS = (1, 4, 16)
N_SHARD = 4
IN_S = 640
OUT_S = 256
FF_S = 704
PROJ_W = N_SHARD * IN_S
RMS_EPS = 1e-6
LN_EPS = 1e-5
ROPE_THETA = 500000.0
ATTN_SCALE = 1.0 / np.sqrt(HEAD_DIM)
NEG = -1e30
TM = 256
VMEM_LIMIT = 56 * 1024 * 1024
SMALL_ROWS = 136

ADAM_LR = 0.001
ADAM_B1 = 0.9
ADAM_B2 = 0.999
ADAM_EPS = 1e-08
ADAM_WD = 0.01
ADAM_STEP = 10

MESH = pl.DeviceIdType.MESH
ANY = pl.BlockSpec(memory_space=pl.ANY)


def _dot(a, b):
    return jnp.dot(a, b, preferred_element_type=F32)


def _dot_nt(a, b):
    return lax.dot_general(a, b, (((1,), (1,)), ((), ())), preferred_element_type=F32)


def _dot_tn(a, b):
    return lax.dot_general(a, b, (((0,), (0,)), ((), ())), preferred_element_type=F32)


def _dot_exact(a, b):
    return jnp.dot(a, b, preferred_element_type=F32, precision=lax.Precision.HIGHEST)


def _rms_stats(x):
    r = lax.rsqrt(jnp.mean(x * x, axis=-1, keepdims=True) + RMS_EPS)
    return x * r, r


def _rms_bwd(xh, r, gain, dy):
    dxh = dy * gain
    dx = r * (dxh - xh * jnp.mean(dxh * xh, axis=-1, keepdims=True))
    return dx, jnp.sum(dy * xh, axis=0, keepdims=True)


_ERF_ALPHA = (-2.72614225801306e-10, 2.77068142495902e-08, -2.10102402082508e-06, -5.69250639462346e-05,
              -7.34990630326855e-04, -2.95459980854025e-03, -1.60960333262415e-02)
_ERF_BETA = (-1.45660718464996e-05, -2.13374055278905e-04, -1.68282697438203e-03, -7.37332916720468e-03,
             -1.42647390514189e-02)


def _erf(x):
    x = jnp.clip(x, -4.0, 4.0)
    x2 = x * x
    p = jnp.full_like(x, _ERF_ALPHA[0])
    for a in _ERF_ALPHA[1:]:
        p = p * x2 + a
    q = jnp.full_like(x, _ERF_BETA[0])
    for b in _ERF_BETA[1:]:
        q = q * x2 + b
    return x * p / q


def _gelu(x):
    return 0.5 * x * (1.0 + _erf(x * np.float32(1.0 / np.sqrt(2.0))))


def _gelu_grad(x):
    cdf = 0.5 * (1.0 + _erf(x * np.float32(1.0 / np.sqrt(2.0))))
    pdf = jnp.exp(-0.5 * x * x) * np.float32(1.0 / np.sqrt(2.0 * np.pi))
    return cdf + x * pdf


def _sigmoid(x):
    return 1.0 / (1.0 + jnp.exp(-x))


_INV_FREQ = tuple(float(np.float32(ROPE_THETA ** (-2.0 * j / 16.0))) for j in range(8))


def _rot_tables(pos):
    lane = lax.broadcasted_iota(jnp.int32, (1, 128), 1)
    d = lane & 63
    j = d & 7
    inv = jnp.zeros((1, 128), F32)
    for jj in range(8):
        inv = jnp.where(j == jj, _INV_FREQ[jj], inv)
    ang = pos.astype(F32) * inv
    c = jnp.cos(ang)
    s = jnp.sin(ang)
    cos_t = jnp.where(d < 16, c, 1.0)
    sin_a = jnp.where(d < 8, -s, 0.0)
    sin_b = jnp.where((d >= 8) & (d < 16), s, 0.0)
    return tuple(jnp.tile(t, (1, 4)) for t in (cos_t, sin_a, sin_b))


def _rope(x, tabs):
    cos_t, sin_a, sin_b = tabs
    return x * cos_t + pltpu.roll(x, 504, 1) * sin_a + pltpu.roll(x, 8, 1) * sin_b


def _rope_bwd(dy, tabs):
    cos_t, sin_a, sin_b = tabs
    return dy * cos_t + pltpu.roll(dy * sin_a, 8, 1) + pltpu.roll(dy * sin_b, 504, 1)


def _left_half():
    return lax.broadcasted_iota(jnp.int32, (CHUNK, CHUNK), 1) < HEAD_DIM


def _group_ones():
    lane = lax.broadcasted_iota(jnp.int32, (SGU_GROUPS, SGU_W), 1)
    row = lax.broadcasted_iota(jnp.int32, (SGU_GROUPS, SGU_W), 0)
    return ((lane >> 6) == row).astype(F32)


def _masked_spatial(w_ref):
    row = lax.broadcasted_iota(jnp.int32, (CHUNK, CHUNK), 0)
    col = lax.broadcasted_iota(jnp.int32, (CHUNK, CHUNK), 1)
    return [jnp.where(col <= row, w_ref[g], 0.0).astype(BF16) for g in range(SGU_GROUPS)]


def _sgu_core(u, vs, lg, lb, wm, bias_full):
    tm = u.shape[0]
    gu = _gelu(u)
    gv = _gelu(vs)
    mu = jnp.mean(gv, axis=-1, keepdims=True)
    xc = gv - mu
    rstd = lax.rsqrt(jnp.mean(xc * xc, axis=-1, keepdims=True) + LN_EPS)
    xh = xc * rstd
    vnb = (xh * lg + lb).astype(BF16)
    left = _left_half()
    rows = []
    for c in range(tm // CHUNK):
        pieces = []
        for p in range(4):
            vp = vnb[c * CHUNK:(c + 1) * CHUNK, p * 128:(p + 1) * 128]
            pieces.append(jnp.where(left, _dot(wm[2 * p], vp), _dot(wm[2 * p + 1], vp)))
        rows.append(jnp.concatenate(pieces, axis=1) + bias_full)
    mixed = jnp.concatenate(rows, axis=0)
    return gu, xh, rstd, vnb, mixed


def _resident(shape):
    n = len(shape)
    return pl.BlockSpec(shape, lambda *_: (0,) * n, pipeline_mode=pl.Buffered(1))


def _rows(ncol, tm=TM):
    return pl.BlockSpec((tm, ncol), lambda i: (i, 0))


def _rows3(nlead, ncol, tm=TM):
    return pl.BlockSpec((nlead, tm, ncol), lambda i: (0, i, 0))


def _acc(ncol, nrow=1):
    return pl.BlockSpec((nrow, ncol), lambda i: (0, 0))


def _view_rows(dil, tm=TM):
    return pl.BlockSpec((tm // dil, dil * ATTN_W), lambda i: (i, 0))


def _view_shape(dil, dtype):
    return _sds((SEQ // dil, dil * ATTN_W), dtype)


def _slab_scratch():
    return pltpu.VMEM((4, TM, 128), F32)


def _store_view(val, out_ref, slabs, dil):
    for j in range(4):
        slabs[j] = val[:, j * 128:(j + 1) * 128]
    for r in range(dil):
        for j in range(4):
            c0 = r * ATTN_W + j * 128
            out_ref[:, c0:c0 + 128] = slabs.at[j][pl.ds(r, TM // dil, stride=dil), :].astype(out_ref.dtype)


def _load_view(in_ref, slabs, dil):
    for r in range(dil):
        for j in range(4):
            c0 = r * ATTN_W + j * 128
            slabs.at[j][pl.ds(r, TM // dil, stride=dil), :] = in_ref[:, c0:c0 + 128].astype(F32)
    return jnp.concatenate([slabs[j] for j in range(4)], axis=1)


def _seq_params():
    return pltpu.CompilerParams(dimension_semantics=("arbitrary",), vmem_limit_bytes=VMEM_LIMIT)


def _sds(shape, dtype):
    return jax.ShapeDtypeStruct(shape, dtype)


class _Comm:
    def __init__(self, args, out_shape, n_sems, start, finish, aliased=False):
        self.args, self.out_shape, self.n_sems = list(args), list(out_shape), n_sems
        self.start, self.finish, self.aliased = start, finish, aliased


def _pcall(body, *, name, grid, in_specs, out_specs, out_shape, args, scratch_shapes=(), comms=(), after=()):
    single = not isinstance(out_shape, (list, tuple))
    out_specs = [out_specs] if single else list(out_specs)
    out_shape = [out_shape] if single else list(out_shape)
    n_in, n_out, n_scr = len(in_specs), len(out_shape), len(scratch_shapes)
    c_args = [a for c in comms for a in c.args]
    c_outs = [o for c in comms for o in c.out_shape]
    aliases, ai, ao = {}, n_in, n_out
    for c in comms:
        if c.aliased:
            aliases.update({ai + k: ao + k for k in range(len(c.args))})
        ai += len(c.args)
        ao += len(c.out_shape)
    sems = [pltpu.SemaphoreType.DMA((c.n_sems,)) for c in comms for _ in range(2)]
    steps = grid[0]

    def wrapped(*refs):
        o0 = n_in + len(c_args) + len(after)
        s0 = o0 + n_out + len(c_outs)
        m_in, m_out, m_sem = refs[n_in:n_in + len(c_args)], refs[o0 + n_out:s0], refs[s0 + n_scr:]

        def each(phase):
            ii = oi = 0
            for k, c in enumerate(comms):
                getattr(c, phase)(m_in[ii:ii + len(c.args)], m_out[oi:oi + len(c.out_shape)],
                                  m_sem[2 * k], m_sem[2 * k + 1])
                ii += len(c.args)
                oi += len(c.out_shape)

        if comms:
            @pl.when(pl.program_id(0) == 0)
            def _():
                each("start")

        body(*refs[:n_in], *refs[o0:o0 + n_out], *refs[s0:s0 + n_scr])

        if comms:
            @pl.when(pl.program_id(0) == steps - 1)
            def _():
                each("finish")

    res = pl.pallas_call(
        wrapped, name=name, grid=grid,
        in_specs=list(in_specs) + [ANY] * (len(c_args) + len(after)), out_specs=out_specs + [ANY] * len(c_outs),
        out_shape=out_shape + c_outs, scratch_shapes=list(scratch_shapes) + sems,
        input_output_aliases=aliases, compiler_params=_seq_params(),
    )(*args, *c_args, *after)
    mine = res[0] if single else list(res[:n_out])
    if not comms:
        return mine
    theirs, oi = [], n_out
    for c in comms:
        theirs.append(list(res[oi:oi + len(c.out_shape)]))
        oi += len(c.out_shape)
    return mine, theirs


def _inproj_fwd(x, pos, g_pre, w_in, comms=()):
    def body(x_ref, pos_ref, g_ref, w_ref, h_ref, u_ref, vs_ref, *rest):
        qkv_refs, slabs = rest[:9], rest[9:]
        xh, _ = _rms_stats(x_ref[...])
        h = (xh * g_ref[...]).astype(BF16)
        h_ref[...] = h
        proj = jnp.concatenate([_dot(h, w_ref[s]) for s in range(N_SHARD)], axis=1)
        tabs = _rot_tables(pos_ref[...])
        u_ref[...] = proj[:, 1536:2048]
        vs_ref[...] = proj[:, 2048:2560]
        qkv = (_rope(proj[:, 0:512], tabs) * np.float32(ATTN_SCALE), _rope(proj[:, 512:1024], tabs),
               proj[:, 1024:1536])
        for t, val in enumerate(qkv):
            qkv_refs[t][...] = val.astype(BF16)
            for i, dil in enumerate(DILATIONS[1:]):
                _store_view(val, qkv_refs[3 * (i + 1) + t], slabs[t], dil)

    return _pcall(
        body, name="inproj_fwd", grid=(SEQ // TM,),
        in_specs=[_rows(D_MODEL), _rows(1), _resident((1, D_MODEL)), _resident((N_SHARD, D_MODEL, IN_S))],
        out_specs=[_rows(D_MODEL), _rows(512), _rows(512)] + [_view_rows(dil) for dil in DILATIONS for _ in range(3)],
        out_shape=[_sds((SEQ, D_MODEL), BF16), _sds((SEQ, 512), F32), _sds((SEQ, 512), F32)]
        + [_view_shape(dil, BF16) for dil in DILATIONS for _ in range(3)],
        scratch_shapes=[_slab_scratch() for _ in range(3)],
        args=(x, pos, g_pre, w_in), comms=comms)


def _sgu_fwd(u, vs, lg, lb, w_sp, b_t, comms=()):
    def body(u_ref, vs_ref, lg_ref, lb_ref, w_ref, bt_ref, out_ref):
        wm = _masked_spatial(w_ref)
        bias_full = _dot_exact(bt_ref[...], _group_ones())
        gu, _, _, _, mixed = _sgu_core(u_ref[...], vs_ref[...], lg_ref[...], lb_ref[...], wm, bias_full)
        out_ref[...] = gu * mixed

    return _pcall(
        body, name="sgu_fwd", grid=(SEQ // TM,),
        in_specs=[_rows(SGU_W), _rows(SGU_W), _resident((1, SGU_W)), _resident((1, SGU_W)),
                  _resident((SGU_GROUPS, CHUNK, CHUNK)), _resident((CHUNK, SGU_GROUPS))],
        out_specs=_rows(SGU_W),
        out_shape=_sds((SEQ, SGU_W), F32),
        args=(u, vs, lg, lb, w_sp, b_t), comms=comms)


def _block_masks():
    row = lax.broadcasted_iota(jnp.int32, (CHUNK, CHUNK), 0)
    col = lax.broadcasted_iota(jnp.int32, (CHUNK, CHUNK), 1)
    return col <= row, col >= row


def _attn_fwd(qv, kv, vv, dil, comms=()):
    seg = SEQ // dil
    nblk = seg // CHUNK

    def body(q_ref, k_ref, v_ref, o_ref, l_ref):
        left = _left_half()
        m_cur, m_prev = _block_masks()
        zero = jnp.zeros((CHUNK, CHUNK), BF16)
        ones = (jnp.where(left, 1.0, 0.0).astype(BF16), jnp.where(left, 0.0, 1.0).astype(BF16))

        def blk(b, carry):
            r0 = pl.multiple_of(b * CHUNK, CHUNK)
            rp = pl.multiple_of(jnp.maximum(b - 1, 0) * CHUNK, CHUNK)
            prev_ok = m_prev & (b > 0)
            sides = tuple(enumerate((left, ~left)))
            tiles, scores = [], []
            for hp in range(4):
                ls = slice(hp * 128, (hp + 1) * 128)
                qp = q_ref[pl.ds(r0, CHUNK), ls]
                kc = k_ref[pl.ds(r0, CHUNK), ls]
                kp = k_ref[pl.ds(rp, CHUNK), ls] if nblk > 1 else None
                tiles.append((ls, v_ref[pl.ds(r0, CHUNK), ls], v_ref[pl.ds(rp, CHUNK), ls] if nblk > 1 else None))
                for _, hm in sides:
                    qh = jnp.where(hm, qp, zero)
                    sc = jnp.where(m_cur, _dot_nt(qh, kc), NEG)
                    sp = jnp.where(prev_ok, _dot_nt(qh, kp), NEG) if nblk > 1 else None
                    scores.append((sc, sp))
            probs = []
            for sc, sp in scores:
                if nblk > 1:
                    m = jnp.max(jnp.maximum(sc, sp), axis=-1, keepdims=True)
                    pc = jnp.exp(sc - m)
                    pp = jnp.exp(sp - m)
                    probs.append((m, pc.astype(BF16), pp.astype(BF16), (pc + pp).astype(BF16)))
                else:
                    m = jnp.max(sc, axis=-1, keepdims=True)
                    pc = jnp.exp(sc - m).astype(BF16)
                    probs.append((m, pc, None, pc))
            for hp, (ls, vc, vp) in enumerate(tiles):
                acc = jnp.zeros((CHUNK, CHUNK), F32)
                den = jnp.zeros((CHUNK, CHUNK), F32)
                for side, hm in sides:
                    _, pc, pp, psum = probs[2 * hp + side]
                    acc = acc + _dot(pc, jnp.where(hm, vc, zero))
                    if nblk > 1:
                        acc = acc + _dot(pp, jnp.where(hm, vp, zero))
                    den = den + _dot(psum, ones[side])
                o_ref[pl.ds(r0, CHUNK), ls] = acc / den
                l_ref[pl.ds(r0, CHUNK), ls] = jnp.where(left, probs[2 * hp][0], probs[2 * hp + 1][0]) + jnp.log(den)
            return carry

        lax.fori_loop(0, nblk, blk, 0)

    spec = pl.BlockSpec((seg, ATTN_W), lambda r: (0, r))
    return _pcall(
        body, name=f"attn_fwd_d{dil}", grid=(dil,),
        in_specs=[spec, spec, spec], out_specs=[spec, spec],
        out_shape=[_sds((seg, dil * ATTN_W), F32), _sds((seg, dil * ATTN_W), F32)],
        args=(qv, kv, vv), comms=comms)


def _mix_out_fwd(o_list, l_list, sgu, x, w_out, g_attn, g_sgu, g_post, comms=()):
    def body(o1, o2, o3, l1, l2, l3, sgu_ref, x_ref, w_ref, ga_ref, gs_ref, gp_ref,
             attn_ref, mixed_ref, y_ref, x1_ref, lse1_ref, lse2_ref, lse3_ref, slabs_a, slabs_b):
        os = [o1[...], _load_view(o2, slabs_a, DILATIONS[1]), _load_view(o3, slabs_b, DILATIONS[2])]
        ls = [l1[...], _load_view(l2, slabs_a, DILATIONS[1]), _load_view(l3, slabs_b, DILATIONS[2])]
        m = jnp.maximum(jnp.maximum(ls[0], ls[1]), ls[2])
        es = [jnp.exp(l - m) for l in ls]
        den = es[0] + es[1] + es[2]
        attn = (es[0] * os[0] + es[1] * os[1] + es[2] * os[2]) / den
        attn_ref[...] = attn
        lse = m + jnp.log(den)
        lse1_ref[...] = lse
        _store_view(lse, lse2_ref, slabs_a, DILATIONS[1])
        _store_view(lse, lse3_ref, slabs_b, DILATIONS[2])
        ah, _ = _rms_stats(attn)
        sh, _ = _rms_stats(sgu_ref[...])
        mixed = jnp.concatenate([ah * ga_ref[...], sh * gs_ref[...]], axis=1).astype(BF16)
        mixed_ref[...] = mixed
        y = _dot(mixed[:, 0:OUT_S], w_ref[0])
        for s in range(1, N_SHARD):
            y = y + _dot(mixed[:, s * OUT_S:(s + 1) * OUT_S], w_ref[s])
        y_ref[...] = y
        yh, _ = _rms_stats(y)
        x1_ref[...] = x_ref[...] + yh * gp_ref[...]

    return _pcall(
        body, name="mix_out_fwd", grid=(SEQ // TM,),
        in_specs=[_view_rows(dil) for dil in DILATIONS] * 2
        + [_rows(512), _rows(D_MODEL), _resident((N_SHARD, OUT_S, D_MODEL)),
           _resident((1, 512)), _resident((1, 512)), _resident((1, D_MODEL))],
        out_specs=[_rows(512), _rows(D_MODEL), _rows(D_MODEL), _rows(D_MODEL)] + [_view_rows(dil) for dil in DILATIONS],
        out_shape=[_sds((SEQ, 512), F32), _sds((SEQ, D_MODEL), BF16), _sds((SEQ, D_MODEL), F32),
                   _sds((SEQ, D_MODEL), F32)] + [_view_shape(dil, F32) for dil in DILATIONS],
        scratch_shapes=[_slab_scratch(), _slab_scratch()],
        args=(*o_list, *l_list, sgu, x, w_out, g_attn, g_sgu, g_post), comms=comms)


def _ffn_fwd_bwd(x1, target, w_gate, w_up, w_down, g_pre, g_post, comms=()):
    def body(x1_ref, t_ref, wg_ref, wu_ref, wd_ref, gpf_ref, gpo_ref,
             h2_ref, a_ref, dg_ref, dup_ref, df_ref, dx1_ref, loss_ref, dgpf_ref, dgpo_ref, g_scr, up_scr):
        @pl.when(pl.program_id(0) == 0)
        def _():
            loss_ref[...] = jnp.zeros_like(loss_ref)
            dgpf_ref[...] = jnp.zeros_like(dgpf_ref)
            dgpo_ref[...] = jnp.zeros_like(dgpo_ref)

        x1 = x1_ref[...]
        gpf = gpf_ref[...]
        gpo = gpo_ref[...]
        xh, r = _rms_stats(x1)
        h2 = (xh * gpf).astype(BF16)
        h2_ref[...] = h2
        f = jnp.zeros((TM, D_MODEL), F32)
        for s in range(N_SHARD):
            g = _dot_nt(h2, wg_ref[s])
            up = _dot_nt(h2, wu_ref[s])
            g_scr[s] = g
            up_scr[s] = up
            a = (g * _sigmoid(g) * up).astype(BF16)
            a_ref[s] = a
            f = f + _dot(a, wd_ref[s])
        fh, rf = _rms_stats(f)
        diff = x1 + fh * gpo - t_ref[...]
        loss_ref[...] += jnp.sum(diff * diff, axis=0, keepdims=True)
        dout = diff * np.float32(1.0 / D_MODEL)
        df, dgpo = _rms_bwd(fh, rf, gpo, dout)
        dgpo_ref[...] += dgpo
        dfb = df.astype(BF16)
        df_ref[...] = dfb
        dh2 = jnp.zeros((TM, D_MODEL), F32)
        for s in range(N_SHARD):
            da = _dot_nt(dfb, wd_ref[s])
            g = g_scr[s]
            up = up_scr[s]
            sg = _sigmoid(g)
            dup = (da * (g * sg)).astype(BF16)
            dg = (da * up * (sg * (1.0 + g * (1.0 - sg)))).astype(BF16)
            dg_ref[s] = dg
            dup_ref[s] = dup
            dh2 = dh2 + _dot(dg, wg_ref[s]) + _dot(dup, wu_ref[s])
        dx, dgpf = _rms_bwd(xh, r, gpf, dh2)
        dgpf_ref[...] += dgpf
        dx1_ref[...] = dout + dx

    return _pcall(
        body, name="ffn_fwd_bwd", grid=(SEQ // TM,),
        in_specs=[_rows(D_MODEL), _rows(D_MODEL), _resident((N_SHARD, FF_S, D_MODEL)),
                  _resident((N_SHARD, FF_S, D_MODEL)), _resident((N_SHARD, FF_S, D_MODEL)),
                  _resident((1, D_MODEL)), _resident((1, D_MODEL))],
        out_specs=[_rows(D_MODEL), _rows3(N_SHARD, FF_S), _rows3(N_SHARD, FF_S), _rows3(N_SHARD, FF_S),
                   _rows(D_MODEL), _rows(D_MODEL), _acc(D_MODEL), _acc(D_MODEL), _acc(D_MODEL)],
        out_shape=[_sds((SEQ, D_MODEL), BF16), _sds((N_SHARD, SEQ, FF_S), BF16), _sds((N_SHARD, SEQ, FF_S), BF16),
                   _sds((N_SHARD, SEQ, FF_S), BF16), _sds((SEQ, D_MODEL), BF16), _sds((SEQ, D_MODEL), F32),
                   _sds((1, D_MODEL), F32), _sds((1, D_MODEL), F32), _sds((1, D_MODEL), F32)],
        scratch_shapes=[pltpu.VMEM((N_SHARD, TM, FF_S), F32), pltpu.VMEM((N_SHARD, TM, FF_S), F32)],
        args=(x1, target, w_gate, w_up, w_down, g_pre, g_post), comms=comms)


def _wgrad(a, b, a_spec, b_spec, out_block, name, comms=()):
    def body(a_ref, b_ref, o_ref):
        av = a_ref[0] if len(a_ref.shape) == 3 else a_ref[...]
        bv = b_ref[0] if len(b_ref.shape) == 3 else b_ref[...]
        o_ref[0] = _dot_tn(av, bv)

    return _pcall(
        body, name=name, grid=(N_SHARD,),
        in_specs=[a_spec, b_spec],
        out_specs=pl.BlockSpec((1,) + out_block, lambda s: (s, 0, 0)),
        out_shape=_sds((N_SHARD,) + out_block, F32),
        args=(a, b), comms=comms)


def _outproj_bwd(dx1, y, attn, sgu, w_out, g_post, g_attn, g_sgu, comms=()):
    def body(dx1_ref, y_ref, attn_ref, sgu_ref, w_ref, gp_ref, ga_ref, gs_ref,
             dy_ref, dsgu_ref, dgp_ref, dga_ref, dgs_ref, *rest):
        dattn_refs, delta_refs, (slabs_a, slabs_b) = rest[0:3], rest[3:6], rest[6:]

        @pl.when(pl.program_id(0) == 0)
        def _():
            dgp_ref[...] = jnp.zeros_like(dgp_ref)
            dga_ref[...] = jnp.zeros_like(dga_ref)
            dgs_ref[...] = jnp.zeros_like(dgs_ref)

        yh, ry = _rms_stats(y_ref[...])
        dy, dgp = _rms_bwd(yh, ry, gp_ref[...], dx1_ref[...])
        dgp_ref[...] += dgp
        dyb = dy.astype(BF16)
        dy_ref[...] = dyb
        dmixed = jnp.concatenate([_dot_nt(dyb, w_ref[s]) for s in range(N_SHARD)], axis=1)
        attn = attn_ref[...]
        ah, ra = _rms_stats(attn)
        dattn, dga = _rms_bwd(ah, ra, ga_ref[...], dmixed[:, 0:512])
        dga_ref[...] += dga
        sh, rs = _rms_stats(sgu_ref[...])
        dsgu, dgs = _rms_bwd(sh, rs, gs_ref[...], dmixed[:, 512:1024])
        dgs_ref[...] += dgs
        dsgu_ref[...] = dsgu
        la = lax.broadcasted_iota(jnp.int32, (ATTN_W, ATTN_W), 0) >> 6
        lb = lax.broadcasted_iota(jnp.int32, (ATTN_W, ATTN_W), 1) >> 6
        delta = _dot_exact(dattn * attn, (la == lb).astype(F32))
        dattn_refs[0][...] = dattn.astype(BF16)
        delta_refs[0][...] = delta
        for i, dil in enumerate(DILATIONS[1:]):
            _store_view(dattn, dattn_refs[i + 1], slabs_a, dil)
            _store_view(delta, delta_refs[i + 1], slabs_b, dil)

    return _pcall(
        body, name="outproj_bwd", grid=(SEQ // TM,),
        in_specs=[_rows(D_MODEL), _rows(D_MODEL), _rows(512), _rows(512), _resident((N_SHARD, OUT_S, D_MODEL)),
                  _resident((1, D_MODEL)), _resident((1, 512)), _resident((1, 512))],
        out_specs=[_rows(D_MODEL), _rows(512), _acc(D_MODEL), _acc(512), _acc(512)]
        + [_view_rows(dil) for dil in DILATIONS] * 2,
        out_shape=[_sds((SEQ, D_MODEL), BF16), _sds((SEQ, 512), F32),
                   _sds((1, D_MODEL), F32), _sds((1, 512), F32), _sds((1, 512), F32)]
        + [_view_shape(dil, BF16) for dil in DILATIONS] + [_view_shape(dil, F32) for dil in DILATIONS],
        scratch_shapes=[_slab_scratch(), _slab_scratch()],
        args=(dx1, y, attn, sgu, w_out, g_post, g_attn, g_sgu), comms=comms)


def _sgu_bwd(u, vs, dsgu, lg, lb, w_sp, b_t, comms=(), after=()):
    nsteps = SEQ // TM

    def body(u_ref, vs_ref, ds_ref, lg_ref, lb_ref, w_ref, bt_ref,
             du_ref, dvs_ref, dw_ref, db_ref, dlg_ref, dlb_ref, dbias_scr):
        i = pl.program_id(0)

        @pl.when(i == 0)
        def _():
            dw_ref[...] = jnp.zeros_like(dw_ref)
            dlg_ref[...] = jnp.zeros_like(dlg_ref)
            dlb_ref[...] = jnp.zeros_like(dlb_ref)
            dbias_scr[...] = jnp.zeros_like(dbias_scr)

        wm = _masked_spatial(w_ref)
        ones_g = _group_ones()
        bias_full = _dot_exact(bt_ref[...], ones_g)
        u = u_ref[...]
        vs = vs_ref[...]
        lg = lg_ref[...]
        gu, xh, rstd, vnb, mixed = _sgu_core(u, vs, lg, lb_ref[...], wm, bias_full)
        dsgu = ds_ref[...]
        du_ref[...] = (dsgu * mixed * _gelu_grad(u)).astype(BF16)
        dmixed = dsgu * gu
        left = _left_half()
        dvn_rows = []
        for c in range(TM // CHUNK):
            rs = slice(c * CHUNK, (c + 1) * CHUNK)
            dm_c = dmixed[rs, :]
            dbias_scr[...] += dm_c
            pieces = []
            for p in range(4):
                ls = slice(p * 128, (p + 1) * 128)
                dmp = dm_c[:, ls]
                vp = vnb[rs, ls]
                dmb = dmp.astype(BF16)
                zero = jnp.zeros_like(dmb)
                dw_ref[2 * p] += _dot_nt(jnp.where(left, dmb, zero), vp)
                dw_ref[2 * p + 1] += _dot_nt(jnp.where(left, zero, dmb), vp)
                pieces.append(jnp.where(left, _dot_tn(wm[2 * p], dmb), _dot_tn(wm[2 * p + 1], dmb)))
            dvn_rows.append(jnp.concatenate(pieces, axis=1))
        dvn = jnp.concatenate(dvn_rows, axis=0)
        dlg_ref[...] += jnp.sum(dvn * xh, axis=0, keepdims=True)
        dlb_ref[...] += jnp.sum(dvn, axis=0, keepdims=True)
        dxh = dvn * lg
        dgv = rstd * (dxh - jnp.mean(dxh, axis=-1, keepdims=True) - xh * jnp.mean(dxh * xh, axis=-1, keepdims=True))
        dvs_ref[...] = (dgv * _gelu_grad(vs)).astype(BF16)

        @pl.when(i == nsteps - 1)
        def _():
            row = lax.broadcasted_iota(jnp.int32, (CHUNK, CHUNK), 0)
            col = lax.broadcasted_iota(jnp.int32, (CHUNK, CHUNK), 1)
            for g in range(SGU_GROUPS):
                dw_ref[g] = jnp.where(col <= row, dw_ref[g], 0.0)
            db_ref[...] = lax.dot_general(ones_g, dbias_scr[...], (((1,), (1,)), ((), ())),
                                          preferred_element_type=F32, precision=lax.Precision.HIGHEST)

    return _pcall(
        body, name="sgu_bwd", grid=(nsteps,),
        in_specs=[_rows(SGU_W), _rows(SGU_W), _rows(SGU_W), _resident((1, SGU_W)), _resident((1, SGU_W)),
                  _resident((SGU_GROUPS, CHUNK, CHUNK)), _resident((CHUNK, SGU_GROUPS))],
        out_specs=[_rows(SGU_W), _rows(SGU_W), pl.BlockSpec((SGU_GROUPS, CHUNK, CHUNK), lambda i: (0, 0, 0)),
                   _acc(CHUNK, SGU_GROUPS), _acc(SGU_W), _acc(SGU_W)],
        out_shape=[_sds((SEQ, SGU_W), BF16), _sds((SEQ, SGU_W), BF16), _sds((SGU_GROUPS, CHUNK, CHUNK), F32),
                   _sds((SGU_GROUPS, CHUNK), F32), _sds((1, SGU_W), F32), _sds((1, SGU_W), F32)],
        scratch_shapes=[pltpu.VMEM((CHUNK, SGU_W), F32)],
        args=(u, vs, dsgu, lg, lb, w_sp, b_t), comms=comms, after=after)


def _attn_bwd(qv, kv, vv, dov, deltav, lsev, dil, comms=(), after=()):
    seg = SEQ // dil
    nblk = seg // CHUNK

    def body(q_ref, k_ref, v_ref, do_ref, dl_ref, lse_ref, dq_ref, dk_ref, dv_ref):
        left = _left_half()
        m_cur, m_prev = _block_masks()

        def blk(b, carry):
            r0 = pl.multiple_of(b * CHUNK, CHUNK)
            rp = pl.multiple_of(jnp.maximum(b - 1, 0) * CHUNK, CHUNK)
            prev_ok = m_prev & (b > 0)
            sides = tuple(enumerate((left, ~left)))
            zero = jnp.zeros((CHUNK, CHUNK), BF16)
            tiles, firsts = [], []
            for hp in range(4):
                ls = slice(hp * 128, (hp + 1) * 128)
                qp = q_ref[pl.ds(r0, CHUNK), ls]
                kc = k_ref[pl.ds(r0, CHUNK), ls]
                vc = v_ref[pl.ds(r0, CHUNK), ls]
                dop = do_ref[pl.ds(r0, CHUNK), ls]
                kp = k_ref[pl.ds(rp, CHUNK), ls] if nblk > 1 else None
                vp = v_ref[pl.ds(rp, CHUNK), ls] if nblk > 1 else None
                tiles.append((ls, kc, kp))
                for side, hm in sides:
                    qh = jnp.where(hm, qp, zero)
                    doh = jnp.where(hm, dop, zero)
                    cur = (_dot_nt(qh, kc), _dot_nt(doh, vc))
                    prev = (_dot_nt(qh, kp), _dot_nt(doh, vp)) if nblk > 1 else None
                    firsts.append((qh, doh, cur, prev))
            seconds = []
            for i, (qh, doh, cur, prev) in enumerate(firsts):
                hp, c0 = i // 2, (i % 2) * HEAD_DIM
                ls = slice(hp * 128, (hp + 1) * 128)
                lse_h = lse_ref[pl.ds(r0, CHUNK), ls][:, c0:c0 + 1]
                dl_h = dl_ref[pl.ds(r0, CHUNK), ls][:, c0:c0 + 1]
                pc = jnp.exp(jnp.where(m_cur, cur[0] - lse_h, NEG))
                out = [pc.astype(BF16), (pc * (cur[1] - dl_h)).astype(BF16), None, None]
                if nblk > 1:
                    pp = jnp.exp(jnp.where(prev_ok, prev[0] - lse_h, NEG))
                    out[2:] = [pp.astype(BF16), (pp * (prev[1] - dl_h)).astype(BF16)]
                seconds.append(out)
            for hp, (ls, kc, kp) in enumerate(tiles):
                dq = jnp.zeros((CHUNK, CHUNK), F32)
                dkc = jnp.zeros((CHUNK, CHUNK), F32)
                dvc = jnp.zeros((CHUNK, CHUNK), F32)
                dkp = jnp.zeros((CHUNK, CHUNK), F32)
                dvp = jnp.zeros((CHUNK, CHUNK), F32)
                for side, hm in sides:
                    qh, doh, _, _ = firsts[2 * hp + side]
                    pcb, dsc, ppb, dsp = seconds[2 * hp + side]
                    dq = dq + _dot(dsc, jnp.where(hm, kc, zero))
                    dkc = dkc + _dot_tn(dsc, qh)
                    dvc = dvc + _dot_tn(pcb, doh)
                    if nblk > 1:
                        dq = dq + _dot(dsp, jnp.where(hm, kp, zero))
                        dkp = dkp + _dot_tn(dsp, qh)
                        dvp = dvp + _dot_tn(ppb, doh)
                dq_ref[pl.ds(r0, CHUNK), ls] = dq
                dk_ref[pl.ds(r0, CHUNK), ls] = dkc
                dv_ref[pl.ds(r0, CHUNK), ls] = dvc
                if nblk > 1:
                    @pl.when(b > 0)
                    def _():
                        dk_ref[pl.ds(rp, CHUNK), ls] += dkp
                        dv_ref[pl.ds(rp, CHUNK), ls] += dvp
            return carry

        lax.fori_loop(0, nblk, blk, 0)

    spec = pl.BlockSpec((seg, ATTN_W), lambda r: (0, r))
    return _pcall(
        body, name=f"attn_bwd_d{dil}", grid=(dil,),
        in_specs=[spec] * 6, out_specs=[spec] * 3,
        out_shape=[_sds((seg, dil * ATTN_W), F32)] * 3,
        args=(qv, kv, vv, dov, deltav, lsev), comms=comms, after=after)


def _inproj_bwd(dqs, dks, dvs, du, dvs_sgu, pos, x, dx1, w_in, g_pre, comms=()):
    def body(dq1, dq2, dq3, dk1, dk2, dk3, dv1, dv2, dv3, du_ref, dvs_ref, pos_ref, x_ref, dx1_ref, w_ref, g_ref,
             dproj_ref, gx_ref, dg_ref, slabs_a, slabs_b):
        @pl.when(pl.program_id(0) == 0)
        def _():
            dg_ref[...] = jnp.zeros_like(dg_ref)

        def total(r1, r2, r3):
            return r1[...] + _load_view(r2, slabs_a, DILATIONS[1]) + _load_view(r3, slabs_b, DILATIONS[2])

        tabs = _rot_tables(pos_ref[...])
        dproj_ref[:, 0:512] = _rope_bwd(total(dq1, dq2, dq3) * np.float32(ATTN_SCALE), tabs).astype(BF16)
        dproj_ref[:, 512:1024] = _rope_bwd(total(dk1, dk2, dk3), tabs).astype(BF16)
        dproj_ref[:, 1024:1536] = total(dv1, dv2, dv3).astype(BF16)
        dproj_ref[:, 1536:2048] = du_ref[...]
        dproj_ref[:, 2048:2560] = dvs_ref[...]
        dh = jnp.zeros((TM, D_MODEL), F32)
        for s in range(N_SHARD):
            dh = dh + _dot_nt(dproj_ref[:, s * IN_S:(s + 1) * IN_S], w_ref[s])
        g = g_ref[...]
        xh, r = _rms_stats(x_ref[...])
        dx, dg = _rms_bwd(xh, r, g, dh)
        dg_ref[...] += dg
        gx_ref[...] = dx1_ref[...] + dx

    return _pcall(
        body, name="inproj_bwd", grid=(SEQ // TM,),
        in_specs=[_view_rows(dil) for dil in DILATIONS] * 3
        + [_rows(512), _rows(512), _rows(1), _rows(D_MODEL), _rows(D_MODEL),
           _resident((N_SHARD, D_MODEL, IN_S)), _resident((1, D_MODEL))],
        out_specs=[_rows(PROJ_W), _rows(D_MODEL), _acc(D_MODEL)],
        out_shape=[_sds((SEQ, PROJ_W), BF16), _sds((SEQ, D_MODEL), F32), _sds((1, D_MODEL), F32)],
        scratch_shapes=[_slab_scratch(), _slab_scratch()],
        args=(*dqs, *dks, *dvs, du, dvs_sgu, pos, x, dx1, w_in, g_pre), comms=comms)


def _to_view(a, dil):
    return a if dil == 1 else a.reshape(SEQ // dil, dil * a.shape[1])


def _from_view(a, dil):
    return a if dil == 1 else a.reshape(SEQ, a.shape[1] // dil)


def _local_step(x, pos, target, w_in, w_out, w_gate, w_up, w_down, small):
    b_t = small["sgu_b_spatial"].T
    h, u, vs, *qkv = _inproj_fwd(x, pos, small["pre_mix_norm"], w_in)
    sgu = _sgu_fwd(u, vs, small["sgu_ln_gain"], small["sgu_ln_bias"], small["sgu_w_spatial"], b_t)
    views = [tuple(qkv[3 * i:3 * i + 3]) for i in range(len(DILATIONS))]
    o_list, l_list = [], []
    for dil, (qv, kv, vv) in zip(DILATIONS, views):
        o, l = _attn_fwd(qv, kv, vv, dil)
        o_list.append(o)
        l_list.append(l)
    attn, mixed, y, x1, *lses = _mix_out_fwd(o_list, l_list, sgu, x, w_out, small["attn_out_norm"],
                                             small["sgu_out_norm"], small["post_mix_norm"])
    h2, a, dg, dup, df, dx1, loss_cols, d_pre_ffn, d_post_ffn = _ffn_fwd_bwd(
        x1, target, w_gate, w_up, w_down, small["pre_ffn_norm"], small["post_ffn_norm"])

    full_tok = pl.BlockSpec((SEQ, D_MODEL), lambda s: (0, 0), pipeline_mode=pl.Buffered(1))
    ff_tok = pl.BlockSpec((1, SEQ, FF_S), lambda s: (s, 0, 0))
    gw_gate = _wgrad(dg, h2, ff_tok, full_tok, (FF_S, D_MODEL), "wgrad_gate")
    gw_up = _wgrad(dup, h2, ff_tok, full_tok, (FF_S, D_MODEL), "wgrad_up")
    gw_down = _wgrad(a, df, ff_tok, full_tok, (FF_S, D_MODEL), "wgrad_down")

    dy, dsgu, d_post_mix, d_attn_norm, d_sgu_norm, *dviews = _outproj_bwd(
        dx1, y, attn, sgu, w_out, small["post_mix_norm"], small["attn_out_norm"], small["sgu_out_norm"])
    gw_out = _wgrad(mixed, dy, pl.BlockSpec((SEQ, OUT_S), lambda s: (0, s)), full_tok, (OUT_S, D_MODEL), "wgrad_out")
    du, dvs_sgu, d_w_sp, d_b_sp, d_ln_gain, d_ln_bias = _sgu_bwd(
        u, vs, dsgu, small["sgu_ln_gain"], small["sgu_ln_bias"], small["sgu_w_spatial"], b_t)

    dqs, dks, dvs = [], [], []
    for i, (dil, (qv, kv, vv)) in enumerate(zip(DILATIONS, views)):
        dq, dk, dv = _attn_bwd(qv, kv, vv, dviews[i], dviews[3 + i], lses[i], dil)
        dqs.append(dq)
        dks.append(dk)
        dvs.append(dv)
    dproj, grad_x, d_pre_mix = _inproj_bwd(dqs, dks, dvs, du, dvs_sgu, pos, x, dx1, w_in, small["pre_mix_norm"])
    gw_in = _wgrad(h, dproj, full_tok, pl.BlockSpec((SEQ, IN_S), lambda s: (0, s)), (D_MODEL, IN_S), "wgrad_in")

    small_grads = {
        "pre_mix_norm": d_pre_mix, "sgu_ln_gain": d_ln_gain, "sgu_ln_bias": d_ln_bias, "sgu_w_spatial": d_w_sp,
        "sgu_b_spatial": d_b_sp, "attn_out_norm": d_attn_norm, "sgu_out_norm": d_sgu_norm,
        "post_mix_norm": d_post_mix, "pre_ffn_norm": d_pre_ffn, "post_ffn_norm": d_post_ffn,
    }
    return loss_cols, grad_x, (gw_in, gw_out, gw_gate, gw_up, gw_down), small_grads


def _coords():
    return lax.axis_index("x"), lax.axis_index("y"), lax.axis_index("c")


def _other_chips(x, y):
    return [(1 - x, y), (x, 1 - y), (1 - x, 1 - y)]


def _comm_call(body, name, n_in, out_shape, scratch_shapes):
    return pl.pallas_call(
        body, name=name, in_specs=[ANY] * n_in, out_specs=[ANY] * len(out_shape), out_shape=out_shape,
        scratch_shapes=scratch_shapes,
        compiler_params=pltpu.CompilerParams(has_side_effects=True),
    )


def _gather_weights(shards):
    n = len(shards)
    halves = [s.reshape(2, s.shape[0] // 2, s.shape[1]) for s in shards]

    def body(*refs):
        ins, outs = refs[:n], refs[n:2 * n]
        send_sems, recv_sems = refs[2 * n:]
        x, y, c = _coords()
        s_me = 2 * x + y
        chips = _other_chips(x, y)
        sibling = (x, y, 1 - c)

        def copy(k, w, shard, cc, to):
            src = ins[w].at[cc] if shard is None else outs[w].at[shard, cc]
            dst = outs[w].at[s_me if shard is None else shard, cc]
            return pltpu.make_async_remote_copy(src_ref=src, dst_ref=dst, send_sem=send_sems.at[k],
                                                recv_sem=recv_sems.at[k], device_id=to, device_id_type=MESH)

        first = [copy(j * n + w, w, None, c, (cx, cy, c)) for j, (cx, cy) in enumerate(chips) for w in range(n)]
        for cp in first:
            cp.start()
        passed = []
        for j, (cx, cy) in enumerate(chips):
            for w in range(n):
                copy(j * n + w, w, 2 * cx + cy, c, (x, y, c)).wait_recv()
                fw = copy((3 + j) * n + w, w, 2 * cx + cy, c, sibling)
                fw.start()
                passed.append(fw)
        for j, (cx, cy) in enumerate(chips):
            for w in range(n):
                copy((3 + j) * n + w, w, 2 * cx + cy, 1 - c, (x, y, c)).wait_recv()
        for cp in first + passed:
            cp.wait_send()

    out_shape = [_sds((N_SHARD,) + h.shape, h.dtype) for h in halves]
    scratch = [pltpu.SemaphoreType.DMA((6 * n,)), pltpu.SemaphoreType.DMA((6 * n,))]
    full = _comm_call(body, "comm_gather_weights", n, out_shape, scratch)(*halves)
    s_me = 2 * lax.axis_index("x") + lax.axis_index("y")
    full = [lax.dynamic_update_slice(f, h[None], (s_me, 0, 0, 0)) for f, h in zip(full, halves)]
    return [f.reshape((N_SHARD,) + s.shape) for f, s in zip(full, shards)]


def _rs_to_sibling(gws):
    n = len(gws)

    def body(*refs):
        ins, outs = refs[:n], refs[n:2 * n]
        send_sems, recv_sems = refs[2 * n:]
        x, y, c = _coords()
        copies = []
        for w in range(n):
            hw = gws[w].shape[1] // 2
            copies.append(pltpu.make_async_remote_copy(
                src_ref=ins[w].at[:, pl.ds((1 - c) * hw, hw), :], dst_ref=outs[w], send_sem=send_sems.at[w],
                recv_sem=recv_sems.at[w], device_id=(x, y, 1 - c), device_id_type=MESH))
        for cp in copies:
            cp.start()
        for cp in copies:
            cp.wait()

    out_shape = [_sds((N_SHARD, g.shape[1] // 2, g.shape[2]), g.dtype) for g in gws]
    scratch = [pltpu.SemaphoreType.DMA((n,)), pltpu.SemaphoreType.DMA((n,))]
    return _comm_call(body, "comm_rs_sibling", n, out_shape, scratch)(*gws)


def _rs_chip_sum(gw, recv, core):
    _, rows, cols = gw.shape
    hw = rows // 2

    def body(c_ref, g_ref, r_ref, o_ref):
        o_ref[...] = (g_ref[...] + r_ref[...]).astype(BF16)

    return pl.pallas_call(
        body, name="rs_chip_sum",
        grid_spec=pltpu.PrefetchScalarGridSpec(
            num_scalar_prefetch=1, grid=(N_SHARD,),
            in_specs=[pl.BlockSpec((1, hw, cols), lambda s, c_ref: (s, c_ref[0], 0)),
                      pl.BlockSpec((1, hw, cols), lambda s, c_ref: (s, 0, 0))],
            out_specs=pl.BlockSpec((1, hw, cols), lambda s, c_ref: (s, 0, 0))),
        out_shape=_sds((N_SHARD, hw, cols), BF16),
        compiler_params=_seq_params(),
    )(core, gw, recv)


def _rs_between_chips(pbs):
    n = len(pbs)

    def body(*refs):
        ins, outs = refs[:n], refs[n:2 * n]
        send_sems, recv_sems = refs[2 * n:]
        x, y, c = _coords()
        copies = []
        for j, (cx, cy) in enumerate(_other_chips(x, y)):
            for w in range(n):
                copies.append(pltpu.make_async_remote_copy(
                    src_ref=ins[w].at[2 * cx + cy], dst_ref=outs[w].at[j], send_sem=send_sems.at[j * n + w],
                    recv_sem=recv_sems.at[j * n + w], device_id=(cx, cy, c), device_id_type=MESH))
        for cp in copies:
            cp.start()
        for cp in copies:
            cp.wait()

    out_shape = [_sds((3,) + p.shape[1:], p.dtype) for p in pbs]
    scratch = [pltpu.SemaphoreType.DMA((3 * n,)), pltpu.SemaphoreType.DMA((3 * n,))]
    return _comm_call(body, "comm_rs_chips", n, out_shape, scratch)(*pbs)


def _rs_final_sum(gw, recv_sib, recv_chips, shard_core):
    _, rows, cols = gw.shape
    hw = rows // 2

    def body(sc_ref, g_ref, r_ref, rc_ref, o_ref):
        acc = g_ref[0] + r_ref[0]
        for j in range(3):
            acc = acc + rc_ref[j].astype(F32)
        o_ref[0] = acc

    return pl.pallas_call(
        body, name="rs_final_sum",
        grid_spec=pltpu.PrefetchScalarGridSpec(
            num_scalar_prefetch=1, grid=(1,),
            in_specs=[pl.BlockSpec((1, hw, cols), lambda i, sc: (sc[0], sc[1], 0)),
                      pl.BlockSpec((1, hw, cols), lambda i, sc: (sc[0], 0, 0)),
                      pl.BlockSpec((3, hw, cols), lambda i, sc: (0, 0, 0))],
            out_specs=pl.BlockSpec((1, hw, cols), lambda i, sc: (sc[1], 0, 0))),
        out_shape=_sds((2, hw, cols), F32),
        compiler_params=_seq_params(),
    )(shard_core, gw, recv_sib, recv_chips)


def _rs_join_halves(halves):
    n = len(halves)

    def body(*refs):
        bufs = refs[n:2 * n]
        send_sems, recv_sems = refs[2 * n:]
        x, y, c = _coords()
        remote = [pltpu.make_async_remote_copy(
            src_ref=bufs[w].at[c], dst_ref=bufs[w].at[c], send_sem=send_sems.at[w], recv_sem=recv_sems.at[w],
            device_id=(x, y, 1 - c), device_id_type=MESH) for w in range(n)]
        for cp in remote:
            cp.start()
        for w in range(n):
            remote[w].wait_send()
            pltpu.make_async_remote_copy(
                src_ref=bufs[w].at[c], dst_ref=bufs[w].at[1 - c], send_sem=send_sems.at[w],
                recv_sem=recv_sems.at[w], device_id=(x, y, c), device_id_type=MESH).wait_recv()

    joined = pl.pallas_call(
        body, name="comm_rs_join", in_specs=[ANY] * n, out_specs=[ANY] * n,
        out_shape=[_sds(h.shape, h.dtype) for h in halves], input_output_aliases={w: w for w in range(n)},
        scratch_shapes=[pltpu.SemaphoreType.DMA((n,)), pltpu.SemaphoreType.DMA((n,))],
        compiler_params=pltpu.CompilerParams(has_side_effects=True),
    )(*halves)
    return [j.reshape(2 * h.shape[1], h.shape[2]) for j, h in zip(joined, halves)]


def _allreduce_small(buf):
    rows, cols = buf.shape

    def body(in_ref, out_ref, slots, send_sems, recv_sems):
        x, y, c = _coords()
        me = 4 * x + 2 * y + c
        copies, peers = [], []
        for k in range(1, 8):
            px = 1 - x if (k >> 2) & 1 else x
            py = 1 - y if (k >> 1) & 1 else y
            pc = 1 - c if k & 1 else c
            peers.append(4 * px + 2 * py + pc)
            copies.append(pltpu.make_async_remote_copy(
                src_ref=in_ref, dst_ref=slots.at[me], send_sem=send_sems.at[k - 1], recv_sem=recv_sems.at[k - 1],
                device_id=(px, py, pc), device_id_type=MESH))
        for cp in copies:
            cp.start()
        slots[me] = in_ref[...]
        for k in range(7):
            pltpu.make_async_remote_copy(
                src_ref=in_ref, dst_ref=slots.at[peers[k]], send_sem=send_sems.at[k], recv_sem=recv_sems.at[k],
                device_id=(x, y, c), device_id_type=MESH).wait_recv()
        for cp in copies:
            cp.wait_send()
        acc = slots[0]
        for i in range(1, 8):
            acc = acc + slots[i]
        out_ref[...] = acc

    vmem = pl.BlockSpec(memory_space=pltpu.VMEM)
    return pl.pallas_call(
        body, name="comm_allreduce_small", in_specs=[vmem], out_specs=vmem, out_shape=_sds((rows, cols), F32),
        scratch_shapes=[pltpu.VMEM((8, rows, cols), F32), pltpu.SemaphoreType.DMA((7,)), pltpu.SemaphoreType.DMA((7,))],
        compiler_params=pltpu.CompilerParams(has_side_effects=True, vmem_limit_bytes=VMEM_LIMIT),
    )(buf)


def _adamw(w, g, m, v, block_rows, name, after=()):
    rows, cols = w.shape

    def body(w_ref, g_ref, m_ref, v_ref, *rest):
        d_ref, nm_ref, nv_ref = rest[len(after):]
        g = g_ref[...]
        m = ADAM_B1 * m_ref[...] + (1.0 - ADAM_B1) * g
        v = ADAM_B2 * v_ref[...] + (1.0 - ADAM_B2) * (g * g)
        m_hat = m / (1.0 - ADAM_B1 ** ADAM_STEP)
        v_hat = v / (1.0 - ADAM_B2 ** ADAM_STEP)
        d_ref[...] = -ADAM_LR * (m_hat / (jnp.sqrt(v_hat) + ADAM_EPS) + ADAM_WD * w_ref[...])
        nm_ref[...] = m
        nv_ref[...] = v

    spec = pl.BlockSpec((block_rows, cols), lambda i: (i, 0))
    return pl.pallas_call(
        body, name=name, grid=(rows // block_rows,), in_specs=[spec] * 4 + [ANY] * len(after), out_specs=[spec] * 3,
        out_shape=[_sds((rows, cols), F32)] * 3,
        compiler_params=_seq_params(),
    )(w, g, m, v, *after)


WEIGHTS = ("pre_mix_norm", "w_in", "sgu_ln_gain", "sgu_ln_bias", "sgu_w_spatial", "sgu_b_spatial", "attn_out_norm",
           "sgu_out_norm", "w_out", "post_mix_norm", "pre_ffn_norm", "w_gate", "w_up", "w_down", "post_ffn_norm")
BIG = ("w_in", "w_out", "w_gate", "w_up", "w_down")
BIG_ADAM_ROWS = {"w_in": 256, "w_out": 128, "w_gate": 352, "w_up": 352, "w_down": 352}
SMALL = ("pre_mix_norm", "post_mix_norm", "pre_ffn_norm", "post_ffn_norm", "sgu_ln_gain", "sgu_ln_bias",
         "attn_out_norm", "sgu_out_norm", "sgu_w_spatial", "sgu_b_spatial")


def _pack_small(d):
    flat = [d[n].reshape(-1) for n in SMALL]
    used = sum(f.shape[0] for f in flat)
    flat.append(jnp.zeros((SMALL_ROWS * 1024 - used,), F32))
    return jnp.concatenate(flat).reshape(SMALL_ROWS, 1024)


def _unpack_small(buf, shapes):
    flat = buf.reshape(-1)
    out, off = {}, 0
    for n in SMALL:
        size = int(np.prod(shapes[n]))
        out[n] = flat[off:off + size].reshape(shapes[n])
        off += size
    return out


def _kernel_unoverlapped(x, positions, pre_mix_norm, w_in, sgu_ln_gain, sgu_ln_bias, sgu_w_spatial, sgu_b_spatial, attn_out_norm, sgu_out_norm, w_out, post_mix_norm, pre_ffn_norm, w_gate, w_up, w_down, post_ffn_norm, loss_target, m_pre_mix_norm, m_w_in, m_sgu_ln_gain, m_sgu_ln_bias, m_sgu_w_spatial, m_sgu_b_spatial, m_attn_out_norm, m_sgu_out_norm, m_w_out, m_post_mix_norm, m_pre_ffn_norm, m_w_gate, m_w_up, m_w_down, m_post_ffn_norm, v_pre_mix_norm, v_w_in, v_sgu_ln_gain, v_sgu_ln_bias, v_sgu_w_spatial, v_sgu_b_spatial, v_attn_out_norm, v_sgu_out_norm, v_w_out, v_post_mix_norm, v_pre_ffn_norm, v_w_gate, v_w_up, v_w_down, v_post_ffn_norm):
    a = dict(locals())
    cx, cy, cc = _coords()
    core = jnp.stack([cc]).astype(jnp.int32)
    shard_core = jnp.stack([2 * cx + cy, cc]).astype(jnp.int32)

    full = _gather_weights([a[n][0].astype(BF16) for n in BIG])
    small = {n: (a[n][0] if a[n].ndim > 2 else a[n]) for n in SMALL}
    loss_cols, grad_x, gws, small_grads = _local_step(
        x[0], positions.reshape(SEQ, 1), loss_target[0], *full, small)
    loss = lax.psum(jnp.sum(loss_cols) * np.float32(0.5 / D_MODEL), ("x", "y", "c"))

    recv_sib = _rs_to_sibling(list(gws))
    chip_part = [_rs_chip_sum(g, r, core) for g, r in zip(gws, recv_sib)]
    recv_chips = _rs_between_chips(chip_part)
    halves = [_rs_final_sum(g, r, rc, shard_core) for g, r, rc in zip(gws, recv_sib, recv_chips)]
    big_grads = dict(zip(BIG, _rs_join_halves(halves)))

    shapes = {n: a[n].shape for n in SMALL}
    small_sum = _allreduce_small(_pack_small(small_grads))

    grads, deltas, new_m, new_v = {}, {}, {}, {}
    for n in BIG:
        grads[n] = big_grads[n][None]
        d, nm, nv = _adamw(a[n][0], big_grads[n], a["m_" + n][0], a["v_" + n][0], BIG_ADAM_ROWS[n], "adamw_" + n)
        deltas[n], new_m[n], new_v[n] = d[None], nm[None], nv[None]
    d, nm, nv = _adamw(_pack_small({n: a[n] for n in SMALL}), small_sum, _pack_small({n: a["m_" + n] for n in SMALL}),
                       _pack_small({n: a["v_" + n] for n in SMALL}), SMALL_ROWS, "adamw_small")
    grads.update(_unpack_small(small_sum, shapes))
    deltas.update(_unpack_small(d, shapes))
    new_m.update(_unpack_small(nm, shapes))
    new_v.update(_unpack_small(nv, shapes))
    return (loss, grad_x[None], *[grads[n] for n in WEIGHTS], *[deltas[n] for n in WEIGHTS],
            *[new_m[n] for n in WEIGHTS], *[new_v[n] for n in WEIGHTS])


def _remote(src, dst, send_sem, recv_sem, to):
    return pltpu.make_async_remote_copy(src_ref=src, dst_ref=dst, send_sem=send_sem, recv_sem=recv_sem,
                                        device_id=to, device_id_type=MESH)


def _halves(a):
    *lead, rows, cols = a.shape
    return a.reshape(*lead, 2, rows // 2, cols)


def _gather_ici(shards):
    n = len(shards)

    def desc(ins, outs, ss, rs, j, w, landed):
        x, y, c = _coords()
        cx, cy = _other_chips(x, y)[j]
        shard = 2 * cx + cy if landed else 2 * x + y
        return _remote(ins[w].at[c], outs[w].at[shard, c], ss.at[j * n + w], rs.at[j * n + w], (cx, cy, c))

    def start(ins, outs, ss, rs):
        for j in range(3):
            for w in range(n):
                desc(ins, outs, ss, rs, j, w, False).start()

    def finish(ins, outs, ss, rs):
        for j in range(3):
            for w in range(n):
                desc(ins, outs, ss, rs, j, w, True).wait_recv()
                desc(ins, outs, ss, rs, j, w, False).wait_send()

    return _Comm(shards, [_sds((N_SHARD,) + s.shape, s.dtype) for s in shards], 3 * n, start, finish)


def _gather_pass(fulls):
    n = len(fulls)

    def desc(bufs, ss, rs, j, w, landed):
        x, y, c = _coords()
        cx, cy = _other_chips(x, y)[j]
        shard = 2 * cx + cy
        return _remote(bufs[w].at[shard, c], bufs[w].at[shard, 1 - c if landed else c],
                       ss.at[j * n + w], rs.at[j * n + w], (x, y, 1 - c))

    def start(ins, outs, ss, rs):
        for j in range(3):
            for w in range(n):
                desc(outs, ss, rs, j, w, False).start()

    def finish(ins, outs, ss, rs):
        for j in range(3):
            for w in range(n):
                desc(outs, ss, rs, j, w, True).wait_recv()
                desc(outs, ss, rs, j, w, False).wait_send()

    return _Comm(fulls, [_sds(f.shape, f.dtype) for f in fulls], 3 * n, start, finish, aliased=True)


def _rs_sibling(gws):
    n = len(gws)

    def desc(ins, outs, ss, rs, w):
        x, y, c = _coords()
        return _remote(ins[w].at[:, 1 - c], outs[w], ss.at[w], rs.at[w], (x, y, 1 - c))

    def start(ins, outs, ss, rs):
        for w in range(n):
            desc(ins, outs, ss, rs, w).start()

    def finish(ins, outs, ss, rs):
        for w in range(n):
            desc(ins, outs, ss, rs, w).wait()

    out_shape = [_sds((N_SHARD, g.shape[1] // 2, g.shape[2]), g.dtype) for g in gws]
    return _Comm([_halves(g) for g in gws], out_shape, n, start, finish)


def _rs_chips(pbs):
    n = len(pbs)

    def desc(ins, outs, ss, rs, j, w):
        x, y, c = _coords()
        cx, cy = _other_chips(x, y)[j]
        return _remote(ins[w].at[2 * cx + cy], outs[w].at[j], ss.at[j * n + w], rs.at[j * n + w], (cx, cy, c))

    def start(ins, outs, ss, rs):
        for j in range(3):
            for w in range(n):
                desc(ins, outs, ss, rs, j, w).start()

    def finish(ins, outs, ss, rs):
        for j in range(3):
            for w in range(n):
                desc(ins, outs, ss, rs, j, w).wait()

    return _Comm(pbs, [_sds((3,) + p.shape[1:], p.dtype) for p in pbs], 3 * n, start, finish)


def _rs_join(halves):
    n = len(halves)

    def desc(bufs, ss, rs, w, landed):
        x, y, c = _coords()
        return _remote(bufs[w].at[c], bufs[w].at[1 - c if landed else c], ss.at[w], rs.at[w], (x, y, 1 - c))

    def start(ins, outs, ss, rs):
        for w in range(n):
            desc(outs, ss, rs, w, False).start()

    def finish(ins, outs, ss, rs):
        for w in range(n):
            desc(outs, ss, rs, w, True).wait_recv()
            desc(outs, ss, rs, w, False).wait_send()

    return _Comm(halves, [_sds(h.shape, h.dtype) for h in halves], n, start, finish, aliased=True)


def _small_exchange(buf):
    def desc(ins, outs, ss, rs, k, landed):
        x, y, c = _coords()
        px = 1 - x if (k >> 2) & 1 else x
        py = 1 - y if (k >> 1) & 1 else y
        pc = 1 - c if k & 1 else c
        slot = 4 * px + 2 * py + pc if landed else 4 * x + 2 * y + c
        return _remote(ins[0], outs[0].at[slot], ss.at[k - 1], rs.at[k - 1], (px, py, pc))

    def start(ins, outs, ss, rs):
        for k in range(1, 8):
            desc(ins, outs, ss, rs, k, False).start()

    def finish(ins, outs, ss, rs):
        for k in range(1, 8):
            desc(ins, outs, ss, rs, k, True).wait_recv()
            desc(ins, outs, ss, rs, k, False).wait_send()

    return _Comm([buf], [_sds((8,) + buf.shape, buf.dtype)], 7, start, finish)


HBM = pl.BlockSpec(memory_space=pltpu.HBM)
SEM = pl.BlockSpec(memory_space=pltpu.SEMAPHORE)
DATAFLOW = pltpu.SideEffectType.DATAFLOW_SIDE_EFFECTING


def _split_start(name, comm, lands, after):
    srcs = [pltpu.with_memory_space_constraint(s, pltpu.HBM) for s in comm.args]
    lands = [pltpu.with_memory_space_constraint(b, pltpu.HBM) for b in lands]
    ns, nb = len(srcs), len(lands)

    def body(*refs):
        send_sems, recv_sems = refs[ns + nb + 1], refs[ns + nb + 2]
        comm.start(refs[:ns], refs[ns:ns + nb], send_sems, recv_sems)
        refs[-1][...] = jnp.zeros_like(refs[-1])

    res = pl.pallas_call(
        body, name=name,
        out_shape=(pltpu.SemaphoreType.DMA((comm.n_sems,)), pltpu.SemaphoreType.DMA((comm.n_sems,)),
                   *[pltpu.HBM(b.shape, b.dtype) for b in srcs + lands], _sds((8, 128), F32)),
        in_specs=[HBM] * (ns + nb) + [ANY],
        out_specs=(SEM, SEM, *[HBM] * (ns + nb), pl.BlockSpec(memory_space=pltpu.VMEM)),
        input_output_aliases={i: 2 + i for i in range(ns + nb)},
        compiler_params=pltpu.CompilerParams(has_side_effects=DATAFLOW),
    )(*srcs, *lands, after)
    return res[0], res[1], list(res[2:2 + ns]), list(res[2 + ns:2 + ns + nb]), res[-1]


def _split_wait(name, comm, send_sems, recv_sems, srcs, lands, after):
    ns, nb = len(srcs), len(lands)

    def body(*refs):
        comm.finish(refs[:ns], refs[ns:ns + nb], refs[ns + nb], refs[ns + nb + 1])

    res = pl.pallas_call(
        body, name=name,
        out_shape=tuple(pltpu.HBM(b.shape, b.dtype) for b in srcs + lands),
        in_specs=[HBM] * (ns + nb) + [SEM, SEM, ANY], out_specs=tuple([HBM] * (ns + nb)),
        input_output_aliases={i: i for i in range(ns + nb)},
        compiler_params=pltpu.CompilerParams(has_side_effects=DATAFLOW),
    )(*srcs, *lands, send_sems, recv_sems, after)
    return list(res[ns:])


LOSS_ROW = "loss_cols"
SMALL_EARLY = ("post_mix_norm", "pre_ffn_norm", "post_ffn_norm", "sgu_ln_gain", "sgu_ln_bias", "attn_out_norm",
               "sgu_out_norm", "sgu_w_spatial", "sgu_b_spatial", LOSS_ROW)
SMALL_LATE = ("pre_mix_norm",)


def _pack(d, names, rows):
    flat = [d[n].reshape(-1) for n in names]
    used = sum(f.shape[0] for f in flat)
    flat.append(jnp.zeros((rows * 1024 - used,), F32))
    return jnp.concatenate(flat).reshape(rows, 1024)


def _unpack(buf, names, shapes):
    flat = buf.reshape(-1)
    out, off = {}, 0
    for n in names:
        size = int(np.prod(shapes[n]))
        out[n] = flat[off:off + size].reshape(shapes[n])
        off += size
    return out


def _comm_only(name, comms):
    return _pcall(lambda: None, name=name, grid=(1,), in_specs=[], out_specs=[], out_shape=[], args=(),
                  comms=comms)[1]


def _adam_math(w, g, m, v):
    m = ADAM_B1 * m + (1.0 - ADAM_B1) * g
    v = ADAM_B2 * v + (1.0 - ADAM_B2) * (g * g)
    m_hat = m / (1.0 - ADAM_B1 ** ADAM_STEP)
    v_hat = v / (1.0 - ADAM_B2 ** ADAM_STEP)
    return -ADAM_LR * (m_hat / (jnp.sqrt(v_hat) + ADAM_EPS) + ADAM_WD * w), m, v


def _adamw_small(own, slots, w, m, v, me):
    rows, cols = own.shape

    def body(me_ref, own_ref, slots_ref, w_ref, m_ref, v_ref, g_ref, d_ref, nm_ref, nv_ref):
        own_v = own_ref[...]
        g = jnp.where(me_ref[0] == 0, own_v, slots_ref[0])
        for i in range(1, 8):
            g = g + jnp.where(me_ref[0] == i, own_v, slots_ref[i])
        g_ref[...] = g
        d_ref[...], nm_ref[...], nv_ref[...] = _adam_math(w_ref[...], g, m_ref[...], v_ref[...])

    flat = pl.BlockSpec((rows, cols), lambda i, me_ref: (0, 0))
    return pl.pallas_call(
        body, name="adamw_small",
        grid_spec=pltpu.PrefetchScalarGridSpec(
            num_scalar_prefetch=1, grid=(1,),
            in_specs=[flat, pl.BlockSpec((8, rows, cols), lambda i, me_ref: (0, 0, 0)), flat, flat, flat],
            out_specs=[flat] * 4),
        out_shape=[_sds((rows, cols), F32)] * 4,
        compiler_params=_seq_params(),
    )(me, own, slots, w, m, v)


def kernel(x, positions, pre_mix_norm, w_in, sgu_ln_gain, sgu_ln_bias, sgu_w_spatial, sgu_b_spatial, attn_out_norm, sgu_out_norm, w_out, post_mix_norm, pre_ffn_norm, w_gate, w_up, w_down, post_ffn_norm, loss_target, m_pre_mix_norm, m_w_in, m_sgu_ln_gain, m_sgu_ln_bias, m_sgu_w_spatial, m_sgu_b_spatial, m_attn_out_norm, m_sgu_out_norm, m_w_out, m_post_mix_norm, m_pre_ffn_norm, m_w_gate, m_w_up, m_w_down, m_post_ffn_norm, v_pre_mix_norm, v_w_in, v_sgu_ln_gain, v_sgu_ln_bias, v_sgu_w_spatial, v_sgu_b_spatial, v_attn_out_norm, v_sgu_out_norm, v_w_out, v_post_mix_norm, v_pre_ffn_norm, v_w_gate, v_w_up, v_w_down, v_post_ffn_norm):
    a = dict(locals())
    cx, cy, cc = _coords()
    s_me = 2 * cx + cy
    core = jnp.stack([cc]).astype(jnp.int32)
    shard_core = jnp.stack([s_me, cc]).astype(jnp.int32)
    me = jnp.stack([4 * cx + 2 * cy + cc]).astype(jnp.int32)
    small = {n: (a[n][0] if a[n].ndim > 2 else a[n]) for n in SMALL}
    b_t = small["sgu_b_spatial"].T
    xs, pos, target = x[0], positions.reshape(SEQ, 1), loss_target[0]
    flipped = ("w_gate", "w_up")

    def big(name, n):
        return jnp.swapaxes(a[name], 1, 2)[0] if n in flipped else a[name][0]

    own = {n: _halves(big(n, n).astype(BF16)) for n in BIG}

    def with_own(full, n):
        full = lax.dynamic_update_slice(full, own[n][None], (s_me, 0, 0, 0))
        return full.reshape((N_SHARD,) + big(n, n).shape)

    ffn = ("w_gate", "w_up", "w_down")
    g_in, g_out, g_ffn = _gather_ici([own["w_in"]]), _gather_ici([own["w_out"]]), _gather_ici([own[n] for n in ffn])
    s_in = _split_start("gather_in_start", g_in, [lax.empty(o.shape, o.dtype) for o in g_in.out_shape],
                        small["pre_mix_norm"])
    s_out = _split_start("gather_out_start", g_out, [lax.empty(o.shape, o.dtype) for o in g_out.out_shape], s_in[4])
    s_ffn = _split_start("gather_ffn_start", g_ffn, [lax.empty(o.shape, o.dtype) for o in g_ffn.out_shape], s_out[4])
    in_lands = _split_wait("gather_in_wait", g_in, *s_in[:4], s_ffn[4])
    ((in_lands,),) = _comm_only("comm_pass_in", [_gather_pass(in_lands)])
    w_in_f = with_own(in_lands, "w_in")
    h, u, vs, *qkv = _inproj_fwd(xs, pos, small["pre_mix_norm"], w_in_f)
    views = [tuple(qkv[3 * i:3 * i + 3]) for i in range(len(DILATIONS))]
    o_list, l_list = [], []
    for dil, (qv, kv, vv) in zip(DILATIONS, views):
        o, l = _attn_fwd(qv, kv, vv, dil)
        o_list.append(o)
        l_list.append(l)
    out_lands = _split_wait("gather_out_wait", g_out, *s_out[:4], l_list[-1])
    sgu, ((out_lands,),) = _sgu_fwd(u, vs, small["sgu_ln_gain"], small["sgu_ln_bias"], small["sgu_w_spatial"], b_t,
                                    comms=[_gather_pass(out_lands)])
    w_out_f = with_own(out_lands, "w_out")
    ffn_lands = _split_wait("gather_ffn_wait", g_ffn, *s_ffn[:4], sgu)
    (attn, mixed, y, x1, *lses), (ffn_lands,) = _mix_out_fwd(
        o_list, l_list, sgu, xs, w_out_f, small["attn_out_norm"], small["sgu_out_norm"], small["post_mix_norm"],
        comms=[_gather_pass(ffn_lands)])
    w_gate_f, w_up_f, w_down_f = (with_own(f, n) for f, n in zip(ffn_lands, ffn))
    h2, act, dg, dup, df, dx1, loss_cols, d_pre_ffn, d_post_ffn = _ffn_fwd_bwd(
        x1, target, w_gate_f, w_up_f, w_down_f, small["pre_ffn_norm"], small["post_ffn_norm"])

    full_tok = pl.BlockSpec((SEQ, D_MODEL), lambda s: (0, 0), pipeline_mode=pl.Buffered(1))
    ff_tok = pl.BlockSpec((1, SEQ, FF_S), lambda s: (s, 0, 0))
    gw = {}
    gw["w_gate"] = _wgrad(dg, h2, ff_tok, full_tok, (FF_S, D_MODEL), "wgrad_gate")
    gw["w_up"], ((sib_gate,),) = _wgrad(dup, h2, ff_tok, full_tok, (FF_S, D_MODEL), "wgrad_up",
                                        comms=[_rs_sibling([gw["w_gate"]])])
    gw["w_down"], ((sib_up,),) = _wgrad(act, df, ff_tok, full_tok, (FF_S, D_MODEL), "wgrad_down",
                                        comms=[_rs_sibling([gw["w_up"]])])
    (dy, dsgu, d_post_mix, d_attn_norm, d_sgu_norm, *dviews), ((sib_down,),) = _outproj_bwd(
        dx1, y, attn, sgu, w_out_f, small["post_mix_norm"], small["attn_out_norm"],
        small["sgu_out_norm"], comms=[_rs_sibling([gw["w_down"]])])
    sib = {"w_gate": sib_gate, "w_up": sib_up, "w_down": sib_down}
    part = {n: _rs_chip_sum(gw[n], sib[n], core) for n in ("w_gate", "w_up", "w_down")}
    gw["w_out"] = _wgrad(mixed, dy, pl.BlockSpec((SEQ, OUT_S), lambda s: (0, s)), full_tok, (OUT_S, D_MODEL),
                         "wgrad_out")
    ffn = ("w_gate", "w_up", "w_down")
    x_ffn = _rs_chips([part[n] for n in ffn])
    s_ffn = _split_start("rs_ffn_start", x_ffn, [lax.empty(o.shape, o.dtype) for o in x_ffn.out_shape],
                         part["w_down"])
    (du, dvs_sgu, d_w_sp, d_b_sp, d_ln_gain, d_ln_bias), ((sib["w_out"],),) = _sgu_bwd(
        u, vs, dsgu, small["sgu_ln_gain"], small["sgu_ln_bias"], small["sgu_w_spatial"], b_t,
        comms=[_rs_sibling([gw["w_out"]])], after=[s_ffn[4]])
    part["w_out"] = _rs_chip_sum(gw["w_out"], sib["w_out"], core)
    packed_early = _pack({
        "sgu_ln_gain": d_ln_gain, "sgu_ln_bias": d_ln_bias, "sgu_w_spatial": d_w_sp, "sgu_b_spatial": d_b_sp,
        "attn_out_norm": d_attn_norm, "sgu_out_norm": d_sgu_norm, "post_mix_norm": d_post_mix,
        "pre_ffn_norm": d_pre_ffn, "post_ffn_norm": d_post_ffn, LOSS_ROW: loss_cols}, SMALL_EARLY, SMALL_ROWS)
    x_out, x_small = _rs_chips([part["w_out"]]), _small_exchange(packed_early)
    s_out = _split_start("rs_out_start", x_out, [lax.empty(o.shape, o.dtype) for o in x_out.out_shape], s_ffn[4])
    s_small = _split_start("small_early_start", x_small,
                           [lax.empty(o.shape, o.dtype) for o in x_small.out_shape], s_out[4])

    dqs, dks, dvs = [], [], []
    for i, (dil, (qv, kv, vv)) in enumerate(zip(DILATIONS, views)):
        dq, dk, dv = _attn_bwd(qv, kv, vv, dviews[i], dviews[3 + i], lses[i], dil, after=[s_small[4]])
        dqs.append(dq)
        dks.append(dk)
        dvs.append(dv)
    half, far, joined = {}, {}, {}
    got = _split_wait("rs_ffn_wait", x_ffn, *s_ffn[:4], dvs[-1])
    for n, f in zip(ffn, got):
        far[n] = f
        half[n] = _rs_final_sum(gw[n], sib[n], f, shard_core)
    (dproj, grad_x, d_pre_mix), (got,) = _inproj_bwd(
        dqs, dks, dvs, du, dvs_sgu, pos, xs, dx1, w_in_f, small["pre_mix_norm"],
        comms=[_rs_join([half[n] for n in ffn])])
    joined.update(zip(ffn, got))
    (far["w_out"],) = _split_wait("rs_out_wait", x_out, *s_out[:4], grad_x)
    (slots_early,) = _split_wait("small_early_wait", x_small, *s_small[:4], far["w_out"])
    half["w_out"] = _rs_final_sum(gw["w_out"], sib["w_out"], far["w_out"], shard_core)
    packed_late = _pack({"pre_mix_norm": d_pre_mix}, SMALL_LATE, 8)
    gw["w_in"], ((joined["w_out"],), (slots_late,)) = _wgrad(
        h, dproj, full_tok, pl.BlockSpec((SEQ, IN_S), lambda s: (0, s)), (D_MODEL, IN_S), "wgrad_in",
        comms=[_rs_join([half["w_out"]]), _small_exchange(packed_late)])
    ((sib["w_in"],),) = _comm_only("comm_rs_sibling_in", [_rs_sibling([gw["w_in"]])])
    part["w_in"] = _rs_chip_sum(gw["w_in"], sib["w_in"], core)
    x_in = _rs_chips([part["w_in"]])
    s_in = _split_start("rs_in_start", x_in, [lax.empty(o.shape, o.dtype) for o in x_in.out_shape], part["w_in"])

    grads, deltas, new_m, new_v = {}, {}, {}, {}

    def update(n, after):
        g = joined[n].reshape(big(n, n).shape)
        outs = (g, *_adamw(big(n, n), g, big("m_" + n, n), big("v_" + n, n), BIG_ADAM_ROWS[n], "adamw_" + n, after))
        grads[n], deltas[n], new_m[n], new_v[n] = (
            jnp.swapaxes(o[None], 1, 2) if n in flipped else o[None] for o in outs)

    for n in ("w_gate", "w_up", "w_down", "w_out"):
        update(n, [s_in[4]])
    (far["w_in"],) = _split_wait("rs_in_wait", x_in, *s_in[:4], new_v["w_out"])
    half["w_in"] = _rs_final_sum(gw["w_in"], sib["w_in"], far["w_in"], shard_core)
    ((joined["w_in"],),) = _comm_only("comm_rs_join_in", [_rs_join([half["w_in"]])])
    update("w_in", [])
    a[LOSS_ROW] = a["m_" + LOSS_ROW] = a["v_" + LOSS_ROW] = jnp.zeros((1, D_MODEL), F32)
    for names, rows, packed, slots in ((SMALL_EARLY, SMALL_ROWS, packed_early, slots_early),
                                       (SMALL_LATE, 8, packed_late, slots_late)):
        outs = _adamw_small(packed, slots, _pack(a, names, rows), _pack({n: a["m_" + n] for n in names}, names, rows),
                            _pack({n: a["v_" + n] for n in names}, names, rows), me)
        for dst, buf in zip((grads, deltas, new_m, new_v), outs):
            dst.update(_unpack(buf, names, {n: a[n].shape for n in names}))
    loss = jnp.sum(grads[LOSS_ROW]) * np.float32(0.5 / D_MODEL)
    return (loss, grad_x[None], *[grads[n] for n in WEIGHTS], *[deltas[n] for n in WEIGHTS],
            *[new_m[n] for n in WEIGHTS], *[new_v[n] for n in WEIGHTS])
```

```python
import numpy as np
import jax
import jax.numpy as jnp
from jax import lax
from jax.experimental import pallas as pl
from jax.experimental.pallas import tpu as pltpu

F32 = jnp.float32
BF16 = jnp.bfloat16

SEQ = 2048
D_MODEL = 1024
HEAD_DIM = 64
ATTN_W = 512
SGU_W = 512
SGU_GROUPS = 8
CHUNK = 128
DILATIONS = (1, 4, 16)
N_SHARD = 4
IN_S = 640
OUT_S = 256
FF_S = 704
PROJ_W = N_SHARD * IN_S
RMS_EPS = 1e-6
LN_EPS = 1e-5
ROPE_THETA = 500000.0
ATTN_SCALE = 1.0 / np.sqrt(HEAD_DIM)
NEG = -1e30
TM = 256
VMEM_LIMIT = 56 * 1024 * 1024
SMALL_ROWS = 136

ADAM_LR = 0.001
ADAM_B1 = 0.9
ADAM_B2 = 0.999
ADAM_EPS = 1e-08
ADAM_WD = 0.01
ADAM_STEP = 10

MESH = pl.DeviceIdType.MESH
ANY = pl.BlockSpec(memory_space=pl.ANY)


def _dot(a, b):
    return jnp.dot(a, b, preferred_element_type=F32)


def _dot_nt(a, b):
    return lax.dot_general(a, b, (((1,), (1,)), ((), ())), preferred_element_type=F32)


def _dot_tn(a, b):
    return lax.dot_general(a, b, (((0,), (0,)), ((), ())), preferred_element_type=F32)


def _dot_exact(a, b):
    return jnp.dot(a, b, preferred_element_type=F32, precision=lax.Precision.HIGHEST)


def _rms_stats(x):
    r = lax.rsqrt(jnp.mean(x * x, axis=-1, keepdims=True) + RMS_EPS)
    return x * r, r


def _rms_bwd(xh, r, gain, dy):
    dxh = dy * gain
    dx = r * (dxh - xh * jnp.mean(dxh * xh, axis=-1, keepdims=True))
    return dx, jnp.sum(dy * xh, axis=0, keepdims=True)


_ERF_ALPHA = (-2.72614225801306e-10, 2.77068142495902e-08, -2.10102402082508e-06, -5.69250639462346e-05,
              -7.34990630326855e-04, -2.95459980854025e-03, -1.60960333262415e-02)
_ERF_BETA = (-1.45660718464996e-05, -2.13374055278905e-04, -1.68282697438203e-03, -7.37332916720468e-03,
             -1.42647390514189e-02)


def _erf(x):
    x = jnp.clip(x, -4.0, 4.0)
    x2 = x * x
    p = jnp.full_like(x, _ERF_ALPHA[0])
    for a in _ERF_ALPHA[1:]:
        p = p * x2 + a
    q = jnp.full_like(x, _ERF_BETA[0])
    for b in _ERF_BETA[1:]:
        q = q * x2 + b
    return x * p / q


def _gelu(x):
    return 0.5 * x * (1.0 + _erf(x * np.float32(1.0 / np.sqrt(2.0))))


def _gelu_grad(x):
    cdf = 0.5 * (1.0 + _erf(x * np.float32(1.0 / np.sqrt(2.0))))
    pdf = jnp.exp(-0.5 * x * x) * np.float32(1.0 / np.sqrt(2.0 * np.pi))
    return cdf + x * pdf


def _sigmoid(x):
    return 1.0 / (1.0 + jnp.exp(-x))


_INV_FREQ = tuple(float(np.float32(ROPE_THETA ** (-2.0 * j / 16.0))) for j in range(8))


def _rot_tables(pos):
    lane = lax.broadcasted_iota(jnp.int32, (1, 128), 1)
    d = lane & 63
    j = d & 7
    inv = jnp.zeros((1, 128), F32)
    for jj in range(8):
        inv = jnp.where(j == jj, _INV_FREQ[jj], inv)
    ang = pos.astype(F32) * inv
    c = jnp.cos(ang)
    s = jnp.sin(ang)
    cos_t = jnp.where(d < 16, c, 1.0)
    sin_a = jnp.where(d < 8, -s, 0.0)
    sin_b = jnp.where((d >= 8) & (d < 16), s, 0.0)
    return tuple(jnp.tile(t, (1, 4)) for t in (cos_t, sin_a, sin_b))


def _rope(x, tabs):
    cos_t, sin_a, sin_b = tabs
    return x * cos_t + pltpu.roll(x, 504, 1) * sin_a + pltpu.roll(x, 8, 1) * sin_b


def _rope_bwd(dy, tabs):
    cos_t, sin_a, sin_b = tabs
    return dy * cos_t + pltpu.roll(dy * sin_a, 8, 1) + pltpu.roll(dy * sin_b, 504, 1)


def _left_half():
    return lax.broadcasted_iota(jnp.int32, (CHUNK, CHUNK), 1) < HEAD_DIM


def _group_ones():
    lane = lax.broadcasted_iota(jnp.int32, (SGU_GROUPS, SGU_W), 1)
    row = lax.broadcasted_iota(jnp.int32, (SGU_GROUPS, SGU_W), 0)
    return ((lane >> 6) == row).astype(F32)


def _masked_spatial(w_ref):
    row = lax.broadcasted_iota(jnp.int32, (CHUNK, CHUNK), 0)
    col = lax.broadcasted_iota(jnp.int32, (CHUNK, CHUNK), 1)
    return [jnp.where(col <= row, w_ref[g], 0.0).astype(BF16) for g in range(SGU_GROUPS)]


def _sgu_core(u, vs, lg, lb, wm, bias_full):
    tm = u.shape[0]
    gu = _gelu(u)
    gv = _gelu(vs)
    mu = jnp.mean(gv, axis=-1, keepdims=True)
    xc = gv - mu
    rstd = lax.rsqrt(jnp.mean(xc * xc, axis=-1, keepdims=True) + LN_EPS)
    xh = xc * rstd
    vnb = (xh * lg + lb).astype(BF16)
    left = _left_half()
    rows = []
    for c in range(tm // CHUNK):
        pieces = []
        for p in range(4):
            vp = vnb[c * CHUNK:(c + 1) * CHUNK, p * 128:(p + 1) * 128]
            pieces.append(jnp.where(left, _dot(wm[2 * p], vp), _dot(wm[2 * p + 1], vp)))
        rows.append(jnp.concatenate(pieces, axis=1) + bias_full)
    mixed = jnp.concatenate(rows, axis=0)
    return gu, xh, rstd, vnb, mixed


def _resident(shape):
    n = len(shape)
    return pl.BlockSpec(shape, lambda *_: (0,) * n, pipeline_mode=pl.Buffered(1))


def _rows(ncol, tm=TM):
    return pl.BlockSpec((tm, ncol), lambda i: (i, 0))


def _rows3(nlead, ncol, tm=TM):
    return pl.BlockSpec((nlead, tm, ncol), lambda i: (0, i, 0))


def _acc(ncol, nrow=1):
    return pl.BlockSpec((nrow, ncol), lambda i: (0, 0))


def _view_rows(dil, tm=TM):
    return pl.BlockSpec((tm // dil, dil * ATTN_W), lambda i: (i, 0))


def _view_shape(dil, dtype):
    return _sds((SEQ // dil, dil * ATTN_W), dtype)


def _slab_scratch():
    return pltpu.VMEM((4, TM, 128), F32)


def _store_view(val, out_ref, slabs, dil):
    for j in range(4):
        slabs[j] = val[:, j * 128:(j + 1) * 128]
    for r in range(dil):
        for j in range(4):
            c0 = r * ATTN_W + j * 128
            out_ref[:, c0:c0 + 128] = slabs.at[j][pl.ds(r, TM // dil, stride=dil), :].astype(out_ref.dtype)


def _load_view(in_ref, slabs, dil):
    for r in range(dil):
        for j in range(4):
            c0 = r * ATTN_W + j * 128
            slabs.at[j][pl.ds(r, TM // dil, stride=dil), :] = in_ref[:, c0:c0 + 128].astype(F32)
    return jnp.concatenate([slabs[j] for j in range(4)], axis=1)


def _seq_params():
    return pltpu.CompilerParams(dimension_semantics=("arbitrary",), vmem_limit_bytes=VMEM_LIMIT)


def _sds(shape, dtype):
    return jax.ShapeDtypeStruct(shape, dtype)


class _Comm:
    def __init__(self, args, out_shape, n_sems, start, finish, aliased=False):
        self.args, self.out_shape, self.n_sems = list(args), list(out_shape), n_sems
        self.start, self.finish, self.aliased = start, finish, aliased


def _pcall(body, *, name, grid, in_specs, out_specs, out_shape, args, scratch_shapes=(), comms=(), after=()):
    single = not isinstance(out_shape, (list, tuple))
    out_specs = [out_specs] if single else list(out_specs)
    out_shape = [out_shape] if single else list(out_shape)
    n_in, n_out, n_scr = len(in_specs), len(out_shape), len(scratch_shapes)
    c_args = [a for c in comms for a in c.args]
    c_outs = [o for c in comms for o in c.out_shape]
    aliases, ai, ao = {}, n_in, n_out
    for c in comms:
        if c.aliased:
            aliases.update({ai + k: ao + k for k in range(len(c.args))})
        ai += len(c.args)
        ao += len(c.out_shape)
    sems = [pltpu.SemaphoreType.DMA((c.n_sems,)) for c in comms for _ in range(2)]
    steps = grid[0]

    def wrapped(*refs):
        o0 = n_in + len(c_args) + len(after)
        s0 = o0 + n_out + len(c_outs)
        m_in, m_out, m_sem = refs[n_in:n_in + len(c_args)], refs[o0 + n_out:s0], refs[s0 + n_scr:]

        def each(phase):
            ii = oi = 0
            for k, c in enumerate(comms):
                getattr(c, phase)(m_in[ii:ii + len(c.args)], m_out[oi:oi + len(c.out_shape)],
                                  m_sem[2 * k], m_sem[2 * k + 1])
                ii += len(c.args)
                oi += len(c.out_shape)

        if comms:
            @pl.when(pl.program_id(0) == 0)
            def _():
                each("start")

        body(*refs[:n_in], *refs[o0:o0 + n_out], *refs[s0:s0 + n_scr])

        if comms:
            @pl.when(pl.program_id(0) == steps - 1)
            def _():
                each("finish")

    res = pl.pallas_call(
        wrapped, name=name, grid=grid,
        in_specs=list(in_specs) + [ANY] * (len(c_args) + len(after)), out_specs=out_specs + [ANY] * len(c_outs),
        out_shape=out_shape + c_outs, scratch_shapes=list(scratch_shapes) + sems,
        input_output_aliases=aliases, compiler_params=_seq_params(),
    )(*args, *c_args, *after)
    mine = res[0] if single else list(res[:n_out])
    if not comms:
        return mine
    theirs, oi = [], n_out
    for c in comms:
        theirs.append(list(res[oi:oi + len(c.out_shape)]))
        oi += len(c.out_shape)
    return mine, theirs


def _inproj_fwd(x, pos, g_pre, w_in, comms=()):
    def body(x_ref, pos_ref, g_ref, w_ref, h_ref, u_ref, vs_ref, *rest):
        qkv_refs, slabs = rest[:9], rest[9:]
        xh, _ = _rms_stats(x_ref[...])
        h = (xh * g_ref[...]).astype(BF16)
        h_ref[...] = h
        proj = jnp.concatenate([_dot(h, w_ref[s]) for s in range(N_SHARD)], axis=1)
        tabs = _rot_tables(pos_ref[...])
        u_ref[...] = proj[:, 1536:2048]
        vs_ref[...] = proj[:, 2048:2560]
        qkv = (_rope(proj[:, 0:512], tabs) * np.float32(ATTN_SCALE), _rope(proj[:, 512:1024], tabs),
               proj[:, 1024:1536])
        for t, val in enumerate(qkv):
            qkv_refs[t][...] = val.astype(BF16)
            for i, dil in enumerate(DILATIONS[1:]):
                _store_view(val, qkv_refs[3 * (i + 1) + t], slabs[t], dil)

    return _pcall(
        body, name="inproj_fwd", grid=(SEQ // TM,),
        in_specs=[_rows(D_MODEL), _rows(1), _resident((1, D_MODEL)), _resident((N_SHARD, D_MODEL, IN_S))],
        out_specs=[_rows(D_MODEL), _rows(512), _rows(512)] + [_view_rows(dil) for dil in DILATIONS for _ in range(3)],
        out_shape=[_sds((SEQ, D_MODEL), BF16), _sds((SEQ, 512), F32), _sds((SEQ, 512), F32)]
        + [_view_shape(dil, BF16) for dil in DILATIONS for _ in range(3)],
        scratch_shapes=[_slab_scratch() for _ in range(3)],
        args=(x, pos, g_pre, w_in), comms=comms)


def _sgu_fwd(u, vs, lg, lb, w_sp, b_t, comms=()):
    def body(u_ref, vs_ref, lg_ref, lb_ref, w_ref, bt_ref, out_ref):
        wm = _masked_spatial(w_ref)
        bias_full = _dot_exact(bt_ref[...], _group_ones())
        gu, _, _, _, mixed = _sgu_core(u_ref[...], vs_ref[...], lg_ref[...], lb_ref[...], wm, bias_full)
        out_ref[...] = gu * mixed

    return _pcall(
        body, name="sgu_fwd", grid=(SEQ // TM,),
        in_specs=[_rows(SGU_W), _rows(SGU_W), _resident((1, SGU_W)), _resident((1, SGU_W)),
                  _resident((SGU_GROUPS, CHUNK, CHUNK)), _resident((CHUNK, SGU_GROUPS))],
        out_specs=_rows(SGU_W),
        out_shape=_sds((SEQ, SGU_W), F32),
        args=(u, vs, lg, lb, w_sp, b_t), comms=comms)


def _block_masks():
    row = lax.broadcasted_iota(jnp.int32, (CHUNK, CHUNK), 0)
    col = lax.broadcasted_iota(jnp.int32, (CHUNK, CHUNK), 1)
    return col <= row, col >= row


def _attn_fwd(qv, kv, vv, dil, comms=()):
    seg = SEQ // dil
    nblk = seg // CHUNK

    def body(q_ref, k_ref, v_ref, o_ref, l_ref):
        left = _left_half()
        m_cur, m_prev = _block_masks()
        zero = jnp.zeros((CHUNK, CHUNK), BF16)
        ones = (jnp.where(left, 1.0, 0.0).astype(BF16), jnp.where(left, 0.0, 1.0).astype(BF16))

        def blk(b, carry):
            r0 = pl.multiple_of(b * CHUNK, CHUNK)
            rp = pl.multiple_of(jnp.maximum(b - 1, 0) * CHUNK, CHUNK)
            prev_ok = m_prev & (b > 0)
            sides = tuple(enumerate((left, ~left)))
            tiles, scores = [], []
            for hp in range(4):
                ls = slice(hp * 128, (hp + 1) * 128)
                qp = q_ref[pl.ds(r0, CHUNK), ls]
                kc = k_ref[pl.ds(r0, CHUNK), ls]
                kp = k_ref[pl.ds(rp, CHUNK), ls] if nblk > 1 else None
                tiles.append((ls, v_ref[pl.ds(r0, CHUNK), ls], v_ref[pl.ds(rp, CHUNK), ls] if nblk > 1 else None))
                for _, hm in sides:
                    qh = jnp.where(hm, qp, zero)
                    sc = jnp.where(m_cur, _dot_nt(qh, kc), NEG)
                    sp = jnp.where(prev_ok, _dot_nt(qh, kp), NEG) if nblk > 1 else None
                    scores.append((sc, sp))
            probs = []
            for sc, sp in scores:
                if nblk > 1:
                    m = jnp.max(jnp.maximum(sc, sp), axis=-1, keepdims=True)
                    pc = jnp.exp(sc - m)
                    pp = jnp.exp(sp - m)
                    probs.append((m, pc.astype(BF16), pp.astype(BF16), (pc + pp).astype(BF16)))
                else:
                    m = jnp.max(sc, axis=-1, keepdims=True)
                    pc = jnp.exp(sc - m).astype(BF16)
                    probs.append((m, pc, None, pc))
            for hp, (ls, vc, vp) in enumerate(tiles):
                acc = jnp.zeros((CHUNK, CHUNK), F32)
                den = jnp.zeros((CHUNK, CHUNK), F32)
                for side, hm in sides:
                    _, pc, pp, psum = probs[2 * hp + side]
                    acc = acc + _dot(pc, jnp.where(hm, vc, zero))
                    if nblk > 1:
                        acc = acc + _dot(pp, jnp.where(hm, vp, zero))
                    den = den + _dot(psum, ones[side])
                o_ref[pl.ds(r0, CHUNK), ls] = acc / den
                l_ref[pl.ds(r0, CHUNK), ls] = jnp.where(left, probs[2 * hp][0], probs[2 * hp + 1][0]) + jnp.log(den)
            return carry

        lax.fori_loop(0, nblk, blk, 0)

    spec = pl.BlockSpec((seg, ATTN_W), lambda r: (0, r))
    return _pcall(
        body, name=f"attn_fwd_d{dil}", grid=(dil,),
        in_specs=[spec, spec, spec], out_specs=[spec, spec],
        out_shape=[_sds((seg, dil * ATTN_W), F32), _sds((seg, dil * ATTN_W), F32)],
        args=(qv, kv, vv), comms=comms)


def _mix_out_fwd(o_list, l_list, sgu, x, w_out, g_attn, g_sgu, g_post, comms=()):
    def body(o1, o2, o3, l1, l2, l3, sgu_ref, x_ref, w_ref, ga_ref, gs_ref, gp_ref,
             attn_ref, mixed_ref, y_ref, x1_ref, lse1_ref, lse2_ref, lse3_ref, slabs_a, slabs_b):
        os = [o1[...], _load_view(o2, slabs_a, DILATIONS[1]), _load_view(o3, slabs_b, DILATIONS[2])]
        ls = [l1[...], _load_view(l2, slabs_a, DILATIONS[1]), _load_view(l3, slabs_b, DILATIONS[2])]
        m = jnp.maximum(jnp.maximum(ls[0], ls[1]), ls[2])
        es = [jnp.exp(l - m) for l in ls]
        den = es[0] + es[1] + es[2]
        attn = (es[0] * os[0] + es[1] * os[1] + es[2] * os[2]) / den
        attn_ref[...] = attn
        lse = m + jnp.log(den)
        lse1_ref[...] = lse
        _store_view(lse, lse2_ref, slabs_a, DILATIONS[1])
        _store_view(lse, lse3_ref, slabs_b, DILATIONS[2])
        ah, _ = _rms_stats(attn)
        sh, _ = _rms_stats(sgu_ref[...])
        mixed = jnp.concatenate([ah * ga_ref[...], sh * gs_ref[...]], axis=1).astype(BF16)
        mixed_ref[...] = mixed
        y = _dot(mixed[:, 0:OUT_S], w_ref[0])
        for s in range(1, N_SHARD):
            y = y + _dot(mixed[:, s * OUT_S:(s + 1) * OUT_S], w_ref[s])
        y_ref[...] = y
        yh, _ = _rms_stats(y)
        x1_ref[...] = x_ref[...] + yh * gp_ref[...]

    return _pcall(
        body, name="mix_out_fwd", grid=(SEQ // TM,),
        in_specs=[_view_rows(dil) for dil in DILATIONS] * 2
        + [_rows(512), _rows(D_MODEL), _resident((N_SHARD, OUT_S, D_MODEL)),
           _resident((1, 512)), _resident((1, 512)), _resident((1, D_MODEL))],
        out_specs=[_rows(512), _rows(D_MODEL), _rows(D_MODEL), _rows(D_MODEL)] + [_view_rows(dil) for dil in DILATIONS],
        out_shape=[_sds((SEQ, 512), F32), _sds((SEQ, D_MODEL), BF16), _sds((SEQ, D_MODEL), F32),
                   _sds((SEQ, D_MODEL), F32)] + [_view_shape(dil, F32) for dil in DILATIONS],
        scratch_shapes=[_slab_scratch(), _slab_scratch()],
        args=(*o_list, *l_list, sgu, x, w_out, g_attn, g_sgu, g_post), comms=comms)


def _ffn_fwd_bwd(x1, target, w_gate, w_up, w_down, g_pre, g_post, comms=()):
    def body(x1_ref, t_ref, wg_ref, wu_ref, wd_ref, gpf_ref, gpo_ref,
             h2_ref, a_ref, dg_ref, dup_ref, df_ref, dx1_ref, loss_ref, dgpf_ref, dgpo_ref, g_scr, up_scr):
        @pl.when(pl.program_id(0) == 0)
        def _():
            loss_ref[...] = jnp.zeros_like(loss_ref)
            dgpf_ref[...] = jnp.zeros_like(dgpf_ref)
            dgpo_ref[...] = jnp.zeros_like(dgpo_ref)

        x1 = x1_ref[...]
        gpf = gpf_ref[...]
        gpo = gpo_ref[...]
        xh, r = _rms_stats(x1)
        h2 = (xh * gpf).astype(BF16)
        h2_ref[...] = h2
        f = jnp.zeros((TM, D_MODEL), F32)
        for s in range(N_SHARD):
            g = _dot_nt(h2, wg_ref[s])
            up = _dot_nt(h2, wu_ref[s])
            g_scr[s] = g
            up_scr[s] = up
            a = (g * _sigmoid(g) * up).astype(BF16)
            a_ref[s] = a
            f = f + _dot(a, wd_ref[s])
        fh, rf = _rms_stats(f)
        diff = x1 + fh * gpo - t_ref[...]
        loss_ref[...] += jnp.sum(diff * diff, axis=0, keepdims=True)
        dout = diff * np.float32(1.0 / D_MODEL)
        df, dgpo = _rms_bwd(fh, rf, gpo, dout)
        dgpo_ref[...] += dgpo
        dfb = df.astype(BF16)
        df_ref[...] = dfb
        dh2 = jnp.zeros((TM, D_MODEL), F32)
        for s in range(N_SHARD):
            da = _dot_nt(dfb, wd_ref[s])
            g = g_scr[s]
            up = up_scr[s]
            sg = _sigmoid(g)
            dup = (da * (g * sg)).astype(BF16)
            dg = (da * up * (sg * (1.0 + g * (1.0 - sg)))).astype(BF16)
            dg_ref[s] = dg
            dup_ref[s] = dup
            dh2 = dh2 + _dot(dg, wg_ref[s]) + _dot(dup, wu_ref[s])
        dx, dgpf = _rms_bwd(xh, r, gpf, dh2)
        dgpf_ref[...] += dgpf
        dx1_ref[...] = dout + dx

    return _pcall(
        body, name="ffn_fwd_bwd", grid=(SEQ // TM,),
        in_specs=[_rows(D_MODEL), _rows(D_MODEL), _resident((N_SHARD, FF_S, D_MODEL)),
                  _resident((N_SHARD, FF_S, D_MODEL)), _resident((N_SHARD, FF_S, D_MODEL)),
                  _resident((1, D_MODEL)), _resident((1, D_MODEL))],
        out_specs=[_rows(D_MODEL), _rows3(N_SHARD, FF_S), _rows3(N_SHARD, FF_S), _rows3(N_SHARD, FF_S),
                   _rows(D_MODEL), _rows(D_MODEL), _acc(D_MODEL), _acc(D_MODEL), _acc(D_MODEL)],
        out_shape=[_sds((SEQ, D_MODEL), BF16), _sds((N_SHARD, SEQ, FF_S), BF16), _sds((N_SHARD, SEQ, FF_S), BF16),
                   _sds((N_SHARD, SEQ, FF_S), BF16), _sds((SEQ, D_MODEL), BF16), _sds((SEQ, D_MODEL), F32),
                   _sds((1, D_MODEL), F32), _sds((1, D_MODEL), F32), _sds((1, D_MODEL), F32)],
        scratch_shapes=[pltpu.VMEM((N_SHARD, TM, FF_S), F32), pltpu.VMEM((N_SHARD, TM, FF_S), F32)],
        args=(x1, target, w_gate, w_up, w_down, g_pre, g_post), comms=comms)


def _wgrad(a, b, a_spec, b_spec, out_block, name, comms=()):
    def body(a_ref, b_ref, o_ref):
        av = a_ref[0] if len(a_ref.shape) == 3 else a_ref[...]
        bv = b_ref[0] if len(b_ref.shape) == 3 else b_ref[...]
        o_ref[0] = _dot_tn(av, bv)

    return _pcall(
        body, name=name, grid=(N_SHARD,),
        in_specs=[a_spec, b_spec],
        out_specs=pl.BlockSpec((1,) + out_block, lambda s: (s, 0, 0)),
        out_shape=_sds((N_SHARD,) + out_block, F32),
        args=(a, b), comms=comms)


def _outproj_bwd(dx1, y, attn, sgu, w_out, g_post, g_attn, g_sgu, comms=()):
    def body(dx1_ref, y_ref, attn_ref, sgu_ref, w_ref, gp_ref, ga_ref, gs_ref,
             dy_ref, dsgu_ref, dgp_ref, dga_ref, dgs_ref, *rest):
        dattn_refs, delta_refs, (slabs_a, slabs_b) = rest[0:3], rest[3:6], rest[6:]

        @pl.when(pl.program_id(0) == 0)
        def _():
            dgp_ref[...] = jnp.zeros_like(dgp_ref)
            dga_ref[...] = jnp.zeros_like(dga_ref)
            dgs_ref[...] = jnp.zeros_like(dgs_ref)

        yh, ry = _rms_stats(y_ref[...])
        dy, dgp = _rms_bwd(yh, ry, gp_ref[...], dx1_ref[...])
        dgp_ref[...] += dgp
        dyb = dy.astype(BF16)
        dy_ref[...] = dyb
        dmixed = jnp.concatenate([_dot_nt(dyb, w_ref[s]) for s in range(N_SHARD)], axis=1)
        attn = attn_ref[...]
        ah, ra = _rms_stats(attn)
        dattn, dga = _rms_bwd(ah, ra, ga_ref[...], dmixed[:, 0:512])
        dga_ref[...] += dga
        sh, rs = _rms_stats(sgu_ref[...])
        dsgu, dgs = _rms_bwd(sh, rs, gs_ref[...], dmixed[:, 512:1024])
        dgs_ref[...] += dgs
        dsgu_ref[...] = dsgu
        la = lax.broadcasted_iota(jnp.int32, (ATTN_W, ATTN_W), 0) >> 6
        lb = lax.broadcasted_iota(jnp.int32, (ATTN_W, ATTN_W), 1) >> 6
        delta = _dot_exact(dattn * attn, (la == lb).astype(F32))
        dattn_refs[0][...] = dattn.astype(BF16)
        delta_refs[0][...] = delta
        for i, dil in enumerate(DILATIONS[1:]):
            _store_view(dattn, dattn_refs[i + 1], slabs_a, dil)
            _store_view(delta, delta_refs[i + 1], slabs_b, dil)

    return _pcall(
        body, name="outproj_bwd", grid=(SEQ // TM,),
        in_specs=[_rows(D_MODEL), _rows(D_MODEL), _rows(512), _rows(512), _resident((N_SHARD, OUT_S, D_MODEL)),
                  _resident((1, D_MODEL)), _resident((1, 512)), _resident((1, 512))],
        out_specs=[_rows(D_MODEL), _rows(512), _acc(D_MODEL), _acc(512), _acc(512)]
        + [_view_rows(dil) for dil in DILATIONS] * 2,
        out_shape=[_sds((SEQ, D_MODEL), BF16), _sds((SEQ, 512), F32),
                   _sds((1, D_MODEL), F32), _sds((1, 512), F32), _sds((1, 512), F32)]
        + [_view_shape(dil, BF16) for dil in DILATIONS] + [_view_shape(dil, F32) for dil in DILATIONS],
        scratch_shapes=[_slab_scratch(), _slab_scratch()],
        args=(dx1, y, attn, sgu, w_out, g_post, g_attn, g_sgu), comms=comms)


def _sgu_bwd(u, vs, dsgu, lg, lb, w_sp, b_t, comms=(), after=()):
    nsteps = SEQ // TM

    def body(u_ref, vs_ref, ds_ref, lg_ref, lb_ref, w_ref, bt_ref,
             du_ref, dvs_ref, dw_ref, db_ref, dlg_ref, dlb_ref, dbias_scr):
        i = pl.program_id(0)

        @pl.when(i == 0)
        def _():
            dw_ref[...] = jnp.zeros_like(dw_ref)
            dlg_ref[...] = jnp.zeros_like(dlg_ref)
            dlb_ref[...] = jnp.zeros_like(dlb_ref)
            dbias_scr[...] = jnp.zeros_like(dbias_scr)

        wm = _masked_spatial(w_ref)
        ones_g = _group_ones()
        bias_full = _dot_exact(bt_ref[...], ones_g)
        u = u_ref[...]
        vs = vs_ref[...]
        lg = lg_ref[...]
        gu, xh, rstd, vnb, mixed = _sgu_core(u, vs, lg, lb_ref[...], wm, bias_full)
        dsgu = ds_ref[...]
        du_ref[...] = (dsgu * mixed * _gelu_grad(u)).astype(BF16)
        dmixed = dsgu * gu
        left = _left_half()
        dvn_rows = []
        for c in range(TM // CHUNK):
            rs = slice(c * CHUNK, (c + 1) * CHUNK)
            dm_c = dmixed[rs, :]
            dbias_scr[...] += dm_c
            pieces = []
            for p in range(4):
                ls = slice(p * 128, (p + 1) * 128)
                dmp = dm_c[:, ls]
                vp = vnb[rs, ls]
                dmb = dmp.astype(BF16)
                zero = jnp.zeros_like(dmb)
                dw_ref[2 * p] += _dot_nt(jnp.where(left, dmb, zero), vp)
                dw_ref[2 * p + 1] += _dot_nt(jnp.where(left, zero, dmb), vp)
                pieces.append(jnp.where(left, _dot_tn(wm[2 * p], dmb), _dot_tn(wm[2 * p + 1], dmb)))
            dvn_rows.append(jnp.concatenate(pieces, axis=1))
        dvn = jnp.concatenate(dvn_rows, axis=0)
        dlg_ref[...] += jnp.sum(dvn * xh, axis=0, keepdims=True)
        dlb_ref[...] += jnp.sum(dvn, axis=0, keepdims=True)
        dxh = dvn * lg
        dgv = rstd * (dxh - jnp.mean(dxh, axis=-1, keepdims=True) - xh * jnp.mean(dxh * xh, axis=-1, keepdims=True))
        dvs_ref[...] = (dgv * _gelu_grad(vs)).astype(BF16)

        @pl.when(i == nsteps - 1)
        def _():
            row = lax.broadcasted_iota(jnp.int32, (CHUNK, CHUNK), 0)
            col = lax.broadcasted_iota(jnp.int32, (CHUNK, CHUNK), 1)
            for g in range(SGU_GROUPS):
                dw_ref[g] = jnp.where(col <= row, dw_ref[g], 0.0)
            db_ref[...] = lax.dot_general(ones_g, dbias_scr[...], (((1,), (1,)), ((), ())),
                                          preferred_element_type=F32, precision=lax.Precision.HIGHEST)

    return _pcall(
        body, name="sgu_bwd", grid=(nsteps,),
        in_specs=[_rows(SGU_W), _rows(SGU_W), _rows(SGU_W), _resident((1, SGU_W)), _resident((1, SGU_W)),
                  _resident((SGU_GROUPS, CHUNK, CHUNK)), _resident((CHUNK, SGU_GROUPS))],
        out_specs=[_rows(SGU_W), _rows(SGU_W), pl.BlockSpec((SGU_GROUPS, CHUNK, CHUNK), lambda i: (0, 0, 0)),
                   _acc(CHUNK, SGU_GROUPS), _acc(SGU_W), _acc(SGU_W)],
        out_shape=[_sds((SEQ, SGU_W), BF16), _sds((SEQ, SGU_W), BF16), _sds((SGU_GROUPS, CHUNK, CHUNK), F32),
                   _sds((SGU_GROUPS, CHUNK), F32), _sds((1, SGU_W), F32), _sds((1, SGU_W), F32)],
        scratch_shapes=[pltpu.VMEM((CHUNK, SGU_W), F32)],
        args=(u, vs, dsgu, lg, lb, w_sp, b_t), comms=comms, after=after)


def _attn_bwd(qv, kv, vv, dov, deltav, lsev, dil, comms=(), after=()):
    seg = SEQ // dil
    nblk = seg // CHUNK

    def body(q_ref, k_ref, v_ref, do_ref, dl_ref, lse_ref, dq_ref, dk_ref, dv_ref):
        left = _left_half()
        m_cur, m_prev = _block_masks()

        def blk(b, carry):
            r0 = pl.multiple_of(b * CHUNK, CHUNK)
            rp = pl.multiple_of(jnp.maximum(b - 1, 0) * CHUNK, CHUNK)
            prev_ok = m_prev & (b > 0)
            sides = tuple(enumerate((left, ~left)))
            zero = jnp.zeros((CHUNK, CHUNK), BF16)
            tiles, firsts = [], []
            for hp in range(4):
                ls = slice(hp * 128, (hp + 1) * 128)
                qp = q_ref[pl.ds(r0, CHUNK), ls]
                kc = k_ref[pl.ds(r0, CHUNK), ls]
                vc = v_ref[pl.ds(r0, CHUNK), ls]
                dop = do_ref[pl.ds(r0, CHUNK), ls]
                kp = k_ref[pl.ds(rp, CHUNK), ls] if nblk > 1 else None
                vp = v_ref[pl.ds(rp, CHUNK), ls] if nblk > 1 else None
                tiles.append((ls, kc, kp))
                for side, hm in sides:
                    qh = jnp.where(hm, qp, zero)
                    doh = jnp.where(hm, dop, zero)
                    cur = (_dot_nt(qh, kc), _dot_nt(doh, vc))
                    prev = (_dot_nt(qh, kp), _dot_nt(doh, vp)) if nblk > 1 else None
                    firsts.append((qh, doh, cur, prev))
            seconds = []
            for i, (qh, doh, cur, prev) in enumerate(firsts):
                hp, c0 = i // 2, (i % 2) * HEAD_DIM
                ls = slice(hp * 128, (hp + 1) * 128)
                lse_h = lse_ref[pl.ds(r0, CHUNK), ls][:, c0:c0 + 1]
                dl_h = dl_ref[pl.ds(r0, CHUNK), ls][:, c0:c0 + 1]
                pc = jnp.exp(jnp.where(m_cur, cur[0] - lse_h, NEG))
                out = [pc.astype(BF16), (pc * (cur[1] - dl_h)).astype(BF16), None, None]
                if nblk > 1:
                    pp = jnp.exp(jnp.where(prev_ok, prev[0] - lse_h, NEG))
                    out[2:] = [pp.astype(BF16), (pp * (prev[1] - dl_h)).astype(BF16)]
                seconds.append(out)
            for hp, (ls, kc, kp) in enumerate(tiles):
                dq = jnp.zeros((CHUNK, CHUNK), F32)
                dkc = jnp.zeros((CHUNK, CHUNK), F32)
                dvc = jnp.zeros((CHUNK, CHUNK), F32)
                dkp = jnp.zeros((CHUNK, CHUNK), F32)
                dvp = jnp.zeros((CHUNK, CHUNK), F32)
                for side, hm in sides:
                    qh, doh, _, _ = firsts[2 * hp + side]
                    pcb, dsc, ppb, dsp = seconds[2 * hp + side]
                    dq = dq + _dot(dsc, jnp.where(hm, kc, zero))
                    dkc = dkc + _dot_tn(dsc, qh)
                    dvc = dvc + _dot_tn(pcb, doh)
                    if nblk > 1:
                        dq = dq + _dot(dsp, jnp.where(hm, kp, zero))
                        dkp = dkp + _dot_tn(dsp, qh)
                        dvp = dvp + _dot_tn(ppb, doh)
                dq_ref[pl.ds(r0, CHUNK), ls] = dq
                dk_ref[pl.ds(r0, CHUNK), ls] = dkc
                dv_ref[pl.ds(r0, CHUNK), ls] = dvc
                if nblk > 1:
                    @pl.when(b > 0)
                    def _():
                        dk_ref[pl.ds(rp, CHUNK), ls] += dkp
                        dv_ref[pl.ds(rp, CHUNK), ls] += dvp
            return carry

        lax.fori_loop(0, nblk, blk, 0)

    spec = pl.BlockSpec((seg, ATTN_W), lambda r: (0, r))
    return _pcall(
        body, name=f"attn_bwd_d{dil}", grid=(dil,),
        in_specs=[spec] * 6, out_specs=[spec] * 3,
        out_shape=[_sds((seg, dil * ATTN_W), F32)] * 3,
        args=(qv, kv, vv, dov, deltav, lsev), comms=comms, after=after)


def _inproj_bwd(dqs, dks, dvs, du, dvs_sgu, pos, x, dx1, w_in, g_pre, comms=()):
    def body(dq1, dq2, dq3, dk1, dk2, dk3, dv1, dv2, dv3, du_ref, dvs_ref, pos_ref, x_ref, dx1_ref, w_ref, g_ref,
             dproj_ref, gx_ref, dg_ref, slabs_a, slabs_b):
        @pl.when(pl.program_id(0) == 0)
        def _():
            dg_ref[...] = jnp.zeros_like(dg_ref)

        def total(r1, r2, r3):
            return r1[...] + _load_view(r2, slabs_a, DILATIONS[1]) + _load_view(r3, slabs_b, DILATIONS[2])

        tabs = _rot_tables(pos_ref[...])
        dproj_ref[:, 0:512] = _rope_bwd(total(dq1, dq2, dq3) * np.float32(ATTN_SCALE), tabs).astype(BF16)
        dproj_ref[:, 512:1024] = _rope_bwd(total(dk1, dk2, dk3), tabs).astype(BF16)
        dproj_ref[:, 1024:1536] = total(dv1, dv2, dv3).astype(BF16)
        dproj_ref[:, 1536:2048] = du_ref[...]
        dproj_ref[:, 2048:2560] = dvs_ref[...]
        dh = jnp.zeros((TM, D_MODEL), F32)
        for s in range(N_SHARD):
            dh = dh + _dot_nt(dproj_ref[:, s * IN_S:(s + 1) * IN_S], w_ref[s])
        g = g_ref[...]
        xh, r = _rms_stats(x_ref[...])
        dx, dg = _rms_bwd(xh, r, g, dh)
        dg_ref[...] += dg
        gx_ref[...] = dx1_ref[...] + dx

    return _pcall(
        body, name="inproj_bwd", grid=(SEQ // TM,),
        in_specs=[_view_rows(dil) for dil in DILATIONS] * 3
        + [_rows(512), _rows(512), _rows(1), _rows(D_MODEL), _rows(D_MODEL),
           _resident((N_SHARD, D_MODEL, IN_S)), _resident((1, D_MODEL))],
        out_specs=[_rows(PROJ_W), _rows(D_MODEL), _acc(D_MODEL)],
        out_shape=[_sds((SEQ, PROJ_W), BF16), _sds((SEQ, D_MODEL), F32), _sds((1, D_MODEL), F32)],
        scratch_shapes=[_slab_scratch(), _slab_scratch()],
        args=(*dqs, *dks, *dvs, du, dvs_sgu, pos, x, dx1, w_in, g_pre), comms=comms)


def _to_view(a, dil):
    return a if dil == 1 else a.reshape(SEQ // dil, dil * a.shape[1])


def _from_view(a, dil):
    return a if dil == 1 else a.reshape(SEQ, a.shape[1] // dil)


def _local_step(x, pos, target, w_in, w_out, w_gate, w_up, w_down, small):
    b_t = small["sgu_b_spatial"].T
    h, u, vs, *qkv = _inproj_fwd(x, pos, small["pre_mix_norm"], w_in)
    sgu = _sgu_fwd(u, vs, small["sgu_ln_gain"], small["sgu_ln_bias"], small["sgu_w_spatial"], b_t)
    views = [tuple(qkv[3 * i:3 * i + 3]) for i in range(len(DILATIONS))]
    o_list, l_list = [], []
    for dil, (qv, kv, vv) in zip(DILATIONS, views):
        o, l = _attn_fwd(qv, kv, vv, dil)
        o_list.append(o)
        l_list.append(l)
    attn, mixed, y, x1, *lses = _mix_out_fwd(o_list, l_list, sgu, x, w_out, small["attn_out_norm"],
                                             small["sgu_out_norm"], small["post_mix_norm"])
    h2, a, dg, dup, df, dx1, loss_cols, d_pre_ffn, d_post_ffn = _ffn_fwd_bwd(
        x1, target, w_gate, w_up, w_down, small["pre_ffn_norm"], small["post_ffn_norm"])

    full_tok = pl.BlockSpec((SEQ, D_MODEL), lambda s: (0, 0), pipeline_mode=pl.Buffered(1))
    ff_tok = pl.BlockSpec((1, SEQ, FF_S), lambda s: (s, 0, 0))
    gw_gate = _wgrad(dg, h2, ff_tok, full_tok, (FF_S, D_MODEL), "wgrad_gate")
    gw_up = _wgrad(dup, h2, ff_tok, full_tok, (FF_S, D_MODEL), "wgrad_up")
    gw_down = _wgrad(a, df, ff_tok, full_tok, (FF_S, D_MODEL), "wgrad_down")

    dy, dsgu, d_post_mix, d_attn_norm, d_sgu_norm, *dviews = _outproj_bwd(
        dx1, y, attn, sgu, w_out, small["post_mix_norm"], small["attn_out_norm"], small["sgu_out_norm"])
    gw_out = _wgrad(mixed, dy, pl.BlockSpec((SEQ, OUT_S), lambda s: (0, s)), full_tok, (OUT_S, D_MODEL), "wgrad_out")
    du, dvs_sgu, d_w_sp, d_b_sp, d_ln_gain, d_ln_bias = _sgu_bwd(
        u, vs, dsgu, small["sgu_ln_gain"], small["sgu_ln_bias"], small["sgu_w_spatial"], b_t)

    dqs, dks, dvs = [], [], []
    for i, (dil, (qv, kv, vv)) in enumerate(zip(DILATIONS, views)):
        dq, dk, dv = _attn_bwd(qv, kv, vv, dviews[i], dviews[3 + i], lses[i], dil)
        dqs.append(dq)
        dks.append(dk)
        dvs.append(dv)
    dproj, grad_x, d_pre_mix = _inproj_bwd(dqs, dks, dvs, du, dvs_sgu, pos, x, dx1, w_in, small["pre_mix_norm"])
    gw_in = _wgrad(h, dproj, full_tok, pl.BlockSpec((SEQ, IN_S), lambda s: (0, s)), (D_MODEL, IN_S), "wgrad_in")

    small_grads = {
        "pre_mix_norm": d_pre_mix, "sgu_ln_gain": d_ln_gain, "sgu_ln_bias": d_ln_bias, "sgu_w_spatial": d_w_sp,
        "sgu_b_spatial": d_b_sp, "attn_out_norm": d_attn_norm, "sgu_out_norm": d_sgu_norm,
        "post_mix_norm": d_post_mix, "pre_ffn_norm": d_pre_ffn, "post_ffn_norm": d_post_ffn,
    }
    return loss_cols, grad_x, (gw_in, gw_out, gw_gate, gw_up, gw_down), small_grads


def _coords():
    return lax.axis_index("x"), lax.axis_index("y"), lax.axis_index("c")


def _other_chips(x, y):
    return [(1 - x, y), (x, 1 - y), (1 - x, 1 - y)]


def _comm_call(body, name, n_in, out_shape, scratch_shapes):
    return pl.pallas_call(
        body, name=name, in_specs=[ANY] * n_in, out_specs=[ANY] * len(out_shape), out_shape=out_shape,
        scratch_shapes=scratch_shapes,
        compiler_params=pltpu.CompilerParams(has_side_effects=True),
    )


def _gather_weights(shards):
    n = len(shards)
    halves = [s.reshape(2, s.shape[0] // 2, s.shape[1]) for s in shards]

    def body(*refs):
        ins, outs = refs[:n], refs[n:2 * n]
        send_sems, recv_sems = refs[2 * n:]
        x, y, c = _coords()
        s_me = 2 * x + y
        chips = _other_chips(x, y)
        sibling = (x, y, 1 - c)

        def copy(k, w, shard, cc, to):
            src = ins[w].at[cc] if shard is None else outs[w].at[shard, cc]
            dst = outs[w].at[s_me if shard is None else shard, cc]
            return pltpu.make_async_remote_copy(src_ref=src, dst_ref=dst, send_sem=send_sems.at[k],
                                                recv_sem=recv_sems.at[k], device_id=to, device_id_type=MESH)

        first = [copy(j * n + w, w, None, c, (cx, cy, c)) for j, (cx, cy) in enumerate(chips) for w in range(n)]
        for cp in first:
            cp.start()
        passed = []
        for j, (cx, cy) in enumerate(chips):
            for w in range(n):
                copy(j * n + w, w, 2 * cx + cy, c, (x, y, c)).wait_recv()
                fw = copy((3 + j) * n + w, w, 2 * cx + cy, c, sibling)
                fw.start()
                passed.append(fw)
        for j, (cx, cy) in enumerate(chips):
            for w in range(n):
                copy((3 + j) * n + w, w, 2 * cx + cy, 1 - c, (x, y, c)).wait_recv()
        for cp in first + passed:
            cp.wait_send()

    out_shape = [_sds((N_SHARD,) + h.shape, h.dtype) for h in halves]
    scratch = [pltpu.SemaphoreType.DMA((6 * n,)), pltpu.SemaphoreType.DMA((6 * n,))]
    full = _comm_call(body, "comm_gather_weights", n, out_shape, scratch)(*halves)
    s_me = 2 * lax.axis_index("x") + lax.axis_index("y")
    full = [lax.dynamic_update_slice(f, h[None], (s_me, 0, 0, 0)) for f, h in zip(full, halves)]
    return [f.reshape((N_SHARD,) + s.shape) for f, s in zip(full, shards)]


def _rs_to_sibling(gws):
    n = len(gws)

    def body(*refs):
        ins, outs = refs[:n], refs[n:2 * n]
        send_sems, recv_sems = refs[2 * n:]
        x, y, c = _coords()
        copies = []
        for w in range(n):
            hw = gws[w].shape[1] // 2
            copies.append(pltpu.make_async_remote_copy(
                src_ref=ins[w].at[:, pl.ds((1 - c) * hw, hw), :], dst_ref=outs[w], send_sem=send_sems.at[w],
                recv_sem=recv_sems.at[w], device_id=(x, y, 1 - c), device_id_type=MESH))
        for cp in copies:
            cp.start()
        for cp in copies:
            cp.wait()

    out_shape = [_sds((N_SHARD, g.shape[1] // 2, g.shape[2]), g.dtype) for g in gws]
    scratch = [pltpu.SemaphoreType.DMA((n,)), pltpu.SemaphoreType.DMA((n,))]
    return _comm_call(body, "comm_rs_sibling", n, out_shape, scratch)(*gws)


def _rs_chip_sum(gw, recv, core):
    _, rows, cols = gw.shape
    hw = rows // 2

    def body(c_ref, g_ref, r_ref, o_ref):
        o_ref[...] = (g_ref[...] + r_ref[...]).astype(BF16)

    return pl.pallas_call(
        body, name="rs_chip_sum",
        grid_spec=pltpu.PrefetchScalarGridSpec(
            num_scalar_prefetch=1, grid=(N_SHARD,),
            in_specs=[pl.BlockSpec((1, hw, cols), lambda s, c_ref: (s, c_ref[0], 0)),
                      pl.BlockSpec((1, hw, cols), lambda s, c_ref: (s, 0, 0))],
            out_specs=pl.BlockSpec((1, hw, cols), lambda s, c_ref: (s, 0, 0))),
        out_shape=_sds((N_SHARD, hw, cols), BF16),
        compiler_params=_seq_params(),
    )(core, gw, recv)


def _rs_between_chips(pbs):
    n = len(pbs)

    def body(*refs):
        ins, outs = refs[:n], refs[n:2 * n]
        send_sems, recv_sems = refs[2 * n:]
        x, y, c = _coords()
        copies = []
        for j, (cx, cy) in enumerate(_other_chips(x, y)):
            for w in range(n):
                copies.append(pltpu.make_async_remote_copy(
                    src_ref=ins[w].at[2 * cx + cy], dst_ref=outs[w].at[j], send_sem=send_sems.at[j * n + w],
                    recv_sem=recv_sems.at[j * n + w], device_id=(cx, cy, c), device_id_type=MESH))
        for cp in copies:
            cp.start()
        for cp in copies:
            cp.wait()

    out_shape = [_sds((3,) + p.shape[1:], p.dtype) for p in pbs]
    scratch = [pltpu.SemaphoreType.DMA((3 * n,)), pltpu.SemaphoreType.DMA((3 * n,))]
    return _comm_call(body, "comm_rs_chips", n, out_shape, scratch)(*pbs)


def _rs_final_sum(gw, recv_sib, recv_chips, shard_core):
    _, rows, cols = gw.shape
    hw = rows // 2

    def body(sc_ref, g_ref, r_ref, rc_ref, o_ref):
        acc = g_ref[0] + r_ref[0]
        for j in range(3):
            acc = acc + rc_ref[j].astype(F32)
        o_ref[0] = acc

    return pl.pallas_call(
        body, name="rs_final_sum",
        grid_spec=pltpu.PrefetchScalarGridSpec(
            num_scalar_prefetch=1, grid=(1,),
            in_specs=[pl.BlockSpec((1, hw, cols), lambda i, sc: (sc[0], sc[1], 0)),
                      pl.BlockSpec((1, hw, cols), lambda i, sc: (sc[0], 0, 0)),
                      pl.BlockSpec((3, hw, cols), lambda i, sc: (0, 0, 0))],
            out_specs=pl.BlockSpec((1, hw, cols), lambda i, sc: (sc[1], 0, 0))),
        out_shape=_sds((2, hw, cols), F32),
        compiler_params=_seq_params(),
    )(shard_core, gw, recv_sib, recv_chips)


def _rs_join_halves(halves):
    n = len(halves)

    def body(*refs):
        bufs = refs[n:2 * n]
        send_sems, recv_sems = refs[2 * n:]
        x, y, c = _coords()
        remote = [pltpu.make_async_remote_copy(
            src_ref=bufs[w].at[c], dst_ref=bufs[w].at[c], send_sem=send_sems.at[w], recv_sem=recv_sems.at[w],
            device_id=(x, y, 1 - c), device_id_type=MESH) for w in range(n)]
        for cp in remote:
            cp.start()
        for w in range(n):
            remote[w].wait_send()
            pltpu.make_async_remote_copy(
                src_ref=bufs[w].at[c], dst_ref=bufs[w].at[1 - c], send_sem=send_sems.at[w],
                recv_sem=recv_sems.at[w], device_id=(x, y, c), device_id_type=MESH).wait_recv()

    joined = pl.pallas_call(
        body, name="comm_rs_join", in_specs=[ANY] * n, out_specs=[ANY] * n,
        out_shape=[_sds(h.shape, h.dtype) for h in halves], input_output_aliases={w: w for w in range(n)},
        scratch_shapes=[pltpu.SemaphoreType.DMA((n,)), pltpu.SemaphoreType.DMA((n,))],
        compiler_params=pltpu.CompilerParams(has_side_effects=True),
    )(*halves)
    return [j.reshape(2 * h.shape[1], h.shape[2]) for j, h in zip(joined, halves)]


def _allreduce_small(buf):
    rows, cols = buf.shape

    def body(in_ref, out_ref, slots, send_sems, recv_sems):
        x, y, c = _coords()
        me = 4 * x + 2 * y + c
        copies, peers = [], []
        for k in range(1, 8):
            px = 1 - x if (k >> 2) & 1 else x
            py = 1 - y if (k >> 1) & 1 else y
            pc = 1 - c if k & 1 else c
            peers.append(4 * px + 2 * py + pc)
            copies.append(pltpu.make_async_remote_copy(
                src_ref=in_ref, dst_ref=slots.at[me], send_sem=send_sems.at[k - 1], recv_sem=recv_sems.at[k - 1],
                device_id=(px, py, pc), device_id_type=MESH))
        for cp in copies:
            cp.start()
        slots[me] = in_ref[...]
        for k in range(7):
            pltpu.make_async_remote_copy(
                src_ref=in_ref, dst_ref=slots.at[peers[k]], send_sem=send_sems.at[k], recv_sem=recv_sems.at[k],
                device_id=(x, y, c), device_id_type=MESH).wait_recv()
        for cp in copies:
            cp.wait_send()
        acc = slots[0]
        for i in range(1, 8):
            acc = acc + slots[i]
        out_ref[...] = acc

    vmem = pl.BlockSpec(memory_space=pltpu.VMEM)
    return pl.pallas_call(
        body, name="comm_allreduce_small", in_specs=[vmem], out_specs=vmem, out_shape=_sds((rows, cols), F32),
        scratch_shapes=[pltpu.VMEM((8, rows, cols), F32), pltpu.SemaphoreType.DMA((7,)), pltpu.SemaphoreType.DMA((7,))],
        compiler_params=pltpu.CompilerParams(has_side_effects=True, vmem_limit_bytes=VMEM_LIMIT),
    )(buf)


def _adamw(w, g, m, v, block_rows, name, after=()):
    rows, cols = w.shape

    def body(w_ref, g_ref, m_ref, v_ref, *rest):
        d_ref, nm_ref, nv_ref = rest[len(after):]
        g = g_ref[...]
        m = ADAM_B1 * m_ref[...] + (1.0 - ADAM_B1) * g
        v = ADAM_B2 * v_ref[...] + (1.0 - ADAM_B2) * (g * g)
        m_hat = m / (1.0 - ADAM_B1 ** ADAM_STEP)
        v_hat = v / (1.0 - ADAM_B2 ** ADAM_STEP)
        d_ref[...] = -ADAM_LR * (m_hat / (jnp.sqrt(v_hat) + ADAM_EPS) + ADAM_WD * w_ref[...])
        nm_ref[...] = m
        nv_ref[...] = v

    spec = pl.BlockSpec((block_rows, cols), lambda i: (i, 0))
    return pl.pallas_call(
        body, name=name, grid=(rows // block_rows,), in_specs=[spec] * 4 + [ANY] * len(after), out_specs=[spec] * 3,
        out_shape=[_sds((rows, cols), F32)] * 3,
        compiler_params=_seq_params(),
    )(w, g, m, v, *after)


WEIGHTS = ("pre_mix_norm", "w_in", "sgu_ln_gain", "sgu_ln_bias", "sgu_w_spatial", "sgu_b_spatial", "attn_out_norm",
           "sgu_out_norm", "w_out", "post_mix_norm", "pre_ffn_norm", "w_gate", "w_up", "w_down", "post_ffn_norm")
BIG = ("w_in", "w_out", "w_gate", "w_up", "w_down")
BIG_ADAM_ROWS = {"w_in": 256, "w_out": 128, "w_gate": 352, "w_up": 352, "w_down": 352}
SMALL = ("pre_mix_norm", "post_mix_norm", "pre_ffn_norm", "post_ffn_norm", "sgu_ln_gain", "sgu_ln_bias",
         "attn_out_norm", "sgu_out_norm", "sgu_w_spatial", "sgu_b_spatial")


def _pack_small(d):
    flat = [d[n].reshape(-1) for n in SMALL]
    used = sum(f.shape[0] for f in flat)
    flat.append(jnp.zeros((SMALL_ROWS * 1024 - used,), F32))
    return jnp.concatenate(flat).reshape(SMALL_ROWS, 1024)


def _unpack_small(buf, shapes):
    flat = buf.reshape(-1)
    out, off = {}, 0
    for n in SMALL:
        size = int(np.prod(shapes[n]))
        out[n] = flat[off:off + size].reshape(shapes[n])
        off += size
    return out


def _kernel_unoverlapped(x, positions, pre_mix_norm, w_in, sgu_ln_gain, sgu_ln_bias, sgu_w_spatial, sgu_b_spatial, attn_out_norm, sgu_out_norm, w_out, post_mix_norm, pre_ffn_norm, w_gate, w_up, w_down, post_ffn_norm, loss_target, m_pre_mix_norm, m_w_in, m_sgu_ln_gain, m_sgu_ln_bias, m_sgu_w_spatial, m_sgu_b_spatial, m_attn_out_norm, m_sgu_out_norm, m_w_out, m_post_mix_norm, m_pre_ffn_norm, m_w_gate, m_w_up, m_w_down, m_post_ffn_norm, v_pre_mix_norm, v_w_in, v_sgu_ln_gain, v_sgu_ln_bias, v_sgu_w_spatial, v_sgu_b_spatial, v_attn_out_norm, v_sgu_out_norm, v_w_out, v_post_mix_norm, v_pre_ffn_norm, v_w_gate, v_w_up, v_w_down, v_post_ffn_norm):
    a = dict(locals())
    cx, cy, cc = _coords()
    core = jnp.stack([cc]).astype(jnp.int32)
    shard_core = jnp.stack([2 * cx + cy, cc]).astype(jnp.int32)

    full = _gather_weights([a[n][0].astype(BF16) for n in BIG])
    small = {n: (a[n][0] if a[n].ndim > 2 else a[n]) for n in SMALL}
    loss_cols, grad_x, gws, small_grads = _local_step(
        x[0], positions.reshape(SEQ, 1), loss_target[0], *full, small)
    loss = lax.psum(jnp.sum(loss_cols) * np.float32(0.5 / D_MODEL), ("x", "y", "c"))

    recv_sib = _rs_to_sibling(list(gws))
    chip_part = [_rs_chip_sum(g, r, core) for g, r in zip(gws, recv_sib)]
    recv_chips = _rs_between_chips(chip_part)
    halves = [_rs_final_sum(g, r, rc, shard_core) for g, r, rc in zip(gws, recv_sib, recv_chips)]
    big_grads = dict(zip(BIG, _rs_join_halves(halves)))

    shapes = {n: a[n].shape for n in SMALL}
    small_sum = _allreduce_small(_pack_small(small_grads))

    grads, deltas, new_m, new_v = {}, {}, {}, {}
    for n in BIG:
        grads[n] = big_grads[n][None]
        d, nm, nv = _adamw(a[n][0], big_grads[n], a["m_" + n][0], a["v_" + n][0], BIG_ADAM_ROWS[n], "adamw_" + n)
        deltas[n], new_m[n], new_v[n] = d[None], nm[None], nv[None]
    d, nm, nv = _adamw(_pack_small({n: a[n] for n in SMALL}), small_sum, _pack_small({n: a["m_" + n] for n in SMALL}),
                       _pack_small({n: a["v_" + n] for n in SMALL}), SMALL_ROWS, "adamw_small")
    grads.update(_unpack_small(small_sum, shapes))
    deltas.update(_unpack_small(d, shapes))
    new_m.update(_unpack_small(nm, shapes))
    new_v.update(_unpack_small(nv, shapes))
    return (loss, grad_x[None], *[grads[n] for n in WEIGHTS], *[deltas[n] for n in WEIGHTS],
            *[new_m[n] for n in WEIGHTS], *[new_v[n] for n in WEIGHTS])


def _remote(src, dst, send_sem, recv_sem, to):
    return pltpu.make_async_remote_copy(src_ref=src, dst_ref=dst, send_sem=send_sem, recv_sem=recv_sem,
                                        device_id=to, device_id_type=MESH)


def _halves(a):
    *lead, rows, cols = a.shape
    return a.reshape(*lead, 2, rows // 2, cols)


def _gather_ici(shards):
    n = len(shards)

    def desc(ins, outs, ss, rs, j, w, landed):
        x, y, c = _coords()
        cx, cy = _other_chips(x, y)[j]
        shard = 2 * cx + cy if landed else 2 * x + y
        return _remote(ins[w].at[c], outs[w].at[shard, c], ss.at[j * n + w], rs.at[j * n + w], (cx, cy, c))

    def start(ins, outs, ss, rs):
        for j in range(3):
            for w in range(n):
                desc(ins, outs, ss, rs, j, w, False).start()

    def finish(ins, outs, ss, rs):
        for j in range(3):
            for w in range(n):
                desc(ins, outs, ss, rs, j, w, True).wait_recv()
                desc(ins, outs, ss, rs, j, w, False).wait_send()

    return _Comm(shards, [_sds((N_SHARD,) + s.shape, s.dtype) for s in shards], 3 * n, start, finish)


def _gather_pass(fulls):
    n = len(fulls)

    def desc(bufs, ss, rs, j, w, landed):
        x, y, c = _coords()
        cx, cy = _other_chips(x, y)[j]
        shard = 2 * cx + cy
        return _remote(bufs[w].at[shard, c], bufs[w].at[shard, 1 - c if landed else c],
                       ss.at[j * n + w], rs.at[j * n + w], (x, y, 1 - c))

    def start(ins, outs, ss, rs):
        for j in range(3):
            for w in range(n):
                desc(outs, ss, rs, j, w, False).start()

    def finish(ins, outs, ss, rs):
        for j in range(3):
            for w in range(n):
                desc(outs, ss, rs, j, w, True).wait_recv()
                desc(outs, ss, rs, j, w, False).wait_send()

    return _Comm(fulls, [_sds(f.shape, f.dtype) for f in fulls], 3 * n, start, finish, aliased=True)


def _rs_sibling(gws):
    n = len(gws)

    def desc(ins, outs, ss, rs, w):
        x, y, c = _coords()
        return _remote(ins[w].at[:, 1 - c], outs[w], ss.at[w], rs.at[w], (x, y, 1 - c))

    def start(ins, outs, ss, rs):
        for w in range(n):
            desc(ins, outs, ss, rs, w).start()

    def finish(ins, outs, ss, rs):
        for w in range(n):
            desc(ins, outs, ss, rs, w).wait()

    out_shape = [_sds((N_SHARD, g.shape[1] // 2, g.shape[2]), g.dtype) for g in gws]
    return _Comm([_halves(g) for g in gws], out_shape, n, start, finish)


def _rs_chips(pbs):
    n = len(pbs)

    def desc(ins, outs, ss, rs, j, w):
        x, y, c = _coords()
        cx, cy = _other_chips(x, y)[j]
        return _remote(ins[w].at[2 * cx + cy], outs[w].at[j], ss.at[j * n + w], rs.at[j * n + w], (cx, cy, c))

    def start(ins, outs, ss, rs):
        for j in range(3):
            for w in range(n):
                desc(ins, outs, ss, rs, j, w).start()

    def finish(ins, outs, ss, rs):
        for j in range(3):
            for w in range(n):
                desc(ins, outs, ss, rs, j, w).wait()

    return _Comm(pbs, [_sds((3,) + p.shape[1:], p.dtype) for p in pbs], 3 * n, start, finish)


def _rs_join(halves):
    n = len(halves)

    def desc(bufs, ss, rs, w, landed):
        x, y, c = _coords()
        return _remote(bufs[w].at[c], bufs[w].at[1 - c if landed else c], ss.at[w], rs.at[w], (x, y, 1 - c))

    def start(ins, outs, ss, rs):
        for w in range(n):
            desc(outs, ss, rs, w, False).start()

    def finish(ins, outs, ss, rs):
        for w in range(n):
            desc(outs, ss, rs, w, True).wait_recv()
            desc(outs, ss, rs, w, False).wait_send()

    return _Comm(halves, [_sds(h.shape, h.dtype) for h in halves], n, start, finish, aliased=True)


def _small_exchange(buf):
    def desc(ins, outs, ss, rs, k, landed):
        x, y, c = _coords()
        px = 1 - x if (k >> 2) & 1 else x
        py = 1 - y if (k >> 1) & 1 else y
        pc = 1 - c if k & 1 else c
        slot = 4 * px + 2 * py + pc if landed else 4 * x + 2 * y + c
        return _remote(ins[0], outs[0].at[slot], ss.at[k - 1], rs.at[k - 1], (px, py, pc))

    def start(ins, outs, ss, rs):
        for k in range(1, 8):
            desc(ins, outs, ss, rs, k, False).start()

    def finish(ins, outs, ss, rs):
        for k in range(1, 8):
            desc(ins, outs, ss, rs, k, True).wait_recv()
            desc(ins, outs, ss, rs, k, False).wait_send()

    return _Comm([buf], [_sds((8,) + buf.shape, buf.dtype)], 7, start, finish)


HBM = pl.BlockSpec(memory_space=pltpu.HBM)
SEM = pl.BlockSpec(memory_space=pltpu.SEMAPHORE)
DATAFLOW = pltpu.SideEffectType.DATAFLOW_SIDE_EFFECTING


def _split_start(name, comm, lands, after):
    srcs = [pltpu.with_memory_space_constraint(s, pltpu.HBM) for s in comm.args]
    lands = [pltpu.with_memory_space_constraint(b, pltpu.HBM) for b in lands]
    ns, nb = len(srcs), len(lands)

    def body(*refs):
        send_sems, recv_sems = refs[ns + nb + 1], refs[ns + nb + 2]
        comm.start(refs[:ns], refs[ns:ns + nb], send_sems, recv_sems)
        refs[-1][...] = jnp.zeros_like(refs[-1])

    res = pl.pallas_call(
        body, name=name,
        out_shape=(pltpu.SemaphoreType.DMA((comm.n_sems,)), pltpu.SemaphoreType.DMA((comm.n_sems,)),
                   *[pltpu.HBM(b.shape, b.dtype) for b in srcs + lands], _sds((8, 128), F32)),
        in_specs=[HBM] * (ns + nb) + [ANY],
        out_specs=(SEM, SEM, *[HBM] * (ns + nb), pl.BlockSpec(memory_space=pltpu.VMEM)),
        input_output_aliases={i: 2 + i for i in range(ns + nb)},
        compiler_params=pltpu.CompilerParams(has_side_effects=DATAFLOW),
    )(*srcs, *lands, after)
    return res[0], res[1], list(res[2:2 + ns]), list(res[2 + ns:2 + ns + nb]), res[-1]


def _split_wait(name, comm, send_sems, recv_sems, srcs, lands, after):
    ns, nb = len(srcs), len(lands)
    after = list(after) if isinstance(after, (list, tuple)) else [after]

    def body(*refs):
        comm.finish(refs[:ns], refs[ns:ns + nb], refs[ns + nb], refs[ns + nb + 1])

    res = pl.pallas_call(
        body, name=name,
        out_shape=tuple(pltpu.HBM(b.shape, b.dtype) for b in srcs + lands),
        in_specs=[HBM] * (ns + nb) + [SEM, SEM] + [ANY] * len(after), out_specs=tuple([HBM] * (ns + nb)),
        input_output_aliases={i: i for i in range(ns + nb)},
        compiler_params=pltpu.CompilerParams(has_side_effects=DATAFLOW),
    )(*srcs, *lands, send_sems, recv_sems, *after)
    return list(res[ns:])


LOSS_ROW = "loss_cols"
SMALL_EARLY = ("post_mix_norm", "pre_ffn_norm", "post_ffn_norm", "sgu_ln_gain", "sgu_ln_bias", "attn_out_norm",
               "sgu_out_norm", "sgu_w_spatial", "sgu_b_spatial", LOSS_ROW)
SMALL_LATE = ("pre_mix_norm",)


def _pack(d, names, rows):
    flat = [d[n].reshape(-1) for n in names]
    used = sum(f.shape[0] for f in flat)
    flat.append(jnp.zeros((rows * 1024 - used,), F32))
    return jnp.concatenate(flat).reshape(rows, 1024)


def _unpack(buf, names, shapes):
    flat = buf.reshape(-1)
    out, off = {}, 0
    for n in names:
        size = int(np.prod(shapes[n]))
        out[n] = flat[off:off + size].reshape(shapes[n])
        off += size
    return out


def _comm_only(name, comms):
    return _pcall(lambda: None, name=name, grid=(1,), in_specs=[], out_specs=[], out_shape=[], args=(),
                  comms=comms)[1]


def _adam_math(w, g, m, v):
    m = ADAM_B1 * m + (1.0 - ADAM_B1) * g
    v = ADAM_B2 * v + (1.0 - ADAM_B2) * (g * g)
    m_hat = m / (1.0 - ADAM_B1 ** ADAM_STEP)
    v_hat = v / (1.0 - ADAM_B2 ** ADAM_STEP)
    return -ADAM_LR * (m_hat / (jnp.sqrt(v_hat) + ADAM_EPS) + ADAM_WD * w), m, v


def _adamw_small(own, slots, w, m, v, me):
    rows, cols = own.shape

    def body(me_ref, own_ref, slots_ref, w_ref, m_ref, v_ref, g_ref, d_ref, nm_ref, nv_ref):
        own_v = own_ref[...]
        g = jnp.where(me_ref[0] == 0, own_v, slots_ref[0])
        for i in range(1, 8):
            g = g + jnp.where(me_ref[0] == i, own_v, slots_ref[i])
        g_ref[...] = g
        d_ref[...], nm_ref[...], nv_ref[...] = _adam_math(w_ref[...], g, m_ref[...], v_ref[...])

    flat = pl.BlockSpec((rows, cols), lambda i, me_ref: (0, 0))
    return pl.pallas_call(
        body, name="adamw_small",
        grid_spec=pltpu.PrefetchScalarGridSpec(
            num_scalar_prefetch=1, grid=(1,),
            in_specs=[flat, pl.BlockSpec((8, rows, cols), lambda i, me_ref: (0, 0, 0)), flat, flat, flat],
            out_specs=[flat] * 4),
        out_shape=[_sds((rows, cols), F32)] * 4,
        compiler_params=_seq_params(),
    )(me, own, slots, w, m, v)


def kernel(x, positions, pre_mix_norm, w_in, sgu_ln_gain, sgu_ln_bias, sgu_w_spatial, sgu_b_spatial, attn_out_norm, sgu_out_norm, w_out, post_mix_norm, pre_ffn_norm, w_gate, w_up, w_down, post_ffn_norm, loss_target, m_pre_mix_norm, m_w_in, m_sgu_ln_gain, m_sgu_ln_bias, m_sgu_w_spatial, m_sgu_b_spatial, m_attn_out_norm, m_sgu_out_norm, m_w_out, m_post_mix_norm, m_pre_ffn_norm, m_w_gate, m_w_up, m_w_down, m_post_ffn_norm, v_pre_mix_norm, v_w_in, v_sgu_ln_gain, v_sgu_ln_bias, v_sgu_w_spatial, v_sgu_b_spatial, v_attn_out_norm, v_sgu_out_norm, v_w_out, v_post_mix_norm, v_pre_ffn_norm, v_w_gate, v_w_up, v_w_down, v_post_ffn_norm):
    a = dict(locals())
    cx, cy, cc = _coords()
    s_me = 2 * cx + cy
    core = jnp.stack([cc]).astype(jnp.int32)
    shard_core = jnp.stack([s_me, cc]).astype(jnp.int32)
    me = jnp.stack([4 * cx + 2 * cy + cc]).astype(jnp.int32)
    small = {n: (a[n][0] if a[n].ndim > 2 else a[n]) for n in SMALL}
    b_t = small["sgu_b_spatial"].T
    xs, pos, target = x[0], positions.reshape(SEQ, 1), loss_target[0]
    flipped = ("w_gate", "w_up")

    def big(name, n):
        return jnp.swapaxes(a[name], 1, 2)[0] if n in flipped else a[name][0]

    own = {n: _halves(big(n, n).astype(BF16)) for n in BIG}

    def with_own(full, n):
        full = lax.dynamic_update_slice(full, own[n][None], (s_me, 0, 0, 0))
        return full.reshape((N_SHARD,) + big(n, n).shape)

    ffn = ("w_gate", "w_up", "w_down")
    g_in, g_out, g_ffn = _gather_ici([own["w_in"]]), _gather_ici([own["w_out"]]), _gather_ici([own[n] for n in ffn])
    s_in = _split_start("gather_in_start", g_in, [lax.empty(o.shape, o.dtype) for o in g_in.out_shape],
                        small["pre_mix_norm"])
    s_out = _split_start("gather_out_start", g_out, [lax.empty(o.shape, o.dtype) for o in g_out.out_shape], s_in[4])
    s_ffn = _split_start("gather_ffn_start", g_ffn, [lax.empty(o.shape, o.dtype) for o in g_ffn.out_shape], s_out[4])
    in_lands = _split_wait("gather_in_wait", g_in, *s_in[:4], s_ffn[4])
    ((in_lands,),) = _comm_only("comm_pass_in", [_gather_pass(in_lands)])
    w_in_f = with_own(in_lands, "w_in")
    h, u, vs, *qkv = _inproj_fwd(xs, pos, small["pre_mix_norm"], w_in_f)
    views = [tuple(qkv[3 * i:3 * i + 3]) for i in range(len(DILATIONS))]
    o_list, l_list = [], []
    for dil, (qv, kv, vv) in zip(DILATIONS, views):
        o, l = _attn_fwd(qv, kv, vv, dil)
        o_list.append(o)
        l_list.append(l)
    out_lands = _split_wait("gather_out_wait", g_out, *s_out[:4], l_list[-1])
    sgu, ((out_lands,),) = _sgu_fwd(u, vs, small["sgu_ln_gain"], small["sgu_ln_bias"], small["sgu_w_spatial"], b_t,
                                    comms=[_gather_pass(out_lands)])
    w_out_f = with_own(out_lands, "w_out")
    ffn_lands = _split_wait("gather_ffn_wait", g_ffn, *s_ffn[:4], sgu)
    (attn, mixed, y, x1, *lses), (ffn_lands,) = _mix_out_fwd(
        o_list, l_list, sgu, xs, w_out_f, small["attn_out_norm"], small["sgu_out_norm"], small["post_mix_norm"],
        comms=[_gather_pass(ffn_lands)])
    w_gate_f, w_up_f, w_down_f = (with_own(f, n) for f, n in zip(ffn_lands, ffn))
    h2, act, dg, dup, df, dx1, loss_cols, d_pre_ffn, d_post_ffn = _ffn_fwd_bwd(
        x1, target, w_gate_f, w_up_f, w_down_f, small["pre_ffn_norm"], small["post_ffn_norm"])

    full_tok = pl.BlockSpec((SEQ, D_MODEL), lambda s: (0, 0), pipeline_mode=pl.Buffered(1))
    ff_tok = pl.BlockSpec((1, SEQ, FF_S), lambda s: (s, 0, 0))
    gw = {}
    gw["w_gate"] = _wgrad(dg, h2, ff_tok, full_tok, (FF_S, D_MODEL), "wgrad_gate")
    gw["w_up"], ((sib_gate,),) = _wgrad(dup, h2, ff_tok, full_tok, (FF_S, D_MODEL), "wgrad_up",
                                        comms=[_rs_sibling([gw["w_gate"]])])
    gw["w_down"], ((sib_up,),) = _wgrad(act, df, ff_tok, full_tok, (FF_S, D_MODEL), "wgrad_down",
                                        comms=[_rs_sibling([gw["w_up"]])])
    (dy, dsgu, d_post_mix, d_attn_norm, d_sgu_norm, *dviews), ((sib_down,),) = _outproj_bwd(
        dx1, y, attn, sgu, w_out_f, small["post_mix_norm"], small["attn_out_norm"],
        small["sgu_out_norm"], comms=[_rs_sibling([gw["w_down"]])])
    sib = {"w_gate": sib_gate, "w_up": sib_up, "w_down": sib_down}
    part = {n: _rs_chip_sum(gw[n], sib[n], core) for n in ("w_gate", "w_up", "w_down")}
    gw["w_out"] = _wgrad(mixed, dy, pl.BlockSpec((SEQ, OUT_S), lambda s: (0, s)), full_tok, (OUT_S, D_MODEL),
                         "wgrad_out")
    ffn = ("w_gate", "w_up", "w_down")
    x_ffn = _rs_chips([part[n] for n in ffn])
    s_ffn = _split_start("rs_ffn_start", x_ffn, [lax.empty(o.shape, o.dtype) for o in x_ffn.out_shape],
                         part["w_down"])
    (du, dvs_sgu, d_w_sp, d_b_sp, d_ln_gain, d_ln_bias), ((sib["w_out"],),) = _sgu_bwd(
        u, vs, dsgu, small["sgu_ln_gain"], small["sgu_ln_bias"], small["sgu_w_spatial"], b_t,
        comms=[_rs_sibling([gw["w_out"]])], after=[s_ffn[4]])
    part["w_out"] = _rs_chip_sum(gw["w_out"], sib["w_out"], core)
    packed_early = _pack({
        "sgu_ln_gain": d_ln_gain, "sgu_ln_bias": d_ln_bias, "sgu_w_spatial": d_w_sp, "sgu_b_spatial": d_b_sp,
        "attn_out_norm": d_attn_norm, "sgu_out_norm": d_sgu_norm, "post_mix_norm": d_post_mix,
        "pre_ffn_norm": d_pre_ffn, "post_ffn_norm": d_post_ffn, LOSS_ROW: loss_cols}, SMALL_EARLY, SMALL_ROWS)
    x_out, x_small = _rs_chips([part["w_out"]]), _small_exchange(packed_early)
    s_out = _split_start("rs_out_start", x_out, [lax.empty(o.shape, o.dtype) for o in x_out.out_shape], s_ffn[4])
    s_small = _split_start("small_early_start", x_small,
                           [lax.empty(o.shape, o.dtype) for o in x_small.out_shape], s_out[4])

    dqs, dks, dvs = [], [], []
    for i, (dil, (qv, kv, vv)) in enumerate(zip(DILATIONS, views)):
        dq, dk, dv = _attn_bwd(qv, kv, vv, dviews[i], dviews[3 + i], lses[i], dil, after=[s_small[4]])
        dqs.append(dq)
        dks.append(dk)
        dvs.append(dv)
    half, far, joined = {}, {}, {}
    got = _split_wait("rs_ffn_wait", x_ffn, *s_ffn[:4], dvs[-1])
    for n, f in zip(ffn, got):
        far[n] = f
        half[n] = _rs_final_sum(gw[n], sib[n], f, shard_core)
    dproj, grad_x, d_pre_mix = _inproj_bwd(dqs, dks, dvs, du, dvs_sgu, pos, xs, dx1, w_in_f, small["pre_mix_norm"])
    (far["w_out"],) = _split_wait("rs_out_wait", x_out, *s_out[:4], grad_x)
    (slots_early,) = _split_wait("small_early_wait", x_small, *s_small[:4], far["w_out"])
    half["w_out"] = _rs_final_sum(gw["w_out"], sib["w_out"], far["w_out"], shard_core)
    packed_late = _pack({"pre_mix_norm": d_pre_mix}, SMALL_LATE, 8)
    names = ffn + ("w_out",)
    gw["w_in"], (got, (slots_late,)) = _wgrad(
        h, dproj, full_tok, pl.BlockSpec((SEQ, IN_S), lambda s: (0, s)), (D_MODEL, IN_S), "wgrad_in",
        comms=[_rs_join([half[n] for n in names]), _small_exchange(packed_late)])
    joined.update(zip(names, got))
    ((sib["w_in"],),) = _comm_only("comm_rs_sibling_in", [_rs_sibling([gw["w_in"]])])
    part["w_in"] = _rs_chip_sum(gw["w_in"], sib["w_in"], core)
    x_in = _rs_chips([part["w_in"]])
    s_in = _split_start("rs_in_start", x_in, [lax.empty(o.shape, o.dtype) for o in x_in.out_shape], part["w_in"])

    grads, deltas, new_m, new_v = {}, {}, {}, {}

    def update(n, after):
        g = joined[n].reshape(big(n, n).shape)
        outs = (g, *_adamw(big(n, n), g, big("m_" + n, n), big("v_" + n, n), BIG_ADAM_ROWS[n], "adamw_" + n, after))
        grads[n], deltas[n], new_m[n], new_v[n] = (
            jnp.swapaxes(o[None], 1, 2) if n in flipped else o[None] for o in outs)

    for n in ("w_gate", "w_up", "w_down", "w_out"):
        update(n, [s_in[4]])
    (far["w_in"],) = _split_wait("rs_in_wait", x_in, *s_in[:4],
                                 [new_v[n] for n in ("w_gate", "w_up", "w_down", "w_out")])
    half["w_in"] = _rs_final_sum(gw["w_in"], sib["w_in"], far["w_in"], shard_core)
    ((joined["w_in"],),) = _comm_only("comm_rs_join_in", [_rs_join([half["w_in"]])])
    update("w_in", [])
    a[LOSS_ROW] = a["m_" + LOSS_ROW] = a["v_" + LOSS_ROW] = jnp.zeros((1, D_MODEL), F32)
    for names, rows, packed, slots in ((SMALL_EARLY, SMALL_ROWS, packed_early, slots_early),
                                       (SMALL_LATE, 8, packed_late, slots_late)):
        outs = _adamw_small(packed, slots, _pack(a, names, rows), _pack({n: a["m_" + n] for n in names}, names, rows),
                            _pack({n: a["v_" + n] for n in names}, names, rows), me)
        for dst, buf in zip((grads, deltas, new_m, new_v), outs):
            dst.update(_unpack(buf, names, {n: a[n].shape for n in names}))
    loss = jnp.sum(grads[LOSS_ROW]) * np.float32(0.5 / D_MODEL)
    return (loss, grad_x[None], *[grads[n] for n in WEIGHTS], *[deltas[n] for n in WEIGHTS],
            *[new_m[n] for n in WEIGHTS], *[new_v[n] for n in WEIGHTS])
```

```python
import numpy as np
import jax
import jax.numpy as jnp
from jax import lax
from jax.experimental import pallas as pl
from jax.experimental.pallas import tpu as pltpu

F32 = jnp.float32
BF16 = jnp.bfloat16

SEQ = 2048
D_MODEL = 1024
HEAD_DIM = 64
ATTN_W = 512
SGU_W = 512
SGU_GROUPS = 8
CHUNK = 128
DILATIONS = (1, 4, 16)
N_SHARD = 4
IN_S = 640
OUT_S = 256
FF_S = 704
PROJ_W = N_SHARD * IN_S
RMS_EPS = 1e-6
LN_EPS = 1e-5
ROPE_THETA = 500000.0
ATTN_SCALE = 1.0 / np.sqrt(HEAD_DIM)
NEG = -1e30
TM = 256
VMEM_LIMIT = 56 * 1024 * 1024
SMALL_ROWS = 136

ADAM_LR = 0.001
ADAM_B1 = 0.9
ADAM_B2 = 0.999
ADAM_EPS = 1e-08
ADAM_WD = 0.01
ADAM_STEP = 10

MESH = pl.DeviceIdType.MESH
ANY = pl.BlockSpec(memory_space=pl.ANY)


def _dot(a, b):
    return jnp.dot(a, b, preferred_element_type=F32)


def _dot_nt(a, b):
    return lax.dot_general(a, b, (((1,), (1,)), ((), ())), preferred_element_type=F32)


def _dot_tn(a, b):
    return lax.dot_general(a, b, (((0,), (0,)), ((), ())), preferred_element_type=F32)


def _dot_exact(a, b):
    return jnp.dot(a, b, preferred_element_type=F32, precision=lax.Precision.HIGHEST)


def _rms_stats(x):
    r = lax.rsqrt(jnp.mean(x * x, axis=-1, keepdims=True) + RMS_EPS)
    return x * r, r


def _rms_bwd(xh, r, gain, dy):
    dxh = dy * gain
    dx = r * (dxh - xh * jnp.mean(dxh * xh, axis=-1, keepdims=True))
    return dx, jnp.sum(dy * xh, axis=0, keepdims=True)


_ERF_ALPHA = (-2.72614225801306e-10, 2.77068142495902e-08, -2.10102402082508e-06, -5.69250639462346e-05,
              -7.34990630326855e-04, -2.95459980854025e-03, -1.60960333262415e-02)
_ERF_BETA = (-1.45660718464996e-05, -2.13374055278905e-04, -1.68282697438203e-03, -7.37332916720468e-03,
             -1.42647390514189e-02)


def _erf(x):
    x = jnp.clip(x, -4.0, 4.0)
    x2 = x * x
    p = jnp.full_like(x, _ERF_ALPHA[0])
    for a in _ERF_ALPHA[1:]:
        p = p * x2 + a
    q = jnp.full_like(x, _ERF_BETA[0])
    for b in _ERF_BETA[1:]:
        q = q * x2 + b
    return x * p / q


def _gelu(x):
    return 0.5 * x * (1.0 + _erf(x * np.float32(1.0 / np.sqrt(2.0))))


def _gelu_grad(x):
    cdf = 0.5 * (1.0 + _erf(x * np.float32(1.0 / np.sqrt(2.0))))
    pdf = jnp.exp(-0.5 * x * x) * np.float32(1.0 / np.sqrt(2.0 * np.pi))
    return cdf + x * pdf


def _sigmoid(x):
    return 1.0 / (1.0 + jnp.exp(-x))


_INV_FREQ = tuple(float(np.float32(ROPE_THETA ** (-2.0 * j / 16.0))) for j in range(8))


def _rot_tables(pos):
    lane = lax.broadcasted_iota(jnp.int32, (1, 128), 1)
    d = lane & 63
    j = d & 7
    inv = jnp.zeros((1, 128), F32)
    for jj in range(8):
        inv = jnp.where(j == jj, _INV_FREQ[jj], inv)
    ang = pos.astype(F32) * inv
    c = jnp.cos(ang)
    s = jnp.sin(ang)
    cos_t = jnp.where(d < 16, c, 1.0)
    sin_a = jnp.where(d < 8, -s, 0.0)
    sin_b = jnp.where((d >= 8) & (d < 16), s, 0.0)
    return tuple(jnp.tile(t, (1, 4)) for t in (cos_t, sin_a, sin_b))


def _rope(x, tabs):
    cos_t, sin_a, sin_b = tabs
    return x * cos_t + pltpu.roll(x, 504, 1) * sin_a + pltpu.roll(x, 8, 1) * sin_b


def _rope_bwd(dy, tabs):
    cos_t, sin_a, sin_b = tabs
    return dy * cos_t + pltpu.roll(dy * sin_a, 8, 1) + pltpu.roll(dy * sin_b, 504, 1)


def _left_half():
    return lax.broadcasted_iota(jnp.int32, (CHUNK, CHUNK), 1) < HEAD_DIM


def _group_ones():
    lane = lax.broadcasted_iota(jnp.int32, (SGU_GROUPS, SGU_W), 1)
    row = lax.broadcasted_iota(jnp.int32, (SGU_GROUPS, SGU_W), 0)
    return ((lane >> 6) == row).astype(F32)


def _masked_spatial(w_ref):
    row = lax.broadcasted_iota(jnp.int32, (CHUNK, CHUNK), 0)
    col = lax.broadcasted_iota(jnp.int32, (CHUNK, CHUNK), 1)
    return [jnp.where(col <= row, w_ref[g], 0.0).astype(BF16) for g in range(SGU_GROUPS)]


def _sgu_core(u, vs, lg, lb, wm, bias_full):
    tm = u.shape[0]
    gu = _gelu(u)
    gv = _gelu(vs)
    mu = jnp.mean(gv, axis=-1, keepdims=True)
    xc = gv - mu
    rstd = lax.rsqrt(jnp.mean(xc * xc, axis=-1, keepdims=True) + LN_EPS)
    xh = xc * rstd
    vnb = (xh * lg + lb).astype(BF16)
    left = _left_half()
    rows = []
    for c in range(tm // CHUNK):
        pieces = []
        for p in range(4):
            vp = vnb[c * CHUNK:(c + 1) * CHUNK, p * 128:(p + 1) * 128]
            pieces.append(jnp.where(left, _dot(wm[2 * p], vp), _dot(wm[2 * p + 1], vp)))
        rows.append(jnp.concatenate(pieces, axis=1) + bias_full)
    mixed = jnp.concatenate(rows, axis=0)
    return gu, xh, rstd, vnb, mixed


def _resident(shape):
    n = len(shape)
    return pl.BlockSpec(shape, lambda *_: (0,) * n, pipeline_mode=pl.Buffered(1))


def _rows(ncol, tm=TM):
    return pl.BlockSpec((tm, ncol), lambda i: (i, 0))


def _rows3(nlead, ncol, tm=TM):
    return pl.BlockSpec((nlead, tm, ncol), lambda i: (0, i, 0))


def _acc(ncol, nrow=1):
    return pl.BlockSpec((nrow, ncol), lambda i: (0, 0))


def _view_rows(dil, tm=TM):
    return pl.BlockSpec((tm // dil, dil * ATTN_W), lambda i: (i, 0))


def _view_shape(dil, dtype):
    return _sds((SEQ // dil, dil * ATTN_W), dtype)


def _slab_scratch():
    return pltpu.VMEM((4, TM, 128), F32)


def _store_view(val, out_ref, slabs, dil):
    for j in range(4):
        slabs[j] = val[:, j * 128:(j + 1) * 128]
    for r in range(dil):
        for j in range(4):
            c0 = r * ATTN_W + j * 128
            out_ref[:, c0:c0 + 128] = slabs.at[j][pl.ds(r, TM // dil, stride=dil), :].astype(out_ref.dtype)


def _load_view(in_ref, slabs, dil):
    for r in range(dil):
        for j in range(4):
            c0 = r * ATTN_W + j * 128
            slabs.at[j][pl.ds(r, TM // dil, stride=dil), :] = in_ref[:, c0:c0 + 128].astype(F32)
    return jnp.concatenate([slabs[j] for j in range(4)], axis=1)


def _seq_params():
    return pltpu.CompilerParams(dimension_semantics=("arbitrary",), vmem_limit_bytes=VMEM_LIMIT)


def _sds(shape, dtype):
    return jax.ShapeDtypeStruct(shape, dtype)


class _Comm:
    def __init__(self, args, out_shape, n_sems, start, finish, aliased=False):
        self.args, self.out_shape, self.n_sems = list(args), list(out_shape), n_sems
        self.start, self.finish, self.aliased = start, finish, aliased


def _pcall(body, *, name, grid, in_specs, out_specs, out_shape, args, scratch_shapes=(), comms=(), after=()):
    single = not isinstance(out_shape, (list, tuple))
    out_specs = [out_specs] if single else list(out_specs)
    out_shape = [out_shape] if single else list(out_shape)
    n_in, n_out, n_scr = len(in_specs), len(out_shape), len(scratch_shapes)
    c_args = [a for c in comms for a in c.args]
    c_outs = [o for c in comms for o in c.out_shape]
    aliases, ai, ao = {}, n_in, n_out
    for c in comms:
        if c.aliased:
            aliases.update({ai + k: ao + k for k in range(len(c.args))})
        ai += len(c.args)
        ao += len(c.out_shape)
    sems = [pltpu.SemaphoreType.DMA((c.n_sems,)) for c in comms for _ in range(2)]
    steps = grid[0]

    def wrapped(*refs):
        o0 = n_in + len(c_args) + len(after)
        s0 = o0 + n_out + len(c_outs)
        m_in, m_out, m_sem = refs[n_in:n_in + len(c_args)], refs[o0 + n_out:s0], refs[s0 + n_scr:]

        def each(phase):
            ii = oi = 0
            for k, c in enumerate(comms):
                getattr(c, phase)(m_in[ii:ii + len(c.args)], m_out[oi:oi + len(c.out_shape)],
                                  m_sem[2 * k], m_sem[2 * k + 1])
                ii += len(c.args)
                oi += len(c.out_shape)

        if comms:
            @pl.when(pl.program_id(0) == 0)
            def _():
                each("start")

        body(*refs[:n_in], *refs[o0:o0 + n_out], *refs[s0:s0 + n_scr])

        if comms:
            @pl.when(pl.program_id(0) == steps - 1)
            def _():
                each("finish")

    res = pl.pallas_call(
        wrapped, name=name, grid=grid,
        in_specs=list(in_specs) + [ANY] * (len(c_args) + len(after)), out_specs=out_specs + [ANY] * len(c_outs),
        out_shape=out_shape + c_outs, scratch_shapes=list(scratch_shapes) + sems,
        input_output_aliases=aliases, compiler_params=_seq_params(),
    )(*args, *c_args, *after)
    mine = res[0] if single else list(res[:n_out])
    if not comms:
        return mine
    theirs, oi = [], n_out
    for c in comms:
        theirs.append(list(res[oi:oi + len(c.out_shape)]))
        oi += len(c.out_shape)
    return mine, theirs


def _inproj_fwd(x, pos, g_pre, w_in, comms=()):
    def body(x_ref, pos_ref, g_ref, w_ref, h_ref, u_ref, vs_ref, *rest):
        qkv_refs, slabs = rest[:9], rest[9:]
        xh, _ = _rms_stats(x_ref[...])
        h = (xh * g_ref[...]).astype(BF16)
        h_ref[...] = h
        proj = jnp.concatenate([_dot(h, w_ref[s]) for s in range(N_SHARD)], axis=1)
        tabs = _rot_tables(pos_ref[...])
        u_ref[...] = proj[:, 1536:2048]
        vs_ref[...] = proj[:, 2048:2560]
        qkv = (_rope(proj[:, 0:512], tabs) * np.float32(ATTN_SCALE), _rope(proj[:, 512:1024], tabs),
               proj[:, 1024:1536])
        for t, val in enumerate(qkv):
            qkv_refs[t][...] = val.astype(BF16)
            for i, dil in enumerate(DILATIONS[1:]):
                _store_view(val, qkv_refs[3 * (i + 1) + t], slabs[t], dil)

    return _pcall(
        body, name="inproj_fwd", grid=(SEQ // TM,),
        in_specs=[_rows(D_MODEL), _rows(1), _resident((1, D_MODEL)), _resident((N_SHARD, D_MODEL, IN_S))],
        out_specs=[_rows(D_MODEL), _rows(512), _rows(512)] + [_view_rows(dil) for dil in DILATIONS for _ in range(3)],
        out_shape=[_sds((SEQ, D_MODEL), BF16), _sds((SEQ, 512), F32), _sds((SEQ, 512), F32)]
        + [_view_shape(dil, BF16) for dil in DILATIONS for _ in range(3)],
        scratch_shapes=[_slab_scratch() for _ in range(3)],
        args=(x, pos, g_pre, w_in), comms=comms)


def _sgu_fwd(u, vs, lg, lb, w_sp, b_t, comms=()):
    def body(u_ref, vs_ref, lg_ref, lb_ref, w_ref, bt_ref, out_ref):
        wm = _masked_spatial(w_ref)
        bias_full = _dot_exact(bt_ref[...], _group_ones())
        gu, _, _, _, mixed = _sgu_core(u_ref[...], vs_ref[...], lg_ref[...], lb_ref[...], wm, bias_full)
        out_ref[...] = gu * mixed

    return _pcall(
        body, name="sgu_fwd", grid=(SEQ // TM,),
        in_specs=[_rows(SGU_W), _rows(SGU_W), _resident((1, SGU_W)), _resident((1, SGU_W)),
                  _resident((SGU_GROUPS, CHUNK, CHUNK)), _resident((CHUNK, SGU_GROUPS))],
        out_specs=_rows(SGU_W),
        out_shape=_sds((SEQ, SGU_W), F32),
        args=(u, vs, lg, lb, w_sp, b_t), comms=comms)


def _block_masks():
    row = lax.broadcasted_iota(jnp.int32, (CHUNK, CHUNK), 0)
    col = lax.broadcasted_iota(jnp.int32, (CHUNK, CHUNK), 1)
    return col <= row, col >= row


def _attn_fwd(qv, kv, vv, dil, comms=()):
    seg = SEQ // dil
    nblk = seg // CHUNK

    def body(q_ref, k_ref, v_ref, o_ref, l_ref):
        left = _left_half()
        m_cur, m_prev = _block_masks()
        zero = jnp.zeros((CHUNK, CHUNK), BF16)
        ones = (jnp.where(left, 1.0, 0.0).astype(BF16), jnp.where(left, 0.0, 1.0).astype(BF16))

        def blk(b, carry):
            r0 = pl.multiple_of(b * CHUNK, CHUNK)
            rp = pl.multiple_of(jnp.maximum(b - 1, 0) * CHUNK, CHUNK)
            prev_ok = m_prev & (b > 0)
            sides = tuple(enumerate((left, ~left)))
            tiles, scores = [], []
            for hp in range(4):
                ls = slice(hp * 128, (hp + 1) * 128)
                qp = q_ref[pl.ds(r0, CHUNK), ls]
                kc = k_ref[pl.ds(r0, CHUNK), ls]
                kp = k_ref[pl.ds(rp, CHUNK), ls] if nblk > 1 else None
                tiles.append((ls, v_ref[pl.ds(r0, CHUNK), ls], v_ref[pl.ds(rp, CHUNK), ls] if nblk > 1 else None))
                for _, hm in sides:
                    qh = jnp.where(hm, qp, zero)
                    sc = jnp.where(m_cur, _dot_nt(qh, kc), NEG)
                    sp = jnp.where(prev_ok, _dot_nt(qh, kp), NEG) if nblk > 1 else None
                    scores.append((sc, sp))
            probs = []
            for sc, sp in scores:
                if nblk > 1:
                    m = jnp.max(jnp.maximum(sc, sp), axis=-1, keepdims=True)
                    pc = jnp.exp(sc - m)
                    pp = jnp.exp(sp - m)
                    probs.append((m, pc.astype(BF16), pp.astype(BF16), (pc + pp).astype(BF16)))
                else:
                    m = jnp.max(sc, axis=-1, keepdims=True)
                    pc = jnp.exp(sc - m).astype(BF16)
                    probs.append((m, pc, None, pc))
            for hp, (ls, vc, vp) in enumerate(tiles):
                acc = jnp.zeros((CHUNK, CHUNK), F32)
                den = jnp.zeros((CHUNK, CHUNK), F32)
                for side, hm in sides:
                    _, pc, pp, psum = probs[2 * hp + side]
                    acc = acc + _dot(pc, jnp.where(hm, vc, zero))
                    if nblk > 1:
                        acc = acc + _dot(pp, jnp.where(hm, vp, zero))
                    den = den + _dot(psum, ones[side])
                o_ref[pl.ds(r0, CHUNK), ls] = acc / den
                l_ref[pl.ds(r0, CHUNK), ls] = jnp.where(left, probs[2 * hp][0], probs[2 * hp + 1][0]) + jnp.log(den)
            return carry

        lax.fori_loop(0, nblk, blk, 0)

    spec = pl.BlockSpec((seg, ATTN_W), lambda r: (0, r))
    return _pcall(
        body, name=f"attn_fwd_d{dil}", grid=(dil,),
        in_specs=[spec, spec, spec], out_specs=[spec, spec],
        out_shape=[_sds((seg, dil * ATTN_W), F32), _sds((seg, dil * ATTN_W), F32)],
        args=(qv, kv, vv), comms=comms)


def _mix_out_fwd(o_list, l_list, sgu, x, w_out, g_attn, g_sgu, g_post, comms=()):
    def body(o1, o2, o3, l1, l2, l3, sgu_ref, x_ref, w_ref, ga_ref, gs_ref, gp_ref,
             attn_ref, mixed_ref, y_ref, x1_ref, lse1_ref, lse2_ref, lse3_ref, slabs_a, slabs_b):
        os = [o1[...], _load_view(o2, slabs_a, DILATIONS[1]), _load_view(o3, slabs_b, DILATIONS[2])]
        ls = [l1[...], _load_view(l2, slabs_a, DILATIONS[1]), _load_view(l3, slabs_b, DILATIONS[2])]
        m = jnp.maximum(jnp.maximum(ls[0], ls[1]), ls[2])
        es = [jnp.exp(l - m) for l in ls]
        den = es[0] + es[1] + es[2]
        attn = (es[0] * os[0] + es[1] * os[1] + es[2] * os[2]) / den
        attn_ref[...] = attn
        lse = m + jnp.log(den)
        lse1_ref[...] = lse
        _store_view(lse, lse2_ref, slabs_a, DILATIONS[1])
        _store_view(lse, lse3_ref, slabs_b, DILATIONS[2])
        ah, _ = _rms_stats(attn)
        sh, _ = _rms_stats(sgu_ref[...])
        mixed = jnp.concatenate([ah * ga_ref[...], sh * gs_ref[...]], axis=1).astype(BF16)
        mixed_ref[...] = mixed
        y = _dot(mixed[:, 0:OUT_S], w_ref[0])
        for s in range(1, N_SHARD):
            y = y + _dot(mixed[:, s * OUT_S:(s + 1) * OUT_S], w_ref[s])
        y_ref[...] = y
        yh, _ = _rms_stats(y)
        x1_ref[...] = x_ref[...] + yh * gp_ref[...]

    return _pcall(
        body, name="mix_out_fwd", grid=(SEQ // TM,),
        in_specs=[_view_rows(dil) for dil in DILATIONS] * 2
        + [_rows(512), _rows(D_MODEL), _resident((N_SHARD, OUT_S, D_MODEL)),
           _resident((1, 512)), _resident((1, 512)), _resident((1, D_MODEL))],
        out_specs=[_rows(512), _rows(D_MODEL), _rows(D_MODEL), _rows(D_MODEL)] + [_view_rows(dil) for dil in DILATIONS],
        out_shape=[_sds((SEQ, 512), F32), _sds((SEQ, D_MODEL), BF16), _sds((SEQ, D_MODEL), F32),
                   _sds((SEQ, D_MODEL), F32)] + [_view_shape(dil, F32) for dil in DILATIONS],
        scratch_shapes=[_slab_scratch(), _slab_scratch()],
        args=(*o_list, *l_list, sgu, x, w_out, g_attn, g_sgu, g_post), comms=comms)


def _ffn_fwd_bwd(x1, target, w_gate, w_up, w_down, g_pre, g_post, comms=()):
    def body(x1_ref, t_ref, wg_ref, wu_ref, wd_ref, gpf_ref, gpo_ref,
             h2_ref, a_ref, dg_ref, dup_ref, df_ref, dx1_ref, loss_ref, dgpf_ref, dgpo_ref, g_scr, up_scr):
        @pl.when(pl.program_id(0) == 0)
        def _():
            loss_ref[...] = jnp.zeros_like(loss_ref)
            dgpf_ref[...] = jnp.zeros_like(dgpf_ref)
            dgpo_ref[...] = jnp.zeros_like(dgpo_ref)

        x1 = x1_ref[...]
        gpf = gpf_ref[...]
        gpo = gpo_ref[...]
        xh, r = _rms_stats(x1)
        h2 = (xh * gpf).astype(BF16)
        h2_ref[...] = h2
        f = jnp.zeros((TM, D_MODEL), F32)
        for s in range(N_SHARD):
            g = _dot_nt(h2, wg_ref[s])
            up = _dot_nt(h2, wu_ref[s])
            g_scr[s] = g
            up_scr[s] = up
            a = (g * _sigmoid(g) * up).astype(BF16)
            a_ref[s] = a
            f = f + _dot(a, wd_ref[s])
        fh, rf = _rms_stats(f)
        diff = x1 + fh * gpo - t_ref[...]
        loss_ref[...] += jnp.sum(diff * diff, axis=0, keepdims=True)
        dout = diff * np.float32(1.0 / D_MODEL)
        df, dgpo = _rms_bwd(fh, rf, gpo, dout)
        dgpo_ref[...] += dgpo
        dfb = df.astype(BF16)
        df_ref[...] = dfb
        dh2 = jnp.zeros((TM, D_MODEL), F32)
        for s in range(N_SHARD):
            da = _dot_nt(dfb, wd_ref[s])
            g = g_scr[s]
            up = up_scr[s]
            sg = _sigmoid(g)
            dup = (da * (g * sg)).astype(BF16)
            dg = (da * up * (sg * (1.0 + g * (1.0 - sg)))).astype(BF16)
            dg_ref[s] = dg
            dup_ref[s] = dup
            dh2 = dh2 + _dot(dg, wg_ref[s]) + _dot(dup, wu_ref[s])
        dx, dgpf = _rms_bwd(xh, r, gpf, dh2)
        dgpf_ref[...] += dgpf
        dx1_ref[...] = dout + dx

    return _pcall(
        body, name="ffn_fwd_bwd", grid=(SEQ // TM,),
        in_specs=[_rows(D_MODEL), _rows(D_MODEL), _resident((N_SHARD, FF_S, D_MODEL)),
                  _resident((N_SHARD, FF_S, D_MODEL)), _resident((N_SHARD, FF_S, D_MODEL)),
                  _resident((1, D_MODEL)), _resident((1, D_MODEL))],
        out_specs=[_rows(D_MODEL), _rows3(N_SHARD, FF_S), _rows3(N_SHARD, FF_S), _rows3(N_SHARD, FF_S),
                   _rows(D_MODEL), _rows(D_MODEL), _acc(D_MODEL), _acc(D_MODEL), _acc(D_MODEL)],
        out_shape=[_sds((SEQ, D_MODEL), BF16), _sds((N_SHARD, SEQ, FF_S), BF16), _sds((N_SHARD, SEQ, FF_S), BF16),
                   _sds((N_SHARD, SEQ, FF_S), BF16), _sds((SEQ, D_MODEL), BF16), _sds((SEQ, D_MODEL), F32),
                   _sds((1, D_MODEL), F32), _sds((1, D_MODEL), F32), _sds((1, D_MODEL), F32)],
        scratch_shapes=[pltpu.VMEM((N_SHARD, TM, FF_S), F32), pltpu.VMEM((N_SHARD, TM, FF_S), F32)],
        args=(x1, target, w_gate, w_up, w_down, g_pre, g_post), comms=comms)


def _wgrad(a, b, a_spec, b_spec, out_block, name, comms=()):
    def body(a_ref, b_ref, o_ref):
        av = a_ref[0] if len(a_ref.shape) == 3 else a_ref[...]
        bv = b_ref[0] if len(b_ref.shape) == 3 else b_ref[...]
        o_ref[0] = _dot_tn(av, bv)

    return _pcall(
        body, name=name, grid=(N_SHARD,),
        in_specs=[a_spec, b_spec],
        out_specs=pl.BlockSpec((1,) + out_block, lambda s: (s, 0, 0)),
        out_shape=_sds((N_SHARD,) + out_block, F32),
        args=(a, b), comms=comms)


def _outproj_bwd(dx1, y, attn, sgu, w_out, g_post, g_attn, g_sgu, comms=()):
    def body(dx1_ref, y_ref, attn_ref, sgu_ref, w_ref, gp_ref, ga_ref, gs_ref,
             dy_ref, dsgu_ref, dgp_ref, dga_ref, dgs_ref, *rest):
        dattn_refs, delta_refs, (slabs_a, slabs_b) = rest[0:3], rest[3:6], rest[6:]

        @pl.when(pl.program_id(0) == 0)
        def _():
            dgp_ref[...] = jnp.zeros_like(dgp_ref)
            dga_ref[...] = jnp.zeros_like(dga_ref)
            dgs_ref[...] = jnp.zeros_like(dgs_ref)

        yh, ry = _rms_stats(y_ref[...])
        dy, dgp = _rms_bwd(yh, ry, gp_ref[...], dx1_ref[...])
        dgp_ref[...] += dgp
        dyb = dy.astype(BF16)
        dy_ref[...] = dyb
        dmixed = jnp.concatenate([_dot_nt(dyb, w_ref[s]) for s in range(N_SHARD)], axis=1)
        attn = attn_ref[...]
        ah, ra = _rms_stats(attn)
        dattn, dga = _rms_bwd(ah, ra, ga_ref[...], dmixed[:, 0:512])
        dga_ref[...] += dga
        sh, rs = _rms_stats(sgu_ref[...])
        dsgu, dgs = _rms_bwd(sh, rs, gs_ref[...], dmixed[:, 512:1024])
        dgs_ref[...] += dgs
        dsgu_ref[...] = dsgu
        la = lax.broadcasted_iota(jnp.int32, (ATTN_W, ATTN_W), 0) >> 6
        lb = lax.broadcasted_iota(jnp.int32, (ATTN_W, ATTN_W), 1) >> 6
        delta = _dot_exact(dattn * attn, (la == lb).astype(F32))
        dattn_refs[0][...] = dattn.astype(BF16)
        delta_refs[0][...] = delta
        for i, dil in enumerate(DILATIONS[1:]):
            _store_view(dattn, dattn_refs[i + 1], slabs_a, dil)
            _store_view(delta, delta_refs[i + 1], slabs_b, dil)

    return _pcall(
        body, name="outproj_bwd", grid=(SEQ // TM,),
        in_specs=[_rows(D_MODEL), _rows(D_MODEL), _rows(512), _rows(512), _resident((N_SHARD, OUT_S, D_MODEL)),
                  _resident((1, D_MODEL)), _resident((1, 512)), _resident((1, 512))],
        out_specs=[_rows(D_MODEL), _rows(512), _acc(D_MODEL), _acc(512), _acc(512)]
        + [_view_rows(dil) for dil in DILATIONS] * 2,
        out_shape=[_sds((SEQ, D_MODEL), BF16), _sds((SEQ, 512), F32),
                   _sds((1, D_MODEL), F32), _sds((1, 512), F32), _sds((1, 512), F32)]
        + [_view_shape(dil, BF16) for dil in DILATIONS] + [_view_shape(dil, F32) for dil in DILATIONS],
        scratch_shapes=[_slab_scratch(), _slab_scratch()],
        args=(dx1, y, attn, sgu, w_out, g_post, g_attn, g_sgu), comms=comms)


def _sgu_bwd(u, vs, dsgu, lg, lb, w_sp, b_t, comms=(), after=()):
    nsteps = SEQ // TM

    def body(u_ref, vs_ref, ds_ref, lg_ref, lb_ref, w_ref, bt_ref,
             du_ref, dvs_ref, dw_ref, db_ref, dlg_ref, dlb_ref, dbias_scr):
        i = pl.program_id(0)

        @pl.when(i == 0)
        def _():
            dw_ref[...] = jnp.zeros_like(dw_ref)
            dlg_ref[...] = jnp.zeros_like(dlg_ref)
            dlb_ref[...] = jnp.zeros_like(dlb_ref)
            dbias_scr[...] = jnp.zeros_like(dbias_scr)

        wm = _masked_spatial(w_ref)
        ones_g = _group_ones()
        bias_full = _dot_exact(bt_ref[...], ones_g)
        u = u_ref[...]
        vs = vs_ref[...]
        lg = lg_ref[...]
        gu, xh, rstd, vnb, mixed = _sgu_core(u, vs, lg, lb_ref[...], wm, bias_full)
        dsgu = ds_ref[...]
        du_ref[...] = (dsgu * mixed * _gelu_grad(u)).astype(BF16)
        dmixed = dsgu * gu
        left = _left_half()
        dvn_rows = []
        for c in range(TM // CHUNK):
            rs = slice(c * CHUNK, (c + 1) * CHUNK)
            dm_c = dmixed[rs, :]
            dbias_scr[...] += dm_c
            pieces = []
            for p in range(4):
                ls = slice(p * 128, (p + 1) * 128)
                dmp = dm_c[:, ls]
                vp = vnb[rs, ls]
                dmb = dmp.astype(BF16)
                zero = jnp.zeros_like(dmb)
                dw_ref[2 * p] += _dot_nt(jnp.where(left, dmb, zero), vp)
                dw_ref[2 * p + 1] += _dot_nt(jnp.where(left, zero, dmb), vp)
                pieces.append(jnp.where(left, _dot_tn(wm[2 * p], dmb), _dot_tn(wm[2 * p + 1], dmb)))
            dvn_rows.append(jnp.concatenate(pieces, axis=1))
        dvn = jnp.concatenate(dvn_rows, axis=0)
        dlg_ref[...] += jnp.sum(dvn * xh, axis=0, keepdims=True)
        dlb_ref[...] += jnp.sum(dvn, axis=0, keepdims=True)
        dxh = dvn * lg
        dgv = rstd * (dxh - jnp.mean(dxh, axis=-1, keepdims=True) - xh * jnp.mean(dxh * xh, axis=-1, keepdims=True))
        dvs_ref[...] = (dgv * _gelu_grad(vs)).astype(BF16)

        @pl.when(i == nsteps - 1)
        def _():
            row = lax.broadcasted_iota(jnp.int32, (CHUNK, CHUNK), 0)
            col = lax.broadcasted_iota(jnp.int32, (CHUNK, CHUNK), 1)
            for g in range(SGU_GROUPS):
                dw_ref[g] = jnp.where(col <= row, dw_ref[g], 0.0)
            db_ref[...] = lax.dot_general(ones_g, dbias_scr[...], (((1,), (1,)), ((), ())),
                                          preferred_element_type=F32, precision=lax.Precision.HIGHEST)

    return _pcall(
        body, name="sgu_bwd", grid=(nsteps,),
        in_specs=[_rows(SGU_W), _rows(SGU_W), _rows(SGU_W), _resident((1, SGU_W)), _resident((1, SGU_W)),
                  _resident((SGU_GROUPS, CHUNK, CHUNK)), _resident((CHUNK, SGU_GROUPS))],
        out_specs=[_rows(SGU_W), _rows(SGU_W), pl.BlockSpec((SGU_GROUPS, CHUNK, CHUNK), lambda i: (0, 0, 0)),
                   _acc(CHUNK, SGU_GROUPS), _acc(SGU_W), _acc(SGU_W)],
        out_shape=[_sds((SEQ, SGU_W), BF16), _sds((SEQ, SGU_W), BF16), _sds((SGU_GROUPS, CHUNK, CHUNK), F32),
                   _sds((SGU_GROUPS, CHUNK), F32), _sds((1, SGU_W), F32), _sds((1, SGU_W), F32)],
        scratch_shapes=[pltpu.VMEM((CHUNK, SGU_W), F32)],
        args=(u, vs, dsgu, lg, lb, w_sp, b_t), comms=comms, after=after)


def _attn_bwd(qv, kv, vv, dov, deltav, lsev, dil, comms=(), after=()):
    seg = SEQ // dil
    nblk = seg // CHUNK

    def body(q_ref, k_ref, v_ref, do_ref, dl_ref, lse_ref, dq_ref, dk_ref, dv_ref):
        left = _left_half()
        m_cur, m_prev = _block_masks()

        def blk(b, carry):
            r0 = pl.multiple_of(b * CHUNK, CHUNK)
            rp = pl.multiple_of(jnp.maximum(b - 1, 0) * CHUNK, CHUNK)
            prev_ok = m_prev & (b > 0)
            sides = tuple(enumerate((left, ~left)))
            zero = jnp.zeros((CHUNK, CHUNK), BF16)
            tiles, firsts = [], []
            for hp in range(4):
                ls = slice(hp * 128, (hp + 1) * 128)
                qp = q_ref[pl.ds(r0, CHUNK), ls]
                kc = k_ref[pl.ds(r0, CHUNK), ls]
                vc = v_ref[pl.ds(r0, CHUNK), ls]
                dop = do_ref[pl.ds(r0, CHUNK), ls]
                kp = k_ref[pl.ds(rp, CHUNK), ls] if nblk > 1 else None
                vp = v_ref[pl.ds(rp, CHUNK), ls] if nblk > 1 else None
                tiles.append((ls, kc, kp))
                for side, hm in sides:
                    qh = jnp.where(hm, qp, zero)
                    doh = jnp.where(hm, dop, zero)
                    cur = (_dot_nt(qh, kc), _dot_nt(doh, vc))
                    prev = (_dot_nt(qh, kp), _dot_nt(doh, vp)) if nblk > 1 else None
                    firsts.append((qh, doh, cur, prev))
            seconds = []
            for i, (qh, doh, cur, prev) in enumerate(firsts):
                hp, c0 = i // 2, (i % 2) * HEAD_DIM
                ls = slice(hp * 128, (hp + 1) * 128)
                lse_h = lse_ref[pl.ds(r0, CHUNK), ls][:, c0:c0 + 1]
                dl_h = dl_ref[pl.ds(r0, CHUNK), ls][:, c0:c0 + 1]
                pc = jnp.exp(jnp.where(m_cur, cur[0] - lse_h, NEG))
                out = [pc.astype(BF16), (pc * (cur[1] - dl_h)).astype(BF16), None, None]
                if nblk > 1:
                    pp = jnp.exp(jnp.where(prev_ok, prev[0] - lse_h, NEG))
                    out[2:] = [pp.astype(BF16), (pp * (prev[1] - dl_h)).astype(BF16)]
                seconds.append(out)
            for hp, (ls, kc, kp) in enumerate(tiles):
                dq = jnp.zeros((CHUNK, CHUNK), F32)
                dkc = jnp.zeros((CHUNK, CHUNK), F32)
                dvc = jnp.zeros((CHUNK, CHUNK), F32)
                dkp = jnp.zeros((CHUNK, CHUNK), F32)
                dvp = jnp.zeros((CHUNK, CHUNK), F32)
                for side, hm in sides:
                    qh, doh, _, _ = firsts[2 * hp + side]
                    pcb, dsc, ppb, dsp = seconds[2 * hp + side]
                    dq = dq + _dot(dsc, jnp.where(hm, kc, zero))
                    dkc = dkc + _dot_tn(dsc, qh)
                    dvc = dvc + _dot_tn(pcb, doh)
                    if nblk > 1:
                        dq = dq + _dot(dsp, jnp.where(hm, kp, zero))
                        dkp = dkp + _dot_tn(dsp, qh)
                        dvp = dvp + _dot_tn(ppb, doh)
                dq_ref[pl.ds(r0, CHUNK), ls] = dq
                dk_ref[pl.ds(r0, CHUNK), ls] = dkc
                dv_ref[pl.ds(r0, CHUNK), ls] = dvc
                if nblk > 1:
                    @pl.when(b > 0)
                    def _():
                        dk_ref[pl.ds(rp, CHUNK), ls] += dkp
                        dv_ref[pl.ds(rp, CHUNK), ls] += dvp
            return carry

        lax.fori_loop(0, nblk, blk, 0)

    spec = pl.BlockSpec((seg, ATTN_W), lambda r: (0, r))
    return _pcall(
        body, name=f"attn_bwd_d{dil}", grid=(dil,),
        in_specs=[spec] * 6, out_specs=[spec] * 3,
        out_shape=[_sds((seg, dil * ATTN_W), F32)] * 3,
        args=(qv, kv, vv, dov, deltav, lsev), comms=comms, after=after)


def _inproj_bwd(dqs, dks, dvs, du, dvs_sgu, pos, x, dx1, w_in, g_pre, comms=()):
    def body(dq1, dq2, dq3, dk1, dk2, dk3, dv1, dv2, dv3, du_ref, dvs_ref, pos_ref, x_ref, dx1_ref, w_ref, g_ref,
             dproj_ref, gx_ref, dg_ref, slabs_a, slabs_b):
        @pl.when(pl.program_id(0) == 0)
        def _():
            dg_ref[...] = jnp.zeros_like(dg_ref)

        def total(r1, r2, r3):
            return r1[...] + _load_view(r2, slabs_a, DILATIONS[1]) + _load_view(r3, slabs_b, DILATIONS[2])

        tabs = _rot_tables(pos_ref[...])
        dproj_ref[:, 0:512] = _rope_bwd(total(dq1, dq2, dq3) * np.float32(ATTN_SCALE), tabs).astype(BF16)
        dproj_ref[:, 512:1024] = _rope_bwd(total(dk1, dk2, dk3), tabs).astype(BF16)
        dproj_ref[:, 1024:1536] = total(dv1, dv2, dv3).astype(BF16)
        dproj_ref[:, 1536:2048] = du_ref[...]
        dproj_ref[:, 2048:2560] = dvs_ref[...]
        dh = jnp.zeros((TM, D_MODEL), F32)
        for s in range(N_SHARD):
            dh = dh + _dot_nt(dproj_ref[:, s * IN_S:(s + 1) * IN_S], w_ref[s])
        g = g_ref[...]
        xh, r = _rms_stats(x_ref[...])
        dx, dg = _rms_bwd(xh, r, g, dh)
        dg_ref[...] += dg
        gx_ref[...] = dx1_ref[...] + dx

    return _pcall(
        body, name="inproj_bwd", grid=(SEQ // TM,),
        in_specs=[_view_rows(dil) for dil in DILATIONS] * 3
        + [_rows(512), _rows(512), _rows(1), _rows(D_MODEL), _rows(D_MODEL),
           _resident((N_SHARD, D_MODEL, IN_S)), _resident((1, D_MODEL))],
        out_specs=[_rows(PROJ_W), _rows(D_MODEL), _acc(D_MODEL)],
        out_shape=[_sds((SEQ, PROJ_W), BF16), _sds((SEQ, D_MODEL), F32), _sds((1, D_MODEL), F32)],
        scratch_shapes=[_slab_scratch(), _slab_scratch()],
        args=(*dqs, *dks, *dvs, du, dvs_sgu, pos, x, dx1, w_in, g_pre), comms=comms)


def _to_view(a, dil):
    return a if dil == 1 else a.reshape(SEQ // dil, dil * a.shape[1])


def _from_view(a, dil):
    return a if dil == 1 else a.reshape(SEQ, a.shape[1] // dil)


def _local_step(x, pos, target, w_in, w_out, w_gate, w_up, w_down, small):
    b_t = small["sgu_b_spatial"].T
    h, u, vs, *qkv = _inproj_fwd(x, pos, small["pre_mix_norm"], w_in)
    sgu = _sgu_fwd(u, vs, small["sgu_ln_gain"], small["sgu_ln_bias"], small["sgu_w_spatial"], b_t)
    views = [tuple(qkv[3 * i:3 * i + 3]) for i in range(len(DILATIONS))]
    o_list, l_list = [], []
    for dil, (qv, kv, vv) in zip(DILATIONS, views):
        o, l = _attn_fwd(qv, kv, vv, dil)
        o_list.append(o)
        l_list.append(l)
    attn, mixed, y, x1, *lses = _mix_out_fwd(o_list, l_list, sgu, x, w_out, small["attn_out_norm"],
                                             small["sgu_out_norm"], small["post_mix_norm"])
    h2, a, dg, dup, df, dx1, loss_cols, d_pre_ffn, d_post_ffn = _ffn_fwd_bwd(
        x1, target, w_gate, w_up, w_down, small["pre_ffn_norm"], small["post_ffn_norm"])

    full_tok = pl.BlockSpec((SEQ, D_MODEL), lambda s: (0, 0), pipeline_mode=pl.Buffered(1))
    ff_tok = pl.BlockSpec((1, SEQ, FF_S), lambda s: (s, 0, 0))
    gw_gate = _wgrad(dg, h2, ff_tok, full_tok, (FF_S, D_MODEL), "wgrad_gate")
    gw_up = _wgrad(dup, h2, ff_tok, full_tok, (FF_S, D_MODEL), "wgrad_up")
    gw_down = _wgrad(a, df, ff_tok, full_tok, (FF_S, D_MODEL), "wgrad_down")

    dy, dsgu, d_post_mix, d_attn_norm, d_sgu_norm, *dviews = _outproj_bwd(
        dx1, y, attn, sgu, w_out, small["post_mix_norm"], small["attn_out_norm"], small["sgu_out_norm"])
    gw_out = _wgrad(mixed, dy, pl.BlockSpec((SEQ, OUT_S), lambda s: (0, s)), full_tok, (OUT_S, D_MODEL), "wgrad_out")
    du, dvs_sgu, d_w_sp, d_b_sp, d_ln_gain, d_ln_bias = _sgu_bwd(
        u, vs, dsgu, small["sgu_ln_gain"], small["sgu_ln_bias"], small["sgu_w_spatial"], b_t)

    dqs, dks, dvs = [], [], []
    for i, (dil, (qv, kv, vv)) in enumerate(zip(DILATIONS, views)):
        dq, dk, dv = _attn_bwd(qv, kv, vv, dviews[i], dviews[3 + i], lses[i], dil)
        dqs.append(dq)
        dks.append(dk)
        dvs.append(dv)
    dproj, grad_x, d_pre_mix = _inproj_bwd(dqs, dks, dvs, du, dvs_sgu, pos, x, dx1, w_in, small["pre_mix_norm"])
    gw_in = _wgrad(h, dproj, full_tok, pl.BlockSpec((SEQ, IN_S), lambda s: (0, s)), (D_MODEL, IN_S), "wgrad_in")

    small_grads = {
        "pre_mix_norm": d_pre_mix, "sgu_ln_gain": d_ln_gain, "sgu_ln_bias": d_ln_bias, "sgu_w_spatial": d_w_sp,
        "sgu_b_spatial": d_b_sp, "attn_out_norm": d_attn_norm, "sgu_out_norm": d_sgu_norm,
        "post_mix_norm": d_post_mix, "pre_ffn_norm": d_pre_ffn, "post_ffn_norm": d_post_ffn,
    }
    return loss_cols, grad_x, (gw_in, gw_out, gw_gate, gw_up, gw_down), small_grads


def _coords():
    return lax.axis_index("x"), lax.axis_index("y"), lax.axis_index("c")


def _other_chips(x, y):
    return [(1 - x, y), (x, 1 - y), (1 - x, 1 - y)]


def _comm_call(body, name, n_in, out_shape, scratch_shapes):
    return pl.pallas_call(
        body, name=name, in_specs=[ANY] * n_in, out_specs=[ANY] * len(out_shape), out_shape=out_shape,
        scratch_shapes=scratch_shapes,
        compiler_params=pltpu.CompilerParams(has_side_effects=True),
    )


def _gather_weights(shards):
    n = len(shards)
    halves = [s.reshape(2, s.shape[0] // 2, s.shape[1]) for s in shards]

    def body(*refs):
        ins, outs = refs[:n], refs[n:2 * n]
        send_sems, recv_sems = refs[2 * n:]
        x, y, c = _coords()
        s_me = 2 * x + y
        chips = _other_chips(x, y)
        sibling = (x, y, 1 - c)

        def copy(k, w, shard, cc, to):
            src = ins[w].at[cc] if shard is None else outs[w].at[shard, cc]
            dst = outs[w].at[s_me if shard is None else shard, cc]
            return pltpu.make_async_remote_copy(src_ref=src, dst_ref=dst, send_sem=send_sems.at[k],
                                                recv_sem=recv_sems.at[k], device_id=to, device_id_type=MESH)

        first = [copy(j * n + w, w, None, c, (cx, cy, c)) for j, (cx, cy) in enumerate(chips) for w in range(n)]
        for cp in first:
            cp.start()
        passed = []
        for j, (cx, cy) in enumerate(chips):
            for w in range(n):
                copy(j * n + w, w, 2 * cx + cy, c, (x, y, c)).wait_recv()
                fw = copy((3 + j) * n + w, w, 2 * cx + cy, c, sibling)
                fw.start()
                passed.append(fw)
        for j, (cx, cy) in enumerate(chips):
            for w in range(n):
                copy((3 + j) * n + w, w, 2 * cx + cy, 1 - c, (x, y, c)).wait_recv()
        for cp in first + passed:
            cp.wait_send()

    out_shape = [_sds((N_SHARD,) + h.shape, h.dtype) for h in halves]
    scratch = [pltpu.SemaphoreType.DMA((6 * n,)), pltpu.SemaphoreType.DMA((6 * n,))]
    full = _comm_call(body, "comm_gather_weights", n, out_shape, scratch)(*halves)
    s_me = 2 * lax.axis_index("x") + lax.axis_index("y")
    full = [lax.dynamic_update_slice(f, h[None], (s_me, 0, 0, 0)) for f, h in zip(full, halves)]
    return [f.reshape((N_SHARD,) + s.shape) for f, s in zip(full, shards)]


def _rs_to_sibling(gws):
    n = len(gws)

    def body(*refs):
        ins, outs = refs[:n], refs[n:2 * n]
        send_sems, recv_sems = refs[2 * n:]
        x, y, c = _coords()
        copies = []
        for w in range(n):
            hw = gws[w].shape[1] // 2
            copies.append(pltpu.make_async_remote_copy(
                src_ref=ins[w].at[:, pl.ds((1 - c) * hw, hw), :], dst_ref=outs[w], send_sem=send_sems.at[w],
                recv_sem=recv_sems.at[w], device_id=(x, y, 1 - c), device_id_type=MESH))
        for cp in copies:
            cp.start()
        for cp in copies:
            cp.wait()

    out_shape = [_sds((N_SHARD, g.shape[1] // 2, g.shape[2]), g.dtype) for g in gws]
    scratch = [pltpu.SemaphoreType.DMA((n,)), pltpu.SemaphoreType.DMA((n,))]
    return _comm_call(body, "comm_rs_sibling", n, out_shape, scratch)(*gws)


def _rs_chip_sum(gw, recv, core):
    _, rows, cols = gw.shape
    hw = rows // 2

    def body(c_ref, g_ref, r_ref, o_ref):
        o_ref[...] = (g_ref[...] + r_ref[...]).astype(BF16)

    return pl.pallas_call(
        body, name="rs_chip_sum",
        grid_spec=pltpu.PrefetchScalarGridSpec(
            num_scalar_prefetch=1, grid=(N_SHARD,),
            in_specs=[pl.BlockSpec((1, hw, cols), lambda s, c_ref: (s, c_ref[0], 0)),
                      pl.BlockSpec((1, hw, cols), lambda s, c_ref: (s, 0, 0))],
            out_specs=pl.BlockSpec((1, hw, cols), lambda s, c_ref: (s, 0, 0))),
        out_shape=_sds((N_SHARD, hw, cols), BF16),
        compiler_params=_seq_params(),
    )(core, gw, recv)


def _rs_between_chips(pbs):
    n = len(pbs)

    def body(*refs):
        ins, outs = refs[:n], refs[n:2 * n]
        send_sems, recv_sems = refs[2 * n:]
        x, y, c = _coords()
        copies = []
        for j, (cx, cy) in enumerate(_other_chips(x, y)):
            for w in range(n):
                copies.append(pltpu.make_async_remote_copy(
                    src_ref=ins[w].at[2 * cx + cy], dst_ref=outs[w].at[j], send_sem=send_sems.at[j * n + w],
                    recv_sem=recv_sems.at[j * n + w], device_id=(cx, cy, c), device_id_type=MESH))
        for cp in copies:
            cp.start()
        for cp in copies:
            cp.wait()

    out_shape = [_sds((3,) + p.shape[1:], p.dtype) for p in pbs]
    scratch = [pltpu.SemaphoreType.DMA((3 * n,)), pltpu.SemaphoreType.DMA((3 * n,))]
    return _comm_call(body, "comm_rs_chips", n, out_shape, scratch)(*pbs)


def _rs_final_sum(gw, recv_sib, recv_chips, shard_core):
    _, rows, cols = gw.shape
    hw = rows // 2

    def body(sc_ref, g_ref, r_ref, rc_ref, o_ref):
        acc = g_ref[0] + r_ref[0]
        for j in range(3):
            acc = acc + rc_ref[j].astype(F32)
        o_ref[0] = acc

    return pl.pallas_call(
        body, name="rs_final_sum",
        grid_spec=pltpu.PrefetchScalarGridSpec(
            num_scalar_prefetch=1, grid=(1,),
            in_specs=[pl.BlockSpec((1, hw, cols), lambda i, sc: (sc[0], sc[1], 0)),
                      pl.BlockSpec((1, hw, cols), lambda i, sc: (sc[0], 0, 0)),
                      pl.BlockSpec((3, hw, cols), lambda i, sc: (0, 0, 0))],
            out_specs=pl.BlockSpec((1, hw, cols), lambda i, sc: (sc[1], 0, 0))),
        out_shape=_sds((2, hw, cols), F32),
        compiler_params=_seq_params(),
    )(shard_core, gw, recv_sib, recv_chips)


def _rs_join_halves(halves):
    n = len(halves)

    def body(*refs):
        bufs = refs[n:2 * n]
        send_sems, recv_sems = refs[2 * n:]
        x, y, c = _coords()
        remote = [pltpu.make_async_remote_copy(
            src_ref=bufs[w].at[c], dst_ref=bufs[w].at[c], send_sem=send_sems.at[w], recv_sem=recv_sems.at[w],
            device_id=(x, y, 1 - c), device_id_type=MESH) for w in range(n)]
        for cp in remote:
            cp.start()
        for w in range(n):
            remote[w].wait_send()
            pltpu.make_async_remote_copy(
                src_ref=bufs[w].at[c], dst_ref=bufs[w].at[1 - c], send_sem=send_sems.at[w],
                recv_sem=recv_sems.at[w], device_id=(x, y, c), device_id_type=MESH).wait_recv()

    joined = pl.pallas_call(
        body, name="comm_rs_join", in_specs=[ANY] * n, out_specs=[ANY] * n,
        out_shape=[_sds(h.shape, h.dtype) for h in halves], input_output_aliases={w: w for w in range(n)},
        scratch_shapes=[pltpu.SemaphoreType.DMA((n,)), pltpu.SemaphoreType.DMA((n,))],
        compiler_params=pltpu.CompilerParams(has_side_effects=True),
    )(*halves)
    return [j.reshape(2 * h.shape[1], h.shape[2]) for j, h in zip(joined, halves)]


def _allreduce_small(buf):
    rows, cols = buf.shape

    def body(in_ref, out_ref, slots, send_sems, recv_sems):
        x, y, c = _coords()
        me = 4 * x + 2 * y + c
        copies, peers = [], []
        for k in range(1, 8):
            px = 1 - x if (k >> 2) & 1 else x
            py = 1 - y if (k >> 1) & 1 else y
            pc = 1 - c if k & 1 else c
            peers.append(4 * px + 2 * py + pc)
            copies.append(pltpu.make_async_remote_copy(
                src_ref=in_ref, dst_ref=slots.at[me], send_sem=send_sems.at[k - 1], recv_sem=recv_sems.at[k - 1],
                device_id=(px, py, pc), device_id_type=MESH))
        for cp in copies:
            cp.start()
        slots[me] = in_ref[...]
        for k in range(7):
            pltpu.make_async_remote_copy(
                src_ref=in_ref, dst_ref=slots.at[peers[k]], send_sem=send_sems.at[k], recv_sem=recv_sems.at[k],
                device_id=(x, y, c), device_id_type=MESH).wait_recv()
        for cp in copies:
            cp.wait_send()
        acc = slots[0]
        for i in range(1, 8):
            acc = acc + slots[i]
        out_ref[...] = acc

    vmem = pl.BlockSpec(memory_space=pltpu.VMEM)
    return pl.pallas_call(
        body, name="comm_allreduce_small", in_specs=[vmem], out_specs=vmem, out_shape=_sds((rows, cols), F32),
        scratch_shapes=[pltpu.VMEM((8, rows, cols), F32), pltpu.SemaphoreType.DMA((7,)), pltpu.SemaphoreType.DMA((7,))],
        compiler_params=pltpu.CompilerParams(has_side_effects=True, vmem_limit_bytes=VMEM_LIMIT),
    )(buf)


def _adamw(w, g, m, v, block_rows, name, after=()):
    rows, cols = w.shape

    def body(w_ref, g_ref, m_ref, v_ref, *rest):
        d_ref, nm_ref, nv_ref = rest[len(after):]
        g = g_ref[...]
        m = ADAM_B1 * m_ref[...] + (1.0 - ADAM_B1) * g
        v = ADAM_B2 * v_ref[...] + (1.0 - ADAM_B2) * (g * g)
        m_hat = m / (1.0 - ADAM_B1 ** ADAM_STEP)
        v_hat = v / (1.0 - ADAM_B2 ** ADAM_STEP)
        d_ref[...] = -ADAM_LR * (m_hat / (jnp.sqrt(v_hat) + ADAM_EPS) + ADAM_WD * w_ref[...])
        nm_ref[...] = m
        nv_ref[...] = v

    spec = pl.BlockSpec((block_rows, cols), lambda i: (i, 0))
    return pl.pallas_call(
        body, name=name, grid=(rows // block_rows,), in_specs=[spec] * 4 + [ANY] * len(after), out_specs=[spec] * 3,
        out_shape=[_sds((rows, cols), F32)] * 3,
        compiler_params=_seq_params(),
    )(w, g, m, v, *after)


WEIGHTS = ("pre_mix_norm", "w_in", "sgu_ln_gain", "sgu_ln_bias", "sgu_w_spatial", "sgu_b_spatial", "attn_out_norm",
           "sgu_out_norm", "w_out", "post_mix_norm", "pre_ffn_norm", "w_gate", "w_up", "w_down", "post_ffn_norm")
BIG = ("w_in", "w_out", "w_gate", "w_up", "w_down")
BIG_ADAM_ROWS = {"w_in": 256, "w_out": 128, "w_gate": 352, "w_up": 352, "w_down": 352}
SMALL = ("pre_mix_norm", "post_mix_norm", "pre_ffn_norm", "post_ffn_norm", "sgu_ln_gain", "sgu_ln_bias",
         "attn_out_norm", "sgu_out_norm", "sgu_w_spatial", "sgu_b_spatial")


def _pack_small(d):
    flat = [d[n].reshape(-1) for n in SMALL]
    used = sum(f.shape[0] for f in flat)
    flat.append(jnp.zeros((SMALL_ROWS * 1024 - used,), F32))
    return jnp.concatenate(flat).reshape(SMALL_ROWS, 1024)


def _unpack_small(buf, shapes):
    flat = buf.reshape(-1)
    out, off = {}, 0
    for n in SMALL:
        size = int(np.prod(shapes[n]))
        out[n] = flat[off:off + size].reshape(shapes[n])
        off += size
    return out


def _kernel_unoverlapped(x, positions, pre_mix_norm, w_in, sgu_ln_gain, sgu_ln_bias, sgu_w_spatial, sgu_b_spatial, attn_out_norm, sgu_out_norm, w_out, post_mix_norm, pre_ffn_norm, w_gate, w_up, w_down, post_ffn_norm, loss_target, m_pre_mix_norm, m_w_in, m_sgu_ln_gain, m_sgu_ln_bias, m_sgu_w_spatial, m_sgu_b_spatial, m_attn_out_norm, m_sgu_out_norm, m_w_out, m_post_mix_norm, m_pre_ffn_norm, m_w_gate, m_w_up, m_w_down, m_post_ffn_norm, v_pre_mix_norm, v_w_in, v_sgu_ln_gain, v_sgu_ln_bias, v_sgu_w_spatial, v_sgu_b_spatial, v_attn_out_norm, v_sgu_out_norm, v_w_out, v_post_mix_norm, v_pre_ffn_norm, v_w_gate, v_w_up, v_w_down, v_post_ffn_norm):
    a = dict(locals())
    cx, cy, cc = _coords()
    core = jnp.stack([cc]).astype(jnp.int32)
    shard_core = jnp.stack([2 * cx + cy, cc]).astype(jnp.int32)

    full = _gather_weights([a[n][0].astype(BF16) for n in BIG])
    small = {n: (a[n][0] if a[n].ndim > 2 else a[n]) for n in SMALL}
    loss_cols, grad_x, gws, small_grads = _local_step(
        x[0], positions.reshape(SEQ, 1), loss_target[0], *full, small)
    loss = lax.psum(jnp.sum(loss_cols) * np.float32(0.5 / D_MODEL), ("x", "y", "c"))

    recv_sib = _rs_to_sibling(list(gws))
    chip_part = [_rs_chip_sum(g, r, core) for g, r in zip(gws, recv_sib)]
    recv_chips = _rs_between_chips(chip_part)
    halves = [_rs_final_sum(g, r, rc, shard_core) for g, r, rc in zip(gws, recv_sib, recv_chips)]
    big_grads = dict(zip(BIG, _rs_join_halves(halves)))

    shapes = {n: a[n].shape for n in SMALL}
    small_sum = _allreduce_small(_pack_small(small_grads))

    grads, deltas, new_m, new_v = {}, {}, {}, {}
    for n in BIG:
        grads[n] = big_grads[n][None]
        d, nm, nv = _adamw(a[n][0], big_grads[n], a["m_" + n][0], a["v_" + n][0], BIG_ADAM_ROWS[n], "adamw_" + n)
        deltas[n], new_m[n], new_v[n] = d[None], nm[None], nv[None]
    d, nm, nv = _adamw(_pack_small({n: a[n] for n in SMALL}), small_sum, _pack_small({n: a["m_" + n] for n in SMALL}),
                       _pack_small({n: a["v_" + n] for n in SMALL}), SMALL_ROWS, "adamw_small")
    grads.update(_unpack_small(small_sum, shapes))
    deltas.update(_unpack_small(d, shapes))
    new_m.update(_unpack_small(nm, shapes))
    new_v.update(_unpack_small(nv, shapes))
    return (loss, grad_x[None], *[grads[n] for n in WEIGHTS], *[deltas[n] for n in WEIGHTS],
            *[new_m[n] for n in WEIGHTS], *[new_v[n] for n in WEIGHTS])


def _remote(src, dst, send_sem, recv_sem, to):
    return pltpu.make_async_remote_copy(src_ref=src, dst_ref=dst, send_sem=send_sem, recv_sem=recv_sem,
                                        device_id=to, device_id_type=MESH)


def _halves(a):
    *lead, rows, cols = a.shape
    return a.reshape(*lead, 2, rows // 2, cols)


def _gather_ici(shards):
    n = len(shards)

    def desc(ins, outs, ss, rs, j, w, landed):
        x, y, c = _coords()
        cx, cy = _other_chips(x, y)[j]
        shard = 2 * cx + cy if landed else 2 * x + y
        return _remote(ins[w].at[c], outs[w].at[shard, c], ss.at[j * n + w], rs.at[j * n + w], (cx, cy, c))

    def start(ins, outs, ss, rs):
        for j in range(3):
            for w in range(n):
                desc(ins, outs, ss, rs, j, w, False).start()

    def finish(ins, outs, ss, rs):
        for j in range(3):
            for w in range(n):
                desc(ins, outs, ss, rs, j, w, True).wait_recv()
                desc(ins, outs, ss, rs, j, w, False).wait_send()

    return _Comm(shards, [_sds((N_SHARD,) + s.shape, s.dtype) for s in shards], 3 * n, start, finish)


def _gather_pass(fulls):
    n = len(fulls)

    def desc(bufs, ss, rs, j, w, landed):
        x, y, c = _coords()
        cx, cy = _other_chips(x, y)[j]
        shard = 2 * cx + cy
        return _remote(bufs[w].at[shard, c], bufs[w].at[shard, 1 - c if landed else c],
                       ss.at[j * n + w], rs.at[j * n + w], (x, y, 1 - c))

    def start(ins, outs, ss, rs):
        for j in range(3):
            for w in range(n):
                desc(outs, ss, rs, j, w, False).start()

    def finish(ins, outs, ss, rs):
        for j in range(3):
            for w in range(n):
                desc(outs, ss, rs, j, w, True).wait_recv()
                desc(outs, ss, rs, j, w, False).wait_send()

    return _Comm(fulls, [_sds(f.shape, f.dtype) for f in fulls], 3 * n, start, finish, aliased=True)


def _rs_sibling(gws):
    n = len(gws)

    def desc(ins, outs, ss, rs, w):
        x, y, c = _coords()
        return _remote(ins[w].at[:, 1 - c], outs[w], ss.at[w], rs.at[w], (x, y, 1 - c))

    def start(ins, outs, ss, rs):
        for w in range(n):
            desc(ins, outs, ss, rs, w).start()

    def finish(ins, outs, ss, rs):
        for w in range(n):
            desc(ins, outs, ss, rs, w).wait()

    out_shape = [_sds((N_SHARD, g.shape[1] // 2, g.shape[2]), g.dtype) for g in gws]
    return _Comm([_halves(g) for g in gws], out_shape, n, start, finish)


def _rs_chips(pbs):
    n = len(pbs)

    def desc(ins, outs, ss, rs, j, w):
        x, y, c = _coords()
        cx, cy = _other_chips(x, y)[j]
        return _remote(ins[w].at[2 * cx + cy], outs[w].at[j], ss.at[j * n + w], rs.at[j * n + w], (cx, cy, c))

    def start(ins, outs, ss, rs):
        for j in range(3):
            for w in range(n):
                desc(ins, outs, ss, rs, j, w).start()

    def finish(ins, outs, ss, rs):
        for j in range(3):
            for w in range(n):
                desc(ins, outs, ss, rs, j, w).wait()

    return _Comm(pbs, [_sds((3,) + p.shape[1:], p.dtype) for p in pbs], 3 * n, start, finish)


def _rs_join(halves):
    n = len(halves)

    def desc(bufs, ss, rs, w, landed):
        x, y, c = _coords()
        return _remote(bufs[w].at[c], bufs[w].at[1 - c if landed else c], ss.at[w], rs.at[w], (x, y, 1 - c))

    def start(ins, outs, ss, rs):
        for w in range(n):
            desc(outs, ss, rs, w, False).start()

    def finish(ins, outs, ss, rs):
        for w in range(n):
            desc(outs, ss, rs, w, True).wait_recv()
            desc(outs, ss, rs, w, False).wait_send()

    return _Comm(halves, [_sds(h.shape, h.dtype) for h in halves], n, start, finish, aliased=True)


def _small_exchange(buf):
    def desc(ins, outs, ss, rs, k, landed):
        x, y, c = _coords()
        px = 1 - x if (k >> 2) & 1 else x
        py = 1 - y if (k >> 1) & 1 else y
        pc = 1 - c if k & 1 else c
        slot = 4 * px + 2 * py + pc if landed else 4 * x + 2 * y + c
        return _remote(ins[0], outs[0].at[slot], ss.at[k - 1], rs.at[k - 1], (px, py, pc))

    def start(ins, outs, ss, rs):
        for k in range(1, 8):
            desc(ins, outs, ss, rs, k, False).start()

    def finish(ins, outs, ss, rs):
        for k in range(1, 8):
            desc(ins, outs, ss, rs, k, True).wait_recv()
            desc(ins, outs, ss, rs, k, False).wait_send()

    return _Comm([buf], [_sds((8,) + buf.shape, buf.dtype)], 7, start, finish)


HBM = pl.BlockSpec(memory_space=pltpu.HBM)
SEM = pl.BlockSpec(memory_space=pltpu.SEMAPHORE)
DATAFLOW = pltpu.SideEffectType.DATAFLOW_SIDE_EFFECTING


def _split_start(name, comm, lands, after):
    srcs = [pltpu.with_memory_space_constraint(s, pltpu.HBM) for s in comm.args]
    lands = [pltpu.with_memory_space_constraint(b, pltpu.HBM) for b in lands]
    ns, nb = len(srcs), len(lands)

    def body(*refs):
        send_sems, recv_sems = refs[ns + nb + 1], refs[ns + nb + 2]
        comm.start(refs[:ns], refs[ns:ns + nb], send_sems, recv_sems)
        refs[-1][...] = jnp.zeros_like(refs[-1])

    res = pl.pallas_call(
        body, name=name,
        out_shape=(pltpu.SemaphoreType.DMA((comm.n_sems,)), pltpu.SemaphoreType.DMA((comm.n_sems,)),
                   *[pltpu.HBM(b.shape, b.dtype) for b in srcs + lands], _sds((8, 128), F32)),
        in_specs=[HBM] * (ns + nb) + [ANY],
        out_specs=(SEM, SEM, *[HBM] * (ns + nb), pl.BlockSpec(memory_space=pltpu.VMEM)),
        input_output_aliases={i: 2 + i for i in range(ns + nb)},
        compiler_params=pltpu.CompilerParams(has_side_effects=DATAFLOW),
    )(*srcs, *lands, after)
    return res[0], res[1], list(res[2:2 + ns]), list(res[2 + ns:2 + ns + nb]), res[-1]


def _split_starts(name, comms, after):
    srcs = [[pltpu.with_memory_space_constraint(s, pltpu.HBM) for s in c.args] for c in comms]
    lands = [[pltpu.with_memory_space_constraint(lax.empty(o.shape, o.dtype), pltpu.HBM) for o in c.out_shape]
             for c in comms]
    bufs = [b for k in range(len(comms)) for b in srcs[k] + lands[k]]
    nb, nc = len(bufs), len(comms)

    def body(*refs):
        sems = refs[nb + 1:nb + 1 + 2 * nc]
        off = 0
        for k, c in enumerate(comms):
            ns, nl = len(srcs[k]), len(lands[k])
            c.start(refs[off:off + ns], refs[off + ns:off + ns + nl], sems[2 * k], sems[2 * k + 1])
            off += ns + nl
        refs[-1][...] = jnp.zeros_like(refs[-1])

    res = pl.pallas_call(
        body, name=name,
        out_shape=(*[pltpu.SemaphoreType.DMA((c.n_sems,)) for c in comms for _ in range(2)],
                   *[pltpu.HBM(b.shape, b.dtype) for b in bufs], _sds((8, 128), F32)),
        in_specs=[HBM] * nb + [ANY],
        out_specs=(*[SEM] * (2 * nc), *[HBM] * nb, pl.BlockSpec(memory_space=pltpu.VMEM)),
        input_output_aliases={i: 2 * nc + i for i in range(nb)},
        compiler_params=pltpu.CompilerParams(has_side_effects=DATAFLOW),
    )(*bufs, after)
    states, off = [], 2 * nc
    for k in range(nc):
        ns, nl = len(srcs[k]), len(lands[k])
        states.append((res[2 * k], res[2 * k + 1], list(res[off:off + ns]), list(res[off + ns:off + ns + nl])))
        off += ns + nl
    return states, res[-1]


def _split_wait(name, comm, send_sems, recv_sems, srcs, lands, after):
    ns, nb = len(srcs), len(lands)
    after = list(after) if isinstance(after, (list, tuple)) else [after]

    def body(*refs):
        comm.finish(refs[:ns], refs[ns:ns + nb], refs[ns + nb], refs[ns + nb + 1])

    res = pl.pallas_call(
        body, name=name,
        out_shape=tuple(pltpu.HBM(b.shape, b.dtype) for b in srcs + lands),
        in_specs=[HBM] * (ns + nb) + [SEM, SEM] + [ANY] * len(after), out_specs=tuple([HBM] * (ns + nb)),
        input_output_aliases={i: i for i in range(ns + nb)},
        compiler_params=pltpu.CompilerParams(has_side_effects=DATAFLOW),
    )(*srcs, *lands, send_sems, recv_sems, *after)
    return list(res[ns:])


LOSS_ROW = "loss_cols"
SMALL_EARLY = ("post_mix_norm", "pre_ffn_norm", "post_ffn_norm", "sgu_ln_gain", "sgu_ln_bias", "attn_out_norm",
               "sgu_out_norm", "sgu_w_spatial", "sgu_b_spatial", LOSS_ROW)
SMALL_LATE = ("pre_mix_norm",)


def _pack(d, names, rows):
    flat = [d[n].reshape(-1) for n in names]
    used = sum(f.shape[0] for f in flat)
    flat.append(jnp.zeros((rows * 1024 - used,), F32))
    return jnp.concatenate(flat).reshape(rows, 1024)


def _unpack(buf, names, shapes):
    flat = buf.reshape(-1)
    out, off = {}, 0
    for n in names:
        size = int(np.prod(shapes[n]))
        out[n] = flat[off:off + size].reshape(shapes[n])
        off += size
    return out


def _merge(comms):
    def run(phase):
        def go(ins, outs, ss, rs):
            ii = oi = si = 0
            for c in comms:
                getattr(c, phase)(ins[ii:ii + len(c.args)], outs[oi:oi + len(c.out_shape)],
                                  ss.at[pl.ds(si, c.n_sems)], rs.at[pl.ds(si, c.n_sems)])
                ii += len(c.args)
                oi += len(c.out_shape)
                si += c.n_sems
        return go

    return _Comm([a for c in comms for a in c.args], [o for c in comms for o in c.out_shape],
                 sum(c.n_sems for c in comms), run("start"), run("finish"))


def _chip_sums(gws, recvs, core):
    n = len(gws)
    _, rows, cols = gws[0].shape
    hw = rows // 2

    def body(c_ref, *refs):
        for k in range(n):
            refs[2 * n + k][...] = (refs[k][...] + refs[n + k][...]).astype(BF16)

    mine = pl.BlockSpec((1, hw, cols), lambda s, c_ref: (s, c_ref[0], 0))
    plain = pl.BlockSpec((1, hw, cols), lambda s, c_ref: (s, 0, 0))
    return pl.pallas_call(
        body, name="rs_chip_sums",
        grid_spec=pltpu.PrefetchScalarGridSpec(num_scalar_prefetch=1, grid=(N_SHARD,),
                                               in_specs=[mine] * n + [plain] * n, out_specs=[plain] * n),
        out_shape=[_sds((N_SHARD, hw, cols), BF16)] * n,
        compiler_params=_seq_params(),
    )(core, *gws, *recvs)


def _final_sums(gws, recv_sibs, recv_chips, shard_core):
    n = len(gws)
    _, rows, cols = gws[0].shape
    hw = rows // 2

    def body(sc_ref, *refs):
        for k in range(n):
            acc = refs[k][0] + refs[n + k][0]
            for j in range(3):
                acc = acc + refs[2 * n + k][j].astype(F32)
            refs[3 * n + k][0] = acc

    return pl.pallas_call(
        body, name="rs_final_sums",
        grid_spec=pltpu.PrefetchScalarGridSpec(
            num_scalar_prefetch=1, grid=(1,),
            in_specs=[pl.BlockSpec((1, hw, cols), lambda i, sc: (sc[0], sc[1], 0))] * n
            + [pl.BlockSpec((1, hw, cols), lambda i, sc: (sc[0], 0, 0))] * n
            + [pl.BlockSpec((3, hw, cols), lambda i, sc: (0, 0, 0))] * n,
            out_specs=[pl.BlockSpec((1, hw, cols), lambda i, sc: (sc[1], 0, 0))] * n),
        out_shape=[_sds((2, hw, cols), F32)] * n,
        compiler_params=_seq_params(),
    )(shard_core, *gws, *recv_sibs, *recv_chips)


def _adamw_multi(ws, gs, ms, vs, block_rows, name, after=()):
    n = len(ws)
    rows, cols = ws[0].shape

    def body(*refs):
        outs = refs[4 * n + len(after):]
        for k in range(n):
            d, m, v = _adam_math(refs[k][...], refs[n + k][...], refs[2 * n + k][...], refs[3 * n + k][...])
            outs[3 * k][...], outs[3 * k + 1][...], outs[3 * k + 2][...] = d, m, v

    spec = pl.BlockSpec((block_rows, cols), lambda i: (i, 0))
    res = pl.pallas_call(
        body, name=name, grid=(rows // block_rows,), in_specs=[spec] * (4 * n) + [ANY] * len(after),
        out_specs=[spec] * (3 * n), out_shape=[_sds((rows, cols), F32)] * (3 * n),
        compiler_params=_seq_params(),
    )(*ws, *gs, *ms, *vs, *after)
    return [tuple(res[3 * k:3 * k + 3]) for k in range(n)]


def _wgrad_pair(a1, a2, b, a_spec, b_spec, out_block, name, comms=()):
    def body(a1_ref, a2_ref, b_ref, o1_ref, o2_ref):
        bv = b_ref[...]
        o1_ref[0] = _dot_tn(a1_ref[0], bv)
        o2_ref[0] = _dot_tn(a2_ref[0], bv)

    out_spec = pl.BlockSpec((1,) + out_block, lambda s: (s, 0, 0))
    return _pcall(
        body, name=name, grid=(N_SHARD,), in_specs=[a_spec, a_spec, b_spec], out_specs=[out_spec, out_spec],
        out_shape=[_sds((N_SHARD,) + out_block, F32)] * 2, args=(a1, a2, b), comms=comms)


def _comm_only(name, comms):
    return _pcall(lambda: None, name=name, grid=(1,), in_specs=[], out_specs=[], out_shape=[], args=(),
                  comms=comms)[1]


def _adam_math(w, g, m, v):
    m = ADAM_B1 * m + (1.0 - ADAM_B1) * g
    v = ADAM_B2 * v + (1.0 - ADAM_B2) * (g * g)
    m_hat = m / (1.0 - ADAM_B1 ** ADAM_STEP)
    v_hat = v / (1.0 - ADAM_B2 ** ADAM_STEP)
    return -ADAM_LR * (m_hat / (jnp.sqrt(v_hat) + ADAM_EPS) + ADAM_WD * w), m, v


def _adamw_small(own, slots, w, m, v, me):
    rows, cols = own.shape

    def body(me_ref, own_ref, slots_ref, w_ref, m_ref, v_ref, g_ref, d_ref, nm_ref, nv_ref):
        own_v = own_ref[...]
        g = jnp.where(me_ref[0] == 0, own_v, slots_ref[0])
        for i in range(1, 8):
            g = g + jnp.where(me_ref[0] == i, own_v, slots_ref[i])
        g_ref[...] = g
        d_ref[...], nm_ref[...], nv_ref[...] = _adam_math(w_ref[...], g, m_ref[...], v_ref[...])

    flat = pl.BlockSpec((rows, cols), lambda i, me_ref: (0, 0))
    return pl.pallas_call(
        body, name="adamw_small",
        grid_spec=pltpu.PrefetchScalarGridSpec(
            num_scalar_prefetch=1, grid=(1,),
            in_specs=[flat, pl.BlockSpec((8, rows, cols), lambda i, me_ref: (0, 0, 0)), flat, flat, flat],
            out_specs=[flat] * 4),
        out_shape=[_sds((rows, cols), F32)] * 4,
        compiler_params=_seq_params(),
    )(me, own, slots, w, m, v)


def kernel(x, positions, pre_mix_norm, w_in, sgu_ln_gain, sgu_ln_bias, sgu_w_spatial, sgu_b_spatial, attn_out_norm, sgu_out_norm, w_out, post_mix_norm, pre_ffn_norm, w_gate, w_up, w_down, post_ffn_norm, loss_target, m_pre_mix_norm, m_w_in, m_sgu_ln_gain, m_sgu_ln_bias, m_sgu_w_spatial, m_sgu_b_spatial, m_attn_out_norm, m_sgu_out_norm, m_w_out, m_post_mix_norm, m_pre_ffn_norm, m_w_gate, m_w_up, m_w_down, m_post_ffn_norm, v_pre_mix_norm, v_w_in, v_sgu_ln_gain, v_sgu_ln_bias, v_sgu_w_spatial, v_sgu_b_spatial, v_attn_out_norm, v_sgu_out_norm, v_w_out, v_post_mix_norm, v_pre_ffn_norm, v_w_gate, v_w_up, v_w_down, v_post_ffn_norm):
    a = dict(locals())
    cx, cy, cc = _coords()
    s_me = 2 * cx + cy
    core = jnp.stack([cc]).astype(jnp.int32)
    shard_core = jnp.stack([s_me, cc]).astype(jnp.int32)
    me = jnp.stack([4 * cx + 2 * cy + cc]).astype(jnp.int32)
    small = {n: (a[n][0] if a[n].ndim > 2 else a[n]) for n in SMALL}
    b_t = small["sgu_b_spatial"].T
    xs, pos, target = x[0], positions.reshape(SEQ, 1), loss_target[0]
    flipped = ("w_gate", "w_up")

    def big(name, n):
        return jnp.swapaxes(a[name], 1, 2)[0] if n in flipped else a[name][0]

    own = {n: _halves(big(n, n).astype(BF16)) for n in BIG}

    def with_own(full, n):
        full = lax.dynamic_update_slice(full, own[n][None], (s_me, 0, 0, 0))
        return full.reshape((N_SHARD,) + big(n, n).shape)

    ffn = ("w_gate", "w_up", "w_down")
    g_in, g_out, g_ffn = _gather_ici([own["w_in"]]), _gather_ici([own["w_out"]]), _gather_ici([own[n] for n in ffn])
    (s_in, s_out, s_ffn), token = _split_starts("gather_start", [g_in, g_out, g_ffn], small["pre_mix_norm"])
    in_lands = _split_wait("gather_in_wait", g_in, *s_in, token)
    ((in_lands,),) = _comm_only("comm_pass_in", [_gather_pass(in_lands)])
    w_in_f = with_own(in_lands, "w_in")
    h, u, vs, *qkv = _inproj_fwd(xs, pos, small["pre_mix_norm"], w_in_f)
    views = [tuple(qkv[3 * i:3 * i + 3]) for i in range(len(DILATIONS))]
    o_list, l_list = [], []
    for dil, (qv, kv, vv) in zip(DILATIONS, views):
        o, l = _attn_fwd(qv, kv, vv, dil)
        o_list.append(o)
        l_list.append(l)
    out_lands = _split_wait("gather_out_wait", g_out, *s_out, l_list[-1])
    sgu, ((out_lands,),) = _sgu_fwd(u, vs, small["sgu_ln_gain"], small["sgu_ln_bias"], small["sgu_w_spatial"], b_t,
                                    comms=[_gather_pass(out_lands)])
    w_out_f = with_own(out_lands, "w_out")
    ffn_lands = _split_wait("gather_ffn_wait", g_ffn, *s_ffn, sgu)
    (attn, mixed, y, x1, *lses), (ffn_lands,) = _mix_out_fwd(
        o_list, l_list, sgu, xs, w_out_f, small["attn_out_norm"], small["sgu_out_norm"], small["post_mix_norm"],
        comms=[_gather_pass(ffn_lands)])
    w_gate_f, w_up_f, w_down_f = (with_own(f, n) for f, n in zip(ffn_lands, ffn))
    h2, act, dg, dup, df, dx1, loss_cols, d_pre_ffn, d_post_ffn = _ffn_fwd_bwd(
        x1, target, w_gate_f, w_up_f, w_down_f, small["pre_ffn_norm"], small["post_ffn_norm"])

    full_tok = pl.BlockSpec((SEQ, D_MODEL), lambda s: (0, 0), pipeline_mode=pl.Buffered(1))
    ff_tok = pl.BlockSpec((1, SEQ, FF_S), lambda s: (s, 0, 0))
    gw = {}
    gw["w_gate"], gw["w_up"] = _wgrad_pair(dg, dup, h2, ff_tok, full_tok, (FF_S, D_MODEL), "wgrad_gate_up")
    gw["w_down"], ((sib_gate, sib_up),) = _wgrad(act, df, ff_tok, full_tok, (FF_S, D_MODEL), "wgrad_down",
                                                 comms=[_rs_sibling([gw["w_gate"], gw["w_up"]])])
    (dy, dsgu, d_post_mix, d_attn_norm, d_sgu_norm, *dviews), ((sib_down,),) = _outproj_bwd(
        dx1, y, attn, sgu, w_out_f, small["post_mix_norm"], small["attn_out_norm"],
        small["sgu_out_norm"], comms=[_rs_sibling([gw["w_down"]])])
    sib = {"w_gate": sib_gate, "w_up": sib_up, "w_down": sib_down}
    part = dict(zip(ffn, _chip_sums([gw[n] for n in ffn], [sib[n] for n in ffn], core)))
    gw["w_out"] = _wgrad(mixed, dy, pl.BlockSpec((SEQ, OUT_S), lambda s: (0, s)), full_tok, (OUT_S, D_MODEL),
                         "wgrad_out")
    x_ffn = _rs_chips([part[n] for n in ffn])
    (s_ffn,), token = _split_starts("rs_ffn_start", [x_ffn], part["w_down"])
    (du, dvs_sgu, d_w_sp, d_b_sp, d_ln_gain, d_ln_bias), ((sib["w_out"],),) = _sgu_bwd(
        u, vs, dsgu, small["sgu_ln_gain"], small["sgu_ln_bias"], small["sgu_w_spatial"], b_t,
        comms=[_rs_sibling([gw["w_out"]])], after=[token])
    part["w_out"] = _rs_chip_sum(gw["w_out"], sib["w_out"], core)
    packed_early = _pack({
        "sgu_ln_gain": d_ln_gain, "sgu_ln_bias": d_ln_bias, "sgu_w_spatial": d_w_sp, "sgu_b_spatial": d_b_sp,
        "attn_out_norm": d_attn_norm, "sgu_out_norm": d_sgu_norm, "post_mix_norm": d_post_mix,
        "pre_ffn_norm": d_pre_ffn, "post_ffn_norm": d_post_ffn, LOSS_ROW: loss_cols}, SMALL_EARLY, SMALL_ROWS)
    x_out, x_small = _rs_chips([part["w_out"]]), _small_exchange(packed_early)
    (s_out, s_small), token = _split_starts("rs_out_small_start", [x_out, x_small], token)

    dqs, dks, dvs = [], [], []
    for i, (dil, (qv, kv, vv)) in enumerate(zip(DILATIONS, views)):
        dq, dk, dv = _attn_bwd(qv, kv, vv, dviews[i], dviews[3 + i], lses[i], dil, after=[token])
        dqs.append(dq)
        dks.append(dk)
        dvs.append(dv)
    half, far, joined = {}, {}, {}
    far.update(zip(ffn, _split_wait("rs_ffn_wait", x_ffn, *s_ffn, dvs[-1])))
    half.update(zip(ffn, _final_sums([gw[n] for n in ffn], [sib[n] for n in ffn], [far[n] for n in ffn],
                                     shard_core)))
    dproj, grad_x, d_pre_mix = _inproj_bwd(dqs, dks, dvs, du, dvs_sgu, pos, xs, dx1, w_in_f, small["pre_mix_norm"])
    (far["w_out"],) = _split_wait("rs_out_wait", x_out, *s_out, grad_x)
    (slots_early,) = _split_wait("small_early_wait", x_small, *s_small, far["w_out"])
    half["w_out"] = _rs_final_sum(gw["w_out"], sib["w_out"], far["w_out"], shard_core)
    packed_late = _pack({"pre_mix_norm": d_pre_mix}, SMALL_LATE, 8)
    names = ffn + ("w_out",)
    gw["w_in"], (got, (slots_late,)) = _wgrad(
        h, dproj, full_tok, pl.BlockSpec((SEQ, IN_S), lambda s: (0, s)), (D_MODEL, IN_S), "wgrad_in",
        comms=[_rs_join([half[n] for n in names]), _small_exchange(packed_late)])
    joined.update(zip(names, got))
    ((sib["w_in"],),) = _comm_only("comm_rs_sibling_in", [_rs_sibling([gw["w_in"]])])
    part["w_in"] = _rs_chip_sum(gw["w_in"], sib["w_in"], core)
    x_in = _rs_chips([part["w_in"]])
    (s_in,), token = _split_starts("rs_in_start", [x_in], part["w_in"])

    grads, deltas, new_m, new_v = {}, {}, {}, {}

    def record(n, outs):
        grads[n], deltas[n], new_m[n], new_v[n] = (
            jnp.swapaxes(o[None], 1, 2) if n in flipped else o[None] for o in outs)

    def update(n, after):
        g = joined[n].reshape(big(n, n).shape)
        record(n, (g, *_adamw(big(n, n), g, big("m_" + n, n), big("v_" + n, n), BIG_ADAM_ROWS[n], "adamw_" + n,
                              after)))

    ffn_grads = [joined[n].reshape(big(n, n).shape) for n in ffn]
    for n, g, outs in zip(ffn, ffn_grads, _adamw_multi(
            [big(n, n) for n in ffn], ffn_grads, [big("m_" + n, n) for n in ffn], [big("v_" + n, n) for n in ffn],
            FF_S // 4, "adamw_ffn", [token])):
        record(n, (g, *outs))
    update("w_out", [token])
    (far["w_in"],) = _split_wait("rs_in_wait", x_in, *s_in, [new_v[n] for n in ("w_down", "w_out")])
    half["w_in"] = _rs_final_sum(gw["w_in"], sib["w_in"], far["w_in"], shard_core)
    ((joined["w_in"],),) = _comm_only("comm_rs_join_in", [_rs_join([half["w_in"]])])
    update("w_in", [])
    a[LOSS_ROW] = a["m_" + LOSS_ROW] = a["v_" + LOSS_ROW] = jnp.zeros((1, D_MODEL), F32)
    for names, rows, packed, slots in ((SMALL_EARLY, SMALL_ROWS, packed_early, slots_early),
                                       (SMALL_LATE, 8, packed_late, slots_late)):
        outs = _adamw_small(packed, slots, _pack(a, names, rows), _pack({n: a["m_" + n] for n in names}, names, rows),
                            _pack({n: a["v_" + n] for n in names}, names, rows), me)
        for dst, buf in zip((grads, deltas, new_m, new_v), outs):
            dst.update(_unpack(buf, names, {n: a[n].shape for n in names}))
    loss = jnp.sum(grads[LOSS_ROW]) * np.float32(0.5 / D_MODEL)
    return (loss, grad_x[None], *[grads[n] for n in WEIGHTS], *[deltas[n] for n in WEIGHTS],
            *[new_m[n] for n in WEIGHTS], *[new_v[n] for n in WEIGHTS])
```

```python
import numpy as np
import jax
import jax.numpy as jnp
from jax import lax
from jax.experimental import pallas as pl
from jax.experimental.pallas import tpu as pltpu

F32 = jnp.float32
BF16 = jnp.bfloat16

SEQ = 2048
D_MODEL = 1024
HEAD_DIM = 64
ATTN_W = 512
SGU_W = 512
SGU_GROUPS = 8
CHUNK = 128
DILATIONS = (1, 4, 16)
N_SHARD = 4
IN_S = 640
OUT_S = 256
FF_S = 704
PROJ_W = N_SHARD * IN_S
RMS_EPS = 1e-6
LN_EPS = 1e-5
ROPE_THETA = 500000.0
ATTN_SCALE = 1.0 / np.sqrt(HEAD_DIM)
NEG = -1e30
TM = 256
VMEM_LIMIT = 56 * 1024 * 1024
SMALL_ROWS = 136

ADAM_LR = 0.001
ADAM_B1 = 0.9
ADAM_B2 = 0.999
ADAM_EPS = 1e-08
ADAM_WD = 0.01
ADAM_STEP = 10

MESH = pl.DeviceIdType.MESH
ANY = pl.BlockSpec(memory_space=pl.ANY)


def _dot(a, b):
    return jnp.dot(a, b, preferred_element_type=F32)


def _dot_nt(a, b):
    return lax.dot_general(a, b, (((1,), (1,)), ((), ())), preferred_element_type=F32)


def _dot_tn(a, b):
    return lax.dot_general(a, b, (((0,), (0,)), ((), ())), preferred_element_type=F32)


def _dot_exact(a, b):
    return jnp.dot(a, b, preferred_element_type=F32, precision=lax.Precision.HIGHEST)


def _rms_stats(x):
    r = lax.rsqrt(jnp.mean(x * x, axis=-1, keepdims=True) + RMS_EPS)
    return x * r, r


def _rms_bwd(xh, r, gain, dy):
    dxh = dy * gain
    dx = r * (dxh - xh * jnp.mean(dxh * xh, axis=-1, keepdims=True))
    return dx, jnp.sum(dy * xh, axis=0, keepdims=True)


_ERF_ALPHA = (-2.72614225801306e-10, 2.77068142495902e-08, -2.10102402082508e-06, -5.69250639462346e-05,
              -7.34990630326855e-04, -2.95459980854025e-03, -1.60960333262415e-02)
_ERF_BETA = (-1.45660718464996e-05, -2.13374055278905e-04, -1.68282697438203e-03, -7.37332916720468e-03,
             -1.42647390514189e-02)


def _erf(x):
    x = jnp.clip(x, -4.0, 4.0)
    x2 = x * x
    p = jnp.full_like(x, _ERF_ALPHA[0])
    for a in _ERF_ALPHA[1:]:
        p = p * x2 + a
    q = jnp.full_like(x, _ERF_BETA[0])
    for b in _ERF_BETA[1:]:
        q = q * x2 + b
    return x * p / q


def _gelu(x):
    return 0.5 * x * (1.0 + _erf(x * np.float32(1.0 / np.sqrt(2.0))))


def _gelu_grad(x):
    cdf = 0.5 * (1.0 + _erf(x * np.float32(1.0 / np.sqrt(2.0))))
    pdf = jnp.exp(-0.5 * x * x) * np.float32(1.0 / np.sqrt(2.0 * np.pi))
    return cdf + x * pdf


def _sigmoid(x):
    return 1.0 / (1.0 + jnp.exp(-x))


_INV_FREQ = tuple(float(np.float32(ROPE_THETA ** (-2.0 * j / 16.0))) for j in range(8))


def _rot_tables(pos):
    lane = lax.broadcasted_iota(jnp.int32, (1, 128), 1)
    d = lane & 63
    j = d & 7
    inv = jnp.zeros((1, 128), F32)
    for jj in range(8):
        inv = jnp.where(j == jj, _INV_FREQ[jj], inv)
    ang = pos.astype(F32) * inv
    c = jnp.cos(ang)
    s = jnp.sin(ang)
    cos_t = jnp.where(d < 16, c, 1.0)
    sin_a = jnp.where(d < 8, -s, 0.0)
    sin_b = jnp.where((d >= 8) & (d < 16), s, 0.0)
    return tuple(jnp.tile(t, (1, 4)) for t in (cos_t, sin_a, sin_b))


def _rope(x, tabs):
    cos_t, sin_a, sin_b = tabs
    return x * cos_t + pltpu.roll(x, 504, 1) * sin_a + pltpu.roll(x, 8, 1) * sin_b


def _rope_bwd(dy, tabs):
    cos_t, sin_a, sin_b = tabs
    return dy * cos_t + pltpu.roll(dy * sin_a, 8, 1) + pltpu.roll(dy * sin_b, 504, 1)


def _left_half():
    return lax.broadcasted_iota(jnp.int32, (CHUNK, CHUNK), 1) < HEAD_DIM


def _group_ones():
    lane = lax.broadcasted_iota(jnp.int32, (SGU_GROUPS, SGU_W), 1)
    row = lax.broadcasted_iota(jnp.int32, (SGU_GROUPS, SGU_W), 0)
    return ((lane >> 6) == row).astype(F32)


def _masked_spatial(w_ref):
    row = lax.broadcasted_iota(jnp.int32, (CHUNK, CHUNK), 0)
    col = lax.broadcasted_iota(jnp.int32, (CHUNK, CHUNK), 1)
    return [jnp.where(col <= row, w_ref[g], 0.0).astype(BF16) for g in range(SGU_GROUPS)]


def _sgu_core(u, vs, lg, lb, wm, bias_full):
    tm = u.shape[0]
    gu = _gelu(u)
    gv = _gelu(vs)
    mu = jnp.mean(gv, axis=-1, keepdims=True)
    xc = gv - mu
    rstd = lax.rsqrt(jnp.mean(xc * xc, axis=-1, keepdims=True) + LN_EPS)
    xh = xc * rstd
    vnb = (xh * lg + lb).astype(BF16)
    left = _left_half()
    rows = []
    for c in range(tm // CHUNK):
        pieces = []
        for p in range(4):
            vp = vnb[c * CHUNK:(c + 1) * CHUNK, p * 128:(p + 1) * 128]
            pieces.append(jnp.where(left, _dot(wm[2 * p], vp), _dot(wm[2 * p + 1], vp)))
        rows.append(jnp.concatenate(pieces, axis=1) + bias_full)
    mixed = jnp.concatenate(rows, axis=0)
    return gu, xh, rstd, vnb, mixed


def _resident(shape):
    n = len(shape)
    return pl.BlockSpec(shape, lambda *_: (0,) * n, pipeline_mode=pl.Buffered(1))


def _rows(ncol, tm=TM):
    return pl.BlockSpec((tm, ncol), lambda i: (i, 0))


def _rows3(nlead, ncol, tm=TM):
    return pl.BlockSpec((nlead, tm, ncol), lambda i: (0, i, 0))


def _acc(ncol, nrow=1):
    return pl.BlockSpec((nrow, ncol), lambda i: (0, 0))


def _view_rows(dil, tm=TM):
    return pl.BlockSpec((tm // dil, dil * ATTN_W), lambda i: (i, 0))


def _view_shape(dil, dtype):
    return _sds((SEQ // dil, dil * ATTN_W), dtype)


def _slab_scratch():
    return pltpu.VMEM((4, TM, 128), F32)


def _store_view(val, out_ref, slabs, dil):
    for j in range(4):
        slabs[j] = val[:, j * 128:(j + 1) * 128]
    for r in range(dil):
        for j in range(4):
            c0 = r * ATTN_W + j * 128
            out_ref[:, c0:c0 + 128] = slabs.at[j][pl.ds(r, TM // dil, stride=dil), :].astype(out_ref.dtype)


def _load_view(in_ref, slabs, dil):
    for r in range(dil):
        for j in range(4):
            c0 = r * ATTN_W + j * 128
            slabs.at[j][pl.ds(r, TM // dil, stride=dil), :] = in_ref[:, c0:c0 + 128].astype(F32)
    return jnp.concatenate([slabs[j] for j in range(4)], axis=1)


def _seq_params():
    return pltpu.CompilerParams(dimension_semantics=("arbitrary",), vmem_limit_bytes=VMEM_LIMIT)


def _sds(shape, dtype):
    return jax.ShapeDtypeStruct(shape, dtype)


class _Comm:
    def __init__(self, args, out_shape, n_sems, start, finish, aliased=False):
        self.args, self.out_shape, self.n_sems = list(args), list(out_shape), n_sems
        self.start, self.finish, self.aliased = start, finish, aliased


def _pcall(body, *, name, grid, in_specs, out_specs, out_shape, args, scratch_shapes=(), comms=(), after=()):
    single = not isinstance(out_shape, (list, tuple))
    out_specs = [out_specs] if single else list(out_specs)
    out_shape = [out_shape] if single else list(out_shape)
    n_in, n_out, n_scr = len(in_specs), len(out_shape), len(scratch_shapes)
    c_args = [a for c in comms for a in c.args]
    c_outs = [o for c in comms for o in c.out_shape]
    aliases, ai, ao = {}, n_in, n_out
    for c in comms:
        if c.aliased:
            aliases.update({ai + k: ao + k for k in range(len(c.args))})
        ai += len(c.args)
        ao += len(c.out_shape)
    sems = [pltpu.SemaphoreType.DMA((c.n_sems,)) for c in comms for _ in range(2)]
    steps = grid[0]

    def wrapped(*refs):
        o0 = n_in + len(c_args) + len(after)
        s0 = o0 + n_out + len(c_outs)
        m_in, m_out, m_sem = refs[n_in:n_in + len(c_args)], refs[o0 + n_out:s0], refs[s0 + n_scr:]

        def each(phase):
            ii = oi = 0
            for k, c in enumerate(comms):
                getattr(c, phase)(m_in[ii:ii + len(c.args)], m_out[oi:oi + len(c.out_shape)],
                                  m_sem[2 * k], m_sem[2 * k + 1])
                ii += len(c.args)
                oi += len(c.out_shape)

        if comms:
            @pl.when(pl.program_id(0) == 0)
            def _():
                each("start")

        body(*refs[:n_in], *refs[o0:o0 + n_out], *refs[s0:s0 + n_scr])

        if comms:
            @pl.when(pl.program_id(0) == steps - 1)
            def _():
                each("finish")

    res = pl.pallas_call(
        wrapped, name=name, grid=grid,
        in_specs=list(in_specs) + [ANY] * (len(c_args) + len(after)), out_specs=out_specs + [ANY] * len(c_outs),
        out_shape=out_shape + c_outs, scratch_shapes=list(scratch_shapes) + sems,
        input_output_aliases=aliases, compiler_params=_seq_params(),
    )(*args, *c_args, *after)
    mine = res[0] if single else list(res[:n_out])
    if not comms:
        return mine
    theirs, oi = [], n_out
    for c in comms:
        theirs.append(list(res[oi:oi + len(c.out_shape)]))
        oi += len(c.out_shape)
    return mine, theirs


def _inproj_fwd(x, pos, g_pre, w_in, comms=()):
    def body(x_ref, pos_ref, g_ref, w_ref, h_ref, u_ref, vs_ref, *rest):
        qkv_refs, slabs = rest[:9], rest[9:]
        xh, _ = _rms_stats(x_ref[...])
        h = (xh * g_ref[...]).astype(BF16)
        h_ref[...] = h
        proj = jnp.concatenate([_dot(h, w_ref[s]) for s in range(N_SHARD)], axis=1)
        tabs = _rot_tables(pos_ref[...])
        u_ref[...] = proj[:, 1536:2048]
        vs_ref[...] = proj[:, 2048:2560]
        qkv = (_rope(proj[:, 0:512], tabs) * np.float32(ATTN_SCALE), _rope(proj[:, 512:1024], tabs),
               proj[:, 1024:1536])
        for t, val in enumerate(qkv):
            qkv_refs[t][...] = val.astype(BF16)
            for i, dil in enumerate(DILATIONS[1:]):
                _store_view(val, qkv_refs[3 * (i + 1) + t], slabs[t], dil)

    return _pcall(
        body, name="inproj_fwd", grid=(SEQ // TM,),
        in_specs=[_rows(D_MODEL), _rows(1), _resident((1, D_MODEL)), _resident((N_SHARD, D_MODEL, IN_S))],
        out_specs=[_rows(D_MODEL), _rows(512), _rows(512)] + [_view_rows(dil) for dil in DILATIONS for _ in range(3)],
        out_shape=[_sds((SEQ, D_MODEL), BF16), _sds((SEQ, 512), F32), _sds((SEQ, 512), F32)]
        + [_view_shape(dil, BF16) for dil in DILATIONS for _ in range(3)],
        scratch_shapes=[_slab_scratch() for _ in range(3)],
        args=(x, pos, g_pre, w_in), comms=comms)


def _sgu_fwd(u, vs, lg, lb, w_sp, b_t, comms=()):
    def body(u_ref, vs_ref, lg_ref, lb_ref, w_ref, bt_ref, out_ref):
        wm = _masked_spatial(w_ref)
        bias_full = _dot_exact(bt_ref[...], _group_ones())
        gu, _, _, _, mixed = _sgu_core(u_ref[...], vs_ref[...], lg_ref[...], lb_ref[...], wm, bias_full)
        out_ref[...] = gu * mixed

    return _pcall(
        body, name="sgu_fwd", grid=(SEQ // TM,),
        in_specs=[_rows(SGU_W), _rows(SGU_W), _resident((1, SGU_W)), _resident((1, SGU_W)),
                  _resident((SGU_GROUPS, CHUNK, CHUNK)), _resident((CHUNK, SGU_GROUPS))],
        out_specs=_rows(SGU_W),
        out_shape=_sds((SEQ, SGU_W), F32),
        args=(u, vs, lg, lb, w_sp, b_t), comms=comms)


def _block_masks():
    row = lax.broadcasted_iota(jnp.int32, (CHUNK, CHUNK), 0)
    col = lax.broadcasted_iota(jnp.int32, (CHUNK, CHUNK), 1)
    return col <= row, col >= row


def _attn_fwd(qv, kv, vv, dil, comms=()):
    seg = SEQ // dil
    nblk = seg // CHUNK

    def body(q_ref, k_ref, v_ref, o_ref, l_ref):
        left = _left_half()
        m_cur, m_prev = _block_masks()
        zero = jnp.zeros((CHUNK, CHUNK), BF16)
        ones = (jnp.where(left, 1.0, 0.0).astype(BF16), jnp.where(left, 0.0, 1.0).astype(BF16))

        def blk(b, carry):
            r0 = pl.multiple_of(b * CHUNK, CHUNK)
            rp = pl.multiple_of(jnp.maximum(b - 1, 0) * CHUNK, CHUNK)
            prev_ok = m_prev & (b > 0)
            sides = tuple(enumerate((left, ~left)))
            tiles, scores = [], []
            for hp in range(4):
                ls = slice(hp * 128, (hp + 1) * 128)
                qp = q_ref[pl.ds(r0, CHUNK), ls]
                kc = k_ref[pl.ds(r0, CHUNK), ls]
                kp = k_ref[pl.ds(rp, CHUNK), ls] if nblk > 1 else None
                tiles.append((ls, v_ref[pl.ds(r0, CHUNK), ls], v_ref[pl.ds(rp, CHUNK), ls] if nblk > 1 else None))
                for _, hm in sides:
                    qh = jnp.where(hm, qp, zero)
                    sc = jnp.where(m_cur, _dot_nt(qh, kc), NEG)
                    sp = jnp.where(prev_ok, _dot_nt(qh, kp), NEG) if nblk > 1 else None
                    scores.append((sc, sp))
            probs = []
            for sc, sp in scores:
                if nblk > 1:
                    m = jnp.max(jnp.maximum(sc, sp), axis=-1, keepdims=True)
                    pc = jnp.exp(sc - m)
                    pp = jnp.exp(sp - m)
                    probs.append((m, pc.astype(BF16), pp.astype(BF16), (pc + pp).astype(BF16)))
                else:
                    m = jnp.max(sc, axis=-1, keepdims=True)
                    pc = jnp.exp(sc - m).astype(BF16)
                    probs.append((m, pc, None, pc))
            for hp, (ls, vc, vp) in enumerate(tiles):
                acc = jnp.zeros((CHUNK, CHUNK), F32)
                den = jnp.zeros((CHUNK, CHUNK), F32)
                for side, hm in sides:
                    _, pc, pp, psum = probs[2 * hp + side]
                    acc = acc + _dot(pc, jnp.where(hm, vc, zero))
                    if nblk > 1:
                        acc = acc + _dot(pp, jnp.where(hm, vp, zero))
                    den = den + _dot(psum, ones[side])
                o_ref[pl.ds(r0, CHUNK), ls] = acc / den
                l_ref[pl.ds(r0, CHUNK), ls] = jnp.where(left, probs[2 * hp][0], probs[2 * hp + 1][0]) + jnp.log(den)
            return carry

        lax.fori_loop(0, nblk, blk, 0)

    spec = pl.BlockSpec((seg, ATTN_W), lambda r: (0, r))
    return _pcall(
        body, name=f"attn_fwd_d{dil}", grid=(dil,),
        in_specs=[spec, spec, spec], out_specs=[spec, spec],
        out_shape=[_sds((seg, dil * ATTN_W), F32), _sds((seg, dil * ATTN_W), F32)],
        args=(qv, kv, vv), comms=comms)


def _mix_out_fwd(o_list, l_list, sgu, x, w_out, g_attn, g_sgu, g_post, comms=()):
    def body(o1, o2, o3, l1, l2, l3, sgu_ref, x_ref, w_ref, ga_ref, gs_ref, gp_ref,
             attn_ref, mixed_ref, y_ref, x1_ref, lse1_ref, lse2_ref, lse3_ref, slabs_a, slabs_b):
        os = [o1[...], _load_view(o2, slabs_a, DILATIONS[1]), _load_view(o3, slabs_b, DILATIONS[2])]
        ls = [l1[...], _load_view(l2, slabs_a, DILATIONS[1]), _load_view(l3, slabs_b, DILATIONS[2])]
        m = jnp.maximum(jnp.maximum(ls[0], ls[1]), ls[2])
        es = [jnp.exp(l - m) for l in ls]
        den = es[0] + es[1] + es[2]
        attn = (es[0] * os[0] + es[1] * os[1] + es[2] * os[2]) / den
        attn_ref[...] = attn
        lse = m + jnp.log(den)
        lse1_ref[...] = lse
        _store_view(lse, lse2_ref, slabs_a, DILATIONS[1])
        _store_view(lse, lse3_ref, slabs_b, DILATIONS[2])
        ah, _ = _rms_stats(attn)
        sh, _ = _rms_stats(sgu_ref[...])
        mixed = jnp.concatenate([ah * ga_ref[...], sh * gs_ref[...]], axis=1).astype(BF16)
        mixed_ref[...] = mixed
        y = _dot(mixed[:, 0:OUT_S], w_ref[0])
        for s in range(1, N_SHARD):
            y = y + _dot(mixed[:, s * OUT_S:(s + 1) * OUT_S], w_ref[s])
        y_ref[...] = y
        yh, _ = _rms_stats(y)
        x1_ref[...] = x_ref[...] + yh * gp_ref[...]

    return _pcall(
        body, name="mix_out_fwd", grid=(SEQ // TM,),
        in_specs=[_view_rows(dil) for dil in DILATIONS] * 2
        + [_rows(512), _rows(D_MODEL), _resident((N_SHARD, OUT_S, D_MODEL)),
           _resident((1, 512)), _resident((1, 512)), _resident((1, D_MODEL))],
        out_specs=[_rows(512), _rows(D_MODEL), _rows(D_MODEL), _rows(D_MODEL)] + [_view_rows(dil) for dil in DILATIONS],
        out_shape=[_sds((SEQ, 512), F32), _sds((SEQ, D_MODEL), BF16), _sds((SEQ, D_MODEL), F32),
                   _sds((SEQ, D_MODEL), F32)] + [_view_shape(dil, F32) for dil in DILATIONS],
        scratch_shapes=[_slab_scratch(), _slab_scratch()],
        args=(*o_list, *l_list, sgu, x, w_out, g_attn, g_sgu, g_post), comms=comms)


def _ffn_fwd_bwd(x1, target, w_gate, w_up, w_down, g_pre, g_post, comms=()):
    def body(x1_ref, t_ref, wg_ref, wu_ref, wd_ref, gpf_ref, gpo_ref,
             h2_ref, a_ref, dg_ref, dup_ref, df_ref, dx1_ref, loss_ref, dgpf_ref, dgpo_ref, g_scr, up_scr):
        @pl.when(pl.program_id(0) == 0)
        def _():
            loss_ref[...] = jnp.zeros_like(loss_ref)
            dgpf_ref[...] = jnp.zeros_like(dgpf_ref)
            dgpo_ref[...] = jnp.zeros_like(dgpo_ref)

        x1 = x1_ref[...]
        gpf = gpf_ref[...]
        gpo = gpo_ref[...]
        xh, r = _rms_stats(x1)
        h2 = (xh * gpf).astype(BF16)
        h2_ref[...] = h2
        f = jnp.zeros((TM, D_MODEL), F32)
        for s in range(N_SHARD):
            g = _dot_nt(h2, wg_ref[s])
            up = _dot_nt(h2, wu_ref[s])
            g_scr[s] = g
            up_scr[s] = up
            a = (g * _sigmoid(g) * up).astype(BF16)
            a_ref[s] = a
            f = f + _dot(a, wd_ref[s])
        fh, rf = _rms_stats(f)
        diff = x1 + fh * gpo - t_ref[...]
        loss_ref[...] += jnp.sum(diff * diff, axis=0, keepdims=True)
        dout = diff * np.float32(1.0 / D_MODEL)
        df, dgpo = _rms_bwd(fh, rf, gpo, dout)
        dgpo_ref[...] += dgpo
        dfb = df.astype(BF16)
        df_ref[...] = dfb
        dh2 = jnp.zeros((TM, D_MODEL), F32)
        for s in range(N_SHARD):
            da = _dot_nt(dfb, wd_ref[s])
            g = g_scr[s]
            up = up_scr[s]
            sg = _sigmoid(g)
            dup = (da * (g * sg)).astype(BF16)
            dg = (da * up * (sg * (1.0 + g * (1.0 - sg)))).astype(BF16)
            dg_ref[s] = dg
            dup_ref[s] = dup
            dh2 = dh2 + _dot(dg, wg_ref[s]) + _dot(dup, wu_ref[s])
        dx, dgpf = _rms_bwd(xh, r, gpf, dh2)
        dgpf_ref[...] += dgpf
        dx1_ref[...] = dout + dx

    return _pcall(
        body, name="ffn_fwd_bwd", grid=(SEQ // TM,),
        in_specs=[_rows(D_MODEL), _rows(D_MODEL), _resident((N_SHARD, FF_S, D_MODEL)),
                  _resident((N_SHARD, FF_S, D_MODEL)), _resident((N_SHARD, FF_S, D_MODEL)),
                  _resident((1, D_MODEL)), _resident((1, D_MODEL))],
        out_specs=[_rows(D_MODEL), _rows3(N_SHARD, FF_S), _rows3(N_SHARD, FF_S), _rows3(N_SHARD, FF_S),
                   _rows(D_MODEL), _rows(D_MODEL), _acc(D_MODEL), _acc(D_MODEL), _acc(D_MODEL)],
        out_shape=[_sds((SEQ, D_MODEL), BF16), _sds((N_SHARD, SEQ, FF_S), BF16), _sds((N_SHARD, SEQ, FF_S), BF16),
                   _sds((N_SHARD, SEQ, FF_S), BF16), _sds((SEQ, D_MODEL), BF16), _sds((SEQ, D_MODEL), F32),
                   _sds((1, D_MODEL), F32), _sds((1, D_MODEL), F32), _sds((1, D_MODEL), F32)],
        scratch_shapes=[pltpu.VMEM((N_SHARD, TM, FF_S), F32), pltpu.VMEM((N_SHARD, TM, FF_S), F32)],
        args=(x1, target, w_gate, w_up, w_down, g_pre, g_post), comms=comms)


def _wgrad(a, b, a_spec, b_spec, out_block, name, comms=()):
    def body(a_ref, b_ref, o_ref):
        av = a_ref[0] if len(a_ref.shape) == 3 else a_ref[...]
        bv = b_ref[0] if len(b_ref.shape) == 3 else b_ref[...]
        o_ref[0] = _dot_tn(av, bv)

    return _pcall(
        body, name=name, grid=(N_SHARD,),
        in_specs=[a_spec, b_spec],
        out_specs=pl.BlockSpec((1,) + out_block, lambda s: (s, 0, 0)),
        out_shape=_sds((N_SHARD,) + out_block, F32),
        args=(a, b), comms=comms)


def _outproj_bwd(dx1, y, attn, sgu, w_out, g_post, g_attn, g_sgu, comms=()):
    def body(dx1_ref, y_ref, attn_ref, sgu_ref, w_ref, gp_ref, ga_ref, gs_ref,
             dy_ref, dsgu_ref, dgp_ref, dga_ref, dgs_ref, *rest):
        dattn_refs, delta_refs, (slabs_a, slabs_b) = rest[0:3], rest[3:6], rest[6:]

        @pl.when(pl.program_id(0) == 0)
        def _():
            dgp_ref[...] = jnp.zeros_like(dgp_ref)
            dga_ref[...] = jnp.zeros_like(dga_ref)
            dgs_ref[...] = jnp.zeros_like(dgs_ref)

        yh, ry = _rms_stats(y_ref[...])
        dy, dgp = _rms_bwd(yh, ry, gp_ref[...], dx1_ref[...])
        dgp_ref[...] += dgp
        dyb = dy.astype(BF16)
        dy_ref[...] = dyb
        dmixed = jnp.concatenate([_dot_nt(dyb, w_ref[s]) for s in range(N_SHARD)], axis=1)
        attn = attn_ref[...]
        ah, ra = _rms_stats(attn)
        dattn, dga = _rms_bwd(ah, ra, ga_ref[...], dmixed[:, 0:512])
        dga_ref[...] += dga
        sh, rs = _rms_stats(sgu_ref[...])
        dsgu, dgs = _rms_bwd(sh, rs, gs_ref[...], dmixed[:, 512:1024])
        dgs_ref[...] += dgs
        dsgu_ref[...] = dsgu
        la = lax.broadcasted_iota(jnp.int32, (ATTN_W, ATTN_W), 0) >> 6
        lb = lax.broadcasted_iota(jnp.int32, (ATTN_W, ATTN_W), 1) >> 6
        delta = _dot_exact(dattn * attn, (la == lb).astype(F32))
        dattn_refs[0][...] = dattn.astype(BF16)
        delta_refs[0][...] = delta
        for i, dil in enumerate(DILATIONS[1:]):
            _store_view(dattn, dattn_refs[i + 1], slabs_a, dil)
            _store_view(delta, delta_refs[i + 1], slabs_b, dil)

    return _pcall(
        body, name="outproj_bwd", grid=(SEQ // TM,),
        in_specs=[_rows(D_MODEL), _rows(D_MODEL), _rows(512), _rows(512), _resident((N_SHARD, OUT_S, D_MODEL)),
                  _resident((1, D_MODEL)), _resident((1, 512)), _resident((1, 512))],
        out_specs=[_rows(D_MODEL), _rows(512), _acc(D_MODEL), _acc(512), _acc(512)]
        + [_view_rows(dil) for dil in DILATIONS] * 2,
        out_shape=[_sds((SEQ, D_MODEL), BF16), _sds((SEQ, 512), F32),
                   _sds((1, D_MODEL), F32), _sds((1, 512), F32), _sds((1, 512), F32)]
        + [_view_shape(dil, BF16) for dil in DILATIONS] + [_view_shape(dil, F32) for dil in DILATIONS],
        scratch_shapes=[_slab_scratch(), _slab_scratch()],
        args=(dx1, y, attn, sgu, w_out, g_post, g_attn, g_sgu), comms=comms)


def _sgu_bwd(u, vs, dsgu, lg, lb, w_sp, b_t, comms=(), after=()):
    nsteps = SEQ // TM

    def body(u_ref, vs_ref, ds_ref, lg_ref, lb_ref, w_ref, bt_ref,
             du_ref, dvs_ref, dw_ref, db_ref, dlg_ref, dlb_ref, dbias_scr):
        i = pl.program_id(0)

        @pl.when(i == 0)
        def _():
            dw_ref[...] = jnp.zeros_like(dw_ref)
            dlg_ref[...] = jnp.zeros_like(dlg_ref)
            dlb_ref[...] = jnp.zeros_like(dlb_ref)
            dbias_scr[...] = jnp.zeros_like(dbias_scr)

        wm = _masked_spatial(w_ref)
        ones_g = _group_ones()
        bias_full = _dot_exact(bt_ref[...], ones_g)
        u = u_ref[...]
        vs = vs_ref[...]
        lg = lg_ref[...]
        gu, xh, rstd, vnb, mixed = _sgu_core(u, vs, lg, lb_ref[...], wm, bias_full)
        dsgu = ds_ref[...]
        du_ref[...] = (dsgu * mixed * _gelu_grad(u)).astype(BF16)
        dmixed = dsgu * gu
        left = _left_half()
        dvn_rows = []
        for c in range(TM // CHUNK):
            rs = slice(c * CHUNK, (c + 1) * CHUNK)
            dm_c = dmixed[rs, :]
            dbias_scr[...] += dm_c
            pieces = []
            for p in range(4):
                ls = slice(p * 128, (p + 1) * 128)
                dmp = dm_c[:, ls]
                vp = vnb[rs, ls]
                dmb = dmp.astype(BF16)
                zero = jnp.zeros_like(dmb)
                dw_ref[2 * p] += _dot_nt(jnp.where(left, dmb, zero), vp)
                dw_ref[2 * p + 1] += _dot_nt(jnp.where(left, zero, dmb), vp)
                pieces.append(jnp.where(left, _dot_tn(wm[2 * p], dmb), _dot_tn(wm[2 * p + 1], dmb)))
            dvn_rows.append(jnp.concatenate(pieces, axis=1))
        dvn = jnp.concatenate(dvn_rows, axis=0)
        dlg_ref[...] += jnp.sum(dvn * xh, axis=0, keepdims=True)
        dlb_ref[...] += jnp.sum(dvn, axis=0, keepdims=True)
        dxh = dvn * lg
        dgv = rstd * (dxh - jnp.mean(dxh, axis=-1, keepdims=True) - xh * jnp.mean(dxh * xh, axis=-1, keepdims=True))
        dvs_ref[...] = (dgv * _gelu_grad(vs)).astype(BF16)

        @pl.when(i == nsteps - 1)
        def _():
            row = lax.broadcasted_iota(jnp.int32, (CHUNK, CHUNK), 0)
            col = lax.broadcasted_iota(jnp.int32, (CHUNK, CHUNK), 1)
            for g in range(SGU_GROUPS):
                dw_ref[g] = jnp.where(col <= row, dw_ref[g], 0.0)
            db_ref[...] = lax.dot_general(ones_g, dbias_scr[...], (((1,), (1,)), ((), ())),
                                          preferred_element_type=F32, precision=lax.Precision.HIGHEST)

    return _pcall(
        body, name="sgu_bwd", grid=(nsteps,),
        in_specs=[_rows(SGU_W), _rows(SGU_W), _rows(SGU_W), _resident((1, SGU_W)), _resident((1, SGU_W)),
                  _resident((SGU_GROUPS, CHUNK, CHUNK)), _resident((CHUNK, SGU_GROUPS))],
        out_specs=[_rows(SGU_W), _rows(SGU_W), pl.BlockSpec((SGU_GROUPS, CHUNK, CHUNK), lambda i: (0, 0, 0)),
                   _acc(CHUNK, SGU_GROUPS), _acc(SGU_W), _acc(SGU_W)],
        out_shape=[_sds((SEQ, SGU_W), BF16), _sds((SEQ, SGU_W), BF16), _sds((SGU_GROUPS, CHUNK, CHUNK), F32),
                   _sds((SGU_GROUPS, CHUNK), F32), _sds((1, SGU_W), F32), _sds((1, SGU_W), F32)],
        scratch_shapes=[pltpu.VMEM((CHUNK, SGU_W), F32)],
        args=(u, vs, dsgu, lg, lb, w_sp, b_t), comms=comms, after=after)


def _attn_bwd(qv, kv, vv, dov, deltav, lsev, dil, comms=(), after=()):
    seg = SEQ // dil
    nblk = seg // CHUNK

    def body(q_ref, k_ref, v_ref, do_ref, dl_ref, lse_ref, dq_ref, dk_ref, dv_ref):
        left = _left_half()
        m_cur, m_prev = _block_masks()

        def blk(b, carry):
            r0 = pl.multiple_of(b * CHUNK, CHUNK)
            rp = pl.multiple_of(jnp.maximum(b - 1, 0) * CHUNK, CHUNK)
            prev_ok = m_prev & (b > 0)
            sides = tuple(enumerate((left, ~left)))
            zero = jnp.zeros((CHUNK, CHUNK), BF16)
            tiles, firsts = [], []
            for hp in range(4):
                ls = slice(hp * 128, (hp + 1) * 128)
                qp = q_ref[pl.ds(r0, CHUNK), ls]
                kc = k_ref[pl.ds(r0, CHUNK), ls]
                vc = v_ref[pl.ds(r0, CHUNK), ls]
                dop = do_ref[pl.ds(r0, CHUNK), ls]
                kp = k_ref[pl.ds(rp, CHUNK), ls] if nblk > 1 else None
                vp = v_ref[pl.ds(rp, CHUNK), ls] if nblk > 1 else None
                tiles.append((ls, kc, kp))
                for side, hm in sides:
                    qh = jnp.where(hm, qp, zero)
                    doh = jnp.where(hm, dop, zero)
                    cur = (_dot_nt(qh, kc), _dot_nt(doh, vc))
                    prev = (_dot_nt(qh, kp), _dot_nt(doh, vp)) if nblk > 1 else None
                    firsts.append((qh, doh, cur, prev))
            seconds = []
            for i, (qh, doh, cur, prev) in enumerate(firsts):
                hp, c0 = i // 2, (i % 2) * HEAD_DIM
                ls = slice(hp * 128, (hp + 1) * 128)
                lse_h = lse_ref[pl.ds(r0, CHUNK), ls][:, c0:c0 + 1]
                dl_h = dl_ref[pl.ds(r0, CHUNK), ls][:, c0:c0 + 1]
                pc = jnp.exp(jnp.where(m_cur, cur[0] - lse_h, NEG))
                out = [pc.astype(BF16), (pc * (cur[1] - dl_h)).astype(BF16), None, None]
                if nblk > 1:
                    pp = jnp.exp(jnp.where(prev_ok, prev[0] - lse_h, NEG))
                    out[2:] = [pp.astype(BF16), (pp * (prev[1] - dl_h)).astype(BF16)]
                seconds.append(out)
            for hp, (ls, kc, kp) in enumerate(tiles):
                dq = jnp.zeros((CHUNK, CHUNK), F32)
                dkc = jnp.zeros((CHUNK, CHUNK), F32)
                dvc = jnp.zeros((CHUNK, CHUNK), F32)
                dkp = jnp.zeros((CHUNK, CHUNK), F32)
                dvp = jnp.zeros((CHUNK, CHUNK), F32)
                for side, hm in sides:
                    qh, doh, _, _ = firsts[2 * hp + side]
                    pcb, dsc, ppb, dsp = seconds[2 * hp + side]
                    dq = dq + _dot(dsc, jnp.where(hm, kc, zero))
                    dkc = dkc + _dot_tn(dsc, qh)
                    dvc = dvc + _dot_tn(pcb, doh)
                    if nblk > 1:
                        dq = dq + _dot(dsp, jnp.where(hm, kp, zero))
                        dkp = dkp + _dot_tn(dsp, qh)
                        dvp = dvp + _dot_tn(ppb, doh)
                dq_ref[pl.ds(r0, CHUNK), ls] = dq
                dk_ref[pl.ds(r0, CHUNK), ls] = dkc
                dv_ref[pl.ds(r0, CHUNK), ls] = dvc
                if nblk > 1:
                    @pl.when(b > 0)
                    def _():
                        dk_ref[pl.ds(rp, CHUNK), ls] += dkp
                        dv_ref[pl.ds(rp, CHUNK), ls] += dvp
            return carry

        lax.fori_loop(0, nblk, blk, 0)

    spec = pl.BlockSpec((seg, ATTN_W), lambda r: (0, r))
    return _pcall(
        body, name=f"attn_bwd_d{dil}", grid=(dil,),
        in_specs=[spec] * 6, out_specs=[spec] * 3,
        out_shape=[_sds((seg, dil * ATTN_W), F32)] * 3,
        args=(qv, kv, vv, dov, deltav, lsev), comms=comms, after=after)


def _inproj_bwd(dqs, dks, dvs, du, dvs_sgu, pos, x, dx1, w_in, g_pre, comms=()):
    def body(dq1, dq2, dq3, dk1, dk2, dk3, dv1, dv2, dv3, du_ref, dvs_ref, pos_ref, x_ref, dx1_ref, w_ref, g_ref,
             dproj_ref, gx_ref, dg_ref, slabs_a, slabs_b):
        @pl.when(pl.program_id(0) == 0)
        def _():
            dg_ref[...] = jnp.zeros_like(dg_ref)

        def total(r1, r2, r3):
            return r1[...] + _load_view(r2, slabs_a, DILATIONS[1]) + _load_view(r3, slabs_b, DILATIONS[2])

        tabs = _rot_tables(pos_ref[...])
        dproj_ref[:, 0:512] = _rope_bwd(total(dq1, dq2, dq3) * np.float32(ATTN_SCALE), tabs).astype(BF16)
        dproj_ref[:, 512:1024] = _rope_bwd(total(dk1, dk2, dk3), tabs).astype(BF16)
        dproj_ref[:, 1024:1536] = total(dv1, dv2, dv3).astype(BF16)
        dproj_ref[:, 1536:2048] = du_ref[...]
        dproj_ref[:, 2048:2560] = dvs_ref[...]
        dh = jnp.zeros((TM, D_MODEL), F32)
        for s in range(N_SHARD):
            dh = dh + _dot_nt(dproj_ref[:, s * IN_S:(s + 1) * IN_S], w_ref[s])
        g = g_ref[...]
        xh, r = _rms_stats(x_ref[...])
        dx, dg = _rms_bwd(xh, r, g, dh)
        dg_ref[...] += dg
        gx_ref[...] = dx1_ref[...] + dx

    return _pcall(
        body, name="inproj_bwd", grid=(SEQ // TM,),
        in_specs=[_view_rows(dil) for dil in DILATIONS] * 3
        + [_rows(512), _rows(512), _rows(1), _rows(D_MODEL), _rows(D_MODEL),
           _resident((N_SHARD, D_MODEL, IN_S)), _resident((1, D_MODEL))],
        out_specs=[_rows(PROJ_W), _rows(D_MODEL), _acc(D_MODEL)],
        out_shape=[_sds((SEQ, PROJ_W), BF16), _sds((SEQ, D_MODEL), F32), _sds((1, D_MODEL), F32)],
        scratch_shapes=[_slab_scratch(), _slab_scratch()],
        args=(*dqs, *dks, *dvs, du, dvs_sgu, pos, x, dx1, w_in, g_pre), comms=comms)


def _to_view(a, dil):
    return a if dil == 1 else a.reshape(SEQ // dil, dil * a.shape[1])


def _from_view(a, dil):
    return a if dil == 1 else a.reshape(SEQ, a.shape[1] // dil)


def _local_step(x, pos, target, w_in, w_out, w_gate, w_up, w_down, small):
    b_t = small["sgu_b_spatial"].T
    h, u, vs, *qkv = _inproj_fwd(x, pos, small["pre_mix_norm"], w_in)
    sgu = _sgu_fwd(u, vs, small["sgu_ln_gain"], small["sgu_ln_bias"], small["sgu_w_spatial"], b_t)
    views = [tuple(qkv[3 * i:3 * i + 3]) for i in range(len(DILATIONS))]
    o_list, l_list = [], []
    for dil, (qv, kv, vv) in zip(DILATIONS, views):
        o, l = _attn_fwd(qv, kv, vv, dil)
        o_list.append(o)
        l_list.append(l)
    attn, mixed, y, x1, *lses = _mix_out_fwd(o_list, l_list, sgu, x, w_out, small["attn_out_norm"],
                                             small["sgu_out_norm"], small["post_mix_norm"])
    h2, a, dg, dup, df, dx1, loss_cols, d_pre_ffn, d_post_ffn = _ffn_fwd_bwd(
        x1, target, w_gate, w_up, w_down, small["pre_ffn_norm"], small["post_ffn_norm"])

    full_tok = pl.BlockSpec((SEQ, D_MODEL), lambda s: (0, 0), pipeline_mode=pl.Buffered(1))
    ff_tok = pl.BlockSpec((1, SEQ, FF_S), lambda s: (s, 0, 0))
    gw_gate = _wgrad(dg, h2, ff_tok, full_tok, (FF_S, D_MODEL), "wgrad_gate")
    gw_up = _wgrad(dup, h2, ff_tok, full_tok, (FF_S, D_MODEL), "wgrad_up")
    gw_down = _wgrad(a, df, ff_tok, full_tok, (FF_S, D_MODEL), "wgrad_down")

    dy, dsgu, d_post_mix, d_attn_norm, d_sgu_norm, *dviews = _outproj_bwd(
        dx1, y, attn, sgu, w_out, small["post_mix_norm"], small["attn_out_norm"], small["sgu_out_norm"])
    gw_out = _wgrad(mixed, dy, pl.BlockSpec((SEQ, OUT_S), lambda s: (0, s)), full_tok, (OUT_S, D_MODEL), "wgrad_out")
    du, dvs_sgu, d_w_sp, d_b_sp, d_ln_gain, d_ln_bias = _sgu_bwd(
        u, vs, dsgu, small["sgu_ln_gain"], small["sgu_ln_bias"], small["sgu_w_spatial"], b_t)

    dqs, dks, dvs = [], [], []
    for i, (dil, (qv, kv, vv)) in enumerate(zip(DILATIONS, views)):
        dq, dk, dv = _attn_bwd(qv, kv, vv, dviews[i], dviews[3 + i], lses[i], dil)
        dqs.append(dq)
        dks.append(dk)
        dvs.append(dv)
    dproj, grad_x, d_pre_mix = _inproj_bwd(dqs, dks, dvs, du, dvs_sgu, pos, x, dx1, w_in, small["pre_mix_norm"])
    gw_in = _wgrad(h, dproj, full_tok, pl.BlockSpec((SEQ, IN_S), lambda s: (0, s)), (D_MODEL, IN_S), "wgrad_in")

    small_grads = {
        "pre_mix_norm": d_pre_mix, "sgu_ln_gain": d_ln_gain, "sgu_ln_bias": d_ln_bias, "sgu_w_spatial": d_w_sp,
        "sgu_b_spatial": d_b_sp, "attn_out_norm": d_attn_norm, "sgu_out_norm": d_sgu_norm,
        "post_mix_norm": d_post_mix, "pre_ffn_norm": d_pre_ffn, "post_ffn_norm": d_post_ffn,
    }
    return loss_cols, grad_x, (gw_in, gw_out, gw_gate, gw_up, gw_down), small_grads


def _coords():
    return lax.axis_index("x"), lax.axis_index("y"), lax.axis_index("c")


def _other_chips(x, y):
    return [(1 - x, y), (x, 1 - y), (1 - x, 1 - y)]


def _comm_call(body, name, n_in, out_shape, scratch_shapes):
    return pl.pallas_call(
        body, name=name, in_specs=[ANY] * n_in, out_specs=[ANY] * len(out_shape), out_shape=out_shape,
        scratch_shapes=scratch_shapes,
        compiler_params=pltpu.CompilerParams(has_side_effects=True),
    )


def _gather_weights(shards):
    n = len(shards)
    halves = [s.reshape(2, s.shape[0] // 2, s.shape[1]) for s in shards]

    def body(*refs):
        ins, outs = refs[:n], refs[n:2 * n]
        send_sems, recv_sems = refs[2 * n:]
        x, y, c = _coords()
        s_me = 2 * x + y
        chips = _other_chips(x, y)
        sibling = (x, y, 1 - c)

        def copy(k, w, shard, cc, to):
            src = ins[w].at[cc] if shard is None else outs[w].at[shard, cc]
            dst = outs[w].at[s_me if shard is None else shard, cc]
            return pltpu.make_async_remote_copy(src_ref=src, dst_ref=dst, send_sem=send_sems.at[k],
                                                recv_sem=recv_sems.at[k], device_id=to, device_id_type=MESH)

        first = [copy(j * n + w, w, None, c, (cx, cy, c)) for j, (cx, cy) in enumerate(chips) for w in range(n)]
        for cp in first:
            cp.start()
        passed = []
        for j, (cx, cy) in enumerate(chips):
            for w in range(n):
                copy(j * n + w, w, 2 * cx + cy, c, (x, y, c)).wait_recv()
                fw = copy((3 + j) * n + w, w, 2 * cx + cy, c, sibling)
                fw.start()
                passed.append(fw)
        for j, (cx, cy) in enumerate(chips):
            for w in range(n):
                copy((3 + j) * n + w, w, 2 * cx + cy, 1 - c, (x, y, c)).wait_recv()
        for cp in first + passed:
            cp.wait_send()

    out_shape = [_sds((N_SHARD,) + h.shape, h.dtype) for h in halves]
    scratch = [pltpu.SemaphoreType.DMA((6 * n,)), pltpu.SemaphoreType.DMA((6 * n,))]
    full = _comm_call(body, "comm_gather_weights", n, out_shape, scratch)(*halves)
    s_me = 2 * lax.axis_index("x") + lax.axis_index("y")
    full = [lax.dynamic_update_slice(f, h[None], (s_me, 0, 0, 0)) for f, h in zip(full, halves)]
    return [f.reshape((N_SHARD,) + s.shape) for f, s in zip(full, shards)]


def _rs_to_sibling(gws):
    n = len(gws)

    def body(*refs):
        ins, outs = refs[:n], refs[n:2 * n]
        send_sems, recv_sems = refs[2 * n:]
        x, y, c = _coords()
        copies = []
        for w in range(n):
            hw = gws[w].shape[1] // 2
            copies.append(pltpu.make_async_remote_copy(
                src_ref=ins[w].at[:, pl.ds((1 - c) * hw, hw), :], dst_ref=outs[w], send_sem=send_sems.at[w],
                recv_sem=recv_sems.at[w], device_id=(x, y, 1 - c), device_id_type=MESH))
        for cp in copies:
            cp.start()
        for cp in copies:
            cp.wait()

    out_shape = [_sds((N_SHARD, g.shape[1] // 2, g.shape[2]), g.dtype) for g in gws]
    scratch = [pltpu.SemaphoreType.DMA((n,)), pltpu.SemaphoreType.DMA((n,))]
    return _comm_call(body, "comm_rs_sibling", n, out_shape, scratch)(*gws)


def _rs_chip_sum(gw, recv, core):
    _, rows, cols = gw.shape
    hw = rows // 2

    def body(c_ref, g_ref, r_ref, o_ref):
        o_ref[...] = (g_ref[...] + r_ref[...]).astype(BF16)

    return pl.pallas_call(
        body, name="rs_chip_sum",
        grid_spec=pltpu.PrefetchScalarGridSpec(
            num_scalar_prefetch=1, grid=(N_SHARD,),
            in_specs=[pl.BlockSpec((1, hw, cols), lambda s, c_ref: (s, c_ref[0], 0)),
                      pl.BlockSpec((1, hw, cols), lambda s, c_ref: (s, 0, 0))],
            out_specs=pl.BlockSpec((1, hw, cols), lambda s, c_ref: (s, 0, 0))),
        out_shape=_sds((N_SHARD, hw, cols), BF16),
        compiler_params=_seq_params(),
    )(core, gw, recv)


def _rs_between_chips(pbs):
    n = len(pbs)

    def body(*refs):
        ins, outs = refs[:n], refs[n:2 * n]
        send_sems, recv_sems = refs[2 * n:]
        x, y, c = _coords()
        copies = []
        for j, (cx, cy) in enumerate(_other_chips(x, y)):
            for w in range(n):
                copies.append(pltpu.make_async_remote_copy(
                    src_ref=ins[w].at[2 * cx + cy], dst_ref=outs[w].at[j], send_sem=send_sems.at[j * n + w],
                    recv_sem=recv_sems.at[j * n + w], device_id=(cx, cy, c), device_id_type=MESH))
        for cp in copies:
            cp.start()
        for cp in copies:
            cp.wait()

    out_shape = [_sds((3,) + p.shape[1:], p.dtype) for p in pbs]
    scratch = [pltpu.SemaphoreType.DMA((3 * n,)), pltpu.SemaphoreType.DMA((3 * n,))]
    return _comm_call(body, "comm_rs_chips", n, out_shape, scratch)(*pbs)


def _rs_final_sum(gw, recv_sib, recv_chips, shard_core):
    _, rows, cols = gw.shape
    hw = rows // 2

    def body(sc_ref, g_ref, r_ref, rc_ref, o_ref):
        acc = g_ref[0] + r_ref[0]
        for j in range(3):
            acc = acc + rc_ref[j].astype(F32)
        o_ref[0] = acc

    return pl.pallas_call(
        body, name="rs_final_sum",
        grid_spec=pltpu.PrefetchScalarGridSpec(
            num_scalar_prefetch=1, grid=(1,),
            in_specs=[pl.BlockSpec((1, hw, cols), lambda i, sc: (sc[0], sc[1], 0)),
                      pl.BlockSpec((1, hw, cols), lambda i, sc: (sc[0], 0, 0)),
                      pl.BlockSpec((3, hw, cols), lambda i, sc: (0, 0, 0))],
            out_specs=pl.BlockSpec((1, hw, cols), lambda i, sc: (sc[1], 0, 0))),
        out_shape=_sds((2, hw, cols), F32),
        compiler_params=_seq_params(),
    )(shard_core, gw, recv_sib, recv_chips)


def _rs_join_halves(halves):
    n = len(halves)

    def body(*refs):
        bufs = refs[n:2 * n]
        send_sems, recv_sems = refs[2 * n:]
        x, y, c = _coords()
        remote = [pltpu.make_async_remote_copy(
            src_ref=bufs[w].at[c], dst_ref=bufs[w].at[c], send_sem=send_sems.at[w], recv_sem=recv_sems.at[w],
            device_id=(x, y, 1 - c), device_id_type=MESH) for w in range(n)]
        for cp in remote:
            cp.start()
        for w in range(n):
            remote[w].wait_send()
            pltpu.make_async_remote_copy(
                src_ref=bufs[w].at[c], dst_ref=bufs[w].at[1 - c], send_sem=send_sems.at[w],
                recv_sem=recv_sems.at[w], device_id=(x, y, c), device_id_type=MESH).wait_recv()

    joined = pl.pallas_call(
        body, name="comm_rs_join", in_specs=[ANY] * n, out_specs=[ANY] * n,
        out_shape=[_sds(h.shape, h.dtype) for h in halves], input_output_aliases={w: w for w in range(n)},
        scratch_shapes=[pltpu.SemaphoreType.DMA((n,)), pltpu.SemaphoreType.DMA((n,))],
        compiler_params=pltpu.CompilerParams(has_side_effects=True),
    )(*halves)
    return [j.reshape(2 * h.shape[1], h.shape[2]) for j, h in zip(joined, halves)]


def _allreduce_small(buf):
    rows, cols = buf.shape

    def body(in_ref, out_ref, slots, send_sems, recv_sems):
        x, y, c = _coords()
        me = 4 * x + 2 * y + c
        copies, peers = [], []
        for k in range(1, 8):
            px = 1 - x if (k >> 2) & 1 else x
            py = 1 - y if (k >> 1) & 1 else y
            pc = 1 - c if k & 1 else c
            peers.append(4 * px + 2 * py + pc)
            copies.append(pltpu.make_async_remote_copy(
                src_ref=in_ref, dst_ref=slots.at[me], send_sem=send_sems.at[k - 1], recv_sem=recv_sems.at[k - 1],
                device_id=(px, py, pc), device_id_type=MESH))
        for cp in copies:
            cp.start()
        slots[me] = in_ref[...]
        for k in range(7):
            pltpu.make_async_remote_copy(
                src_ref=in_ref, dst_ref=slots.at[peers[k]], send_sem=send_sems.at[k], recv_sem=recv_sems.at[k],
                device_id=(x, y, c), device_id_type=MESH).wait_recv()
        for cp in copies:
            cp.wait_send()
        acc = slots[0]
        for i in range(1, 8):
            acc = acc + slots[i]
        out_ref[...] = acc

    vmem = pl.BlockSpec(memory_space=pltpu.VMEM)
    return pl.pallas_call(
        body, name="comm_allreduce_small", in_specs=[vmem], out_specs=vmem, out_shape=_sds((rows, cols), F32),
        scratch_shapes=[pltpu.VMEM((8, rows, cols), F32), pltpu.SemaphoreType.DMA((7,)), pltpu.SemaphoreType.DMA((7,))],
        compiler_params=pltpu.CompilerParams(has_side_effects=True, vmem_limit_bytes=VMEM_LIMIT),
    )(buf)


def _adamw(w, g, m, v, block_rows, name, after=()):
    rows, cols = w.shape

    def body(w_ref, g_ref, m_ref, v_ref, *rest):
        d_ref, nm_ref, nv_ref = rest[len(after):]
        g = g_ref[...]
        m = ADAM_B1 * m_ref[...] + (1.0 - ADAM_B1) * g
        v = ADAM_B2 * v_ref[...] + (1.0 - ADAM_B2) * (g * g)
        m_hat = m / (1.0 - ADAM_B1 ** ADAM_STEP)
        v_hat = v / (1.0 - ADAM_B2 ** ADAM_STEP)
        d_ref[...] = -ADAM_LR * (m_hat / (jnp.sqrt(v_hat) + ADAM_EPS) + ADAM_WD * w_ref[...])
        nm_ref[...] = m
        nv_ref[...] = v

    spec = pl.BlockSpec((block_rows, cols), lambda i: (i, 0))
    return pl.pallas_call(
        body, name=name, grid=(rows // block_rows,), in_specs=[spec] * 4 + [ANY] * len(after), out_specs=[spec] * 3,
        out_shape=[_sds((rows, cols), F32)] * 3,
        compiler_params=_seq_params(),
    )(w, g, m, v, *after)


WEIGHTS = ("pre_mix_norm", "w_in", "sgu_ln_gain", "sgu_ln_bias", "sgu_w_spatial", "sgu_b_spatial", "attn_out_norm",
           "sgu_out_norm", "w_out", "post_mix_norm", "pre_ffn_norm", "w_gate", "w_up", "w_down", "post_ffn_norm")
BIG = ("w_in", "w_out", "w_gate", "w_up", "w_down")
BIG_ADAM_ROWS = {"w_in": 256, "w_out": 128, "w_gate": 352, "w_up": 352, "w_down": 352}
SMALL = ("pre_mix_norm", "post_mix_norm", "pre_ffn_norm", "post_ffn_norm", "sgu_ln_gain", "sgu_ln_bias",
         "attn_out_norm", "sgu_out_norm", "sgu_w_spatial", "sgu_b_spatial")


def _pack_small(d):
    flat = [d[n].reshape(-1) for n in SMALL]
    used = sum(f.shape[0] for f in flat)
    flat.append(jnp.zeros((SMALL_ROWS * 1024 - used,), F32))
    return jnp.concatenate(flat).reshape(SMALL_ROWS, 1024)


def _unpack_small(buf, shapes):
    flat = buf.reshape(-1)
    out, off = {}, 0
    for n in SMALL:
        size = int(np.prod(shapes[n]))
        out[n] = flat[off:off + size].reshape(shapes[n])
        off += size
    return out


def _kernel_unoverlapped(x, positions, pre_mix_norm, w_in, sgu_ln_gain, sgu_ln_bias, sgu_w_spatial, sgu_b_spatial, attn_out_norm, sgu_out_norm, w_out, post_mix_norm, pre_ffn_norm, w_gate, w_up, w_down, post_ffn_norm, loss_target, m_pre_mix_norm, m_w_in, m_sgu_ln_gain, m_sgu_ln_bias, m_sgu_w_spatial, m_sgu_b_spatial, m_attn_out_norm, m_sgu_out_norm, m_w_out, m_post_mix_norm, m_pre_ffn_norm, m_w_gate, m_w_up, m_w_down, m_post_ffn_norm, v_pre_mix_norm, v_w_in, v_sgu_ln_gain, v_sgu_ln_bias, v_sgu_w_spatial, v_sgu_b_spatial, v_attn_out_norm, v_sgu_out_norm, v_w_out, v_post_mix_norm, v_pre_ffn_norm, v_w_gate, v_w_up, v_w_down, v_post_ffn_norm):
    a = dict(locals())
    cx, cy, cc = _coords()
    core = jnp.stack([cc]).astype(jnp.int32)
    shard_core = jnp.stack([2 * cx + cy, cc]).astype(jnp.int32)

    full = _gather_weights([a[n][0].astype(BF16) for n in BIG])
    small = {n: (a[n][0] if a[n].ndim > 2 else a[n]) for n in SMALL}
    loss_cols, grad_x, gws, small_grads = _local_step(
        x[0], positions.reshape(SEQ, 1), loss_target[0], *full, small)
    loss = lax.psum(jnp.sum(loss_cols) * np.float32(0.5 / D_MODEL), ("x", "y", "c"))

    recv_sib = _rs_to_sibling(list(gws))
    chip_part = [_rs_chip_sum(g, r, core) for g, r in zip(gws, recv_sib)]
    recv_chips = _rs_between_chips(chip_part)
    halves = [_rs_final_sum(g, r, rc, shard_core) for g, r, rc in zip(gws, recv_sib, recv_chips)]
    big_grads = dict(zip(BIG, _rs_join_halves(halves)))

    shapes = {n: a[n].shape for n in SMALL}
    small_sum = _allreduce_small(_pack_small(small_grads))

    grads, deltas, new_m, new_v = {}, {}, {}, {}
    for n in BIG:
        grads[n] = big_grads[n][None]
        d, nm, nv = _adamw(a[n][0], big_grads[n], a["m_" + n][0], a["v_" + n][0], BIG_ADAM_ROWS[n], "adamw_" + n)
        deltas[n], new_m[n], new_v[n] = d[None], nm[None], nv[None]
    d, nm, nv = _adamw(_pack_small({n: a[n] for n in SMALL}), small_sum, _pack_small({n: a["m_" + n] for n in SMALL}),
                       _pack_small({n: a["v_" + n] for n in SMALL}), SMALL_ROWS, "adamw_small")
    grads.update(_unpack_small(small_sum, shapes))
    deltas.update(_unpack_small(d, shapes))
    new_m.update(_unpack_small(nm, shapes))
    new_v.update(_unpack_small(nv, shapes))
    return (loss, grad_x[None], *[grads[n] for n in WEIGHTS], *[deltas[n] for n in WEIGHTS],
            *[new_m[n] for n in WEIGHTS], *[new_v[n] for n in WEIGHTS])


def _remote(src, dst, send_sem, recv_sem, to):
    return pltpu.make_async_remote_copy(src_ref=src, dst_ref=dst, send_sem=send_sem, recv_sem=recv_sem,
                                        device_id=to, device_id_type=MESH)


def _halves(a):
    *lead, rows, cols = a.shape
    return a.reshape(*lead, 2, rows // 2, cols)


def _gather_ici(shards):
    n = len(shards)

    def desc(ins, outs, ss, rs, j, w, landed):
        x, y, c = _coords()
        cx, cy = _other_chips(x, y)[j]
        shard = 2 * cx + cy if landed else 2 * x + y
        return _remote(ins[w].at[c], outs[w].at[shard, c], ss.at[j * n + w], rs.at[j * n + w], (cx, cy, c))

    def start(ins, outs, ss, rs):
        for j in range(3):
            for w in range(n):
                desc(ins, outs, ss, rs, j, w, False).start()

    def finish(ins, outs, ss, rs):
        for j in range(3):
            for w in range(n):
                desc(ins, outs, ss, rs, j, w, True).wait_recv()
                desc(ins, outs, ss, rs, j, w, False).wait_send()

    return _Comm(shards, [_sds((N_SHARD,) + s.shape, s.dtype) for s in shards], 3 * n, start, finish)


def _gather_pass(fulls):
    n = len(fulls)

    def desc(bufs, ss, rs, j, w, landed):
        x, y, c = _coords()
        cx, cy = _other_chips(x, y)[j]
        shard = 2 * cx + cy
        return _remote(bufs[w].at[shard, c], bufs[w].at[shard, 1 - c if landed else c],
                       ss.at[j * n + w], rs.at[j * n + w], (x, y, 1 - c))

    def start(ins, outs, ss, rs):
        for j in range(3):
            for w in range(n):
                desc(outs, ss, rs, j, w, False).start()

    def finish(ins, outs, ss, rs):
        for j in range(3):
            for w in range(n):
                desc(outs, ss, rs, j, w, True).wait_recv()
                desc(outs, ss, rs, j, w, False).wait_send()

    return _Comm(fulls, [_sds(f.shape, f.dtype) for f in fulls], 3 * n, start, finish, aliased=True)


def _rs_sibling(gws):
    n = len(gws)

    def desc(ins, outs, ss, rs, w):
        x, y, c = _coords()
        return _remote(ins[w].at[:, 1 - c], outs[w], ss.at[w], rs.at[w], (x, y, 1 - c))

    def start(ins, outs, ss, rs):
        for w in range(n):
            desc(ins, outs, ss, rs, w).start()

    def finish(ins, outs, ss, rs):
        for w in range(n):
            desc(ins, outs, ss, rs, w).wait()

    out_shape = [_sds((N_SHARD, g.shape[1] // 2, g.shape[2]), g.dtype) for g in gws]
    return _Comm([_halves(g) for g in gws], out_shape, n, start, finish)


def _rs_chips(pbs):
    n = len(pbs)

    def desc(ins, outs, ss, rs, j, w):
        x, y, c = _coords()
        cx, cy = _other_chips(x, y)[j]
        return _remote(ins[w].at[2 * cx + cy], outs[w].at[j], ss.at[j * n + w], rs.at[j * n + w], (cx, cy, c))

    def start(ins, outs, ss, rs):
        for j in range(3):
            for w in range(n):
                desc(ins, outs, ss, rs, j, w).start()

    def finish(ins, outs, ss, rs):
        for j in range(3):
            for w in range(n):
                desc(ins, outs, ss, rs, j, w).wait()

    return _Comm(pbs, [_sds((3,) + p.shape[1:], p.dtype) for p in pbs], 3 * n, start, finish)


def _rs_join(halves):
    n = len(halves)

    def desc(bufs, ss, rs, w, landed):
        x, y, c = _coords()
        return _remote(bufs[w].at[c], bufs[w].at[1 - c if landed else c], ss.at[w], rs.at[w], (x, y, 1 - c))

    def start(ins, outs, ss, rs):
        for w in range(n):
            desc(outs, ss, rs, w, False).start()

    def finish(ins, outs, ss, rs):
        for w in range(n):
            desc(outs, ss, rs, w, True).wait_recv()
            desc(outs, ss, rs, w, False).wait_send()

    return _Comm(halves, [_sds(h.shape, h.dtype) for h in halves], n, start, finish, aliased=True)


def _small_exchange(buf):
    def desc(ins, outs, ss, rs, k, landed):
        x, y, c = _coords()
        px = 1 - x if (k >> 2) & 1 else x
        py = 1 - y if (k >> 1) & 1 else y
        pc = 1 - c if k & 1 else c
        slot = 4 * px + 2 * py + pc if landed else 4 * x + 2 * y + c
        return _remote(ins[0], outs[0].at[slot], ss.at[k - 1], rs.at[k - 1], (px, py, pc))

    def start(ins, outs, ss, rs):
        for k in range(1, 8):
            desc(ins, outs, ss, rs, k, False).start()

    def finish(ins, outs, ss, rs):
        for k in range(1, 8):
            desc(ins, outs, ss, rs, k, True).wait_recv()
            desc(ins, outs, ss, rs, k, False).wait_send()

    return _Comm([buf], [_sds((8,) + buf.shape, buf.dtype)], 7, start, finish)


HBM = pl.BlockSpec(memory_space=pltpu.HBM)
SEM = pl.BlockSpec(memory_space=pltpu.SEMAPHORE)
DATAFLOW = pltpu.SideEffectType.DATAFLOW_SIDE_EFFECTING


def _split_start(name, comm, lands, after):
    srcs = [pltpu.with_memory_space_constraint(s, pltpu.HBM) for s in comm.args]
    lands = [pltpu.with_memory_space_constraint(b, pltpu.HBM) for b in lands]
    ns, nb = len(srcs), len(lands)

    def body(*refs):
        send_sems, recv_sems = refs[ns + nb + 1], refs[ns + nb + 2]
        comm.start(refs[:ns], refs[ns:ns + nb], send_sems, recv_sems)
        refs[-1][...] = jnp.zeros_like(refs[-1])

    res = pl.pallas_call(
        body, name=name,
        out_shape=(pltpu.SemaphoreType.DMA((comm.n_sems,)), pltpu.SemaphoreType.DMA((comm.n_sems,)),
                   *[pltpu.HBM(b.shape, b.dtype) for b in srcs + lands], _sds((8, 128), F32)),
        in_specs=[HBM] * (ns + nb) + [ANY],
        out_specs=(SEM, SEM, *[HBM] * (ns + nb), pl.BlockSpec(memory_space=pltpu.VMEM)),
        input_output_aliases={i: 2 + i for i in range(ns + nb)},
        compiler_params=pltpu.CompilerParams(has_side_effects=DATAFLOW),
    )(*srcs, *lands, after)
    return res[0], res[1], list(res[2:2 + ns]), list(res[2 + ns:2 + ns + nb]), res[-1]


def _split_starts(name, comms, after):
    srcs = [[pltpu.with_memory_space_constraint(s, pltpu.HBM) for s in c.args] for c in comms]
    lands = [[pltpu.with_memory_space_constraint(lax.empty(o.shape, o.dtype), pltpu.HBM) for o in c.out_shape]
             for c in comms]
    bufs = [b for k in range(len(comms)) for b in srcs[k] + lands[k]]
    nb, nc = len(bufs), len(comms)

    def body(*refs):
        sems = refs[nb + 1:nb + 1 + 2 * nc]
        off = 0
        for k, c in enumerate(comms):
            ns, nl = len(srcs[k]), len(lands[k])
            c.start(refs[off:off + ns], refs[off + ns:off + ns + nl], sems[2 * k], sems[2 * k + 1])
            off += ns + nl
        refs[-1][...] = jnp.zeros_like(refs[-1])

    res = pl.pallas_call(
        body, name=name,
        out_shape=(*[pltpu.SemaphoreType.DMA((c.n_sems,)) for c in comms for _ in range(2)],
                   *[pltpu.HBM(b.shape, b.dtype) for b in bufs], _sds((8, 128), F32)),
        in_specs=[HBM] * nb + [ANY],
        out_specs=(*[SEM] * (2 * nc), *[HBM] * nb, pl.BlockSpec(memory_space=pltpu.VMEM)),
        input_output_aliases={i: 2 * nc + i for i in range(nb)},
        compiler_params=pltpu.CompilerParams(has_side_effects=DATAFLOW),
    )(*bufs, after)
    states, off = [], 2 * nc
    for k in range(nc):
        ns, nl = len(srcs[k]), len(lands[k])
        states.append((res[2 * k], res[2 * k + 1], list(res[off:off + ns]), list(res[off + ns:off + ns + nl])))
        off += ns + nl
    return states, res[-1]


def _split_wait(name, comm, send_sems, recv_sems, srcs, lands, after):
    ns, nb = len(srcs), len(lands)
    after = list(after) if isinstance(after, (list, tuple)) else [after]

    def body(*refs):
        comm.finish(refs[:ns], refs[ns:ns + nb], refs[ns + nb], refs[ns + nb + 1])

    res = pl.pallas_call(
        body, name=name,
        out_shape=tuple(pltpu.HBM(b.shape, b.dtype) for b in srcs + lands),
        in_specs=[HBM] * (ns + nb) + [SEM, SEM] + [ANY] * len(after), out_specs=tuple([HBM] * (ns + nb)),
        input_output_aliases={i: i for i in range(ns + nb)},
        compiler_params=pltpu.CompilerParams(has_side_effects=DATAFLOW),
    )(*srcs, *lands, send_sems, recv_sems, *after)
    return list(res[ns:])


LOSS_ROW = "loss_cols"
SMALL_EARLY = ("post_mix_norm", "pre_ffn_norm", "post_ffn_norm", "sgu_ln_gain", "sgu_ln_bias", "attn_out_norm",
               "sgu_out_norm", "sgu_w_spatial", "sgu_b_spatial", LOSS_ROW)
SMALL_LATE = ("pre_mix_norm",)


def _pack(d, names, rows):
    flat = [d[n].reshape(-1) for n in names]
    used = sum(f.shape[0] for f in flat)
    flat.append(jnp.zeros((rows * 1024 - used,), F32))
    return jnp.concatenate(flat).reshape(rows, 1024)


def _unpack(buf, names, shapes):
    flat = buf.reshape(-1)
    out, off = {}, 0
    for n in names:
        size = int(np.prod(shapes[n]))
        out[n] = flat[off:off + size].reshape(shapes[n])
        off += size
    return out


def _merge(comms):
    def run(phase):
        def go(ins, outs, ss, rs):
            ii = oi = si = 0
            for c in comms:
                getattr(c, phase)(ins[ii:ii + len(c.args)], outs[oi:oi + len(c.out_shape)],
                                  ss.at[pl.ds(si, c.n_sems)], rs.at[pl.ds(si, c.n_sems)])
                ii += len(c.args)
                oi += len(c.out_shape)
                si += c.n_sems
        return go

    return _Comm([a for c in comms for a in c.args], [o for c in comms for o in c.out_shape],
                 sum(c.n_sems for c in comms), run("start"), run("finish"))


def _chip_sums(gws, recvs, core):
    n = len(gws)
    _, rows, cols = gws[0].shape
    hw = rows // 2

    def body(c_ref, *refs):
        for k in range(n):
            refs[2 * n + k][...] = (refs[k][...] + refs[n + k][...]).astype(BF16)

    mine = pl.BlockSpec((1, hw, cols), lambda s, c_ref: (s, c_ref[0], 0))
    plain = pl.BlockSpec((1, hw, cols), lambda s, c_ref: (s, 0, 0))
    return pl.pallas_call(
        body, name="rs_chip_sums",
        grid_spec=pltpu.PrefetchScalarGridSpec(num_scalar_prefetch=1, grid=(N_SHARD,),
                                               in_specs=[mine] * n + [plain] * n, out_specs=[plain] * n),
        out_shape=[_sds((N_SHARD, hw, cols), BF16)] * n,
        compiler_params=_seq_params(),
    )(core, *gws, *recvs)


def _final_sums(gws, recv_sibs, recv_chips, shard_core):
    n = len(gws)
    _, rows, cols = gws[0].shape
    hw = rows // 2

    def body(sc_ref, *refs):
        for k in range(n):
            acc = refs[k][0] + refs[n + k][0]
            for j in range(3):
                acc = acc + refs[2 * n + k][j].astype(F32)
            refs[3 * n + k][0] = acc

    return pl.pallas_call(
        body, name="rs_final_sums",
        grid_spec=pltpu.PrefetchScalarGridSpec(
            num_scalar_prefetch=1, grid=(1,),
            in_specs=[pl.BlockSpec((1, hw, cols), lambda i, sc: (sc[0], sc[1], 0))] * n
            + [pl.BlockSpec((1, hw, cols), lambda i, sc: (sc[0], 0, 0))] * n
            + [pl.BlockSpec((3, hw, cols), lambda i, sc: (0, 0, 0))] * n,
            out_specs=[pl.BlockSpec((1, hw, cols), lambda i, sc: (sc[1], 0, 0))] * n),
        out_shape=[_sds((2, hw, cols), F32)] * n,
        compiler_params=_seq_params(),
    )(shard_core, *gws, *recv_sibs, *recv_chips)


def _adamw_multi(ws, gs, ms, vs, block_rows, name, after=()):
    n = len(ws)
    rows, cols = ws[0].shape

    def body(*refs):
        outs = refs[4 * n + len(after):]
        for k in range(n):
            g = refs[n + k][...]
            d, m, v = _adam_math(refs[k][...], g, refs[2 * n + k][...], refs[3 * n + k][...])
            outs[4 * k][...], outs[4 * k + 1][...], outs[4 * k + 2][...], outs[4 * k + 3][...] = g, d, m, v

    spec = pl.BlockSpec((block_rows, cols), lambda i: (i, 0))
    res = pl.pallas_call(
        body, name=name, grid=(rows // block_rows,), in_specs=[spec] * (4 * n) + [ANY] * len(after),
        out_specs=[spec] * (4 * n), out_shape=[_sds((rows, cols), F32)] * (4 * n),
        compiler_params=_seq_params(),
    )(*ws, *gs, *ms, *vs, *after)
    return [tuple(res[4 * k:4 * k + 4]) for k in range(n)]


def _wgrad_pair(a1, a2, b, a_spec, b_spec, out_block, name, comms=()):
    def body(a1_ref, a2_ref, b_ref, o1_ref, o2_ref):
        bv = b_ref[...]
        o1_ref[0] = _dot_tn(a1_ref[0], bv)
        o2_ref[0] = _dot_tn(a2_ref[0], bv)

    out_spec = pl.BlockSpec((1,) + out_block, lambda s: (s, 0, 0))
    return _pcall(
        body, name=name, grid=(N_SHARD,), in_specs=[a_spec, a_spec, b_spec], out_specs=[out_spec, out_spec],
        out_shape=[_sds((N_SHARD,) + out_block, F32)] * 2, args=(a1, a2, b), comms=comms)


def _comm_only(name, comms):
    return _pcall(lambda: None, name=name, grid=(1,), in_specs=[], out_specs=[], out_shape=[], args=(),
                  comms=comms)[1]


def _adam_math(w, g, m, v):
    m = ADAM_B1 * m + (1.0 - ADAM_B1) * g
    v = ADAM_B2 * v + (1.0 - ADAM_B2) * (g * g)
    m_hat = m / (1.0 - ADAM_B1 ** ADAM_STEP)
    v_hat = v / (1.0 - ADAM_B2 ** ADAM_STEP)
    return -ADAM_LR * (m_hat / (jnp.sqrt(v_hat) + ADAM_EPS) + ADAM_WD * w), m, v


def _adamw_small(own, slots, w, m, v, me):
    rows, cols = own.shape

    def body(me_ref, own_ref, slots_ref, w_ref, m_ref, v_ref, g_ref, d_ref, nm_ref, nv_ref):
        own_v = own_ref[...]
        g = jnp.where(me_ref[0] == 0, own_v, slots_ref[0])
        for i in range(1, 8):
            g = g + jnp.where(me_ref[0] == i, own_v, slots_ref[i])
        g_ref[...] = g
        d_ref[...], nm_ref[...], nv_ref[...] = _adam_math(w_ref[...], g, m_ref[...], v_ref[...])

    flat = pl.BlockSpec((rows, cols), lambda i, me_ref: (0, 0))
    return pl.pallas_call(
        body, name="adamw_small",
        grid_spec=pltpu.PrefetchScalarGridSpec(
            num_scalar_prefetch=1, grid=(1,),
            in_specs=[flat, pl.BlockSpec((8, rows, cols), lambda i, me_ref: (0, 0, 0)), flat, flat, flat],
            out_specs=[flat] * 4),
        out_shape=[_sds((rows, cols), F32)] * 4,
        compiler_params=_seq_params(),
    )(me, own, slots, w, m, v)


def kernel(x, positions, pre_mix_norm, w_in, sgu_ln_gain, sgu_ln_bias, sgu_w_spatial, sgu_b_spatial, attn_out_norm, sgu_out_norm, w_out, post_mix_norm, pre_ffn_norm, w_gate, w_up, w_down, post_ffn_norm, loss_target, m_pre_mix_norm, m_w_in, m_sgu_ln_gain, m_sgu_ln_bias, m_sgu_w_spatial, m_sgu_b_spatial, m_attn_out_norm, m_sgu_out_norm, m_w_out, m_post_mix_norm, m_pre_ffn_norm, m_w_gate, m_w_up, m_w_down, m_post_ffn_norm, v_pre_mix_norm, v_w_in, v_sgu_ln_gain, v_sgu_ln_bias, v_sgu_w_spatial, v_sgu_b_spatial, v_attn_out_norm, v_sgu_out_norm, v_w_out, v_post_mix_norm, v_pre_ffn_norm, v_w_gate, v_w_up, v_w_down, v_post_ffn_norm):
    a = dict(locals())
    cx, cy, cc = _coords()
    s_me = 2 * cx + cy
    core = jnp.stack([cc]).astype(jnp.int32)
    shard_core = jnp.stack([s_me, cc]).astype(jnp.int32)
    me = jnp.stack([4 * cx + 2 * cy + cc]).astype(jnp.int32)
    small = {n: (a[n][0] if a[n].ndim > 2 else a[n]) for n in SMALL}
    b_t = small["sgu_b_spatial"].T
    xs, pos, target = x[0], positions.reshape(SEQ, 1), loss_target[0]
    flipped = ("w_gate", "w_up")

    def big(name, n):
        return jnp.swapaxes(a[name], 1, 2)[0] if n in flipped else a[name][0]

    own = {n: _halves(big(n, n).astype(BF16)) for n in BIG}

    def with_own(full, n):
        full = lax.dynamic_update_slice(full, own[n][None], (s_me, 0, 0, 0))
        return full.reshape((N_SHARD,) + big(n, n).shape)

    ffn = ("w_gate", "w_up", "w_down")
    g_in, g_out, g_ffn = _gather_ici([own["w_in"]]), _gather_ici([own["w_out"]]), _gather_ici([own[n] for n in ffn])
    (s_in, s_out, s_ffn), token = _split_starts("gather_start", [g_in, g_out, g_ffn], small["pre_mix_norm"])
    in_lands = _split_wait("gather_in_wait", g_in, *s_in, token)
    ((in_lands,),) = _comm_only("comm_pass_in", [_gather_pass(in_lands)])
    w_in_f = with_own(in_lands, "w_in")
    h, u, vs, *qkv = _inproj_fwd(xs, pos, small["pre_mix_norm"], w_in_f)
    views = [tuple(qkv[3 * i:3 * i + 3]) for i in range(len(DILATIONS))]
    o_list, l_list = [], []
    for dil, (qv, kv, vv) in zip(DILATIONS, views):
        o, l = _attn_fwd(qv, kv, vv, dil)
        o_list.append(o)
        l_list.append(l)
    out_lands = _split_wait("gather_out_wait", g_out, *s_out, l_list[-1])
    sgu, ((out_lands,),) = _sgu_fwd(u, vs, small["sgu_ln_gain"], small["sgu_ln_bias"], small["sgu_w_spatial"], b_t,
                                    comms=[_gather_pass(out_lands)])
    w_out_f = with_own(out_lands, "w_out")
    ffn_lands = _split_wait("gather_ffn_wait", g_ffn, *s_ffn, sgu)
    (attn, mixed, y, x1, *lses), (ffn_lands,) = _mix_out_fwd(
        o_list, l_list, sgu, xs, w_out_f, small["attn_out_norm"], small["sgu_out_norm"], small["post_mix_norm"],
        comms=[_gather_pass(ffn_lands)])
    w_gate_f, w_up_f, w_down_f = (with_own(f, n) for f, n in zip(ffn_lands, ffn))
    h2, act, dg, dup, df, dx1, loss_cols, d_pre_ffn, d_post_ffn = _ffn_fwd_bwd(
        x1, target, w_gate_f, w_up_f, w_down_f, small["pre_ffn_norm"], small["post_ffn_norm"])

    full_tok = pl.BlockSpec((SEQ, D_MODEL), lambda s: (0, 0), pipeline_mode=pl.Buffered(1))
    ff_tok = pl.BlockSpec((1, SEQ, FF_S), lambda s: (s, 0, 0))
    gw = {}
    gw["w_gate"], gw["w_up"] = _wgrad_pair(dg, dup, h2, ff_tok, full_tok, (FF_S, D_MODEL), "wgrad_gate_up")
    gw["w_down"], ((sib_gate, sib_up),) = _wgrad(act, df, ff_tok, full_tok, (FF_S, D_MODEL), "wgrad_down",
                                                 comms=[_rs_sibling([gw["w_gate"], gw["w_up"]])])
    (dy, dsgu, d_post_mix, d_attn_norm, d_sgu_norm, *dviews), ((sib_down,),) = _outproj_bwd(
        dx1, y, attn, sgu, w_out_f, small["post_mix_norm"], small["attn_out_norm"],
        small["sgu_out_norm"], comms=[_rs_sibling([gw["w_down"]])])
    sib = {"w_gate": sib_gate, "w_up": sib_up, "w_down": sib_down}
    part = dict(zip(ffn, _chip_sums([gw[n] for n in ffn], [sib[n] for n in ffn], core)))
    gw["w_out"] = _wgrad(mixed, dy, pl.BlockSpec((SEQ, OUT_S), lambda s: (0, s)), full_tok, (OUT_S, D_MODEL),
                         "wgrad_out")
    x_ffn = _rs_chips([part[n] for n in ffn])
    (s_ffn,), token = _split_starts("rs_ffn_start", [x_ffn], small["pre_mix_norm"])
    (du, dvs_sgu, d_w_sp, d_b_sp, d_ln_gain, d_ln_bias), ((sib["w_out"],),) = _sgu_bwd(
        u, vs, dsgu, small["sgu_ln_gain"], small["sgu_ln_bias"], small["sgu_w_spatial"], b_t,
        comms=[_rs_sibling([gw["w_out"]])], after=[token])
    part["w_out"] = _rs_chip_sum(gw["w_out"], sib["w_out"], core)
    packed_early = _pack({
        "sgu_ln_gain": d_ln_gain, "sgu_ln_bias": d_ln_bias, "sgu_w_spatial": d_w_sp, "sgu_b_spatial": d_b_sp,
        "attn_out_norm": d_attn_norm, "sgu_out_norm": d_sgu_norm, "post_mix_norm": d_post_mix,
        "pre_ffn_norm": d_pre_ffn, "post_ffn_norm": d_post_ffn, LOSS_ROW: loss_cols}, SMALL_EARLY, SMALL_ROWS)
    x_out, x_small = _rs_chips([part["w_out"]]), _small_exchange(packed_early)
    (s_out, s_small), token = _split_starts("rs_out_small_start", [x_out, x_small], token)

    dqs, dks, dvs = [], [], []
    for i, (dil, (qv, kv, vv)) in enumerate(zip(DILATIONS, views)):
        dq, dk, dv = _attn_bwd(qv, kv, vv, dviews[i], dviews[3 + i], lses[i], dil, after=[token])
        dqs.append(dq)
        dks.append(dk)
        dvs.append(dv)
    half, far, joined = {}, {}, {}
    far.update(zip(ffn, _split_wait("rs_ffn_wait", x_ffn, *s_ffn, dvs[-1])))
    half.update(zip(ffn, _final_sums([gw[n] for n in ffn], [sib[n] for n in ffn], [far[n] for n in ffn],
                                     shard_core)))
    dproj, grad_x, d_pre_mix = _inproj_bwd(dqs, dks, dvs, du, dvs_sgu, pos, xs, dx1, w_in_f, small["pre_mix_norm"])
    (far["w_out"],) = _split_wait("rs_out_wait", x_out, *s_out, grad_x)
    (slots_early,) = _split_wait("small_early_wait", x_small, *s_small, far["w_out"])
    half["w_out"] = _rs_final_sum(gw["w_out"], sib["w_out"], far["w_out"], shard_core)
    packed_late = _pack({"pre_mix_norm": d_pre_mix}, SMALL_LATE, 8)
    names = ffn + ("w_out",)
    gw["w_in"], (got, (slots_late,)) = _wgrad(
        h, dproj, full_tok, pl.BlockSpec((SEQ, IN_S), lambda s: (0, s)), (D_MODEL, IN_S), "wgrad_in",
        comms=[_rs_join([half[n] for n in names]), _small_exchange(packed_late)])
    joined.update(zip(names, got))
    ((sib["w_in"],),) = _comm_only("comm_rs_sibling_in", [_rs_sibling([gw["w_in"]])])
    part["w_in"] = _rs_chip_sum(gw["w_in"], sib["w_in"], core)
    x_in = _rs_chips([part["w_in"]])
    (s_in,), token = _split_starts("rs_in_start", [x_in], small["pre_mix_norm"])

    grads, deltas, new_m, new_v = {}, {}, {}, {}

    def record(n, outs):
        grads[n], deltas[n], new_m[n], new_v[n] = (
            jnp.swapaxes(o[None], 1, 2) if n in flipped else o[None] for o in outs)

    def update(names, block_rows, name, after):
        for n, outs in zip(names, _adamw_multi(
                [big(n, n) for n in names], [joined[n].reshape(big(n, n).shape) for n in names],
                [big("m_" + n, n) for n in names], [big("v_" + n, n) for n in names], block_rows, name, after)):
            record(n, outs)

    update(ffn, FF_S // 4, "adamw_ffn", [token])
    update(("w_out",), BIG_ADAM_ROWS["w_out"], "adamw_w_out", [token])
    (far["w_in"],) = _split_wait("rs_in_wait", x_in, *s_in, [new_v[n] for n in ("w_down", "w_out")])
    half["w_in"] = _rs_final_sum(gw["w_in"], sib["w_in"], far["w_in"], shard_core)
    ((joined["w_in"],),) = _comm_only("comm_rs_join_in", [_rs_join([half["w_in"]])])
    update(("w_in",), BIG_ADAM_ROWS["w_in"], "adamw_w_in", [])
    a[LOSS_ROW] = a["m_" + LOSS_ROW] = a["v_" + LOSS_ROW] = jnp.zeros((1, D_MODEL), F32)
    for names, rows, packed, slots in ((SMALL_EARLY, SMALL_ROWS, packed_early, slots_early),
                                       (SMALL_LATE, 8, packed_late, slots_late)):
        outs = _adamw_small(packed, slots, _pack(a, names, rows), _pack({n: a["m_" + n] for n in names}, names, rows),
                            _pack({n: a["v_" + n] for n in names}, names, rows), me)
        for dst, buf in zip((grads, deltas, new_m, new_v), outs):
            dst.update(_unpack(buf, names, {n: a[n].shape for n in names}))
    loss = jnp.sum(grads[LOSS_ROW]) * np.float32(0.5 / D_MODEL)
    return (loss, grad_x[None], *[grads[n] for n in WEIGHTS], *[deltas[n] for n in WEIGHTS],
            *[new_m[n] for n in WEIGHTS], *[new_v[n] for n in WEIGHTS])
```

```python
import numpy as np
import jax
import jax.numpy as jnp
from jax import lax
from jax.experimental import pallas as pl
from jax.experimental.pallas import tpu as pltpu

F32 = jnp.float32
BF16 = jnp.bfloat16

SEQ = 2048
D_MODEL = 1024
HEAD_DIM = 64
ATTN_W = 512
SGU_W = 512
SGU_GROUPS = 8
CHUNK = 128
DILATIONS = (1, 4, 16)
N_SHARD = 4
IN_S = 640
OUT_S = 256
FF_S = 704
PROJ_W = N_SHARD * IN_S
RMS_EPS = 1e-6
LN_EPS = 1e-5
ROPE_THETA = 500000.0
ATTN_SCALE = 1.0 / np.sqrt(HEAD_DIM)
NEG = -1e30
TM = 256
VMEM_LIMIT = 56 * 1024 * 1024
SMALL_ROWS = 136

ADAM_LR = 0.001
ADAM_B1 = 0.9
ADAM_B2 = 0.999
ADAM_EPS = 1e-08
ADAM_WD = 0.01
ADAM_STEP = 10

MESH = pl.DeviceIdType.MESH
ANY = pl.BlockSpec(memory_space=pl.ANY)


def _dot(a, b):
    return jnp.dot(a, b, preferred_element_type=F32)


def _dot_nt(a, b):
    return lax.dot_general(a, b, (((1,), (1,)), ((), ())), preferred_element_type=F32)


def _dot_tn(a, b):
    return lax.dot_general(a, b, (((0,), (0,)), ((), ())), preferred_element_type=F32)


def _dot_exact(a, b):
    return jnp.dot(a, b, preferred_element_type=F32, precision=lax.Precision.HIGHEST)


def _rms_stats(x):
    r = lax.rsqrt(jnp.mean(x * x, axis=-1, keepdims=True) + RMS_EPS)
    return x * r, r


def _rms_bwd(xh, r, gain, dy):
    dxh = dy * gain
    dx = r * (dxh - xh * jnp.mean(dxh * xh, axis=-1, keepdims=True))
    return dx, jnp.sum(dy * xh, axis=0, keepdims=True)


_ERF_ALPHA = (-2.72614225801306e-10, 2.77068142495902e-08, -2.10102402082508e-06, -5.69250639462346e-05,
              -7.34990630326855e-04, -2.95459980854025e-03, -1.60960333262415e-02)
_ERF_BETA = (-1.45660718464996e-05, -2.13374055278905e-04, -1.68282697438203e-03, -7.37332916720468e-03,
             -1.42647390514189e-02)


def _erf(x):
    x = jnp.clip(x, -4.0, 4.0)
    x2 = x * x
    p = jnp.full_like(x, _ERF_ALPHA[0])
    for a in _ERF_ALPHA[1:]:
        p = p * x2 + a
    q = jnp.full_like(x, _ERF_BETA[0])
    for b in _ERF_BETA[1:]:
        q = q * x2 + b
    return x * p / q


def _gelu(x):
    return 0.5 * x * (1.0 + _erf(x * np.float32(1.0 / np.sqrt(2.0))))


def _gelu_grad(x):
    cdf = 0.5 * (1.0 + _erf(x * np.float32(1.0 / np.sqrt(2.0))))
    pdf = jnp.exp(-0.5 * x * x) * np.float32(1.0 / np.sqrt(2.0 * np.pi))
    return cdf + x * pdf


def _sigmoid(x):
    return 1.0 / (1.0 + jnp.exp(-x))


_INV_FREQ = tuple(float(np.float32(ROPE_THETA ** (-2.0 * j / 16.0))) for j in range(8))


def _rot_tables(pos):
    lane = lax.broadcasted_iota(jnp.int32, (1, 128), 1)
    d = lane & 63
    j = d & 7
    inv = jnp.zeros((1, 128), F32)
    for jj in range(8):
        inv = jnp.where(j == jj, _INV_FREQ[jj], inv)
    ang = pos.astype(F32) * inv
    c = jnp.cos(ang)
    s = jnp.sin(ang)
    cos_t = jnp.where(d < 16, c, 1.0)
    sin_a = jnp.where(d < 8, -s, 0.0)
    sin_b = jnp.where((d >= 8) & (d < 16), s, 0.0)
    return tuple(jnp.tile(t, (1, 4)) for t in (cos_t, sin_a, sin_b))


def _rope(x, tabs):
    cos_t, sin_a, sin_b = tabs
    return x * cos_t + pltpu.roll(x, 504, 1) * sin_a + pltpu.roll(x, 8, 1) * sin_b


def _rope_bwd(dy, tabs):
    cos_t, sin_a, sin_b = tabs
    return dy * cos_t + pltpu.roll(dy * sin_a, 8, 1) + pltpu.roll(dy * sin_b, 504, 1)


def _left_half():
    return lax.broadcasted_iota(jnp.int32, (CHUNK, CHUNK), 1) < HEAD_DIM


def _group_ones():
    lane = lax.broadcasted_iota(jnp.int32, (SGU_GROUPS, SGU_W), 1)
    row = lax.broadcasted_iota(jnp.int32, (SGU_GROUPS, SGU_W), 0)
    return ((lane >> 6) == row).astype(F32)


def _masked_spatial(w_ref):
    row = lax.broadcasted_iota(jnp.int32, (CHUNK, CHUNK), 0)
    col = lax.broadcasted_iota(jnp.int32, (CHUNK, CHUNK), 1)
    return [jnp.where(col <= row, w_ref[g], 0.0).astype(BF16) for g in range(SGU_GROUPS)]


def _sgu_core(u, vs, lg, lb, wm, bias_full):
    tm = u.shape[0]
    gu = _gelu(u)
    gv = _gelu(vs)
    mu = jnp.mean(gv, axis=-1, keepdims=True)
    xc = gv - mu
    rstd = lax.rsqrt(jnp.mean(xc * xc, axis=-1, keepdims=True) + LN_EPS)
    xh = xc * rstd
    vnb = (xh * lg + lb).astype(BF16)
    left = _left_half()
    rows = []
    for c in range(tm // CHUNK):
        pieces = []
        for p in range(4):
            vp = vnb[c * CHUNK:(c + 1) * CHUNK, p * 128:(p + 1) * 128]
            pieces.append(jnp.where(left, _dot(wm[2 * p], vp), _dot(wm[2 * p + 1], vp)))
        rows.append(jnp.concatenate(pieces, axis=1) + bias_full)
    mixed = jnp.concatenate(rows, axis=0)
    return gu, xh, rstd, vnb, mixed


def _resident(shape):
    n = len(shape)
    return pl.BlockSpec(shape, lambda *_: (0,) * n, pipeline_mode=pl.Buffered(1))


def _rows(ncol, tm=TM):
    return pl.BlockSpec((tm, ncol), lambda i: (i, 0))


def _rows3(nlead, ncol, tm=TM):
    return pl.BlockSpec((nlead, tm, ncol), lambda i: (0, i, 0))


def _acc(ncol, nrow=1):
    return pl.BlockSpec((nrow, ncol), lambda i: (0, 0))


HEAD_W = 128


def _view_rows(dil, width=ATTN_W, tm=TM):
    return pl.BlockSpec((tm // dil, dil * width), lambda i: (i, 0))


def _view_shape(dil, dtype, width=ATTN_W):
    return _sds((SEQ // dil, dil * width), dtype)


def _slab_scratch():
    return pltpu.VMEM((4, TM, 128), F32)


def _store_view(val, out_ref, slabs, dil):
    width = val.shape[1]
    for j in range(width // 128):
        slabs[j] = val[:, j * 128:(j + 1) * 128]
    for r in range(dil):
        for j in range(width // 128):
            c0 = r * width + j * 128
            out_ref[:, c0:c0 + 128] = slabs.at[j][pl.ds(r, TM // dil, stride=dil), :].astype(out_ref.dtype)


def _load_view(in_ref, slabs, dil, width=ATTN_W):
    for r in range(dil):
        for j in range(width // 128):
            c0 = r * width + j * 128
            slabs.at[j][pl.ds(r, TM // dil, stride=dil), :] = in_ref[:, c0:c0 + 128].astype(F32)
    return jnp.concatenate([slabs[j] for j in range(width // 128)], axis=1)


def _head_spread():
    m = lax.broadcasted_iota(jnp.int32, (HEAD_W, ATTN_W), 0)
    lane = lax.broadcasted_iota(jnp.int32, (HEAD_W, ATTN_W), 1)
    return (m == 16 * (lane >> 6)).astype(F32)


def _head_sum():
    lane = lax.broadcasted_iota(jnp.int32, (ATTN_W, HEAD_W), 0)
    m = lax.broadcasted_iota(jnp.int32, (ATTN_W, HEAD_W), 1)
    return ((lane >> 6) == (m >> 4)).astype(F32)


def _seq_params():
    return pltpu.CompilerParams(dimension_semantics=("arbitrary",), vmem_limit_bytes=VMEM_LIMIT)


def _sds(shape, dtype):
    return jax.ShapeDtypeStruct(shape, dtype)


class _Comm:
    def __init__(self, args, out_shape, n_sems, start, finish, aliased=False):
        self.args, self.out_shape, self.n_sems = list(args), list(out_shape), n_sems
        self.start, self.finish, self.aliased = start, finish, aliased


def _pcall(body, *, name, grid, in_specs, out_specs, out_shape, args, scratch_shapes=(), comms=(), after=()):
    single = not isinstance(out_shape, (list, tuple))
    out_specs = [out_specs] if single else list(out_specs)
    out_shape = [out_shape] if single else list(out_shape)
    n_in, n_out, n_scr = len(in_specs), len(out_shape), len(scratch_shapes)
    c_args = [a for c in comms for a in c.args]
    c_outs = [o for c in comms for o in c.out_shape]
    aliases, ai, ao = {}, n_in, n_out
    for c in comms:
        if c.aliased:
            aliases.update({ai + k: ao + k for k in range(len(c.args))})
        ai += len(c.args)
        ao += len(c.out_shape)
    sems = [pltpu.SemaphoreType.DMA((c.n_sems,)) for c in comms for _ in range(2)]
    steps = grid[0]

    def wrapped(*refs):
        o0 = n_in + len(c_args) + len(after)
        s0 = o0 + n_out + len(c_outs)
        m_in, m_out, m_sem = refs[n_in:n_in + len(c_args)], refs[o0 + n_out:s0], refs[s0 + n_scr:]

        def each(phase):
            ii = oi = 0
            for k, c in enumerate(comms):
                getattr(c, phase)(m_in[ii:ii + len(c.args)], m_out[oi:oi + len(c.out_shape)],
                                  m_sem[2 * k], m_sem[2 * k + 1])
                ii += len(c.args)
                oi += len(c.out_shape)

        if comms:
            @pl.when(pl.program_id(0) == 0)
            def _():
                each("start")

        body(*refs[:n_in], *refs[o0:o0 + n_out], *refs[s0:s0 + n_scr])

        if comms:
            @pl.when(pl.program_id(0) == steps - 1)
            def _():
                each("finish")

    res = pl.pallas_call(
        wrapped, name=name, grid=grid,
        in_specs=list(in_specs) + [ANY] * (len(c_args) + len(after)), out_specs=out_specs + [ANY] * len(c_outs),
        out_shape=out_shape + c_outs, scratch_shapes=list(scratch_shapes) + sems,
        input_output_aliases=aliases, compiler_params=_seq_params(),
    )(*args, *c_args, *after)
    mine = res[0] if single else list(res[:n_out])
    if not comms:
        return mine
    theirs, oi = [], n_out
    for c in comms:
        theirs.append(list(res[oi:oi + len(c.out_shape)]))
        oi += len(c.out_shape)
    return mine, theirs


def _inproj_fwd(x, pos, g_pre, w_in, comms=()):
    def body(x_ref, pos_ref, g_ref, w_ref, h_ref, u_ref, vs_ref, *rest):
        qkv_refs, slabs = rest[:9], rest[9:]
        xh, _ = _rms_stats(x_ref[...])
        h = (xh * g_ref[...]).astype(BF16)
        h_ref[...] = h
        proj = jnp.concatenate([_dot(h, w_ref[s]) for s in range(N_SHARD)], axis=1)
        tabs = _rot_tables(pos_ref[...])
        u_ref[...] = proj[:, 1536:2048]
        vs_ref[...] = proj[:, 2048:2560]
        qkv = (_rope(proj[:, 0:512], tabs) * np.float32(ATTN_SCALE), _rope(proj[:, 512:1024], tabs),
               proj[:, 1024:1536])
        for t, val in enumerate(qkv):
            qkv_refs[t][...] = val.astype(BF16)
            for i, dil in enumerate(DILATIONS[1:]):
                _store_view(val, qkv_refs[3 * (i + 1) + t], slabs[t], dil)

    return _pcall(
        body, name="inproj_fwd", grid=(SEQ // TM,),
        in_specs=[_rows(D_MODEL), _rows(1), _resident((1, D_MODEL)), _resident((N_SHARD, D_MODEL, IN_S))],
        out_specs=[_rows(D_MODEL), _rows(512), _rows(512)] + [_view_rows(dil) for dil in DILATIONS for _ in range(3)],
        out_shape=[_sds((SEQ, D_MODEL), BF16), _sds((SEQ, 512), F32), _sds((SEQ, 512), F32)]
        + [_view_shape(dil, BF16) for dil in DILATIONS for _ in range(3)],
        scratch_shapes=[_slab_scratch() for _ in range(3)],
        args=(x, pos, g_pre, w_in), comms=comms)


def _sgu_fwd(u, vs, lg, lb, w_sp, b_t, comms=()):
    def body(u_ref, vs_ref, lg_ref, lb_ref, w_ref, bt_ref, out_ref):
        wm = _masked_spatial(w_ref)
        bias_full = _dot_exact(bt_ref[...], _group_ones())
        gu, _, _, _, mixed = _sgu_core(u_ref[...], vs_ref[...], lg_ref[...], lb_ref[...], wm, bias_full)
        out_ref[...] = gu * mixed

    return _pcall(
        body, name="sgu_fwd", grid=(SEQ // TM,),
        in_specs=[_rows(SGU_W), _rows(SGU_W), _resident((1, SGU_W)), _resident((1, SGU_W)),
                  _resident((SGU_GROUPS, CHUNK, CHUNK)), _resident((CHUNK, SGU_GROUPS))],
        out_specs=_rows(SGU_W),
        out_shape=_sds((SEQ, SGU_W), F32),
        args=(u, vs, lg, lb, w_sp, b_t), comms=comms)


def _block_masks():
    row = lax.broadcasted_iota(jnp.int32, (CHUNK, CHUNK), 0)
    col = lax.broadcasted_iota(jnp.int32, (CHUNK, CHUNK), 1)
    return col <= row, col >= row


def _attn_fwd(qv, kv, vv, dil, comms=()):
    seg = SEQ // dil
    nblk = seg // CHUNK

    def body(q_ref, k_ref, v_ref, o_ref, l_ref):
        left = _left_half()
        m_cur, m_prev = _block_masks()
        zero = jnp.zeros((CHUNK, CHUNK), BF16)
        ones = (jnp.where(left, 1.0, 0.0).astype(BF16), jnp.where(left, 0.0, 1.0).astype(BF16))

        def blk(b, carry):
            r0 = pl.multiple_of(b * CHUNK, CHUNK)
            rp = pl.multiple_of(jnp.maximum(b - 1, 0) * CHUNK, CHUNK)
            prev_ok = m_prev & (b > 0)
            sides = tuple(enumerate((left, ~left)))
            tiles, scores = [], []
            for hp in range(4):
                ls = slice(hp * 128, (hp + 1) * 128)
                qp = q_ref[pl.ds(r0, CHUNK), ls]
                kc = k_ref[pl.ds(r0, CHUNK), ls]
                kp = k_ref[pl.ds(rp, CHUNK), ls] if nblk > 1 else None
                tiles.append((ls, v_ref[pl.ds(r0, CHUNK), ls], v_ref[pl.ds(rp, CHUNK), ls] if nblk > 1 else None))
                for _, hm in sides:
                    qh = jnp.where(hm, qp, zero)
                    sc = jnp.where(m_cur, _dot_nt(qh, kc), NEG)
                    sp = jnp.where(prev_ok, _dot_nt(qh, kp), NEG) if nblk > 1 else None
                    scores.append((sc, sp))
            probs = []
            for sc, sp in scores:
                if nblk > 1:
                    m = jnp.max(jnp.maximum(sc, sp), axis=-1, keepdims=True)
                    pc = jnp.exp(sc - m)
                    pp = jnp.exp(sp - m)
                    probs.append((m, pc.astype(BF16), pp.astype(BF16), (pc + pp).astype(BF16)))
                else:
                    m = jnp.max(sc, axis=-1, keepdims=True)
                    pc = jnp.exp(sc - m).astype(BF16)
                    probs.append((m, pc, None, pc))
            for hp, (ls, vc, vp) in enumerate(tiles):
                acc = jnp.zeros((CHUNK, CHUNK), F32)
                den = jnp.zeros((CHUNK, CHUNK), F32)
                for side, hm in sides:
                    _, pc, pp, psum = probs[2 * hp + side]
                    acc = acc + _dot(pc, jnp.where(hm, vc, zero))
                    if nblk > 1:
                        acc = acc + _dot(pp, jnp.where(hm, vp, zero))
                    den = den + _dot(psum, ones[side])
                o_ref[pl.ds(r0, CHUNK), ls] = (acc / den).astype(o_ref.dtype)
                lse = jnp.where(left, probs[2 * hp][0], probs[2 * hp + 1][0]) + jnp.log(den)
                l_ref[pl.ds(r0, CHUNK), 32 * hp:32 * hp + 32] = lse[:, 48:80]
            return carry

        lax.fori_loop(0, nblk, blk, 0)

    spec = pl.BlockSpec((seg, ATTN_W), lambda r: (0, r))
    return _pcall(
        body, name=f"attn_fwd_d{dil}", grid=(dil,),
        in_specs=[spec, spec, spec], out_specs=[spec, pl.BlockSpec((seg, HEAD_W), lambda r: (0, r))],
        out_shape=[_sds((seg, dil * ATTN_W), BF16), _sds((seg, dil * HEAD_W), F32)],
        args=(qv, kv, vv), comms=comms)


def _mix_out_fwd(o_list, l_list, sgu, x, w_out, g_attn, g_sgu, g_post, comms=()):
    def body(o1, o2, o3, l1, l2, l3, sgu_ref, x_ref, w_ref, ga_ref, gs_ref, gp_ref,
             attn_ref, mixed_ref, y_ref, x1_ref, lse1_ref, lse2_ref, lse3_ref, slabs_a, slabs_b):
        os = [o1[...], _load_view(o2, slabs_a, DILATIONS[1]), _load_view(o3, slabs_b, DILATIONS[2])]
        ls = [l1[...], _load_view(l2, slabs_a, DILATIONS[1], HEAD_W), _load_view(l3, slabs_b, DILATIONS[2], HEAD_W)]
        m = jnp.maximum(jnp.maximum(ls[0], ls[1]), ls[2])
        es = [jnp.exp(l - m) for l in ls]
        den = es[0] + es[1] + es[2]
        spread = _head_spread()
        attn = sum(_dot_exact(e / den, spread) * o for e, o in zip(es, os))
        attn_ref[...] = attn
        lse = m + jnp.log(den)
        lse1_ref[...] = lse
        _store_view(lse, lse2_ref, slabs_a, DILATIONS[1])
        _store_view(lse, lse3_ref, slabs_b, DILATIONS[2])
        ah, _ = _rms_stats(attn)
        sh, _ = _rms_stats(sgu_ref[...])
        mixed = jnp.concatenate([ah * ga_ref[...], sh * gs_ref[...]], axis=1).astype(BF16)
        mixed_ref[...] = mixed
        y = _dot(mixed[:, 0:OUT_S], w_ref[0])
        for s in range(1, N_SHARD):
            y = y + _dot(mixed[:, s * OUT_S:(s + 1) * OUT_S], w_ref[s])
        y_ref[...] = y
        yh, _ = _rms_stats(y)
        x1_ref[...] = x_ref[...] + yh * gp_ref[...]

    return _pcall(
        body, name="mix_out_fwd", grid=(SEQ // TM,),
        in_specs=[_view_rows(dil) for dil in DILATIONS] + [_view_rows(dil, HEAD_W) for dil in DILATIONS]
        + [_rows(512), _rows(D_MODEL), _resident((N_SHARD, OUT_S, D_MODEL)),
           _resident((1, 512)), _resident((1, 512)), _resident((1, D_MODEL))],
        out_specs=[_rows(512), _rows(D_MODEL), _rows(D_MODEL), _rows(D_MODEL)]
        + [_view_rows(dil, HEAD_W) for dil in DILATIONS],
        out_shape=[_sds((SEQ, 512), F32), _sds((SEQ, D_MODEL), BF16), _sds((SEQ, D_MODEL), F32),
                   _sds((SEQ, D_MODEL), F32)] + [_view_shape(dil, F32, HEAD_W) for dil in DILATIONS],
        scratch_shapes=[_slab_scratch(), _slab_scratch()],
        args=(*o_list, *l_list, sgu, x, w_out, g_attn, g_sgu, g_post), comms=comms)


def _ffn_fwd_bwd(x1, target, w_gate, w_up, w_down, g_pre, g_post, comms=()):
    def body(x1_ref, t_ref, wg_ref, wu_ref, wd_ref, gpf_ref, gpo_ref,
             h2_ref, a_ref, dg_ref, dup_ref, df_ref, dx1_ref, loss_ref, dgpf_ref, dgpo_ref, g_scr, up_scr):
        @pl.when(pl.program_id(0) == 0)
        def _():
            loss_ref[...] = jnp.zeros_like(loss_ref)
            dgpf_ref[...] = jnp.zeros_like(dgpf_ref)
            dgpo_ref[...] = jnp.zeros_like(dgpo_ref)

        x1 = x1_ref[...]
        gpf = gpf_ref[...]
        gpo = gpo_ref[...]
        xh, r = _rms_stats(x1)
        h2 = (xh * gpf).astype(BF16)
        h2_ref[...] = h2
        f = jnp.zeros((TM, D_MODEL), F32)
        for s in range(N_SHARD):
            g = _dot_nt(h2, wg_ref[s])
            up = _dot_nt(h2, wu_ref[s])
            g_scr[s] = g
            up_scr[s] = up
            a = (g * _sigmoid(g) * up).astype(BF16)
            a_ref[s] = a
            f = f + _dot(a, wd_ref[s])
        fh, rf = _rms_stats(f)
        diff = x1 + fh * gpo - t_ref[...]
        loss_ref[...] += jnp.sum(diff * diff, axis=0, keepdims=True)
        dout = diff * np.float32(1.0 / D_MODEL)
        df, dgpo = _rms_bwd(fh, rf, gpo, dout)
        dgpo_ref[...] += dgpo
        dfb = df.astype(BF16)
        df_ref[...] = dfb
        dh2 = jnp.zeros((TM, D_MODEL), F32)
        for s in range(N_SHARD):
            da = _dot_nt(dfb, wd_ref[s])
            g = g_scr[s]
            up = up_scr[s]
            sg = _sigmoid(g)
            dup = (da * (g * sg)).astype(BF16)
            dg = (da * up * (sg * (1.0 + g * (1.0 - sg)))).astype(BF16)
            dg_ref[s] = dg
            dup_ref[s] = dup
            dh2 = dh2 + _dot(dg, wg_ref[s]) + _dot(dup, wu_ref[s])
        dx, dgpf = _rms_bwd(xh, r, gpf, dh2)
        dgpf_ref[...] += dgpf
        dx1_ref[...] = dout + dx

    return _pcall(
        body, name="ffn_fwd_bwd", grid=(SEQ // TM,),
        in_specs=[_rows(D_MODEL), _rows(D_MODEL), _resident((N_SHARD, FF_S, D_MODEL)),
                  _resident((N_SHARD, FF_S, D_MODEL)), _resident((N_SHARD, FF_S, D_MODEL)),
                  _resident((1, D_MODEL)), _resident((1, D_MODEL))],
        out_specs=[_rows(D_MODEL), _rows3(N_SHARD, FF_S), _rows3(N_SHARD, FF_S), _rows3(N_SHARD, FF_S),
                   _rows(D_MODEL), _rows(D_MODEL), _acc(D_MODEL), _acc(D_MODEL), _acc(D_MODEL)],
        out_shape=[_sds((SEQ, D_MODEL), BF16), _sds((N_SHARD, SEQ, FF_S), BF16), _sds((N_SHARD, SEQ, FF_S), BF16),
                   _sds((N_SHARD, SEQ, FF_S), BF16), _sds((SEQ, D_MODEL), BF16), _sds((SEQ, D_MODEL), F32),
                   _sds((1, D_MODEL), F32), _sds((1, D_MODEL), F32), _sds((1, D_MODEL), F32)],
        scratch_shapes=[pltpu.VMEM((N_SHARD, TM, FF_S), F32), pltpu.VMEM((N_SHARD, TM, FF_S), F32)],
        args=(x1, target, w_gate, w_up, w_down, g_pre, g_post), comms=comms)


def _wgrad(a, b, a_spec, b_spec, out_block, name, comms=()):
    def body(a_ref, b_ref, o_ref):
        av = a_ref[0] if len(a_ref.shape) == 3 else a_ref[...]
        bv = b_ref[0] if len(b_ref.shape) == 3 else b_ref[...]
        o_ref[0] = _dot_tn(av, bv)

    return _pcall(
        body, name=name, grid=(N_SHARD,),
        in_specs=[a_spec, b_spec],
        out_specs=pl.BlockSpec((1,) + out_block, lambda s: (s, 0, 0)),
        out_shape=_sds((N_SHARD,) + out_block, F32),
        args=(a, b), comms=comms)


def _outproj_bwd(dx1, y, attn, sgu, w_out, g_post, g_attn, g_sgu, comms=()):
    def body(dx1_ref, y_ref, attn_ref, sgu_ref, w_ref, gp_ref, ga_ref, gs_ref,
             dy_ref, dsgu_ref, dgp_ref, dga_ref, dgs_ref, *rest):
        dattn_refs, delta_refs, (slabs_a, slabs_b) = rest[0:3], rest[3:6], rest[6:]

        @pl.when(pl.program_id(0) == 0)
        def _():
            dgp_ref[...] = jnp.zeros_like(dgp_ref)
            dga_ref[...] = jnp.zeros_like(dga_ref)
            dgs_ref[...] = jnp.zeros_like(dgs_ref)

        yh, ry = _rms_stats(y_ref[...])
        dy, dgp = _rms_bwd(yh, ry, gp_ref[...], dx1_ref[...])
        dgp_ref[...] += dgp
        dyb = dy.astype(BF16)
        dy_ref[...] = dyb
        dmixed = jnp.concatenate([_dot_nt(dyb, w_ref[s]) for s in range(N_SHARD)], axis=1)
        attn = attn_ref[...]
        ah, ra = _rms_stats(attn)
        dattn, dga = _rms_bwd(ah, ra, ga_ref[...], dmixed[:, 0:512])
        dga_ref[...] += dga
        sh, rs = _rms_stats(sgu_ref[...])
        dsgu, dgs = _rms_bwd(sh, rs, gs_ref[...], dmixed[:, 512:1024])
        dgs_ref[...] += dgs
        dsgu_ref[...] = dsgu
        delta = _dot_exact(dattn * attn, _head_sum())
        dattn_refs[0][...] = dattn.astype(BF16)
        delta_refs[0][...] = delta
        for i, dil in enumerate(DILATIONS[1:]):
            _store_view(dattn, dattn_refs[i + 1], slabs_a, dil)
            _store_view(delta, delta_refs[i + 1], slabs_b, dil)

    return _pcall(
        body, name="outproj_bwd", grid=(SEQ // TM,),
        in_specs=[_rows(D_MODEL), _rows(D_MODEL), _rows(512), _rows(512), _resident((N_SHARD, OUT_S, D_MODEL)),
                  _resident((1, D_MODEL)), _resident((1, 512)), _resident((1, 512))],
        out_specs=[_rows(D_MODEL), _rows(512), _acc(D_MODEL), _acc(512), _acc(512)]
        + [_view_rows(dil) for dil in DILATIONS] + [_view_rows(dil, HEAD_W) for dil in DILATIONS],
        out_shape=[_sds((SEQ, D_MODEL), BF16), _sds((SEQ, 512), F32),
                   _sds((1, D_MODEL), F32), _sds((1, 512), F32), _sds((1, 512), F32)]
        + [_view_shape(dil, BF16) for dil in DILATIONS] + [_view_shape(dil, F32, HEAD_W) for dil in DILATIONS],
        scratch_shapes=[_slab_scratch(), _slab_scratch()],
        args=(dx1, y, attn, sgu, w_out, g_post, g_attn, g_sgu), comms=comms)


def _sgu_bwd(u, vs, dsgu, lg, lb, w_sp, b_t, comms=(), after=()):
    nsteps = SEQ // TM

    def body(u_ref, vs_ref, ds_ref, lg_ref, lb_ref, w_ref, bt_ref,
             du_ref, dvs_ref, dw_ref, db_ref, dlg_ref, dlb_ref, dbias_scr):
        i = pl.program_id(0)

        @pl.when(i == 0)
        def _():
            dw_ref[...] = jnp.zeros_like(dw_ref)
            dlg_ref[...] = jnp.zeros_like(dlg_ref)
            dlb_ref[...] = jnp.zeros_like(dlb_ref)
            dbias_scr[...] = jnp.zeros_like(dbias_scr)

        wm = _masked_spatial(w_ref)
        ones_g = _group_ones()
        bias_full = _dot_exact(bt_ref[...], ones_g)
        u = u_ref[...]
        vs = vs_ref[...]
        lg = lg_ref[...]
        gu, xh, rstd, vnb, mixed = _sgu_core(u, vs, lg, lb_ref[...], wm, bias_full)
        dsgu = ds_ref[...]
        du_ref[...] = (dsgu * mixed * _gelu_grad(u)).astype(BF16)
        dmixed = dsgu * gu
        left = _left_half()
        dvn_rows = []
        for c in range(TM // CHUNK):
            rs = slice(c * CHUNK, (c + 1) * CHUNK)
            dm_c = dmixed[rs, :]
            dbias_scr[...] += dm_c
            pieces = []
            for p in range(4):
                ls = slice(p * 128, (p + 1) * 128)
                dmp = dm_c[:, ls]
                vp = vnb[rs, ls]
                dmb = dmp.astype(BF16)
                zero = jnp.zeros_like(dmb)
                dw_ref[2 * p] += _dot_nt(jnp.where(left, dmb, zero), vp)
                dw_ref[2 * p + 1] += _dot_nt(jnp.where(left, zero, dmb), vp)
                pieces.append(jnp.where(left, _dot_tn(wm[2 * p], dmb), _dot_tn(wm[2 * p + 1], dmb)))
            dvn_rows.append(jnp.concatenate(pieces, axis=1))
        dvn = jnp.concatenate(dvn_rows, axis=0)
        dlg_ref[...] += jnp.sum(dvn * xh, axis=0, keepdims=True)
        dlb_ref[...] += jnp.sum(dvn, axis=0, keepdims=True)
        dxh = dvn * lg
        dgv = rstd * (dxh - jnp.mean(dxh, axis=-1, keepdims=True) - xh * jnp.mean(dxh * xh, axis=-1, keepdims=True))
        dvs_ref[...] = (dgv * _gelu_grad(vs)).astype(BF16)

        @pl.when(i == nsteps - 1)
        def _():
            row = lax.broadcasted_iota(jnp.int32, (CHUNK, CHUNK), 0)
            col = lax.broadcasted_iota(jnp.int32, (CHUNK, CHUNK), 1)
            for g in range(SGU_GROUPS):
                dw_ref[g] = jnp.where(col <= row, dw_ref[g], 0.0)
            db_ref[...] = lax.dot_general(ones_g, dbias_scr[...], (((1,), (1,)), ((), ())),
                                          preferred_element_type=F32, precision=lax.Precision.HIGHEST)

    return _pcall(
        body, name="sgu_bwd", grid=(nsteps,),
        in_specs=[_rows(SGU_W), _rows(SGU_W), _rows(SGU_W), _resident((1, SGU_W)), _resident((1, SGU_W)),
                  _resident((SGU_GROUPS, CHUNK, CHUNK)), _resident((CHUNK, SGU_GROUPS))],
        out_specs=[_rows(SGU_W), _rows(SGU_W), pl.BlockSpec((SGU_GROUPS, CHUNK, CHUNK), lambda i: (0, 0, 0)),
                   _acc(CHUNK, SGU_GROUPS), _acc(SGU_W), _acc(SGU_W)],
        out_shape=[_sds((SEQ, SGU_W), BF16), _sds((SEQ, SGU_W), BF16), _sds((SGU_GROUPS, CHUNK, CHUNK), F32),
                   _sds((SGU_GROUPS, CHUNK), F32), _sds((1, SGU_W), F32), _sds((1, SGU_W), F32)],
        scratch_shapes=[pltpu.VMEM((CHUNK, SGU_W), F32)],
        args=(u, vs, dsgu, lg, lb, w_sp, b_t), comms=comms, after=after)


def _attn_bwd(qv, kv, vv, dov, deltav, lsev, dil, comms=(), after=()):
    seg = SEQ // dil
    nblk = seg // CHUNK

    def body(q_ref, k_ref, v_ref, do_ref, dl_ref, lse_ref, dq_ref, dk_ref, dv_ref, dk_wait, dv_wait):
        left = _left_half()
        m_cur, m_prev = _block_masks()

        def blk(b, carry):
            r0 = pl.multiple_of(b * CHUNK, CHUNK)
            rp = pl.multiple_of(jnp.maximum(b - 1, 0) * CHUNK, CHUNK)
            prev_ok = m_prev & (b > 0)
            sides = tuple(enumerate((left, ~left)))
            zero = jnp.zeros((CHUNK, CHUNK), BF16)
            tiles, firsts = [], []
            for hp in range(4):
                ls = slice(hp * 128, (hp + 1) * 128)
                qp = q_ref[pl.ds(r0, CHUNK), ls]
                kc = k_ref[pl.ds(r0, CHUNK), ls]
                vc = v_ref[pl.ds(r0, CHUNK), ls]
                dop = do_ref[pl.ds(r0, CHUNK), ls]
                kp = k_ref[pl.ds(rp, CHUNK), ls] if nblk > 1 else None
                vp = v_ref[pl.ds(rp, CHUNK), ls] if nblk > 1 else None
                tiles.append((ls, kc, kp))
                for side, hm in sides:
                    qh = jnp.where(hm, qp, zero)
                    doh = jnp.where(hm, dop, zero)
                    cur = (_dot_nt(qh, kc), _dot_nt(doh, vc))
                    prev = (_dot_nt(qh, kp), _dot_nt(doh, vp)) if nblk > 1 else None
                    firsts.append((qh, doh, cur, prev))
            seconds = []
            for i, (qh, doh, cur, prev) in enumerate(firsts):
                lse_h = lse_ref[pl.ds(r0, CHUNK), :][:, 16 * i:16 * i + 1]
                dl_h = dl_ref[pl.ds(r0, CHUNK), :][:, 16 * i:16 * i + 1]
                pc = jnp.exp(jnp.where(m_cur, cur[0] - lse_h, NEG))
                out = [pc.astype(BF16), (pc * (cur[1] - dl_h)).astype(BF16), None, None]
                if nblk > 1:
                    pp = jnp.exp(jnp.where(prev_ok, prev[0] - lse_h, NEG))
                    out[2:] = [pp.astype(BF16), (pp * (prev[1] - dl_h)).astype(BF16)]
                seconds.append(out)
            for hp, (ls, kc, kp) in enumerate(tiles):
                dq = jnp.zeros((CHUNK, CHUNK), F32)
                dkc = jnp.zeros((CHUNK, CHUNK), F32)
                dvc = jnp.zeros((CHUNK, CHUNK), F32)
                dkp = jnp.zeros((CHUNK, CHUNK), F32)
                dvp = jnp.zeros((CHUNK, CHUNK), F32)
                for side, hm in sides:
                    qh, doh, _, _ = firsts[2 * hp + side]
                    pcb, dsc, ppb, dsp = seconds[2 * hp + side]
                    dq = dq + _dot(dsc, jnp.where(hm, kc, zero))
                    dkc = dkc + _dot_tn(dsc, qh)
                    dvc = dvc + _dot_tn(pcb, doh)
                    if nblk > 1:
                        dq = dq + _dot(dsp, jnp.where(hm, kp, zero))
                        dkp = dkp + _dot_tn(dsp, qh)
                        dvp = dvp + _dot_tn(ppb, doh)
                dq_ref[pl.ds(r0, CHUNK), ls] = dq.astype(dq_ref.dtype)
                if nblk == 1:
                    dk_ref[pl.ds(r0, CHUNK), ls] = dkc.astype(dk_ref.dtype)
                    dv_ref[pl.ds(r0, CHUNK), ls] = dvc.astype(dv_ref.dtype)
                else:
                    @pl.when(b > 0)
                    def _():
                        dk_ref[pl.ds(rp, CHUNK), ls] = (dk_wait[:, ls] + dkp).astype(dk_ref.dtype)
                        dv_ref[pl.ds(rp, CHUNK), ls] = (dv_wait[:, ls] + dvp).astype(dv_ref.dtype)

                    dk_wait[:, ls] = dkc
                    dv_wait[:, ls] = dvc
            return carry

        lax.fori_loop(0, nblk, blk, 0)
        if nblk > 1:
            last = (nblk - 1) * CHUNK
            dk_ref[last:last + CHUNK, :] = dk_wait[...].astype(dk_ref.dtype)
            dv_ref[last:last + CHUNK, :] = dv_wait[...].astype(dv_ref.dtype)

    spec = pl.BlockSpec((seg, ATTN_W), lambda r: (0, r))
    return _pcall(
        body, name=f"attn_bwd_d{dil}", grid=(dil,),
        in_specs=[spec] * 4 + [pl.BlockSpec((seg, HEAD_W), lambda r: (0, r))] * 2, out_specs=[spec] * 3,
        out_shape=[_sds((seg, dil * ATTN_W), BF16)] * 3,
        scratch_shapes=[pltpu.VMEM((CHUNK, ATTN_W), F32), pltpu.VMEM((CHUNK, ATTN_W), F32)],
        args=(qv, kv, vv, dov, deltav, lsev), comms=comms, after=after)


def _inproj_bwd(dqs, dks, dvs, du, dvs_sgu, pos, x, dx1, w_in, g_pre, comms=()):
    def body(dq1, dq2, dq3, dk1, dk2, dk3, dv1, dv2, dv3, du_ref, dvs_ref, pos_ref, x_ref, dx1_ref, w_ref, g_ref,
             dproj_ref, gx_ref, dg_ref, slabs_a, slabs_b):
        @pl.when(pl.program_id(0) == 0)
        def _():
            dg_ref[...] = jnp.zeros_like(dg_ref)

        def total(r1, r2, r3):
            return r1[...] + _load_view(r2, slabs_a, DILATIONS[1]) + _load_view(r3, slabs_b, DILATIONS[2])

        tabs = _rot_tables(pos_ref[...])
        dproj_ref[:, 0:512] = _rope_bwd(total(dq1, dq2, dq3) * np.float32(ATTN_SCALE), tabs).astype(BF16)
        dproj_ref[:, 512:1024] = _rope_bwd(total(dk1, dk2, dk3), tabs).astype(BF16)
        dproj_ref[:, 1024:1536] = total(dv1, dv2, dv3).astype(BF16)
        dproj_ref[:, 1536:2048] = du_ref[...]
        dproj_ref[:, 2048:2560] = dvs_ref[...]
        dh = jnp.zeros((TM, D_MODEL), F32)
        for s in range(N_SHARD):
            dh = dh + _dot_nt(dproj_ref[:, s * IN_S:(s + 1) * IN_S], w_ref[s])
        g = g_ref[...]
        xh, r = _rms_stats(x_ref[...])
        dx, dg = _rms_bwd(xh, r, g, dh)
        dg_ref[...] += dg
        gx_ref[...] = dx1_ref[...] + dx

    return _pcall(
        body, name="inproj_bwd", grid=(SEQ // TM,),
        in_specs=[_view_rows(dil) for dil in DILATIONS] * 3
        + [_rows(512), _rows(512), _rows(1), _rows(D_MODEL), _rows(D_MODEL),
           _resident((N_SHARD, D_MODEL, IN_S)), _resident((1, D_MODEL))],
        out_specs=[_rows(PROJ_W), _rows(D_MODEL), _acc(D_MODEL)],
        out_shape=[_sds((SEQ, PROJ_W), BF16), _sds((SEQ, D_MODEL), F32), _sds((1, D_MODEL), F32)],
        scratch_shapes=[_slab_scratch(), _slab_scratch()],
        args=(*dqs, *dks, *dvs, du, dvs_sgu, pos, x, dx1, w_in, g_pre), comms=comms)


def _to_view(a, dil):
    return a if dil == 1 else a.reshape(SEQ // dil, dil * a.shape[1])


def _from_view(a, dil):
    return a if dil == 1 else a.reshape(SEQ, a.shape[1] // dil)


def _local_step(x, pos, target, w_in, w_out, w_gate, w_up, w_down, small):
    b_t = small["sgu_b_spatial"].T
    h, u, vs, *qkv = _inproj_fwd(x, pos, small["pre_mix_norm"], w_in)
    sgu = _sgu_fwd(u, vs, small["sgu_ln_gain"], small["sgu_ln_bias"], small["sgu_w_spatial"], b_t)
    views = [tuple(qkv[3 * i:3 * i + 3]) for i in range(len(DILATIONS))]
    o_list, l_list = [], []
    for dil, (qv, kv, vv) in zip(DILATIONS, views):
        o, l = _attn_fwd(qv, kv, vv, dil)
        o_list.append(o)
        l_list.append(l)
    attn, mixed, y, x1, *lses = _mix_out_fwd(o_list, l_list, sgu, x, w_out, small["attn_out_norm"],
                                             small["sgu_out_norm"], small["post_mix_norm"])
    h2, a, dg, dup, df, dx1, loss_cols, d_pre_ffn, d_post_ffn = _ffn_fwd_bwd(
        x1, target, w_gate, w_up, w_down, small["pre_ffn_norm"], small["post_ffn_norm"])

    full_tok = pl.BlockSpec((SEQ, D_MODEL), lambda s: (0, 0), pipeline_mode=pl.Buffered(1))
    ff_tok = pl.BlockSpec((1, SEQ, FF_S), lambda s: (s, 0, 0))
    gw_gate = _wgrad(dg, h2, ff_tok, full_tok, (FF_S, D_MODEL), "wgrad_gate")
    gw_up = _wgrad(dup, h2, ff_tok, full_tok, (FF_S, D_MODEL), "wgrad_up")
    gw_down = _wgrad(a, df, ff_tok, full_tok, (FF_S, D_MODEL), "wgrad_down")

    dy, dsgu, d_post_mix, d_attn_norm, d_sgu_norm, *dviews = _outproj_bwd(
        dx1, y, attn, sgu, w_out, small["post_mix_norm"], small["attn_out_norm"], small["sgu_out_norm"])
    gw_out = _wgrad(mixed, dy, pl.BlockSpec((SEQ, OUT_S), lambda s: (0, s)), full_tok, (OUT_S, D_MODEL), "wgrad_out")
    du, dvs_sgu, d_w_sp, d_b_sp, d_ln_gain, d_ln_bias = _sgu_bwd(
        u, vs, dsgu, small["sgu_ln_gain"], small["sgu_ln_bias"], small["sgu_w_spatial"], b_t)

    dqs, dks, dvs = [], [], []
    for i, (dil, (qv, kv, vv)) in enumerate(zip(DILATIONS, views)):
        dq, dk, dv = _attn_bwd(qv, kv, vv, dviews[i], dviews[3 + i], lses[i], dil)
        dqs.append(dq)
        dks.append(dk)
        dvs.append(dv)
    dproj, grad_x, d_pre_mix = _inproj_bwd(dqs, dks, dvs, du, dvs_sgu, pos, x, dx1, w_in, small["pre_mix_norm"])
    gw_in = _wgrad(h, dproj, full_tok, pl.BlockSpec((SEQ, IN_S), lambda s: (0, s)), (D_MODEL, IN_S), "wgrad_in")

    small_grads = {
        "pre_mix_norm": d_pre_mix, "sgu_ln_gain": d_ln_gain, "sgu_ln_bias": d_ln_bias, "sgu_w_spatial": d_w_sp,
        "sgu_b_spatial": d_b_sp, "attn_out_norm": d_attn_norm, "sgu_out_norm": d_sgu_norm,
        "post_mix_norm": d_post_mix, "pre_ffn_norm": d_pre_ffn, "post_ffn_norm": d_post_ffn,
    }
    return loss_cols, grad_x, (gw_in, gw_out, gw_gate, gw_up, gw_down), small_grads


def _coords():
    return lax.axis_index("x"), lax.axis_index("y"), lax.axis_index("c")


def _other_chips(x, y):
    return [(1 - x, y), (x, 1 - y), (1 - x, 1 - y)]


def _comm_call(body, name, n_in, out_shape, scratch_shapes):
    return pl.pallas_call(
        body, name=name, in_specs=[ANY] * n_in, out_specs=[ANY] * len(out_shape), out_shape=out_shape,
        scratch_shapes=scratch_shapes,
        compiler_params=pltpu.CompilerParams(has_side_effects=True),
    )


def _gather_weights(shards):
    n = len(shards)
    halves = [s.reshape(2, s.shape[0] // 2, s.shape[1]) for s in shards]

    def body(*refs):
        ins, outs = refs[:n], refs[n:2 * n]
        send_sems, recv_sems = refs[2 * n:]
        x, y, c = _coords()
        s_me = 2 * x + y
        chips = _other_chips(x, y)
        sibling = (x, y, 1 - c)

        def copy(k, w, shard, cc, to):
            src = ins[w].at[cc] if shard is None else outs[w].at[shard, cc]
            dst = outs[w].at[s_me if shard is None else shard, cc]
            return pltpu.make_async_remote_copy(src_ref=src, dst_ref=dst, send_sem=send_sems.at[k],
                                                recv_sem=recv_sems.at[k], device_id=to, device_id_type=MESH)

        first = [copy(j * n + w, w, None, c, (cx, cy, c)) for j, (cx, cy) in enumerate(chips) for w in range(n)]
        for cp in first:
            cp.start()
        passed = []
        for j, (cx, cy) in enumerate(chips):
            for w in range(n):
                copy(j * n + w, w, 2 * cx + cy, c, (x, y, c)).wait_recv()
                fw = copy((3 + j) * n + w, w, 2 * cx + cy, c, sibling)
                fw.start()
                passed.append(fw)
        for j, (cx, cy) in enumerate(chips):
            for w in range(n):
                copy((3 + j) * n + w, w, 2 * cx + cy, 1 - c, (x, y, c)).wait_recv()
        for cp in first + passed:
            cp.wait_send()

    out_shape = [_sds((N_SHARD,) + h.shape, h.dtype) for h in halves]
    scratch = [pltpu.SemaphoreType.DMA((6 * n,)), pltpu.SemaphoreType.DMA((6 * n,))]
    full = _comm_call(body, "comm_gather_weights", n, out_shape, scratch)(*halves)
    s_me = 2 * lax.axis_index("x") + lax.axis_index("y")
    full = [lax.dynamic_update_slice(f, h[None], (s_me, 0, 0, 0)) for f, h in zip(full, halves)]
    return [f.reshape((N_SHARD,) + s.shape) for f, s in zip(full, shards)]


def _rs_to_sibling(gws):
    n = len(gws)

    def body(*refs):
        ins, outs = refs[:n], refs[n:2 * n]
        send_sems, recv_sems = refs[2 * n:]
        x, y, c = _coords()
        copies = []
        for w in range(n):
            hw = gws[w].shape[1] // 2
            copies.append(pltpu.make_async_remote_copy(
                src_ref=ins[w].at[:, pl.ds((1 - c) * hw, hw), :], dst_ref=outs[w], send_sem=send_sems.at[w],
                recv_sem=recv_sems.at[w], device_id=(x, y, 1 - c), device_id_type=MESH))
        for cp in copies:
            cp.start()
        for cp in copies:
            cp.wait()

    out_shape = [_sds((N_SHARD, g.shape[1] // 2, g.shape[2]), g.dtype) for g in gws]
    scratch = [pltpu.SemaphoreType.DMA((n,)), pltpu.SemaphoreType.DMA((n,))]
    return _comm_call(body, "comm_rs_sibling", n, out_shape, scratch)(*gws)


def _rs_chip_sum(gw, recv, core):
    _, rows, cols = gw.shape
    hw = rows // 2

    def body(c_ref, g_ref, r_ref, o_ref):
        o_ref[...] = (g_ref[...] + r_ref[...]).astype(BF16)

    return pl.pallas_call(
        body, name="rs_chip_sum",
        grid_spec=pltpu.PrefetchScalarGridSpec(
            num_scalar_prefetch=1, grid=(N_SHARD,),
            in_specs=[pl.BlockSpec((1, hw, cols), lambda s, c_ref: (s, c_ref[0], 0)),
                      pl.BlockSpec((1, hw, cols), lambda s, c_ref: (s, 0, 0))],
            out_specs=pl.BlockSpec((1, hw, cols), lambda s, c_ref: (s, 0, 0))),
        out_shape=_sds((N_SHARD, hw, cols), BF16),
        compiler_params=_seq_params(),
    )(core, gw, recv)


def _rs_between_chips(pbs):
    n = len(pbs)

    def body(*refs):
        ins, outs = refs[:n], refs[n:2 * n]
        send_sems, recv_sems = refs[2 * n:]
        x, y, c = _coords()
        copies = []
        for j, (cx, cy) in enumerate(_other_chips(x, y)):
            for w in range(n):
                copies.append(pltpu.make_async_remote_copy(
                    src_ref=ins[w].at[2 * cx + cy], dst_ref=outs[w].at[j], send_sem=send_sems.at[j * n + w],
                    recv_sem=recv_sems.at[j * n + w], device_id=(cx, cy, c), device_id_type=MESH))
        for cp in copies:
            cp.start()
        for cp in copies:
            cp.wait()

    out_shape = [_sds((3,) + p.shape[1:], p.dtype) for p in pbs]
    scratch = [pltpu.SemaphoreType.DMA((3 * n,)), pltpu.SemaphoreType.DMA((3 * n,))]
    return _comm_call(body, "comm_rs_chips", n, out_shape, scratch)(*pbs)


def _rs_final_sum(gw, recv_sib, recv_chips, shard_core):
    _, rows, cols = gw.shape
    hw = rows // 2

    def body(sc_ref, g_ref, r_ref, rc_ref, o_ref):
        acc = g_ref[0] + r_ref[0]
        for j in range(3):
            acc = acc + rc_ref[j].astype(F32)
        o_ref[0] = acc

    return pl.pallas_call(
        body, name="rs_final_sum",
        grid_spec=pltpu.PrefetchScalarGridSpec(
            num_scalar_prefetch=1, grid=(1,),
            in_specs=[pl.BlockSpec((1, hw, cols), lambda i, sc: (sc[0], sc[1], 0)),
                      pl.BlockSpec((1, hw, cols), lambda i, sc: (sc[0], 0, 0)),
                      pl.BlockSpec((3, hw, cols), lambda i, sc: (0, 0, 0))],
            out_specs=pl.BlockSpec((1, hw, cols), lambda i, sc: (sc[1], 0, 0))),
        out_shape=_sds((2, hw, cols), F32),
        compiler_params=_seq_params(),
    )(shard_core, gw, recv_sib, recv_chips)


def _rs_join_halves(halves):
    n = len(halves)

    def body(*refs):
        bufs = refs[n:2 * n]
        send_sems, recv_sems = refs[2 * n:]
        x, y, c = _coords()
        remote = [pltpu.make_async_remote_copy(
            src_ref=bufs[w].at[c], dst_ref=bufs[w].at[c], send_sem=send_sems.at[w], recv_sem=recv_sems.at[w],
            device_id=(x, y, 1 - c), device_id_type=MESH) for w in range(n)]
        for cp in remote:
            cp.start()
        for w in range(n):
            remote[w].wait_send()
            pltpu.make_async_remote_copy(
                src_ref=bufs[w].at[c], dst_ref=bufs[w].at[1 - c], send_sem=send_sems.at[w],
                recv_sem=recv_sems.at[w], device_id=(x, y, c), device_id_type=MESH).wait_recv()

    joined = pl.pallas_call(
        body, name="comm_rs_join", in_specs=[ANY] * n, out_specs=[ANY] * n,
        out_shape=[_sds(h.shape, h.dtype) for h in halves], input_output_aliases={w: w for w in range(n)},
        scratch_shapes=[pltpu.SemaphoreType.DMA((n,)), pltpu.SemaphoreType.DMA((n,))],
        compiler_params=pltpu.CompilerParams(has_side_effects=True),
    )(*halves)
    return [j.reshape(2 * h.shape[1], h.shape[2]) for j, h in zip(joined, halves)]


def _allreduce_small(buf):
    rows, cols = buf.shape

    def body(in_ref, out_ref, slots, send_sems, recv_sems):
        x, y, c = _coords()
        me = 4 * x + 2 * y + c
        copies, peers = [], []
        for k in range(1, 8):
            px = 1 - x if (k >> 2) & 1 else x
            py = 1 - y if (k >> 1) & 1 else y
            pc = 1 - c if k & 1 else c
            peers.append(4 * px + 2 * py + pc)
            copies.append(pltpu.make_async_remote_copy(
                src_ref=in_ref, dst_ref=slots.at[me], send_sem=send_sems.at[k - 1], recv_sem=recv_sems.at[k - 1],
                device_id=(px, py, pc), device_id_type=MESH))
        for cp in copies:
            cp.start()
        slots[me] = in_ref[...]
        for k in range(7):
            pltpu.make_async_remote_copy(
                src_ref=in_ref, dst_ref=slots.at[peers[k]], send_sem=send_sems.at[k], recv_sem=recv_sems.at[k],
                device_id=(x, y, c), device_id_type=MESH).wait_recv()
        for cp in copies:
            cp.wait_send()
        acc = slots[0]
        for i in range(1, 8):
            acc = acc + slots[i]
        out_ref[...] = acc

    vmem = pl.BlockSpec(memory_space=pltpu.VMEM)
    return pl.pallas_call(
        body, name="comm_allreduce_small", in_specs=[vmem], out_specs=vmem, out_shape=_sds((rows, cols), F32),
        scratch_shapes=[pltpu.VMEM((8, rows, cols), F32), pltpu.SemaphoreType.DMA((7,)), pltpu.SemaphoreType.DMA((7,))],
        compiler_params=pltpu.CompilerParams(has_side_effects=True, vmem_limit_bytes=VMEM_LIMIT),
    )(buf)


def _adamw(w, g, m, v, block_rows, name, after=()):
    rows, cols = w.shape

    def body(w_ref, g_ref, m_ref, v_ref, *rest):
        d_ref, nm_ref, nv_ref = rest[len(after):]
        g = g_ref[...]
        m = ADAM_B1 * m_ref[...] + (1.0 - ADAM_B1) * g
        v = ADAM_B2 * v_ref[...] + (1.0 - ADAM_B2) * (g * g)
        m_hat = m / (1.0 - ADAM_B1 ** ADAM_STEP)
        v_hat = v / (1.0 - ADAM_B2 ** ADAM_STEP)
        d_ref[...] = -ADAM_LR * (m_hat / (jnp.sqrt(v_hat) + ADAM_EPS) + ADAM_WD * w_ref[...])
        nm_ref[...] = m
        nv_ref[...] = v

    spec = pl.BlockSpec((block_rows, cols), lambda i: (i, 0))
    return pl.pallas_call(
        body, name=name, grid=(rows // block_rows,), in_specs=[spec] * 4 + [ANY] * len(after), out_specs=[spec] * 3,
        out_shape=[_sds((rows, cols), F32)] * 3,
        compiler_params=_seq_params(),
    )(w, g, m, v, *after)


WEIGHTS = ("pre_mix_norm", "w_in", "sgu_ln_gain", "sgu_ln_bias", "sgu_w_spatial", "sgu_b_spatial", "attn_out_norm",
           "sgu_out_norm", "w_out", "post_mix_norm", "pre_ffn_norm", "w_gate", "w_up", "w_down", "post_ffn_norm")
BIG = ("w_in", "w_out", "w_gate", "w_up", "w_down")
BIG_ADAM_ROWS = {"w_in": 256, "w_out": 128, "w_gate": 352, "w_up": 352, "w_down": 352}
SMALL = ("pre_mix_norm", "post_mix_norm", "pre_ffn_norm", "post_ffn_norm", "sgu_ln_gain", "sgu_ln_bias",
         "attn_out_norm", "sgu_out_norm", "sgu_w_spatial", "sgu_b_spatial")


def _pack_small(d):
    flat = [d[n].reshape(-1) for n in SMALL]
    used = sum(f.shape[0] for f in flat)
    flat.append(jnp.zeros((SMALL_ROWS * 1024 - used,), F32))
    return jnp.concatenate(flat).reshape(SMALL_ROWS, 1024)


def _unpack_small(buf, shapes):
    flat = buf.reshape(-1)
    out, off = {}, 0
    for n in SMALL:
        size = int(np.prod(shapes[n]))
        out[n] = flat[off:off + size].reshape(shapes[n])
        off += size
    return out


def _kernel_unoverlapped(x, positions, pre_mix_norm, w_in, sgu_ln_gain, sgu_ln_bias, sgu_w_spatial, sgu_b_spatial, attn_out_norm, sgu_out_norm, w_out, post_mix_norm, pre_ffn_norm, w_gate, w_up, w_down, post_ffn_norm, loss_target, m_pre_mix_norm, m_w_in, m_sgu_ln_gain, m_sgu_ln_bias, m_sgu_w_spatial, m_sgu_b_spatial, m_attn_out_norm, m_sgu_out_norm, m_w_out, m_post_mix_norm, m_pre_ffn_norm, m_w_gate, m_w_up, m_w_down, m_post_ffn_norm, v_pre_mix_norm, v_w_in, v_sgu_ln_gain, v_sgu_ln_bias, v_sgu_w_spatial, v_sgu_b_spatial, v_attn_out_norm, v_sgu_out_norm, v_w_out, v_post_mix_norm, v_pre_ffn_norm, v_w_gate, v_w_up, v_w_down, v_post_ffn_norm):
    a = dict(locals())
    cx, cy, cc = _coords()
    core = jnp.stack([cc]).astype(jnp.int32)
    shard_core = jnp.stack([2 * cx + cy, cc]).astype(jnp.int32)

    full = _gather_weights([a[n][0].astype(BF16) for n in BIG])
    small = {n: (a[n][0] if a[n].ndim > 2 else a[n]) for n in SMALL}
    loss_cols, grad_x, gws, small_grads = _local_step(
        x[0], positions.reshape(SEQ, 1), loss_target[0], *full, small)
    loss = lax.psum(jnp.sum(loss_cols) * np.float32(0.5 / D_MODEL), ("x", "y", "c"))

    recv_sib = _rs_to_sibling(list(gws))
    chip_part = [_rs_chip_sum(g, r, core) for g, r in zip(gws, recv_sib)]
    recv_chips = _rs_between_chips(chip_part)
    halves = [_rs_final_sum(g, r, rc, shard_core) for g, r, rc in zip(gws, recv_sib, recv_chips)]
    big_grads = dict(zip(BIG, _rs_join_halves(halves)))

    shapes = {n: a[n].shape for n in SMALL}
    small_sum = _allreduce_small(_pack_small(small_grads))

    grads, deltas, new_m, new_v = {}, {}, {}, {}
    for n in BIG:
        grads[n] = big_grads[n][None]
        d, nm, nv = _adamw(a[n][0], big_grads[n], a["m_" + n][0], a["v_" + n][0], BIG_ADAM_ROWS[n], "adamw_" + n)
        deltas[n], new_m[n], new_v[n] = d[None], nm[None], nv[None]
    d, nm, nv = _adamw(_pack_small({n: a[n] for n in SMALL}), small_sum, _pack_small({n: a["m_" + n] for n in SMALL}),
                       _pack_small({n: a["v_" + n] for n in SMALL}), SMALL_ROWS, "adamw_small")
    grads.update(_unpack_small(small_sum, shapes))
    deltas.update(_unpack_small(d, shapes))
    new_m.update(_unpack_small(nm, shapes))
    new_v.update(_unpack_small(nv, shapes))
    return (loss, grad_x[None], *[grads[n] for n in WEIGHTS], *[deltas[n] for n in WEIGHTS],
            *[new_m[n] for n in WEIGHTS], *[new_v[n] for n in WEIGHTS])


def _remote(src, dst, send_sem, recv_sem, to):
    return pltpu.make_async_remote_copy(src_ref=src, dst_ref=dst, send_sem=send_sem, recv_sem=recv_sem,
                                        device_id=to, device_id_type=MESH)


def _halves(a):
    *lead, rows, cols = a.shape
    return a.reshape(*lead, 2, rows // 2, cols)


def _gather_ici(shards):
    n = len(shards)

    def desc(ins, outs, ss, rs, j, w, landed):
        x, y, c = _coords()
        cx, cy = _other_chips(x, y)[j]
        shard = 2 * cx + cy if landed else 2 * x + y
        return _remote(ins[w].at[c], outs[w].at[shard, c], ss.at[j * n + w], rs.at[j * n + w], (cx, cy, c))

    def start(ins, outs, ss, rs):
        for j in range(3):
            for w in range(n):
                desc(ins, outs, ss, rs, j, w, False).start()

    def finish(ins, outs, ss, rs):
        for j in range(3):
            for w in range(n):
                desc(ins, outs, ss, rs, j, w, True).wait_recv()
                desc(ins, outs, ss, rs, j, w, False).wait_send()

    return _Comm(shards, [_sds((N_SHARD,) + s.shape, s.dtype) for s in shards], 3 * n, start, finish)


def _gather_pass(fulls):
    n = len(fulls)

    def desc(bufs, ss, rs, j, w, landed):
        x, y, c = _coords()
        cx, cy = _other_chips(x, y)[j]
        shard = 2 * cx + cy
        return _remote(bufs[w].at[shard, c], bufs[w].at[shard, 1 - c if landed else c],
                       ss.at[j * n + w], rs.at[j * n + w], (x, y, 1 - c))

    def start(ins, outs, ss, rs):
        for j in range(3):
            for w in range(n):
                desc(outs, ss, rs, j, w, False).start()

    def finish(ins, outs, ss, rs):
        for j in range(3):
            for w in range(n):
                desc(outs, ss, rs, j, w, True).wait_recv()
                desc(outs, ss, rs, j, w, False).wait_send()

    return _Comm(fulls, [_sds(f.shape, f.dtype) for f in fulls], 3 * n, start, finish, aliased=True)


def _rs_sibling(gws):
    n = len(gws)

    def desc(ins, outs, ss, rs, w):
        x, y, c = _coords()
        return _remote(ins[w].at[:, 1 - c], outs[w], ss.at[w], rs.at[w], (x, y, 1 - c))

    def start(ins, outs, ss, rs):
        for w in range(n):
            desc(ins, outs, ss, rs, w).start()

    def finish(ins, outs, ss, rs):
        for w in range(n):
            desc(ins, outs, ss, rs, w).wait()

    out_shape = [_sds((N_SHARD, g.shape[1] // 2, g.shape[2]), g.dtype) for g in gws]
    return _Comm([_halves(g) for g in gws], out_shape, n, start, finish)


def _rs_chips(pbs):
    n = len(pbs)

    def desc(ins, outs, ss, rs, j, w):
        x, y, c = _coords()
        cx, cy = _other_chips(x, y)[j]
        return _remote(ins[w].at[2 * cx + cy], outs[w].at[j], ss.at[j * n + w], rs.at[j * n + w], (cx, cy, c))

    def start(ins, outs, ss, rs):
        for j in range(3):
            for w in range(n):
                desc(ins, outs, ss, rs, j, w).start()

    def finish(ins, outs, ss, rs):
        for j in range(3):
            for w in range(n):
                desc(ins, outs, ss, rs, j, w).wait()

    return _Comm(pbs, [_sds((3,) + p.shape[1:], p.dtype) for p in pbs], 3 * n, start, finish)


def _rs_join(halves):
    n = len(halves)

    def desc(bufs, ss, rs, w, landed):
        x, y, c = _coords()
        return _remote(bufs[w].at[c], bufs[w].at[1 - c if landed else c], ss.at[w], rs.at[w], (x, y, 1 - c))

    def start(ins, outs, ss, rs):
        for w in range(n):
            desc(outs, ss, rs, w, False).start()

    def finish(ins, outs, ss, rs):
        for w in range(n):
            desc(outs, ss, rs, w, True).wait_recv()
            desc(outs, ss, rs, w, False).wait_send()

    return _Comm(halves, [_sds(h.shape, h.dtype) for h in halves], n, start, finish, aliased=True)


def _small_exchange(buf):
    def desc(ins, outs, ss, rs, k, landed):
        x, y, c = _coords()
        px = 1 - x if (k >> 2) & 1 else x
        py = 1 - y if (k >> 1) & 1 else y
        pc = 1 - c if k & 1 else c
        slot = 4 * px + 2 * py + pc if landed else 4 * x + 2 * y + c
        return _remote(ins[0], outs[0].at[slot], ss.at[k - 1], rs.at[k - 1], (px, py, pc))

    def start(ins, outs, ss, rs):
        for k in range(1, 8):
            desc(ins, outs, ss, rs, k, False).start()

    def finish(ins, outs, ss, rs):
        for k in range(1, 8):
            desc(ins, outs, ss, rs, k, True).wait_recv()
            desc(ins, outs, ss, rs, k, False).wait_send()

    return _Comm([buf], [_sds((8,) + buf.shape, buf.dtype)], 7, start, finish)


HBM = pl.BlockSpec(memory_space=pltpu.HBM)
SEM = pl.BlockSpec(memory_space=pltpu.SEMAPHORE)
DATAFLOW = pltpu.SideEffectType.DATAFLOW_SIDE_EFFECTING


def _split_start(name, comm, lands, after):
    srcs = [pltpu.with_memory_space_constraint(s, pltpu.HBM) for s in comm.args]
    lands = [pltpu.with_memory_space_constraint(b, pltpu.HBM) for b in lands]
    ns, nb = len(srcs), len(lands)

    def body(*refs):
        send_sems, recv_sems = refs[ns + nb + 1], refs[ns + nb + 2]
        comm.start(refs[:ns], refs[ns:ns + nb], send_sems, recv_sems)
        refs[-1][...] = jnp.zeros_like(refs[-1])

    res = pl.pallas_call(
        body, name=name,
        out_shape=(pltpu.SemaphoreType.DMA((comm.n_sems,)), pltpu.SemaphoreType.DMA((comm.n_sems,)),
                   *[pltpu.HBM(b.shape, b.dtype) for b in srcs + lands], _sds((8, 128), F32)),
        in_specs=[HBM] * (ns + nb) + [ANY],
        out_specs=(SEM, SEM, *[HBM] * (ns + nb), pl.BlockSpec(memory_space=pltpu.VMEM)),
        input_output_aliases={i: 2 + i for i in range(ns + nb)},
        compiler_params=pltpu.CompilerParams(has_side_effects=DATAFLOW),
    )(*srcs, *lands, after)
    return res[0], res[1], list(res[2:2 + ns]), list(res[2 + ns:2 + ns + nb]), res[-1]


def _split_starts(name, comms, after):
    srcs = [[pltpu.with_memory_space_constraint(s, pltpu.HBM) for s in c.args] for c in comms]
    lands = [[pltpu.with_memory_space_constraint(lax.empty(o.shape, o.dtype), pltpu.HBM) for o in c.out_shape]
             for c in comms]
    bufs = [b for k in range(len(comms)) for b in srcs[k] + lands[k]]
    nb, nc = len(bufs), len(comms)

    def body(*refs):
        sems = refs[nb + 1:nb + 1 + 2 * nc]
        off = 0
        for k, c in enumerate(comms):
            ns, nl = len(srcs[k]), len(lands[k])
            c.start(refs[off:off + ns], refs[off + ns:off + ns + nl], sems[2 * k], sems[2 * k + 1])
            off += ns + nl
        refs[-1][...] = jnp.zeros_like(refs[-1])

    res = pl.pallas_call(
        body, name=name,
        out_shape=(*[pltpu.SemaphoreType.DMA((c.n_sems,)) for c in comms for _ in range(2)],
                   *[pltpu.HBM(b.shape, b.dtype) for b in bufs], _sds((8, 128), F32)),
        in_specs=[HBM] * nb + [ANY],
        out_specs=(*[SEM] * (2 * nc), *[HBM] * nb, pl.BlockSpec(memory_space=pltpu.VMEM)),
        input_output_aliases={i: 2 * nc + i for i in range(nb)},
        compiler_params=pltpu.CompilerParams(has_side_effects=DATAFLOW),
    )(*bufs, after)
    states, off = [], 2 * nc
    for k in range(nc):
        ns, nl = len(srcs[k]), len(lands[k])
        states.append((res[2 * k], res[2 * k + 1], list(res[off:off + ns]), list(res[off + ns:off + ns + nl])))
        off += ns + nl
    return states, res[-1]


def _split_wait(name, comm, send_sems, recv_sems, srcs, lands, after):
    ns, nb = len(srcs), len(lands)
    after = list(after) if isinstance(after, (list, tuple)) else [after]

    def body(*refs):
        comm.finish(refs[:ns], refs[ns:ns + nb], refs[ns + nb], refs[ns + nb + 1])

    res = pl.pallas_call(
        body, name=name,
        out_shape=tuple(pltpu.HBM(b.shape, b.dtype) for b in srcs + lands),
        in_specs=[HBM] * (ns + nb) + [SEM, SEM] + [ANY] * len(after), out_specs=tuple([HBM] * (ns + nb)),
        input_output_aliases={i: i for i in range(ns + nb)},
        compiler_params=pltpu.CompilerParams(has_side_effects=DATAFLOW),
    )(*srcs, *lands, send_sems, recv_sems, *after)
    return list(res[ns:])


LOSS_ROW = "loss_cols"
SMALL_EARLY = ("post_mix_norm", "pre_ffn_norm", "post_ffn_norm", "sgu_ln_gain", "sgu_ln_bias", "attn_out_norm",
               "sgu_out_norm", "sgu_w_spatial", "sgu_b_spatial", LOSS_ROW)
SMALL_LATE = ("pre_mix_norm",)


def _pack(d, names, rows):
    flat = [d[n].reshape(-1) for n in names]
    used = sum(f.shape[0] for f in flat)
    flat.append(jnp.zeros((rows * 1024 - used,), F32))
    return jnp.concatenate(flat).reshape(rows, 1024)


def _unpack(buf, names, shapes):
    flat = buf.reshape(-1)
    out, off = {}, 0
    for n in names:
        size = int(np.prod(shapes[n]))
        out[n] = flat[off:off + size].reshape(shapes[n])
        off += size
    return out


def _merge(comms):
    def run(phase):
        def go(ins, outs, ss, rs):
            ii = oi = si = 0
            for c in comms:
                getattr(c, phase)(ins[ii:ii + len(c.args)], outs[oi:oi + len(c.out_shape)],
                                  ss.at[pl.ds(si, c.n_sems)], rs.at[pl.ds(si, c.n_sems)])
                ii += len(c.args)
                oi += len(c.out_shape)
                si += c.n_sems
        return go

    return _Comm([a for c in comms for a in c.args], [o for c in comms for o in c.out_shape],
                 sum(c.n_sems for c in comms), run("start"), run("finish"))


def _chip_sums(gws, recvs, core):
    n = len(gws)
    _, rows, cols = gws[0].shape
    hw = rows // 2

    def body(c_ref, *refs):
        for k in range(n):
            refs[2 * n + k][...] = (refs[k][...] + refs[n + k][...]).astype(BF16)

    mine = pl.BlockSpec((1, hw, cols), lambda s, c_ref: (s, c_ref[0], 0))
    plain = pl.BlockSpec((1, hw, cols), lambda s, c_ref: (s, 0, 0))
    return pl.pallas_call(
        body, name="rs_chip_sums",
        grid_spec=pltpu.PrefetchScalarGridSpec(num_scalar_prefetch=1, grid=(N_SHARD,),
                                               in_specs=[mine] * n + [plain] * n, out_specs=[plain] * n),
        out_shape=[_sds((N_SHARD, hw, cols), BF16)] * n,
        compiler_params=_seq_params(),
    )(core, *gws, *recvs)


def _final_sums(gws, recv_sibs, recv_chips, shard_core):
    n = len(gws)
    _, rows, cols = gws[0].shape
    hw = rows // 2

    def body(sc_ref, *refs):
        for k in range(n):
            acc = refs[k][0] + refs[n + k][0]
            for j in range(3):
                acc = acc + refs[2 * n + k][j].astype(F32)
            refs[3 * n + k][0] = acc

    return pl.pallas_call(
        body, name="rs_final_sums",
        grid_spec=pltpu.PrefetchScalarGridSpec(
            num_scalar_prefetch=1, grid=(1,),
            in_specs=[pl.BlockSpec((1, hw, cols), lambda i, sc: (sc[0], sc[1], 0))] * n
            + [pl.BlockSpec((1, hw, cols), lambda i, sc: (sc[0], 0, 0))] * n
            + [pl.BlockSpec((3, hw, cols), lambda i, sc: (0, 0, 0))] * n,
            out_specs=[pl.BlockSpec((1, hw, cols), lambda i, sc: (sc[1], 0, 0))] * n),
        out_shape=[_sds((2, hw, cols), F32)] * n,
        compiler_params=_seq_params(),
    )(shard_core, *gws, *recv_sibs, *recv_chips)


def _adamw_multi(ws, gs, ms, vs, block_rows, name, after=()):
    n = len(ws)
    rows, cols = ws[0].shape

    def body(*refs):
        outs = refs[4 * n + len(after):]
        for k in range(n):
            g = refs[n + k][...]
            d, m, v = _adam_math(refs[k][...], g, refs[2 * n + k][...], refs[3 * n + k][...])
            outs[4 * k][...], outs[4 * k + 1][...], outs[4 * k + 2][...], outs[4 * k + 3][...] = g, d, m, v

    spec = pl.BlockSpec((block_rows, cols), lambda i: (i, 0))
    res = pl.pallas_call(
        body, name=name, grid=(rows // block_rows,), in_specs=[spec] * (4 * n) + [ANY] * len(after),
        out_specs=[spec] * (4 * n), out_shape=[_sds((rows, cols), F32)] * (4 * n),
        compiler_params=_seq_params(),
    )(*ws, *gs, *ms, *vs, *after)
    return [tuple(res[4 * k:4 * k + 4]) for k in range(n)]


def _wgrad_pair(a1, a2, b, a_spec, b_spec, out_block, name, comms=()):
    def body(a1_ref, a2_ref, b_ref, o1_ref, o2_ref):
        bv = b_ref[...]
        o1_ref[0] = _dot_tn(a1_ref[0], bv)
        o2_ref[0] = _dot_tn(a2_ref[0], bv)

    out_spec = pl.BlockSpec((1,) + out_block, lambda s: (s, 0, 0))
    return _pcall(
        body, name=name, grid=(N_SHARD,), in_specs=[a_spec, a_spec, b_spec], out_specs=[out_spec, out_spec],
        out_shape=[_sds((N_SHARD,) + out_block, F32)] * 2, args=(a1, a2, b), comms=comms)


def _comm_only(name, comms):
    return _pcall(lambda: None, name=name, grid=(1,), in_specs=[], out_specs=[], out_shape=[], args=(),
                  comms=comms)[1]


def _adam_math(w, g, m, v):
    m = ADAM_B1 * m + (1.0 - ADAM_B1) * g
    v = ADAM_B2 * v + (1.0 - ADAM_B2) * (g * g)
    m_hat = m / (1.0 - ADAM_B1 ** ADAM_STEP)
    v_hat = v / (1.0 - ADAM_B2 ** ADAM_STEP)
    return -ADAM_LR * (m_hat / (jnp.sqrt(v_hat) + ADAM_EPS) + ADAM_WD * w), m, v


def _adamw_small(own, slots, w, m, v, me):
    rows, cols = own.shape

    def body(me_ref, own_ref, slots_ref, w_ref, m_ref, v_ref, g_ref, d_ref, nm_ref, nv_ref):
        own_v = own_ref[...]
        g = jnp.where(me_ref[0] == 0, own_v, slots_ref[0])
        for i in range(1, 8):
            g = g + jnp.where(me_ref[0] == i, own_v, slots_ref[i])
        g_ref[...] = g
        d_ref[...], nm_ref[...], nv_ref[...] = _adam_math(w_ref[...], g, m_ref[...], v_ref[...])

    flat = pl.BlockSpec((rows, cols), lambda i, me_ref: (0, 0))
    return pl.pallas_call(
        body, name="adamw_small",
        grid_spec=pltpu.PrefetchScalarGridSpec(
            num_scalar_prefetch=1, grid=(1,),
            in_specs=[flat, pl.BlockSpec((8, rows, cols), lambda i, me_ref: (0, 0, 0)), flat, flat, flat],
            out_specs=[flat] * 4),
        out_shape=[_sds((rows, cols), F32)] * 4,
        compiler_params=_seq_params(),
    )(me, own, slots, w, m, v)


def kernel(x, positions, pre_mix_norm, w_in, sgu_ln_gain, sgu_ln_bias, sgu_w_spatial, sgu_b_spatial, attn_out_norm, sgu_out_norm, w_out, post_mix_norm, pre_ffn_norm, w_gate, w_up, w_down, post_ffn_norm, loss_target, m_pre_mix_norm, m_w_in, m_sgu_ln_gain, m_sgu_ln_bias, m_sgu_w_spatial, m_sgu_b_spatial, m_attn_out_norm, m_sgu_out_norm, m_w_out, m_post_mix_norm, m_pre_ffn_norm, m_w_gate, m_w_up, m_w_down, m_post_ffn_norm, v_pre_mix_norm, v_w_in, v_sgu_ln_gain, v_sgu_ln_bias, v_sgu_w_spatial, v_sgu_b_spatial, v_attn_out_norm, v_sgu_out_norm, v_w_out, v_post_mix_norm, v_pre_ffn_norm, v_w_gate, v_w_up, v_w_down, v_post_ffn_norm):
    a = dict(locals())
    cx, cy, cc = _coords()
    s_me = 2 * cx + cy
    core = jnp.stack([cc]).astype(jnp.int32)
    shard_core = jnp.stack([s_me, cc]).astype(jnp.int32)
    me = jnp.stack([4 * cx + 2 * cy + cc]).astype(jnp.int32)
    small = {n: (a[n][0] if a[n].ndim > 2 else a[n]) for n in SMALL}
    b_t = small["sgu_b_spatial"].T
    xs, pos, target = x[0], positions.reshape(SEQ, 1), loss_target[0]
    flipped = ("w_gate", "w_up")

    def big(name, n):
        return jnp.swapaxes(a[name], 1, 2)[0] if n in flipped else a[name][0]

    own = {n: _halves(big(n, n).astype(BF16)) for n in BIG}

    def with_own(full, n):
        full = lax.dynamic_update_slice(full, own[n][None], (s_me, 0, 0, 0))
        return full.reshape((N_SHARD,) + big(n, n).shape)

    ffn = ("w_gate", "w_up", "w_down")
    g_in, g_out, g_ffn = _gather_ici([own["w_in"]]), _gather_ici([own["w_out"]]), _gather_ici([own[n] for n in ffn])
    (s_in, s_out, s_ffn), token = _split_starts("gather_start", [g_in, g_out, g_ffn], small["pre_mix_norm"])
    in_lands = _split_wait("gather_in_wait", g_in, *s_in, token)
    ((in_lands,),) = _comm_only("comm_pass_in", [_gather_pass(in_lands)])
    w_in_f = with_own(in_lands, "w_in")
    h, u, vs, *qkv = _inproj_fwd(xs, pos, small["pre_mix_norm"], w_in_f)
    views = [tuple(qkv[3 * i:3 * i + 3]) for i in range(len(DILATIONS))]
    o_list, l_list = [], []
    for dil, (qv, kv, vv) in zip(DILATIONS, views):
        o, l = _attn_fwd(qv, kv, vv, dil)
        o_list.append(o)
        l_list.append(l)
    out_lands = _split_wait("gather_out_wait", g_out, *s_out, l_list[-1])
    sgu, ((out_lands,),) = _sgu_fwd(u, vs, small["sgu_ln_gain"], small["sgu_ln_bias"], small["sgu_w_spatial"], b_t,
                                    comms=[_gather_pass(out_lands)])
    w_out_f = with_own(out_lands, "w_out")
    ffn_lands = _split_wait("gather_ffn_wait", g_ffn, *s_ffn, sgu)
    (attn, mixed, y, x1, *lses), (ffn_lands,) = _mix_out_fwd(
        o_list, l_list, sgu, xs, w_out_f, small["attn_out_norm"], small["sgu_out_norm"], small["post_mix_norm"],
        comms=[_gather_pass(ffn_lands)])
    w_gate_f, w_up_f, w_down_f = (with_own(f, n) for f, n in zip(ffn_lands, ffn))
    h2, act, dg, dup, df, dx1, loss_cols, d_pre_ffn, d_post_ffn = _ffn_fwd_bwd(
        x1, target, w_gate_f, w_up_f, w_down_f, small["pre_ffn_norm"], small["post_ffn_norm"])

    full_tok = pl.BlockSpec((SEQ, D_MODEL), lambda s: (0, 0), pipeline_mode=pl.Buffered(1))
    ff_tok = pl.BlockSpec((1, SEQ, FF_S), lambda s: (s, 0, 0))
    gw = {}
    gw["w_gate"], gw["w_up"] = _wgrad_pair(dg, dup, h2, ff_tok, full_tok, (FF_S, D_MODEL), "wgrad_gate_up")
    gw["w_down"], ((sib_gate, sib_up),) = _wgrad(act, df, ff_tok, full_tok, (FF_S, D_MODEL), "wgrad_down",
                                                 comms=[_rs_sibling([gw["w_gate"], gw["w_up"]])])
    (dy, dsgu, d_post_mix, d_attn_norm, d_sgu_norm, *dviews), ((sib_down,),) = _outproj_bwd(
        dx1, y, attn, sgu, w_out_f, small["post_mix_norm"], small["attn_out_norm"],
        small["sgu_out_norm"], comms=[_rs_sibling([gw["w_down"]])])
    sib = {"w_gate": sib_gate, "w_up": sib_up, "w_down": sib_down}
    part = dict(zip(ffn, _chip_sums([gw[n] for n in ffn], [sib[n] for n in ffn], core)))
    gw["w_out"] = _wgrad(mixed, dy, pl.BlockSpec((SEQ, OUT_S), lambda s: (0, s)), full_tok, (OUT_S, D_MODEL),
                         "wgrad_out")
    x_ffn = _rs_chips([part[n] for n in ffn])
    (s_ffn,), token = _split_starts("rs_ffn_start", [x_ffn], small["pre_mix_norm"])
    (du, dvs_sgu, d_w_sp, d_b_sp, d_ln_gain, d_ln_bias), ((sib["w_out"],),) = _sgu_bwd(
        u, vs, dsgu, small["sgu_ln_gain"], small["sgu_ln_bias"], small["sgu_w_spatial"], b_t,
        comms=[_rs_sibling([gw["w_out"]])], after=[token])
    part["w_out"] = _rs_chip_sum(gw["w_out"], sib["w_out"], core)
    packed_early = _pack({
        "sgu_ln_gain": d_ln_gain, "sgu_ln_bias": d_ln_bias, "sgu_w_spatial": d_w_sp, "sgu_b_spatial": d_b_sp,
        "attn_out_norm": d_attn_norm, "sgu_out_norm": d_sgu_norm, "post_mix_norm": d_post_mix,
        "pre_ffn_norm": d_pre_ffn, "post_ffn_norm": d_post_ffn, LOSS_ROW: loss_cols}, SMALL_EARLY, SMALL_ROWS)
    x_out, x_small = _rs_chips([part["w_out"]]), _small_exchange(packed_early)
    (s_out, s_small), token = _split_starts("rs_out_small_start", [x_out, x_small], token)

    dqs, dks, dvs = [], [], []
    for i, (dil, (qv, kv, vv)) in enumerate(zip(DILATIONS, views)):
        dq, dk, dv = _attn_bwd(qv, kv, vv, dviews[i], dviews[3 + i], lses[i], dil, after=[token])
        dqs.append(dq)
        dks.append(dk)
        dvs.append(dv)
    half, far, joined = {}, {}, {}
    far.update(zip(ffn, _split_wait("rs_ffn_wait", x_ffn, *s_ffn, dvs[-1])))
    half.update(zip(ffn, _final_sums([gw[n] for n in ffn], [sib[n] for n in ffn], [far[n] for n in ffn],
                                     shard_core)))
    dproj, grad_x, d_pre_mix = _inproj_bwd(dqs, dks, dvs, du, dvs_sgu, pos, xs, dx1, w_in_f, small["pre_mix_norm"])
    (far["w_out"],) = _split_wait("rs_out_wait", x_out, *s_out, grad_x)
    (slots_early,) = _split_wait("small_early_wait", x_small, *s_small, far["w_out"])
    half["w_out"] = _rs_final_sum(gw["w_out"], sib["w_out"], far["w_out"], shard_core)
    packed_late = _pack({"pre_mix_norm": d_pre_mix}, SMALL_LATE, 8)
    names = ffn + ("w_out",)
    gw["w_in"], (got, (slots_late,)) = _wgrad(
        h, dproj, full_tok, pl.BlockSpec((SEQ, IN_S), lambda s: (0, s)), (D_MODEL, IN_S), "wgrad_in",
        comms=[_rs_join([half[n] for n in names]), _small_exchange(packed_late)])
    joined.update(zip(names, got))
    ((sib["w_in"],),) = _comm_only("comm_rs_sibling_in", [_rs_sibling([gw["w_in"]])])
    part["w_in"] = _rs_chip_sum(gw["w_in"], sib["w_in"], core)
    x_in = _rs_chips([part["w_in"]])
    (s_in,), token = _split_starts("rs_in_start", [x_in], small["pre_mix_norm"])

    grads, deltas, new_m, new_v = {}, {}, {}, {}

    def record(n, outs):
        grads[n], deltas[n], new_m[n], new_v[n] = (
            jnp.swapaxes(o[None], 1, 2) if n in flipped else o[None] for o in outs)

    def update(names, block_rows, name, after):
        for n, outs in zip(names, _adamw_multi(
                [big(n, n) for n in names], [joined[n].reshape(big(n, n).shape) for n in names],
                [big("m_" + n, n) for n in names], [big("v_" + n, n) for n in names], block_rows, name, after)):
            record(n, outs)

    update(ffn, FF_S // 4, "adamw_ffn", [token])
    update(("w_out",), BIG_ADAM_ROWS["w_out"], "adamw_w_out", [token])
    (far["w_in"],) = _split_wait("rs_in_wait", x_in, *s_in, [new_v[n] for n in ("w_down", "w_out")])
    half["w_in"] = _rs_final_sum(gw["w_in"], sib["w_in"], far["w_in"], shard_core)
    ((joined["w_in"],),) = _comm_only("comm_rs_join_in", [_rs_join([half["w_in"]])])
    update(("w_in",), BIG_ADAM_ROWS["w_in"], "adamw_w_in", [])
    a[LOSS_ROW] = a["m_" + LOSS_ROW] = a["v_" + LOSS_ROW] = jnp.zeros((1, D_MODEL), F32)
    for names, rows, packed, slots in ((SMALL_EARLY, SMALL_ROWS, packed_early, slots_early),
                                       (SMALL_LATE, 8, packed_late, slots_late)):
        outs = _adamw_small(packed, slots, _pack(a, names, rows), _pack({n: a["m_" + n] for n in names}, names, rows),
                            _pack({n: a["v_" + n] for n in names}, names, rows), me)
        for dst, buf in zip((grads, deltas, new_m, new_v), outs):
            dst.update(_unpack(buf, names, {n: a[n].shape for n in names}))
    loss = jnp.sum(grads[LOSS_ROW]) * np.float32(0.5 / D_MODEL)
    return (loss, grad_x[None], *[grads[n] for n in WEIGHTS], *[deltas[n] for n in WEIGHTS],
            *[new_m[n] for n in WEIGHTS], *[new_v[n] for n in WEIGHTS])
```

```python
import numpy as np
import jax
import jax.numpy as jnp
from jax import lax
from jax.experimental import pallas as pl
from jax.experimental.pallas import tpu as pltpu

F32 = jnp.float32
BF16 = jnp.bfloat16

SEQ = 2048
D_MODEL = 1024
HEAD_DIM = 64
ATTN_W = 512
SGU_W = 512
SGU_GROUPS = 8
CHUNK = 128
DILATIONS = (1, 4, 16)
N_SHARD = 4
IN_S = 640
OUT_S = 256
FF_S = 704
PROJ_W = N_SHARD * IN_S
RMS_EPS = 1e-6
LN_EPS = 1e-5
ROPE_THETA = 500000.0
ATTN_SCALE = 1.0 / np.sqrt(HEAD_DIM)
NEG = -1e30
TM = 512
TM_FFN = 256
VMEM_LIMIT = 56 * 1024 * 1024
SMALL_ROWS = 136

ADAM_LR = 0.001
ADAM_B1 = 0.9
ADAM_B2 = 0.999
ADAM_EPS = 1e-08
ADAM_WD = 0.01
ADAM_STEP = 10

MESH = pl.DeviceIdType.MESH
ANY = pl.BlockSpec(memory_space=pl.ANY)


def _dot(a, b):
    return jnp.dot(a, b, preferred_element_type=F32)


def _dot_nt(a, b):
    return lax.dot_general(a, b, (((1,), (1,)), ((), ())), preferred_element_type=F32)


def _dot_tn(a, b):
    return lax.dot_general(a, b, (((0,), (0,)), ((), ())), preferred_element_type=F32)


def _dot_exact(a, b):
    return jnp.dot(a, b, preferred_element_type=F32, precision=lax.Precision.HIGHEST)


def _rms_stats(x):
    r = lax.rsqrt(jnp.mean(x * x, axis=-1, keepdims=True) + RMS_EPS)
    return x * r, r


def _rms_bwd(xh, r, gain, dy):
    dxh = dy * gain
    dx = r * (dxh - xh * jnp.mean(dxh * xh, axis=-1, keepdims=True))
    return dx, jnp.sum(dy * xh, axis=0, keepdims=True)


_ERF_ALPHA = (-2.72614225801306e-10, 2.77068142495902e-08, -2.10102402082508e-06, -5.69250639462346e-05,
              -7.34990630326855e-04, -2.95459980854025e-03, -1.60960333262415e-02)
_ERF_BETA = (-1.45660718464996e-05, -2.13374055278905e-04, -1.68282697438203e-03, -7.37332916720468e-03,
             -1.42647390514189e-02)


def _erf(x):
    x = jnp.clip(x, -4.0, 4.0)
    x2 = x * x
    p = jnp.full_like(x, _ERF_ALPHA[0])
    for a in _ERF_ALPHA[1:]:
        p = p * x2 + a
    q = jnp.full_like(x, _ERF_BETA[0])
    for b in _ERF_BETA[1:]:
        q = q * x2 + b
    return x * p / q


def _gelu(x):
    return 0.5 * x * (1.0 + _erf(x * np.float32(1.0 / np.sqrt(2.0))))


def _gelu_grad(x):
    cdf = 0.5 * (1.0 + _erf(x * np.float32(1.0 / np.sqrt(2.0))))
    pdf = jnp.exp(-0.5 * x * x) * np.float32(1.0 / np.sqrt(2.0 * np.pi))
    return cdf + x * pdf


def _sigmoid(x):
    return 1.0 / (1.0 + jnp.exp(-x))


_INV_FREQ = tuple(float(np.float32(ROPE_THETA ** (-2.0 * j / 16.0))) for j in range(8))


def _rot_tables(pos):
    lane = lax.broadcasted_iota(jnp.int32, (1, 128), 1)
    d = lane & 63
    j = d & 7
    inv = jnp.zeros((1, 128), F32)
    for jj in range(8):
        inv = jnp.where(j == jj, _INV_FREQ[jj], inv)
    ang = pos.astype(F32) * inv
    c = jnp.cos(ang)
    s = jnp.sin(ang)
    cos_t = jnp.where(d < 16, c, 1.0)
    sin_a = jnp.where(d < 8, -s, 0.0)
    sin_b = jnp.where((d >= 8) & (d < 16), s, 0.0)
    return tuple(jnp.tile(t, (1, 4)) for t in (cos_t, sin_a, sin_b))


def _rope(x, tabs):
    cos_t, sin_a, sin_b = tabs
    return x * cos_t + pltpu.roll(x, 504, 1) * sin_a + pltpu.roll(x, 8, 1) * sin_b


def _rope_bwd(dy, tabs):
    cos_t, sin_a, sin_b = tabs
    return dy * cos_t + pltpu.roll(dy * sin_a, 8, 1) + pltpu.roll(dy * sin_b, 504, 1)


def _left_half():
    return lax.broadcasted_iota(jnp.int32, (CHUNK, CHUNK), 1) < HEAD_DIM


def _group_ones():
    lane = lax.broadcasted_iota(jnp.int32, (SGU_GROUPS, SGU_W), 1)
    row = lax.broadcasted_iota(jnp.int32, (SGU_GROUPS, SGU_W), 0)
    return ((lane >> 6) == row).astype(F32)


def _masked_spatial(w_ref):
    row = lax.broadcasted_iota(jnp.int32, (CHUNK, CHUNK), 0)
    col = lax.broadcasted_iota(jnp.int32, (CHUNK, CHUNK), 1)
    return [jnp.where(col <= row, w_ref[g], 0.0).astype(BF16) for g in range(SGU_GROUPS)]


def _sgu_core(u, vs, lg, lb, wm, bias_full):
    tm = u.shape[0]
    gu = _gelu(u)
    gv = _gelu(vs)
    mu = jnp.mean(gv, axis=-1, keepdims=True)
    xc = gv - mu
    rstd = lax.rsqrt(jnp.mean(xc * xc, axis=-1, keepdims=True) + LN_EPS)
    xh = xc * rstd
    vnb = (xh * lg + lb).astype(BF16)
    left = _left_half()
    rows = []
    for c in range(tm // CHUNK):
        pieces = []
        for p in range(4):
            vp = vnb[c * CHUNK:(c + 1) * CHUNK, p * 128:(p + 1) * 128]
            pieces.append(jnp.where(left, _dot(wm[2 * p], vp), _dot(wm[2 * p + 1], vp)))
        rows.append(jnp.concatenate(pieces, axis=1) + bias_full)
    mixed = jnp.concatenate(rows, axis=0)
    return gu, xh, rstd, vnb, mixed


def _resident(shape):
    n = len(shape)
    return pl.BlockSpec(shape, lambda *_: (0,) * n, pipeline_mode=pl.Buffered(1))


def _rows(ncol, tm=TM):
    return pl.BlockSpec((tm, ncol), lambda i: (i, 0))


def _rows3(nlead, ncol, tm=TM):
    return pl.BlockSpec((nlead, tm, ncol), lambda i: (0, i, 0))


def _acc(ncol, nrow=1):
    return pl.BlockSpec((nrow, ncol), lambda i: (0, 0))


HEAD_W = 128


def _view_rows(dil, width=ATTN_W, tm=TM):
    return pl.BlockSpec((tm // dil, dil * width), lambda i: (i, 0))


def _view_shape(dil, dtype, width=ATTN_W):
    return _sds((SEQ // dil, dil * width), dtype)


def _slab_scratch():
    return pltpu.VMEM((4, TM, 128), F32)


def _store_view(val, out_ref, slabs, dil):
    width = val.shape[1]
    for j in range(width // 128):
        slabs[j] = val[:, j * 128:(j + 1) * 128]
    for r in range(dil):
        for j in range(width // 128):
            c0 = r * width + j * 128
            out_ref[:, c0:c0 + 128] = slabs.at[j][pl.ds(r, TM // dil, stride=dil), :].astype(out_ref.dtype)


def _load_view(in_ref, slabs, dil, width=ATTN_W):
    for r in range(dil):
        for j in range(width // 128):
            c0 = r * width + j * 128
            slabs.at[j][pl.ds(r, TM // dil, stride=dil), :] = in_ref[:, c0:c0 + 128].astype(F32)
    return jnp.concatenate([slabs[j] for j in range(width // 128)], axis=1)


def _head_spread():
    m = lax.broadcasted_iota(jnp.int32, (HEAD_W, ATTN_W), 0)
    lane = lax.broadcasted_iota(jnp.int32, (HEAD_W, ATTN_W), 1)
    return (m == 16 * (lane >> 6)).astype(F32)


def _head_sum():
    lane = lax.broadcasted_iota(jnp.int32, (ATTN_W, HEAD_W), 0)
    m = lax.broadcasted_iota(jnp.int32, (ATTN_W, HEAD_W), 1)
    return ((lane >> 6) == (m >> 4)).astype(F32)


def _seq_params():
    return pltpu.CompilerParams(dimension_semantics=("arbitrary",), vmem_limit_bytes=VMEM_LIMIT)


def _sds(shape, dtype):
    return jax.ShapeDtypeStruct(shape, dtype)


class _Comm:
    def __init__(self, args, out_shape, n_sems, start, finish, aliased=False):
        self.args, self.out_shape, self.n_sems = list(args), list(out_shape), n_sems
        self.start, self.finish, self.aliased = start, finish, aliased


def _pcall(body, *, name, grid, in_specs, out_specs, out_shape, args, scratch_shapes=(), comms=(), after=()):
    single = not isinstance(out_shape, (list, tuple))
    out_specs = [out_specs] if single else list(out_specs)
    out_shape = [out_shape] if single else list(out_shape)
    n_in, n_out, n_scr = len(in_specs), len(out_shape), len(scratch_shapes)
    c_args = [a for c in comms for a in c.args]
    c_outs = [o for c in comms for o in c.out_shape]
    aliases, ai, ao = {}, n_in, n_out
    for c in comms:
        if c.aliased:
            aliases.update({ai + k: ao + k for k in range(len(c.args))})
        ai += len(c.args)
        ao += len(c.out_shape)
    sems = [pltpu.SemaphoreType.DMA((c.n_sems,)) for c in comms for _ in range(2)]
    steps = grid[0]

    def wrapped(*refs):
        o0 = n_in + len(c_args) + len(after)
        s0 = o0 + n_out + len(c_outs)
        m_in, m_out, m_sem = refs[n_in:n_in + len(c_args)], refs[o0 + n_out:s0], refs[s0 + n_scr:]

        def each(phase):
            ii = oi = 0
            for k, c in enumerate(comms):
                getattr(c, phase)(m_in[ii:ii + len(c.args)], m_out[oi:oi + len(c.out_shape)],
                                  m_sem[2 * k], m_sem[2 * k + 1])
                ii += len(c.args)
                oi += len(c.out_shape)

        if comms:
            @pl.when(pl.program_id(0) == 0)
            def _():
                each("start")

        body(*refs[:n_in], *refs[o0:o0 + n_out], *refs[s0:s0 + n_scr])

        if comms:
            @pl.when(pl.program_id(0) == steps - 1)
            def _():
                each("finish")

    res = pl.pallas_call(
        wrapped, name=name, grid=grid,
        in_specs=list(in_specs) + [ANY] * (len(c_args) + len(after)), out_specs=out_specs + [ANY] * len(c_outs),
        out_shape=out_shape + c_outs, scratch_shapes=list(scratch_shapes) + sems,
        input_output_aliases=aliases, compiler_params=_seq_params(),
    )(*args, *c_args, *after)
    mine = res[0] if single else list(res[:n_out])
    if not comms:
        return mine
    theirs, oi = [], n_out
    for c in comms:
        theirs.append(list(res[oi:oi + len(c.out_shape)]))
        oi += len(c.out_shape)
    return mine, theirs


def _inproj_fwd(x, pos, g_pre, w_in, comms=()):
    def body(x_ref, pos_ref, g_ref, w_ref, h_ref, u_ref, vs_ref, *rest):
        qkv_refs, slabs = rest[:9], rest[9:]
        xh, _ = _rms_stats(x_ref[...])
        h = (xh * g_ref[...]).astype(BF16)
        h_ref[...] = h
        proj = jnp.concatenate([_dot(h, w_ref[s]) for s in range(N_SHARD)], axis=1)
        tabs = _rot_tables(pos_ref[...])
        u_ref[...] = proj[:, 1536:2048]
        vs_ref[...] = proj[:, 2048:2560]
        qkv = (_rope(proj[:, 0:512], tabs) * np.float32(ATTN_SCALE), _rope(proj[:, 512:1024], tabs),
               proj[:, 1024:1536])
        for t, val in enumerate(qkv):
            qkv_refs[t][...] = val.astype(BF16)
            for i, dil in enumerate(DILATIONS[1:]):
                _store_view(val, qkv_refs[3 * (i + 1) + t], slabs[t], dil)

    return _pcall(
        body, name="inproj_fwd", grid=(SEQ // TM,),
        in_specs=[_rows(D_MODEL), _rows(1), _resident((1, D_MODEL)), _resident((N_SHARD, D_MODEL, IN_S))],
        out_specs=[_rows(D_MODEL), _rows(512), _rows(512)] + [_view_rows(dil) for dil in DILATIONS for _ in range(3)],
        out_shape=[_sds((SEQ, D_MODEL), BF16), _sds((SEQ, 512), F32), _sds((SEQ, 512), F32)]
        + [_view_shape(dil, BF16) for dil in DILATIONS for _ in range(3)],
        scratch_shapes=[_slab_scratch() for _ in range(3)],
        args=(x, pos, g_pre, w_in), comms=comms)


def _sgu_fwd(u, vs, lg, lb, w_sp, b_t, comms=()):
    def body(u_ref, vs_ref, lg_ref, lb_ref, w_ref, bt_ref, out_ref):
        wm = _masked_spatial(w_ref)
        bias_full = _dot_exact(bt_ref[...], _group_ones())
        gu, _, _, _, mixed = _sgu_core(u_ref[...], vs_ref[...], lg_ref[...], lb_ref[...], wm, bias_full)
        out_ref[...] = gu * mixed

    return _pcall(
        body, name="sgu_fwd", grid=(SEQ // TM,),
        in_specs=[_rows(SGU_W), _rows(SGU_W), _resident((1, SGU_W)), _resident((1, SGU_W)),
                  _resident((SGU_GROUPS, CHUNK, CHUNK)), _resident((CHUNK, SGU_GROUPS))],
        out_specs=_rows(SGU_W),
        out_shape=_sds((SEQ, SGU_W), F32),
        args=(u, vs, lg, lb, w_sp, b_t), comms=comms)


def _block_masks():
    row = lax.broadcasted_iota(jnp.int32, (CHUNK, CHUNK), 0)
    col = lax.broadcasted_iota(jnp.int32, (CHUNK, CHUNK), 1)
    return col <= row, col >= row


def _attn_fwd(qv, kv, vv, dil, comms=()):
    seg = SEQ // dil
    nblk = seg // CHUNK
    rps = 4 if nblk == 1 else 1

    def body(q_ref, k_ref, v_ref, o_ref, l_ref):
        left = _left_half()
        m_cur, m_prev = _block_masks()
        zero = jnp.zeros((CHUNK, CHUNK), BF16)
        ones = (jnp.where(left, 1.0, 0.0).astype(BF16), jnp.where(left, 0.0, 1.0).astype(BF16))

        def blk(rr, b, carry):
            r0 = pl.multiple_of(b * CHUNK, CHUNK)
            rp = pl.multiple_of(jnp.maximum(b - 1, 0) * CHUNK, CHUNK)
            prev_ok = m_prev & (b > 0)
            sides = tuple(enumerate((left, ~left)))
            tiles, scores = [], []
            for hp in range(4):
                ls = slice(rr * ATTN_W + hp * 128, rr * ATTN_W + (hp + 1) * 128)
                qp = q_ref[pl.ds(r0, CHUNK), ls]
                kc = k_ref[pl.ds(r0, CHUNK), ls]
                kp = k_ref[pl.ds(rp, CHUNK), ls] if nblk > 1 else None
                tiles.append((ls, v_ref[pl.ds(r0, CHUNK), ls], v_ref[pl.ds(rp, CHUNK), ls] if nblk > 1 else None))
                for _, hm in sides:
                    qh = jnp.where(hm, qp, zero)
                    sc = jnp.where(m_cur, _dot_nt(qh, kc), NEG)
                    sp = jnp.where(prev_ok, _dot_nt(qh, kp), NEG) if nblk > 1 else None
                    scores.append((sc, sp))
            probs = []
            for sc, sp in scores:
                if nblk > 1:
                    m = jnp.max(jnp.maximum(sc, sp), axis=-1, keepdims=True)
                    pc = jnp.exp(sc - m)
                    pp = jnp.exp(sp - m)
                    probs.append((m, pc.astype(BF16), pp.astype(BF16), (pc + pp).astype(BF16)))
                else:
                    m = jnp.max(sc, axis=-1, keepdims=True)
                    pc = jnp.exp(sc - m).astype(BF16)
                    probs.append((m, pc, None, pc))
            for hp, (ls, vc, vp) in enumerate(tiles):
                acc = jnp.zeros((CHUNK, CHUNK), F32)
                den = jnp.zeros((CHUNK, CHUNK), F32)
                for side, hm in sides:
                    _, pc, pp, psum = probs[2 * hp + side]
                    acc = acc + _dot(pc, jnp.where(hm, vc, zero))
                    if nblk > 1:
                        acc = acc + _dot(pp, jnp.where(hm, vp, zero))
                    den = den + _dot(psum, ones[side])
                o_ref[pl.ds(r0, CHUNK), ls] = (acc / den).astype(o_ref.dtype)
                lse = jnp.where(left, probs[2 * hp][0], probs[2 * hp + 1][0]) + jnp.log(den)
                l_ref[pl.ds(r0, CHUNK), rr * HEAD_W + 32 * hp:rr * HEAD_W + 32 * hp + 32] = lse[:, 48:80]
            return carry

        for rr in range(rps):
            lax.fori_loop(0, nblk, lambda b, c, rr=rr: blk(rr, b, c), 0)

    spec = pl.BlockSpec((seg, rps * ATTN_W), lambda r: (0, r))
    return _pcall(
        body, name=f"attn_fwd_d{dil}", grid=(dil // rps,),
        in_specs=[spec, spec, spec], out_specs=[spec, pl.BlockSpec((seg, rps * HEAD_W), lambda r: (0, r))],
        out_shape=[_sds((seg, dil * ATTN_W), BF16), _sds((seg, dil * HEAD_W), F32)],
        args=(qv, kv, vv), comms=comms)


def _mix_out_fwd(o_list, l_list, sgu, x, w_out, g_attn, g_sgu, g_post, comms=()):
    def body(o1, o2, o3, l1, l2, l3, sgu_ref, x_ref, w_ref, ga_ref, gs_ref, gp_ref,
             attn_ref, mixed_ref, y_ref, x1_ref, lse1_ref, lse2_ref, lse3_ref, slabs_a, slabs_b):
        os = [o1[...], _load_view(o2, slabs_a, DILATIONS[1]), _load_view(o3, slabs_b, DILATIONS[2])]
        ls = [l1[...], _load_view(l2, slabs_a, DILATIONS[1], HEAD_W), _load_view(l3, slabs_b, DILATIONS[2], HEAD_W)]
        m = jnp.maximum(jnp.maximum(ls[0], ls[1]), ls[2])
        es = [jnp.exp(l - m) for l in ls]
        den = es[0] + es[1] + es[2]
        spread = _head_spread()
        attn = sum(_dot_exact(e / den, spread) * o for e, o in zip(es, os))
        attn_ref[...] = attn
        lse = m + jnp.log(den)
        lse1_ref[...] = lse
        _store_view(lse, lse2_ref, slabs_a, DILATIONS[1])
        _store_view(lse, lse3_ref, slabs_b, DILATIONS[2])
        ah, _ = _rms_stats(attn)
        sh, _ = _rms_stats(sgu_ref[...])
        mixed = jnp.concatenate([ah * ga_ref[...], sh * gs_ref[...]], axis=1).astype(BF16)
        mixed_ref[...] = mixed
        y = _dot(mixed[:, 0:OUT_S], w_ref[0])
        for s in range(1, N_SHARD):
            y = y + _dot(mixed[:, s * OUT_S:(s + 1) * OUT_S], w_ref[s])
        y_ref[...] = y
        yh, _ = _rms_stats(y)
        x1_ref[...] = x_ref[...] + yh * gp_ref[...]

    return _pcall(
        body, name="mix_out_fwd", grid=(SEQ // TM,),
        in_specs=[_view_rows(dil) for dil in DILATIONS] + [_view_rows(dil, HEAD_W) for dil in DILATIONS]
        + [_rows(512), _rows(D_MODEL), _resident((N_SHARD, OUT_S, D_MODEL)),
           _resident((1, 512)), _resident((1, 512)), _resident((1, D_MODEL))],
        out_specs=[_rows(512), _rows(D_MODEL), _rows(D_MODEL), _rows(D_MODEL)]
        + [_view_rows(dil, HEAD_W) for dil in DILATIONS],
        out_shape=[_sds((SEQ, 512), F32), _sds((SEQ, D_MODEL), BF16), _sds((SEQ, D_MODEL), F32),
                   _sds((SEQ, D_MODEL), F32)] + [_view_shape(dil, F32, HEAD_W) for dil in DILATIONS],
        scratch_shapes=[_slab_scratch(), _slab_scratch()],
        args=(*o_list, *l_list, sgu, x, w_out, g_attn, g_sgu, g_post), comms=comms)


def _ffn_fwd_bwd(x1, target, w_gate, w_up, w_down, g_pre, g_post, comms=()):
    def body(x1_ref, t_ref, wg_ref, wu_ref, wd_ref, gpf_ref, gpo_ref,
             h2_ref, a_ref, dg_ref, dup_ref, df_ref, dx1_ref, loss_ref, dgpf_ref, dgpo_ref, g_scr, up_scr):
        @pl.when(pl.program_id(0) == 0)
        def _():
            loss_ref[...] = jnp.zeros_like(loss_ref)
            dgpf_ref[...] = jnp.zeros_like(dgpf_ref)
            dgpo_ref[...] = jnp.zeros_like(dgpo_ref)

        x1 = x1_ref[...]
        gpf = gpf_ref[...]
        gpo = gpo_ref[...]
        xh, r = _rms_stats(x1)
        h2 = (xh * gpf).astype(BF16)
        h2_ref[...] = h2
        f = jnp.zeros((TM_FFN, D_MODEL), F32)
        for s in range(N_SHARD):
            g = _dot_nt(h2, wg_ref[s])
            up = _dot_nt(h2, wu_ref[s])
            g_scr[s] = g
            up_scr[s] = up
            a = (g * _sigmoid(g) * up).astype(BF16)
            a_ref[s] = a
            f = f + _dot(a, wd_ref[s])
        fh, rf = _rms_stats(f)
        diff = x1 + fh * gpo - t_ref[...]
        loss_ref[...] += jnp.sum(diff * diff, axis=0, keepdims=True)
        dout = diff * np.float32(1.0 / D_MODEL)
        df, dgpo = _rms_bwd(fh, rf, gpo, dout)
        dgpo_ref[...] += dgpo
        dfb = df.astype(BF16)
        df_ref[...] = dfb
        dh2 = jnp.zeros((TM_FFN, D_MODEL), F32)
        for s in range(N_SHARD):
            da = _dot_nt(dfb, wd_ref[s])
            g = g_scr[s]
            up = up_scr[s]
            sg = _sigmoid(g)
            dup = (da * (g * sg)).astype(BF16)
            dg = (da * up * (sg * (1.0 + g * (1.0 - sg)))).astype(BF16)
            dg_ref[s] = dg
            dup_ref[s] = dup
            dh2 = dh2 + _dot(dg, wg_ref[s]) + _dot(dup, wu_ref[s])
        dx, dgpf = _rms_bwd(xh, r, gpf, dh2)
        dgpf_ref[...] += dgpf
        dx1_ref[...] = dout + dx

    return _pcall(
        body, name="ffn_fwd_bwd", grid=(SEQ // TM_FFN,),
        in_specs=[_rows(D_MODEL, TM_FFN), _rows(D_MODEL, TM_FFN), _resident((N_SHARD, FF_S, D_MODEL)),
                  _resident((N_SHARD, FF_S, D_MODEL)), _resident((N_SHARD, FF_S, D_MODEL)),
                  _resident((1, D_MODEL)), _resident((1, D_MODEL))],
        out_specs=[_rows(D_MODEL, TM_FFN), _rows3(N_SHARD, FF_S, TM_FFN), _rows3(N_SHARD, FF_S, TM_FFN),
                   _rows3(N_SHARD, FF_S, TM_FFN), _rows(D_MODEL, TM_FFN), _rows(D_MODEL, TM_FFN),
                   _acc(D_MODEL), _acc(D_MODEL), _acc(D_MODEL)],
        out_shape=[_sds((SEQ, D_MODEL), BF16), _sds((N_SHARD, SEQ, FF_S), BF16), _sds((N_SHARD, SEQ, FF_S), BF16),
                   _sds((N_SHARD, SEQ, FF_S), BF16), _sds((SEQ, D_MODEL), BF16), _sds((SEQ, D_MODEL), F32),
                   _sds((1, D_MODEL), F32), _sds((1, D_MODEL), F32), _sds((1, D_MODEL), F32)],
        scratch_shapes=[pltpu.VMEM((N_SHARD, TM_FFN, FF_S), F32), pltpu.VMEM((N_SHARD, TM_FFN, FF_S), F32)],
        args=(x1, target, w_gate, w_up, w_down, g_pre, g_post), comms=comms)


def _wgrad(a, b, a_spec, b_spec, out_block, name, comms=()):
    def body(a_ref, b_ref, o_ref):
        av = a_ref[0] if len(a_ref.shape) == 3 else a_ref[...]
        bv = b_ref[0] if len(b_ref.shape) == 3 else b_ref[...]
        o_ref[0] = _dot_tn(av, bv)

    return _pcall(
        body, name=name, grid=(N_SHARD,),
        in_specs=[a_spec, b_spec],
        out_specs=pl.BlockSpec((1,) + out_block, lambda s: (s, 0, 0)),
        out_shape=_sds((N_SHARD,) + out_block, F32),
        args=(a, b), comms=comms)


def _outproj_bwd(dx1, y, attn, sgu, w_out, g_post, g_attn, g_sgu, comms=()):
    def body(dx1_ref, y_ref, attn_ref, sgu_ref, w_ref, gp_ref, ga_ref, gs_ref,
             dy_ref, dsgu_ref, dgp_ref, dga_ref, dgs_ref, *rest):
        dattn_refs, delta_refs, (slabs_a, slabs_b) = rest[0:3], rest[3:6], rest[6:]

        @pl.when(pl.program_id(0) == 0)
        def _():
            dgp_ref[...] = jnp.zeros_like(dgp_ref)
            dga_ref[...] = jnp.zeros_like(dga_ref)
            dgs_ref[...] = jnp.zeros_like(dgs_ref)

        yh, ry = _rms_stats(y_ref[...])
        dy, dgp = _rms_bwd(yh, ry, gp_ref[...], dx1_ref[...])
        dgp_ref[...] += dgp
        dyb = dy.astype(BF16)
        dy_ref[...] = dyb
        dmixed = jnp.concatenate([_dot_nt(dyb, w_ref[s]) for s in range(N_SHARD)], axis=1)
        attn = attn_ref[...]
        ah, ra = _rms_stats(attn)
        dattn, dga = _rms_bwd(ah, ra, ga_ref[...], dmixed[:, 0:512])
        dga_ref[...] += dga
        sh, rs = _rms_stats(sgu_ref[...])
        dsgu, dgs = _rms_bwd(sh, rs, gs_ref[...], dmixed[:, 512:1024])
        dgs_ref[...] += dgs
        dsgu_ref[...] = dsgu
        delta = _dot_exact(dattn * attn, _head_sum())
        dattn_refs[0][...] = dattn.astype(BF16)
        delta_refs[0][...] = delta
        for i, dil in enumerate(DILATIONS[1:]):
            _store_view(dattn, dattn_refs[i + 1], slabs_a, dil)
            _store_view(delta, delta_refs[i + 1], slabs_b, dil)

    return _pcall(
        body, name="outproj_bwd", grid=(SEQ // TM,),
        in_specs=[_rows(D_MODEL), _rows(D_MODEL), _rows(512), _rows(512), _resident((N_SHARD, OUT_S, D_MODEL)),
                  _resident((1, D_MODEL)), _resident((1, 512)), _resident((1, 512))],
        out_specs=[_rows(D_MODEL), _rows(512), _acc(D_MODEL), _acc(512), _acc(512)]
        + [_view_rows(dil) for dil in DILATIONS] + [_view_rows(dil, HEAD_W) for dil in DILATIONS],
        out_shape=[_sds((SEQ, D_MODEL), BF16), _sds((SEQ, 512), F32),
                   _sds((1, D_MODEL), F32), _sds((1, 512), F32), _sds((1, 512), F32)]
        + [_view_shape(dil, BF16) for dil in DILATIONS] + [_view_shape(dil, F32, HEAD_W) for dil in DILATIONS],
        scratch_shapes=[_slab_scratch(), _slab_scratch()],
        args=(dx1, y, attn, sgu, w_out, g_post, g_attn, g_sgu), comms=comms)


def _sgu_bwd(u, vs, dsgu, lg, lb, w_sp, b_t, comms=(), after=()):
    nsteps = SEQ // TM

    def body(u_ref, vs_ref, ds_ref, lg_ref, lb_ref, w_ref, bt_ref,
             du_ref, dvs_ref, dw_ref, db_ref, dlg_ref, dlb_ref, dbias_scr):
        i = pl.program_id(0)

        @pl.when(i == 0)
        def _():
            dw_ref[...] = jnp.zeros_like(dw_ref)
            dlg_ref[...] = jnp.zeros_like(dlg_ref)
            dlb_ref[...] = jnp.zeros_like(dlb_ref)
            dbias_scr[...] = jnp.zeros_like(dbias_scr)

        wm = _masked_spatial(w_ref)
        ones_g = _group_ones()
        bias_full = _dot_exact(bt_ref[...], ones_g)
        u = u_ref[...]
        vs = vs_ref[...]
        lg = lg_ref[...]
        gu, xh, rstd, vnb, mixed = _sgu_core(u, vs, lg, lb_ref[...], wm, bias_full)
        dsgu = ds_ref[...]
        du_ref[...] = (dsgu * mixed * _gelu_grad(u)).astype(BF16)
        dmixed = dsgu * gu
        left = _left_half()
        dvn_rows = []
        for c in range(TM // CHUNK):
            rs = slice(c * CHUNK, (c + 1) * CHUNK)
            dm_c = dmixed[rs, :]
            dbias_scr[...] += dm_c
            pieces = []
            for p in range(4):
                ls = slice(p * 128, (p + 1) * 128)
                dmp = dm_c[:, ls]
                vp = vnb[rs, ls]
                dmb = dmp.astype(BF16)
                zero = jnp.zeros_like(dmb)
                dw_ref[2 * p] += _dot_nt(jnp.where(left, dmb, zero), vp)
                dw_ref[2 * p + 1] += _dot_nt(jnp.where(left, zero, dmb), vp)
                pieces.append(jnp.where(left, _dot_tn(wm[2 * p], dmb), _dot_tn(wm[2 * p + 1], dmb)))
            dvn_rows.append(jnp.concatenate(pieces, axis=1))
        dvn = jnp.concatenate(dvn_rows, axis=0)
        dlg_ref[...] += jnp.sum(dvn * xh, axis=0, keepdims=True)
        dlb_ref[...] += jnp.sum(dvn, axis=0, keepdims=True)
        dxh = dvn * lg
        dgv = rstd * (dxh - jnp.mean(dxh, axis=-1, keepdims=True) - xh * jnp.mean(dxh * xh, axis=-1, keepdims=True))
        dvs_ref[...] = (dgv * _gelu_grad(vs)).astype(BF16)

        @pl.when(i == nsteps - 1)
        def _():
            row = lax.broadcasted_iota(jnp.int32, (CHUNK, CHUNK), 0)
            col = lax.broadcasted_iota(jnp.int32, (CHUNK, CHUNK), 1)
            for g in range(SGU_GROUPS):
                dw_ref[g] = jnp.where(col <= row, dw_ref[g], 0.0)
            db_ref[...] = lax.dot_general(ones_g, dbias_scr[...], (((1,), (1,)), ((), ())),
                                          preferred_element_type=F32, precision=lax.Precision.HIGHEST)

    return _pcall(
        body, name="sgu_bwd", grid=(nsteps,),
        in_specs=[_rows(SGU_W), _rows(SGU_W), _rows(SGU_W), _resident((1, SGU_W)), _resident((1, SGU_W)),
                  _resident((SGU_GROUPS, CHUNK, CHUNK)), _resident((CHUNK, SGU_GROUPS))],
        out_specs=[_rows(SGU_W), _rows(SGU_W), pl.BlockSpec((SGU_GROUPS, CHUNK, CHUNK), lambda i: (0, 0, 0)),
                   _acc(CHUNK, SGU_GROUPS), _acc(SGU_W), _acc(SGU_W)],
        out_shape=[_sds((SEQ, SGU_W), BF16), _sds((SEQ, SGU_W), BF16), _sds((SGU_GROUPS, CHUNK, CHUNK), F32),
                   _sds((SGU_GROUPS, CHUNK), F32), _sds((1, SGU_W), F32), _sds((1, SGU_W), F32)],
        scratch_shapes=[pltpu.VMEM((CHUNK, SGU_W), F32)],
        args=(u, vs, dsgu, lg, lb, w_sp, b_t), comms=comms, after=after)


def _attn_bwd(qv, kv, vv, dov, deltav, lsev, dil, comms=(), after=()):
    seg = SEQ // dil
    nblk = seg // CHUNK
    rps = 4 if nblk == 1 else 1

    def body(q_ref, k_ref, v_ref, do_ref, dl_ref, lse_ref, dq_ref, dk_ref, dv_ref, dk_wait, dv_wait):
        left = _left_half()
        m_cur, m_prev = _block_masks()

        def blk(rr, b, carry):
            r0 = pl.multiple_of(b * CHUNK, CHUNK)
            rp = pl.multiple_of(jnp.maximum(b - 1, 0) * CHUNK, CHUNK)
            prev_ok = m_prev & (b > 0)
            sides = tuple(enumerate((left, ~left)))
            zero = jnp.zeros((CHUNK, CHUNK), BF16)
            tiles, firsts = [], []
            for hp in range(4):
                ls = slice(rr * ATTN_W + hp * 128, rr * ATTN_W + (hp + 1) * 128)
                qp = q_ref[pl.ds(r0, CHUNK), ls]
                kc = k_ref[pl.ds(r0, CHUNK), ls]
                vc = v_ref[pl.ds(r0, CHUNK), ls]
                dop = do_ref[pl.ds(r0, CHUNK), ls]
                kp = k_ref[pl.ds(rp, CHUNK), ls] if nblk > 1 else None
                vp = v_ref[pl.ds(rp, CHUNK), ls] if nblk > 1 else None
                tiles.append((ls, kc, kp))
                for side, hm in sides:
                    qh = jnp.where(hm, qp, zero)
                    doh = jnp.where(hm, dop, zero)
                    cur = (_dot_nt(qh, kc), _dot_nt(doh, vc))
                    prev = (_dot_nt(qh, kp), _dot_nt(doh, vp)) if nblk > 1 else None
                    firsts.append((qh, doh, cur, prev))
            seconds = []
            for i, (qh, doh, cur, prev) in enumerate(firsts):
                c0 = rr * HEAD_W + 16 * i
                lse_h = lse_ref[pl.ds(r0, CHUNK), :][:, c0:c0 + 1]
                dl_h = dl_ref[pl.ds(r0, CHUNK), :][:, c0:c0 + 1]
                pc = jnp.exp(jnp.where(m_cur, cur[0] - lse_h, NEG))
                out = [pc.astype(BF16), (pc * (cur[1] - dl_h)).astype(BF16), None, None]
                if nblk > 1:
                    pp = jnp.exp(jnp.where(prev_ok, prev[0] - lse_h, NEG))
                    out[2:] = [pp.astype(BF16), (pp * (prev[1] - dl_h)).astype(BF16)]
                seconds.append(out)
            for hp, (ls, kc, kp) in enumerate(tiles):
                dq = jnp.zeros((CHUNK, CHUNK), F32)
                dkc = jnp.zeros((CHUNK, CHUNK), F32)
                dvc = jnp.zeros((CHUNK, CHUNK), F32)
                dkp = jnp.zeros((CHUNK, CHUNK), F32)
                dvp = jnp.zeros((CHUNK, CHUNK), F32)
                for side, hm in sides:
                    qh, doh, _, _ = firsts[2 * hp + side]
                    pcb, dsc, ppb, dsp = seconds[2 * hp + side]
                    dq = dq + _dot(dsc, jnp.where(hm, kc, zero))
                    dkc = dkc + _dot_tn(dsc, qh)
                    dvc = dvc + _dot_tn(pcb, doh)
                    if nblk > 1:
                        dq = dq + _dot(dsp, jnp.where(hm, kp, zero))
                        dkp = dkp + _dot_tn(dsp, qh)
                        dvp = dvp + _dot_tn(ppb, doh)
                dq_ref[pl.ds(r0, CHUNK), ls] = dq.astype(dq_ref.dtype)
                if nblk == 1:
                    dk_ref[pl.ds(r0, CHUNK), ls] = dkc.astype(dk_ref.dtype)
                    dv_ref[pl.ds(r0, CHUNK), ls] = dvc.astype(dv_ref.dtype)
                else:
                    @pl.when(b > 0)
                    def _():
                        dk_ref[pl.ds(rp, CHUNK), ls] = (dk_wait[:, ls] + dkp).astype(dk_ref.dtype)
                        dv_ref[pl.ds(rp, CHUNK), ls] = (dv_wait[:, ls] + dvp).astype(dv_ref.dtype)

                    dk_wait[:, ls] = dkc
                    dv_wait[:, ls] = dvc
            return carry

        for rr in range(rps):
            lax.fori_loop(0, nblk, lambda b, c, rr=rr: blk(rr, b, c), 0)
        if nblk > 1:
            last = (nblk - 1) * CHUNK
            dk_ref[last:last + CHUNK, :] = dk_wait[...].astype(dk_ref.dtype)
            dv_ref[last:last + CHUNK, :] = dv_wait[...].astype(dv_ref.dtype)

    spec = pl.BlockSpec((seg, rps * ATTN_W), lambda r: (0, r))
    return _pcall(
        body, name=f"attn_bwd_d{dil}", grid=(dil // rps,),
        in_specs=[spec] * 4 + [pl.BlockSpec((seg, rps * HEAD_W), lambda r: (0, r))] * 2, out_specs=[spec] * 3,
        out_shape=[_sds((seg, dil * ATTN_W), BF16)] * 3,
        scratch_shapes=[pltpu.VMEM((CHUNK, ATTN_W), F32), pltpu.VMEM((CHUNK, ATTN_W), F32)],
        args=(qv, kv, vv, dov, deltav, lsev), comms=comms, after=after)


def _inproj_bwd(dqs, dks, dvs, du, dvs_sgu, pos, x, dx1, w_in, g_pre, comms=()):
    def body(dq1, dq2, dq3, dk1, dk2, dk3, dv1, dv2, dv3, du_ref, dvs_ref, pos_ref, x_ref, dx1_ref, w_ref, g_ref,
             dproj_ref, gx_ref, dg_ref, slabs_a, slabs_b):
        @pl.when(pl.program_id(0) == 0)
        def _():
            dg_ref[...] = jnp.zeros_like(dg_ref)

        def total(r1, r2, r3):
            return r1[...] + _load_view(r2, slabs_a, DILATIONS[1]) + _load_view(r3, slabs_b, DILATIONS[2])

        tabs = _rot_tables(pos_ref[...])
        dproj_ref[:, 0:512] = _rope_bwd(total(dq1, dq2, dq3) * np.float32(ATTN_SCALE), tabs).astype(BF16)
        dproj_ref[:, 512:1024] = _rope_bwd(total(dk1, dk2, dk3), tabs).astype(BF16)
        dproj_ref[:, 1024:1536] = total(dv1, dv2, dv3).astype(BF16)
        dproj_ref[:, 1536:2048] = du_ref[...]
        dproj_ref[:, 2048:2560] = dvs_ref[...]
        dh = jnp.zeros((TM, D_MODEL), F32)
        for s in range(N_SHARD):
            dh = dh + _dot_nt(dproj_ref[:, s * IN_S:(s + 1) * IN_S], w_ref[s])
        g = g_ref[...]
        xh, r = _rms_stats(x_ref[...])
        dx, dg = _rms_bwd(xh, r, g, dh)
        dg_ref[...] += dg
        gx_ref[...] = dx1_ref[...] + dx

    return _pcall(
        body, name="inproj_bwd", grid=(SEQ // TM,),
        in_specs=[_view_rows(dil) for dil in DILATIONS] * 3
        + [_rows(512), _rows(512), _rows(1), _rows(D_MODEL), _rows(D_MODEL),
           _resident((N_SHARD, D_MODEL, IN_S)), _resident((1, D_MODEL))],
        out_specs=[_rows(PROJ_W), _rows(D_MODEL), _acc(D_MODEL)],
        out_shape=[_sds((SEQ, PROJ_W), BF16), _sds((SEQ, D_MODEL), F32), _sds((1, D_MODEL), F32)],
        scratch_shapes=[_slab_scratch(), _slab_scratch()],
        args=(*dqs, *dks, *dvs, du, dvs_sgu, pos, x, dx1, w_in, g_pre), comms=comms)


def _to_view(a, dil):
    return a if dil == 1 else a.reshape(SEQ // dil, dil * a.shape[1])


def _from_view(a, dil):
    return a if dil == 1 else a.reshape(SEQ, a.shape[1] // dil)


def _local_step(x, pos, target, w_in, w_out, w_gate, w_up, w_down, small):
    b_t = small["sgu_b_spatial"].T
    h, u, vs, *qkv = _inproj_fwd(x, pos, small["pre_mix_norm"], w_in)
    sgu = _sgu_fwd(u, vs, small["sgu_ln_gain"], small["sgu_ln_bias"], small["sgu_w_spatial"], b_t)
    views = [tuple(qkv[3 * i:3 * i + 3]) for i in range(len(DILATIONS))]
    o_list, l_list = [], []
    for dil, (qv, kv, vv) in zip(DILATIONS, views):
        o, l = _attn_fwd(qv, kv, vv, dil)
        o_list.append(o)
        l_list.append(l)
    attn, mixed, y, x1, *lses = _mix_out_fwd(o_list, l_list, sgu, x, w_out, small["attn_out_norm"],
                                             small["sgu_out_norm"], small["post_mix_norm"])
    h2, a, dg, dup, df, dx1, loss_cols, d_pre_ffn, d_post_ffn = _ffn_fwd_bwd(
        x1, target, w_gate, w_up, w_down, small["pre_ffn_norm"], small["post_ffn_norm"])

    full_tok = pl.BlockSpec((SEQ, D_MODEL), lambda s: (0, 0), pipeline_mode=pl.Buffered(1))
    ff_tok = pl.BlockSpec((1, SEQ, FF_S), lambda s: (s, 0, 0))
    gw_gate = _wgrad(dg, h2, ff_tok, full_tok, (FF_S, D_MODEL), "wgrad_gate")
    gw_up = _wgrad(dup, h2, ff_tok, full_tok, (FF_S, D_MODEL), "wgrad_up")
    gw_down = _wgrad(a, df, ff_tok, full_tok, (FF_S, D_MODEL), "wgrad_down")

    dy, dsgu, d_post_mix, d_attn_norm, d_sgu_norm, *dviews = _outproj_bwd(
        dx1, y, attn, sgu, w_out, small["post_mix_norm"], small["attn_out_norm"], small["sgu_out_norm"])
    gw_out = _wgrad(mixed, dy, pl.BlockSpec((SEQ, OUT_S), lambda s: (0, s)), full_tok, (OUT_S, D_MODEL), "wgrad_out")
    du, dvs_sgu, d_w_sp, d_b_sp, d_ln_gain, d_ln_bias = _sgu_bwd(
        u, vs, dsgu, small["sgu_ln_gain"], small["sgu_ln_bias"], small["sgu_w_spatial"], b_t)

    dqs, dks, dvs = [], [], []
    for i, (dil, (qv, kv, vv)) in enumerate(zip(DILATIONS, views)):
        dq, dk, dv = _attn_bwd(qv, kv, vv, dviews[i], dviews[3 + i], lses[i], dil)
        dqs.append(dq)
        dks.append(dk)
        dvs.append(dv)
    dproj, grad_x, d_pre_mix = _inproj_bwd(dqs, dks, dvs, du, dvs_sgu, pos, x, dx1, w_in, small["pre_mix_norm"])
    gw_in = _wgrad(h, dproj, full_tok, pl.BlockSpec((SEQ, IN_S), lambda s: (0, s)), (D_MODEL, IN_S), "wgrad_in")

    small_grads = {
        "pre_mix_norm": d_pre_mix, "sgu_ln_gain": d_ln_gain, "sgu_ln_bias": d_ln_bias, "sgu_w_spatial": d_w_sp,
        "sgu_b_spatial": d_b_sp, "attn_out_norm": d_attn_norm, "sgu_out_norm": d_sgu_norm,
        "post_mix_norm": d_post_mix, "pre_ffn_norm": d_pre_ffn, "post_ffn_norm": d_post_ffn,
    }
    return loss_cols, grad_x, (gw_in, gw_out, gw_gate, gw_up, gw_down), small_grads


def _coords():
    return lax.axis_index("x"), lax.axis_index("y"), lax.axis_index("c")


def _other_chips(x, y):
    return [(1 - x, y), (x, 1 - y), (1 - x, 1 - y)]


def _comm_call(body, name, n_in, out_shape, scratch_shapes):
    return pl.pallas_call(
        body, name=name, in_specs=[ANY] * n_in, out_specs=[ANY] * len(out_shape), out_shape=out_shape,
        scratch_shapes=scratch_shapes,
        compiler_params=pltpu.CompilerParams(has_side_effects=True),
    )


def _gather_weights(shards):
    n = len(shards)
    halves = [s.reshape(2, s.shape[0] // 2, s.shape[1]) for s in shards]

    def body(*refs):
        ins, outs = refs[:n], refs[n:2 * n]
        send_sems, recv_sems = refs[2 * n:]
        x, y, c = _coords()
        s_me = 2 * x + y
        chips = _other_chips(x, y)
        sibling = (x, y, 1 - c)

        def copy(k, w, shard, cc, to):
            src = ins[w].at[cc] if shard is None else outs[w].at[shard, cc]
            dst = outs[w].at[s_me if shard is None else shard, cc]
            return pltpu.make_async_remote_copy(src_ref=src, dst_ref=dst, send_sem=send_sems.at[k],
                                                recv_sem=recv_sems.at[k], device_id=to, device_id_type=MESH)

        first = [copy(j * n + w, w, None, c, (cx, cy, c)) for j, (cx, cy) in enumerate(chips) for w in range(n)]
        for cp in first:
            cp.start()
        passed = []
        for j, (cx, cy) in enumerate(chips):
            for w in range(n):
                copy(j * n + w, w, 2 * cx + cy, c, (x, y, c)).wait_recv()
                fw = copy((3 + j) * n + w, w, 2 * cx + cy, c, sibling)
                fw.start()
                passed.append(fw)
        for j, (cx, cy) in enumerate(chips):
            for w in range(n):
                copy((3 + j) * n + w, w, 2 * cx + cy, 1 - c, (x, y, c)).wait_recv()
        for cp in first + passed:
            cp.wait_send()

    out_shape = [_sds((N_SHARD,) + h.shape, h.dtype) for h in halves]
    scratch = [pltpu.SemaphoreType.DMA((6 * n,)), pltpu.SemaphoreType.DMA((6 * n,))]
    full = _comm_call(body, "comm_gather_weights", n, out_shape, scratch)(*halves)
    s_me = 2 * lax.axis_index("x") + lax.axis_index("y")
    full = [lax.dynamic_update_slice(f, h[None], (s_me, 0, 0, 0)) for f, h in zip(full, halves)]
    return [f.reshape((N_SHARD,) + s.shape) for f, s in zip(full, shards)]


def _rs_to_sibling(gws):
    n = len(gws)

    def body(*refs):
        ins, outs = refs[:n], refs[n:2 * n]
        send_sems, recv_sems = refs[2 * n:]
        x, y, c = _coords()
        copies = []
        for w in range(n):
            hw = gws[w].shape[1] // 2
            copies.append(pltpu.make_async_remote_copy(
                src_ref=ins[w].at[:, pl.ds((1 - c) * hw, hw), :], dst_ref=outs[w], send_sem=send_sems.at[w],
                recv_sem=recv_sems.at[w], device_id=(x, y, 1 - c), device_id_type=MESH))
        for cp in copies:
            cp.start()
        for cp in copies:
            cp.wait()

    out_shape = [_sds((N_SHARD, g.shape[1] // 2, g.shape[2]), g.dtype) for g in gws]
    scratch = [pltpu.SemaphoreType.DMA((n,)), pltpu.SemaphoreType.DMA((n,))]
    return _comm_call(body, "comm_rs_sibling", n, out_shape, scratch)(*gws)


def _rs_chip_sum(gw, recv, core):
    _, rows, cols = gw.shape
    hw = rows // 2

    def body(c_ref, g_ref, r_ref, o_ref):
        o_ref[...] = (g_ref[...] + r_ref[...]).astype(BF16)

    return pl.pallas_call(
        body, name="rs_chip_sum",
        grid_spec=pltpu.PrefetchScalarGridSpec(
            num_scalar_prefetch=1, grid=(N_SHARD,),
            in_specs=[pl.BlockSpec((1, hw, cols), lambda s, c_ref: (s, c_ref[0], 0)),
                      pl.BlockSpec((1, hw, cols), lambda s, c_ref: (s, 0, 0))],
            out_specs=pl.BlockSpec((1, hw, cols), lambda s, c_ref: (s, 0, 0))),
        out_shape=_sds((N_SHARD, hw, cols), BF16),
        compiler_params=_seq_params(),
    )(core, gw, recv)


def _rs_between_chips(pbs):
    n = len(pbs)

    def body(*refs):
        ins, outs = refs[:n], refs[n:2 * n]
        send_sems, recv_sems = refs[2 * n:]
        x, y, c = _coords()
        copies = []
        for j, (cx, cy) in enumerate(_other_chips(x, y)):
            for w in range(n):
                copies.append(pltpu.make_async_remote_copy(
                    src_ref=ins[w].at[2 * cx + cy], dst_ref=outs[w].at[j], send_sem=send_sems.at[j * n + w],
                    recv_sem=recv_sems.at[j * n + w], device_id=(cx, cy, c), device_id_type=MESH))
        for cp in copies:
            cp.start()
        for cp in copies:
            cp.wait()

    out_shape = [_sds((3,) + p.shape[1:], p.dtype) for p in pbs]
    scratch = [pltpu.SemaphoreType.DMA((3 * n,)), pltpu.SemaphoreType.DMA((3 * n,))]
    return _comm_call(body, "comm_rs_chips", n, out_shape, scratch)(*pbs)


def _rs_final_sum(gw, recv_sib, recv_chips, shard_core):
    _, rows, cols = gw.shape
    hw = rows // 2

    def body(sc_ref, g_ref, r_ref, rc_ref, o_ref):
        acc = g_ref[0] + r_ref[0]
        for j in range(3):
            acc = acc + rc_ref[j].astype(F32)
        o_ref[0] = acc

    return pl.pallas_call(
        body, name="rs_final_sum",
        grid_spec=pltpu.PrefetchScalarGridSpec(
            num_scalar_prefetch=1, grid=(1,),
            in_specs=[pl.BlockSpec((1, hw, cols), lambda i, sc: (sc[0], sc[1], 0)),
                      pl.BlockSpec((1, hw, cols), lambda i, sc: (sc[0], 0, 0)),
                      pl.BlockSpec((3, hw, cols), lambda i, sc: (0, 0, 0))],
            out_specs=pl.BlockSpec((1, hw, cols), lambda i, sc: (sc[1], 0, 0))),
        out_shape=_sds((2, hw, cols), F32),
        compiler_params=_seq_params(),
    )(shard_core, gw, recv_sib, recv_chips)


def _rs_join_halves(halves):
    n = len(halves)

    def body(*refs):
        bufs = refs[n:2 * n]
        send_sems, recv_sems = refs[2 * n:]
        x, y, c = _coords()
        remote = [pltpu.make_async_remote_copy(
            src_ref=bufs[w].at[c], dst_ref=bufs[w].at[c], send_sem=send_sems.at[w], recv_sem=recv_sems.at[w],
            device_id=(x, y, 1 - c), device_id_type=MESH) for w in range(n)]
        for cp in remote:
            cp.start()
        for w in range(n):
            remote[w].wait_send()
            pltpu.make_async_remote_copy(
                src_ref=bufs[w].at[c], dst_ref=bufs[w].at[1 - c], send_sem=send_sems.at[w],
                recv_sem=recv_sems.at[w], device_id=(x, y, c), device_id_type=MESH).wait_recv()

    joined = pl.pallas_call(
        body, name="comm_rs_join", in_specs=[ANY] * n, out_specs=[ANY] * n,
        out_shape=[_sds(h.shape, h.dtype) for h in halves], input_output_aliases={w: w for w in range(n)},
        scratch_shapes=[pltpu.SemaphoreType.DMA((n,)), pltpu.SemaphoreType.DMA((n,))],
        compiler_params=pltpu.CompilerParams(has_side_effects=True),
    )(*halves)
    return [j.reshape(2 * h.shape[1], h.shape[2]) for j, h in zip(joined, halves)]


def _allreduce_small(buf):
    rows, cols = buf.shape

    def body(in_ref, out_ref, slots, send_sems, recv_sems):
        x, y, c = _coords()
        me = 4 * x + 2 * y + c
        copies, peers = [], []
        for k in range(1, 8):
            px = 1 - x if (k >> 2) & 1 else x
            py = 1 - y if (k >> 1) & 1 else y
            pc = 1 - c if k & 1 else c
            peers.append(4 * px + 2 * py + pc)
            copies.append(pltpu.make_async_remote_copy(
                src_ref=in_ref, dst_ref=slots.at[me], send_sem=send_sems.at[k - 1], recv_sem=recv_sems.at[k - 1],
                device_id=(px, py, pc), device_id_type=MESH))
        for cp in copies:
            cp.start()
        slots[me] = in_ref[...]
        for k in range(7):
            pltpu.make_async_remote_copy(
                src_ref=in_ref, dst_ref=slots.at[peers[k]], send_sem=send_sems.at[k], recv_sem=recv_sems.at[k],
                device_id=(x, y, c), device_id_type=MESH).wait_recv()
        for cp in copies:
            cp.wait_send()
        acc = slots[0]
        for i in range(1, 8):
            acc = acc + slots[i]
        out_ref[...] = acc

    vmem = pl.BlockSpec(memory_space=pltpu.VMEM)
    return pl.pallas_call(
        body, name="comm_allreduce_small", in_specs=[vmem], out_specs=vmem, out_shape=_sds((rows, cols), F32),
        scratch_shapes=[pltpu.VMEM((8, rows, cols), F32), pltpu.SemaphoreType.DMA((7,)), pltpu.SemaphoreType.DMA((7,))],
        compiler_params=pltpu.CompilerParams(has_side_effects=True, vmem_limit_bytes=VMEM_LIMIT),
    )(buf)


def _adamw(w, g, m, v, block_rows, name, after=()):
    rows, cols = w.shape

    def body(w_ref, g_ref, m_ref, v_ref, *rest):
        d_ref, nm_ref, nv_ref = rest[len(after):]
        g = g_ref[...]
        m = ADAM_B1 * m_ref[...] + (1.0 - ADAM_B1) * g
        v = ADAM_B2 * v_ref[...] + (1.0 - ADAM_B2) * (g * g)
        m_hat = m / (1.0 - ADAM_B1 ** ADAM_STEP)
        v_hat = v / (1.0 - ADAM_B2 ** ADAM_STEP)
        d_ref[...] = -ADAM_LR * (m_hat / (jnp.sqrt(v_hat) + ADAM_EPS) + ADAM_WD * w_ref[...])
        nm_ref[...] = m
        nv_ref[...] = v

    spec = pl.BlockSpec((block_rows, cols), lambda i: (i, 0))
    return pl.pallas_call(
        body, name=name, grid=(rows // block_rows,), in_specs=[spec] * 4 + [ANY] * len(after), out_specs=[spec] * 3,
        out_shape=[_sds((rows, cols), F32)] * 3,
        compiler_params=_seq_params(),
    )(w, g, m, v, *after)


WEIGHTS = ("pre_mix_norm", "w_in", "sgu_ln_gain", "sgu_ln_bias", "sgu_w_spatial", "sgu_b_spatial", "attn_out_norm",
           "sgu_out_norm", "w_out", "post_mix_norm", "pre_ffn_norm", "w_gate", "w_up", "w_down", "post_ffn_norm")
BIG = ("w_in", "w_out", "w_gate", "w_up", "w_down")
BIG_ADAM_ROWS = {"w_in": 256, "w_out": 128, "w_gate": 352, "w_up": 352, "w_down": 352}
SMALL = ("pre_mix_norm", "post_mix_norm", "pre_ffn_norm", "post_ffn_norm", "sgu_ln_gain", "sgu_ln_bias",
         "attn_out_norm", "sgu_out_norm", "sgu_w_spatial", "sgu_b_spatial")


def _pack_small(d):
    flat = [d[n].reshape(-1) for n in SMALL]
    used = sum(f.shape[0] for f in flat)
    flat.append(jnp.zeros((SMALL_ROWS * 1024 - used,), F32))
    return jnp.concatenate(flat).reshape(SMALL_ROWS, 1024)


def _unpack_small(buf, shapes):
    flat = buf.reshape(-1)
    out, off = {}, 0
    for n in SMALL:
        size = int(np.prod(shapes[n]))
        out[n] = flat[off:off + size].reshape(shapes[n])
        off += size
    return out


def _kernel_unoverlapped(x, positions, pre_mix_norm, w_in, sgu_ln_gain, sgu_ln_bias, sgu_w_spatial, sgu_b_spatial, attn_out_norm, sgu_out_norm, w_out, post_mix_norm, pre_ffn_norm, w_gate, w_up, w_down, post_ffn_norm, loss_target, m_pre_mix_norm, m_w_in, m_sgu_ln_gain, m_sgu_ln_bias, m_sgu_w_spatial, m_sgu_b_spatial, m_attn_out_norm, m_sgu_out_norm, m_w_out, m_post_mix_norm, m_pre_ffn_norm, m_w_gate, m_w_up, m_w_down, m_post_ffn_norm, v_pre_mix_norm, v_w_in, v_sgu_ln_gain, v_sgu_ln_bias, v_sgu_w_spatial, v_sgu_b_spatial, v_attn_out_norm, v_sgu_out_norm, v_w_out, v_post_mix_norm, v_pre_ffn_norm, v_w_gate, v_w_up, v_w_down, v_post_ffn_norm):
    a = dict(locals())
    cx, cy, cc = _coords()
    core = jnp.stack([cc]).astype(jnp.int32)
    shard_core = jnp.stack([2 * cx + cy, cc]).astype(jnp.int32)

    full = _gather_weights([a[n][0].astype(BF16) for n in BIG])
    small = {n: (a[n][0] if a[n].ndim > 2 else a[n]) for n in SMALL}
    loss_cols, grad_x, gws, small_grads = _local_step(
        x[0], positions.reshape(SEQ, 1), loss_target[0], *full, small)
    loss = lax.psum(jnp.sum(loss_cols) * np.float32(0.5 / D_MODEL), ("x", "y", "c"))

    recv_sib = _rs_to_sibling(list(gws))
    chip_part = [_rs_chip_sum(g, r, core) for g, r in zip(gws, recv_sib)]
    recv_chips = _rs_between_chips(chip_part)
    halves = [_rs_final_sum(g, r, rc, shard_core) for g, r, rc in zip(gws, recv_sib, recv_chips)]
    big_grads = dict(zip(BIG, _rs_join_halves(halves)))

    shapes = {n: a[n].shape for n in SMALL}
    small_sum = _allreduce_small(_pack_small(small_grads))

    grads, deltas, new_m, new_v = {}, {}, {}, {}
    for n in BIG:
        grads[n] = big_grads[n][None]
        d, nm, nv = _adamw(a[n][0], big_grads[n], a["m_" + n][0], a["v_" + n][0], BIG_ADAM_ROWS[n], "adamw_" + n)
        deltas[n], new_m[n], new_v[n] = d[None], nm[None], nv[None]
    d, nm, nv = _adamw(_pack_small({n: a[n] for n in SMALL}), small_sum, _pack_small({n: a["m_" + n] for n in SMALL}),
                       _pack_small({n: a["v_" + n] for n in SMALL}), SMALL_ROWS, "adamw_small")
    grads.update(_unpack_small(small_sum, shapes))
    deltas.update(_unpack_small(d, shapes))
    new_m.update(_unpack_small(nm, shapes))
    new_v.update(_unpack_small(nv, shapes))
    return (loss, grad_x[None], *[grads[n] for n in WEIGHTS], *[deltas[n] for n in WEIGHTS],
            *[new_m[n] for n in WEIGHTS], *[new_v[n] for n in WEIGHTS])


def _remote(src, dst, send_sem, recv_sem, to):
    return pltpu.make_async_remote_copy(src_ref=src, dst_ref=dst, send_sem=send_sem, recv_sem=recv_sem,
                                        device_id=to, device_id_type=MESH)


def _halves(a):
    *lead, rows, cols = a.shape
    return a.reshape(*lead, 2, rows // 2, cols)


def _gather_ici(shards):
    n = len(shards)

    def desc(ins, outs, ss, rs, j, w, landed):
        x, y, c = _coords()
        cx, cy = _other_chips(x, y)[j]
        shard = 2 * cx + cy if landed else 2 * x + y
        return _remote(ins[w].at[c], outs[w].at[shard, c], ss.at[j * n + w], rs.at[j * n + w], (cx, cy, c))

    def start(ins, outs, ss, rs):
        for j in range(3):
            for w in range(n):
                desc(ins, outs, ss, rs, j, w, False).start()

    def finish(ins, outs, ss, rs):
        for j in range(3):
            for w in range(n):
                desc(ins, outs, ss, rs, j, w, True).wait_recv()
                desc(ins, outs, ss, rs, j, w, False).wait_send()

    return _Comm(shards, [_sds((N_SHARD,) + s.shape, s.dtype) for s in shards], 3 * n, start, finish)


def _gather_pass(fulls):
    n = len(fulls)

    def desc(bufs, ss, rs, j, w, landed):
        x, y, c = _coords()
        cx, cy = _other_chips(x, y)[j]
        shard = 2 * cx + cy
        return _remote(bufs[w].at[shard, c], bufs[w].at[shard, 1 - c if landed else c],
                       ss.at[j * n + w], rs.at[j * n + w], (x, y, 1 - c))

    def start(ins, outs, ss, rs):
        for j in range(3):
            for w in range(n):
                desc(outs, ss, rs, j, w, False).start()

    def finish(ins, outs, ss, rs):
        for j in range(3):
            for w in range(n):
                desc(outs, ss, rs, j, w, True).wait_recv()
                desc(outs, ss, rs, j, w, False).wait_send()

    return _Comm(fulls, [_sds(f.shape, f.dtype) for f in fulls], 3 * n, start, finish, aliased=True)


def _rs_sibling(gws):
    n = len(gws)

    def desc(ins, outs, ss, rs, w):
        x, y, c = _coords()
        return _remote(ins[w].at[:, 1 - c], outs[w], ss.at[w], rs.at[w], (x, y, 1 - c))

    def start(ins, outs, ss, rs):
        for w in range(n):
            desc(ins, outs, ss, rs, w).start()

    def finish(ins, outs, ss, rs):
        for w in range(n):
            desc(ins, outs, ss, rs, w).wait()

    out_shape = [_sds((N_SHARD, g.shape[1] // 2, g.shape[2]), g.dtype) for g in gws]
    return _Comm([_halves(g) for g in gws], out_shape, n, start, finish)


def _rs_chips(pbs):
    n = len(pbs)

    def desc(ins, outs, ss, rs, j, w):
        x, y, c = _coords()
        cx, cy = _other_chips(x, y)[j]
        return _remote(ins[w].at[2 * cx + cy], outs[w].at[j], ss.at[j * n + w], rs.at[j * n + w], (cx, cy, c))

    def start(ins, outs, ss, rs):
        for j in range(3):
            for w in range(n):
                desc(ins, outs, ss, rs, j, w).start()

    def finish(ins, outs, ss, rs):
        for j in range(3):
            for w in range(n):
                desc(ins, outs, ss, rs, j, w).wait()

    return _Comm(pbs, [_sds((3,) + p.shape[1:], p.dtype) for p in pbs], 3 * n, start, finish)


def _rs_join(halves):
    n = len(halves)

    def desc(bufs, ss, rs, w, landed):
        x, y, c = _coords()
        return _remote(bufs[w].at[c], bufs[w].at[1 - c if landed else c], ss.at[w], rs.at[w], (x, y, 1 - c))

    def start(ins, outs, ss, rs):
        for w in range(n):
            desc(outs, ss, rs, w, False).start()

    def finish(ins, outs, ss, rs):
        for w in range(n):
            desc(outs, ss, rs, w, True).wait_recv()
            desc(outs, ss, rs, w, False).wait_send()

    return _Comm(halves, [_sds(h.shape, h.dtype) for h in halves], n, start, finish, aliased=True)


def _small_exchange(buf):
    def desc(ins, outs, ss, rs, k, landed):
        x, y, c = _coords()
        px = 1 - x if (k >> 2) & 1 else x
        py = 1 - y if (k >> 1) & 1 else y
        pc = 1 - c if k & 1 else c
        slot = 4 * px + 2 * py + pc if landed else 4 * x + 2 * y + c
        return _remote(ins[0], outs[0].at[slot], ss.at[k - 1], rs.at[k - 1], (px, py, pc))

    def start(ins, outs, ss, rs):
        for k in range(1, 8):
            desc(ins, outs, ss, rs, k, False).start()

    def finish(ins, outs, ss, rs):
        for k in range(1, 8):
            desc(ins, outs, ss, rs, k, True).wait_recv()
            desc(ins, outs, ss, rs, k, False).wait_send()

    return _Comm([buf], [_sds((8,) + buf.shape, buf.dtype)], 7, start, finish)


HBM = pl.BlockSpec(memory_space=pltpu.HBM)
SEM = pl.BlockSpec(memory_space=pltpu.SEMAPHORE)
DATAFLOW = pltpu.SideEffectType.DATAFLOW_SIDE_EFFECTING


def _split_start(name, comm, lands, after):
    srcs = [pltpu.with_memory_space_constraint(s, pltpu.HBM) for s in comm.args]
    lands = [pltpu.with_memory_space_constraint(b, pltpu.HBM) for b in lands]
    ns, nb = len(srcs), len(lands)

    def body(*refs):
        send_sems, recv_sems = refs[ns + nb + 1], refs[ns + nb + 2]
        comm.start(refs[:ns], refs[ns:ns + nb], send_sems, recv_sems)
        refs[-1][...] = jnp.zeros_like(refs[-1])

    res = pl.pallas_call(
        body, name=name,
        out_shape=(pltpu.SemaphoreType.DMA((comm.n_sems,)), pltpu.SemaphoreType.DMA((comm.n_sems,)),
                   *[pltpu.HBM(b.shape, b.dtype) for b in srcs + lands], _sds((8, 128), F32)),
        in_specs=[HBM] * (ns + nb) + [ANY],
        out_specs=(SEM, SEM, *[HBM] * (ns + nb), pl.BlockSpec(memory_space=pltpu.VMEM)),
        input_output_aliases={i: 2 + i for i in range(ns + nb)},
        compiler_params=pltpu.CompilerParams(has_side_effects=DATAFLOW),
    )(*srcs, *lands, after)
    return res[0], res[1], list(res[2:2 + ns]), list(res[2 + ns:2 + ns + nb]), res[-1]


def _split_starts(name, comms, after):
    srcs = [[pltpu.with_memory_space_constraint(s, pltpu.HBM) for s in c.args] for c in comms]
    lands = [[pltpu.with_memory_space_constraint(lax.empty(o.shape, o.dtype), pltpu.HBM) for o in c.out_shape]
             for c in comms]
    bufs = [b for k in range(len(comms)) for b in srcs[k] + lands[k]]
    nb, nc = len(bufs), len(comms)

    def body(*refs):
        sems = refs[nb + 1:nb + 1 + 2 * nc]
        off = 0
        for k, c in enumerate(comms):
            ns, nl = len(srcs[k]), len(lands[k])
            c.start(refs[off:off + ns], refs[off + ns:off + ns + nl], sems[2 * k], sems[2 * k + 1])
            off += ns + nl
        refs[-1][...] = jnp.zeros_like(refs[-1])

    res = pl.pallas_call(
        body, name=name,
        out_shape=(*[pltpu.SemaphoreType.DMA((c.n_sems,)) for c in comms for _ in range(2)],
                   *[pltpu.HBM(b.shape, b.dtype) for b in bufs], _sds((8, 128), F32)),
        in_specs=[HBM] * nb + [ANY],
        out_specs=(*[SEM] * (2 * nc), *[HBM] * nb, pl.BlockSpec(memory_space=pltpu.VMEM)),
        input_output_aliases={i: 2 * nc + i for i in range(nb)},
        compiler_params=pltpu.CompilerParams(has_side_effects=DATAFLOW),
    )(*bufs, after)
    states, off = [], 2 * nc
    for k in range(nc):
        ns, nl = len(srcs[k]), len(lands[k])
        states.append((res[2 * k], res[2 * k + 1], list(res[off:off + ns]), list(res[off + ns:off + ns + nl])))
        off += ns + nl
    return states, res[-1]


def _split_wait(name, comm, send_sems, recv_sems, srcs, lands, after):
    ns, nb = len(srcs), len(lands)
    after = list(after) if isinstance(after, (list, tuple)) else [after]

    def body(*refs):
        comm.finish(refs[:ns], refs[ns:ns + nb], refs[ns + nb], refs[ns + nb + 1])

    res = pl.pallas_call(
        body, name=name,
        out_shape=tuple(pltpu.HBM(b.shape, b.dtype) for b in srcs + lands),
        in_specs=[HBM] * (ns + nb) + [SEM, SEM] + [ANY] * len(after), out_specs=tuple([HBM] * (ns + nb)),
        input_output_aliases={i: i for i in range(ns + nb)},
        compiler_params=pltpu.CompilerParams(has_side_effects=DATAFLOW),
    )(*srcs, *lands, send_sems, recv_sems, *after)
    return list(res[ns:])


LOSS_ROW = "loss_cols"
SMALL_EARLY = ("post_mix_norm", "pre_ffn_norm", "post_ffn_norm", "sgu_ln_gain", "sgu_ln_bias", "attn_out_norm",
               "sgu_out_norm", "sgu_w_spatial", "sgu_b_spatial", LOSS_ROW)
SMALL_LATE = ("pre_mix_norm",)


def _pack(d, names, rows):
    flat = [d[n].reshape(-1) for n in names]
    used = sum(f.shape[0] for f in flat)
    flat.append(jnp.zeros((rows * 1024 - used,), F32))
    return jnp.concatenate(flat).reshape(rows, 1024)


def _unpack(buf, names, shapes):
    flat = buf.reshape(-1)
    out, off = {}, 0
    for n in names:
        size = int(np.prod(shapes[n]))
        out[n] = flat[off:off + size].reshape(shapes[n])
        off += size
    return out


def _merge(comms):
    def run(phase):
        def go(ins, outs, ss, rs):
            ii = oi = si = 0
            for c in comms:
                getattr(c, phase)(ins[ii:ii + len(c.args)], outs[oi:oi + len(c.out_shape)],
                                  ss.at[pl.ds(si, c.n_sems)], rs.at[pl.ds(si, c.n_sems)])
                ii += len(c.args)
                oi += len(c.out_shape)
                si += c.n_sems
        return go

    return _Comm([a for c in comms for a in c.args], [o for c in comms for o in c.out_shape],
                 sum(c.n_sems for c in comms), run("start"), run("finish"))


def _chip_sums(gws, recvs, core):
    n = len(gws)
    _, rows, cols = gws[0].shape
    hw = rows // 2

    def body(c_ref, *refs):
        for k in range(n):
            refs[2 * n + k][...] = (refs[k][...] + refs[n + k][...]).astype(BF16)

    mine = pl.BlockSpec((1, hw, cols), lambda s, c_ref: (s, c_ref[0], 0))
    plain = pl.BlockSpec((1, hw, cols), lambda s, c_ref: (s, 0, 0))
    return pl.pallas_call(
        body, name="rs_chip_sums",
        grid_spec=pltpu.PrefetchScalarGridSpec(num_scalar_prefetch=1, grid=(N_SHARD,),
                                               in_specs=[mine] * n + [plain] * n, out_specs=[plain] * n),
        out_shape=[_sds((N_SHARD, hw, cols), BF16)] * n,
        compiler_params=_seq_params(),
    )(core, *gws, *recvs)


def _final_sums(gws, recv_sibs, recv_chips, shard_core):
    n = len(gws)
    _, rows, cols = gws[0].shape
    hw = rows // 2

    def body(sc_ref, *refs):
        for k in range(n):
            acc = refs[k][0] + refs[n + k][0]
            for j in range(3):
                acc = acc + refs[2 * n + k][j].astype(F32)
            refs[3 * n + k][0] = acc

    return pl.pallas_call(
        body, name="rs_final_sums",
        grid_spec=pltpu.PrefetchScalarGridSpec(
            num_scalar_prefetch=1, grid=(1,),
            in_specs=[pl.BlockSpec((1, hw, cols), lambda i, sc: (sc[0], sc[1], 0))] * n
            + [pl.BlockSpec((1, hw, cols), lambda i, sc: (sc[0], 0, 0))] * n
            + [pl.BlockSpec((3, hw, cols), lambda i, sc: (0, 0, 0))] * n,
            out_specs=[pl.BlockSpec((1, hw, cols), lambda i, sc: (sc[1], 0, 0))] * n),
        out_shape=[_sds((2, hw, cols), F32)] * n,
        compiler_params=_seq_params(),
    )(shard_core, *gws, *recv_sibs, *recv_chips)


def _adamw_multi(ws, gs, ms, vs, block_rows, name, after=()):
    n = len(ws)
    rows, cols = ws[0].shape

    def body(*refs):
        outs = refs[4 * n + len(after):]
        for k in range(n):
            g = refs[n + k][...]
            d, m, v = _adam_math(refs[k][...], g, refs[2 * n + k][...], refs[3 * n + k][...])
            outs[4 * k][...], outs[4 * k + 1][...], outs[4 * k + 2][...], outs[4 * k + 3][...] = g, d, m, v

    spec = pl.BlockSpec((block_rows, cols), lambda i: (i, 0))
    res = pl.pallas_call(
        body, name=name, grid=(rows // block_rows,), in_specs=[spec] * (4 * n) + [ANY] * len(after),
        out_specs=[spec] * (4 * n), out_shape=[_sds((rows, cols), F32)] * (4 * n),
        compiler_params=_seq_params(),
    )(*ws, *gs, *ms, *vs, *after)
    return [tuple(res[4 * k:4 * k + 4]) for k in range(n)]


def _wgrad_pair(a1, a2, b, a_spec, b_spec, out_block, name, comms=()):
    def body(a1_ref, a2_ref, b_ref, o1_ref, o2_ref):
        bv = b_ref[...]
        o1_ref[0] = _dot_tn(a1_ref[0], bv)
        o2_ref[0] = _dot_tn(a2_ref[0], bv)

    out_spec = pl.BlockSpec((1,) + out_block, lambda s: (s, 0, 0))
    return _pcall(
        body, name=name, grid=(N_SHARD,), in_specs=[a_spec, a_spec, b_spec], out_specs=[out_spec, out_spec],
        out_shape=[_sds((N_SHARD,) + out_block, F32)] * 2, args=(a1, a2, b), comms=comms)


def _comm_only(name, comms):
    return _pcall(lambda: None, name=name, grid=(1,), in_specs=[], out_specs=[], out_shape=[], args=(),
                  comms=comms)[1]


def _adam_math(w, g, m, v):
    m = ADAM_B1 * m + (1.0 - ADAM_B1) * g
    v = ADAM_B2 * v + (1.0 - ADAM_B2) * (g * g)
    m_hat = m / (1.0 - ADAM_B1 ** ADAM_STEP)
    v_hat = v / (1.0 - ADAM_B2 ** ADAM_STEP)
    return -ADAM_LR * (m_hat / (jnp.sqrt(v_hat) + ADAM_EPS) + ADAM_WD * w), m, v


def _adamw_small(own, slots, w, m, v, me):
    rows, cols = own.shape

    def body(me_ref, own_ref, slots_ref, w_ref, m_ref, v_ref, g_ref, d_ref, nm_ref, nv_ref):
        own_v = own_ref[...]
        g = jnp.where(me_ref[0] == 0, own_v, slots_ref[0])
        for i in range(1, 8):
            g = g + jnp.where(me_ref[0] == i, own_v, slots_ref[i])
        g_ref[...] = g
        d_ref[...], nm_ref[...], nv_ref[...] = _adam_math(w_ref[...], g, m_ref[...], v_ref[...])

    flat = pl.BlockSpec((rows, cols), lambda i, me_ref: (0, 0))
    return pl.pallas_call(
        body, name="adamw_small",
        grid_spec=pltpu.PrefetchScalarGridSpec(
            num_scalar_prefetch=1, grid=(1,),
            in_specs=[flat, pl.BlockSpec((8, rows, cols), lambda i, me_ref: (0, 0, 0)), flat, flat, flat],
            out_specs=[flat] * 4),
        out_shape=[_sds((rows, cols), F32)] * 4,
        compiler_params=_seq_params(),
    )(me, own, slots, w, m, v)


def kernel(x, positions, pre_mix_norm, w_in, sgu_ln_gain, sgu_ln_bias, sgu_w_spatial, sgu_b_spatial, attn_out_norm, sgu_out_norm, w_out, post_mix_norm, pre_ffn_norm, w_gate, w_up, w_down, post_ffn_norm, loss_target, m_pre_mix_norm, m_w_in, m_sgu_ln_gain, m_sgu_ln_bias, m_sgu_w_spatial, m_sgu_b_spatial, m_attn_out_norm, m_sgu_out_norm, m_w_out, m_post_mix_norm, m_pre_ffn_norm, m_w_gate, m_w_up, m_w_down, m_post_ffn_norm, v_pre_mix_norm, v_w_in, v_sgu_ln_gain, v_sgu_ln_bias, v_sgu_w_spatial, v_sgu_b_spatial, v_attn_out_norm, v_sgu_out_norm, v_w_out, v_post_mix_norm, v_pre_ffn_norm, v_w_gate, v_w_up, v_w_down, v_post_ffn_norm):
    a = dict(locals())
    cx, cy, cc = _coords()
    s_me = 2 * cx + cy
    core = jnp.stack([cc]).astype(jnp.int32)
    shard_core = jnp.stack([s_me, cc]).astype(jnp.int32)
    me = jnp.stack([4 * cx + 2 * cy + cc]).astype(jnp.int32)
    small = {n: (a[n][0] if a[n].ndim > 2 else a[n]) for n in SMALL}
    b_t = small["sgu_b_spatial"].T
    xs, pos, target = x[0], positions.reshape(SEQ, 1), loss_target[0]
    flipped = ("w_gate", "w_up")

    def big(name, n):
        return jnp.swapaxes(a[name], 1, 2)[0] if n in flipped else a[name][0]

    own = {n: _halves(big(n, n).astype(BF16)) for n in BIG}

    def with_own(full, n):
        full = lax.dynamic_update_slice(full, own[n][None], (s_me, 0, 0, 0))
        return full.reshape((N_SHARD,) + big(n, n).shape)

    ffn = ("w_gate", "w_up", "w_down")
    g_in, g_out, g_ffn = _gather_ici([own["w_in"]]), _gather_ici([own["w_out"]]), _gather_ici([own[n] for n in ffn])
    (s_in, s_out, s_ffn), token = _split_starts("gather_start", [g_in, g_out, g_ffn], small["pre_mix_norm"])
    in_lands = _split_wait("gather_in_wait", g_in, *s_in, token)
    ((in_lands,),) = _comm_only("comm_pass_in", [_gather_pass(in_lands)])
    w_in_f = with_own(in_lands, "w_in")
    h, u, vs, *qkv = _inproj_fwd(xs, pos, small["pre_mix_norm"], w_in_f)
    views = [tuple(qkv[3 * i:3 * i + 3]) for i in range(len(DILATIONS))]
    o_list, l_list = [], []
    for dil, (qv, kv, vv) in zip(DILATIONS, views):
        o, l = _attn_fwd(qv, kv, vv, dil)
        o_list.append(o)
        l_list.append(l)
    out_lands = _split_wait("gather_out_wait", g_out, *s_out, l_list[-1])
    sgu, ((out_lands,),) = _sgu_fwd(u, vs, small["sgu_ln_gain"], small["sgu_ln_bias"], small["sgu_w_spatial"], b_t,
                                    comms=[_gather_pass(out_lands)])
    w_out_f = with_own(out_lands, "w_out")
    ffn_lands = _split_wait("gather_ffn_wait", g_ffn, *s_ffn, sgu)
    (attn, mixed, y, x1, *lses), (ffn_lands,) = _mix_out_fwd(
        o_list, l_list, sgu, xs, w_out_f, small["attn_out_norm"], small["sgu_out_norm"], small["post_mix_norm"],
        comms=[_gather_pass(ffn_lands)])
    w_gate_f, w_up_f, w_down_f = (with_own(f, n) for f, n in zip(ffn_lands, ffn))
    h2, act, dg, dup, df, dx1, loss_cols, d_pre_ffn, d_post_ffn = _ffn_fwd_bwd(
        x1, target, w_gate_f, w_up_f, w_down_f, small["pre_ffn_norm"], small["post_ffn_norm"])

    full_tok = pl.BlockSpec((SEQ, D_MODEL), lambda s: (0, 0), pipeline_mode=pl.Buffered(1))
    ff_tok = pl.BlockSpec((1, SEQ, FF_S), lambda s: (s, 0, 0))
    gw = {}
    gw["w_gate"], gw["w_up"] = _wgrad_pair(dg, dup, h2, ff_tok, full_tok, (FF_S, D_MODEL), "wgrad_gate_up")
    gw["w_down"], ((sib_gate, sib_up),) = _wgrad(act, df, ff_tok, full_tok, (FF_S, D_MODEL), "wgrad_down",
                                                 comms=[_rs_sibling([gw["w_gate"], gw["w_up"]])])
    (dy, dsgu, d_post_mix, d_attn_norm, d_sgu_norm, *dviews), ((sib_down,),) = _outproj_bwd(
        dx1, y, attn, sgu, w_out_f, small["post_mix_norm"], small["attn_out_norm"],
        small["sgu_out_norm"], comms=[_rs_sibling([gw["w_down"]])])
    sib = {"w_gate": sib_gate, "w_up": sib_up, "w_down": sib_down}
    part = dict(zip(ffn, _chip_sums([gw[n] for n in ffn], [sib[n] for n in ffn], core)))
    gw["w_out"] = _wgrad(mixed, dy, pl.BlockSpec((SEQ, OUT_S), lambda s: (0, s)), full_tok, (OUT_S, D_MODEL),
                         "wgrad_out")
    x_ffn = _rs_chips([part[n] for n in ffn])
    (s_ffn,), token = _split_starts("rs_ffn_start", [x_ffn], small["pre_mix_norm"])
    (du, dvs_sgu, d_w_sp, d_b_sp, d_ln_gain, d_ln_bias), ((sib["w_out"],),) = _sgu_bwd(
        u, vs, dsgu, small["sgu_ln_gain"], small["sgu_ln_bias"], small["sgu_w_spatial"], b_t,
        comms=[_rs_sibling([gw["w_out"]])], after=[token])
    part["w_out"] = _rs_chip_sum(gw["w_out"], sib["w_out"], core)
    packed_early = _pack({
        "sgu_ln_gain": d_ln_gain, "sgu_ln_bias": d_ln_bias, "sgu_w_spatial": d_w_sp, "sgu_b_spatial": d_b_sp,
        "attn_out_norm": d_attn_norm, "sgu_out_norm": d_sgu_norm, "post_mix_norm": d_post_mix,
        "pre_ffn_norm": d_pre_ffn, "post_ffn_norm": d_post_ffn, LOSS_ROW: loss_cols}, SMALL_EARLY, SMALL_ROWS)
    x_out, x_small = _rs_chips([part["w_out"]]), _small_exchange(packed_early)
    (s_out, s_small), token = _split_starts("rs_out_small_start", [x_out, x_small], token)

    dqs, dks, dvs = [], [], []
    for i, (dil, (qv, kv, vv)) in enumerate(zip(DILATIONS, views)):
        dq, dk, dv = _attn_bwd(qv, kv, vv, dviews[i], dviews[3 + i], lses[i], dil, after=[token])
        dqs.append(dq)
        dks.append(dk)
        dvs.append(dv)
    half, far, joined = {}, {}, {}
    far.update(zip(ffn, _split_wait("rs_ffn_wait", x_ffn, *s_ffn, dvs[-1])))
    half.update(zip(ffn, _final_sums([gw[n] for n in ffn], [sib[n] for n in ffn], [far[n] for n in ffn],
                                     shard_core)))
    dproj, grad_x, d_pre_mix = _inproj_bwd(dqs, dks, dvs, du, dvs_sgu, pos, xs, dx1, w_in_f, small["pre_mix_norm"])
    (far["w_out"],) = _split_wait("rs_out_wait", x_out, *s_out, grad_x)
    (slots_early,) = _split_wait("small_early_wait", x_small, *s_small, far["w_out"])
    half["w_out"] = _rs_final_sum(gw["w_out"], sib["w_out"], far["w_out"], shard_core)
    packed_late = _pack({"pre_mix_norm": d_pre_mix}, SMALL_LATE, 8)
    names = ffn + ("w_out",)
    gw["w_in"], (got, (slots_late,)) = _wgrad(
        h, dproj, full_tok, pl.BlockSpec((SEQ, IN_S), lambda s: (0, s)), (D_MODEL, IN_S), "wgrad_in",
        comms=[_rs_join([half[n] for n in names]), _small_exchange(packed_late)])
    joined.update(zip(names, got))
    ((sib["w_in"],),) = _comm_only("comm_rs_sibling_in", [_rs_sibling([gw["w_in"]])])
    part["w_in"] = _rs_chip_sum(gw["w_in"], sib["w_in"], core)
    x_in = _rs_chips([part["w_in"]])
    (s_in,), token = _split_starts("rs_in_start", [x_in], small["pre_mix_norm"])

    grads, deltas, new_m, new_v = {}, {}, {}, {}

    def record(n, outs):
        grads[n], deltas[n], new_m[n], new_v[n] = (
            jnp.swapaxes(o[None], 1, 2) if n in flipped else o[None] for o in outs)

    def update(names, block_rows, name, after):
        for n, outs in zip(names, _adamw_multi(
                [big(n, n) for n in names], [joined[n].reshape(big(n, n).shape) for n in names],
                [big("m_" + n, n) for n in names], [big("v_" + n, n) for n in names], block_rows, name, after)):
            record(n, outs)

    update(ffn, FF_S // 4, "adamw_ffn", [token])
    update(("w_out",), BIG_ADAM_ROWS["w_out"], "adamw_w_out", [token])
    (far["w_in"],) = _split_wait("rs_in_wait", x_in, *s_in, [new_v[n] for n in ("w_down", "w_out")])
    half["w_in"] = _rs_final_sum(gw["w_in"], sib["w_in"], far["w_in"], shard_core)
    ((joined["w_in"],),) = _comm_only("comm_rs_join_in", [_rs_join([half["w_in"]])])
    update(("w_in",), BIG_ADAM_ROWS["w_in"], "adamw_w_in", [])
    a[LOSS_ROW] = a["m_" + LOSS_ROW] = a["v_" + LOSS_ROW] = jnp.zeros((1, D_MODEL), F32)
    for names, rows, packed, slots in ((SMALL_EARLY, SMALL_ROWS, packed_early, slots_early),
                                       (SMALL_LATE, 8, packed_late, slots_late)):
        outs = _adamw_small(packed, slots, _pack(a, names, rows), _pack({n: a["m_" + n] for n in names}, names, rows),
                            _pack({n: a["v_" + n] for n in names}, names, rows), me)
        for dst, buf in zip((grads, deltas, new_m, new_v), outs):
            dst.update(_unpack(buf, names, {n: a[n].shape for n in names}))
    loss = jnp.sum(grads[LOSS_ROW]) * np.float32(0.5 / D_MODEL)
    return (loss, grad_x[None], *[grads[n] for n in WEIGHTS], *[deltas[n] for n in WEIGHTS],
            *[new_m[n] for n in WEIGHTS], *[new_v[n] for n in WEIGHTS])
```

```python
import numpy as np
import jax
import jax.numpy as jnp
from jax import lax
from jax.experimental import pallas as pl
from jax.experimental.pallas import tpu as pltpu

F32 = jnp.float32
BF16 = jnp.bfloat16

SEQ = 2048
D_MODEL = 1024
HEAD_DIM = 64
ATTN_W = 512
SGU_W = 512
SGU_GROUPS = 8
CHUNK = 128
DILATIONS = (1, 4, 16)
N_SHARD = 4
IN_S = 640
OUT_S = 256
FF_S = 704
PROJ_W = N_SHARD * IN_S
FF = N_SHARD * FF_S
FF_CHUNKS = ((0, 1024), (1024, 2048), (2048, FF))
RMS_EPS = 1e-6
LN_EPS = 1e-5
ROPE_THETA = 500000.0
ATTN_SCALE = 1.0 / np.sqrt(HEAD_DIM)
NEG = -1e30
TM = 512
TM_FFN = 256
VMEM_LIMIT = 56 * 1024 * 1024
SMALL_ROWS = 136

ADAM_LR = 0.001
ADAM_B1 = 0.9
ADAM_B2 = 0.999
ADAM_EPS = 1e-08
ADAM_WD = 0.01
ADAM_STEP = 10

MESH = pl.DeviceIdType.MESH
ANY = pl.BlockSpec(memory_space=pl.ANY)


def _dot(a, b):
    return jnp.dot(a, b, preferred_element_type=F32)


def _dot_nt(a, b):
    return lax.dot_general(a, b, (((1,), (1,)), ((), ())), preferred_element_type=F32)


def _dot_tn(a, b):
    return lax.dot_general(a, b, (((0,), (0,)), ((), ())), preferred_element_type=F32)


def _dot_exact(a, b):
    return jnp.dot(a, b, preferred_element_type=F32, precision=lax.Precision.HIGHEST)


def _rms_stats(x):
    r = lax.rsqrt(jnp.mean(x * x, axis=-1, keepdims=True) + RMS_EPS)
    return x * r, r


def _rms_bwd(xh, r, gain, dy):
    dxh = dy * gain
    dx = r * (dxh - xh * jnp.mean(dxh * xh, axis=-1, keepdims=True))
    return dx, jnp.sum(dy * xh, axis=0, keepdims=True)


_ERF_ALPHA = (-2.72614225801306e-10, 2.77068142495902e-08, -2.10102402082508e-06, -5.69250639462346e-05,
              -7.34990630326855e-04, -2.95459980854025e-03, -1.60960333262415e-02)
_ERF_BETA = (-1.45660718464996e-05, -2.13374055278905e-04, -1.68282697438203e-03, -7.37332916720468e-03,
             -1.42647390514189e-02)


def _erf(x):
    x = jnp.clip(x, -4.0, 4.0)
    x2 = x * x
    p = jnp.full_like(x, _ERF_ALPHA[0])
    for a in _ERF_ALPHA[1:]:
        p = p * x2 + a
    q = jnp.full_like(x, _ERF_BETA[0])
    for b in _ERF_BETA[1:]:
        q = q * x2 + b
    return x * p / q


def _gelu(x):
    return 0.5 * x * (1.0 + _erf(x * np.float32(1.0 / np.sqrt(2.0))))


def _gelu_grad(x):
    cdf = 0.5 * (1.0 + _erf(x * np.float32(1.0 / np.sqrt(2.0))))
    pdf = jnp.exp(-0.5 * x * x) * np.float32(1.0 / np.sqrt(2.0 * np.pi))
    return cdf + x * pdf


def _sigmoid(x):
    return 1.0 / (1.0 + jnp.exp(-x))


_INV_FREQ = tuple(float(np.float32(ROPE_THETA ** (-2.0 * j / 16.0))) for j in range(8))


def _rot_tables(pos):
    lane = lax.broadcasted_iota(jnp.int32, (1, 128), 1)
    d = lane & 63
    j = d & 7
    inv = jnp.zeros((1, 128), F32)
    for jj in range(8):
        inv = jnp.where(j == jj, _INV_FREQ[jj], inv)
    ang = pos.astype(F32) * inv
    c = jnp.cos(ang)
    s = jnp.sin(ang)
    cos_t = jnp.where(d < 16, c, 1.0)
    sin_a = jnp.where(d < 8, -s, 0.0)
    sin_b = jnp.where((d >= 8) & (d < 16), s, 0.0)
    return tuple(jnp.tile(t, (1, 4)) for t in (cos_t, sin_a, sin_b))


def _rope(x, tabs):
    cos_t, sin_a, sin_b = tabs
    return x * cos_t + pltpu.roll(x, 504, 1) * sin_a + pltpu.roll(x, 8, 1) * sin_b


def _rope_bwd(dy, tabs):
    cos_t, sin_a, sin_b = tabs
    return dy * cos_t + pltpu.roll(dy * sin_a, 8, 1) + pltpu.roll(dy * sin_b, 504, 1)


def _left_half():
    return lax.broadcasted_iota(jnp.int32, (CHUNK, CHUNK), 1) < HEAD_DIM


def _group_ones():
    lane = lax.broadcasted_iota(jnp.int32, (SGU_GROUPS, SGU_W), 1)
    row = lax.broadcasted_iota(jnp.int32, (SGU_GROUPS, SGU_W), 0)
    return ((lane >> 6) == row).astype(F32)


def _masked_spatial(w_ref):
    row = lax.broadcasted_iota(jnp.int32, (CHUNK, CHUNK), 0)
    col = lax.broadcasted_iota(jnp.int32, (CHUNK, CHUNK), 1)
    return [jnp.where(col <= row, w_ref[g], 0.0).astype(BF16) for g in range(SGU_GROUPS)]


def _sgu_core(u, vs, lg, lb, wm, bias_full):
    tm = u.shape[0]
    gu = _gelu(u)
    gv = _gelu(vs)
    mu = jnp.mean(gv, axis=-1, keepdims=True)
    xc = gv - mu
    rstd = lax.rsqrt(jnp.mean(xc * xc, axis=-1, keepdims=True) + LN_EPS)
    xh = xc * rstd
    vnb = (xh * lg + lb).astype(BF16)
    left = _left_half()
    rows = []
    for c in range(tm // CHUNK):
        pieces = []
        for p in range(4):
            vp = vnb[c * CHUNK:(c + 1) * CHUNK, p * 128:(p + 1) * 128]
            pieces.append(jnp.where(left, _dot(wm[2 * p], vp), _dot(wm[2 * p + 1], vp)))
        rows.append(jnp.concatenate(pieces, axis=1) + bias_full)
    mixed = jnp.concatenate(rows, axis=0)
    return gu, xh, rstd, vnb, mixed


def _resident(shape):
    n = len(shape)
    return pl.BlockSpec(shape, lambda *_: (0,) * n, pipeline_mode=pl.Buffered(1))


def _rows(ncol, tm=TM):
    return pl.BlockSpec((tm, ncol), lambda i: (i, 0))


def _rows3(nlead, ncol, tm=TM):
    return pl.BlockSpec((nlead, tm, ncol), lambda i: (0, i, 0))


def _acc(ncol, nrow=1):
    return pl.BlockSpec((nrow, ncol), lambda i: (0, 0))


HEAD_W = 128


def _view_rows(dil, width=ATTN_W, tm=TM):
    return pl.BlockSpec((tm // dil, dil * width), lambda i: (i, 0))


def _view_shape(dil, dtype, width=ATTN_W):
    return _sds((SEQ // dil, dil * width), dtype)


def _slab_scratch():
    return pltpu.VMEM((4, TM, 128), F32)


def _store_view(val, out_ref, slabs, dil):
    width = val.shape[1]
    for j in range(width // 128):
        slabs[j] = val[:, j * 128:(j + 1) * 128]
    for r in range(dil):
        for j in range(width // 128):
            c0 = r * width + j * 128
            out_ref[:, c0:c0 + 128] = slabs.at[j][pl.ds(r, TM // dil, stride=dil), :].astype(out_ref.dtype)


def _load_view(in_ref, slabs, dil, width=ATTN_W):
    for r in range(dil):
        for j in range(width // 128):
            c0 = r * width + j * 128
            slabs.at[j][pl.ds(r, TM // dil, stride=dil), :] = in_ref[:, c0:c0 + 128].astype(F32)
    return jnp.concatenate([slabs[j] for j in range(width // 128)], axis=1)


def _head_spread():
    m = lax.broadcasted_iota(jnp.int32, (HEAD_W, ATTN_W), 0)
    lane = lax.broadcasted_iota(jnp.int32, (HEAD_W, ATTN_W), 1)
    return (m == 16 * (lane >> 6)).astype(F32)


def _head_sum():
    lane = lax.broadcasted_iota(jnp.int32, (ATTN_W, HEAD_W), 0)
    m = lax.broadcasted_iota(jnp.int32, (ATTN_W, HEAD_W), 1)
    return ((lane >> 6) == (m >> 4)).astype(F32)


def _seq_params():
    return pltpu.CompilerParams(dimension_semantics=("arbitrary",), vmem_limit_bytes=VMEM_LIMIT)


def _sds(shape, dtype):
    return jax.ShapeDtypeStruct(shape, dtype)


class _Comm:
    def __init__(self, args, out_shape, n_sems, start, finish, aliased=False):
        self.args, self.out_shape, self.n_sems = list(args), list(out_shape), n_sems
        self.start, self.finish, self.aliased = start, finish, aliased


def _pcall(body, *, name, grid, in_specs, out_specs, out_shape, args, scratch_shapes=(), comms=(), after=()):
    single = not isinstance(out_shape, (list, tuple))
    out_specs = [out_specs] if single else list(out_specs)
    out_shape = [out_shape] if single else list(out_shape)
    n_in, n_out, n_scr = len(in_specs), len(out_shape), len(scratch_shapes)
    c_args = [a for c in comms for a in c.args]
    c_outs = [o for c in comms for o in c.out_shape]
    aliases, ai, ao = {}, n_in, n_out
    for c in comms:
        if c.aliased:
            aliases.update({ai + k: ao + k for k in range(len(c.args))})
        ai += len(c.args)
        ao += len(c.out_shape)
    sems = [pltpu.SemaphoreType.DMA((c.n_sems,)) for c in comms for _ in range(2)]
    steps = grid[0]

    def wrapped(*refs):
        o0 = n_in + len(c_args) + len(after)
        s0 = o0 + n_out + len(c_outs)
        m_in, m_out, m_sem = refs[n_in:n_in + len(c_args)], refs[o0 + n_out:s0], refs[s0 + n_scr:]

        def each(phase):
            ii = oi = 0
            for k, c in enumerate(comms):
                getattr(c, phase)(m_in[ii:ii + len(c.args)], m_out[oi:oi + len(c.out_shape)],
                                  m_sem[2 * k], m_sem[2 * k + 1])
                ii += len(c.args)
                oi += len(c.out_shape)

        if comms:
            @pl.when(pl.program_id(0) == 0)
            def _():
                each("start")

        body(*refs[:n_in], *refs[o0:o0 + n_out], *refs[s0:s0 + n_scr])

        if comms:
            @pl.when(pl.program_id(0) == steps - 1)
            def _():
                each("finish")

    res = pl.pallas_call(
        wrapped, name=name, grid=grid,
        in_specs=list(in_specs) + [ANY] * (len(c_args) + len(after)), out_specs=out_specs + [ANY] * len(c_outs),
        out_shape=out_shape + c_outs, scratch_shapes=list(scratch_shapes) + sems,
        input_output_aliases=aliases, compiler_params=_seq_params(),
    )(*args, *c_args, *after)
    mine = res[0] if single else list(res[:n_out])
    if not comms:
        return mine
    theirs, oi = [], n_out
    for c in comms:
        theirs.append(list(res[oi:oi + len(c.out_shape)]))
        oi += len(c.out_shape)
    return mine, theirs


def _inproj_fwd(x, pos, g_pre, w_in, comms=()):
    def body(x_ref, pos_ref, g_ref, w_ref, h_ref, u_ref, vs_ref, *rest):
        qkv_refs, slabs = rest[:9], rest[9:]
        xh, _ = _rms_stats(x_ref[...])
        h = (xh * g_ref[...]).astype(BF16)
        h_ref[...] = h
        proj = jnp.concatenate([_dot(h, w_ref[s]) for s in range(N_SHARD)], axis=1)
        tabs = _rot_tables(pos_ref[...])
        u_ref[...] = proj[:, 1536:2048]
        vs_ref[...] = proj[:, 2048:2560]
        qkv = (_rope(proj[:, 0:512], tabs) * np.float32(ATTN_SCALE), _rope(proj[:, 512:1024], tabs),
               proj[:, 1024:1536])
        for t, val in enumerate(qkv):
            qkv_refs[t][...] = val.astype(BF16)
            for i, dil in enumerate(DILATIONS[1:]):
                _store_view(val, qkv_refs[3 * (i + 1) + t], slabs[t], dil)

    return _pcall(
        body, name="inproj_fwd", grid=(SEQ // TM,),
        in_specs=[_rows(D_MODEL), _rows(1), _resident((1, D_MODEL)), _resident((N_SHARD, D_MODEL, IN_S))],
        out_specs=[_rows(D_MODEL), _rows(512), _rows(512)] + [_view_rows(dil) for dil in DILATIONS for _ in range(3)],
        out_shape=[_sds((SEQ, D_MODEL), BF16), _sds((SEQ, 512), F32), _sds((SEQ, 512), F32)]
        + [_view_shape(dil, BF16) for dil in DILATIONS for _ in range(3)],
        scratch_shapes=[_slab_scratch() for _ in range(3)],
        args=(x, pos, g_pre, w_in), comms=comms)


def _sgu_fwd(u, vs, lg, lb, w_sp, b_t, comms=()):
    def body(u_ref, vs_ref, lg_ref, lb_ref, w_ref, bt_ref, out_ref):
        wm = _masked_spatial(w_ref)
        bias_full = _dot_exact(bt_ref[...], _group_ones())
        gu, _, _, _, mixed = _sgu_core(u_ref[...], vs_ref[...], lg_ref[...], lb_ref[...], wm, bias_full)
        out_ref[...] = gu * mixed

    return _pcall(
        body, name="sgu_fwd", grid=(SEQ // TM,),
        in_specs=[_rows(SGU_W), _rows(SGU_W), _resident((1, SGU_W)), _resident((1, SGU_W)),
                  _resident((SGU_GROUPS, CHUNK, CHUNK)), _resident((CHUNK, SGU_GROUPS))],
        out_specs=_rows(SGU_W),
        out_shape=_sds((SEQ, SGU_W), F32),
        args=(u, vs, lg, lb, w_sp, b_t), comms=comms)


def _block_masks():
    row = lax.broadcasted_iota(jnp.int32, (CHUNK, CHUNK), 0)
    col = lax.broadcasted_iota(jnp.int32, (CHUNK, CHUNK), 1)
    return col <= row, col >= row


def _attn_fwd(qv, kv, vv, dil, comms=()):
    seg = SEQ // dil
    nblk = seg // CHUNK
    rps = 4 if nblk == 1 else 1

    def body(q_ref, k_ref, v_ref, o_ref, l_ref):
        left = _left_half()
        m_cur, m_prev = _block_masks()
        zero = jnp.zeros((CHUNK, CHUNK), BF16)
        ones = (jnp.where(left, 1.0, 0.0).astype(BF16), jnp.where(left, 0.0, 1.0).astype(BF16))

        def blk(rr, b, carry):
            r0 = pl.multiple_of(b * CHUNK, CHUNK)
            rp = pl.multiple_of(jnp.maximum(b - 1, 0) * CHUNK, CHUNK)
            prev_ok = m_prev & (b > 0)
            sides = tuple(enumerate((left, ~left)))
            tiles, scores = [], []
            for hp in range(4):
                ls = slice(rr * ATTN_W + hp * 128, rr * ATTN_W + (hp + 1) * 128)
                qp = q_ref[pl.ds(r0, CHUNK), ls]
                kc = k_ref[pl.ds(r0, CHUNK), ls]
                kp = k_ref[pl.ds(rp, CHUNK), ls] if nblk > 1 else None
                tiles.append((ls, v_ref[pl.ds(r0, CHUNK), ls], v_ref[pl.ds(rp, CHUNK), ls] if nblk > 1 else None))
                for _, hm in sides:
                    qh = jnp.where(hm, qp, zero)
                    sc = jnp.where(m_cur, _dot_nt(qh, kc), NEG)
                    sp = jnp.where(prev_ok, _dot_nt(qh, kp), NEG) if nblk > 1 else None
                    scores.append((sc, sp))
            probs = []
            for sc, sp in scores:
                if nblk > 1:
                    m = jnp.max(jnp.maximum(sc, sp), axis=-1, keepdims=True)
                    pc = jnp.exp(sc - m)
                    pp = jnp.exp(sp - m)
                    probs.append((m, pc.astype(BF16), pp.astype(BF16), (pc + pp).astype(BF16)))
                else:
                    m = jnp.max(sc, axis=-1, keepdims=True)
                    pc = jnp.exp(sc - m).astype(BF16)
                    probs.append((m, pc, None, pc))
            for hp, (ls, vc, vp) in enumerate(tiles):
                acc = jnp.zeros((CHUNK, CHUNK), F32)
                den = jnp.zeros((CHUNK, CHUNK), F32)
                for side, hm in sides:
                    _, pc, pp, psum = probs[2 * hp + side]
                    acc = acc + _dot(pc, jnp.where(hm, vc, zero))
                    if nblk > 1:
                        acc = acc + _dot(pp, jnp.where(hm, vp, zero))
                    den = den + _dot(psum, ones[side])
                o_ref[pl.ds(r0, CHUNK), ls] = (acc / den).astype(o_ref.dtype)
                lse = jnp.where(left, probs[2 * hp][0], probs[2 * hp + 1][0]) + jnp.log(den)
                l_ref[pl.ds(r0, CHUNK), rr * HEAD_W + 32 * hp:rr * HEAD_W + 32 * hp + 32] = lse[:, 48:80]
            return carry

        for rr in range(rps):
            lax.fori_loop(0, nblk, lambda b, c, rr=rr: blk(rr, b, c), 0)

    spec = pl.BlockSpec((seg, rps * ATTN_W), lambda r: (0, r))
    return _pcall(
        body, name=f"attn_fwd_d{dil}", grid=(dil // rps,),
        in_specs=[spec, spec, spec], out_specs=[spec, pl.BlockSpec((seg, rps * HEAD_W), lambda r: (0, r))],
        out_shape=[_sds((seg, dil * ATTN_W), BF16), _sds((seg, dil * HEAD_W), F32)],
        args=(qv, kv, vv), comms=comms)


def _mix_out_fwd(o_list, l_list, sgu, x, w_out, g_attn, g_sgu, g_post, comms=()):
    def body(o1, o2, o3, l1, l2, l3, sgu_ref, x_ref, w_ref, ga_ref, gs_ref, gp_ref,
             attn_ref, mixed_ref, y_ref, x1_ref, lse1_ref, lse2_ref, lse3_ref, slabs_a, slabs_b):
        os = [o1[...], _load_view(o2, slabs_a, DILATIONS[1]), _load_view(o3, slabs_b, DILATIONS[2])]
        ls = [l1[...], _load_view(l2, slabs_a, DILATIONS[1], HEAD_W), _load_view(l3, slabs_b, DILATIONS[2], HEAD_W)]
        m = jnp.maximum(jnp.maximum(ls[0], ls[1]), ls[2])
        es = [jnp.exp(l - m) for l in ls]
        den = es[0] + es[1] + es[2]
        spread = _head_spread()
        attn = sum(_dot_exact(e / den, spread) * o for e, o in zip(es, os))
        attn_ref[...] = attn
        lse = m + jnp.log(den)
        lse1_ref[...] = lse
        _store_view(lse, lse2_ref, slabs_a, DILATIONS[1])
        _store_view(lse, lse3_ref, slabs_b, DILATIONS[2])
        ah, _ = _rms_stats(attn)
        sh, _ = _rms_stats(sgu_ref[...])
        mixed = jnp.concatenate([ah * ga_ref[...], sh * gs_ref[...]], axis=1).astype(BF16)
        mixed_ref[...] = mixed
        y = _dot(mixed[:, 0:OUT_S], w_ref[0])
        for s in range(1, N_SHARD):
            y = y + _dot(mixed[:, s * OUT_S:(s + 1) * OUT_S], w_ref[s])
        y_ref[...] = y
        yh, _ = _rms_stats(y)
        x1_ref[...] = x_ref[...] + yh * gp_ref[...]

    return _pcall(
        body, name="mix_out_fwd", grid=(SEQ // TM,),
        in_specs=[_view_rows(dil) for dil in DILATIONS] + [_view_rows(dil, HEAD_W) for dil in DILATIONS]
        + [_rows(512), _rows(D_MODEL), _resident((N_SHARD, OUT_S, D_MODEL)),
           _resident((1, 512)), _resident((1, 512)), _resident((1, D_MODEL))],
        out_specs=[_rows(512), _rows(D_MODEL), _rows(D_MODEL), _rows(D_MODEL)]
        + [_view_rows(dil, HEAD_W) for dil in DILATIONS],
        out_shape=[_sds((SEQ, 512), F32), _sds((SEQ, D_MODEL), BF16), _sds((SEQ, D_MODEL), F32),
                   _sds((SEQ, D_MODEL), F32)] + [_view_shape(dil, F32, HEAD_W) for dil in DILATIONS],
        scratch_shapes=[_slab_scratch(), _slab_scratch()],
        args=(*o_list, *l_list, sgu, x, w_out, g_attn, g_sgu, g_post), comms=comms)


def _ffn_fwd_bwd(x1, target, w_gate, w_up, w_down, g_pre, g_post, comms=()):
    def body(x1_ref, t_ref, wg_ref, wu_ref, wd_ref, gpf_ref, gpo_ref,
             h2_ref, a_ref, dg_ref, dup_ref, df_ref, dx1_ref, loss_ref, dgpf_ref, dgpo_ref, g_scr, up_scr):
        @pl.when(pl.program_id(0) == 0)
        def _():
            loss_ref[...] = jnp.zeros_like(loss_ref)
            dgpf_ref[...] = jnp.zeros_like(dgpf_ref)
            dgpo_ref[...] = jnp.zeros_like(dgpo_ref)

        x1 = x1_ref[...]
        gpf = gpf_ref[...]
        gpo = gpo_ref[...]
        xh, r = _rms_stats(x1)
        h2 = (xh * gpf).astype(BF16)
        h2_ref[...] = h2
        f = jnp.zeros((TM_FFN, D_MODEL), F32)
        for c0, c1 in FF_CHUNKS:
            g = _dot_nt(h2, wg_ref[c0:c1, :])
            up = _dot_nt(h2, wu_ref[c0:c1, :])
            g_scr[:, c0:c1] = g
            up_scr[:, c0:c1] = up
            a = (g * _sigmoid(g) * up).astype(BF16)
            a_ref[:, c0:c1] = a
            f = f + _dot(a, wd_ref[c0:c1, :])
        fh, rf = _rms_stats(f)
        diff = x1 + fh * gpo - t_ref[...]
        loss_ref[...] += jnp.sum(diff * diff, axis=0, keepdims=True)
        dout = diff * np.float32(1.0 / D_MODEL)
        df, dgpo = _rms_bwd(fh, rf, gpo, dout)
        dgpo_ref[...] += dgpo
        dfb = df.astype(BF16)
        df_ref[...] = dfb
        dh2 = jnp.zeros((TM_FFN, D_MODEL), F32)
        for c0, c1 in FF_CHUNKS:
            da = _dot_nt(dfb, wd_ref[c0:c1, :])
            g = g_scr[:, c0:c1]
            up = up_scr[:, c0:c1]
            sg = _sigmoid(g)
            dup = (da * (g * sg)).astype(BF16)
            dg = (da * up * (sg * (1.0 + g * (1.0 - sg)))).astype(BF16)
            dg_ref[:, c0:c1] = dg
            dup_ref[:, c0:c1] = dup
            dh2 = dh2 + _dot(dg, wg_ref[c0:c1, :]) + _dot(dup, wu_ref[c0:c1, :])
        dx, dgpf = _rms_bwd(xh, r, gpf, dh2)
        dgpf_ref[...] += dgpf
        dx1_ref[...] = dout + dx

    return _pcall(
        body, name="ffn_fwd_bwd", grid=(SEQ // TM_FFN,),
        in_specs=[_rows(D_MODEL, TM_FFN), _rows(D_MODEL, TM_FFN), _resident((FF, D_MODEL)),
                  _resident((FF, D_MODEL)), _resident((FF, D_MODEL)),
                  _resident((1, D_MODEL)), _resident((1, D_MODEL))],
        out_specs=[_rows(D_MODEL, TM_FFN), _rows(FF, TM_FFN), _rows(FF, TM_FFN), _rows(FF, TM_FFN),
                   _rows(D_MODEL, TM_FFN), _rows(D_MODEL, TM_FFN), _acc(D_MODEL), _acc(D_MODEL), _acc(D_MODEL)],
        out_shape=[_sds((SEQ, D_MODEL), BF16), _sds((SEQ, FF), BF16), _sds((SEQ, FF), BF16),
                   _sds((SEQ, FF), BF16), _sds((SEQ, D_MODEL), BF16), _sds((SEQ, D_MODEL), F32),
                   _sds((1, D_MODEL), F32), _sds((1, D_MODEL), F32), _sds((1, D_MODEL), F32)],
        scratch_shapes=[pltpu.VMEM((TM_FFN, FF), F32), pltpu.VMEM((TM_FFN, FF), F32)],
        args=(x1, target, w_gate, w_up, w_down, g_pre, g_post), comms=comms)


def _wgrad(a, b, a_spec, b_spec, out_block, name, comms=()):
    def body(a_ref, b_ref, o_ref):
        av = a_ref[0] if len(a_ref.shape) == 3 else a_ref[...]
        bv = b_ref[0] if len(b_ref.shape) == 3 else b_ref[...]
        o_ref[0] = _dot_tn(av, bv)

    return _pcall(
        body, name=name, grid=(N_SHARD,),
        in_specs=[a_spec, b_spec],
        out_specs=pl.BlockSpec((1,) + out_block, lambda s: (s, 0, 0)),
        out_shape=_sds((N_SHARD,) + out_block, F32),
        args=(a, b), comms=comms)


def _outproj_bwd(dx1, y, attn, sgu, w_out, g_post, g_attn, g_sgu, comms=()):
    def body(dx1_ref, y_ref, attn_ref, sgu_ref, w_ref, gp_ref, ga_ref, gs_ref,
             dy_ref, dsgu_ref, dgp_ref, dga_ref, dgs_ref, *rest):
        dattn_refs, delta_refs, (slabs_a, slabs_b) = rest[0:3], rest[3:6], rest[6:]

        @pl.when(pl.program_id(0) == 0)
        def _():
            dgp_ref[...] = jnp.zeros_like(dgp_ref)
            dga_ref[...] = jnp.zeros_like(dga_ref)
            dgs_ref[...] = jnp.zeros_like(dgs_ref)

        yh, ry = _rms_stats(y_ref[...])
        dy, dgp = _rms_bwd(yh, ry, gp_ref[...], dx1_ref[...])
        dgp_ref[...] += dgp
        dyb = dy.astype(BF16)
        dy_ref[...] = dyb
        dmixed = jnp.concatenate([_dot_nt(dyb, w_ref[s]) for s in range(N_SHARD)], axis=1)
        attn = attn_ref[...]
        ah, ra = _rms_stats(attn)
        dattn, dga = _rms_bwd(ah, ra, ga_ref[...], dmixed[:, 0:512])
        dga_ref[...] += dga
        sh, rs = _rms_stats(sgu_ref[...])
        dsgu, dgs = _rms_bwd(sh, rs, gs_ref[...], dmixed[:, 512:1024])
        dgs_ref[...] += dgs
        dsgu_ref[...] = dsgu
        delta = _dot_exact(dattn * attn, _head_sum())
        dattn_refs[0][...] = dattn.astype(BF16)
        delta_refs[0][...] = delta
        for i, dil in enumerate(DILATIONS[1:]):
            _store_view(dattn, dattn_refs[i + 1], slabs_a, dil)
            _store_view(delta, delta_refs[i + 1], slabs_b, dil)

    return _pcall(
        body, name="outproj_bwd", grid=(SEQ // TM,),
        in_specs=[_rows(D_MODEL), _rows(D_MODEL), _rows(512), _rows(512), _resident((N_SHARD, OUT_S, D_MODEL)),
                  _resident((1, D_MODEL)), _resident((1, 512)), _resident((1, 512))],
        out_specs=[_rows(D_MODEL), _rows(512), _acc(D_MODEL), _acc(512), _acc(512)]
        + [_view_rows(dil) for dil in DILATIONS] + [_view_rows(dil, HEAD_W) for dil in DILATIONS],
        out_shape=[_sds((SEQ, D_MODEL), BF16), _sds((SEQ, 512), F32),
                   _sds((1, D_MODEL), F32), _sds((1, 512), F32), _sds((1, 512), F32)]
        + [_view_shape(dil, BF16) for dil in DILATIONS] + [_view_shape(dil, F32, HEAD_W) for dil in DILATIONS],
        scratch_shapes=[_slab_scratch(), _slab_scratch()],
        args=(dx1, y, attn, sgu, w_out, g_post, g_attn, g_sgu), comms=comms)


def _sgu_bwd(u, vs, dsgu, lg, lb, w_sp, b_t, comms=(), after=()):
    nsteps = SEQ // TM

    def body(u_ref, vs_ref, ds_ref, lg_ref, lb_ref, w_ref, bt_ref,
             du_ref, dvs_ref, dw_ref, db_ref, dlg_ref, dlb_ref, dbias_scr):
        i = pl.program_id(0)

        @pl.when(i == 0)
        def _():
            dw_ref[...] = jnp.zeros_like(dw_ref)
            dlg_ref[...] = jnp.zeros_like(dlg_ref)
            dlb_ref[...] = jnp.zeros_like(dlb_ref)
            dbias_scr[...] = jnp.zeros_like(dbias_scr)

        wm = _masked_spatial(w_ref)
        ones_g = _group_ones()
        bias_full = _dot_exact(bt_ref[...], ones_g)
        u = u_ref[...]
        vs = vs_ref[...]
        lg = lg_ref[...]
        gu, xh, rstd, vnb, mixed = _sgu_core(u, vs, lg, lb_ref[...], wm, bias_full)
        dsgu = ds_ref[...]
        du_ref[...] = (dsgu * mixed * _gelu_grad(u)).astype(BF16)
        dmixed = dsgu * gu
        left = _left_half()
        dvn_rows = []
        for c in range(TM // CHUNK):
            rs = slice(c * CHUNK, (c + 1) * CHUNK)
            dm_c = dmixed[rs, :]
            dbias_scr[...] += dm_c
            pieces = []
            for p in range(4):
                ls = slice(p * 128, (p + 1) * 128)
                dmp = dm_c[:, ls]
                vp = vnb[rs, ls]
                dmb = dmp.astype(BF16)
                zero = jnp.zeros_like(dmb)
                dw_ref[2 * p] += _dot_nt(jnp.where(left, dmb, zero), vp)
                dw_ref[2 * p + 1] += _dot_nt(jnp.where(left, zero, dmb), vp)
                pieces.append(jnp.where(left, _dot_tn(wm[2 * p], dmb), _dot_tn(wm[2 * p + 1], dmb)))
            dvn_rows.append(jnp.concatenate(pieces, axis=1))
        dvn = jnp.concatenate(dvn_rows, axis=0)
        dlg_ref[...] += jnp.sum(dvn * xh, axis=0, keepdims=True)
        dlb_ref[...] += jnp.sum(dvn, axis=0, keepdims=True)
        dxh = dvn * lg
        dgv = rstd * (dxh - jnp.mean(dxh, axis=-1, keepdims=True) - xh * jnp.mean(dxh * xh, axis=-1, keepdims=True))
        dvs_ref[...] = (dgv * _gelu_grad(vs)).astype(BF16)

        @pl.when(i == nsteps - 1)
        def _():
            row = lax.broadcasted_iota(jnp.int32, (CHUNK, CHUNK), 0)
            col = lax.broadcasted_iota(jnp.int32, (CHUNK, CHUNK), 1)
            for g in range(SGU_GROUPS):
                dw_ref[g] = jnp.where(col <= row, dw_ref[g], 0.0)
            db_ref[...] = lax.dot_general(ones_g, dbias_scr[...], (((1,), (1,)), ((), ())),
                                          preferred_element_type=F32, precision=lax.Precision.HIGHEST)

    return _pcall(
        body, name="sgu_bwd", grid=(nsteps,),
        in_specs=[_rows(SGU_W), _rows(SGU_W), _rows(SGU_W), _resident((1, SGU_W)), _resident((1, SGU_W)),
                  _resident((SGU_GROUPS, CHUNK, CHUNK)), _resident((CHUNK, SGU_GROUPS))],
        out_specs=[_rows(SGU_W), _rows(SGU_W), pl.BlockSpec((SGU_GROUPS, CHUNK, CHUNK), lambda i: (0, 0, 0)),
                   _acc(CHUNK, SGU_GROUPS), _acc(SGU_W), _acc(SGU_W)],
        out_shape=[_sds((SEQ, SGU_W), BF16), _sds((SEQ, SGU_W), BF16), _sds((SGU_GROUPS, CHUNK, CHUNK), F32),
                   _sds((SGU_GROUPS, CHUNK), F32), _sds((1, SGU_W), F32), _sds((1, SGU_W), F32)],
        scratch_shapes=[pltpu.VMEM((CHUNK, SGU_W), F32)],
        args=(u, vs, dsgu, lg, lb, w_sp, b_t), comms=comms, after=after)


def _attn_bwd(qv, kv, vv, dov, deltav, lsev, dil, comms=(), after=()):
    seg = SEQ // dil
    nblk = seg // CHUNK
    rps = 4 if nblk == 1 else 1

    def body(q_ref, k_ref, v_ref, do_ref, dl_ref, lse_ref, dq_ref, dk_ref, dv_ref, dk_wait, dv_wait):
        left = _left_half()
        m_cur, m_prev = _block_masks()

        def blk(rr, b, carry):
            r0 = pl.multiple_of(b * CHUNK, CHUNK)
            rp = pl.multiple_of(jnp.maximum(b - 1, 0) * CHUNK, CHUNK)
            prev_ok = m_prev & (b > 0)
            sides = tuple(enumerate((left, ~left)))
            zero = jnp.zeros((CHUNK, CHUNK), BF16)
            tiles, firsts = [], []
            for hp in range(4):
                ls = slice(rr * ATTN_W + hp * 128, rr * ATTN_W + (hp + 1) * 128)
                qp = q_ref[pl.ds(r0, CHUNK), ls]
                kc = k_ref[pl.ds(r0, CHUNK), ls]
                vc = v_ref[pl.ds(r0, CHUNK), ls]
                dop = do_ref[pl.ds(r0, CHUNK), ls]
                kp = k_ref[pl.ds(rp, CHUNK), ls] if nblk > 1 else None
                vp = v_ref[pl.ds(rp, CHUNK), ls] if nblk > 1 else None
                tiles.append((ls, kc, kp))
                for side, hm in sides:
                    qh = jnp.where(hm, qp, zero)
                    doh = jnp.where(hm, dop, zero)
                    cur = (_dot_nt(qh, kc), _dot_nt(doh, vc))
                    prev = (_dot_nt(qh, kp), _dot_nt(doh, vp)) if nblk > 1 else None
                    firsts.append((qh, doh, cur, prev))
            seconds = []
            for i, (qh, doh, cur, prev) in enumerate(firsts):
                c0 = rr * HEAD_W + 16 * i
                lse_h = lse_ref[pl.ds(r0, CHUNK), :][:, c0:c0 + 1]
                dl_h = dl_ref[pl.ds(r0, CHUNK), :][:, c0:c0 + 1]
                pc = jnp.exp(jnp.where(m_cur, cur[0] - lse_h, NEG))
                out = [pc.astype(BF16), (pc * (cur[1] - dl_h)).astype(BF16), None, None]
                if nblk > 1:
                    pp = jnp.exp(jnp.where(prev_ok, prev[0] - lse_h, NEG))
                    out[2:] = [pp.astype(BF16), (pp * (prev[1] - dl_h)).astype(BF16)]
                seconds.append(out)
            for hp, (ls, kc, kp) in enumerate(tiles):
                dq = jnp.zeros((CHUNK, CHUNK), F32)
                dkc = jnp.zeros((CHUNK, CHUNK), F32)
                dvc = jnp.zeros((CHUNK, CHUNK), F32)
                dkp = jnp.zeros((CHUNK, CHUNK), F32)
                dvp = jnp.zeros((CHUNK, CHUNK), F32)
                for side, hm in sides:
                    qh, doh, _, _ = firsts[2 * hp + side]
                    pcb, dsc, ppb, dsp = seconds[2 * hp + side]
                    dq = dq + _dot(dsc, jnp.where(hm, kc, zero))
                    dkc = dkc + _dot_tn(dsc, qh)
                    dvc = dvc + _dot_tn(pcb, doh)
                    if nblk > 1:
                        dq = dq + _dot(dsp, jnp.where(hm, kp, zero))
                        dkp = dkp + _dot_tn(dsp, qh)
                        dvp = dvp + _dot_tn(ppb, doh)
                dq_ref[pl.ds(r0, CHUNK), ls] = dq.astype(dq_ref.dtype)
                if nblk == 1:
                    dk_ref[pl.ds(r0, CHUNK), ls] = dkc.astype(dk_ref.dtype)
                    dv_ref[pl.ds(r0, CHUNK), ls] = dvc.astype(dv_ref.dtype)
                else:
                    @pl.when(b > 0)
                    def _():
                        dk_ref[pl.ds(rp, CHUNK), ls] = (dk_wait[:, ls] + dkp).astype(dk_ref.dtype)
                        dv_ref[pl.ds(rp, CHUNK), ls] = (dv_wait[:, ls] + dvp).astype(dv_ref.dtype)

                    dk_wait[:, ls] = dkc
                    dv_wait[:, ls] = dvc
            return carry

        for rr in range(rps):
            lax.fori_loop(0, nblk, lambda b, c, rr=rr: blk(rr, b, c), 0)
        if nblk > 1:
            last = (nblk - 1) * CHUNK
            dk_ref[last:last + CHUNK, :] = dk_wait[...].astype(dk_ref.dtype)
            dv_ref[last:last + CHUNK, :] = dv_wait[...].astype(dv_ref.dtype)

    spec = pl.BlockSpec((seg, rps * ATTN_W), lambda r: (0, r))
    return _pcall(
        body, name=f"attn_bwd_d{dil}", grid=(dil // rps,),
        in_specs=[spec] * 4 + [pl.BlockSpec((seg, rps * HEAD_W), lambda r: (0, r))] * 2, out_specs=[spec] * 3,
        out_shape=[_sds((seg, dil * ATTN_W), BF16)] * 3,
        scratch_shapes=[pltpu.VMEM((CHUNK, ATTN_W), F32), pltpu.VMEM((CHUNK, ATTN_W), F32)],
        args=(qv, kv, vv, dov, deltav, lsev), comms=comms, after=after)


def _inproj_bwd(dqs, dks, dvs, du, dvs_sgu, pos, x, dx1, w_in, g_pre, comms=()):
    def body(dq1, dq2, dq3, dk1, dk2, dk3, dv1, dv2, dv3, du_ref, dvs_ref, pos_ref, x_ref, dx1_ref, w_ref, g_ref,
             dproj_ref, gx_ref, dg_ref, slabs_a, slabs_b):
        @pl.when(pl.program_id(0) == 0)
        def _():
            dg_ref[...] = jnp.zeros_like(dg_ref)

        def total(r1, r2, r3):
            return r1[...] + _load_view(r2, slabs_a, DILATIONS[1]) + _load_view(r3, slabs_b, DILATIONS[2])

        tabs = _rot_tables(pos_ref[...])
        dproj_ref[:, 0:512] = _rope_bwd(total(dq1, dq2, dq3) * np.float32(ATTN_SCALE), tabs).astype(BF16)
        dproj_ref[:, 512:1024] = _rope_bwd(total(dk1, dk2, dk3), tabs).astype(BF16)
        dproj_ref[:, 1024:1536] = total(dv1, dv2, dv3).astype(BF16)
        dproj_ref[:, 1536:2048] = du_ref[...]
        dproj_ref[:, 2048:2560] = dvs_ref[...]
        dh = jnp.zeros((TM, D_MODEL), F32)
        for s in range(N_SHARD):
            dh = dh + _dot_nt(dproj_ref[:, s * IN_S:(s + 1) * IN_S], w_ref[s])
        g = g_ref[...]
        xh, r = _rms_stats(x_ref[...])
        dx, dg = _rms_bwd(xh, r, g, dh)
        dg_ref[...] += dg
        gx_ref[...] = dx1_ref[...] + dx

    return _pcall(
        body, name="inproj_bwd", grid=(SEQ // TM,),
        in_specs=[_view_rows(dil) for dil in DILATIONS] * 3
        + [_rows(512), _rows(512), _rows(1), _rows(D_MODEL), _rows(D_MODEL),
           _resident((N_SHARD, D_MODEL, IN_S)), _resident((1, D_MODEL))],
        out_specs=[_rows(PROJ_W), _rows(D_MODEL), _acc(D_MODEL)],
        out_shape=[_sds((SEQ, PROJ_W), BF16), _sds((SEQ, D_MODEL), F32), _sds((1, D_MODEL), F32)],
        scratch_shapes=[_slab_scratch(), _slab_scratch()],
        args=(*dqs, *dks, *dvs, du, dvs_sgu, pos, x, dx1, w_in, g_pre), comms=comms)


def _to_view(a, dil):
    return a if dil == 1 else a.reshape(SEQ // dil, dil * a.shape[1])


def _from_view(a, dil):
    return a if dil == 1 else a.reshape(SEQ, a.shape[1] // dil)


def _local_step(x, pos, target, w_in, w_out, w_gate, w_up, w_down, small):
    b_t = small["sgu_b_spatial"].T
    h, u, vs, *qkv = _inproj_fwd(x, pos, small["pre_mix_norm"], w_in)
    sgu = _sgu_fwd(u, vs, small["sgu_ln_gain"], small["sgu_ln_bias"], small["sgu_w_spatial"], b_t)
    views = [tuple(qkv[3 * i:3 * i + 3]) for i in range(len(DILATIONS))]
    o_list, l_list = [], []
    for dil, (qv, kv, vv) in zip(DILATIONS, views):
        o, l = _attn_fwd(qv, kv, vv, dil)
        o_list.append(o)
        l_list.append(l)
    attn, mixed, y, x1, *lses = _mix_out_fwd(o_list, l_list, sgu, x, w_out, small["attn_out_norm"],
                                             small["sgu_out_norm"], small["post_mix_norm"])
    h2, a, dg, dup, df, dx1, loss_cols, d_pre_ffn, d_post_ffn = _ffn_fwd_bwd(
        x1, target, w_gate, w_up, w_down, small["pre_ffn_norm"], small["post_ffn_norm"])

    full_tok = pl.BlockSpec((SEQ, D_MODEL), lambda s: (0, 0), pipeline_mode=pl.Buffered(1))
    ff_tok = pl.BlockSpec((1, SEQ, FF_S), lambda s: (s, 0, 0))
    gw_gate, gw_up = _wgrad_ff([dg, dup], h2, "wgrad_gate_up")
    (gw_down,) = _wgrad_ff([a], df, "wgrad_down")

    dy, dsgu, d_post_mix, d_attn_norm, d_sgu_norm, *dviews = _outproj_bwd(
        dx1, y, attn, sgu, w_out, small["post_mix_norm"], small["attn_out_norm"], small["sgu_out_norm"])
    gw_out = _wgrad(mixed, dy, pl.BlockSpec((SEQ, OUT_S), lambda s: (0, s)), full_tok, (OUT_S, D_MODEL), "wgrad_out")
    du, dvs_sgu, d_w_sp, d_b_sp, d_ln_gain, d_ln_bias = _sgu_bwd(
        u, vs, dsgu, small["sgu_ln_gain"], small["sgu_ln_bias"], small["sgu_w_spatial"], b_t)

    dqs, dks, dvs = [], [], []
    for i, (dil, (qv, kv, vv)) in enumerate(zip(DILATIONS, views)):
        dq, dk, dv = _attn_bwd(qv, kv, vv, dviews[i], dviews[3 + i], lses[i], dil)
        dqs.append(dq)
        dks.append(dk)
        dvs.append(dv)
    dproj, grad_x, d_pre_mix = _inproj_bwd(dqs, dks, dvs, du, dvs_sgu, pos, x, dx1, w_in, small["pre_mix_norm"])
    gw_in = _wgrad(h, dproj, full_tok, pl.BlockSpec((SEQ, IN_S), lambda s: (0, s)), (D_MODEL, IN_S), "wgrad_in")

    small_grads = {
        "pre_mix_norm": d_pre_mix, "sgu_ln_gain": d_ln_gain, "sgu_ln_bias": d_ln_bias, "sgu_w_spatial": d_w_sp,
        "sgu_b_spatial": d_b_sp, "attn_out_norm": d_attn_norm, "sgu_out_norm": d_sgu_norm,
        "post_mix_norm": d_post_mix, "pre_ffn_norm": d_pre_ffn, "post_ffn_norm": d_post_ffn,
    }
    return loss_cols, grad_x, (gw_in, gw_out, gw_gate, gw_up, gw_down), small_grads


def _coords():
    return lax.axis_index("x"), lax.axis_index("y"), lax.axis_index("c")


def _other_chips(x, y):
    return [(1 - x, y), (x, 1 - y), (1 - x, 1 - y)]


def _comm_call(body, name, n_in, out_shape, scratch_shapes):
    return pl.pallas_call(
        body, name=name, in_specs=[ANY] * n_in, out_specs=[ANY] * len(out_shape), out_shape=out_shape,
        scratch_shapes=scratch_shapes,
        compiler_params=pltpu.CompilerParams(has_side_effects=True),
    )


def _gather_weights(shards):
    n = len(shards)
    halves = [s.reshape(2, s.shape[0] // 2, s.shape[1]) for s in shards]

    def body(*refs):
        ins, outs = refs[:n], refs[n:2 * n]
        send_sems, recv_sems = refs[2 * n:]
        x, y, c = _coords()
        s_me = 2 * x + y
        chips = _other_chips(x, y)
        sibling = (x, y, 1 - c)

        def copy(k, w, shard, cc, to):
            src = ins[w].at[cc] if shard is None else outs[w].at[shard, cc]
            dst = outs[w].at[s_me if shard is None else shard, cc]
            return pltpu.make_async_remote_copy(src_ref=src, dst_ref=dst, send_sem=send_sems.at[k],
                                                recv_sem=recv_sems.at[k], device_id=to, device_id_type=MESH)

        first = [copy(j * n + w, w, None, c, (cx, cy, c)) for j, (cx, cy) in enumerate(chips) for w in range(n)]
        for cp in first:
            cp.start()
        passed = []
        for j, (cx, cy) in enumerate(chips):
            for w in range(n):
                copy(j * n + w, w, 2 * cx + cy, c, (x, y, c)).wait_recv()
                fw = copy((3 + j) * n + w, w, 2 * cx + cy, c, sibling)
                fw.start()
                passed.append(fw)
        for j, (cx, cy) in enumerate(chips):
            for w in range(n):
                copy((3 + j) * n + w, w, 2 * cx + cy, 1 - c, (x, y, c)).wait_recv()
        for cp in first + passed:
            cp.wait_send()

    out_shape = [_sds((N_SHARD,) + h.shape, h.dtype) for h in halves]
    scratch = [pltpu.SemaphoreType.DMA((6 * n,)), pltpu.SemaphoreType.DMA((6 * n,))]
    full = _comm_call(body, "comm_gather_weights", n, out_shape, scratch)(*halves)
    s_me = 2 * lax.axis_index("x") + lax.axis_index("y")
    full = [lax.dynamic_update_slice(f, h[None], (s_me, 0, 0, 0)) for f, h in zip(full, halves)]
    return [f.reshape((N_SHARD,) + s.shape) for f, s in zip(full, shards)]


def _rs_to_sibling(gws):
    n = len(gws)

    def body(*refs):
        ins, outs = refs[:n], refs[n:2 * n]
        send_sems, recv_sems = refs[2 * n:]
        x, y, c = _coords()
        copies = []
        for w in range(n):
            hw = gws[w].shape[1] // 2
            copies.append(pltpu.make_async_remote_copy(
                src_ref=ins[w].at[:, pl.ds((1 - c) * hw, hw), :], dst_ref=outs[w], send_sem=send_sems.at[w],
                recv_sem=recv_sems.at[w], device_id=(x, y, 1 - c), device_id_type=MESH))
        for cp in copies:
            cp.start()
        for cp in copies:
            cp.wait()

    out_shape = [_sds((N_SHARD, g.shape[1] // 2, g.shape[2]), g.dtype) for g in gws]
    scratch = [pltpu.SemaphoreType.DMA((n,)), pltpu.SemaphoreType.DMA((n,))]
    return _comm_call(body, "comm_rs_sibling", n, out_shape, scratch)(*gws)


def _rs_chip_sum(gw, recv, core):
    _, rows, cols = gw.shape
    hw = rows // 2

    def body(c_ref, g_ref, r_ref, o_ref):
        o_ref[...] = (g_ref[...] + r_ref[...]).astype(BF16)

    return pl.pallas_call(
        body, name="rs_chip_sum",
        grid_spec=pltpu.PrefetchScalarGridSpec(
            num_scalar_prefetch=1, grid=(N_SHARD,),
            in_specs=[pl.BlockSpec((1, hw, cols), lambda s, c_ref: (s, c_ref[0], 0)),
                      pl.BlockSpec((1, hw, cols), lambda s, c_ref: (s, 0, 0))],
            out_specs=pl.BlockSpec((1, hw, cols), lambda s, c_ref: (s, 0, 0))),
        out_shape=_sds((N_SHARD, hw, cols), BF16),
        compiler_params=_seq_params(),
    )(core, gw, recv)


def _rs_between_chips(pbs):
    n = len(pbs)

    def body(*refs):
        ins, outs = refs[:n], refs[n:2 * n]
        send_sems, recv_sems = refs[2 * n:]
        x, y, c = _coords()
        copies = []
        for j, (cx, cy) in enumerate(_other_chips(x, y)):
            for w in range(n):
                copies.append(pltpu.make_async_remote_copy(
                    src_ref=ins[w].at[2 * cx + cy], dst_ref=outs[w].at[j], send_sem=send_sems.at[j * n + w],
                    recv_sem=recv_sems.at[j * n + w], device_id=(cx, cy, c), device_id_type=MESH))
        for cp in copies:
            cp.start()
        for cp in copies:
            cp.wait()

    out_shape = [_sds((3,) + p.shape[1:], p.dtype) for p in pbs]
    scratch = [pltpu.SemaphoreType.DMA((3 * n,)), pltpu.SemaphoreType.DMA((3 * n,))]
    return _comm_call(body, "comm_rs_chips", n, out_shape, scratch)(*pbs)


def _rs_final_sum(gw, recv_sib, recv_chips, shard_core):
    _, rows, cols = gw.shape
    hw = rows // 2

    def body(sc_ref, g_ref, r_ref, rc_ref, o_ref):
        acc = g_ref[0] + r_ref[0]
        for j in range(3):
            acc = acc + rc_ref[j].astype(F32)
        o_ref[0] = acc

    return pl.pallas_call(
        body, name="rs_final_sum",
        grid_spec=pltpu.PrefetchScalarGridSpec(
            num_scalar_prefetch=1, grid=(1,),
            in_specs=[pl.BlockSpec((1, hw, cols), lambda i, sc: (sc[0], sc[1], 0)),
                      pl.BlockSpec((1, hw, cols), lambda i, sc: (sc[0], 0, 0)),
                      pl.BlockSpec((3, hw, cols), lambda i, sc: (0, 0, 0))],
            out_specs=pl.BlockSpec((1, hw, cols), lambda i, sc: (sc[1], 0, 0))),
        out_shape=_sds((2, hw, cols), F32),
        compiler_params=_seq_params(),
    )(shard_core, gw, recv_sib, recv_chips)


def _rs_join_halves(halves):
    n = len(halves)

    def body(*refs):
        bufs = refs[n:2 * n]
        send_sems, recv_sems = refs[2 * n:]
        x, y, c = _coords()
        remote = [pltpu.make_async_remote_copy(
            src_ref=bufs[w].at[c], dst_ref=bufs[w].at[c], send_sem=send_sems.at[w], recv_sem=recv_sems.at[w],
            device_id=(x, y, 1 - c), device_id_type=MESH) for w in range(n)]
        for cp in remote:
            cp.start()
        for w in range(n):
            remote[w].wait_send()
            pltpu.make_async_remote_copy(
                src_ref=bufs[w].at[c], dst_ref=bufs[w].at[1 - c], send_sem=send_sems.at[w],
                recv_sem=recv_sems.at[w], device_id=(x, y, c), device_id_type=MESH).wait_recv()

    joined = pl.pallas_call(
        body, name="comm_rs_join", in_specs=[ANY] * n, out_specs=[ANY] * n,
        out_shape=[_sds(h.shape, h.dtype) for h in halves], input_output_aliases={w: w for w in range(n)},
        scratch_shapes=[pltpu.SemaphoreType.DMA((n,)), pltpu.SemaphoreType.DMA((n,))],
        compiler_params=pltpu.CompilerParams(has_side_effects=True),
    )(*halves)
    return [j.reshape(2 * h.shape[1], h.shape[2]) for j, h in zip(joined, halves)]


def _allreduce_small(buf):
    rows, cols = buf.shape

    def body(in_ref, out_ref, slots, send_sems, recv_sems):
        x, y, c = _coords()
        me = 4 * x + 2 * y + c
        copies, peers = [], []
        for k in range(1, 8):
            px = 1 - x if (k >> 2) & 1 else x
            py = 1 - y if (k >> 1) & 1 else y
            pc = 1 - c if k & 1 else c
            peers.append(4 * px + 2 * py + pc)
            copies.append(pltpu.make_async_remote_copy(
                src_ref=in_ref, dst_ref=slots.at[me], send_sem=send_sems.at[k - 1], recv_sem=recv_sems.at[k - 1],
                device_id=(px, py, pc), device_id_type=MESH))
        for cp in copies:
            cp.start()
        slots[me] = in_ref[...]
        for k in range(7):
            pltpu.make_async_remote_copy(
                src_ref=in_ref, dst_ref=slots.at[peers[k]], send_sem=send_sems.at[k], recv_sem=recv_sems.at[k],
                device_id=(x, y, c), device_id_type=MESH).wait_recv()
        for cp in copies:
            cp.wait_send()
        acc = slots[0]
        for i in range(1, 8):
            acc = acc + slots[i]
        out_ref[...] = acc

    vmem = pl.BlockSpec(memory_space=pltpu.VMEM)
    return pl.pallas_call(
        body, name="comm_allreduce_small", in_specs=[vmem], out_specs=vmem, out_shape=_sds((rows, cols), F32),
        scratch_shapes=[pltpu.VMEM((8, rows, cols), F32), pltpu.SemaphoreType.DMA((7,)), pltpu.SemaphoreType.DMA((7,))],
        compiler_params=pltpu.CompilerParams(has_side_effects=True, vmem_limit_bytes=VMEM_LIMIT),
    )(buf)


def _adamw(w, g, m, v, block_rows, name, after=()):
    rows, cols = w.shape

    def body(w_ref, g_ref, m_ref, v_ref, *rest):
        d_ref, nm_ref, nv_ref = rest[len(after):]
        g = g_ref[...]
        m = ADAM_B1 * m_ref[...] + (1.0 - ADAM_B1) * g
        v = ADAM_B2 * v_ref[...] + (1.0 - ADAM_B2) * (g * g)
        m_hat = m / (1.0 - ADAM_B1 ** ADAM_STEP)
        v_hat = v / (1.0 - ADAM_B2 ** ADAM_STEP)
        d_ref[...] = -ADAM_LR * (m_hat / (jnp.sqrt(v_hat) + ADAM_EPS) + ADAM_WD * w_ref[...])
        nm_ref[...] = m
        nv_ref[...] = v

    spec = pl.BlockSpec((block_rows, cols), lambda i: (i, 0))
    return pl.pallas_call(
        body, name=name, grid=(rows // block_rows,), in_specs=[spec] * 4 + [ANY] * len(after), out_specs=[spec] * 3,
        out_shape=[_sds((rows, cols), F32)] * 3,
        compiler_params=_seq_params(),
    )(w, g, m, v, *after)


WEIGHTS = ("pre_mix_norm", "w_in", "sgu_ln_gain", "sgu_ln_bias", "sgu_w_spatial", "sgu_b_spatial", "attn_out_norm",
           "sgu_out_norm", "w_out", "post_mix_norm", "pre_ffn_norm", "w_gate", "w_up", "w_down", "post_ffn_norm")
BIG = ("w_in", "w_out", "w_gate", "w_up", "w_down")
BIG_ADAM_ROWS = {"w_in": 256, "w_out": 128, "w_gate": 352, "w_up": 352, "w_down": 352}
SMALL = ("pre_mix_norm", "post_mix_norm", "pre_ffn_norm", "post_ffn_norm", "sgu_ln_gain", "sgu_ln_bias",
         "attn_out_norm", "sgu_out_norm", "sgu_w_spatial", "sgu_b_spatial")


def _pack_small(d):
    flat = [d[n].reshape(-1) for n in SMALL]
    used = sum(f.shape[0] for f in flat)
    flat.append(jnp.zeros((SMALL_ROWS * 1024 - used,), F32))
    return jnp.concatenate(flat).reshape(SMALL_ROWS, 1024)


def _unpack_small(buf, shapes):
    flat = buf.reshape(-1)
    out, off = {}, 0
    for n in SMALL:
        size = int(np.prod(shapes[n]))
        out[n] = flat[off:off + size].reshape(shapes[n])
        off += size
    return out


def _kernel_unoverlapped(x, positions, pre_mix_norm, w_in, sgu_ln_gain, sgu_ln_bias, sgu_w_spatial, sgu_b_spatial, attn_out_norm, sgu_out_norm, w_out, post_mix_norm, pre_ffn_norm, w_gate, w_up, w_down, post_ffn_norm, loss_target, m_pre_mix_norm, m_w_in, m_sgu_ln_gain, m_sgu_ln_bias, m_sgu_w_spatial, m_sgu_b_spatial, m_attn_out_norm, m_sgu_out_norm, m_w_out, m_post_mix_norm, m_pre_ffn_norm, m_w_gate, m_w_up, m_w_down, m_post_ffn_norm, v_pre_mix_norm, v_w_in, v_sgu_ln_gain, v_sgu_ln_bias, v_sgu_w_spatial, v_sgu_b_spatial, v_attn_out_norm, v_sgu_out_norm, v_w_out, v_post_mix_norm, v_pre_ffn_norm, v_w_gate, v_w_up, v_w_down, v_post_ffn_norm):
    a = dict(locals())
    cx, cy, cc = _coords()
    core = jnp.stack([cc]).astype(jnp.int32)
    shard_core = jnp.stack([2 * cx + cy, cc]).astype(jnp.int32)

    full = _gather_weights([a[n][0].astype(BF16) for n in BIG])
    small = {n: (a[n][0] if a[n].ndim > 2 else a[n]) for n in SMALL}
    loss_cols, grad_x, gws, small_grads = _local_step(
        x[0], positions.reshape(SEQ, 1), loss_target[0], *full, small)
    loss = lax.psum(jnp.sum(loss_cols) * np.float32(0.5 / D_MODEL), ("x", "y", "c"))

    recv_sib = _rs_to_sibling(list(gws))
    chip_part = [_rs_chip_sum(g, r, core) for g, r in zip(gws, recv_sib)]
    recv_chips = _rs_between_chips(chip_part)
    halves = [_rs_final_sum(g, r, rc, shard_core) for g, r, rc in zip(gws, recv_sib, recv_chips)]
    big_grads = dict(zip(BIG, _rs_join_halves(halves)))

    shapes = {n: a[n].shape for n in SMALL}
    small_sum = _allreduce_small(_pack_small(small_grads))

    grads, deltas, new_m, new_v = {}, {}, {}, {}
    for n in BIG:
        grads[n] = big_grads[n][None]
        d, nm, nv = _adamw(a[n][0], big_grads[n], a["m_" + n][0], a["v_" + n][0], BIG_ADAM_ROWS[n], "adamw_" + n)
        deltas[n], new_m[n], new_v[n] = d[None], nm[None], nv[None]
    d, nm, nv = _adamw(_pack_small({n: a[n] for n in SMALL}), small_sum, _pack_small({n: a["m_" + n] for n in SMALL}),
                       _pack_small({n: a["v_" + n] for n in SMALL}), SMALL_ROWS, "adamw_small")
    grads.update(_unpack_small(small_sum, shapes))
    deltas.update(_unpack_small(d, shapes))
    new_m.update(_unpack_small(nm, shapes))
    new_v.update(_unpack_small(nv, shapes))
    return (loss, grad_x[None], *[grads[n] for n in WEIGHTS], *[deltas[n] for n in WEIGHTS],
            *[new_m[n] for n in WEIGHTS], *[new_v[n] for n in WEIGHTS])


def _remote(src, dst, send_sem, recv_sem, to):
    return pltpu.make_async_remote_copy(src_ref=src, dst_ref=dst, send_sem=send_sem, recv_sem=recv_sem,
                                        device_id=to, device_id_type=MESH)


def _halves(a):
    *lead, rows, cols = a.shape
    return a.reshape(*lead, 2, rows // 2, cols)


def _gather_ici(shards):
    n = len(shards)

    def desc(ins, outs, ss, rs, j, w, landed):
        x, y, c = _coords()
        cx, cy = _other_chips(x, y)[j]
        shard = 2 * cx + cy if landed else 2 * x + y
        return _remote(ins[w].at[c], outs[w].at[shard, c], ss.at[j * n + w], rs.at[j * n + w], (cx, cy, c))

    def start(ins, outs, ss, rs):
        for j in range(3):
            for w in range(n):
                desc(ins, outs, ss, rs, j, w, False).start()

    def finish(ins, outs, ss, rs):
        for j in range(3):
            for w in range(n):
                desc(ins, outs, ss, rs, j, w, True).wait_recv()
                desc(ins, outs, ss, rs, j, w, False).wait_send()

    return _Comm(shards, [_sds((N_SHARD,) + s.shape, s.dtype) for s in shards], 3 * n, start, finish)


def _gather_pass(fulls):
    n = len(fulls)

    def desc(bufs, ss, rs, j, w, landed):
        x, y, c = _coords()
        cx, cy = _other_chips(x, y)[j]
        shard = 2 * cx + cy
        return _remote(bufs[w].at[shard, c], bufs[w].at[shard, 1 - c if landed else c],
                       ss.at[j * n + w], rs.at[j * n + w], (x, y, 1 - c))

    def start(ins, outs, ss, rs):
        for j in range(3):
            for w in range(n):
                desc(outs, ss, rs, j, w, False).start()

    def finish(ins, outs, ss, rs):
        for j in range(3):
            for w in range(n):
                desc(outs, ss, rs, j, w, True).wait_recv()
                desc(outs, ss, rs, j, w, False).wait_send()

    return _Comm(fulls, [_sds(f.shape, f.dtype) for f in fulls], 3 * n, start, finish, aliased=True)


def _rs_sibling(gws):
    n = len(gws)

    def desc(ins, outs, ss, rs, w):
        x, y, c = _coords()
        return _remote(ins[w].at[:, 1 - c], outs[w], ss.at[w], rs.at[w], (x, y, 1 - c))

    def start(ins, outs, ss, rs):
        for w in range(n):
            desc(ins, outs, ss, rs, w).start()

    def finish(ins, outs, ss, rs):
        for w in range(n):
            desc(ins, outs, ss, rs, w).wait()

    out_shape = [_sds((N_SHARD, g.shape[1] // 2, g.shape[2]), g.dtype) for g in gws]
    return _Comm([_halves(g) for g in gws], out_shape, n, start, finish)


def _rs_chips(pbs):
    n = len(pbs)

    def desc(ins, outs, ss, rs, j, w):
        x, y, c = _coords()
        cx, cy = _other_chips(x, y)[j]
        return _remote(ins[w].at[2 * cx + cy], outs[w].at[j], ss.at[j * n + w], rs.at[j * n + w], (cx, cy, c))

    def start(ins, outs, ss, rs):
        for j in range(3):
            for w in range(n):
                desc(ins, outs, ss, rs, j, w).start()

    def finish(ins, outs, ss, rs):
        for j in range(3):
            for w in range(n):
                desc(ins, outs, ss, rs, j, w).wait()

    return _Comm(pbs, [_sds((3,) + p.shape[1:], p.dtype) for p in pbs], 3 * n, start, finish)


def _rs_join(halves):
    n = len(halves)

    def desc(bufs, ss, rs, w, landed):
        x, y, c = _coords()
        return _remote(bufs[w].at[c], bufs[w].at[1 - c if landed else c], ss.at[w], rs.at[w], (x, y, 1 - c))

    def start(ins, outs, ss, rs):
        for w in range(n):
            desc(outs, ss, rs, w, False).start()

    def finish(ins, outs, ss, rs):
        for w in range(n):
            desc(outs, ss, rs, w, True).wait_recv()
            desc(outs, ss, rs, w, False).wait_send()

    return _Comm(halves, [_sds(h.shape, h.dtype) for h in halves], n, start, finish, aliased=True)


def _small_exchange(buf):
    def desc(ins, outs, ss, rs, k, landed):
        x, y, c = _coords()
        px = 1 - x if (k >> 2) & 1 else x
        py = 1 - y if (k >> 1) & 1 else y
        pc = 1 - c if k & 1 else c
        slot = 4 * px + 2 * py + pc if landed else 4 * x + 2 * y + c
        return _remote(ins[0], outs[0].at[slot], ss.at[k - 1], rs.at[k - 1], (px, py, pc))

    def start(ins, outs, ss, rs):
        for k in range(1, 8):
            desc(ins, outs, ss, rs, k, False).start()

    def finish(ins, outs, ss, rs):
        for k in range(1, 8):
            desc(ins, outs, ss, rs, k, True).wait_recv()
            desc(ins, outs, ss, rs, k, False).wait_send()

    return _Comm([buf], [_sds((8,) + buf.shape, buf.dtype)], 7, start, finish)


HBM = pl.BlockSpec(memory_space=pltpu.HBM)
SEM = pl.BlockSpec(memory_space=pltpu.SEMAPHORE)
DATAFLOW = pltpu.SideEffectType.DATAFLOW_SIDE_EFFECTING


def _split_start(name, comm, lands, after):
    srcs = [pltpu.with_memory_space_constraint(s, pltpu.HBM) for s in comm.args]
    lands = [pltpu.with_memory_space_constraint(b, pltpu.HBM) for b in lands]
    ns, nb = len(srcs), len(lands)

    def body(*refs):
        send_sems, recv_sems = refs[ns + nb + 1], refs[ns + nb + 2]
        comm.start(refs[:ns], refs[ns:ns + nb], send_sems, recv_sems)
        refs[-1][...] = jnp.zeros_like(refs[-1])

    res = pl.pallas_call(
        body, name=name,
        out_shape=(pltpu.SemaphoreType.DMA((comm.n_sems,)), pltpu.SemaphoreType.DMA((comm.n_sems,)),
                   *[pltpu.HBM(b.shape, b.dtype) for b in srcs + lands], _sds((8, 128), F32)),
        in_specs=[HBM] * (ns + nb) + [ANY],
        out_specs=(SEM, SEM, *[HBM] * (ns + nb), pl.BlockSpec(memory_space=pltpu.VMEM)),
        input_output_aliases={i: 2 + i for i in range(ns + nb)},
        compiler_params=pltpu.CompilerParams(has_side_effects=DATAFLOW),
    )(*srcs, *lands, after)
    return res[0], res[1], list(res[2:2 + ns]), list(res[2 + ns:2 + ns + nb]), res[-1]


def _split_starts(name, comms, after):
    srcs = [[pltpu.with_memory_space_constraint(s, pltpu.HBM) for s in c.args] for c in comms]
    lands = [[pltpu.with_memory_space_constraint(lax.empty(o.shape, o.dtype), pltpu.HBM) for o in c.out_shape]
             for c in comms]
    bufs = [b for k in range(len(comms)) for b in srcs[k] + lands[k]]
    nb, nc = len(bufs), len(comms)

    def body(*refs):
        sems = refs[nb + 1:nb + 1 + 2 * nc]
        off = 0
        for k, c in enumerate(comms):
            ns, nl = len(srcs[k]), len(lands[k])
            c.start(refs[off:off + ns], refs[off + ns:off + ns + nl], sems[2 * k], sems[2 * k + 1])
            off += ns + nl
        refs[-1][...] = jnp.zeros_like(refs[-1])

    res = pl.pallas_call(
        body, name=name,
        out_shape=(*[pltpu.SemaphoreType.DMA((c.n_sems,)) for c in comms for _ in range(2)],
                   *[pltpu.HBM(b.shape, b.dtype) for b in bufs], _sds((8, 128), F32)),
        in_specs=[HBM] * nb + [ANY],
        out_specs=(*[SEM] * (2 * nc), *[HBM] * nb, pl.BlockSpec(memory_space=pltpu.VMEM)),
        input_output_aliases={i: 2 * nc + i for i in range(nb)},
        compiler_params=pltpu.CompilerParams(has_side_effects=DATAFLOW),
    )(*bufs, after)
    states, off = [], 2 * nc
    for k in range(nc):
        ns, nl = len(srcs[k]), len(lands[k])
        states.append((res[2 * k], res[2 * k + 1], list(res[off:off + ns]), list(res[off + ns:off + ns + nl])))
        off += ns + nl
    return states, res[-1]


def _split_wait(name, comm, send_sems, recv_sems, srcs, lands, after):
    ns, nb = len(srcs), len(lands)
    after = list(after) if isinstance(after, (list, tuple)) else [after]

    def body(*refs):
        comm.finish(refs[:ns], refs[ns:ns + nb], refs[ns + nb], refs[ns + nb + 1])

    res = pl.pallas_call(
        body, name=name,
        out_shape=tuple(pltpu.HBM(b.shape, b.dtype) for b in srcs + lands),
        in_specs=[HBM] * (ns + nb) + [SEM, SEM] + [ANY] * len(after), out_specs=tuple([HBM] * (ns + nb)),
        input_output_aliases={i: i for i in range(ns + nb)},
        compiler_params=pltpu.CompilerParams(has_side_effects=DATAFLOW),
    )(*srcs, *lands, send_sems, recv_sems, *after)
    return list(res[ns:])


LOSS_ROW = "loss_cols"
SMALL_EARLY = ("post_mix_norm", "pre_ffn_norm", "post_ffn_norm", "sgu_ln_gain", "sgu_ln_bias", "attn_out_norm",
               "sgu_out_norm", "sgu_w_spatial", "sgu_b_spatial", LOSS_ROW)
SMALL_LATE = ("pre_mix_norm",)


def _pack(d, names, rows):
    flat = [d[n].reshape(-1) for n in names]
    used = sum(f.shape[0] for f in flat)
    flat.append(jnp.zeros((rows * 1024 - used,), F32))
    return jnp.concatenate(flat).reshape(rows, 1024)


def _unpack(buf, names, shapes):
    flat = buf.reshape(-1)
    out, off = {}, 0
    for n in names:
        size = int(np.prod(shapes[n]))
        out[n] = flat[off:off + size].reshape(shapes[n])
        off += size
    return out


def _merge(comms):
    def run(phase):
        def go(ins, outs, ss, rs):
            ii = oi = si = 0
            for c in comms:
                getattr(c, phase)(ins[ii:ii + len(c.args)], outs[oi:oi + len(c.out_shape)],
                                  ss.at[pl.ds(si, c.n_sems)], rs.at[pl.ds(si, c.n_sems)])
                ii += len(c.args)
                oi += len(c.out_shape)
                si += c.n_sems
        return go

    return _Comm([a for c in comms for a in c.args], [o for c in comms for o in c.out_shape],
                 sum(c.n_sems for c in comms), run("start"), run("finish"))


def _chip_sums(gws, recvs, shard_core):
    n = len(gws)
    _, rows, cols = gws[0].shape
    hw = rows // 2

    def body(sc_ref, *refs):
        for k in range(n):
            refs[2 * n + k][...] = (refs[k][...] + refs[n + k][...]).astype(BF16)

    def other(s, sc):
        return jnp.where(s >= sc[0], s + 1, s)

    mine = pl.BlockSpec((1, hw, cols), lambda s, sc: (other(s, sc), sc[1], 0))
    plain = pl.BlockSpec((1, hw, cols), lambda s, sc: (other(s, sc), 0, 0))
    return pl.pallas_call(
        body, name="rs_chip_sums",
        grid_spec=pltpu.PrefetchScalarGridSpec(num_scalar_prefetch=1, grid=(N_SHARD - 1,),
                                               in_specs=[mine] * n + [plain] * n, out_specs=[plain] * n),
        out_shape=[_sds((N_SHARD, hw, cols), BF16)] * n,
        compiler_params=_seq_params(),
    )(shard_core, *gws, *recvs)


def _final_sums(gws, recv_sibs, recv_chips, shard_core):
    n = len(gws)
    _, rows, cols = gws[0].shape
    hw = rows // 2

    def body(sc_ref, *refs):
        for k in range(n):
            acc = refs[k][0] + refs[n + k][0]
            for j in range(3):
                acc = acc + refs[2 * n + k][j].astype(F32)
            refs[3 * n + k][0] = acc

    return pl.pallas_call(
        body, name="rs_final_sums",
        grid_spec=pltpu.PrefetchScalarGridSpec(
            num_scalar_prefetch=1, grid=(1,),
            in_specs=[pl.BlockSpec((1, hw, cols), lambda i, sc: (sc[0], sc[1], 0))] * n
            + [pl.BlockSpec((1, hw, cols), lambda i, sc: (sc[0], 0, 0))] * n
            + [pl.BlockSpec((3, hw, cols), lambda i, sc: (0, 0, 0))] * n,
            out_specs=[pl.BlockSpec((1, hw, cols), lambda i, sc: (sc[1], 0, 0))] * n),
        out_shape=[_sds((2, hw, cols), F32)] * n,
        compiler_params=_seq_params(),
    )(shard_core, *gws, *recv_sibs, *recv_chips)


def _adamw_multi(ws, gs, ms, vs, block_rows, name, after=()):
    n = len(ws)
    rows, cols = ws[0].shape

    def body(*refs):
        outs = refs[4 * n + len(after):]
        for k in range(n):
            g = refs[n + k][...]
            d, m, v = _adam_math(refs[k][...], g, refs[2 * n + k][...], refs[3 * n + k][...])
            outs[4 * k][...], outs[4 * k + 1][...], outs[4 * k + 2][...], outs[4 * k + 3][...] = g, d, m, v

    spec = pl.BlockSpec((block_rows, cols), lambda i: (i, 0))
    res = pl.pallas_call(
        body, name=name, grid=(rows // block_rows,), in_specs=[spec] * (4 * n) + [ANY] * len(after),
        out_specs=[spec] * (4 * n), out_shape=[_sds((rows, cols), F32)] * (4 * n),
        compiler_params=_seq_params(),
    )(*ws, *gs, *ms, *vs, *after)
    return [tuple(res[4 * k:4 * k + 4]) for k in range(n)]


def _wgrad_ff(a_list, b, name, comms=()):
    n = len(a_list)
    cols = 256

    def body(*refs):
        bv = refs[n][...]
        for k in range(n):
            refs[n + 1 + k][...] = _dot_tn(refs[k][...], bv)

    res = _pcall(
        body, name=name, grid=(FF // cols,),
        in_specs=[pl.BlockSpec((SEQ, cols), lambda j: (0, j))] * n
        + [pl.BlockSpec((SEQ, D_MODEL), lambda j: (0, 0), pipeline_mode=pl.Buffered(1))],
        out_specs=[pl.BlockSpec((cols, D_MODEL), lambda j: (j, 0))] * n,
        out_shape=[_sds((FF, D_MODEL), F32)] * n, args=(*a_list, b), comms=comms)
    mine, theirs = res if comms else (res, None)
    mine = [m.reshape(N_SHARD, FF_S, D_MODEL) for m in mine]
    return (mine, theirs) if comms else mine


def _wgrad_pair(a1, a2, b, a_spec, b_spec, out_block, name, comms=()):
    def body(a1_ref, a2_ref, b_ref, o1_ref, o2_ref):
        bv = b_ref[...]
        o1_ref[0] = _dot_tn(a1_ref[0], bv)
        o2_ref[0] = _dot_tn(a2_ref[0], bv)

    out_spec = pl.BlockSpec((1,) + out_block, lambda s: (s, 0, 0))
    return _pcall(
        body, name=name, grid=(N_SHARD,), in_specs=[a_spec, a_spec, b_spec], out_specs=[out_spec, out_spec],
        out_shape=[_sds((N_SHARD,) + out_block, F32)] * 2, args=(a1, a2, b), comms=comms)


def _comm_only(name, comms):
    return _pcall(lambda: None, name=name, grid=(1,), in_specs=[], out_specs=[], out_shape=[], args=(),
                  comms=comms)[1]


def _adam_math(w, g, m, v):
    m = ADAM_B1 * m + (1.0 - ADAM_B1) * g
    v = ADAM_B2 * v + (1.0 - ADAM_B2) * (g * g)
    m_hat = m / (1.0 - ADAM_B1 ** ADAM_STEP)
    v_hat = v / (1.0 - ADAM_B2 ** ADAM_STEP)
    return -ADAM_LR * (m_hat / (jnp.sqrt(v_hat) + ADAM_EPS) + ADAM_WD * w), m, v


def _adamw_small(own, slots, w, m, v, me):
    rows, cols = own.shape

    def body(me_ref, own_ref, slots_ref, w_ref, m_ref, v_ref, g_ref, d_ref, nm_ref, nv_ref):
        own_v = own_ref[...]
        g = jnp.where(me_ref[0] == 0, own_v, slots_ref[0])
        for i in range(1, 8):
            g = g + jnp.where(me_ref[0] == i, own_v, slots_ref[i])
        g_ref[...] = g
        d_ref[...], nm_ref[...], nv_ref[...] = _adam_math(w_ref[...], g, m_ref[...], v_ref[...])

    flat = pl.BlockSpec((rows, cols), lambda i, me_ref: (0, 0))
    return pl.pallas_call(
        body, name="adamw_small",
        grid_spec=pltpu.PrefetchScalarGridSpec(
            num_scalar_prefetch=1, grid=(1,),
            in_specs=[flat, pl.BlockSpec((8, rows, cols), lambda i, me_ref: (0, 0, 0)), flat, flat, flat],
            out_specs=[flat] * 4),
        out_shape=[_sds((rows, cols), F32)] * 4,
        compiler_params=_seq_params(),
    )(me, own, slots, w, m, v)


def kernel(x, positions, pre_mix_norm, w_in, sgu_ln_gain, sgu_ln_bias, sgu_w_spatial, sgu_b_spatial, attn_out_norm, sgu_out_norm, w_out, post_mix_norm, pre_ffn_norm, w_gate, w_up, w_down, post_ffn_norm, loss_target, m_pre_mix_norm, m_w_in, m_sgu_ln_gain, m_sgu_ln_bias, m_sgu_w_spatial, m_sgu_b_spatial, m_attn_out_norm, m_sgu_out_norm, m_w_out, m_post_mix_norm, m_pre_ffn_norm, m_w_gate, m_w_up, m_w_down, m_post_ffn_norm, v_pre_mix_norm, v_w_in, v_sgu_ln_gain, v_sgu_ln_bias, v_sgu_w_spatial, v_sgu_b_spatial, v_attn_out_norm, v_sgu_out_norm, v_w_out, v_post_mix_norm, v_pre_ffn_norm, v_w_gate, v_w_up, v_w_down, v_post_ffn_norm):
    a = dict(locals())
    cx, cy, cc = _coords()
    s_me = 2 * cx + cy
    core = jnp.stack([cc]).astype(jnp.int32)
    shard_core = jnp.stack([s_me, cc]).astype(jnp.int32)
    me = jnp.stack([4 * cx + 2 * cy + cc]).astype(jnp.int32)
    small = {n: (a[n][0] if a[n].ndim > 2 else a[n]) for n in SMALL}
    b_t = small["sgu_b_spatial"].T
    xs, pos, target = x[0], positions.reshape(SEQ, 1), loss_target[0]
    flipped = ("w_gate", "w_up")

    def big(name, n):
        return jnp.swapaxes(a[name], 1, 2)[0] if n in flipped else a[name][0]

    own = {n: _halves(big(n, n).astype(BF16)) for n in BIG}

    def with_own(full, n):
        full = lax.dynamic_update_slice(full, own[n][None], (s_me, 0, 0, 0))
        return full.reshape((N_SHARD,) + big(n, n).shape)

    ffn = ("w_gate", "w_up", "w_down")
    g_in, g_out, g_ffn = _gather_ici([own["w_in"]]), _gather_ici([own["w_out"]]), _gather_ici([own[n] for n in ffn])
    (s_in, s_out, s_ffn), token = _split_starts("gather_start", [g_in, g_out, g_ffn], small["pre_mix_norm"])
    in_lands = _split_wait("gather_in_wait", g_in, *s_in, token)
    ((in_lands,),) = _comm_only("comm_pass_in", [_gather_pass(in_lands)])
    w_in_f = with_own(in_lands, "w_in")
    h, u, vs, *qkv = _inproj_fwd(xs, pos, small["pre_mix_norm"], w_in_f)
    views = [tuple(qkv[3 * i:3 * i + 3]) for i in range(len(DILATIONS))]
    o_list, l_list = [], []
    for dil, (qv, kv, vv) in zip(DILATIONS, views):
        o, l = _attn_fwd(qv, kv, vv, dil)
        o_list.append(o)
        l_list.append(l)
    out_lands = _split_wait("gather_out_wait", g_out, *s_out, l_list[-1])
    sgu, ((out_lands,),) = _sgu_fwd(u, vs, small["sgu_ln_gain"], small["sgu_ln_bias"], small["sgu_w_spatial"], b_t,
                                    comms=[_gather_pass(out_lands)])
    w_out_f = with_own(out_lands, "w_out")
    ffn_lands = _split_wait("gather_ffn_wait", g_ffn, *s_ffn, sgu)
    (attn, mixed, y, x1, *lses), (ffn_lands,) = _mix_out_fwd(
        o_list, l_list, sgu, xs, w_out_f, small["attn_out_norm"], small["sgu_out_norm"], small["post_mix_norm"],
        comms=[_gather_pass(ffn_lands)])
    w_gate_f, w_up_f, w_down_f = (with_own(f, n).reshape(FF, D_MODEL) for f, n in zip(ffn_lands, ffn))
    h2, act, dg, dup, df, dx1, loss_cols, d_pre_ffn, d_post_ffn = _ffn_fwd_bwd(
        x1, target, w_gate_f, w_up_f, w_down_f, small["pre_ffn_norm"], small["post_ffn_norm"])

    full_tok = pl.BlockSpec((SEQ, D_MODEL), lambda s: (0, 0), pipeline_mode=pl.Buffered(1))
    ff_tok = pl.BlockSpec((1, SEQ, FF_S), lambda s: (s, 0, 0))
    gw = {}
    gw["w_gate"], gw["w_up"] = _wgrad_ff([dg, dup], h2, "wgrad_gate_up")
    (gw["w_down"],), ((sib_gate,),) = _wgrad_ff([act], df, "wgrad_down", comms=[_rs_sibling([gw["w_gate"]])])
    (dy, dsgu, d_post_mix, d_attn_norm, d_sgu_norm, *dviews), ((sib_up, sib_down),) = _outproj_bwd(
        dx1, y, attn, sgu, w_out_f, small["post_mix_norm"], small["attn_out_norm"],
        small["sgu_out_norm"], comms=[_rs_sibling([gw["w_up"], gw["w_down"]])])
    sib = {"w_gate": sib_gate, "w_up": sib_up, "w_down": sib_down}
    part = dict(zip(ffn, _chip_sums([gw[n] for n in ffn], [sib[n] for n in ffn], shard_core)))
    gw["w_out"] = _wgrad(mixed, dy, pl.BlockSpec((SEQ, OUT_S), lambda s: (0, s)), full_tok, (OUT_S, D_MODEL),
                         "wgrad_out")
    x_ffn = _rs_chips([part[n] for n in ffn])
    (s_ffn,), token = _split_starts("rs_ffn_start", [x_ffn], small["pre_mix_norm"])
    (du, dvs_sgu, d_w_sp, d_b_sp, d_ln_gain, d_ln_bias), ((sib["w_out"],),) = _sgu_bwd(
        u, vs, dsgu, small["sgu_ln_gain"], small["sgu_ln_bias"], small["sgu_w_spatial"], b_t,
        comms=[_rs_sibling([gw["w_out"]])], after=[token])
    (part["w_out"],) = _chip_sums([gw["w_out"]], [sib["w_out"]], shard_core)
    packed_early = _pack({
        "sgu_ln_gain": d_ln_gain, "sgu_ln_bias": d_ln_bias, "sgu_w_spatial": d_w_sp, "sgu_b_spatial": d_b_sp,
        "attn_out_norm": d_attn_norm, "sgu_out_norm": d_sgu_norm, "post_mix_norm": d_post_mix,
        "pre_ffn_norm": d_pre_ffn, "post_ffn_norm": d_post_ffn, LOSS_ROW: loss_cols}, SMALL_EARLY, SMALL_ROWS)
    x_out, x_small = _rs_chips([part["w_out"]]), _small_exchange(packed_early)
    (s_out, s_small), token = _split_starts("rs_out_small_start", [x_out, x_small], token)

    dqs, dks, dvs = [], [], []
    for i, (dil, (qv, kv, vv)) in enumerate(zip(DILATIONS, views)):
        dq, dk, dv = _attn_bwd(qv, kv, vv, dviews[i], dviews[3 + i], lses[i], dil, after=[token])
        dqs.append(dq)
        dks.append(dk)
        dvs.append(dv)
    half, far, joined = {}, {}, {}
    far.update(zip(ffn, _split_wait("rs_ffn_wait", x_ffn, *s_ffn, dvs[-1])))
    half.update(zip(ffn, _final_sums([gw[n] for n in ffn], [sib[n] for n in ffn], [far[n] for n in ffn],
                                     shard_core)))
    dproj, grad_x, d_pre_mix = _inproj_bwd(dqs, dks, dvs, du, dvs_sgu, pos, xs, dx1, w_in_f, small["pre_mix_norm"])
    (far["w_out"],) = _split_wait("rs_out_wait", x_out, *s_out, grad_x)
    (slots_early,) = _split_wait("small_early_wait", x_small, *s_small, far["w_out"])
    half["w_out"] = _rs_final_sum(gw["w_out"], sib["w_out"], far["w_out"], shard_core)
    packed_late = _pack({"pre_mix_norm": d_pre_mix}, SMALL_LATE, 8)
    names = ffn + ("w_out",)
    gw["w_in"], (got, (slots_late,)) = _wgrad(
        h, dproj, full_tok, pl.BlockSpec((SEQ, IN_S), lambda s: (0, s)), (D_MODEL, IN_S), "wgrad_in",
        comms=[_rs_join([half[n] for n in names]), _small_exchange(packed_late)])
    joined.update(zip(names, got))
    ((sib["w_in"],),) = _comm_only("comm_rs_sibling_in", [_rs_sibling([gw["w_in"]])])
    (part["w_in"],) = _chip_sums([gw["w_in"]], [sib["w_in"]], shard_core)
    x_in = _rs_chips([part["w_in"]])
    (s_in,), token = _split_starts("rs_in_start", [x_in], small["pre_mix_norm"])

    grads, deltas, new_m, new_v = {}, {}, {}, {}

    def record(n, outs):
        grads[n], deltas[n], new_m[n], new_v[n] = (
            jnp.swapaxes(o[None], 1, 2) if n in flipped else o[None] for o in outs)

    def update(names, block_rows, name, after):
        for n, outs in zip(names, _adamw_multi(
                [big(n, n) for n in names], [joined[n].reshape(big(n, n).shape) for n in names],
                [big("m_" + n, n) for n in names], [big("v_" + n, n) for n in names], block_rows, name, after)):
            record(n, outs)

    update(ffn, FF_S // 4, "adamw_ffn", [token])
    update(("w_out",), BIG_ADAM_ROWS["w_out"], "adamw_w_out", [token])
    (far["w_in"],) = _split_wait("rs_in_wait", x_in, *s_in, [new_v[n] for n in ("w_down", "w_out")])
    half["w_in"] = _rs_final_sum(gw["w_in"], sib["w_in"], far["w_in"], shard_core)
    ((joined["w_in"],),) = _comm_only("comm_rs_join_in", [_rs_join([half["w_in"]])])
    update(("w_in",), BIG_ADAM_ROWS["w_in"], "adamw_w_in", [])
    a[LOSS_ROW] = a["m_" + LOSS_ROW] = a["v_" + LOSS_ROW] = jnp.zeros((1, D_MODEL), F32)
    for names, rows, packed, slots in ((SMALL_EARLY, SMALL_ROWS, packed_early, slots_early),
                                       (SMALL_LATE, 8, packed_late, slots_late)):
        outs = _adamw_small(packed, slots, _pack(a, names, rows), _pack({n: a["m_" + n] for n in names}, names, rows),
                            _pack({n: a["v_" + n] for n in names}, names, rows), me)
        for dst, buf in zip((grads, deltas, new_m, new_v), outs):
            dst.update(_unpack(buf, names, {n: a[n].shape for n in names}))
    loss = jnp.sum(grads[LOSS_ROW]) * np.float32(0.5 / D_MODEL)
    return (loss, grad_x[None], *[grads[n] for n in WEIGHTS], *[deltas[n] for n in WEIGHTS],
            *[new_m[n] for n in WEIGHTS], *[new_v[n] for n in WEIGHTS])
```

```python
import numpy as np
import jax
import jax.numpy as jnp
from jax import lax
from jax.experimental import pallas as pl
from jax.experimental.pallas import tpu as pltpu

F32 = jnp.float32
BF16 = jnp.bfloat16

SEQ = 2048
D_MODEL = 1024
HEAD_DIM = 64
ATTN_W = 512
SGU_W = 512
SGU_GROUPS = 8
CHUNK = 128
DILATIONS = (1, 4, 16)
N_SHARD = 4
IN_S = 640
OUT_S = 256
FF_S = 704
PROJ_W = N_SHARD * IN_S
FF = N_SHARD * FF_S
FF_CHUNKS = ((0, 1024), (1024, 2048), (2048, FF))
RMS_EPS = 1e-6
LN_EPS = 1e-5
ROPE_THETA = 500000.0
ATTN_SCALE = 1.0 / np.sqrt(HEAD_DIM)
NEG = -1e30
TM = 512
TM_FFN = 256
VMEM_LIMIT = 56 * 1024 * 1024
SMALL_ROWS = 136

ADAM_LR = 0.001
ADAM_B1 = 0.9
ADAM_B2 = 0.999
ADAM_EPS = 1e-08
ADAM_WD = 0.01
ADAM_STEP = 10

MESH = pl.DeviceIdType.MESH
ANY = pl.BlockSpec(memory_space=pl.ANY)


def _dot(a, b):
    return jnp.dot(a, b, preferred_element_type=F32)


def _dot_nt(a, b):
    return lax.dot_general(a, b, (((1,), (1,)), ((), ())), preferred_element_type=F32)


def _dot_tn(a, b):
    return lax.dot_general(a, b, (((0,), (0,)), ((), ())), preferred_element_type=F32)


def _dot_exact(a, b):
    return jnp.dot(a, b, preferred_element_type=F32, precision=lax.Precision.HIGHEST)


def _dot_select(a, sel):
    hi = a.astype(BF16)
    lo = (a - hi.astype(F32)).astype(BF16)
    sel = sel.astype(BF16)
    return _dot(hi, sel) + _dot(lo, sel)


def _rms_stats(x):
    r = lax.rsqrt(jnp.mean(x * x, axis=-1, keepdims=True) + RMS_EPS)
    return x * r, r


def _rms_bwd(xh, r, gain, dy):
    dxh = dy * gain
    dx = r * (dxh - xh * jnp.mean(dxh * xh, axis=-1, keepdims=True))
    return dx, jnp.sum(dy * xh, axis=0, keepdims=True)


_ERF_ALPHA = (-2.72614225801306e-10, 2.77068142495902e-08, -2.10102402082508e-06, -5.69250639462346e-05,
              -7.34990630326855e-04, -2.95459980854025e-03, -1.60960333262415e-02)
_ERF_BETA = (-1.45660718464996e-05, -2.13374055278905e-04, -1.68282697438203e-03, -7.37332916720468e-03,
             -1.42647390514189e-02)


def _erf(x):
    x = jnp.clip(x, -4.0, 4.0)
    x2 = x * x
    p = jnp.full_like(x, _ERF_ALPHA[0])
    for a in _ERF_ALPHA[1:]:
        p = p * x2 + a
    q = jnp.full_like(x, _ERF_BETA[0])
    for b in _ERF_BETA[1:]:
        q = q * x2 + b
    return x * p / q


def _normal_cdf(x):
    return 0.5 * (1.0 + _erf(x * np.float32(1.0 / np.sqrt(2.0))))


def _gelu_grad(x, cdf):
    pdf = jnp.exp(-0.5 * x * x) * np.float32(1.0 / np.sqrt(2.0 * np.pi))
    return cdf + x * pdf


def _sigmoid(x):
    return 1.0 / (1.0 + jnp.exp(-x))


_INV_FREQ = tuple(float(np.float32(ROPE_THETA ** (-2.0 * j / 16.0))) for j in range(8))


def _rot_tables(pos):
    lane = lax.broadcasted_iota(jnp.int32, (1, 128), 1)
    d = lane & 63
    j = d & 7
    inv = jnp.zeros((1, 128), F32)
    for jj in range(8):
        inv = jnp.where(j == jj, _INV_FREQ[jj], inv)
    ang = pos.astype(F32) * inv
    c = jnp.cos(ang)
    s = jnp.sin(ang)
    cos_t = jnp.where(d < 16, c, 1.0)
    sin_a = jnp.where(d < 8, -s, 0.0)
    sin_b = jnp.where((d >= 8) & (d < 16), s, 0.0)
    return tuple(jnp.tile(t, (1, 4)) for t in (cos_t, sin_a, sin_b))


def _rope(x, tabs):
    cos_t, sin_a, sin_b = tabs
    return x * cos_t + pltpu.roll(x, 504, 1) * sin_a + pltpu.roll(x, 8, 1) * sin_b


def _rope_bwd(dy, tabs):
    cos_t, sin_a, sin_b = tabs
    return dy * cos_t + pltpu.roll(dy * sin_a, 8, 1) + pltpu.roll(dy * sin_b, 504, 1)


def _left_half():
    return lax.broadcasted_iota(jnp.int32, (CHUNK, CHUNK), 1) < HEAD_DIM


def _group_ones():
    lane = lax.broadcasted_iota(jnp.int32, (SGU_GROUPS, SGU_W), 1)
    row = lax.broadcasted_iota(jnp.int32, (SGU_GROUPS, SGU_W), 0)
    return ((lane >> 6) == row).astype(F32)


def _masked_spatial(w_ref):
    row = lax.broadcasted_iota(jnp.int32, (CHUNK, CHUNK), 0)
    col = lax.broadcasted_iota(jnp.int32, (CHUNK, CHUNK), 1)
    return [jnp.where(col <= row, w_ref[g], 0.0).astype(BF16) for g in range(SGU_GROUPS)]


def _sgu_core(u, vs, lg, lb, wm, bias_full):
    tm = u.shape[0]
    cdf_u, cdf_vs = _normal_cdf(u), _normal_cdf(vs)
    gu = u * cdf_u
    gv = vs * cdf_vs
    mu = jnp.mean(gv, axis=-1, keepdims=True)
    xc = gv - mu
    rstd = lax.rsqrt(jnp.mean(xc * xc, axis=-1, keepdims=True) + LN_EPS)
    xh = xc * rstd
    vnb = (xh * lg + lb).astype(BF16)
    left = _left_half()
    rows = []
    for c in range(tm // CHUNK):
        pieces = []
        for p in range(4):
            vp = vnb[c * CHUNK:(c + 1) * CHUNK, p * 128:(p + 1) * 128]
            pieces.append(jnp.where(left, _dot(wm[2 * p], vp), _dot(wm[2 * p + 1], vp)))
        rows.append(jnp.concatenate(pieces, axis=1) + bias_full)
    mixed = jnp.concatenate(rows, axis=0)
    return gu, xh, rstd, vnb, mixed, cdf_u, cdf_vs


def _resident(shape):
    n = len(shape)
    return pl.BlockSpec(shape, lambda *_: (0,) * n, pipeline_mode=pl.Buffered(1))


def _rows(ncol, tm=TM):
    return pl.BlockSpec((tm, ncol), lambda i: (i, 0))


def _rows3(nlead, ncol, tm=TM):
    return pl.BlockSpec((nlead, tm, ncol), lambda i: (0, i, 0))


def _acc(ncol, nrow=1):
    return pl.BlockSpec((nrow, ncol), lambda i: (0, 0))


HEAD_W = 128


def _view_rows(dil, width=ATTN_W, tm=TM):
    return pl.BlockSpec((tm // dil, dil * width), lambda i: (i, 0))


def _view_shape(dil, dtype, width=ATTN_W):
    return _sds((SEQ // dil, dil * width), dtype)


def _slab_scratch():
    return pltpu.VMEM((4, TM, 128), F32)


def _store_view(val, out_ref, slabs, dil):
    width = val.shape[1]
    for j in range(width // 128):
        slabs[j] = val[:, j * 128:(j + 1) * 128]
    for r in range(dil):
        for j in range(width // 128):
            c0 = r * width + j * 128
            out_ref[:, c0:c0 + 128] = slabs.at[j][pl.ds(r, TM // dil, stride=dil), :].astype(out_ref.dtype)


def _load_view(in_ref, slabs, dil, width=ATTN_W):
    for r in range(dil):
        for j in range(width // 128):
            c0 = r * width + j * 128
            slabs.at[j][pl.ds(r, TM // dil, stride=dil), :] = in_ref[:, c0:c0 + 128].astype(F32)
    return jnp.concatenate([slabs[j] for j in range(width // 128)], axis=1)


def _head_spread():
    m = lax.broadcasted_iota(jnp.int32, (HEAD_W, ATTN_W), 0)
    lane = lax.broadcasted_iota(jnp.int32, (HEAD_W, ATTN_W), 1)
    return (m == 16 * (lane >> 6)).astype(F32)


def _head_sum():
    lane = lax.broadcasted_iota(jnp.int32, (ATTN_W, HEAD_W), 0)
    m = lax.broadcasted_iota(jnp.int32, (ATTN_W, HEAD_W), 1)
    return ((lane >> 6) == (m >> 4)).astype(F32)


def _seq_params():
    return pltpu.CompilerParams(dimension_semantics=("arbitrary",), vmem_limit_bytes=VMEM_LIMIT)


def _sds(shape, dtype):
    return jax.ShapeDtypeStruct(shape, dtype)


class _Comm:
    def __init__(self, args, out_shape, n_sems, start, finish, aliased=False):
        self.args, self.out_shape, self.n_sems = list(args), list(out_shape), n_sems
        self.start, self.finish, self.aliased = start, finish, aliased


def _pcall(body, *, name, grid, in_specs, out_specs, out_shape, args, scratch_shapes=(), comms=(), after=()):
    single = not isinstance(out_shape, (list, tuple))
    out_specs = [out_specs] if single else list(out_specs)
    out_shape = [out_shape] if single else list(out_shape)
    n_in, n_out, n_scr = len(in_specs), len(out_shape), len(scratch_shapes)
    c_args = [a for c in comms for a in c.args]
    c_outs = [o for c in comms for o in c.out_shape]
    aliases, ai, ao = {}, n_in, n_out
    for c in comms:
        if c.aliased:
            aliases.update({ai + k: ao + k for k in range(len(c.args))})
        ai += len(c.args)
        ao += len(c.out_shape)
    sems = [pltpu.SemaphoreType.DMA((c.n_sems,)) for c in comms for _ in range(2)]
    steps = grid[0]

    def wrapped(*refs):
        o0 = n_in + len(c_args) + len(after)
        s0 = o0 + n_out + len(c_outs)
        m_in, m_out, m_sem = refs[n_in:n_in + len(c_args)], refs[o0 + n_out:s0], refs[s0 + n_scr:]

        def each(phase):
            ii = oi = 0
            for k, c in enumerate(comms):
                getattr(c, phase)(m_in[ii:ii + len(c.args)], m_out[oi:oi + len(c.out_shape)],
                                  m_sem[2 * k], m_sem[2 * k + 1])
                ii += len(c.args)
                oi += len(c.out_shape)

        if comms:
            @pl.when(pl.program_id(0) == 0)
            def _():
                each("start")

        body(*refs[:n_in], *refs[o0:o0 + n_out], *refs[s0:s0 + n_scr])

        if comms:
            @pl.when(pl.program_id(0) == steps - 1)
            def _():
                each("finish")

    res = pl.pallas_call(
        wrapped, name=name, grid=grid,
        in_specs=list(in_specs) + [ANY] * (len(c_args) + len(after)), out_specs=out_specs + [ANY] * len(c_outs),
        out_shape=out_shape + c_outs, scratch_shapes=list(scratch_shapes) + sems,
        input_output_aliases=aliases, compiler_params=_seq_params(),
    )(*args, *c_args, *after)
    mine = res[0] if single else list(res[:n_out])
    if not comms:
        return mine
    theirs, oi = [], n_out
    for c in comms:
        theirs.append(list(res[oi:oi + len(c.out_shape)]))
        oi += len(c.out_shape)
    return mine, theirs


def _inproj_fwd(x, pos, g_pre, w_in, comms=()):
    def body(x_ref, pos_ref, g_ref, w_ref, h_ref, u_ref, vs_ref, *rest):
        qkv_refs, slabs = rest[:9], rest[9:]
        xh, _ = _rms_stats(x_ref[...])
        h = (xh * g_ref[...]).astype(BF16)
        h_ref[...] = h
        proj = jnp.concatenate([_dot(h, w_ref[s]) for s in range(N_SHARD)], axis=1)
        tabs = _rot_tables(pos_ref[...])
        u_ref[...] = proj[:, 1536:2048]
        vs_ref[...] = proj[:, 2048:2560]
        qkv = (_rope(proj[:, 0:512], tabs) * np.float32(ATTN_SCALE), _rope(proj[:, 512:1024], tabs),
               proj[:, 1024:1536])
        for t, val in enumerate(qkv):
            qkv_refs[t][...] = val.astype(BF16)
            for i, dil in enumerate(DILATIONS[1:]):
                _store_view(val, qkv_refs[3 * (i + 1) + t], slabs[t], dil)

    return _pcall(
        body, name="inproj_fwd", grid=(SEQ // TM,),
        in_specs=[_rows(D_MODEL), _rows(1), _resident((1, D_MODEL)), _resident((N_SHARD, D_MODEL, IN_S))],
        out_specs=[_rows(D_MODEL), _rows(512), _rows(512)] + [_view_rows(dil) for dil in DILATIONS for _ in range(3)],
        out_shape=[_sds((SEQ, D_MODEL), BF16), _sds((SEQ, 512), F32), _sds((SEQ, 512), F32)]
        + [_view_shape(dil, BF16) for dil in DILATIONS for _ in range(3)],
        scratch_shapes=[_slab_scratch() for _ in range(3)],
        args=(x, pos, g_pre, w_in), comms=comms)


def _sgu_fwd(u, vs, lg, lb, w_sp, b_t, comms=()):
    def body(u_ref, vs_ref, lg_ref, lb_ref, w_ref, bt_ref, out_ref):
        wm = _masked_spatial(w_ref)
        bias_full = _dot_exact(bt_ref[...], _group_ones())
        gu, _, _, _, mixed, _, _ = _sgu_core(u_ref[...], vs_ref[...], lg_ref[...], lb_ref[...], wm, bias_full)
        out_ref[...] = gu * mixed

    return _pcall(
        body, name="sgu_fwd", grid=(SEQ // TM,),
        in_specs=[_rows(SGU_W), _rows(SGU_W), _resident((1, SGU_W)), _resident((1, SGU_W)),
                  _resident((SGU_GROUPS, CHUNK, CHUNK)), _resident((CHUNK, SGU_GROUPS))],
        out_specs=_rows(SGU_W),
        out_shape=_sds((SEQ, SGU_W), F32),
        args=(u, vs, lg, lb, w_sp, b_t), comms=comms)


def _block_masks():
    row = lax.broadcasted_iota(jnp.int32, (CHUNK, CHUNK), 0)
    col = lax.broadcasted_iota(jnp.int32, (CHUNK, CHUNK), 1)
    return col <= row, col >= row


def _attn_fwd(qv, kv, vv, dil, comms=()):
    seg = SEQ // dil
    nblk = seg // CHUNK
    rps = 4 if nblk == 1 else 1

    def body(q_ref, k_ref, v_ref, o_ref, l_ref):
        left = _left_half()
        m_cur, m_prev = _block_masks()
        zero = jnp.zeros((CHUNK, CHUNK), BF16)
        ones = (jnp.where(left, 1.0, 0.0).astype(BF16), jnp.where(left, 0.0, 1.0).astype(BF16))

        def blk(rr, b, carry):
            r0 = pl.multiple_of(b * CHUNK, CHUNK)
            rp = pl.multiple_of(jnp.maximum(b - 1, 0) * CHUNK, CHUNK)
            prev_ok = m_prev & (b > 0)
            sides = tuple(enumerate((left, ~left)))
            tiles, scores = [], []
            for hp in range(4):
                ls = slice(rr * ATTN_W + hp * 128, rr * ATTN_W + (hp + 1) * 128)
                qp = q_ref[pl.ds(r0, CHUNK), ls]
                kc = k_ref[pl.ds(r0, CHUNK), ls]
                kp = k_ref[pl.ds(rp, CHUNK), ls] if nblk > 1 else None
                tiles.append((ls, v_ref[pl.ds(r0, CHUNK), ls], v_ref[pl.ds(rp, CHUNK), ls] if nblk > 1 else None))
                for _, hm in sides:
                    qh = jnp.where(hm, qp, zero)
                    sc = jnp.where(m_cur, _dot_nt(qh, kc), NEG)
                    sp = jnp.where(prev_ok, _dot_nt(qh, kp), NEG) if nblk > 1 else None
                    scores.append((sc, sp))
            probs = []
            for sc, sp in scores:
                if nblk > 1:
                    m = jnp.max(jnp.maximum(sc, sp), axis=-1, keepdims=True)
                    pc = jnp.exp(sc - m)
                    pp = jnp.exp(sp - m)
                    probs.append((m, pc.astype(BF16), pp.astype(BF16), (pc + pp).astype(BF16)))
                else:
                    m = jnp.max(sc, axis=-1, keepdims=True)
                    pc = jnp.exp(sc - m).astype(BF16)
                    probs.append((m, pc, None, pc))
            for hp, (ls, vc, vp) in enumerate(tiles):
                acc = jnp.zeros((CHUNK, CHUNK), F32)
                den = jnp.zeros((CHUNK, CHUNK), F32)
                for side, hm in sides:
                    _, pc, pp, psum = probs[2 * hp + side]
                    acc = acc + _dot(pc, jnp.where(hm, vc, zero))
                    if nblk > 1:
                        acc = acc + _dot(pp, jnp.where(hm, vp, zero))
                    den = den + _dot(psum, ones[side])
                o_ref[pl.ds(r0, CHUNK), ls] = (acc / den).astype(o_ref.dtype)
                lse = jnp.where(left, probs[2 * hp][0], probs[2 * hp + 1][0]) + jnp.log(den)
                l_ref[pl.ds(r0, CHUNK), rr * HEAD_W + 32 * hp:rr * HEAD_W + 32 * hp + 32] = lse[:, 48:80]
            return carry

        for rr in range(rps):
            lax.fori_loop(0, nblk, lambda b, c, rr=rr: blk(rr, b, c), 0)

    spec = pl.BlockSpec((seg, rps * ATTN_W), lambda r: (0, r))
    return _pcall(
        body, name=f"attn_fwd_d{dil}", grid=(dil // rps,),
        in_specs=[spec, spec, spec], out_specs=[spec, pl.BlockSpec((seg, rps * HEAD_W), lambda r: (0, r))],
        out_shape=[_sds((seg, dil * ATTN_W), BF16), _sds((seg, dil * HEAD_W), F32)],
        args=(qv, kv, vv), comms=comms)


def _mix_out_fwd(o_list, l_list, sgu, x, w_out, g_attn, g_sgu, g_post, comms=()):
    def body(o1, o2, o3, l1, l2, l3, sgu_ref, x_ref, w_ref, ga_ref, gs_ref, gp_ref,
             attn_ref, mixed_ref, y_ref, x1_ref, lse1_ref, lse2_ref, lse3_ref, slabs_a, slabs_b):
        os = [o1[...], _load_view(o2, slabs_a, DILATIONS[1]), _load_view(o3, slabs_b, DILATIONS[2])]
        ls = [l1[...], _load_view(l2, slabs_a, DILATIONS[1], HEAD_W), _load_view(l3, slabs_b, DILATIONS[2], HEAD_W)]
        m = jnp.maximum(jnp.maximum(ls[0], ls[1]), ls[2])
        es = [jnp.exp(l - m) for l in ls]
        den = es[0] + es[1] + es[2]
        spread = _head_spread()
        attn = sum(_dot_select(e / den, spread) * o for e, o in zip(es, os))
        attn_ref[...] = attn
        lse = m + jnp.log(den)
        lse1_ref[...] = lse
        _store_view(lse, lse2_ref, slabs_a, DILATIONS[1])
        _store_view(lse, lse3_ref, slabs_b, DILATIONS[2])
        ah, _ = _rms_stats(attn)
        sh, _ = _rms_stats(sgu_ref[...])
        mixed = jnp.concatenate([ah * ga_ref[...], sh * gs_ref[...]], axis=1).astype(BF16)
        mixed_ref[...] = mixed
        y = _dot(mixed[:, 0:OUT_S], w_ref[0])
        for s in range(1, N_SHARD):
            y = y + _dot(mixed[:, s * OUT_S:(s + 1) * OUT_S], w_ref[s])
        y_ref[...] = y
        yh, _ = _rms_stats(y)
        x1_ref[...] = x_ref[...] + yh * gp_ref[...]

    return _pcall(
        body, name="mix_out_fwd", grid=(SEQ // TM,),
        in_specs=[_view_rows(dil) for dil in DILATIONS] + [_view_rows(dil, HEAD_W) for dil in DILATIONS]
        + [_rows(512), _rows(D_MODEL), _resident((N_SHARD, OUT_S, D_MODEL)),
           _resident((1, 512)), _resident((1, 512)), _resident((1, D_MODEL))],
        out_specs=[_rows(512), _rows(D_MODEL), _rows(D_MODEL), _rows(D_MODEL)]
        + [_view_rows(dil, HEAD_W) for dil in DILATIONS],
        out_shape=[_sds((SEQ, 512), F32), _sds((SEQ, D_MODEL), BF16), _sds((SEQ, D_MODEL), F32),
                   _sds((SEQ, D_MODEL), F32)] + [_view_shape(dil, F32, HEAD_W) for dil in DILATIONS],
        scratch_shapes=[_slab_scratch(), _slab_scratch()],
        args=(*o_list, *l_list, sgu, x, w_out, g_attn, g_sgu, g_post), comms=comms)


def _ffn_fwd_bwd(x1, target, w_gate, w_up, w_down, g_pre, g_post, comms=()):
    def body(x1_ref, t_ref, wg_ref, wu_ref, wd_ref, gpf_ref, gpo_ref,
             h2_ref, a_ref, dg_ref, dup_ref, df_ref, dx1_ref, loss_ref, dgpf_ref, dgpo_ref, g_scr, up_scr):
        @pl.when(pl.program_id(0) == 0)
        def _():
            loss_ref[...] = jnp.zeros_like(loss_ref)
            dgpf_ref[...] = jnp.zeros_like(dgpf_ref)
            dgpo_ref[...] = jnp.zeros_like(dgpo_ref)

        x1 = x1_ref[...]
        gpf = gpf_ref[...]
        gpo = gpo_ref[...]
        xh, r = _rms_stats(x1)
        h2 = (xh * gpf).astype(BF16)
        h2_ref[...] = h2
        f = jnp.zeros((TM_FFN, D_MODEL), F32)
        for c0, c1 in FF_CHUNKS:
            g = _dot_nt(h2, wg_ref[c0:c1, :])
            up = _dot_nt(h2, wu_ref[c0:c1, :])
            g_scr[:, c0:c1] = g
            up_scr[:, c0:c1] = up
            a = (g * _sigmoid(g) * up).astype(BF16)
            a_ref[:, c0:c1] = a
            f = f + _dot(a, wd_ref[c0:c1, :])
        fh, rf = _rms_stats(f)
        diff = x1 + fh * gpo - t_ref[...]
        loss_ref[...] += jnp.sum(diff * diff, axis=0, keepdims=True)
        dout = diff * np.float32(1.0 / D_MODEL)
        df, dgpo = _rms_bwd(fh, rf, gpo, dout)
        dgpo_ref[...] += dgpo
        dfb = df.astype(BF16)
        df_ref[...] = dfb
        dh2 = jnp.zeros((TM_FFN, D_MODEL), F32)
        for c0, c1 in FF_CHUNKS:
            da = _dot_nt(dfb, wd_ref[c0:c1, :])
            g = g_scr[:, c0:c1]
            up = up_scr[:, c0:c1]
            sg = _sigmoid(g)
            dup = (da * (g * sg)).astype(BF16)
            dg = (da * up * (sg * (1.0 + g * (1.0 - sg)))).astype(BF16)
            dg_ref[:, c0:c1] = dg
            dup_ref[:, c0:c1] = dup
            dh2 = dh2 + _dot(dg, wg_ref[c0:c1, :]) + _dot(dup, wu_ref[c0:c1, :])
        dx, dgpf = _rms_bwd(xh, r, gpf, dh2)
        dgpf_ref[...] += dgpf
        dx1_ref[...] = dout + dx

    return _pcall(
        body, name="ffn_fwd_bwd", grid=(SEQ // TM_FFN,),
        in_specs=[_rows(D_MODEL, TM_FFN), _rows(D_MODEL, TM_FFN), _resident((FF, D_MODEL)),
                  _resident((FF, D_MODEL)), _resident((FF, D_MODEL)),
                  _resident((1, D_MODEL)), _resident((1, D_MODEL))],
        out_specs=[_rows(D_MODEL, TM_FFN), _rows(FF, TM_FFN), _rows(FF, TM_FFN), _rows(FF, TM_FFN),
                   _rows(D_MODEL, TM_FFN), _rows(D_MODEL, TM_FFN), _acc(D_MODEL), _acc(D_MODEL), _acc(D_MODEL)],
        out_shape=[_sds((SEQ, D_MODEL), BF16), _sds((SEQ, FF), BF16), _sds((SEQ, FF), BF16),
                   _sds((SEQ, FF), BF16), _sds((SEQ, D_MODEL), BF16), _sds((SEQ, D_MODEL), F32),
                   _sds((1, D_MODEL), F32), _sds((1, D_MODEL), F32), _sds((1, D_MODEL), F32)],
        scratch_shapes=[pltpu.VMEM((TM_FFN, FF), F32), pltpu.VMEM((TM_FFN, FF), F32)],
        args=(x1, target, w_gate, w_up, w_down, g_pre, g_post), comms=comms)


def _wgrad(a, b, a_spec, b_spec, out_block, name, comms=()):
    def body(a_ref, b_ref, o_ref):
        av = a_ref[0] if len(a_ref.shape) == 3 else a_ref[...]
        bv = b_ref[0] if len(b_ref.shape) == 3 else b_ref[...]
        o_ref[0] = _dot_tn(av, bv)

    return _pcall(
        body, name=name, grid=(N_SHARD,),
        in_specs=[a_spec, b_spec],
        out_specs=pl.BlockSpec((1,) + out_block, lambda s: (s, 0, 0)),
        out_shape=_sds((N_SHARD,) + out_block, F32),
        args=(a, b), comms=comms)


def _outproj_bwd(dx1, y, attn, sgu, w_out, g_post, g_attn, g_sgu, comms=()):
    def body(dx1_ref, y_ref, attn_ref, sgu_ref, w_ref, gp_ref, ga_ref, gs_ref,
             dy_ref, dsgu_ref, dgp_ref, dga_ref, dgs_ref, *rest):
        dattn_refs, delta_refs, (slabs_a, slabs_b) = rest[0:3], rest[3:6], rest[6:]

        @pl.when(pl.program_id(0) == 0)
        def _():
            dgp_ref[...] = jnp.zeros_like(dgp_ref)
            dga_ref[...] = jnp.zeros_like(dga_ref)
            dgs_ref[...] = jnp.zeros_like(dgs_ref)

        yh, ry = _rms_stats(y_ref[...])
        dy, dgp = _rms_bwd(yh, ry, gp_ref[...], dx1_ref[...])
        dgp_ref[...] += dgp
        dyb = dy.astype(BF16)
        dy_ref[...] = dyb
        dmixed = jnp.concatenate([_dot_nt(dyb, w_ref[s]) for s in range(N_SHARD)], axis=1)
        attn = attn_ref[...]
        ah, ra = _rms_stats(attn)
        dattn, dga = _rms_bwd(ah, ra, ga_ref[...], dmixed[:, 0:512])
        dga_ref[...] += dga
        sh, rs = _rms_stats(sgu_ref[...])
        dsgu, dgs = _rms_bwd(sh, rs, gs_ref[...], dmixed[:, 512:1024])
        dgs_ref[...] += dgs
        dsgu_ref[...] = dsgu
        delta = _dot_select(dattn * attn, _head_sum())
        dattn_refs[0][...] = dattn.astype(BF16)
        delta_refs[0][...] = delta
        for i, dil in enumerate(DILATIONS[1:]):
            _store_view(dattn, dattn_refs[i + 1], slabs_a, dil)
            _store_view(delta, delta_refs[i + 1], slabs_b, dil)

    return _pcall(
        body, name="outproj_bwd", grid=(SEQ // TM,),
        in_specs=[_rows(D_MODEL), _rows(D_MODEL), _rows(512), _rows(512), _resident((N_SHARD, OUT_S, D_MODEL)),
                  _resident((1, D_MODEL)), _resident((1, 512)), _resident((1, 512))],
        out_specs=[_rows(D_MODEL), _rows(512), _acc(D_MODEL), _acc(512), _acc(512)]
        + [_view_rows(dil) for dil in DILATIONS] + [_view_rows(dil, HEAD_W) for dil in DILATIONS],
        out_shape=[_sds((SEQ, D_MODEL), BF16), _sds((SEQ, 512), F32),
                   _sds((1, D_MODEL), F32), _sds((1, 512), F32), _sds((1, 512), F32)]
        + [_view_shape(dil, BF16) for dil in DILATIONS] + [_view_shape(dil, F32, HEAD_W) for dil in DILATIONS],
        scratch_shapes=[_slab_scratch(), _slab_scratch()],
        args=(dx1, y, attn, sgu, w_out, g_post, g_attn, g_sgu), comms=comms)


def _sgu_bwd(u, vs, dsgu, lg, lb, w_sp, b_t, comms=(), after=()):
    nsteps = SEQ // TM

    def body(u_ref, vs_ref, ds_ref, lg_ref, lb_ref, w_ref, bt_ref,
             du_ref, dvs_ref, dw_ref, db_ref, dlg_ref, dlb_ref, dbias_scr):
        i = pl.program_id(0)

        @pl.when(i == 0)
        def _():
            dw_ref[...] = jnp.zeros_like(dw_ref)
            dlg_ref[...] = jnp.zeros_like(dlg_ref)
            dlb_ref[...] = jnp.zeros_like(dlb_ref)
            dbias_scr[...] = jnp.zeros_like(dbias_scr)

        wm = _masked_spatial(w_ref)
        ones_g = _group_ones()
        bias_full = _dot_exact(bt_ref[...], ones_g)
        u = u_ref[...]
        vs = vs_ref[...]
        lg = lg_ref[...]
        gu, xh, rstd, vnb, mixed, cdf_u, cdf_vs = _sgu_core(u, vs, lg, lb_ref[...], wm, bias_full)
        dsgu = ds_ref[...]
        du_ref[...] = (dsgu * mixed * _gelu_grad(u, cdf_u)).astype(BF16)
        dmixed = dsgu * gu
        left = _left_half()
        dvn_rows = []
        for c in range(TM // CHUNK):
            rs = slice(c * CHUNK, (c + 1) * CHUNK)
            dm_c = dmixed[rs, :]
            dbias_scr[...] += dm_c
            pieces = []
            for p in range(4):
                ls = slice(p * 128, (p + 1) * 128)
                dmp = dm_c[:, ls]
                vp = vnb[rs, ls]
                dmb = dmp.astype(BF16)
                zero = jnp.zeros_like(dmb)
                dw_ref[2 * p] += _dot_nt(jnp.where(left, dmb, zero), vp)
                dw_ref[2 * p + 1] += _dot_nt(jnp.where(left, zero, dmb), vp)
                pieces.append(jnp.where(left, _dot_tn(wm[2 * p], dmb), _dot_tn(wm[2 * p + 1], dmb)))
            dvn_rows.append(jnp.concatenate(pieces, axis=1))
        dvn = jnp.concatenate(dvn_rows, axis=0)
        dlg_ref[...] += jnp.sum(dvn * xh, axis=0, keepdims=True)
        dlb_ref[...] += jnp.sum(dvn, axis=0, keepdims=True)
        dxh = dvn * lg
        dgv = rstd * (dxh - jnp.mean(dxh, axis=-1, keepdims=True) - xh * jnp.mean(dxh * xh, axis=-1, keepdims=True))
        dvs_ref[...] = (dgv * _gelu_grad(vs, cdf_vs)).astype(BF16)

        @pl.when(i == nsteps - 1)
        def _():
            row = lax.broadcasted_iota(jnp.int32, (CHUNK, CHUNK), 0)
            col = lax.broadcasted_iota(jnp.int32, (CHUNK, CHUNK), 1)
            for g in range(SGU_GROUPS):
                dw_ref[g] = jnp.where(col <= row, dw_ref[g], 0.0)
            db_ref[...] = lax.dot_general(ones_g, dbias_scr[...], (((1,), (1,)), ((), ())),
                                          preferred_element_type=F32, precision=lax.Precision.HIGHEST)

    return _pcall(
        body, name="sgu_bwd", grid=(nsteps,),
        in_specs=[_rows(SGU_W), _rows(SGU_W), _rows(SGU_W), _resident((1, SGU_W)), _resident((1, SGU_W)),
                  _resident((SGU_GROUPS, CHUNK, CHUNK)), _resident((CHUNK, SGU_GROUPS))],
        out_specs=[_rows(SGU_W), _rows(SGU_W), pl.BlockSpec((SGU_GROUPS, CHUNK, CHUNK), lambda i: (0, 0, 0)),
                   _acc(CHUNK, SGU_GROUPS), _acc(SGU_W), _acc(SGU_W)],
        out_shape=[_sds((SEQ, SGU_W), BF16), _sds((SEQ, SGU_W), BF16), _sds((SGU_GROUPS, CHUNK, CHUNK), F32),
                   _sds((SGU_GROUPS, CHUNK), F32), _sds((1, SGU_W), F32), _sds((1, SGU_W), F32)],
        scratch_shapes=[pltpu.VMEM((CHUNK, SGU_W), F32)],
        args=(u, vs, dsgu, lg, lb, w_sp, b_t), comms=comms, after=after)


def _attn_bwd(qv, kv, vv, dov, deltav, lsev, dil, comms=(), after=()):
    seg = SEQ // dil
    nblk = seg // CHUNK
    rps = 4 if nblk == 1 else 1

    def body(q_ref, k_ref, v_ref, do_ref, dl_ref, lse_ref, dq_ref, dk_ref, dv_ref, dk_wait, dv_wait):
        left = _left_half()
        m_cur, m_prev = _block_masks()

        def blk(rr, b, carry):
            r0 = pl.multiple_of(b * CHUNK, CHUNK)
            rp = pl.multiple_of(jnp.maximum(b - 1, 0) * CHUNK, CHUNK)
            prev_ok = m_prev & (b > 0)
            sides = tuple(enumerate((left, ~left)))
            zero = jnp.zeros((CHUNK, CHUNK), BF16)
            tiles, firsts = [], []
            for hp in range(4):
                ls = slice(rr * ATTN_W + hp * 128, rr * ATTN_W + (hp + 1) * 128)
                qp = q_ref[pl.ds(r0, CHUNK), ls]
                kc = k_ref[pl.ds(r0, CHUNK), ls]
                vc = v_ref[pl.ds(r0, CHUNK), ls]
                dop = do_ref[pl.ds(r0, CHUNK), ls]
                kp = k_ref[pl.ds(rp, CHUNK), ls] if nblk > 1 else None
                vp = v_ref[pl.ds(rp, CHUNK), ls] if nblk > 1 else None
                tiles.append((ls, kc, kp))
                for side, hm in sides:
                    qh = jnp.where(hm, qp, zero)
                    doh = jnp.where(hm, dop, zero)
                    cur = (_dot_nt(qh, kc), _dot_nt(doh, vc))
                    prev = (_dot_nt(qh, kp), _dot_nt(doh, vp)) if nblk > 1 else None
                    firsts.append((qh, doh, cur, prev))
            seconds = []
            for i, (qh, doh, cur, prev) in enumerate(firsts):
                c0 = rr * HEAD_W + 16 * i
                lse_h = lse_ref[pl.ds(r0, CHUNK), :][:, c0:c0 + 1]
                dl_h = dl_ref[pl.ds(r0, CHUNK), :][:, c0:c0 + 1]
                pc = jnp.exp(jnp.where(m_cur, cur[0] - lse_h, NEG))
                out = [pc.astype(BF16), (pc * (cur[1] - dl_h)).astype(BF16), None, None]
                if nblk > 1:
                    pp = jnp.exp(jnp.where(prev_ok, prev[0] - lse_h, NEG))
                    out[2:] = [pp.astype(BF16), (pp * (prev[1] - dl_h)).astype(BF16)]
                seconds.append(out)
            for hp, (ls, kc, kp) in enumerate(tiles):
                dq = jnp.zeros((CHUNK, CHUNK), F32)
                dkc = jnp.zeros((CHUNK, CHUNK), F32)
                dvc = jnp.zeros((CHUNK, CHUNK), F32)
                dkp = jnp.zeros((CHUNK, CHUNK), F32)
                dvp = jnp.zeros((CHUNK, CHUNK), F32)
                for side, hm in sides:
                    qh, doh, _, _ = firsts[2 * hp + side]
                    pcb, dsc, ppb, dsp = seconds[2 * hp + side]
                    dq = dq + _dot(dsc, jnp.where(hm, kc, zero))
                    dkc = dkc + _dot_tn(dsc, qh)
                    dvc = dvc + _dot_tn(pcb, doh)
                    if nblk > 1:
                        dq = dq + _dot(dsp, jnp.where(hm, kp, zero))
                        dkp = dkp + _dot_tn(dsp, qh)
                        dvp = dvp + _dot_tn(ppb, doh)
                dq_ref[pl.ds(r0, CHUNK), ls] = dq.astype(dq_ref.dtype)
                if nblk == 1:
                    dk_ref[pl.ds(r0, CHUNK), ls] = dkc.astype(dk_ref.dtype)
                    dv_ref[pl.ds(r0, CHUNK), ls] = dvc.astype(dv_ref.dtype)
                else:
                    @pl.when(b > 0)
                    def _():
                        dk_ref[pl.ds(rp, CHUNK), ls] = (dk_wait[:, ls] + dkp).astype(dk_ref.dtype)
                        dv_ref[pl.ds(rp, CHUNK), ls] = (dv_wait[:, ls] + dvp).astype(dv_ref.dtype)

                    dk_wait[:, ls] = dkc
                    dv_wait[:, ls] = dvc
            return carry

        for rr in range(rps):
            lax.fori_loop(0, nblk, lambda b, c, rr=rr: blk(rr, b, c), 0)
        if nblk > 1:
            last = (nblk - 1) * CHUNK
            dk_ref[last:last + CHUNK, :] = dk_wait[...].astype(dk_ref.dtype)
            dv_ref[last:last + CHUNK, :] = dv_wait[...].astype(dv_ref.dtype)

    spec = pl.BlockSpec((seg, rps * ATTN_W), lambda r: (0, r))
    return _pcall(
        body, name=f"attn_bwd_d{dil}", grid=(dil // rps,),
        in_specs=[spec] * 4 + [pl.BlockSpec((seg, rps * HEAD_W), lambda r: (0, r))] * 2, out_specs=[spec] * 3,
        out_shape=[_sds((seg, dil * ATTN_W), BF16)] * 3,
        scratch_shapes=[pltpu.VMEM((CHUNK, ATTN_W), F32), pltpu.VMEM((CHUNK, ATTN_W), F32)],
        args=(qv, kv, vv, dov, deltav, lsev), comms=comms, after=after)


def _inproj_bwd(dqs, dks, dvs, du, dvs_sgu, pos, x, dx1, w_in, g_pre, comms=()):
    def body(dq1, dq2, dq3, dk1, dk2, dk3, dv1, dv2, dv3, du_ref, dvs_ref, pos_ref, x_ref, dx1_ref, w_ref, g_ref,
             dproj_ref, gx_ref, dg_ref, slabs_a, slabs_b):
        @pl.when(pl.program_id(0) == 0)
        def _():
            dg_ref[...] = jnp.zeros_like(dg_ref)

        def total(r1, r2, r3):
            return r1[...] + _load_view(r2, slabs_a, DILATIONS[1]) + _load_view(r3, slabs_b, DILATIONS[2])

        tabs = _rot_tables(pos_ref[...])
        dproj_ref[:, 0:512] = _rope_bwd(total(dq1, dq2, dq3) * np.float32(ATTN_SCALE), tabs).astype(BF16)
        dproj_ref[:, 512:1024] = _rope_bwd(total(dk1, dk2, dk3), tabs).astype(BF16)
        dproj_ref[:, 1024:1536] = total(dv1, dv2, dv3).astype(BF16)
        dproj_ref[:, 1536:2048] = du_ref[...]
        dproj_ref[:, 2048:2560] = dvs_ref[...]
        dh = jnp.zeros((TM, D_MODEL), F32)
        for s in range(N_SHARD):
            dh = dh + _dot_nt(dproj_ref[:, s * IN_S:(s + 1) * IN_S], w_ref[s])
        g = g_ref[...]
        xh, r = _rms_stats(x_ref[...])
        dx, dg = _rms_bwd(xh, r, g, dh)
        dg_ref[...] += dg
        gx_ref[...] = dx1_ref[...] + dx

    return _pcall(
        body, name="inproj_bwd", grid=(SEQ // TM,),
        in_specs=[_view_rows(dil) for dil in DILATIONS] * 3
        + [_rows(512), _rows(512), _rows(1), _rows(D_MODEL), _rows(D_MODEL),
           _resident((N_SHARD, D_MODEL, IN_S)), _resident((1, D_MODEL))],
        out_specs=[_rows(PROJ_W), _rows(D_MODEL), _acc(D_MODEL)],
        out_shape=[_sds((SEQ, PROJ_W), BF16), _sds((SEQ, D_MODEL), F32), _sds((1, D_MODEL), F32)],
        scratch_shapes=[_slab_scratch(), _slab_scratch()],
        args=(*dqs, *dks, *dvs, du, dvs_sgu, pos, x, dx1, w_in, g_pre), comms=comms)


def _to_view(a, dil):
    return a if dil == 1 else a.reshape(SEQ // dil, dil * a.shape[1])


def _from_view(a, dil):
    return a if dil == 1 else a.reshape(SEQ, a.shape[1] // dil)


def _local_step(x, pos, target, w_in, w_out, w_gate, w_up, w_down, small):
    b_t = small["sgu_b_spatial"].T
    h, u, vs, *qkv = _inproj_fwd(x, pos, small["pre_mix_norm"], w_in)
    sgu = _sgu_fwd(u, vs, small["sgu_ln_gain"], small["sgu_ln_bias"], small["sgu_w_spatial"], b_t)
    views = [tuple(qkv[3 * i:3 * i + 3]) for i in range(len(DILATIONS))]
    o_list, l_list = [], []
    for dil, (qv, kv, vv) in zip(DILATIONS, views):
        o, l = _attn_fwd(qv, kv, vv, dil)
        o_list.append(o)
        l_list.append(l)
    attn, mixed, y, x1, *lses = _mix_out_fwd(o_list, l_list, sgu, x, w_out, small["attn_out_norm"],
                                             small["sgu_out_norm"], small["post_mix_norm"])
    h2, a, dg, dup, df, dx1, loss_cols, d_pre_ffn, d_post_ffn = _ffn_fwd_bwd(
        x1, target, w_gate, w_up, w_down, small["pre_ffn_norm"], small["post_ffn_norm"])

    full_tok = pl.BlockSpec((SEQ, D_MODEL), lambda s: (0, 0), pipeline_mode=pl.Buffered(1))
    ff_tok = pl.BlockSpec((1, SEQ, FF_S), lambda s: (s, 0, 0))
    gw_gate, gw_up = _wgrad_ff([dg, dup], h2, "wgrad_gate_up")
    (gw_down,) = _wgrad_ff([a], df, "wgrad_down")

    dy, dsgu, d_post_mix, d_attn_norm, d_sgu_norm, *dviews = _outproj_bwd(
        dx1, y, attn, sgu, w_out, small["post_mix_norm"], small["attn_out_norm"], small["sgu_out_norm"])
    gw_out = _wgrad(mixed, dy, pl.BlockSpec((SEQ, OUT_S), lambda s: (0, s)), full_tok, (OUT_S, D_MODEL), "wgrad_out")
    du, dvs_sgu, d_w_sp, d_b_sp, d_ln_gain, d_ln_bias = _sgu_bwd(
        u, vs, dsgu, small["sgu_ln_gain"], small["sgu_ln_bias"], small["sgu_w_spatial"], b_t)

    dqs, dks, dvs = [], [], []
    for i, (dil, (qv, kv, vv)) in enumerate(zip(DILATIONS, views)):
        dq, dk, dv = _attn_bwd(qv, kv, vv, dviews[i], dviews[3 + i], lses[i], dil)
        dqs.append(dq)
        dks.append(dk)
        dvs.append(dv)
    dproj, grad_x, d_pre_mix = _inproj_bwd(dqs, dks, dvs, du, dvs_sgu, pos, x, dx1, w_in, small["pre_mix_norm"])
    gw_in = _wgrad(h, dproj, full_tok, pl.BlockSpec((SEQ, IN_S), lambda s: (0, s)), (D_MODEL, IN_S), "wgrad_in")

    small_grads = {
        "pre_mix_norm": d_pre_mix, "sgu_ln_gain": d_ln_gain, "sgu_ln_bias": d_ln_bias, "sgu_w_spatial": d_w_sp,
        "sgu_b_spatial": d_b_sp, "attn_out_norm": d_attn_norm, "sgu_out_norm": d_sgu_norm,
        "post_mix_norm": d_post_mix, "pre_ffn_norm": d_pre_ffn, "post_ffn_norm": d_post_ffn,
    }
    return loss_cols, grad_x, (gw_in, gw_out, gw_gate, gw_up, gw_down), small_grads


def _coords():
    return lax.axis_index("x"), lax.axis_index("y"), lax.axis_index("c")


def _other_chips(x, y):
    return [(1 - x, y), (x, 1 - y), (1 - x, 1 - y)]


def _comm_call(body, name, n_in, out_shape, scratch_shapes):
    return pl.pallas_call(
        body, name=name, in_specs=[ANY] * n_in, out_specs=[ANY] * len(out_shape), out_shape=out_shape,
        scratch_shapes=scratch_shapes,
        compiler_params=pltpu.CompilerParams(has_side_effects=True),
    )


def _gather_weights(shards):
    n = len(shards)
    halves = [s.reshape(2, s.shape[0] // 2, s.shape[1]) for s in shards]

    def body(*refs):
        ins, outs = refs[:n], refs[n:2 * n]
        send_sems, recv_sems = refs[2 * n:]
        x, y, c = _coords()
        s_me = 2 * x + y
        chips = _other_chips(x, y)
        sibling = (x, y, 1 - c)

        def copy(k, w, shard, cc, to):
            src = ins[w].at[cc] if shard is None else outs[w].at[shard, cc]
            dst = outs[w].at[s_me if shard is None else shard, cc]
            return pltpu.make_async_remote_copy(src_ref=src, dst_ref=dst, send_sem=send_sems.at[k],
                                                recv_sem=recv_sems.at[k], device_id=to, device_id_type=MESH)

        first = [copy(j * n + w, w, None, c, (cx, cy, c)) for j, (cx, cy) in enumerate(chips) for w in range(n)]
        for cp in first:
            cp.start()
        passed = []
        for j, (cx, cy) in enumerate(chips):
            for w in range(n):
                copy(j * n + w, w, 2 * cx + cy, c, (x, y, c)).wait_recv()
                fw = copy((3 + j) * n + w, w, 2 * cx + cy, c, sibling)
                fw.start()
                passed.append(fw)
        for j, (cx, cy) in enumerate(chips):
            for w in range(n):
                copy((3 + j) * n + w, w, 2 * cx + cy, 1 - c, (x, y, c)).wait_recv()
        for cp in first + passed:
            cp.wait_send()

    out_shape = [_sds((N_SHARD,) + h.shape, h.dtype) for h in halves]
    scratch = [pltpu.SemaphoreType.DMA((6 * n,)), pltpu.SemaphoreType.DMA((6 * n,))]
    full = _comm_call(body, "comm_gather_weights", n, out_shape, scratch)(*halves)
    s_me = 2 * lax.axis_index("x") + lax.axis_index("y")
    full = [lax.dynamic_update_slice(f, h[None], (s_me, 0, 0, 0)) for f, h in zip(full, halves)]
    return [f.reshape((N_SHARD,) + s.shape) for f, s in zip(full, shards)]


def _rs_to_sibling(gws):
    n = len(gws)

    def body(*refs):
        ins, outs = refs[:n], refs[n:2 * n]
        send_sems, recv_sems = refs[2 * n:]
        x, y, c = _coords()
        copies = []
        for w in range(n):
            hw = gws[w].shape[1] // 2
            copies.append(pltpu.make_async_remote_copy(
                src_ref=ins[w].at[:, pl.ds((1 - c) * hw, hw), :], dst_ref=outs[w], send_sem=send_sems.at[w],
                recv_sem=recv_sems.at[w], device_id=(x, y, 1 - c), device_id_type=MESH))
        for cp in copies:
            cp.start()
        for cp in copies:
            cp.wait()

    out_shape = [_sds((N_SHARD, g.shape[1] // 2, g.shape[2]), g.dtype) for g in gws]
    scratch = [pltpu.SemaphoreType.DMA((n,)), pltpu.SemaphoreType.DMA((n,))]
    return _comm_call(body, "comm_rs_sibling", n, out_shape, scratch)(*gws)


def _rs_chip_sum(gw, recv, core):
    _, rows, cols = gw.shape
    hw = rows // 2

    def body(c_ref, g_ref, r_ref, o_ref):
        o_ref[...] = (g_ref[...] + r_ref[...]).astype(BF16)

    return pl.pallas_call(
        body, name="rs_chip_sum",
        grid_spec=pltpu.PrefetchScalarGridSpec(
            num_scalar_prefetch=1, grid=(N_SHARD,),
            in_specs=[pl.BlockSpec((1, hw, cols), lambda s, c_ref: (s, c_ref[0], 0)),
                      pl.BlockSpec((1, hw, cols), lambda s, c_ref: (s, 0, 0))],
            out_specs=pl.BlockSpec((1, hw, cols), lambda s, c_ref: (s, 0, 0))),
        out_shape=_sds((N_SHARD, hw, cols), BF16),
        compiler_params=_seq_params(),
    )(core, gw, recv)


def _rs_between_chips(pbs):
    n = len(pbs)

    def body(*refs):
        ins, outs = refs[:n], refs[n:2 * n]
        send_sems, recv_sems = refs[2 * n:]
        x, y, c = _coords()
        copies = []
        for j, (cx, cy) in enumerate(_other_chips(x, y)):
            for w in range(n):
                copies.append(pltpu.make_async_remote_copy(
                    src_ref=ins[w].at[2 * cx + cy], dst_ref=outs[w].at[j], send_sem=send_sems.at[j * n + w],
                    recv_sem=recv_sems.at[j * n + w], device_id=(cx, cy, c), device_id_type=MESH))
        for cp in copies:
            cp.start()
        for cp in copies:
            cp.wait()

    out_shape = [_sds((3,) + p.shape[1:], p.dtype) for p in pbs]
    scratch = [pltpu.SemaphoreType.DMA((3 * n,)), pltpu.SemaphoreType.DMA((3 * n,))]
    return _comm_call(body, "comm_rs_chips", n, out_shape, scratch)(*pbs)


def _rs_final_sum(gw, recv_sib, recv_chips, shard_core):
    _, rows, cols = gw.shape
    hw = rows // 2

    def body(sc_ref, g_ref, r_ref, rc_ref, o_ref):
        acc = g_ref[0] + r_ref[0]
        for j in range(3):
            acc = acc + rc_ref[j].astype(F32)
        o_ref[0] = acc

    return pl.pallas_call(
        body, name="rs_final_sum",
        grid_spec=pltpu.PrefetchScalarGridSpec(
            num_scalar_prefetch=1, grid=(1,),
            in_specs=[pl.BlockSpec((1, hw, cols), lambda i, sc: (sc[0], sc[1], 0)),
                      pl.BlockSpec((1, hw, cols), lambda i, sc: (sc[0], 0, 0)),
                      pl.BlockSpec((3, hw, cols), lambda i, sc: (0, 0, 0))],
            out_specs=pl.BlockSpec((1, hw, cols), lambda i, sc: (sc[1], 0, 0))),
        out_shape=_sds((2, hw, cols), F32),
        compiler_params=_seq_params(),
    )(shard_core, gw, recv_sib, recv_chips)


def _rs_join_halves(halves):
    n = len(halves)

    def body(*refs):
        bufs = refs[n:2 * n]
        send_sems, recv_sems = refs[2 * n:]
        x, y, c = _coords()
        remote = [pltpu.make_async_remote_copy(
            src_ref=bufs[w].at[c], dst_ref=bufs[w].at[c], send_sem=send_sems.at[w], recv_sem=recv_sems.at[w],
            device_id=(x, y, 1 - c), device_id_type=MESH) for w in range(n)]
        for cp in remote:
            cp.start()
        for w in range(n):
            remote[w].wait_send()
            pltpu.make_async_remote_copy(
                src_ref=bufs[w].at[c], dst_ref=bufs[w].at[1 - c], send_sem=send_sems.at[w],
                recv_sem=recv_sems.at[w], device_id=(x, y, c), device_id_type=MESH).wait_recv()

    joined = pl.pallas_call(
        body, name="comm_rs_join", in_specs=[ANY] * n, out_specs=[ANY] * n,
        out_shape=[_sds(h.shape, h.dtype) for h in halves], input_output_aliases={w: w for w in range(n)},
        scratch_shapes=[pltpu.SemaphoreType.DMA((n,)), pltpu.SemaphoreType.DMA((n,))],
        compiler_params=pltpu.CompilerParams(has_side_effects=True),
    )(*halves)
    return [j.reshape(2 * h.shape[1], h.shape[2]) for j, h in zip(joined, halves)]


def _allreduce_small(buf):
    rows, cols = buf.shape

    def body(in_ref, out_ref, slots, send_sems, recv_sems):
        x, y, c = _coords()
        me = 4 * x + 2 * y + c
        copies, peers = [], []
        for k in range(1, 8):
            px = 1 - x if (k >> 2) & 1 else x
            py = 1 - y if (k >> 1) & 1 else y
            pc = 1 - c if k & 1 else c
            peers.append(4 * px + 2 * py + pc)
            copies.append(pltpu.make_async_remote_copy(
                src_ref=in_ref, dst_ref=slots.at[me], send_sem=send_sems.at[k - 1], recv_sem=recv_sems.at[k - 1],
                device_id=(px, py, pc), device_id_type=MESH))
        for cp in copies:
            cp.start()
        slots[me] = in_ref[...]
        for k in range(7):
            pltpu.make_async_remote_copy(
                src_ref=in_ref, dst_ref=slots.at[peers[k]], send_sem=send_sems.at[k], recv_sem=recv_sems.at[k],
                device_id=(x, y, c), device_id_type=MESH).wait_recv()
        for cp in copies:
            cp.wait_send()
        acc = slots[0]
        for i in range(1, 8):
            acc = acc + slots[i]
        out_ref[...] = acc

    vmem = pl.BlockSpec(memory_space=pltpu.VMEM)
    return pl.pallas_call(
        body, name="comm_allreduce_small", in_specs=[vmem], out_specs=vmem, out_shape=_sds((rows, cols), F32),
        scratch_shapes=[pltpu.VMEM((8, rows, cols), F32), pltpu.SemaphoreType.DMA((7,)), pltpu.SemaphoreType.DMA((7,))],
        compiler_params=pltpu.CompilerParams(has_side_effects=True, vmem_limit_bytes=VMEM_LIMIT),
    )(buf)


def _adamw(w, g, m, v, block_rows, name, after=()):
    rows, cols = w.shape

    def body(w_ref, g_ref, m_ref, v_ref, *rest):
        d_ref, nm_ref, nv_ref = rest[len(after):]
        g = g_ref[...]
        m = ADAM_B1 * m_ref[...] + (1.0 - ADAM_B1) * g
        v = ADAM_B2 * v_ref[...] + (1.0 - ADAM_B2) * (g * g)
        m_hat = m / (1.0 - ADAM_B1 ** ADAM_STEP)
        v_hat = v / (1.0 - ADAM_B2 ** ADAM_STEP)
        d_ref[...] = -ADAM_LR * (m_hat / (jnp.sqrt(v_hat) + ADAM_EPS) + ADAM_WD * w_ref[...])
        nm_ref[...] = m
        nv_ref[...] = v

    spec = pl.BlockSpec((block_rows, cols), lambda i: (i, 0))
    return pl.pallas_call(
        body, name=name, grid=(rows // block_rows,), in_specs=[spec] * 4 + [ANY] * len(after), out_specs=[spec] * 3,
        out_shape=[_sds((rows, cols), F32)] * 3,
        compiler_params=_seq_params(),
    )(w, g, m, v, *after)


WEIGHTS = ("pre_mix_norm", "w_in", "sgu_ln_gain", "sgu_ln_bias", "sgu_w_spatial", "sgu_b_spatial", "attn_out_norm",
           "sgu_out_norm", "w_out", "post_mix_norm", "pre_ffn_norm", "w_gate", "w_up", "w_down", "post_ffn_norm")
BIG = ("w_in", "w_out", "w_gate", "w_up", "w_down")
BIG_ADAM_ROWS = {"w_in": 256, "w_out": 128, "w_gate": 352, "w_up": 352, "w_down": 352}
SMALL = ("pre_mix_norm", "post_mix_norm", "pre_ffn_norm", "post_ffn_norm", "sgu_ln_gain", "sgu_ln_bias",
         "attn_out_norm", "sgu_out_norm", "sgu_w_spatial", "sgu_b_spatial")


def _pack_small(d):
    flat = [d[n].reshape(-1) for n in SMALL]
    used = sum(f.shape[0] for f in flat)
    flat.append(jnp.zeros((SMALL_ROWS * 1024 - used,), F32))
    return jnp.concatenate(flat).reshape(SMALL_ROWS, 1024)


def _unpack_small(buf, shapes):
    flat = buf.reshape(-1)
    out, off = {}, 0
    for n in SMALL:
        size = int(np.prod(shapes[n]))
        out[n] = flat[off:off + size].reshape(shapes[n])
        off += size
    return out


def _kernel_unoverlapped(x, positions, pre_mix_norm, w_in, sgu_ln_gain, sgu_ln_bias, sgu_w_spatial, sgu_b_spatial, attn_out_norm, sgu_out_norm, w_out, post_mix_norm, pre_ffn_norm, w_gate, w_up, w_down, post_ffn_norm, loss_target, m_pre_mix_norm, m_w_in, m_sgu_ln_gain, m_sgu_ln_bias, m_sgu_w_spatial, m_sgu_b_spatial, m_attn_out_norm, m_sgu_out_norm, m_w_out, m_post_mix_norm, m_pre_ffn_norm, m_w_gate, m_w_up, m_w_down, m_post_ffn_norm, v_pre_mix_norm, v_w_in, v_sgu_ln_gain, v_sgu_ln_bias, v_sgu_w_spatial, v_sgu_b_spatial, v_attn_out_norm, v_sgu_out_norm, v_w_out, v_post_mix_norm, v_pre_ffn_norm, v_w_gate, v_w_up, v_w_down, v_post_ffn_norm):
    a = dict(locals())
    cx, cy, cc = _coords()
    core = jnp.stack([cc]).astype(jnp.int32)
    shard_core = jnp.stack([2 * cx + cy, cc]).astype(jnp.int32)

    full = _gather_weights([a[n][0].astype(BF16) for n in BIG])
    small = {n: (a[n][0] if a[n].ndim > 2 else a[n]) for n in SMALL}
    loss_cols, grad_x, gws, small_grads = _local_step(
        x[0], positions.reshape(SEQ, 1), loss_target[0], *full, small)
    loss = lax.psum(jnp.sum(loss_cols) * np.float32(0.5 / D_MODEL), ("x", "y", "c"))

    recv_sib = _rs_to_sibling(list(gws))
    chip_part = [_rs_chip_sum(g, r, core) for g, r in zip(gws, recv_sib)]
    recv_chips = _rs_between_chips(chip_part)
    halves = [_rs_final_sum(g, r, rc, shard_core) for g, r, rc in zip(gws, recv_sib, recv_chips)]
    big_grads = dict(zip(BIG, _rs_join_halves(halves)))

    shapes = {n: a[n].shape for n in SMALL}
    small_sum = _allreduce_small(_pack_small(small_grads))

    grads, deltas, new_m, new_v = {}, {}, {}, {}
    for n in BIG:
        grads[n] = big_grads[n][None]
        d, nm, nv = _adamw(a[n][0], big_grads[n], a["m_" + n][0], a["v_" + n][0], BIG_ADAM_ROWS[n], "adamw_" + n)
        deltas[n], new_m[n], new_v[n] = d[None], nm[None], nv[None]
    d, nm, nv = _adamw(_pack_small({n: a[n] for n in SMALL}), small_sum, _pack_small({n: a["m_" + n] for n in SMALL}),
                       _pack_small({n: a["v_" + n] for n in SMALL}), SMALL_ROWS, "adamw_small")
    grads.update(_unpack_small(small_sum, shapes))
    deltas.update(_unpack_small(d, shapes))
    new_m.update(_unpack_small(nm, shapes))
    new_v.update(_unpack_small(nv, shapes))
    return (loss, grad_x[None], *[grads[n] for n in WEIGHTS], *[deltas[n] for n in WEIGHTS],
            *[new_m[n] for n in WEIGHTS], *[new_v[n] for n in WEIGHTS])


def _remote(src, dst, send_sem, recv_sem, to):
    return pltpu.make_async_remote_copy(src_ref=src, dst_ref=dst, send_sem=send_sem, recv_sem=recv_sem,
                                        device_id=to, device_id_type=MESH)


def _halves(a):
    *lead, rows, cols = a.shape
    return a.reshape(*lead, 2, rows // 2, cols)


def _gather_ici(shards):
    n = len(shards)

    def desc(ins, outs, ss, rs, j, w, landed):
        x, y, c = _coords()
        cx, cy = _other_chips(x, y)[j]
        shard = 2 * cx + cy if landed else 2 * x + y
        return _remote(ins[w].at[c], outs[w].at[shard, c], ss.at[j * n + w], rs.at[j * n + w], (cx, cy, c))

    def start(ins, outs, ss, rs):
        for j in range(3):
            for w in range(n):
                desc(ins, outs, ss, rs, j, w, False).start()

    def finish(ins, outs, ss, rs):
        for j in range(3):
            for w in range(n):
                desc(ins, outs, ss, rs, j, w, True).wait_recv()
                desc(ins, outs, ss, rs, j, w, False).wait_send()

    return _Comm(shards, [_sds((N_SHARD,) + s.shape, s.dtype) for s in shards], 3 * n, start, finish)


def _gather_pass(fulls):
    n = len(fulls)

    def desc(bufs, ss, rs, j, w, landed):
        x, y, c = _coords()
        cx, cy = _other_chips(x, y)[j]
        shard = 2 * cx + cy
        return _remote(bufs[w].at[shard, c], bufs[w].at[shard, 1 - c if landed else c],
                       ss.at[j * n + w], rs.at[j * n + w], (x, y, 1 - c))

    def start(ins, outs, ss, rs):
        for j in range(3):
            for w in range(n):
                desc(outs, ss, rs, j, w, False).start()

    def finish(ins, outs, ss, rs):
        for j in range(3):
            for w in range(n):
                desc(outs, ss, rs, j, w, True).wait_recv()
                desc(outs, ss, rs, j, w, False).wait_send()

    return _Comm(fulls, [_sds(f.shape, f.dtype) for f in fulls], 3 * n, start, finish, aliased=True)


def _rs_sibling(gws):
    n = len(gws)

    def desc(ins, outs, ss, rs, w):
        x, y, c = _coords()
        return _remote(ins[w].at[:, 1 - c], outs[w], ss.at[w], rs.at[w], (x, y, 1 - c))

    def start(ins, outs, ss, rs):
        for w in range(n):
            desc(ins, outs, ss, rs, w).start()

    def finish(ins, outs, ss, rs):
        for w in range(n):
            desc(ins, outs, ss, rs, w).wait()

    out_shape = [_sds((N_SHARD, g.shape[1] // 2, g.shape[2]), g.dtype) for g in gws]
    return _Comm([_halves(g) for g in gws], out_shape, n, start, finish)


def _rs_chips(pbs):
    n = len(pbs)

    def desc(ins, outs, ss, rs, j, w):
        x, y, c = _coords()
        cx, cy = _other_chips(x, y)[j]
        return _remote(ins[w].at[2 * cx + cy], outs[w].at[j], ss.at[j * n + w], rs.at[j * n + w], (cx, cy, c))

    def start(ins, outs, ss, rs):
        for j in range(3):
            for w in range(n):
                desc(ins, outs, ss, rs, j, w).start()

    def finish(ins, outs, ss, rs):
        for j in range(3):
            for w in range(n):
                desc(ins, outs, ss, rs, j, w).wait()

    return _Comm(pbs, [_sds((3,) + p.shape[1:], p.dtype) for p in pbs], 3 * n, start, finish)


def _rs_join(halves):
    n = len(halves)

    def desc(bufs, ss, rs, w, landed):
        x, y, c = _coords()
        return _remote(bufs[w].at[c], bufs[w].at[1 - c if landed else c], ss.at[w], rs.at[w], (x, y, 1 - c))

    def start(ins, outs, ss, rs):
        for w in range(n):
            desc(outs, ss, rs, w, False).start()

    def finish(ins, outs, ss, rs):
        for w in range(n):
            desc(outs, ss, rs, w, True).wait_recv()
            desc(outs, ss, rs, w, False).wait_send()

    return _Comm(halves, [_sds(h.shape, h.dtype) for h in halves], n, start, finish, aliased=True)


def _small_exchange(buf):
    def desc(ins, outs, ss, rs, k, landed):
        x, y, c = _coords()
        px = 1 - x if (k >> 2) & 1 else x
        py = 1 - y if (k >> 1) & 1 else y
        pc = 1 - c if k & 1 else c
        slot = 4 * px + 2 * py + pc if landed else 4 * x + 2 * y + c
        return _remote(ins[0], outs[0].at[slot], ss.at[k - 1], rs.at[k - 1], (px, py, pc))

    def start(ins, outs, ss, rs):
        for k in range(1, 8):
            desc(ins, outs, ss, rs, k, False).start()

    def finish(ins, outs, ss, rs):
        for k in range(1, 8):
            desc(ins, outs, ss, rs, k, True).wait_recv()
            desc(ins, outs, ss, rs, k, False).wait_send()

    return _Comm([buf], [_sds((8,) + buf.shape, buf.dtype)], 7, start, finish)


HBM = pl.BlockSpec(memory_space=pltpu.HBM)
SEM = pl.BlockSpec(memory_space=pltpu.SEMAPHORE)
DATAFLOW = pltpu.SideEffectType.DATAFLOW_SIDE_EFFECTING


def _split_start(name, comm, lands, after):
    srcs = [pltpu.with_memory_space_constraint(s, pltpu.HBM) for s in comm.args]
    lands = [pltpu.with_memory_space_constraint(b, pltpu.HBM) for b in lands]
    ns, nb = len(srcs), len(lands)

    def body(*refs):
        send_sems, recv_sems = refs[ns + nb + 1], refs[ns + nb + 2]
        comm.start(refs[:ns], refs[ns:ns + nb], send_sems, recv_sems)
        refs[-1][...] = jnp.zeros_like(refs[-1])

    res = pl.pallas_call(
        body, name=name,
        out_shape=(pltpu.SemaphoreType.DMA((comm.n_sems,)), pltpu.SemaphoreType.DMA((comm.n_sems,)),
                   *[pltpu.HBM(b.shape, b.dtype) for b in srcs + lands], _sds((8, 128), F32)),
        in_specs=[HBM] * (ns + nb) + [ANY],
        out_specs=(SEM, SEM, *[HBM] * (ns + nb), pl.BlockSpec(memory_space=pltpu.VMEM)),
        input_output_aliases={i: 2 + i for i in range(ns + nb)},
        compiler_params=pltpu.CompilerParams(has_side_effects=DATAFLOW),
    )(*srcs, *lands, after)
    return res[0], res[1], list(res[2:2 + ns]), list(res[2 + ns:2 + ns + nb]), res[-1]


def _split_starts(name, comms, after):
    srcs = [[pltpu.with_memory_space_constraint(s, pltpu.HBM) for s in c.args] for c in comms]
    lands = [[pltpu.with_memory_space_constraint(lax.empty(o.shape, o.dtype), pltpu.HBM) for o in c.out_shape]
             for c in comms]
    bufs = [b for k in range(len(comms)) for b in srcs[k] + lands[k]]
    nb, nc = len(bufs), len(comms)

    def body(*refs):
        sems = refs[nb + 1:nb + 1 + 2 * nc]
        off = 0
        for k, c in enumerate(comms):
            ns, nl = len(srcs[k]), len(lands[k])
            c.start(refs[off:off + ns], refs[off + ns:off + ns + nl], sems[2 * k], sems[2 * k + 1])
            off += ns + nl
        refs[-1][...] = jnp.zeros_like(refs[-1])

    res = pl.pallas_call(
        body, name=name,
        out_shape=(*[pltpu.SemaphoreType.DMA((c.n_sems,)) for c in comms for _ in range(2)],
                   *[pltpu.HBM(b.shape, b.dtype) for b in bufs], _sds((8, 128), F32)),
        in_specs=[HBM] * nb + [ANY],
        out_specs=(*[SEM] * (2 * nc), *[HBM] * nb, pl.BlockSpec(memory_space=pltpu.VMEM)),
        input_output_aliases={i: 2 * nc + i for i in range(nb)},
        compiler_params=pltpu.CompilerParams(has_side_effects=DATAFLOW),
    )(*bufs, after)
    states, off = [], 2 * nc
    for k in range(nc):
        ns, nl = len(srcs[k]), len(lands[k])
        states.append((res[2 * k], res[2 * k + 1], list(res[off:off + ns]), list(res[off + ns:off + ns + nl])))
        off += ns + nl
    return states, res[-1]


def _split_wait(name, comm, send_sems, recv_sems, srcs, lands, after):
    ns, nb = len(srcs), len(lands)
    after = list(after) if isinstance(after, (list, tuple)) else [after]

    def body(*refs):
        comm.finish(refs[:ns], refs[ns:ns + nb], refs[ns + nb], refs[ns + nb + 1])

    res = pl.pallas_call(
        body, name=name,
        out_shape=tuple(pltpu.HBM(b.shape, b.dtype) for b in srcs + lands),
        in_specs=[HBM] * (ns + nb) + [SEM, SEM] + [ANY] * len(after), out_specs=tuple([HBM] * (ns + nb)),
        input_output_aliases={i: i for i in range(ns + nb)},
        compiler_params=pltpu.CompilerParams(has_side_effects=DATAFLOW),
    )(*srcs, *lands, send_sems, recv_sems, *after)
    return list(res[ns:])


LOSS_ROW = "loss_cols"
SMALL_EARLY = ("post_mix_norm", "pre_ffn_norm", "post_ffn_norm", "sgu_ln_gain", "sgu_ln_bias", "attn_out_norm",
               "sgu_out_norm", "sgu_w_spatial", "sgu_b_spatial", LOSS_ROW)
SMALL_LATE = ("pre_mix_norm",)


def _pack(d, names, rows):
    flat = [d[n].reshape(-1) for n in names]
    used = sum(f.shape[0] for f in flat)
    flat.append(jnp.zeros((rows * 1024 - used,), F32))
    return jnp.concatenate(flat).reshape(rows, 1024)


def _unpack(buf, names, shapes):
    flat = buf.reshape(-1)
    out, off = {}, 0
    for n in names:
        size = int(np.prod(shapes[n]))
        out[n] = flat[off:off + size].reshape(shapes[n])
        off += size
    return out


def _merge(comms):
    def run(phase):
        def go(ins, outs, ss, rs):
            ii = oi = si = 0
            for c in comms:
                getattr(c, phase)(ins[ii:ii + len(c.args)], outs[oi:oi + len(c.out_shape)],
                                  ss.at[pl.ds(si, c.n_sems)], rs.at[pl.ds(si, c.n_sems)])
                ii += len(c.args)
                oi += len(c.out_shape)
                si += c.n_sems
        return go

    return _Comm([a for c in comms for a in c.args], [o for c in comms for o in c.out_shape],
                 sum(c.n_sems for c in comms), run("start"), run("finish"))


def _chip_sums(gws, recvs, shard_core):
    n = len(gws)
    _, rows, cols = gws[0].shape
    hw = rows // 2

    def body(sc_ref, *refs):
        for k in range(n):
            refs[2 * n + k][...] = (refs[k][...] + refs[n + k][...]).astype(BF16)

    def other(s, sc):
        return jnp.where(s >= sc[0], s + 1, s)

    mine = pl.BlockSpec((1, hw, cols), lambda s, sc: (other(s, sc), sc[1], 0))
    plain = pl.BlockSpec((1, hw, cols), lambda s, sc: (other(s, sc), 0, 0))
    return pl.pallas_call(
        body, name="rs_chip_sums",
        grid_spec=pltpu.PrefetchScalarGridSpec(num_scalar_prefetch=1, grid=(N_SHARD - 1,),
                                               in_specs=[mine] * n + [plain] * n, out_specs=[plain] * n),
        out_shape=[_sds((N_SHARD, hw, cols), BF16)] * n,
        compiler_params=_seq_params(),
    )(shard_core, *gws, *recvs)


def _final_sums(gws, recv_sibs, recv_chips, shard_core):
    n = len(gws)
    _, rows, cols = gws[0].shape
    hw = rows // 2

    def body(sc_ref, *refs):
        for k in range(n):
            acc = refs[k][0] + refs[n + k][0]
            for j in range(3):
                acc = acc + refs[2 * n + k][j].astype(F32)
            refs[3 * n + k][0] = acc

    return pl.pallas_call(
        body, name="rs_final_sums",
        grid_spec=pltpu.PrefetchScalarGridSpec(
            num_scalar_prefetch=1, grid=(1,),
            in_specs=[pl.BlockSpec((1, hw, cols), lambda i, sc: (sc[0], sc[1], 0))] * n
            + [pl.BlockSpec((1, hw, cols), lambda i, sc: (sc[0], 0, 0))] * n
            + [pl.BlockSpec((3, hw, cols), lambda i, sc: (0, 0, 0))] * n,
            out_specs=[pl.BlockSpec((1, hw, cols), lambda i, sc: (sc[1], 0, 0))] * n),
        out_shape=[_sds((2, hw, cols), F32)] * n,
        compiler_params=_seq_params(),
    )(shard_core, *gws, *recv_sibs, *recv_chips)


def _adamw_multi(ws, gs, ms, vs, block_rows, name, after=()):
    n = len(ws)
    rows, cols = ws[0].shape

    def body(*refs):
        outs = refs[4 * n + len(after):]
        for k in range(n):
            g = refs[n + k][...]
            d, m, v = _adam_math(refs[k][...], g, refs[2 * n + k][...], refs[3 * n + k][...])
            outs[4 * k][...], outs[4 * k + 1][...], outs[4 * k + 2][...], outs[4 * k + 3][...] = g, d, m, v

    spec = pl.BlockSpec((block_rows, cols), lambda i: (i, 0))
    res = pl.pallas_call(
        body, name=name, grid=(rows // block_rows,), in_specs=[spec] * (4 * n) + [ANY] * len(after),
        out_specs=[spec] * (4 * n), out_shape=[_sds((rows, cols), F32)] * (4 * n),
        compiler_params=_seq_params(),
    )(*ws, *gs, *ms, *vs, *after)
    return [tuple(res[4 * k:4 * k + 4]) for k in range(n)]


def _wgrad_ff(a_list, b, name, comms=()):
    n = len(a_list)
    cols = 256

    def body(*refs):
        bv = refs[n][...]
        for k in range(n):
            refs[n + 1 + k][...] = _dot_tn(refs[k][...], bv)

    res = _pcall(
        body, name=name, grid=(FF // cols,),
        in_specs=[pl.BlockSpec((SEQ, cols), lambda j: (0, j))] * n
        + [pl.BlockSpec((SEQ, D_MODEL), lambda j: (0, 0), pipeline_mode=pl.Buffered(1))],
        out_specs=[pl.BlockSpec((cols, D_MODEL), lambda j: (j, 0))] * n,
        out_shape=[_sds((FF, D_MODEL), F32)] * n, args=(*a_list, b), comms=comms)
    mine, theirs = res if comms else (res, None)
    mine = [m.reshape(N_SHARD, FF_S, D_MODEL) for m in mine]
    return (mine, theirs) if comms else mine


def _wgrad_pair(a1, a2, b, a_spec, b_spec, out_block, name, comms=()):
    def body(a1_ref, a2_ref, b_ref, o1_ref, o2_ref):
        bv = b_ref[...]
        o1_ref[0] = _dot_tn(a1_ref[0], bv)
        o2_ref[0] = _dot_tn(a2_ref[0], bv)

    out_spec = pl.BlockSpec((1,) + out_block, lambda s: (s, 0, 0))
    return _pcall(
        body, name=name, grid=(N_SHARD,), in_specs=[a_spec, a_spec, b_spec], out_specs=[out_spec, out_spec],
        out_shape=[_sds((N_SHARD,) + out_block, F32)] * 2, args=(a1, a2, b), comms=comms)


def _comm_only(name, comms):
    return _pcall(lambda: None, name=name, grid=(1,), in_specs=[], out_specs=[], out_shape=[], args=(),
                  comms=comms)[1]


def _adam_math(w, g, m, v):
    m = ADAM_B1 * m + (1.0 - ADAM_B1) * g
    v = ADAM_B2 * v + (1.0 - ADAM_B2) * (g * g)
    m_hat = m / (1.0 - ADAM_B1 ** ADAM_STEP)
    v_hat = v / (1.0 - ADAM_B2 ** ADAM_STEP)
    return -ADAM_LR * (m_hat / (jnp.sqrt(v_hat) + ADAM_EPS) + ADAM_WD * w), m, v


def _adamw_small(own, slots, w, m, v, me):
    rows, cols = own.shape

    def body(me_ref, own_ref, slots_ref, w_ref, m_ref, v_ref, g_ref, d_ref, nm_ref, nv_ref):
        own_v = own_ref[...]
        g = jnp.where(me_ref[0] == 0, own_v, slots_ref[0])
        for i in range(1, 8):
            g = g + jnp.where(me_ref[0] == i, own_v, slots_ref[i])
        g_ref[...] = g
        d_ref[...], nm_ref[...], nv_ref[...] = _adam_math(w_ref[...], g, m_ref[...], v_ref[...])

    flat = pl.BlockSpec((rows, cols), lambda i, me_ref: (0, 0))
    return pl.pallas_call(
        body, name="adamw_small",
        grid_spec=pltpu.PrefetchScalarGridSpec(
            num_scalar_prefetch=1, grid=(1,),
            in_specs=[flat, pl.BlockSpec((8, rows, cols), lambda i, me_ref: (0, 0, 0)), flat, flat, flat],
            out_specs=[flat] * 4),
        out_shape=[_sds((rows, cols), F32)] * 4,
        compiler_params=_seq_params(),
    )(me, own, slots, w, m, v)


def kernel(x, positions, pre_mix_norm, w_in, sgu_ln_gain, sgu_ln_bias, sgu_w_spatial, sgu_b_spatial, attn_out_norm, sgu_out_norm, w_out, post_mix_norm, pre_ffn_norm, w_gate, w_up, w_down, post_ffn_norm, loss_target, m_pre_mix_norm, m_w_in, m_sgu_ln_gain, m_sgu_ln_bias, m_sgu_w_spatial, m_sgu_b_spatial, m_attn_out_norm, m_sgu_out_norm, m_w_out, m_post_mix_norm, m_pre_ffn_norm, m_w_gate, m_w_up, m_w_down, m_post_ffn_norm, v_pre_mix_norm, v_w_in, v_sgu_ln_gain, v_sgu_ln_bias, v_sgu_w_spatial, v_sgu_b_spatial, v_attn_out_norm, v_sgu_out_norm, v_w_out, v_post_mix_norm, v_pre_ffn_norm, v_w_gate, v_w_up, v_w_down, v_post_ffn_norm):
    a = dict(locals())
    cx, cy, cc = _coords()
    s_me = 2 * cx + cy
    core = jnp.stack([cc]).astype(jnp.int32)
    shard_core = jnp.stack([s_me, cc]).astype(jnp.int32)
    me = jnp.stack([4 * cx + 2 * cy + cc]).astype(jnp.int32)
    small = {n: (a[n][0] if a[n].ndim > 2 else a[n]) for n in SMALL}
    b_t = small["sgu_b_spatial"].T
    xs, pos, target = x[0], positions.reshape(SEQ, 1), loss_target[0]
    flipped = ("w_gate", "w_up")

    def big(name, n):
        return jnp.swapaxes(a[name], 1, 2)[0] if n in flipped else a[name][0]

    own = {"w_in": _halves(big("w_in", "w_in").astype(BF16))}

    def with_own(full, n):
        full = lax.dynamic_update_slice(full, own[n][None], (s_me, 0, 0, 0))
        return full.reshape((N_SHARD,) + big(n, n).shape)

    ffn = ("w_gate", "w_up", "w_down")
    g_in = _gather_ici([own["w_in"]])
    (s_in,), token = _split_starts("gather_in_start", [g_in], small["pre_mix_norm"])
    for n in ("w_out",) + ffn:
        own[n] = _halves((big(n, n) + token[0:1, 0:1]).astype(BF16))
    g_out, g_ffn = _gather_ici([own["w_out"]]), _gather_ici([own[n] for n in ffn])
    (s_out, s_ffn), token = _split_starts("gather_rest_start", [g_out, g_ffn], token)
    in_lands = _split_wait("gather_in_wait", g_in, *s_in, token)
    ((in_lands,),) = _comm_only("comm_pass_in", [_gather_pass(in_lands)])
    w_in_f = with_own(in_lands, "w_in")
    h, u, vs, *qkv = _inproj_fwd(xs, pos, small["pre_mix_norm"], w_in_f)
    views = [tuple(qkv[3 * i:3 * i + 3]) for i in range(len(DILATIONS))]
    o_list, l_list = [], []
    for dil, (qv, kv, vv) in zip(DILATIONS, views):
        o, l = _attn_fwd(qv, kv, vv, dil)
        o_list.append(o)
        l_list.append(l)
    out_lands = _split_wait("gather_out_wait", g_out, *s_out, l_list[-1])
    sgu, ((out_lands,),) = _sgu_fwd(u, vs, small["sgu_ln_gain"], small["sgu_ln_bias"], small["sgu_w_spatial"], b_t,
                                    comms=[_gather_pass(out_lands)])
    w_out_f = with_own(out_lands, "w_out")
    ffn_lands = _split_wait("gather_ffn_wait", g_ffn, *s_ffn, sgu)
    (attn, mixed, y, x1, *lses), (ffn_lands,) = _mix_out_fwd(
        o_list, l_list, sgu, xs, w_out_f, small["attn_out_norm"], small["sgu_out_norm"], small["post_mix_norm"],
        comms=[_gather_pass(ffn_lands)])
    w_gate_f, w_up_f, w_down_f = (with_own(f, n).reshape(FF, D_MODEL) for f, n in zip(ffn_lands, ffn))
    h2, act, dg, dup, df, dx1, loss_cols, d_pre_ffn, d_post_ffn = _ffn_fwd_bwd(
        x1, target, w_gate_f, w_up_f, w_down_f, small["pre_ffn_norm"], small["post_ffn_norm"])

    full_tok = pl.BlockSpec((SEQ, D_MODEL), lambda s: (0, 0), pipeline_mode=pl.Buffered(1))
    ff_tok = pl.BlockSpec((1, SEQ, FF_S), lambda s: (s, 0, 0))
    gw = {}
    gw["w_gate"], gw["w_up"] = _wgrad_ff([dg, dup], h2, "wgrad_gate_up")
    (gw["w_down"],), ((sib_gate,),) = _wgrad_ff([act], df, "wgrad_down", comms=[_rs_sibling([gw["w_gate"]])])
    (dy, dsgu, d_post_mix, d_attn_norm, d_sgu_norm, *dviews), ((sib_up, sib_down),) = _outproj_bwd(
        dx1, y, attn, sgu, w_out_f, small["post_mix_norm"], small["attn_out_norm"],
        small["sgu_out_norm"], comms=[_rs_sibling([gw["w_up"], gw["w_down"]])])
    sib = {"w_gate": sib_gate, "w_up": sib_up, "w_down": sib_down}
    part = dict(zip(ffn, _chip_sums([gw[n] for n in ffn], [sib[n] for n in ffn], shard_core)))
    gw["w_out"] = _wgrad(mixed, dy, pl.BlockSpec((SEQ, OUT_S), lambda s: (0, s)), full_tok, (OUT_S, D_MODEL),
                         "wgrad_out")
    x_ffn = _rs_chips([part[n] for n in ffn])
    (s_ffn,), token = _split_starts("rs_ffn_start", [x_ffn], small["pre_mix_norm"])
    (du, dvs_sgu, d_w_sp, d_b_sp, d_ln_gain, d_ln_bias), ((sib["w_out"],),) = _sgu_bwd(
        u, vs, dsgu, small["sgu_ln_gain"], small["sgu_ln_bias"], small["sgu_w_spatial"], b_t,
        comms=[_rs_sibling([gw["w_out"]])], after=[token])
    (part["w_out"],) = _chip_sums([gw["w_out"]], [sib["w_out"]], shard_core)
    packed_early = _pack({
        "sgu_ln_gain": d_ln_gain, "sgu_ln_bias": d_ln_bias, "sgu_w_spatial": d_w_sp, "sgu_b_spatial": d_b_sp,
        "attn_out_norm": d_attn_norm, "sgu_out_norm": d_sgu_norm, "post_mix_norm": d_post_mix,
        "pre_ffn_norm": d_pre_ffn, "post_ffn_norm": d_post_ffn, LOSS_ROW: loss_cols}, SMALL_EARLY, SMALL_ROWS)
    x_out, x_small = _rs_chips([part["w_out"]]), _small_exchange(packed_early)
    (s_out, s_small), token = _split_starts("rs_out_small_start", [x_out, x_small], token)

    dqs, dks, dvs = [], [], []
    for i, (dil, (qv, kv, vv)) in enumerate(zip(DILATIONS, views)):
        dq, dk, dv = _attn_bwd(qv, kv, vv, dviews[i], dviews[3 + i], lses[i], dil, after=[token])
        dqs.append(dq)
        dks.append(dk)
        dvs.append(dv)
    half, far, joined = {}, {}, {}
    far.update(zip(ffn, _split_wait("rs_ffn_wait", x_ffn, *s_ffn, dvs[-1])))
    half.update(zip(ffn, _final_sums([gw[n] for n in ffn], [sib[n] for n in ffn], [far[n] for n in ffn],
                                     shard_core)))
    dproj, grad_x, d_pre_mix = _inproj_bwd(dqs, dks, dvs, du, dvs_sgu, pos, xs, dx1, w_in_f, small["pre_mix_norm"])
    (far["w_out"],) = _split_wait("rs_out_wait", x_out, *s_out, grad_x)
    (slots_early,) = _split_wait("small_early_wait", x_small, *s_small, far["w_out"])
    half["w_out"] = _rs_final_sum(gw["w_out"], sib["w_out"], far["w_out"], shard_core)
    packed_late = _pack({"pre_mix_norm": d_pre_mix}, SMALL_LATE, 8)
    names = ffn + ("w_out",)
    gw["w_in"], (got, (slots_late,)) = _wgrad(
        h, dproj, full_tok, pl.BlockSpec((SEQ, IN_S), lambda s: (0, s)), (D_MODEL, IN_S), "wgrad_in",
        comms=[_rs_join([half[n] for n in names]), _small_exchange(packed_late)])
    joined.update(zip(names, got))
    ((sib["w_in"],),) = _comm_only("comm_rs_sibling_in", [_rs_sibling([gw["w_in"]])])
    (part["w_in"],) = _chip_sums([gw["w_in"]], [sib["w_in"]], shard_core)
    x_in = _rs_chips([part["w_in"]])
    (s_in,), token = _split_starts("rs_in_start", [x_in], small["pre_mix_norm"])

    grads, deltas, new_m, new_v = {}, {}, {}, {}

    def record(n, outs):
        grads[n], deltas[n], new_m[n], new_v[n] = (
            jnp.swapaxes(o[None], 1, 2) if n in flipped else o[None] for o in outs)

    def update(names, block_rows, name, after):
        for n, outs in zip(names, _adamw_multi(
                [big(n, n) for n in names], [joined[n].reshape(big(n, n).shape) for n in names],
                [big("m_" + n, n) for n in names], [big("v_" + n, n) for n in names], block_rows, name, after)):
            record(n, outs)

    update(ffn, FF_S // 4, "adamw_ffn", [token])
    update(("w_out",), BIG_ADAM_ROWS["w_out"], "adamw_w_out", [token])
    (far["w_in"],) = _split_wait("rs_in_wait", x_in, *s_in, [new_v[n] for n in ("w_down", "w_out")])
    half["w_in"] = _rs_final_sum(gw["w_in"], sib["w_in"], far["w_in"], shard_core)
    ((joined["w_in"],),) = _comm_only("comm_rs_join_in", [_rs_join([half["w_in"]])])
    update(("w_in",), BIG_ADAM_ROWS["w_in"], "adamw_w_in", [])
    a[LOSS_ROW] = a["m_" + LOSS_ROW] = a["v_" + LOSS_ROW] = jnp.zeros((1, D_MODEL), F32)
    for names, rows, packed, slots in ((SMALL_EARLY, SMALL_ROWS, packed_early, slots_early),
                                       (SMALL_LATE, 8, packed_late, slots_late)):
        outs = _adamw_small(packed, slots, _pack(a, names, rows), _pack({n: a["m_" + n] for n in names}, names, rows),
                            _pack({n: a["v_" + n] for n in names}, names, rows), me)
        for dst, buf in zip((grads, deltas, new_m, new_v), outs):
            dst.update(_unpack(buf, names, {n: a[n].shape for n in names}))
    loss = jnp.sum(grads[LOSS_ROW]) * np.float32(0.5 / D_MODEL)
    return (loss, grad_x[None], *[grads[n] for n in WEIGHTS], *[deltas[n] for n in WEIGHTS],
            *[new_m[n] for n in WEIGHTS], *[new_v[n] for n in WEIGHTS])
```

```python
import numpy as np
import jax
import jax.numpy as jnp
from jax import lax
from jax.experimental import pallas as pl
from jax.experimental.pallas import tpu as pltpu

F32 = jnp.float32
BF16 = jnp.bfloat16

SEQ = 2048
D_MODEL = 1024
HEAD_DIM = 64
ATTN_W = 512
SGU_W = 512
SGU_GROUPS = 8
CHUNK = 128
DILATIONS = (1, 4, 16)
N_SHARD = 4
IN_S = 640
OUT_S = 256
FF_S = 704
PROJ_W = N_SHARD * IN_S
FF = N_SHARD * FF_S
FF_CHUNKS = ((0, 1024), (1024, 2048), (2048, FF))
RMS_EPS = 1e-6
LN_EPS = 1e-5
ROPE_THETA = 500000.0
ATTN_SCALE = 1.0 / np.sqrt(HEAD_DIM)
NEG = -1e30
TM = 512
TM_FFN = 256
VMEM_LIMIT = 56 * 1024 * 1024
SMALL_ROWS = 136

ADAM_LR = 0.001
ADAM_B1 = 0.9
ADAM_B2 = 0.999
ADAM_EPS = 1e-08
ADAM_WD = 0.01
ADAM_STEP = 10

MESH = pl.DeviceIdType.MESH
ANY = pl.BlockSpec(memory_space=pl.ANY)


def _dot(a, b):
    return jnp.dot(a, b, preferred_element_type=F32)


def _dot_nt(a, b):
    return lax.dot_general(a, b, (((1,), (1,)), ((), ())), preferred_element_type=F32)


def _dot_tn(a, b):
    return lax.dot_general(a, b, (((0,), (0,)), ((), ())), preferred_element_type=F32)


def _dot_exact(a, b):
    return jnp.dot(a, b, preferred_element_type=F32, precision=lax.Precision.HIGHEST)


def _dot_select(a, sel):
    hi = a.astype(BF16)
    lo = (a - hi.astype(F32)).astype(BF16)
    sel = sel.astype(BF16)
    return _dot(hi, sel) + _dot(lo, sel)


def _rms_stats(x):
    r = lax.rsqrt(jnp.mean(x * x, axis=-1, keepdims=True) + RMS_EPS)
    return x * r, r


def _rms_bwd(xh, r, gain, dy):
    dxh = dy * gain
    dx = r * (dxh - xh * jnp.mean(dxh * xh, axis=-1, keepdims=True))
    return dx, jnp.sum(dy * xh, axis=0, keepdims=True)


_ERF_ALPHA = (-2.72614225801306e-10, 2.77068142495902e-08, -2.10102402082508e-06, -5.69250639462346e-05,
              -7.34990630326855e-04, -2.95459980854025e-03, -1.60960333262415e-02)
_ERF_BETA = (-1.45660718464996e-05, -2.13374055278905e-04, -1.68282697438203e-03, -7.37332916720468e-03,
             -1.42647390514189e-02)


def _erf(x):
    x = jnp.clip(x, -4.0, 4.0)
    x2 = x * x
    p = jnp.full_like(x, _ERF_ALPHA[0])
    for a in _ERF_ALPHA[1:]:
        p = p * x2 + a
    q = jnp.full_like(x, _ERF_BETA[0])
    for b in _ERF_BETA[1:]:
        q = q * x2 + b
    return x * p / q


def _normal_cdf(x):
    return 0.5 * (1.0 + _erf(x * np.float32(1.0 / np.sqrt(2.0))))


def _gelu_grad(x, cdf):
    pdf = jnp.exp(-0.5 * x * x) * np.float32(1.0 / np.sqrt(2.0 * np.pi))
    return cdf + x * pdf


def _sigmoid(x):
    return 1.0 / (1.0 + jnp.exp(-x))


_INV_FREQ = tuple(float(np.float32(ROPE_THETA ** (-2.0 * j / 16.0))) for j in range(8))


def _rot_tables(pos):
    lane = lax.broadcasted_iota(jnp.int32, (1, 128), 1)
    d = lane & 63
    j = d & 7
    inv = jnp.zeros((1, 128), F32)
    for jj in range(8):
        inv = jnp.where(j == jj, _INV_FREQ[jj], inv)
    ang = pos.astype(F32) * inv
    c = jnp.cos(ang)
    s = jnp.sin(ang)
    cos_t = jnp.where(d < 16, c, 1.0)
    sin_a = jnp.where(d < 8, -s, 0.0)
    sin_b = jnp.where((d >= 8) & (d < 16), s, 0.0)
    return tuple(jnp.tile(t, (1, 4)) for t in (cos_t, sin_a, sin_b))


def _rope(x, tabs):
    cos_t, sin_a, sin_b = tabs
    return x * cos_t + pltpu.roll(x, 504, 1) * sin_a + pltpu.roll(x, 8, 1) * sin_b


def _rope_bwd(dy, tabs):
    cos_t, sin_a, sin_b = tabs
    return dy * cos_t + pltpu.roll(dy * sin_a, 8, 1) + pltpu.roll(dy * sin_b, 504, 1)


def _left_half():
    return lax.broadcasted_iota(jnp.int32, (CHUNK, CHUNK), 1) < HEAD_DIM


def _group_ones():
    lane = lax.broadcasted_iota(jnp.int32, (SGU_GROUPS, SGU_W), 1)
    row = lax.broadcasted_iota(jnp.int32, (SGU_GROUPS, SGU_W), 0)
    return ((lane >> 6) == row).astype(F32)


def _masked_spatial(w_ref):
    row = lax.broadcasted_iota(jnp.int32, (CHUNK, CHUNK), 0)
    col = lax.broadcasted_iota(jnp.int32, (CHUNK, CHUNK), 1)
    return [jnp.where(col <= row, w_ref[g], 0.0).astype(BF16) for g in range(SGU_GROUPS)]


def _sgu_core(u, vs, lg, lb, wm, bias_full):
    tm = u.shape[0]
    cdf_u, cdf_vs = _normal_cdf(u), _normal_cdf(vs)
    gu = u * cdf_u
    gv = vs * cdf_vs
    mu = jnp.mean(gv, axis=-1, keepdims=True)
    xc = gv - mu
    rstd = lax.rsqrt(jnp.mean(xc * xc, axis=-1, keepdims=True) + LN_EPS)
    xh = xc * rstd
    vnb = (xh * lg + lb).astype(BF16)
    left = _left_half()
    rows = []
    for c in range(tm // CHUNK):
        pieces = []
        for p in range(4):
            vp = vnb[c * CHUNK:(c + 1) * CHUNK, p * 128:(p + 1) * 128]
            pieces.append(jnp.where(left, _dot(wm[2 * p], vp), _dot(wm[2 * p + 1], vp)))
        rows.append(jnp.concatenate(pieces, axis=1) + bias_full)
    mixed = jnp.concatenate(rows, axis=0)
    return gu, xh, rstd, vnb, mixed, cdf_u, cdf_vs


def _resident(shape):
    n = len(shape)
    return pl.BlockSpec(shape, lambda *_: (0,) * n, pipeline_mode=pl.Buffered(1))


def _rows(ncol, tm=TM):
    return pl.BlockSpec((tm, ncol), lambda i: (i, 0))


def _rows3(nlead, ncol, tm=TM):
    return pl.BlockSpec((nlead, tm, ncol), lambda i: (0, i, 0))


def _acc(ncol, nrow=1):
    return pl.BlockSpec((nrow, ncol), lambda i: (0, 0))


HEAD_W = 128


def _view_rows(dil, width=ATTN_W, tm=TM):
    return pl.BlockSpec((tm // dil, dil * width), lambda i: (i, 0))


def _view_shape(dil, dtype, width=ATTN_W):
    return _sds((SEQ // dil, dil * width), dtype)


def _slab_scratch():
    return pltpu.VMEM((4, TM, 128), F32)


def _store_view(val, out_ref, slabs, dil):
    width = val.shape[1]
    for j in range(width // 128):
        slabs[j] = val[:, j * 128:(j + 1) * 128]
    for r in range(dil):
        for j in range(width // 128):
            c0 = r * width + j * 128
            out_ref[:, c0:c0 + 128] = slabs.at[j][pl.ds(r, TM // dil, stride=dil), :].astype(out_ref.dtype)


def _load_view(in_ref, slabs, dil, width=ATTN_W):
    for r in range(dil):
        for j in range(width // 128):
            c0 = r * width + j * 128
            slabs.at[j][pl.ds(r, TM // dil, stride=dil), :] = in_ref[:, c0:c0 + 128].astype(F32)
    return jnp.concatenate([slabs[j] for j in range(width // 128)], axis=1)


def _head_spread():
    m = lax.broadcasted_iota(jnp.int32, (HEAD_W, ATTN_W), 0)
    lane = lax.broadcasted_iota(jnp.int32, (HEAD_W, ATTN_W), 1)
    return (m == 16 * (lane >> 6)).astype(F32)


def _head_sum():
    lane = lax.broadcasted_iota(jnp.int32, (ATTN_W, HEAD_W), 0)
    m = lax.broadcasted_iota(jnp.int32, (ATTN_W, HEAD_W), 1)
    return ((lane >> 6) == (m >> 4)).astype(F32)


def _seq_params():
    return pltpu.CompilerParams(dimension_semantics=("arbitrary",), vmem_limit_bytes=VMEM_LIMIT)


def _sds(shape, dtype):
    return jax.ShapeDtypeStruct(shape, dtype)


class _Comm:
    def __init__(self, args, out_shape, n_sems, start, finish, aliased=False):
        self.args, self.out_shape, self.n_sems = list(args), list(out_shape), n_sems
        self.start, self.finish, self.aliased = start, finish, aliased


def _pcall(body, *, name, grid, in_specs, out_specs, out_shape, args, scratch_shapes=(), comms=(), after=()):
    single = not isinstance(out_shape, (list, tuple))
    out_specs = [out_specs] if single else list(out_specs)
    out_shape = [out_shape] if single else list(out_shape)
    n_in, n_out, n_scr = len(in_specs), len(out_shape), len(scratch_shapes)
    c_args = [a for c in comms for a in c.args]
    c_outs = [o for c in comms for o in c.out_shape]
    aliases, ai, ao = {}, n_in, n_out
    for c in comms:
        if c.aliased:
            aliases.update({ai + k: ao + k for k in range(len(c.args))})
        ai += len(c.args)
        ao += len(c.out_shape)
    sems = [pltpu.SemaphoreType.DMA((c.n_sems,)) for c in comms for _ in range(2)]
    steps = grid[0]

    def wrapped(*refs):
        o0 = n_in + len(c_args) + len(after)
        s0 = o0 + n_out + len(c_outs)
        m_in, m_out, m_sem = refs[n_in:n_in + len(c_args)], refs[o0 + n_out:s0], refs[s0 + n_scr:]

        def each(phase):
            ii = oi = 0
            for k, c in enumerate(comms):
                getattr(c, phase)(m_in[ii:ii + len(c.args)], m_out[oi:oi + len(c.out_shape)],
                                  m_sem[2 * k], m_sem[2 * k + 1])
                ii += len(c.args)
                oi += len(c.out_shape)

        if comms:
            @pl.when(pl.program_id(0) == 0)
            def _():
                each("start")

        body(*refs[:n_in], *refs[o0:o0 + n_out], *refs[s0:s0 + n_scr])

        if comms:
            @pl.when(pl.program_id(0) == steps - 1)
            def _():
                each("finish")

    res = pl.pallas_call(
        wrapped, name=name, grid=grid,
        in_specs=list(in_specs) + [ANY] * (len(c_args) + len(after)), out_specs=out_specs + [ANY] * len(c_outs),
        out_shape=out_shape + c_outs, scratch_shapes=list(scratch_shapes) + sems,
        input_output_aliases=aliases, compiler_params=_seq_params(),
    )(*args, *c_args, *after)
    mine = res[0] if single else list(res[:n_out])
    if not comms:
        return mine
    theirs, oi = [], n_out
    for c in comms:
        theirs.append(list(res[oi:oi + len(c.out_shape)]))
        oi += len(c.out_shape)
    return mine, theirs


def _inproj_fwd(x, pos, g_pre, w_in, comms=()):
    def body(x_ref, pos_ref, g_ref, w_ref, h_ref, u_ref, vs_ref, *rest):
        qkv_refs, slabs = rest[:9], rest[9:]
        xh, _ = _rms_stats(x_ref[...])
        h = (xh * g_ref[...]).astype(BF16)
        h_ref[...] = h
        proj = jnp.concatenate([_dot(h, w_ref[s]) for s in range(N_SHARD)], axis=1)
        tabs = _rot_tables(pos_ref[...])
        u_ref[...] = proj[:, 1536:2048]
        vs_ref[...] = proj[:, 2048:2560]
        qkv = (_rope(proj[:, 0:512], tabs) * np.float32(ATTN_SCALE), _rope(proj[:, 512:1024], tabs),
               proj[:, 1024:1536])
        for t, val in enumerate(qkv):
            qkv_refs[t][...] = val.astype(BF16)
            for i, dil in enumerate(DILATIONS[1:]):
                _store_view(val, qkv_refs[3 * (i + 1) + t], slabs[t], dil)

    return _pcall(
        body, name="inproj_fwd", grid=(SEQ // TM,),
        in_specs=[_rows(D_MODEL), _rows(1), _resident((1, D_MODEL)), _resident((N_SHARD, D_MODEL, IN_S))],
        out_specs=[_rows(D_MODEL), _rows(512), _rows(512)] + [_view_rows(dil) for dil in DILATIONS for _ in range(3)],
        out_shape=[_sds((SEQ, D_MODEL), BF16), _sds((SEQ, 512), F32), _sds((SEQ, 512), F32)]
        + [_view_shape(dil, BF16) for dil in DILATIONS for _ in range(3)],
        scratch_shapes=[_slab_scratch() for _ in range(3)],
        args=(x, pos, g_pre, w_in), comms=comms)


def _sgu_fwd(u, vs, lg, lb, w_sp, b_t, comms=()):
    def body(u_ref, vs_ref, lg_ref, lb_ref, w_ref, bt_ref, out_ref):
        wm = _masked_spatial(w_ref)
        bias_full = _dot_exact(bt_ref[...], _group_ones())
        gu, _, _, _, mixed, _, _ = _sgu_core(u_ref[...], vs_ref[...], lg_ref[...], lb_ref[...], wm, bias_full)
        out_ref[...] = gu * mixed

    return _pcall(
        body, name="sgu_fwd", grid=(SEQ // TM,),
        in_specs=[_rows(SGU_W), _rows(SGU_W), _resident((1, SGU_W)), _resident((1, SGU_W)),
                  _resident((SGU_GROUPS, CHUNK, CHUNK)), _resident((CHUNK, SGU_GROUPS))],
        out_specs=_rows(SGU_W),
        out_shape=_sds((SEQ, SGU_W), F32),
        args=(u, vs, lg, lb, w_sp, b_t), comms=comms)


def _block_masks():
    row = lax.broadcasted_iota(jnp.int32, (CHUNK, CHUNK), 0)
    col = lax.broadcasted_iota(jnp.int32, (CHUNK, CHUNK), 1)
    return col <= row, col >= row


def _attn_fwd(qv, kv, vv, dil, comms=()):
    seg = SEQ // dil
    nblk = seg // CHUNK
    rps = 4 if nblk == 1 else 1

    def body(q_ref, k_ref, v_ref, o_ref, l_ref):
        left = _left_half()
        m_cur, m_prev = _block_masks()
        zero = jnp.zeros((CHUNK, CHUNK), BF16)
        ones = (jnp.where(left, 1.0, 0.0).astype(BF16), jnp.where(left, 0.0, 1.0).astype(BF16))

        sides = tuple(enumerate((left, ~left)))

        def rows(b):
            if isinstance(b, int):
                return b * CHUNK, max(b - 1, 0) * CHUNK
            return pl.multiple_of(b * CHUNK, CHUNK), pl.multiple_of(jnp.maximum(b - 1, 0) * CHUNK, CHUNK)

        def first(rr, b):
            r0, rp = rows(b)
            prev_ok = m_prev & (b > 0)
            tiles, scores = [], []
            for hp in range(4):
                ls = slice(rr * ATTN_W + hp * 128, rr * ATTN_W + (hp + 1) * 128)
                qp = q_ref[pl.ds(r0, CHUNK), ls]
                kc = k_ref[pl.ds(r0, CHUNK), ls]
                kp = k_ref[pl.ds(rp, CHUNK), ls] if nblk > 1 else None
                tiles.append((ls, v_ref[pl.ds(r0, CHUNK), ls], v_ref[pl.ds(rp, CHUNK), ls] if nblk > 1 else None))
                for _, hm in sides:
                    qh = jnp.where(hm, qp, zero)
                    sc = jnp.where(m_cur, _dot_nt(qh, kc), NEG)
                    sp = jnp.where(prev_ok, _dot_nt(qh, kp), NEG) if nblk > 1 else None
                    scores.append((sc, sp))
            return tiles, scores

        def second(scores):
            probs = []
            for sc, sp in scores:
                if nblk > 1:
                    m = jnp.max(jnp.maximum(sc, sp), axis=-1, keepdims=True)
                    pc = jnp.exp(sc - m)
                    pp = jnp.exp(sp - m)
                    probs.append((m, pc.astype(BF16), pp.astype(BF16), (pc + pp).astype(BF16)))
                else:
                    m = jnp.max(sc, axis=-1, keepdims=True)
                    pc = jnp.exp(sc - m).astype(BF16)
                    probs.append((m, pc, None, pc))
            return probs

        def third(rr, b, tiles, probs):
            r0, _ = rows(b)
            for hp, (ls, vc, vp) in enumerate(tiles):
                acc = jnp.zeros((CHUNK, CHUNK), F32)
                den = jnp.zeros((CHUNK, CHUNK), F32)
                for side, hm in sides:
                    _, pc, pp, psum = probs[2 * hp + side]
                    acc = acc + _dot(pc, jnp.where(hm, vc, zero))
                    if nblk > 1:
                        acc = acc + _dot(pp, jnp.where(hm, vp, zero))
                    den = den + _dot(psum, ones[side])
                o_ref[pl.ds(r0, CHUNK), ls] = (acc / den).astype(o_ref.dtype)
                lse = jnp.where(left, probs[2 * hp][0], probs[2 * hp + 1][0]) + jnp.log(den)
                l_ref[pl.ds(r0, CHUNK), rr * HEAD_W + 32 * hp:rr * HEAD_W + 32 * hp + 32] = lse[:, 48:80]

        def run(units):
            data = [first(rr, b) for rr, b in units]
            probs = [second(scores) for _, scores in data]
            for (rr, b), (tiles, _), pr in zip(units, data, probs):
                third(rr, b, tiles, pr)

        if nblk == 1:
            run([(rr, 0) for rr in range(rps)])
        else:
            def one(b, carry):
                run([(0, b)])
                return carry

            lax.fori_loop(0, nblk, one, 0)

    spec = pl.BlockSpec((seg, rps * ATTN_W), lambda r: (0, r))
    return _pcall(
        body, name=f"attn_fwd_d{dil}", grid=(dil // rps,),
        in_specs=[spec, spec, spec], out_specs=[spec, pl.BlockSpec((seg, rps * HEAD_W), lambda r: (0, r))],
        out_shape=[_sds((seg, dil * ATTN_W), BF16), _sds((seg, dil * HEAD_W), F32)],
        args=(qv, kv, vv), comms=comms)


def _lane_left(nrows):
    return lax.broadcasted_iota(jnp.int32, (nrows, CHUNK), 1) < HEAD_DIM


def _attn_rows(b):
    if isinstance(b, int):
        return b * CHUNK, max(b - 1, 0) * CHUNK
    return pl.multiple_of(b * CHUNK, CHUNK), pl.multiple_of(jnp.maximum(b - 1, 0) * CHUNK, CHUNK)


def _attn_fwd_fused(qv, kv, vv, dil, comms=()):
    seg = SEQ // dil
    nblk = seg // CHUNK
    rps = 4 if nblk == 1 else 1

    def body(q_ref, k_ref, v_ref, o_ref, l_ref):
        left = _left_half()
        m_cur, m_prev = _block_masks()
        sides = tuple(enumerate((left, ~left)))
        zero = jnp.zeros((CHUNK, CHUNK), BF16)

        def first(rr, b, both):
            r0, rp = _attn_rows(b)
            keys = pl.ds(rp, 2 * CHUNK) if both else pl.ds(r0, CHUNK)
            ok = jnp.concatenate([m_prev, m_cur], axis=1) if both else m_cur
            tiles, scores = [], []
            for hp in range(4):
                ls = slice(rr * ATTN_W + hp * 128, rr * ATTN_W + (hp + 1) * 128)
                qp = q_ref[pl.ds(r0, CHUNK), ls]
                k2 = k_ref[keys, ls]
                tiles.append((ls, v_ref[keys, ls]))
                for _, hm in sides:
                    scores.append(jnp.where(ok, _dot_nt(jnp.where(hm, qp, zero), k2), NEG))
            return tiles, scores

        def second(scores):
            probs = []
            for s in scores:
                m = jnp.max(s, axis=-1, keepdims=True)
                probs.append((m, jnp.exp(s - m).astype(BF16)))
            return probs

        def third(rr, b, tiles, probs, both):
            r0, _ = _attn_rows(b)
            nk = 2 * CHUNK if both else CHUNK
            left_k = _lane_left(nk)
            ones = (jnp.where(left_k, 1.0, 0.0).astype(BF16), jnp.where(left_k, 0.0, 1.0).astype(BF16))
            zero_k = jnp.zeros((nk, CHUNK), BF16)
            for hp, (ls, v2) in enumerate(tiles):
                acc = jnp.zeros((CHUNK, CHUNK), F32)
                den = jnp.zeros((CHUNK, CHUNK), F32)
                for side in range(2):
                    p = probs[2 * hp + side][1]
                    acc = acc + _dot(p, jnp.where(left_k if side == 0 else ~left_k, v2, zero_k))
                    den = den + _dot(p, ones[side])
                o_ref[pl.ds(r0, CHUNK), ls] = (acc / den).astype(o_ref.dtype)
                lse = jnp.where(left, probs[2 * hp][0], probs[2 * hp + 1][0]) + jnp.log(den)
                l_ref[pl.ds(r0, CHUNK), rr * HEAD_W + 32 * hp:rr * HEAD_W + 32 * hp + 32] = lse[:, 48:80]

        def run(units, both):
            data = [first(rr, b, both) for rr, b in units]
            probs = [second(scores) for _, scores in data]
            for (rr, b), (tiles, _), pr in zip(units, data, probs):
                third(rr, b, tiles, pr, both)

        run([(rr, 0) for rr in range(rps)], False)
        if nblk > 1:
            def one(b, carry):
                run([(0, b)], True)
                return carry

            lax.fori_loop(1, nblk, one, 0)

    spec = pl.BlockSpec((seg, rps * ATTN_W), lambda r: (0, r))
    return _pcall(
        body, name=f"attn_fwd_d{dil}", grid=(dil // rps,),
        in_specs=[spec, spec, spec], out_specs=[spec, pl.BlockSpec((seg, rps * HEAD_W), lambda r: (0, r))],
        out_shape=[_sds((seg, dil * ATTN_W), BF16), _sds((seg, dil * HEAD_W), F32)],
        args=(qv, kv, vv), comms=comms)


def _mix_out_fwd(o_list, l_list, sgu, x, w_out, g_attn, g_sgu, g_post, comms=()):
    def body(o1, o2, o3, l1, l2, l3, sgu_ref, x_ref, w_ref, ga_ref, gs_ref, gp_ref,
             attn_ref, mixed_ref, y_ref, x1_ref, lse1_ref, lse2_ref, lse3_ref, slabs_a, slabs_b):
        os = [o1[...], _load_view(o2, slabs_a, DILATIONS[1]), _load_view(o3, slabs_b, DILATIONS[2])]
        ls = [l1[...], _load_view(l2, slabs_a, DILATIONS[1], HEAD_W), _load_view(l3, slabs_b, DILATIONS[2], HEAD_W)]
        m = jnp.maximum(jnp.maximum(ls[0], ls[1]), ls[2])
        es = [jnp.exp(l - m) for l in ls]
        den = es[0] + es[1] + es[2]
        spread = _head_spread()
        attn = sum(_dot_select(e / den, spread) * o for e, o in zip(es, os))
        attn_ref[...] = attn
        lse = m + jnp.log(den)
        lse1_ref[...] = lse
        _store_view(lse, lse2_ref, slabs_a, DILATIONS[1])
        _store_view(lse, lse3_ref, slabs_b, DILATIONS[2])
        ah, _ = _rms_stats(attn)
        sh, _ = _rms_stats(sgu_ref[...])
        mixed = jnp.concatenate([ah * ga_ref[...], sh * gs_ref[...]], axis=1).astype(BF16)
        mixed_ref[...] = mixed
        y = _dot(mixed[:, 0:OUT_S], w_ref[0])
        for s in range(1, N_SHARD):
            y = y + _dot(mixed[:, s * OUT_S:(s + 1) * OUT_S], w_ref[s])
        y_ref[...] = y
        yh, _ = _rms_stats(y)
        x1_ref[...] = x_ref[...] + yh * gp_ref[...]

    return _pcall(
        body, name="mix_out_fwd", grid=(SEQ // TM,),
        in_specs=[_view_rows(dil) for dil in DILATIONS] + [_view_rows(dil, HEAD_W) for dil in DILATIONS]
        + [_rows(512), _rows(D_MODEL), _resident((N_SHARD, OUT_S, D_MODEL)),
           _resident((1, 512)), _resident((1, 512)), _resident((1, D_MODEL))],
        out_specs=[_rows(512), _rows(D_MODEL), _rows(D_MODEL), _rows(D_MODEL)]
        + [_view_rows(dil, HEAD_W) for dil in DILATIONS],
        out_shape=[_sds((SEQ, 512), F32), _sds((SEQ, D_MODEL), BF16), _sds((SEQ, D_MODEL), F32),
                   _sds((SEQ, D_MODEL), F32)] + [_view_shape(dil, F32, HEAD_W) for dil in DILATIONS],
        scratch_shapes=[_slab_scratch(), _slab_scratch()],
        args=(*o_list, *l_list, sgu, x, w_out, g_attn, g_sgu, g_post), comms=comms)


def _ffn_fwd_bwd(x1, target, w_gate, w_up, w_down, g_pre, g_post, comms=()):
    def body(x1_ref, t_ref, wg_ref, wu_ref, wd_ref, gpf_ref, gpo_ref,
             h2_ref, a_ref, dg_ref, dup_ref, df_ref, dx1_ref, loss_ref, dgpf_ref, dgpo_ref, g_scr, up_scr):
        @pl.when(pl.program_id(0) == 0)
        def _():
            loss_ref[...] = jnp.zeros_like(loss_ref)
            dgpf_ref[...] = jnp.zeros_like(dgpf_ref)
            dgpo_ref[...] = jnp.zeros_like(dgpo_ref)

        x1 = x1_ref[...]
        gpf = gpf_ref[...]
        gpo = gpo_ref[...]
        xh, r = _rms_stats(x1)
        h2 = (xh * gpf).astype(BF16)
        h2_ref[...] = h2
        f = jnp.zeros((TM_FFN, D_MODEL), F32)
        for c0, c1 in FF_CHUNKS:
            g = _dot_nt(h2, wg_ref[c0:c1, :])
            up = _dot_nt(h2, wu_ref[c0:c1, :])
            g_scr[:, c0:c1] = g
            up_scr[:, c0:c1] = up
            a = (g * _sigmoid(g) * up).astype(BF16)
            a_ref[:, c0:c1] = a
            f = f + _dot(a, wd_ref[c0:c1, :])
        fh, rf = _rms_stats(f)
        diff = x1 + fh * gpo - t_ref[...]
        loss_ref[...] += jnp.sum(diff * diff, axis=0, keepdims=True)
        dout = diff * np.float32(1.0 / D_MODEL)
        df, dgpo = _rms_bwd(fh, rf, gpo, dout)
        dgpo_ref[...] += dgpo
        dfb = df.astype(BF16)
        df_ref[...] = dfb
        dh2 = jnp.zeros((TM_FFN, D_MODEL), F32)
        for c0, c1 in FF_CHUNKS:
            da = _dot_nt(dfb, wd_ref[c0:c1, :])
            g = g_scr[:, c0:c1]
            up = up_scr[:, c0:c1]
            sg = _sigmoid(g)
            dup = (da * (g * sg)).astype(BF16)
            dg = (da * up * (sg * (1.0 + g * (1.0 - sg)))).astype(BF16)
            dg_ref[:, c0:c1] = dg
            dup_ref[:, c0:c1] = dup
            dh2 = dh2 + _dot(dg, wg_ref[c0:c1, :]) + _dot(dup, wu_ref[c0:c1, :])
        dx, dgpf = _rms_bwd(xh, r, gpf, dh2)
        dgpf_ref[...] += dgpf
        dx1_ref[...] = dout + dx

    return _pcall(
        body, name="ffn_fwd_bwd", grid=(SEQ // TM_FFN,),
        in_specs=[_rows(D_MODEL, TM_FFN), _rows(D_MODEL, TM_FFN), _resident((FF, D_MODEL)),
                  _resident((FF, D_MODEL)), _resident((FF, D_MODEL)),
                  _resident((1, D_MODEL)), _resident((1, D_MODEL))],
        out_specs=[_rows(D_MODEL, TM_FFN), _rows(FF, TM_FFN), _rows(FF, TM_FFN), _rows(FF, TM_FFN),
                   _rows(D_MODEL, TM_FFN), _rows(D_MODEL, TM_FFN), _acc(D_MODEL), _acc(D_MODEL), _acc(D_MODEL)],
        out_shape=[_sds((SEQ, D_MODEL), BF16), _sds((SEQ, FF), BF16), _sds((SEQ, FF), BF16),
                   _sds((SEQ, FF), BF16), _sds((SEQ, D_MODEL), BF16), _sds((SEQ, D_MODEL), F32),
                   _sds((1, D_MODEL), F32), _sds((1, D_MODEL), F32), _sds((1, D_MODEL), F32)],
        scratch_shapes=[pltpu.VMEM((TM_FFN, FF), F32), pltpu.VMEM((TM_FFN, FF), F32)],
        args=(x1, target, w_gate, w_up, w_down, g_pre, g_post), comms=comms)


def _wgrad(a, b, a_spec, b_spec, out_block, name, comms=()):
    def body(a_ref, b_ref, o_ref):
        av = a_ref[0] if len(a_ref.shape) == 3 else a_ref[...]
        bv = b_ref[0] if len(b_ref.shape) == 3 else b_ref[...]
        o_ref[0] = _dot_tn(av, bv)

    return _pcall(
        body, name=name, grid=(N_SHARD,),
        in_specs=[a_spec, b_spec],
        out_specs=pl.BlockSpec((1,) + out_block, lambda s: (s, 0, 0)),
        out_shape=_sds((N_SHARD,) + out_block, F32),
        args=(a, b), comms=comms)


def _outproj_bwd(dx1, y, attn, sgu, w_out, g_post, g_attn, g_sgu, comms=()):
    def body(dx1_ref, y_ref, attn_ref, sgu_ref, w_ref, gp_ref, ga_ref, gs_ref,
             dy_ref, dsgu_ref, dgp_ref, dga_ref, dgs_ref, *rest):
        dattn_refs, delta_refs, (slabs_a, slabs_b) = rest[0:3], rest[3:6], rest[6:]

        @pl.when(pl.program_id(0) == 0)
        def _():
            dgp_ref[...] = jnp.zeros_like(dgp_ref)
            dga_ref[...] = jnp.zeros_like(dga_ref)
            dgs_ref[...] = jnp.zeros_like(dgs_ref)

        yh, ry = _rms_stats(y_ref[...])
        dy, dgp = _rms_bwd(yh, ry, gp_ref[...], dx1_ref[...])
        dgp_ref[...] += dgp
        dyb = dy.astype(BF16)
        dy_ref[...] = dyb
        dmixed = jnp.concatenate([_dot_nt(dyb, w_ref[s]) for s in range(N_SHARD)], axis=1)
        attn = attn_ref[...]
        ah, ra = _rms_stats(attn)
        dattn, dga = _rms_bwd(ah, ra, ga_ref[...], dmixed[:, 0:512])
        dga_ref[...] += dga
        sh, rs = _rms_stats(sgu_ref[...])
        dsgu, dgs = _rms_bwd(sh, rs, gs_ref[...], dmixed[:, 512:1024])
        dgs_ref[...] += dgs
        dsgu_ref[...] = dsgu
        delta = _dot_select(dattn * attn, _head_sum())
        dattn_refs[0][...] = dattn.astype(BF16)
        delta_refs[0][...] = delta
        for i, dil in enumerate(DILATIONS[1:]):
            _store_view(dattn, dattn_refs[i + 1], slabs_a, dil)
            _store_view(delta, delta_refs[i + 1], slabs_b, dil)

    return _pcall(
        body, name="outproj_bwd", grid=(SEQ // TM,),
        in_specs=[_rows(D_MODEL), _rows(D_MODEL), _rows(512), _rows(512), _resident((N_SHARD, OUT_S, D_MODEL)),
                  _resident((1, D_MODEL)), _resident((1, 512)), _resident((1, 512))],
        out_specs=[_rows(D_MODEL), _rows(512), _acc(D_MODEL), _acc(512), _acc(512)]
        + [_view_rows(dil) for dil in DILATIONS] + [_view_rows(dil, HEAD_W) for dil in DILATIONS],
        out_shape=[_sds((SEQ, D_MODEL), BF16), _sds((SEQ, 512), F32),
                   _sds((1, D_MODEL), F32), _sds((1, 512), F32), _sds((1, 512), F32)]
        + [_view_shape(dil, BF16) for dil in DILATIONS] + [_view_shape(dil, F32, HEAD_W) for dil in DILATIONS],
        scratch_shapes=[_slab_scratch(), _slab_scratch()],
        args=(dx1, y, attn, sgu, w_out, g_post, g_attn, g_sgu), comms=comms)


def _sgu_bwd(u, vs, dsgu, lg, lb, w_sp, b_t, comms=(), after=()):
    nsteps = SEQ // TM

    def body(u_ref, vs_ref, ds_ref, lg_ref, lb_ref, w_ref, bt_ref,
             du_ref, dvs_ref, dw_ref, db_ref, dlg_ref, dlb_ref, dbias_scr):
        i = pl.program_id(0)

        @pl.when(i == 0)
        def _():
            dw_ref[...] = jnp.zeros_like(dw_ref)
            dlg_ref[...] = jnp.zeros_like(dlg_ref)
            dlb_ref[...] = jnp.zeros_like(dlb_ref)
            dbias_scr[...] = jnp.zeros_like(dbias_scr)

        wm = _masked_spatial(w_ref)
        ones_g = _group_ones()
        bias_full = _dot_exact(bt_ref[...], ones_g)
        u = u_ref[...]
        vs = vs_ref[...]
        lg = lg_ref[...]
        gu, xh, rstd, vnb, mixed, cdf_u, cdf_vs = _sgu_core(u, vs, lg, lb_ref[...], wm, bias_full)
        dsgu = ds_ref[...]
        du_ref[...] = (dsgu * mixed * _gelu_grad(u, cdf_u)).astype(BF16)
        dmixed = dsgu * gu
        left = _left_half()
        dvn_rows = []
        for c in range(TM // CHUNK):
            rs = slice(c * CHUNK, (c + 1) * CHUNK)
            dm_c = dmixed[rs, :]
            dbias_scr[...] += dm_c
            pieces = []
            for p in range(4):
                ls = slice(p * 128, (p + 1) * 128)
                dmp = dm_c[:, ls]
                vp = vnb[rs, ls]
                dmb = dmp.astype(BF16)
                zero = jnp.zeros_like(dmb)
                dw_ref[2 * p] += _dot_nt(jnp.where(left, dmb, zero), vp)
                dw_ref[2 * p + 1] += _dot_nt(jnp.where(left, zero, dmb), vp)
                pieces.append(jnp.where(left, _dot_tn(wm[2 * p], dmb), _dot_tn(wm[2 * p + 1], dmb)))
            dvn_rows.append(jnp.concatenate(pieces, axis=1))
        dvn = jnp.concatenate(dvn_rows, axis=0)
        dlg_ref[...] += jnp.sum(dvn * xh, axis=0, keepdims=True)
        dlb_ref[...] += jnp.sum(dvn, axis=0, keepdims=True)
        dxh = dvn * lg
        dgv = rstd * (dxh - jnp.mean(dxh, axis=-1, keepdims=True) - xh * jnp.mean(dxh * xh, axis=-1, keepdims=True))
        dvs_ref[...] = (dgv * _gelu_grad(vs, cdf_vs)).astype(BF16)

        @pl.when(i == nsteps - 1)
        def _():
            row = lax.broadcasted_iota(jnp.int32, (CHUNK, CHUNK), 0)
            col = lax.broadcasted_iota(jnp.int32, (CHUNK, CHUNK), 1)
            for g in range(SGU_GROUPS):
                dw_ref[g] = jnp.where(col <= row, dw_ref[g], 0.0)
            db_ref[...] = lax.dot_general(ones_g, dbias_scr[...], (((1,), (1,)), ((), ())),
                                          preferred_element_type=F32, precision=lax.Precision.HIGHEST)

    return _pcall(
        body, name="sgu_bwd", grid=(nsteps,),
        in_specs=[_rows(SGU_W), _rows(SGU_W), _rows(SGU_W), _resident((1, SGU_W)), _resident((1, SGU_W)),
                  _resident((SGU_GROUPS, CHUNK, CHUNK)), _resident((CHUNK, SGU_GROUPS))],
        out_specs=[_rows(SGU_W), _rows(SGU_W), pl.BlockSpec((SGU_GROUPS, CHUNK, CHUNK), lambda i: (0, 0, 0)),
                   _acc(CHUNK, SGU_GROUPS), _acc(SGU_W), _acc(SGU_W)],
        out_shape=[_sds((SEQ, SGU_W), BF16), _sds((SEQ, SGU_W), BF16), _sds((SGU_GROUPS, CHUNK, CHUNK), F32),
                   _sds((SGU_GROUPS, CHUNK), F32), _sds((1, SGU_W), F32), _sds((1, SGU_W), F32)],
        scratch_shapes=[pltpu.VMEM((CHUNK, SGU_W), F32)],
        args=(u, vs, dsgu, lg, lb, w_sp, b_t), comms=comms, after=after)


def _attn_bwd_v1(qv, kv, vv, dov, deltav, lsev, dil, comms=(), after=()):
    seg = SEQ // dil
    nblk = seg // CHUNK
    rps = 4 if nblk == 1 else 1

    def body(q_ref, k_ref, v_ref, do_ref, dl_ref, lse_ref, dq_ref, dk_ref, dv_ref, dk_wait, dv_wait):
        left = _left_half()
        m_cur, m_prev = _block_masks()

        sides = tuple(enumerate((left, ~left)))
        zero = jnp.zeros((CHUNK, CHUNK), BF16)

        def rows(b):
            if isinstance(b, int):
                return b * CHUNK, max(b - 1, 0) * CHUNK
            return pl.multiple_of(b * CHUNK, CHUNK), pl.multiple_of(jnp.maximum(b - 1, 0) * CHUNK, CHUNK)

        def first(rr, b):
            r0, rp = rows(b)
            tiles, firsts = [], []
            for hp in range(4):
                ls = slice(rr * ATTN_W + hp * 128, rr * ATTN_W + (hp + 1) * 128)
                qp = q_ref[pl.ds(r0, CHUNK), ls]
                kc = k_ref[pl.ds(r0, CHUNK), ls]
                vc = v_ref[pl.ds(r0, CHUNK), ls]
                dop = do_ref[pl.ds(r0, CHUNK), ls]
                kp = k_ref[pl.ds(rp, CHUNK), ls] if nblk > 1 else None
                vp = v_ref[pl.ds(rp, CHUNK), ls] if nblk > 1 else None
                tiles.append((ls, kc, kp))
                for side, hm in sides:
                    qh = jnp.where(hm, qp, zero)
                    doh = jnp.where(hm, dop, zero)
                    cur = (_dot_nt(kc, qh), _dot_nt(vc, doh))
                    prev = (_dot_nt(kp, qh), _dot_nt(vp, doh)) if nblk > 1 else None
                    firsts.append((qh, doh, cur, prev))
            return tiles, firsts

        def second(rr, b, firsts):
            r0, _ = rows(b)
            own_ok, prev_ok = m_prev, m_cur & (b > 0)
            lanes = slice(rr * HEAD_W, (rr + 1) * HEAD_W)
            lse_t = lse_ref[pl.ds(r0, CHUNK), lanes].T
            dl_t = dl_ref[pl.ds(r0, CHUNK), lanes].T
            seconds = []
            for i, (qh, doh, cur, prev) in enumerate(firsts):
                lse_h = lse_t[16 * i:16 * i + 1, :]
                dl_h = dl_t[16 * i:16 * i + 1, :]
                pc = jnp.exp(jnp.where(own_ok, cur[0] - lse_h, NEG))
                out = [pc.astype(BF16), (pc * (cur[1] - dl_h)).astype(BF16), None, None]
                if nblk > 1:
                    pp = jnp.exp(jnp.where(prev_ok, prev[0] - lse_h, NEG))
                    out[2:] = [pp.astype(BF16), (pp * (prev[1] - dl_h)).astype(BF16)]
                seconds.append(out)
            return seconds

        def third(rr, b, tiles, firsts, seconds):
            r0, rp = rows(b)
            for hp, (ls, kc, kp) in enumerate(tiles):
                dq = jnp.zeros((CHUNK, CHUNK), F32)
                dkc = jnp.zeros((CHUNK, CHUNK), F32)
                dvc = jnp.zeros((CHUNK, CHUNK), F32)
                dkp = jnp.zeros((CHUNK, CHUNK), F32)
                dvp = jnp.zeros((CHUNK, CHUNK), F32)
                for side, hm in sides:
                    qh, doh, _, _ = firsts[2 * hp + side]
                    pcb, dsc, ppb, dsp = seconds[2 * hp + side]
                    dq = dq + _dot_tn(dsc, jnp.where(hm, kc, zero))
                    dkc = dkc + _dot(dsc, qh)
                    dvc = dvc + _dot(pcb, doh)
                    if nblk > 1:
                        dq = dq + _dot_tn(dsp, jnp.where(hm, kp, zero))
                        dkp = dkp + _dot(dsp, qh)
                        dvp = dvp + _dot(ppb, doh)
                dq_ref[pl.ds(r0, CHUNK), ls] = dq.astype(dq_ref.dtype)
                if nblk == 1:
                    dk_ref[pl.ds(r0, CHUNK), ls] = dkc.astype(dk_ref.dtype)
                    dv_ref[pl.ds(r0, CHUNK), ls] = dvc.astype(dv_ref.dtype)
                else:
                    @pl.when(b > 0)
                    def _():
                        dk_ref[pl.ds(rp, CHUNK), ls] = (dk_wait[:, ls] + dkp).astype(dk_ref.dtype)
                        dv_ref[pl.ds(rp, CHUNK), ls] = (dv_wait[:, ls] + dvp).astype(dv_ref.dtype)

                    dk_wait[:, ls] = dkc
                    dv_wait[:, ls] = dvc

        def run(units):
            data = [first(rr, b) for rr, b in units]
            probs = [second(rr, b, firsts) for (rr, b), (_, firsts) in zip(units, data)]
            for (rr, b), (tiles, firsts), seconds in zip(units, data, probs):
                third(rr, b, tiles, firsts, seconds)

        if nblk == 1:
            run([(rr, 0) for rr in range(rps)])
        else:
            def one(b, carry):
                run([(0, b)])
                return carry

            lax.fori_loop(0, nblk, one, 0)
            last = (nblk - 1) * CHUNK
            dk_ref[last:last + CHUNK, :] = dk_wait[...].astype(dk_ref.dtype)
            dv_ref[last:last + CHUNK, :] = dv_wait[...].astype(dv_ref.dtype)

    spec = pl.BlockSpec((seg, rps * ATTN_W), lambda r: (0, r))
    return _pcall(
        body, name=f"attn_bwd_d{dil}", grid=(dil // rps,),
        in_specs=[spec] * 4 + [pl.BlockSpec((seg, rps * HEAD_W), lambda r: (0, r))] * 2, out_specs=[spec] * 3,
        out_shape=[_sds((seg, dil * ATTN_W), BF16)] * 3,
        scratch_shapes=[pltpu.VMEM((CHUNK, ATTN_W), F32), pltpu.VMEM((CHUNK, ATTN_W), F32)],
        args=(qv, kv, vv, dov, deltav, lsev), comms=comms, after=after)


def _attn_bwd(qv, kv, vv, dov, deltav, lsev, dil, comms=(), after=()):
    seg = SEQ // dil
    nblk = seg // CHUNK
    rps = 4 if nblk == 1 else 1

    def body(q_ref, k_ref, v_ref, do_ref, dl_ref, lse_ref, dq_ref, dk_ref, dv_ref, dk_wait, dv_wait):
        left = _left_half()
        m_cur, m_prev = _block_masks()
        sides = tuple(enumerate((left, ~left)))
        zero = jnp.zeros((CHUNK, CHUNK), BF16)

        def first(rr, b, both):
            r0, rp = _attn_rows(b)
            keys = pl.ds(rp, 2 * CHUNK) if both else pl.ds(r0, CHUNK)
            tiles, heads = [], []
            for hp in range(4):
                ls = slice(rr * ATTN_W + hp * 128, rr * ATTN_W + (hp + 1) * 128)
                qp = q_ref[pl.ds(r0, CHUNK), ls]
                dop = do_ref[pl.ds(r0, CHUNK), ls]
                k2 = k_ref[keys, ls]
                v2 = v_ref[keys, ls]
                tiles.append((ls, k2))
                for _, hm in sides:
                    qh = jnp.where(hm, qp, zero)
                    doh = jnp.where(hm, dop, zero)
                    heads.append((qh, doh, _dot_nt(k2, qh), _dot_nt(v2, doh)))
            return tiles, heads

        def second(rr, b, heads, both):
            r0, _ = _attn_rows(b)
            ok = jnp.concatenate([m_cur, m_prev], axis=0) if both else m_prev
            lanes = slice(rr * HEAD_W, (rr + 1) * HEAD_W)
            lse_t = lse_ref[pl.ds(r0, CHUNK), lanes].T
            dl_t = dl_ref[pl.ds(r0, CHUNK), lanes].T
            out = []
            for i, (_, _, s_t, dp_t) in enumerate(heads):
                p = jnp.exp(jnp.where(ok, s_t - lse_t[16 * i:16 * i + 1, :], NEG))
                out.append((p.astype(BF16), (p * (dp_t - dl_t[16 * i:16 * i + 1, :])).astype(BF16)))
            return out

        def third(rr, b, tiles, heads, probs, both):
            r0, rp = _attn_rows(b)
            nk = 2 * CHUNK if both else CHUNK
            left_k = _lane_left(nk)
            zero_k = jnp.zeros((nk, CHUNK), BF16)
            for hp, (ls, k2) in enumerate(tiles):
                dq = jnp.zeros((CHUNK, CHUNK), F32)
                dk2 = jnp.zeros((nk, CHUNK), F32)
                dv2 = jnp.zeros((nk, CHUNK), F32)
                for side in range(2):
                    qh, doh, _, _ = heads[2 * hp + side]
                    p, ds = probs[2 * hp + side]
                    dq = dq + _dot_tn(ds, jnp.where(left_k if side == 0 else ~left_k, k2, zero_k))
                    dk2 = dk2 + _dot(ds, qh)
                    dv2 = dv2 + _dot(p, doh)
                dq_ref[pl.ds(r0, CHUNK), ls] = dq.astype(dq_ref.dtype)
                if nblk == 1:
                    dk_ref[pl.ds(r0, CHUNK), ls] = dk2.astype(dk_ref.dtype)
                    dv_ref[pl.ds(r0, CHUNK), ls] = dv2.astype(dv_ref.dtype)
                elif both:
                    dk_ref[pl.ds(rp, CHUNK), ls] = (dk_wait[:, ls] + dk2[0:CHUNK]).astype(dk_ref.dtype)
                    dv_ref[pl.ds(rp, CHUNK), ls] = (dv_wait[:, ls] + dv2[0:CHUNK]).astype(dv_ref.dtype)
                    dk_wait[:, ls] = dk2[CHUNK:]
                    dv_wait[:, ls] = dv2[CHUNK:]
                else:
                    dk_wait[:, ls] = dk2
                    dv_wait[:, ls] = dv2

        def run(units, both):
            data = [first(rr, b, both) for rr, b in units]
            probs = [second(rr, b, heads, both) for (rr, b), (_, heads) in zip(units, data)]
            for (rr, b), (tiles, heads), pr in zip(units, data, probs):
                third(rr, b, tiles, heads, pr, both)

        run([(rr, 0) for rr in range(rps)], False)
        if nblk > 1:
            def one(b, carry):
                run([(0, b)], True)
                return carry

            lax.fori_loop(1, nblk, one, 0)
            last = (nblk - 1) * CHUNK
            dk_ref[last:last + CHUNK, :] = dk_wait[...].astype(dk_ref.dtype)
            dv_ref[last:last + CHUNK, :] = dv_wait[...].astype(dv_ref.dtype)

    spec = pl.BlockSpec((seg, rps * ATTN_W), lambda r: (0, r))
    return _pcall(
        body, name=f"attn_bwd_d{dil}", grid=(dil // rps,),
        in_specs=[spec] * 4 + [pl.BlockSpec((seg, rps * HEAD_W), lambda r: (0, r))] * 2, out_specs=[spec] * 3,
        out_shape=[_sds((seg, dil * ATTN_W), BF16)] * 3,
        scratch_shapes=[pltpu.VMEM((CHUNK, ATTN_W), F32), pltpu.VMEM((CHUNK, ATTN_W), F32)],
        args=(qv, kv, vv, dov, deltav, lsev), comms=comms, after=after)


def _inproj_bwd(dqs, dks, dvs, du, dvs_sgu, pos, x, dx1, w_in, g_pre, comms=()):
    def body(dq1, dq2, dq3, dk1, dk2, dk3, dv1, dv2, dv3, du_ref, dvs_ref, pos_ref, x_ref, dx1_ref, w_ref, g_ref,
             dproj_ref, gx_ref, dg_ref, slabs_a, slabs_b):
        @pl.when(pl.program_id(0) == 0)
        def _():
            dg_ref[...] = jnp.zeros_like(dg_ref)

        def total(r1, r2, r3):
            return r1[...] + _load_view(r2, slabs_a, DILATIONS[1]) + _load_view(r3, slabs_b, DILATIONS[2])

        tabs = _rot_tables(pos_ref[...])
        dproj_ref[:, 0:512] = _rope_bwd(total(dq1, dq2, dq3) * np.float32(ATTN_SCALE), tabs).astype(BF16)
        dproj_ref[:, 512:1024] = _rope_bwd(total(dk1, dk2, dk3), tabs).astype(BF16)
        dproj_ref[:, 1024:1536] = total(dv1, dv2, dv3).astype(BF16)
        dproj_ref[:, 1536:2048] = du_ref[...]
        dproj_ref[:, 2048:2560] = dvs_ref[...]
        dh = jnp.zeros((TM, D_MODEL), F32)
        for s in range(N_SHARD):
            dh = dh + _dot_nt(dproj_ref[:, s * IN_S:(s + 1) * IN_S], w_ref[s])
        g = g_ref[...]
        xh, r = _rms_stats(x_ref[...])
        dx, dg = _rms_bwd(xh, r, g, dh)
        dg_ref[...] += dg
        gx_ref[...] = dx1_ref[...] + dx

    return _pcall(
        body, name="inproj_bwd", grid=(SEQ // TM,),
        in_specs=[_view_rows(dil) for dil in DILATIONS] * 3
        + [_rows(512), _rows(512), _rows(1), _rows(D_MODEL), _rows(D_MODEL),
           _resident((N_SHARD, D_MODEL, IN_S)), _resident((1, D_MODEL))],
        out_specs=[_rows(PROJ_W), _rows(D_MODEL), _acc(D_MODEL)],
        out_shape=[_sds((SEQ, PROJ_W), BF16), _sds((SEQ, D_MODEL), F32), _sds((1, D_MODEL), F32)],
        scratch_shapes=[_slab_scratch(), _slab_scratch()],
        args=(*dqs, *dks, *dvs, du, dvs_sgu, pos, x, dx1, w_in, g_pre), comms=comms)


def _to_view(a, dil):
    return a if dil == 1 else a.reshape(SEQ // dil, dil * a.shape[1])


def _from_view(a, dil):
    return a if dil == 1 else a.reshape(SEQ, a.shape[1] // dil)


def _local_step(x, pos, target, w_in, w_out, w_gate, w_up, w_down, small):
    b_t = small["sgu_b_spatial"].T
    h, u, vs, *qkv = _inproj_fwd(x, pos, small["pre_mix_norm"], w_in)
    sgu = _sgu_fwd(u, vs, small["sgu_ln_gain"], small["sgu_ln_bias"], small["sgu_w_spatial"], b_t)
    views = [tuple(qkv[3 * i:3 * i + 3]) for i in range(len(DILATIONS))]
    o_list, l_list = [], []
    for dil, (qv, kv, vv) in zip(DILATIONS, views):
        o, l = _attn_fwd(qv, kv, vv, dil)
        o_list.append(o)
        l_list.append(l)
    attn, mixed, y, x1, *lses = _mix_out_fwd(o_list, l_list, sgu, x, w_out, small["attn_out_norm"],
                                             small["sgu_out_norm"], small["post_mix_norm"])
    h2, a, dg, dup, df, dx1, loss_cols, d_pre_ffn, d_post_ffn = _ffn_fwd_bwd(
        x1, target, w_gate, w_up, w_down, small["pre_ffn_norm"], small["post_ffn_norm"])

    full_tok = pl.BlockSpec((SEQ, D_MODEL), lambda s: (0, 0), pipeline_mode=pl.Buffered(1))
    ff_tok = pl.BlockSpec((1, SEQ, FF_S), lambda s: (s, 0, 0))
    gw_gate, gw_up = _wgrad_ff([dg, dup], h2, "wgrad_gate_up")
    (gw_down,) = _wgrad_ff([a], df, "wgrad_down")

    dy, dsgu, d_post_mix, d_attn_norm, d_sgu_norm, *dviews = _outproj_bwd(
        dx1, y, attn, sgu, w_out, small["post_mix_norm"], small["attn_out_norm"], small["sgu_out_norm"])
    gw_out = _wgrad(mixed, dy, pl.BlockSpec((SEQ, OUT_S), lambda s: (0, s)), full_tok, (OUT_S, D_MODEL), "wgrad_out")
    du, dvs_sgu, d_w_sp, d_b_sp, d_ln_gain, d_ln_bias = _sgu_bwd(
        u, vs, dsgu, small["sgu_ln_gain"], small["sgu_ln_bias"], small["sgu_w_spatial"], b_t)

    dqs, dks, dvs = [], [], []
    for i, (dil, (qv, kv, vv)) in enumerate(zip(DILATIONS, views)):
        dq, dk, dv = _attn_bwd(qv, kv, vv, dviews[i], dviews[3 + i], lses[i], dil)
        dqs.append(dq)
        dks.append(dk)
        dvs.append(dv)
    dproj, grad_x, d_pre_mix = _inproj_bwd(dqs, dks, dvs, du, dvs_sgu, pos, x, dx1, w_in, small["pre_mix_norm"])
    gw_in = _wgrad(h, dproj, full_tok, pl.BlockSpec((SEQ, IN_S), lambda s: (0, s)), (D_MODEL, IN_S), "wgrad_in")

    small_grads = {
        "pre_mix_norm": d_pre_mix, "sgu_ln_gain": d_ln_gain, "sgu_ln_bias": d_ln_bias, "sgu_w_spatial": d_w_sp,
        "sgu_b_spatial": d_b_sp, "attn_out_norm": d_attn_norm, "sgu_out_norm": d_sgu_norm,
        "post_mix_norm": d_post_mix, "pre_ffn_norm": d_pre_ffn, "post_ffn_norm": d_post_ffn,
    }
    return loss_cols, grad_x, (gw_in, gw_out, gw_gate, gw_up, gw_down), small_grads


def _coords():
    return lax.axis_index("x"), lax.axis_index("y"), lax.axis_index("c")


def _other_chips(x, y):
    return [(1 - x, y), (x, 1 - y), (1 - x, 1 - y)]


def _comm_call(body, name, n_in, out_shape, scratch_shapes):
    return pl.pallas_call(
        body, name=name, in_specs=[ANY] * n_in, out_specs=[ANY] * len(out_shape), out_shape=out_shape,
        scratch_shapes=scratch_shapes,
        compiler_params=pltpu.CompilerParams(has_side_effects=True),
    )


def _gather_weights(shards):
    n = len(shards)
    halves = [s.reshape(2, s.shape[0] // 2, s.shape[1]) for s in shards]

    def body(*refs):
        ins, outs = refs[:n], refs[n:2 * n]
        send_sems, recv_sems = refs[2 * n:]
        x, y, c = _coords()
        s_me = 2 * x + y
        chips = _other_chips(x, y)
        sibling = (x, y, 1 - c)

        def copy(k, w, shard, cc, to):
            src = ins[w].at[cc] if shard is None else outs[w].at[shard, cc]
            dst = outs[w].at[s_me if shard is None else shard, cc]
            return pltpu.make_async_remote_copy(src_ref=src, dst_ref=dst, send_sem=send_sems.at[k],
                                                recv_sem=recv_sems.at[k], device_id=to, device_id_type=MESH)

        first = [copy(j * n + w, w, None, c, (cx, cy, c)) for j, (cx, cy) in enumerate(chips) for w in range(n)]
        for cp in first:
            cp.start()
        passed = []
        for j, (cx, cy) in enumerate(chips):
            for w in range(n):
                copy(j * n + w, w, 2 * cx + cy, c, (x, y, c)).wait_recv()
                fw = copy((3 + j) * n + w, w, 2 * cx + cy, c, sibling)
                fw.start()
                passed.append(fw)
        for j, (cx, cy) in enumerate(chips):
            for w in range(n):
                copy((3 + j) * n + w, w, 2 * cx + cy, 1 - c, (x, y, c)).wait_recv()
        for cp in first + passed:
            cp.wait_send()

    out_shape = [_sds((N_SHARD,) + h.shape, h.dtype) for h in halves]
    scratch = [pltpu.SemaphoreType.DMA((6 * n,)), pltpu.SemaphoreType.DMA((6 * n,))]
    full = _comm_call(body, "comm_gather_weights", n, out_shape, scratch)(*halves)
    s_me = 2 * lax.axis_index("x") + lax.axis_index("y")
    full = [lax.dynamic_update_slice(f, h[None], (s_me, 0, 0, 0)) for f, h in zip(full, halves)]
    return [f.reshape((N_SHARD,) + s.shape) for f, s in zip(full, shards)]


def _rs_to_sibling(gws):
    n = len(gws)

    def body(*refs):
        ins, outs = refs[:n], refs[n:2 * n]
        send_sems, recv_sems = refs[2 * n:]
        x, y, c = _coords()
        copies = []
        for w in range(n):
            hw = gws[w].shape[1] // 2
            copies.append(pltpu.make_async_remote_copy(
                src_ref=ins[w].at[:, pl.ds((1 - c) * hw, hw), :], dst_ref=outs[w], send_sem=send_sems.at[w],
                recv_sem=recv_sems.at[w], device_id=(x, y, 1 - c), device_id_type=MESH))
        for cp in copies:
            cp.start()
        for cp in copies:
            cp.wait()

    out_shape = [_sds((N_SHARD, g.shape[1] // 2, g.shape[2]), g.dtype) for g in gws]
    scratch = [pltpu.SemaphoreType.DMA((n,)), pltpu.SemaphoreType.DMA((n,))]
    return _comm_call(body, "comm_rs_sibling", n, out_shape, scratch)(*gws)


def _rs_chip_sum(gw, recv, core):
    _, rows, cols = gw.shape
    hw = rows // 2

    def body(c_ref, g_ref, r_ref, o_ref):
        o_ref[...] = (g_ref[...] + r_ref[...]).astype(BF16)

    return pl.pallas_call(
        body, name="rs_chip_sum",
        grid_spec=pltpu.PrefetchScalarGridSpec(
            num_scalar_prefetch=1, grid=(N_SHARD,),
            in_specs=[pl.BlockSpec((1, hw, cols), lambda s, c_ref: (s, c_ref[0], 0)),
                      pl.BlockSpec((1, hw, cols), lambda s, c_ref: (s, 0, 0))],
            out_specs=pl.BlockSpec((1, hw, cols), lambda s, c_ref: (s, 0, 0))),
        out_shape=_sds((N_SHARD, hw, cols), BF16),
        compiler_params=_seq_params(),
    )(core, gw, recv)


def _rs_between_chips(pbs):
    n = len(pbs)

    def body(*refs):
        ins, outs = refs[:n], refs[n:2 * n]
        send_sems, recv_sems = refs[2 * n:]
        x, y, c = _coords()
        copies = []
        for j, (cx, cy) in enumerate(_other_chips(x, y)):
            for w in range(n):
                copies.append(pltpu.make_async_remote_copy(
                    src_ref=ins[w].at[2 * cx + cy], dst_ref=outs[w].at[j], send_sem=send_sems.at[j * n + w],
                    recv_sem=recv_sems.at[j * n + w], device_id=(cx, cy, c), device_id_type=MESH))
        for cp in copies:
            cp.start()
        for cp in copies:
            cp.wait()

    out_shape = [_sds((3,) + p.shape[1:], p.dtype) for p in pbs]
    scratch = [pltpu.SemaphoreType.DMA((3 * n,)), pltpu.SemaphoreType.DMA((3 * n,))]
    return _comm_call(body, "comm_rs_chips", n, out_shape, scratch)(*pbs)


def _rs_final_sum(gw, recv_sib, recv_chips, shard_core):
    _, rows, cols = gw.shape
    hw = rows // 2

    def body(sc_ref, g_ref, r_ref, rc_ref, o_ref):
        acc = g_ref[0] + r_ref[0]
        for j in range(3):
            acc = acc + rc_ref[j].astype(F32)
        o_ref[0] = acc

    return pl.pallas_call(
        body, name="rs_final_sum",
        grid_spec=pltpu.PrefetchScalarGridSpec(
            num_scalar_prefetch=1, grid=(1,),
            in_specs=[pl.BlockSpec((1, hw, cols), lambda i, sc: (sc[0], sc[1], 0)),
                      pl.BlockSpec((1, hw, cols), lambda i, sc: (sc[0], 0, 0)),
                      pl.BlockSpec((3, hw, cols), lambda i, sc: (0, 0, 0))],
            out_specs=pl.BlockSpec((1, hw, cols), lambda i, sc: (sc[1], 0, 0))),
        out_shape=_sds((2, hw, cols), F32),
        compiler_params=_seq_params(),
    )(shard_core, gw, recv_sib, recv_chips)


def _rs_join_halves(halves):
    n = len(halves)

    def body(*refs):
        bufs = refs[n:2 * n]
        send_sems, recv_sems = refs[2 * n:]
        x, y, c = _coords()
        remote = [pltpu.make_async_remote_copy(
            src_ref=bufs[w].at[c], dst_ref=bufs[w].at[c], send_sem=send_sems.at[w], recv_sem=recv_sems.at[w],
            device_id=(x, y, 1 - c), device_id_type=MESH) for w in range(n)]
        for cp in remote:
            cp.start()
        for w in range(n):
            remote[w].wait_send()
            pltpu.make_async_remote_copy(
                src_ref=bufs[w].at[c], dst_ref=bufs[w].at[1 - c], send_sem=send_sems.at[w],
                recv_sem=recv_sems.at[w], device_id=(x, y, c), device_id_type=MESH).wait_recv()

    joined = pl.pallas_call(
        body, name="comm_rs_join", in_specs=[ANY] * n, out_specs=[ANY] * n,
        out_shape=[_sds(h.shape, h.dtype) for h in halves], input_output_aliases={w: w for w in range(n)},
        scratch_shapes=[pltpu.SemaphoreType.DMA((n,)), pltpu.SemaphoreType.DMA((n,))],
        compiler_params=pltpu.CompilerParams(has_side_effects=True),
    )(*halves)
    return [j.reshape(2 * h.shape[1], h.shape[2]) for j, h in zip(joined, halves)]


def _allreduce_small(buf):
    rows, cols = buf.shape

    def body(in_ref, out_ref, slots, send_sems, recv_sems):
        x, y, c = _coords()
        me = 4 * x + 2 * y + c
        copies, peers = [], []
        for k in range(1, 8):
            px = 1 - x if (k >> 2) & 1 else x
            py = 1 - y if (k >> 1) & 1 else y
            pc = 1 - c if k & 1 else c
            peers.append(4 * px + 2 * py + pc)
            copies.append(pltpu.make_async_remote_copy(
                src_ref=in_ref, dst_ref=slots.at[me], send_sem=send_sems.at[k - 1], recv_sem=recv_sems.at[k - 1],
                device_id=(px, py, pc), device_id_type=MESH))
        for cp in copies:
            cp.start()
        slots[me] = in_ref[...]
        for k in range(7):
            pltpu.make_async_remote_copy(
                src_ref=in_ref, dst_ref=slots.at[peers[k]], send_sem=send_sems.at[k], recv_sem=recv_sems.at[k],
                device_id=(x, y, c), device_id_type=MESH).wait_recv()
        for cp in copies:
            cp.wait_send()
        acc = slots[0]
        for i in range(1, 8):
            acc = acc + slots[i]
        out_ref[...] = acc

    vmem = pl.BlockSpec(memory_space=pltpu.VMEM)
    return pl.pallas_call(
        body, name="comm_allreduce_small", in_specs=[vmem], out_specs=vmem, out_shape=_sds((rows, cols), F32),
        scratch_shapes=[pltpu.VMEM((8, rows, cols), F32), pltpu.SemaphoreType.DMA((7,)), pltpu.SemaphoreType.DMA((7,))],
        compiler_params=pltpu.CompilerParams(has_side_effects=True, vmem_limit_bytes=VMEM_LIMIT),
    )(buf)


def _adamw(w, g, m, v, block_rows, name, after=()):
    rows, cols = w.shape

    def body(w_ref, g_ref, m_ref, v_ref, *rest):
        d_ref, nm_ref, nv_ref = rest[len(after):]
        g = g_ref[...]
        m = ADAM_B1 * m_ref[...] + (1.0 - ADAM_B1) * g
        v = ADAM_B2 * v_ref[...] + (1.0 - ADAM_B2) * (g * g)
        m_hat = m / (1.0 - ADAM_B1 ** ADAM_STEP)
        v_hat = v / (1.0 - ADAM_B2 ** ADAM_STEP)
        d_ref[...] = -ADAM_LR * (m_hat / (jnp.sqrt(v_hat) + ADAM_EPS) + ADAM_WD * w_ref[...])
        nm_ref[...] = m
        nv_ref[...] = v

    spec = pl.BlockSpec((block_rows, cols), lambda i: (i, 0))
    return pl.pallas_call(
        body, name=name, grid=(rows // block_rows,), in_specs=[spec] * 4 + [ANY] * len(after), out_specs=[spec] * 3,
        out_shape=[_sds((rows, cols), F32)] * 3,
        compiler_params=_seq_params(),
    )(w, g, m, v, *after)


WEIGHTS = ("pre_mix_norm", "w_in", "sgu_ln_gain", "sgu_ln_bias", "sgu_w_spatial", "sgu_b_spatial", "attn_out_norm",
           "sgu_out_norm", "w_out", "post_mix_norm", "pre_ffn_norm", "w_gate", "w_up", "w_down", "post_ffn_norm")
BIG = ("w_in", "w_out", "w_gate", "w_up", "w_down")
BIG_ADAM_ROWS = {"w_in": 256, "w_out": 128, "w_gate": 352, "w_up": 352, "w_down": 352}
SMALL = ("pre_mix_norm", "post_mix_norm", "pre_ffn_norm", "post_ffn_norm", "sgu_ln_gain", "sgu_ln_bias",
         "attn_out_norm", "sgu_out_norm", "sgu_w_spatial", "sgu_b_spatial")


def _pack_small(d):
    flat = [d[n].reshape(-1) for n in SMALL]
    used = sum(f.shape[0] for f in flat)
    flat.append(jnp.zeros((SMALL_ROWS * 1024 - used,), F32))
    return jnp.concatenate(flat).reshape(SMALL_ROWS, 1024)


def _unpack_small(buf, shapes):
    flat = buf.reshape(-1)
    out, off = {}, 0
    for n in SMALL:
        size = int(np.prod(shapes[n]))
        out[n] = flat[off:off + size].reshape(shapes[n])
        off += size
    return out


def _kernel_unoverlapped(x, positions, pre_mix_norm, w_in, sgu_ln_gain, sgu_ln_bias, sgu_w_spatial, sgu_b_spatial, attn_out_norm, sgu_out_norm, w_out, post_mix_norm, pre_ffn_norm, w_gate, w_up, w_down, post_ffn_norm, loss_target, m_pre_mix_norm, m_w_in, m_sgu_ln_gain, m_sgu_ln_bias, m_sgu_w_spatial, m_sgu_b_spatial, m_attn_out_norm, m_sgu_out_norm, m_w_out, m_post_mix_norm, m_pre_ffn_norm, m_w_gate, m_w_up, m_w_down, m_post_ffn_norm, v_pre_mix_norm, v_w_in, v_sgu_ln_gain, v_sgu_ln_bias, v_sgu_w_spatial, v_sgu_b_spatial, v_attn_out_norm, v_sgu_out_norm, v_w_out, v_post_mix_norm, v_pre_ffn_norm, v_w_gate, v_w_up, v_w_down, v_post_ffn_norm):
    a = dict(locals())
    cx, cy, cc = _coords()
    core = jnp.stack([cc]).astype(jnp.int32)
    shard_core = jnp.stack([2 * cx + cy, cc]).astype(jnp.int32)

    full = _gather_weights([a[n][0].astype(BF16) for n in BIG])
    small = {n: (a[n][0] if a[n].ndim > 2 else a[n]) for n in SMALL}
    loss_cols, grad_x, gws, small_grads = _local_step(
        x[0], positions.reshape(SEQ, 1), loss_target[0], *full, small)
    loss = lax.psum(jnp.sum(loss_cols) * np.float32(0.5 / D_MODEL), ("x", "y", "c"))

    recv_sib = _rs_to_sibling(list(gws))
    chip_part = [_rs_chip_sum(g, r, core) for g, r in zip(gws, recv_sib)]
    recv_chips = _rs_between_chips(chip_part)
    halves = [_rs_final_sum(g, r, rc, shard_core) for g, r, rc in zip(gws, recv_sib, recv_chips)]
    big_grads = dict(zip(BIG, _rs_join_halves(halves)))

    shapes = {n: a[n].shape for n in SMALL}
    small_sum = _allreduce_small(_pack_small(small_grads))

    grads, deltas, new_m, new_v = {}, {}, {}, {}
    for n in BIG:
        grads[n] = big_grads[n][None]
        d, nm, nv = _adamw(a[n][0], big_grads[n], a["m_" + n][0], a["v_" + n][0], BIG_ADAM_ROWS[n], "adamw_" + n)
        deltas[n], new_m[n], new_v[n] = d[None], nm[None], nv[None]
    d, nm, nv = _adamw(_pack_small({n: a[n] for n in SMALL}), small_sum, _pack_small({n: a["m_" + n] for n in SMALL}),
                       _pack_small({n: a["v_" + n] for n in SMALL}), SMALL_ROWS, "adamw_small")
    grads.update(_unpack_small(small_sum, shapes))
    deltas.update(_unpack_small(d, shapes))
    new_m.update(_unpack_small(nm, shapes))
    new_v.update(_unpack_small(nv, shapes))
    return (loss, grad_x[None], *[grads[n] for n in WEIGHTS], *[deltas[n] for n in WEIGHTS],
            *[new_m[n] for n in WEIGHTS], *[new_v[n] for n in WEIGHTS])


def _remote(src, dst, send_sem, recv_sem, to):
    return pltpu.make_async_remote_copy(src_ref=src, dst_ref=dst, send_sem=send_sem, recv_sem=recv_sem,
                                        device_id=to, device_id_type=MESH)


def _halves(a):
    *lead, rows, cols = a.shape
    return a.reshape(*lead, 2, rows // 2, cols)


def _gather_ici(shards):
    n = len(shards)

    def desc(ins, outs, ss, rs, j, w, landed):
        x, y, c = _coords()
        cx, cy = _other_chips(x, y)[j]
        shard = 2 * cx + cy if landed else 2 * x + y
        return _remote(ins[w].at[c], outs[w].at[shard, c], ss.at[j * n + w], rs.at[j * n + w], (cx, cy, c))

    def start(ins, outs, ss, rs):
        for j in range(3):
            for w in range(n):
                desc(ins, outs, ss, rs, j, w, False).start()

    def finish(ins, outs, ss, rs):
        for j in range(3):
            for w in range(n):
                desc(ins, outs, ss, rs, j, w, True).wait_recv()
                desc(ins, outs, ss, rs, j, w, False).wait_send()

    return _Comm(shards, [_sds((N_SHARD,) + s.shape, s.dtype) for s in shards], 3 * n, start, finish)


def _gather_pass(fulls):
    n = len(fulls)

    def desc(bufs, ss, rs, j, w, landed):
        x, y, c = _coords()
        cx, cy = _other_chips(x, y)[j]
        shard = 2 * cx + cy
        return _remote(bufs[w].at[shard, c], bufs[w].at[shard, 1 - c if landed else c],
                       ss.at[j * n + w], rs.at[j * n + w], (x, y, 1 - c))

    def start(ins, outs, ss, rs):
        for j in range(3):
            for w in range(n):
                desc(outs, ss, rs, j, w, False).start()

    def finish(ins, outs, ss, rs):
        for j in range(3):
            for w in range(n):
                desc(outs, ss, rs, j, w, True).wait_recv()
                desc(outs, ss, rs, j, w, False).wait_send()

    return _Comm(fulls, [_sds(f.shape, f.dtype) for f in fulls], 3 * n, start, finish, aliased=True)


def _rs_sibling(gws):
    n = len(gws)

    def desc(ins, outs, ss, rs, w):
        x, y, c = _coords()
        return _remote(ins[w].at[:, 1 - c], outs[w], ss.at[w], rs.at[w], (x, y, 1 - c))

    def start(ins, outs, ss, rs):
        for w in range(n):
            desc(ins, outs, ss, rs, w).start()

    def finish(ins, outs, ss, rs):
        for w in range(n):
            desc(ins, outs, ss, rs, w).wait()

    out_shape = [_sds((N_SHARD, g.shape[1] // 2, g.shape[2]), g.dtype) for g in gws]
    return _Comm([_halves(g) for g in gws], out_shape, n, start, finish)


def _rs_chips(pbs):
    n = len(pbs)

    def desc(ins, outs, ss, rs, j, w):
        x, y, c = _coords()
        cx, cy = _other_chips(x, y)[j]
        return _remote(ins[w].at[2 * cx + cy], outs[w].at[j], ss.at[j * n + w], rs.at[j * n + w], (cx, cy, c))

    def start(ins, outs, ss, rs):
        for j in range(3):
            for w in range(n):
                desc(ins, outs, ss, rs, j, w).start()

    def finish(ins, outs, ss, rs):
        for j in range(3):
            for w in range(n):
                desc(ins, outs, ss, rs, j, w).wait()

    return _Comm(pbs, [_sds((3,) + p.shape[1:], p.dtype) for p in pbs], 3 * n, start, finish)


def _rs_join(halves):
    n = len(halves)

    def desc(bufs, ss, rs, w, landed):
        x, y, c = _coords()
        return _remote(bufs[w].at[c], bufs[w].at[1 - c if landed else c], ss.at[w], rs.at[w], (x, y, 1 - c))

    def start(ins, outs, ss, rs):
        for w in range(n):
            desc(outs, ss, rs, w, False).start()

    def finish(ins, outs, ss, rs):
        for w in range(n):
            desc(outs, ss, rs, w, True).wait_recv()
            desc(outs, ss, rs, w, False).wait_send()

    return _Comm(halves, [_sds(h.shape, h.dtype) for h in halves], n, start, finish, aliased=True)


def _small_exchange(buf):
    def desc(ins, outs, ss, rs, k, landed):
        x, y, c = _coords()
        px = 1 - x if (k >> 2) & 1 else x
        py = 1 - y if (k >> 1) & 1 else y
        pc = 1 - c if k & 1 else c
        slot = 4 * px + 2 * py + pc if landed else 4 * x + 2 * y + c
        return _remote(ins[0], outs[0].at[slot], ss.at[k - 1], rs.at[k - 1], (px, py, pc))

    def start(ins, outs, ss, rs):
        for k in range(1, 8):
            desc(ins, outs, ss, rs, k, False).start()

    def finish(ins, outs, ss, rs):
        for k in range(1, 8):
            desc(ins, outs, ss, rs, k, True).wait_recv()
            desc(ins, outs, ss, rs, k, False).wait_send()

    return _Comm([buf], [_sds((8,) + buf.shape, buf.dtype)], 7, start, finish)


HBM = pl.BlockSpec(memory_space=pltpu.HBM)
SEM = pl.BlockSpec(memory_space=pltpu.SEMAPHORE)
DATAFLOW = pltpu.SideEffectType.DATAFLOW_SIDE_EFFECTING


def _split_start(name, comm, lands, after):
    srcs = [pltpu.with_memory_space_constraint(s, pltpu.HBM) for s in comm.args]
    lands = [pltpu.with_memory_space_constraint(b, pltpu.HBM) for b in lands]
    ns, nb = len(srcs), len(lands)

    def body(*refs):
        send_sems, recv_sems = refs[ns + nb + 1], refs[ns + nb + 2]
        comm.start(refs[:ns], refs[ns:ns + nb], send_sems, recv_sems)
        refs[-1][...] = jnp.zeros_like(refs[-1])

    res = pl.pallas_call(
        body, name=name,
        out_shape=(pltpu.SemaphoreType.DMA((comm.n_sems,)), pltpu.SemaphoreType.DMA((comm.n_sems,)),
                   *[pltpu.HBM(b.shape, b.dtype) for b in srcs + lands], _sds((8, 128), F32)),
        in_specs=[HBM] * (ns + nb) + [ANY],
        out_specs=(SEM, SEM, *[HBM] * (ns + nb), pl.BlockSpec(memory_space=pltpu.VMEM)),
        input_output_aliases={i: 2 + i for i in range(ns + nb)},
        compiler_params=pltpu.CompilerParams(has_side_effects=DATAFLOW),
    )(*srcs, *lands, after)
    return res[0], res[1], list(res[2:2 + ns]), list(res[2 + ns:2 + ns + nb]), res[-1]


def _split_starts(name, comms, after):
    srcs = [[pltpu.with_memory_space_constraint(s, pltpu.HBM) for s in c.args] for c in comms]
    lands = [[pltpu.with_memory_space_constraint(lax.empty(o.shape, o.dtype), pltpu.HBM) for o in c.out_shape]
             for c in comms]
    bufs = [b for k in range(len(comms)) for b in srcs[k] + lands[k]]
    nb, nc = len(bufs), len(comms)

    def body(*refs):
        sems = refs[nb + 1:nb + 1 + 2 * nc]
        off = 0
        for k, c in enumerate(comms):
            ns, nl = len(srcs[k]), len(lands[k])
            c.start(refs[off:off + ns], refs[off + ns:off + ns + nl], sems[2 * k], sems[2 * k + 1])
            off += ns + nl
        refs[-1][...] = jnp.zeros_like(refs[-1])

    res = pl.pallas_call(
        body, name=name,
        out_shape=(*[pltpu.SemaphoreType.DMA((c.n_sems,)) for c in comms for _ in range(2)],
                   *[pltpu.HBM(b.shape, b.dtype) for b in bufs], _sds((8, 128), F32)),
        in_specs=[HBM] * nb + [ANY],
        out_specs=(*[SEM] * (2 * nc), *[HBM] * nb, pl.BlockSpec(memory_space=pltpu.VMEM)),
        input_output_aliases={i: 2 * nc + i for i in range(nb)},
        compiler_params=pltpu.CompilerParams(has_side_effects=DATAFLOW),
    )(*bufs, after)
    states, off = [], 2 * nc
    for k in range(nc):
        ns, nl = len(srcs[k]), len(lands[k])
        states.append((res[2 * k], res[2 * k + 1], list(res[off:off + ns]), list(res[off + ns:off + ns + nl])))
        off += ns + nl
    return states, res[-1]


def _split_wait(name, comm, send_sems, recv_sems, srcs, lands, after):
    ns, nb = len(srcs), len(lands)
    after = list(after) if isinstance(after, (list, tuple)) else [after]

    def body(*refs):
        comm.finish(refs[:ns], refs[ns:ns + nb], refs[ns + nb], refs[ns + nb + 1])

    res = pl.pallas_call(
        body, name=name,
        out_shape=tuple(pltpu.HBM(b.shape, b.dtype) for b in srcs + lands),
        in_specs=[HBM] * (ns + nb) + [SEM, SEM] + [ANY] * len(after), out_specs=tuple([HBM] * (ns + nb)),
        input_output_aliases={i: i for i in range(ns + nb)},
        compiler_params=pltpu.CompilerParams(has_side_effects=DATAFLOW),
    )(*srcs, *lands, send_sems, recv_sems, *after)
    return list(res[ns:])


LOSS_ROW = "loss_cols"
SMALL_EARLY = ("post_mix_norm", "pre_ffn_norm", "post_ffn_norm", "sgu_ln_gain", "sgu_ln_bias", "attn_out_norm",
               "sgu_out_norm", "sgu_w_spatial", "sgu_b_spatial", LOSS_ROW)
SMALL_LATE = ("pre_mix_norm",)


def _pack(d, names, rows):
    flat = [d[n].reshape(-1) for n in names]
    used = sum(f.shape[0] for f in flat)
    flat.append(jnp.zeros((rows * 1024 - used,), F32))
    return jnp.concatenate(flat).reshape(rows, 1024)


def _unpack(buf, names, shapes):
    flat = buf.reshape(-1)
    out, off = {}, 0
    for n in names:
        size = int(np.prod(shapes[n]))
        out[n] = flat[off:off + size].reshape(shapes[n])
        off += size
    return out


def _merge(comms):
    def run(phase):
        def go(ins, outs, ss, rs):
            ii = oi = si = 0
            for c in comms:
                getattr(c, phase)(ins[ii:ii + len(c.args)], outs[oi:oi + len(c.out_shape)],
                                  ss.at[pl.ds(si, c.n_sems)], rs.at[pl.ds(si, c.n_sems)])
                ii += len(c.args)
                oi += len(c.out_shape)
                si += c.n_sems
        return go

    return _Comm([a for c in comms for a in c.args], [o for c in comms for o in c.out_shape],
                 sum(c.n_sems for c in comms), run("start"), run("finish"))


def _chip_sums(gws, recvs, shard_core):
    n = len(gws)
    _, rows, cols = gws[0].shape
    hw = rows // 2

    def body(sc_ref, *refs):
        for k in range(n):
            refs[2 * n + k][...] = (refs[k][...] + refs[n + k][...]).astype(BF16)

    def other(s, sc):
        return jnp.where(s >= sc[0], s + 1, s)

    mine = pl.BlockSpec((1, hw, cols), lambda s, sc: (other(s, sc), sc[1], 0))
    plain = pl.BlockSpec((1, hw, cols), lambda s, sc: (other(s, sc), 0, 0))
    return pl.pallas_call(
        body, name="rs_chip_sums",
        grid_spec=pltpu.PrefetchScalarGridSpec(num_scalar_prefetch=1, grid=(N_SHARD - 1,),
                                               in_specs=[mine] * n + [plain] * n, out_specs=[plain] * n),
        out_shape=[_sds((N_SHARD, hw, cols), BF16)] * n,
        compiler_params=_seq_params(),
    )(shard_core, *gws, *recvs)


def _final_sums(gws, recv_sibs, recv_chips, shard_core):
    n = len(gws)
    _, rows, cols = gws[0].shape
    hw = rows // 2

    def body(sc_ref, *refs):
        for k in range(n):
            acc = refs[k][0] + refs[n + k][0]
            for j in range(3):
                acc = acc + refs[2 * n + k][j].astype(F32)
            refs[3 * n + k][0] = acc

    return pl.pallas_call(
        body, name="rs_final_sums",
        grid_spec=pltpu.PrefetchScalarGridSpec(
            num_scalar_prefetch=1, grid=(1,),
            in_specs=[pl.BlockSpec((1, hw, cols), lambda i, sc: (sc[0], sc[1], 0))] * n
            + [pl.BlockSpec((1, hw, cols), lambda i, sc: (sc[0], 0, 0))] * n
            + [pl.BlockSpec((3, hw, cols), lambda i, sc: (0, 0, 0))] * n,
            out_specs=[pl.BlockSpec((1, hw, cols), lambda i, sc: (sc[1], 0, 0))] * n),
        out_shape=[_sds((2, hw, cols), F32)] * n,
        compiler_params=_seq_params(),
    )(shard_core, *gws, *recv_sibs, *recv_chips)


def _adamw_multi(ws, gs, ms, vs, block_rows, name, after=()):
    n = len(ws)
    rows, cols = ws[0].shape

    def body(*refs):
        outs = refs[4 * n + len(after):]
        for k in range(n):
            g = refs[n + k][...]
            d, m, v = _adam_math(refs[k][...], g, refs[2 * n + k][...], refs[3 * n + k][...])
            outs[4 * k][...], outs[4 * k + 1][...], outs[4 * k + 2][...], outs[4 * k + 3][...] = g, d, m, v

    spec = pl.BlockSpec((block_rows, cols), lambda i: (i, 0))
    res = pl.pallas_call(
        body, name=name, grid=(rows // block_rows,), in_specs=[spec] * (4 * n) + [ANY] * len(after),
        out_specs=[spec] * (4 * n), out_shape=[_sds((rows, cols), F32)] * (4 * n),
        compiler_params=_seq_params(),
    )(*ws, *gs, *ms, *vs, *after)
    return [tuple(res[4 * k:4 * k + 4]) for k in range(n)]


def _wgrad_ff(a_list, b, name, comms=()):
    n = len(a_list)
    cols = 256

    def body(*refs):
        bv = refs[n][...]
        for k in range(n):
            refs[n + 1 + k][...] = _dot_tn(refs[k][...], bv)

    res = _pcall(
        body, name=name, grid=(FF // cols,),
        in_specs=[pl.BlockSpec((SEQ, cols), lambda j: (0, j))] * n
        + [pl.BlockSpec((SEQ, D_MODEL), lambda j: (0, 0), pipeline_mode=pl.Buffered(1))],
        out_specs=[pl.BlockSpec((cols, D_MODEL), lambda j: (j, 0))] * n,
        out_shape=[_sds((FF, D_MODEL), F32)] * n, args=(*a_list, b), comms=comms)
    mine, theirs = res if comms else (res, None)
    mine = [m.reshape(N_SHARD, FF_S, D_MODEL) for m in mine]
    return (mine, theirs) if comms else mine


def _wgrad_pair(a1, a2, b, a_spec, b_spec, out_block, name, comms=()):
    def body(a1_ref, a2_ref, b_ref, o1_ref, o2_ref):
        bv = b_ref[...]
        o1_ref[0] = _dot_tn(a1_ref[0], bv)
        o2_ref[0] = _dot_tn(a2_ref[0], bv)

    out_spec = pl.BlockSpec((1,) + out_block, lambda s: (s, 0, 0))
    return _pcall(
        body, name=name, grid=(N_SHARD,), in_specs=[a_spec, a_spec, b_spec], out_specs=[out_spec, out_spec],
        out_shape=[_sds((N_SHARD,) + out_block, F32)] * 2, args=(a1, a2, b), comms=comms)


def _comm_only(name, comms):
    return _pcall(lambda: None, name=name, grid=(1,), in_specs=[], out_specs=[], out_shape=[], args=(),
                  comms=comms)[1]


def _adam_math(w, g, m, v):
    m = ADAM_B1 * m + (1.0 - ADAM_B1) * g
    v = ADAM_B2 * v + (1.0 - ADAM_B2) * (g * g)
    m_hat = m / (1.0 - ADAM_B1 ** ADAM_STEP)
    v_hat = v / (1.0 - ADAM_B2 ** ADAM_STEP)
    return -ADAM_LR * (m_hat / (jnp.sqrt(v_hat) + ADAM_EPS) + ADAM_WD * w), m, v


def _adamw_small(own, slots, w, m, v, me):
    rows, cols = own.shape

    def body(me_ref, own_ref, slots_ref, w_ref, m_ref, v_ref, g_ref, d_ref, nm_ref, nv_ref):
        own_v = own_ref[...]
        g = jnp.where(me_ref[0] == 0, own_v, slots_ref[0])
        for i in range(1, 8):
            g = g + jnp.where(me_ref[0] == i, own_v, slots_ref[i])
        g_ref[...] = g
        d_ref[...], nm_ref[...], nv_ref[...] = _adam_math(w_ref[...], g, m_ref[...], v_ref[...])

    flat = pl.BlockSpec((rows, cols), lambda i, me_ref: (0, 0))
    return pl.pallas_call(
        body, name="adamw_small",
        grid_spec=pltpu.PrefetchScalarGridSpec(
            num_scalar_prefetch=1, grid=(1,),
            in_specs=[flat, pl.BlockSpec((8, rows, cols), lambda i, me_ref: (0, 0, 0)), flat, flat, flat],
            out_specs=[flat] * 4),
        out_shape=[_sds((rows, cols), F32)] * 4,
        compiler_params=_seq_params(),
    )(me, own, slots, w, m, v)


def kernel(x, positions, pre_mix_norm, w_in, sgu_ln_gain, sgu_ln_bias, sgu_w_spatial, sgu_b_spatial, attn_out_norm, sgu_out_norm, w_out, post_mix_norm, pre_ffn_norm, w_gate, w_up, w_down, post_ffn_norm, loss_target, m_pre_mix_norm, m_w_in, m_sgu_ln_gain, m_sgu_ln_bias, m_sgu_w_spatial, m_sgu_b_spatial, m_attn_out_norm, m_sgu_out_norm, m_w_out, m_post_mix_norm, m_pre_ffn_norm, m_w_gate, m_w_up, m_w_down, m_post_ffn_norm, v_pre_mix_norm, v_w_in, v_sgu_ln_gain, v_sgu_ln_bias, v_sgu_w_spatial, v_sgu_b_spatial, v_attn_out_norm, v_sgu_out_norm, v_w_out, v_post_mix_norm, v_pre_ffn_norm, v_w_gate, v_w_up, v_w_down, v_post_ffn_norm):
    a = dict(locals())
    cx, cy, cc = _coords()
    s_me = 2 * cx + cy
    core = jnp.stack([cc]).astype(jnp.int32)
    shard_core = jnp.stack([s_me, cc]).astype(jnp.int32)
    me = jnp.stack([4 * cx + 2 * cy + cc]).astype(jnp.int32)
    small = {n: (a[n][0] if a[n].ndim > 2 else a[n]) for n in SMALL}
    b_t = small["sgu_b_spatial"].T
    xs, pos, target = x[0], positions.reshape(SEQ, 1), loss_target[0]
    flipped = ("w_gate", "w_up")

    def big(name, n):
        return jnp.swapaxes(a[name], 1, 2)[0] if n in flipped else a[name][0]

    own = {"w_in": _halves(big("w_in", "w_in").astype(BF16))}

    def with_own(full, n):
        full = lax.dynamic_update_slice(full, own[n][None], (s_me, 0, 0, 0))
        return full.reshape((N_SHARD,) + big(n, n).shape)

    ffn = ("w_gate", "w_up", "w_down")
    g_in = _gather_ici([own["w_in"]])
    (s_in,), token = _split_starts("gather_in_start", [g_in], small["pre_mix_norm"])
    for n in ("w_out",) + ffn:
        own[n] = _halves((big(n, n) + token[0:1, 0:1]).astype(BF16))
    g_out, g_ffn = _gather_ici([own["w_out"]]), _gather_ici([own[n] for n in ffn])
    (s_out, s_ffn), token = _split_starts("gather_rest_start", [g_out, g_ffn], token)
    in_lands = _split_wait("gather_in_wait", g_in, *s_in, token)
    ((in_lands,),) = _comm_only("comm_pass_in", [_gather_pass(in_lands)])
    w_in_f = with_own(in_lands, "w_in")
    h, u, vs, *qkv = _inproj_fwd(xs, pos, small["pre_mix_norm"], w_in_f)
    views = [tuple(qkv[3 * i:3 * i + 3]) for i in range(len(DILATIONS))]
    o_list, l_list = [], []
    for dil, (qv, kv, vv) in zip(DILATIONS, views):
        o, l = _attn_fwd(qv, kv, vv, dil)
        o_list.append(o)
        l_list.append(l)
    out_lands = _split_wait("gather_out_wait", g_out, *s_out, l_list[-1])
    sgu, ((out_lands,),) = _sgu_fwd(u, vs, small["sgu_ln_gain"], small["sgu_ln_bias"], small["sgu_w_spatial"], b_t,
                                    comms=[_gather_pass(out_lands)])
    w_out_f = with_own(out_lands, "w_out")
    ffn_lands = _split_wait("gather_ffn_wait", g_ffn, *s_ffn, sgu)
    (attn, mixed, y, x1, *lses), (ffn_lands,) = _mix_out_fwd(
        o_list, l_list, sgu, xs, w_out_f, small["attn_out_norm"], small["sgu_out_norm"], small["post_mix_norm"],
        comms=[_gather_pass(ffn_lands)])
    w_gate_f, w_up_f, w_down_f = (with_own(f, n).reshape(FF, D_MODEL) for f, n in zip(ffn_lands, ffn))
    h2, act, dg, dup, df, dx1, loss_cols, d_pre_ffn, d_post_ffn = _ffn_fwd_bwd(
        x1, target, w_gate_f, w_up_f, w_down_f, small["pre_ffn_norm"], small["post_ffn_norm"])

    full_tok = pl.BlockSpec((SEQ, D_MODEL), lambda s: (0, 0), pipeline_mode=pl.Buffered(1))
    ff_tok = pl.BlockSpec((1, SEQ, FF_S), lambda s: (s, 0, 0))
    gw = {}
    gw["w_gate"], gw["w_up"] = _wgrad_ff([dg, dup], h2, "wgrad_gate_up")
    (gw["w_down"],), ((sib_gate,),) = _wgrad_ff([act], df, "wgrad_down", comms=[_rs_sibling([gw["w_gate"]])])
    (dy, dsgu, d_post_mix, d_attn_norm, d_sgu_norm, *dviews), ((sib_up, sib_down),) = _outproj_bwd(
        dx1, y, attn, sgu, w_out_f, small["post_mix_norm"], small["attn_out_norm"],
        small["sgu_out_norm"], comms=[_rs_sibling([gw["w_up"], gw["w_down"]])])
    sib = {"w_gate": sib_gate, "w_up": sib_up, "w_down": sib_down}
    part = dict(zip(ffn, _chip_sums([gw[n] for n in ffn], [sib[n] for n in ffn], shard_core)))
    gw["w_out"] = _wgrad(mixed, dy, pl.BlockSpec((SEQ, OUT_S), lambda s: (0, s)), full_tok, (OUT_S, D_MODEL),
                         "wgrad_out")
    x_ffn = _rs_chips([part[n] for n in ffn])
    (s_ffn,), token = _split_starts("rs_ffn_start", [x_ffn], small["pre_mix_norm"])
    (du, dvs_sgu, d_w_sp, d_b_sp, d_ln_gain, d_ln_bias), ((sib["w_out"],),) = _sgu_bwd(
        u, vs, dsgu, small["sgu_ln_gain"], small["sgu_ln_bias"], small["sgu_w_spatial"], b_t,
        comms=[_rs_sibling([gw["w_out"]])], after=[token])
    (part["w_out"],) = _chip_sums([gw["w_out"]], [sib["w_out"]], shard_core)
    packed_early = _pack({
        "sgu_ln_gain": d_ln_gain, "sgu_ln_bias": d_ln_bias, "sgu_w_spatial": d_w_sp, "sgu_b_spatial": d_b_sp,
        "attn_out_norm": d_attn_norm, "sgu_out_norm": d_sgu_norm, "post_mix_norm": d_post_mix,
        "pre_ffn_norm": d_pre_ffn, "post_ffn_norm": d_post_ffn, LOSS_ROW: loss_cols}, SMALL_EARLY, SMALL_ROWS)
    x_out, x_small = _rs_chips([part["w_out"]]), _small_exchange(packed_early)
    (s_out, s_small), token = _split_starts("rs_out_small_start", [x_out, x_small], token)

    dqs, dks, dvs = [], [], []
    for i, (dil, (qv, kv, vv)) in enumerate(zip(DILATIONS, views)):
        dq, dk, dv = _attn_bwd(qv, kv, vv, dviews[i], dviews[3 + i], lses[i], dil, after=[token])
        dqs.append(dq)
        dks.append(dk)
        dvs.append(dv)
    half, far, joined = {}, {}, {}
    far.update(zip(ffn, _split_wait("rs_ffn_wait", x_ffn, *s_ffn, dvs[-1])))
    half.update(zip(ffn, _final_sums([gw[n] for n in ffn], [sib[n] for n in ffn], [far[n] for n in ffn],
                                     shard_core)))
    dproj, grad_x, d_pre_mix = _inproj_bwd(dqs, dks, dvs, du, dvs_sgu, pos, xs, dx1, w_in_f, small["pre_mix_norm"])
    (far["w_out"],) = _split_wait("rs_out_wait", x_out, *s_out, grad_x)
    (slots_early,) = _split_wait("small_early_wait", x_small, *s_small, far["w_out"])
    half["w_out"] = _rs_final_sum(gw["w_out"], sib["w_out"], far["w_out"], shard_core)
    packed_late = _pack({"pre_mix_norm": d_pre_mix}, SMALL_LATE, 8)
    names = ffn + ("w_out",)
    gw["w_in"], (got, (slots_late,)) = _wgrad(
        h, dproj, full_tok, pl.BlockSpec((SEQ, IN_S), lambda s: (0, s)), (D_MODEL, IN_S), "wgrad_in",
        comms=[_rs_join([half[n] for n in names]), _small_exchange(packed_late)])
    joined.update(zip(names, got))
    ((sib["w_in"],),) = _comm_only("comm_rs_sibling_in", [_rs_sibling([gw["w_in"]])])
    (part["w_in"],) = _chip_sums([gw["w_in"]], [sib["w_in"]], shard_core)
    x_in = _rs_chips([part["w_in"]])
    (s_in,), token = _split_starts("rs_in_start", [x_in], small["pre_mix_norm"])

    grads, deltas, new_m, new_v = {}, {}, {}, {}

    def record(n, outs):
        grads[n], deltas[n], new_m[n], new_v[n] = (
            jnp.swapaxes(o[None], 1, 2) if n in flipped else o[None] for o in outs)

    def update(names, block_rows, name, after):
        for n, outs in zip(names, _adamw_multi(
                [big(n, n) for n in names], [joined[n].reshape(big(n, n).shape) for n in names],
                [big("m_" + n, n) for n in names], [big("v_" + n, n) for n in names], block_rows, name, after)):
            record(n, outs)

    update(ffn, FF_S // 4, "adamw_ffn", [token])
    update(("w_out",), BIG_ADAM_ROWS["w_out"], "adamw_w_out", [token])
    (far["w_in"],) = _split_wait("rs_in_wait", x_in, *s_in, [new_v[n] for n in ("w_down", "w_out")])
    half["w_in"] = _rs_final_sum(gw["w_in"], sib["w_in"], far["w_in"], shard_core)
    ((joined["w_in"],),) = _comm_only("comm_rs_join_in", [_rs_join([half["w_in"]])])
    update(("w_in",), BIG_ADAM_ROWS["w_in"], "adamw_w_in", [])
    a[LOSS_ROW] = a["m_" + LOSS_ROW] = a["v_" + LOSS_ROW] = jnp.zeros((1, D_MODEL), F32)
    for names, rows, packed, slots in ((SMALL_EARLY, SMALL_ROWS, packed_early, slots_early),
                                       (SMALL_LATE, 8, packed_late, slots_late)):
        outs = _adamw_small(packed, slots, _pack(a, names, rows), _pack({n: a["m_" + n] for n in names}, names, rows),
                            _pack({n: a["v_" + n] for n in names}, names, rows), me)
        for dst, buf in zip((grads, deltas, new_m, new_v), outs):
            dst.update(_unpack(buf, names, {n: a[n].shape for n in names}))
    loss = jnp.sum(grads[LOSS_ROW]) * np.float32(0.5 / D_MODEL)
    return (loss, grad_x[None], *[grads[n] for n in WEIGHTS], *[deltas[n] for n in WEIGHTS],
            *[new_m[n] for n in WEIGHTS], *[new_v[n] for n in WEIGHTS])
```

```python
import numpy as np
import jax
import jax.numpy as jnp
from jax import lax
from jax.experimental import pallas as pl
from jax.experimental.pallas import tpu as pltpu

F32 = jnp.float32
BF16 = jnp.bfloat16

SEQ = 2048
D_MODEL = 1024
HEAD_DIM = 64
ATTN_W = 512
SGU_W = 512
SGU_GROUPS = 8
CHUNK = 128
DILATIONS = (1, 4, 16)
N_SHARD = 4
IN_S = 640
OUT_S = 256
FF_S = 704
PROJ_W = N_SHARD * IN_S
FF = N_SHARD * FF_S
FF_CHUNKS = ((0, 1024), (1024, 2048), (2048, FF))
RMS_EPS = 1e-6
LN_EPS = 1e-5
ROPE_THETA = 500000.0
ATTN_SCALE = 1.0 / np.sqrt(HEAD_DIM)
NEG = -1e30
TM = 512
TM_FFN = 256
VMEM_LIMIT = 56 * 1024 * 1024
SMALL_ROWS = 136

ADAM_LR = 0.001
ADAM_B1 = 0.9
ADAM_B2 = 0.999
ADAM_EPS = 1e-08
ADAM_WD = 0.01
ADAM_STEP = 10

MESH = pl.DeviceIdType.MESH
ANY = pl.BlockSpec(memory_space=pl.ANY)


def _dot(a, b):
    return jnp.dot(a, b, preferred_element_type=F32)


def _dot_nt(a, b):
    return lax.dot_general(a, b, (((1,), (1,)), ((), ())), preferred_element_type=F32)


def _dot_tn(a, b):
    return lax.dot_general(a, b, (((0,), (0,)), ((), ())), preferred_element_type=F32)


def _dot_exact(a, b):
    return jnp.dot(a, b, preferred_element_type=F32, precision=lax.Precision.HIGHEST)


def _dot_select(a, sel):
    hi = a.astype(BF16)
    lo = (a - hi.astype(F32)).astype(BF16)
    sel = sel.astype(BF16)
    return _dot(hi, sel) + _dot(lo, sel)


def _rms_stats(x):
    r = lax.rsqrt(jnp.mean(x * x, axis=-1, keepdims=True) + RMS_EPS)
    return x * r, r


def _rms_bwd(xh, r, gain, dy):
    dxh = dy * gain
    dx = r * (dxh - xh * jnp.mean(dxh * xh, axis=-1, keepdims=True))
    return dx, jnp.sum(dy * xh, axis=0, keepdims=True)


_ERF_ALPHA = (-2.72614225801306e-10, 2.77068142495902e-08, -2.10102402082508e-06, -5.69250639462346e-05,
              -7.34990630326855e-04, -2.95459980854025e-03, -1.60960333262415e-02)
_ERF_BETA = (-1.45660718464996e-05, -2.13374055278905e-04, -1.68282697438203e-03, -7.37332916720468e-03,
             -1.42647390514189e-02)


def _erf(x):
    x = jnp.clip(x, -4.0, 4.0)
    x2 = x * x
    p = jnp.full_like(x, _ERF_ALPHA[0])
    for a in _ERF_ALPHA[1:]:
        p = p * x2 + a
    q = jnp.full_like(x, _ERF_BETA[0])
    for b in _ERF_BETA[1:]:
        q = q * x2 + b
    return x * p / q


def _normal_cdf(x):
    return 0.5 * (1.0 + _erf(x * np.float32(1.0 / np.sqrt(2.0))))


def _gelu_grad(x, cdf):
    pdf = jnp.exp(-0.5 * x * x) * np.float32(1.0 / np.sqrt(2.0 * np.pi))
    return cdf + x * pdf


def _sigmoid(x):
    return 1.0 / (1.0 + jnp.exp(-x))


_INV_FREQ = tuple(float(np.float32(ROPE_THETA ** (-2.0 * j / 16.0))) for j in range(8))


def _rot_tables(pos):
    lane = lax.broadcasted_iota(jnp.int32, (1, 128), 1)
    d = lane & 63
    j = d & 7
    inv = jnp.zeros((1, 128), F32)
    for jj in range(8):
        inv = jnp.where(j == jj, _INV_FREQ[jj], inv)
    ang = pos.astype(F32) * inv
    c = jnp.cos(ang)
    s = jnp.sin(ang)
    cos_t = jnp.where(d < 16, c, 1.0)
    sin_a = jnp.where(d < 8, -s, 0.0)
    sin_b = jnp.where((d >= 8) & (d < 16), s, 0.0)
    return tuple(jnp.tile(t, (1, 4)) for t in (cos_t, sin_a, sin_b))


def _rope(x, tabs):
    cos_t, sin_a, sin_b = tabs
    return x * cos_t + pltpu.roll(x, 504, 1) * sin_a + pltpu.roll(x, 8, 1) * sin_b


def _rope_bwd(dy, tabs):
    cos_t, sin_a, sin_b = tabs
    return dy * cos_t + pltpu.roll(dy * sin_a, 8, 1) + pltpu.roll(dy * sin_b, 504, 1)


def _left_half():
    return lax.broadcasted_iota(jnp.int32, (CHUNK, CHUNK), 1) < HEAD_DIM


def _group_ones():
    lane = lax.broadcasted_iota(jnp.int32, (SGU_GROUPS, SGU_W), 1)
    row = lax.broadcasted_iota(jnp.int32, (SGU_GROUPS, SGU_W), 0)
    return ((lane >> 6) == row).astype(F32)


def _masked_spatial(w_ref):
    row = lax.broadcasted_iota(jnp.int32, (CHUNK, CHUNK), 0)
    col = lax.broadcasted_iota(jnp.int32, (CHUNK, CHUNK), 1)
    return [jnp.where(col <= row, w_ref[g], 0.0).astype(BF16) for g in range(SGU_GROUPS)]


def _sgu_core(u, vs, lg, lb, wm, bias_full):
    tm = u.shape[0]
    cdf_u, cdf_vs = _normal_cdf(u), _normal_cdf(vs)
    gu = u * cdf_u
    gv = vs * cdf_vs
    mu = jnp.mean(gv, axis=-1, keepdims=True)
    xc = gv - mu
    rstd = lax.rsqrt(jnp.mean(xc * xc, axis=-1, keepdims=True) + LN_EPS)
    xh = xc * rstd
    vnb = (xh * lg + lb).astype(BF16)
    left = _left_half()
    rows = []
    for c in range(tm // CHUNK):
        pieces = []
        for p in range(4):
            vp = vnb[c * CHUNK:(c + 1) * CHUNK, p * 128:(p + 1) * 128]
            pieces.append(jnp.where(left, _dot(wm[2 * p], vp), _dot(wm[2 * p + 1], vp)))
        rows.append(jnp.concatenate(pieces, axis=1) + bias_full)
    mixed = jnp.concatenate(rows, axis=0)
    return gu, xh, rstd, vnb, mixed, cdf_u, cdf_vs


def _resident(shape):
    n = len(shape)
    return pl.BlockSpec(shape, lambda *_: (0,) * n, pipeline_mode=pl.Buffered(1))


def _rows(ncol, tm=TM):
    return pl.BlockSpec((tm, ncol), lambda i: (i, 0))


def _rows3(nlead, ncol, tm=TM):
    return pl.BlockSpec((nlead, tm, ncol), lambda i: (0, i, 0))


def _acc(ncol, nrow=1):
    return pl.BlockSpec((nrow, ncol), lambda i: (0, 0))


HEAD_W = 128


def _view_rows(dil, width=ATTN_W, tm=TM):
    return pl.BlockSpec((tm // dil, dil * width), lambda i: (i, 0))


def _view_shape(dil, dtype, width=ATTN_W):
    return _sds((SEQ // dil, dil * width), dtype)


def _slab_scratch():
    return pltpu.VMEM((4, TM, 128), F32)


def _store_view(val, out_ref, slabs, dil):
    width = val.shape[1]
    for j in range(width // 128):
        slabs[j] = val[:, j * 128:(j + 1) * 128]
    for r in range(dil):
        for j in range(width // 128):
            c0 = r * width + j * 128
            out_ref[:, c0:c0 + 128] = slabs.at[j][pl.ds(r, TM // dil, stride=dil), :].astype(out_ref.dtype)


def _load_view(in_ref, slabs, dil, width=ATTN_W):
    for r in range(dil):
        for j in range(width // 128):
            c0 = r * width + j * 128
            slabs.at[j][pl.ds(r, TM // dil, stride=dil), :] = in_ref[:, c0:c0 + 128].astype(F32)
    return jnp.concatenate([slabs[j] for j in range(width // 128)], axis=1)


def _head_spread():
    m = lax.broadcasted_iota(jnp.int32, (HEAD_W, ATTN_W), 0)
    lane = lax.broadcasted_iota(jnp.int32, (HEAD_W, ATTN_W), 1)
    return (m == 16 * (lane >> 6)).astype(F32)


def _head_sum():
    lane = lax.broadcasted_iota(jnp.int32, (ATTN_W, HEAD_W), 0)
    m = lax.broadcasted_iota(jnp.int32, (ATTN_W, HEAD_W), 1)
    return ((lane >> 6) == (m >> 4)).astype(F32)


def _seq_params():
    return pltpu.CompilerParams(dimension_semantics=("arbitrary",), vmem_limit_bytes=VMEM_LIMIT)


def _sds(shape, dtype):
    return jax.ShapeDtypeStruct(shape, dtype)


class _Comm:
    def __init__(self, args, out_shape, n_sems, start, finish, aliased=False):
        self.args, self.out_shape, self.n_sems = list(args), list(out_shape), n_sems
        self.start, self.finish, self.aliased = start, finish, aliased


def _pcall(body, *, name, grid, in_specs, out_specs, out_shape, args, scratch_shapes=(), comms=(), after=()):
    single = not isinstance(out_shape, (list, tuple))
    out_specs = [out_specs] if single else list(out_specs)
    out_shape = [out_shape] if single else list(out_shape)
    n_in, n_out, n_scr = len(in_specs), len(out_shape), len(scratch_shapes)
    c_args = [a for c in comms for a in c.args]
    c_outs = [o for c in comms for o in c.out_shape]
    aliases, ai, ao = {}, n_in, n_out
    for c in comms:
        if c.aliased:
            aliases.update({ai + k: ao + k for k in range(len(c.args))})
        ai += len(c.args)
        ao += len(c.out_shape)
    sems = [pltpu.SemaphoreType.DMA((c.n_sems,)) for c in comms for _ in range(2)]
    steps = grid[0]

    def wrapped(*refs):
        o0 = n_in + len(c_args) + len(after)
        s0 = o0 + n_out + len(c_outs)
        m_in, m_out, m_sem = refs[n_in:n_in + len(c_args)], refs[o0 + n_out:s0], refs[s0 + n_scr:]

        def each(phase):
            ii = oi = 0
            for k, c in enumerate(comms):
                getattr(c, phase)(m_in[ii:ii + len(c.args)], m_out[oi:oi + len(c.out_shape)],
                                  m_sem[2 * k], m_sem[2 * k + 1])
                ii += len(c.args)
                oi += len(c.out_shape)

        if comms:
            @pl.when(pl.program_id(0) == 0)
            def _():
                each("start")

        body(*refs[:n_in], *refs[o0:o0 + n_out], *refs[s0:s0 + n_scr])

        if comms:
            @pl.when(pl.program_id(0) == steps - 1)
            def _():
                each("finish")

    res = pl.pallas_call(
        wrapped, name=name, grid=grid,
        in_specs=list(in_specs) + [ANY] * (len(c_args) + len(after)), out_specs=out_specs + [ANY] * len(c_outs),
        out_shape=out_shape + c_outs, scratch_shapes=list(scratch_shapes) + sems,
        input_output_aliases=aliases, compiler_params=_seq_params(),
    )(*args, *c_args, *after)
    mine = res[0] if single else list(res[:n_out])
    if not comms:
        return mine
    theirs, oi = [], n_out
    for c in comms:
        theirs.append(list(res[oi:oi + len(c.out_shape)]))
        oi += len(c.out_shape)
    return mine, theirs


def _inproj_fwd(x, pos, g_pre, w_in, comms=()):
    def body(x_ref, pos_ref, g_ref, w_ref, h_ref, u_ref, vs_ref, *rest):
        qkv_refs, slabs = rest[:9], rest[9:]
        xh, _ = _rms_stats(x_ref[...])
        h = (xh * g_ref[...]).astype(BF16)
        h_ref[...] = h
        proj = jnp.concatenate([_dot(h, w_ref[s]) for s in range(N_SHARD)], axis=1)
        tabs = _rot_tables(pos_ref[...])
        u_ref[...] = proj[:, 1536:2048]
        vs_ref[...] = proj[:, 2048:2560]
        qkv = (_rope(proj[:, 0:512], tabs) * np.float32(ATTN_SCALE), _rope(proj[:, 512:1024], tabs),
               proj[:, 1024:1536])
        for t, val in enumerate(qkv):
            qkv_refs[t][...] = val.astype(BF16)
            for i, dil in enumerate(DILATIONS[1:]):
                _store_view(val, qkv_refs[3 * (i + 1) + t], slabs[t], dil)

    return _pcall(
        body, name="inproj_fwd", grid=(SEQ // TM,),
        in_specs=[_rows(D_MODEL), _rows(1), _resident((1, D_MODEL)), _resident((N_SHARD, D_MODEL, IN_S))],
        out_specs=[_rows(D_MODEL), _rows(512), _rows(512)] + [_view_rows(dil) for dil in DILATIONS for _ in range(3)],
        out_shape=[_sds((SEQ, D_MODEL), BF16), _sds((SEQ, 512), F32), _sds((SEQ, 512), F32)]
        + [_view_shape(dil, BF16) for dil in DILATIONS for _ in range(3)],
        scratch_shapes=[_slab_scratch() for _ in range(3)],
        args=(x, pos, g_pre, w_in), comms=comms)


def _sgu_fwd(u, vs, lg, lb, w_sp, b_t, comms=()):
    def body(u_ref, vs_ref, lg_ref, lb_ref, w_ref, bt_ref, out_ref):
        wm = _masked_spatial(w_ref)
        bias_full = _dot_exact(bt_ref[...], _group_ones())
        gu, _, _, _, mixed, _, _ = _sgu_core(u_ref[...], vs_ref[...], lg_ref[...], lb_ref[...], wm, bias_full)
        out_ref[...] = gu * mixed

    return _pcall(
        body, name="sgu_fwd", grid=(SEQ // TM,),
        in_specs=[_rows(SGU_W), _rows(SGU_W), _resident((1, SGU_W)), _resident((1, SGU_W)),
                  _resident((SGU_GROUPS, CHUNK, CHUNK)), _resident((CHUNK, SGU_GROUPS))],
        out_specs=_rows(SGU_W),
        out_shape=_sds((SEQ, SGU_W), F32),
        args=(u, vs, lg, lb, w_sp, b_t), comms=comms)


def _block_masks():
    row = lax.broadcasted_iota(jnp.int32, (CHUNK, CHUNK), 0)
    col = lax.broadcasted_iota(jnp.int32, (CHUNK, CHUNK), 1)
    return col <= row, col >= row


def _attn_fwd(qv, kv, vv, dil, comms=()):
    seg = SEQ // dil
    nblk = seg // CHUNK
    rps = 4 if nblk == 1 else 1

    def body(q_ref, k_ref, v_ref, o_ref, l_ref):
        left = _left_half()
        m_cur, m_prev = _block_masks()
        zero = jnp.zeros((CHUNK, CHUNK), BF16)
        ones = (jnp.where(left, 1.0, 0.0).astype(BF16), jnp.where(left, 0.0, 1.0).astype(BF16))

        sides = tuple(enumerate((left, ~left)))

        def rows(b):
            if isinstance(b, int):
                return b * CHUNK, max(b - 1, 0) * CHUNK
            return pl.multiple_of(b * CHUNK, CHUNK), pl.multiple_of(jnp.maximum(b - 1, 0) * CHUNK, CHUNK)

        def first(rr, b):
            r0, rp = rows(b)
            prev_ok = m_prev & (b > 0)
            tiles, scores = [], []
            for hp in range(4):
                ls = slice(rr * ATTN_W + hp * 128, rr * ATTN_W + (hp + 1) * 128)
                qp = q_ref[pl.ds(r0, CHUNK), ls]
                kc = k_ref[pl.ds(r0, CHUNK), ls]
                kp = k_ref[pl.ds(rp, CHUNK), ls] if nblk > 1 else None
                tiles.append((ls, v_ref[pl.ds(r0, CHUNK), ls], v_ref[pl.ds(rp, CHUNK), ls] if nblk > 1 else None))
                for _, hm in sides:
                    qh = jnp.where(hm, qp, zero)
                    sc = jnp.where(m_cur, _dot_nt(qh, kc), NEG)
                    sp = jnp.where(prev_ok, _dot_nt(qh, kp), NEG) if nblk > 1 else None
                    scores.append((sc, sp))
            return tiles, scores

        def second(scores):
            probs = []
            for sc, sp in scores:
                if nblk > 1:
                    m = jnp.max(jnp.maximum(sc, sp), axis=-1, keepdims=True)
                    pc = jnp.exp(sc - m)
                    pp = jnp.exp(sp - m)
                    probs.append((m, pc.astype(BF16), pp.astype(BF16), (pc + pp).astype(BF16)))
                else:
                    m = jnp.max(sc, axis=-1, keepdims=True)
                    pc = jnp.exp(sc - m).astype(BF16)
                    probs.append((m, pc, None, pc))
            return probs

        def third(rr, b, tiles, probs):
            r0, _ = rows(b)
            for hp, (ls, vc, vp) in enumerate(tiles):
                acc = jnp.zeros((CHUNK, CHUNK), F32)
                den = jnp.zeros((CHUNK, CHUNK), F32)
                for side, hm in sides:
                    _, pc, pp, psum = probs[2 * hp + side]
                    acc = acc + _dot(pc, jnp.where(hm, vc, zero))
                    if nblk > 1:
                        acc = acc + _dot(pp, jnp.where(hm, vp, zero))
                    den = den + _dot(psum, ones[side])
                o_ref[pl.ds(r0, CHUNK), ls] = (acc / den).astype(o_ref.dtype)
                lse = jnp.where(left, probs[2 * hp][0], probs[2 * hp + 1][0]) + jnp.log(den)
                l_ref[pl.ds(r0, CHUNK), rr * HEAD_W + 32 * hp:rr * HEAD_W + 32 * hp + 32] = lse[:, 48:80]

        def run(units):
            data = [first(rr, b) for rr, b in units]
            probs = [second(scores) for _, scores in data]
            for (rr, b), (tiles, _), pr in zip(units, data, probs):
                third(rr, b, tiles, pr)

        if nblk == 1:
            run([(rr, 0) for rr in range(rps)])
        else:
            def one(b, carry):
                run([(0, b)])
                return carry

            lax.fori_loop(0, nblk, one, 0)

    spec = pl.BlockSpec((seg, rps * ATTN_W), lambda r: (0, r))
    return _pcall(
        body, name=f"attn_fwd_d{dil}", grid=(dil // rps,),
        in_specs=[spec, spec, spec], out_specs=[spec, pl.BlockSpec((seg, rps * HEAD_W), lambda r: (0, r))],
        out_shape=[_sds((seg, dil * ATTN_W), BF16), _sds((seg, dil * HEAD_W), F32)],
        args=(qv, kv, vv), comms=comms)


def _lane_left(nrows):
    return lax.broadcasted_iota(jnp.int32, (nrows, CHUNK), 1) < HEAD_DIM


def _attn_rows(b):
    if isinstance(b, int):
        return b * CHUNK, max(b - 1, 0) * CHUNK
    return pl.multiple_of(b * CHUNK, CHUNK), pl.multiple_of(jnp.maximum(b - 1, 0) * CHUNK, CHUNK)


def _attn_fwd_fused(qv, kv, vv, dil, comms=()):
    seg = SEQ // dil
    nblk = seg // CHUNK
    rps = 4 if nblk == 1 else 1

    def body(q_ref, k_ref, v_ref, o_ref, l_ref):
        left = _left_half()
        m_cur, m_prev = _block_masks()
        sides = tuple(enumerate((left, ~left)))
        zero = jnp.zeros((CHUNK, CHUNK), BF16)

        def first(rr, b, both):
            r0, rp = _attn_rows(b)
            keys = pl.ds(rp, 2 * CHUNK) if both else pl.ds(r0, CHUNK)
            ok = jnp.concatenate([m_prev, m_cur], axis=1) if both else m_cur
            tiles, scores = [], []
            for hp in range(4):
                ls = slice(rr * ATTN_W + hp * 128, rr * ATTN_W + (hp + 1) * 128)
                qp = q_ref[pl.ds(r0, CHUNK), ls]
                k2 = k_ref[keys, ls]
                tiles.append((ls, v_ref[keys, ls]))
                for _, hm in sides:
                    scores.append(jnp.where(ok, _dot_nt(jnp.where(hm, qp, zero), k2), NEG))
            return tiles, scores

        def second(scores):
            probs = []
            for s in scores:
                m = jnp.max(s, axis=-1, keepdims=True)
                probs.append((m, jnp.exp(s - m).astype(BF16)))
            return probs

        def third(rr, b, tiles, probs, both):
            r0, _ = _attn_rows(b)
            nk = 2 * CHUNK if both else CHUNK
            left_k = _lane_left(nk)
            ones = (jnp.where(left_k, 1.0, 0.0).astype(BF16), jnp.where(left_k, 0.0, 1.0).astype(BF16))
            zero_k = jnp.zeros((nk, CHUNK), BF16)
            for hp, (ls, v2) in enumerate(tiles):
                acc = jnp.zeros((CHUNK, CHUNK), F32)
                den = jnp.zeros((CHUNK, CHUNK), F32)
                for side in range(2):
                    p = probs[2 * hp + side][1]
                    acc = acc + _dot(p, jnp.where(left_k if side == 0 else ~left_k, v2, zero_k))
                    den = den + _dot(p, ones[side])
                o_ref[pl.ds(r0, CHUNK), ls] = (acc / den).astype(o_ref.dtype)
                lse = jnp.where(left, probs[2 * hp][0], probs[2 * hp + 1][0]) + jnp.log(den)
                l_ref[pl.ds(r0, CHUNK), rr * HEAD_W + 32 * hp:rr * HEAD_W + 32 * hp + 32] = lse[:, 48:80]

        def run(units, both):
            data = [first(rr, b, both) for rr, b in units]
            probs = [second(scores) for _, scores in data]
            for (rr, b), (tiles, _), pr in zip(units, data, probs):
                third(rr, b, tiles, pr, both)

        run([(rr, 0) for rr in range(rps)], False)
        if nblk > 1:
            def one(b, carry):
                run([(0, b)], True)
                return carry

            lax.fori_loop(1, nblk, one, 0)

    spec = pl.BlockSpec((seg, rps * ATTN_W), lambda r: (0, r))
    return _pcall(
        body, name=f"attn_fwd_d{dil}", grid=(dil // rps,),
        in_specs=[spec, spec, spec], out_specs=[spec, pl.BlockSpec((seg, rps * HEAD_W), lambda r: (0, r))],
        out_shape=[_sds((seg, dil * ATTN_W), BF16), _sds((seg, dil * HEAD_W), F32)],
        args=(qv, kv, vv), comms=comms)


def _mix_out_fwd(o_list, l_list, sgu, x, w_out, g_attn, g_sgu, g_post, comms=()):
    def body(o1, o2, o3, l1, l2, l3, sgu_ref, x_ref, w_ref, ga_ref, gs_ref, gp_ref,
             attn_ref, mixed_ref, y_ref, x1_ref, lse1_ref, lse2_ref, lse3_ref, slabs_a, slabs_b):
        os = [o1[...], _load_view(o2, slabs_a, DILATIONS[1]), _load_view(o3, slabs_b, DILATIONS[2])]
        ls = [l1[...], _load_view(l2, slabs_a, DILATIONS[1], HEAD_W), _load_view(l3, slabs_b, DILATIONS[2], HEAD_W)]
        m = jnp.maximum(jnp.maximum(ls[0], ls[1]), ls[2])
        es = [jnp.exp(l - m) for l in ls]
        den = es[0] + es[1] + es[2]
        spread = _head_spread()
        attn = sum(_dot_select(e / den, spread) * o for e, o in zip(es, os))
        attn_ref[...] = attn
        lse = m + jnp.log(den)
        lse1_ref[...] = lse
        _store_view(lse, lse2_ref, slabs_a, DILATIONS[1])
        _store_view(lse, lse3_ref, slabs_b, DILATIONS[2])
        ah, _ = _rms_stats(attn)
        sh, _ = _rms_stats(sgu_ref[...])
        mixed = jnp.concatenate([ah * ga_ref[...], sh * gs_ref[...]], axis=1).astype(BF16)
        mixed_ref[...] = mixed
        y = _dot(mixed[:, 0:OUT_S], w_ref[0])
        for s in range(1, N_SHARD):
            y = y + _dot(mixed[:, s * OUT_S:(s + 1) * OUT_S], w_ref[s])
        y_ref[...] = y
        yh, _ = _rms_stats(y)
        x1_ref[...] = x_ref[...] + yh * gp_ref[...]

    return _pcall(
        body, name="mix_out_fwd", grid=(SEQ // TM,),
        in_specs=[_view_rows(dil) for dil in DILATIONS] + [_view_rows(dil, HEAD_W) for dil in DILATIONS]
        + [_rows(512), _rows(D_MODEL), _resident((N_SHARD, OUT_S, D_MODEL)),
           _resident((1, 512)), _resident((1, 512)), _resident((1, D_MODEL))],
        out_specs=[_rows(512), _rows(D_MODEL), _rows(D_MODEL), _rows(D_MODEL)]
        + [_view_rows(dil, HEAD_W) for dil in DILATIONS],
        out_shape=[_sds((SEQ, 512), F32), _sds((SEQ, D_MODEL), BF16), _sds((SEQ, D_MODEL), F32),
                   _sds((SEQ, D_MODEL), F32)] + [_view_shape(dil, F32, HEAD_W) for dil in DILATIONS],
        scratch_shapes=[_slab_scratch(), _slab_scratch()],
        args=(*o_list, *l_list, sgu, x, w_out, g_attn, g_sgu, g_post), comms=comms)


def _ffn_fwd_bwd(x1, target, w_gate, w_up, w_down, g_pre, g_post, comms=()):
    def body(x1_ref, t_ref, wg_ref, wu_ref, wd_ref, gpf_ref, gpo_ref,
             h2_ref, a_ref, dg_ref, dup_ref, df_ref, dx1_ref, loss_ref, dgpf_ref, dgpo_ref, g_scr, up_scr):
        @pl.when(pl.program_id(0) == 0)
        def _():
            loss_ref[...] = jnp.zeros_like(loss_ref)
            dgpf_ref[...] = jnp.zeros_like(dgpf_ref)
            dgpo_ref[...] = jnp.zeros_like(dgpo_ref)

        x1 = x1_ref[...]
        gpf = gpf_ref[...]
        gpo = gpo_ref[...]
        xh, r = _rms_stats(x1)
        h2 = (xh * gpf).astype(BF16)
        h2_ref[...] = h2
        f = jnp.zeros((TM_FFN, D_MODEL), F32)
        for c0, c1 in FF_CHUNKS:
            g = _dot_nt(h2, wg_ref[c0:c1, :])
            up = _dot_nt(h2, wu_ref[c0:c1, :])
            g_scr[:, c0:c1] = g
            up_scr[:, c0:c1] = up
            a = (g * _sigmoid(g) * up).astype(BF16)
            a_ref[:, c0:c1] = a
            f = f + _dot(a, wd_ref[c0:c1, :])
        fh, rf = _rms_stats(f)
        diff = x1 + fh * gpo - t_ref[...]
        loss_ref[...] += jnp.sum(diff * diff, axis=0, keepdims=True)
        dout = diff * np.float32(1.0 / D_MODEL)
        df, dgpo = _rms_bwd(fh, rf, gpo, dout)
        dgpo_ref[...] += dgpo
        dfb = df.astype(BF16)
        df_ref[...] = dfb
        dh2 = jnp.zeros((TM_FFN, D_MODEL), F32)
        for c0, c1 in FF_CHUNKS:
            da = _dot_nt(dfb, wd_ref[c0:c1, :])
            g = g_scr[:, c0:c1]
            up = up_scr[:, c0:c1]
            sg = _sigmoid(g)
            dup = (da * (g * sg)).astype(BF16)
            dg = (da * up * (sg * (1.0 + g * (1.0 - sg)))).astype(BF16)
            dg_ref[:, c0:c1] = dg
            dup_ref[:, c0:c1] = dup
            dh2 = dh2 + _dot(dg, wg_ref[c0:c1, :]) + _dot(dup, wu_ref[c0:c1, :])
        dx, dgpf = _rms_bwd(xh, r, gpf, dh2)
        dgpf_ref[...] += dgpf
        dx1_ref[...] = dout + dx

    return _pcall(
        body, name="ffn_fwd_bwd", grid=(SEQ // TM_FFN,),
        in_specs=[_rows(D_MODEL, TM_FFN), _rows(D_MODEL, TM_FFN), _resident((FF, D_MODEL)),
                  _resident((FF, D_MODEL)), _resident((FF, D_MODEL)),
                  _resident((1, D_MODEL)), _resident((1, D_MODEL))],
        out_specs=[_rows(D_MODEL, TM_FFN), _rows(FF, TM_FFN), _rows(FF, TM_FFN), _rows(FF, TM_FFN),
                   _rows(D_MODEL, TM_FFN), _rows(D_MODEL, TM_FFN), _acc(D_MODEL), _acc(D_MODEL), _acc(D_MODEL)],
        out_shape=[_sds((SEQ, D_MODEL), BF16), _sds((SEQ, FF), BF16), _sds((SEQ, FF), BF16),
                   _sds((SEQ, FF), BF16), _sds((SEQ, D_MODEL), BF16), _sds((SEQ, D_MODEL), F32),
                   _sds((1, D_MODEL), F32), _sds((1, D_MODEL), F32), _sds((1, D_MODEL), F32)],
        scratch_shapes=[pltpu.VMEM((TM_FFN, FF), F32), pltpu.VMEM((TM_FFN, FF), F32)],
        args=(x1, target, w_gate, w_up, w_down, g_pre, g_post), comms=comms)


def _wgrad(a, b, a_spec, b_spec, out_block, name, comms=()):
    def body(a_ref, b_ref, o_ref):
        av = a_ref[0] if len(a_ref.shape) == 3 else a_ref[...]
        bv = b_ref[0] if len(b_ref.shape) == 3 else b_ref[...]
        o_ref[0] = _dot_tn(av, bv)

    return _pcall(
        body, name=name, grid=(N_SHARD,),
        in_specs=[a_spec, b_spec],
        out_specs=pl.BlockSpec((1,) + out_block, lambda s: (s, 0, 0)),
        out_shape=_sds((N_SHARD,) + out_block, F32),
        args=(a, b), comms=comms)


def _outproj_bwd(dx1, y, attn, sgu, w_out, g_post, g_attn, g_sgu, comms=()):
    def body(dx1_ref, y_ref, attn_ref, sgu_ref, w_ref, gp_ref, ga_ref, gs_ref,
             dy_ref, dsgu_ref, dgp_ref, dga_ref, dgs_ref, *rest):
        dattn_refs, delta_refs, (slabs_a, slabs_b) = rest[0:3], rest[3:6], rest[6:]

        @pl.when(pl.program_id(0) == 0)
        def _():
            dgp_ref[...] = jnp.zeros_like(dgp_ref)
            dga_ref[...] = jnp.zeros_like(dga_ref)
            dgs_ref[...] = jnp.zeros_like(dgs_ref)

        yh, ry = _rms_stats(y_ref[...])
        dy, dgp = _rms_bwd(yh, ry, gp_ref[...], dx1_ref[...])
        dgp_ref[...] += dgp
        dyb = dy.astype(BF16)
        dy_ref[...] = dyb
        dmixed = jnp.concatenate([_dot_nt(dyb, w_ref[s]) for s in range(N_SHARD)], axis=1)
        attn = attn_ref[...]
        ah, ra = _rms_stats(attn)
        dattn, dga = _rms_bwd(ah, ra, ga_ref[...], dmixed[:, 0:512])
        dga_ref[...] += dga
        sh, rs = _rms_stats(sgu_ref[...])
        dsgu, dgs = _rms_bwd(sh, rs, gs_ref[...], dmixed[:, 512:1024])
        dgs_ref[...] += dgs
        dsgu_ref[...] = dsgu
        delta = _dot_select(dattn * attn, _head_sum())
        dattn_refs[0][...] = dattn.astype(BF16)
        delta_refs[0][...] = delta
        for i, dil in enumerate(DILATIONS[1:]):
            _store_view(dattn, dattn_refs[i + 1], slabs_a, dil)
            _store_view(delta, delta_refs[i + 1], slabs_b, dil)

    return _pcall(
        body, name="outproj_bwd", grid=(SEQ // TM,),
        in_specs=[_rows(D_MODEL), _rows(D_MODEL), _rows(512), _rows(512), _resident((N_SHARD, OUT_S, D_MODEL)),
                  _resident((1, D_MODEL)), _resident((1, 512)), _resident((1, 512))],
        out_specs=[_rows(D_MODEL), _rows(512), _acc(D_MODEL), _acc(512), _acc(512)]
        + [_view_rows(dil) for dil in DILATIONS] + [_view_rows(dil, HEAD_W) for dil in DILATIONS],
        out_shape=[_sds((SEQ, D_MODEL), BF16), _sds((SEQ, 512), F32),
                   _sds((1, D_MODEL), F32), _sds((1, 512), F32), _sds((1, 512), F32)]
        + [_view_shape(dil, BF16) for dil in DILATIONS] + [_view_shape(dil, F32, HEAD_W) for dil in DILATIONS],
        scratch_shapes=[_slab_scratch(), _slab_scratch()],
        args=(dx1, y, attn, sgu, w_out, g_post, g_attn, g_sgu), comms=comms)


def _sgu_bwd(u, vs, dsgu, lg, lb, w_sp, b_t, comms=(), after=()):
    nsteps = SEQ // TM

    def body(u_ref, vs_ref, ds_ref, lg_ref, lb_ref, w_ref, bt_ref,
             du_ref, dvs_ref, dw_ref, db_ref, dlg_ref, dlb_ref, dbias_scr):
        i = pl.program_id(0)

        @pl.when(i == 0)
        def _():
            dw_ref[...] = jnp.zeros_like(dw_ref)
            dlg_ref[...] = jnp.zeros_like(dlg_ref)
            dlb_ref[...] = jnp.zeros_like(dlb_ref)
            dbias_scr[...] = jnp.zeros_like(dbias_scr)

        wm = _masked_spatial(w_ref)
        ones_g = _group_ones()
        bias_full = _dot_exact(bt_ref[...], ones_g)
        u = u_ref[...]
        vs = vs_ref[...]
        lg = lg_ref[...]
        gu, xh, rstd, vnb, mixed, cdf_u, cdf_vs = _sgu_core(u, vs, lg, lb_ref[...], wm, bias_full)
        dsgu = ds_ref[...]
        du_ref[...] = (dsgu * mixed * _gelu_grad(u, cdf_u)).astype(BF16)
        dmixed = dsgu * gu
        left = _left_half()
        dvn_rows = []
        for c in range(TM // CHUNK):
            rs = slice(c * CHUNK, (c + 1) * CHUNK)
            dm_c = dmixed[rs, :]
            dbias_scr[...] += dm_c
            pieces = []
            for p in range(4):
                ls = slice(p * 128, (p + 1) * 128)
                dmp = dm_c[:, ls]
                vp = vnb[rs, ls]
                dmb = dmp.astype(BF16)
                zero = jnp.zeros_like(dmb)
                dw_ref[2 * p] += _dot_nt(jnp.where(left, dmb, zero), vp)
                dw_ref[2 * p + 1] += _dot_nt(jnp.where(left, zero, dmb), vp)
                pieces.append(jnp.where(left, _dot_tn(wm[2 * p], dmb), _dot_tn(wm[2 * p + 1], dmb)))
            dvn_rows.append(jnp.concatenate(pieces, axis=1))
        dvn = jnp.concatenate(dvn_rows, axis=0)
        dlg_ref[...] += jnp.sum(dvn * xh, axis=0, keepdims=True)
        dlb_ref[...] += jnp.sum(dvn, axis=0, keepdims=True)
        dxh = dvn * lg
        dgv = rstd * (dxh - jnp.mean(dxh, axis=-1, keepdims=True) - xh * jnp.mean(dxh * xh, axis=-1, keepdims=True))
        dvs_ref[...] = (dgv * _gelu_grad(vs, cdf_vs)).astype(BF16)

        @pl.when(i == nsteps - 1)
        def _():
            row = lax.broadcasted_iota(jnp.int32, (CHUNK, CHUNK), 0)
            col = lax.broadcasted_iota(jnp.int32, (CHUNK, CHUNK), 1)
            for g in range(SGU_GROUPS):
                dw_ref[g] = jnp.where(col <= row, dw_ref[g], 0.0)
            db_ref[...] = lax.dot_general(ones_g, dbias_scr[...], (((1,), (1,)), ((), ())),
                                          preferred_element_type=F32, precision=lax.Precision.HIGHEST)

    return _pcall(
        body, name="sgu_bwd", grid=(nsteps,),
        in_specs=[_rows(SGU_W), _rows(SGU_W), _rows(SGU_W), _resident((1, SGU_W)), _resident((1, SGU_W)),
                  _resident((SGU_GROUPS, CHUNK, CHUNK)), _resident((CHUNK, SGU_GROUPS))],
        out_specs=[_rows(SGU_W), _rows(SGU_W), pl.BlockSpec((SGU_GROUPS, CHUNK, CHUNK), lambda i: (0, 0, 0)),
                   _acc(CHUNK, SGU_GROUPS), _acc(SGU_W), _acc(SGU_W)],
        out_shape=[_sds((SEQ, SGU_W), BF16), _sds((SEQ, SGU_W), BF16), _sds((SGU_GROUPS, CHUNK, CHUNK), F32),
                   _sds((SGU_GROUPS, CHUNK), F32), _sds((1, SGU_W), F32), _sds((1, SGU_W), F32)],
        scratch_shapes=[pltpu.VMEM((CHUNK, SGU_W), F32)],
        args=(u, vs, dsgu, lg, lb, w_sp, b_t), comms=comms, after=after)


def _attn_bwd_v1(qv, kv, vv, dov, deltav, lsev, dil, comms=(), after=()):
    seg = SEQ // dil
    nblk = seg // CHUNK
    rps = 4 if nblk == 1 else 1

    def body(q_ref, k_ref, v_ref, do_ref, dl_ref, lse_ref, dq_ref, dk_ref, dv_ref, dk_wait, dv_wait):
        left = _left_half()
        m_cur, m_prev = _block_masks()

        sides = tuple(enumerate((left, ~left)))
        zero = jnp.zeros((CHUNK, CHUNK), BF16)

        def rows(b):
            if isinstance(b, int):
                return b * CHUNK, max(b - 1, 0) * CHUNK
            return pl.multiple_of(b * CHUNK, CHUNK), pl.multiple_of(jnp.maximum(b - 1, 0) * CHUNK, CHUNK)

        def first(rr, b):
            r0, rp = rows(b)
            tiles, firsts = [], []
            for hp in range(4):
                ls = slice(rr * ATTN_W + hp * 128, rr * ATTN_W + (hp + 1) * 128)
                qp = q_ref[pl.ds(r0, CHUNK), ls]
                kc = k_ref[pl.ds(r0, CHUNK), ls]
                vc = v_ref[pl.ds(r0, CHUNK), ls]
                dop = do_ref[pl.ds(r0, CHUNK), ls]
                kp = k_ref[pl.ds(rp, CHUNK), ls] if nblk > 1 else None
                vp = v_ref[pl.ds(rp, CHUNK), ls] if nblk > 1 else None
                tiles.append((ls, kc, kp))
                for side, hm in sides:
                    qh = jnp.where(hm, qp, zero)
                    doh = jnp.where(hm, dop, zero)
                    cur = (_dot_nt(kc, qh), _dot_nt(vc, doh))
                    prev = (_dot_nt(kp, qh), _dot_nt(vp, doh)) if nblk > 1 else None
                    firsts.append((qh, doh, cur, prev))
            return tiles, firsts

        def second(rr, b, firsts):
            r0, _ = rows(b)
            own_ok, prev_ok = m_prev, m_cur & (b > 0)
            lanes = slice(rr * HEAD_W, (rr + 1) * HEAD_W)
            lse_t = lse_ref[pl.ds(r0, CHUNK), lanes].T
            dl_t = dl_ref[pl.ds(r0, CHUNK), lanes].T
            seconds = []
            for i, (qh, doh, cur, prev) in enumerate(firsts):
                lse_h = lse_t[16 * i:16 * i + 1, :]
                dl_h = dl_t[16 * i:16 * i + 1, :]
                pc = jnp.exp(jnp.where(own_ok, cur[0] - lse_h, NEG))
                out = [pc.astype(BF16), (pc * (cur[1] - dl_h)).astype(BF16), None, None]
                if nblk > 1:
                    pp = jnp.exp(jnp.where(prev_ok, prev[0] - lse_h, NEG))
                    out[2:] = [pp.astype(BF16), (pp * (prev[1] - dl_h)).astype(BF16)]
                seconds.append(out)
            return seconds

        def third(rr, b, tiles, firsts, seconds):
            r0, rp = rows(b)
            for hp, (ls, kc, kp) in enumerate(tiles):
                dq = jnp.zeros((CHUNK, CHUNK), F32)
                dkc = jnp.zeros((CHUNK, CHUNK), F32)
                dvc = jnp.zeros((CHUNK, CHUNK), F32)
                dkp = jnp.zeros((CHUNK, CHUNK), F32)
                dvp = jnp.zeros((CHUNK, CHUNK), F32)
                for side, hm in sides:
                    qh, doh, _, _ = firsts[2 * hp + side]
                    pcb, dsc, ppb, dsp = seconds[2 * hp + side]
                    dq = dq + _dot_tn(dsc, jnp.where(hm, kc, zero))
                    dkc = dkc + _dot(dsc, qh)
                    dvc = dvc + _dot(pcb, doh)
                    if nblk > 1:
                        dq = dq + _dot_tn(dsp, jnp.where(hm, kp, zero))
                        dkp = dkp + _dot(dsp, qh)
                        dvp = dvp + _dot(ppb, doh)
                dq_ref[pl.ds(r0, CHUNK), ls] = dq.astype(dq_ref.dtype)
                if nblk == 1:
                    dk_ref[pl.ds(r0, CHUNK), ls] = dkc.astype(dk_ref.dtype)
                    dv_ref[pl.ds(r0, CHUNK), ls] = dvc.astype(dv_ref.dtype)
                else:
                    @pl.when(b > 0)
                    def _():
                        dk_ref[pl.ds(rp, CHUNK), ls] = (dk_wait[:, ls] + dkp).astype(dk_ref.dtype)
                        dv_ref[pl.ds(rp, CHUNK), ls] = (dv_wait[:, ls] + dvp).astype(dv_ref.dtype)

                    dk_wait[:, ls] = dkc
                    dv_wait[:, ls] = dvc

        def run(units):
            data = [first(rr, b) for rr, b in units]
            probs = [second(rr, b, firsts) for (rr, b), (_, firsts) in zip(units, data)]
            for (rr, b), (tiles, firsts), seconds in zip(units, data, probs):
                third(rr, b, tiles, firsts, seconds)

        if nblk == 1:
            run([(rr, 0) for rr in range(rps)])
        else:
            def one(b, carry):
                run([(0, b)])
                return carry

            lax.fori_loop(0, nblk, one, 0)
            last = (nblk - 1) * CHUNK
            dk_ref[last:last + CHUNK, :] = dk_wait[...].astype(dk_ref.dtype)
            dv_ref[last:last + CHUNK, :] = dv_wait[...].astype(dv_ref.dtype)

    spec = pl.BlockSpec((seg, rps * ATTN_W), lambda r: (0, r))
    return _pcall(
        body, name=f"attn_bwd_d{dil}", grid=(dil // rps,),
        in_specs=[spec] * 4 + [pl.BlockSpec((seg, rps * HEAD_W), lambda r: (0, r))] * 2, out_specs=[spec] * 3,
        out_shape=[_sds((seg, dil * ATTN_W), BF16)] * 3,
        scratch_shapes=[pltpu.VMEM((CHUNK, ATTN_W), F32), pltpu.VMEM((CHUNK, ATTN_W), F32)],
        args=(qv, kv, vv, dov, deltav, lsev), comms=comms, after=after)


def _attn_bwd(qv, kv, vv, dov, deltav, lsev, dil, comms=(), after=()):
    seg = SEQ // dil
    nblk = seg // CHUNK
    rps = 4 if nblk == 1 else 1

    def body(q_ref, k_ref, v_ref, do_ref, dl_ref, lse_ref, dq_ref, dk_ref, dv_ref, dk_wait, dv_wait):
        left = _left_half()
        m_cur, m_prev = _block_masks()
        sides = tuple(enumerate((left, ~left)))
        zero = jnp.zeros((CHUNK, CHUNK), BF16)

        def first(rr, b, both):
            r0, rp = _attn_rows(b)
            keys = pl.ds(rp, 2 * CHUNK) if both else pl.ds(r0, CHUNK)
            tiles, heads = [], []
            for hp in range(4):
                ls = slice(rr * ATTN_W + hp * 128, rr * ATTN_W + (hp + 1) * 128)
                qp = q_ref[pl.ds(r0, CHUNK), ls]
                dop = do_ref[pl.ds(r0, CHUNK), ls]
                k2 = k_ref[keys, ls]
                v2 = v_ref[keys, ls]
                tiles.append((ls, k2))
                for _, hm in sides:
                    qh = jnp.where(hm, qp, zero)
                    doh = jnp.where(hm, dop, zero)
                    heads.append((qh, doh, _dot_nt(k2, qh), _dot_nt(v2, doh)))
            return tiles, heads

        def second(rr, b, heads, both):
            r0, _ = _attn_rows(b)
            ok = jnp.concatenate([m_cur, m_prev], axis=0) if both else m_prev
            lanes = slice(rr * HEAD_W, (rr + 1) * HEAD_W)
            lse_t = lse_ref[pl.ds(r0, CHUNK), lanes].T
            dl_t = dl_ref[pl.ds(r0, CHUNK), lanes].T
            out = []
            for i, (_, _, s_t, dp_t) in enumerate(heads):
                p = jnp.exp(jnp.where(ok, s_t - lse_t[16 * i:16 * i + 1, :], NEG))
                out.append((p.astype(BF16), (p * (dp_t - dl_t[16 * i:16 * i + 1, :])).astype(BF16)))
            return out

        def third(rr, b, tiles, heads, probs, both):
            r0, rp = _attn_rows(b)
            nk = 2 * CHUNK if both else CHUNK
            left_k = _lane_left(nk)
            zero_k = jnp.zeros((nk, CHUNK), BF16)
            for hp, (ls, k2) in enumerate(tiles):
                dq = jnp.zeros((CHUNK, CHUNK), F32)
                dk2 = jnp.zeros((nk, CHUNK), F32)
                dv2 = jnp.zeros((nk, CHUNK), F32)
                for side in range(2):
                    qh, doh, _, _ = heads[2 * hp + side]
                    p, ds = probs[2 * hp + side]
                    dq = dq + _dot_tn(ds, jnp.where(left_k if side == 0 else ~left_k, k2, zero_k))
                    dk2 = dk2 + _dot(ds, qh)
                    dv2 = dv2 + _dot(p, doh)
                dq_ref[pl.ds(r0, CHUNK), ls] = dq.astype(dq_ref.dtype)
                if nblk == 1:
                    dk_ref[pl.ds(r0, CHUNK), ls] = dk2.astype(dk_ref.dtype)
                    dv_ref[pl.ds(r0, CHUNK), ls] = dv2.astype(dv_ref.dtype)
                elif both:
                    dk_ref[pl.ds(rp, CHUNK), ls] = (dk_wait[:, ls] + dk2[0:CHUNK]).astype(dk_ref.dtype)
                    dv_ref[pl.ds(rp, CHUNK), ls] = (dv_wait[:, ls] + dv2[0:CHUNK]).astype(dv_ref.dtype)
                    dk_wait[:, ls] = dk2[CHUNK:]
                    dv_wait[:, ls] = dv2[CHUNK:]
                else:
                    dk_wait[:, ls] = dk2
                    dv_wait[:, ls] = dv2

        def run(units, both):
            data = [first(rr, b, both) for rr, b in units]
            probs = [second(rr, b, heads, both) for (rr, b), (_, heads) in zip(units, data)]
            for (rr, b), (tiles, heads), pr in zip(units, data, probs):
                third(rr, b, tiles, heads, pr, both)

        run([(rr, 0) for rr in range(rps)], False)
        if nblk > 1:
            def one(b, carry):
                run([(0, b)], True)
                return carry

            lax.fori_loop(1, nblk, one, 0)
            last = (nblk - 1) * CHUNK
            dk_ref[last:last + CHUNK, :] = dk_wait[...].astype(dk_ref.dtype)
            dv_ref[last:last + CHUNK, :] = dv_wait[...].astype(dv_ref.dtype)

    spec = pl.BlockSpec((seg, rps * ATTN_W), lambda r: (0, r))
    return _pcall(
        body, name=f"attn_bwd_d{dil}", grid=(dil // rps,),
        in_specs=[spec] * 4 + [pl.BlockSpec((seg, rps * HEAD_W), lambda r: (0, r))] * 2, out_specs=[spec] * 3,
        out_shape=[_sds((seg, dil * ATTN_W), BF16)] * 3,
        scratch_shapes=[pltpu.VMEM((CHUNK, ATTN_W), F32), pltpu.VMEM((CHUNK, ATTN_W), F32)],
        args=(qv, kv, vv, dov, deltav, lsev), comms=comms, after=after)


def _inproj_bwd(dqs, dks, dvs, du, dvs_sgu, pos, x, dx1, w_in, g_pre, comms=()):
    def body(dq1, dq2, dq3, dk1, dk2, dk3, dv1, dv2, dv3, du_ref, dvs_ref, pos_ref, x_ref, dx1_ref, w_ref, g_ref,
             dproj_ref, gx_ref, dg_ref, slabs_a, slabs_b):
        @pl.when(pl.program_id(0) == 0)
        def _():
            dg_ref[...] = jnp.zeros_like(dg_ref)

        def total(r1, r2, r3):
            return r1[...] + _load_view(r2, slabs_a, DILATIONS[1]) + _load_view(r3, slabs_b, DILATIONS[2])

        tabs = _rot_tables(pos_ref[...])
        dproj_ref[:, 0:512] = _rope_bwd(total(dq1, dq2, dq3) * np.float32(ATTN_SCALE), tabs).astype(BF16)
        dproj_ref[:, 512:1024] = _rope_bwd(total(dk1, dk2, dk3), tabs).astype(BF16)
        dproj_ref[:, 1024:1536] = total(dv1, dv2, dv3).astype(BF16)
        dproj_ref[:, 1536:2048] = du_ref[...]
        dproj_ref[:, 2048:2560] = dvs_ref[...]
        dh = jnp.zeros((TM, D_MODEL), F32)
        for s in range(N_SHARD):
            dh = dh + _dot_nt(dproj_ref[:, s * IN_S:(s + 1) * IN_S], w_ref[s])
        g = g_ref[...]
        xh, r = _rms_stats(x_ref[...])
        dx, dg = _rms_bwd(xh, r, g, dh)
        dg_ref[...] += dg
        gx_ref[...] = dx1_ref[...] + dx

    return _pcall(
        body, name="inproj_bwd", grid=(SEQ // TM,),
        in_specs=[_view_rows(dil) for dil in DILATIONS] * 3
        + [_rows(512), _rows(512), _rows(1), _rows(D_MODEL), _rows(D_MODEL),
           _resident((N_SHARD, D_MODEL, IN_S)), _resident((1, D_MODEL))],
        out_specs=[_rows(PROJ_W), _rows(D_MODEL), _acc(D_MODEL)],
        out_shape=[_sds((SEQ, PROJ_W), BF16), _sds((SEQ, D_MODEL), F32), _sds((1, D_MODEL), F32)],
        scratch_shapes=[_slab_scratch(), _slab_scratch()],
        args=(*dqs, *dks, *dvs, du, dvs_sgu, pos, x, dx1, w_in, g_pre), comms=comms)


def _to_view(a, dil):
    return a if dil == 1 else a.reshape(SEQ // dil, dil * a.shape[1])


def _from_view(a, dil):
    return a if dil == 1 else a.reshape(SEQ, a.shape[1] // dil)


def _local_step(x, pos, target, w_in, w_out, w_gate, w_up, w_down, small):
    b_t = small["sgu_b_spatial"].T
    h, u, vs, *qkv = _inproj_fwd(x, pos, small["pre_mix_norm"], w_in)
    sgu = _sgu_fwd(u, vs, small["sgu_ln_gain"], small["sgu_ln_bias"], small["sgu_w_spatial"], b_t)
    views = [tuple(qkv[3 * i:3 * i + 3]) for i in range(len(DILATIONS))]
    o_list, l_list = [], []
    for dil, (qv, kv, vv) in zip(DILATIONS, views):
        o, l = _attn_fwd(qv, kv, vv, dil)
        o_list.append(o)
        l_list.append(l)
    attn, mixed, y, x1, *lses = _mix_out_fwd(o_list, l_list, sgu, x, w_out, small["attn_out_norm"],
                                             small["sgu_out_norm"], small["post_mix_norm"])
    h2, a, dg, dup, df, dx1, loss_cols, d_pre_ffn, d_post_ffn = _ffn_fwd_bwd(
        x1, target, w_gate, w_up, w_down, small["pre_ffn_norm"], small["post_ffn_norm"])

    full_tok = pl.BlockSpec((SEQ, D_MODEL), lambda s: (0, 0), pipeline_mode=pl.Buffered(1))
    ff_tok = pl.BlockSpec((1, SEQ, FF_S), lambda s: (s, 0, 0))
    gw_gate, gw_up = _wgrad_ff([dg, dup], h2, "wgrad_gate_up")
    (gw_down,) = _wgrad_ff([a], df, "wgrad_down")

    dy, dsgu, d_post_mix, d_attn_norm, d_sgu_norm, *dviews = _outproj_bwd(
        dx1, y, attn, sgu, w_out, small["post_mix_norm"], small["attn_out_norm"], small["sgu_out_norm"])
    gw_out = _wgrad(mixed, dy, pl.BlockSpec((SEQ, OUT_S), lambda s: (0, s)), full_tok, (OUT_S, D_MODEL), "wgrad_out")
    du, dvs_sgu, d_w_sp, d_b_sp, d_ln_gain, d_ln_bias = _sgu_bwd(
        u, vs, dsgu, small["sgu_ln_gain"], small["sgu_ln_bias"], small["sgu_w_spatial"], b_t)

    dqs, dks, dvs = [], [], []
    for i, (dil, (qv, kv, vv)) in enumerate(zip(DILATIONS, views)):
        dq, dk, dv = _attn_bwd(qv, kv, vv, dviews[i], dviews[3 + i], lses[i], dil)
        dqs.append(dq)
        dks.append(dk)
        dvs.append(dv)
    dproj, grad_x, d_pre_mix = _inproj_bwd(dqs, dks, dvs, du, dvs_sgu, pos, x, dx1, w_in, small["pre_mix_norm"])
    gw_in = _wgrad(h, dproj, full_tok, pl.BlockSpec((SEQ, IN_S), lambda s: (0, s)), (D_MODEL, IN_S), "wgrad_in")

    small_grads = {
        "pre_mix_norm": d_pre_mix, "sgu_ln_gain": d_ln_gain, "sgu_ln_bias": d_ln_bias, "sgu_w_spatial": d_w_sp,
        "sgu_b_spatial": d_b_sp, "attn_out_norm": d_attn_norm, "sgu_out_norm": d_sgu_norm,
        "post_mix_norm": d_post_mix, "pre_ffn_norm": d_pre_ffn, "post_ffn_norm": d_post_ffn,
    }
    return loss_cols, grad_x, (gw_in, gw_out, gw_gate, gw_up, gw_down), small_grads


def _coords():
    return lax.axis_index("x"), lax.axis_index("y"), lax.axis_index("c")


def _other_chips(x, y):
    return [(1 - x, y), (x, 1 - y), (1 - x, 1 - y)]


def _comm_call(body, name, n_in, out_shape, scratch_shapes):
    return pl.pallas_call(
        body, name=name, in_specs=[ANY] * n_in, out_specs=[ANY] * len(out_shape), out_shape=out_shape,
        scratch_shapes=scratch_shapes,
        compiler_params=pltpu.CompilerParams(has_side_effects=True),
    )


def _gather_weights(shards):
    n = len(shards)
    halves = [s.reshape(2, s.shape[0] // 2, s.shape[1]) for s in shards]

    def body(*refs):
        ins, outs = refs[:n], refs[n:2 * n]
        send_sems, recv_sems = refs[2 * n:]
        x, y, c = _coords()
        s_me = 2 * x + y
        chips = _other_chips(x, y)
        sibling = (x, y, 1 - c)

        def copy(k, w, shard, cc, to):
            src = ins[w].at[cc] if shard is None else outs[w].at[shard, cc]
            dst = outs[w].at[s_me if shard is None else shard, cc]
            return pltpu.make_async_remote_copy(src_ref=src, dst_ref=dst, send_sem=send_sems.at[k],
                                                recv_sem=recv_sems.at[k], device_id=to, device_id_type=MESH)

        first = [copy(j * n + w, w, None, c, (cx, cy, c)) for j, (cx, cy) in enumerate(chips) for w in range(n)]
        for cp in first:
            cp.start()
        passed = []
        for j, (cx, cy) in enumerate(chips):
            for w in range(n):
                copy(j * n + w, w, 2 * cx + cy, c, (x, y, c)).wait_recv()
                fw = copy((3 + j) * n + w, w, 2 * cx + cy, c, sibling)
                fw.start()
                passed.append(fw)
        for j, (cx, cy) in enumerate(chips):
            for w in range(n):
                copy((3 + j) * n + w, w, 2 * cx + cy, 1 - c, (x, y, c)).wait_recv()
        for cp in first + passed:
            cp.wait_send()

    out_shape = [_sds((N_SHARD,) + h.shape, h.dtype) for h in halves]
    scratch = [pltpu.SemaphoreType.DMA((6 * n,)), pltpu.SemaphoreType.DMA((6 * n,))]
    full = _comm_call(body, "comm_gather_weights", n, out_shape, scratch)(*halves)
    s_me = 2 * lax.axis_index("x") + lax.axis_index("y")
    full = [lax.dynamic_update_slice(f, h[None], (s_me, 0, 0, 0)) for f, h in zip(full, halves)]
    return [f.reshape((N_SHARD,) + s.shape) for f, s in zip(full, shards)]


def _rs_to_sibling(gws):
    n = len(gws)

    def body(*refs):
        ins, outs = refs[:n], refs[n:2 * n]
        send_sems, recv_sems = refs[2 * n:]
        x, y, c = _coords()
        copies = []
        for w in range(n):
            hw = gws[w].shape[1] // 2
            copies.append(pltpu.make_async_remote_copy(
                src_ref=ins[w].at[:, pl.ds((1 - c) * hw, hw), :], dst_ref=outs[w], send_sem=send_sems.at[w],
                recv_sem=recv_sems.at[w], device_id=(x, y, 1 - c), device_id_type=MESH))
        for cp in copies:
            cp.start()
        for cp in copies:
            cp.wait()

    out_shape = [_sds((N_SHARD, g.shape[1] // 2, g.shape[2]), g.dtype) for g in gws]
    scratch = [pltpu.SemaphoreType.DMA((n,)), pltpu.SemaphoreType.DMA((n,))]
    return _comm_call(body, "comm_rs_sibling", n, out_shape, scratch)(*gws)


def _rs_chip_sum(gw, recv, core):
    _, rows, cols = gw.shape
    hw = rows // 2

    def body(c_ref, g_ref, r_ref, o_ref):
        o_ref[...] = (g_ref[...] + r_ref[...]).astype(BF16)

    return pl.pallas_call(
        body, name="rs_chip_sum",
        grid_spec=pltpu.PrefetchScalarGridSpec(
            num_scalar_prefetch=1, grid=(N_SHARD,),
            in_specs=[pl.BlockSpec((1, hw, cols), lambda s, c_ref: (s, c_ref[0], 0)),
                      pl.BlockSpec((1, hw, cols), lambda s, c_ref: (s, 0, 0))],
            out_specs=pl.BlockSpec((1, hw, cols), lambda s, c_ref: (s, 0, 0))),
        out_shape=_sds((N_SHARD, hw, cols), BF16),
        compiler_params=_seq_params(),
    )(core, gw, recv)


def _rs_between_chips(pbs):
    n = len(pbs)

    def body(*refs):
        ins, outs = refs[:n], refs[n:2 * n]
        send_sems, recv_sems = refs[2 * n:]
        x, y, c = _coords()
        copies = []
        for j, (cx, cy) in enumerate(_other_chips(x, y)):
            for w in range(n):
                copies.append(pltpu.make_async_remote_copy(
                    src_ref=ins[w].at[2 * cx + cy], dst_ref=outs[w].at[j], send_sem=send_sems.at[j * n + w],
                    recv_sem=recv_sems.at[j * n + w], device_id=(cx, cy, c), device_id_type=MESH))
        for cp in copies:
            cp.start()
        for cp in copies:
            cp.wait()

    out_shape = [_sds((3,) + p.shape[1:], p.dtype) for p in pbs]
    scratch = [pltpu.SemaphoreType.DMA((3 * n,)), pltpu.SemaphoreType.DMA((3 * n,))]
    return _comm_call(body, "comm_rs_chips", n, out_shape, scratch)(*pbs)


def _rs_final_sum(gw, recv_sib, recv_chips, shard_core):
    _, rows, cols = gw.shape
    hw = rows // 2

    def body(sc_ref, g_ref, r_ref, rc_ref, o_ref):
        acc = g_ref[0] + r_ref[0]
        for j in range(3):
            acc = acc + rc_ref[j].astype(F32)
        o_ref[0] = acc

    return pl.pallas_call(
        body, name="rs_final_sum",
        grid_spec=pltpu.PrefetchScalarGridSpec(
            num_scalar_prefetch=1, grid=(1,),
            in_specs=[pl.BlockSpec((1, hw, cols), lambda i, sc: (sc[0], sc[1], 0)),
                      pl.BlockSpec((1, hw, cols), lambda i, sc: (sc[0], 0, 0)),
                      pl.BlockSpec((3, hw, cols), lambda i, sc: (0, 0, 0))],
            out_specs=pl.BlockSpec((1, hw, cols), lambda i, sc: (sc[1], 0, 0))),
        out_shape=_sds((2, hw, cols), F32),
        compiler_params=_seq_params(),
    )(shard_core, gw, recv_sib, recv_chips)


def _rs_join_halves(halves):
    n = len(halves)

    def body(*refs):
        bufs = refs[n:2 * n]
        send_sems, recv_sems = refs[2 * n:]
        x, y, c = _coords()
        remote = [pltpu.make_async_remote_copy(
            src_ref=bufs[w].at[c], dst_ref=bufs[w].at[c], send_sem=send_sems.at[w], recv_sem=recv_sems.at[w],
            device_id=(x, y, 1 - c), device_id_type=MESH) for w in range(n)]
        for cp in remote:
            cp.start()
        for w in range(n):
            remote[w].wait_send()
            pltpu.make_async_remote_copy(
                src_ref=bufs[w].at[c], dst_ref=bufs[w].at[1 - c], send_sem=send_sems.at[w],
                recv_sem=recv_sems.at[w], device_id=(x, y, c), device_id_type=MESH).wait_recv()

    joined = pl.pallas_call(
        body, name="comm_rs_join", in_specs=[ANY] * n, out_specs=[ANY] * n,
        out_shape=[_sds(h.shape, h.dtype) for h in halves], input_output_aliases={w: w for w in range(n)},
        scratch_shapes=[pltpu.SemaphoreType.DMA((n,)), pltpu.SemaphoreType.DMA((n,))],
        compiler_params=pltpu.CompilerParams(has_side_effects=True),
    )(*halves)
    return [j.reshape(2 * h.shape[1], h.shape[2]) for j, h in zip(joined, halves)]


def _allreduce_small(buf):
    rows, cols = buf.shape

    def body(in_ref, out_ref, slots, send_sems, recv_sems):
        x, y, c = _coords()
        me = 4 * x + 2 * y + c
        copies, peers = [], []
        for k in range(1, 8):
            px = 1 - x if (k >> 2) & 1 else x
            py = 1 - y if (k >> 1) & 1 else y
            pc = 1 - c if k & 1 else c
            peers.append(4 * px + 2 * py + pc)
            copies.append(pltpu.make_async_remote_copy(
                src_ref=in_ref, dst_ref=slots.at[me], send_sem=send_sems.at[k - 1], recv_sem=recv_sems.at[k - 1],
                device_id=(px, py, pc), device_id_type=MESH))
        for cp in copies:
            cp.start()
        slots[me] = in_ref[...]
        for k in range(7):
            pltpu.make_async_remote_copy(
                src_ref=in_ref, dst_ref=slots.at[peers[k]], send_sem=send_sems.at[k], recv_sem=recv_sems.at[k],
                device_id=(x, y, c), device_id_type=MESH).wait_recv()
        for cp in copies:
            cp.wait_send()
        acc = slots[0]
        for i in range(1, 8):
            acc = acc + slots[i]
        out_ref[...] = acc

    vmem = pl.BlockSpec(memory_space=pltpu.VMEM)
    return pl.pallas_call(
        body, name="comm_allreduce_small", in_specs=[vmem], out_specs=vmem, out_shape=_sds((rows, cols), F32),
        scratch_shapes=[pltpu.VMEM((8, rows, cols), F32), pltpu.SemaphoreType.DMA((7,)), pltpu.SemaphoreType.DMA((7,))],
        compiler_params=pltpu.CompilerParams(has_side_effects=True, vmem_limit_bytes=VMEM_LIMIT),
    )(buf)


def _adamw(w, g, m, v, block_rows, name, after=()):
    rows, cols = w.shape

    def body(w_ref, g_ref, m_ref, v_ref, *rest):
        d_ref, nm_ref, nv_ref = rest[len(after):]
        g = g_ref[...]
        m = ADAM_B1 * m_ref[...] + (1.0 - ADAM_B1) * g
        v = ADAM_B2 * v_ref[...] + (1.0 - ADAM_B2) * (g * g)
        m_hat = m / (1.0 - ADAM_B1 ** ADAM_STEP)
        v_hat = v / (1.0 - ADAM_B2 ** ADAM_STEP)
        d_ref[...] = -ADAM_LR * (m_hat / (jnp.sqrt(v_hat) + ADAM_EPS) + ADAM_WD * w_ref[...])
        nm_ref[...] = m
        nv_ref[...] = v

    spec = pl.BlockSpec((block_rows, cols), lambda i: (i, 0))
    return pl.pallas_call(
        body, name=name, grid=(rows // block_rows,), in_specs=[spec] * 4 + [ANY] * len(after), out_specs=[spec] * 3,
        out_shape=[_sds((rows, cols), F32)] * 3,
        compiler_params=_seq_params(),
    )(w, g, m, v, *after)


WEIGHTS = ("pre_mix_norm", "w_in", "sgu_ln_gain", "sgu_ln_bias", "sgu_w_spatial", "sgu_b_spatial", "attn_out_norm",
           "sgu_out_norm", "w_out", "post_mix_norm", "pre_ffn_norm", "w_gate", "w_up", "w_down", "post_ffn_norm")
BIG = ("w_in", "w_out", "w_gate", "w_up", "w_down")
BIG_ADAM_ROWS = {"w_in": 256, "w_out": 128, "w_gate": 352, "w_up": 352, "w_down": 352}
SMALL = ("pre_mix_norm", "post_mix_norm", "pre_ffn_norm", "post_ffn_norm", "sgu_ln_gain", "sgu_ln_bias",
         "attn_out_norm", "sgu_out_norm", "sgu_w_spatial", "sgu_b_spatial")


def _pack_small(d):
    flat = [d[n].reshape(-1) for n in SMALL]
    used = sum(f.shape[0] for f in flat)
    flat.append(jnp.zeros((SMALL_ROWS * 1024 - used,), F32))
    return jnp.concatenate(flat).reshape(SMALL_ROWS, 1024)


def _unpack_small(buf, shapes):
    flat = buf.reshape(-1)
    out, off = {}, 0
    for n in SMALL:
        size = int(np.prod(shapes[n]))
        out[n] = flat[off:off + size].reshape(shapes[n])
        off += size
    return out


def _kernel_unoverlapped(x, positions, pre_mix_norm, w_in, sgu_ln_gain, sgu_ln_bias, sgu_w_spatial, sgu_b_spatial, attn_out_norm, sgu_out_norm, w_out, post_mix_norm, pre_ffn_norm, w_gate, w_up, w_down, post_ffn_norm, loss_target, m_pre_mix_norm, m_w_in, m_sgu_ln_gain, m_sgu_ln_bias, m_sgu_w_spatial, m_sgu_b_spatial, m_attn_out_norm, m_sgu_out_norm, m_w_out, m_post_mix_norm, m_pre_ffn_norm, m_w_gate, m_w_up, m_w_down, m_post_ffn_norm, v_pre_mix_norm, v_w_in, v_sgu_ln_gain, v_sgu_ln_bias, v_sgu_w_spatial, v_sgu_b_spatial, v_attn_out_norm, v_sgu_out_norm, v_w_out, v_post_mix_norm, v_pre_ffn_norm, v_w_gate, v_w_up, v_w_down, v_post_ffn_norm):
    a = dict(locals())
    cx, cy, cc = _coords()
    core = jnp.stack([cc]).astype(jnp.int32)
    shard_core = jnp.stack([2 * cx + cy, cc]).astype(jnp.int32)

    full = _gather_weights([a[n][0].astype(BF16) for n in BIG])
    small = {n: (a[n][0] if a[n].ndim > 2 else a[n]) for n in SMALL}
    loss_cols, grad_x, gws, small_grads = _local_step(
        x[0], positions.reshape(SEQ, 1), loss_target[0], *full, small)
    loss = lax.psum(jnp.sum(loss_cols) * np.float32(0.5 / D_MODEL), ("x", "y", "c"))

    recv_sib = _rs_to_sibling(list(gws))
    chip_part = [_rs_chip_sum(g, r, core) for g, r in zip(gws, recv_sib)]
    recv_chips = _rs_between_chips(chip_part)
    halves = [_rs_final_sum(g, r, rc, shard_core) for g, r, rc in zip(gws, recv_sib, recv_chips)]
    big_grads = dict(zip(BIG, _rs_join_halves(halves)))

    shapes = {n: a[n].shape for n in SMALL}
    small_sum = _allreduce_small(_pack_small(small_grads))

    grads, deltas, new_m, new_v = {}, {}, {}, {}
    for n in BIG:
        grads[n] = big_grads[n][None]
        d, nm, nv = _adamw(a[n][0], big_grads[n], a["m_" + n][0], a["v_" + n][0], BIG_ADAM_ROWS[n], "adamw_" + n)
        deltas[n], new_m[n], new_v[n] = d[None], nm[None], nv[None]
    d, nm, nv = _adamw(_pack_small({n: a[n] for n in SMALL}), small_sum, _pack_small({n: a["m_" + n] for n in SMALL}),
                       _pack_small({n: a["v_" + n] for n in SMALL}), SMALL_ROWS, "adamw_small")
    grads.update(_unpack_small(small_sum, shapes))
    deltas.update(_unpack_small(d, shapes))
    new_m.update(_unpack_small(nm, shapes))
    new_v.update(_unpack_small(nv, shapes))
    return (loss, grad_x[None], *[grads[n] for n in WEIGHTS], *[deltas[n] for n in WEIGHTS],
            *[new_m[n] for n in WEIGHTS], *[new_v[n] for n in WEIGHTS])


def _remote(src, dst, send_sem, recv_sem, to):
    return pltpu.make_async_remote_copy(src_ref=src, dst_ref=dst, send_sem=send_sem, recv_sem=recv_sem,
                                        device_id=to, device_id_type=MESH)


def _halves(a):
    *lead, rows, cols = a.shape
    return a.reshape(*lead, 2, rows // 2, cols)


def _gather_ici(shards):
    n = len(shards)

    def desc(ins, outs, ss, rs, j, w, landed):
        x, y, c = _coords()
        cx, cy = _other_chips(x, y)[j]
        shard = 2 * cx + cy if landed else 2 * x + y
        return _remote(ins[w].at[c], outs[w].at[shard, c], ss.at[j * n + w], rs.at[j * n + w], (cx, cy, c))

    def start(ins, outs, ss, rs):
        for j in range(3):
            for w in range(n):
                desc(ins, outs, ss, rs, j, w, False).start()

    def finish(ins, outs, ss, rs):
        for j in range(3):
            for w in range(n):
                desc(ins, outs, ss, rs, j, w, True).wait_recv()
                desc(ins, outs, ss, rs, j, w, False).wait_send()

    return _Comm(shards, [_sds((N_SHARD,) + s.shape, s.dtype) for s in shards], 3 * n, start, finish)


def _gather_pass(fulls):
    n = len(fulls)

    def desc(bufs, ss, rs, j, w, landed):
        x, y, c = _coords()
        cx, cy = _other_chips(x, y)[j]
        shard = 2 * cx + cy
        return _remote(bufs[w].at[shard, c], bufs[w].at[shard, 1 - c if landed else c],
                       ss.at[j * n + w], rs.at[j * n + w], (x, y, 1 - c))

    def start(ins, outs, ss, rs):
        for j in range(3):
            for w in range(n):
                desc(outs, ss, rs, j, w, False).start()

    def finish(ins, outs, ss, rs):
        for j in range(3):
            for w in range(n):
                desc(outs, ss, rs, j, w, True).wait_recv()
                desc(outs, ss, rs, j, w, False).wait_send()

    return _Comm(fulls, [_sds(f.shape, f.dtype) for f in fulls], 3 * n, start, finish, aliased=True)


def _rs_sibling(gws):
    n = len(gws)

    def desc(ins, outs, ss, rs, w):
        x, y, c = _coords()
        return _remote(ins[w].at[:, 1 - c], outs[w], ss.at[w], rs.at[w], (x, y, 1 - c))

    def start(ins, outs, ss, rs):
        for w in range(n):
            desc(ins, outs, ss, rs, w).start()

    def finish(ins, outs, ss, rs):
        for w in range(n):
            desc(ins, outs, ss, rs, w).wait()

    out_shape = [_sds((N_SHARD, g.shape[1] // 2, g.shape[2]), g.dtype) for g in gws]
    return _Comm([_halves(g) for g in gws], out_shape, n, start, finish)


def _rs_chips(pbs):
    n = len(pbs)

    def desc(ins, outs, ss, rs, j, w):
        x, y, c = _coords()
        cx, cy = _other_chips(x, y)[j]
        return _remote(ins[w].at[2 * cx + cy], outs[w].at[j], ss.at[j * n + w], rs.at[j * n + w], (cx, cy, c))

    def start(ins, outs, ss, rs):
        for j in range(3):
            for w in range(n):
                desc(ins, outs, ss, rs, j, w).start()

    def finish(ins, outs, ss, rs):
        for j in range(3):
            for w in range(n):
                desc(ins, outs, ss, rs, j, w).wait()

    return _Comm(pbs, [_sds((3,) + p.shape[1:], p.dtype) for p in pbs], 3 * n, start, finish)


def _rs_join(halves):
    n = len(halves)

    def desc(bufs, ss, rs, w, landed):
        x, y, c = _coords()
        return _remote(bufs[w].at[c], bufs[w].at[1 - c if landed else c], ss.at[w], rs.at[w], (x, y, 1 - c))

    def start(ins, outs, ss, rs):
        for w in range(n):
            desc(outs, ss, rs, w, False).start()

    def finish(ins, outs, ss, rs):
        for w in range(n):
            desc(outs, ss, rs, w, True).wait_recv()
            desc(outs, ss, rs, w, False).wait_send()

    return _Comm(halves, [_sds(h.shape, h.dtype) for h in halves], n, start, finish, aliased=True)


def _small_exchange(buf):
    def desc(ins, outs, ss, rs, k, landed):
        x, y, c = _coords()
        px = 1 - x if (k >> 2) & 1 else x
        py = 1 - y if (k >> 1) & 1 else y
        pc = 1 - c if k & 1 else c
        slot = 4 * px + 2 * py + pc if landed else 4 * x + 2 * y + c
        return _remote(ins[0], outs[0].at[slot], ss.at[k - 1], rs.at[k - 1], (px, py, pc))

    def start(ins, outs, ss, rs):
        for k in range(1, 8):
            desc(ins, outs, ss, rs, k, False).start()

    def finish(ins, outs, ss, rs):
        for k in range(1, 8):
            desc(ins, outs, ss, rs, k, True).wait_recv()
            desc(ins, outs, ss, rs, k, False).wait_send()

    return _Comm([buf], [_sds((8,) + buf.shape, buf.dtype)], 7, start, finish)


HBM = pl.BlockSpec(memory_space=pltpu.HBM)
SEM = pl.BlockSpec(memory_space=pltpu.SEMAPHORE)
DATAFLOW = pltpu.SideEffectType.DATAFLOW_SIDE_EFFECTING


def _split_start(name, comm, lands, after):
    srcs = [pltpu.with_memory_space_constraint(s, pltpu.HBM) for s in comm.args]
    lands = [pltpu.with_memory_space_constraint(b, pltpu.HBM) for b in lands]
    ns, nb = len(srcs), len(lands)

    def body(*refs):
        send_sems, recv_sems = refs[ns + nb + 1], refs[ns + nb + 2]
        comm.start(refs[:ns], refs[ns:ns + nb], send_sems, recv_sems)
        refs[-1][...] = jnp.zeros_like(refs[-1])

    res = pl.pallas_call(
        body, name=name,
        out_shape=(pltpu.SemaphoreType.DMA((comm.n_sems,)), pltpu.SemaphoreType.DMA((comm.n_sems,)),
                   *[pltpu.HBM(b.shape, b.dtype) for b in srcs + lands], _sds((8, 128), F32)),
        in_specs=[HBM] * (ns + nb) + [ANY],
        out_specs=(SEM, SEM, *[HBM] * (ns + nb), pl.BlockSpec(memory_space=pltpu.VMEM)),
        input_output_aliases={i: 2 + i for i in range(ns + nb)},
        compiler_params=pltpu.CompilerParams(has_side_effects=DATAFLOW),
    )(*srcs, *lands, after)
    return res[0], res[1], list(res[2:2 + ns]), list(res[2 + ns:2 + ns + nb]), res[-1]


def _split_starts(name, comms, after):
    srcs = [[pltpu.with_memory_space_constraint(s, pltpu.HBM) for s in c.args] for c in comms]
    lands = [[pltpu.with_memory_space_constraint(lax.empty(o.shape, o.dtype), pltpu.HBM) for o in c.out_shape]
             for c in comms]
    bufs = [b for k in range(len(comms)) for b in srcs[k] + lands[k]]
    nb, nc = len(bufs), len(comms)

    def body(*refs):
        sems = refs[nb + 1:nb + 1 + 2 * nc]
        off = 0
        for k, c in enumerate(comms):
            ns, nl = len(srcs[k]), len(lands[k])
            c.start(refs[off:off + ns], refs[off + ns:off + ns + nl], sems[2 * k], sems[2 * k + 1])
            off += ns + nl
        refs[-1][...] = jnp.zeros_like(refs[-1])

    res = pl.pallas_call(
        body, name=name,
        out_shape=(*[pltpu.SemaphoreType.DMA((c.n_sems,)) for c in comms for _ in range(2)],
                   *[pltpu.HBM(b.shape, b.dtype) for b in bufs], _sds((8, 128), F32)),
        in_specs=[HBM] * nb + [ANY],
        out_specs=(*[SEM] * (2 * nc), *[HBM] * nb, pl.BlockSpec(memory_space=pltpu.VMEM)),
        input_output_aliases={i: 2 * nc + i for i in range(nb)},
        compiler_params=pltpu.CompilerParams(has_side_effects=DATAFLOW),
    )(*bufs, after)
    states, off = [], 2 * nc
    for k in range(nc):
        ns, nl = len(srcs[k]), len(lands[k])
        states.append((res[2 * k], res[2 * k + 1], list(res[off:off + ns]), list(res[off + ns:off + ns + nl])))
        off += ns + nl
    return states, res[-1]


def _split_wait(name, comm, send_sems, recv_sems, srcs, lands, after):
    ns, nb = len(srcs), len(lands)
    after = list(after) if isinstance(after, (list, tuple)) else [after]

    def body(*refs):
        comm.finish(refs[:ns], refs[ns:ns + nb], refs[ns + nb], refs[ns + nb + 1])

    res = pl.pallas_call(
        body, name=name,
        out_shape=tuple(pltpu.HBM(b.shape, b.dtype) for b in srcs + lands),
        in_specs=[HBM] * (ns + nb) + [SEM, SEM] + [ANY] * len(after), out_specs=tuple([HBM] * (ns + nb)),
        input_output_aliases={i: i for i in range(ns + nb)},
        compiler_params=pltpu.CompilerParams(has_side_effects=DATAFLOW),
    )(*srcs, *lands, send_sems, recv_sems, *after)
    return list(res[ns:])


LOSS_ROW = "loss_cols"
SMALL_EARLY = ("post_mix_norm", "pre_ffn_norm", "post_ffn_norm", "sgu_ln_gain", "sgu_ln_bias", "attn_out_norm",
               "sgu_out_norm", "sgu_w_spatial", "sgu_b_spatial", LOSS_ROW)
SMALL_LATE = ("pre_mix_norm",)


def _pack(d, names, rows):
    flat = [d[n].reshape(-1) for n in names]
    used = sum(f.shape[0] for f in flat)
    flat.append(jnp.zeros((rows * 1024 - used,), F32))
    return jnp.concatenate(flat).reshape(rows, 1024)


def _unpack(buf, names, shapes):
    flat = buf.reshape(-1)
    out, off = {}, 0
    for n in names:
        size = int(np.prod(shapes[n]))
        out[n] = flat[off:off + size].reshape(shapes[n])
        off += size
    return out


def _merge(comms):
    def run(phase):
        def go(ins, outs, ss, rs):
            ii = oi = si = 0
            for c in comms:
                getattr(c, phase)(ins[ii:ii + len(c.args)], outs[oi:oi + len(c.out_shape)],
                                  ss.at[pl.ds(si, c.n_sems)], rs.at[pl.ds(si, c.n_sems)])
                ii += len(c.args)
                oi += len(c.out_shape)
                si += c.n_sems
        return go

    return _Comm([a for c in comms for a in c.args], [o for c in comms for o in c.out_shape],
                 sum(c.n_sems for c in comms), run("start"), run("finish"))


def _chip_sums(gws, recvs, shard_core):
    n = len(gws)
    _, rows, cols = gws[0].shape
    hw = rows // 2

    def body(sc_ref, *refs):
        for k in range(n):
            refs[2 * n + k][...] = (refs[k][...] + refs[n + k][...]).astype(BF16)

    def other(s, sc):
        return jnp.where(s >= sc[0], s + 1, s)

    mine = pl.BlockSpec((1, hw, cols), lambda s, sc: (other(s, sc), sc[1], 0))
    plain = pl.BlockSpec((1, hw, cols), lambda s, sc: (other(s, sc), 0, 0))
    return pl.pallas_call(
        body, name="rs_chip_sums",
        grid_spec=pltpu.PrefetchScalarGridSpec(num_scalar_prefetch=1, grid=(N_SHARD - 1,),
                                               in_specs=[mine] * n + [plain] * n, out_specs=[plain] * n),
        out_shape=[_sds((N_SHARD, hw, cols), BF16)] * n,
        compiler_params=_seq_params(),
    )(shard_core, *gws, *recvs)


def _final_sums(gws, recv_sibs, recv_chips, shard_core):
    n = len(gws)
    _, rows, cols = gws[0].shape
    hw = rows // 2

    def body(sc_ref, *refs):
        for k in range(n):
            acc = refs[k][0] + refs[n + k][0]
            for j in range(3):
                acc = acc + refs[2 * n + k][j].astype(F32)
            refs[3 * n + k][0] = acc

    return pl.pallas_call(
        body, name="rs_final_sums",
        grid_spec=pltpu.PrefetchScalarGridSpec(
            num_scalar_prefetch=1, grid=(1,),
            in_specs=[pl.BlockSpec((1, hw, cols), lambda i, sc: (sc[0], sc[1], 0))] * n
            + [pl.BlockSpec((1, hw, cols), lambda i, sc: (sc[0], 0, 0))] * n
            + [pl.BlockSpec((3, hw, cols), lambda i, sc: (0, 0, 0))] * n,
            out_specs=[pl.BlockSpec((1, hw, cols), lambda i, sc: (sc[1], 0, 0))] * n),
        out_shape=[_sds((2, hw, cols), F32)] * n,
        compiler_params=_seq_params(),
    )(shard_core, *gws, *recv_sibs, *recv_chips)


def _adamw_multi(ws, gs, ms, vs, block_rows, name, after=()):
    n = len(ws)
    rows, cols = ws[0].shape

    def body(*refs):
        outs = refs[4 * n + len(after):]
        for k in range(n):
            g = refs[n + k][...]
            d, m, v = _adam_math(refs[k][...], g, refs[2 * n + k][...], refs[3 * n + k][...])
            outs[4 * k][...], outs[4 * k + 1][...], outs[4 * k + 2][...], outs[4 * k + 3][...] = g, d, m, v

    spec = pl.BlockSpec((block_rows, cols), lambda i: (i, 0))
    res = pl.pallas_call(
        body, name=name, grid=(rows // block_rows,), in_specs=[spec] * (4 * n) + [ANY] * len(after),
        out_specs=[spec] * (4 * n), out_shape=[_sds((rows, cols), F32)] * (4 * n),
        compiler_params=_seq_params(),
    )(*ws, *gs, *ms, *vs, *after)
    return [tuple(res[4 * k:4 * k + 4]) for k in range(n)]


def _wgrad_ff(a_list, b, name, comms=()):
    n = len(a_list)
    cols = 256

    def body(*refs):
        bv = refs[n][...]
        for k in range(n):
            refs[n + 1 + k][...] = _dot_tn(refs[k][...], bv)

    res = _pcall(
        body, name=name, grid=(FF // cols,),
        in_specs=[pl.BlockSpec((SEQ, cols), lambda j: (0, j))] * n
        + [pl.BlockSpec((SEQ, D_MODEL), lambda j: (0, 0), pipeline_mode=pl.Buffered(1))],
        out_specs=[pl.BlockSpec((cols, D_MODEL), lambda j: (j, 0))] * n,
        out_shape=[_sds((FF, D_MODEL), F32)] * n, args=(*a_list, b), comms=comms)
    mine, theirs = res if comms else (res, None)
    mine = [m.reshape(N_SHARD, FF_S, D_MODEL) for m in mine]
    return (mine, theirs) if comms else mine


def _wgrad_pair(a1, a2, b, a_spec, b_spec, out_block, name, comms=()):
    def body(a1_ref, a2_ref, b_ref, o1_ref, o2_ref):
        bv = b_ref[...]
        o1_ref[0] = _dot_tn(a1_ref[0], bv)
        o2_ref[0] = _dot_tn(a2_ref[0], bv)

    out_spec = pl.BlockSpec((1,) + out_block, lambda s: (s, 0, 0))
    return _pcall(
        body, name=name, grid=(N_SHARD,), in_specs=[a_spec, a_spec, b_spec], out_specs=[out_spec, out_spec],
        out_shape=[_sds((N_SHARD,) + out_block, F32)] * 2, args=(a1, a2, b), comms=comms)


def _comm_only(name, comms):
    return _pcall(lambda: None, name=name, grid=(1,), in_specs=[], out_specs=[], out_shape=[], args=(),
                  comms=comms)[1]


def _adam_math(w, g, m, v):
    m = ADAM_B1 * m + (1.0 - ADAM_B1) * g
    v = ADAM_B2 * v + (1.0 - ADAM_B2) * (g * g)
    m_hat = m / (1.0 - ADAM_B1 ** ADAM_STEP)
    v_hat = v / (1.0 - ADAM_B2 ** ADAM_STEP)
    return -ADAM_LR * (m_hat / (jnp.sqrt(v_hat) + ADAM_EPS) + ADAM_WD * w), m, v


def _adamw_small(own, slots, w, m, v, me):
    rows, cols = own.shape

    def body(me_ref, own_ref, slots_ref, w_ref, m_ref, v_ref, g_ref, d_ref, nm_ref, nv_ref):
        own_v = own_ref[...]
        g = jnp.where(me_ref[0] == 0, own_v, slots_ref[0])
        for i in range(1, 8):
            g = g + jnp.where(me_ref[0] == i, own_v, slots_ref[i])
        g_ref[...] = g
        d_ref[...], nm_ref[...], nv_ref[...] = _adam_math(w_ref[...], g, m_ref[...], v_ref[...])

    flat = pl.BlockSpec((rows, cols), lambda i, me_ref: (0, 0))
    return pl.pallas_call(
        body, name="adamw_small",
        grid_spec=pltpu.PrefetchScalarGridSpec(
            num_scalar_prefetch=1, grid=(1,),
            in_specs=[flat, pl.BlockSpec((8, rows, cols), lambda i, me_ref: (0, 0, 0)), flat, flat, flat],
            out_specs=[flat] * 4),
        out_shape=[_sds((rows, cols), F32)] * 4,
        compiler_params=_seq_params(),
    )(me, own, slots, w, m, v)


def kernel(x, positions, pre_mix_norm, w_in, sgu_ln_gain, sgu_ln_bias, sgu_w_spatial, sgu_b_spatial, attn_out_norm, sgu_out_norm, w_out, post_mix_norm, pre_ffn_norm, w_gate, w_up, w_down, post_ffn_norm, loss_target, m_pre_mix_norm, m_w_in, m_sgu_ln_gain, m_sgu_ln_bias, m_sgu_w_spatial, m_sgu_b_spatial, m_attn_out_norm, m_sgu_out_norm, m_w_out, m_post_mix_norm, m_pre_ffn_norm, m_w_gate, m_w_up, m_w_down, m_post_ffn_norm, v_pre_mix_norm, v_w_in, v_sgu_ln_gain, v_sgu_ln_bias, v_sgu_w_spatial, v_sgu_b_spatial, v_attn_out_norm, v_sgu_out_norm, v_w_out, v_post_mix_norm, v_pre_ffn_norm, v_w_gate, v_w_up, v_w_down, v_post_ffn_norm):
    a = dict(locals())
    cx, cy, cc = _coords()
    s_me = 2 * cx + cy
    core = jnp.stack([cc]).astype(jnp.int32)
    shard_core = jnp.stack([s_me, cc]).astype(jnp.int32)
    me = jnp.stack([4 * cx + 2 * cy + cc]).astype(jnp.int32)
    small = {n: (a[n][0] if a[n].ndim > 2 else a[n]) for n in SMALL}
    b_t = small["sgu_b_spatial"].T
    xs, pos, target = x[0], positions.reshape(SEQ, 1), loss_target[0]
    flipped = ("w_gate", "w_up")

    def big(name, n):
        return jnp.swapaxes(a[name], 1, 2)[0] if n in flipped else a[name][0]

    own = {"w_in": _halves(big("w_in", "w_in").astype(BF16))}

    def with_own(full, n):
        full = lax.dynamic_update_slice(full, own[n][None], (s_me, 0, 0, 0))
        return full.reshape((N_SHARD,) + big(n, n).shape)

    ffn = ("w_gate", "w_up", "w_down")
    g_in = _gather_ici([own["w_in"]])
    (s_in,), token = _split_starts("gather_in_start", [g_in], small["pre_mix_norm"])
    for n in ("w_out",) + ffn:
        own[n] = _halves((big(n, n) + token[0:1, 0:1]).astype(BF16))
    g_out, g_ffn = _gather_ici([own["w_out"]]), _gather_ici([own[n] for n in ffn])
    (s_out, s_ffn), token = _split_starts("gather_rest_start", [g_out, g_ffn], token)
    in_lands = _split_wait("gather_in_wait", g_in, *s_in, token)
    ((in_lands,),) = _comm_only("comm_pass_in", [_gather_pass(in_lands)])
    w_in_f = with_own(in_lands, "w_in")
    h, u, vs, *qkv = _inproj_fwd(xs, pos, small["pre_mix_norm"], w_in_f)
    views = [tuple(qkv[3 * i:3 * i + 3]) for i in range(len(DILATIONS))]
    o_list, l_list = [], []
    for dil, (qv, kv, vv) in zip(DILATIONS, views):
        o, l = _attn_fwd(qv, kv, vv, dil)
        o_list.append(o)
        l_list.append(l)
    out_lands = _split_wait("gather_out_wait", g_out, *s_out, l_list[-1])
    sgu, ((out_lands,),) = _sgu_fwd(u, vs, small["sgu_ln_gain"], small["sgu_ln_bias"], small["sgu_w_spatial"], b_t,
                                    comms=[_gather_pass(out_lands)])
    w_out_f = with_own(out_lands, "w_out")
    ffn_lands = _split_wait("gather_ffn_wait", g_ffn, *s_ffn, sgu)
    (attn, mixed, y, x1, *lses), (ffn_lands,) = _mix_out_fwd(
        o_list, l_list, sgu, xs, w_out_f, small["attn_out_norm"], small["sgu_out_norm"], small["post_mix_norm"],
        comms=[_gather_pass(ffn_lands)])
    w_gate_f, w_up_f, w_down_f = (with_own(f, n).reshape(FF, D_MODEL) for f, n in zip(ffn_lands, ffn))
    h2, act, dg, dup, df, dx1, loss_cols, d_pre_ffn, d_post_ffn = _ffn_fwd_bwd(
        x1, target, w_gate_f, w_up_f, w_down_f, small["pre_ffn_norm"], small["post_ffn_norm"])

    full_tok = pl.BlockSpec((SEQ, D_MODEL), lambda s: (0, 0), pipeline_mode=pl.Buffered(1))
    ff_tok = pl.BlockSpec((1, SEQ, FF_S), lambda s: (s, 0, 0))
    gw = {}
    gw["w_gate"], gw["w_up"] = _wgrad_ff([dg, dup], h2, "wgrad_gate_up")
    (gw["w_down"],), ((sib_gate,),) = _wgrad_ff([act], df, "wgrad_down", comms=[_rs_sibling([gw["w_gate"]])])
    (dy, dsgu, d_post_mix, d_attn_norm, d_sgu_norm, *dviews), ((sib_up, sib_down),) = _outproj_bwd(
        dx1, y, attn, sgu, w_out_f, small["post_mix_norm"], small["attn_out_norm"],
        small["sgu_out_norm"], comms=[_rs_sibling([gw["w_up"], gw["w_down"]])])
    sib = {"w_gate": sib_gate, "w_up": sib_up, "w_down": sib_down}
    part = dict(zip(ffn, _chip_sums([gw[n] for n in ffn], [sib[n] for n in ffn], shard_core)))
    gw["w_out"] = _wgrad(mixed, dy, pl.BlockSpec((SEQ, OUT_S), lambda s: (0, s)), full_tok, (OUT_S, D_MODEL),
                         "wgrad_out")
    x_ffn = _rs_chips([part[n] for n in ffn])
    (s_ffn,), token = _split_starts("rs_ffn_start", [x_ffn], small["pre_mix_norm"])
    (du, dvs_sgu, d_w_sp, d_b_sp, d_ln_gain, d_ln_bias), ((sib["w_out"],),) = _sgu_bwd(
        u, vs, dsgu, small["sgu_ln_gain"], small["sgu_ln_bias"], small["sgu_w_spatial"], b_t,
        comms=[_rs_sibling([gw["w_out"]])], after=[token])
    (part["w_out"],) = _chip_sums([gw["w_out"]], [sib["w_out"]], shard_core)
    packed_early = _pack({
        "sgu_ln_gain": d_ln_gain, "sgu_ln_bias": d_ln_bias, "sgu_w_spatial": d_w_sp, "sgu_b_spatial": d_b_sp,
        "attn_out_norm": d_attn_norm, "sgu_out_norm": d_sgu_norm, "post_mix_norm": d_post_mix,
        "pre_ffn_norm": d_pre_ffn, "post_ffn_norm": d_post_ffn, LOSS_ROW: loss_cols}, SMALL_EARLY, SMALL_ROWS)
    x_out, x_small = _rs_chips([part["w_out"]]), _small_exchange(packed_early)
    (s_out, s_small), token = _split_starts("rs_out_small_start", [x_out, x_small], token)

    dqs, dks, dvs = [], [], []
    for i, (dil, (qv, kv, vv)) in enumerate(zip(DILATIONS, views)):
        dq, dk, dv = _attn_bwd(qv, kv, vv, dviews[i], dviews[3 + i], lses[i], dil, after=[token])
        dqs.append(dq)
        dks.append(dk)
        dvs.append(dv)
    half, far, joined = {}, {}, {}
    far.update(zip(ffn, _split_wait("rs_ffn_wait", x_ffn, *s_ffn, dvs[-1])))
    half.update(zip(ffn, _final_sums([gw[n] for n in ffn], [sib[n] for n in ffn], [far[n] for n in ffn],
                                     shard_core)))
    dproj, grad_x, d_pre_mix = _inproj_bwd(dqs, dks, dvs, du, dvs_sgu, pos, xs, dx1, w_in_f, small["pre_mix_norm"])
    (far["w_out"],) = _split_wait("rs_out_wait", x_out, *s_out, grad_x)
    half["w_out"] = _rs_final_sum(gw["w_out"], sib["w_out"], far["w_out"], shard_core)
    packed_late = _pack({"pre_mix_norm": d_pre_mix}, SMALL_LATE, 8)
    names = ffn + ("w_out",)
    gw["w_in"], (got, (slots_late,)) = _wgrad(
        h, dproj, full_tok, pl.BlockSpec((SEQ, IN_S), lambda s: (0, s)), (D_MODEL, IN_S), "wgrad_in",
        comms=[_rs_join([half[n] for n in names]), _small_exchange(packed_late)])
    joined.update(zip(names, got))
    ((sib["w_in"],),) = _comm_only("comm_rs_sibling_in", [_rs_sibling([gw["w_in"]])])
    (part["w_in"],) = _chip_sums([gw["w_in"]], [sib["w_in"]], shard_core)
    x_in = _rs_chips([part["w_in"]])
    (s_in,), token = _split_starts("rs_in_start", [x_in], small["pre_mix_norm"])

    grads, deltas, new_m, new_v = {}, {}, {}, {}

    def record(n, outs):
        grads[n], deltas[n], new_m[n], new_v[n] = (
            jnp.swapaxes(o[None], 1, 2) if n in flipped else o[None] for o in outs)

    def update(names, block_rows, name, after):
        for n, outs in zip(names, _adamw_multi(
                [big(n, n) for n in names], [joined[n].reshape(big(n, n).shape) for n in names],
                [big("m_" + n, n) for n in names], [big("v_" + n, n) for n in names], block_rows, name, after)):
            record(n, outs)

    update(ffn, FF_S // 4, "adamw_ffn", [token])
    update(("w_out",), BIG_ADAM_ROWS["w_out"], "adamw_w_out", [token])
    (slots_early,) = _split_wait("small_early_wait", x_small, *s_small, [new_v[n] for n in ("w_down", "w_out")])
    (far["w_in"],) = _split_wait("rs_in_wait", x_in, *s_in, slots_early)
    half["w_in"] = _rs_final_sum(gw["w_in"], sib["w_in"], far["w_in"], shard_core)
    ((joined["w_in"],),) = _comm_only("comm_rs_join_in", [_rs_join([half["w_in"]])])
    update(("w_in",), BIG_ADAM_ROWS["w_in"], "adamw_w_in", [])
    a[LOSS_ROW] = a["m_" + LOSS_ROW] = a["v_" + LOSS_ROW] = jnp.zeros((1, D_MODEL), F32)
    for names, rows, packed, slots in ((SMALL_EARLY, SMALL_ROWS, packed_early, slots_early),
                                       (SMALL_LATE, 8, packed_late, slots_late)):
        outs = _adamw_small(packed, slots, _pack(a, names, rows), _pack({n: a["m_" + n] for n in names}, names, rows),
                            _pack({n: a["v_" + n] for n in names}, names, rows), me)
        for dst, buf in zip((grads, deltas, new_m, new_v), outs):
            dst.update(_unpack(buf, names, {n: a[n].shape for n in names}))
    loss = jnp.sum(grads[LOSS_ROW]) * np.float32(0.5 / D_MODEL)
    return (loss, grad_x[None], *[grads[n] for n in WEIGHTS], *[deltas[n] for n in WEIGHTS],
            *[new_m[n] for n in WEIGHTS], *[new_v[n] for n in WEIGHTS])
```

```python
import numpy as np
import jax
import jax.numpy as jnp
from jax import lax
from jax.experimental import pallas as pl
from jax.experimental.pallas import tpu as pltpu

F32 = jnp.float32
BF16 = jnp.bfloat16

SEQ = 2048
D_MODEL = 1024
HEAD_DIM = 64
ATTN_W = 512
SGU_W = 512
SGU_GROUPS = 8
CHUNK = 128
DILATIONS = (1, 4, 16)
N_SHARD = 4
IN_S = 640
OUT_S = 256
FF_S = 704
PROJ_W = N_SHARD * IN_S
FF = N_SHARD * FF_S
FF_CHUNKS = ((0, 1024), (1024, 2048), (2048, FF))
RMS_EPS = 1e-6
LN_EPS = 1e-5
ROPE_THETA = 500000.0
ATTN_SCALE = 1.0 / np.sqrt(HEAD_DIM)
NEG = -1e30
TM = 512
TM_FFN = 256
VMEM_LIMIT = 56 * 1024 * 1024
SMALL_ROWS = 136

ADAM_LR = 0.001
ADAM_B1 = 0.9
ADAM_B2 = 0.999
ADAM_EPS = 1e-08
ADAM_WD = 0.01
ADAM_STEP = 10

MESH = pl.DeviceIdType.MESH
ANY = pl.BlockSpec(memory_space=pl.ANY)


def _dot(a, b):
    return jnp.dot(a, b, preferred_element_type=F32)


def _dot_nt(a, b):
    return lax.dot_general(a, b, (((1,), (1,)), ((), ())), preferred_element_type=F32)


def _dot_tn(a, b):
    return lax.dot_general(a, b, (((0,), (0,)), ((), ())), preferred_element_type=F32)


def _dot_exact(a, b):
    return jnp.dot(a, b, preferred_element_type=F32, precision=lax.Precision.HIGHEST)


def _dot_select(a, sel):
    hi = a.astype(BF16)
    lo = (a - hi.astype(F32)).astype(BF16)
    sel = sel.astype(BF16)
    return _dot(hi, sel) + _dot(lo, sel)


def _rms_stats(x):
    r = lax.rsqrt(jnp.mean(x * x, axis=-1, keepdims=True) + RMS_EPS)
    return x * r, r


def _rms_bwd(xh, r, gain, dy):
    dxh = dy * gain
    dx = r * (dxh - xh * jnp.mean(dxh * xh, axis=-1, keepdims=True))
    return dx, jnp.sum(dy * xh, axis=0, keepdims=True)


_ERF_ALPHA = (-2.72614225801306e-10, 2.77068142495902e-08, -2.10102402082508e-06, -5.69250639462346e-05,
              -7.34990630326855e-04, -2.95459980854025e-03, -1.60960333262415e-02)
_ERF_BETA = (-1.45660718464996e-05, -2.13374055278905e-04, -1.68282697438203e-03, -7.37332916720468e-03,
             -1.42647390514189e-02)


def _erf(x):
    x = jnp.clip(x, -4.0, 4.0)
    x2 = x * x
    p = jnp.full_like(x, _ERF_ALPHA[0])
    for a in _ERF_ALPHA[1:]:
        p = p * x2 + a
    q = jnp.full_like(x, _ERF_BETA[0])
    for b in _ERF_BETA[1:]:
        q = q * x2 + b
    return x * p / q


def _normal_cdf(x):
    return 0.5 * (1.0 + _erf(x * np.float32(1.0 / np.sqrt(2.0))))


def _gelu_grad(x, cdf):
    pdf = jnp.exp(-0.5 * x * x) * np.float32(1.0 / np.sqrt(2.0 * np.pi))
    return cdf + x * pdf


def _sigmoid(x):
    return 1.0 / (1.0 + jnp.exp(-x))


_INV_FREQ = tuple(float(np.float32(ROPE_THETA ** (-2.0 * j / 16.0))) for j in range(8))


def _rot_tables(pos):
    lane = lax.broadcasted_iota(jnp.int32, (1, 128), 1)
    d = lane & 63
    j = d & 7
    inv = jnp.zeros((1, 128), F32)
    for jj in range(8):
        inv = jnp.where(j == jj, _INV_FREQ[jj], inv)
    ang = pos.astype(F32) * inv
    c = jnp.cos(ang)
    s = jnp.sin(ang)
    cos_t = jnp.where(d < 16, c, 1.0)
    sin_a = jnp.where(d < 8, -s, 0.0)
    sin_b = jnp.where((d >= 8) & (d < 16), s, 0.0)
    return tuple(jnp.tile(t, (1, 4)) for t in (cos_t, sin_a, sin_b))


def _rope(x, tabs):
    cos_t, sin_a, sin_b = tabs
    return x * cos_t + pltpu.roll(x, 504, 1) * sin_a + pltpu.roll(x, 8, 1) * sin_b


def _rope_bwd(dy, tabs):
    cos_t, sin_a, sin_b = tabs
    return dy * cos_t + pltpu.roll(dy * sin_a, 8, 1) + pltpu.roll(dy * sin_b, 504, 1)


def _left_half():
    return lax.broadcasted_iota(jnp.int32, (CHUNK, CHUNK), 1) < HEAD_DIM


def _group_ones():
    lane = lax.broadcasted_iota(jnp.int32, (SGU_GROUPS, SGU_W), 1)
    row = lax.broadcasted_iota(jnp.int32, (SGU_GROUPS, SGU_W), 0)
    return ((lane >> 6) == row).astype(F32)


def _masked_spatial(w_ref):
    row = lax.broadcasted_iota(jnp.int32, (CHUNK, CHUNK), 0)
    col = lax.broadcasted_iota(jnp.int32, (CHUNK, CHUNK), 1)
    return [jnp.where(col <= row, w_ref[g], 0.0).astype(BF16) for g in range(SGU_GROUPS)]


def _sgu_core(u, vs, lg, lb, wm, bias_full):
    tm = u.shape[0]
    cdf_u, cdf_vs = _normal_cdf(u), _normal_cdf(vs)
    gu = u * cdf_u
    gv = vs * cdf_vs
    mu = jnp.mean(gv, axis=-1, keepdims=True)
    xc = gv - mu
    rstd = lax.rsqrt(jnp.mean(xc * xc, axis=-1, keepdims=True) + LN_EPS)
    xh = xc * rstd
    vnb = (xh * lg + lb).astype(BF16)
    left = _left_half()
    rows = []
    for c in range(tm // CHUNK):
        pieces = []
        for p in range(4):
            vp = vnb[c * CHUNK:(c + 1) * CHUNK, p * 128:(p + 1) * 128]
            pieces.append(jnp.where(left, _dot(wm[2 * p], vp), _dot(wm[2 * p + 1], vp)))
        rows.append(jnp.concatenate(pieces, axis=1) + bias_full)
    mixed = jnp.concatenate(rows, axis=0)
    return gu, xh, rstd, vnb, mixed, cdf_u, cdf_vs


def _resident(shape):
    n = len(shape)
    return pl.BlockSpec(shape, lambda *_: (0,) * n, pipeline_mode=pl.Buffered(1))


def _rows(ncol, tm=TM):
    return pl.BlockSpec((tm, ncol), lambda i: (i, 0))


def _rows3(nlead, ncol, tm=TM):
    return pl.BlockSpec((nlead, tm, ncol), lambda i: (0, i, 0))


def _acc(ncol, nrow=1):
    return pl.BlockSpec((nrow, ncol), lambda i: (0, 0))


HEAD_W = 128


def _view_rows(dil, width=ATTN_W, tm=TM):
    return pl.BlockSpec((tm // dil, dil * width), lambda i: (i, 0))


def _view_shape(dil, dtype, width=ATTN_W):
    return _sds((SEQ // dil, dil * width), dtype)


def _slab_scratch():
    return pltpu.VMEM((4, TM, 128), F32)


def _store_view(val, out_ref, slabs, dil):
    width = val.shape[1]
    for j in range(width // 128):
        slabs[j] = val[:, j * 128:(j + 1) * 128]
    for r in range(dil):
        for j in range(width // 128):
            c0 = r * width + j * 128
            out_ref[:, c0:c0 + 128] = slabs.at[j][pl.ds(r, TM // dil, stride=dil), :].astype(out_ref.dtype)


def _load_view(in_ref, slabs, dil, width=ATTN_W):
    for r in range(dil):
        for j in range(width // 128):
            c0 = r * width + j * 128
            slabs.at[j][pl.ds(r, TM // dil, stride=dil), :] = in_ref[:, c0:c0 + 128].astype(F32)
    return jnp.concatenate([slabs[j] for j in range(width // 128)], axis=1)


def _head_spread():
    m = lax.broadcasted_iota(jnp.int32, (HEAD_W, ATTN_W), 0)
    lane = lax.broadcasted_iota(jnp.int32, (HEAD_W, ATTN_W), 1)
    return (m == 16 * (lane >> 6)).astype(F32)


def _head_sum():
    lane = lax.broadcasted_iota(jnp.int32, (ATTN_W, HEAD_W), 0)
    m = lax.broadcasted_iota(jnp.int32, (ATTN_W, HEAD_W), 1)
    return ((lane >> 6) == (m >> 4)).astype(F32)


def _seq_params():
    return pltpu.CompilerParams(dimension_semantics=("arbitrary",), vmem_limit_bytes=VMEM_LIMIT)


def _sds(shape, dtype):
    return jax.ShapeDtypeStruct(shape, dtype)


class _Comm:
    def __init__(self, args, out_shape, n_sems, start, finish, aliased=False):
        self.args, self.out_shape, self.n_sems = list(args), list(out_shape), n_sems
        self.start, self.finish, self.aliased = start, finish, aliased


def _pcall(body, *, name, grid, in_specs, out_specs, out_shape, args, scratch_shapes=(), comms=(), after=()):
    single = not isinstance(out_shape, (list, tuple))
    out_specs = [out_specs] if single else list(out_specs)
    out_shape = [out_shape] if single else list(out_shape)
    n_in, n_out, n_scr = len(in_specs), len(out_shape), len(scratch_shapes)
    c_args = [a for c in comms for a in c.args]
    c_outs = [o for c in comms for o in c.out_shape]
    aliases, ai, ao = {}, n_in, n_out
    for c in comms:
        if c.aliased:
            aliases.update({ai + k: ao + k for k in range(len(c.args))})
        ai += len(c.args)
        ao += len(c.out_shape)
    sems = [pltpu.SemaphoreType.DMA((c.n_sems,)) for c in comms for _ in range(2)]
    steps = grid[0]

    def wrapped(*refs):
        o0 = n_in + len(c_args) + len(after)
        s0 = o0 + n_out + len(c_outs)
        m_in, m_out, m_sem = refs[n_in:n_in + len(c_args)], refs[o0 + n_out:s0], refs[s0 + n_scr:]

        def each(phase):
            ii = oi = 0
            for k, c in enumerate(comms):
                getattr(c, phase)(m_in[ii:ii + len(c.args)], m_out[oi:oi + len(c.out_shape)],
                                  m_sem[2 * k], m_sem[2 * k + 1])
                ii += len(c.args)
                oi += len(c.out_shape)

        if comms:
            @pl.when(pl.program_id(0) == 0)
            def _():
                each("start")

        body(*refs[:n_in], *refs[o0:o0 + n_out], *refs[s0:s0 + n_scr])

        if comms:
            @pl.when(pl.program_id(0) == steps - 1)
            def _():
                each("finish")

    res = pl.pallas_call(
        wrapped, name=name, grid=grid,
        in_specs=list(in_specs) + [ANY] * (len(c_args) + len(after)), out_specs=out_specs + [ANY] * len(c_outs),
        out_shape=out_shape + c_outs, scratch_shapes=list(scratch_shapes) + sems,
        input_output_aliases=aliases, compiler_params=_seq_params(),
    )(*args, *c_args, *after)
    mine = res[0] if single else list(res[:n_out])
    if not comms:
        return mine
    theirs, oi = [], n_out
    for c in comms:
        theirs.append(list(res[oi:oi + len(c.out_shape)]))
        oi += len(c.out_shape)
    return mine, theirs


def _in_pieces(g):
    lo, hi = 512 * g, 512 * (g + 1)
    return [(s, max(lo, IN_S * s) - IN_S * s, min(hi, IN_S * (s + 1)) - IN_S * s)
            for s in range(N_SHARD) if max(lo, IN_S * s) < min(hi, IN_S * (s + 1))]


def _inproj_fwd(x, pos, g_pre, w_in, comms=()):
    def body(x_ref, pos_ref, g_ref, w_ref, h_ref, u_ref, vs_ref, *rest):
        qkv_refs, slabs = rest[:9], rest[9:]
        xh, _ = _rms_stats(x_ref[...])
        h = (xh * g_ref[...]).astype(BF16)
        h_ref[...] = h
        tabs = _rot_tables(pos_ref[...])

        def group(g):
            return jnp.concatenate([_dot(h, w_ref[s, :, a:b]) for s, a, b in _in_pieces(g)], axis=1)

        for t in range(3):
            val = group(t)
            if t < 2:
                val = _rope(val, tabs)
            if t == 0:
                val = val * np.float32(ATTN_SCALE)
            qkv_refs[t][...] = val.astype(BF16)
            for i, dil in enumerate(DILATIONS[1:]):
                _store_view(val, qkv_refs[3 * (i + 1) + t], slabs[t], dil)
        u_ref[...] = group(3)
        vs_ref[...] = group(4)

    return _pcall(
        body, name="inproj_fwd", grid=(SEQ // TM,),
        in_specs=[_rows(D_MODEL), _rows(1), _resident((1, D_MODEL)), _resident((N_SHARD, D_MODEL, IN_S))],
        out_specs=[_rows(D_MODEL), _rows(512), _rows(512)] + [_view_rows(dil) for dil in DILATIONS for _ in range(3)],
        out_shape=[_sds((SEQ, D_MODEL), BF16), _sds((SEQ, 512), F32), _sds((SEQ, 512), F32)]
        + [_view_shape(dil, BF16) for dil in DILATIONS for _ in range(3)],
        scratch_shapes=[_slab_scratch() for _ in range(3)],
        args=(x, pos, g_pre, w_in), comms=comms)


def _sgu_fwd(u, vs, lg, lb, w_sp, b_t, comms=()):
    def body(u_ref, vs_ref, lg_ref, lb_ref, w_ref, bt_ref, out_ref):
        wm = _masked_spatial(w_ref)
        bias_full = _dot_exact(bt_ref[...], _group_ones())
        gu, _, _, _, mixed, _, _ = _sgu_core(u_ref[...], vs_ref[...], lg_ref[...], lb_ref[...], wm, bias_full)
        out_ref[...] = gu * mixed

    return _pcall(
        body, name="sgu_fwd", grid=(SEQ // TM,),
        in_specs=[_rows(SGU_W), _rows(SGU_W), _resident((1, SGU_W)), _resident((1, SGU_W)),
                  _resident((SGU_GROUPS, CHUNK, CHUNK)), _resident((CHUNK, SGU_GROUPS))],
        out_specs=_rows(SGU_W),
        out_shape=_sds((SEQ, SGU_W), F32),
        args=(u, vs, lg, lb, w_sp, b_t), comms=comms)


def _block_masks():
    row = lax.broadcasted_iota(jnp.int32, (CHUNK, CHUNK), 0)
    col = lax.broadcasted_iota(jnp.int32, (CHUNK, CHUNK), 1)
    return col <= row, col >= row


def _attn_fwd(qv, kv, vv, dil, comms=()):
    seg = SEQ // dil
    nblk = seg // CHUNK
    rps = 4 if nblk == 1 else 1

    def body(q_ref, k_ref, v_ref, o_ref, l_ref):
        left = _left_half()
        m_cur, m_prev = _block_masks()
        zero = jnp.zeros((CHUNK, CHUNK), BF16)
        ones = (jnp.where(left, 1.0, 0.0).astype(BF16), jnp.where(left, 0.0, 1.0).astype(BF16))

        sides = tuple(enumerate((left, ~left)))

        def rows(b):
            if isinstance(b, int):
                return b * CHUNK, max(b - 1, 0) * CHUNK
            return pl.multiple_of(b * CHUNK, CHUNK), pl.multiple_of(jnp.maximum(b - 1, 0) * CHUNK, CHUNK)

        def first(rr, b):
            r0, rp = rows(b)
            prev_ok = m_prev & (b > 0)
            tiles, scores = [], []
            for hp in range(4):
                ls = slice(rr * ATTN_W + hp * 128, rr * ATTN_W + (hp + 1) * 128)
                qp = q_ref[pl.ds(r0, CHUNK), ls]
                kc = k_ref[pl.ds(r0, CHUNK), ls]
                kp = k_ref[pl.ds(rp, CHUNK), ls] if nblk > 1 else None
                tiles.append((ls, v_ref[pl.ds(r0, CHUNK), ls], v_ref[pl.ds(rp, CHUNK), ls] if nblk > 1 else None))
                for _, hm in sides:
                    qh = jnp.where(hm, qp, zero)
                    sc = jnp.where(m_cur, _dot_nt(qh, kc), NEG)
                    sp = jnp.where(prev_ok, _dot_nt(qh, kp), NEG) if nblk > 1 else None
                    scores.append((sc, sp))
            return tiles, scores

        def second(scores):
            probs = []
            for sc, sp in scores:
                if nblk > 1:
                    m = jnp.max(jnp.maximum(sc, sp), axis=-1, keepdims=True)
                    pc = jnp.exp(sc - m)
                    pp = jnp.exp(sp - m)
                    probs.append((m, pc.astype(BF16), pp.astype(BF16), (pc + pp).astype(BF16)))
                else:
                    m = jnp.max(sc, axis=-1, keepdims=True)
                    pc = jnp.exp(sc - m).astype(BF16)
                    probs.append((m, pc, None, pc))
            return probs

        def third(rr, b, tiles, probs):
            r0, _ = rows(b)
            for hp, (ls, vc, vp) in enumerate(tiles):
                acc = jnp.zeros((CHUNK, CHUNK), F32)
                den = jnp.zeros((CHUNK, CHUNK), F32)
                for side, hm in sides:
                    _, pc, pp, psum = probs[2 * hp + side]
                    acc = acc + _dot(pc, jnp.where(hm, vc, zero))
                    if nblk > 1:
                        acc = acc + _dot(pp, jnp.where(hm, vp, zero))
                    den = den + _dot(psum, ones[side])
                o_ref[pl.ds(r0, CHUNK), ls] = (acc / den).astype(o_ref.dtype)
                lse = jnp.where(left, probs[2 * hp][0], probs[2 * hp + 1][0]) + jnp.log(den)
                l_ref[pl.ds(r0, CHUNK), rr * HEAD_W + 32 * hp:rr * HEAD_W + 32 * hp + 32] = lse[:, 48:80]

        def run(units):
            data = [first(rr, b) for rr, b in units]
            probs = [second(scores) for _, scores in data]
            for (rr, b), (tiles, _), pr in zip(units, data, probs):
                third(rr, b, tiles, pr)

        if nblk == 1:
            run([(rr, 0) for rr in range(rps)])
        else:
            def one(b, carry):
                run([(0, b)])
                return carry

            lax.fori_loop(0, nblk, one, 0)

    spec = pl.BlockSpec((seg, rps * ATTN_W), lambda r: (0, r))
    return _pcall(
        body, name=f"attn_fwd_d{dil}", grid=(dil // rps,),
        in_specs=[spec, spec, spec], out_specs=[spec, pl.BlockSpec((seg, rps * HEAD_W), lambda r: (0, r))],
        out_shape=[_sds((seg, dil * ATTN_W), BF16), _sds((seg, dil * HEAD_W), F32)],
        args=(qv, kv, vv), comms=comms)


def _lane_left(nrows):
    return lax.broadcasted_iota(jnp.int32, (nrows, CHUNK), 1) < HEAD_DIM


def _attn_rows(b):
    if isinstance(b, int):
        return b * CHUNK, max(b - 1, 0) * CHUNK
    return pl.multiple_of(b * CHUNK, CHUNK), pl.multiple_of(jnp.maximum(b - 1, 0) * CHUNK, CHUNK)


def _attn_fwd_fused(qv, kv, vv, dil, comms=()):
    seg = SEQ // dil
    nblk = seg // CHUNK
    rps = 4 if nblk == 1 else 1

    def body(q_ref, k_ref, v_ref, o_ref, l_ref):
        left = _left_half()
        m_cur, m_prev = _block_masks()
        sides = tuple(enumerate((left, ~left)))
        zero = jnp.zeros((CHUNK, CHUNK), BF16)

        def first(rr, b, both):
            r0, rp = _attn_rows(b)
            keys = pl.ds(rp, 2 * CHUNK) if both else pl.ds(r0, CHUNK)
            ok = jnp.concatenate([m_prev, m_cur], axis=1) if both else m_cur
            tiles, scores = [], []
            for hp in range(4):
                ls = slice(rr * ATTN_W + hp * 128, rr * ATTN_W + (hp + 1) * 128)
                qp = q_ref[pl.ds(r0, CHUNK), ls]
                k2 = k_ref[keys, ls]
                tiles.append((ls, v_ref[keys, ls]))
                for _, hm in sides:
                    scores.append(jnp.where(ok, _dot_nt(jnp.where(hm, qp, zero), k2), NEG))
            return tiles, scores

        def second(scores):
            probs = []
            for s in scores:
                m = jnp.max(s, axis=-1, keepdims=True)
                probs.append((m, jnp.exp(s - m).astype(BF16)))
            return probs

        def third(rr, b, tiles, probs, both):
            r0, _ = _attn_rows(b)
            nk = 2 * CHUNK if both else CHUNK
            left_k = _lane_left(nk)
            ones = (jnp.where(left_k, 1.0, 0.0).astype(BF16), jnp.where(left_k, 0.0, 1.0).astype(BF16))
            zero_k = jnp.zeros((nk, CHUNK), BF16)
            for hp, (ls, v2) in enumerate(tiles):
                acc = jnp.zeros((CHUNK, CHUNK), F32)
                den = jnp.zeros((CHUNK, CHUNK), F32)
                for side in range(2):
                    p = probs[2 * hp + side][1]
                    acc = acc + _dot(p, jnp.where(left_k if side == 0 else ~left_k, v2, zero_k))
                    den = den + _dot(p, ones[side])
                o_ref[pl.ds(r0, CHUNK), ls] = (acc / den).astype(o_ref.dtype)
                lse = jnp.where(left, probs[2 * hp][0], probs[2 * hp + 1][0]) + jnp.log(den)
                l_ref[pl.ds(r0, CHUNK), rr * HEAD_W + 32 * hp:rr * HEAD_W + 32 * hp + 32] = lse[:, 48:80]

        def run(units, both):
            data = [first(rr, b, both) for rr, b in units]
            probs = [second(scores) for _, scores in data]
            for (rr, b), (tiles, _), pr in zip(units, data, probs):
                third(rr, b, tiles, pr, both)

        run([(rr, 0) for rr in range(rps)], False)
        if nblk > 1:
            def one(b, carry):
                run([(0, b)], True)
                return carry

            lax.fori_loop(1, nblk, one, 0)

    spec = pl.BlockSpec((seg, rps * ATTN_W), lambda r: (0, r))
    return _pcall(
        body, name=f"attn_fwd_d{dil}", grid=(dil // rps,),
        in_specs=[spec, spec, spec], out_specs=[spec, pl.BlockSpec((seg, rps * HEAD_W), lambda r: (0, r))],
        out_shape=[_sds((seg, dil * ATTN_W), BF16), _sds((seg, dil * HEAD_W), F32)],
        args=(qv, kv, vv), comms=comms)


def _mix_out_fwd(o_list, l_list, sgu, x, w_out, g_attn, g_sgu, g_post, comms=()):
    def body(o1, o2, o3, l1, l2, l3, sgu_ref, x_ref, w_ref, ga_ref, gs_ref, gp_ref,
             attn_ref, mixed_ref, y_ref, x1_ref, lse1_ref, lse2_ref, lse3_ref, slabs_a, slabs_b):
        os = [o1[...], _load_view(o2, slabs_a, DILATIONS[1]), _load_view(o3, slabs_b, DILATIONS[2])]
        ls = [l1[...], _load_view(l2, slabs_a, DILATIONS[1], HEAD_W), _load_view(l3, slabs_b, DILATIONS[2], HEAD_W)]
        m = jnp.maximum(jnp.maximum(ls[0], ls[1]), ls[2])
        es = [jnp.exp(l - m) for l in ls]
        den = es[0] + es[1] + es[2]
        spread = _head_spread()
        attn = sum(_dot_select(e / den, spread) * o for e, o in zip(es, os))
        attn_ref[...] = attn
        lse = m + jnp.log(den)
        lse1_ref[...] = lse
        _store_view(lse, lse2_ref, slabs_a, DILATIONS[1])
        _store_view(lse, lse3_ref, slabs_b, DILATIONS[2])
        ah, _ = _rms_stats(attn)
        sh, _ = _rms_stats(sgu_ref[...])
        mixed = jnp.concatenate([ah * ga_ref[...], sh * gs_ref[...]], axis=1).astype(BF16)
        mixed_ref[...] = mixed
        y = _dot(mixed[:, 0:OUT_S], w_ref[0])
        for s in range(1, N_SHARD):
            y = y + _dot(mixed[:, s * OUT_S:(s + 1) * OUT_S], w_ref[s])
        y_ref[...] = y
        yh, _ = _rms_stats(y)
        x1_ref[...] = x_ref[...] + yh * gp_ref[...]

    return _pcall(
        body, name="mix_out_fwd", grid=(SEQ // TM,),
        in_specs=[_view_rows(dil) for dil in DILATIONS] + [_view_rows(dil, HEAD_W) for dil in DILATIONS]
        + [_rows(512), _rows(D_MODEL), _resident((N_SHARD, OUT_S, D_MODEL)),
           _resident((1, 512)), _resident((1, 512)), _resident((1, D_MODEL))],
        out_specs=[_rows(512), _rows(D_MODEL), _rows(D_MODEL), _rows(D_MODEL)]
        + [_view_rows(dil, HEAD_W) for dil in DILATIONS],
        out_shape=[_sds((SEQ, 512), F32), _sds((SEQ, D_MODEL), BF16), _sds((SEQ, D_MODEL), F32),
                   _sds((SEQ, D_MODEL), F32)] + [_view_shape(dil, F32, HEAD_W) for dil in DILATIONS],
        scratch_shapes=[_slab_scratch(), _slab_scratch()],
        args=(*o_list, *l_list, sgu, x, w_out, g_attn, g_sgu, g_post), comms=comms)


def _ffn_fwd_bwd(x1, target, w_gate, w_up, w_down, g_pre, g_post, comms=()):
    def body(x1_ref, t_ref, wg_ref, wu_ref, wd_ref, gpf_ref, gpo_ref,
             h2_ref, a_ref, dg_ref, dup_ref, df_ref, dx1_ref, loss_ref, dgpf_ref, dgpo_ref, g_scr, up_scr):
        @pl.when(pl.program_id(0) == 0)
        def _():
            loss_ref[...] = jnp.zeros_like(loss_ref)
            dgpf_ref[...] = jnp.zeros_like(dgpf_ref)
            dgpo_ref[...] = jnp.zeros_like(dgpo_ref)

        x1 = x1_ref[...]
        gpf = gpf_ref[...]
        gpo = gpo_ref[...]
        xh, r = _rms_stats(x1)
        h2 = (xh * gpf).astype(BF16)
        h2_ref[...] = h2
        f = jnp.zeros((TM_FFN, D_MODEL), F32)
        for c0, c1 in FF_CHUNKS:
            g = _dot_nt(h2, wg_ref[c0:c1, :])
            up = _dot_nt(h2, wu_ref[c0:c1, :])
            g_scr[:, c0:c1] = g
            up_scr[:, c0:c1] = up
            a = (g * _sigmoid(g) * up).astype(BF16)
            a_ref[:, c0:c1] = a
            f = f + _dot(a, wd_ref[c0:c1, :])
        fh, rf = _rms_stats(f)
        diff = x1 + fh * gpo - t_ref[...]
        loss_ref[...] += jnp.sum(diff * diff, axis=0, keepdims=True)
        dout = diff * np.float32(1.0 / D_MODEL)
        df, dgpo = _rms_bwd(fh, rf, gpo, dout)
        dgpo_ref[...] += dgpo
        dfb = df.astype(BF16)
        df_ref[...] = dfb
        dh2 = jnp.zeros((TM_FFN, D_MODEL), F32)
        for c0, c1 in FF_CHUNKS:
            da = _dot_nt(dfb, wd_ref[c0:c1, :])
            g = g_scr[:, c0:c1]
            up = up_scr[:, c0:c1]
            sg = _sigmoid(g)
            dup = (da * (g * sg)).astype(BF16)
            dg = (da * up * (sg * (1.0 + g * (1.0 - sg)))).astype(BF16)
            dg_ref[:, c0:c1] = dg
            dup_ref[:, c0:c1] = dup
            dh2 = dh2 + _dot(dg, wg_ref[c0:c1, :]) + _dot(dup, wu_ref[c0:c1, :])
        dx, dgpf = _rms_bwd(xh, r, gpf, dh2)
        dgpf_ref[...] += dgpf
        dx1_ref[...] = dout + dx

    return _pcall(
        body, name="ffn_fwd_bwd", grid=(SEQ // TM_FFN,),
        in_specs=[_rows(D_MODEL, TM_FFN), _rows(D_MODEL, TM_FFN), _resident((FF, D_MODEL)),
                  _resident((FF, D_MODEL)), _resident((FF, D_MODEL)),
                  _resident((1, D_MODEL)), _resident((1, D_MODEL))],
        out_specs=[_rows(D_MODEL, TM_FFN), _rows(FF, TM_FFN), _rows(FF, TM_FFN), _rows(FF, TM_FFN),
                   _rows(D_MODEL, TM_FFN), _rows(D_MODEL, TM_FFN), _acc(D_MODEL), _acc(D_MODEL), _acc(D_MODEL)],
        out_shape=[_sds((SEQ, D_MODEL), BF16), _sds((SEQ, FF), BF16), _sds((SEQ, FF), BF16),
                   _sds((SEQ, FF), BF16), _sds((SEQ, D_MODEL), BF16), _sds((SEQ, D_MODEL), F32),
                   _sds((1, D_MODEL), F32), _sds((1, D_MODEL), F32), _sds((1, D_MODEL), F32)],
        scratch_shapes=[pltpu.VMEM((TM_FFN, FF), F32), pltpu.VMEM((TM_FFN, FF), F32)],
        args=(x1, target, w_gate, w_up, w_down, g_pre, g_post), comms=comms)


def _wgrad(a, b, a_spec, b_spec, out_block, name, comms=()):
    def body(a_ref, b_ref, o_ref):
        av = a_ref[0] if len(a_ref.shape) == 3 else a_ref[...]
        bv = b_ref[0] if len(b_ref.shape) == 3 else b_ref[...]
        o_ref[0] = _dot_tn(av, bv)

    return _pcall(
        body, name=name, grid=(N_SHARD,),
        in_specs=[a_spec, b_spec],
        out_specs=pl.BlockSpec((1,) + out_block, lambda s: (s, 0, 0)),
        out_shape=_sds((N_SHARD,) + out_block, F32),
        args=(a, b), comms=comms)


def _outproj_bwd(dx1, y, attn, sgu, w_out, g_post, g_attn, g_sgu, comms=()):
    def body(dx1_ref, y_ref, attn_ref, sgu_ref, w_ref, gp_ref, ga_ref, gs_ref,
             dy_ref, dsgu_ref, dgp_ref, dga_ref, dgs_ref, *rest):
        dattn_refs, delta_refs, (slabs_a, slabs_b) = rest[0:3], rest[3:6], rest[6:]

        @pl.when(pl.program_id(0) == 0)
        def _():
            dgp_ref[...] = jnp.zeros_like(dgp_ref)
            dga_ref[...] = jnp.zeros_like(dga_ref)
            dgs_ref[...] = jnp.zeros_like(dgs_ref)

        yh, ry = _rms_stats(y_ref[...])
        dy, dgp = _rms_bwd(yh, ry, gp_ref[...], dx1_ref[...])
        dgp_ref[...] += dgp
        dyb = dy.astype(BF16)
        dy_ref[...] = dyb
        dmixed = jnp.concatenate([_dot_nt(dyb, w_ref[s]) for s in range(N_SHARD)], axis=1)
        attn = attn_ref[...]
        ah, ra = _rms_stats(attn)
        dattn, dga = _rms_bwd(ah, ra, ga_ref[...], dmixed[:, 0:512])
        dga_ref[...] += dga
        sh, rs = _rms_stats(sgu_ref[...])
        dsgu, dgs = _rms_bwd(sh, rs, gs_ref[...], dmixed[:, 512:1024])
        dgs_ref[...] += dgs
        dsgu_ref[...] = dsgu
        delta = _dot_select(dattn * attn, _head_sum())
        dattn_refs[0][...] = dattn.astype(BF16)
        delta_refs[0][...] = delta
        for i, dil in enumerate(DILATIONS[1:]):
            _store_view(dattn, dattn_refs[i + 1], slabs_a, dil)
            _store_view(delta, delta_refs[i + 1], slabs_b, dil)

    return _pcall(
        body, name="outproj_bwd", grid=(SEQ // TM,),
        in_specs=[_rows(D_MODEL), _rows(D_MODEL), _rows(512), _rows(512), _resident((N_SHARD, OUT_S, D_MODEL)),
                  _resident((1, D_MODEL)), _resident((1, 512)), _resident((1, 512))],
        out_specs=[_rows(D_MODEL), _rows(512), _acc(D_MODEL), _acc(512), _acc(512)]
        + [_view_rows(dil) for dil in DILATIONS] + [_view_rows(dil, HEAD_W) for dil in DILATIONS],
        out_shape=[_sds((SEQ, D_MODEL), BF16), _sds((SEQ, 512), F32),
                   _sds((1, D_MODEL), F32), _sds((1, 512), F32), _sds((1, 512), F32)]
        + [_view_shape(dil, BF16) for dil in DILATIONS] + [_view_shape(dil, F32, HEAD_W) for dil in DILATIONS],
        scratch_shapes=[_slab_scratch(), _slab_scratch()],
        args=(dx1, y, attn, sgu, w_out, g_post, g_attn, g_sgu), comms=comms)


def _sgu_bwd(u, vs, dsgu, lg, lb, w_sp, b_t, comms=(), after=()):
    nsteps = SEQ // TM

    def body(u_ref, vs_ref, ds_ref, lg_ref, lb_ref, w_ref, bt_ref,
             du_ref, dvs_ref, dw_ref, db_ref, dlg_ref, dlb_ref, dbias_scr):
        i = pl.program_id(0)

        @pl.when(i == 0)
        def _():
            dw_ref[...] = jnp.zeros_like(dw_ref)
            dlg_ref[...] = jnp.zeros_like(dlg_ref)
            dlb_ref[...] = jnp.zeros_like(dlb_ref)
            dbias_scr[...] = jnp.zeros_like(dbias_scr)

        wm = _masked_spatial(w_ref)
        ones_g = _group_ones()
        bias_full = _dot_exact(bt_ref[...], ones_g)
        u = u_ref[...]
        vs = vs_ref[...]
        lg = lg_ref[...]
        gu, xh, rstd, vnb, mixed, cdf_u, cdf_vs = _sgu_core(u, vs, lg, lb_ref[...], wm, bias_full)
        dsgu = ds_ref[...]
        du_ref[...] = (dsgu * mixed * _gelu_grad(u, cdf_u)).astype(BF16)
        dmixed = dsgu * gu
        left = _left_half()
        dvn_rows = []
        for c in range(TM // CHUNK):
            rs = slice(c * CHUNK, (c + 1) * CHUNK)
            dm_c = dmixed[rs, :]
            dbias_scr[...] += dm_c
            pieces = []
            for p in range(4):
                ls = slice(p * 128, (p + 1) * 128)
                dmp = dm_c[:, ls]
                vp = vnb[rs, ls]
                dmb = dmp.astype(BF16)
                zero = jnp.zeros_like(dmb)
                dw_ref[2 * p] += _dot_nt(jnp.where(left, dmb, zero), vp)
                dw_ref[2 * p + 1] += _dot_nt(jnp.where(left, zero, dmb), vp)
                pieces.append(jnp.where(left, _dot_tn(wm[2 * p], dmb), _dot_tn(wm[2 * p + 1], dmb)))
            dvn_rows.append(jnp.concatenate(pieces, axis=1))
        dvn = jnp.concatenate(dvn_rows, axis=0)
        dlg_ref[...] += jnp.sum(dvn * xh, axis=0, keepdims=True)
        dlb_ref[...] += jnp.sum(dvn, axis=0, keepdims=True)
        dxh = dvn * lg
        dgv = rstd * (dxh - jnp.mean(dxh, axis=-1, keepdims=True) - xh * jnp.mean(dxh * xh, axis=-1, keepdims=True))
        dvs_ref[...] = (dgv * _gelu_grad(vs, cdf_vs)).astype(BF16)

        @pl.when(i == nsteps - 1)
        def _():
            row = lax.broadcasted_iota(jnp.int32, (CHUNK, CHUNK), 0)
            col = lax.broadcasted_iota(jnp.int32, (CHUNK, CHUNK), 1)
            for g in range(SGU_GROUPS):
                dw_ref[g] = jnp.where(col <= row, dw_ref[g], 0.0)
            db_ref[...] = lax.dot_general(ones_g, dbias_scr[...], (((1,), (1,)), ((), ())),
                                          preferred_element_type=F32, precision=lax.Precision.HIGHEST)

    return _pcall(
        body, name="sgu_bwd", grid=(nsteps,),
        in_specs=[_rows(SGU_W), _rows(SGU_W), _rows(SGU_W), _resident((1, SGU_W)), _resident((1, SGU_W)),
                  _resident((SGU_GROUPS, CHUNK, CHUNK)), _resident((CHUNK, SGU_GROUPS))],
        out_specs=[_rows(SGU_W), _rows(SGU_W), pl.BlockSpec((SGU_GROUPS, CHUNK, CHUNK), lambda i: (0, 0, 0)),
                   _acc(CHUNK, SGU_GROUPS), _acc(SGU_W), _acc(SGU_W)],
        out_shape=[_sds((SEQ, SGU_W), BF16), _sds((SEQ, SGU_W), BF16), _sds((SGU_GROUPS, CHUNK, CHUNK), F32),
                   _sds((SGU_GROUPS, CHUNK), F32), _sds((1, SGU_W), F32), _sds((1, SGU_W), F32)],
        scratch_shapes=[pltpu.VMEM((CHUNK, SGU_W), F32)],
        args=(u, vs, dsgu, lg, lb, w_sp, b_t), comms=comms, after=after)


def _attn_bwd_v1(qv, kv, vv, dov, deltav, lsev, dil, comms=(), after=()):
    seg = SEQ // dil
    nblk = seg // CHUNK
    rps = 4 if nblk == 1 else 1

    def body(q_ref, k_ref, v_ref, do_ref, dl_ref, lse_ref, dq_ref, dk_ref, dv_ref, dk_wait, dv_wait):
        left = _left_half()
        m_cur, m_prev = _block_masks()

        sides = tuple(enumerate((left, ~left)))
        zero = jnp.zeros((CHUNK, CHUNK), BF16)

        def rows(b):
            if isinstance(b, int):
                return b * CHUNK, max(b - 1, 0) * CHUNK
            return pl.multiple_of(b * CHUNK, CHUNK), pl.multiple_of(jnp.maximum(b - 1, 0) * CHUNK, CHUNK)

        def first(rr, b):
            r0, rp = rows(b)
            tiles, firsts = [], []
            for hp in range(4):
                ls = slice(rr * ATTN_W + hp * 128, rr * ATTN_W + (hp + 1) * 128)
                qp = q_ref[pl.ds(r0, CHUNK), ls]
                kc = k_ref[pl.ds(r0, CHUNK), ls]
                vc = v_ref[pl.ds(r0, CHUNK), ls]
                dop = do_ref[pl.ds(r0, CHUNK), ls]
                kp = k_ref[pl.ds(rp, CHUNK), ls] if nblk > 1 else None
                vp = v_ref[pl.ds(rp, CHUNK), ls] if nblk > 1 else None
                tiles.append((ls, kc, kp))
                for side, hm in sides:
                    qh = jnp.where(hm, qp, zero)
                    doh = jnp.where(hm, dop, zero)
                    cur = (_dot_nt(kc, qh), _dot_nt(vc, doh))
                    prev = (_dot_nt(kp, qh), _dot_nt(vp, doh)) if nblk > 1 else None
                    firsts.append((qh, doh, cur, prev))
            return tiles, firsts

        def second(rr, b, firsts):
            r0, _ = rows(b)
            own_ok, prev_ok = m_prev, m_cur & (b > 0)
            lanes = slice(rr * HEAD_W, (rr + 1) * HEAD_W)
            lse_t = lse_ref[pl.ds(r0, CHUNK), lanes].T
            dl_t = dl_ref[pl.ds(r0, CHUNK), lanes].T
            seconds = []
            for i, (qh, doh, cur, prev) in enumerate(firsts):
                lse_h = lse_t[16 * i:16 * i + 1, :]
                dl_h = dl_t[16 * i:16 * i + 1, :]
                pc = jnp.exp(jnp.where(own_ok, cur[0] - lse_h, NEG))
                out = [pc.astype(BF16), (pc * (cur[1] - dl_h)).astype(BF16), None, None]
                if nblk > 1:
                    pp = jnp.exp(jnp.where(prev_ok, prev[0] - lse_h, NEG))
                    out[2:] = [pp.astype(BF16), (pp * (prev[1] - dl_h)).astype(BF16)]
                seconds.append(out)
            return seconds

        def third(rr, b, tiles, firsts, seconds):
            r0, rp = rows(b)
            for hp, (ls, kc, kp) in enumerate(tiles):
                dq = jnp.zeros((CHUNK, CHUNK), F32)
                dkc = jnp.zeros((CHUNK, CHUNK), F32)
                dvc = jnp.zeros((CHUNK, CHUNK), F32)
                dkp = jnp.zeros((CHUNK, CHUNK), F32)
                dvp = jnp.zeros((CHUNK, CHUNK), F32)
                for side, hm in sides:
                    qh, doh, _, _ = firsts[2 * hp + side]
                    pcb, dsc, ppb, dsp = seconds[2 * hp + side]
                    dq = dq + _dot_tn(dsc, jnp.where(hm, kc, zero))
                    dkc = dkc + _dot(dsc, qh)
                    dvc = dvc + _dot(pcb, doh)
                    if nblk > 1:
                        dq = dq + _dot_tn(dsp, jnp.where(hm, kp, zero))
                        dkp = dkp + _dot(dsp, qh)
                        dvp = dvp + _dot(ppb, doh)
                dq_ref[pl.ds(r0, CHUNK), ls] = dq.astype(dq_ref.dtype)
                if nblk == 1:
                    dk_ref[pl.ds(r0, CHUNK), ls] = dkc.astype(dk_ref.dtype)
                    dv_ref[pl.ds(r0, CHUNK), ls] = dvc.astype(dv_ref.dtype)
                else:
                    @pl.when(b > 0)
                    def _():
                        dk_ref[pl.ds(rp, CHUNK), ls] = (dk_wait[:, ls] + dkp).astype(dk_ref.dtype)
                        dv_ref[pl.ds(rp, CHUNK), ls] = (dv_wait[:, ls] + dvp).astype(dv_ref.dtype)

                    dk_wait[:, ls] = dkc
                    dv_wait[:, ls] = dvc

        def run(units):
            data = [first(rr, b) for rr, b in units]
            probs = [second(rr, b, firsts) for (rr, b), (_, firsts) in zip(units, data)]
            for (rr, b), (tiles, firsts), seconds in zip(units, data, probs):
                third(rr, b, tiles, firsts, seconds)

        if nblk == 1:
            run([(rr, 0) for rr in range(rps)])
        else:
            def one(b, carry):
                run([(0, b)])
                return carry

            lax.fori_loop(0, nblk, one, 0)
            last = (nblk - 1) * CHUNK
            dk_ref[last:last + CHUNK, :] = dk_wait[...].astype(dk_ref.dtype)
            dv_ref[last:last + CHUNK, :] = dv_wait[...].astype(dv_ref.dtype)

    spec = pl.BlockSpec((seg, rps * ATTN_W), lambda r: (0, r))
    return _pcall(
        body, name=f"attn_bwd_d{dil}", grid=(dil // rps,),
        in_specs=[spec] * 4 + [pl.BlockSpec((seg, rps * HEAD_W), lambda r: (0, r))] * 2, out_specs=[spec] * 3,
        out_shape=[_sds((seg, dil * ATTN_W), BF16)] * 3,
        scratch_shapes=[pltpu.VMEM((CHUNK, ATTN_W), F32), pltpu.VMEM((CHUNK, ATTN_W), F32)],
        args=(qv, kv, vv, dov, deltav, lsev), comms=comms, after=after)


def _attn_bwd(qv, kv, vv, dov, deltav, lsev, dil, comms=(), after=()):
    seg = SEQ // dil
    nblk = seg // CHUNK
    rps = 4 if nblk == 1 else 1

    def body(q_ref, k_ref, v_ref, do_ref, dl_ref, lse_ref, dq_ref, dk_ref, dv_ref, dk_wait, dv_wait):
        left = _left_half()
        m_cur, m_prev = _block_masks()
        sides = tuple(enumerate((left, ~left)))
        zero = jnp.zeros((CHUNK, CHUNK), BF16)

        def first(rr, b, both):
            r0, rp = _attn_rows(b)
            keys = pl.ds(rp, 2 * CHUNK) if both else pl.ds(r0, CHUNK)
            tiles, heads = [], []
            for hp in range(4):
                ls = slice(rr * ATTN_W + hp * 128, rr * ATTN_W + (hp + 1) * 128)
                qp = q_ref[pl.ds(r0, CHUNK), ls]
                dop = do_ref[pl.ds(r0, CHUNK), ls]
                k2 = k_ref[keys, ls]
                v2 = v_ref[keys, ls]
                tiles.append((ls, k2))
                for _, hm in sides:
                    qh = jnp.where(hm, qp, zero)
                    doh = jnp.where(hm, dop, zero)
                    heads.append((qh, doh, _dot_nt(k2, qh), _dot_nt(v2, doh)))
            return tiles, heads

        def second(rr, b, heads, both):
            r0, _ = _attn_rows(b)
            ok = jnp.concatenate([m_cur, m_prev], axis=0) if both else m_prev
            lanes = slice(rr * HEAD_W, (rr + 1) * HEAD_W)
            lse_t = lse_ref[pl.ds(r0, CHUNK), lanes].T
            dl_t = dl_ref[pl.ds(r0, CHUNK), lanes].T
            out = []
            for i, (_, _, s_t, dp_t) in enumerate(heads):
                p = jnp.exp(jnp.where(ok, s_t - lse_t[16 * i:16 * i + 1, :], NEG))
                out.append((p.astype(BF16), (p * (dp_t - dl_t[16 * i:16 * i + 1, :])).astype(BF16)))
            return out

        def third(rr, b, tiles, heads, probs, both):
            r0, rp = _attn_rows(b)
            nk = 2 * CHUNK if both else CHUNK
            left_k = _lane_left(nk)
            zero_k = jnp.zeros((nk, CHUNK), BF16)
            for hp, (ls, k2) in enumerate(tiles):
                dq = jnp.zeros((CHUNK, CHUNK), F32)
                dk2 = jnp.zeros((nk, CHUNK), F32)
                dv2 = jnp.zeros((nk, CHUNK), F32)
                for side in range(2):
                    qh, doh, _, _ = heads[2 * hp + side]
                    p, ds = probs[2 * hp + side]
                    dq = dq + _dot_tn(ds, jnp.where(left_k if side == 0 else ~left_k, k2, zero_k))
                    dk2 = dk2 + _dot(ds, qh)
                    dv2 = dv2 + _dot(p, doh)
                dq_ref[pl.ds(r0, CHUNK), ls] = dq.astype(dq_ref.dtype)
                if nblk == 1:
                    dk_ref[pl.ds(r0, CHUNK), ls] = dk2.astype(dk_ref.dtype)
                    dv_ref[pl.ds(r0, CHUNK), ls] = dv2.astype(dv_ref.dtype)
                elif both:
                    dk_ref[pl.ds(rp, CHUNK), ls] = (dk_wait[:, ls] + dk2[0:CHUNK]).astype(dk_ref.dtype)
                    dv_ref[pl.ds(rp, CHUNK), ls] = (dv_wait[:, ls] + dv2[0:CHUNK]).astype(dv_ref.dtype)
                    dk_wait[:, ls] = dk2[CHUNK:]
                    dv_wait[:, ls] = dv2[CHUNK:]
                else:
                    dk_wait[:, ls] = dk2
                    dv_wait[:, ls] = dv2

        def run(units, both):
            data = [first(rr, b, both) for rr, b in units]
            probs = [second(rr, b, heads, both) for (rr, b), (_, heads) in zip(units, data)]
            for (rr, b), (tiles, heads), pr in zip(units, data, probs):
                third(rr, b, tiles, heads, pr, both)

        run([(rr, 0) for rr in range(rps)], False)
        if nblk > 1:
            def one(b, carry):
                run([(0, b)], True)
                return carry

            lax.fori_loop(1, nblk, one, 0)
            last = (nblk - 1) * CHUNK
            dk_ref[last:last + CHUNK, :] = dk_wait[...].astype(dk_ref.dtype)
            dv_ref[last:last + CHUNK, :] = dv_wait[...].astype(dv_ref.dtype)

    spec = pl.BlockSpec((seg, rps * ATTN_W), lambda r: (0, r))
    return _pcall(
        body, name=f"attn_bwd_d{dil}", grid=(dil // rps,),
        in_specs=[spec] * 4 + [pl.BlockSpec((seg, rps * HEAD_W), lambda r: (0, r))] * 2, out_specs=[spec] * 3,
        out_shape=[_sds((seg, dil * ATTN_W), BF16)] * 3,
        scratch_shapes=[pltpu.VMEM((CHUNK, ATTN_W), F32), pltpu.VMEM((CHUNK, ATTN_W), F32)],
        args=(qv, kv, vv, dov, deltav, lsev), comms=comms, after=after)


def _inproj_bwd(dqs, dks, dvs, du, dvs_sgu, pos, x, dx1, w_in, g_pre, comms=()):
    def body(dq1, dq2, dq3, dk1, dk2, dk3, dv1, dv2, dv3, du_ref, dvs_ref, pos_ref, x_ref, dx1_ref, w_ref, g_ref,
             dproj_ref, gx_ref, dg_ref, slabs_a, slabs_b):
        @pl.when(pl.program_id(0) == 0)
        def _():
            dg_ref[...] = jnp.zeros_like(dg_ref)

        def total(r1, r2, r3):
            return r1[...] + _load_view(r2, slabs_a, DILATIONS[1]) + _load_view(r3, slabs_b, DILATIONS[2])

        tabs = _rot_tables(pos_ref[...])
        groups = {3: du_ref[...], 4: dvs_ref[...]}
        dh = jnp.zeros((TM, D_MODEL), F32)
        for g in (3, 4, 0, 1, 2):
            if g == 0:
                groups[g] = _rope_bwd(total(dq1, dq2, dq3) * np.float32(ATTN_SCALE), tabs).astype(BF16)
            elif g == 1:
                groups[g] = _rope_bwd(total(dk1, dk2, dk3), tabs).astype(BF16)
            elif g == 2:
                groups[g] = total(dv1, dv2, dv3).astype(BF16)
            dproj_ref[:, 512 * g:512 * (g + 1)] = groups[g]
            off = 0
            for s, a, b in _in_pieces(g):
                dh = dh + _dot_nt(groups[g][:, off:off + b - a], w_ref[s, :, a:b])
                off += b - a
        g = g_ref[...]
        xh, r = _rms_stats(x_ref[...])
        dx, dg = _rms_bwd(xh, r, g, dh)
        dg_ref[...] += dg
        gx_ref[...] = dx1_ref[...] + dx

    return _pcall(
        body, name="inproj_bwd", grid=(SEQ // TM,),
        in_specs=[_view_rows(dil) for dil in DILATIONS] * 3
        + [_rows(512), _rows(512), _rows(1), _rows(D_MODEL), _rows(D_MODEL),
           _resident((N_SHARD, D_MODEL, IN_S)), _resident((1, D_MODEL))],
        out_specs=[_rows(PROJ_W), _rows(D_MODEL), _acc(D_MODEL)],
        out_shape=[_sds((SEQ, PROJ_W), BF16), _sds((SEQ, D_MODEL), F32), _sds((1, D_MODEL), F32)],
        scratch_shapes=[_slab_scratch(), _slab_scratch()],
        args=(*dqs, *dks, *dvs, du, dvs_sgu, pos, x, dx1, w_in, g_pre), comms=comms)


def _to_view(a, dil):
    return a if dil == 1 else a.reshape(SEQ // dil, dil * a.shape[1])


def _from_view(a, dil):
    return a if dil == 1 else a.reshape(SEQ, a.shape[1] // dil)


def _local_step(x, pos, target, w_in, w_out, w_gate, w_up, w_down, small):
    b_t = small["sgu_b_spatial"].T
    h, u, vs, *qkv = _inproj_fwd(x, pos, small["pre_mix_norm"], w_in)
    sgu = _sgu_fwd(u, vs, small["sgu_ln_gain"], small["sgu_ln_bias"], small["sgu_w_spatial"], b_t)
    views = [tuple(qkv[3 * i:3 * i + 3]) for i in range(len(DILATIONS))]
    o_list, l_list = [], []
    for dil, (qv, kv, vv) in zip(DILATIONS, views):
        o, l = _attn_fwd(qv, kv, vv, dil)
        o_list.append(o)
        l_list.append(l)
    attn, mixed, y, x1, *lses = _mix_out_fwd(o_list, l_list, sgu, x, w_out, small["attn_out_norm"],
                                             small["sgu_out_norm"], small["post_mix_norm"])
    h2, a, dg, dup, df, dx1, loss_cols, d_pre_ffn, d_post_ffn = _ffn_fwd_bwd(
        x1, target, w_gate, w_up, w_down, small["pre_ffn_norm"], small["post_ffn_norm"])

    full_tok = pl.BlockSpec((SEQ, D_MODEL), lambda s: (0, 0), pipeline_mode=pl.Buffered(1))
    ff_tok = pl.BlockSpec((1, SEQ, FF_S), lambda s: (s, 0, 0))
    gw_gate, gw_up = _wgrad_ff([dg, dup], h2, "wgrad_gate_up")
    (gw_down,) = _wgrad_ff([a], df, "wgrad_down")

    dy, dsgu, d_post_mix, d_attn_norm, d_sgu_norm, *dviews = _outproj_bwd(
        dx1, y, attn, sgu, w_out, small["post_mix_norm"], small["attn_out_norm"], small["sgu_out_norm"])
    gw_out = _wgrad(mixed, dy, pl.BlockSpec((SEQ, OUT_S), lambda s: (0, s)), full_tok, (OUT_S, D_MODEL), "wgrad_out")
    du, dvs_sgu, d_w_sp, d_b_sp, d_ln_gain, d_ln_bias = _sgu_bwd(
        u, vs, dsgu, small["sgu_ln_gain"], small["sgu_ln_bias"], small["sgu_w_spatial"], b_t)

    dqs, dks, dvs = [], [], []
    for i, (dil, (qv, kv, vv)) in enumerate(zip(DILATIONS, views)):
        dq, dk, dv = _attn_bwd(qv, kv, vv, dviews[i], dviews[3 + i], lses[i], dil)
        dqs.append(dq)
        dks.append(dk)
        dvs.append(dv)
    dproj, grad_x, d_pre_mix = _inproj_bwd(dqs, dks, dvs, du, dvs_sgu, pos, x, dx1, w_in, small["pre_mix_norm"])
    gw_in = _wgrad(h, dproj, full_tok, pl.BlockSpec((SEQ, IN_S), lambda s: (0, s)), (D_MODEL, IN_S), "wgrad_in")

    small_grads = {
        "pre_mix_norm": d_pre_mix, "sgu_ln_gain": d_ln_gain, "sgu_ln_bias": d_ln_bias, "sgu_w_spatial": d_w_sp,
        "sgu_b_spatial": d_b_sp, "attn_out_norm": d_attn_norm, "sgu_out_norm": d_sgu_norm,
        "post_mix_norm": d_post_mix, "pre_ffn_norm": d_pre_ffn, "post_ffn_norm": d_post_ffn,
    }
    return loss_cols, grad_x, (gw_in, gw_out, gw_gate, gw_up, gw_down), small_grads


def _coords():
    return lax.axis_index("x"), lax.axis_index("y"), lax.axis_index("c")


def _other_chips(x, y):
    return [(1 - x, y), (x, 1 - y), (1 - x, 1 - y)]


def _comm_call(body, name, n_in, out_shape, scratch_shapes):
    return pl.pallas_call(
        body, name=name, in_specs=[ANY] * n_in, out_specs=[ANY] * len(out_shape), out_shape=out_shape,
        scratch_shapes=scratch_shapes,
        compiler_params=pltpu.CompilerParams(has_side_effects=True),
    )


def _gather_weights(shards):
    n = len(shards)
    halves = [s.reshape(2, s.shape[0] // 2, s.shape[1]) for s in shards]

    def body(*refs):
        ins, outs = refs[:n], refs[n:2 * n]
        send_sems, recv_sems = refs[2 * n:]
        x, y, c = _coords()
        s_me = 2 * x + y
        chips = _other_chips(x, y)
        sibling = (x, y, 1 - c)

        def copy(k, w, shard, cc, to):
            src = ins[w].at[cc] if shard is None else outs[w].at[shard, cc]
            dst = outs[w].at[s_me if shard is None else shard, cc]
            return pltpu.make_async_remote_copy(src_ref=src, dst_ref=dst, send_sem=send_sems.at[k],
                                                recv_sem=recv_sems.at[k], device_id=to, device_id_type=MESH)

        first = [copy(j * n + w, w, None, c, (cx, cy, c)) for j, (cx, cy) in enumerate(chips) for w in range(n)]
        for cp in first:
            cp.start()
        passed = []
        for j, (cx, cy) in enumerate(chips):
            for w in range(n):
                copy(j * n + w, w, 2 * cx + cy, c, (x, y, c)).wait_recv()
                fw = copy((3 + j) * n + w, w, 2 * cx + cy, c, sibling)
                fw.start()
                passed.append(fw)
        for j, (cx, cy) in enumerate(chips):
            for w in range(n):
                copy((3 + j) * n + w, w, 2 * cx + cy, 1 - c, (x, y, c)).wait_recv()
        for cp in first + passed:
            cp.wait_send()

    out_shape = [_sds((N_SHARD,) + h.shape, h.dtype) for h in halves]
    scratch = [pltpu.SemaphoreType.DMA((6 * n,)), pltpu.SemaphoreType.DMA((6 * n,))]
    full = _comm_call(body, "comm_gather_weights", n, out_shape, scratch)(*halves)
    s_me = 2 * lax.axis_index("x") + lax.axis_index("y")
    full = [lax.dynamic_update_slice(f, h[None], (s_me, 0, 0, 0)) for f, h in zip(full, halves)]
    return [f.reshape((N_SHARD,) + s.shape) for f, s in zip(full, shards)]


def _rs_to_sibling(gws):
    n = len(gws)

    def body(*refs):
        ins, outs = refs[:n], refs[n:2 * n]
        send_sems, recv_sems = refs[2 * n:]
        x, y, c = _coords()
        copies = []
        for w in range(n):
            hw = gws[w].shape[1] // 2
            copies.append(pltpu.make_async_remote_copy(
                src_ref=ins[w].at[:, pl.ds((1 - c) * hw, hw), :], dst_ref=outs[w], send_sem=send_sems.at[w],
                recv_sem=recv_sems.at[w], device_id=(x, y, 1 - c), device_id_type=MESH))
        for cp in copies:
            cp.start()
        for cp in copies:
            cp.wait()

    out_shape = [_sds((N_SHARD, g.shape[1] // 2, g.shape[2]), g.dtype) for g in gws]
    scratch = [pltpu.SemaphoreType.DMA((n,)), pltpu.SemaphoreType.DMA((n,))]
    return _comm_call(body, "comm_rs_sibling", n, out_shape, scratch)(*gws)


def _rs_chip_sum(gw, recv, core):
    _, rows, cols = gw.shape
    hw = rows // 2

    def body(c_ref, g_ref, r_ref, o_ref):
        o_ref[...] = (g_ref[...] + r_ref[...]).astype(BF16)

    return pl.pallas_call(
        body, name="rs_chip_sum",
        grid_spec=pltpu.PrefetchScalarGridSpec(
            num_scalar_prefetch=1, grid=(N_SHARD,),
            in_specs=[pl.BlockSpec((1, hw, cols), lambda s, c_ref: (s, c_ref[0], 0)),
                      pl.BlockSpec((1, hw, cols), lambda s, c_ref: (s, 0, 0))],
            out_specs=pl.BlockSpec((1, hw, cols), lambda s, c_ref: (s, 0, 0))),
        out_shape=_sds((N_SHARD, hw, cols), BF16),
        compiler_params=_seq_params(),
    )(core, gw, recv)


def _rs_between_chips(pbs):
    n = len(pbs)

    def body(*refs):
        ins, outs = refs[:n], refs[n:2 * n]
        send_sems, recv_sems = refs[2 * n:]
        x, y, c = _coords()
        copies = []
        for j, (cx, cy) in enumerate(_other_chips(x, y)):
            for w in range(n):
                copies.append(pltpu.make_async_remote_copy(
                    src_ref=ins[w].at[2 * cx + cy], dst_ref=outs[w].at[j], send_sem=send_sems.at[j * n + w],
                    recv_sem=recv_sems.at[j * n + w], device_id=(cx, cy, c), device_id_type=MESH))
        for cp in copies:
            cp.start()
        for cp in copies:
            cp.wait()

    out_shape = [_sds((3,) + p.shape[1:], p.dtype) for p in pbs]
    scratch = [pltpu.SemaphoreType.DMA((3 * n,)), pltpu.SemaphoreType.DMA((3 * n,))]
    return _comm_call(body, "comm_rs_chips", n, out_shape, scratch)(*pbs)


def _rs_final_sum(gw, recv_sib, recv_chips, shard_core):
    _, rows, cols = gw.shape
    hw = rows // 2

    def body(sc_ref, g_ref, r_ref, rc_ref, o_ref):
        acc = g_ref[0] + r_ref[0]
        for j in range(3):
            acc = acc + rc_ref[j].astype(F32)
        o_ref[0] = acc

    return pl.pallas_call(
        body, name="rs_final_sum",
        grid_spec=pltpu.PrefetchScalarGridSpec(
            num_scalar_prefetch=1, grid=(1,),
            in_specs=[pl.BlockSpec((1, hw, cols), lambda i, sc: (sc[0], sc[1], 0)),
                      pl.BlockSpec((1, hw, cols), lambda i, sc: (sc[0], 0, 0)),
                      pl.BlockSpec((3, hw, cols), lambda i, sc: (0, 0, 0))],
            out_specs=pl.BlockSpec((1, hw, cols), lambda i, sc: (sc[1], 0, 0))),
        out_shape=_sds((2, hw, cols), F32),
        compiler_params=_seq_params(),
    )(shard_core, gw, recv_sib, recv_chips)


def _rs_join_halves(halves):
    n = len(halves)

    def body(*refs):
        bufs = refs[n:2 * n]
        send_sems, recv_sems = refs[2 * n:]
        x, y, c = _coords()
        remote = [pltpu.make_async_remote_copy(
            src_ref=bufs[w].at[c], dst_ref=bufs[w].at[c], send_sem=send_sems.at[w], recv_sem=recv_sems.at[w],
            device_id=(x, y, 1 - c), device_id_type=MESH) for w in range(n)]
        for cp in remote:
            cp.start()
        for w in range(n):
            remote[w].wait_send()
            pltpu.make_async_remote_copy(
                src_ref=bufs[w].at[c], dst_ref=bufs[w].at[1 - c], send_sem=send_sems.at[w],
                recv_sem=recv_sems.at[w], device_id=(x, y, c), device_id_type=MESH).wait_recv()

    joined = pl.pallas_call(
        body, name="comm_rs_join", in_specs=[ANY] * n, out_specs=[ANY] * n,
        out_shape=[_sds(h.shape, h.dtype) for h in halves], input_output_aliases={w: w for w in range(n)},
        scratch_shapes=[pltpu.SemaphoreType.DMA((n,)), pltpu.SemaphoreType.DMA((n,))],
        compiler_params=pltpu.CompilerParams(has_side_effects=True),
    )(*halves)
    return [j.reshape(2 * h.shape[1], h.shape[2]) for j, h in zip(joined, halves)]


def _allreduce_small(buf):
    rows, cols = buf.shape

    def body(in_ref, out_ref, slots, send_sems, recv_sems):
        x, y, c = _coords()
        me = 4 * x + 2 * y + c
        copies, peers = [], []
        for k in range(1, 8):
            px = 1 - x if (k >> 2) & 1 else x
            py = 1 - y if (k >> 1) & 1 else y
            pc = 1 - c if k & 1 else c
            peers.append(4 * px + 2 * py + pc)
            copies.append(pltpu.make_async_remote_copy(
                src_ref=in_ref, dst_ref=slots.at[me], send_sem=send_sems.at[k - 1], recv_sem=recv_sems.at[k - 1],
                device_id=(px, py, pc), device_id_type=MESH))
        for cp in copies:
            cp.start()
        slots[me] = in_ref[...]
        for k in range(7):
            pltpu.make_async_remote_copy(
                src_ref=in_ref, dst_ref=slots.at[peers[k]], send_sem=send_sems.at[k], recv_sem=recv_sems.at[k],
                device_id=(x, y, c), device_id_type=MESH).wait_recv()
        for cp in copies:
            cp.wait_send()
        acc = slots[0]
        for i in range(1, 8):
            acc = acc + slots[i]
        out_ref[...] = acc

    vmem = pl.BlockSpec(memory_space=pltpu.VMEM)
    return pl.pallas_call(
        body, name="comm_allreduce_small", in_specs=[vmem], out_specs=vmem, out_shape=_sds((rows, cols), F32),
        scratch_shapes=[pltpu.VMEM((8, rows, cols), F32), pltpu.SemaphoreType.DMA((7,)), pltpu.SemaphoreType.DMA((7,))],
        compiler_params=pltpu.CompilerParams(has_side_effects=True, vmem_limit_bytes=VMEM_LIMIT),
    )(buf)


def _adamw(w, g, m, v, block_rows, name, after=()):
    rows, cols = w.shape

    def body(w_ref, g_ref, m_ref, v_ref, *rest):
        d_ref, nm_ref, nv_ref = rest[len(after):]
        g = g_ref[...]
        m = ADAM_B1 * m_ref[...] + (1.0 - ADAM_B1) * g
        v = ADAM_B2 * v_ref[...] + (1.0 - ADAM_B2) * (g * g)
        m_hat = m / (1.0 - ADAM_B1 ** ADAM_STEP)
        v_hat = v / (1.0 - ADAM_B2 ** ADAM_STEP)
        d_ref[...] = -ADAM_LR * (m_hat / (jnp.sqrt(v_hat) + ADAM_EPS) + ADAM_WD * w_ref[...])
        nm_ref[...] = m
        nv_ref[...] = v

    spec = pl.BlockSpec((block_rows, cols), lambda i: (i, 0))
    return pl.pallas_call(
        body, name=name, grid=(rows // block_rows,), in_specs=[spec] * 4 + [ANY] * len(after), out_specs=[spec] * 3,
        out_shape=[_sds((rows, cols), F32)] * 3,
        compiler_params=_seq_params(),
    )(w, g, m, v, *after)


WEIGHTS = ("pre_mix_norm", "w_in", "sgu_ln_gain", "sgu_ln_bias", "sgu_w_spatial", "sgu_b_spatial", "attn_out_norm",
           "sgu_out_norm", "w_out", "post_mix_norm", "pre_ffn_norm", "w_gate", "w_up", "w_down", "post_ffn_norm")
BIG = ("w_in", "w_out", "w_gate", "w_up", "w_down")
BIG_ADAM_ROWS = {"w_in": 256, "w_out": 128, "w_gate": 352, "w_up": 352, "w_down": 352}
SMALL = ("pre_mix_norm", "post_mix_norm", "pre_ffn_norm", "post_ffn_norm", "sgu_ln_gain", "sgu_ln_bias",
         "attn_out_norm", "sgu_out_norm", "sgu_w_spatial", "sgu_b_spatial")


def _pack_small(d):
    flat = [d[n].reshape(-1) for n in SMALL]
    used = sum(f.shape[0] for f in flat)
    flat.append(jnp.zeros((SMALL_ROWS * 1024 - used,), F32))
    return jnp.concatenate(flat).reshape(SMALL_ROWS, 1024)


def _unpack_small(buf, shapes):
    flat = buf.reshape(-1)
    out, off = {}, 0
    for n in SMALL:
        size = int(np.prod(shapes[n]))
        out[n] = flat[off:off + size].reshape(shapes[n])
        off += size
    return out


def _kernel_unoverlapped(x, positions, pre_mix_norm, w_in, sgu_ln_gain, sgu_ln_bias, sgu_w_spatial, sgu_b_spatial, attn_out_norm, sgu_out_norm, w_out, post_mix_norm, pre_ffn_norm, w_gate, w_up, w_down, post_ffn_norm, loss_target, m_pre_mix_norm, m_w_in, m_sgu_ln_gain, m_sgu_ln_bias, m_sgu_w_spatial, m_sgu_b_spatial, m_attn_out_norm, m_sgu_out_norm, m_w_out, m_post_mix_norm, m_pre_ffn_norm, m_w_gate, m_w_up, m_w_down, m_post_ffn_norm, v_pre_mix_norm, v_w_in, v_sgu_ln_gain, v_sgu_ln_bias, v_sgu_w_spatial, v_sgu_b_spatial, v_attn_out_norm, v_sgu_out_norm, v_w_out, v_post_mix_norm, v_pre_ffn_norm, v_w_gate, v_w_up, v_w_down, v_post_ffn_norm):
    a = dict(locals())
    cx, cy, cc = _coords()
    core = jnp.stack([cc]).astype(jnp.int32)
    shard_core = jnp.stack([2 * cx + cy, cc]).astype(jnp.int32)

    full = _gather_weights([a[n][0].astype(BF16) for n in BIG])
    small = {n: (a[n][0] if a[n].ndim > 2 else a[n]) for n in SMALL}
    loss_cols, grad_x, gws, small_grads = _local_step(
        x[0], positions.reshape(SEQ, 1), loss_target[0], *full, small)
    loss = lax.psum(jnp.sum(loss_cols) * np.float32(0.5 / D_MODEL), ("x", "y", "c"))

    recv_sib = _rs_to_sibling(list(gws))
    chip_part = [_rs_chip_sum(g, r, core) for g, r in zip(gws, recv_sib)]
    recv_chips = _rs_between_chips(chip_part)
    halves = [_rs_final_sum(g, r, rc, shard_core) for g, r, rc in zip(gws, recv_sib, recv_chips)]
    big_grads = dict(zip(BIG, _rs_join_halves(halves)))

    shapes = {n: a[n].shape for n in SMALL}
    small_sum = _allreduce_small(_pack_small(small_grads))

    grads, deltas, new_m, new_v = {}, {}, {}, {}
    for n in BIG:
        grads[n] = big_grads[n][None]
        d, nm, nv = _adamw(a[n][0], big_grads[n], a["m_" + n][0], a["v_" + n][0], BIG_ADAM_ROWS[n], "adamw_" + n)
        deltas[n], new_m[n], new_v[n] = d[None], nm[None], nv[None]
    d, nm, nv = _adamw(_pack_small({n: a[n] for n in SMALL}), small_sum, _pack_small({n: a["m_" + n] for n in SMALL}),
                       _pack_small({n: a["v_" + n] for n in SMALL}), SMALL_ROWS, "adamw_small")
    grads.update(_unpack_small(small_sum, shapes))
    deltas.update(_unpack_small(d, shapes))
    new_m.update(_unpack_small(nm, shapes))
    new_v.update(_unpack_small(nv, shapes))
    return (loss, grad_x[None], *[grads[n] for n in WEIGHTS], *[deltas[n] for n in WEIGHTS],
            *[new_m[n] for n in WEIGHTS], *[new_v[n] for n in WEIGHTS])


def _remote(src, dst, send_sem, recv_sem, to):
    return pltpu.make_async_remote_copy(src_ref=src, dst_ref=dst, send_sem=send_sem, recv_sem=recv_sem,
                                        device_id=to, device_id_type=MESH)


def _halves(a):
    *lead, rows, cols = a.shape
    return a.reshape(*lead, 2, rows // 2, cols)


def _gather_ici(shards):
    n = len(shards)

    def desc(ins, outs, ss, rs, j, w, landed):
        x, y, c = _coords()
        cx, cy = _other_chips(x, y)[j]
        shard = 2 * cx + cy if landed else 2 * x + y
        return _remote(ins[w].at[c], outs[w].at[shard, c], ss.at[j * n + w], rs.at[j * n + w], (cx, cy, c))

    def start(ins, outs, ss, rs):
        for j in range(3):
            for w in range(n):
                desc(ins, outs, ss, rs, j, w, False).start()

    def finish(ins, outs, ss, rs):
        for j in range(3):
            for w in range(n):
                desc(ins, outs, ss, rs, j, w, True).wait_recv()
                desc(ins, outs, ss, rs, j, w, False).wait_send()

    return _Comm(shards, [_sds((N_SHARD,) + s.shape, s.dtype) for s in shards], 3 * n, start, finish)


def _gather_pass(fulls):
    n = len(fulls)

    def desc(bufs, ss, rs, j, w, landed):
        x, y, c = _coords()
        cx, cy = _other_chips(x, y)[j]
        shard = 2 * cx + cy
        return _remote(bufs[w].at[shard, c], bufs[w].at[shard, 1 - c if landed else c],
                       ss.at[j * n + w], rs.at[j * n + w], (x, y, 1 - c))

    def start(ins, outs, ss, rs):
        for j in range(3):
            for w in range(n):
                desc(outs, ss, rs, j, w, False).start()

    def finish(ins, outs, ss, rs):
        for j in range(3):
            for w in range(n):
                desc(outs, ss, rs, j, w, True).wait_recv()
                desc(outs, ss, rs, j, w, False).wait_send()

    return _Comm(fulls, [_sds(f.shape, f.dtype) for f in fulls], 3 * n, start, finish, aliased=True)


def _rs_sibling(gws):
    n = len(gws)

    def desc(ins, outs, ss, rs, w):
        x, y, c = _coords()
        return _remote(ins[w].at[:, 1 - c], outs[w], ss.at[w], rs.at[w], (x, y, 1 - c))

    def start(ins, outs, ss, rs):
        for w in range(n):
            desc(ins, outs, ss, rs, w).start()

    def finish(ins, outs, ss, rs):
        for w in range(n):
            desc(ins, outs, ss, rs, w).wait()

    out_shape = [_sds((N_SHARD, g.shape[1] // 2, g.shape[2]), g.dtype) for g in gws]
    return _Comm([_halves(g) for g in gws], out_shape, n, start, finish)


def _rs_chips(pbs):
    n = len(pbs)

    def desc(ins, outs, ss, rs, j, w):
        x, y, c = _coords()
        cx, cy = _other_chips(x, y)[j]
        return _remote(ins[w].at[2 * cx + cy], outs[w].at[j], ss.at[j * n + w], rs.at[j * n + w], (cx, cy, c))

    def start(ins, outs, ss, rs):
        for j in range(3):
            for w in range(n):
                desc(ins, outs, ss, rs, j, w).start()

    def finish(ins, outs, ss, rs):
        for j in range(3):
            for w in range(n):
                desc(ins, outs, ss, rs, j, w).wait()

    return _Comm(pbs, [_sds((3,) + p.shape[1:], p.dtype) for p in pbs], 3 * n, start, finish)


def _rs_join(halves):
    n = len(halves)

    def desc(bufs, ss, rs, w, landed):
        x, y, c = _coords()
        return _remote(bufs[w].at[c], bufs[w].at[1 - c if landed else c], ss.at[w], rs.at[w], (x, y, 1 - c))

    def start(ins, outs, ss, rs):
        for w in range(n):
            desc(outs, ss, rs, w, False).start()

    def finish(ins, outs, ss, rs):
        for w in range(n):
            desc(outs, ss, rs, w, True).wait_recv()
            desc(outs, ss, rs, w, False).wait_send()

    return _Comm(halves, [_sds(h.shape, h.dtype) for h in halves], n, start, finish, aliased=True)


def _small_exchange(buf):
    def desc(ins, outs, ss, rs, k, landed):
        x, y, c = _coords()
        px = 1 - x if (k >> 2) & 1 else x
        py = 1 - y if (k >> 1) & 1 else y
        pc = 1 - c if k & 1 else c
        slot = 4 * px + 2 * py + pc if landed else 4 * x + 2 * y + c
        return _remote(ins[0], outs[0].at[slot], ss.at[k - 1], rs.at[k - 1], (px, py, pc))

    def start(ins, outs, ss, rs):
        for k in range(1, 8):
            desc(ins, outs, ss, rs, k, False).start()

    def finish(ins, outs, ss, rs):
        for k in range(1, 8):
            desc(ins, outs, ss, rs, k, True).wait_recv()
            desc(ins, outs, ss, rs, k, False).wait_send()

    return _Comm([buf], [_sds((8,) + buf.shape, buf.dtype)], 7, start, finish)


HBM = pl.BlockSpec(memory_space=pltpu.HBM)
SEM = pl.BlockSpec(memory_space=pltpu.SEMAPHORE)
DATAFLOW = pltpu.SideEffectType.DATAFLOW_SIDE_EFFECTING


def _split_start(name, comm, lands, after):
    srcs = [pltpu.with_memory_space_constraint(s, pltpu.HBM) for s in comm.args]
    lands = [pltpu.with_memory_space_constraint(b, pltpu.HBM) for b in lands]
    ns, nb = len(srcs), len(lands)

    def body(*refs):
        send_sems, recv_sems = refs[ns + nb + 1], refs[ns + nb + 2]
        comm.start(refs[:ns], refs[ns:ns + nb], send_sems, recv_sems)
        refs[-1][...] = jnp.zeros_like(refs[-1])

    res = pl.pallas_call(
        body, name=name,
        out_shape=(pltpu.SemaphoreType.DMA((comm.n_sems,)), pltpu.SemaphoreType.DMA((comm.n_sems,)),
                   *[pltpu.HBM(b.shape, b.dtype) for b in srcs + lands], _sds((8, 128), F32)),
        in_specs=[HBM] * (ns + nb) + [ANY],
        out_specs=(SEM, SEM, *[HBM] * (ns + nb), pl.BlockSpec(memory_space=pltpu.VMEM)),
        input_output_aliases={i: 2 + i for i in range(ns + nb)},
        compiler_params=pltpu.CompilerParams(has_side_effects=DATAFLOW),
    )(*srcs, *lands, after)
    return res[0], res[1], list(res[2:2 + ns]), list(res[2 + ns:2 + ns + nb]), res[-1]


def _split_starts(name, comms, after):
    srcs = [[pltpu.with_memory_space_constraint(s, pltpu.HBM) for s in c.args] for c in comms]
    lands = [[pltpu.with_memory_space_constraint(lax.empty(o.shape, o.dtype), pltpu.HBM) for o in c.out_shape]
             for c in comms]
    bufs = [b for k in range(len(comms)) for b in srcs[k] + lands[k]]
    nb, nc = len(bufs), len(comms)

    def body(*refs):
        sems = refs[nb + 1:nb + 1 + 2 * nc]
        off = 0
        for k, c in enumerate(comms):
            ns, nl = len(srcs[k]), len(lands[k])
            c.start(refs[off:off + ns], refs[off + ns:off + ns + nl], sems[2 * k], sems[2 * k + 1])
            off += ns + nl
        refs[-1][...] = jnp.zeros_like(refs[-1])

    res = pl.pallas_call(
        body, name=name,
        out_shape=(*[pltpu.SemaphoreType.DMA((c.n_sems,)) for c in comms for _ in range(2)],
                   *[pltpu.HBM(b.shape, b.dtype) for b in bufs], _sds((8, 128), F32)),
        in_specs=[HBM] * nb + [ANY],
        out_specs=(*[SEM] * (2 * nc), *[HBM] * nb, pl.BlockSpec(memory_space=pltpu.VMEM)),
        input_output_aliases={i: 2 * nc + i for i in range(nb)},
        compiler_params=pltpu.CompilerParams(has_side_effects=DATAFLOW),
    )(*bufs, after)
    states, off = [], 2 * nc
    for k in range(nc):
        ns, nl = len(srcs[k]), len(lands[k])
        states.append((res[2 * k], res[2 * k + 1], list(res[off:off + ns]), list(res[off + ns:off + ns + nl])))
        off += ns + nl
    return states, res[-1]


def _split_wait(name, comm, send_sems, recv_sems, srcs, lands, after):
    ns, nb = len(srcs), len(lands)
    after = list(after) if isinstance(after, (list, tuple)) else [after]

    def body(*refs):
        comm.finish(refs[:ns], refs[ns:ns + nb], refs[ns + nb], refs[ns + nb + 1])

    res = pl.pallas_call(
        body, name=name,
        out_shape=tuple(pltpu.HBM(b.shape, b.dtype) for b in srcs + lands),
        in_specs=[HBM] * (ns + nb) + [SEM, SEM] + [ANY] * len(after), out_specs=tuple([HBM] * (ns + nb)),
        input_output_aliases={i: i for i in range(ns + nb)},
        compiler_params=pltpu.CompilerParams(has_side_effects=DATAFLOW),
    )(*srcs, *lands, send_sems, recv_sems, *after)
    return list(res[ns:])


LOSS_ROW = "loss_cols"
SMALL_EARLY = ("post_mix_norm", "pre_ffn_norm", "post_ffn_norm", "sgu_ln_gain", "sgu_ln_bias", "attn_out_norm",
               "sgu_out_norm", "sgu_w_spatial", "sgu_b_spatial", LOSS_ROW)
SMALL_LATE = ("pre_mix_norm",)


def _pack(d, names, rows):
    flat = [d[n].reshape(-1) for n in names]
    used = sum(f.shape[0] for f in flat)
    flat.append(jnp.zeros((rows * 1024 - used,), F32))
    return jnp.concatenate(flat).reshape(rows, 1024)


def _unpack(buf, names, shapes):
    flat = buf.reshape(-1)
    out, off = {}, 0
    for n in names:
        size = int(np.prod(shapes[n]))
        out[n] = flat[off:off + size].reshape(shapes[n])
        off += size
    return out


def _merge(comms):
    def run(phase):
        def go(ins, outs, ss, rs):
            ii = oi = si = 0
            for c in comms:
                getattr(c, phase)(ins[ii:ii + len(c.args)], outs[oi:oi + len(c.out_shape)],
                                  ss.at[pl.ds(si, c.n_sems)], rs.at[pl.ds(si, c.n_sems)])
                ii += len(c.args)
                oi += len(c.out_shape)
                si += c.n_sems
        return go

    return _Comm([a for c in comms for a in c.args], [o for c in comms for o in c.out_shape],
                 sum(c.n_sems for c in comms), run("start"), run("finish"))


def _chip_sums(gws, recvs, shard_core):
    n = len(gws)
    _, rows, cols = gws[0].shape
    hw = rows // 2

    def body(sc_ref, *refs):
        for k in range(n):
            refs[2 * n + k][...] = (refs[k][...] + refs[n + k][...]).astype(BF16)

    def other(s, sc):
        return jnp.where(s >= sc[0], s + 1, s)

    mine = pl.BlockSpec((1, hw, cols), lambda s, sc: (other(s, sc), sc[1], 0))
    plain = pl.BlockSpec((1, hw, cols), lambda s, sc: (other(s, sc), 0, 0))
    return pl.pallas_call(
        body, name="rs_chip_sums",
        grid_spec=pltpu.PrefetchScalarGridSpec(num_scalar_prefetch=1, grid=(N_SHARD - 1,),
                                               in_specs=[mine] * n + [plain] * n, out_specs=[plain] * n),
        out_shape=[_sds((N_SHARD, hw, cols), BF16)] * n,
        compiler_params=_seq_params(),
    )(shard_core, *gws, *recvs)


def _final_sums(gws, recv_sibs, recv_chips, shard_core):
    n = len(gws)
    _, rows, cols = gws[0].shape
    hw = rows // 2

    def body(sc_ref, *refs):
        for k in range(n):
            acc = refs[k][0] + refs[n + k][0]
            for j in range(3):
                acc = acc + refs[2 * n + k][j].astype(F32)
            refs[3 * n + k][0] = acc

    return pl.pallas_call(
        body, name="rs_final_sums",
        grid_spec=pltpu.PrefetchScalarGridSpec(
            num_scalar_prefetch=1, grid=(1,),
            in_specs=[pl.BlockSpec((1, hw, cols), lambda i, sc: (sc[0], sc[1], 0))] * n
            + [pl.BlockSpec((1, hw, cols), lambda i, sc: (sc[0], 0, 0))] * n
            + [pl.BlockSpec((3, hw, cols), lambda i, sc: (0, 0, 0))] * n,
            out_specs=[pl.BlockSpec((1, hw, cols), lambda i, sc: (sc[1], 0, 0))] * n),
        out_shape=[_sds((2, hw, cols), F32)] * n,
        compiler_params=_seq_params(),
    )(shard_core, *gws, *recv_sibs, *recv_chips)


def _adamw_multi(ws, gs, ms, vs, block_rows, name, after=()):
    n = len(ws)
    rows, cols = ws[0].shape

    def body(*refs):
        outs = refs[4 * n + len(after):]
        for k in range(n):
            g = refs[n + k][...]
            d, m, v = _adam_math(refs[k][...], g, refs[2 * n + k][...], refs[3 * n + k][...])
            outs[4 * k][...], outs[4 * k + 1][...], outs[4 * k + 2][...], outs[4 * k + 3][...] = g, d, m, v

    spec = pl.BlockSpec((block_rows, cols), lambda i: (i, 0))
    res = pl.pallas_call(
        body, name=name, grid=(rows // block_rows,), in_specs=[spec] * (4 * n) + [ANY] * len(after),
        out_specs=[spec] * (4 * n), out_shape=[_sds((rows, cols), F32)] * (4 * n),
        compiler_params=_seq_params(),
    )(*ws, *gs, *ms, *vs, *after)
    return [tuple(res[4 * k:4 * k + 4]) for k in range(n)]


def _wgrad_ff(a_list, b, name, comms=()):
    n = len(a_list)
    cols = 256

    def body(*refs):
        bv = refs[n][...]
        for k in range(n):
            refs[n + 1 + k][...] = _dot_tn(refs[k][...], bv)

    res = _pcall(
        body, name=name, grid=(FF // cols,),
        in_specs=[pl.BlockSpec((SEQ, cols), lambda j: (0, j))] * n
        + [pl.BlockSpec((SEQ, D_MODEL), lambda j: (0, 0), pipeline_mode=pl.Buffered(1))],
        out_specs=[pl.BlockSpec((cols, D_MODEL), lambda j: (j, 0))] * n,
        out_shape=[_sds((FF, D_MODEL), F32)] * n, args=(*a_list, b), comms=comms)
    mine, theirs = res if comms else (res, None)
    mine = [m.reshape(N_SHARD, FF_S, D_MODEL) for m in mine]
    return (mine, theirs) if comms else mine


def _wgrad_pair(a1, a2, b, a_spec, b_spec, out_block, name, comms=()):
    def body(a1_ref, a2_ref, b_ref, o1_ref, o2_ref):
        bv = b_ref[...]
        o1_ref[0] = _dot_tn(a1_ref[0], bv)
        o2_ref[0] = _dot_tn(a2_ref[0], bv)

    out_spec = pl.BlockSpec((1,) + out_block, lambda s: (s, 0, 0))
    return _pcall(
        body, name=name, grid=(N_SHARD,), in_specs=[a_spec, a_spec, b_spec], out_specs=[out_spec, out_spec],
        out_shape=[_sds((N_SHARD,) + out_block, F32)] * 2, args=(a1, a2, b), comms=comms)


def _comm_only(name, comms):
    return _pcall(lambda: None, name=name, grid=(1,), in_specs=[], out_specs=[], out_shape=[], args=(),
                  comms=comms)[1]


def _adam_math(w, g, m, v):
    m = ADAM_B1 * m + (1.0 - ADAM_B1) * g
    v = ADAM_B2 * v + (1.0 - ADAM_B2) * (g * g)
    m_hat = m / (1.0 - ADAM_B1 ** ADAM_STEP)
    v_hat = v / (1.0 - ADAM_B2 ** ADAM_STEP)
    return -ADAM_LR * (m_hat / (jnp.sqrt(v_hat) + ADAM_EPS) + ADAM_WD * w), m, v


def _adamw_small(own, slots, w, m, v, me):
    rows, cols = own.shape

    def body(me_ref, own_ref, slots_ref, w_ref, m_ref, v_ref, g_ref, d_ref, nm_ref, nv_ref):
        own_v = own_ref[...]
        g = jnp.where(me_ref[0] == 0, own_v, slots_ref[0])
        for i in range(1, 8):
            g = g + jnp.where(me_ref[0] == i, own_v, slots_ref[i])
        g_ref[...] = g
        d_ref[...], nm_ref[...], nv_ref[...] = _adam_math(w_ref[...], g, m_ref[...], v_ref[...])

    flat = pl.BlockSpec((rows, cols), lambda i, me_ref: (0, 0))
    return pl.pallas_call(
        body, name="adamw_small",
        grid_spec=pltpu.PrefetchScalarGridSpec(
            num_scalar_prefetch=1, grid=(1,),
            in_specs=[flat, pl.BlockSpec((8, rows, cols), lambda i, me_ref: (0, 0, 0)), flat, flat, flat],
            out_specs=[flat] * 4),
        out_shape=[_sds((rows, cols), F32)] * 4,
        compiler_params=_seq_params(),
    )(me, own, slots, w, m, v)


def kernel(x, positions, pre_mix_norm, w_in, sgu_ln_gain, sgu_ln_bias, sgu_w_spatial, sgu_b_spatial, attn_out_norm, sgu_out_norm, w_out, post_mix_norm, pre_ffn_norm, w_gate, w_up, w_down, post_ffn_norm, loss_target, m_pre_mix_norm, m_w_in, m_sgu_ln_gain, m_sgu_ln_bias, m_sgu_w_spatial, m_sgu_b_spatial, m_attn_out_norm, m_sgu_out_norm, m_w_out, m_post_mix_norm, m_pre_ffn_norm, m_w_gate, m_w_up, m_w_down, m_post_ffn_norm, v_pre_mix_norm, v_w_in, v_sgu_ln_gain, v_sgu_ln_bias, v_sgu_w_spatial, v_sgu_b_spatial, v_attn_out_norm, v_sgu_out_norm, v_w_out, v_post_mix_norm, v_pre_ffn_norm, v_w_gate, v_w_up, v_w_down, v_post_ffn_norm):
    a = dict(locals())
    cx, cy, cc = _coords()
    s_me = 2 * cx + cy
    core = jnp.stack([cc]).astype(jnp.int32)
    shard_core = jnp.stack([s_me, cc]).astype(jnp.int32)
    me = jnp.stack([4 * cx + 2 * cy + cc]).astype(jnp.int32)
    small = {n: (a[n][0] if a[n].ndim > 2 else a[n]) for n in SMALL}
    b_t = small["sgu_b_spatial"].T
    xs, pos, target = x[0], positions.reshape(SEQ, 1), loss_target[0]
    flipped = ("w_gate", "w_up")

    def big(name, n):
        return jnp.swapaxes(a[name], 1, 2)[0] if n in flipped else a[name][0]

    own = {"w_in": _halves(big("w_in", "w_in").astype(BF16))}

    def with_own(full, n):
        full = lax.dynamic_update_slice(full, own[n][None], (s_me, 0, 0, 0))
        return full.reshape((N_SHARD,) + big(n, n).shape)

    ffn = ("w_gate", "w_up", "w_down")
    g_in = _gather_ici([own["w_in"]])
    (s_in,), token = _split_starts("gather_in_start", [g_in], small["pre_mix_norm"])
    for n in ("w_out",) + ffn:
        own[n] = _halves((big(n, n) + token[0:1, 0:1]).astype(BF16))
    g_out, g_ffn = _gather_ici([own["w_out"]]), _gather_ici([own[n] for n in ffn])
    (s_out, s_ffn), token = _split_starts("gather_rest_start", [g_out, g_ffn], token)
    in_lands = _split_wait("gather_in_wait", g_in, *s_in, token)
    ((in_lands,),) = _comm_only("comm_pass_in", [_gather_pass(in_lands)])
    w_in_f = with_own(in_lands, "w_in")
    h, u, vs, *qkv = _inproj_fwd(xs, pos, small["pre_mix_norm"], w_in_f)
    views = [tuple(qkv[3 * i:3 * i + 3]) for i in range(len(DILATIONS))]
    o_list, l_list = [], []
    for dil, (qv, kv, vv) in zip(DILATIONS, views):
        o, l = _attn_fwd(qv, kv, vv, dil)
        o_list.append(o)
        l_list.append(l)
    out_lands = _split_wait("gather_out_wait", g_out, *s_out, l_list[-1])
    sgu, ((out_lands,),) = _sgu_fwd(u, vs, small["sgu_ln_gain"], small["sgu_ln_bias"], small["sgu_w_spatial"], b_t,
                                    comms=[_gather_pass(out_lands)])
    w_out_f = with_own(out_lands, "w_out")
    ffn_lands = _split_wait("gather_ffn_wait", g_ffn, *s_ffn, sgu)
    (attn, mixed, y, x1, *lses), (ffn_lands,) = _mix_out_fwd(
        o_list, l_list, sgu, xs, w_out_f, small["attn_out_norm"], small["sgu_out_norm"], small["post_mix_norm"],
        comms=[_gather_pass(ffn_lands)])
    w_gate_f, w_up_f, w_down_f = (with_own(f, n).reshape(FF, D_MODEL) for f, n in zip(ffn_lands, ffn))
    h2, act, dg, dup, df, dx1, loss_cols, d_pre_ffn, d_post_ffn = _ffn_fwd_bwd(
        x1, target, w_gate_f, w_up_f, w_down_f, small["pre_ffn_norm"], small["post_ffn_norm"])

    full_tok = pl.BlockSpec((SEQ, D_MODEL), lambda s: (0, 0), pipeline_mode=pl.Buffered(1))
    ff_tok = pl.BlockSpec((1, SEQ, FF_S), lambda s: (s, 0, 0))
    gw = {}
    gw["w_gate"], gw["w_up"] = _wgrad_ff([dg, dup], h2, "wgrad_gate_up")
    (gw["w_down"],), ((sib_gate,),) = _wgrad_ff([act], df, "wgrad_down", comms=[_rs_sibling([gw["w_gate"]])])
    (dy, dsgu, d_post_mix, d_attn_norm, d_sgu_norm, *dviews), ((sib_up, sib_down),) = _outproj_bwd(
        dx1, y, attn, sgu, w_out_f, small["post_mix_norm"], small["attn_out_norm"],
        small["sgu_out_norm"], comms=[_rs_sibling([gw["w_up"], gw["w_down"]])])
    sib = {"w_gate": sib_gate, "w_up": sib_up, "w_down": sib_down}
    part = dict(zip(ffn, _chip_sums([gw[n] for n in ffn], [sib[n] for n in ffn], shard_core)))
    gw["w_out"] = _wgrad(mixed, dy, pl.BlockSpec((SEQ, OUT_S), lambda s: (0, s)), full_tok, (OUT_S, D_MODEL),
                         "wgrad_out")
    x_ffn = _rs_chips([part[n] for n in ffn])
    (s_ffn,), token = _split_starts("rs_ffn_start", [x_ffn], small["pre_mix_norm"])
    (du, dvs_sgu, d_w_sp, d_b_sp, d_ln_gain, d_ln_bias), ((sib["w_out"],),) = _sgu_bwd(
        u, vs, dsgu, small["sgu_ln_gain"], small["sgu_ln_bias"], small["sgu_w_spatial"], b_t,
        comms=[_rs_sibling([gw["w_out"]])], after=[token])
    (part["w_out"],) = _chip_sums([gw["w_out"]], [sib["w_out"]], shard_core)
    packed_early = _pack({
        "sgu_ln_gain": d_ln_gain, "sgu_ln_bias": d_ln_bias, "sgu_w_spatial": d_w_sp, "sgu_b_spatial": d_b_sp,
        "attn_out_norm": d_attn_norm, "sgu_out_norm": d_sgu_norm, "post_mix_norm": d_post_mix,
        "pre_ffn_norm": d_pre_ffn, "post_ffn_norm": d_post_ffn, LOSS_ROW: loss_cols}, SMALL_EARLY, SMALL_ROWS)
    x_out, x_small = _rs_chips([part["w_out"]]), _small_exchange(packed_early)
    (s_out, s_small), token = _split_starts("rs_out_small_start", [x_out, x_small], token)

    dqs, dks, dvs = [], [], []
    for i, (dil, (qv, kv, vv)) in enumerate(zip(DILATIONS, views)):
        dq, dk, dv = _attn_bwd(qv, kv, vv, dviews[i], dviews[3 + i], lses[i], dil, after=[token])
        dqs.append(dq)
        dks.append(dk)
        dvs.append(dv)
    half, far, joined = {}, {}, {}
    far.update(zip(ffn, _split_wait("rs_ffn_wait", x_ffn, *s_ffn, dvs[-1])))
    half.update(zip(ffn, _final_sums([gw[n] for n in ffn], [sib[n] for n in ffn], [far[n] for n in ffn],
                                     shard_core)))
    dproj, grad_x, d_pre_mix = _inproj_bwd(dqs, dks, dvs, du, dvs_sgu, pos, xs, dx1, w_in_f, small["pre_mix_norm"])
    (far["w_out"],) = _split_wait("rs_out_wait", x_out, *s_out, grad_x)
    half["w_out"] = _rs_final_sum(gw["w_out"], sib["w_out"], far["w_out"], shard_core)
    packed_late = _pack({"pre_mix_norm": d_pre_mix}, SMALL_LATE, 8)
    names = ffn + ("w_out",)
    gw["w_in"], (got, (slots_late,)) = _wgrad(
        h, dproj, full_tok, pl.BlockSpec((SEQ, IN_S), lambda s: (0, s)), (D_MODEL, IN_S), "wgrad_in",
        comms=[_rs_join([half[n] for n in names]), _small_exchange(packed_late)])
    joined.update(zip(names, got))
    ((sib["w_in"],),) = _comm_only("comm_rs_sibling_in", [_rs_sibling([gw["w_in"]])])
    (part["w_in"],) = _chip_sums([gw["w_in"]], [sib["w_in"]], shard_core)
    x_in = _rs_chips([part["w_in"]])
    (s_in,), token = _split_starts("rs_in_start", [x_in], small["pre_mix_norm"])

    grads, deltas, new_m, new_v = {}, {}, {}, {}

    def record(n, outs):
        grads[n], deltas[n], new_m[n], new_v[n] = (
            jnp.swapaxes(o[None], 1, 2) if n in flipped else o[None] for o in outs)

    def update(names, block_rows, name, after):
        for n, outs in zip(names, _adamw_multi(
                [big(n, n) for n in names], [joined[n].reshape(big(n, n).shape) for n in names],
                [big("m_" + n, n) for n in names], [big("v_" + n, n) for n in names], block_rows, name, after)):
            record(n, outs)

    update(ffn, FF_S // 4, "adamw_ffn", [token])
    update(("w_out",), BIG_ADAM_ROWS["w_out"], "adamw_w_out", [token])
    (slots_early,) = _split_wait("small_early_wait", x_small, *s_small, [new_v[n] for n in ("w_down", "w_out")])
    (far["w_in"],) = _split_wait("rs_in_wait", x_in, *s_in, slots_early)
    half["w_in"] = _rs_final_sum(gw["w_in"], sib["w_in"], far["w_in"], shard_core)
    ((joined["w_in"],),) = _comm_only("comm_rs_join_in", [_rs_join([half["w_in"]])])
    update(("w_in",), BIG_ADAM_ROWS["w_in"], "adamw_w_in", [])
    a[LOSS_ROW] = a["m_" + LOSS_ROW] = a["v_" + LOSS_ROW] = jnp.zeros((1, D_MODEL), F32)
    for names, rows, packed, slots in ((SMALL_EARLY, SMALL_ROWS, packed_early, slots_early),
                                       (SMALL_LATE, 8, packed_late, slots_late)):
        outs = _adamw_small(packed, slots, _pack(a, names, rows), _pack({n: a["m_" + n] for n in names}, names, rows),
                            _pack({n: a["v_" + n] for n in names}, names, rows), me)
        for dst, buf in zip((grads, deltas, new_m, new_v), outs):
            dst.update(_unpack(buf, names, {n: a[n].shape for n in names}))
    loss = jnp.sum(grads[LOSS_ROW]) * np.float32(0.5 / D_MODEL)
    return (loss, grad_x[None], *[grads[n] for n in WEIGHTS], *[deltas[n] for n in WEIGHTS],
            *[new_m[n] for n in WEIGHTS], *[new_v[n] for n in WEIGHTS])
```

```python
import numpy as np
import jax
import jax.numpy as jnp
from jax import lax
from jax.experimental import pallas as pl
from jax.experimental.pallas import tpu as pltpu

F32 = jnp.float32
BF16 = jnp.bfloat16

SEQ = 2048
D_MODEL = 1024
HEAD_DIM = 64
ATTN_W = 512
SGU_W = 512
SGU_GROUPS = 8
CHUNK = 128
DILATIONS = (1, 4, 16)
N_SHARD = 4
IN_S = 640
OUT_S = 256
FF_S = 704
PROJ_W = N_SHARD * IN_S
FF = N_SHARD * FF_S
FF_CHUNKS = ((0, 1024), (1024, 2048), (2048, FF))
RMS_EPS = 1e-6
LN_EPS = 1e-5
ROPE_THETA = 500000.0
ATTN_SCALE = 1.0 / np.sqrt(HEAD_DIM)
NEG = -1e30
TM = 512
TM_FFN = 256
VMEM_LIMIT = 56 * 1024 * 1024
SMALL_ROWS = 136

ADAM_LR = 0.001
ADAM_B1 = 0.9
ADAM_B2 = 0.999
ADAM_EPS = 1e-08
ADAM_WD = 0.01
ADAM_STEP = 10

MESH = pl.DeviceIdType.MESH
ANY = pl.BlockSpec(memory_space=pl.ANY)


def _dot(a, b):
    return jnp.dot(a, b, preferred_element_type=F32)


def _dot_nt(a, b):
    return lax.dot_general(a, b, (((1,), (1,)), ((), ())), preferred_element_type=F32)


def _dot_tn(a, b):
    return lax.dot_general(a, b, (((0,), (0,)), ((), ())), preferred_element_type=F32)


def _dot_exact(a, b):
    return jnp.dot(a, b, preferred_element_type=F32, precision=lax.Precision.HIGHEST)


def _dot_select(a, sel):
    hi = a.astype(BF16)
    lo = (a - hi.astype(F32)).astype(BF16)
    sel = sel.astype(BF16)
    return _dot(hi, sel) + _dot(lo, sel)


def _rms_stats(x):
    r = lax.rsqrt(jnp.mean(x * x, axis=-1, keepdims=True) + RMS_EPS)
    return x * r, r


def _rms_bwd(xh, r, gain, dy):
    dxh = dy * gain
    dx = r * (dxh - xh * jnp.mean(dxh * xh, axis=-1, keepdims=True))
    return dx, jnp.sum(dy * xh, axis=0, keepdims=True)


_ERF_ALPHA = (-2.72614225801306e-10, 2.77068142495902e-08, -2.10102402082508e-06, -5.69250639462346e-05,
              -7.34990630326855e-04, -2.95459980854025e-03, -1.60960333262415e-02)
_ERF_BETA = (-1.45660718464996e-05, -2.13374055278905e-04, -1.68282697438203e-03, -7.37332916720468e-03,
             -1.42647390514189e-02)


def _erf(x):
    x = jnp.clip(x, -4.0, 4.0)
    x2 = x * x
    p = jnp.full_like(x, _ERF_ALPHA[0])
    for a in _ERF_ALPHA[1:]:
        p = p * x2 + a
    q = jnp.full_like(x, _ERF_BETA[0])
    for b in _ERF_BETA[1:]:
        q = q * x2 + b
    return x * p / q


def _normal_cdf(x):
    return 0.5 * (1.0 + _erf(x * np.float32(1.0 / np.sqrt(2.0))))


def _gelu_grad(x, cdf):
    pdf = jnp.exp(-0.5 * x * x) * np.float32(1.0 / np.sqrt(2.0 * np.pi))
    return cdf + x * pdf


def _sigmoid(x):
    return 1.0 / (1.0 + jnp.exp(-x))


_INV_FREQ = tuple(float(np.float32(ROPE_THETA ** (-2.0 * j / 16.0))) for j in range(8))


def _rot_tables(pos):
    lane = lax.broadcasted_iota(jnp.int32, (1, 128), 1)
    d = lane & 63
    j = d & 7
    inv = jnp.zeros((1, 128), F32)
    for jj in range(8):
        inv = jnp.where(j == jj, _INV_FREQ[jj], inv)
    ang = pos.astype(F32) * inv
    c = jnp.cos(ang)
    s = jnp.sin(ang)
    cos_t = jnp.where(d < 16, c, 1.0)
    sin_a = jnp.where(d < 8, -s, 0.0)
    sin_b = jnp.where((d >= 8) & (d < 16), s, 0.0)
    return tuple(jnp.tile(t, (1, 4)) for t in (cos_t, sin_a, sin_b))


def _rope(x, tabs):
    cos_t, sin_a, sin_b = tabs
    return x * cos_t + pltpu.roll(x, 504, 1) * sin_a + pltpu.roll(x, 8, 1) * sin_b


def _rope_bwd(dy, tabs):
    cos_t, sin_a, sin_b = tabs
    return dy * cos_t + pltpu.roll(dy * sin_a, 8, 1) + pltpu.roll(dy * sin_b, 504, 1)


def _left_half():
    return lax.broadcasted_iota(jnp.int32, (CHUNK, CHUNK), 1) < HEAD_DIM


def _group_ones():
    lane = lax.broadcasted_iota(jnp.int32, (SGU_GROUPS, SGU_W), 1)
    row = lax.broadcasted_iota(jnp.int32, (SGU_GROUPS, SGU_W), 0)
    return ((lane >> 6) == row).astype(F32)


def _masked_spatial(w_ref):
    row = lax.broadcasted_iota(jnp.int32, (CHUNK, CHUNK), 0)
    col = lax.broadcasted_iota(jnp.int32, (CHUNK, CHUNK), 1)
    return [jnp.where(col <= row, w_ref[g], 0.0).astype(BF16) for g in range(SGU_GROUPS)]


def _sgu_core(u, vs, lg, lb, wm, bias_full):
    tm = u.shape[0]
    cdf_u, cdf_vs = _normal_cdf(u), _normal_cdf(vs)
    gu = u * cdf_u
    gv = vs * cdf_vs
    mu = jnp.mean(gv, axis=-1, keepdims=True)
    xc = gv - mu
    rstd = lax.rsqrt(jnp.mean(xc * xc, axis=-1, keepdims=True) + LN_EPS)
    xh = xc * rstd
    vnb = (xh * lg + lb).astype(BF16)
    left = _left_half()
    rows = []
    for c in range(tm // CHUNK):
        pieces = []
        for p in range(4):
            vp = vnb[c * CHUNK:(c + 1) * CHUNK, p * 128:(p + 1) * 128]
            pieces.append(jnp.where(left, _dot(wm[2 * p], vp), _dot(wm[2 * p + 1], vp)))
        rows.append(jnp.concatenate(pieces, axis=1) + bias_full)
    mixed = jnp.concatenate(rows, axis=0)
    return gu, xh, rstd, vnb, mixed, cdf_u, cdf_vs


def _resident(shape):
    n = len(shape)
    return pl.BlockSpec(shape, lambda *_: (0,) * n, pipeline_mode=pl.Buffered(1))


def _rows(ncol, tm=TM):
    return pl.BlockSpec((tm, ncol), lambda i: (i, 0))


def _rows3(nlead, ncol, tm=TM):
    return pl.BlockSpec((nlead, tm, ncol), lambda i: (0, i, 0))


def _acc(ncol, nrow=1):
    return pl.BlockSpec((nrow, ncol), lambda i: (0, 0))


HEAD_W = 128


def _view_rows(dil, width=ATTN_W, tm=TM):
    return pl.BlockSpec((tm // dil, dil * width), lambda i: (i, 0))


def _view_shape(dil, dtype, width=ATTN_W):
    return _sds((SEQ // dil, dil * width), dtype)


def _slab_scratch():
    return pltpu.VMEM((4, TM, 128), F32)


def _store_view(val, out_ref, slabs, dil):
    width = val.shape[1]
    for j in range(width // 128):
        slabs[j] = val[:, j * 128:(j + 1) * 128]
    for r in range(dil):
        for j in range(width // 128):
            c0 = r * width + j * 128
            out_ref[:, c0:c0 + 128] = slabs.at[j][pl.ds(r, TM // dil, stride=dil), :].astype(out_ref.dtype)


def _load_view(in_ref, slabs, dil, width=ATTN_W):
    for r in range(dil):
        for j in range(width // 128):
            c0 = r * width + j * 128
            slabs.at[j][pl.ds(r, TM // dil, stride=dil), :] = in_ref[:, c0:c0 + 128].astype(F32)
    return jnp.concatenate([slabs[j] for j in range(width // 128)], axis=1)


def _head_spread():
    m = lax.broadcasted_iota(jnp.int32, (HEAD_W, ATTN_W), 0)
    lane = lax.broadcasted_iota(jnp.int32, (HEAD_W, ATTN_W), 1)
    return (m == 16 * (lane >> 6)).astype(F32)


def _head_sum():
    lane = lax.broadcasted_iota(jnp.int32, (ATTN_W, HEAD_W), 0)
    m = lax.broadcasted_iota(jnp.int32, (ATTN_W, HEAD_W), 1)
    return ((lane >> 6) == (m >> 4)).astype(F32)


def _seq_params():
    return pltpu.CompilerParams(dimension_semantics=("arbitrary",), vmem_limit_bytes=VMEM_LIMIT)


def _sds(shape, dtype):
    return jax.ShapeDtypeStruct(shape, dtype)


class _Comm:
    def __init__(self, args, out_shape, n_sems, start, finish, aliased=False):
        self.args, self.out_shape, self.n_sems = list(args), list(out_shape), n_sems
        self.start, self.finish, self.aliased = start, finish, aliased


def _pcall(body, *, name, grid, in_specs, out_specs, out_shape, args, scratch_shapes=(), comms=(), after=()):
    single = not isinstance(out_shape, (list, tuple))
    out_specs = [out_specs] if single else list(out_specs)
    out_shape = [out_shape] if single else list(out_shape)
    n_in, n_out, n_scr = len(in_specs), len(out_shape), len(scratch_shapes)
    c_args = [a for c in comms for a in c.args]
    c_outs = [o for c in comms for o in c.out_shape]
    aliases, ai, ao = {}, n_in, n_out
    for c in comms:
        if c.aliased:
            aliases.update({ai + k: ao + k for k in range(len(c.args))})
        ai += len(c.args)
        ao += len(c.out_shape)
    sems = [pltpu.SemaphoreType.DMA((c.n_sems,)) for c in comms for _ in range(2)]
    steps = grid[0]

    def wrapped(*refs):
        o0 = n_in + len(c_args) + len(after)
        s0 = o0 + n_out + len(c_outs)
        m_in, m_out, m_sem = refs[n_in:n_in + len(c_args)], refs[o0 + n_out:s0], refs[s0 + n_scr:]

        def each(phase):
            ii = oi = 0
            for k, c in enumerate(comms):
                getattr(c, phase)(m_in[ii:ii + len(c.args)], m_out[oi:oi + len(c.out_shape)],
                                  m_sem[2 * k], m_sem[2 * k + 1])
                ii += len(c.args)
                oi += len(c.out_shape)

        if comms:
            @pl.when(pl.program_id(0) == 0)
            def _():
                each("start")

        body(*refs[:n_in], *refs[o0:o0 + n_out], *refs[s0:s0 + n_scr])

        if comms:
            @pl.when(pl.program_id(0) == steps - 1)
            def _():
                each("finish")

    res = pl.pallas_call(
        wrapped, name=name, grid=grid,
        in_specs=list(in_specs) + [ANY] * (len(c_args) + len(after)), out_specs=out_specs + [ANY] * len(c_outs),
        out_shape=out_shape + c_outs, scratch_shapes=list(scratch_shapes) + sems,
        input_output_aliases=aliases, compiler_params=_seq_params(),
    )(*args, *c_args, *after)
    mine = res[0] if single else list(res[:n_out])
    if not comms:
        return mine
    theirs, oi = [], n_out
    for c in comms:
        theirs.append(list(res[oi:oi + len(c.out_shape)]))
        oi += len(c.out_shape)
    return mine, theirs


def _in_pieces(g):
    lo, hi = 512 * g, 512 * (g + 1)
    return [(s, max(lo, IN_S * s) - IN_S * s, min(hi, IN_S * (s + 1)) - IN_S * s)
            for s in range(N_SHARD) if max(lo, IN_S * s) < min(hi, IN_S * (s + 1))]


def _inproj_fwd(x, pos, g_pre, w_in, lg, lb, w_sp, b_t, comms=()):
    def body(x_ref, pos_ref, g_ref, w_ref, lg_ref, lb_ref, wsp_ref, bt_ref, h_ref, u_ref, vs_ref, sgu_ref, *rest):
        qkv_refs, slabs = rest[:9], rest[9:]
        xh, _ = _rms_stats(x_ref[...])
        h = (xh * g_ref[...]).astype(BF16)
        h_ref[...] = h
        tabs = _rot_tables(pos_ref[...])

        def group(g):
            return jnp.concatenate([_dot(h, w_ref[s, :, a:b]) for s, a, b in _in_pieces(g)], axis=1)

        for t in range(3):
            val = group(t)
            if t < 2:
                val = _rope(val, tabs)
            if t == 0:
                val = val * np.float32(ATTN_SCALE)
            qkv_refs[t][...] = val.astype(BF16)
            for i, dil in enumerate(DILATIONS[1:]):
                _store_view(val, qkv_refs[3 * (i + 1) + t], slabs[t], dil)
        u = group(3)
        vs = group(4)
        u_ref[...] = u
        vs_ref[...] = vs
        bias_full = _dot_exact(bt_ref[...], _group_ones())
        gu, _, _, _, mixed, _, _ = _sgu_core(u, vs, lg_ref[...], lb_ref[...], _masked_spatial(wsp_ref), bias_full)
        sgu_ref[...] = gu * mixed

    return _pcall(
        body, name="inproj_sgu_fwd", grid=(SEQ // TM,),
        in_specs=[_rows(D_MODEL), _rows(1), _resident((1, D_MODEL)), _resident((N_SHARD, D_MODEL, IN_S)),
                  _resident((1, SGU_W)), _resident((1, SGU_W)), _resident((SGU_GROUPS, CHUNK, CHUNK)),
                  _resident((CHUNK, SGU_GROUPS))],
        out_specs=[_rows(D_MODEL), _rows(512), _rows(512), _rows(512)]
        + [_view_rows(dil) for dil in DILATIONS for _ in range(3)],
        out_shape=[_sds((SEQ, D_MODEL), BF16), _sds((SEQ, 512), F32), _sds((SEQ, 512), F32), _sds((SEQ, 512), F32)]
        + [_view_shape(dil, BF16) for dil in DILATIONS for _ in range(3)],
        scratch_shapes=[_slab_scratch() for _ in range(3)],
        args=(x, pos, g_pre, w_in, lg, lb, w_sp, b_t), comms=comms)


def _sgu_fwd(u, vs, lg, lb, w_sp, b_t, comms=()):
    def body(u_ref, vs_ref, lg_ref, lb_ref, w_ref, bt_ref, out_ref):
        wm = _masked_spatial(w_ref)
        bias_full = _dot_exact(bt_ref[...], _group_ones())
        gu, _, _, _, mixed, _, _ = _sgu_core(u_ref[...], vs_ref[...], lg_ref[...], lb_ref[...], wm, bias_full)
        out_ref[...] = gu * mixed

    return _pcall(
        body, name="sgu_fwd", grid=(SEQ // TM,),
        in_specs=[_rows(SGU_W), _rows(SGU_W), _resident((1, SGU_W)), _resident((1, SGU_W)),
                  _resident((SGU_GROUPS, CHUNK, CHUNK)), _resident((CHUNK, SGU_GROUPS))],
        out_specs=_rows(SGU_W),
        out_shape=_sds((SEQ, SGU_W), F32),
        args=(u, vs, lg, lb, w_sp, b_t), comms=comms)


def _block_masks():
    row = lax.broadcasted_iota(jnp.int32, (CHUNK, CHUNK), 0)
    col = lax.broadcasted_iota(jnp.int32, (CHUNK, CHUNK), 1)
    return col <= row, col >= row


def _attn_fwd(qv, kv, vv, dil, comms=()):
    seg = SEQ // dil
    nblk = seg // CHUNK
    rps = 4 if nblk == 1 else 1

    def body(q_ref, k_ref, v_ref, o_ref, l_ref):
        left = _left_half()
        m_cur, m_prev = _block_masks()
        zero = jnp.zeros((CHUNK, CHUNK), BF16)
        ones = (jnp.where(left, 1.0, 0.0).astype(BF16), jnp.where(left, 0.0, 1.0).astype(BF16))

        sides = tuple(enumerate((left, ~left)))

        def rows(b):
            if isinstance(b, int):
                return b * CHUNK, max(b - 1, 0) * CHUNK
            return pl.multiple_of(b * CHUNK, CHUNK), pl.multiple_of(jnp.maximum(b - 1, 0) * CHUNK, CHUNK)

        def first(rr, b):
            r0, rp = rows(b)
            prev_ok = m_prev & (b > 0)
            tiles, scores = [], []
            for hp in range(4):
                ls = slice(rr * ATTN_W + hp * 128, rr * ATTN_W + (hp + 1) * 128)
                qp = q_ref[pl.ds(r0, CHUNK), ls]
                kc = k_ref[pl.ds(r0, CHUNK), ls]
                kp = k_ref[pl.ds(rp, CHUNK), ls] if nblk > 1 else None
                tiles.append((ls, v_ref[pl.ds(r0, CHUNK), ls], v_ref[pl.ds(rp, CHUNK), ls] if nblk > 1 else None))
                for _, hm in sides:
                    qh = jnp.where(hm, qp, zero)
                    sc = jnp.where(m_cur, _dot_nt(qh, kc), NEG)
                    sp = jnp.where(prev_ok, _dot_nt(qh, kp), NEG) if nblk > 1 else None
                    scores.append((sc, sp))
            return tiles, scores

        def second(scores):
            probs = []
            for sc, sp in scores:
                if nblk > 1:
                    m = jnp.max(jnp.maximum(sc, sp), axis=-1, keepdims=True)
                    pc = jnp.exp(sc - m)
                    pp = jnp.exp(sp - m)
                    probs.append((m, pc.astype(BF16), pp.astype(BF16), (pc + pp).astype(BF16)))
                else:
                    m = jnp.max(sc, axis=-1, keepdims=True)
                    pc = jnp.exp(sc - m).astype(BF16)
                    probs.append((m, pc, None, pc))
            return probs

        def third(rr, b, tiles, probs):
            r0, _ = rows(b)
            for hp, (ls, vc, vp) in enumerate(tiles):
                acc = jnp.zeros((CHUNK, CHUNK), F32)
                den = jnp.zeros((CHUNK, CHUNK), F32)
                for side, hm in sides:
                    _, pc, pp, psum = probs[2 * hp + side]
                    acc = acc + _dot(pc, jnp.where(hm, vc, zero))
                    if nblk > 1:
                        acc = acc + _dot(pp, jnp.where(hm, vp, zero))
                    den = den + _dot(psum, ones[side])
                o_ref[pl.ds(r0, CHUNK), ls] = (acc / den).astype(o_ref.dtype)
                lse = jnp.where(left, probs[2 * hp][0], probs[2 * hp + 1][0]) + jnp.log(den)
                l_ref[pl.ds(r0, CHUNK), rr * HEAD_W + 32 * hp:rr * HEAD_W + 32 * hp + 32] = lse[:, 48:80]

        def run(units):
            data = [first(rr, b) for rr, b in units]
            probs = [second(scores) for _, scores in data]
            for (rr, b), (tiles, _), pr in zip(units, data, probs):
                third(rr, b, tiles, pr)

        if nblk == 1:
            run([(rr, 0) for rr in range(rps)])
        else:
            def one(b, carry):
                run([(0, b)])
                return carry

            lax.fori_loop(0, nblk, one, 0)

    spec = pl.BlockSpec((seg, rps * ATTN_W), lambda r: (0, r))
    return _pcall(
        body, name=f"attn_fwd_d{dil}", grid=(dil // rps,),
        in_specs=[spec, spec, spec], out_specs=[spec, pl.BlockSpec((seg, rps * HEAD_W), lambda r: (0, r))],
        out_shape=[_sds((seg, dil * ATTN_W), BF16), _sds((seg, dil * HEAD_W), F32)],
        args=(qv, kv, vv), comms=comms)


def _lane_left(nrows):
    return lax.broadcasted_iota(jnp.int32, (nrows, CHUNK), 1) < HEAD_DIM


def _attn_rows(b):
    if isinstance(b, int):
        return b * CHUNK, max(b - 1, 0) * CHUNK
    return pl.multiple_of(b * CHUNK, CHUNK), pl.multiple_of(jnp.maximum(b - 1, 0) * CHUNK, CHUNK)


def _attn_fwd_fused(qv, kv, vv, dil, comms=()):
    seg = SEQ // dil
    nblk = seg // CHUNK
    rps = 4 if nblk == 1 else 1

    def body(q_ref, k_ref, v_ref, o_ref, l_ref):
        left = _left_half()
        m_cur, m_prev = _block_masks()
        sides = tuple(enumerate((left, ~left)))
        zero = jnp.zeros((CHUNK, CHUNK), BF16)

        def first(rr, b, both):
            r0, rp = _attn_rows(b)
            keys = pl.ds(rp, 2 * CHUNK) if both else pl.ds(r0, CHUNK)
            ok = jnp.concatenate([m_prev, m_cur], axis=1) if both else m_cur
            tiles, scores = [], []
            for hp in range(4):
                ls = slice(rr * ATTN_W + hp * 128, rr * ATTN_W + (hp + 1) * 128)
                qp = q_ref[pl.ds(r0, CHUNK), ls]
                k2 = k_ref[keys, ls]
                tiles.append((ls, v_ref[keys, ls]))
                for _, hm in sides:
                    scores.append(jnp.where(ok, _dot_nt(jnp.where(hm, qp, zero), k2), NEG))
            return tiles, scores

        def second(scores):
            probs = []
            for s in scores:
                m = jnp.max(s, axis=-1, keepdims=True)
                probs.append((m, jnp.exp(s - m).astype(BF16)))
            return probs

        def third(rr, b, tiles, probs, both):
            r0, _ = _attn_rows(b)
            nk = 2 * CHUNK if both else CHUNK
            left_k = _lane_left(nk)
            ones = (jnp.where(left_k, 1.0, 0.0).astype(BF16), jnp.where(left_k, 0.0, 1.0).astype(BF16))
            zero_k = jnp.zeros((nk, CHUNK), BF16)
            for hp, (ls, v2) in enumerate(tiles):
                acc = jnp.zeros((CHUNK, CHUNK), F32)
                den = jnp.zeros((CHUNK, CHUNK), F32)
                for side in range(2):
                    p = probs[2 * hp + side][1]
                    acc = acc + _dot(p, jnp.where(left_k if side == 0 else ~left_k, v2, zero_k))
                    den = den + _dot(p, ones[side])
                o_ref[pl.ds(r0, CHUNK), ls] = (acc / den).astype(o_ref.dtype)
                lse = jnp.where(left, probs[2 * hp][0], probs[2 * hp + 1][0]) + jnp.log(den)
                l_ref[pl.ds(r0, CHUNK), rr * HEAD_W + 32 * hp:rr * HEAD_W + 32 * hp + 32] = lse[:, 48:80]

        def run(units, both):
            data = [first(rr, b, both) for rr, b in units]
            probs = [second(scores) for _, scores in data]
            for (rr, b), (tiles, _), pr in zip(units, data, probs):
                third(rr, b, tiles, pr, both)

        run([(rr, 0) for rr in range(rps)], False)
        if nblk > 1:
            def one(b, carry):
                run([(0, b)], True)
                return carry

            lax.fori_loop(1, nblk, one, 0)

    spec = pl.BlockSpec((seg, rps * ATTN_W), lambda r: (0, r))
    return _pcall(
        body, name=f"attn_fwd_d{dil}", grid=(dil // rps,),
        in_specs=[spec, spec, spec], out_specs=[spec, pl.BlockSpec((seg, rps * HEAD_W), lambda r: (0, r))],
        out_shape=[_sds((seg, dil * ATTN_W), BF16), _sds((seg, dil * HEAD_W), F32)],
        args=(qv, kv, vv), comms=comms)


def _mix_out_fwd(o_list, l_list, sgu, x, w_out, g_attn, g_sgu, g_post, comms=()):
    def body(o1, o2, o3, l1, l2, l3, sgu_ref, x_ref, w_ref, ga_ref, gs_ref, gp_ref,
             attn_ref, mixed_ref, y_ref, x1_ref, lse1_ref, lse2_ref, lse3_ref, slabs_a, slabs_b):
        os = [o1[...], _load_view(o2, slabs_a, DILATIONS[1]), _load_view(o3, slabs_b, DILATIONS[2])]
        ls = [l1[...], _load_view(l2, slabs_a, DILATIONS[1], HEAD_W), _load_view(l3, slabs_b, DILATIONS[2], HEAD_W)]
        m = jnp.maximum(jnp.maximum(ls[0], ls[1]), ls[2])
        es = [jnp.exp(l - m) for l in ls]
        den = es[0] + es[1] + es[2]
        spread = _head_spread()
        attn = sum(_dot_select(e / den, spread) * o for e, o in zip(es, os))
        attn_ref[...] = attn
        lse = m + jnp.log(den)
        lse1_ref[...] = lse
        _store_view(lse, lse2_ref, slabs_a, DILATIONS[1])
        _store_view(lse, lse3_ref, slabs_b, DILATIONS[2])
        ah, _ = _rms_stats(attn)
        sh, _ = _rms_stats(sgu_ref[...])
        mixed = jnp.concatenate([ah * ga_ref[...], sh * gs_ref[...]], axis=1).astype(BF16)
        mixed_ref[...] = mixed
        y = _dot(mixed[:, 0:OUT_S], w_ref[0])
        for s in range(1, N_SHARD):
            y = y + _dot(mixed[:, s * OUT_S:(s + 1) * OUT_S], w_ref[s])
        y_ref[...] = y
        yh, _ = _rms_stats(y)
        x1_ref[...] = x_ref[...] + yh * gp_ref[...]

    return _pcall(
        body, name="mix_out_fwd", grid=(SEQ // TM,),
        in_specs=[_view_rows(dil) for dil in DILATIONS] + [_view_rows(dil, HEAD_W) for dil in DILATIONS]
        + [_rows(512), _rows(D_MODEL), _resident((N_SHARD, OUT_S, D_MODEL)),
           _resident((1, 512)), _resident((1, 512)), _resident((1, D_MODEL))],
        out_specs=[_rows(512), _rows(D_MODEL), _rows(D_MODEL), _rows(D_MODEL)]
        + [_view_rows(dil, HEAD_W) for dil in DILATIONS],
        out_shape=[_sds((SEQ, 512), F32), _sds((SEQ, D_MODEL), BF16), _sds((SEQ, D_MODEL), F32),
                   _sds((SEQ, D_MODEL), F32)] + [_view_shape(dil, F32, HEAD_W) for dil in DILATIONS],
        scratch_shapes=[_slab_scratch(), _slab_scratch()],
        args=(*o_list, *l_list, sgu, x, w_out, g_attn, g_sgu, g_post), comms=comms)


def _ffn_fwd_bwd(x1, target, w_gate, w_up, w_down, g_pre, g_post, comms=()):
    def body(x1_ref, t_ref, wg_ref, wu_ref, wd_ref, gpf_ref, gpo_ref,
             h2_ref, a_ref, dg_ref, dup_ref, df_ref, dx1_ref, loss_ref, dgpf_ref, dgpo_ref, g_scr, up_scr):
        @pl.when(pl.program_id(0) == 0)
        def _():
            loss_ref[...] = jnp.zeros_like(loss_ref)
            dgpf_ref[...] = jnp.zeros_like(dgpf_ref)
            dgpo_ref[...] = jnp.zeros_like(dgpo_ref)

        x1 = x1_ref[...]
        gpf = gpf_ref[...]
        gpo = gpo_ref[...]
        xh, r = _rms_stats(x1)
        h2 = (xh * gpf).astype(BF16)
        h2_ref[...] = h2
        f = jnp.zeros((TM_FFN, D_MODEL), F32)
        for c0, c1 in FF_CHUNKS:
            g = _dot_nt(h2, wg_ref[c0:c1, :])
            up = _dot_nt(h2, wu_ref[c0:c1, :])
            g_scr[:, c0:c1] = g
            up_scr[:, c0:c1] = up
            a = (g * _sigmoid(g) * up).astype(BF16)
            a_ref[:, c0:c1] = a
            f = f + _dot(a, wd_ref[c0:c1, :])
        fh, rf = _rms_stats(f)
        diff = x1 + fh * gpo - t_ref[...]
        loss_ref[...] += jnp.sum(diff * diff, axis=0, keepdims=True)
        dout = diff * np.float32(1.0 / D_MODEL)
        df, dgpo = _rms_bwd(fh, rf, gpo, dout)
        dgpo_ref[...] += dgpo
        dfb = df.astype(BF16)
        df_ref[...] = dfb
        dh2 = jnp.zeros((TM_FFN, D_MODEL), F32)
        for c0, c1 in FF_CHUNKS:
            da = _dot_nt(dfb, wd_ref[c0:c1, :])
            g = g_scr[:, c0:c1]
            up = up_scr[:, c0:c1]
            sg = _sigmoid(g)
            dup = (da * (g * sg)).astype(BF16)
            dg = (da * up * (sg * (1.0 + g * (1.0 - sg)))).astype(BF16)
            dg_ref[:, c0:c1] = dg
            dup_ref[:, c0:c1] = dup
            dh2 = dh2 + _dot(dg, wg_ref[c0:c1, :]) + _dot(dup, wu_ref[c0:c1, :])
        dx, dgpf = _rms_bwd(xh, r, gpf, dh2)
        dgpf_ref[...] += dgpf
        dx1_ref[...] = dout + dx

    return _pcall(
        body, name="ffn_fwd_bwd", grid=(SEQ // TM_FFN,),
        in_specs=[_rows(D_MODEL, TM_FFN), _rows(D_MODEL, TM_FFN), _resident((FF, D_MODEL)),
                  _resident((FF, D_MODEL)), _resident((FF, D_MODEL)),
                  _resident((1, D_MODEL)), _resident((1, D_MODEL))],
        out_specs=[_rows(D_MODEL, TM_FFN), _rows(FF, TM_FFN), _rows(FF, TM_FFN), _rows(FF, TM_FFN),
                   _rows(D_MODEL, TM_FFN), _rows(D_MODEL, TM_FFN), _acc(D_MODEL), _acc(D_MODEL), _acc(D_MODEL)],
        out_shape=[_sds((SEQ, D_MODEL), BF16), _sds((SEQ, FF), BF16), _sds((SEQ, FF), BF16),
                   _sds((SEQ, FF), BF16), _sds((SEQ, D_MODEL), BF16), _sds((SEQ, D_MODEL), F32),
                   _sds((1, D_MODEL), F32), _sds((1, D_MODEL), F32), _sds((1, D_MODEL), F32)],
        scratch_shapes=[pltpu.VMEM((TM_FFN, FF), F32), pltpu.VMEM((TM_FFN, FF), F32)],
        args=(x1, target, w_gate, w_up, w_down, g_pre, g_post), comms=comms)


def _wgrad(a, b, a_spec, b_spec, out_block, name, comms=()):
    def body(a_ref, b_ref, o_ref):
        av = a_ref[0] if len(a_ref.shape) == 3 else a_ref[...]
        bv = b_ref[0] if len(b_ref.shape) == 3 else b_ref[...]
        o_ref[0] = _dot_tn(av, bv)

    return _pcall(
        body, name=name, grid=(N_SHARD,),
        in_specs=[a_spec, b_spec],
        out_specs=pl.BlockSpec((1,) + out_block, lambda s: (s, 0, 0)),
        out_shape=_sds((N_SHARD,) + out_block, F32),
        args=(a, b), comms=comms)


def _outproj_bwd(dx1, y, attn, sgu, w_out, g_post, g_attn, g_sgu, comms=()):
    def body(dx1_ref, y_ref, attn_ref, sgu_ref, w_ref, gp_ref, ga_ref, gs_ref,
             dy_ref, dsgu_ref, dgp_ref, dga_ref, dgs_ref, *rest):
        dattn_refs, delta_refs, (slabs_a, slabs_b) = rest[0:3], rest[3:6], rest[6:]

        @pl.when(pl.program_id(0) == 0)
        def _():
            dgp_ref[...] = jnp.zeros_like(dgp_ref)
            dga_ref[...] = jnp.zeros_like(dga_ref)
            dgs_ref[...] = jnp.zeros_like(dgs_ref)

        yh, ry = _rms_stats(y_ref[...])
        dy, dgp = _rms_bwd(yh, ry, gp_ref[...], dx1_ref[...])
        dgp_ref[...] += dgp
        dyb = dy.astype(BF16)
        dy_ref[...] = dyb
        dmixed = jnp.concatenate([_dot_nt(dyb, w_ref[s]) for s in range(N_SHARD)], axis=1)
        attn = attn_ref[...]
        ah, ra = _rms_stats(attn)
        dattn, dga = _rms_bwd(ah, ra, ga_ref[...], dmixed[:, 0:512])
        dga_ref[...] += dga
        sh, rs = _rms_stats(sgu_ref[...])
        dsgu, dgs = _rms_bwd(sh, rs, gs_ref[...], dmixed[:, 512:1024])
        dgs_ref[...] += dgs
        dsgu_ref[...] = dsgu
        delta = _dot_select(dattn * attn, _head_sum())
        dattn_refs[0][...] = dattn.astype(BF16)
        delta_refs[0][...] = delta
        for i, dil in enumerate(DILATIONS[1:]):
            _store_view(dattn, dattn_refs[i + 1], slabs_a, dil)
            _store_view(delta, delta_refs[i + 1], slabs_b, dil)

    return _pcall(
        body, name="outproj_bwd", grid=(SEQ // TM,),
        in_specs=[_rows(D_MODEL), _rows(D_MODEL), _rows(512), _rows(512), _resident((N_SHARD, OUT_S, D_MODEL)),
                  _resident((1, D_MODEL)), _resident((1, 512)), _resident((1, 512))],
        out_specs=[_rows(D_MODEL), _rows(512), _acc(D_MODEL), _acc(512), _acc(512)]
        + [_view_rows(dil) for dil in DILATIONS] + [_view_rows(dil, HEAD_W) for dil in DILATIONS],
        out_shape=[_sds((SEQ, D_MODEL), BF16), _sds((SEQ, 512), F32),
                   _sds((1, D_MODEL), F32), _sds((1, 512), F32), _sds((1, 512), F32)]
        + [_view_shape(dil, BF16) for dil in DILATIONS] + [_view_shape(dil, F32, HEAD_W) for dil in DILATIONS],
        scratch_shapes=[_slab_scratch(), _slab_scratch()],
        args=(dx1, y, attn, sgu, w_out, g_post, g_attn, g_sgu), comms=comms)


def _sgu_bwd(u, vs, dsgu, lg, lb, w_sp, b_t, comms=(), after=()):
    nsteps = SEQ // TM

    def body(u_ref, vs_ref, ds_ref, lg_ref, lb_ref, w_ref, bt_ref,
             du_ref, dvs_ref, dw_ref, db_ref, dlg_ref, dlb_ref, dbias_scr):
        i = pl.program_id(0)

        @pl.when(i == 0)
        def _():
            dw_ref[...] = jnp.zeros_like(dw_ref)
            dlg_ref[...] = jnp.zeros_like(dlg_ref)
            dlb_ref[...] = jnp.zeros_like(dlb_ref)
            dbias_scr[...] = jnp.zeros_like(dbias_scr)

        wm = _masked_spatial(w_ref)
        ones_g = _group_ones()
        bias_full = _dot_exact(bt_ref[...], ones_g)
        u = u_ref[...]
        vs = vs_ref[...]
        lg = lg_ref[...]
        gu, xh, rstd, vnb, mixed, cdf_u, cdf_vs = _sgu_core(u, vs, lg, lb_ref[...], wm, bias_full)
        dsgu = ds_ref[...]
        du_ref[...] = (dsgu * mixed * _gelu_grad(u, cdf_u)).astype(BF16)
        dmixed = dsgu * gu
        left = _left_half()
        dvn_rows = []
        for c in range(TM // CHUNK):
            rs = slice(c * CHUNK, (c + 1) * CHUNK)
            dm_c = dmixed[rs, :]
            dbias_scr[...] += dm_c
            pieces = []
            for p in range(4):
                ls = slice(p * 128, (p + 1) * 128)
                dmp = dm_c[:, ls]
                vp = vnb[rs, ls]
                dmb = dmp.astype(BF16)
                zero = jnp.zeros_like(dmb)
                dw_ref[2 * p] += _dot_nt(jnp.where(left, dmb, zero), vp)
                dw_ref[2 * p + 1] += _dot_nt(jnp.where(left, zero, dmb), vp)
                pieces.append(jnp.where(left, _dot_tn(wm[2 * p], dmb), _dot_tn(wm[2 * p + 1], dmb)))
            dvn_rows.append(jnp.concatenate(pieces, axis=1))
        dvn = jnp.concatenate(dvn_rows, axis=0)
        dlg_ref[...] += jnp.sum(dvn * xh, axis=0, keepdims=True)
        dlb_ref[...] += jnp.sum(dvn, axis=0, keepdims=True)
        dxh = dvn * lg
        dgv = rstd * (dxh - jnp.mean(dxh, axis=-1, keepdims=True) - xh * jnp.mean(dxh * xh, axis=-1, keepdims=True))
        dvs_ref[...] = (dgv * _gelu_grad(vs, cdf_vs)).astype(BF16)

        @pl.when(i == nsteps - 1)
        def _():
            row = lax.broadcasted_iota(jnp.int32, (CHUNK, CHUNK), 0)
            col = lax.broadcasted_iota(jnp.int32, (CHUNK, CHUNK), 1)
            for g in range(SGU_GROUPS):
                dw_ref[g] = jnp.where(col <= row, dw_ref[g], 0.0)
            db_ref[...] = lax.dot_general(ones_g, dbias_scr[...], (((1,), (1,)), ((), ())),
                                          preferred_element_type=F32, precision=lax.Precision.HIGHEST)

    return _pcall(
        body, name="sgu_bwd", grid=(nsteps,),
        in_specs=[_rows(SGU_W), _rows(SGU_W), _rows(SGU_W), _resident((1, SGU_W)), _resident((1, SGU_W)),
                  _resident((SGU_GROUPS, CHUNK, CHUNK)), _resident((CHUNK, SGU_GROUPS))],
        out_specs=[_rows(SGU_W), _rows(SGU_W), pl.BlockSpec((SGU_GROUPS, CHUNK, CHUNK), lambda i: (0, 0, 0)),
                   _acc(CHUNK, SGU_GROUPS), _acc(SGU_W), _acc(SGU_W)],
        out_shape=[_sds((SEQ, SGU_W), BF16), _sds((SEQ, SGU_W), BF16), _sds((SGU_GROUPS, CHUNK, CHUNK), F32),
                   _sds((SGU_GROUPS, CHUNK), F32), _sds((1, SGU_W), F32), _sds((1, SGU_W), F32)],
        scratch_shapes=[pltpu.VMEM((CHUNK, SGU_W), F32)],
        args=(u, vs, dsgu, lg, lb, w_sp, b_t), comms=comms, after=after)


def _attn_bwd_v1(qv, kv, vv, dov, deltav, lsev, dil, comms=(), after=()):
    seg = SEQ // dil
    nblk = seg // CHUNK
    rps = 4 if nblk == 1 else 1

    def body(q_ref, k_ref, v_ref, do_ref, dl_ref, lse_ref, dq_ref, dk_ref, dv_ref, dk_wait, dv_wait):
        left = _left_half()
        m_cur, m_prev = _block_masks()

        sides = tuple(enumerate((left, ~left)))
        zero = jnp.zeros((CHUNK, CHUNK), BF16)

        def rows(b):
            if isinstance(b, int):
                return b * CHUNK, max(b - 1, 0) * CHUNK
            return pl.multiple_of(b * CHUNK, CHUNK), pl.multiple_of(jnp.maximum(b - 1, 0) * CHUNK, CHUNK)

        def first(rr, b):
            r0, rp = rows(b)
            tiles, firsts = [], []
            for hp in range(4):
                ls = slice(rr * ATTN_W + hp * 128, rr * ATTN_W + (hp + 1) * 128)
                qp = q_ref[pl.ds(r0, CHUNK), ls]
                kc = k_ref[pl.ds(r0, CHUNK), ls]
                vc = v_ref[pl.ds(r0, CHUNK), ls]
                dop = do_ref[pl.ds(r0, CHUNK), ls]
                kp = k_ref[pl.ds(rp, CHUNK), ls] if nblk > 1 else None
                vp = v_ref[pl.ds(rp, CHUNK), ls] if nblk > 1 else None
                tiles.append((ls, kc, kp))
                for side, hm in sides:
                    qh = jnp.where(hm, qp, zero)
                    doh = jnp.where(hm, dop, zero)
                    cur = (_dot_nt(kc, qh), _dot_nt(vc, doh))
                    prev = (_dot_nt(kp, qh), _dot_nt(vp, doh)) if nblk > 1 else None
                    firsts.append((qh, doh, cur, prev))
            return tiles, firsts

        def second(rr, b, firsts):
            r0, _ = rows(b)
            own_ok, prev_ok = m_prev, m_cur & (b > 0)
            lanes = slice(rr * HEAD_W, (rr + 1) * HEAD_W)
            lse_t = lse_ref[pl.ds(r0, CHUNK), lanes].T
            dl_t = dl_ref[pl.ds(r0, CHUNK), lanes].T
            seconds = []
            for i, (qh, doh, cur, prev) in enumerate(firsts):
                lse_h = lse_t[16 * i:16 * i + 1, :]
                dl_h = dl_t[16 * i:16 * i + 1, :]
                pc = jnp.exp(jnp.where(own_ok, cur[0] - lse_h, NEG))
                out = [pc.astype(BF16), (pc * (cur[1] - dl_h)).astype(BF16), None, None]
                if nblk > 1:
                    pp = jnp.exp(jnp.where(prev_ok, prev[0] - lse_h, NEG))
                    out[2:] = [pp.astype(BF16), (pp * (prev[1] - dl_h)).astype(BF16)]
                seconds.append(out)
            return seconds

        def third(rr, b, tiles, firsts, seconds):
            r0, rp = rows(b)
            for hp, (ls, kc, kp) in enumerate(tiles):
                dq = jnp.zeros((CHUNK, CHUNK), F32)
                dkc = jnp.zeros((CHUNK, CHUNK), F32)
                dvc = jnp.zeros((CHUNK, CHUNK), F32)
                dkp = jnp.zeros((CHUNK, CHUNK), F32)
                dvp = jnp.zeros((CHUNK, CHUNK), F32)
                for side, hm in sides:
                    qh, doh, _, _ = firsts[2 * hp + side]
                    pcb, dsc, ppb, dsp = seconds[2 * hp + side]
                    dq = dq + _dot_tn(dsc, jnp.where(hm, kc, zero))
                    dkc = dkc + _dot(dsc, qh)
                    dvc = dvc + _dot(pcb, doh)
                    if nblk > 1:
                        dq = dq + _dot_tn(dsp, jnp.where(hm, kp, zero))
                        dkp = dkp + _dot(dsp, qh)
                        dvp = dvp + _dot(ppb, doh)
                dq_ref[pl.ds(r0, CHUNK), ls] = dq.astype(dq_ref.dtype)
                if nblk == 1:
                    dk_ref[pl.ds(r0, CHUNK), ls] = dkc.astype(dk_ref.dtype)
                    dv_ref[pl.ds(r0, CHUNK), ls] = dvc.astype(dv_ref.dtype)
                else:
                    @pl.when(b > 0)
                    def _():
                        dk_ref[pl.ds(rp, CHUNK), ls] = (dk_wait[:, ls] + dkp).astype(dk_ref.dtype)
                        dv_ref[pl.ds(rp, CHUNK), ls] = (dv_wait[:, ls] + dvp).astype(dv_ref.dtype)

                    dk_wait[:, ls] = dkc
                    dv_wait[:, ls] = dvc

        def run(units):
            data = [first(rr, b) for rr, b in units]
            probs = [second(rr, b, firsts) for (rr, b), (_, firsts) in zip(units, data)]
            for (rr, b), (tiles, firsts), seconds in zip(units, data, probs):
                third(rr, b, tiles, firsts, seconds)

        if nblk == 1:
            run([(rr, 0) for rr in range(rps)])
        else:
            def one(b, carry):
                run([(0, b)])
                return carry

            lax.fori_loop(0, nblk, one, 0)
            last = (nblk - 1) * CHUNK
            dk_ref[last:last + CHUNK, :] = dk_wait[...].astype(dk_ref.dtype)
            dv_ref[last:last + CHUNK, :] = dv_wait[...].astype(dv_ref.dtype)

    spec = pl.BlockSpec((seg, rps * ATTN_W), lambda r: (0, r))
    return _pcall(
        body, name=f"attn_bwd_d{dil}", grid=(dil // rps,),
        in_specs=[spec] * 4 + [pl.BlockSpec((seg, rps * HEAD_W), lambda r: (0, r))] * 2, out_specs=[spec] * 3,
        out_shape=[_sds((seg, dil * ATTN_W), BF16)] * 3,
        scratch_shapes=[pltpu.VMEM((CHUNK, ATTN_W), F32), pltpu.VMEM((CHUNK, ATTN_W), F32)],
        args=(qv, kv, vv, dov, deltav, lsev), comms=comms, after=after)


def _attn_bwd(qv, kv, vv, dov, deltav, lsev, dil, comms=(), after=()):
    seg = SEQ // dil
    nblk = seg // CHUNK
    rps = 4 if nblk == 1 else 1

    def body(q_ref, k_ref, v_ref, do_ref, dl_ref, lse_ref, dq_ref, dk_ref, dv_ref, dk_wait, dv_wait):
        left = _left_half()
        m_cur, m_prev = _block_masks()
        sides = tuple(enumerate((left, ~left)))
        zero = jnp.zeros((CHUNK, CHUNK), BF16)

        def first(rr, b, both):
            r0, rp = _attn_rows(b)
            keys = pl.ds(rp, 2 * CHUNK) if both else pl.ds(r0, CHUNK)
            tiles, heads = [], []
            for hp in range(4):
                ls = slice(rr * ATTN_W + hp * 128, rr * ATTN_W + (hp + 1) * 128)
                qp = q_ref[pl.ds(r0, CHUNK), ls]
                dop = do_ref[pl.ds(r0, CHUNK), ls]
                k2 = k_ref[keys, ls]
                v2 = v_ref[keys, ls]
                tiles.append((ls, k2))
                for _, hm in sides:
                    qh = jnp.where(hm, qp, zero)
                    doh = jnp.where(hm, dop, zero)
                    heads.append((qh, doh, _dot_nt(k2, qh), _dot_nt(v2, doh)))
            return tiles, heads

        def second(rr, b, heads, both):
            r0, _ = _attn_rows(b)
            ok = jnp.concatenate([m_cur, m_prev], axis=0) if both else m_prev
            lanes = slice(rr * HEAD_W, (rr + 1) * HEAD_W)
            lse_t = lse_ref[pl.ds(r0, CHUNK), lanes].T
            dl_t = dl_ref[pl.ds(r0, CHUNK), lanes].T
            out = []
            for i, (_, _, s_t, dp_t) in enumerate(heads):
                p = jnp.exp(jnp.where(ok, s_t - lse_t[16 * i:16 * i + 1, :], NEG))
                out.append((p.astype(BF16), (p * (dp_t - dl_t[16 * i:16 * i + 1, :])).astype(BF16)))
            return out

        def third(rr, b, tiles, heads, probs, both):
            r0, rp = _attn_rows(b)
            nk = 2 * CHUNK if both else CHUNK
            left_k = _lane_left(nk)
            zero_k = jnp.zeros((nk, CHUNK), BF16)
            for hp, (ls, k2) in enumerate(tiles):
                dq = jnp.zeros((CHUNK, CHUNK), F32)
                dk2 = jnp.zeros((nk, CHUNK), F32)
                dv2 = jnp.zeros((nk, CHUNK), F32)
                for side in range(2):
                    qh, doh, _, _ = heads[2 * hp + side]
                    p, ds = probs[2 * hp + side]
                    dq = dq + _dot_tn(ds, jnp.where(left_k if side == 0 else ~left_k, k2, zero_k))
                    dk2 = dk2 + _dot(ds, qh)
                    dv2 = dv2 + _dot(p, doh)
                dq_ref[pl.ds(r0, CHUNK), ls] = dq.astype(dq_ref.dtype)
                if nblk == 1:
                    dk_ref[pl.ds(r0, CHUNK), ls] = dk2.astype(dk_ref.dtype)
                    dv_ref[pl.ds(r0, CHUNK), ls] = dv2.astype(dv_ref.dtype)
                elif both:
                    dk_ref[pl.ds(rp, CHUNK), ls] = (dk_wait[:, ls] + dk2[0:CHUNK]).astype(dk_ref.dtype)
                    dv_ref[pl.ds(rp, CHUNK), ls] = (dv_wait[:, ls] + dv2[0:CHUNK]).astype(dv_ref.dtype)
                    dk_wait[:, ls] = dk2[CHUNK:]
                    dv_wait[:, ls] = dv2[CHUNK:]
                else:
                    dk_wait[:, ls] = dk2
                    dv_wait[:, ls] = dv2

        def run(units, both):
            data = [first(rr, b, both) for rr, b in units]
            probs = [second(rr, b, heads, both) for (rr, b), (_, heads) in zip(units, data)]
            for (rr, b), (tiles, heads), pr in zip(units, data, probs):
                third(rr, b, tiles, heads, pr, both)

        run([(rr, 0) for rr in range(rps)], False)
        if nblk > 1:
            def one(b, carry):
                run([(0, b)], True)
                return carry

            lax.fori_loop(1, nblk, one, 0)
            last = (nblk - 1) * CHUNK
            dk_ref[last:last + CHUNK, :] = dk_wait[...].astype(dk_ref.dtype)
            dv_ref[last:last + CHUNK, :] = dv_wait[...].astype(dv_ref.dtype)

    spec = pl.BlockSpec((seg, rps * ATTN_W), lambda r: (0, r))
    return _pcall(
        body, name=f"attn_bwd_d{dil}", grid=(dil // rps,),
        in_specs=[spec] * 4 + [pl.BlockSpec((seg, rps * HEAD_W), lambda r: (0, r))] * 2, out_specs=[spec] * 3,
        out_shape=[_sds((seg, dil * ATTN_W), BF16)] * 3,
        scratch_shapes=[pltpu.VMEM((CHUNK, ATTN_W), F32), pltpu.VMEM((CHUNK, ATTN_W), F32)],
        args=(qv, kv, vv, dov, deltav, lsev), comms=comms, after=after)


def _inproj_bwd(dqs, dks, dvs, du, dvs_sgu, pos, x, dx1, w_in, g_pre, comms=()):
    def body(dq1, dq2, dq3, dk1, dk2, dk3, dv1, dv2, dv3, du_ref, dvs_ref, pos_ref, x_ref, dx1_ref, w_ref, g_ref,
             dproj_ref, gx_ref, dg_ref, slabs_a, slabs_b):
        @pl.when(pl.program_id(0) == 0)
        def _():
            dg_ref[...] = jnp.zeros_like(dg_ref)

        def total(r1, r2, r3):
            return r1[...] + _load_view(r2, slabs_a, DILATIONS[1]) + _load_view(r3, slabs_b, DILATIONS[2])

        tabs = _rot_tables(pos_ref[...])
        groups = {3: du_ref[...], 4: dvs_ref[...]}
        dh = jnp.zeros((TM, D_MODEL), F32)
        for g in (3, 4, 0, 1, 2):
            if g == 0:
                groups[g] = _rope_bwd(total(dq1, dq2, dq3) * np.float32(ATTN_SCALE), tabs).astype(BF16)
            elif g == 1:
                groups[g] = _rope_bwd(total(dk1, dk2, dk3), tabs).astype(BF16)
            elif g == 2:
                groups[g] = total(dv1, dv2, dv3).astype(BF16)
            dproj_ref[:, 512 * g:512 * (g + 1)] = groups[g]
            off = 0
            for s, a, b in _in_pieces(g):
                dh = dh + _dot_nt(groups[g][:, off:off + b - a], w_ref[s, :, a:b])
                off += b - a
        g = g_ref[...]
        xh, r = _rms_stats(x_ref[...])
        dx, dg = _rms_bwd(xh, r, g, dh)
        dg_ref[...] += dg
        gx_ref[...] = dx1_ref[...] + dx

    return _pcall(
        body, name="inproj_bwd", grid=(SEQ // TM,),
        in_specs=[_view_rows(dil) for dil in DILATIONS] * 3
        + [_rows(512), _rows(512), _rows(1), _rows(D_MODEL), _rows(D_MODEL),
           _resident((N_SHARD, D_MODEL, IN_S)), _resident((1, D_MODEL))],
        out_specs=[_rows(PROJ_W), _rows(D_MODEL), _acc(D_MODEL)],
        out_shape=[_sds((SEQ, PROJ_W), BF16), _sds((SEQ, D_MODEL), F32), _sds((1, D_MODEL), F32)],
        scratch_shapes=[_slab_scratch(), _slab_scratch()],
        args=(*dqs, *dks, *dvs, du, dvs_sgu, pos, x, dx1, w_in, g_pre), comms=comms)


def _to_view(a, dil):
    return a if dil == 1 else a.reshape(SEQ // dil, dil * a.shape[1])


def _from_view(a, dil):
    return a if dil == 1 else a.reshape(SEQ, a.shape[1] // dil)


def _local_step(x, pos, target, w_in, w_out, w_gate, w_up, w_down, small):
    b_t = small["sgu_b_spatial"].T
    h, u, vs, sgu, *qkv = _inproj_fwd(x, pos, small["pre_mix_norm"], w_in, small["sgu_ln_gain"],
                                      small["sgu_ln_bias"], small["sgu_w_spatial"], b_t)
    views = [tuple(qkv[3 * i:3 * i + 3]) for i in range(len(DILATIONS))]
    o_list, l_list = [], []
    for dil, (qv, kv, vv) in zip(DILATIONS, views):
        o, l = _attn_fwd(qv, kv, vv, dil)
        o_list.append(o)
        l_list.append(l)
    attn, mixed, y, x1, *lses = _mix_out_fwd(o_list, l_list, sgu, x, w_out, small["attn_out_norm"],
                                             small["sgu_out_norm"], small["post_mix_norm"])
    h2, a, dg, dup, df, dx1, loss_cols, d_pre_ffn, d_post_ffn = _ffn_fwd_bwd(
        x1, target, w_gate, w_up, w_down, small["pre_ffn_norm"], small["post_ffn_norm"])

    full_tok = pl.BlockSpec((SEQ, D_MODEL), lambda s: (0, 0), pipeline_mode=pl.Buffered(1))
    ff_tok = pl.BlockSpec((1, SEQ, FF_S), lambda s: (s, 0, 0))
    gw_gate, gw_up = _wgrad_ff([dg, dup], h2, "wgrad_gate_up")
    (gw_down,) = _wgrad_ff([a], df, "wgrad_down")

    dy, dsgu, d_post_mix, d_attn_norm, d_sgu_norm, *dviews = _outproj_bwd(
        dx1, y, attn, sgu, w_out, small["post_mix_norm"], small["attn_out_norm"], small["sgu_out_norm"])
    gw_out = _wgrad(mixed, dy, pl.BlockSpec((SEQ, OUT_S), lambda s: (0, s)), full_tok, (OUT_S, D_MODEL), "wgrad_out")
    du, dvs_sgu, d_w_sp, d_b_sp, d_ln_gain, d_ln_bias = _sgu_bwd(
        u, vs, dsgu, small["sgu_ln_gain"], small["sgu_ln_bias"], small["sgu_w_spatial"], b_t)

    dqs, dks, dvs = [], [], []
    for i, (dil, (qv, kv, vv)) in enumerate(zip(DILATIONS, views)):
        dq, dk, dv = _attn_bwd(qv, kv, vv, dviews[i], dviews[3 + i], lses[i], dil)
        dqs.append(dq)
        dks.append(dk)
        dvs.append(dv)
    dproj, grad_x, d_pre_mix = _inproj_bwd(dqs, dks, dvs, du, dvs_sgu, pos, x, dx1, w_in, small["pre_mix_norm"])
    gw_in = _wgrad(h, dproj, full_tok, pl.BlockSpec((SEQ, IN_S), lambda s: (0, s)), (D_MODEL, IN_S), "wgrad_in")

    small_grads = {
        "pre_mix_norm": d_pre_mix, "sgu_ln_gain": d_ln_gain, "sgu_ln_bias": d_ln_bias, "sgu_w_spatial": d_w_sp,
        "sgu_b_spatial": d_b_sp, "attn_out_norm": d_attn_norm, "sgu_out_norm": d_sgu_norm,
        "post_mix_norm": d_post_mix, "pre_ffn_norm": d_pre_ffn, "post_ffn_norm": d_post_ffn,
    }
    return loss_cols, grad_x, (gw_in, gw_out, gw_gate, gw_up, gw_down), small_grads


def _coords():
    return lax.axis_index("x"), lax.axis_index("y"), lax.axis_index("c")


def _other_chips(x, y):
    return [(1 - x, y), (x, 1 - y), (1 - x, 1 - y)]


def _comm_call(body, name, n_in, out_shape, scratch_shapes):
    return pl.pallas_call(
        body, name=name, in_specs=[ANY] * n_in, out_specs=[ANY] * len(out_shape), out_shape=out_shape,
        scratch_shapes=scratch_shapes,
        compiler_params=pltpu.CompilerParams(has_side_effects=True),
    )


def _gather_weights(shards):
    n = len(shards)
    halves = [s.reshape(2, s.shape[0] // 2, s.shape[1]) for s in shards]

    def body(*refs):
        ins, outs = refs[:n], refs[n:2 * n]
        send_sems, recv_sems = refs[2 * n:]
        x, y, c = _coords()
        s_me = 2 * x + y
        chips = _other_chips(x, y)
        sibling = (x, y, 1 - c)

        def copy(k, w, shard, cc, to):
            src = ins[w].at[cc] if shard is None else outs[w].at[shard, cc]
            dst = outs[w].at[s_me if shard is None else shard, cc]
            return pltpu.make_async_remote_copy(src_ref=src, dst_ref=dst, send_sem=send_sems.at[k],
                                                recv_sem=recv_sems.at[k], device_id=to, device_id_type=MESH)

        first = [copy(j * n + w, w, None, c, (cx, cy, c)) for j, (cx, cy) in enumerate(chips) for w in range(n)]
        for cp in first:
            cp.start()
        passed = []
        for j, (cx, cy) in enumerate(chips):
            for w in range(n):
                copy(j * n + w, w, 2 * cx + cy, c, (x, y, c)).wait_recv()
                fw = copy((3 + j) * n + w, w, 2 * cx + cy, c, sibling)
                fw.start()
                passed.append(fw)
        for j, (cx, cy) in enumerate(chips):
            for w in range(n):
                copy((3 + j) * n + w, w, 2 * cx + cy, 1 - c, (x, y, c)).wait_recv()
        for cp in first + passed:
            cp.wait_send()

    out_shape = [_sds((N_SHARD,) + h.shape, h.dtype) for h in halves]
    scratch = [pltpu.SemaphoreType.DMA((6 * n,)), pltpu.SemaphoreType.DMA((6 * n,))]
    full = _comm_call(body, "comm_gather_weights", n, out_shape, scratch)(*halves)
    s_me = 2 * lax.axis_index("x") + lax.axis_index("y")
    full = [lax.dynamic_update_slice(f, h[None], (s_me, 0, 0, 0)) for f, h in zip(full, halves)]
    return [f.reshape((N_SHARD,) + s.shape) for f, s in zip(full, shards)]


def _rs_to_sibling(gws):
    n = len(gws)

    def body(*refs):
        ins, outs = refs[:n], refs[n:2 * n]
        send_sems, recv_sems = refs[2 * n:]
        x, y, c = _coords()
        copies = []
        for w in range(n):
            hw = gws[w].shape[1] // 2
            copies.append(pltpu.make_async_remote_copy(
                src_ref=ins[w].at[:, pl.ds((1 - c) * hw, hw), :], dst_ref=outs[w], send_sem=send_sems.at[w],
                recv_sem=recv_sems.at[w], device_id=(x, y, 1 - c), device_id_type=MESH))
        for cp in copies:
            cp.start()
        for cp in copies:
            cp.wait()

    out_shape = [_sds((N_SHARD, g.shape[1] // 2, g.shape[2]), g.dtype) for g in gws]
    scratch = [pltpu.SemaphoreType.DMA((n,)), pltpu.SemaphoreType.DMA((n,))]
    return _comm_call(body, "comm_rs_sibling", n, out_shape, scratch)(*gws)


def _rs_chip_sum(gw, recv, core):
    _, rows, cols = gw.shape
    hw = rows // 2

    def body(c_ref, g_ref, r_ref, o_ref):
        o_ref[...] = (g_ref[...] + r_ref[...]).astype(BF16)

    return pl.pallas_call(
        body, name="rs_chip_sum",
        grid_spec=pltpu.PrefetchScalarGridSpec(
            num_scalar_prefetch=1, grid=(N_SHARD,),
            in_specs=[pl.BlockSpec((1, hw, cols), lambda s, c_ref: (s, c_ref[0], 0)),
                      pl.BlockSpec((1, hw, cols), lambda s, c_ref: (s, 0, 0))],
            out_specs=pl.BlockSpec((1, hw, cols), lambda s, c_ref: (s, 0, 0))),
        out_shape=_sds((N_SHARD, hw, cols), BF16),
        compiler_params=_seq_params(),
    )(core, gw, recv)


def _rs_between_chips(pbs):
    n = len(pbs)

    def body(*refs):
        ins, outs = refs[:n], refs[n:2 * n]
        send_sems, recv_sems = refs[2 * n:]
        x, y, c = _coords()
        copies = []
        for j, (cx, cy) in enumerate(_other_chips(x, y)):
            for w in range(n):
                copies.append(pltpu.make_async_remote_copy(
                    src_ref=ins[w].at[2 * cx + cy], dst_ref=outs[w].at[j], send_sem=send_sems.at[j * n + w],
                    recv_sem=recv_sems.at[j * n + w], device_id=(cx, cy, c), device_id_type=MESH))
        for cp in copies:
            cp.start()
        for cp in copies:
            cp.wait()

    out_shape = [_sds((3,) + p.shape[1:], p.dtype) for p in pbs]
    scratch = [pltpu.SemaphoreType.DMA((3 * n,)), pltpu.SemaphoreType.DMA((3 * n,))]
    return _comm_call(body, "comm_rs_chips", n, out_shape, scratch)(*pbs)


def _rs_final_sum(gw, recv_sib, recv_chips, shard_core):
    _, rows, cols = gw.shape
    hw = rows // 2

    def body(sc_ref, g_ref, r_ref, rc_ref, o_ref):
        acc = g_ref[0] + r_ref[0]
        for j in range(3):
            acc = acc + rc_ref[j].astype(F32)
        o_ref[0] = acc

    return pl.pallas_call(
        body, name="rs_final_sum",
        grid_spec=pltpu.PrefetchScalarGridSpec(
            num_scalar_prefetch=1, grid=(1,),
            in_specs=[pl.BlockSpec((1, hw, cols), lambda i, sc: (sc[0], sc[1], 0)),
                      pl.BlockSpec((1, hw, cols), lambda i, sc: (sc[0], 0, 0)),
                      pl.BlockSpec((3, hw, cols), lambda i, sc: (0, 0, 0))],
            out_specs=pl.BlockSpec((1, hw, cols), lambda i, sc: (sc[1], 0, 0))),
        out_shape=_sds((2, hw, cols), F32),
        compiler_params=_seq_params(),
    )(shard_core, gw, recv_sib, recv_chips)


def _rs_join_halves(halves):
    n = len(halves)

    def body(*refs):
        bufs = refs[n:2 * n]
        send_sems, recv_sems = refs[2 * n:]
        x, y, c = _coords()
        remote = [pltpu.make_async_remote_copy(
            src_ref=bufs[w].at[c], dst_ref=bufs[w].at[c], send_sem=send_sems.at[w], recv_sem=recv_sems.at[w],
            device_id=(x, y, 1 - c), device_id_type=MESH) for w in range(n)]
        for cp in remote:
            cp.start()
        for w in range(n):
            remote[w].wait_send()
            pltpu.make_async_remote_copy(
                src_ref=bufs[w].at[c], dst_ref=bufs[w].at[1 - c], send_sem=send_sems.at[w],
                recv_sem=recv_sems.at[w], device_id=(x, y, c), device_id_type=MESH).wait_recv()

    joined = pl.pallas_call(
        body, name="comm_rs_join", in_specs=[ANY] * n, out_specs=[ANY] * n,
        out_shape=[_sds(h.shape, h.dtype) for h in halves], input_output_aliases={w: w for w in range(n)},
        scratch_shapes=[pltpu.SemaphoreType.DMA((n,)), pltpu.SemaphoreType.DMA((n,))],
        compiler_params=pltpu.CompilerParams(has_side_effects=True),
    )(*halves)
    return [j.reshape(2 * h.shape[1], h.shape[2]) for j, h in zip(joined, halves)]


def _allreduce_small(buf):
    rows, cols = buf.shape

    def body(in_ref, out_ref, slots, send_sems, recv_sems):
        x, y, c = _coords()
        me = 4 * x + 2 * y + c
        copies, peers = [], []
        for k in range(1, 8):
            px = 1 - x if (k >> 2) & 1 else x
            py = 1 - y if (k >> 1) & 1 else y
            pc = 1 - c if k & 1 else c
            peers.append(4 * px + 2 * py + pc)
            copies.append(pltpu.make_async_remote_copy(
                src_ref=in_ref, dst_ref=slots.at[me], send_sem=send_sems.at[k - 1], recv_sem=recv_sems.at[k - 1],
                device_id=(px, py, pc), device_id_type=MESH))
        for cp in copies:
            cp.start()
        slots[me] = in_ref[...]
        for k in range(7):
            pltpu.make_async_remote_copy(
                src_ref=in_ref, dst_ref=slots.at[peers[k]], send_sem=send_sems.at[k], recv_sem=recv_sems.at[k],
                device_id=(x, y, c), device_id_type=MESH).wait_recv()
        for cp in copies:
            cp.wait_send()
        acc = slots[0]
        for i in range(1, 8):
            acc = acc + slots[i]
        out_ref[...] = acc

    vmem = pl.BlockSpec(memory_space=pltpu.VMEM)
    return pl.pallas_call(
        body, name="comm_allreduce_small", in_specs=[vmem], out_specs=vmem, out_shape=_sds((rows, cols), F32),
        scratch_shapes=[pltpu.VMEM((8, rows, cols), F32), pltpu.SemaphoreType.DMA((7,)), pltpu.SemaphoreType.DMA((7,))],
        compiler_params=pltpu.CompilerParams(has_side_effects=True, vmem_limit_bytes=VMEM_LIMIT),
    )(buf)


def _adamw(w, g, m, v, block_rows, name, after=()):
    rows, cols = w.shape

    def body(w_ref, g_ref, m_ref, v_ref, *rest):
        d_ref, nm_ref, nv_ref = rest[len(after):]
        g = g_ref[...]
        m = ADAM_B1 * m_ref[...] + (1.0 - ADAM_B1) * g
        v = ADAM_B2 * v_ref[...] + (1.0 - ADAM_B2) * (g * g)
        m_hat = m / (1.0 - ADAM_B1 ** ADAM_STEP)
        v_hat = v / (1.0 - ADAM_B2 ** ADAM_STEP)
        d_ref[...] = -ADAM_LR * (m_hat / (jnp.sqrt(v_hat) + ADAM_EPS) + ADAM_WD * w_ref[...])
        nm_ref[...] = m
        nv_ref[...] = v

    spec = pl.BlockSpec((block_rows, cols), lambda i: (i, 0))
    return pl.pallas_call(
        body, name=name, grid=(rows // block_rows,), in_specs=[spec] * 4 + [ANY] * len(after), out_specs=[spec] * 3,
        out_shape=[_sds((rows, cols), F32)] * 3,
        compiler_params=_seq_params(),
    )(w, g, m, v, *after)


WEIGHTS = ("pre_mix_norm", "w_in", "sgu_ln_gain", "sgu_ln_bias", "sgu_w_spatial", "sgu_b_spatial", "attn_out_norm",
           "sgu_out_norm", "w_out", "post_mix_norm", "pre_ffn_norm", "w_gate", "w_up", "w_down", "post_ffn_norm")
BIG = ("w_in", "w_out", "w_gate", "w_up", "w_down")
BIG_ADAM_ROWS = {"w_in": 256, "w_out": 128, "w_gate": 352, "w_up": 352, "w_down": 352}
SMALL = ("pre_mix_norm", "post_mix_norm", "pre_ffn_norm", "post_ffn_norm", "sgu_ln_gain", "sgu_ln_bias",
         "attn_out_norm", "sgu_out_norm", "sgu_w_spatial", "sgu_b_spatial")


def _pack_small(d):
    flat = [d[n].reshape(-1) for n in SMALL]
    used = sum(f.shape[0] for f in flat)
    flat.append(jnp.zeros((SMALL_ROWS * 1024 - used,), F32))
    return jnp.concatenate(flat).reshape(SMALL_ROWS, 1024)


def _unpack_small(buf, shapes):
    flat = buf.reshape(-1)
    out, off = {}, 0
    for n in SMALL:
        size = int(np.prod(shapes[n]))
        out[n] = flat[off:off + size].reshape(shapes[n])
        off += size
    return out


def _kernel_unoverlapped(x, positions, pre_mix_norm, w_in, sgu_ln_gain, sgu_ln_bias, sgu_w_spatial, sgu_b_spatial, attn_out_norm, sgu_out_norm, w_out, post_mix_norm, pre_ffn_norm, w_gate, w_up, w_down, post_ffn_norm, loss_target, m_pre_mix_norm, m_w_in, m_sgu_ln_gain, m_sgu_ln_bias, m_sgu_w_spatial, m_sgu_b_spatial, m_attn_out_norm, m_sgu_out_norm, m_w_out, m_post_mix_norm, m_pre_ffn_norm, m_w_gate, m_w_up, m_w_down, m_post_ffn_norm, v_pre_mix_norm, v_w_in, v_sgu_ln_gain, v_sgu_ln_bias, v_sgu_w_spatial, v_sgu_b_spatial, v_attn_out_norm, v_sgu_out_norm, v_w_out, v_post_mix_norm, v_pre_ffn_norm, v_w_gate, v_w_up, v_w_down, v_post_ffn_norm):
    a = dict(locals())
    cx, cy, cc = _coords()
    core = jnp.stack([cc]).astype(jnp.int32)
    shard_core = jnp.stack([2 * cx + cy, cc]).astype(jnp.int32)

    full = _gather_weights([a[n][0].astype(BF16) for n in BIG])
    small = {n: (a[n][0] if a[n].ndim > 2 else a[n]) for n in SMALL}
    loss_cols, grad_x, gws, small_grads = _local_step(
        x[0], positions.reshape(SEQ, 1), loss_target[0], *full, small)
    loss = lax.psum(jnp.sum(loss_cols) * np.float32(0.5 / D_MODEL), ("x", "y", "c"))

    recv_sib = _rs_to_sibling(list(gws))
    chip_part = [_rs_chip_sum(g, r, core) for g, r in zip(gws, recv_sib)]
    recv_chips = _rs_between_chips(chip_part)
    halves = [_rs_final_sum(g, r, rc, shard_core) for g, r, rc in zip(gws, recv_sib, recv_chips)]
    big_grads = dict(zip(BIG, _rs_join_halves(halves)))

    shapes = {n: a[n].shape for n in SMALL}
    small_sum = _allreduce_small(_pack_small(small_grads))

    grads, deltas, new_m, new_v = {}, {}, {}, {}
    for n in BIG:
        grads[n] = big_grads[n][None]
        d, nm, nv = _adamw(a[n][0], big_grads[n], a["m_" + n][0], a["v_" + n][0], BIG_ADAM_ROWS[n], "adamw_" + n)
        deltas[n], new_m[n], new_v[n] = d[None], nm[None], nv[None]
    d, nm, nv = _adamw(_pack_small({n: a[n] for n in SMALL}), small_sum, _pack_small({n: a["m_" + n] for n in SMALL}),
                       _pack_small({n: a["v_" + n] for n in SMALL}), SMALL_ROWS, "adamw_small")
    grads.update(_unpack_small(small_sum, shapes))
    deltas.update(_unpack_small(d, shapes))
    new_m.update(_unpack_small(nm, shapes))
    new_v.update(_unpack_small(nv, shapes))
    return (loss, grad_x[None], *[grads[n] for n in WEIGHTS], *[deltas[n] for n in WEIGHTS],
            *[new_m[n] for n in WEIGHTS], *[new_v[n] for n in WEIGHTS])


def _remote(src, dst, send_sem, recv_sem, to):
    return pltpu.make_async_remote_copy(src_ref=src, dst_ref=dst, send_sem=send_sem, recv_sem=recv_sem,
                                        device_id=to, device_id_type=MESH)


def _halves(a):
    *lead, rows, cols = a.shape
    return a.reshape(*lead, 2, rows // 2, cols)


def _gather_ici(shards):
    n = len(shards)

    def desc(ins, outs, ss, rs, j, w, landed):
        x, y, c = _coords()
        cx, cy = _other_chips(x, y)[j]
        shard = 2 * cx + cy if landed else 2 * x + y
        return _remote(ins[w].at[c], outs[w].at[shard, c], ss.at[j * n + w], rs.at[j * n + w], (cx, cy, c))

    def start(ins, outs, ss, rs):
        for j in range(3):
            for w in range(n):
                desc(ins, outs, ss, rs, j, w, False).start()

    def finish(ins, outs, ss, rs):
        for j in range(3):
            for w in range(n):
                desc(ins, outs, ss, rs, j, w, True).wait_recv()
                desc(ins, outs, ss, rs, j, w, False).wait_send()

    return _Comm(shards, [_sds((N_SHARD,) + s.shape, s.dtype) for s in shards], 3 * n, start, finish)


def _gather_pass(fulls):
    n = len(fulls)

    def desc(bufs, ss, rs, j, w, landed):
        x, y, c = _coords()
        cx, cy = _other_chips(x, y)[j]
        shard = 2 * cx + cy
        return _remote(bufs[w].at[shard, c], bufs[w].at[shard, 1 - c if landed else c],
                       ss.at[j * n + w], rs.at[j * n + w], (x, y, 1 - c))

    def start(ins, outs, ss, rs):
        for j in range(3):
            for w in range(n):
                desc(outs, ss, rs, j, w, False).start()

    def finish(ins, outs, ss, rs):
        for j in range(3):
            for w in range(n):
                desc(outs, ss, rs, j, w, True).wait_recv()
                desc(outs, ss, rs, j, w, False).wait_send()

    return _Comm(fulls, [_sds(f.shape, f.dtype) for f in fulls], 3 * n, start, finish, aliased=True)


def _rs_sibling(gws):
    n = len(gws)

    def desc(ins, outs, ss, rs, w):
        x, y, c = _coords()
        return _remote(ins[w].at[:, 1 - c], outs[w], ss.at[w], rs.at[w], (x, y, 1 - c))

    def start(ins, outs, ss, rs):
        for w in range(n):
            desc(ins, outs, ss, rs, w).start()

    def finish(ins, outs, ss, rs):
        for w in range(n):
            desc(ins, outs, ss, rs, w).wait()

    out_shape = [_sds((N_SHARD, g.shape[1] // 2, g.shape[2]), g.dtype) for g in gws]
    return _Comm([_halves(g) for g in gws], out_shape, n, start, finish)


def _rs_chips(pbs):
    n = len(pbs)

    def desc(ins, outs, ss, rs, j, w):
        x, y, c = _coords()
        cx, cy = _other_chips(x, y)[j]
        return _remote(ins[w].at[2 * cx + cy], outs[w].at[j], ss.at[j * n + w], rs.at[j * n + w], (cx, cy, c))

    def start(ins, outs, ss, rs):
        for j in range(3):
            for w in range(n):
                desc(ins, outs, ss, rs, j, w).start()

    def finish(ins, outs, ss, rs):
        for j in range(3):
            for w in range(n):
                desc(ins, outs, ss, rs, j, w).wait()

    return _Comm(pbs, [_sds((3,) + p.shape[1:], p.dtype) for p in pbs], 3 * n, start, finish)


def _rs_join(halves):
    n = len(halves)

    def desc(bufs, ss, rs, w, landed):
        x, y, c = _coords()
        return _remote(bufs[w].at[c], bufs[w].at[1 - c if landed else c], ss.at[w], rs.at[w], (x, y, 1 - c))

    def start(ins, outs, ss, rs):
        for w in range(n):
            desc(outs, ss, rs, w, False).start()

    def finish(ins, outs, ss, rs):
        for w in range(n):
            desc(outs, ss, rs, w, True).wait_recv()
            desc(outs, ss, rs, w, False).wait_send()

    return _Comm(halves, [_sds(h.shape, h.dtype) for h in halves], n, start, finish, aliased=True)


def _small_exchange(buf):
    def desc(ins, outs, ss, rs, k, landed):
        x, y, c = _coords()
        px = 1 - x if (k >> 2) & 1 else x
        py = 1 - y if (k >> 1) & 1 else y
        pc = 1 - c if k & 1 else c
        slot = 4 * px + 2 * py + pc if landed else 4 * x + 2 * y + c
        return _remote(ins[0], outs[0].at[slot], ss.at[k - 1], rs.at[k - 1], (px, py, pc))

    def start(ins, outs, ss, rs):
        for k in range(1, 8):
            desc(ins, outs, ss, rs, k, False).start()

    def finish(ins, outs, ss, rs):
        for k in range(1, 8):
            desc(ins, outs, ss, rs, k, True).wait_recv()
            desc(ins, outs, ss, rs, k, False).wait_send()

    return _Comm([buf], [_sds((8,) + buf.shape, buf.dtype)], 7, start, finish)


HBM = pl.BlockSpec(memory_space=pltpu.HBM)
SEM = pl.BlockSpec(memory_space=pltpu.SEMAPHORE)
DATAFLOW = pltpu.SideEffectType.DATAFLOW_SIDE_EFFECTING


def _split_start(name, comm, lands, after):
    srcs = [pltpu.with_memory_space_constraint(s, pltpu.HBM) for s in comm.args]
    lands = [pltpu.with_memory_space_constraint(b, pltpu.HBM) for b in lands]
    ns, nb = len(srcs), len(lands)

    def body(*refs):
        send_sems, recv_sems = refs[ns + nb + 1], refs[ns + nb + 2]
        comm.start(refs[:ns], refs[ns:ns + nb], send_sems, recv_sems)
        refs[-1][...] = jnp.zeros_like(refs[-1])

    res = pl.pallas_call(
        body, name=name,
        out_shape=(pltpu.SemaphoreType.DMA((comm.n_sems,)), pltpu.SemaphoreType.DMA((comm.n_sems,)),
                   *[pltpu.HBM(b.shape, b.dtype) for b in srcs + lands], _sds((8, 128), F32)),
        in_specs=[HBM] * (ns + nb) + [ANY],
        out_specs=(SEM, SEM, *[HBM] * (ns + nb), pl.BlockSpec(memory_space=pltpu.VMEM)),
        input_output_aliases={i: 2 + i for i in range(ns + nb)},
        compiler_params=pltpu.CompilerParams(has_side_effects=DATAFLOW),
    )(*srcs, *lands, after)
    return res[0], res[1], list(res[2:2 + ns]), list(res[2 + ns:2 + ns + nb]), res[-1]


def _split_starts(name, comms, after):
    srcs = [[pltpu.with_memory_space_constraint(s, pltpu.HBM) for s in c.args] for c in comms]
    lands = [[pltpu.with_memory_space_constraint(lax.empty(o.shape, o.dtype), pltpu.HBM) for o in c.out_shape]
             for c in comms]
    bufs = [b for k in range(len(comms)) for b in srcs[k] + lands[k]]
    nb, nc = len(bufs), len(comms)

    def body(*refs):
        sems = refs[nb + 1:nb + 1 + 2 * nc]
        off = 0
        for k, c in enumerate(comms):
            ns, nl = len(srcs[k]), len(lands[k])
            c.start(refs[off:off + ns], refs[off + ns:off + ns + nl], sems[2 * k], sems[2 * k + 1])
            off += ns + nl
        refs[-1][...] = jnp.zeros_like(refs[-1])

    res = pl.pallas_call(
        body, name=name,
        out_shape=(*[pltpu.SemaphoreType.DMA((c.n_sems,)) for c in comms for _ in range(2)],
                   *[pltpu.HBM(b.shape, b.dtype) for b in bufs], _sds((8, 128), F32)),
        in_specs=[HBM] * nb + [ANY],
        out_specs=(*[SEM] * (2 * nc), *[HBM] * nb, pl.BlockSpec(memory_space=pltpu.VMEM)),
        input_output_aliases={i: 2 * nc + i for i in range(nb)},
        compiler_params=pltpu.CompilerParams(has_side_effects=DATAFLOW),
    )(*bufs, after)
    states, off = [], 2 * nc
    for k in range(nc):
        ns, nl = len(srcs[k]), len(lands[k])
        states.append((res[2 * k], res[2 * k + 1], list(res[off:off + ns]), list(res[off + ns:off + ns + nl])))
        off += ns + nl
    return states, res[-1]


def _split_wait(name, comm, send_sems, recv_sems, srcs, lands, after):
    ns, nb = len(srcs), len(lands)
    after = list(after) if isinstance(after, (list, tuple)) else [after]

    def body(*refs):
        comm.finish(refs[:ns], refs[ns:ns + nb], refs[ns + nb], refs[ns + nb + 1])

    res = pl.pallas_call(
        body, name=name,
        out_shape=tuple(pltpu.HBM(b.shape, b.dtype) for b in srcs + lands),
        in_specs=[HBM] * (ns + nb) + [SEM, SEM] + [ANY] * len(after), out_specs=tuple([HBM] * (ns + nb)),
        input_output_aliases={i: i for i in range(ns + nb)},
        compiler_params=pltpu.CompilerParams(has_side_effects=DATAFLOW),
    )(*srcs, *lands, send_sems, recv_sems, *after)
    return list(res[ns:])


LOSS_ROW = "loss_cols"
SMALL_EARLY = ("post_mix_norm", "pre_ffn_norm", "post_ffn_norm", "sgu_ln_gain", "sgu_ln_bias", "attn_out_norm",
               "sgu_out_norm", "sgu_w_spatial", "sgu_b_spatial", LOSS_ROW)
SMALL_LATE = ("pre_mix_norm",)


def _pack(d, names, rows):
    flat = [d[n].reshape(-1) for n in names]
    used = sum(f.shape[0] for f in flat)
    flat.append(jnp.zeros((rows * 1024 - used,), F32))
    return jnp.concatenate(flat).reshape(rows, 1024)


def _unpack(buf, names, shapes):
    flat = buf.reshape(-1)
    out, off = {}, 0
    for n in names:
        size = int(np.prod(shapes[n]))
        out[n] = flat[off:off + size].reshape(shapes[n])
        off += size
    return out


def _merge(comms):
    def run(phase):
        def go(ins, outs, ss, rs):
            ii = oi = si = 0
            for c in comms:
                getattr(c, phase)(ins[ii:ii + len(c.args)], outs[oi:oi + len(c.out_shape)],
                                  ss.at[pl.ds(si, c.n_sems)], rs.at[pl.ds(si, c.n_sems)])
                ii += len(c.args)
                oi += len(c.out_shape)
                si += c.n_sems
        return go

    return _Comm([a for c in comms for a in c.args], [o for c in comms for o in c.out_shape],
                 sum(c.n_sems for c in comms), run("start"), run("finish"))


def _chip_sums(gws, recvs, shard_core):
    n = len(gws)
    _, rows, cols = gws[0].shape
    hw = rows // 2

    def body(sc_ref, *refs):
        for k in range(n):
            refs[2 * n + k][...] = (refs[k][...] + refs[n + k][...]).astype(BF16)

    def other(s, sc):
        return jnp.where(s >= sc[0], s + 1, s)

    mine = pl.BlockSpec((1, hw, cols), lambda s, sc: (other(s, sc), sc[1], 0))
    plain = pl.BlockSpec((1, hw, cols), lambda s, sc: (other(s, sc), 0, 0))
    return pl.pallas_call(
        body, name="rs_chip_sums",
        grid_spec=pltpu.PrefetchScalarGridSpec(num_scalar_prefetch=1, grid=(N_SHARD - 1,),
                                               in_specs=[mine] * n + [plain] * n, out_specs=[plain] * n),
        out_shape=[_sds((N_SHARD, hw, cols), BF16)] * n,
        compiler_params=_seq_params(),
    )(shard_core, *gws, *recvs)


def _final_sums(gws, recv_sibs, recv_chips, shard_core):
    n = len(gws)
    _, rows, cols = gws[0].shape
    hw = rows // 2

    def body(sc_ref, *refs):
        for k in range(n):
            acc = refs[k][0] + refs[n + k][0]
            for j in range(3):
                acc = acc + refs[2 * n + k][j].astype(F32)
            refs[3 * n + k][0] = acc

    return pl.pallas_call(
        body, name="rs_final_sums",
        grid_spec=pltpu.PrefetchScalarGridSpec(
            num_scalar_prefetch=1, grid=(1,),
            in_specs=[pl.BlockSpec((1, hw, cols), lambda i, sc: (sc[0], sc[1], 0))] * n
            + [pl.BlockSpec((1, hw, cols), lambda i, sc: (sc[0], 0, 0))] * n
            + [pl.BlockSpec((3, hw, cols), lambda i, sc: (0, 0, 0))] * n,
            out_specs=[pl.BlockSpec((1, hw, cols), lambda i, sc: (sc[1], 0, 0))] * n),
        out_shape=[_sds((2, hw, cols), F32)] * n,
        compiler_params=_seq_params(),
    )(shard_core, *gws, *recv_sibs, *recv_chips)


def _adamw_multi(ws, gs, ms, vs, block_rows, name, after=()):
    n = len(ws)
    rows, cols = ws[0].shape

    def body(*refs):
        outs = refs[4 * n + len(after):]
        for k in range(n):
            g = refs[n + k][...]
            d, m, v = _adam_math(refs[k][...], g, refs[2 * n + k][...], refs[3 * n + k][...])
            outs[4 * k][...], outs[4 * k + 1][...], outs[4 * k + 2][...], outs[4 * k + 3][...] = g, d, m, v

    spec = pl.BlockSpec((block_rows, cols), lambda i: (i, 0))
    res = pl.pallas_call(
        body, name=name, grid=(rows // block_rows,), in_specs=[spec] * (4 * n) + [ANY] * len(after),
        out_specs=[spec] * (4 * n), out_shape=[_sds((rows, cols), F32)] * (4 * n),
        compiler_params=_seq_params(),
    )(*ws, *gs, *ms, *vs, *after)
    return [tuple(res[4 * k:4 * k + 4]) for k in range(n)]


def _wgrad_ff(a_list, b, name, comms=()):
    n = len(a_list)
    cols = 256

    def body(*refs):
        bv = refs[n][...]
        for k in range(n):
            refs[n + 1 + k][...] = _dot_tn(refs[k][...], bv)

    res = _pcall(
        body, name=name, grid=(FF // cols,),
        in_specs=[pl.BlockSpec((SEQ, cols), lambda j: (0, j))] * n
        + [pl.BlockSpec((SEQ, D_MODEL), lambda j: (0, 0), pipeline_mode=pl.Buffered(1))],
        out_specs=[pl.BlockSpec((cols, D_MODEL), lambda j: (j, 0))] * n,
        out_shape=[_sds((FF, D_MODEL), F32)] * n, args=(*a_list, b), comms=comms)
    mine, theirs = res if comms else (res, None)
    mine = [m.reshape(N_SHARD, FF_S, D_MODEL) for m in mine]
    return (mine, theirs) if comms else mine


def _wgrad_pair(a1, a2, b, a_spec, b_spec, out_block, name, comms=()):
    def body(a1_ref, a2_ref, b_ref, o1_ref, o2_ref):
        bv = b_ref[...]
        o1_ref[0] = _dot_tn(a1_ref[0], bv)
        o2_ref[0] = _dot_tn(a2_ref[0], bv)

    out_spec = pl.BlockSpec((1,) + out_block, lambda s: (s, 0, 0))
    return _pcall(
        body, name=name, grid=(N_SHARD,), in_specs=[a_spec, a_spec, b_spec], out_specs=[out_spec, out_spec],
        out_shape=[_sds((N_SHARD,) + out_block, F32)] * 2, args=(a1, a2, b), comms=comms)


def _comm_only(name, comms):
    return _pcall(lambda: None, name=name, grid=(1,), in_specs=[], out_specs=[], out_shape=[], args=(),
                  comms=comms)[1]


def _adam_math(w, g, m, v):
    m = ADAM_B1 * m + (1.0 - ADAM_B1) * g
    v = ADAM_B2 * v + (1.0 - ADAM_B2) * (g * g)
    m_hat = m / (1.0 - ADAM_B1 ** ADAM_STEP)
    v_hat = v / (1.0 - ADAM_B2 ** ADAM_STEP)
    return -ADAM_LR * (m_hat / (jnp.sqrt(v_hat) + ADAM_EPS) + ADAM_WD * w), m, v


def _adamw_small(own, slots, w, m, v, me):
    rows, cols = own.shape

    def body(me_ref, own_ref, slots_ref, w_ref, m_ref, v_ref, g_ref, d_ref, nm_ref, nv_ref):
        own_v = own_ref[...]
        g = jnp.where(me_ref[0] == 0, own_v, slots_ref[0])
        for i in range(1, 8):
            g = g + jnp.where(me_ref[0] == i, own_v, slots_ref[i])
        g_ref[...] = g
        d_ref[...], nm_ref[...], nv_ref[...] = _adam_math(w_ref[...], g, m_ref[...], v_ref[...])

    flat = pl.BlockSpec((rows, cols), lambda i, me_ref: (0, 0))
    return pl.pallas_call(
        body, name="adamw_small",
        grid_spec=pltpu.PrefetchScalarGridSpec(
            num_scalar_prefetch=1, grid=(1,),
            in_specs=[flat, pl.BlockSpec((8, rows, cols), lambda i, me_ref: (0, 0, 0)), flat, flat, flat],
            out_specs=[flat] * 4),
        out_shape=[_sds((rows, cols), F32)] * 4,
        compiler_params=_seq_params(),
    )(me, own, slots, w, m, v)


def kernel(x, positions, pre_mix_norm, w_in, sgu_ln_gain, sgu_ln_bias, sgu_w_spatial, sgu_b_spatial, attn_out_norm, sgu_out_norm, w_out, post_mix_norm, pre_ffn_norm, w_gate, w_up, w_down, post_ffn_norm, loss_target, m_pre_mix_norm, m_w_in, m_sgu_ln_gain, m_sgu_ln_bias, m_sgu_w_spatial, m_sgu_b_spatial, m_attn_out_norm, m_sgu_out_norm, m_w_out, m_post_mix_norm, m_pre_ffn_norm, m_w_gate, m_w_up, m_w_down, m_post_ffn_norm, v_pre_mix_norm, v_w_in, v_sgu_ln_gain, v_sgu_ln_bias, v_sgu_w_spatial, v_sgu_b_spatial, v_attn_out_norm, v_sgu_out_norm, v_w_out, v_post_mix_norm, v_pre_ffn_norm, v_w_gate, v_w_up, v_w_down, v_post_ffn_norm):
    a = dict(locals())
    cx, cy, cc = _coords()
    s_me = 2 * cx + cy
    core = jnp.stack([cc]).astype(jnp.int32)
    shard_core = jnp.stack([s_me, cc]).astype(jnp.int32)
    me = jnp.stack([4 * cx + 2 * cy + cc]).astype(jnp.int32)
    small = {n: (a[n][0] if a[n].ndim > 2 else a[n]) for n in SMALL}
    b_t = small["sgu_b_spatial"].T
    xs, pos, target = x[0], positions.reshape(SEQ, 1), loss_target[0]
    flipped = ("w_gate", "w_up")

    def big(name, n):
        return jnp.swapaxes(a[name], 1, 2)[0] if n in flipped else a[name][0]

    own = {"w_in": _halves(big("w_in", "w_in").astype(BF16))}

    def with_own(full, n):
        full = lax.dynamic_update_slice(full, own[n][None], (s_me, 0, 0, 0))
        return full.reshape((N_SHARD,) + big(n, n).shape)

    ffn = ("w_gate", "w_up", "w_down")
    g_in = _gather_ici([own["w_in"]])
    (s_in,), token = _split_starts("gather_in_start", [g_in], small["pre_mix_norm"])
    for n in ("w_out",) + ffn:
        own[n] = _halves((big(n, n) + token[0:1, 0:1]).astype(BF16))
    g_out, g_ffn = _gather_ici([own["w_out"]]), _gather_ici([own[n] for n in ffn])
    (s_out, s_ffn), token = _split_starts("gather_rest_start", [g_out, g_ffn], token)
    in_lands = _split_wait("gather_in_wait", g_in, *s_in, token)
    ((in_lands,),) = _comm_only("comm_pass_in", [_gather_pass(in_lands)])
    w_in_f = with_own(in_lands, "w_in")
    h, u, vs, sgu, *qkv = _inproj_fwd(xs, pos, small["pre_mix_norm"], w_in_f, small["sgu_ln_gain"],
                                      small["sgu_ln_bias"], small["sgu_w_spatial"], b_t)
    views = [tuple(qkv[3 * i:3 * i + 3]) for i in range(len(DILATIONS))]
    out_lands = _split_wait("gather_out_wait", g_out, *s_out, sgu)
    o_list, l_list = [], []
    for dil, (qv, kv, vv) in zip(DILATIONS, views):
        if dil == 1:
            (o, l), ((out_lands,),) = _attn_fwd(qv, kv, vv, dil, comms=[_gather_pass(out_lands)])
        else:
            o, l = _attn_fwd(qv, kv, vv, dil)
        o_list.append(o)
        l_list.append(l)
    w_out_f = with_own(out_lands, "w_out")
    ffn_lands = _split_wait("gather_ffn_wait", g_ffn, *s_ffn, l_list[-1])
    (attn, mixed, y, x1, *lses), (ffn_lands,) = _mix_out_fwd(
        o_list, l_list, sgu, xs, w_out_f, small["attn_out_norm"], small["sgu_out_norm"], small["post_mix_norm"],
        comms=[_gather_pass(ffn_lands)])
    w_gate_f, w_up_f, w_down_f = (with_own(f, n).reshape(FF, D_MODEL) for f, n in zip(ffn_lands, ffn))
    h2, act, dg, dup, df, dx1, loss_cols, d_pre_ffn, d_post_ffn = _ffn_fwd_bwd(
        x1, target, w_gate_f, w_up_f, w_down_f, small["pre_ffn_norm"], small["post_ffn_norm"])

    full_tok = pl.BlockSpec((SEQ, D_MODEL), lambda s: (0, 0), pipeline_mode=pl.Buffered(1))
    ff_tok = pl.BlockSpec((1, SEQ, FF_S), lambda s: (s, 0, 0))
    gw = {}
    gw["w_gate"], gw["w_up"] = _wgrad_ff([dg, dup], h2, "wgrad_gate_up")
    (gw["w_down"],), ((sib_gate,),) = _wgrad_ff([act], df, "wgrad_down", comms=[_rs_sibling([gw["w_gate"]])])
    (dy, dsgu, d_post_mix, d_attn_norm, d_sgu_norm, *dviews), ((sib_up, sib_down),) = _outproj_bwd(
        dx1, y, attn, sgu, w_out_f, small["post_mix_norm"], small["attn_out_norm"],
        small["sgu_out_norm"], comms=[_rs_sibling([gw["w_up"], gw["w_down"]])])
    sib = {"w_gate": sib_gate, "w_up": sib_up, "w_down": sib_down}
    part = dict(zip(ffn, _chip_sums([gw[n] for n in ffn], [sib[n] for n in ffn], shard_core)))
    gw["w_out"] = _wgrad(mixed, dy, pl.BlockSpec((SEQ, OUT_S), lambda s: (0, s)), full_tok, (OUT_S, D_MODEL),
                         "wgrad_out")
    x_ffn = _rs_chips([part[n] for n in ffn])
    (s_ffn,), token = _split_starts("rs_ffn_start", [x_ffn], small["pre_mix_norm"])
    (du, dvs_sgu, d_w_sp, d_b_sp, d_ln_gain, d_ln_bias), ((sib["w_out"],),) = _sgu_bwd(
        u, vs, dsgu, small["sgu_ln_gain"], small["sgu_ln_bias"], small["sgu_w_spatial"], b_t,
        comms=[_rs_sibling([gw["w_out"]])], after=[token])
    (part["w_out"],) = _chip_sums([gw["w_out"]], [sib["w_out"]], shard_core)
    packed_early = _pack({
        "sgu_ln_gain": d_ln_gain, "sgu_ln_bias": d_ln_bias, "sgu_w_spatial": d_w_sp, "sgu_b_spatial": d_b_sp,
        "attn_out_norm": d_attn_norm, "sgu_out_norm": d_sgu_norm, "post_mix_norm": d_post_mix,
        "pre_ffn_norm": d_pre_ffn, "post_ffn_norm": d_post_ffn, LOSS_ROW: loss_cols}, SMALL_EARLY, SMALL_ROWS)
    x_out, x_small = _rs_chips([part["w_out"]]), _small_exchange(packed_early)
    (s_out, s_small), token = _split_starts("rs_out_small_start", [x_out, x_small], token)

    dqs, dks, dvs = [], [], []
    for i, (dil, (qv, kv, vv)) in enumerate(zip(DILATIONS, views)):
        dq, dk, dv = _attn_bwd(qv, kv, vv, dviews[i], dviews[3 + i], lses[i], dil, after=[token])
        dqs.append(dq)
        dks.append(dk)
        dvs.append(dv)
    half, far, joined = {}, {}, {}
    far.update(zip(ffn, _split_wait("rs_ffn_wait", x_ffn, *s_ffn, dvs[-1])))
    half.update(zip(ffn, _final_sums([gw[n] for n in ffn], [sib[n] for n in ffn], [far[n] for n in ffn],
                                     shard_core)))
    dproj, grad_x, d_pre_mix = _inproj_bwd(dqs, dks, dvs, du, dvs_sgu, pos, xs, dx1, w_in_f, small["pre_mix_norm"])
    (far["w_out"],) = _split_wait("rs_out_wait", x_out, *s_out, grad_x)
    half["w_out"] = _rs_final_sum(gw["w_out"], sib["w_out"], far["w_out"], shard_core)
    packed_late = _pack({"pre_mix_norm": d_pre_mix}, SMALL_LATE, 8)
    names = ffn + ("w_out",)
    gw["w_in"], (got, (slots_late,)) = _wgrad(
        h, dproj, full_tok, pl.BlockSpec((SEQ, IN_S), lambda s: (0, s)), (D_MODEL, IN_S), "wgrad_in",
        comms=[_rs_join([half[n] for n in names]), _small_exchange(packed_late)])
    joined.update(zip(names, got))
    ((sib["w_in"],),) = _comm_only("comm_rs_sibling_in", [_rs_sibling([gw["w_in"]])])
    (part["w_in"],) = _chip_sums([gw["w_in"]], [sib["w_in"]], shard_core)
    x_in = _rs_chips([part["w_in"]])
    (s_in,), token = _split_starts("rs_in_start", [x_in], small["pre_mix_norm"])

    grads, deltas, new_m, new_v = {}, {}, {}, {}

    def record(n, outs):
        grads[n], deltas[n], new_m[n], new_v[n] = (
            jnp.swapaxes(o[None], 1, 2) if n in flipped else o[None] for o in outs)

    def update(names, block_rows, name, after):
        for n, outs in zip(names, _adamw_multi(
                [big(n, n) for n in names], [joined[n].reshape(big(n, n).shape) for n in names],
                [big("m_" + n, n) for n in names], [big("v_" + n, n) for n in names], block_rows, name, after)):
            record(n, outs)

    update(ffn, FF_S // 4, "adamw_ffn", [token])
    update(("w_out",), BIG_ADAM_ROWS["w_out"], "adamw_w_out", [token])
    (slots_early,) = _split_wait("small_early_wait", x_small, *s_small, [new_v[n] for n in ("w_down", "w_out")])
    (far["w_in"],) = _split_wait("rs_in_wait", x_in, *s_in, slots_early)
    half["w_in"] = _rs_final_sum(gw["w_in"], sib["w_in"], far["w_in"], shard_core)
    ((joined["w_in"],),) = _comm_only("comm_rs_join_in", [_rs_join([half["w_in"]])])
    update(("w_in",), BIG_ADAM_ROWS["w_in"], "adamw_w_in", [])
    a[LOSS_ROW] = a["m_" + LOSS_ROW] = a["v_" + LOSS_ROW] = jnp.zeros((1, D_MODEL), F32)
    for names, rows, packed, slots in ((SMALL_EARLY, SMALL_ROWS, packed_early, slots_early),
                                       (SMALL_LATE, 8, packed_late, slots_late)):
        outs = _adamw_small(packed, slots, _pack(a, names, rows), _pack({n: a["m_" + n] for n in names}, names, rows),
                            _pack({n: a["v_" + n] for n in names}, names, rows), me)
        for dst, buf in zip((grads, deltas, new_m, new_v), outs):
            dst.update(_unpack(buf, names, {n: a[n].shape for n in names}))
    loss = jnp.sum(grads[LOSS_ROW]) * np.float32(0.5 / D_MODEL)
    return (loss, grad_x[None], *[grads[n] for n in WEIGHTS], *[deltas[n] for n in WEIGHTS],
            *[new_m[n] for n in WEIGHTS], *[new_v[n] for n in WEIGHTS])
```

```python
import numpy as np
import jax
import jax.numpy as jnp
from jax import lax
from jax.experimental import pallas as pl
from jax.experimental.pallas import tpu as pltpu

F32 = jnp.float32
BF16 = jnp.bfloat16

SEQ = 2048
D_MODEL = 1024
HEAD_DIM = 64
ATTN_W = 512
SGU_W = 512
SGU_GROUPS = 8
CHUNK = 128
DILATIONS = (1, 4, 16)
N_SHARD = 4
IN_S = 640
OUT_S = 256
FF_S = 704
PROJ_W = N_SHARD * IN_S
FF = N_SHARD * FF_S
FF_CHUNKS = ((0, 1024), (1024, 2048), (2048, FF))
RMS_EPS = 1e-6
LN_EPS = 1e-5
ROPE_THETA = 500000.0
ATTN_SCALE = 1.0 / np.sqrt(HEAD_DIM)
NEG = -1e30
TM = 512
TM_FFN = 256
VMEM_LIMIT = 56 * 1024 * 1024
SMALL_ROWS = 136

ADAM_LR = 0.001
ADAM_B1 = 0.9
ADAM_B2 = 0.999
ADAM_EPS = 1e-08
ADAM_WD = 0.01
ADAM_STEP = 10

MESH = pl.DeviceIdType.MESH
ANY = pl.BlockSpec(memory_space=pl.ANY)


def _dot(a, b):
    return jnp.dot(a, b, preferred_element_type=F32)


def _dot_nt(a, b):
    return lax.dot_general(a, b, (((1,), (1,)), ((), ())), preferred_element_type=F32)


def _dot_tn(a, b):
    return lax.dot_general(a, b, (((0,), (0,)), ((), ())), preferred_element_type=F32)


def _dot_exact(a, b):
    return jnp.dot(a, b, preferred_element_type=F32, precision=lax.Precision.HIGHEST)


def _dot_select(a, sel):
    hi = a.astype(BF16)
    lo = (a - hi.astype(F32)).astype(BF16)
    sel = sel.astype(BF16)
    return _dot(hi, sel) + _dot(lo, sel)


def _rms_stats(x):
    r = lax.rsqrt(jnp.mean(x * x, axis=-1, keepdims=True) + RMS_EPS)
    return x * r, r


def _rms_bwd(xh, r, gain, dy):
    dxh = dy * gain
    dx = r * (dxh - xh * jnp.mean(dxh * xh, axis=-1, keepdims=True))
    return dx, jnp.sum(dy * xh, axis=0, keepdims=True)


_ERF_ALPHA = (-2.72614225801306e-10, 2.77068142495902e-08, -2.10102402082508e-06, -5.69250639462346e-05,
              -7.34990630326855e-04, -2.95459980854025e-03, -1.60960333262415e-02)
_ERF_BETA = (-1.45660718464996e-05, -2.13374055278905e-04, -1.68282697438203e-03, -7.37332916720468e-03,
             -1.42647390514189e-02)


def _erf(x):
    x = jnp.clip(x, -4.0, 4.0)
    x2 = x * x
    p = jnp.full_like(x, _ERF_ALPHA[0])
    for a in _ERF_ALPHA[1:]:
        p = p * x2 + a
    q = jnp.full_like(x, _ERF_BETA[0])
    for b in _ERF_BETA[1:]:
        q = q * x2 + b
    return x * p / q


def _normal_cdf(x):
    return 0.5 * (1.0 + _erf(x * np.float32(1.0 / np.sqrt(2.0))))


def _gelu_grad(x, cdf):
    pdf = jnp.exp(-0.5 * x * x) * np.float32(1.0 / np.sqrt(2.0 * np.pi))
    return cdf + x * pdf


def _sigmoid(x):
    return 1.0 / (1.0 + jnp.exp(-x))


_INV_FREQ = tuple(float(np.float32(ROPE_THETA ** (-2.0 * j / 16.0))) for j in range(8))


def _rot_tables(pos):
    lane = lax.broadcasted_iota(jnp.int32, (1, 128), 1)
    d = lane & 63
    j = d & 7
    inv = jnp.zeros((1, 128), F32)
    for jj in range(8):
        inv = jnp.where(j == jj, _INV_FREQ[jj], inv)
    ang = pos.astype(F32) * inv
    c = jnp.cos(ang)
    s = jnp.sin(ang)
    cos_t = jnp.where(d < 16, c, 1.0)
    sin_a = jnp.where(d < 8, -s, 0.0)
    sin_b = jnp.where((d >= 8) & (d < 16), s, 0.0)
    return tuple(jnp.tile(t, (1, 4)) for t in (cos_t, sin_a, sin_b))


def _rope(x, tabs):
    cos_t, sin_a, sin_b = tabs
    return x * cos_t + pltpu.roll(x, 504, 1) * sin_a + pltpu.roll(x, 8, 1) * sin_b


def _rope_bwd(dy, tabs):
    cos_t, sin_a, sin_b = tabs
    return dy * cos_t + pltpu.roll(dy * sin_a, 8, 1) + pltpu.roll(dy * sin_b, 504, 1)


def _left_half():
    return lax.broadcasted_iota(jnp.int32, (CHUNK, CHUNK), 1) < HEAD_DIM


def _group_ones():
    lane = lax.broadcasted_iota(jnp.int32, (SGU_GROUPS, SGU_W), 1)
    row = lax.broadcasted_iota(jnp.int32, (SGU_GROUPS, SGU_W), 0)
    return ((lane >> 6) == row).astype(F32)


def _masked_spatial(w_ref):
    row = lax.broadcasted_iota(jnp.int32, (CHUNK, CHUNK), 0)
    col = lax.broadcasted_iota(jnp.int32, (CHUNK, CHUNK), 1)
    return [jnp.where(col <= row, w_ref[g], 0.0).astype(BF16) for g in range(SGU_GROUPS)]


def _sgu_core(u, vs, lg, lb, wm, bias_full):
    tm = u.shape[0]
    cdf_u, cdf_vs = _normal_cdf(u), _normal_cdf(vs)
    gu = u * cdf_u
    gv = vs * cdf_vs
    mu = jnp.mean(gv, axis=-1, keepdims=True)
    xc = gv - mu
    rstd = lax.rsqrt(jnp.mean(xc * xc, axis=-1, keepdims=True) + LN_EPS)
    xh = xc * rstd
    vnb = (xh * lg + lb).astype(BF16)
    left = _left_half()
    rows = []
    for c in range(tm // CHUNK):
        pieces = []
        for p in range(4):
            vp = vnb[c * CHUNK:(c + 1) * CHUNK, p * 128:(p + 1) * 128]
            pieces.append(jnp.where(left, _dot(wm[2 * p], vp), _dot(wm[2 * p + 1], vp)))
        rows.append(jnp.concatenate(pieces, axis=1) + bias_full)
    mixed = jnp.concatenate(rows, axis=0)
    return gu, xh, rstd, vnb, mixed, cdf_u, cdf_vs


def _resident(shape):
    n = len(shape)
    return pl.BlockSpec(shape, lambda *_: (0,) * n, pipeline_mode=pl.Buffered(1))


def _rows(ncol, tm=TM):
    return pl.BlockSpec((tm, ncol), lambda i: (i, 0))


def _rows3(nlead, ncol, tm=TM):
    return pl.BlockSpec((nlead, tm, ncol), lambda i: (0, i, 0))


def _acc(ncol, nrow=1):
    return pl.BlockSpec((nrow, ncol), lambda i: (0, 0))


HEAD_W = 128


def _view_rows(dil, width=ATTN_W, tm=TM):
    return pl.BlockSpec((tm // dil, dil * width), lambda i: (i, 0))


def _view_shape(dil, dtype, width=ATTN_W):
    return _sds((SEQ // dil, dil * width), dtype)


def _slab_scratch():
    return pltpu.VMEM((4, TM, 128), F32)


def _store_view(val, out_ref, slabs, dil):
    width = val.shape[1]
    for j in range(width // 128):
        slabs[j] = val[:, j * 128:(j + 1) * 128]
    for r in range(dil):
        for j in range(width // 128):
            c0 = r * width + j * 128
            out_ref[:, c0:c0 + 128] = slabs.at[j][pl.ds(r, TM // dil, stride=dil), :].astype(out_ref.dtype)


def _load_view(in_ref, slabs, dil, width=ATTN_W):
    for r in range(dil):
        for j in range(width // 128):
            c0 = r * width + j * 128
            slabs.at[j][pl.ds(r, TM // dil, stride=dil), :] = in_ref[:, c0:c0 + 128].astype(F32)
    return jnp.concatenate([slabs[j] for j in range(width // 128)], axis=1)


def _head_spread():
    m = lax.broadcasted_iota(jnp.int32, (HEAD_W, ATTN_W), 0)
    lane = lax.broadcasted_iota(jnp.int32, (HEAD_W, ATTN_W), 1)
    return (m == 16 * (lane >> 6)).astype(F32)


def _head_sum():
    lane = lax.broadcasted_iota(jnp.int32, (ATTN_W, HEAD_W), 0)
    m = lax.broadcasted_iota(jnp.int32, (ATTN_W, HEAD_W), 1)
    return ((lane >> 6) == (m >> 4)).astype(F32)


def _seq_params():
    return pltpu.CompilerParams(dimension_semantics=("arbitrary",), vmem_limit_bytes=VMEM_LIMIT)


def _sds(shape, dtype):
    return jax.ShapeDtypeStruct(shape, dtype)


class _Comm:
    def __init__(self, args, out_shape, n_sems, start, finish, aliased=False):
        self.args, self.out_shape, self.n_sems = list(args), list(out_shape), n_sems
        self.start, self.finish, self.aliased = start, finish, aliased


def _pcall(body, *, name, grid, in_specs, out_specs, out_shape, args, scratch_shapes=(), comms=(), after=()):
    single = not isinstance(out_shape, (list, tuple))
    out_specs = [out_specs] if single else list(out_specs)
    out_shape = [out_shape] if single else list(out_shape)
    n_in, n_out, n_scr = len(in_specs), len(out_shape), len(scratch_shapes)
    c_args = [a for c in comms for a in c.args]
    c_outs = [o for c in comms for o in c.out_shape]
    aliases, ai, ao = {}, n_in, n_out
    for c in comms:
        if c.aliased:
            aliases.update({ai + k: ao + k for k in range(len(c.args))})
        ai += len(c.args)
        ao += len(c.out_shape)
    sems = [pltpu.SemaphoreType.DMA((c.n_sems,)) for c in comms for _ in range(2)]
    steps = grid[0]

    def wrapped(*refs):
        o0 = n_in + len(c_args) + len(after)
        s0 = o0 + n_out + len(c_outs)
        m_in, m_out, m_sem = refs[n_in:n_in + len(c_args)], refs[o0 + n_out:s0], refs[s0 + n_scr:]

        def each(phase):
            ii = oi = 0
            for k, c in enumerate(comms):
                getattr(c, phase)(m_in[ii:ii + len(c.args)], m_out[oi:oi + len(c.out_shape)],
                                  m_sem[2 * k], m_sem[2 * k + 1])
                ii += len(c.args)
                oi += len(c.out_shape)

        if comms:
            @pl.when(pl.program_id(0) == 0)
            def _():
                each("start")

        body(*refs[:n_in], *refs[o0:o0 + n_out], *refs[s0:s0 + n_scr])

        if comms:
            @pl.when(pl.program_id(0) == steps - 1)
            def _():
                each("finish")

    res = pl.pallas_call(
        wrapped, name=name, grid=grid,
        in_specs=list(in_specs) + [ANY] * (len(c_args) + len(after)), out_specs=out_specs + [ANY] * len(c_outs),
        out_shape=out_shape + c_outs, scratch_shapes=list(scratch_shapes) + sems,
        input_output_aliases=aliases, compiler_params=_seq_params(),
    )(*args, *c_args, *after)
    mine = res[0] if single else list(res[:n_out])
    if not comms:
        return mine
    theirs, oi = [], n_out
    for c in comms:
        theirs.append(list(res[oi:oi + len(c.out_shape)]))
        oi += len(c.out_shape)
    return mine, theirs


def _in_pieces(g):
    lo, hi = 512 * g, 512 * (g + 1)
    return [(s, max(lo, IN_S * s) - IN_S * s, min(hi, IN_S * (s + 1)) - IN_S * s)
            for s in range(N_SHARD) if max(lo, IN_S * s) < min(hi, IN_S * (s + 1))]


def _inproj_fwd(x, pos, g_pre, w_in, lg, lb, w_sp, b_t, comms=()):
    def body(x_ref, pos_ref, g_ref, w_ref, lg_ref, lb_ref, wsp_ref, bt_ref, h_ref, u_ref, vs_ref, sgu_ref, *rest):
        qkv_refs, slabs = rest[:9], rest[9:]
        xh, _ = _rms_stats(x_ref[...])
        h = (xh * g_ref[...]).astype(BF16)
        h_ref[...] = h
        tabs = _rot_tables(pos_ref[...])

        def group(g):
            return jnp.concatenate([_dot(h, w_ref[s, :, a:b]) for s, a, b in _in_pieces(g)], axis=1)

        for t in range(3):
            val = group(t)
            if t < 2:
                val = _rope(val, tabs)
            if t == 0:
                val = val * np.float32(ATTN_SCALE)
            qkv_refs[t][...] = val.astype(BF16)
            for i, dil in enumerate(DILATIONS[1:]):
                _store_view(val, qkv_refs[3 * (i + 1) + t], slabs[t], dil)
        u = group(3)
        vs = group(4)
        u_ref[...] = u
        vs_ref[...] = vs
        bias_full = _dot_exact(bt_ref[...], _group_ones())
        gu, _, _, _, mixed, _, _ = _sgu_core(u, vs, lg_ref[...], lb_ref[...], _masked_spatial(wsp_ref), bias_full)
        sgu_ref[...] = gu * mixed

    return _pcall(
        body, name="inproj_sgu_fwd", grid=(SEQ // TM,),
        in_specs=[_rows(D_MODEL), _rows(1), _resident((1, D_MODEL)), _resident((N_SHARD, D_MODEL, IN_S)),
                  _resident((1, SGU_W)), _resident((1, SGU_W)), _resident((SGU_GROUPS, CHUNK, CHUNK)),
                  _resident((CHUNK, SGU_GROUPS))],
        out_specs=[_rows(D_MODEL), _rows(512), _rows(512), _rows(512)]
        + [_view_rows(dil) for dil in DILATIONS for _ in range(3)],
        out_shape=[_sds((SEQ, D_MODEL), BF16), _sds((SEQ, 512), F32), _sds((SEQ, 512), F32), _sds((SEQ, 512), F32)]
        + [_view_shape(dil, BF16) for dil in DILATIONS for _ in range(3)],
        scratch_shapes=[_slab_scratch() for _ in range(3)],
        args=(x, pos, g_pre, w_in, lg, lb, w_sp, b_t), comms=comms)


def _sgu_fwd(u, vs, lg, lb, w_sp, b_t, comms=()):
    def body(u_ref, vs_ref, lg_ref, lb_ref, w_ref, bt_ref, out_ref):
        wm = _masked_spatial(w_ref)
        bias_full = _dot_exact(bt_ref[...], _group_ones())
        gu, _, _, _, mixed, _, _ = _sgu_core(u_ref[...], vs_ref[...], lg_ref[...], lb_ref[...], wm, bias_full)
        out_ref[...] = gu * mixed

    return _pcall(
        body, name="sgu_fwd", grid=(SEQ // TM,),
        in_specs=[_rows(SGU_W), _rows(SGU_W), _resident((1, SGU_W)), _resident((1, SGU_W)),
                  _resident((SGU_GROUPS, CHUNK, CHUNK)), _resident((CHUNK, SGU_GROUPS))],
        out_specs=_rows(SGU_W),
        out_shape=_sds((SEQ, SGU_W), F32),
        args=(u, vs, lg, lb, w_sp, b_t), comms=comms)


def _block_masks():
    row = lax.broadcasted_iota(jnp.int32, (CHUNK, CHUNK), 0)
    col = lax.broadcasted_iota(jnp.int32, (CHUNK, CHUNK), 1)
    return col <= row, col >= row


def _attn_fwd(qv, kv, vv, dil, comms=()):
    seg = SEQ // dil
    nblk = seg // CHUNK
    rps = 4 if nblk == 1 else 1

    def body(q_ref, k_ref, v_ref, o_ref, l_ref):
        left = _left_half()
        m_cur, m_prev = _block_masks()
        zero = jnp.zeros((CHUNK, CHUNK), BF16)
        ones = (jnp.where(left, 1.0, 0.0).astype(BF16), jnp.where(left, 0.0, 1.0).astype(BF16))

        sides = tuple(enumerate((left, ~left)))

        def rows(b):
            if isinstance(b, int):
                return b * CHUNK, max(b - 1, 0) * CHUNK
            return pl.multiple_of(b * CHUNK, CHUNK), pl.multiple_of(jnp.maximum(b - 1, 0) * CHUNK, CHUNK)

        def first(rr, b):
            r0, rp = rows(b)
            prev_ok = m_prev & (b > 0)
            tiles, scores = [], []
            for hp in range(4):
                ls = slice(rr * ATTN_W + hp * 128, rr * ATTN_W + (hp + 1) * 128)
                qp = q_ref[pl.ds(r0, CHUNK), ls]
                kc = k_ref[pl.ds(r0, CHUNK), ls]
                kp = k_ref[pl.ds(rp, CHUNK), ls] if nblk > 1 else None
                tiles.append((ls, v_ref[pl.ds(r0, CHUNK), ls], v_ref[pl.ds(rp, CHUNK), ls] if nblk > 1 else None))
                for _, hm in sides:
                    qh = jnp.where(hm, qp, zero)
                    sc = jnp.where(m_cur, _dot_nt(qh, kc), NEG)
                    sp = jnp.where(prev_ok, _dot_nt(qh, kp), NEG) if nblk > 1 else None
                    scores.append((sc, sp))
            return tiles, scores

        def second(scores):
            probs = []
            for sc, sp in scores:
                if nblk > 1:
                    m = jnp.max(jnp.maximum(sc, sp), axis=-1, keepdims=True)
                    pc = jnp.exp(sc - m)
                    pp = jnp.exp(sp - m)
                    probs.append((m, pc.astype(BF16), pp.astype(BF16), (pc + pp).astype(BF16)))
                else:
                    m = jnp.max(sc, axis=-1, keepdims=True)
                    pc = jnp.exp(sc - m).astype(BF16)
                    probs.append((m, pc, None, pc))
            return probs

        def third(rr, b, tiles, probs):
            r0, _ = rows(b)
            for hp, (ls, vc, vp) in enumerate(tiles):
                acc = jnp.zeros((CHUNK, CHUNK), F32)
                den = jnp.zeros((CHUNK, CHUNK), F32)
                for side, hm in sides:
                    _, pc, pp, psum = probs[2 * hp + side]
                    acc = acc + _dot(pc, jnp.where(hm, vc, zero))
                    if nblk > 1:
                        acc = acc + _dot(pp, jnp.where(hm, vp, zero))
                    den = den + _dot(psum, ones[side])
                o_ref[pl.ds(r0, CHUNK), ls] = (acc / den).astype(o_ref.dtype)
                lse = jnp.where(left, probs[2 * hp][0], probs[2 * hp + 1][0]) + jnp.log(den)
                l_ref[pl.ds(r0, CHUNK), rr * HEAD_W + 32 * hp:rr * HEAD_W + 32 * hp + 32] = lse[:, 48:80]

        def run(units):
            data = [first(rr, b) for rr, b in units]
            probs = [second(scores) for _, scores in data]
            for (rr, b), (tiles, _), pr in zip(units, data, probs):
                third(rr, b, tiles, pr)

        if nblk == 1:
            run([(rr, 0) for rr in range(rps)])
        else:
            def one(b, carry):
                run([(0, b)])
                return carry

            lax.fori_loop(0, nblk, one, 0)

    spec = pl.BlockSpec((seg, rps * ATTN_W), lambda r: (0, r))
    return _pcall(
        body, name=f"attn_fwd_d{dil}", grid=(dil // rps,),
        in_specs=[spec, spec, spec], out_specs=[spec, pl.BlockSpec((seg, rps * HEAD_W), lambda r: (0, r))],
        out_shape=[_sds((seg, dil * ATTN_W), BF16), _sds((seg, dil * HEAD_W), F32)],
        args=(qv, kv, vv), comms=comms)


def _lane_left(nrows):
    return lax.broadcasted_iota(jnp.int32, (nrows, CHUNK), 1) < HEAD_DIM


def _attn_rows(b):
    if isinstance(b, int):
        return b * CHUNK, max(b - 1, 0) * CHUNK
    return pl.multiple_of(b * CHUNK, CHUNK), pl.multiple_of(jnp.maximum(b - 1, 0) * CHUNK, CHUNK)


def _attn_fwd_fused(qv, kv, vv, dil, comms=()):
    seg = SEQ // dil
    nblk = seg // CHUNK
    rps = 4 if nblk == 1 else 1

    def body(q_ref, k_ref, v_ref, o_ref, l_ref):
        left = _left_half()
        m_cur, m_prev = _block_masks()
        sides = tuple(enumerate((left, ~left)))
        zero = jnp.zeros((CHUNK, CHUNK), BF16)

        def first(rr, b, both):
            r0, rp = _attn_rows(b)
            keys = pl.ds(rp, 2 * CHUNK) if both else pl.ds(r0, CHUNK)
            ok = jnp.concatenate([m_prev, m_cur], axis=1) if both else m_cur
            tiles, scores = [], []
            for hp in range(4):
                ls = slice(rr * ATTN_W + hp * 128, rr * ATTN_W + (hp + 1) * 128)
                qp = q_ref[pl.ds(r0, CHUNK), ls]
                k2 = k_ref[keys, ls]
                tiles.append((ls, v_ref[keys, ls]))
                for _, hm in sides:
                    scores.append(jnp.where(ok, _dot_nt(jnp.where(hm, qp, zero), k2), NEG))
            return tiles, scores

        def second(scores):
            probs = []
            for s in scores:
                m = jnp.max(s, axis=-1, keepdims=True)
                probs.append((m, jnp.exp(s - m).astype(BF16)))
            return probs

        def third(rr, b, tiles, probs, both):
            r0, _ = _attn_rows(b)
            nk = 2 * CHUNK if both else CHUNK
            left_k = _lane_left(nk)
            ones = (jnp.where(left_k, 1.0, 0.0).astype(BF16), jnp.where(left_k, 0.0, 1.0).astype(BF16))
            zero_k = jnp.zeros((nk, CHUNK), BF16)
            for hp, (ls, v2) in enumerate(tiles):
                acc = jnp.zeros((CHUNK, CHUNK), F32)
                den = jnp.zeros((CHUNK, CHUNK), F32)
                for side in range(2):
                    p = probs[2 * hp + side][1]
                    acc = acc + _dot(p, jnp.where(left_k if side == 0 else ~left_k, v2, zero_k))
                    den = den + _dot(p, ones[side])
                o_ref[pl.ds(r0, CHUNK), ls] = (acc / den).astype(o_ref.dtype)
                lse = jnp.where(left, probs[2 * hp][0], probs[2 * hp + 1][0]) + jnp.log(den)
                l_ref[pl.ds(r0, CHUNK), rr * HEAD_W + 32 * hp:rr * HEAD_W + 32 * hp + 32] = lse[:, 48:80]

        def run(units, both):
            data = [first(rr, b, both) for rr, b in units]
            probs = [second(scores) for _, scores in data]
            for (rr, b), (tiles, _), pr in zip(units, data, probs):
                third(rr, b, tiles, pr, both)

        run([(rr, 0) for rr in range(rps)], False)
        if nblk > 1:
            def one(b, carry):
                run([(0, b)], True)
                return carry

            lax.fori_loop(1, nblk, one, 0)

    spec = pl.BlockSpec((seg, rps * ATTN_W), lambda r: (0, r))
    return _pcall(
        body, name=f"attn_fwd_d{dil}", grid=(dil // rps,),
        in_specs=[spec, spec, spec], out_specs=[spec, pl.BlockSpec((seg, rps * HEAD_W), lambda r: (0, r))],
        out_shape=[_sds((seg, dil * ATTN_W), BF16), _sds((seg, dil * HEAD_W), F32)],
        args=(qv, kv, vv), comms=comms)


def _mix_out_fwd(o_list, l_list, sgu, x, w_out, g_attn, g_sgu, g_post, comms=()):
    def body(o1, o2, o3, l1, l2, l3, sgu_ref, x_ref, w_ref, ga_ref, gs_ref, gp_ref,
             attn_ref, mixed_ref, y_ref, x1_ref, lse1_ref, lse2_ref, lse3_ref, slabs_a, slabs_b):
        os = [o1[...], _load_view(o2, slabs_a, DILATIONS[1]), _load_view(o3, slabs_b, DILATIONS[2])]
        ls = [l1[...], _load_view(l2, slabs_a, DILATIONS[1], HEAD_W), _load_view(l3, slabs_b, DILATIONS[2], HEAD_W)]
        m = jnp.maximum(jnp.maximum(ls[0], ls[1]), ls[2])
        es = [jnp.exp(l - m) for l in ls]
        den = es[0] + es[1] + es[2]
        spread = _head_spread()
        attn = sum(_dot_select(e / den, spread) * o for e, o in zip(es, os))
        attn_ref[...] = attn
        lse = m + jnp.log(den)
        lse1_ref[...] = lse
        _store_view(lse, lse2_ref, slabs_a, DILATIONS[1])
        _store_view(lse, lse3_ref, slabs_b, DILATIONS[2])
        ah, _ = _rms_stats(attn)
        sh, _ = _rms_stats(sgu_ref[...])
        mixed = jnp.concatenate([ah * ga_ref[...], sh * gs_ref[...]], axis=1).astype(BF16)
        mixed_ref[...] = mixed
        y = _dot(mixed[:, 0:OUT_S], w_ref[0])
        for s in range(1, N_SHARD):
            y = y + _dot(mixed[:, s * OUT_S:(s + 1) * OUT_S], w_ref[s])
        y_ref[...] = y
        yh, _ = _rms_stats(y)
        x1_ref[...] = x_ref[...] + yh * gp_ref[...]

    return _pcall(
        body, name="mix_out_fwd", grid=(SEQ // TM,),
        in_specs=[_view_rows(dil) for dil in DILATIONS] + [_view_rows(dil, HEAD_W) for dil in DILATIONS]
        + [_rows(512), _rows(D_MODEL), _resident((N_SHARD, OUT_S, D_MODEL)),
           _resident((1, 512)), _resident((1, 512)), _resident((1, D_MODEL))],
        out_specs=[_rows(512), _rows(D_MODEL), _rows(D_MODEL), _rows(D_MODEL)]
        + [_view_rows(dil, HEAD_W) for dil in DILATIONS],
        out_shape=[_sds((SEQ, 512), F32), _sds((SEQ, D_MODEL), BF16), _sds((SEQ, D_MODEL), F32),
                   _sds((SEQ, D_MODEL), F32)] + [_view_shape(dil, F32, HEAD_W) for dil in DILATIONS],
        scratch_shapes=[_slab_scratch(), _slab_scratch()],
        args=(*o_list, *l_list, sgu, x, w_out, g_attn, g_sgu, g_post), comms=comms)


def _ffn_fwd_bwd(x1, target, w_gate, w_up, w_down, g_pre, g_post, comms=()):
    def body(x1_ref, t_ref, wg_ref, wu_ref, wd_ref, gpf_ref, gpo_ref,
             h2_ref, a_ref, dg_ref, dup_ref, df_ref, dx1_ref, loss_ref, dgpf_ref, dgpo_ref, g_scr, up_scr):
        @pl.when(pl.program_id(0) == 0)
        def _():
            loss_ref[...] = jnp.zeros_like(loss_ref)
            dgpf_ref[...] = jnp.zeros_like(dgpf_ref)
            dgpo_ref[...] = jnp.zeros_like(dgpo_ref)

        x1 = x1_ref[...]
        gpf = gpf_ref[...]
        gpo = gpo_ref[...]
        xh, r = _rms_stats(x1)
        h2 = (xh * gpf).astype(BF16)
        h2_ref[...] = h2
        f = jnp.zeros((TM_FFN, D_MODEL), F32)
        for c0, c1 in FF_CHUNKS:
            g = _dot_nt(h2, wg_ref[c0:c1, :])
            up = _dot_nt(h2, wu_ref[c0:c1, :])
            g_scr[:, c0:c1] = g
            up_scr[:, c0:c1] = up
            a = (g * _sigmoid(g) * up).astype(BF16)
            a_ref[:, c0:c1] = a
            f = f + _dot(a, wd_ref[c0:c1, :])
        fh, rf = _rms_stats(f)
        diff = x1 + fh * gpo - t_ref[...]
        loss_ref[...] += jnp.sum(diff * diff, axis=0, keepdims=True)
        dout = diff * np.float32(1.0 / D_MODEL)
        df, dgpo = _rms_bwd(fh, rf, gpo, dout)
        dgpo_ref[...] += dgpo
        dfb = df.astype(BF16)
        df_ref[...] = dfb
        dh2 = jnp.zeros((TM_FFN, D_MODEL), F32)
        for c0, c1 in FF_CHUNKS:
            da = _dot_nt(dfb, wd_ref[c0:c1, :])
            g = g_scr[:, c0:c1]
            up = up_scr[:, c0:c1]
            sg = _sigmoid(g)
            dup = (da * (g * sg)).astype(BF16)
            dg = (da * up * (sg * (1.0 + g * (1.0 - sg)))).astype(BF16)
            dg_ref[:, c0:c1] = dg
            dup_ref[:, c0:c1] = dup
            dh2 = dh2 + _dot(dg, wg_ref[c0:c1, :]) + _dot(dup, wu_ref[c0:c1, :])
        dx, dgpf = _rms_bwd(xh, r, gpf, dh2)
        dgpf_ref[...] += dgpf
        dx1_ref[...] = dout + dx

    return _pcall(
        body, name="ffn_fwd_bwd", grid=(SEQ // TM_FFN,),
        in_specs=[_rows(D_MODEL, TM_FFN), _rows(D_MODEL, TM_FFN), _resident((FF, D_MODEL)),
                  _resident((FF, D_MODEL)), _resident((FF, D_MODEL)),
                  _resident((1, D_MODEL)), _resident((1, D_MODEL))],
        out_specs=[_rows(D_MODEL, TM_FFN), _rows(FF, TM_FFN), _rows(FF, TM_FFN), _rows(FF, TM_FFN),
                   _rows(D_MODEL, TM_FFN), _rows(D_MODEL, TM_FFN), _acc(D_MODEL), _acc(D_MODEL), _acc(D_MODEL)],
        out_shape=[_sds((SEQ, D_MODEL), BF16), _sds((SEQ, FF), BF16), _sds((SEQ, FF), BF16),
                   _sds((SEQ, FF), BF16), _sds((SEQ, D_MODEL), BF16), _sds((SEQ, D_MODEL), F32),
                   _sds((1, D_MODEL), F32), _sds((1, D_MODEL), F32), _sds((1, D_MODEL), F32)],
        scratch_shapes=[pltpu.VMEM((TM_FFN, FF), F32), pltpu.VMEM((TM_FFN, FF), F32)],
        args=(x1, target, w_gate, w_up, w_down, g_pre, g_post), comms=comms)


def _wgrad(a, b, a_spec, b_spec, out_block, name, comms=()):
    def body(a_ref, b_ref, o_ref):
        av = a_ref[0] if len(a_ref.shape) == 3 else a_ref[...]
        bv = b_ref[0] if len(b_ref.shape) == 3 else b_ref[...]
        o_ref[0] = _dot_tn(av, bv)

    return _pcall(
        body, name=name, grid=(N_SHARD,),
        in_specs=[a_spec, b_spec],
        out_specs=pl.BlockSpec((1,) + out_block, lambda s: (s, 0, 0)),
        out_shape=_sds((N_SHARD,) + out_block, F32),
        args=(a, b), comms=comms)


def _outproj_bwd(dx1, y, attn, sgu, w_out, g_post, g_attn, g_sgu, u, vs, lg, lb, w_sp, b_t, comms=(), after=()):
    sgu_bwd = _sgu_bwd_body()

    def body(dx1_ref, y_ref, attn_ref, sgu_ref, w_ref, gp_ref, ga_ref, gs_ref, u_ref, vs_ref, lg_ref, lb_ref,
             wsp_ref, bt_ref, dy_ref, du_ref, dvs_ref, dgp_ref, dga_ref, dgs_ref, dw_ref, db_ref, dlg_ref, dlb_ref,
             *rest):
        dattn_refs, delta_refs, (slabs_a, slabs_b, dsgu_ref, dbias_scr) = rest[0:3], rest[3:6], rest[6:]

        @pl.when(pl.program_id(0) == 0)
        def _():
            dgp_ref[...] = jnp.zeros_like(dgp_ref)
            dga_ref[...] = jnp.zeros_like(dga_ref)
            dgs_ref[...] = jnp.zeros_like(dgs_ref)

        yh, ry = _rms_stats(y_ref[...])
        dy, dgp = _rms_bwd(yh, ry, gp_ref[...], dx1_ref[...])
        dgp_ref[...] += dgp
        dyb = dy.astype(BF16)
        dy_ref[...] = dyb
        dmixed = jnp.concatenate([_dot_nt(dyb, w_ref[s]) for s in range(N_SHARD)], axis=1)
        attn = attn_ref[...]
        ah, ra = _rms_stats(attn)
        dattn, dga = _rms_bwd(ah, ra, ga_ref[...], dmixed[:, 0:512])
        dga_ref[...] += dga
        sh, rs = _rms_stats(sgu_ref[...])
        dsgu, dgs = _rms_bwd(sh, rs, gs_ref[...], dmixed[:, 512:1024])
        dgs_ref[...] += dgs
        dsgu_ref[...] = dsgu
        delta = _dot_select(dattn * attn, _head_sum())
        dattn_refs[0][...] = dattn.astype(BF16)
        delta_refs[0][...] = delta
        for i, dil in enumerate(DILATIONS[1:]):
            _store_view(dattn, dattn_refs[i + 1], slabs_a, dil)
            _store_view(delta, delta_refs[i + 1], slabs_b, dil)
        sgu_bwd(u_ref, vs_ref, dsgu_ref, lg_ref, lb_ref, wsp_ref, bt_ref,
                du_ref, dvs_ref, dw_ref, db_ref, dlg_ref, dlb_ref, dbias_scr)

    return _pcall(
        body, name="outproj_sgu_bwd", grid=(SEQ // TM,),
        in_specs=[_rows(D_MODEL), _rows(D_MODEL), _rows(512), _rows(512), _resident((N_SHARD, OUT_S, D_MODEL)),
                  _resident((1, D_MODEL)), _resident((1, 512)), _resident((1, 512)),
                  _rows(SGU_W), _rows(SGU_W), _resident((1, SGU_W)), _resident((1, SGU_W)),
                  _resident((SGU_GROUPS, CHUNK, CHUNK)), _resident((CHUNK, SGU_GROUPS))],
        out_specs=[_rows(D_MODEL), _rows(SGU_W), _rows(SGU_W), _acc(D_MODEL), _acc(512), _acc(512),
                   pl.BlockSpec((SGU_GROUPS, CHUNK, CHUNK), lambda i: (0, 0, 0)), _acc(CHUNK, SGU_GROUPS),
                   _acc(SGU_W), _acc(SGU_W)]
        + [_view_rows(dil) for dil in DILATIONS] + [_view_rows(dil, HEAD_W) for dil in DILATIONS],
        out_shape=[_sds((SEQ, D_MODEL), BF16), _sds((SEQ, SGU_W), BF16), _sds((SEQ, SGU_W), BF16),
                   _sds((1, D_MODEL), F32), _sds((1, 512), F32), _sds((1, 512), F32),
                   _sds((SGU_GROUPS, CHUNK, CHUNK), F32), _sds((SGU_GROUPS, CHUNK), F32),
                   _sds((1, SGU_W), F32), _sds((1, SGU_W), F32)]
        + [_view_shape(dil, BF16) for dil in DILATIONS] + [_view_shape(dil, F32, HEAD_W) for dil in DILATIONS],
        scratch_shapes=[_slab_scratch(), _slab_scratch(), pltpu.VMEM((TM, SGU_W), F32),
                        pltpu.VMEM((CHUNK, SGU_W), F32)],
        args=(dx1, y, attn, sgu, w_out, g_post, g_attn, g_sgu, u, vs, lg, lb, w_sp, b_t), comms=comms, after=after)


def _sgu_bwd_body():
    nsteps = SEQ // TM

    def body(u_ref, vs_ref, ds_ref, lg_ref, lb_ref, w_ref, bt_ref,
             du_ref, dvs_ref, dw_ref, db_ref, dlg_ref, dlb_ref, dbias_scr):
        i = pl.program_id(0)

        @pl.when(i == 0)
        def _():
            dw_ref[...] = jnp.zeros_like(dw_ref)
            dlg_ref[...] = jnp.zeros_like(dlg_ref)
            dlb_ref[...] = jnp.zeros_like(dlb_ref)
            dbias_scr[...] = jnp.zeros_like(dbias_scr)

        wm = _masked_spatial(w_ref)
        ones_g = _group_ones()
        bias_full = _dot_exact(bt_ref[...], ones_g)
        u = u_ref[...]
        vs = vs_ref[...]
        lg = lg_ref[...]
        gu, xh, rstd, vnb, mixed, cdf_u, cdf_vs = _sgu_core(u, vs, lg, lb_ref[...], wm, bias_full)
        dsgu = ds_ref[...]
        du_ref[...] = (dsgu * mixed * _gelu_grad(u, cdf_u)).astype(BF16)
        dmixed = dsgu * gu
        left = _left_half()
        dvn_rows = []
        for c in range(TM // CHUNK):
            rs = slice(c * CHUNK, (c + 1) * CHUNK)
            dm_c = dmixed[rs, :]
            dbias_scr[...] += dm_c
            pieces = []
            for p in range(4):
                ls = slice(p * 128, (p + 1) * 128)
                dmp = dm_c[:, ls]
                vp = vnb[rs, ls]
                dmb = dmp.astype(BF16)
                zero = jnp.zeros_like(dmb)
                dw_ref[2 * p] += _dot_nt(jnp.where(left, dmb, zero), vp)
                dw_ref[2 * p + 1] += _dot_nt(jnp.where(left, zero, dmb), vp)
                pieces.append(jnp.where(left, _dot_tn(wm[2 * p], dmb), _dot_tn(wm[2 * p + 1], dmb)))
            dvn_rows.append(jnp.concatenate(pieces, axis=1))
        dvn = jnp.concatenate(dvn_rows, axis=0)
        dlg_ref[...] += jnp.sum(dvn * xh, axis=0, keepdims=True)
        dlb_ref[...] += jnp.sum(dvn, axis=0, keepdims=True)
        dxh = dvn * lg
        dgv = rstd * (dxh - jnp.mean(dxh, axis=-1, keepdims=True) - xh * jnp.mean(dxh * xh, axis=-1, keepdims=True))
        dvs_ref[...] = (dgv * _gelu_grad(vs, cdf_vs)).astype(BF16)

        @pl.when(i == nsteps - 1)
        def _():
            row = lax.broadcasted_iota(jnp.int32, (CHUNK, CHUNK), 0)
            col = lax.broadcasted_iota(jnp.int32, (CHUNK, CHUNK), 1)
            for g in range(SGU_GROUPS):
                dw_ref[g] = jnp.where(col <= row, dw_ref[g], 0.0)
            db_ref[...] = lax.dot_general(ones_g, dbias_scr[...], (((1,), (1,)), ((), ())),
                                          preferred_element_type=F32, precision=lax.Precision.HIGHEST)

    return body


def _attn_bwd_v1(qv, kv, vv, dov, deltav, lsev, dil, comms=(), after=()):
    seg = SEQ // dil
    nblk = seg // CHUNK
    rps = 4 if nblk == 1 else 1

    def body(q_ref, k_ref, v_ref, do_ref, dl_ref, lse_ref, dq_ref, dk_ref, dv_ref, dk_wait, dv_wait):
        left = _left_half()
        m_cur, m_prev = _block_masks()

        sides = tuple(enumerate((left, ~left)))
        zero = jnp.zeros((CHUNK, CHUNK), BF16)

        def rows(b):
            if isinstance(b, int):
                return b * CHUNK, max(b - 1, 0) * CHUNK
            return pl.multiple_of(b * CHUNK, CHUNK), pl.multiple_of(jnp.maximum(b - 1, 0) * CHUNK, CHUNK)

        def first(rr, b):
            r0, rp = rows(b)
            tiles, firsts = [], []
            for hp in range(4):
                ls = slice(rr * ATTN_W + hp * 128, rr * ATTN_W + (hp + 1) * 128)
                qp = q_ref[pl.ds(r0, CHUNK), ls]
                kc = k_ref[pl.ds(r0, CHUNK), ls]
                vc = v_ref[pl.ds(r0, CHUNK), ls]
                dop = do_ref[pl.ds(r0, CHUNK), ls]
                kp = k_ref[pl.ds(rp, CHUNK), ls] if nblk > 1 else None
                vp = v_ref[pl.ds(rp, CHUNK), ls] if nblk > 1 else None
                tiles.append((ls, kc, kp))
                for side, hm in sides:
                    qh = jnp.where(hm, qp, zero)
                    doh = jnp.where(hm, dop, zero)
                    cur = (_dot_nt(kc, qh), _dot_nt(vc, doh))
                    prev = (_dot_nt(kp, qh), _dot_nt(vp, doh)) if nblk > 1 else None
                    firsts.append((qh, doh, cur, prev))
            return tiles, firsts

        def second(rr, b, firsts):
            r0, _ = rows(b)
            own_ok, prev_ok = m_prev, m_cur & (b > 0)
            lanes = slice(rr * HEAD_W, (rr + 1) * HEAD_W)
            lse_t = lse_ref[pl.ds(r0, CHUNK), lanes].T
            dl_t = dl_ref[pl.ds(r0, CHUNK), lanes].T
            seconds = []
            for i, (qh, doh, cur, prev) in enumerate(firsts):
                lse_h = lse_t[16 * i:16 * i + 1, :]
                dl_h = dl_t[16 * i:16 * i + 1, :]
                pc = jnp.exp(jnp.where(own_ok, cur[0] - lse_h, NEG))
                out = [pc.astype(BF16), (pc * (cur[1] - dl_h)).astype(BF16), None, None]
                if nblk > 1:
                    pp = jnp.exp(jnp.where(prev_ok, prev[0] - lse_h, NEG))
                    out[2:] = [pp.astype(BF16), (pp * (prev[1] - dl_h)).astype(BF16)]
                seconds.append(out)
            return seconds

        def third(rr, b, tiles, firsts, seconds):
            r0, rp = rows(b)
            for hp, (ls, kc, kp) in enumerate(tiles):
                dq = jnp.zeros((CHUNK, CHUNK), F32)
                dkc = jnp.zeros((CHUNK, CHUNK), F32)
                dvc = jnp.zeros((CHUNK, CHUNK), F32)
                dkp = jnp.zeros((CHUNK, CHUNK), F32)
                dvp = jnp.zeros((CHUNK, CHUNK), F32)
                for side, hm in sides:
                    qh, doh, _, _ = firsts[2 * hp + side]
                    pcb, dsc, ppb, dsp = seconds[2 * hp + side]
                    dq = dq + _dot_tn(dsc, jnp.where(hm, kc, zero))
                    dkc = dkc + _dot(dsc, qh)
                    dvc = dvc + _dot(pcb, doh)
                    if nblk > 1:
                        dq = dq + _dot_tn(dsp, jnp.where(hm, kp, zero))
                        dkp = dkp + _dot(dsp, qh)
                        dvp = dvp + _dot(ppb, doh)
                dq_ref[pl.ds(r0, CHUNK), ls] = dq.astype(dq_ref.dtype)
                if nblk == 1:
                    dk_ref[pl.ds(r0, CHUNK), ls] = dkc.astype(dk_ref.dtype)
                    dv_ref[pl.ds(r0, CHUNK), ls] = dvc.astype(dv_ref.dtype)
                else:
                    @pl.when(b > 0)
                    def _():
                        dk_ref[pl.ds(rp, CHUNK), ls] = (dk_wait[:, ls] + dkp).astype(dk_ref.dtype)
                        dv_ref[pl.ds(rp, CHUNK), ls] = (dv_wait[:, ls] + dvp).astype(dv_ref.dtype)

                    dk_wait[:, ls] = dkc
                    dv_wait[:, ls] = dvc

        def run(units):
            data = [first(rr, b) for rr, b in units]
            probs = [second(rr, b, firsts) for (rr, b), (_, firsts) in zip(units, data)]
            for (rr, b), (tiles, firsts), seconds in zip(units, data, probs):
                third(rr, b, tiles, firsts, seconds)

        if nblk == 1:
            run([(rr, 0) for rr in range(rps)])
        else:
            def one(b, carry):
                run([(0, b)])
                return carry

            lax.fori_loop(0, nblk, one, 0)
            last = (nblk - 1) * CHUNK
            dk_ref[last:last + CHUNK, :] = dk_wait[...].astype(dk_ref.dtype)
            dv_ref[last:last + CHUNK, :] = dv_wait[...].astype(dv_ref.dtype)

    spec = pl.BlockSpec((seg, rps * ATTN_W), lambda r: (0, r))
    return _pcall(
        body, name=f"attn_bwd_d{dil}", grid=(dil // rps,),
        in_specs=[spec] * 4 + [pl.BlockSpec((seg, rps * HEAD_W), lambda r: (0, r))] * 2, out_specs=[spec] * 3,
        out_shape=[_sds((seg, dil * ATTN_W), BF16)] * 3,
        scratch_shapes=[pltpu.VMEM((CHUNK, ATTN_W), F32), pltpu.VMEM((CHUNK, ATTN_W), F32)],
        args=(qv, kv, vv, dov, deltav, lsev), comms=comms, after=after)


def _attn_bwd(qv, kv, vv, dov, deltav, lsev, dil, comms=(), after=()):
    seg = SEQ // dil
    nblk = seg // CHUNK
    rps = 4 if nblk == 1 else 1

    def body(q_ref, k_ref, v_ref, do_ref, dl_ref, lse_ref, dq_ref, dk_ref, dv_ref, dk_wait, dv_wait):
        left = _left_half()
        m_cur, m_prev = _block_masks()
        sides = tuple(enumerate((left, ~left)))
        zero = jnp.zeros((CHUNK, CHUNK), BF16)

        def first(rr, b, both):
            r0, rp = _attn_rows(b)
            keys = pl.ds(rp, 2 * CHUNK) if both else pl.ds(r0, CHUNK)
            tiles, heads = [], []
            for hp in range(4):
                ls = slice(rr * ATTN_W + hp * 128, rr * ATTN_W + (hp + 1) * 128)
                qp = q_ref[pl.ds(r0, CHUNK), ls]
                dop = do_ref[pl.ds(r0, CHUNK), ls]
                k2 = k_ref[keys, ls]
                v2 = v_ref[keys, ls]
                tiles.append((ls, k2))
                for _, hm in sides:
                    qh = jnp.where(hm, qp, zero)
                    doh = jnp.where(hm, dop, zero)
                    heads.append((qh, doh, _dot_nt(k2, qh), _dot_nt(v2, doh)))
            return tiles, heads

        def second(rr, b, heads, both):
            r0, _ = _attn_rows(b)
            ok = jnp.concatenate([m_cur, m_prev], axis=0) if both else m_prev
            lanes = slice(rr * HEAD_W, (rr + 1) * HEAD_W)
            lse_t = lse_ref[pl.ds(r0, CHUNK), lanes].T
            dl_t = dl_ref[pl.ds(r0, CHUNK), lanes].T
            out = []
            for i, (_, _, s_t, dp_t) in enumerate(heads):
                p = jnp.exp(jnp.where(ok, s_t - lse_t[16 * i:16 * i + 1, :], NEG))
                out.append((p.astype(BF16), (p * (dp_t - dl_t[16 * i:16 * i + 1, :])).astype(BF16)))
            return out

        def third(rr, b, tiles, heads, probs, both):
            r0, rp = _attn_rows(b)
            nk = 2 * CHUNK if both else CHUNK
            left_k = _lane_left(nk)
            zero_k = jnp.zeros((nk, CHUNK), BF16)
            for hp, (ls, k2) in enumerate(tiles):
                dq = jnp.zeros((CHUNK, CHUNK), F32)
                dk2 = jnp.zeros((nk, CHUNK), F32)
                dv2 = jnp.zeros((nk, CHUNK), F32)
                for side in range(2):
                    qh, doh, _, _ = heads[2 * hp + side]
                    p, ds = probs[2 * hp + side]
                    dq = dq + _dot_tn(ds, jnp.where(left_k if side == 0 else ~left_k, k2, zero_k))
                    dk2 = dk2 + _dot(ds, qh)
                    dv2 = dv2 + _dot(p, doh)
                dq_ref[pl.ds(r0, CHUNK), ls] = dq.astype(dq_ref.dtype)
                if nblk == 1:
                    dk_ref[pl.ds(r0, CHUNK), ls] = dk2.astype(dk_ref.dtype)
                    dv_ref[pl.ds(r0, CHUNK), ls] = dv2.astype(dv_ref.dtype)
                elif both:
                    dk_ref[pl.ds(rp, CHUNK), ls] = (dk_wait[:, ls] + dk2[0:CHUNK]).astype(dk_ref.dtype)
                    dv_ref[pl.ds(rp, CHUNK), ls] = (dv_wait[:, ls] + dv2[0:CHUNK]).astype(dv_ref.dtype)
                    dk_wait[:, ls] = dk2[CHUNK:]
                    dv_wait[:, ls] = dv2[CHUNK:]
                else:
                    dk_wait[:, ls] = dk2
                    dv_wait[:, ls] = dv2

        def run(units, both):
            data = [first(rr, b, both) for rr, b in units]
            probs = [second(rr, b, heads, both) for (rr, b), (_, heads) in zip(units, data)]
            for (rr, b), (tiles, heads), pr in zip(units, data, probs):
                third(rr, b, tiles, heads, pr, both)

        run([(rr, 0) for rr in range(rps)], False)
        if nblk > 1:
            def one(b, carry):
                run([(0, b)], True)
                return carry

            lax.fori_loop(1, nblk, one, 0)
            last = (nblk - 1) * CHUNK
            dk_ref[last:last + CHUNK, :] = dk_wait[...].astype(dk_ref.dtype)
            dv_ref[last:last + CHUNK, :] = dv_wait[...].astype(dv_ref.dtype)

    spec = pl.BlockSpec((seg, rps * ATTN_W), lambda r: (0, r))
    return _pcall(
        body, name=f"attn_bwd_d{dil}", grid=(dil // rps,),
        in_specs=[spec] * 4 + [pl.BlockSpec((seg, rps * HEAD_W), lambda r: (0, r))] * 2, out_specs=[spec] * 3,
        out_shape=[_sds((seg, dil * ATTN_W), BF16)] * 3,
        scratch_shapes=[pltpu.VMEM((CHUNK, ATTN_W), F32), pltpu.VMEM((CHUNK, ATTN_W), F32)],
        args=(qv, kv, vv, dov, deltav, lsev), comms=comms, after=after)


def _inproj_bwd(dqs, dks, dvs, du, dvs_sgu, pos, x, dx1, w_in, g_pre, comms=()):
    def body(dq1, dq2, dq3, dk1, dk2, dk3, dv1, dv2, dv3, du_ref, dvs_ref, pos_ref, x_ref, dx1_ref, w_ref, g_ref,
             dproj_ref, gx_ref, dg_ref, slabs_a, slabs_b):
        @pl.when(pl.program_id(0) == 0)
        def _():
            dg_ref[...] = jnp.zeros_like(dg_ref)

        def total(r1, r2, r3):
            return r1[...] + _load_view(r2, slabs_a, DILATIONS[1]) + _load_view(r3, slabs_b, DILATIONS[2])

        tabs = _rot_tables(pos_ref[...])
        groups = {3: du_ref[...], 4: dvs_ref[...]}
        dh = jnp.zeros((TM, D_MODEL), F32)
        for g in (3, 4, 0, 1, 2):
            if g == 0:
                groups[g] = _rope_bwd(total(dq1, dq2, dq3) * np.float32(ATTN_SCALE), tabs).astype(BF16)
            elif g == 1:
                groups[g] = _rope_bwd(total(dk1, dk2, dk3), tabs).astype(BF16)
            elif g == 2:
                groups[g] = total(dv1, dv2, dv3).astype(BF16)
            dproj_ref[:, 512 * g:512 * (g + 1)] = groups[g]
            off = 0
            for s, a, b in _in_pieces(g):
                dh = dh + _dot_nt(groups[g][:, off:off + b - a], w_ref[s, :, a:b])
                off += b - a
        g = g_ref[...]
        xh, r = _rms_stats(x_ref[...])
        dx, dg = _rms_bwd(xh, r, g, dh)
        dg_ref[...] += dg
        gx_ref[...] = dx1_ref[...] + dx

    return _pcall(
        body, name="inproj_bwd", grid=(SEQ // TM,),
        in_specs=[_view_rows(dil) for dil in DILATIONS] * 3
        + [_rows(512), _rows(512), _rows(1), _rows(D_MODEL), _rows(D_MODEL),
           _resident((N_SHARD, D_MODEL, IN_S)), _resident((1, D_MODEL))],
        out_specs=[_rows(PROJ_W), _rows(D_MODEL), _acc(D_MODEL)],
        out_shape=[_sds((SEQ, PROJ_W), BF16), _sds((SEQ, D_MODEL), F32), _sds((1, D_MODEL), F32)],
        scratch_shapes=[_slab_scratch(), _slab_scratch()],
        args=(*dqs, *dks, *dvs, du, dvs_sgu, pos, x, dx1, w_in, g_pre), comms=comms)


def _to_view(a, dil):
    return a if dil == 1 else a.reshape(SEQ // dil, dil * a.shape[1])


def _from_view(a, dil):
    return a if dil == 1 else a.reshape(SEQ, a.shape[1] // dil)


def _local_step(x, pos, target, w_in, w_out, w_gate, w_up, w_down, small):
    b_t = small["sgu_b_spatial"].T
    h, u, vs, sgu, *qkv = _inproj_fwd(x, pos, small["pre_mix_norm"], w_in, small["sgu_ln_gain"],
                                      small["sgu_ln_bias"], small["sgu_w_spatial"], b_t)
    views = [tuple(qkv[3 * i:3 * i + 3]) for i in range(len(DILATIONS))]
    o_list, l_list = [], []
    for dil, (qv, kv, vv) in zip(DILATIONS, views):
        o, l = _attn_fwd(qv, kv, vv, dil)
        o_list.append(o)
        l_list.append(l)
    attn, mixed, y, x1, *lses = _mix_out_fwd(o_list, l_list, sgu, x, w_out, small["attn_out_norm"],
                                             small["sgu_out_norm"], small["post_mix_norm"])
    h2, a, dg, dup, df, dx1, loss_cols, d_pre_ffn, d_post_ffn = _ffn_fwd_bwd(
        x1, target, w_gate, w_up, w_down, small["pre_ffn_norm"], small["post_ffn_norm"])

    full_tok = pl.BlockSpec((SEQ, D_MODEL), lambda s: (0, 0), pipeline_mode=pl.Buffered(1))
    ff_tok = pl.BlockSpec((1, SEQ, FF_S), lambda s: (s, 0, 0))
    gw_gate, gw_up = _wgrad_ff([dg, dup], h2, "wgrad_gate_up")
    (gw_down,) = _wgrad_ff([a], df, "wgrad_down")

    (dy, du, dvs_sgu, d_post_mix, d_attn_norm, d_sgu_norm, d_w_sp, d_b_sp, d_ln_gain, d_ln_bias,
     *dviews) = _outproj_bwd(
        dx1, y, attn, sgu, w_out, small["post_mix_norm"], small["attn_out_norm"], small["sgu_out_norm"],
        u, vs, small["sgu_ln_gain"], small["sgu_ln_bias"], small["sgu_w_spatial"], b_t)
    gw_out = _wgrad(mixed, dy, pl.BlockSpec((SEQ, OUT_S), lambda s: (0, s)), full_tok, (OUT_S, D_MODEL), "wgrad_out")

    dqs, dks, dvs = [], [], []
    for i, (dil, (qv, kv, vv)) in enumerate(zip(DILATIONS, views)):
        dq, dk, dv = _attn_bwd(qv, kv, vv, dviews[i], dviews[3 + i], lses[i], dil)
        dqs.append(dq)
        dks.append(dk)
        dvs.append(dv)
    dproj, grad_x, d_pre_mix = _inproj_bwd(dqs, dks, dvs, du, dvs_sgu, pos, x, dx1, w_in, small["pre_mix_norm"])
    gw_in = _wgrad(h, dproj, full_tok, pl.BlockSpec((SEQ, IN_S), lambda s: (0, s)), (D_MODEL, IN_S), "wgrad_in")

    small_grads = {
        "pre_mix_norm": d_pre_mix, "sgu_ln_gain": d_ln_gain, "sgu_ln_bias": d_ln_bias, "sgu_w_spatial": d_w_sp,
        "sgu_b_spatial": d_b_sp, "attn_out_norm": d_attn_norm, "sgu_out_norm": d_sgu_norm,
        "post_mix_norm": d_post_mix, "pre_ffn_norm": d_pre_ffn, "post_ffn_norm": d_post_ffn,
    }
    return loss_cols, grad_x, (gw_in, gw_out, gw_gate, gw_up, gw_down), small_grads


def _coords():
    return lax.axis_index("x"), lax.axis_index("y"), lax.axis_index("c")


def _other_chips(x, y):
    return [(1 - x, y), (x, 1 - y), (1 - x, 1 - y)]


def _comm_call(body, name, n_in, out_shape, scratch_shapes):
    return pl.pallas_call(
        body, name=name, in_specs=[ANY] * n_in, out_specs=[ANY] * len(out_shape), out_shape=out_shape,
        scratch_shapes=scratch_shapes,
        compiler_params=pltpu.CompilerParams(has_side_effects=True),
    )


def _gather_weights(shards):
    n = len(shards)
    halves = [s.reshape(2, s.shape[0] // 2, s.shape[1]) for s in shards]

    def body(*refs):
        ins, outs = refs[:n], refs[n:2 * n]
        send_sems, recv_sems = refs[2 * n:]
        x, y, c = _coords()
        s_me = 2 * x + y
        chips = _other_chips(x, y)
        sibling = (x, y, 1 - c)

        def copy(k, w, shard, cc, to):
            src = ins[w].at[cc] if shard is None else outs[w].at[shard, cc]
            dst = outs[w].at[s_me if shard is None else shard, cc]
            return pltpu.make_async_remote_copy(src_ref=src, dst_ref=dst, send_sem=send_sems.at[k],
                                                recv_sem=recv_sems.at[k], device_id=to, device_id_type=MESH)

        first = [copy(j * n + w, w, None, c, (cx, cy, c)) for j, (cx, cy) in enumerate(chips) for w in range(n)]
        for cp in first:
            cp.start()
        passed = []
        for j, (cx, cy) in enumerate(chips):
            for w in range(n):
                copy(j * n + w, w, 2 * cx + cy, c, (x, y, c)).wait_recv()
                fw = copy((3 + j) * n + w, w, 2 * cx + cy, c, sibling)
                fw.start()
                passed.append(fw)
        for j, (cx, cy) in enumerate(chips):
            for w in range(n):
                copy((3 + j) * n + w, w, 2 * cx + cy, 1 - c, (x, y, c)).wait_recv()
        for cp in first + passed:
            cp.wait_send()

    out_shape = [_sds((N_SHARD,) + h.shape, h.dtype) for h in halves]
    scratch = [pltpu.SemaphoreType.DMA((6 * n,)), pltpu.SemaphoreType.DMA((6 * n,))]
    full = _comm_call(body, "comm_gather_weights", n, out_shape, scratch)(*halves)
    s_me = 2 * lax.axis_index("x") + lax.axis_index("y")
    full = [lax.dynamic_update_slice(f, h[None], (s_me, 0, 0, 0)) for f, h in zip(full, halves)]
    return [f.reshape((N_SHARD,) + s.shape) for f, s in zip(full, shards)]


def _rs_to_sibling(gws):
    n = len(gws)

    def body(*refs):
        ins, outs = refs[:n], refs[n:2 * n]
        send_sems, recv_sems = refs[2 * n:]
        x, y, c = _coords()
        copies = []
        for w in range(n):
            hw = gws[w].shape[1] // 2
            copies.append(pltpu.make_async_remote_copy(
                src_ref=ins[w].at[:, pl.ds((1 - c) * hw, hw), :], dst_ref=outs[w], send_sem=send_sems.at[w],
                recv_sem=recv_sems.at[w], device_id=(x, y, 1 - c), device_id_type=MESH))
        for cp in copies:
            cp.start()
        for cp in copies:
            cp.wait()

    out_shape = [_sds((N_SHARD, g.shape[1] // 2, g.shape[2]), g.dtype) for g in gws]
    scratch = [pltpu.SemaphoreType.DMA((n,)), pltpu.SemaphoreType.DMA((n,))]
    return _comm_call(body, "comm_rs_sibling", n, out_shape, scratch)(*gws)


def _rs_chip_sum(gw, recv, core):
    _, rows, cols = gw.shape
    hw = rows // 2

    def body(c_ref, g_ref, r_ref, o_ref):
        o_ref[...] = (g_ref[...] + r_ref[...]).astype(BF16)

    return pl.pallas_call(
        body, name="rs_chip_sum",
        grid_spec=pltpu.PrefetchScalarGridSpec(
            num_scalar_prefetch=1, grid=(N_SHARD,),
            in_specs=[pl.BlockSpec((1, hw, cols), lambda s, c_ref: (s, c_ref[0], 0)),
                      pl.BlockSpec((1, hw, cols), lambda s, c_ref: (s, 0, 0))],
            out_specs=pl.BlockSpec((1, hw, cols), lambda s, c_ref: (s, 0, 0))),
        out_shape=_sds((N_SHARD, hw, cols), BF16),
        compiler_params=_seq_params(),
    )(core, gw, recv)


def _rs_between_chips(pbs):
    n = len(pbs)

    def body(*refs):
        ins, outs = refs[:n], refs[n:2 * n]
        send_sems, recv_sems = refs[2 * n:]
        x, y, c = _coords()
        copies = []
        for j, (cx, cy) in enumerate(_other_chips(x, y)):
            for w in range(n):
                copies.append(pltpu.make_async_remote_copy(
                    src_ref=ins[w].at[2 * cx + cy], dst_ref=outs[w].at[j], send_sem=send_sems.at[j * n + w],
                    recv_sem=recv_sems.at[j * n + w], device_id=(cx, cy, c), device_id_type=MESH))
        for cp in copies:
            cp.start()
        for cp in copies:
            cp.wait()

    out_shape = [_sds((3,) + p.shape[1:], p.dtype) for p in pbs]
    scratch = [pltpu.SemaphoreType.DMA((3 * n,)), pltpu.SemaphoreType.DMA((3 * n,))]
    return _comm_call(body, "comm_rs_chips", n, out_shape, scratch)(*pbs)


def _rs_final_sum(gw, recv_sib, recv_chips, shard_core):
    _, rows, cols = gw.shape
    hw = rows // 2

    def body(sc_ref, g_ref, r_ref, rc_ref, o_ref):
        acc = g_ref[0] + r_ref[0]
        for j in range(3):
            acc = acc + rc_ref[j].astype(F32)
        o_ref[0] = acc

    return pl.pallas_call(
        body, name="rs_final_sum",
        grid_spec=pltpu.PrefetchScalarGridSpec(
            num_scalar_prefetch=1, grid=(1,),
            in_specs=[pl.BlockSpec((1, hw, cols), lambda i, sc: (sc[0], sc[1], 0)),
                      pl.BlockSpec((1, hw, cols), lambda i, sc: (sc[0], 0, 0)),
                      pl.BlockSpec((3, hw, cols), lambda i, sc: (0, 0, 0))],
            out_specs=pl.BlockSpec((1, hw, cols), lambda i, sc: (sc[1], 0, 0))),
        out_shape=_sds((2, hw, cols), F32),
        compiler_params=_seq_params(),
    )(shard_core, gw, recv_sib, recv_chips)


def _rs_join_halves(halves):
    n = len(halves)

    def body(*refs):
        bufs = refs[n:2 * n]
        send_sems, recv_sems = refs[2 * n:]
        x, y, c = _coords()
        remote = [pltpu.make_async_remote_copy(
            src_ref=bufs[w].at[c], dst_ref=bufs[w].at[c], send_sem=send_sems.at[w], recv_sem=recv_sems.at[w],
            device_id=(x, y, 1 - c), device_id_type=MESH) for w in range(n)]
        for cp in remote:
            cp.start()
        for w in range(n):
            remote[w].wait_send()
            pltpu.make_async_remote_copy(
                src_ref=bufs[w].at[c], dst_ref=bufs[w].at[1 - c], send_sem=send_sems.at[w],
                recv_sem=recv_sems.at[w], device_id=(x, y, c), device_id_type=MESH).wait_recv()

    joined = pl.pallas_call(
        body, name="comm_rs_join", in_specs=[ANY] * n, out_specs=[ANY] * n,
        out_shape=[_sds(h.shape, h.dtype) for h in halves], input_output_aliases={w: w for w in range(n)},
        scratch_shapes=[pltpu.SemaphoreType.DMA((n,)), pltpu.SemaphoreType.DMA((n,))],
        compiler_params=pltpu.CompilerParams(has_side_effects=True),
    )(*halves)
    return [j.reshape(2 * h.shape[1], h.shape[2]) for j, h in zip(joined, halves)]


def _allreduce_small(buf):
    rows, cols = buf.shape

    def body(in_ref, out_ref, slots, send_sems, recv_sems):
        x, y, c = _coords()
        me = 4 * x + 2 * y + c
        copies, peers = [], []
        for k in range(1, 8):
            px = 1 - x if (k >> 2) & 1 else x
            py = 1 - y if (k >> 1) & 1 else y
            pc = 1 - c if k & 1 else c
            peers.append(4 * px + 2 * py + pc)
            copies.append(pltpu.make_async_remote_copy(
                src_ref=in_ref, dst_ref=slots.at[me], send_sem=send_sems.at[k - 1], recv_sem=recv_sems.at[k - 1],
                device_id=(px, py, pc), device_id_type=MESH))
        for cp in copies:
            cp.start()
        slots[me] = in_ref[...]
        for k in range(7):
            pltpu.make_async_remote_copy(
                src_ref=in_ref, dst_ref=slots.at[peers[k]], send_sem=send_sems.at[k], recv_sem=recv_sems.at[k],
                device_id=(x, y, c), device_id_type=MESH).wait_recv()
        for cp in copies:
            cp.wait_send()
        acc = slots[0]
        for i in range(1, 8):
            acc = acc + slots[i]
        out_ref[...] = acc

    vmem = pl.BlockSpec(memory_space=pltpu.VMEM)
    return pl.pallas_call(
        body, name="comm_allreduce_small", in_specs=[vmem], out_specs=vmem, out_shape=_sds((rows, cols), F32),
        scratch_shapes=[pltpu.VMEM((8, rows, cols), F32), pltpu.SemaphoreType.DMA((7,)), pltpu.SemaphoreType.DMA((7,))],
        compiler_params=pltpu.CompilerParams(has_side_effects=True, vmem_limit_bytes=VMEM_LIMIT),
    )(buf)


def _adamw(w, g, m, v, block_rows, name, after=()):
    rows, cols = w.shape

    def body(w_ref, g_ref, m_ref, v_ref, *rest):
        d_ref, nm_ref, nv_ref = rest[len(after):]
        g = g_ref[...]
        m = ADAM_B1 * m_ref[...] + (1.0 - ADAM_B1) * g
        v = ADAM_B2 * v_ref[...] + (1.0 - ADAM_B2) * (g * g)
        m_hat = m / (1.0 - ADAM_B1 ** ADAM_STEP)
        v_hat = v / (1.0 - ADAM_B2 ** ADAM_STEP)
        d_ref[...] = -ADAM_LR * (m_hat / (jnp.sqrt(v_hat) + ADAM_EPS) + ADAM_WD * w_ref[...])
        nm_ref[...] = m
        nv_ref[...] = v

    spec = pl.BlockSpec((block_rows, cols), lambda i: (i, 0))
    return pl.pallas_call(
        body, name=name, grid=(rows // block_rows,), in_specs=[spec] * 4 + [ANY] * len(after), out_specs=[spec] * 3,
        out_shape=[_sds((rows, cols), F32)] * 3,
        compiler_params=_seq_params(),
    )(w, g, m, v, *after)


WEIGHTS = ("pre_mix_norm", "w_in", "sgu_ln_gain", "sgu_ln_bias", "sgu_w_spatial", "sgu_b_spatial", "attn_out_norm",
           "sgu_out_norm", "w_out", "post_mix_norm", "pre_ffn_norm", "w_gate", "w_up", "w_down", "post_ffn_norm")
BIG = ("w_in", "w_out", "w_gate", "w_up", "w_down")
BIG_ADAM_ROWS = {"w_in": 256, "w_out": 128, "w_gate": 352, "w_up": 352, "w_down": 352}
SMALL = ("pre_mix_norm", "post_mix_norm", "pre_ffn_norm", "post_ffn_norm", "sgu_ln_gain", "sgu_ln_bias",
         "attn_out_norm", "sgu_out_norm", "sgu_w_spatial", "sgu_b_spatial")


def _pack_small(d):
    flat = [d[n].reshape(-1) for n in SMALL]
    used = sum(f.shape[0] for f in flat)
    flat.append(jnp.zeros((SMALL_ROWS * 1024 - used,), F32))
    return jnp.concatenate(flat).reshape(SMALL_ROWS, 1024)


def _unpack_small(buf, shapes):
    flat = buf.reshape(-1)
    out, off = {}, 0
    for n in SMALL:
        size = int(np.prod(shapes[n]))
        out[n] = flat[off:off + size].reshape(shapes[n])
        off += size
    return out


def _kernel_unoverlapped(x, positions, pre_mix_norm, w_in, sgu_ln_gain, sgu_ln_bias, sgu_w_spatial, sgu_b_spatial, attn_out_norm, sgu_out_norm, w_out, post_mix_norm, pre_ffn_norm, w_gate, w_up, w_down, post_ffn_norm, loss_target, m_pre_mix_norm, m_w_in, m_sgu_ln_gain, m_sgu_ln_bias, m_sgu_w_spatial, m_sgu_b_spatial, m_attn_out_norm, m_sgu_out_norm, m_w_out, m_post_mix_norm, m_pre_ffn_norm, m_w_gate, m_w_up, m_w_down, m_post_ffn_norm, v_pre_mix_norm, v_w_in, v_sgu_ln_gain, v_sgu_ln_bias, v_sgu_w_spatial, v_sgu_b_spatial, v_attn_out_norm, v_sgu_out_norm, v_w_out, v_post_mix_norm, v_pre_ffn_norm, v_w_gate, v_w_up, v_w_down, v_post_ffn_norm):
    a = dict(locals())
    cx, cy, cc = _coords()
    core = jnp.stack([cc]).astype(jnp.int32)
    shard_core = jnp.stack([2 * cx + cy, cc]).astype(jnp.int32)

    full = _gather_weights([a[n][0].astype(BF16) for n in BIG])
    small = {n: (a[n][0] if a[n].ndim > 2 else a[n]) for n in SMALL}
    loss_cols, grad_x, gws, small_grads = _local_step(
        x[0], positions.reshape(SEQ, 1), loss_target[0], *full, small)
    loss = lax.psum(jnp.sum(loss_cols) * np.float32(0.5 / D_MODEL), ("x", "y", "c"))

    recv_sib = _rs_to_sibling(list(gws))
    chip_part = [_rs_chip_sum(g, r, core) for g, r in zip(gws, recv_sib)]
    recv_chips = _rs_between_chips(chip_part)
    halves = [_rs_final_sum(g, r, rc, shard_core) for g, r, rc in zip(gws, recv_sib, recv_chips)]
    big_grads = dict(zip(BIG, _rs_join_halves(halves)))

    shapes = {n: a[n].shape for n in SMALL}
    small_sum = _allreduce_small(_pack_small(small_grads))

    grads, deltas, new_m, new_v = {}, {}, {}, {}
    for n in BIG:
        grads[n] = big_grads[n][None]
        d, nm, nv = _adamw(a[n][0], big_grads[n], a["m_" + n][0], a["v_" + n][0], BIG_ADAM_ROWS[n], "adamw_" + n)
        deltas[n], new_m[n], new_v[n] = d[None], nm[None], nv[None]
    d, nm, nv = _adamw(_pack_small({n: a[n] for n in SMALL}), small_sum, _pack_small({n: a["m_" + n] for n in SMALL}),
                       _pack_small({n: a["v_" + n] for n in SMALL}), SMALL_ROWS, "adamw_small")
    grads.update(_unpack_small(small_sum, shapes))
    deltas.update(_unpack_small(d, shapes))
    new_m.update(_unpack_small(nm, shapes))
    new_v.update(_unpack_small(nv, shapes))
    return (loss, grad_x[None], *[grads[n] for n in WEIGHTS], *[deltas[n] for n in WEIGHTS],
            *[new_m[n] for n in WEIGHTS], *[new_v[n] for n in WEIGHTS])


def _remote(src, dst, send_sem, recv_sem, to):
    return pltpu.make_async_remote_copy(src_ref=src, dst_ref=dst, send_sem=send_sem, recv_sem=recv_sem,
                                        device_id=to, device_id_type=MESH)


def _halves(a):
    *lead, rows, cols = a.shape
    return a.reshape(*lead, 2, rows // 2, cols)


def _gather_ici(shards):
    n = len(shards)

    def desc(ins, outs, ss, rs, j, w, landed):
        x, y, c = _coords()
        cx, cy = _other_chips(x, y)[j]
        shard = 2 * cx + cy if landed else 2 * x + y
        return _remote(ins[w].at[c], outs[w].at[shard, c], ss.at[j * n + w], rs.at[j * n + w], (cx, cy, c))

    def start(ins, outs, ss, rs):
        for j in range(3):
            for w in range(n):
                desc(ins, outs, ss, rs, j, w, False).start()

    def finish(ins, outs, ss, rs):
        for j in range(3):
            for w in range(n):
                desc(ins, outs, ss, rs, j, w, True).wait_recv()
                desc(ins, outs, ss, rs, j, w, False).wait_send()

    return _Comm(shards, [_sds((N_SHARD,) + s.shape, s.dtype) for s in shards], 3 * n, start, finish)


def _gather_pass(fulls):
    n = len(fulls)

    def desc(bufs, ss, rs, j, w, landed):
        x, y, c = _coords()
        cx, cy = _other_chips(x, y)[j]
        shard = 2 * cx + cy
        return _remote(bufs[w].at[shard, c], bufs[w].at[shard, 1 - c if landed else c],
                       ss.at[j * n + w], rs.at[j * n + w], (x, y, 1 - c))

    def start(ins, outs, ss, rs):
        for j in range(3):
            for w in range(n):
                desc(outs, ss, rs, j, w, False).start()

    def finish(ins, outs, ss, rs):
        for j in range(3):
            for w in range(n):
                desc(outs, ss, rs, j, w, True).wait_recv()
                desc(outs, ss, rs, j, w, False).wait_send()

    return _Comm(fulls, [_sds(f.shape, f.dtype) for f in fulls], 3 * n, start, finish, aliased=True)


def _rs_sibling(gws):
    n = len(gws)

    def desc(ins, outs, ss, rs, w):
        x, y, c = _coords()
        return _remote(ins[w].at[:, 1 - c], outs[w], ss.at[w], rs.at[w], (x, y, 1 - c))

    def start(ins, outs, ss, rs):
        for w in range(n):
            desc(ins, outs, ss, rs, w).start()

    def finish(ins, outs, ss, rs):
        for w in range(n):
            desc(ins, outs, ss, rs, w).wait()

    out_shape = [_sds((N_SHARD, g.shape[1] // 2, g.shape[2]), g.dtype) for g in gws]
    return _Comm([_halves(g) for g in gws], out_shape, n, start, finish)


def _rs_chips(pbs):
    n = len(pbs)

    def desc(ins, outs, ss, rs, j, w):
        x, y, c = _coords()
        cx, cy = _other_chips(x, y)[j]
        return _remote(ins[w].at[2 * cx + cy], outs[w].at[j], ss.at[j * n + w], rs.at[j * n + w], (cx, cy, c))

    def start(ins, outs, ss, rs):
        for j in range(3):
            for w in range(n):
                desc(ins, outs, ss, rs, j, w).start()

    def finish(ins, outs, ss, rs):
        for j in range(3):
            for w in range(n):
                desc(ins, outs, ss, rs, j, w).wait()

    return _Comm(pbs, [_sds((3,) + p.shape[1:], p.dtype) for p in pbs], 3 * n, start, finish)


def _rs_join(halves):
    n = len(halves)

    def desc(bufs, ss, rs, w, landed):
        x, y, c = _coords()
        return _remote(bufs[w].at[c], bufs[w].at[1 - c if landed else c], ss.at[w], rs.at[w], (x, y, 1 - c))

    def start(ins, outs, ss, rs):
        for w in range(n):
            desc(outs, ss, rs, w, False).start()

    def finish(ins, outs, ss, rs):
        for w in range(n):
            desc(outs, ss, rs, w, True).wait_recv()
            desc(outs, ss, rs, w, False).wait_send()

    return _Comm(halves, [_sds(h.shape, h.dtype) for h in halves], n, start, finish, aliased=True)


def _small_exchange(buf):
    def desc(ins, outs, ss, rs, k, landed):
        x, y, c = _coords()
        px = 1 - x if (k >> 2) & 1 else x
        py = 1 - y if (k >> 1) & 1 else y
        pc = 1 - c if k & 1 else c
        slot = 4 * px + 2 * py + pc if landed else 4 * x + 2 * y + c
        return _remote(ins[0], outs[0].at[slot], ss.at[k - 1], rs.at[k - 1], (px, py, pc))

    def start(ins, outs, ss, rs):
        for k in range(1, 8):
            desc(ins, outs, ss, rs, k, False).start()

    def finish(ins, outs, ss, rs):
        for k in range(1, 8):
            desc(ins, outs, ss, rs, k, True).wait_recv()
            desc(ins, outs, ss, rs, k, False).wait_send()

    return _Comm([buf], [_sds((8,) + buf.shape, buf.dtype)], 7, start, finish)


HBM = pl.BlockSpec(memory_space=pltpu.HBM)
SEM = pl.BlockSpec(memory_space=pltpu.SEMAPHORE)
DATAFLOW = pltpu.SideEffectType.DATAFLOW_SIDE_EFFECTING


def _split_start(name, comm, lands, after):
    srcs = [pltpu.with_memory_space_constraint(s, pltpu.HBM) for s in comm.args]
    lands = [pltpu.with_memory_space_constraint(b, pltpu.HBM) for b in lands]
    ns, nb = len(srcs), len(lands)

    def body(*refs):
        send_sems, recv_sems = refs[ns + nb + 1], refs[ns + nb + 2]
        comm.start(refs[:ns], refs[ns:ns + nb], send_sems, recv_sems)
        refs[-1][...] = jnp.zeros_like(refs[-1])

    res = pl.pallas_call(
        body, name=name,
        out_shape=(pltpu.SemaphoreType.DMA((comm.n_sems,)), pltpu.SemaphoreType.DMA((comm.n_sems,)),
                   *[pltpu.HBM(b.shape, b.dtype) for b in srcs + lands], _sds((8, 128), F32)),
        in_specs=[HBM] * (ns + nb) + [ANY],
        out_specs=(SEM, SEM, *[HBM] * (ns + nb), pl.BlockSpec(memory_space=pltpu.VMEM)),
        input_output_aliases={i: 2 + i for i in range(ns + nb)},
        compiler_params=pltpu.CompilerParams(has_side_effects=DATAFLOW),
    )(*srcs, *lands, after)
    return res[0], res[1], list(res[2:2 + ns]), list(res[2 + ns:2 + ns + nb]), res[-1]


def _split_starts(name, comms, after):
    srcs = [[pltpu.with_memory_space_constraint(s, pltpu.HBM) for s in c.args] for c in comms]
    lands = [[pltpu.with_memory_space_constraint(lax.empty(o.shape, o.dtype), pltpu.HBM) for o in c.out_shape]
             for c in comms]
    bufs = [b for k in range(len(comms)) for b in srcs[k] + lands[k]]
    nb, nc = len(bufs), len(comms)

    def body(*refs):
        sems = refs[nb + 1:nb + 1 + 2 * nc]
        off = 0
        for k, c in enumerate(comms):
            ns, nl = len(srcs[k]), len(lands[k])
            c.start(refs[off:off + ns], refs[off + ns:off + ns + nl], sems[2 * k], sems[2 * k + 1])
            off += ns + nl
        refs[-1][...] = jnp.zeros_like(refs[-1])

    res = pl.pallas_call(
        body, name=name,
        out_shape=(*[pltpu.SemaphoreType.DMA((c.n_sems,)) for c in comms for _ in range(2)],
                   *[pltpu.HBM(b.shape, b.dtype) for b in bufs], _sds((8, 128), F32)),
        in_specs=[HBM] * nb + [ANY],
        out_specs=(*[SEM] * (2 * nc), *[HBM] * nb, pl.BlockSpec(memory_space=pltpu.VMEM)),
        input_output_aliases={i: 2 * nc + i for i in range(nb)},
        compiler_params=pltpu.CompilerParams(has_side_effects=DATAFLOW),
    )(*bufs, after)
    states, off = [], 2 * nc
    for k in range(nc):
        ns, nl = len(srcs[k]), len(lands[k])
        states.append((res[2 * k], res[2 * k + 1], list(res[off:off + ns]), list(res[off + ns:off + ns + nl])))
        off += ns + nl
    return states, res[-1]


def _split_wait(name, comm, send_sems, recv_sems, srcs, lands, after):
    ns, nb = len(srcs), len(lands)
    after = list(after) if isinstance(after, (list, tuple)) else [after]

    def body(*refs):
        comm.finish(refs[:ns], refs[ns:ns + nb], refs[ns + nb], refs[ns + nb + 1])

    res = pl.pallas_call(
        body, name=name,
        out_shape=tuple(pltpu.HBM(b.shape, b.dtype) for b in srcs + lands),
        in_specs=[HBM] * (ns + nb) + [SEM, SEM] + [ANY] * len(after), out_specs=tuple([HBM] * (ns + nb)),
        input_output_aliases={i: i for i in range(ns + nb)},
        compiler_params=pltpu.CompilerParams(has_side_effects=DATAFLOW),
    )(*srcs, *lands, send_sems, recv_sems, *after)
    return list(res[ns:])


LOSS_ROW = "loss_cols"
SMALL_EARLY = ("post_mix_norm", "pre_ffn_norm", "post_ffn_norm", "sgu_ln_gain", "sgu_ln_bias", "attn_out_norm",
               "sgu_out_norm", "sgu_w_spatial", "sgu_b_spatial", LOSS_ROW)
SMALL_LATE = ("pre_mix_norm",)


def _pack(d, names, rows):
    flat = [d[n].reshape(-1) for n in names]
    used = sum(f.shape[0] for f in flat)
    flat.append(jnp.zeros((rows * 1024 - used,), F32))
    return jnp.concatenate(flat).reshape(rows, 1024)


def _unpack(buf, names, shapes):
    flat = buf.reshape(-1)
    out, off = {}, 0
    for n in names:
        size = int(np.prod(shapes[n]))
        out[n] = flat[off:off + size].reshape(shapes[n])
        off += size
    return out


def _merge(comms):
    def run(phase):
        def go(ins, outs, ss, rs):
            ii = oi = si = 0
            for c in comms:
                getattr(c, phase)(ins[ii:ii + len(c.args)], outs[oi:oi + len(c.out_shape)],
                                  ss.at[pl.ds(si, c.n_sems)], rs.at[pl.ds(si, c.n_sems)])
                ii += len(c.args)
                oi += len(c.out_shape)
                si += c.n_sems
        return go

    return _Comm([a for c in comms for a in c.args], [o for c in comms for o in c.out_shape],
                 sum(c.n_sems for c in comms), run("start"), run("finish"))


def _chip_sums(gws, recvs, shard_core):
    n = len(gws)
    _, rows, cols = gws[0].shape
    hw = rows // 2

    def body(sc_ref, *refs):
        for k in range(n):
            refs[2 * n + k][...] = (refs[k][...] + refs[n + k][...]).astype(BF16)

    def other(s, sc):
        return jnp.where(s >= sc[0], s + 1, s)

    mine = pl.BlockSpec((1, hw, cols), lambda s, sc: (other(s, sc), sc[1], 0))
    plain = pl.BlockSpec((1, hw, cols), lambda s, sc: (other(s, sc), 0, 0))
    return pl.pallas_call(
        body, name="rs_chip_sums",
        grid_spec=pltpu.PrefetchScalarGridSpec(num_scalar_prefetch=1, grid=(N_SHARD - 1,),
                                               in_specs=[mine] * n + [plain] * n, out_specs=[plain] * n),
        out_shape=[_sds((N_SHARD, hw, cols), BF16)] * n,
        compiler_params=_seq_params(),
    )(shard_core, *gws, *recvs)


def _final_sums(gws, recv_sibs, recv_chips, shard_core):
    n = len(gws)
    _, rows, cols = gws[0].shape
    hw = rows // 2

    def body(sc_ref, *refs):
        for k in range(n):
            acc = refs[k][0] + refs[n + k][0]
            for j in range(3):
                acc = acc + refs[2 * n + k][j].astype(F32)
            refs[3 * n + k][0] = acc

    return pl.pallas_call(
        body, name="rs_final_sums",
        grid_spec=pltpu.PrefetchScalarGridSpec(
            num_scalar_prefetch=1, grid=(1,),
            in_specs=[pl.BlockSpec((1, hw, cols), lambda i, sc: (sc[0], sc[1], 0))] * n
            + [pl.BlockSpec((1, hw, cols), lambda i, sc: (sc[0], 0, 0))] * n
            + [pl.BlockSpec((3, hw, cols), lambda i, sc: (0, 0, 0))] * n,
            out_specs=[pl.BlockSpec((1, hw, cols), lambda i, sc: (sc[1], 0, 0))] * n),
        out_shape=[_sds((2, hw, cols), F32)] * n,
        compiler_params=_seq_params(),
    )(shard_core, *gws, *recv_sibs, *recv_chips)


def _adamw_multi(ws, gs, ms, vs, block_rows, name, after=()):
    n = len(ws)
    rows, cols = ws[0].shape

    def body(*refs):
        outs = refs[4 * n + len(after):]
        for k in range(n):
            g = refs[n + k][...]
            d, m, v = _adam_math(refs[k][...], g, refs[2 * n + k][...], refs[3 * n + k][...])
            outs[4 * k][...], outs[4 * k + 1][...], outs[4 * k + 2][...], outs[4 * k + 3][...] = g, d, m, v

    spec = pl.BlockSpec((block_rows, cols), lambda i: (i, 0))
    res = pl.pallas_call(
        body, name=name, grid=(rows // block_rows,), in_specs=[spec] * (4 * n) + [ANY] * len(after),
        out_specs=[spec] * (4 * n), out_shape=[_sds((rows, cols), F32)] * (4 * n),
        compiler_params=_seq_params(),
    )(*ws, *gs, *ms, *vs, *after)
    return [tuple(res[4 * k:4 * k + 4]) for k in range(n)]


def _wgrad_ff(a_list, b, name, comms=()):
    n = len(a_list)
    cols = 256

    def body(*refs):
        bv = refs[n][...]
        for k in range(n):
            refs[n + 1 + k][...] = _dot_tn(refs[k][...], bv)

    res = _pcall(
        body, name=name, grid=(FF // cols,),
        in_specs=[pl.BlockSpec((SEQ, cols), lambda j: (0, j))] * n
        + [pl.BlockSpec((SEQ, D_MODEL), lambda j: (0, 0), pipeline_mode=pl.Buffered(1))],
        out_specs=[pl.BlockSpec((cols, D_MODEL), lambda j: (j, 0))] * n,
        out_shape=[_sds((FF, D_MODEL), F32)] * n, args=(*a_list, b), comms=comms)
    mine, theirs = res if comms else (res, None)
    mine = [m.reshape(N_SHARD, FF_S, D_MODEL) for m in mine]
    return (mine, theirs) if comms else mine


def _wgrad_pair(a1, a2, b, a_spec, b_spec, out_block, name, comms=()):
    def body(a1_ref, a2_ref, b_ref, o1_ref, o2_ref):
        bv = b_ref[...]
        o1_ref[0] = _dot_tn(a1_ref[0], bv)
        o2_ref[0] = _dot_tn(a2_ref[0], bv)

    out_spec = pl.BlockSpec((1,) + out_block, lambda s: (s, 0, 0))
    return _pcall(
        body, name=name, grid=(N_SHARD,), in_specs=[a_spec, a_spec, b_spec], out_specs=[out_spec, out_spec],
        out_shape=[_sds((N_SHARD,) + out_block, F32)] * 2, args=(a1, a2, b), comms=comms)


def _comm_only(name, comms):
    return _pcall(lambda: None, name=name, grid=(1,), in_specs=[], out_specs=[], out_shape=[], args=(),
                  comms=comms)[1]


def _adam_math(w, g, m, v):
    m = ADAM_B1 * m + (1.0 - ADAM_B1) * g
    v = ADAM_B2 * v + (1.0 - ADAM_B2) * (g * g)
    m_hat = m / (1.0 - ADAM_B1 ** ADAM_STEP)
    v_hat = v / (1.0 - ADAM_B2 ** ADAM_STEP)
    return -ADAM_LR * (m_hat / (jnp.sqrt(v_hat) + ADAM_EPS) + ADAM_WD * w), m, v


def _adamw_small(own, slots, w, m, v, me):
    rows, cols = own.shape

    def body(me_ref, own_ref, slots_ref, w_ref, m_ref, v_ref, g_ref, d_ref, nm_ref, nv_ref):
        own_v = own_ref[...]
        g = jnp.where(me_ref[0] == 0, own_v, slots_ref[0])
        for i in range(1, 8):
            g = g + jnp.where(me_ref[0] == i, own_v, slots_ref[i])
        g_ref[...] = g
        d_ref[...], nm_ref[...], nv_ref[...] = _adam_math(w_ref[...], g, m_ref[...], v_ref[...])

    flat = pl.BlockSpec((rows, cols), lambda i, me_ref: (0, 0))
    return pl.pallas_call(
        body, name="adamw_small",
        grid_spec=pltpu.PrefetchScalarGridSpec(
            num_scalar_prefetch=1, grid=(1,),
            in_specs=[flat, pl.BlockSpec((8, rows, cols), lambda i, me_ref: (0, 0, 0)), flat, flat, flat],
            out_specs=[flat] * 4),
        out_shape=[_sds((rows, cols), F32)] * 4,
        compiler_params=_seq_params(),
    )(me, own, slots, w, m, v)


def kernel(x, positions, pre_mix_norm, w_in, sgu_ln_gain, sgu_ln_bias, sgu_w_spatial, sgu_b_spatial, attn_out_norm, sgu_out_norm, w_out, post_mix_norm, pre_ffn_norm, w_gate, w_up, w_down, post_ffn_norm, loss_target, m_pre_mix_norm, m_w_in, m_sgu_ln_gain, m_sgu_ln_bias, m_sgu_w_spatial, m_sgu_b_spatial, m_attn_out_norm, m_sgu_out_norm, m_w_out, m_post_mix_norm, m_pre_ffn_norm, m_w_gate, m_w_up, m_w_down, m_post_ffn_norm, v_pre_mix_norm, v_w_in, v_sgu_ln_gain, v_sgu_ln_bias, v_sgu_w_spatial, v_sgu_b_spatial, v_attn_out_norm, v_sgu_out_norm, v_w_out, v_post_mix_norm, v_pre_ffn_norm, v_w_gate, v_w_up, v_w_down, v_post_ffn_norm):
    a = dict(locals())
    cx, cy, cc = _coords()
    s_me = 2 * cx + cy
    core = jnp.stack([cc]).astype(jnp.int32)
    shard_core = jnp.stack([s_me, cc]).astype(jnp.int32)
    me = jnp.stack([4 * cx + 2 * cy + cc]).astype(jnp.int32)
    small = {n: (a[n][0] if a[n].ndim > 2 else a[n]) for n in SMALL}
    b_t = small["sgu_b_spatial"].T
    xs, pos, target = x[0], positions.reshape(SEQ, 1), loss_target[0]
    flipped = ("w_gate", "w_up")

    def big(name, n):
        return jnp.swapaxes(a[name], 1, 2)[0] if n in flipped else a[name][0]

    own = {"w_in": _halves(big("w_in", "w_in").astype(BF16))}

    def with_own(full, n):
        full = lax.dynamic_update_slice(full, own[n][None], (s_me, 0, 0, 0))
        return full.reshape((N_SHARD,) + big(n, n).shape)

    ffn = ("w_gate", "w_up", "w_down")
    g_in = _gather_ici([own["w_in"]])
    (s_in,), token = _split_starts("gather_in_start", [g_in], small["pre_mix_norm"])
    for n in ("w_out",) + ffn:
        own[n] = _halves((big(n, n) + token[0:1, 0:1]).astype(BF16))
    g_out, g_ffn = _gather_ici([own["w_out"]]), _gather_ici([own[n] for n in ffn])
    (s_out, s_ffn), token = _split_starts("gather_rest_start", [g_out, g_ffn], token)
    in_lands = _split_wait("gather_in_wait", g_in, *s_in, token)
    ((in_lands,),) = _comm_only("comm_pass_in", [_gather_pass(in_lands)])
    w_in_f = with_own(in_lands, "w_in")
    h, u, vs, sgu, *qkv = _inproj_fwd(xs, pos, small["pre_mix_norm"], w_in_f, small["sgu_ln_gain"],
                                      small["sgu_ln_bias"], small["sgu_w_spatial"], b_t)
    views = [tuple(qkv[3 * i:3 * i + 3]) for i in range(len(DILATIONS))]
    out_lands = _split_wait("gather_out_wait", g_out, *s_out, sgu)
    o_list, l_list = [], []
    for dil, (qv, kv, vv) in zip(DILATIONS, views):
        if dil == 1:
            (o, l), ((out_lands,),) = _attn_fwd(qv, kv, vv, dil, comms=[_gather_pass(out_lands)])
        else:
            o, l = _attn_fwd(qv, kv, vv, dil)
        o_list.append(o)
        l_list.append(l)
    w_out_f = with_own(out_lands, "w_out")
    ffn_lands = _split_wait("gather_ffn_wait", g_ffn, *s_ffn, l_list[-1])
    (attn, mixed, y, x1, *lses), (ffn_lands,) = _mix_out_fwd(
        o_list, l_list, sgu, xs, w_out_f, small["attn_out_norm"], small["sgu_out_norm"], small["post_mix_norm"],
        comms=[_gather_pass(ffn_lands)])
    w_gate_f, w_up_f, w_down_f = (with_own(f, n).reshape(FF, D_MODEL) for f, n in zip(ffn_lands, ffn))
    h2, act, dg, dup, df, dx1, loss_cols, d_pre_ffn, d_post_ffn = _ffn_fwd_bwd(
        x1, target, w_gate_f, w_up_f, w_down_f, small["pre_ffn_norm"], small["post_ffn_norm"])

    full_tok = pl.BlockSpec((SEQ, D_MODEL), lambda s: (0, 0), pipeline_mode=pl.Buffered(1))
    ff_tok = pl.BlockSpec((1, SEQ, FF_S), lambda s: (s, 0, 0))
    gw = {}
    gw["w_gate"], gw["w_up"] = _wgrad_ff([dg, dup], h2, "wgrad_gate_up")
    (gw["w_down"],), ((sib_gate,),) = _wgrad_ff([act], df, "wgrad_down", comms=[_rs_sibling([gw["w_gate"]])])
    (dy, du, dvs_sgu, d_post_mix, d_attn_norm, d_sgu_norm, d_w_sp, d_b_sp, d_ln_gain, d_ln_bias,
     *dviews), ((sib_up, sib_down),) = _outproj_bwd(
        dx1, y, attn, sgu, w_out_f, small["post_mix_norm"], small["attn_out_norm"], small["sgu_out_norm"],
        u, vs, small["sgu_ln_gain"], small["sgu_ln_bias"], small["sgu_w_spatial"], b_t,
        comms=[_rs_sibling([gw["w_up"], gw["w_down"]])])
    sib = {"w_gate": sib_gate, "w_up": sib_up, "w_down": sib_down}
    part = dict(zip(ffn, _chip_sums([gw[n] for n in ffn], [sib[n] for n in ffn], shard_core)))
    gw["w_out"] = _wgrad(mixed, dy, pl.BlockSpec((SEQ, OUT_S), lambda s: (0, s)), full_tok, (OUT_S, D_MODEL),
                         "wgrad_out")
    x_ffn = _rs_chips([part[n] for n in ffn])
    (s_ffn,), token = _split_starts("rs_ffn_start", [x_ffn], small["pre_mix_norm"])
    packed_early = _pack({
        "sgu_ln_gain": d_ln_gain, "sgu_ln_bias": d_ln_bias, "sgu_w_spatial": d_w_sp, "sgu_b_spatial": d_b_sp,
        "attn_out_norm": d_attn_norm, "sgu_out_norm": d_sgu_norm, "post_mix_norm": d_post_mix,
        "pre_ffn_norm": d_pre_ffn, "post_ffn_norm": d_post_ffn, LOSS_ROW: loss_cols}, SMALL_EARLY, SMALL_ROWS)

    dqs, dks, dvs = [], [], []
    for i, (dil, (qv, kv, vv)) in enumerate(zip(DILATIONS, views)):
        if dil == 1:
            (dq, dk, dv), ((sib["w_out"],),) = _attn_bwd(qv, kv, vv, dviews[i], dviews[3 + i], lses[i], dil,
                                                        comms=[_rs_sibling([gw["w_out"]])], after=[token])
            (part["w_out"],) = _chip_sums([gw["w_out"]], [sib["w_out"]], shard_core)
            x_out, x_small = _rs_chips([part["w_out"]]), _small_exchange(packed_early)
            (s_out, s_small), token = _split_starts("rs_out_small_start", [x_out, x_small], token)
        else:
            dq, dk, dv = _attn_bwd(qv, kv, vv, dviews[i], dviews[3 + i], lses[i], dil, after=[token])
        dqs.append(dq)
        dks.append(dk)
        dvs.append(dv)
    half, far, joined = {}, {}, {}
    far.update(zip(ffn, _split_wait("rs_ffn_wait", x_ffn, *s_ffn, dvs[-1])))
    half.update(zip(ffn, _final_sums([gw[n] for n in ffn], [sib[n] for n in ffn], [far[n] for n in ffn],
                                     shard_core)))
    dproj, grad_x, d_pre_mix = _inproj_bwd(dqs, dks, dvs, du, dvs_sgu, pos, xs, dx1, w_in_f, small["pre_mix_norm"])
    (far["w_out"],) = _split_wait("rs_out_wait", x_out, *s_out, grad_x)
    half["w_out"] = _rs_final_sum(gw["w_out"], sib["w_out"], far["w_out"], shard_core)
    packed_late = _pack({"pre_mix_norm": d_pre_mix}, SMALL_LATE, 8)
    names = ffn + ("w_out",)
    gw["w_in"], (got, (slots_late,)) = _wgrad(
        h, dproj, full_tok, pl.BlockSpec((SEQ, IN_S), lambda s: (0, s)), (D_MODEL, IN_S), "wgrad_in",
        comms=[_rs_join([half[n] for n in names]), _small_exchange(packed_late)])
    joined.update(zip(names, got))
    ((sib["w_in"],),) = _comm_only("comm_rs_sibling_in", [_rs_sibling([gw["w_in"]])])
    (part["w_in"],) = _chip_sums([gw["w_in"]], [sib["w_in"]], shard_core)
    x_in = _rs_chips([part["w_in"]])
    (s_in,), token = _split_starts("rs_in_start", [x_in], small["pre_mix_norm"])

    grads, deltas, new_m, new_v = {}, {}, {}, {}

    def record(n, outs):
        grads[n], deltas[n], new_m[n], new_v[n] = (
            jnp.swapaxes(o[None], 1, 2) if n in flipped else o[None] for o in outs)

    def update(names, block_rows, name, after):
        for n, outs in zip(names, _adamw_multi(
                [big(n, n) for n in names], [joined[n].reshape(big(n, n).shape) for n in names],
                [big("m_" + n, n) for n in names], [big("v_" + n, n) for n in names], block_rows, name, after)):
            record(n, outs)

    update(ffn, FF_S // 4, "adamw_ffn", [token])
    update(("w_out",), BIG_ADAM_ROWS["w_out"], "adamw_w_out", [token])
    (slots_early,) = _split_wait("small_early_wait", x_small, *s_small, [new_v[n] for n in ("w_down", "w_out")])
    (far["w_in"],) = _split_wait("rs_in_wait", x_in, *s_in, slots_early)
    half["w_in"] = _rs_final_sum(gw["w_in"], sib["w_in"], far["w_in"], shard_core)
    ((joined["w_in"],),) = _comm_only("comm_rs_join_in", [_rs_join([half["w_in"]])])
    update(("w_in",), BIG_ADAM_ROWS["w_in"], "adamw_w_in", [])
    a[LOSS_ROW] = a["m_" + LOSS_ROW] = a["v_" + LOSS_ROW] = jnp.zeros((1, D_MODEL), F32)
    for names, rows, packed, slots in ((SMALL_EARLY, SMALL_ROWS, packed_early, slots_early),
                                       (SMALL_LATE, 8, packed_late, slots_late)):
        outs = _adamw_small(packed, slots, _pack(a, names, rows), _pack({n: a["m_" + n] for n in names}, names, rows),
                            _pack({n: a["v_" + n] for n in names}, names, rows), me)
        for dst, buf in zip((grads, deltas, new_m, new_v), outs):
            dst.update(_unpack(buf, names, {n: a[n].shape for n in names}))
    loss = jnp.sum(grads[LOSS_ROW]) * np.float32(0.5 / D_MODEL)
    return (loss, grad_x[None], *[grads[n] for n in WEIGHTS], *[deltas[n] for n in WEIGHTS],
            *[new_m[n] for n in WEIGHTS], *[new_v[n] for n in WEIGHTS])
```

```python
import numpy as np
import jax
import jax.numpy as jnp
from jax import lax
from jax.experimental import pallas as pl
from jax.experimental.pallas import tpu as pltpu

F32 = jnp.float32
BF16 = jnp.bfloat16

SEQ = 2048
D_MODEL = 1024
HEAD_DIM = 64
ATTN_W = 512
SGU_W = 512
SGU_GROUPS = 8
CHUNK = 128
DILATIONS = (1, 4, 16)
N_SHARD = 4
IN_S = 640
OUT_S = 256
FF_S = 704
PROJ_W = N_SHARD * IN_S
FF = N_SHARD * FF_S
FF_CHUNKS = ((0, 1024), (1024, 2048), (2048, FF))
RMS_EPS = 1e-6
LN_EPS = 1e-5
ROPE_THETA = 500000.0
ATTN_SCALE = 1.0 / np.sqrt(HEAD_DIM)
NEG = -1e30
TM = 512
TM_FFN = 256
VMEM_LIMIT = 56 * 1024 * 1024
SMALL_ROWS = 136

ADAM_LR = 0.001
ADAM_B1 = 0.9
ADAM_B2 = 0.999
ADAM_EPS = 1e-08
ADAM_WD = 0.01
ADAM_STEP = 10

MESH = pl.DeviceIdType.MESH
ANY = pl.BlockSpec(memory_space=pl.ANY)


def _dot(a, b):
    return jnp.dot(a, b, preferred_element_type=F32)


def _dot_nt(a, b):
    return lax.dot_general(a, b, (((1,), (1,)), ((), ())), preferred_element_type=F32)


def _dot_tn(a, b):
    return lax.dot_general(a, b, (((0,), (0,)), ((), ())), preferred_element_type=F32)


def _dot_exact(a, b):
    return jnp.dot(a, b, preferred_element_type=F32, precision=lax.Precision.HIGHEST)


def _dot_select(a, sel):
    hi = a.astype(BF16)
    lo = (a - hi.astype(F32)).astype(BF16)
    sel = sel.astype(BF16)
    return _dot(hi, sel) + _dot(lo, sel)


def _rms_stats(x):
    r = lax.rsqrt(jnp.mean(x * x, axis=-1, keepdims=True) + RMS_EPS)
    return x * r, r


def _rms_bwd(xh, r, gain, dy):
    dxh = dy * gain
    dx = r * (dxh - xh * jnp.mean(dxh * xh, axis=-1, keepdims=True))
    return dx, jnp.sum(dy * xh, axis=0, keepdims=True)


_ERF_ALPHA = (-2.72614225801306e-10, 2.77068142495902e-08, -2.10102402082508e-06, -5.69250639462346e-05,
              -7.34990630326855e-04, -2.95459980854025e-03, -1.60960333262415e-02)
_ERF_BETA = (-1.45660718464996e-05, -2.13374055278905e-04, -1.68282697438203e-03, -7.37332916720468e-03,
             -1.42647390514189e-02)


def _erf(x):
    x = jnp.clip(x, -4.0, 4.0)
    x2 = x * x
    p = jnp.full_like(x, _ERF_ALPHA[0])
    for a in _ERF_ALPHA[1:]:
        p = p * x2 + a
    q = jnp.full_like(x, _ERF_BETA[0])
    for b in _ERF_BETA[1:]:
        q = q * x2 + b
    return x * p / q


def _normal_cdf(x):
    return 0.5 * (1.0 + _erf(x * np.float32(1.0 / np.sqrt(2.0))))


def _gelu_grad(x, cdf):
    pdf = jnp.exp(-0.5 * x * x) * np.float32(1.0 / np.sqrt(2.0 * np.pi))
    return cdf + x * pdf


def _sigmoid(x):
    return 1.0 / (1.0 + jnp.exp(-x))


_INV_FREQ = tuple(float(np.float32(ROPE_THETA ** (-2.0 * j / 16.0))) for j in range(8))


def _rot_tables(pos):
    lane = lax.broadcasted_iota(jnp.int32, (1, 128), 1)
    d = lane & 63
    j = d & 7
    inv = jnp.zeros((1, 128), F32)
    for jj in range(8):
        inv = jnp.where(j == jj, _INV_FREQ[jj], inv)
    ang = pos.astype(F32) * inv
    c = jnp.cos(ang)
    s = jnp.sin(ang)
    cos_t = jnp.where(d < 16, c, 1.0)
    sin_a = jnp.where(d < 8, -s, 0.0)
    sin_b = jnp.where((d >= 8) & (d < 16), s, 0.0)
    return tuple(jnp.tile(t, (1, 4)) for t in (cos_t, sin_a, sin_b))


def _rope(x, tabs):
    cos_t, sin_a, sin_b = tabs
    return x * cos_t + pltpu.roll(x, 504, 1) * sin_a + pltpu.roll(x, 8, 1) * sin_b


def _rope_bwd(dy, tabs):
    cos_t, sin_a, sin_b = tabs
    return dy * cos_t + pltpu.roll(dy * sin_a, 8, 1) + pltpu.roll(dy * sin_b, 504, 1)


def _left_half():
    return lax.broadcasted_iota(jnp.int32, (CHUNK, CHUNK), 1) < HEAD_DIM


def _group_ones():
    lane = lax.broadcasted_iota(jnp.int32, (SGU_GROUPS, SGU_W), 1)
    row = lax.broadcasted_iota(jnp.int32, (SGU_GROUPS, SGU_W), 0)
    return ((lane >> 6) == row).astype(F32)


def _masked_spatial(w_ref):
    row = lax.broadcasted_iota(jnp.int32, (CHUNK, CHUNK), 0)
    col = lax.broadcasted_iota(jnp.int32, (CHUNK, CHUNK), 1)
    return [jnp.where(col <= row, w_ref[g], 0.0).astype(BF16) for g in range(SGU_GROUPS)]


def _sgu_core(u, vs, lg, lb, wm, bias_full):
    tm = u.shape[0]
    cdf_u, cdf_vs = _normal_cdf(u), _normal_cdf(vs)
    gu = u * cdf_u
    gv = vs * cdf_vs
    mu = jnp.mean(gv, axis=-1, keepdims=True)
    xc = gv - mu
    rstd = lax.rsqrt(jnp.mean(xc * xc, axis=-1, keepdims=True) + LN_EPS)
    xh = xc * rstd
    vnb = (xh * lg + lb).astype(BF16)
    left = _left_half()
    rows = []
    for c in range(tm // CHUNK):
        pieces = []
        for p in range(4):
            vp = vnb[c * CHUNK:(c + 1) * CHUNK, p * 128:(p + 1) * 128]
            pieces.append(jnp.where(left, _dot(wm[2 * p], vp), _dot(wm[2 * p + 1], vp)))
        rows.append(jnp.concatenate(pieces, axis=1) + bias_full)
    mixed = jnp.concatenate(rows, axis=0)
    return gu, xh, rstd, vnb, mixed, cdf_u, cdf_vs


def _resident(shape):
    n = len(shape)
    return pl.BlockSpec(shape, lambda *_: (0,) * n, pipeline_mode=pl.Buffered(1))


def _rows(ncol, tm=TM):
    return pl.BlockSpec((tm, ncol), lambda i: (i, 0))


def _rows3(nlead, ncol, tm=TM):
    return pl.BlockSpec((nlead, tm, ncol), lambda i: (0, i, 0))


def _acc(ncol, nrow=1):
    return pl.BlockSpec((nrow, ncol), lambda i: (0, 0))


HEAD_W = 128


def _view_rows(dil, width=ATTN_W, tm=TM):
    return pl.BlockSpec((tm // dil, dil * width), lambda i: (i, 0))


def _view_shape(dil, dtype, width=ATTN_W):
    return _sds((SEQ // dil, dil * width), dtype)


def _slab_scratch():
    return pltpu.VMEM((4, TM, 128), F32)


def _store_view(val, out_ref, slabs, dil):
    width = val.shape[1]
    for j in range(width // 128):
        slabs[j] = val[:, j * 128:(j + 1) * 128]
    for r in range(dil):
        for j in range(width // 128):
            c0 = r * width + j * 128
            out_ref[:, c0:c0 + 128] = slabs.at[j][pl.ds(r, TM // dil, stride=dil), :].astype(out_ref.dtype)


def _load_view(in_ref, slabs, dil, width=ATTN_W):
    for r in range(dil):
        for j in range(width // 128):
            c0 = r * width + j * 128
            slabs.at[j][pl.ds(r, TM // dil, stride=dil), :] = in_ref[:, c0:c0 + 128].astype(F32)
    return jnp.concatenate([slabs[j] for j in range(width // 128)], axis=1)


def _head_spread():
    m = lax.broadcasted_iota(jnp.int32, (HEAD_W, ATTN_W), 0)
    lane = lax.broadcasted_iota(jnp.int32, (HEAD_W, ATTN_W), 1)
    return (m == 16 * (lane >> 6)).astype(F32)


def _head_sum():
    lane = lax.broadcasted_iota(jnp.int32, (ATTN_W, HEAD_W), 0)
    m = lax.broadcasted_iota(jnp.int32, (ATTN_W, HEAD_W), 1)
    return ((lane >> 6) == (m >> 4)).astype(F32)


def _seq_params():
    return pltpu.CompilerParams(dimension_semantics=("arbitrary",), vmem_limit_bytes=VMEM_LIMIT)


def _sds(shape, dtype):
    return jax.ShapeDtypeStruct(shape, dtype)


class _Comm:
    def __init__(self, args, out_shape, n_sems, start, finish, aliased=False):
        self.args, self.out_shape, self.n_sems = list(args), list(out_shape), n_sems
        self.start, self.finish, self.aliased = start, finish, aliased


def _pcall(body, *, name, grid, in_specs, out_specs, out_shape, args, scratch_shapes=(), comms=(), after=()):
    single = not isinstance(out_shape, (list, tuple))
    out_specs = [out_specs] if single else list(out_specs)
    out_shape = [out_shape] if single else list(out_shape)
    n_in, n_out, n_scr = len(in_specs), len(out_shape), len(scratch_shapes)
    c_args = [a for c in comms for a in c.args]
    c_outs = [o for c in comms for o in c.out_shape]
    aliases, ai, ao = {}, n_in, n_out
    for c in comms:
        if c.aliased:
            aliases.update({ai + k: ao + k for k in range(len(c.args))})
        ai += len(c.args)
        ao += len(c.out_shape)
    sems = [pltpu.SemaphoreType.DMA((c.n_sems,)) for c in comms for _ in range(2)]
    steps = grid[0]

    def wrapped(*refs):
        o0 = n_in + len(c_args) + len(after)
        s0 = o0 + n_out + len(c_outs)
        m_in, m_out, m_sem = refs[n_in:n_in + len(c_args)], refs[o0 + n_out:s0], refs[s0 + n_scr:]

        def each(phase):
            ii = oi = 0
            for k, c in enumerate(comms):
                getattr(c, phase)(m_in[ii:ii + len(c.args)], m_out[oi:oi + len(c.out_shape)],
                                  m_sem[2 * k], m_sem[2 * k + 1])
                ii += len(c.args)
                oi += len(c.out_shape)

        if comms:
            @pl.when(pl.program_id(0) == 0)
            def _():
                each("start")

        body(*refs[:n_in], *refs[o0:o0 + n_out], *refs[s0:s0 + n_scr])

        if comms:
            @pl.when(pl.program_id(0) == steps - 1)
            def _():
                each("finish")

    res = pl.pallas_call(
        wrapped, name=name, grid=grid,
        in_specs=list(in_specs) + [ANY] * (len(c_args) + len(after)), out_specs=out_specs + [ANY] * len(c_outs),
        out_shape=out_shape + c_outs, scratch_shapes=list(scratch_shapes) + sems,
        input_output_aliases=aliases, compiler_params=_seq_params(),
    )(*args, *c_args, *after)
    mine = res[0] if single else list(res[:n_out])
    if not comms:
        return mine
    theirs, oi = [], n_out
    for c in comms:
        theirs.append(list(res[oi:oi + len(c.out_shape)]))
        oi += len(c.out_shape)
    return mine, theirs


def _in_pieces(g):
    lo, hi = 512 * g, 512 * (g + 1)
    return [(s, max(lo, IN_S * s) - IN_S * s, min(hi, IN_S * (s + 1)) - IN_S * s)
            for s in range(N_SHARD) if max(lo, IN_S * s) < min(hi, IN_S * (s + 1))]


def _inproj_fwd(x, pos, g_pre, w_in, lg, lb, w_sp, b_t, comms=()):
    def body(x_ref, pos_ref, g_ref, w_ref, lg_ref, lb_ref, wsp_ref, bt_ref, h_ref, u_ref, vs_ref, sgu_ref, *rest):
        qkv_refs, slabs = rest[:9], rest[9:]
        xh, _ = _rms_stats(x_ref[...])
        h = (xh * g_ref[...]).astype(BF16)
        h_ref[...] = h
        tabs = _rot_tables(pos_ref[...])

        def group(g):
            return jnp.concatenate([_dot(h, w_ref[s, :, a:b]) for s, a, b in _in_pieces(g)], axis=1)

        for t in range(3):
            val = group(t)
            if t < 2:
                val = _rope(val, tabs)
            if t == 0:
                val = val * np.float32(ATTN_SCALE)
            qkv_refs[t][...] = val.astype(BF16)
            for i, dil in enumerate(DILATIONS[1:]):
                _store_view(val, qkv_refs[3 * (i + 1) + t], slabs[t], dil)
        u = group(3)
        vs = group(4)
        u_ref[...] = u
        vs_ref[...] = vs
        bias_full = _dot_exact(bt_ref[...], _group_ones())
        gu, _, _, _, mixed, _, _ = _sgu_core(u, vs, lg_ref[...], lb_ref[...], _masked_spatial(wsp_ref), bias_full)
        sgu_ref[...] = gu * mixed

    return _pcall(
        body, name="inproj_sgu_fwd", grid=(SEQ // TM,),
        in_specs=[_rows(D_MODEL), _rows(1), _resident((1, D_MODEL)), _resident((N_SHARD, D_MODEL, IN_S)),
                  _resident((1, SGU_W)), _resident((1, SGU_W)), _resident((SGU_GROUPS, CHUNK, CHUNK)),
                  _resident((CHUNK, SGU_GROUPS))],
        out_specs=[_rows(D_MODEL), _rows(512), _rows(512), _rows(512)]
        + [_view_rows(dil) for dil in DILATIONS for _ in range(3)],
        out_shape=[_sds((SEQ, D_MODEL), BF16), _sds((SEQ, 512), F32), _sds((SEQ, 512), F32), _sds((SEQ, 512), F32)]
        + [_view_shape(dil, BF16) for dil in DILATIONS for _ in range(3)],
        scratch_shapes=[_slab_scratch() for _ in range(3)],
        args=(x, pos, g_pre, w_in, lg, lb, w_sp, b_t), comms=comms)


def _sgu_fwd(u, vs, lg, lb, w_sp, b_t, comms=()):
    def body(u_ref, vs_ref, lg_ref, lb_ref, w_ref, bt_ref, out_ref):
        wm = _masked_spatial(w_ref)
        bias_full = _dot_exact(bt_ref[...], _group_ones())
        gu, _, _, _, mixed, _, _ = _sgu_core(u_ref[...], vs_ref[...], lg_ref[...], lb_ref[...], wm, bias_full)
        out_ref[...] = gu * mixed

    return _pcall(
        body, name="sgu_fwd", grid=(SEQ // TM,),
        in_specs=[_rows(SGU_W), _rows(SGU_W), _resident((1, SGU_W)), _resident((1, SGU_W)),
                  _resident((SGU_GROUPS, CHUNK, CHUNK)), _resident((CHUNK, SGU_GROUPS))],
        out_specs=_rows(SGU_W),
        out_shape=_sds((SEQ, SGU_W), F32),
        args=(u, vs, lg, lb, w_sp, b_t), comms=comms)


def _block_masks():
    row = lax.broadcasted_iota(jnp.int32, (CHUNK, CHUNK), 0)
    col = lax.broadcasted_iota(jnp.int32, (CHUNK, CHUNK), 1)
    return col <= row, col >= row


def _attn_fwd(qv, kv, vv, dil, comms=()):
    seg = SEQ // dil
    nblk = seg // CHUNK
    rps = 4 if nblk == 1 else 1

    def body(q_ref, k_ref, v_ref, o_ref, l_ref):
        left = _left_half()
        m_cur, m_prev = _block_masks()
        zero = jnp.zeros((CHUNK, CHUNK), BF16)
        ones = (jnp.where(left, 1.0, 0.0).astype(BF16), jnp.where(left, 0.0, 1.0).astype(BF16))

        sides = tuple(enumerate((left, ~left)))

        def rows(b):
            if isinstance(b, int):
                return b * CHUNK, max(b - 1, 0) * CHUNK
            return pl.multiple_of(b * CHUNK, CHUNK), pl.multiple_of(jnp.maximum(b - 1, 0) * CHUNK, CHUNK)

        def first(rr, b):
            r0, rp = rows(b)
            prev_ok = m_prev & (b > 0)
            tiles, scores = [], []
            for hp in range(4):
                ls = slice(rr * ATTN_W + hp * 128, rr * ATTN_W + (hp + 1) * 128)
                qp = q_ref[pl.ds(r0, CHUNK), ls]
                kc = k_ref[pl.ds(r0, CHUNK), ls]
                kp = k_ref[pl.ds(rp, CHUNK), ls] if nblk > 1 else None
                tiles.append((ls, v_ref[pl.ds(r0, CHUNK), ls], v_ref[pl.ds(rp, CHUNK), ls] if nblk > 1 else None))
                for _, hm in sides:
                    qh = jnp.where(hm, qp, zero)
                    sc = jnp.where(m_cur, _dot_nt(qh, kc), NEG)
                    sp = jnp.where(prev_ok, _dot_nt(qh, kp), NEG) if nblk > 1 else None
                    scores.append((sc, sp))
            return tiles, scores

        def second(scores):
            probs = []
            for sc, sp in scores:
                if nblk > 1:
                    m = jnp.max(jnp.maximum(sc, sp), axis=-1, keepdims=True)
                    pc = jnp.exp(sc - m)
                    pp = jnp.exp(sp - m)
                    probs.append((m, pc.astype(BF16), pp.astype(BF16), (pc + pp).astype(BF16)))
                else:
                    m = jnp.max(sc, axis=-1, keepdims=True)
                    pc = jnp.exp(sc - m).astype(BF16)
                    probs.append((m, pc, None, pc))
            return probs

        def third(rr, b, tiles, probs):
            r0, _ = rows(b)
            for hp, (ls, vc, vp) in enumerate(tiles):
                acc = jnp.zeros((CHUNK, CHUNK), F32)
                den = jnp.zeros((CHUNK, CHUNK), F32)
                for side, hm in sides:
                    _, pc, pp, psum = probs[2 * hp + side]
                    acc = acc + _dot(pc, jnp.where(hm, vc, zero))
                    if nblk > 1:
                        acc = acc + _dot(pp, jnp.where(hm, vp, zero))
                    den = den + _dot(psum, ones[side])
                o_ref[pl.ds(r0, CHUNK), ls] = (acc / den).astype(o_ref.dtype)
                lse = jnp.where(left, probs[2 * hp][0], probs[2 * hp + 1][0]) + jnp.log(den)
                l_ref[pl.ds(r0, CHUNK), rr * HEAD_W + 32 * hp:rr * HEAD_W + 32 * hp + 32] = lse[:, 48:80]

        def run(units):
            data = [first(rr, b) for rr, b in units]
            probs = [second(scores) for _, scores in data]
            for (rr, b), (tiles, _), pr in zip(units, data, probs):
                third(rr, b, tiles, pr)

        if nblk == 1:
            run([(rr, 0) for rr in range(rps)])
        else:
            def one(b, carry):
                run([(0, b)])
                return carry

            lax.fori_loop(0, nblk, one, 0)

    spec = pl.BlockSpec((seg, rps * ATTN_W), lambda r: (0, r))
    return _pcall(
        body, name=f"attn_fwd_d{dil}", grid=(dil // rps,),
        in_specs=[spec, spec, spec], out_specs=[spec, pl.BlockSpec((seg, rps * HEAD_W), lambda r: (0, r))],
        out_shape=[_sds((seg, dil * ATTN_W), BF16), _sds((seg, dil * HEAD_W), F32)],
        args=(qv, kv, vv), comms=comms)


def _lane_left(nrows):
    return lax.broadcasted_iota(jnp.int32, (nrows, CHUNK), 1) < HEAD_DIM


def _attn_rows(b):
    if isinstance(b, int):
        return b * CHUNK, max(b - 1, 0) * CHUNK
    return pl.multiple_of(b * CHUNK, CHUNK), pl.multiple_of(jnp.maximum(b - 1, 0) * CHUNK, CHUNK)


def _attn_fwd_fused(qv, kv, vv, dil, comms=()):
    seg = SEQ // dil
    nblk = seg // CHUNK
    rps = 4 if nblk == 1 else 1

    def body(q_ref, k_ref, v_ref, o_ref, l_ref):
        left = _left_half()
        m_cur, m_prev = _block_masks()
        sides = tuple(enumerate((left, ~left)))
        zero = jnp.zeros((CHUNK, CHUNK), BF16)

        def first(rr, b, both):
            r0, rp = _attn_rows(b)
            keys = pl.ds(rp, 2 * CHUNK) if both else pl.ds(r0, CHUNK)
            ok = jnp.concatenate([m_prev, m_cur], axis=1) if both else m_cur
            tiles, scores = [], []
            for hp in range(4):
                ls = slice(rr * ATTN_W + hp * 128, rr * ATTN_W + (hp + 1) * 128)
                qp = q_ref[pl.ds(r0, CHUNK), ls]
                k2 = k_ref[keys, ls]
                tiles.append((ls, v_ref[keys, ls]))
                for _, hm in sides:
                    scores.append(jnp.where(ok, _dot_nt(jnp.where(hm, qp, zero), k2), NEG))
            return tiles, scores

        def second(scores):
            probs = []
            for s in scores:
                m = jnp.max(s, axis=-1, keepdims=True)
                probs.append((m, jnp.exp(s - m).astype(BF16)))
            return probs

        def third(rr, b, tiles, probs, both):
            r0, _ = _attn_rows(b)
            nk = 2 * CHUNK if both else CHUNK
            left_k = _lane_left(nk)
            ones = (jnp.where(left_k, 1.0, 0.0).astype(BF16), jnp.where(left_k, 0.0, 1.0).astype(BF16))
            zero_k = jnp.zeros((nk, CHUNK), BF16)
            for hp, (ls, v2) in enumerate(tiles):
                acc = jnp.zeros((CHUNK, CHUNK), F32)
                den = jnp.zeros((CHUNK, CHUNK), F32)
                for side in range(2):
                    p = probs[2 * hp + side][1]
                    acc = acc + _dot(p, jnp.where(left_k if side == 0 else ~left_k, v2, zero_k))
                    den = den + _dot(p, ones[side])
                o_ref[pl.ds(r0, CHUNK), ls] = (acc / den).astype(o_ref.dtype)
                lse = jnp.where(left, probs[2 * hp][0], probs[2 * hp + 1][0]) + jnp.log(den)
                l_ref[pl.ds(r0, CHUNK), rr * HEAD_W + 32 * hp:rr * HEAD_W + 32 * hp + 32] = lse[:, 48:80]

        def run(units, both):
            data = [first(rr, b, both) for rr, b in units]
            probs = [second(scores) for _, scores in data]
            for (rr, b), (tiles, _), pr in zip(units, data, probs):
                third(rr, b, tiles, pr, both)

        run([(rr, 0) for rr in range(rps)], False)
        if nblk > 1:
            def one(b, carry):
                run([(0, b)], True)
                return carry

            lax.fori_loop(1, nblk, one, 0)

    spec = pl.BlockSpec((seg, rps * ATTN_W), lambda r: (0, r))
    return _pcall(
        body, name=f"attn_fwd_d{dil}", grid=(dil // rps,),
        in_specs=[spec, spec, spec], out_specs=[spec, pl.BlockSpec((seg, rps * HEAD_W), lambda r: (0, r))],
        out_shape=[_sds((seg, dil * ATTN_W), BF16), _sds((seg, dil * HEAD_W), F32)],
        args=(qv, kv, vv), comms=comms)


def _mix_out_fwd(o_list, l_list, sgu, x, w_out, g_attn, g_sgu, g_post, comms=()):
    def body(o1, o2, o3, l1, l2, l3, sgu_ref, x_ref, w_ref, ga_ref, gs_ref, gp_ref,
             attn_ref, mixed_ref, y_ref, x1_ref, lse1_ref, lse2_ref, lse3_ref, slabs_a, slabs_b):
        os = [o1[...], _load_view(o2, slabs_a, DILATIONS[1]), _load_view(o3, slabs_b, DILATIONS[2])]
        ls = [l1[...], _load_view(l2, slabs_a, DILATIONS[1], HEAD_W), _load_view(l3, slabs_b, DILATIONS[2], HEAD_W)]
        m = jnp.maximum(jnp.maximum(ls[0], ls[1]), ls[2])
        es = [jnp.exp(l - m) for l in ls]
        den = es[0] + es[1] + es[2]
        spread = _head_spread()
        attn = sum(_dot_select(e / den, spread) * o for e, o in zip(es, os))
        attn_ref[...] = attn
        lse = m + jnp.log(den)
        lse1_ref[...] = lse
        _store_view(lse, lse2_ref, slabs_a, DILATIONS[1])
        _store_view(lse, lse3_ref, slabs_b, DILATIONS[2])
        ah, _ = _rms_stats(attn)
        sh, _ = _rms_stats(sgu_ref[...])
        mixed = jnp.concatenate([ah * ga_ref[...], sh * gs_ref[...]], axis=1).astype(BF16)
        mixed_ref[...] = mixed
        y = _dot(mixed[:, 0:OUT_S], w_ref[0])
        for s in range(1, N_SHARD):
            y = y + _dot(mixed[:, s * OUT_S:(s + 1) * OUT_S], w_ref[s])
        y_ref[...] = y
        yh, _ = _rms_stats(y)
        x1_ref[...] = x_ref[...] + yh * gp_ref[...]

    return _pcall(
        body, name="mix_out_fwd", grid=(SEQ // TM,),
        in_specs=[_view_rows(dil) for dil in DILATIONS] + [_view_rows(dil, HEAD_W) for dil in DILATIONS]
        + [_rows(512), _rows(D_MODEL), _resident((N_SHARD, OUT_S, D_MODEL)),
           _resident((1, 512)), _resident((1, 512)), _resident((1, D_MODEL))],
        out_specs=[_rows(512), _rows(D_MODEL), _rows(D_MODEL), _rows(D_MODEL)]
        + [_view_rows(dil, HEAD_W) for dil in DILATIONS],
        out_shape=[_sds((SEQ, 512), F32), _sds((SEQ, D_MODEL), BF16), _sds((SEQ, D_MODEL), F32),
                   _sds((SEQ, D_MODEL), F32)] + [_view_shape(dil, F32, HEAD_W) for dil in DILATIONS],
        scratch_shapes=[_slab_scratch(), _slab_scratch()],
        args=(*o_list, *l_list, sgu, x, w_out, g_attn, g_sgu, g_post), comms=comms)


def _ffn_fwd_bwd(x1, target, w_gate, w_up, w_down, g_pre, g_post, comms=()):
    def body(x1_ref, t_ref, wg_ref, wu_ref, wd_ref, gpf_ref, gpo_ref,
             h2_ref, a_ref, dg_ref, dup_ref, df_ref, dx1_ref, loss_ref, dgpf_ref, dgpo_ref, g_scr, up_scr):
        @pl.when(pl.program_id(0) == 0)
        def _():
            loss_ref[...] = jnp.zeros_like(loss_ref)
            dgpf_ref[...] = jnp.zeros_like(dgpf_ref)
            dgpo_ref[...] = jnp.zeros_like(dgpo_ref)

        x1 = x1_ref[...]
        gpf = gpf_ref[...]
        gpo = gpo_ref[...]
        xh, r = _rms_stats(x1)
        h2 = (xh * gpf).astype(BF16)
        h2_ref[...] = h2
        f = jnp.zeros((TM_FFN, D_MODEL), F32)
        for c0, c1 in FF_CHUNKS:
            g = _dot_nt(h2, wg_ref[c0:c1, :])
            up = _dot_nt(h2, wu_ref[c0:c1, :])
            g_scr[:, c0:c1] = g
            up_scr[:, c0:c1] = up
            a = (g * _sigmoid(g) * up).astype(BF16)
            a_ref[:, c0:c1] = a
            f = f + _dot(a, wd_ref[c0:c1, :])
        fh, rf = _rms_stats(f)
        diff = x1 + fh * gpo - t_ref[...]
        loss_ref[...] += jnp.sum(diff * diff, axis=0, keepdims=True)
        dout = diff * np.float32(1.0 / D_MODEL)
        df, dgpo = _rms_bwd(fh, rf, gpo, dout)
        dgpo_ref[...] += dgpo
        dfb = df.astype(BF16)
        df_ref[...] = dfb
        dh2 = jnp.zeros((TM_FFN, D_MODEL), F32)
        for c0, c1 in FF_CHUNKS:
            da = _dot_nt(dfb, wd_ref[c0:c1, :])
            g = g_scr[:, c0:c1]
            up = up_scr[:, c0:c1]
            sg = _sigmoid(g)
            dup = (da * (g * sg)).astype(BF16)
            dg = (da * up * (sg * (1.0 + g * (1.0 - sg)))).astype(BF16)
            dg_ref[:, c0:c1] = dg
            dup_ref[:, c0:c1] = dup
            dh2 = dh2 + _dot(dg, wg_ref[c0:c1, :]) + _dot(dup, wu_ref[c0:c1, :])
        dx, dgpf = _rms_bwd(xh, r, gpf, dh2)
        dgpf_ref[...] += dgpf
        dx1_ref[...] = dout + dx

    return _pcall(
        body, name="ffn_fwd_bwd", grid=(SEQ // TM_FFN,),
        in_specs=[_rows(D_MODEL, TM_FFN), _rows(D_MODEL, TM_FFN), _resident((FF, D_MODEL)),
                  _resident((FF, D_MODEL)), _resident((FF, D_MODEL)),
                  _resident((1, D_MODEL)), _resident((1, D_MODEL))],
        out_specs=[_rows(D_MODEL, TM_FFN), _rows(FF, TM_FFN), _rows(FF, TM_FFN), _rows(FF, TM_FFN),
                   _rows(D_MODEL, TM_FFN), _rows(D_MODEL, TM_FFN), _acc(D_MODEL), _acc(D_MODEL), _acc(D_MODEL)],
        out_shape=[_sds((SEQ, D_MODEL), BF16), _sds((SEQ, FF), BF16), _sds((SEQ, FF), BF16),
                   _sds((SEQ, FF), BF16), _sds((SEQ, D_MODEL), BF16), _sds((SEQ, D_MODEL), F32),
                   _sds((1, D_MODEL), F32), _sds((1, D_MODEL), F32), _sds((1, D_MODEL), F32)],
        scratch_shapes=[pltpu.VMEM((TM_FFN, FF), F32), pltpu.VMEM((TM_FFN, FF), F32)],
        args=(x1, target, w_gate, w_up, w_down, g_pre, g_post), comms=comms)


def _wgrad(a, b, a_spec, b_spec, out_block, name, comms=()):
    def body(a_ref, b_ref, o_ref):
        av = a_ref[0] if len(a_ref.shape) == 3 else a_ref[...]
        bv = b_ref[0] if len(b_ref.shape) == 3 else b_ref[...]
        o_ref[0] = _dot_tn(av, bv)

    return _pcall(
        body, name=name, grid=(N_SHARD,),
        in_specs=[a_spec, b_spec],
        out_specs=pl.BlockSpec((1,) + out_block, lambda s: (s, 0, 0)),
        out_shape=_sds((N_SHARD,) + out_block, F32),
        args=(a, b), comms=comms)


def _outproj_bwd(dx1, y, attn, sgu, w_out, g_post, g_attn, g_sgu, u, vs, lg, lb, w_sp, b_t, comms=(), after=()):
    sgu_bwd = _sgu_bwd_body()

    def body(dx1_ref, y_ref, attn_ref, sgu_ref, w_ref, gp_ref, ga_ref, gs_ref, u_ref, vs_ref, lg_ref, lb_ref,
             wsp_ref, bt_ref, dy_ref, du_ref, dvs_ref, dgp_ref, dga_ref, dgs_ref, dw_ref, db_ref, dlg_ref, dlb_ref,
             *rest):
        dattn_refs, delta_refs, (slabs_a, slabs_b, dsgu_ref, dbias_scr) = rest[0:3], rest[3:6], rest[6:]

        @pl.when(pl.program_id(0) == 0)
        def _():
            dgp_ref[...] = jnp.zeros_like(dgp_ref)
            dga_ref[...] = jnp.zeros_like(dga_ref)
            dgs_ref[...] = jnp.zeros_like(dgs_ref)

        yh, ry = _rms_stats(y_ref[...])
        dy, dgp = _rms_bwd(yh, ry, gp_ref[...], dx1_ref[...])
        dgp_ref[...] += dgp
        dyb = dy.astype(BF16)
        dy_ref[...] = dyb
        dmixed = jnp.concatenate([_dot_nt(dyb, w_ref[s]) for s in range(N_SHARD)], axis=1)
        attn = attn_ref[...]
        ah, ra = _rms_stats(attn)
        dattn, dga = _rms_bwd(ah, ra, ga_ref[...], dmixed[:, 0:512])
        dga_ref[...] += dga
        sh, rs = _rms_stats(sgu_ref[...])
        dsgu, dgs = _rms_bwd(sh, rs, gs_ref[...], dmixed[:, 512:1024])
        dgs_ref[...] += dgs
        dsgu_ref[...] = dsgu
        delta = _dot_select(dattn * attn, _head_sum())
        dattn_refs[0][...] = dattn.astype(BF16)
        delta_refs[0][...] = delta
        for i, dil in enumerate(DILATIONS[1:]):
            _store_view(dattn, dattn_refs[i + 1], slabs_a, dil)
            _store_view(delta, delta_refs[i + 1], slabs_b, dil)
        sgu_bwd(u_ref, vs_ref, dsgu_ref, lg_ref, lb_ref, wsp_ref, bt_ref,
                du_ref, dvs_ref, dw_ref, db_ref, dlg_ref, dlb_ref, dbias_scr)

    return _pcall(
        body, name="outproj_sgu_bwd", grid=(SEQ // TM,),
        in_specs=[_rows(D_MODEL), _rows(D_MODEL), _rows(512), _rows(512), _resident((N_SHARD, OUT_S, D_MODEL)),
                  _resident((1, D_MODEL)), _resident((1, 512)), _resident((1, 512)),
                  _rows(SGU_W), _rows(SGU_W), _resident((1, SGU_W)), _resident((1, SGU_W)),
                  _resident((SGU_GROUPS, CHUNK, CHUNK)), _resident((CHUNK, SGU_GROUPS))],
        out_specs=[_rows(D_MODEL), _rows(SGU_W), _rows(SGU_W), _acc(D_MODEL), _acc(512), _acc(512),
                   pl.BlockSpec((SGU_GROUPS, CHUNK, CHUNK), lambda i: (0, 0, 0)), _acc(CHUNK, SGU_GROUPS),
                   _acc(SGU_W), _acc(SGU_W)]
        + [_view_rows(dil) for dil in DILATIONS] + [_view_rows(dil, HEAD_W) for dil in DILATIONS],
        out_shape=[_sds((SEQ, D_MODEL), BF16), _sds((SEQ, SGU_W), BF16), _sds((SEQ, SGU_W), BF16),
                   _sds((1, D_MODEL), F32), _sds((1, 512), F32), _sds((1, 512), F32),
                   _sds((SGU_GROUPS, CHUNK, CHUNK), F32), _sds((SGU_GROUPS, CHUNK), F32),
                   _sds((1, SGU_W), F32), _sds((1, SGU_W), F32)]
        + [_view_shape(dil, BF16) for dil in DILATIONS] + [_view_shape(dil, F32, HEAD_W) for dil in DILATIONS],
        scratch_shapes=[_slab_scratch(), _slab_scratch(), pltpu.VMEM((TM, SGU_W), F32),
                        pltpu.VMEM((CHUNK, SGU_W), F32)],
        args=(dx1, y, attn, sgu, w_out, g_post, g_attn, g_sgu, u, vs, lg, lb, w_sp, b_t), comms=comms, after=after)


def _sgu_bwd_body():
    nsteps = SEQ // TM

    def body(u_ref, vs_ref, ds_ref, lg_ref, lb_ref, w_ref, bt_ref,
             du_ref, dvs_ref, dw_ref, db_ref, dlg_ref, dlb_ref, dbias_scr):
        i = pl.program_id(0)

        @pl.when(i == 0)
        def _():
            dw_ref[...] = jnp.zeros_like(dw_ref)
            dlg_ref[...] = jnp.zeros_like(dlg_ref)
            dlb_ref[...] = jnp.zeros_like(dlb_ref)
            dbias_scr[...] = jnp.zeros_like(dbias_scr)

        wm = _masked_spatial(w_ref)
        ones_g = _group_ones()
        bias_full = _dot_exact(bt_ref[...], ones_g)
        u = u_ref[...]
        vs = vs_ref[...]
        lg = lg_ref[...]
        gu, xh, rstd, vnb, mixed, cdf_u, cdf_vs = _sgu_core(u, vs, lg, lb_ref[...], wm, bias_full)
        dsgu = ds_ref[...]
        du_ref[...] = (dsgu * mixed * _gelu_grad(u, cdf_u)).astype(BF16)
        dmixed = dsgu * gu
        left = _left_half()
        dvn_rows = []
        for c in range(TM // CHUNK):
            rs = slice(c * CHUNK, (c + 1) * CHUNK)
            dm_c = dmixed[rs, :]
            dbias_scr[...] += dm_c
            pieces = []
            for p in range(4):
                ls = slice(p * 128, (p + 1) * 128)
                dmp = dm_c[:, ls]
                vp = vnb[rs, ls]
                dmb = dmp.astype(BF16)
                zero = jnp.zeros_like(dmb)
                dw_ref[2 * p] += _dot_nt(jnp.where(left, dmb, zero), vp)
                dw_ref[2 * p + 1] += _dot_nt(jnp.where(left, zero, dmb), vp)
                pieces.append(jnp.where(left, _dot_tn(wm[2 * p], dmb), _dot_tn(wm[2 * p + 1], dmb)))
            dvn_rows.append(jnp.concatenate(pieces, axis=1))
        dvn = jnp.concatenate(dvn_rows, axis=0)
        dlg_ref[...] += jnp.sum(dvn * xh, axis=0, keepdims=True)
        dlb_ref[...] += jnp.sum(dvn, axis=0, keepdims=True)
        dxh = dvn * lg
        dgv = rstd * (dxh - jnp.mean(dxh, axis=-1, keepdims=True) - xh * jnp.mean(dxh * xh, axis=-1, keepdims=True))
        dvs_ref[...] = (dgv * _gelu_grad(vs, cdf_vs)).astype(BF16)

        @pl.when(i == nsteps - 1)
        def _():
            row = lax.broadcasted_iota(jnp.int32, (CHUNK, CHUNK), 0)
            col = lax.broadcasted_iota(jnp.int32, (CHUNK, CHUNK), 1)
            for g in range(SGU_GROUPS):
                dw_ref[g] = jnp.where(col <= row, dw_ref[g], 0.0)
            db_ref[...] = lax.dot_general(ones_g, dbias_scr[...], (((1,), (1,)), ((), ())),
                                          preferred_element_type=F32, precision=lax.Precision.HIGHEST)

    return body


def _attn_bwd_v1(qv, kv, vv, dov, deltav, lsev, dil, comms=(), after=()):
    seg = SEQ // dil
    nblk = seg // CHUNK
    rps = 4 if nblk == 1 else 1

    def body(q_ref, k_ref, v_ref, do_ref, dl_ref, lse_ref, dq_ref, dk_ref, dv_ref, dk_wait, dv_wait):
        left = _left_half()
        m_cur, m_prev = _block_masks()

        sides = tuple(enumerate((left, ~left)))
        zero = jnp.zeros((CHUNK, CHUNK), BF16)

        def rows(b):
            if isinstance(b, int):
                return b * CHUNK, max(b - 1, 0) * CHUNK
            return pl.multiple_of(b * CHUNK, CHUNK), pl.multiple_of(jnp.maximum(b - 1, 0) * CHUNK, CHUNK)

        def first(rr, b):
            r0, rp = rows(b)
            tiles, firsts = [], []
            for hp in range(4):
                ls = slice(rr * ATTN_W + hp * 128, rr * ATTN_W + (hp + 1) * 128)
                qp = q_ref[pl.ds(r0, CHUNK), ls]
                kc = k_ref[pl.ds(r0, CHUNK), ls]
                vc = v_ref[pl.ds(r0, CHUNK), ls]
                dop = do_ref[pl.ds(r0, CHUNK), ls]
                kp = k_ref[pl.ds(rp, CHUNK), ls] if nblk > 1 else None
                vp = v_ref[pl.ds(rp, CHUNK), ls] if nblk > 1 else None
                tiles.append((ls, kc, kp))
                for side, hm in sides:
                    qh = jnp.where(hm, qp, zero)
                    doh = jnp.where(hm, dop, zero)
                    cur = (_dot_nt(kc, qh), _dot_nt(vc, doh))
                    prev = (_dot_nt(kp, qh), _dot_nt(vp, doh)) if nblk > 1 else None
                    firsts.append((qh, doh, cur, prev))
            return tiles, firsts

        def second(rr, b, firsts):
            r0, _ = rows(b)
            own_ok, prev_ok = m_prev, m_cur & (b > 0)
            lanes = slice(rr * HEAD_W, (rr + 1) * HEAD_W)
            lse_t = lse_ref[pl.ds(r0, CHUNK), lanes].T
            dl_t = dl_ref[pl.ds(r0, CHUNK), lanes].T
            seconds = []
            for i, (qh, doh, cur, prev) in enumerate(firsts):
                lse_h = lse_t[16 * i:16 * i + 1, :]
                dl_h = dl_t[16 * i:16 * i + 1, :]
                pc = jnp.exp(jnp.where(own_ok, cur[0] - lse_h, NEG))
                out = [pc.astype(BF16), (pc * (cur[1] - dl_h)).astype(BF16), None, None]
                if nblk > 1:
                    pp = jnp.exp(jnp.where(prev_ok, prev[0] - lse_h, NEG))
                    out[2:] = [pp.astype(BF16), (pp * (prev[1] - dl_h)).astype(BF16)]
                seconds.append(out)
            return seconds

        def third(rr, b, tiles, firsts, seconds):
            r0, rp = rows(b)
            for hp, (ls, kc, kp) in enumerate(tiles):
                dq = jnp.zeros((CHUNK, CHUNK), F32)
                dkc = jnp.zeros((CHUNK, CHUNK), F32)
                dvc = jnp.zeros((CHUNK, CHUNK), F32)
                dkp = jnp.zeros((CHUNK, CHUNK), F32)
                dvp = jnp.zeros((CHUNK, CHUNK), F32)
                for side, hm in sides:
                    qh, doh, _, _ = firsts[2 * hp + side]
                    pcb, dsc, ppb, dsp = seconds[2 * hp + side]
                    dq = dq + _dot_tn(dsc, jnp.where(hm, kc, zero))
                    dkc = dkc + _dot(dsc, qh)
                    dvc = dvc + _dot(pcb, doh)
                    if nblk > 1:
                        dq = dq + _dot_tn(dsp, jnp.where(hm, kp, zero))
                        dkp = dkp + _dot(dsp, qh)
                        dvp = dvp + _dot(ppb, doh)
                dq_ref[pl.ds(r0, CHUNK), ls] = dq.astype(dq_ref.dtype)
                if nblk == 1:
                    dk_ref[pl.ds(r0, CHUNK), ls] = dkc.astype(dk_ref.dtype)
                    dv_ref[pl.ds(r0, CHUNK), ls] = dvc.astype(dv_ref.dtype)
                else:
                    @pl.when(b > 0)
                    def _():
                        dk_ref[pl.ds(rp, CHUNK), ls] = (dk_wait[:, ls] + dkp).astype(dk_ref.dtype)
                        dv_ref[pl.ds(rp, CHUNK), ls] = (dv_wait[:, ls] + dvp).astype(dv_ref.dtype)

                    dk_wait[:, ls] = dkc
                    dv_wait[:, ls] = dvc

        def run(units):
            data = [first(rr, b) for rr, b in units]
            probs = [second(rr, b, firsts) for (rr, b), (_, firsts) in zip(units, data)]
            for (rr, b), (tiles, firsts), seconds in zip(units, data, probs):
                third(rr, b, tiles, firsts, seconds)

        if nblk == 1:
            run([(rr, 0) for rr in range(rps)])
        else:
            def one(b, carry):
                run([(0, b)])
                return carry

            lax.fori_loop(0, nblk, one, 0)
            last = (nblk - 1) * CHUNK
            dk_ref[last:last + CHUNK, :] = dk_wait[...].astype(dk_ref.dtype)
            dv_ref[last:last + CHUNK, :] = dv_wait[...].astype(dv_ref.dtype)

    spec = pl.BlockSpec((seg, rps * ATTN_W), lambda r: (0, r))
    return _pcall(
        body, name=f"attn_bwd_d{dil}", grid=(dil // rps,),
        in_specs=[spec] * 4 + [pl.BlockSpec((seg, rps * HEAD_W), lambda r: (0, r))] * 2, out_specs=[spec] * 3,
        out_shape=[_sds((seg, dil * ATTN_W), BF16)] * 3,
        scratch_shapes=[pltpu.VMEM((CHUNK, ATTN_W), F32), pltpu.VMEM((CHUNK, ATTN_W), F32)],
        args=(qv, kv, vv, dov, deltav, lsev), comms=comms, after=after)


def _attn_bwd(qv, kv, vv, dov, deltav, lsev, dil, comms=(), after=()):
    seg = SEQ // dil
    nblk = seg // CHUNK
    rps = 4 if nblk == 1 else 1

    def body(q_ref, k_ref, v_ref, do_ref, dl_ref, lse_ref, dq_ref, dk_ref, dv_ref, dk_wait, dv_wait):
        left = _left_half()
        m_cur, m_prev = _block_masks()
        sides = tuple(enumerate((left, ~left)))
        zero = jnp.zeros((CHUNK, CHUNK), BF16)

        def first(rr, b, both):
            r0, rp = _attn_rows(b)
            keys = pl.ds(rp, 2 * CHUNK) if both else pl.ds(r0, CHUNK)
            tiles, heads = [], []
            for hp in range(4):
                ls = slice(rr * ATTN_W + hp * 128, rr * ATTN_W + (hp + 1) * 128)
                qp = q_ref[pl.ds(r0, CHUNK), ls]
                dop = do_ref[pl.ds(r0, CHUNK), ls]
                k2 = k_ref[keys, ls]
                v2 = v_ref[keys, ls]
                tiles.append((ls, k2))
                for _, hm in sides:
                    qh = jnp.where(hm, qp, zero)
                    doh = jnp.where(hm, dop, zero)
                    heads.append((qh, doh, _dot_nt(k2, qh), _dot_nt(v2, doh)))
            return tiles, heads

        def second(rr, b, heads, both):
            r0, _ = _attn_rows(b)
            ok = jnp.concatenate([m_cur, m_prev], axis=0) if both else m_prev
            lanes = slice(rr * HEAD_W, (rr + 1) * HEAD_W)
            lse_t = lse_ref[pl.ds(r0, CHUNK), lanes].T
            dl_t = dl_ref[pl.ds(r0, CHUNK), lanes].T
            out = []
            for i, (_, _, s_t, dp_t) in enumerate(heads):
                p = jnp.exp(jnp.where(ok, s_t - lse_t[16 * i:16 * i + 1, :], NEG))
                out.append((p.astype(BF16), (p * (dp_t - dl_t[16 * i:16 * i + 1, :])).astype(BF16)))
            return out

        def third(rr, b, tiles, heads, probs, both):
            r0, rp = _attn_rows(b)
            nk = 2 * CHUNK if both else CHUNK
            left_k = _lane_left(nk)
            zero_k = jnp.zeros((nk, CHUNK), BF16)
            for hp, (ls, k2) in enumerate(tiles):
                dq = jnp.zeros((CHUNK, CHUNK), F32)
                dk2 = jnp.zeros((nk, CHUNK), F32)
                dv2 = jnp.zeros((nk, CHUNK), F32)
                for side in range(2):
                    qh, doh, _, _ = heads[2 * hp + side]
                    p, ds = probs[2 * hp + side]
                    dq = dq + _dot_tn(ds, jnp.where(left_k if side == 0 else ~left_k, k2, zero_k))
                    dk2 = dk2 + _dot(ds, qh)
                    dv2 = dv2 + _dot(p, doh)
                dq_ref[pl.ds(r0, CHUNK), ls] = dq.astype(dq_ref.dtype)
                if nblk == 1:
                    dk_ref[pl.ds(r0, CHUNK), ls] = dk2.astype(dk_ref.dtype)
                    dv_ref[pl.ds(r0, CHUNK), ls] = dv2.astype(dv_ref.dtype)
                elif both:
                    dk_ref[pl.ds(rp, CHUNK), ls] = (dk_wait[:, ls] + dk2[0:CHUNK]).astype(dk_ref.dtype)
                    dv_ref[pl.ds(rp, CHUNK), ls] = (dv_wait[:, ls] + dv2[0:CHUNK]).astype(dv_ref.dtype)
                    dk_wait[:, ls] = dk2[CHUNK:]
                    dv_wait[:, ls] = dv2[CHUNK:]
                else:
                    dk_wait[:, ls] = dk2
                    dv_wait[:, ls] = dv2

        def run(units, both):
            data = [first(rr, b, both) for rr, b in units]
            probs = [second(rr, b, heads, both) for (rr, b), (_, heads) in zip(units, data)]
            for (rr, b), (tiles, heads), pr in zip(units, data, probs):
                third(rr, b, tiles, heads, pr, both)

        run([(rr, 0) for rr in range(rps)], False)
        if nblk > 1:
            def one(b, carry):
                run([(0, b)], True)
                return carry

            lax.fori_loop(1, nblk, one, 0)
            last = (nblk - 1) * CHUNK
            dk_ref[last:last + CHUNK, :] = dk_wait[...].astype(dk_ref.dtype)
            dv_ref[last:last + CHUNK, :] = dv_wait[...].astype(dv_ref.dtype)

    spec = pl.BlockSpec((seg, rps * ATTN_W), lambda r: (0, r))
    return _pcall(
        body, name=f"attn_bwd_d{dil}", grid=(dil // rps,),
        in_specs=[spec] * 4 + [pl.BlockSpec((seg, rps * HEAD_W), lambda r: (0, r))] * 2, out_specs=[spec] * 3,
        out_shape=[_sds((seg, dil * ATTN_W), BF16)] * 3,
        scratch_shapes=[pltpu.VMEM((CHUNK, ATTN_W), F32), pltpu.VMEM((CHUNK, ATTN_W), F32)],
        args=(qv, kv, vv, dov, deltav, lsev), comms=comms, after=after)


def _inproj_bwd(dqs, dks, dvs, du, dvs_sgu, pos, x, dx1, w_in, g_pre, comms=()):
    def body(dq1, dq2, dq3, dk1, dk2, dk3, dv1, dv2, dv3, du_ref, dvs_ref, pos_ref, x_ref, dx1_ref, w_ref, g_ref,
             dproj_ref, gx_ref, dg_ref, slabs_a, slabs_b):
        @pl.when(pl.program_id(0) == 0)
        def _():
            dg_ref[...] = jnp.zeros_like(dg_ref)

        def total(r1, r2, r3):
            return r1[...] + _load_view(r2, slabs_a, DILATIONS[1]) + _load_view(r3, slabs_b, DILATIONS[2])

        tabs = _rot_tables(pos_ref[...])
        groups = {3: du_ref[...], 4: dvs_ref[...]}
        dh = jnp.zeros((TM, D_MODEL), F32)
        for g in (3, 4, 0, 1, 2):
            if g == 0:
                groups[g] = _rope_bwd(total(dq1, dq2, dq3) * np.float32(ATTN_SCALE), tabs).astype(BF16)
            elif g == 1:
                groups[g] = _rope_bwd(total(dk1, dk2, dk3), tabs).astype(BF16)
            elif g == 2:
                groups[g] = total(dv1, dv2, dv3).astype(BF16)
            dproj_ref[:, 512 * g:512 * (g + 1)] = groups[g]
            off = 0
            for s, a, b in _in_pieces(g):
                dh = dh + _dot_nt(groups[g][:, off:off + b - a], w_ref[s, :, a:b])
                off += b - a
        g = g_ref[...]
        xh, r = _rms_stats(x_ref[...])
        dx, dg = _rms_bwd(xh, r, g, dh)
        dg_ref[...] += dg
        gx_ref[...] = dx1_ref[...] + dx

    return _pcall(
        body, name="inproj_bwd", grid=(SEQ // TM,),
        in_specs=[_view_rows(dil) for dil in DILATIONS] * 3
        + [_rows(512), _rows(512), _rows(1), _rows(D_MODEL), _rows(D_MODEL),
           _resident((N_SHARD, D_MODEL, IN_S)), _resident((1, D_MODEL))],
        out_specs=[_rows(PROJ_W), _rows(D_MODEL), _acc(D_MODEL)],
        out_shape=[_sds((SEQ, PROJ_W), BF16), _sds((SEQ, D_MODEL), F32), _sds((1, D_MODEL), F32)],
        scratch_shapes=[_slab_scratch(), _slab_scratch()],
        args=(*dqs, *dks, *dvs, du, dvs_sgu, pos, x, dx1, w_in, g_pre), comms=comms)


def _to_view(a, dil):
    return a if dil == 1 else a.reshape(SEQ // dil, dil * a.shape[1])


def _from_view(a, dil):
    return a if dil == 1 else a.reshape(SEQ, a.shape[1] // dil)


def _local_step(x, pos, target, w_in, w_out, w_gate, w_up, w_down, small):
    b_t = small["sgu_b_spatial"].T
    h, u, vs, sgu, *qkv = _inproj_fwd(x, pos, small["pre_mix_norm"], w_in, small["sgu_ln_gain"],
                                      small["sgu_ln_bias"], small["sgu_w_spatial"], b_t)
    views = [tuple(qkv[3 * i:3 * i + 3]) for i in range(len(DILATIONS))]
    o_list, l_list = [], []
    for dil, (qv, kv, vv) in zip(DILATIONS, views):
        o, l = _attn_fwd(qv, kv, vv, dil)
        o_list.append(o)
        l_list.append(l)
    attn, mixed, y, x1, *lses = _mix_out_fwd(o_list, l_list, sgu, x, w_out, small["attn_out_norm"],
                                             small["sgu_out_norm"], small["post_mix_norm"])
    h2, a, dg, dup, df, dx1, loss_cols, d_pre_ffn, d_post_ffn = _ffn_fwd_bwd(
        x1, target, w_gate, w_up, w_down, small["pre_ffn_norm"], small["post_ffn_norm"])

    full_tok = pl.BlockSpec((SEQ, D_MODEL), lambda s: (0, 0), pipeline_mode=pl.Buffered(1))
    ff_tok = pl.BlockSpec((1, SEQ, FF_S), lambda s: (s, 0, 0))
    gw_gate, gw_up = _wgrad_ff([dg, dup], h2, "wgrad_gate_up")
    (gw_down,) = _wgrad_ff([a], df, "wgrad_down")

    (dy, du, dvs_sgu, d_post_mix, d_attn_norm, d_sgu_norm, d_w_sp, d_b_sp, d_ln_gain, d_ln_bias,
     *dviews) = _outproj_bwd(
        dx1, y, attn, sgu, w_out, small["post_mix_norm"], small["attn_out_norm"], small["sgu_out_norm"],
        u, vs, small["sgu_ln_gain"], small["sgu_ln_bias"], small["sgu_w_spatial"], b_t)
    gw_out = _wgrad(mixed, dy, pl.BlockSpec((SEQ, OUT_S), lambda s: (0, s)), full_tok, (OUT_S, D_MODEL), "wgrad_out")

    dqs, dks, dvs = [], [], []
    for i, (dil, (qv, kv, vv)) in enumerate(zip(DILATIONS, views)):
        dq, dk, dv = _attn_bwd(qv, kv, vv, dviews[i], dviews[3 + i], lses[i], dil)
        dqs.append(dq)
        dks.append(dk)
        dvs.append(dv)
    dproj, grad_x, d_pre_mix = _inproj_bwd(dqs, dks, dvs, du, dvs_sgu, pos, x, dx1, w_in, small["pre_mix_norm"])
    gw_in = _wgrad(h, dproj, full_tok, pl.BlockSpec((SEQ, IN_S), lambda s: (0, s)), (D_MODEL, IN_S), "wgrad_in")

    small_grads = {
        "pre_mix_norm": d_pre_mix, "sgu_ln_gain": d_ln_gain, "sgu_ln_bias": d_ln_bias, "sgu_w_spatial": d_w_sp,
        "sgu_b_spatial": d_b_sp, "attn_out_norm": d_attn_norm, "sgu_out_norm": d_sgu_norm,
        "post_mix_norm": d_post_mix, "pre_ffn_norm": d_pre_ffn, "post_ffn_norm": d_post_ffn,
    }
    return loss_cols, grad_x, (gw_in, gw_out, gw_gate, gw_up, gw_down), small_grads


def _coords():
    return lax.axis_index("x"), lax.axis_index("y"), lax.axis_index("c")


def _other_chips(x, y):
    return [(1 - x, y), (x, 1 - y), (1 - x, 1 - y)]


def _comm_call(body, name, n_in, out_shape, scratch_shapes):
    return pl.pallas_call(
        body, name=name, in_specs=[ANY] * n_in, out_specs=[ANY] * len(out_shape), out_shape=out_shape,
        scratch_shapes=scratch_shapes,
        compiler_params=pltpu.CompilerParams(has_side_effects=True),
    )


def _gather_weights(shards):
    n = len(shards)
    halves = [s.reshape(2, s.shape[0] // 2, s.shape[1]) for s in shards]

    def body(*refs):
        ins, outs = refs[:n], refs[n:2 * n]
        send_sems, recv_sems = refs[2 * n:]
        x, y, c = _coords()
        s_me = 2 * x + y
        chips = _other_chips(x, y)
        sibling = (x, y, 1 - c)

        def copy(k, w, shard, cc, to):
            src = ins[w].at[cc] if shard is None else outs[w].at[shard, cc]
            dst = outs[w].at[s_me if shard is None else shard, cc]
            return pltpu.make_async_remote_copy(src_ref=src, dst_ref=dst, send_sem=send_sems.at[k],
                                                recv_sem=recv_sems.at[k], device_id=to, device_id_type=MESH)

        first = [copy(j * n + w, w, None, c, (cx, cy, c)) for j, (cx, cy) in enumerate(chips) for w in range(n)]
        for cp in first:
            cp.start()
        passed = []
        for j, (cx, cy) in enumerate(chips):
            for w in range(n):
                copy(j * n + w, w, 2 * cx + cy, c, (x, y, c)).wait_recv()
                fw = copy((3 + j) * n + w, w, 2 * cx + cy, c, sibling)
                fw.start()
                passed.append(fw)
        for j, (cx, cy) in enumerate(chips):
            for w in range(n):
                copy((3 + j) * n + w, w, 2 * cx + cy, 1 - c, (x, y, c)).wait_recv()
        for cp in first + passed:
            cp.wait_send()

    out_shape = [_sds((N_SHARD,) + h.shape, h.dtype) for h in halves]
    scratch = [pltpu.SemaphoreType.DMA((6 * n,)), pltpu.SemaphoreType.DMA((6 * n,))]
    full = _comm_call(body, "comm_gather_weights", n, out_shape, scratch)(*halves)
    s_me = 2 * lax.axis_index("x") + lax.axis_index("y")
    full = [lax.dynamic_update_slice(f, h[None], (s_me, 0, 0, 0)) for f, h in zip(full, halves)]
    return [f.reshape((N_SHARD,) + s.shape) for f, s in zip(full, shards)]


def _rs_to_sibling(gws):
    n = len(gws)

    def body(*refs):
        ins, outs = refs[:n], refs[n:2 * n]
        send_sems, recv_sems = refs[2 * n:]
        x, y, c = _coords()
        copies = []
        for w in range(n):
            hw = gws[w].shape[1] // 2
            copies.append(pltpu.make_async_remote_copy(
                src_ref=ins[w].at[:, pl.ds((1 - c) * hw, hw), :], dst_ref=outs[w], send_sem=send_sems.at[w],
                recv_sem=recv_sems.at[w], device_id=(x, y, 1 - c), device_id_type=MESH))
        for cp in copies:
            cp.start()
        for cp in copies:
            cp.wait()

    out_shape = [_sds((N_SHARD, g.shape[1] // 2, g.shape[2]), g.dtype) for g in gws]
    scratch = [pltpu.SemaphoreType.DMA((n,)), pltpu.SemaphoreType.DMA((n,))]
    return _comm_call(body, "comm_rs_sibling", n, out_shape, scratch)(*gws)


def _rs_chip_sum(gw, recv, core):
    _, rows, cols = gw.shape
    hw = rows // 2

    def body(c_ref, g_ref, r_ref, o_ref):
        o_ref[...] = (g_ref[...] + r_ref[...]).astype(BF16)

    return pl.pallas_call(
        body, name="rs_chip_sum",
        grid_spec=pltpu.PrefetchScalarGridSpec(
            num_scalar_prefetch=1, grid=(N_SHARD,),
            in_specs=[pl.BlockSpec((1, hw, cols), lambda s, c_ref: (s, c_ref[0], 0)),
                      pl.BlockSpec((1, hw, cols), lambda s, c_ref: (s, 0, 0))],
            out_specs=pl.BlockSpec((1, hw, cols), lambda s, c_ref: (s, 0, 0))),
        out_shape=_sds((N_SHARD, hw, cols), BF16),
        compiler_params=_seq_params(),
    )(core, gw, recv)


def _rs_between_chips(pbs):
    n = len(pbs)

    def body(*refs):
        ins, outs = refs[:n], refs[n:2 * n]
        send_sems, recv_sems = refs[2 * n:]
        x, y, c = _coords()
        copies = []
        for j, (cx, cy) in enumerate(_other_chips(x, y)):
            for w in range(n):
                copies.append(pltpu.make_async_remote_copy(
                    src_ref=ins[w].at[2 * cx + cy], dst_ref=outs[w].at[j], send_sem=send_sems.at[j * n + w],
                    recv_sem=recv_sems.at[j * n + w], device_id=(cx, cy, c), device_id_type=MESH))
        for cp in copies:
            cp.start()
        for cp in copies:
            cp.wait()

    out_shape = [_sds((3,) + p.shape[1:], p.dtype) for p in pbs]
    scratch = [pltpu.SemaphoreType.DMA((3 * n,)), pltpu.SemaphoreType.DMA((3 * n,))]
    return _comm_call(body, "comm_rs_chips", n, out_shape, scratch)(*pbs)


def _rs_final_sum(gw, recv_sib, recv_chips, shard_core):
    _, rows, cols = gw.shape
    hw = rows // 2

    def body(sc_ref, g_ref, r_ref, rc_ref, o_ref):
        acc = g_ref[0] + r_ref[0]
        for j in range(3):
            acc = acc + rc_ref[j].astype(F32)
        o_ref[0] = acc

    return pl.pallas_call(
        body, name="rs_final_sum",
        grid_spec=pltpu.PrefetchScalarGridSpec(
            num_scalar_prefetch=1, grid=(1,),
            in_specs=[pl.BlockSpec((1, hw, cols), lambda i, sc: (sc[0], sc[1], 0)),
                      pl.BlockSpec((1, hw, cols), lambda i, sc: (sc[0], 0, 0)),
                      pl.BlockSpec((3, hw, cols), lambda i, sc: (0, 0, 0))],
            out_specs=pl.BlockSpec((1, hw, cols), lambda i, sc: (sc[1], 0, 0))),
        out_shape=_sds((2, hw, cols), F32),
        compiler_params=_seq_params(),
    )(shard_core, gw, recv_sib, recv_chips)


def _rs_join_halves(halves):
    n = len(halves)

    def body(*refs):
        bufs = refs[n:2 * n]
        send_sems, recv_sems = refs[2 * n:]
        x, y, c = _coords()
        remote = [pltpu.make_async_remote_copy(
            src_ref=bufs[w].at[c], dst_ref=bufs[w].at[c], send_sem=send_sems.at[w], recv_sem=recv_sems.at[w],
            device_id=(x, y, 1 - c), device_id_type=MESH) for w in range(n)]
        for cp in remote:
            cp.start()
        for w in range(n):
            remote[w].wait_send()
            pltpu.make_async_remote_copy(
                src_ref=bufs[w].at[c], dst_ref=bufs[w].at[1 - c], send_sem=send_sems.at[w],
                recv_sem=recv_sems.at[w], device_id=(x, y, c), device_id_type=MESH).wait_recv()

    joined = pl.pallas_call(
        body, name="comm_rs_join", in_specs=[ANY] * n, out_specs=[ANY] * n,
        out_shape=[_sds(h.shape, h.dtype) for h in halves], input_output_aliases={w: w for w in range(n)},
        scratch_shapes=[pltpu.SemaphoreType.DMA((n,)), pltpu.SemaphoreType.DMA((n,))],
        compiler_params=pltpu.CompilerParams(has_side_effects=True),
    )(*halves)
    return [j.reshape(2 * h.shape[1], h.shape[2]) for j, h in zip(joined, halves)]


def _allreduce_small(buf):
    rows, cols = buf.shape

    def body(in_ref, out_ref, slots, send_sems, recv_sems):
        x, y, c = _coords()
        me = 4 * x + 2 * y + c
        copies, peers = [], []
        for k in range(1, 8):
            px = 1 - x if (k >> 2) & 1 else x
            py = 1 - y if (k >> 1) & 1 else y
            pc = 1 - c if k & 1 else c
            peers.append(4 * px + 2 * py + pc)
            copies.append(pltpu.make_async_remote_copy(
                src_ref=in_ref, dst_ref=slots.at[me], send_sem=send_sems.at[k - 1], recv_sem=recv_sems.at[k - 1],
                device_id=(px, py, pc), device_id_type=MESH))
        for cp in copies:
            cp.start()
        slots[me] = in_ref[...]
        for k in range(7):
            pltpu.make_async_remote_copy(
                src_ref=in_ref, dst_ref=slots.at[peers[k]], send_sem=send_sems.at[k], recv_sem=recv_sems.at[k],
                device_id=(x, y, c), device_id_type=MESH).wait_recv()
        for cp in copies:
            cp.wait_send()
        acc = slots[0]
        for i in range(1, 8):
            acc = acc + slots[i]
        out_ref[...] = acc

    vmem = pl.BlockSpec(memory_space=pltpu.VMEM)
    return pl.pallas_call(
        body, name="comm_allreduce_small", in_specs=[vmem], out_specs=vmem, out_shape=_sds((rows, cols), F32),
        scratch_shapes=[pltpu.VMEM((8, rows, cols), F32), pltpu.SemaphoreType.DMA((7,)), pltpu.SemaphoreType.DMA((7,))],
        compiler_params=pltpu.CompilerParams(has_side_effects=True, vmem_limit_bytes=VMEM_LIMIT),
    )(buf)


def _adamw(w, g, m, v, block_rows, name, after=()):
    rows, cols = w.shape

    def body(w_ref, g_ref, m_ref, v_ref, *rest):
        d_ref, nm_ref, nv_ref = rest[len(after):]
        g = g_ref[...]
        m = ADAM_B1 * m_ref[...] + (1.0 - ADAM_B1) * g
        v = ADAM_B2 * v_ref[...] + (1.0 - ADAM_B2) * (g * g)
        m_hat = m / (1.0 - ADAM_B1 ** ADAM_STEP)
        v_hat = v / (1.0 - ADAM_B2 ** ADAM_STEP)
        d_ref[...] = -ADAM_LR * (m_hat / (jnp.sqrt(v_hat) + ADAM_EPS) + ADAM_WD * w_ref[...])
        nm_ref[...] = m
        nv_ref[...] = v

    spec = pl.BlockSpec((block_rows, cols), lambda i: (i, 0))
    return pl.pallas_call(
        body, name=name, grid=(rows // block_rows,), in_specs=[spec] * 4 + [ANY] * len(after), out_specs=[spec] * 3,
        out_shape=[_sds((rows, cols), F32)] * 3,
        compiler_params=_seq_params(),
    )(w, g, m, v, *after)


WEIGHTS = ("pre_mix_norm", "w_in", "sgu_ln_gain", "sgu_ln_bias", "sgu_w_spatial", "sgu_b_spatial", "attn_out_norm",
           "sgu_out_norm", "w_out", "post_mix_norm", "pre_ffn_norm", "w_gate", "w_up", "w_down", "post_ffn_norm")
BIG = ("w_in", "w_out", "w_gate", "w_up", "w_down")
BIG_ADAM_ROWS = {"w_in": 256, "w_out": 128, "w_gate": 352, "w_up": 352, "w_down": 352}
SMALL = ("pre_mix_norm", "post_mix_norm", "pre_ffn_norm", "post_ffn_norm", "sgu_ln_gain", "sgu_ln_bias",
         "attn_out_norm", "sgu_out_norm", "sgu_w_spatial", "sgu_b_spatial")


def _pack_small(d):
    flat = [d[n].reshape(-1) for n in SMALL]
    used = sum(f.shape[0] for f in flat)
    flat.append(jnp.zeros((SMALL_ROWS * 1024 - used,), F32))
    return jnp.concatenate(flat).reshape(SMALL_ROWS, 1024)


def _unpack_small(buf, shapes):
    flat = buf.reshape(-1)
    out, off = {}, 0
    for n in SMALL:
        size = int(np.prod(shapes[n]))
        out[n] = flat[off:off + size].reshape(shapes[n])
        off += size
    return out


def _kernel_unoverlapped(x, positions, pre_mix_norm, w_in, sgu_ln_gain, sgu_ln_bias, sgu_w_spatial, sgu_b_spatial, attn_out_norm, sgu_out_norm, w_out, post_mix_norm, pre_ffn_norm, w_gate, w_up, w_down, post_ffn_norm, loss_target, m_pre_mix_norm, m_w_in, m_sgu_ln_gain, m_sgu_ln_bias, m_sgu_w_spatial, m_sgu_b_spatial, m_attn_out_norm, m_sgu_out_norm, m_w_out, m_post_mix_norm, m_pre_ffn_norm, m_w_gate, m_w_up, m_w_down, m_post_ffn_norm, v_pre_mix_norm, v_w_in, v_sgu_ln_gain, v_sgu_ln_bias, v_sgu_w_spatial, v_sgu_b_spatial, v_attn_out_norm, v_sgu_out_norm, v_w_out, v_post_mix_norm, v_pre_ffn_norm, v_w_gate, v_w_up, v_w_down, v_post_ffn_norm):
    a = dict(locals())
    cx, cy, cc = _coords()
    core = jnp.stack([cc]).astype(jnp.int32)
    shard_core = jnp.stack([2 * cx + cy, cc]).astype(jnp.int32)

    full = _gather_weights([a[n][0].astype(BF16) for n in BIG])
    small = {n: (a[n][0] if a[n].ndim > 2 else a[n]) for n in SMALL}
    loss_cols, grad_x, gws, small_grads = _local_step(
        x[0], positions.reshape(SEQ, 1), loss_target[0], *full, small)
    loss = lax.psum(jnp.sum(loss_cols) * np.float32(0.5 / D_MODEL), ("x", "y", "c"))

    recv_sib = _rs_to_sibling(list(gws))
    chip_part = [_rs_chip_sum(g, r, core) for g, r in zip(gws, recv_sib)]
    recv_chips = _rs_between_chips(chip_part)
    halves = [_rs_final_sum(g, r, rc, shard_core) for g, r, rc in zip(gws, recv_sib, recv_chips)]
    big_grads = dict(zip(BIG, _rs_join_halves(halves)))

    shapes = {n: a[n].shape for n in SMALL}
    small_sum = _allreduce_small(_pack_small(small_grads))

    grads, deltas, new_m, new_v = {}, {}, {}, {}
    for n in BIG:
        grads[n] = big_grads[n][None]
        d, nm, nv = _adamw(a[n][0], big_grads[n], a["m_" + n][0], a["v_" + n][0], BIG_ADAM_ROWS[n], "adamw_" + n)
        deltas[n], new_m[n], new_v[n] = d[None], nm[None], nv[None]
    d, nm, nv = _adamw(_pack_small({n: a[n] for n in SMALL}), small_sum, _pack_small({n: a["m_" + n] for n in SMALL}),
                       _pack_small({n: a["v_" + n] for n in SMALL}), SMALL_ROWS, "adamw_small")
    grads.update(_unpack_small(small_sum, shapes))
    deltas.update(_unpack_small(d, shapes))
    new_m.update(_unpack_small(nm, shapes))
    new_v.update(_unpack_small(nv, shapes))
    return (loss, grad_x[None], *[grads[n] for n in WEIGHTS], *[deltas[n] for n in WEIGHTS],
            *[new_m[n] for n in WEIGHTS], *[new_v[n] for n in WEIGHTS])


def _remote(src, dst, send_sem, recv_sem, to):
    return pltpu.make_async_remote_copy(src_ref=src, dst_ref=dst, send_sem=send_sem, recv_sem=recv_sem,
                                        device_id=to, device_id_type=MESH)


def _halves(a):
    *lead, rows, cols = a.shape
    return a.reshape(*lead, 2, rows // 2, cols)


def _gather_ici(shards):
    n = len(shards)

    def desc(ins, outs, ss, rs, j, w, landed):
        x, y, c = _coords()
        cx, cy = _other_chips(x, y)[j]
        shard = 2 * cx + cy if landed else 2 * x + y
        return _remote(ins[w].at[c], outs[w].at[shard, c], ss.at[j * n + w], rs.at[j * n + w], (cx, cy, c))

    def start(ins, outs, ss, rs):
        for j in range(3):
            for w in range(n):
                desc(ins, outs, ss, rs, j, w, False).start()

    def finish(ins, outs, ss, rs):
        for j in range(3):
            for w in range(n):
                desc(ins, outs, ss, rs, j, w, True).wait_recv()
                desc(ins, outs, ss, rs, j, w, False).wait_send()

    return _Comm(shards, [_sds((N_SHARD,) + s.shape, s.dtype) for s in shards], 3 * n, start, finish)


def _gather_pass(fulls):
    n = len(fulls)

    def desc(bufs, ss, rs, j, w, landed):
        x, y, c = _coords()
        cx, cy = _other_chips(x, y)[j]
        shard = 2 * cx + cy
        return _remote(bufs[w].at[shard, c], bufs[w].at[shard, 1 - c if landed else c],
                       ss.at[j * n + w], rs.at[j * n + w], (x, y, 1 - c))

    def start(ins, outs, ss, rs):
        for j in range(3):
            for w in range(n):
                desc(outs, ss, rs, j, w, False).start()

    def finish(ins, outs, ss, rs):
        for j in range(3):
            for w in range(n):
                desc(outs, ss, rs, j, w, True).wait_recv()
                desc(outs, ss, rs, j, w, False).wait_send()

    return _Comm(fulls, [_sds(f.shape, f.dtype) for f in fulls], 3 * n, start, finish, aliased=True)


def _rs_sibling(gws):
    n = len(gws)

    def desc(ins, outs, ss, rs, w):
        x, y, c = _coords()
        return _remote(ins[w].at[:, 1 - c], outs[w], ss.at[w], rs.at[w], (x, y, 1 - c))

    def start(ins, outs, ss, rs):
        for w in range(n):
            desc(ins, outs, ss, rs, w).start()

    def finish(ins, outs, ss, rs):
        for w in range(n):
            desc(ins, outs, ss, rs, w).wait()

    out_shape = [_sds((N_SHARD, g.shape[1] // 2, g.shape[2]), g.dtype) for g in gws]
    return _Comm([_halves(g) for g in gws], out_shape, n, start, finish)


def _rs_chips(pbs):
    n = len(pbs)

    def desc(ins, outs, ss, rs, j, w):
        x, y, c = _coords()
        cx, cy = _other_chips(x, y)[j]
        return _remote(ins[w].at[2 * cx + cy], outs[w].at[j], ss.at[j * n + w], rs.at[j * n + w], (cx, cy, c))

    def start(ins, outs, ss, rs):
        for j in range(3):
            for w in range(n):
                desc(ins, outs, ss, rs, j, w).start()

    def finish(ins, outs, ss, rs):
        for j in range(3):
            for w in range(n):
                desc(ins, outs, ss, rs, j, w).wait()

    return _Comm(pbs, [_sds((3,) + p.shape[1:], p.dtype) for p in pbs], 3 * n, start, finish)


def _rs_join(halves):
    n = len(halves)

    def desc(bufs, ss, rs, w, landed):
        x, y, c = _coords()
        return _remote(bufs[w].at[c], bufs[w].at[1 - c if landed else c], ss.at[w], rs.at[w], (x, y, 1 - c))

    def start(ins, outs, ss, rs):
        for w in range(n):
            desc(outs, ss, rs, w, False).start()

    def finish(ins, outs, ss, rs):
        for w in range(n):
            desc(outs, ss, rs, w, True).wait_recv()
            desc(outs, ss, rs, w, False).wait_send()

    return _Comm(halves, [_sds(h.shape, h.dtype) for h in halves], n, start, finish, aliased=True)


def _small_exchange(buf):
    def desc(ins, outs, ss, rs, k, landed):
        x, y, c = _coords()
        px = 1 - x if (k >> 2) & 1 else x
        py = 1 - y if (k >> 1) & 1 else y
        pc = 1 - c if k & 1 else c
        slot = 4 * px + 2 * py + pc if landed else 4 * x + 2 * y + c
        return _remote(ins[0], outs[0].at[slot], ss.at[k - 1], rs.at[k - 1], (px, py, pc))

    def start(ins, outs, ss, rs):
        for k in range(1, 8):
            desc(ins, outs, ss, rs, k, False).start()

    def finish(ins, outs, ss, rs):
        for k in range(1, 8):
            desc(ins, outs, ss, rs, k, True).wait_recv()
            desc(ins, outs, ss, rs, k, False).wait_send()

    return _Comm([buf], [_sds((8,) + buf.shape, buf.dtype)], 7, start, finish)


HBM = pl.BlockSpec(memory_space=pltpu.HBM)
SEM = pl.BlockSpec(memory_space=pltpu.SEMAPHORE)
DATAFLOW = pltpu.SideEffectType.DATAFLOW_SIDE_EFFECTING


def _split_start(name, comm, lands, after):
    srcs = [pltpu.with_memory_space_constraint(s, pltpu.HBM) for s in comm.args]
    lands = [pltpu.with_memory_space_constraint(b, pltpu.HBM) for b in lands]
    ns, nb = len(srcs), len(lands)

    def body(*refs):
        send_sems, recv_sems = refs[ns + nb + 1], refs[ns + nb + 2]
        comm.start(refs[:ns], refs[ns:ns + nb], send_sems, recv_sems)
        refs[-1][...] = jnp.zeros_like(refs[-1])

    res = pl.pallas_call(
        body, name=name,
        out_shape=(pltpu.SemaphoreType.DMA((comm.n_sems,)), pltpu.SemaphoreType.DMA((comm.n_sems,)),
                   *[pltpu.HBM(b.shape, b.dtype) for b in srcs + lands], _sds((8, 128), F32)),
        in_specs=[HBM] * (ns + nb) + [ANY],
        out_specs=(SEM, SEM, *[HBM] * (ns + nb), pl.BlockSpec(memory_space=pltpu.VMEM)),
        input_output_aliases={i: 2 + i for i in range(ns + nb)},
        compiler_params=pltpu.CompilerParams(has_side_effects=DATAFLOW),
    )(*srcs, *lands, after)
    return res[0], res[1], list(res[2:2 + ns]), list(res[2 + ns:2 + ns + nb]), res[-1]


def _split_starts(name, comms, after):
    srcs = [[pltpu.with_memory_space_constraint(s, pltpu.HBM) for s in c.args] for c in comms]
    lands = [[pltpu.with_memory_space_constraint(lax.empty(o.shape, o.dtype), pltpu.HBM) for o in c.out_shape]
             for c in comms]
    bufs = [b for k in range(len(comms)) for b in srcs[k] + lands[k]]
    nb, nc = len(bufs), len(comms)

    def body(*refs):
        sems = refs[nb + 1:nb + 1 + 2 * nc]
        off = 0
        for k, c in enumerate(comms):
            ns, nl = len(srcs[k]), len(lands[k])
            c.start(refs[off:off + ns], refs[off + ns:off + ns + nl], sems[2 * k], sems[2 * k + 1])
            off += ns + nl
        refs[-1][...] = jnp.zeros_like(refs[-1])

    res = pl.pallas_call(
        body, name=name,
        out_shape=(*[pltpu.SemaphoreType.DMA((c.n_sems,)) for c in comms for _ in range(2)],
                   *[pltpu.HBM(b.shape, b.dtype) for b in bufs], _sds((8, 128), F32)),
        in_specs=[HBM] * nb + [ANY],
        out_specs=(*[SEM] * (2 * nc), *[HBM] * nb, pl.BlockSpec(memory_space=pltpu.VMEM)),
        input_output_aliases={i: 2 * nc + i for i in range(nb)},
        compiler_params=pltpu.CompilerParams(has_side_effects=DATAFLOW),
    )(*bufs, after)
    states, off = [], 2 * nc
    for k in range(nc):
        ns, nl = len(srcs[k]), len(lands[k])
        states.append((res[2 * k], res[2 * k + 1], list(res[off:off + ns]), list(res[off + ns:off + ns + nl])))
        off += ns + nl
    return states, res[-1]


def _split_wait(name, comm, send_sems, recv_sems, srcs, lands, after):
    ns, nb = len(srcs), len(lands)
    after = list(after) if isinstance(after, (list, tuple)) else [after]

    def body(*refs):
        comm.finish(refs[:ns], refs[ns:ns + nb], refs[ns + nb], refs[ns + nb + 1])

    res = pl.pallas_call(
        body, name=name,
        out_shape=tuple(pltpu.HBM(b.shape, b.dtype) for b in srcs + lands),
        in_specs=[HBM] * (ns + nb) + [SEM, SEM] + [ANY] * len(after), out_specs=tuple([HBM] * (ns + nb)),
        input_output_aliases={i: i for i in range(ns + nb)},
        compiler_params=pltpu.CompilerParams(has_side_effects=DATAFLOW),
    )(*srcs, *lands, send_sems, recv_sems, *after)
    return list(res[ns:])


LOSS_ROW = "loss_cols"
SMALL_EARLY = ("post_mix_norm", "pre_ffn_norm", "post_ffn_norm", "sgu_ln_gain", "sgu_ln_bias", "attn_out_norm",
               "sgu_out_norm", "sgu_b_spatial", LOSS_ROW)
SMALL_LATE = ("pre_mix_norm",)
SMALL_WSP = ("sgu_w_spatial",)


def _pack(d, names, rows, dtype=F32):
    flat = [d[n].reshape(-1) for n in names]
    used = sum(f.shape[0] for f in flat)
    flat.append(jnp.zeros((rows * 1024 - used,), F32))
    return jnp.concatenate(flat).reshape(rows, 1024).astype(dtype)


def _unpack(buf, names, shapes):
    flat = buf.reshape(-1)
    out, off = {}, 0
    for n in names:
        size = int(np.prod(shapes[n]))
        out[n] = flat[off:off + size].reshape(shapes[n])
        off += size
    return out


def _merge(comms):
    def run(phase):
        def go(ins, outs, ss, rs):
            ii = oi = si = 0
            for c in comms:
                getattr(c, phase)(ins[ii:ii + len(c.args)], outs[oi:oi + len(c.out_shape)],
                                  ss.at[pl.ds(si, c.n_sems)], rs.at[pl.ds(si, c.n_sems)])
                ii += len(c.args)
                oi += len(c.out_shape)
                si += c.n_sems
        return go

    return _Comm([a for c in comms for a in c.args], [o for c in comms for o in c.out_shape],
                 sum(c.n_sems for c in comms), run("start"), run("finish"))


def _chip_sums(gws, recvs, shard_core):
    n = len(gws)
    _, rows, cols = gws[0].shape
    hw = rows // 2

    def body(sc_ref, *refs):
        for k in range(n):
            refs[2 * n + k][...] = (refs[k][...] + refs[n + k][...]).astype(BF16)

    def other(s, sc):
        return jnp.where(s >= sc[0], s + 1, s)

    mine = pl.BlockSpec((1, hw, cols), lambda s, sc: (other(s, sc), sc[1], 0))
    plain = pl.BlockSpec((1, hw, cols), lambda s, sc: (other(s, sc), 0, 0))
    return pl.pallas_call(
        body, name="rs_chip_sums",
        grid_spec=pltpu.PrefetchScalarGridSpec(num_scalar_prefetch=1, grid=(N_SHARD - 1,),
                                               in_specs=[mine] * n + [plain] * n, out_specs=[plain] * n),
        out_shape=[_sds((N_SHARD, hw, cols), BF16)] * n,
        compiler_params=_seq_params(),
    )(shard_core, *gws, *recvs)


def _final_sums(gws, recv_sibs, recv_chips, shard_core):
    n = len(gws)
    _, rows, cols = gws[0].shape
    hw = rows // 2

    def body(sc_ref, *refs):
        for k in range(n):
            acc = refs[k][0] + refs[n + k][0]
            for j in range(3):
                acc = acc + refs[2 * n + k][j].astype(F32)
            refs[3 * n + k][0] = acc

    return pl.pallas_call(
        body, name="rs_final_sums",
        grid_spec=pltpu.PrefetchScalarGridSpec(
            num_scalar_prefetch=1, grid=(1,),
            in_specs=[pl.BlockSpec((1, hw, cols), lambda i, sc: (sc[0], sc[1], 0))] * n
            + [pl.BlockSpec((1, hw, cols), lambda i, sc: (sc[0], 0, 0))] * n
            + [pl.BlockSpec((3, hw, cols), lambda i, sc: (0, 0, 0))] * n,
            out_specs=[pl.BlockSpec((1, hw, cols), lambda i, sc: (sc[1], 0, 0))] * n),
        out_shape=[_sds((2, hw, cols), F32)] * n,
        compiler_params=_seq_params(),
    )(shard_core, *gws, *recv_sibs, *recv_chips)


def _adamw_multi(ws, gs, ms, vs, block_rows, name, after=()):
    n = len(ws)
    rows, cols = ws[0].shape

    def body(*refs):
        outs = refs[4 * n + len(after):]
        for k in range(n):
            g = refs[n + k][...]
            d, m, v = _adam_math(refs[k][...], g, refs[2 * n + k][...], refs[3 * n + k][...])
            outs[4 * k][...], outs[4 * k + 1][...], outs[4 * k + 2][...], outs[4 * k + 3][...] = g, d, m, v

    spec = pl.BlockSpec((block_rows, cols), lambda i: (i, 0))
    res = pl.pallas_call(
        body, name=name, grid=(rows // block_rows,), in_specs=[spec] * (4 * n) + [ANY] * len(after),
        out_specs=[spec] * (4 * n), out_shape=[_sds((rows, cols), F32)] * (4 * n),
        compiler_params=_seq_params(),
    )(*ws, *gs, *ms, *vs, *after)
    return [tuple(res[4 * k:4 * k + 4]) for k in range(n)]


def _wgrad_ff(a_list, b, name, comms=()):
    n = len(a_list)
    cols = 256

    def body(*refs):
        bv = refs[n][...]
        for k in range(n):
            refs[n + 1 + k][...] = _dot_tn(refs[k][...], bv)

    res = _pcall(
        body, name=name, grid=(FF // cols,),
        in_specs=[pl.BlockSpec((SEQ, cols), lambda j: (0, j))] * n
        + [pl.BlockSpec((SEQ, D_MODEL), lambda j: (0, 0), pipeline_mode=pl.Buffered(1))],
        out_specs=[pl.BlockSpec((cols, D_MODEL), lambda j: (j, 0))] * n,
        out_shape=[_sds((FF, D_MODEL), F32)] * n, args=(*a_list, b), comms=comms)
    mine, theirs = res if comms else (res, None)
    mine = [m.reshape(N_SHARD, FF_S, D_MODEL) for m in mine]
    return (mine, theirs) if comms else mine


def _wgrad_pair(a1, a2, b, a_spec, b_spec, out_block, name, comms=()):
    def body(a1_ref, a2_ref, b_ref, o1_ref, o2_ref):
        bv = b_ref[...]
        o1_ref[0] = _dot_tn(a1_ref[0], bv)
        o2_ref[0] = _dot_tn(a2_ref[0], bv)

    out_spec = pl.BlockSpec((1,) + out_block, lambda s: (s, 0, 0))
    return _pcall(
        body, name=name, grid=(N_SHARD,), in_specs=[a_spec, a_spec, b_spec], out_specs=[out_spec, out_spec],
        out_shape=[_sds((N_SHARD,) + out_block, F32)] * 2, args=(a1, a2, b), comms=comms)


def _comm_only(name, comms):
    return _pcall(lambda: None, name=name, grid=(1,), in_specs=[], out_specs=[], out_shape=[], args=(),
                  comms=comms)[1]


def _adam_math(w, g, m, v):
    m = ADAM_B1 * m + (1.0 - ADAM_B1) * g
    v = ADAM_B2 * v + (1.0 - ADAM_B2) * (g * g)
    m_hat = m / (1.0 - ADAM_B1 ** ADAM_STEP)
    v_hat = v / (1.0 - ADAM_B2 ** ADAM_STEP)
    return -ADAM_LR * (m_hat / (jnp.sqrt(v_hat) + ADAM_EPS) + ADAM_WD * w), m, v


def _adamw_small(own, slots, w, m, v, me):
    rows, cols = own.shape

    def body(me_ref, own_ref, slots_ref, w_ref, m_ref, v_ref, g_ref, d_ref, nm_ref, nv_ref):
        own_v = own_ref[...].astype(F32)
        g = jnp.where(me_ref[0] == 0, own_v, slots_ref[0].astype(F32))
        for i in range(1, 8):
            g = g + jnp.where(me_ref[0] == i, own_v, slots_ref[i].astype(F32))
        g_ref[...] = g
        d_ref[...], nm_ref[...], nv_ref[...] = _adam_math(w_ref[...], g, m_ref[...], v_ref[...])

    flat = pl.BlockSpec((rows, cols), lambda i, me_ref: (0, 0))
    return pl.pallas_call(
        body, name="adamw_small",
        grid_spec=pltpu.PrefetchScalarGridSpec(
            num_scalar_prefetch=1, grid=(1,),
            in_specs=[flat, pl.BlockSpec((8, rows, cols), lambda i, me_ref: (0, 0, 0)), flat, flat, flat],
            out_specs=[flat] * 4),
        out_shape=[_sds((rows, cols), F32)] * 4,
        compiler_params=_seq_params(),
    )(me, own, slots, w, m, v)


def kernel(x, positions, pre_mix_norm, w_in, sgu_ln_gain, sgu_ln_bias, sgu_w_spatial, sgu_b_spatial, attn_out_norm, sgu_out_norm, w_out, post_mix_norm, pre_ffn_norm, w_gate, w_up, w_down, post_ffn_norm, loss_target, m_pre_mix_norm, m_w_in, m_sgu_ln_gain, m_sgu_ln_bias, m_sgu_w_spatial, m_sgu_b_spatial, m_attn_out_norm, m_sgu_out_norm, m_w_out, m_post_mix_norm, m_pre_ffn_norm, m_w_gate, m_w_up, m_w_down, m_post_ffn_norm, v_pre_mix_norm, v_w_in, v_sgu_ln_gain, v_sgu_ln_bias, v_sgu_w_spatial, v_sgu_b_spatial, v_attn_out_norm, v_sgu_out_norm, v_w_out, v_post_mix_norm, v_pre_ffn_norm, v_w_gate, v_w_up, v_w_down, v_post_ffn_norm):
    a = dict(locals())
    cx, cy, cc = _coords()
    s_me = 2 * cx + cy
    core = jnp.stack([cc]).astype(jnp.int32)
    shard_core = jnp.stack([s_me, cc]).astype(jnp.int32)
    me = jnp.stack([4 * cx + 2 * cy + cc]).astype(jnp.int32)
    small = {n: (a[n][0] if a[n].ndim > 2 else a[n]) for n in SMALL}
    b_t = small["sgu_b_spatial"].T
    xs, pos, target = x[0], positions.reshape(SEQ, 1), loss_target[0]
    flipped = ("w_gate", "w_up")

    def big(name, n):
        return jnp.swapaxes(a[name], 1, 2)[0] if n in flipped else a[name][0]

    own = {"w_in": _halves(big("w_in", "w_in").astype(BF16))}

    def with_own(full, n):
        full = lax.dynamic_update_slice(full, own[n][None], (s_me, 0, 0, 0))
        return full.reshape((N_SHARD,) + big(n, n).shape)

    ffn = ("w_gate", "w_up", "w_down")
    g_in = _gather_ici([own["w_in"]])
    (s_in,), token = _split_starts("gather_in_start", [g_in], small["pre_mix_norm"])
    for n in ("w_out",) + ffn:
        own[n] = _halves((big(n, n) + token[0:1, 0:1]).astype(BF16))
    g_out, g_ffn = _gather_ici([own["w_out"]]), _gather_ici([own[n] for n in ffn])
    (s_out, s_ffn), token = _split_starts("gather_rest_start", [g_out, g_ffn], token)
    in_lands = _split_wait("gather_in_wait", g_in, *s_in, token)
    ((in_lands,),) = _comm_only("comm_pass_in", [_gather_pass(in_lands)])
    w_in_f = with_own(in_lands, "w_in")
    h, u, vs, sgu, *qkv = _inproj_fwd(xs, pos, small["pre_mix_norm"], w_in_f, small["sgu_ln_gain"],
                                      small["sgu_ln_bias"], small["sgu_w_spatial"], b_t)
    views = [tuple(qkv[3 * i:3 * i + 3]) for i in range(len(DILATIONS))]
    out_lands = _split_wait("gather_out_wait", g_out, *s_out, sgu)
    o_list, l_list = [], []
    for dil, (qv, kv, vv) in zip(DILATIONS, views):
        if dil == 1:
            (o, l), ((out_lands,),) = _attn_fwd(qv, kv, vv, dil, comms=[_gather_pass(out_lands)])
        else:
            o, l = _attn_fwd(qv, kv, vv, dil)
        o_list.append(o)
        l_list.append(l)
    w_out_f = with_own(out_lands, "w_out")
    ffn_lands = _split_wait("gather_ffn_wait", g_ffn, *s_ffn, l_list[-1])
    (attn, mixed, y, x1, *lses), (ffn_lands,) = _mix_out_fwd(
        o_list, l_list, sgu, xs, w_out_f, small["attn_out_norm"], small["sgu_out_norm"], small["post_mix_norm"],
        comms=[_gather_pass(ffn_lands)])
    w_gate_f, w_up_f, w_down_f = (with_own(f, n).reshape(FF, D_MODEL) for f, n in zip(ffn_lands, ffn))
    h2, act, dg, dup, df, dx1, loss_cols, d_pre_ffn, d_post_ffn = _ffn_fwd_bwd(
        x1, target, w_gate_f, w_up_f, w_down_f, small["pre_ffn_norm"], small["post_ffn_norm"])

    full_tok = pl.BlockSpec((SEQ, D_MODEL), lambda s: (0, 0), pipeline_mode=pl.Buffered(1))
    ff_tok = pl.BlockSpec((1, SEQ, FF_S), lambda s: (s, 0, 0))
    gw = {}
    gw["w_gate"], gw["w_up"] = _wgrad_ff([dg, dup], h2, "wgrad_gate_up")
    (gw["w_down"],), ((sib_gate,),) = _wgrad_ff([act], df, "wgrad_down", comms=[_rs_sibling([gw["w_gate"]])])
    (dy, du, dvs_sgu, d_post_mix, d_attn_norm, d_sgu_norm, d_w_sp, d_b_sp, d_ln_gain, d_ln_bias,
     *dviews), ((sib_up, sib_down),) = _outproj_bwd(
        dx1, y, attn, sgu, w_out_f, small["post_mix_norm"], small["attn_out_norm"], small["sgu_out_norm"],
        u, vs, small["sgu_ln_gain"], small["sgu_ln_bias"], small["sgu_w_spatial"], b_t,
        comms=[_rs_sibling([gw["w_up"], gw["w_down"]])])
    sib = {"w_gate": sib_gate, "w_up": sib_up, "w_down": sib_down}
    part = dict(zip(ffn, _chip_sums([gw[n] for n in ffn], [sib[n] for n in ffn], shard_core)))
    gw["w_out"] = _wgrad(mixed, dy, pl.BlockSpec((SEQ, OUT_S), lambda s: (0, s)), full_tok, (OUT_S, D_MODEL),
                         "wgrad_out")
    x_ffn = _rs_chips([part[n] for n in ffn])
    packed_early = _pack({
        "sgu_ln_gain": d_ln_gain, "sgu_ln_bias": d_ln_bias, "sgu_b_spatial": d_b_sp,
        "attn_out_norm": d_attn_norm, "sgu_out_norm": d_sgu_norm, "post_mix_norm": d_post_mix,
        "pre_ffn_norm": d_pre_ffn, "post_ffn_norm": d_post_ffn, LOSS_ROW: loss_cols}, SMALL_EARLY, 8)
    packed_wsp = _pack({"sgu_w_spatial": d_w_sp}, SMALL_WSP, CHUNK, BF16)
    x_small, x_wsp = _small_exchange(packed_early), _small_exchange(packed_wsp)
    (s_ffn, s_small, s_wsp), token = _split_starts("rs_ffn_small_start", [x_ffn, x_small, x_wsp],
                                                   small["pre_mix_norm"])

    dqs, dks, dvs = [], [], []
    for i, (dil, (qv, kv, vv)) in enumerate(zip(DILATIONS, views)):
        if dil == 1:
            (dq, dk, dv), ((sib["w_out"],),) = _attn_bwd(qv, kv, vv, dviews[i], dviews[3 + i], lses[i], dil,
                                                        comms=[_rs_sibling([gw["w_out"]])], after=[token])
            (part["w_out"],) = _chip_sums([gw["w_out"]], [sib["w_out"]], shard_core)
            x_out = _rs_chips([part["w_out"]])
            (s_out,), token = _split_starts("rs_out_start", [x_out], token)
        else:
            dq, dk, dv = _attn_bwd(qv, kv, vv, dviews[i], dviews[3 + i], lses[i], dil, after=[token])
        dqs.append(dq)
        dks.append(dk)
        dvs.append(dv)
    half, far, joined = {}, {}, {}
    far.update(zip(ffn, _split_wait("rs_ffn_wait", x_ffn, *s_ffn, dvs[-1])))
    half.update(zip(ffn, _final_sums([gw[n] for n in ffn], [sib[n] for n in ffn], [far[n] for n in ffn],
                                     shard_core)))
    dproj, grad_x, d_pre_mix = _inproj_bwd(dqs, dks, dvs, du, dvs_sgu, pos, xs, dx1, w_in_f, small["pre_mix_norm"])
    (far["w_out"],) = _split_wait("rs_out_wait", x_out, *s_out, grad_x)
    half["w_out"] = _rs_final_sum(gw["w_out"], sib["w_out"], far["w_out"], shard_core)
    packed_late = _pack({"pre_mix_norm": d_pre_mix}, SMALL_LATE, 8)
    names = ffn + ("w_out",)
    gw["w_in"], (got, (slots_late,)) = _wgrad(
        h, dproj, full_tok, pl.BlockSpec((SEQ, IN_S), lambda s: (0, s)), (D_MODEL, IN_S), "wgrad_in",
        comms=[_rs_join([half[n] for n in names]), _small_exchange(packed_late)])
    joined.update(zip(names, got))
    ((sib["w_in"],),) = _comm_only("comm_rs_sibling_in", [_rs_sibling([gw["w_in"]])])
    (part["w_in"],) = _chip_sums([gw["w_in"]], [sib["w_in"]], shard_core)
    x_in = _rs_chips([part["w_in"]])
    (s_in,), token = _split_starts("rs_in_start", [x_in], small["pre_mix_norm"])

    grads, deltas, new_m, new_v = {}, {}, {}, {}

    def record(n, outs):
        grads[n], deltas[n], new_m[n], new_v[n] = (
            jnp.swapaxes(o[None], 1, 2) if n in flipped else o[None] for o in outs)

    def update(names, block_rows, name, after):
        for n, outs in zip(names, _adamw_multi(
                [big(n, n) for n in names], [joined[n].reshape(big(n, n).shape) for n in names],
                [big("m_" + n, n) for n in names], [big("v_" + n, n) for n in names], block_rows, name, after)):
            record(n, outs)

    update(ffn, FF_S // 4, "adamw_ffn", [token])
    update(("w_out",), BIG_ADAM_ROWS["w_out"], "adamw_w_out", [token])
    (slots_early,) = _split_wait("small_early_wait", x_small, *s_small, [new_v[n] for n in ("w_down", "w_out")])
    (slots_wsp,) = _split_wait("small_wsp_wait", x_wsp, *s_wsp, slots_early)
    (far["w_in"],) = _split_wait("rs_in_wait", x_in, *s_in, slots_wsp)
    half["w_in"] = _rs_final_sum(gw["w_in"], sib["w_in"], far["w_in"], shard_core)
    ((joined["w_in"],),) = _comm_only("comm_rs_join_in", [_rs_join([half["w_in"]])])
    update(("w_in",), BIG_ADAM_ROWS["w_in"], "adamw_w_in", [])
    a[LOSS_ROW] = a["m_" + LOSS_ROW] = a["v_" + LOSS_ROW] = jnp.zeros((1, D_MODEL), F32)
    for names, rows, packed, slots in ((SMALL_EARLY, 8, packed_early, slots_early),
                                       (SMALL_WSP, CHUNK, packed_wsp, slots_wsp),
                                       (SMALL_LATE, 8, packed_late, slots_late)):
        outs = _adamw_small(packed, slots, _pack(a, names, rows), _pack({n: a["m_" + n] for n in names}, names, rows),
                            _pack({n: a["v_" + n] for n in names}, names, rows), me)
        for dst, buf in zip((grads, deltas, new_m, new_v), outs):
            dst.update(_unpack(buf, names, {n: a[n].shape for n in names}))
    loss = jnp.sum(grads[LOSS_ROW]) * np.float32(0.5 / D_MODEL)
    return (loss, grad_x[None], *[grads[n] for n in WEIGHTS], *[deltas[n] for n in WEIGHTS],
            *[new_m[n] for n in WEIGHTS], *[new_v[n] for n in WEIGHTS])
```

```python
import numpy as np
import jax
import jax.numpy as jnp
from jax import lax
from jax.experimental import pallas as pl
from jax.experimental.pallas import tpu as pltpu

F32 = jnp.float32
BF16 = jnp.bfloat16

SEQ = 2048
D_MODEL = 1024
HEAD_DIM = 64
ATTN_W = 512
SGU_W = 512
SGU_GROUPS = 8
CHUNK = 128
DILATIONS = (1, 4, 16)
N_SHARD = 4
IN_S = 640
OUT_S = 256
FF_S = 704
PROJ_W = N_SHARD * IN_S
FF = N_SHARD * FF_S
FF_CHUNKS = ((0, 1024), (1024, 2048), (2048, FF))
RMS_EPS = 1e-6
LN_EPS = 1e-5
ROPE_THETA = 500000.0
ATTN_SCALE = 1.0 / np.sqrt(HEAD_DIM)
NEG = -1e30
TM = 512
TM_FFN = 256
VMEM_LIMIT = 56 * 1024 * 1024
SMALL_ROWS = 136

ADAM_LR = 0.001
ADAM_B1 = 0.9
ADAM_B2 = 0.999
ADAM_EPS = 1e-08
ADAM_WD = 0.01
ADAM_STEP = 10

MESH = pl.DeviceIdType.MESH
ANY = pl.BlockSpec(memory_space=pl.ANY)


def _dot(a, b):
    return jnp.dot(a, b, preferred_element_type=F32)


def _dot_nt(a, b):
    return lax.dot_general(a, b, (((1,), (1,)), ((), ())), preferred_element_type=F32)


def _dot_tn(a, b):
    return lax.dot_general(a, b, (((0,), (0,)), ((), ())), preferred_element_type=F32)


def _dot_exact(a, b):
    return jnp.dot(a, b, preferred_element_type=F32, precision=lax.Precision.HIGHEST)


def _dot_select(a, sel):
    hi = a.astype(BF16)
    lo = (a - hi.astype(F32)).astype(BF16)
    sel = sel.astype(BF16)
    return _dot(hi, sel) + _dot(lo, sel)


def _rms_stats(x):
    r = lax.rsqrt(jnp.mean(x * x, axis=-1, keepdims=True) + RMS_EPS)
    return x * r, r


def _rms_bwd(xh, r, gain, dy):
    dxh = dy * gain
    dx = r * (dxh - xh * jnp.mean(dxh * xh, axis=-1, keepdims=True))
    return dx, jnp.sum(dy * xh, axis=0, keepdims=True)


_ERF_ALPHA = (-2.72614225801306e-10, 2.77068142495902e-08, -2.10102402082508e-06, -5.69250639462346e-05,
              -7.34990630326855e-04, -2.95459980854025e-03, -1.60960333262415e-02)
_ERF_BETA = (-1.45660718464996e-05, -2.13374055278905e-04, -1.68282697438203e-03, -7.37332916720468e-03,
             -1.42647390514189e-02)


def _erf(x):
    x = jnp.clip(x, -4.0, 4.0)
    x2 = x * x
    p = jnp.full_like(x, _ERF_ALPHA[0])
    for a in _ERF_ALPHA[1:]:
        p = p * x2 + a
    q = jnp.full_like(x, _ERF_BETA[0])
    for b in _ERF_BETA[1:]:
        q = q * x2 + b
    return x * p / q


def _normal_cdf(x):
    return 0.5 * (1.0 + _erf(x * np.float32(1.0 / np.sqrt(2.0))))


def _gelu_grad(x, cdf):
    pdf = jnp.exp(-0.5 * x * x) * np.float32(1.0 / np.sqrt(2.0 * np.pi))
    return cdf + x * pdf


def _sigmoid(x):
    return 1.0 / (1.0 + jnp.exp(-x))


_INV_FREQ = tuple(float(np.float32(ROPE_THETA ** (-2.0 * j / 16.0))) for j in range(8))


def _rot_tables(pos):
    lane = lax.broadcasted_iota(jnp.int32, (1, 128), 1)
    d = lane & 63
    j = d & 7
    inv = jnp.zeros((1, 128), F32)
    for jj in range(8):
        inv = jnp.where(j == jj, _INV_FREQ[jj], inv)
    ang = pos.astype(F32) * inv
    c = jnp.cos(ang)
    s = jnp.sin(ang)
    cos_t = jnp.where(d < 16, c, 1.0)
    sin_a = jnp.where(d < 8, -s, 0.0)
    sin_b = jnp.where((d >= 8) & (d < 16), s, 0.0)
    return tuple(jnp.tile(t, (1, 4)) for t in (cos_t, sin_a, sin_b))


def _rope(x, tabs):
    cos_t, sin_a, sin_b = tabs
    return x * cos_t + pltpu.roll(x, 504, 1) * sin_a + pltpu.roll(x, 8, 1) * sin_b


def _rope_bwd(dy, tabs):
    cos_t, sin_a, sin_b = tabs
    return dy * cos_t + pltpu.roll(dy * sin_a, 8, 1) + pltpu.roll(dy * sin_b, 504, 1)


def _left_half():
    return lax.broadcasted_iota(jnp.int32, (CHUNK, CHUNK), 1) < HEAD_DIM


def _group_ones():
    lane = lax.broadcasted_iota(jnp.int32, (SGU_GROUPS, SGU_W), 1)
    row = lax.broadcasted_iota(jnp.int32, (SGU_GROUPS, SGU_W), 0)
    return ((lane >> 6) == row).astype(F32)


def _masked_spatial(w_ref):
    row = lax.broadcasted_iota(jnp.int32, (CHUNK, CHUNK), 0)
    col = lax.broadcasted_iota(jnp.int32, (CHUNK, CHUNK), 1)
    return [jnp.where(col <= row, w_ref[g], 0.0).astype(BF16) for g in range(SGU_GROUPS)]


def _sgu_core(u, vs, lg, lb, wm, bias_full):
    tm = u.shape[0]
    cdf_u, cdf_vs = _normal_cdf(u), _normal_cdf(vs)
    gu = u * cdf_u
    gv = vs * cdf_vs
    mu = jnp.mean(gv, axis=-1, keepdims=True)
    xc = gv - mu
    rstd = lax.rsqrt(jnp.mean(xc * xc, axis=-1, keepdims=True) + LN_EPS)
    xh = xc * rstd
    vnb = (xh * lg + lb).astype(BF16)
    left = _left_half()
    rows = []
    for c in range(tm // CHUNK):
        pieces = []
        for p in range(4):
            vp = vnb[c * CHUNK:(c + 1) * CHUNK, p * 128:(p + 1) * 128]
            pieces.append(jnp.where(left, _dot(wm[2 * p], vp), _dot(wm[2 * p + 1], vp)))
        rows.append(jnp.concatenate(pieces, axis=1) + bias_full)
    mixed = jnp.concatenate(rows, axis=0)
    return gu, xh, rstd, vnb, mixed, cdf_u, cdf_vs


def _resident(shape):
    n = len(shape)
    return pl.BlockSpec(shape, lambda *_: (0,) * n, pipeline_mode=pl.Buffered(1))


def _rows(ncol, tm=TM):
    return pl.BlockSpec((tm, ncol), lambda i: (i, 0))


def _rows3(nlead, ncol, tm=TM):
    return pl.BlockSpec((nlead, tm, ncol), lambda i: (0, i, 0))


def _acc(ncol, nrow=1):
    return pl.BlockSpec((nrow, ncol), lambda i: (0, 0))


HEAD_W = 128


def _view_rows(dil, width=ATTN_W, tm=TM):
    return pl.BlockSpec((tm // dil, dil * width), lambda i: (i, 0))


def _view_shape(dil, dtype, width=ATTN_W):
    return _sds((SEQ // dil, dil * width), dtype)


def _slab_scratch():
    return pltpu.VMEM((4, TM, 128), F32)


def _store_view(val, out_ref, slabs, dil):
    width = val.shape[1]
    for j in range(width // 128):
        slabs[j] = val[:, j * 128:(j + 1) * 128]
    for r in range(dil):
        for j in range(width // 128):
            c0 = r * width + j * 128
            out_ref[:, c0:c0 + 128] = slabs.at[j][pl.ds(r, TM // dil, stride=dil), :].astype(out_ref.dtype)


def _load_view(in_ref, slabs, dil, width=ATTN_W):
    for r in range(dil):
        for j in range(width // 128):
            c0 = r * width + j * 128
            slabs.at[j][pl.ds(r, TM // dil, stride=dil), :] = in_ref[:, c0:c0 + 128].astype(F32)
    return jnp.concatenate([slabs[j] for j in range(width // 128)], axis=1)


def _head_spread():
    m = lax.broadcasted_iota(jnp.int32, (HEAD_W, ATTN_W), 0)
    lane = lax.broadcasted_iota(jnp.int32, (HEAD_W, ATTN_W), 1)
    return (m == 16 * (lane >> 6)).astype(F32)


def _head_sum():
    lane = lax.broadcasted_iota(jnp.int32, (ATTN_W, HEAD_W), 0)
    m = lax.broadcasted_iota(jnp.int32, (ATTN_W, HEAD_W), 1)
    return ((lane >> 6) == (m >> 4)).astype(F32)


def _seq_params():
    return pltpu.CompilerParams(dimension_semantics=("arbitrary",), vmem_limit_bytes=VMEM_LIMIT)


def _sds(shape, dtype):
    return jax.ShapeDtypeStruct(shape, dtype)


class _Comm:
    def __init__(self, args, out_shape, n_sems, start, finish, aliased=False):
        self.args, self.out_shape, self.n_sems = list(args), list(out_shape), n_sems
        self.start, self.finish, self.aliased = start, finish, aliased


def _pcall(body, *, name, grid, in_specs, out_specs, out_shape, args, scratch_shapes=(), comms=(), after=()):
    single = not isinstance(out_shape, (list, tuple))
    out_specs = [out_specs] if single else list(out_specs)
    out_shape = [out_shape] if single else list(out_shape)
    n_in, n_out, n_scr = len(in_specs), len(out_shape), len(scratch_shapes)
    c_args = [a for c in comms for a in c.args]
    c_outs = [o for c in comms for o in c.out_shape]
    aliases, ai, ao = {}, n_in, n_out
    for c in comms:
        if c.aliased:
            aliases.update({ai + k: ao + k for k in range(len(c.args))})
        ai += len(c.args)
        ao += len(c.out_shape)
    sems = [pltpu.SemaphoreType.DMA((c.n_sems,)) for c in comms for _ in range(2)]
    steps = grid[0]

    def wrapped(*refs):
        o0 = n_in + len(c_args) + len(after)
        s0 = o0 + n_out + len(c_outs)
        m_in, m_out, m_sem = refs[n_in:n_in + len(c_args)], refs[o0 + n_out:s0], refs[s0 + n_scr:]

        def each(phase):
            ii = oi = 0
            for k, c in enumerate(comms):
                getattr(c, phase)(m_in[ii:ii + len(c.args)], m_out[oi:oi + len(c.out_shape)],
                                  m_sem[2 * k], m_sem[2 * k + 1])
                ii += len(c.args)
                oi += len(c.out_shape)

        if comms:
            @pl.when(pl.program_id(0) == 0)
            def _():
                each("start")

        body(*refs[:n_in], *refs[o0:o0 + n_out], *refs[s0:s0 + n_scr])

        if comms:
            @pl.when(pl.program_id(0) == steps - 1)
            def _():
                each("finish")

    res = pl.pallas_call(
        wrapped, name=name, grid=grid,
        in_specs=list(in_specs) + [ANY] * (len(c_args) + len(after)), out_specs=out_specs + [ANY] * len(c_outs),
        out_shape=out_shape + c_outs, scratch_shapes=list(scratch_shapes) + sems,
        input_output_aliases=aliases, compiler_params=_seq_params(),
    )(*args, *c_args, *after)
    mine = res[0] if single else list(res[:n_out])
    if not comms:
        return mine
    theirs, oi = [], n_out
    for c in comms:
        theirs.append(list(res[oi:oi + len(c.out_shape)]))
        oi += len(c.out_shape)
    return mine, theirs


def _in_pieces(g):
    lo, hi = 512 * g, 512 * (g + 1)
    return [(s, max(lo, IN_S * s) - IN_S * s, min(hi, IN_S * (s + 1)) - IN_S * s)
            for s in range(N_SHARD) if max(lo, IN_S * s) < min(hi, IN_S * (s + 1))]


def _inproj_fwd(x, pos, g_pre, w_in, lg, lb, w_sp, b_t, comms=()):
    def body(x_ref, pos_ref, g_ref, w_ref, lg_ref, lb_ref, wsp_ref, bt_ref, h_ref, u_ref, vs_ref, sgu_ref, *rest):
        qkv_refs, slabs = rest[:9], rest[9:]
        xh, _ = _rms_stats(x_ref[...])
        h = (xh * g_ref[...]).astype(BF16)
        h_ref[...] = h
        tabs = _rot_tables(pos_ref[...])

        def group(g):
            return jnp.concatenate([_dot(h, w_ref[s, :, a:b]) for s, a, b in _in_pieces(g)], axis=1)

        for t in range(3):
            val = group(t)
            if t < 2:
                val = _rope(val, tabs)
            if t == 0:
                val = val * np.float32(ATTN_SCALE)
            qkv_refs[t][...] = val.astype(BF16)
            for i, dil in enumerate(DILATIONS[1:]):
                _store_view(val, qkv_refs[3 * (i + 1) + t], slabs[t], dil)
        u = group(3)
        vs = group(4)
        u_ref[...] = u
        vs_ref[...] = vs
        bias_full = _dot_exact(bt_ref[...], _group_ones())
        gu, _, _, _, mixed, _, _ = _sgu_core(u, vs, lg_ref[...], lb_ref[...], _masked_spatial(wsp_ref), bias_full)
        sgu_ref[...] = gu * mixed

    return _pcall(
        body, name="inproj_sgu_fwd", grid=(SEQ // TM,),
        in_specs=[_rows(D_MODEL), _rows(1), _resident((1, D_MODEL)), _resident((N_SHARD, D_MODEL, IN_S)),
                  _resident((1, SGU_W)), _resident((1, SGU_W)), _resident((SGU_GROUPS, CHUNK, CHUNK)),
                  _resident((CHUNK, SGU_GROUPS))],
        out_specs=[_rows(D_MODEL), _rows(512), _rows(512), _rows(512)]
        + [_view_rows(dil) for dil in DILATIONS for _ in range(3)],
        out_shape=[_sds((SEQ, D_MODEL), BF16), _sds((SEQ, 512), F32), _sds((SEQ, 512), F32), _sds((SEQ, 512), F32)]
        + [_view_shape(dil, BF16) for dil in DILATIONS for _ in range(3)],
        scratch_shapes=[_slab_scratch() for _ in range(3)],
        args=(x, pos, g_pre, w_in, lg, lb, w_sp, b_t), comms=comms)


def _sgu_fwd(u, vs, lg, lb, w_sp, b_t, comms=()):
    def body(u_ref, vs_ref, lg_ref, lb_ref, w_ref, bt_ref, out_ref):
        wm = _masked_spatial(w_ref)
        bias_full = _dot_exact(bt_ref[...], _group_ones())
        gu, _, _, _, mixed, _, _ = _sgu_core(u_ref[...], vs_ref[...], lg_ref[...], lb_ref[...], wm, bias_full)
        out_ref[...] = gu * mixed

    return _pcall(
        body, name="sgu_fwd", grid=(SEQ // TM,),
        in_specs=[_rows(SGU_W), _rows(SGU_W), _resident((1, SGU_W)), _resident((1, SGU_W)),
                  _resident((SGU_GROUPS, CHUNK, CHUNK)), _resident((CHUNK, SGU_GROUPS))],
        out_specs=_rows(SGU_W),
        out_shape=_sds((SEQ, SGU_W), F32),
        args=(u, vs, lg, lb, w_sp, b_t), comms=comms)


def _block_masks():
    row = lax.broadcasted_iota(jnp.int32, (CHUNK, CHUNK), 0)
    col = lax.broadcasted_iota(jnp.int32, (CHUNK, CHUNK), 1)
    return col <= row, col >= row


def _attn_fwd(qv, kv, vv, dil, comms=()):
    seg = SEQ // dil
    nblk = seg // CHUNK
    rps = 4 if nblk == 1 else 1

    def body(q_ref, k_ref, v_ref, o_ref, l_ref):
        left = _left_half()
        m_cur, m_prev = _block_masks()
        zero = jnp.zeros((CHUNK, CHUNK), BF16)
        ones = (jnp.where(left, 1.0, 0.0).astype(BF16), jnp.where(left, 0.0, 1.0).astype(BF16))

        sides = tuple(enumerate((left, ~left)))

        def rows(b):
            if isinstance(b, int):
                return b * CHUNK, max(b - 1, 0) * CHUNK
            return pl.multiple_of(b * CHUNK, CHUNK), pl.multiple_of(jnp.maximum(b - 1, 0) * CHUNK, CHUNK)

        def first(rr, b):
            r0, rp = rows(b)
            prev_ok = m_prev & (b > 0)
            tiles, scores = [], []
            for hp in range(4):
                ls = slice(rr * ATTN_W + hp * 128, rr * ATTN_W + (hp + 1) * 128)
                qp = q_ref[pl.ds(r0, CHUNK), ls]
                kc = k_ref[pl.ds(r0, CHUNK), ls]
                kp = k_ref[pl.ds(rp, CHUNK), ls] if nblk > 1 else None
                tiles.append((ls, v_ref[pl.ds(r0, CHUNK), ls], v_ref[pl.ds(rp, CHUNK), ls] if nblk > 1 else None))
                for _, hm in sides:
                    qh = jnp.where(hm, qp, zero)
                    sc = jnp.where(m_cur, _dot_nt(qh, kc), NEG)
                    sp = jnp.where(prev_ok, _dot_nt(qh, kp), NEG) if nblk > 1 else None
                    scores.append((sc, sp))
            return tiles, scores

        def second(scores):
            probs = []
            for sc, sp in scores:
                if nblk > 1:
                    m = jnp.max(jnp.maximum(sc, sp), axis=-1, keepdims=True)
                    pc = jnp.exp(sc - m)
                    pp = jnp.exp(sp - m)
                    probs.append((m, pc.astype(BF16), pp.astype(BF16), (pc + pp).astype(BF16)))
                else:
                    m = jnp.max(sc, axis=-1, keepdims=True)
                    pc = jnp.exp(sc - m).astype(BF16)
                    probs.append((m, pc, None, pc))
            return probs

        def third(rr, b, tiles, probs):
            r0, _ = rows(b)
            for hp, (ls, vc, vp) in enumerate(tiles):
                acc = jnp.zeros((CHUNK, CHUNK), F32)
                den = jnp.zeros((CHUNK, CHUNK), F32)
                for side, hm in sides:
                    _, pc, pp, psum = probs[2 * hp + side]
                    acc = acc + _dot(pc, jnp.where(hm, vc, zero))
                    if nblk > 1:
                        acc = acc + _dot(pp, jnp.where(hm, vp, zero))
                    den = den + _dot(psum, ones[side])
                o_ref[pl.ds(r0, CHUNK), ls] = (acc / den).astype(o_ref.dtype)
                lse = jnp.where(left, probs[2 * hp][0], probs[2 * hp + 1][0]) + jnp.log(den)
                l_ref[pl.ds(r0, CHUNK), rr * HEAD_W + 32 * hp:rr * HEAD_W + 32 * hp + 32] = lse[:, 48:80]

        def run(units):
            data = [first(rr, b) for rr, b in units]
            probs = [second(scores) for _, scores in data]
            for (rr, b), (tiles, _), pr in zip(units, data, probs):
                third(rr, b, tiles, pr)

        if nblk == 1:
            run([(rr, 0) for rr in range(rps)])
        else:
            def one(b, carry):
                run([(0, b)])
                return carry

            lax.fori_loop(0, nblk, one, 0)

    spec = pl.BlockSpec((seg, rps * ATTN_W), lambda r: (0, r))
    return _pcall(
        body, name=f"attn_fwd_d{dil}", grid=(dil // rps,),
        in_specs=[spec, spec, spec], out_specs=[spec, pl.BlockSpec((seg, rps * HEAD_W), lambda r: (0, r))],
        out_shape=[_sds((seg, dil * ATTN_W), BF16), _sds((seg, dil * HEAD_W), F32)],
        args=(qv, kv, vv), comms=comms)


def _lane_left(nrows):
    return lax.broadcasted_iota(jnp.int32, (nrows, CHUNK), 1) < HEAD_DIM


def _attn_rows(b):
    if isinstance(b, int):
        return b * CHUNK, max(b - 1, 0) * CHUNK
    return pl.multiple_of(b * CHUNK, CHUNK), pl.multiple_of(jnp.maximum(b - 1, 0) * CHUNK, CHUNK)


def _attn_fwd_fused(qv, kv, vv, dil, comms=()):
    seg = SEQ // dil
    nblk = seg // CHUNK
    rps = 4 if nblk == 1 else 1

    def body(q_ref, k_ref, v_ref, o_ref, l_ref):
        left = _left_half()
        m_cur, m_prev = _block_masks()
        sides = tuple(enumerate((left, ~left)))
        zero = jnp.zeros((CHUNK, CHUNK), BF16)

        def first(rr, b, both):
            r0, rp = _attn_rows(b)
            keys = pl.ds(rp, 2 * CHUNK) if both else pl.ds(r0, CHUNK)
            ok = jnp.concatenate([m_prev, m_cur], axis=1) if both else m_cur
            tiles, scores = [], []
            for hp in range(4):
                ls = slice(rr * ATTN_W + hp * 128, rr * ATTN_W + (hp + 1) * 128)
                qp = q_ref[pl.ds(r0, CHUNK), ls]
                k2 = k_ref[keys, ls]
                tiles.append((ls, v_ref[keys, ls]))
                for _, hm in sides:
                    scores.append(jnp.where(ok, _dot_nt(jnp.where(hm, qp, zero), k2), NEG))
            return tiles, scores

        def second(scores):
            probs = []
            for s in scores:
                m = jnp.max(s, axis=-1, keepdims=True)
                probs.append((m, jnp.exp(s - m).astype(BF16)))
            return probs

        def third(rr, b, tiles, probs, both):
            r0, _ = _attn_rows(b)
            nk = 2 * CHUNK if both else CHUNK
            left_k = _lane_left(nk)
            ones = (jnp.where(left_k, 1.0, 0.0).astype(BF16), jnp.where(left_k, 0.0, 1.0).astype(BF16))
            zero_k = jnp.zeros((nk, CHUNK), BF16)
            for hp, (ls, v2) in enumerate(tiles):
                acc = jnp.zeros((CHUNK, CHUNK), F32)
                den = jnp.zeros((CHUNK, CHUNK), F32)
                for side in range(2):
                    p = probs[2 * hp + side][1]
                    acc = acc + _dot(p, jnp.where(left_k if side == 0 else ~left_k, v2, zero_k))
                    den = den + _dot(p, ones[side])
                o_ref[pl.ds(r0, CHUNK), ls] = (acc / den).astype(o_ref.dtype)
                lse = jnp.where(left, probs[2 * hp][0], probs[2 * hp + 1][0]) + jnp.log(den)
                l_ref[pl.ds(r0, CHUNK), rr * HEAD_W + 32 * hp:rr * HEAD_W + 32 * hp + 32] = lse[:, 48:80]

        def run(units, both):
            data = [first(rr, b, both) for rr, b in units]
            probs = [second(scores) for _, scores in data]
            for (rr, b), (tiles, _), pr in zip(units, data, probs):
                third(rr, b, tiles, pr, both)

        run([(rr, 0) for rr in range(rps)], False)
        if nblk > 1:
            def one(b, carry):
                run([(0, b)], True)
                return carry

            lax.fori_loop(1, nblk, one, 0)

    spec = pl.BlockSpec((seg, rps * ATTN_W), lambda r: (0, r))
    return _pcall(
        body, name=f"attn_fwd_d{dil}", grid=(dil // rps,),
        in_specs=[spec, spec, spec], out_specs=[spec, pl.BlockSpec((seg, rps * HEAD_W), lambda r: (0, r))],
        out_shape=[_sds((seg, dil * ATTN_W), BF16), _sds((seg, dil * HEAD_W), F32)],
        args=(qv, kv, vv), comms=comms)


def _mix_out_fwd(o_list, l_list, sgu, x, w_out, g_attn, g_sgu, g_post, comms=()):
    def body(o1, o2, o3, l1, l2, l3, sgu_ref, x_ref, w_ref, ga_ref, gs_ref, gp_ref,
             attn_ref, mixed_ref, y_ref, x1_ref, lse1_ref, lse2_ref, lse3_ref, slabs_a, slabs_b):
        os = [o1[...], _load_view(o2, slabs_a, DILATIONS[1]), _load_view(o3, slabs_b, DILATIONS[2])]
        ls = [l1[...], _load_view(l2, slabs_a, DILATIONS[1], HEAD_W), _load_view(l3, slabs_b, DILATIONS[2], HEAD_W)]
        m = jnp.maximum(jnp.maximum(ls[0], ls[1]), ls[2])
        es = [jnp.exp(l - m) for l in ls]
        den = es[0] + es[1] + es[2]
        spread = _head_spread()
        attn = sum(_dot_select(e / den, spread) * o for e, o in zip(es, os))
        attn_ref[...] = attn
        lse = m + jnp.log(den)
        lse1_ref[...] = lse
        _store_view(lse, lse2_ref, slabs_a, DILATIONS[1])
        _store_view(lse, lse3_ref, slabs_b, DILATIONS[2])
        ah, _ = _rms_stats(attn)
        sh, _ = _rms_stats(sgu_ref[...])
        mixed = jnp.concatenate([ah * ga_ref[...], sh * gs_ref[...]], axis=1).astype(BF16)
        mixed_ref[...] = mixed
        y = _dot(mixed[:, 0:OUT_S], w_ref[0])
        for s in range(1, N_SHARD):
            y = y + _dot(mixed[:, s * OUT_S:(s + 1) * OUT_S], w_ref[s])
        y_ref[...] = y
        yh, _ = _rms_stats(y)
        x1_ref[...] = x_ref[...] + yh * gp_ref[...]

    return _pcall(
        body, name="mix_out_fwd", grid=(SEQ // TM,),
        in_specs=[_view_rows(dil) for dil in DILATIONS] + [_view_rows(dil, HEAD_W) for dil in DILATIONS]
        + [_rows(512), _rows(D_MODEL), _resident((N_SHARD, OUT_S, D_MODEL)),
           _resident((1, 512)), _resident((1, 512)), _resident((1, D_MODEL))],
        out_specs=[_rows(512), _rows(D_MODEL), _rows(D_MODEL), _rows(D_MODEL)]
        + [_view_rows(dil, HEAD_W) for dil in DILATIONS],
        out_shape=[_sds((SEQ, 512), F32), _sds((SEQ, D_MODEL), BF16), _sds((SEQ, D_MODEL), F32),
                   _sds((SEQ, D_MODEL), F32)] + [_view_shape(dil, F32, HEAD_W) for dil in DILATIONS],
        scratch_shapes=[_slab_scratch(), _slab_scratch()],
        args=(*o_list, *l_list, sgu, x, w_out, g_attn, g_sgu, g_post), comms=comms)


def _ffn_fwd_bwd(x1, target, w_gate, w_up, w_down, g_pre, g_post, comms=()):
    def body(x1_ref, t_ref, wg_ref, wu_ref, wd_ref, gpf_ref, gpo_ref,
             h2_ref, a_ref, dg_ref, dup_ref, df_ref, dx1_ref, loss_ref, dgpf_ref, dgpo_ref, g_scr, up_scr):
        @pl.when(pl.program_id(0) == 0)
        def _():
            loss_ref[...] = jnp.zeros_like(loss_ref)
            dgpf_ref[...] = jnp.zeros_like(dgpf_ref)
            dgpo_ref[...] = jnp.zeros_like(dgpo_ref)

        x1 = x1_ref[...]
        gpf = gpf_ref[...]
        gpo = gpo_ref[...]
        xh, r = _rms_stats(x1)
        h2 = (xh * gpf).astype(BF16)
        h2_ref[...] = h2
        f = jnp.zeros((TM_FFN, D_MODEL), F32)
        for c0, c1 in FF_CHUNKS:
            g = _dot_nt(h2, wg_ref[c0:c1, :])
            up = _dot_nt(h2, wu_ref[c0:c1, :])
            g_scr[:, c0:c1] = g
            up_scr[:, c0:c1] = up
            a = (g * _sigmoid(g) * up).astype(BF16)
            a_ref[:, c0:c1] = a
            f = f + _dot(a, wd_ref[c0:c1, :])
        fh, rf = _rms_stats(f)
        diff = x1 + fh * gpo - t_ref[...]
        loss_ref[...] += jnp.sum(diff * diff, axis=0, keepdims=True)
        dout = diff * np.float32(1.0 / D_MODEL)
        df, dgpo = _rms_bwd(fh, rf, gpo, dout)
        dgpo_ref[...] += dgpo
        dfb = df.astype(BF16)
        df_ref[...] = dfb
        dh2 = jnp.zeros((TM_FFN, D_MODEL), F32)
        for c0, c1 in FF_CHUNKS:
            da = _dot_nt(dfb, wd_ref[c0:c1, :])
            g = g_scr[:, c0:c1]
            up = up_scr[:, c0:c1]
            sg = _sigmoid(g)
            dup = (da * (g * sg)).astype(BF16)
            dg = (da * up * (sg * (1.0 + g * (1.0 - sg)))).astype(BF16)
            dg_ref[:, c0:c1] = dg
            dup_ref[:, c0:c1] = dup
            dh2 = dh2 + _dot(dg, wg_ref[c0:c1, :]) + _dot(dup, wu_ref[c0:c1, :])
        dx, dgpf = _rms_bwd(xh, r, gpf, dh2)
        dgpf_ref[...] += dgpf
        dx1_ref[...] = dout + dx

    return _pcall(
        body, name="ffn_fwd_bwd", grid=(SEQ // TM_FFN,),
        in_specs=[_rows(D_MODEL, TM_FFN), _rows(D_MODEL, TM_FFN), _resident((FF, D_MODEL)),
                  _resident((FF, D_MODEL)), _resident((FF, D_MODEL)),
                  _resident((1, D_MODEL)), _resident((1, D_MODEL))],
        out_specs=[_rows(D_MODEL, TM_FFN), _rows(FF, TM_FFN), _rows(FF, TM_FFN), _rows(FF, TM_FFN),
                   _rows(D_MODEL, TM_FFN), _rows(D_MODEL, TM_FFN), _acc(D_MODEL), _acc(D_MODEL), _acc(D_MODEL)],
        out_shape=[_sds((SEQ, D_MODEL), BF16), _sds((SEQ, FF), BF16), _sds((SEQ, FF), BF16),
                   _sds((SEQ, FF), BF16), _sds((SEQ, D_MODEL), BF16), _sds((SEQ, D_MODEL), F32),
                   _sds((1, D_MODEL), F32), _sds((1, D_MODEL), F32), _sds((1, D_MODEL), F32)],
        scratch_shapes=[pltpu.VMEM((TM_FFN, FF), F32), pltpu.VMEM((TM_FFN, FF), F32)],
        args=(x1, target, w_gate, w_up, w_down, g_pre, g_post), comms=comms)


def _wgrad(a, b, a_spec, b_spec, out_block, name, comms=()):
    def body(a_ref, b_ref, o_ref):
        av = a_ref[0] if len(a_ref.shape) == 3 else a_ref[...]
        bv = b_ref[0] if len(b_ref.shape) == 3 else b_ref[...]
        o_ref[0] = _dot_tn(av, bv)

    return _pcall(
        body, name=name, grid=(N_SHARD,),
        in_specs=[a_spec, b_spec],
        out_specs=pl.BlockSpec((1,) + out_block, lambda s: (s, 0, 0)),
        out_shape=_sds((N_SHARD,) + out_block, F32),
        args=(a, b), comms=comms)


def _outproj_bwd(dx1, y, attn, sgu, w_out, g_post, g_attn, g_sgu, u, vs, lg, lb, w_sp, b_t, comms=(), after=()):
    sgu_bwd = _sgu_bwd_body()

    def body(dx1_ref, y_ref, attn_ref, sgu_ref, w_ref, gp_ref, ga_ref, gs_ref, u_ref, vs_ref, lg_ref, lb_ref,
             wsp_ref, bt_ref, dy_ref, du_ref, dvs_ref, dgp_ref, dga_ref, dgs_ref, dw_ref, db_ref, dlg_ref, dlb_ref,
             *rest):
        dattn_refs, delta_refs, (slabs_a, slabs_b, dsgu_ref, dbias_scr) = rest[0:3], rest[3:6], rest[6:]

        @pl.when(pl.program_id(0) == 0)
        def _():
            dgp_ref[...] = jnp.zeros_like(dgp_ref)
            dga_ref[...] = jnp.zeros_like(dga_ref)
            dgs_ref[...] = jnp.zeros_like(dgs_ref)

        yh, ry = _rms_stats(y_ref[...])
        dy, dgp = _rms_bwd(yh, ry, gp_ref[...], dx1_ref[...])
        dgp_ref[...] += dgp
        dyb = dy.astype(BF16)
        dy_ref[...] = dyb
        dmixed = jnp.concatenate([_dot_nt(dyb, w_ref[s]) for s in range(N_SHARD)], axis=1)
        attn = attn_ref[...]
        ah, ra = _rms_stats(attn)
        dattn, dga = _rms_bwd(ah, ra, ga_ref[...], dmixed[:, 0:512])
        dga_ref[...] += dga
        sh, rs = _rms_stats(sgu_ref[...])
        dsgu, dgs = _rms_bwd(sh, rs, gs_ref[...], dmixed[:, 512:1024])
        dgs_ref[...] += dgs
        dsgu_ref[...] = dsgu
        delta = _dot_select(dattn * attn, _head_sum())
        dattn_refs[0][...] = dattn.astype(BF16)
        delta_refs[0][...] = delta
        for i, dil in enumerate(DILATIONS[1:]):
            _store_view(dattn, dattn_refs[i + 1], slabs_a, dil)
            _store_view(delta, delta_refs[i + 1], slabs_b, dil)
        sgu_bwd(u_ref, vs_ref, dsgu_ref, lg_ref, lb_ref, wsp_ref, bt_ref,
                du_ref, dvs_ref, dw_ref, db_ref, dlg_ref, dlb_ref, dbias_scr)

    return _pcall(
        body, name="outproj_sgu_bwd", grid=(SEQ // TM,),
        in_specs=[_rows(D_MODEL), _rows(D_MODEL), _rows(512), _rows(512), _resident((N_SHARD, OUT_S, D_MODEL)),
                  _resident((1, D_MODEL)), _resident((1, 512)), _resident((1, 512)),
                  _rows(SGU_W), _rows(SGU_W), _resident((1, SGU_W)), _resident((1, SGU_W)),
                  _resident((SGU_GROUPS, CHUNK, CHUNK)), _resident((CHUNK, SGU_GROUPS))],
        out_specs=[_rows(D_MODEL), _rows(SGU_W), _rows(SGU_W), _acc(D_MODEL), _acc(512), _acc(512),
                   pl.BlockSpec((SGU_GROUPS, CHUNK, CHUNK), lambda i: (0, 0, 0)), _acc(CHUNK, SGU_GROUPS),
                   _acc(SGU_W), _acc(SGU_W)]
        + [_view_rows(dil) for dil in DILATIONS] + [_view_rows(dil, HEAD_W) for dil in DILATIONS],
        out_shape=[_sds((SEQ, D_MODEL), BF16), _sds((SEQ, SGU_W), BF16), _sds((SEQ, SGU_W), BF16),
                   _sds((1, D_MODEL), F32), _sds((1, 512), F32), _sds((1, 512), F32),
                   _sds((SGU_GROUPS, CHUNK, CHUNK), F32), _sds((SGU_GROUPS, CHUNK), F32),
                   _sds((1, SGU_W), F32), _sds((1, SGU_W), F32)]
        + [_view_shape(dil, BF16) for dil in DILATIONS] + [_view_shape(dil, F32, HEAD_W) for dil in DILATIONS],
        scratch_shapes=[_slab_scratch(), _slab_scratch(), pltpu.VMEM((TM, SGU_W), F32),
                        pltpu.VMEM((CHUNK, SGU_W), F32)],
        args=(dx1, y, attn, sgu, w_out, g_post, g_attn, g_sgu, u, vs, lg, lb, w_sp, b_t), comms=comms, after=after)


def _sgu_bwd_body():
    nsteps = SEQ // TM

    def body(u_ref, vs_ref, ds_ref, lg_ref, lb_ref, w_ref, bt_ref,
             du_ref, dvs_ref, dw_ref, db_ref, dlg_ref, dlb_ref, dbias_scr):
        i = pl.program_id(0)

        @pl.when(i == 0)
        def _():
            dw_ref[...] = jnp.zeros_like(dw_ref)
            dlg_ref[...] = jnp.zeros_like(dlg_ref)
            dlb_ref[...] = jnp.zeros_like(dlb_ref)
            dbias_scr[...] = jnp.zeros_like(dbias_scr)

        wm = _masked_spatial(w_ref)
        ones_g = _group_ones()
        bias_full = _dot_exact(bt_ref[...], ones_g)
        u = u_ref[...]
        vs = vs_ref[...]
        lg = lg_ref[...]
        gu, xh, rstd, vnb, mixed, cdf_u, cdf_vs = _sgu_core(u, vs, lg, lb_ref[...], wm, bias_full)
        dsgu = ds_ref[...]
        du_ref[...] = (dsgu * mixed * _gelu_grad(u, cdf_u)).astype(BF16)
        dmixed = dsgu * gu
        left = _left_half()
        dvn_rows = []
        for c in range(TM // CHUNK):
            rs = slice(c * CHUNK, (c + 1) * CHUNK)
            dm_c = dmixed[rs, :]
            dbias_scr[...] += dm_c
            pieces = []
            for p in range(4):
                ls = slice(p * 128, (p + 1) * 128)
                dmp = dm_c[:, ls]
                vp = vnb[rs, ls]
                dmb = dmp.astype(BF16)
                zero = jnp.zeros_like(dmb)
                dw_ref[2 * p] += _dot_nt(jnp.where(left, dmb, zero), vp)
                dw_ref[2 * p + 1] += _dot_nt(jnp.where(left, zero, dmb), vp)
                pieces.append(jnp.where(left, _dot_tn(wm[2 * p], dmb), _dot_tn(wm[2 * p + 1], dmb)))
            dvn_rows.append(jnp.concatenate(pieces, axis=1))
        dvn = jnp.concatenate(dvn_rows, axis=0)
        dlg_ref[...] += jnp.sum(dvn * xh, axis=0, keepdims=True)
        dlb_ref[...] += jnp.sum(dvn, axis=0, keepdims=True)
        dxh = dvn * lg
        dgv = rstd * (dxh - jnp.mean(dxh, axis=-1, keepdims=True) - xh * jnp.mean(dxh * xh, axis=-1, keepdims=True))
        dvs_ref[...] = (dgv * _gelu_grad(vs, cdf_vs)).astype(BF16)

        @pl.when(i == nsteps - 1)
        def _():
            row = lax.broadcasted_iota(jnp.int32, (CHUNK, CHUNK), 0)
            col = lax.broadcasted_iota(jnp.int32, (CHUNK, CHUNK), 1)
            for g in range(SGU_GROUPS):
                dw_ref[g] = jnp.where(col <= row, dw_ref[g], 0.0)
            db_ref[...] = lax.dot_general(ones_g, dbias_scr[...], (((1,), (1,)), ((), ())),
                                          preferred_element_type=F32, precision=lax.Precision.HIGHEST)

    return body


def _attn_bwd_v1(qv, kv, vv, dov, deltav, lsev, dil, comms=(), after=()):
    seg = SEQ // dil
    nblk = seg // CHUNK
    rps = 4 if nblk == 1 else 1

    def body(q_ref, k_ref, v_ref, do_ref, dl_ref, lse_ref, dq_ref, dk_ref, dv_ref, dk_wait, dv_wait):
        left = _left_half()
        m_cur, m_prev = _block_masks()

        sides = tuple(enumerate((left, ~left)))
        zero = jnp.zeros((CHUNK, CHUNK), BF16)

        def rows(b):
            if isinstance(b, int):
                return b * CHUNK, max(b - 1, 0) * CHUNK
            return pl.multiple_of(b * CHUNK, CHUNK), pl.multiple_of(jnp.maximum(b - 1, 0) * CHUNK, CHUNK)

        def first(rr, b):
            r0, rp = rows(b)
            tiles, firsts = [], []
            for hp in range(4):
                ls = slice(rr * ATTN_W + hp * 128, rr * ATTN_W + (hp + 1) * 128)
                qp = q_ref[pl.ds(r0, CHUNK), ls]
                kc = k_ref[pl.ds(r0, CHUNK), ls]
                vc = v_ref[pl.ds(r0, CHUNK), ls]
                dop = do_ref[pl.ds(r0, CHUNK), ls]
                kp = k_ref[pl.ds(rp, CHUNK), ls] if nblk > 1 else None
                vp = v_ref[pl.ds(rp, CHUNK), ls] if nblk > 1 else None
                tiles.append((ls, kc, kp))
                for side, hm in sides:
                    qh = jnp.where(hm, qp, zero)
                    doh = jnp.where(hm, dop, zero)
                    cur = (_dot_nt(kc, qh), _dot_nt(vc, doh))
                    prev = (_dot_nt(kp, qh), _dot_nt(vp, doh)) if nblk > 1 else None
                    firsts.append((qh, doh, cur, prev))
            return tiles, firsts

        def second(rr, b, firsts):
            r0, _ = rows(b)
            own_ok, prev_ok = m_prev, m_cur & (b > 0)
            lanes = slice(rr * HEAD_W, (rr + 1) * HEAD_W)
            lse_t = lse_ref[pl.ds(r0, CHUNK), lanes].T
            dl_t = dl_ref[pl.ds(r0, CHUNK), lanes].T
            seconds = []
            for i, (qh, doh, cur, prev) in enumerate(firsts):
                lse_h = lse_t[16 * i:16 * i + 1, :]
                dl_h = dl_t[16 * i:16 * i + 1, :]
                pc = jnp.exp(jnp.where(own_ok, cur[0] - lse_h, NEG))
                out = [pc.astype(BF16), (pc * (cur[1] - dl_h)).astype(BF16), None, None]
                if nblk > 1:
                    pp = jnp.exp(jnp.where(prev_ok, prev[0] - lse_h, NEG))
                    out[2:] = [pp.astype(BF16), (pp * (prev[1] - dl_h)).astype(BF16)]
                seconds.append(out)
            return seconds

        def third(rr, b, tiles, firsts, seconds):
            r0, rp = rows(b)
            for hp, (ls, kc, kp) in enumerate(tiles):
                dq = jnp.zeros((CHUNK, CHUNK), F32)
                dkc = jnp.zeros((CHUNK, CHUNK), F32)
                dvc = jnp.zeros((CHUNK, CHUNK), F32)
                dkp = jnp.zeros((CHUNK, CHUNK), F32)
                dvp = jnp.zeros((CHUNK, CHUNK), F32)
                for side, hm in sides:
                    qh, doh, _, _ = firsts[2 * hp + side]
                    pcb, dsc, ppb, dsp = seconds[2 * hp + side]
                    dq = dq + _dot_tn(dsc, jnp.where(hm, kc, zero))
                    dkc = dkc + _dot(dsc, qh)
                    dvc = dvc + _dot(pcb, doh)
                    if nblk > 1:
                        dq = dq + _dot_tn(dsp, jnp.where(hm, kp, zero))
                        dkp = dkp + _dot(dsp, qh)
                        dvp = dvp + _dot(ppb, doh)
                dq_ref[pl.ds(r0, CHUNK), ls] = dq.astype(dq_ref.dtype)
                if nblk == 1:
                    dk_ref[pl.ds(r0, CHUNK), ls] = dkc.astype(dk_ref.dtype)
                    dv_ref[pl.ds(r0, CHUNK), ls] = dvc.astype(dv_ref.dtype)
                else:
                    @pl.when(b > 0)
                    def _():
                        dk_ref[pl.ds(rp, CHUNK), ls] = (dk_wait[:, ls] + dkp).astype(dk_ref.dtype)
                        dv_ref[pl.ds(rp, CHUNK), ls] = (dv_wait[:, ls] + dvp).astype(dv_ref.dtype)

                    dk_wait[:, ls] = dkc
                    dv_wait[:, ls] = dvc

        def run(units):
            data = [first(rr, b) for rr, b in units]
            probs = [second(rr, b, firsts) for (rr, b), (_, firsts) in zip(units, data)]
            for (rr, b), (tiles, firsts), seconds in zip(units, data, probs):
                third(rr, b, tiles, firsts, seconds)

        if nblk == 1:
            run([(rr, 0) for rr in range(rps)])
        else:
            def one(b, carry):
                run([(0, b)])
                return carry

            lax.fori_loop(0, nblk, one, 0)
            last = (nblk - 1) * CHUNK
            dk_ref[last:last + CHUNK, :] = dk_wait[...].astype(dk_ref.dtype)
            dv_ref[last:last + CHUNK, :] = dv_wait[...].astype(dv_ref.dtype)

    spec = pl.BlockSpec((seg, rps * ATTN_W), lambda r: (0, r))
    return _pcall(
        body, name=f"attn_bwd_d{dil}", grid=(dil // rps,),
        in_specs=[spec] * 4 + [pl.BlockSpec((seg, rps * HEAD_W), lambda r: (0, r))] * 2, out_specs=[spec] * 3,
        out_shape=[_sds((seg, dil * ATTN_W), BF16)] * 3,
        scratch_shapes=[pltpu.VMEM((CHUNK, ATTN_W), F32), pltpu.VMEM((CHUNK, ATTN_W), F32)],
        args=(qv, kv, vv, dov, deltav, lsev), comms=comms, after=after)


def _attn_bwd(qv, kv, vv, dov, deltav, lsev, dil, comms=(), after=()):
    seg = SEQ // dil
    nblk = seg // CHUNK
    rps = 4 if nblk == 1 else 1

    def body(q_ref, k_ref, v_ref, do_ref, dl_ref, lse_ref, dq_ref, dk_ref, dv_ref, dk_wait, dv_wait):
        left = _left_half()
        m_cur, m_prev = _block_masks()
        sides = tuple(enumerate((left, ~left)))
        zero = jnp.zeros((CHUNK, CHUNK), BF16)

        def first(rr, b, both):
            r0, rp = _attn_rows(b)
            keys = pl.ds(rp, 2 * CHUNK) if both else pl.ds(r0, CHUNK)
            tiles, heads = [], []
            for hp in range(4):
                ls = slice(rr * ATTN_W + hp * 128, rr * ATTN_W + (hp + 1) * 128)
                qp = q_ref[pl.ds(r0, CHUNK), ls]
                dop = do_ref[pl.ds(r0, CHUNK), ls]
                k2 = k_ref[keys, ls]
                v2 = v_ref[keys, ls]
                tiles.append((ls, k2))
                for _, hm in sides:
                    qh = jnp.where(hm, qp, zero)
                    doh = jnp.where(hm, dop, zero)
                    heads.append((qh, doh, _dot_nt(k2, qh), _dot_nt(v2, doh)))
            return tiles, heads

        def second(rr, b, heads, both):
            r0, _ = _attn_rows(b)
            ok = jnp.concatenate([m_cur, m_prev], axis=0) if both else m_prev
            lanes = slice(rr * HEAD_W, (rr + 1) * HEAD_W)
            lse_t = lse_ref[pl.ds(r0, CHUNK), lanes].T
            dl_t = dl_ref[pl.ds(r0, CHUNK), lanes].T
            out = []
            for i, (_, _, s_t, dp_t) in enumerate(heads):
                p = jnp.exp(jnp.where(ok, s_t - lse_t[16 * i:16 * i + 1, :], NEG))
                out.append((p.astype(BF16), (p * (dp_t - dl_t[16 * i:16 * i + 1, :])).astype(BF16)))
            return out

        def third(rr, b, tiles, heads, probs, both):
            r0, rp = _attn_rows(b)
            nk = 2 * CHUNK if both else CHUNK
            left_k = _lane_left(nk)
            zero_k = jnp.zeros((nk, CHUNK), BF16)
            for hp, (ls, k2) in enumerate(tiles):
                dq = jnp.zeros((CHUNK, CHUNK), F32)
                dk2 = jnp.zeros((nk, CHUNK), F32)
                dv2 = jnp.zeros((nk, CHUNK), F32)
                for side in range(2):
                    qh, doh, _, _ = heads[2 * hp + side]
                    p, ds = probs[2 * hp + side]
                    dq = dq + _dot_tn(ds, jnp.where(left_k if side == 0 else ~left_k, k2, zero_k))
                    dk2 = dk2 + _dot(ds, qh)
                    dv2 = dv2 + _dot(p, doh)
                dq_ref[pl.ds(r0, CHUNK), ls] = dq.astype(dq_ref.dtype)
                if nblk == 1:
                    dk_ref[pl.ds(r0, CHUNK), ls] = dk2.astype(dk_ref.dtype)
                    dv_ref[pl.ds(r0, CHUNK), ls] = dv2.astype(dv_ref.dtype)
                elif both:
                    dk_ref[pl.ds(rp, CHUNK), ls] = (dk_wait[:, ls] + dk2[0:CHUNK]).astype(dk_ref.dtype)
                    dv_ref[pl.ds(rp, CHUNK), ls] = (dv_wait[:, ls] + dv2[0:CHUNK]).astype(dv_ref.dtype)
                    dk_wait[:, ls] = dk2[CHUNK:]
                    dv_wait[:, ls] = dv2[CHUNK:]
                else:
                    dk_wait[:, ls] = dk2
                    dv_wait[:, ls] = dv2

        def run(units, both):
            data = [first(rr, b, both) for rr, b in units]
            probs = [second(rr, b, heads, both) for (rr, b), (_, heads) in zip(units, data)]
            for (rr, b), (tiles, heads), pr in zip(units, data, probs):
                third(rr, b, tiles, heads, pr, both)

        run([(rr, 0) for rr in range(rps)], False)
        if nblk > 1:
            def one(b, carry):
                run([(0, b)], True)
                return carry

            lax.fori_loop(1, nblk, one, 0)
            last = (nblk - 1) * CHUNK
            dk_ref[last:last + CHUNK, :] = dk_wait[...].astype(dk_ref.dtype)
            dv_ref[last:last + CHUNK, :] = dv_wait[...].astype(dv_ref.dtype)

    spec = pl.BlockSpec((seg, rps * ATTN_W), lambda r: (0, r))
    return _pcall(
        body, name=f"attn_bwd_d{dil}", grid=(dil // rps,),
        in_specs=[spec] * 4 + [pl.BlockSpec((seg, rps * HEAD_W), lambda r: (0, r))] * 2, out_specs=[spec] * 3,
        out_shape=[_sds((seg, dil * ATTN_W), BF16)] * 3,
        scratch_shapes=[pltpu.VMEM((CHUNK, ATTN_W), F32), pltpu.VMEM((CHUNK, ATTN_W), F32)],
        args=(qv, kv, vv, dov, deltav, lsev), comms=comms, after=after)


def _inproj_bwd(dqs, dks, dvs, du, dvs_sgu, pos, x, dx1, w_in, g_pre, comms=()):
    def body(dq1, dq2, dq3, dk1, dk2, dk3, dv1, dv2, dv3, du_ref, dvs_ref, pos_ref, x_ref, dx1_ref, w_ref, g_ref,
             dproj_ref, gx_ref, dg_ref, slabs_a, slabs_b):
        @pl.when(pl.program_id(0) == 0)
        def _():
            dg_ref[...] = jnp.zeros_like(dg_ref)

        def total(r1, r2, r3):
            return r1[...] + _load_view(r2, slabs_a, DILATIONS[1]) + _load_view(r3, slabs_b, DILATIONS[2])

        tabs = _rot_tables(pos_ref[...])
        groups = {3: du_ref[...], 4: dvs_ref[...]}
        dh = jnp.zeros((TM, D_MODEL), F32)
        for g in (3, 4, 0, 1, 2):
            if g == 0:
                groups[g] = _rope_bwd(total(dq1, dq2, dq3) * np.float32(ATTN_SCALE), tabs).astype(BF16)
            elif g == 1:
                groups[g] = _rope_bwd(total(dk1, dk2, dk3), tabs).astype(BF16)
            elif g == 2:
                groups[g] = total(dv1, dv2, dv3).astype(BF16)
            dproj_ref[:, 512 * g:512 * (g + 1)] = groups[g]
            off = 0
            for s, a, b in _in_pieces(g):
                dh = dh + _dot_nt(groups[g][:, off:off + b - a], w_ref[s, :, a:b])
                off += b - a
        g = g_ref[...]
        xh, r = _rms_stats(x_ref[...])
        dx, dg = _rms_bwd(xh, r, g, dh)
        dg_ref[...] += dg
        gx_ref[...] = dx1_ref[...] + dx

    return _pcall(
        body, name="inproj_bwd", grid=(SEQ // TM,),
        in_specs=[_view_rows(dil) for dil in DILATIONS] * 3
        + [_rows(512), _rows(512), _rows(1), _rows(D_MODEL), _rows(D_MODEL),
           _resident((N_SHARD, D_MODEL, IN_S)), _resident((1, D_MODEL))],
        out_specs=[_rows(PROJ_W), _rows(D_MODEL), _acc(D_MODEL)],
        out_shape=[_sds((SEQ, PROJ_W), BF16), _sds((SEQ, D_MODEL), F32), _sds((1, D_MODEL), F32)],
        scratch_shapes=[_slab_scratch(), _slab_scratch()],
        args=(*dqs, *dks, *dvs, du, dvs_sgu, pos, x, dx1, w_in, g_pre), comms=comms)


def _to_view(a, dil):
    return a if dil == 1 else a.reshape(SEQ // dil, dil * a.shape[1])


def _from_view(a, dil):
    return a if dil == 1 else a.reshape(SEQ, a.shape[1] // dil)


def _local_step(x, pos, target, w_in, w_out, w_gate, w_up, w_down, small):
    b_t = small["sgu_b_spatial"].T
    h, u, vs, sgu, *qkv = _inproj_fwd(x, pos, small["pre_mix_norm"], w_in, small["sgu_ln_gain"],
                                      small["sgu_ln_bias"], small["sgu_w_spatial"], b_t)
    views = [tuple(qkv[3 * i:3 * i + 3]) for i in range(len(DILATIONS))]
    o_list, l_list = [], []
    for dil, (qv, kv, vv) in zip(DILATIONS, views):
        o, l = _attn_fwd(qv, kv, vv, dil)
        o_list.append(o)
        l_list.append(l)
    attn, mixed, y, x1, *lses = _mix_out_fwd(o_list, l_list, sgu, x, w_out, small["attn_out_norm"],
                                             small["sgu_out_norm"], small["post_mix_norm"])
    h2, a, dg, dup, df, dx1, loss_cols, d_pre_ffn, d_post_ffn = _ffn_fwd_bwd(
        x1, target, w_gate, w_up, w_down, small["pre_ffn_norm"], small["post_ffn_norm"])

    full_tok = pl.BlockSpec((SEQ, D_MODEL), lambda s: (0, 0), pipeline_mode=pl.Buffered(1))
    ff_tok = pl.BlockSpec((1, SEQ, FF_S), lambda s: (s, 0, 0))
    gw_gate, gw_up = _wgrad_ff([dg, dup], h2, "wgrad_gate_up")
    (gw_down,) = _wgrad_ff([a], df, "wgrad_down")

    (dy, du, dvs_sgu, d_post_mix, d_attn_norm, d_sgu_norm, d_w_sp, d_b_sp, d_ln_gain, d_ln_bias,
     *dviews) = _outproj_bwd(
        dx1, y, attn, sgu, w_out, small["post_mix_norm"], small["attn_out_norm"], small["sgu_out_norm"],
        u, vs, small["sgu_ln_gain"], small["sgu_ln_bias"], small["sgu_w_spatial"], b_t)
    gw_out = _wgrad(mixed, dy, pl.BlockSpec((SEQ, OUT_S), lambda s: (0, s)), full_tok, (OUT_S, D_MODEL), "wgrad_out")

    dqs, dks, dvs = [], [], []
    for i, (dil, (qv, kv, vv)) in enumerate(zip(DILATIONS, views)):
        dq, dk, dv = _attn_bwd(qv, kv, vv, dviews[i], dviews[3 + i], lses[i], dil)
        dqs.append(dq)
        dks.append(dk)
        dvs.append(dv)
    dproj, grad_x, d_pre_mix = _inproj_bwd(dqs, dks, dvs, du, dvs_sgu, pos, x, dx1, w_in, small["pre_mix_norm"])
    gw_in = _wgrad(h, dproj, full_tok, pl.BlockSpec((SEQ, IN_S), lambda s: (0, s)), (D_MODEL, IN_S), "wgrad_in")

    small_grads = {
        "pre_mix_norm": d_pre_mix, "sgu_ln_gain": d_ln_gain, "sgu_ln_bias": d_ln_bias, "sgu_w_spatial": d_w_sp,
        "sgu_b_spatial": d_b_sp, "attn_out_norm": d_attn_norm, "sgu_out_norm": d_sgu_norm,
        "post_mix_norm": d_post_mix, "pre_ffn_norm": d_pre_ffn, "post_ffn_norm": d_post_ffn,
    }
    return loss_cols, grad_x, (gw_in, gw_out, gw_gate, gw_up, gw_down), small_grads


def _coords():
    return lax.axis_index("x"), lax.axis_index("y"), lax.axis_index("c")


def _other_chips(x, y):
    return [(1 - x, y), (x, 1 - y), (1 - x, 1 - y)]


def _comm_call(body, name, n_in, out_shape, scratch_shapes):
    return pl.pallas_call(
        body, name=name, in_specs=[ANY] * n_in, out_specs=[ANY] * len(out_shape), out_shape=out_shape,
        scratch_shapes=scratch_shapes,
        compiler_params=pltpu.CompilerParams(has_side_effects=True),
    )


def _gather_weights(shards):
    n = len(shards)
    halves = [s.reshape(2, s.shape[0] // 2, s.shape[1]) for s in shards]

    def body(*refs):
        ins, outs = refs[:n], refs[n:2 * n]
        send_sems, recv_sems = refs[2 * n:]
        x, y, c = _coords()
        s_me = 2 * x + y
        chips = _other_chips(x, y)
        sibling = (x, y, 1 - c)

        def copy(k, w, shard, cc, to):
            src = ins[w].at[cc] if shard is None else outs[w].at[shard, cc]
            dst = outs[w].at[s_me if shard is None else shard, cc]
            return pltpu.make_async_remote_copy(src_ref=src, dst_ref=dst, send_sem=send_sems.at[k],
                                                recv_sem=recv_sems.at[k], device_id=to, device_id_type=MESH)

        first = [copy(j * n + w, w, None, c, (cx, cy, c)) for j, (cx, cy) in enumerate(chips) for w in range(n)]
        for cp in first:
            cp.start()
        passed = []
        for j, (cx, cy) in enumerate(chips):
            for w in range(n):
                copy(j * n + w, w, 2 * cx + cy, c, (x, y, c)).wait_recv()
                fw = copy((3 + j) * n + w, w, 2 * cx + cy, c, sibling)
                fw.start()
                passed.append(fw)
        for j, (cx, cy) in enumerate(chips):
            for w in range(n):
                copy((3 + j) * n + w, w, 2 * cx + cy, 1 - c, (x, y, c)).wait_recv()
        for cp in first + passed:
            cp.wait_send()

    out_shape = [_sds((N_SHARD,) + h.shape, h.dtype) for h in halves]
    scratch = [pltpu.SemaphoreType.DMA((6 * n,)), pltpu.SemaphoreType.DMA((6 * n,))]
    full = _comm_call(body, "comm_gather_weights", n, out_shape, scratch)(*halves)
    s_me = 2 * lax.axis_index("x") + lax.axis_index("y")
    full = [lax.dynamic_update_slice(f, h[None], (s_me, 0, 0, 0)) for f, h in zip(full, halves)]
    return [f.reshape((N_SHARD,) + s.shape) for f, s in zip(full, shards)]


def _rs_to_sibling(gws):
    n = len(gws)

    def body(*refs):
        ins, outs = refs[:n], refs[n:2 * n]
        send_sems, recv_sems = refs[2 * n:]
        x, y, c = _coords()
        copies = []
        for w in range(n):
            hw = gws[w].shape[1] // 2
            copies.append(pltpu.make_async_remote_copy(
                src_ref=ins[w].at[:, pl.ds((1 - c) * hw, hw), :], dst_ref=outs[w], send_sem=send_sems.at[w],
                recv_sem=recv_sems.at[w], device_id=(x, y, 1 - c), device_id_type=MESH))
        for cp in copies:
            cp.start()
        for cp in copies:
            cp.wait()

    out_shape = [_sds((N_SHARD, g.shape[1] // 2, g.shape[2]), g.dtype) for g in gws]
    scratch = [pltpu.SemaphoreType.DMA((n,)), pltpu.SemaphoreType.DMA((n,))]
    return _comm_call(body, "comm_rs_sibling", n, out_shape, scratch)(*gws)


def _rs_chip_sum(gw, recv, core):
    _, rows, cols = gw.shape
    hw = rows // 2

    def body(c_ref, g_ref, r_ref, o_ref):
        o_ref[...] = (g_ref[...] + r_ref[...]).astype(BF16)

    return pl.pallas_call(
        body, name="rs_chip_sum",
        grid_spec=pltpu.PrefetchScalarGridSpec(
            num_scalar_prefetch=1, grid=(N_SHARD,),
            in_specs=[pl.BlockSpec((1, hw, cols), lambda s, c_ref: (s, c_ref[0], 0)),
                      pl.BlockSpec((1, hw, cols), lambda s, c_ref: (s, 0, 0))],
            out_specs=pl.BlockSpec((1, hw, cols), lambda s, c_ref: (s, 0, 0))),
        out_shape=_sds((N_SHARD, hw, cols), BF16),
        compiler_params=_seq_params(),
    )(core, gw, recv)


def _rs_between_chips(pbs):
    n = len(pbs)

    def body(*refs):
        ins, outs = refs[:n], refs[n:2 * n]
        send_sems, recv_sems = refs[2 * n:]
        x, y, c = _coords()
        copies = []
        for j, (cx, cy) in enumerate(_other_chips(x, y)):
            for w in range(n):
                copies.append(pltpu.make_async_remote_copy(
                    src_ref=ins[w].at[2 * cx + cy], dst_ref=outs[w].at[j], send_sem=send_sems.at[j * n + w],
                    recv_sem=recv_sems.at[j * n + w], device_id=(cx, cy, c), device_id_type=MESH))
        for cp in copies:
            cp.start()
        for cp in copies:
            cp.wait()

    out_shape = [_sds((3,) + p.shape[1:], p.dtype) for p in pbs]
    scratch = [pltpu.SemaphoreType.DMA((3 * n,)), pltpu.SemaphoreType.DMA((3 * n,))]
    return _comm_call(body, "comm_rs_chips", n, out_shape, scratch)(*pbs)


def _rs_final_sum(gw, recv_sib, recv_chips, shard_core):
    _, rows, cols = gw.shape
    hw = rows // 2

    def body(sc_ref, g_ref, r_ref, rc_ref, o_ref):
        acc = g_ref[0] + r_ref[0]
        for j in range(3):
            acc = acc + rc_ref[j].astype(F32)
        o_ref[0] = acc

    return pl.pallas_call(
        body, name="rs_final_sum",
        grid_spec=pltpu.PrefetchScalarGridSpec(
            num_scalar_prefetch=1, grid=(1,),
            in_specs=[pl.BlockSpec((1, hw, cols), lambda i, sc: (sc[0], sc[1], 0)),
                      pl.BlockSpec((1, hw, cols), lambda i, sc: (sc[0], 0, 0)),
                      pl.BlockSpec((3, hw, cols), lambda i, sc: (0, 0, 0))],
            out_specs=pl.BlockSpec((1, hw, cols), lambda i, sc: (sc[1], 0, 0))),
        out_shape=_sds((2, hw, cols), F32),
        compiler_params=_seq_params(),
    )(shard_core, gw, recv_sib, recv_chips)


def _rs_join_halves(halves):
    n = len(halves)

    def body(*refs):
        bufs = refs[n:2 * n]
        send_sems, recv_sems = refs[2 * n:]
        x, y, c = _coords()
        remote = [pltpu.make_async_remote_copy(
            src_ref=bufs[w].at[c], dst_ref=bufs[w].at[c], send_sem=send_sems.at[w], recv_sem=recv_sems.at[w],
            device_id=(x, y, 1 - c), device_id_type=MESH) for w in range(n)]
        for cp in remote:
            cp.start()
        for w in range(n):
            remote[w].wait_send()
            pltpu.make_async_remote_copy(
                src_ref=bufs[w].at[c], dst_ref=bufs[w].at[1 - c], send_sem=send_sems.at[w],
                recv_sem=recv_sems.at[w], device_id=(x, y, c), device_id_type=MESH).wait_recv()

    joined = pl.pallas_call(
        body, name="comm_rs_join", in_specs=[ANY] * n, out_specs=[ANY] * n,
        out_shape=[_sds(h.shape, h.dtype) for h in halves], input_output_aliases={w: w for w in range(n)},
        scratch_shapes=[pltpu.SemaphoreType.DMA((n,)), pltpu.SemaphoreType.DMA((n,))],
        compiler_params=pltpu.CompilerParams(has_side_effects=True),
    )(*halves)
    return [j.reshape(2 * h.shape[1], h.shape[2]) for j, h in zip(joined, halves)]


def _allreduce_small(buf):
    rows, cols = buf.shape

    def body(in_ref, out_ref, slots, send_sems, recv_sems):
        x, y, c = _coords()
        me = 4 * x + 2 * y + c
        copies, peers = [], []
        for k in range(1, 8):
            px = 1 - x if (k >> 2) & 1 else x
            py = 1 - y if (k >> 1) & 1 else y
            pc = 1 - c if k & 1 else c
            peers.append(4 * px + 2 * py + pc)
            copies.append(pltpu.make_async_remote_copy(
                src_ref=in_ref, dst_ref=slots.at[me], send_sem=send_sems.at[k - 1], recv_sem=recv_sems.at[k - 1],
                device_id=(px, py, pc), device_id_type=MESH))
        for cp in copies:
            cp.start()
        slots[me] = in_ref[...]
        for k in range(7):
            pltpu.make_async_remote_copy(
                src_ref=in_ref, dst_ref=slots.at[peers[k]], send_sem=send_sems.at[k], recv_sem=recv_sems.at[k],
                device_id=(x, y, c), device_id_type=MESH).wait_recv()
        for cp in copies:
            cp.wait_send()
        acc = slots[0]
        for i in range(1, 8):
            acc = acc + slots[i]
        out_ref[...] = acc

    vmem = pl.BlockSpec(memory_space=pltpu.VMEM)
    return pl.pallas_call(
        body, name="comm_allreduce_small", in_specs=[vmem], out_specs=vmem, out_shape=_sds((rows, cols), F32),
        scratch_shapes=[pltpu.VMEM((8, rows, cols), F32), pltpu.SemaphoreType.DMA((7,)), pltpu.SemaphoreType.DMA((7,))],
        compiler_params=pltpu.CompilerParams(has_side_effects=True, vmem_limit_bytes=VMEM_LIMIT),
    )(buf)


def _adamw(w, g, m, v, block_rows, name, after=()):
    rows, cols = w.shape

    def body(w_ref, g_ref, m_ref, v_ref, *rest):
        d_ref, nm_ref, nv_ref = rest[len(after):]
        g = g_ref[...]
        m = ADAM_B1 * m_ref[...] + (1.0 - ADAM_B1) * g
        v = ADAM_B2 * v_ref[...] + (1.0 - ADAM_B2) * (g * g)
        m_hat = m / (1.0 - ADAM_B1 ** ADAM_STEP)
        v_hat = v / (1.0 - ADAM_B2 ** ADAM_STEP)
        d_ref[...] = -ADAM_LR * (m_hat / (jnp.sqrt(v_hat) + ADAM_EPS) + ADAM_WD * w_ref[...])
        nm_ref[...] = m
        nv_ref[...] = v

    spec = pl.BlockSpec((block_rows, cols), lambda i: (i, 0))
    return pl.pallas_call(
        body, name=name, grid=(rows // block_rows,), in_specs=[spec] * 4 + [ANY] * len(after), out_specs=[spec] * 3,
        out_shape=[_sds((rows, cols), F32)] * 3,
        compiler_params=_seq_params(),
    )(w, g, m, v, *after)


WEIGHTS = ("pre_mix_norm", "w_in", "sgu_ln_gain", "sgu_ln_bias", "sgu_w_spatial", "sgu_b_spatial", "attn_out_norm",
           "sgu_out_norm", "w_out", "post_mix_norm", "pre_ffn_norm", "w_gate", "w_up", "w_down", "post_ffn_norm")
BIG = ("w_in", "w_out", "w_gate", "w_up", "w_down")
BIG_ADAM_ROWS = {"w_in": 256, "w_out": 128, "w_gate": 352, "w_up": 352, "w_down": 352}
SMALL = ("pre_mix_norm", "post_mix_norm", "pre_ffn_norm", "post_ffn_norm", "sgu_ln_gain", "sgu_ln_bias",
         "attn_out_norm", "sgu_out_norm", "sgu_w_spatial", "sgu_b_spatial")


def _pack_small(d):
    flat = [d[n].reshape(-1) for n in SMALL]
    used = sum(f.shape[0] for f in flat)
    flat.append(jnp.zeros((SMALL_ROWS * 1024 - used,), F32))
    return jnp.concatenate(flat).reshape(SMALL_ROWS, 1024)


def _unpack_small(buf, shapes):
    flat = buf.reshape(-1)
    out, off = {}, 0
    for n in SMALL:
        size = int(np.prod(shapes[n]))
        out[n] = flat[off:off + size].reshape(shapes[n])
        off += size
    return out


def _kernel_unoverlapped(x, positions, pre_mix_norm, w_in, sgu_ln_gain, sgu_ln_bias, sgu_w_spatial, sgu_b_spatial, attn_out_norm, sgu_out_norm, w_out, post_mix_norm, pre_ffn_norm, w_gate, w_up, w_down, post_ffn_norm, loss_target, m_pre_mix_norm, m_w_in, m_sgu_ln_gain, m_sgu_ln_bias, m_sgu_w_spatial, m_sgu_b_spatial, m_attn_out_norm, m_sgu_out_norm, m_w_out, m_post_mix_norm, m_pre_ffn_norm, m_w_gate, m_w_up, m_w_down, m_post_ffn_norm, v_pre_mix_norm, v_w_in, v_sgu_ln_gain, v_sgu_ln_bias, v_sgu_w_spatial, v_sgu_b_spatial, v_attn_out_norm, v_sgu_out_norm, v_w_out, v_post_mix_norm, v_pre_ffn_norm, v_w_gate, v_w_up, v_w_down, v_post_ffn_norm):
    a = dict(locals())
    cx, cy, cc = _coords()
    core = jnp.stack([cc]).astype(jnp.int32)
    shard_core = jnp.stack([2 * cx + cy, cc]).astype(jnp.int32)

    full = _gather_weights([a[n][0].astype(BF16) for n in BIG])
    small = {n: (a[n][0] if a[n].ndim > 2 else a[n]) for n in SMALL}
    loss_cols, grad_x, gws, small_grads = _local_step(
        x[0], positions.reshape(SEQ, 1), loss_target[0], *full, small)
    loss = lax.psum(jnp.sum(loss_cols) * np.float32(0.5 / D_MODEL), ("x", "y", "c"))

    recv_sib = _rs_to_sibling(list(gws))
    chip_part = [_rs_chip_sum(g, r, core) for g, r in zip(gws, recv_sib)]
    recv_chips = _rs_between_chips(chip_part)
    halves = [_rs_final_sum(g, r, rc, shard_core) for g, r, rc in zip(gws, recv_sib, recv_chips)]
    big_grads = dict(zip(BIG, _rs_join_halves(halves)))

    shapes = {n: a[n].shape for n in SMALL}
    small_sum = _allreduce_small(_pack_small(small_grads))

    grads, deltas, new_m, new_v = {}, {}, {}, {}
    for n in BIG:
        grads[n] = big_grads[n][None]
        d, nm, nv = _adamw(a[n][0], big_grads[n], a["m_" + n][0], a["v_" + n][0], BIG_ADAM_ROWS[n], "adamw_" + n)
        deltas[n], new_m[n], new_v[n] = d[None], nm[None], nv[None]
    d, nm, nv = _adamw(_pack_small({n: a[n] for n in SMALL}), small_sum, _pack_small({n: a["m_" + n] for n in SMALL}),
                       _pack_small({n: a["v_" + n] for n in SMALL}), SMALL_ROWS, "adamw_small")
    grads.update(_unpack_small(small_sum, shapes))
    deltas.update(_unpack_small(d, shapes))
    new_m.update(_unpack_small(nm, shapes))
    new_v.update(_unpack_small(nv, shapes))
    return (loss, grad_x[None], *[grads[n] for n in WEIGHTS], *[deltas[n] for n in WEIGHTS],
            *[new_m[n] for n in WEIGHTS], *[new_v[n] for n in WEIGHTS])


def _remote(src, dst, send_sem, recv_sem, to):
    return pltpu.make_async_remote_copy(src_ref=src, dst_ref=dst, send_sem=send_sem, recv_sem=recv_sem,
                                        device_id=to, device_id_type=MESH)


def _halves(a):
    *lead, rows, cols = a.shape
    return a.reshape(*lead, 2, rows // 2, cols)


def _gather_ici(shards):
    n = len(shards)

    def desc(ins, outs, ss, rs, j, w, landed):
        x, y, c = _coords()
        cx, cy = _other_chips(x, y)[j]
        shard = 2 * cx + cy if landed else 2 * x + y
        return _remote(ins[w].at[c], outs[w].at[shard, c], ss.at[j * n + w], rs.at[j * n + w], (cx, cy, c))

    def start(ins, outs, ss, rs):
        for j in range(3):
            for w in range(n):
                desc(ins, outs, ss, rs, j, w, False).start()

    def finish(ins, outs, ss, rs):
        for j in range(3):
            for w in range(n):
                desc(ins, outs, ss, rs, j, w, True).wait_recv()
                desc(ins, outs, ss, rs, j, w, False).wait_send()

    return _Comm(shards, [_sds((N_SHARD,) + s.shape, s.dtype) for s in shards], 3 * n, start, finish)


def _gather_pass(fulls):
    n = len(fulls)

    def desc(bufs, ss, rs, j, w, landed):
        x, y, c = _coords()
        cx, cy = _other_chips(x, y)[j]
        shard = 2 * cx + cy
        return _remote(bufs[w].at[shard, c], bufs[w].at[shard, 1 - c if landed else c],
                       ss.at[j * n + w], rs.at[j * n + w], (x, y, 1 - c))

    def start(ins, outs, ss, rs):
        for j in range(3):
            for w in range(n):
                desc(outs, ss, rs, j, w, False).start()

    def finish(ins, outs, ss, rs):
        for j in range(3):
            for w in range(n):
                desc(outs, ss, rs, j, w, True).wait_recv()
                desc(outs, ss, rs, j, w, False).wait_send()

    return _Comm(fulls, [_sds(f.shape, f.dtype) for f in fulls], 3 * n, start, finish, aliased=True)


def _rs_sibling(gws):
    n = len(gws)

    def desc(ins, outs, ss, rs, w):
        x, y, c = _coords()
        return _remote(ins[w].at[:, 1 - c], outs[w], ss.at[w], rs.at[w], (x, y, 1 - c))

    def start(ins, outs, ss, rs):
        for w in range(n):
            desc(ins, outs, ss, rs, w).start()

    def finish(ins, outs, ss, rs):
        for w in range(n):
            desc(ins, outs, ss, rs, w).wait()

    out_shape = [_sds((N_SHARD, g.shape[1] // 2, g.shape[2]), g.dtype) for g in gws]
    return _Comm([_halves(g) for g in gws], out_shape, n, start, finish)


def _rs_chips(pbs):
    n = len(pbs)

    def desc(ins, outs, ss, rs, j, w):
        x, y, c = _coords()
        cx, cy = _other_chips(x, y)[j]
        return _remote(ins[w].at[2 * cx + cy], outs[w].at[j], ss.at[j * n + w], rs.at[j * n + w], (cx, cy, c))

    def start(ins, outs, ss, rs):
        for j in range(3):
            for w in range(n):
                desc(ins, outs, ss, rs, j, w).start()

    def finish(ins, outs, ss, rs):
        for j in range(3):
            for w in range(n):
                desc(ins, outs, ss, rs, j, w).wait()

    return _Comm(pbs, [_sds((3,) + p.shape[1:], p.dtype) for p in pbs], 3 * n, start, finish)


def _rs_join(halves):
    n = len(halves)

    def desc(bufs, ss, rs, w, landed):
        x, y, c = _coords()
        return _remote(bufs[w].at[c], bufs[w].at[1 - c if landed else c], ss.at[w], rs.at[w], (x, y, 1 - c))

    def start(ins, outs, ss, rs):
        for w in range(n):
            desc(outs, ss, rs, w, False).start()

    def finish(ins, outs, ss, rs):
        for w in range(n):
            desc(outs, ss, rs, w, True).wait_recv()
            desc(outs, ss, rs, w, False).wait_send()

    return _Comm(halves, [_sds(h.shape, h.dtype) for h in halves], n, start, finish, aliased=True)


def _small_exchange(buf):
    def desc(ins, outs, ss, rs, k, landed):
        x, y, c = _coords()
        px = 1 - x if (k >> 2) & 1 else x
        py = 1 - y if (k >> 1) & 1 else y
        pc = 1 - c if k & 1 else c
        slot = 4 * px + 2 * py + pc if landed else 4 * x + 2 * y + c
        return _remote(ins[0], outs[0].at[slot], ss.at[k - 1], rs.at[k - 1], (px, py, pc))

    def start(ins, outs, ss, rs):
        for k in range(1, 8):
            desc(ins, outs, ss, rs, k, False).start()

    def finish(ins, outs, ss, rs):
        for k in range(1, 8):
            desc(ins, outs, ss, rs, k, True).wait_recv()
            desc(ins, outs, ss, rs, k, False).wait_send()

    return _Comm([buf], [_sds((8,) + buf.shape, buf.dtype)], 7, start, finish)


HBM = pl.BlockSpec(memory_space=pltpu.HBM)
SEM = pl.BlockSpec(memory_space=pltpu.SEMAPHORE)
DATAFLOW = pltpu.SideEffectType.DATAFLOW_SIDE_EFFECTING


def _split_start(name, comm, lands, after):
    srcs = [pltpu.with_memory_space_constraint(s, pltpu.HBM) for s in comm.args]
    lands = [pltpu.with_memory_space_constraint(b, pltpu.HBM) for b in lands]
    ns, nb = len(srcs), len(lands)

    def body(*refs):
        send_sems, recv_sems = refs[ns + nb + 1], refs[ns + nb + 2]
        comm.start(refs[:ns], refs[ns:ns + nb], send_sems, recv_sems)
        refs[-1][...] = jnp.zeros_like(refs[-1])

    res = pl.pallas_call(
        body, name=name,
        out_shape=(pltpu.SemaphoreType.DMA((comm.n_sems,)), pltpu.SemaphoreType.DMA((comm.n_sems,)),
                   *[pltpu.HBM(b.shape, b.dtype) for b in srcs + lands], _sds((8, 128), F32)),
        in_specs=[HBM] * (ns + nb) + [ANY],
        out_specs=(SEM, SEM, *[HBM] * (ns + nb), pl.BlockSpec(memory_space=pltpu.VMEM)),
        input_output_aliases={i: 2 + i for i in range(ns + nb)},
        compiler_params=pltpu.CompilerParams(has_side_effects=DATAFLOW),
    )(*srcs, *lands, after)
    return res[0], res[1], list(res[2:2 + ns]), list(res[2 + ns:2 + ns + nb]), res[-1]


def _split_starts(name, comms, after):
    srcs = [[pltpu.with_memory_space_constraint(s, pltpu.HBM) for s in c.args] for c in comms]
    lands = [[pltpu.with_memory_space_constraint(lax.empty(o.shape, o.dtype), pltpu.HBM) for o in c.out_shape]
             for c in comms]
    bufs = [b for k in range(len(comms)) for b in srcs[k] + lands[k]]
    nb, nc = len(bufs), len(comms)

    def body(*refs):
        sems = refs[nb + 1:nb + 1 + 2 * nc]
        off = 0
        for k, c in enumerate(comms):
            ns, nl = len(srcs[k]), len(lands[k])
            c.start(refs[off:off + ns], refs[off + ns:off + ns + nl], sems[2 * k], sems[2 * k + 1])
            off += ns + nl
        refs[-1][...] = jnp.zeros_like(refs[-1])

    res = pl.pallas_call(
        body, name=name,
        out_shape=(*[pltpu.SemaphoreType.DMA((c.n_sems,)) for c in comms for _ in range(2)],
                   *[pltpu.HBM(b.shape, b.dtype) for b in bufs], _sds((8, 128), F32)),
        in_specs=[HBM] * nb + [ANY],
        out_specs=(*[SEM] * (2 * nc), *[HBM] * nb, pl.BlockSpec(memory_space=pltpu.VMEM)),
        input_output_aliases={i: 2 * nc + i for i in range(nb)},
        compiler_params=pltpu.CompilerParams(has_side_effects=DATAFLOW),
    )(*bufs, after)
    states, off = [], 2 * nc
    for k in range(nc):
        ns, nl = len(srcs[k]), len(lands[k])
        states.append((res[2 * k], res[2 * k + 1], list(res[off:off + ns]), list(res[off + ns:off + ns + nl])))
        off += ns + nl
    return states, res[-1]


def _split_wait(name, comm, send_sems, recv_sems, srcs, lands, after):
    ns, nb = len(srcs), len(lands)
    after = list(after) if isinstance(after, (list, tuple)) else [after]

    def body(*refs):
        comm.finish(refs[:ns], refs[ns:ns + nb], refs[ns + nb], refs[ns + nb + 1])

    res = pl.pallas_call(
        body, name=name,
        out_shape=tuple(pltpu.HBM(b.shape, b.dtype) for b in srcs + lands),
        in_specs=[HBM] * (ns + nb) + [SEM, SEM] + [ANY] * len(after), out_specs=tuple([HBM] * (ns + nb)),
        input_output_aliases={i: i for i in range(ns + nb)},
        compiler_params=pltpu.CompilerParams(has_side_effects=DATAFLOW),
    )(*srcs, *lands, send_sems, recv_sems, *after)
    return list(res[ns:])


LOSS_ROW = "loss_cols"
SMALL_EARLY = ("post_mix_norm", "pre_ffn_norm", "post_ffn_norm", "sgu_ln_gain", "sgu_ln_bias", "attn_out_norm",
               "sgu_out_norm", "sgu_b_spatial", LOSS_ROW)
SMALL_LATE = ("pre_mix_norm",)
SMALL_WSP = ("sgu_w_spatial",)


def _pack(d, names, rows, dtype=F32):
    flat = [d[n].reshape(-1) for n in names]
    used = sum(f.shape[0] for f in flat)
    flat.append(jnp.zeros((rows * 1024 - used,), F32))
    return jnp.concatenate(flat).reshape(rows, 1024).astype(dtype)


def _unpack(buf, names, shapes):
    flat = buf.reshape(-1)
    out, off = {}, 0
    for n in names:
        size = int(np.prod(shapes[n]))
        out[n] = flat[off:off + size].reshape(shapes[n])
        off += size
    return out


def _merge(comms):
    def run(phase):
        def go(ins, outs, ss, rs):
            ii = oi = si = 0
            for c in comms:
                getattr(c, phase)(ins[ii:ii + len(c.args)], outs[oi:oi + len(c.out_shape)],
                                  ss.at[pl.ds(si, c.n_sems)], rs.at[pl.ds(si, c.n_sems)])
                ii += len(c.args)
                oi += len(c.out_shape)
                si += c.n_sems
        return go

    return _Comm([a for c in comms for a in c.args], [o for c in comms for o in c.out_shape],
                 sum(c.n_sems for c in comms), run("start"), run("finish"))


def _chip_sums(gws, recvs, shard_core):
    n = len(gws)
    _, rows, cols = gws[0].shape
    hw = rows // 2

    def body(sc_ref, *refs):
        for k in range(n):
            refs[2 * n + k][...] = (refs[k][...] + refs[n + k][...]).astype(BF16)

    def other(s, sc):
        return jnp.where(s >= sc[0], s + 1, s)

    mine = pl.BlockSpec((1, hw, cols), lambda s, sc: (other(s, sc), sc[1], 0))
    plain = pl.BlockSpec((1, hw, cols), lambda s, sc: (other(s, sc), 0, 0))
    return pl.pallas_call(
        body, name="rs_chip_sums",
        grid_spec=pltpu.PrefetchScalarGridSpec(num_scalar_prefetch=1, grid=(N_SHARD - 1,),
                                               in_specs=[mine] * n + [plain] * n, out_specs=[plain] * n),
        out_shape=[_sds((N_SHARD, hw, cols), BF16)] * n,
        compiler_params=_seq_params(),
    )(shard_core, *gws, *recvs)


def _final_sums(gws, recv_sibs, recv_chips, shard_core):
    n = len(gws)
    _, rows, cols = gws[0].shape
    hw = rows // 2

    def body(sc_ref, *refs):
        for k in range(n):
            acc = refs[k][0] + refs[n + k][0]
            for j in range(3):
                acc = acc + refs[2 * n + k][j].astype(F32)
            refs[3 * n + k][0] = acc

    return pl.pallas_call(
        body, name="rs_final_sums",
        grid_spec=pltpu.PrefetchScalarGridSpec(
            num_scalar_prefetch=1, grid=(1,),
            in_specs=[pl.BlockSpec((1, hw, cols), lambda i, sc: (sc[0], sc[1], 0))] * n
            + [pl.BlockSpec((1, hw, cols), lambda i, sc: (sc[0], 0, 0))] * n
            + [pl.BlockSpec((3, hw, cols), lambda i, sc: (0, 0, 0))] * n,
            out_specs=[pl.BlockSpec((1, hw, cols), lambda i, sc: (sc[1], 0, 0))] * n),
        out_shape=[_sds((2, hw, cols), F32)] * n,
        compiler_params=_seq_params(),
    )(shard_core, *gws, *recv_sibs, *recv_chips)


def _adamw_multi(ws, gs, ms, vs, block_rows, name, after=()):
    n = len(ws)
    rows, cols = ws[0].shape

    def body(*refs):
        outs = refs[4 * n + len(after):]
        for k in range(n):
            g = refs[n + k][...]
            d, m, v = _adam_math(refs[k][...], g, refs[2 * n + k][...], refs[3 * n + k][...])
            outs[4 * k][...], outs[4 * k + 1][...], outs[4 * k + 2][...], outs[4 * k + 3][...] = g, d, m, v

    spec = pl.BlockSpec((block_rows, cols), lambda i: (i, 0))
    res = pl.pallas_call(
        body, name=name, grid=(rows // block_rows,), in_specs=[spec] * (4 * n) + [ANY] * len(after),
        out_specs=[spec] * (4 * n), out_shape=[_sds((rows, cols), F32)] * (4 * n),
        compiler_params=_seq_params(),
    )(*ws, *gs, *ms, *vs, *after)
    return [tuple(res[4 * k:4 * k + 4]) for k in range(n)]


def _wgrad_ff(a_list, b, name, comms=()):
    n = len(a_list)
    cols = 256

    def body(*refs):
        bv = refs[n][...]
        for k in range(n):
            refs[n + 1 + k][...] = _dot_tn(refs[k][...], bv)

    res = _pcall(
        body, name=name, grid=(FF // cols,),
        in_specs=[pl.BlockSpec((SEQ, cols), lambda j: (0, j))] * n
        + [pl.BlockSpec((SEQ, D_MODEL), lambda j: (0, 0), pipeline_mode=pl.Buffered(1))],
        out_specs=[pl.BlockSpec((cols, D_MODEL), lambda j: (j, 0))] * n,
        out_shape=[_sds((FF, D_MODEL), F32)] * n, args=(*a_list, b), comms=comms)
    mine, theirs = res if comms else (res, None)
    mine = [m.reshape(N_SHARD, FF_S, D_MODEL) for m in mine]
    return (mine, theirs) if comms else mine


def _wgrad_pair(a1, a2, b, a_spec, b_spec, out_block, name, comms=()):
    def body(a1_ref, a2_ref, b_ref, o1_ref, o2_ref):
        bv = b_ref[...]
        o1_ref[0] = _dot_tn(a1_ref[0], bv)
        o2_ref[0] = _dot_tn(a2_ref[0], bv)

    out_spec = pl.BlockSpec((1,) + out_block, lambda s: (s, 0, 0))
    return _pcall(
        body, name=name, grid=(N_SHARD,), in_specs=[a_spec, a_spec, b_spec], out_specs=[out_spec, out_spec],
        out_shape=[_sds((N_SHARD,) + out_block, F32)] * 2, args=(a1, a2, b), comms=comms)


def _comm_only(name, comms):
    return _pcall(lambda: None, name=name, grid=(1,), in_specs=[], out_specs=[], out_shape=[], args=(),
                  comms=comms)[1]


def _adam_math(w, g, m, v):
    m = ADAM_B1 * m + (1.0 - ADAM_B1) * g
    v = ADAM_B2 * v + (1.0 - ADAM_B2) * (g * g)
    m_hat = m / (1.0 - ADAM_B1 ** ADAM_STEP)
    v_hat = v / (1.0 - ADAM_B2 ** ADAM_STEP)
    return -ADAM_LR * (m_hat / (jnp.sqrt(v_hat) + ADAM_EPS) + ADAM_WD * w), m, v


def _adamw_small(own, slots, w, m, v, me):
    rows, cols = own.shape

    def body(me_ref, own_ref, slots_ref, w_ref, m_ref, v_ref, g_ref, d_ref, nm_ref, nv_ref):
        own_v = own_ref[...].astype(F32)
        g = jnp.where(me_ref[0] == 0, own_v, slots_ref[0].astype(F32))
        for i in range(1, 8):
            g = g + jnp.where(me_ref[0] == i, own_v, slots_ref[i].astype(F32))
        g_ref[...] = g
        d_ref[...], nm_ref[...], nv_ref[...] = _adam_math(w_ref[...], g, m_ref[...], v_ref[...])

    flat = pl.BlockSpec((rows, cols), lambda i, me_ref: (0, 0))
    return pl.pallas_call(
        body, name="adamw_small",
        grid_spec=pltpu.PrefetchScalarGridSpec(
            num_scalar_prefetch=1, grid=(1,),
            in_specs=[flat, pl.BlockSpec((8, rows, cols), lambda i, me_ref: (0, 0, 0)), flat, flat, flat],
            out_specs=[flat] * 4),
        out_shape=[_sds((rows, cols), F32)] * 4,
        compiler_params=_seq_params(),
    )(me, own, slots, w, m, v)


def kernel(x, positions, pre_mix_norm, w_in, sgu_ln_gain, sgu_ln_bias, sgu_w_spatial, sgu_b_spatial, attn_out_norm, sgu_out_norm, w_out, post_mix_norm, pre_ffn_norm, w_gate, w_up, w_down, post_ffn_norm, loss_target, m_pre_mix_norm, m_w_in, m_sgu_ln_gain, m_sgu_ln_bias, m_sgu_w_spatial, m_sgu_b_spatial, m_attn_out_norm, m_sgu_out_norm, m_w_out, m_post_mix_norm, m_pre_ffn_norm, m_w_gate, m_w_up, m_w_down, m_post_ffn_norm, v_pre_mix_norm, v_w_in, v_sgu_ln_gain, v_sgu_ln_bias, v_sgu_w_spatial, v_sgu_b_spatial, v_attn_out_norm, v_sgu_out_norm, v_w_out, v_post_mix_norm, v_pre_ffn_norm, v_w_gate, v_w_up, v_w_down, v_post_ffn_norm):
    a = dict(locals())
    cx, cy, cc = _coords()
    s_me = 2 * cx + cy
    core = jnp.stack([cc]).astype(jnp.int32)
    shard_core = jnp.stack([s_me, cc]).astype(jnp.int32)
    me = jnp.stack([4 * cx + 2 * cy + cc]).astype(jnp.int32)
    small = {n: (a[n][0] if a[n].ndim > 2 else a[n]) for n in SMALL}
    b_t = small["sgu_b_spatial"].T
    xs, pos, target = x[0], positions.reshape(SEQ, 1), loss_target[0]
    flipped = ("w_gate", "w_up")

    def big(name, n):
        return jnp.swapaxes(a[name], 1, 2)[0] if n in flipped else a[name][0]

    own = {"w_in": _halves(big("w_in", "w_in").astype(BF16))}

    def with_own(full, n):
        full = lax.dynamic_update_slice(full, own[n][None], (s_me, 0, 0, 0))
        return full.reshape((N_SHARD,) + big(n, n).shape)

    ffn = ("w_gate", "w_up", "w_down")
    g_in = _gather_ici([own["w_in"]])
    (s_in,), token = _split_starts("gather_in_start", [g_in], small["pre_mix_norm"])
    for n in ("w_out",) + ffn:
        own[n] = _halves((big(n, n) + token[0:1, 0:1]).astype(BF16))
    g_out, g_ffn = _gather_ici([own["w_out"]]), _gather_ici([own[n] for n in ffn])
    (s_out, s_ffn), token = _split_starts("gather_rest_start", [g_out, g_ffn], token)
    in_lands = _split_wait("gather_in_wait", g_in, *s_in, token)
    ((in_lands,),) = _comm_only("comm_pass_in", [_gather_pass(in_lands)])
    w_in_f = with_own(in_lands, "w_in")
    h, u, vs, sgu, *qkv = _inproj_fwd(xs, pos, small["pre_mix_norm"], w_in_f, small["sgu_ln_gain"],
                                      small["sgu_ln_bias"], small["sgu_w_spatial"], b_t)
    views = [tuple(qkv[3 * i:3 * i + 3]) for i in range(len(DILATIONS))]
    out_lands = _split_wait("gather_out_wait", g_out, *s_out, sgu)
    o_list, l_list = [], []
    for dil, (qv, kv, vv) in zip(DILATIONS, views):
        if dil == 1:
            (o, l), ((out_lands,),) = _attn_fwd(qv, kv, vv, dil, comms=[_gather_pass(out_lands)])
        else:
            o, l = _attn_fwd(qv, kv, vv, dil)
        o_list.append(o)
        l_list.append(l)
    w_out_f = with_own(out_lands, "w_out")
    ffn_lands = _split_wait("gather_ffn_wait", g_ffn, *s_ffn, l_list[-1])
    (attn, mixed, y, x1, *lses), (ffn_lands,) = _mix_out_fwd(
        o_list, l_list, sgu, xs, w_out_f, small["attn_out_norm"], small["sgu_out_norm"], small["post_mix_norm"],
        comms=[_gather_pass(ffn_lands)])
    w_gate_f, w_up_f, w_down_f = (with_own(f, n).reshape(FF, D_MODEL) for f, n in zip(ffn_lands, ffn))
    h2, act, dg, dup, df, dx1, loss_cols, d_pre_ffn, d_post_ffn = _ffn_fwd_bwd(
        x1, target, w_gate_f, w_up_f, w_down_f, small["pre_ffn_norm"], small["post_ffn_norm"])

    full_tok = pl.BlockSpec((SEQ, D_MODEL), lambda s: (0, 0), pipeline_mode=pl.Buffered(1))
    ff_tok = pl.BlockSpec((1, SEQ, FF_S), lambda s: (s, 0, 0))
    gw = {}
    gw["w_gate"], gw["w_up"] = _wgrad_ff([dg, dup], h2, "wgrad_gate_up")
    (gw["w_down"],), ((sib_gate,),) = _wgrad_ff([act], df, "wgrad_down", comms=[_rs_sibling([gw["w_gate"]])])
    (dy, du, dvs_sgu, d_post_mix, d_attn_norm, d_sgu_norm, d_w_sp, d_b_sp, d_ln_gain, d_ln_bias,
     *dviews), ((sib_up, sib_down),) = _outproj_bwd(
        dx1, y, attn, sgu, w_out_f, small["post_mix_norm"], small["attn_out_norm"], small["sgu_out_norm"],
        u, vs, small["sgu_ln_gain"], small["sgu_ln_bias"], small["sgu_w_spatial"], b_t,
        comms=[_rs_sibling([gw["w_up"], gw["w_down"]])])
    sib = {"w_gate": sib_gate, "w_up": sib_up, "w_down": sib_down}
    part = dict(zip(ffn, _chip_sums([gw[n] for n in ffn], [sib[n] for n in ffn], shard_core)))
    gw["w_out"] = _wgrad(mixed, dy, pl.BlockSpec((SEQ, OUT_S), lambda s: (0, s)), full_tok, (OUT_S, D_MODEL),
                         "wgrad_out")
    x_ffn = _rs_chips([part[n] for n in ffn])
    packed_early = _pack({
        "sgu_ln_gain": d_ln_gain, "sgu_ln_bias": d_ln_bias, "sgu_b_spatial": d_b_sp,
        "attn_out_norm": d_attn_norm, "sgu_out_norm": d_sgu_norm, "post_mix_norm": d_post_mix,
        "pre_ffn_norm": d_pre_ffn, "post_ffn_norm": d_post_ffn, LOSS_ROW: loss_cols}, SMALL_EARLY, 8)
    packed_wsp = _pack({"sgu_w_spatial": d_w_sp}, SMALL_WSP, CHUNK, BF16)
    x_small, x_wsp = _small_exchange(packed_early), _small_exchange(packed_wsp)
    (s_ffn,), token = _split_starts("rs_ffn_start", [x_ffn], small["pre_mix_norm"])

    dqs, dks, dvs = [], [], []
    for i, (dil, (qv, kv, vv)) in enumerate(zip(DILATIONS, views)):
        if dil == 1:
            (dq, dk, dv), ((sib["w_out"],),) = _attn_bwd(qv, kv, vv, dviews[i], dviews[3 + i], lses[i], dil,
                                                        comms=[_rs_sibling([gw["w_out"]])], after=[token])
            (part["w_out"],) = _chip_sums([gw["w_out"]], [sib["w_out"]], shard_core)
            x_out = _rs_chips([part["w_out"]])
            (s_out, s_small, s_wsp), token = _split_starts("rs_out_small_start", [x_out, x_small, x_wsp], token)
        else:
            dq, dk, dv = _attn_bwd(qv, kv, vv, dviews[i], dviews[3 + i], lses[i], dil, after=[token])
        dqs.append(dq)
        dks.append(dk)
        dvs.append(dv)
    half, far, joined = {}, {}, {}
    far.update(zip(ffn, _split_wait("rs_ffn_wait", x_ffn, *s_ffn, dvs[-1])))
    half.update(zip(ffn, _final_sums([gw[n] for n in ffn], [sib[n] for n in ffn], [far[n] for n in ffn],
                                     shard_core)))
    dproj, grad_x, d_pre_mix = _inproj_bwd(dqs, dks, dvs, du, dvs_sgu, pos, xs, dx1, w_in_f, small["pre_mix_norm"])
    (far["w_out"],) = _split_wait("rs_out_wait", x_out, *s_out, grad_x)
    half["w_out"] = _rs_final_sum(gw["w_out"], sib["w_out"], far["w_out"], shard_core)
    packed_late = _pack({"pre_mix_norm": d_pre_mix}, SMALL_LATE, 8)
    names = ffn + ("w_out",)
    gw["w_in"], (got, (slots_late,)) = _wgrad(
        h, dproj, full_tok, pl.BlockSpec((SEQ, IN_S), lambda s: (0, s)), (D_MODEL, IN_S), "wgrad_in",
        comms=[_rs_join([half[n] for n in names]), _small_exchange(packed_late)])
    joined.update(zip(names, got))
    ((sib["w_in"],),) = _comm_only("comm_rs_sibling_in", [_rs_sibling([gw["w_in"]])])
    (part["w_in"],) = _chip_sums([gw["w_in"]], [sib["w_in"]], shard_core)
    x_in = _rs_chips([part["w_in"]])
    (s_in,), token = _split_starts("rs_in_start", [x_in], small["pre_mix_norm"])

    grads, deltas, new_m, new_v = {}, {}, {}, {}

    def record(n, outs):
        grads[n], deltas[n], new_m[n], new_v[n] = (
            jnp.swapaxes(o[None], 1, 2) if n in flipped else o[None] for o in outs)

    def update(names, block_rows, name, after):
        for n, outs in zip(names, _adamw_multi(
                [big(n, n) for n in names], [joined[n].reshape(big(n, n).shape) for n in names],
                [big("m_" + n, n) for n in names], [big("v_" + n, n) for n in names], block_rows, name, after)):
            record(n, outs)

    update(ffn, FF_S // 4, "adamw_ffn", [token])
    update(("w_out",), BIG_ADAM_ROWS["w_out"], "adamw_w_out", [token])
    (slots_early,) = _split_wait("small_early_wait", x_small, *s_small, [new_v[n] for n in ("w_down", "w_out")])
    (slots_wsp,) = _split_wait("small_wsp_wait", x_wsp, *s_wsp, slots_early)
    (far["w_in"],) = _split_wait("rs_in_wait", x_in, *s_in, slots_wsp)
    half["w_in"] = _rs_final_sum(gw["w_in"], sib["w_in"], far["w_in"], shard_core)
    ((joined["w_in"],),) = _comm_only("comm_rs_join_in", [_rs_join([half["w_in"]])])
    update(("w_in",), BIG_ADAM_ROWS["w_in"], "adamw_w_in", [])
    a[LOSS_ROW] = a["m_" + LOSS_ROW] = a["v_" + LOSS_ROW] = jnp.zeros((1, D_MODEL), F32)
    for names, rows, packed, slots in ((SMALL_EARLY, 8, packed_early, slots_early),
                                       (SMALL_WSP, CHUNK, packed_wsp, slots_wsp),
                                       (SMALL_LATE, 8, packed_late, slots_late)):
        outs = _adamw_small(packed, slots, _pack(a, names, rows), _pack({n: a["m_" + n] for n in names}, names, rows),
                            _pack({n: a["v_" + n] for n in names}, names, rows), me)
        for dst, buf in zip((grads, deltas, new_m, new_v), outs):
            dst.update(_unpack(buf, names, {n: a[n].shape for n in names}))
    loss = jnp.sum(grads[LOSS_ROW]) * np.float32(0.5 / D_MODEL)
    return (loss, grad_x[None], *[grads[n] for n in WEIGHTS], *[deltas[n] for n in WEIGHTS],
            *[new_m[n] for n in WEIGHTS], *[new_v[n] for n in WEIGHTS])
```

```python
import numpy as np
import jax
import jax.numpy as jnp
from jax import lax
from jax.experimental import pallas as pl
from jax.experimental.pallas import tpu as pltpu

F32 = jnp.float32
BF16 = jnp.bfloat16

SEQ = 2048
D_MODEL = 1024
HEAD_DIM = 64
ATTN_W = 512
SGU_W = 512
SGU_GROUPS = 8
CHUNK = 128
DILATIONS = (1, 4, 16)
N_SHARD = 4
IN_S = 640
OUT_S = 256
FF_S = 704
PROJ_W = N_SHARD * IN_S
FF = N_SHARD * FF_S
FF_CHUNKS = ((0, 1024), (1024, 2048), (2048, FF))
RMS_EPS = 1e-6
LN_EPS = 1e-5
ROPE_THETA = 500000.0
ATTN_SCALE = 1.0 / np.sqrt(HEAD_DIM)
NEG = -1e30
TM = 512
TM_FFN = 256
VMEM_LIMIT = 56 * 1024 * 1024

ADAM_LR = 0.001
ADAM_B1 = 0.9
ADAM_B2 = 0.999
ADAM_EPS = 1e-08
ADAM_WD = 0.01
ADAM_STEP = 10

MESH = pl.DeviceIdType.MESH
ANY = pl.BlockSpec(memory_space=pl.ANY)


def _dot(a, b):
    return jnp.dot(a, b, preferred_element_type=F32)


def _dot_nt(a, b):
    return lax.dot_general(a, b, (((1,), (1,)), ((), ())), preferred_element_type=F32)


def _dot_tn(a, b):
    return lax.dot_general(a, b, (((0,), (0,)), ((), ())), preferred_element_type=F32)


def _dot_exact(a, b):
    return jnp.dot(a, b, preferred_element_type=F32, precision=lax.Precision.HIGHEST)


def _dot_select(a, sel):
    hi = a.astype(BF16)
    lo = (a - hi.astype(F32)).astype(BF16)
    sel = sel.astype(BF16)
    return _dot(hi, sel) + _dot(lo, sel)


def _rms_stats(x):
    r = lax.rsqrt(jnp.mean(x * x, axis=-1, keepdims=True) + RMS_EPS)
    return x * r, r


def _rms_bwd(xh, r, gain, dy):
    dxh = dy * gain
    dx = r * (dxh - xh * jnp.mean(dxh * xh, axis=-1, keepdims=True))
    return dx, jnp.sum(dy * xh, axis=0, keepdims=True)


_ERF_ALPHA = (-2.72614225801306e-10, 2.77068142495902e-08, -2.10102402082508e-06, -5.69250639462346e-05,
              -7.34990630326855e-04, -2.95459980854025e-03, -1.60960333262415e-02)
_ERF_BETA = (-1.45660718464996e-05, -2.13374055278905e-04, -1.68282697438203e-03, -7.37332916720468e-03,
             -1.42647390514189e-02)


def _erf(x):
    x = jnp.clip(x, -4.0, 4.0)
    x2 = x * x
    p = jnp.full_like(x, _ERF_ALPHA[0])
    for a in _ERF_ALPHA[1:]:
        p = p * x2 + a
    q = jnp.full_like(x, _ERF_BETA[0])
    for b in _ERF_BETA[1:]:
        q = q * x2 + b
    return x * p / q


def _normal_cdf(x):
    return 0.5 * (1.0 + _erf(x * np.float32(1.0 / np.sqrt(2.0))))


def _gelu_grad(x, cdf):
    pdf = jnp.exp(-0.5 * x * x) * np.float32(1.0 / np.sqrt(2.0 * np.pi))
    return cdf + x * pdf


def _sigmoid(x):
    return 1.0 / (1.0 + jnp.exp(-x))


_INV_FREQ = tuple(float(np.float32(ROPE_THETA ** (-2.0 * j / 16.0))) for j in range(8))


def _rot_tables(pos):
    lane = lax.broadcasted_iota(jnp.int32, (1, 128), 1)
    d = lane & 63
    j = d & 7
    inv = jnp.zeros((1, 128), F32)
    for jj in range(8):
        inv = jnp.where(j == jj, _INV_FREQ[jj], inv)
    ang = pos.astype(F32) * inv
    c = jnp.cos(ang)
    s = jnp.sin(ang)
    cos_t = jnp.where(d < 16, c, 1.0)
    sin_a = jnp.where(d < 8, -s, 0.0)
    sin_b = jnp.where((d >= 8) & (d < 16), s, 0.0)
    return tuple(jnp.tile(t, (1, 4)) for t in (cos_t, sin_a, sin_b))


def _rope(x, tabs):
    cos_t, sin_a, sin_b = tabs
    return x * cos_t + pltpu.roll(x, 504, 1) * sin_a + pltpu.roll(x, 8, 1) * sin_b


def _rope_bwd(dy, tabs):
    cos_t, sin_a, sin_b = tabs
    return dy * cos_t + pltpu.roll(dy * sin_a, 8, 1) + pltpu.roll(dy * sin_b, 504, 1)


def _left_half():
    return lax.broadcasted_iota(jnp.int32, (CHUNK, CHUNK), 1) < HEAD_DIM


def _group_ones():
    lane = lax.broadcasted_iota(jnp.int32, (SGU_GROUPS, SGU_W), 1)
    row = lax.broadcasted_iota(jnp.int32, (SGU_GROUPS, SGU_W), 0)
    return ((lane >> 6) == row).astype(F32)


def _masked_spatial(w_ref):
    row = lax.broadcasted_iota(jnp.int32, (CHUNK, CHUNK), 0)
    col = lax.broadcasted_iota(jnp.int32, (CHUNK, CHUNK), 1)
    return [jnp.where(col <= row, w_ref[g], 0.0).astype(BF16) for g in range(SGU_GROUPS)]


def _sgu_core(u, vs, lg, lb, wm, bias_full):
    tm = u.shape[0]
    cdf_u, cdf_vs = _normal_cdf(u), _normal_cdf(vs)
    gu = u * cdf_u
    gv = vs * cdf_vs
    mu = jnp.mean(gv, axis=-1, keepdims=True)
    xc = gv - mu
    rstd = lax.rsqrt(jnp.mean(xc * xc, axis=-1, keepdims=True) + LN_EPS)
    xh = xc * rstd
    vnb = (xh * lg + lb).astype(BF16)
    left = _left_half()
    rows = []
    for c in range(tm // CHUNK):
        pieces = []
        for p in range(4):
            vp = vnb[c * CHUNK:(c + 1) * CHUNK, p * 128:(p + 1) * 128]
            pieces.append(jnp.where(left, _dot(wm[2 * p], vp), _dot(wm[2 * p + 1], vp)))
        rows.append(jnp.concatenate(pieces, axis=1) + bias_full)
    mixed = jnp.concatenate(rows, axis=0)
    return gu, xh, rstd, vnb, mixed, cdf_u, cdf_vs


def _resident(shape):
    n = len(shape)
    return pl.BlockSpec(shape, lambda *_: (0,) * n, pipeline_mode=pl.Buffered(1))


def _rows(ncol, tm=TM):
    return pl.BlockSpec((tm, ncol), lambda i: (i, 0))


def _acc(ncol, nrow=1):
    return pl.BlockSpec((nrow, ncol), lambda i: (0, 0))


HEAD_W = 128


def _view_rows(dil, width=ATTN_W, tm=TM):
    return pl.BlockSpec((tm // dil, dil * width), lambda i: (i, 0))


def _view_shape(dil, dtype, width=ATTN_W):
    return _sds((SEQ // dil, dil * width), dtype)


def _slab_scratch():
    return pltpu.VMEM((4, TM, 128), F32)


def _store_view(val, out_ref, slabs, dil):
    width = val.shape[1]
    for j in range(width // 128):
        slabs[j] = val[:, j * 128:(j + 1) * 128]
    for r in range(dil):
        for j in range(width // 128):
            c0 = r * width + j * 128
            out_ref[:, c0:c0 + 128] = slabs.at[j][pl.ds(r, TM // dil, stride=dil), :].astype(out_ref.dtype)


def _load_view(in_ref, slabs, dil, width=ATTN_W):
    for r in range(dil):
        for j in range(width // 128):
            c0 = r * width + j * 128
            slabs.at[j][pl.ds(r, TM // dil, stride=dil), :] = in_ref[:, c0:c0 + 128].astype(F32)
    return jnp.concatenate([slabs[j] for j in range(width // 128)], axis=1)


def _head_spread():
    m = lax.broadcasted_iota(jnp.int32, (HEAD_W, ATTN_W), 0)
    lane = lax.broadcasted_iota(jnp.int32, (HEAD_W, ATTN_W), 1)
    return (m == 16 * (lane >> 6)).astype(F32)


def _head_sum():
    lane = lax.broadcasted_iota(jnp.int32, (ATTN_W, HEAD_W), 0)
    m = lax.broadcasted_iota(jnp.int32, (ATTN_W, HEAD_W), 1)
    return ((lane >> 6) == (m >> 4)).astype(F32)


def _seq_params():
    return pltpu.CompilerParams(dimension_semantics=("arbitrary",), vmem_limit_bytes=VMEM_LIMIT)


def _sds(shape, dtype):
    return jax.ShapeDtypeStruct(shape, dtype)


class _Comm:
    def __init__(self, args, out_shape, n_sems, start, finish, aliased=False):
        self.args, self.out_shape, self.n_sems = list(args), list(out_shape), n_sems
        self.start, self.finish, self.aliased = start, finish, aliased


def _pcall(body, *, name, grid, in_specs, out_specs, out_shape, args, scratch_shapes=(), comms=(), after=()):
    single = not isinstance(out_shape, (list, tuple))
    out_specs = [out_specs] if single else list(out_specs)
    out_shape = [out_shape] if single else list(out_shape)
    n_in, n_out, n_scr = len(in_specs), len(out_shape), len(scratch_shapes)
    c_args = [a for c in comms for a in c.args]
    c_outs = [o for c in comms for o in c.out_shape]
    aliases, ai, ao = {}, n_in, n_out
    for c in comms:
        if c.aliased:
            aliases.update({ai + k: ao + k for k in range(len(c.args))})
        ai += len(c.args)
        ao += len(c.out_shape)
    sems = [pltpu.SemaphoreType.DMA((c.n_sems,)) for c in comms for _ in range(2)]
    steps = grid[0]

    def wrapped(*refs):
        o0 = n_in + len(c_args) + len(after)
        s0 = o0 + n_out + len(c_outs)
        m_in, m_out, m_sem = refs[n_in:n_in + len(c_args)], refs[o0 + n_out:s0], refs[s0 + n_scr:]

        def each(phase):
            ii = oi = 0
            for k, c in enumerate(comms):
                getattr(c, phase)(m_in[ii:ii + len(c.args)], m_out[oi:oi + len(c.out_shape)],
                                  m_sem[2 * k], m_sem[2 * k + 1])
                ii += len(c.args)
                oi += len(c.out_shape)

        if comms:
            @pl.when(pl.program_id(0) == 0)
            def _():
                each("start")

        body(*refs[:n_in], *refs[o0:o0 + n_out], *refs[s0:s0 + n_scr])

        if comms:
            @pl.when(pl.program_id(0) == steps - 1)
            def _():
                each("finish")

    res = pl.pallas_call(
        wrapped, name=name, grid=grid,
        in_specs=list(in_specs) + [ANY] * (len(c_args) + len(after)), out_specs=out_specs + [ANY] * len(c_outs),
        out_shape=out_shape + c_outs, scratch_shapes=list(scratch_shapes) + sems,
        input_output_aliases=aliases, compiler_params=_seq_params(),
    )(*args, *c_args, *after)
    mine = res[0] if single else list(res[:n_out])
    if not comms:
        return mine
    theirs, oi = [], n_out
    for c in comms:
        theirs.append(list(res[oi:oi + len(c.out_shape)]))
        oi += len(c.out_shape)
    return mine, theirs


def _in_pieces(g):
    lo, hi = 512 * g, 512 * (g + 1)
    return [(s, max(lo, IN_S * s) - IN_S * s, min(hi, IN_S * (s + 1)) - IN_S * s)
            for s in range(N_SHARD) if max(lo, IN_S * s) < min(hi, IN_S * (s + 1))]


def _inproj_fwd(x, pos, g_pre, w_in, lg, lb, w_sp, b_t, comms=()):
    def body(x_ref, pos_ref, g_ref, w_ref, lg_ref, lb_ref, wsp_ref, bt_ref, h_ref, u_ref, vs_ref, sgu_ref, *rest):
        qkv_refs, slabs = rest[:9], rest[9:]
        xh, _ = _rms_stats(x_ref[...])
        h = (xh * g_ref[...]).astype(BF16)
        h_ref[...] = h
        tabs = _rot_tables(pos_ref[...])

        def group(g):
            return jnp.concatenate([_dot(h, w_ref[s, :, a:b]) for s, a, b in _in_pieces(g)], axis=1)

        for t in range(3):
            val = group(t)
            if t < 2:
                val = _rope(val, tabs)
            if t == 0:
                val = val * np.float32(ATTN_SCALE)
            qkv_refs[t][...] = val.astype(BF16)
            for i, dil in enumerate(DILATIONS[1:]):
                _store_view(val, qkv_refs[3 * (i + 1) + t], slabs[t], dil)
        u = group(3)
        vs = group(4)
        u_ref[...] = u
        vs_ref[...] = vs
        bias_full = _dot_exact(bt_ref[...], _group_ones())
        gu, _, _, _, mixed, _, _ = _sgu_core(u, vs, lg_ref[...], lb_ref[...], _masked_spatial(wsp_ref), bias_full)
        sgu_ref[...] = gu * mixed

    return _pcall(
        body, name="inproj_sgu_fwd", grid=(SEQ // TM,),
        in_specs=[_rows(D_MODEL), _rows(1), _resident((1, D_MODEL)), _resident((N_SHARD, D_MODEL, IN_S)),
                  _resident((1, SGU_W)), _resident((1, SGU_W)), _resident((SGU_GROUPS, CHUNK, CHUNK)),
                  _resident((CHUNK, SGU_GROUPS))],
        out_specs=[_rows(D_MODEL), _rows(512), _rows(512), _rows(512)]
        + [_view_rows(dil) for dil in DILATIONS for _ in range(3)],
        out_shape=[_sds((SEQ, D_MODEL), BF16), _sds((SEQ, 512), F32), _sds((SEQ, 512), F32), _sds((SEQ, 512), F32)]
        + [_view_shape(dil, BF16) for dil in DILATIONS for _ in range(3)],
        scratch_shapes=[_slab_scratch() for _ in range(3)],
        args=(x, pos, g_pre, w_in, lg, lb, w_sp, b_t), comms=comms)


def _block_masks():
    row = lax.broadcasted_iota(jnp.int32, (CHUNK, CHUNK), 0)
    col = lax.broadcasted_iota(jnp.int32, (CHUNK, CHUNK), 1)
    return col <= row, col >= row


def _attn_fwd(qv, kv, vv, dil, comms=()):
    seg = SEQ // dil
    nblk = seg // CHUNK
    rps = 4 if nblk == 1 else 1

    def body(q_ref, k_ref, v_ref, o_ref, l_ref):
        left = _left_half()
        m_cur, m_prev = _block_masks()
        zero = jnp.zeros((CHUNK, CHUNK), BF16)
        ones = (jnp.where(left, 1.0, 0.0).astype(BF16), jnp.where(left, 0.0, 1.0).astype(BF16))

        sides = tuple(enumerate((left, ~left)))

        def rows(b):
            if isinstance(b, int):
                return b * CHUNK, max(b - 1, 0) * CHUNK
            return pl.multiple_of(b * CHUNK, CHUNK), pl.multiple_of(jnp.maximum(b - 1, 0) * CHUNK, CHUNK)

        def first(rr, b):
            r0, rp = rows(b)
            prev_ok = m_prev & (b > 0)
            tiles, scores = [], []
            for hp in range(4):
                ls = slice(rr * ATTN_W + hp * 128, rr * ATTN_W + (hp + 1) * 128)
                qp = q_ref[pl.ds(r0, CHUNK), ls]
                kc = k_ref[pl.ds(r0, CHUNK), ls]
                kp = k_ref[pl.ds(rp, CHUNK), ls] if nblk > 1 else None
                tiles.append((ls, v_ref[pl.ds(r0, CHUNK), ls], v_ref[pl.ds(rp, CHUNK), ls] if nblk > 1 else None))
                for _, hm in sides:
                    qh = jnp.where(hm, qp, zero)
                    sc = jnp.where(m_cur, _dot_nt(qh, kc), NEG)
                    sp = jnp.where(prev_ok, _dot_nt(qh, kp), NEG) if nblk > 1 else None
                    scores.append((sc, sp))
            return tiles, scores

        def second(scores):
            probs = []
            for sc, sp in scores:
                if nblk > 1:
                    m = jnp.max(jnp.maximum(sc, sp), axis=-1, keepdims=True)
                    pc = jnp.exp(sc - m)
                    pp = jnp.exp(sp - m)
                    probs.append((m, pc.astype(BF16), pp.astype(BF16), (pc + pp).astype(BF16)))
                else:
                    m = jnp.max(sc, axis=-1, keepdims=True)
                    pc = jnp.exp(sc - m).astype(BF16)
                    probs.append((m, pc, None, pc))
            return probs

        def third(rr, b, tiles, probs):
            r0, _ = rows(b)
            for hp, (ls, vc, vp) in enumerate(tiles):
                acc = jnp.zeros((CHUNK, CHUNK), F32)
                den = jnp.zeros((CHUNK, CHUNK), F32)
                for side, hm in sides:
                    _, pc, pp, psum = probs[2 * hp + side]
                    acc = acc + _dot(pc, jnp.where(hm, vc, zero))
                    if nblk > 1:
                        acc = acc + _dot(pp, jnp.where(hm, vp, zero))
                    den = den + _dot(psum, ones[side])
                o_ref[pl.ds(r0, CHUNK), ls] = (acc / den).astype(o_ref.dtype)
                lse = jnp.where(left, probs[2 * hp][0], probs[2 * hp + 1][0]) + jnp.log(den)
                l_ref[pl.ds(r0, CHUNK), rr * HEAD_W + 32 * hp:rr * HEAD_W + 32 * hp + 32] = lse[:, 48:80]

        def run(units):
            data = [first(rr, b) for rr, b in units]
            probs = [second(scores) for _, scores in data]
            for (rr, b), (tiles, _), pr in zip(units, data, probs):
                third(rr, b, tiles, pr)

        if nblk == 1:
            run([(rr, 0) for rr in range(rps)])
        else:
            def one(b, carry):
                run([(0, b)])
                return carry

            lax.fori_loop(0, nblk, one, 0)

    spec = pl.BlockSpec((seg, rps * ATTN_W), lambda r: (0, r))
    return _pcall(
        body, name=f"attn_fwd_d{dil}", grid=(dil // rps,),
        in_specs=[spec, spec, spec], out_specs=[spec, pl.BlockSpec((seg, rps * HEAD_W), lambda r: (0, r))],
        out_shape=[_sds((seg, dil * ATTN_W), BF16), _sds((seg, dil * HEAD_W), F32)],
        args=(qv, kv, vv), comms=comms)


def _lane_left(nrows):
    return lax.broadcasted_iota(jnp.int32, (nrows, CHUNK), 1) < HEAD_DIM


def _attn_rows(b):
    if isinstance(b, int):
        return b * CHUNK, max(b - 1, 0) * CHUNK
    return pl.multiple_of(b * CHUNK, CHUNK), pl.multiple_of(jnp.maximum(b - 1, 0) * CHUNK, CHUNK)


def _mix_out_fwd(o_list, l_list, sgu, x, w_out, g_attn, g_sgu, g_post, comms=()):
    def body(o1, o2, o3, l1, l2, l3, sgu_ref, x_ref, w_ref, ga_ref, gs_ref, gp_ref,
             attn_ref, mixed_ref, y_ref, x1_ref, lse1_ref, lse2_ref, lse3_ref, slabs_a, slabs_b):
        os = [o1[...], _load_view(o2, slabs_a, DILATIONS[1]), _load_view(o3, slabs_b, DILATIONS[2])]
        ls = [l1[...], _load_view(l2, slabs_a, DILATIONS[1], HEAD_W), _load_view(l3, slabs_b, DILATIONS[2], HEAD_W)]
        m = jnp.maximum(jnp.maximum(ls[0], ls[1]), ls[2])
        es = [jnp.exp(l - m) for l in ls]
        den = es[0] + es[1] + es[2]
        spread = _head_spread()
        attn = sum(_dot_select(e / den, spread) * o for e, o in zip(es, os))
        attn_ref[...] = attn
        lse = m + jnp.log(den)
        lse1_ref[...] = lse
        _store_view(lse, lse2_ref, slabs_a, DILATIONS[1])
        _store_view(lse, lse3_ref, slabs_b, DILATIONS[2])
        ah, _ = _rms_stats(attn)
        sh, _ = _rms_stats(sgu_ref[...])
        mixed = jnp.concatenate([ah * ga_ref[...], sh * gs_ref[...]], axis=1).astype(BF16)
        mixed_ref[...] = mixed
        y = _dot(mixed[:, 0:OUT_S], w_ref[0])
        for s in range(1, N_SHARD):
            y = y + _dot(mixed[:, s * OUT_S:(s + 1) * OUT_S], w_ref[s])
        y_ref[...] = y
        yh, _ = _rms_stats(y)
        x1_ref[...] = x_ref[...] + yh * gp_ref[...]

    return _pcall(
        body, name="mix_out_fwd", grid=(SEQ // TM,),
        in_specs=[_view_rows(dil) for dil in DILATIONS] + [_view_rows(dil, HEAD_W) for dil in DILATIONS]
        + [_rows(512), _rows(D_MODEL), _resident((N_SHARD, OUT_S, D_MODEL)),
           _resident((1, 512)), _resident((1, 512)), _resident((1, D_MODEL))],
        out_specs=[_rows(512), _rows(D_MODEL), _rows(D_MODEL), _rows(D_MODEL)]
        + [_view_rows(dil, HEAD_W) for dil in DILATIONS],
        out_shape=[_sds((SEQ, 512), F32), _sds((SEQ, D_MODEL), BF16), _sds((SEQ, D_MODEL), F32),
                   _sds((SEQ, D_MODEL), F32)] + [_view_shape(dil, F32, HEAD_W) for dil in DILATIONS],
        scratch_shapes=[_slab_scratch(), _slab_scratch()],
        args=(*o_list, *l_list, sgu, x, w_out, g_attn, g_sgu, g_post), comms=comms)


def _ffn_fwd_bwd(x1, target, w_gate, w_up, w_down, g_pre, g_post, comms=()):
    def body(x1_ref, t_ref, wg_ref, wu_ref, wd_ref, gpf_ref, gpo_ref,
             h2_ref, a_ref, dg_ref, dup_ref, df_ref, dx1_ref, loss_ref, dgpf_ref, dgpo_ref, g_scr, up_scr):
        @pl.when(pl.program_id(0) == 0)
        def _():
            loss_ref[...] = jnp.zeros_like(loss_ref)
            dgpf_ref[...] = jnp.zeros_like(dgpf_ref)
            dgpo_ref[...] = jnp.zeros_like(dgpo_ref)

        x1 = x1_ref[...]
        gpf = gpf_ref[...]
        gpo = gpo_ref[...]
        xh, r = _rms_stats(x1)
        h2 = (xh * gpf).astype(BF16)
        h2_ref[...] = h2
        f = jnp.zeros((TM_FFN, D_MODEL), F32)
        for c0, c1 in FF_CHUNKS:
            g = _dot_nt(h2, wg_ref[c0:c1, :])
            up = _dot_nt(h2, wu_ref[c0:c1, :])
            g_scr[:, c0:c1] = g
            up_scr[:, c0:c1] = up
            a = (g * _sigmoid(g) * up).astype(BF16)
            a_ref[:, c0:c1] = a
            f = f + _dot(a, wd_ref[c0:c1, :])
        fh, rf = _rms_stats(f)
        diff = x1 + fh * gpo - t_ref[...]
        loss_ref[...] += jnp.sum(diff * diff, axis=0, keepdims=True)
        dout = diff * np.float32(1.0 / D_MODEL)
        df, dgpo = _rms_bwd(fh, rf, gpo, dout)
        dgpo_ref[...] += dgpo
        dfb = df.astype(BF16)
        df_ref[...] = dfb
        dh2 = jnp.zeros((TM_FFN, D_MODEL), F32)
        for c0, c1 in FF_CHUNKS:
            da = _dot_nt(dfb, wd_ref[c0:c1, :])
            g = g_scr[:, c0:c1]
            up = up_scr[:, c0:c1]
            sg = _sigmoid(g)
            dup = (da * (g * sg)).astype(BF16)
            dg = (da * up * (sg * (1.0 + g * (1.0 - sg)))).astype(BF16)
            dg_ref[:, c0:c1] = dg
            dup_ref[:, c0:c1] = dup
            dh2 = dh2 + _dot(dg, wg_ref[c0:c1, :]) + _dot(dup, wu_ref[c0:c1, :])
        dx, dgpf = _rms_bwd(xh, r, gpf, dh2)
        dgpf_ref[...] += dgpf
        dx1_ref[...] = dout + dx

    return _pcall(
        body, name="ffn_fwd_bwd", grid=(SEQ // TM_FFN,),
        in_specs=[_rows(D_MODEL, TM_FFN), _rows(D_MODEL, TM_FFN), _resident((FF, D_MODEL)),
                  _resident((FF, D_MODEL)), _resident((FF, D_MODEL)),
                  _resident((1, D_MODEL)), _resident((1, D_MODEL))],
        out_specs=[_rows(D_MODEL, TM_FFN), _rows(FF, TM_FFN), _rows(FF, TM_FFN), _rows(FF, TM_FFN),
                   _rows(D_MODEL, TM_FFN), _rows(D_MODEL, TM_FFN), _acc(D_MODEL), _acc(D_MODEL), _acc(D_MODEL)],
        out_shape=[_sds((SEQ, D_MODEL), BF16), _sds((SEQ, FF), BF16), _sds((SEQ, FF), BF16),
                   _sds((SEQ, FF), BF16), _sds((SEQ, D_MODEL), BF16), _sds((SEQ, D_MODEL), F32),
                   _sds((1, D_MODEL), F32), _sds((1, D_MODEL), F32), _sds((1, D_MODEL), F32)],
        scratch_shapes=[pltpu.VMEM((TM_FFN, FF), F32), pltpu.VMEM((TM_FFN, FF), F32)],
        args=(x1, target, w_gate, w_up, w_down, g_pre, g_post), comms=comms)


def _wgrad(a, b, a_spec, b_spec, out_block, name, comms=()):
    def body(a_ref, b_ref, o_ref):
        av = a_ref[0] if len(a_ref.shape) == 3 else a_ref[...]
        bv = b_ref[0] if len(b_ref.shape) == 3 else b_ref[...]
        o_ref[0] = _dot_tn(av, bv)

    return _pcall(
        body, name=name, grid=(N_SHARD,),
        in_specs=[a_spec, b_spec],
        out_specs=pl.BlockSpec((1,) + out_block, lambda s: (s, 0, 0)),
        out_shape=_sds((N_SHARD,) + out_block, F32),
        args=(a, b), comms=comms)


def _outproj_bwd(dx1, y, attn, sgu, w_out, g_post, g_attn, g_sgu, u, vs, lg, lb, w_sp, b_t, comms=(), after=()):
    sgu_bwd = _sgu_bwd_body()

    def body(dx1_ref, y_ref, attn_ref, sgu_ref, w_ref, gp_ref, ga_ref, gs_ref, u_ref, vs_ref, lg_ref, lb_ref,
             wsp_ref, bt_ref, dy_ref, du_ref, dvs_ref, dgp_ref, dga_ref, dgs_ref, dw_ref, db_ref, dlg_ref, dlb_ref,
             *rest):
        dattn_refs, delta_refs, (slabs_a, slabs_b, dsgu_ref, dbias_scr) = rest[0:3], rest[3:6], rest[6:]

        @pl.when(pl.program_id(0) == 0)
        def _():
            dgp_ref[...] = jnp.zeros_like(dgp_ref)
            dga_ref[...] = jnp.zeros_like(dga_ref)
            dgs_ref[...] = jnp.zeros_like(dgs_ref)

        yh, ry = _rms_stats(y_ref[...])
        dy, dgp = _rms_bwd(yh, ry, gp_ref[...], dx1_ref[...])
        dgp_ref[...] += dgp
        dyb = dy.astype(BF16)
        dy_ref[...] = dyb
        dmixed = jnp.concatenate([_dot_nt(dyb, w_ref[s]) for s in range(N_SHARD)], axis=1)
        attn = attn_ref[...]
        ah, ra = _rms_stats(attn)
        dattn, dga = _rms_bwd(ah, ra, ga_ref[...], dmixed[:, 0:512])
        dga_ref[...] += dga
        sh, rs = _rms_stats(sgu_ref[...])
        dsgu, dgs = _rms_bwd(sh, rs, gs_ref[...], dmixed[:, 512:1024])
        dgs_ref[...] += dgs
        dsgu_ref[...] = dsgu
        delta = _dot_select(dattn * attn, _head_sum())
        dattn_refs[0][...] = dattn.astype(BF16)
        delta_refs[0][...] = delta
        for i, dil in enumerate(DILATIONS[1:]):
            _store_view(dattn, dattn_refs[i + 1], slabs_a, dil)
            _store_view(delta, delta_refs[i + 1], slabs_b, dil)
        sgu_bwd(u_ref, vs_ref, dsgu_ref, lg_ref, lb_ref, wsp_ref, bt_ref,
                du_ref, dvs_ref, dw_ref, db_ref, dlg_ref, dlb_ref, dbias_scr)

    return _pcall(
        body, name="outproj_sgu_bwd", grid=(SEQ // TM,),
        in_specs=[_rows(D_MODEL), _rows(D_MODEL), _rows(512), _rows(512), _resident((N_SHARD, OUT_S, D_MODEL)),
                  _resident((1, D_MODEL)), _resident((1, 512)), _resident((1, 512)),
                  _rows(SGU_W), _rows(SGU_W), _resident((1, SGU_W)), _resident((1, SGU_W)),
                  _resident((SGU_GROUPS, CHUNK, CHUNK)), _resident((CHUNK, SGU_GROUPS))],
        out_specs=[_rows(D_MODEL), _rows(SGU_W), _rows(SGU_W), _acc(D_MODEL), _acc(512), _acc(512),
                   pl.BlockSpec((SGU_GROUPS, CHUNK, CHUNK), lambda i: (0, 0, 0)), _acc(CHUNK, SGU_GROUPS),
                   _acc(SGU_W), _acc(SGU_W)]
        + [_view_rows(dil) for dil in DILATIONS] + [_view_rows(dil, HEAD_W) for dil in DILATIONS],
        out_shape=[_sds((SEQ, D_MODEL), BF16), _sds((SEQ, SGU_W), BF16), _sds((SEQ, SGU_W), BF16),
                   _sds((1, D_MODEL), F32), _sds((1, 512), F32), _sds((1, 512), F32),
                   _sds((SGU_GROUPS, CHUNK, CHUNK), F32), _sds((SGU_GROUPS, CHUNK), F32),
                   _sds((1, SGU_W), F32), _sds((1, SGU_W), F32)]
        + [_view_shape(dil, BF16) for dil in DILATIONS] + [_view_shape(dil, F32, HEAD_W) for dil in DILATIONS],
        scratch_shapes=[_slab_scratch(), _slab_scratch(), pltpu.VMEM((TM, SGU_W), F32),
                        pltpu.VMEM((CHUNK, SGU_W), F32)],
        args=(dx1, y, attn, sgu, w_out, g_post, g_attn, g_sgu, u, vs, lg, lb, w_sp, b_t), comms=comms, after=after)


def _sgu_bwd_body():
    nsteps = SEQ // TM

    def body(u_ref, vs_ref, ds_ref, lg_ref, lb_ref, w_ref, bt_ref,
             du_ref, dvs_ref, dw_ref, db_ref, dlg_ref, dlb_ref, dbias_scr):
        i = pl.program_id(0)

        @pl.when(i == 0)
        def _():
            dw_ref[...] = jnp.zeros_like(dw_ref)
            dlg_ref[...] = jnp.zeros_like(dlg_ref)
            dlb_ref[...] = jnp.zeros_like(dlb_ref)
            dbias_scr[...] = jnp.zeros_like(dbias_scr)

        wm = _masked_spatial(w_ref)
        ones_g = _group_ones()
        bias_full = _dot_exact(bt_ref[...], ones_g)
        u = u_ref[...]
        vs = vs_ref[...]
        lg = lg_ref[...]
        gu, xh, rstd, vnb, mixed, cdf_u, cdf_vs = _sgu_core(u, vs, lg, lb_ref[...], wm, bias_full)
        dsgu = ds_ref[...]
        du_ref[...] = (dsgu * mixed * _gelu_grad(u, cdf_u)).astype(BF16)
        dmixed = dsgu * gu
        left = _left_half()
        dvn_rows = []
        for c in range(TM // CHUNK):
            rs = slice(c * CHUNK, (c + 1) * CHUNK)
            dm_c = dmixed[rs, :]
            dbias_scr[...] += dm_c
            pieces = []
            for p in range(4):
                ls = slice(p * 128, (p + 1) * 128)
                dmp = dm_c[:, ls]
                vp = vnb[rs, ls]
                dmb = dmp.astype(BF16)
                zero = jnp.zeros_like(dmb)
                dw_ref[2 * p] += _dot_nt(jnp.where(left, dmb, zero), vp)
                dw_ref[2 * p + 1] += _dot_nt(jnp.where(left, zero, dmb), vp)
                pieces.append(jnp.where(left, _dot_tn(wm[2 * p], dmb), _dot_tn(wm[2 * p + 1], dmb)))
            dvn_rows.append(jnp.concatenate(pieces, axis=1))
        dvn = jnp.concatenate(dvn_rows, axis=0)
        dlg_ref[...] += jnp.sum(dvn * xh, axis=0, keepdims=True)
        dlb_ref[...] += jnp.sum(dvn, axis=0, keepdims=True)
        dxh = dvn * lg
        dgv = rstd * (dxh - jnp.mean(dxh, axis=-1, keepdims=True) - xh * jnp.mean(dxh * xh, axis=-1, keepdims=True))
        dvs_ref[...] = (dgv * _gelu_grad(vs, cdf_vs)).astype(BF16)

        @pl.when(i == nsteps - 1)
        def _():
            row = lax.broadcasted_iota(jnp.int32, (CHUNK, CHUNK), 0)
            col = lax.broadcasted_iota(jnp.int32, (CHUNK, CHUNK), 1)
            for g in range(SGU_GROUPS):
                dw_ref[g] = jnp.where(col <= row, dw_ref[g], 0.0)
            db_ref[...] = lax.dot_general(ones_g, dbias_scr[...], (((1,), (1,)), ((), ())),
                                          preferred_element_type=F32, precision=lax.Precision.HIGHEST)

    return body


def _attn_bwd(qv, kv, vv, dov, deltav, lsev, dil, comms=(), after=()):
    seg = SEQ // dil
    nblk = seg // CHUNK
    rps = 4 if nblk == 1 else 1

    def body(q_ref, k_ref, v_ref, do_ref, dl_ref, lse_ref, dq_ref, dk_ref, dv_ref, dk_wait, dv_wait):
        left = _left_half()
        m_cur, m_prev = _block_masks()
        sides = tuple(enumerate((left, ~left)))
        zero = jnp.zeros((CHUNK, CHUNK), BF16)

        def first(rr, b, both):
            r0, rp = _attn_rows(b)
            keys = pl.ds(rp, 2 * CHUNK) if both else pl.ds(r0, CHUNK)
            tiles, heads = [], []
            for hp in range(4):
                ls = slice(rr * ATTN_W + hp * 128, rr * ATTN_W + (hp + 1) * 128)
                qp = q_ref[pl.ds(r0, CHUNK), ls]
                dop = do_ref[pl.ds(r0, CHUNK), ls]
                k2 = k_ref[keys, ls]
                v2 = v_ref[keys, ls]
                tiles.append((ls, k2))
                for _, hm in sides:
                    qh = jnp.where(hm, qp, zero)
                    doh = jnp.where(hm, dop, zero)
                    heads.append((qh, doh, _dot_nt(k2, qh), _dot_nt(v2, doh)))
            return tiles, heads

        def second(rr, b, heads, both):
            r0, _ = _attn_rows(b)
            ok = jnp.concatenate([m_cur, m_prev], axis=0) if both else m_prev
            lanes = slice(rr * HEAD_W, (rr + 1) * HEAD_W)
            lse_t = lse_ref[pl.ds(r0, CHUNK), lanes].T
            dl_t = dl_ref[pl.ds(r0, CHUNK), lanes].T
            out = []
            for i, (_, _, s_t, dp_t) in enumerate(heads):
                p = jnp.exp(jnp.where(ok, s_t - lse_t[16 * i:16 * i + 1, :], NEG))
                out.append((p.astype(BF16), (p * (dp_t - dl_t[16 * i:16 * i + 1, :])).astype(BF16)))
            return out

        def third(rr, b, tiles, heads, probs, both):
            r0, rp = _attn_rows(b)
            nk = 2 * CHUNK if both else CHUNK
            left_k = _lane_left(nk)
            zero_k = jnp.zeros((nk, CHUNK), BF16)
            for hp, (ls, k2) in enumerate(tiles):
                dq = jnp.zeros((CHUNK, CHUNK), F32)
                dk2 = jnp.zeros((nk, CHUNK), F32)
                dv2 = jnp.zeros((nk, CHUNK), F32)
                for side in range(2):
                    qh, doh, _, _ = heads[2 * hp + side]
                    p, ds = probs[2 * hp + side]
                    dq = dq + _dot_tn(ds, jnp.where(left_k if side == 0 else ~left_k, k2, zero_k))
                    dk2 = dk2 + _dot(ds, qh)
                    dv2 = dv2 + _dot(p, doh)
                dq_ref[pl.ds(r0, CHUNK), ls] = dq.astype(dq_ref.dtype)
                if nblk == 1:
                    dk_ref[pl.ds(r0, CHUNK), ls] = dk2.astype(dk_ref.dtype)
                    dv_ref[pl.ds(r0, CHUNK), ls] = dv2.astype(dv_ref.dtype)
                elif both:
                    dk_ref[pl.ds(rp, CHUNK), ls] = (dk_wait[:, ls] + dk2[0:CHUNK]).astype(dk_ref.dtype)
                    dv_ref[pl.ds(rp, CHUNK), ls] = (dv_wait[:, ls] + dv2[0:CHUNK]).astype(dv_ref.dtype)
                    dk_wait[:, ls] = dk2[CHUNK:]
                    dv_wait[:, ls] = dv2[CHUNK:]
                else:
                    dk_wait[:, ls] = dk2
                    dv_wait[:, ls] = dv2

        def run(units, both):
            data = [first(rr, b, both) for rr, b in units]
            probs = [second(rr, b, heads, both) for (rr, b), (_, heads) in zip(units, data)]
            for (rr, b), (tiles, heads), pr in zip(units, data, probs):
                third(rr, b, tiles, heads, pr, both)

        run([(rr, 0) for rr in range(rps)], False)
        if nblk > 1:
            def one(b, carry):
                run([(0, b)], True)
                return carry

            lax.fori_loop(1, nblk, one, 0)
            last = (nblk - 1) * CHUNK
            dk_ref[last:last + CHUNK, :] = dk_wait[...].astype(dk_ref.dtype)
            dv_ref[last:last + CHUNK, :] = dv_wait[...].astype(dv_ref.dtype)

    spec = pl.BlockSpec((seg, rps * ATTN_W), lambda r: (0, r))
    return _pcall(
        body, name=f"attn_bwd_d{dil}", grid=(dil // rps,),
        in_specs=[spec] * 4 + [pl.BlockSpec((seg, rps * HEAD_W), lambda r: (0, r))] * 2, out_specs=[spec] * 3,
        out_shape=[_sds((seg, dil * ATTN_W), BF16)] * 3,
        scratch_shapes=[pltpu.VMEM((CHUNK, ATTN_W), F32), pltpu.VMEM((CHUNK, ATTN_W), F32)],
        args=(qv, kv, vv, dov, deltav, lsev), comms=comms, after=after)


def _inproj_bwd(dqs, dks, dvs, du, dvs_sgu, pos, x, dx1, w_in, g_pre, comms=()):
    def body(dq1, dq2, dq3, dk1, dk2, dk3, dv1, dv2, dv3, du_ref, dvs_ref, pos_ref, x_ref, dx1_ref, w_ref, g_ref,
             dproj_ref, gx_ref, dg_ref, slabs_a, slabs_b):
        @pl.when(pl.program_id(0) == 0)
        def _():
            dg_ref[...] = jnp.zeros_like(dg_ref)

        def total(r1, r2, r3):
            return r1[...] + _load_view(r2, slabs_a, DILATIONS[1]) + _load_view(r3, slabs_b, DILATIONS[2])

        tabs = _rot_tables(pos_ref[...])
        groups = {3: du_ref[...], 4: dvs_ref[...]}
        dh = jnp.zeros((TM, D_MODEL), F32)
        for g in (3, 4, 0, 1, 2):
            if g == 0:
                groups[g] = _rope_bwd(total(dq1, dq2, dq3) * np.float32(ATTN_SCALE), tabs).astype(BF16)
            elif g == 1:
                groups[g] = _rope_bwd(total(dk1, dk2, dk3), tabs).astype(BF16)
            elif g == 2:
                groups[g] = total(dv1, dv2, dv3).astype(BF16)
            dproj_ref[:, 512 * g:512 * (g + 1)] = groups[g]
            off = 0
            for s, a, b in _in_pieces(g):
                dh = dh + _dot_nt(groups[g][:, off:off + b - a], w_ref[s, :, a:b])
                off += b - a
        g = g_ref[...]
        xh, r = _rms_stats(x_ref[...])
        dx, dg = _rms_bwd(xh, r, g, dh)
        dg_ref[...] += dg
        gx_ref[...] = dx1_ref[...] + dx

    return _pcall(
        body, name="inproj_bwd", grid=(SEQ // TM,),
        in_specs=[_view_rows(dil) for dil in DILATIONS] * 3
        + [_rows(512), _rows(512), _rows(1), _rows(D_MODEL), _rows(D_MODEL),
           _resident((N_SHARD, D_MODEL, IN_S)), _resident((1, D_MODEL))],
        out_specs=[_rows(PROJ_W), _rows(D_MODEL), _acc(D_MODEL)],
        out_shape=[_sds((SEQ, PROJ_W), BF16), _sds((SEQ, D_MODEL), F32), _sds((1, D_MODEL), F32)],
        scratch_shapes=[_slab_scratch(), _slab_scratch()],
        args=(*dqs, *dks, *dvs, du, dvs_sgu, pos, x, dx1, w_in, g_pre), comms=comms)


def _coords():
    return lax.axis_index("x"), lax.axis_index("y"), lax.axis_index("c")


def _other_chips(x, y):
    return [(1 - x, y), (x, 1 - y), (1 - x, 1 - y)]


WEIGHTS = ("pre_mix_norm", "w_in", "sgu_ln_gain", "sgu_ln_bias", "sgu_w_spatial", "sgu_b_spatial", "attn_out_norm",
           "sgu_out_norm", "w_out", "post_mix_norm", "pre_ffn_norm", "w_gate", "w_up", "w_down", "post_ffn_norm")
SMALL = ("pre_mix_norm", "post_mix_norm", "pre_ffn_norm", "post_ffn_norm", "sgu_ln_gain", "sgu_ln_bias",
         "attn_out_norm", "sgu_out_norm", "sgu_w_spatial", "sgu_b_spatial")


def _remote(src, dst, send_sem, recv_sem, to):
    return pltpu.make_async_remote_copy(src_ref=src, dst_ref=dst, send_sem=send_sem, recv_sem=recv_sem,
                                        device_id=to, device_id_type=MESH)


def _halves(a):
    *lead, rows, cols = a.shape
    return a.reshape(*lead, 2, rows // 2, cols)


def _gather_ici(shards):
    n = len(shards)

    def desc(ins, outs, ss, rs, j, w, landed):
        x, y, c = _coords()
        cx, cy = _other_chips(x, y)[j]
        shard = 2 * cx + cy if landed else 2 * x + y
        return _remote(ins[w].at[c], outs[w].at[shard, c], ss.at[j * n + w], rs.at[j * n + w], (cx, cy, c))

    def start(ins, outs, ss, rs):
        for j in range(3):
            for w in range(n):
                desc(ins, outs, ss, rs, j, w, False).start()

    def finish(ins, outs, ss, rs):
        for j in range(3):
            for w in range(n):
                desc(ins, outs, ss, rs, j, w, True).wait_recv()
                desc(ins, outs, ss, rs, j, w, False).wait_send()

    return _Comm(shards, [_sds((N_SHARD,) + s.shape, s.dtype) for s in shards], 3 * n, start, finish)


def _gather_pass(fulls):
    n = len(fulls)

    def desc(bufs, ss, rs, j, w, landed):
        x, y, c = _coords()
        cx, cy = _other_chips(x, y)[j]
        shard = 2 * cx + cy
        return _remote(bufs[w].at[shard, c], bufs[w].at[shard, 1 - c if landed else c],
                       ss.at[j * n + w], rs.at[j * n + w], (x, y, 1 - c))

    def start(ins, outs, ss, rs):
        for j in range(3):
            for w in range(n):
                desc(outs, ss, rs, j, w, False).start()

    def finish(ins, outs, ss, rs):
        for j in range(3):
            for w in range(n):
                desc(outs, ss, rs, j, w, True).wait_recv()
                desc(outs, ss, rs, j, w, False).wait_send()

    return _Comm(fulls, [_sds(f.shape, f.dtype) for f in fulls], 3 * n, start, finish, aliased=True)


def _rs_sibling(gws):
    n = len(gws)

    def desc(ins, outs, ss, rs, w):
        x, y, c = _coords()
        return _remote(ins[w].at[:, 1 - c], outs[w], ss.at[w], rs.at[w], (x, y, 1 - c))

    def start(ins, outs, ss, rs):
        for w in range(n):
            desc(ins, outs, ss, rs, w).start()

    def finish(ins, outs, ss, rs):
        for w in range(n):
            desc(ins, outs, ss, rs, w).wait()

    out_shape = [_sds((N_SHARD, g.shape[1] // 2, g.shape[2]), g.dtype) for g in gws]
    return _Comm([_halves(g) for g in gws], out_shape, n, start, finish)


def _rs_chips(pbs):
    n = len(pbs)

    def desc(ins, outs, ss, rs, j, w):
        x, y, c = _coords()
        cx, cy = _other_chips(x, y)[j]
        return _remote(ins[w].at[2 * cx + cy], outs[w].at[j], ss.at[j * n + w], rs.at[j * n + w], (cx, cy, c))

    def start(ins, outs, ss, rs):
        for j in range(3):
            for w in range(n):
                desc(ins, outs, ss, rs, j, w).start()

    def finish(ins, outs, ss, rs):
        for j in range(3):
            for w in range(n):
                desc(ins, outs, ss, rs, j, w).wait()

    return _Comm(pbs, [_sds((3,) + p.shape[1:], p.dtype) for p in pbs], 3 * n, start, finish)


def _rs_join(halves):
    n = len(halves)

    def desc(bufs, ss, rs, w, landed):
        x, y, c = _coords()
        return _remote(bufs[w].at[c], bufs[w].at[1 - c if landed else c], ss.at[w], rs.at[w], (x, y, 1 - c))

    def start(ins, outs, ss, rs):
        for w in range(n):
            desc(outs, ss, rs, w, False).start()

    def finish(ins, outs, ss, rs):
        for w in range(n):
            desc(outs, ss, rs, w, True).wait_recv()
            desc(outs, ss, rs, w, False).wait_send()

    return _Comm(halves, [_sds(h.shape, h.dtype) for h in halves], n, start, finish, aliased=True)


def _small_exchange(buf):
    def desc(ins, outs, ss, rs, k, landed):
        x, y, c = _coords()
        px = 1 - x if (k >> 2) & 1 else x
        py = 1 - y if (k >> 1) & 1 else y
        pc = 1 - c if k & 1 else c
        slot = 4 * px + 2 * py + pc if landed else 4 * x + 2 * y + c
        return _remote(ins[0], outs[0].at[slot], ss.at[k - 1], rs.at[k - 1], (px, py, pc))

    def start(ins, outs, ss, rs):
        for k in range(1, 8):
            desc(ins, outs, ss, rs, k, False).start()

    def finish(ins, outs, ss, rs):
        for k in range(1, 8):
            desc(ins, outs, ss, rs, k, True).wait_recv()
            desc(ins, outs, ss, rs, k, False).wait_send()

    return _Comm([buf], [_sds((8,) + buf.shape, buf.dtype)], 7, start, finish)


HBM = pl.BlockSpec(memory_space=pltpu.HBM)
SEM = pl.BlockSpec(memory_space=pltpu.SEMAPHORE)
DATAFLOW = pltpu.SideEffectType.DATAFLOW_SIDE_EFFECTING


def _split_starts(name, comms, after):
    srcs = [[pltpu.with_memory_space_constraint(s, pltpu.HBM) for s in c.args] for c in comms]
    lands = [[pltpu.with_memory_space_constraint(lax.empty(o.shape, o.dtype), pltpu.HBM) for o in c.out_shape]
             for c in comms]
    bufs = [b for k in range(len(comms)) for b in srcs[k] + lands[k]]
    nb, nc = len(bufs), len(comms)

    def body(*refs):
        sems = refs[nb + 1:nb + 1 + 2 * nc]
        off = 0
        for k, c in enumerate(comms):
            ns, nl = len(srcs[k]), len(lands[k])
            c.start(refs[off:off + ns], refs[off + ns:off + ns + nl], sems[2 * k], sems[2 * k + 1])
            off += ns + nl
        refs[-1][...] = jnp.zeros_like(refs[-1])

    res = pl.pallas_call(
        body, name=name,
        out_shape=(*[pltpu.SemaphoreType.DMA((c.n_sems,)) for c in comms for _ in range(2)],
                   *[pltpu.HBM(b.shape, b.dtype) for b in bufs], _sds((8, 128), F32)),
        in_specs=[HBM] * nb + [ANY],
        out_specs=(*[SEM] * (2 * nc), *[HBM] * nb, pl.BlockSpec(memory_space=pltpu.VMEM)),
        input_output_aliases={i: 2 * nc + i for i in range(nb)},
        compiler_params=pltpu.CompilerParams(has_side_effects=DATAFLOW),
    )(*bufs, after)
    states, off = [], 2 * nc
    for k in range(nc):
        ns, nl = len(srcs[k]), len(lands[k])
        states.append((res[2 * k], res[2 * k + 1], list(res[off:off + ns]), list(res[off + ns:off + ns + nl])))
        off += ns + nl
    return states, res[-1]


def _split_wait(name, comm, send_sems, recv_sems, srcs, lands, after):
    ns, nb = len(srcs), len(lands)
    after = list(after) if isinstance(after, (list, tuple)) else [after]

    def body(*refs):
        comm.finish(refs[:ns], refs[ns:ns + nb], refs[ns + nb], refs[ns + nb + 1])

    res = pl.pallas_call(
        body, name=name,
        out_shape=tuple(pltpu.HBM(b.shape, b.dtype) for b in srcs + lands),
        in_specs=[HBM] * (ns + nb) + [SEM, SEM] + [ANY] * len(after), out_specs=tuple([HBM] * (ns + nb)),
        input_output_aliases={i: i for i in range(ns + nb)},
        compiler_params=pltpu.CompilerParams(has_side_effects=DATAFLOW),
    )(*srcs, *lands, send_sems, recv_sems, *after)
    return list(res[ns:])


LOSS_ROW = "loss_cols"
SMALL_EARLY = ("post_mix_norm", "pre_ffn_norm", "post_ffn_norm", "sgu_ln_gain", "sgu_ln_bias", "attn_out_norm",
               "sgu_out_norm", "sgu_b_spatial", LOSS_ROW)
SMALL_LATE = ("pre_mix_norm",)
SMALL_WSP = "sgu_w_spatial"


def _pack(d, names, rows):
    flat = [d[n].reshape(-1) for n in names]
    used = sum(f.shape[0] for f in flat)
    flat.append(jnp.zeros((rows * 1024 - used,), F32))
    return jnp.concatenate(flat).reshape(rows, 1024)


def _unpack(buf, names, shapes):
    flat = buf.reshape(-1)
    out, off = {}, 0
    for n in names:
        size = int(np.prod(shapes[n]))
        out[n] = flat[off:off + size].reshape(shapes[n])
        off += size
    return out


def _chip_sums(gws, recvs, shard_core):
    n = len(gws)
    _, rows, cols = gws[0].shape
    hw = rows // 2

    def body(sc_ref, *refs):
        for k in range(n):
            refs[2 * n + k][...] = (refs[k][...] + refs[n + k][...]).astype(BF16)

    def other(s, sc):
        return jnp.where(s >= sc[0], s + 1, s)

    mine = pl.BlockSpec((1, hw, cols), lambda s, sc: (other(s, sc), sc[1], 0))
    plain = pl.BlockSpec((1, hw, cols), lambda s, sc: (other(s, sc), 0, 0))
    return pl.pallas_call(
        body, name="rs_chip_sums",
        grid_spec=pltpu.PrefetchScalarGridSpec(num_scalar_prefetch=1, grid=(N_SHARD - 1,),
                                               in_specs=[mine] * n + [plain] * n, out_specs=[plain] * n),
        out_shape=[_sds((N_SHARD, hw, cols), BF16)] * n,
        compiler_params=_seq_params(),
    )(shard_core, *gws, *recvs)


def _final_sums(gws, recv_sibs, recv_chips, shard_core):
    n = len(gws)
    halves = [(g.shape[1] // 2, g.shape[2]) for g in gws]

    def body(sc_ref, *refs):
        for k in range(n):
            acc = refs[k][0] + refs[n + k][0]
            for j in range(3):
                acc = acc + refs[2 * n + k][j].astype(F32)
            refs[3 * n + k][0] = acc

    def specs(lead, index):
        return [pl.BlockSpec((lead, hw, cols), index) for hw, cols in halves]

    return pl.pallas_call(
        body, name="rs_final_sums",
        grid_spec=pltpu.PrefetchScalarGridSpec(
            num_scalar_prefetch=1, grid=(1,),
            in_specs=specs(1, lambda i, sc: (sc[0], sc[1], 0)) + specs(1, lambda i, sc: (sc[0], 0, 0))
            + specs(3, lambda i, sc: (0, 0, 0)),
            out_specs=specs(1, lambda i, sc: (sc[1], 0, 0))),
        out_shape=[_sds((2, hw, cols), F32) for hw, cols in halves],
        compiler_params=_seq_params(),
    )(shard_core, *gws, *recv_sibs, *recv_chips)


ADAM_BLOCKS = 4


def _adamw_multi(ws, gs, ms, vs, name, after=()):
    n = len(ws)

    def body(*refs):
        outs = refs[4 * n + len(after):]
        for k in range(n):
            g = refs[n + k][...]
            d, m, v = _adam_math(refs[k][...], g, refs[2 * n + k][...], refs[3 * n + k][...])
            outs[4 * k][...], outs[4 * k + 1][...], outs[4 * k + 2][...], outs[4 * k + 3][...] = g, d, m, v

    specs = [pl.BlockSpec((w.shape[0] // ADAM_BLOCKS, w.shape[1]), lambda i: (i, 0)) for w in ws]
    res = pl.pallas_call(
        body, name=name, grid=(ADAM_BLOCKS,), in_specs=specs * 4 + [ANY] * len(after),
        out_specs=[s for s in specs for _ in range(4)],
        out_shape=[_sds(w.shape, F32) for w in ws for _ in range(4)],
        compiler_params=_seq_params(),
    )(*ws, *gs, *ms, *vs, *after)
    return [tuple(res[4 * k:4 * k + 4]) for k in range(n)]


def _wgrad_ff(a_list, b, name, comms=()):
    n = len(a_list)
    cols = 256

    def body(*refs):
        bv = refs[n][...]
        for k in range(n):
            refs[n + 1 + k][...] = _dot_tn(refs[k][...], bv)

    res = _pcall(
        body, name=name, grid=(FF // cols,),
        in_specs=[pl.BlockSpec((SEQ, cols), lambda j: (0, j))] * n
        + [pl.BlockSpec((SEQ, D_MODEL), lambda j: (0, 0), pipeline_mode=pl.Buffered(1))],
        out_specs=[pl.BlockSpec((cols, D_MODEL), lambda j: (j, 0))] * n,
        out_shape=[_sds((FF, D_MODEL), F32)] * n, args=(*a_list, b), comms=comms)
    mine, theirs = res if comms else (res, None)
    mine = [m.reshape(N_SHARD, FF_S, D_MODEL) for m in mine]
    return (mine, theirs) if comms else mine


def _comm_only(name, comms):
    return _pcall(lambda: None, name=name, grid=(1,), in_specs=[], out_specs=[], out_shape=[], args=(),
                  comms=comms)[1]


def _adam_math(w, g, m, v):
    m = ADAM_B1 * m + (1.0 - ADAM_B1) * g
    v = ADAM_B2 * v + (1.0 - ADAM_B2) * (g * g)
    m_hat = m / (1.0 - ADAM_B1 ** ADAM_STEP)
    v_hat = v / (1.0 - ADAM_B2 ** ADAM_STEP)
    return -ADAM_LR * (m_hat / (jnp.sqrt(v_hat) + ADAM_EPS) + ADAM_WD * w), m, v


def _adamw_small(own, slots, w, m, v, me):
    rows, cols = own.shape

    def body(me_ref, own_ref, slots_ref, w_ref, m_ref, v_ref, g_ref, d_ref, nm_ref, nv_ref):
        own_v = own_ref[...].astype(F32)
        g = jnp.where(me_ref[0] == 0, own_v, slots_ref[0].astype(F32))
        for i in range(1, 8):
            g = g + jnp.where(me_ref[0] == i, own_v, slots_ref[i].astype(F32))
        g_ref[...] = g
        d_ref[...], nm_ref[...], nv_ref[...] = _adam_math(w_ref[...], g, m_ref[...], v_ref[...])

    flat = pl.BlockSpec((rows, cols), lambda i, me_ref: (0, 0))
    return pl.pallas_call(
        body, name="adamw_small",
        grid_spec=pltpu.PrefetchScalarGridSpec(
            num_scalar_prefetch=1, grid=(1,),
            in_specs=[flat, pl.BlockSpec((8, rows, cols), lambda i, me_ref: (0, 0, 0)), flat, flat, flat],
            out_specs=[flat] * 4),
        out_shape=[_sds((rows, cols), F32)] * 4,
        compiler_params=_seq_params(),
    )(me, own, slots, w, m, v)


def kernel(x, positions, pre_mix_norm, w_in, sgu_ln_gain, sgu_ln_bias, sgu_w_spatial, sgu_b_spatial, attn_out_norm, sgu_out_norm, w_out, post_mix_norm, pre_ffn_norm, w_gate, w_up, w_down, post_ffn_norm, loss_target, m_pre_mix_norm, m_w_in, m_sgu_ln_gain, m_sgu_ln_bias, m_sgu_w_spatial, m_sgu_b_spatial, m_attn_out_norm, m_sgu_out_norm, m_w_out, m_post_mix_norm, m_pre_ffn_norm, m_w_gate, m_w_up, m_w_down, m_post_ffn_norm, v_pre_mix_norm, v_w_in, v_sgu_ln_gain, v_sgu_ln_bias, v_sgu_w_spatial, v_sgu_b_spatial, v_attn_out_norm, v_sgu_out_norm, v_w_out, v_post_mix_norm, v_pre_ffn_norm, v_w_gate, v_w_up, v_w_down, v_post_ffn_norm):
    a = dict(locals())
    cx, cy, cc = _coords()
    s_me = 2 * cx + cy
    shard_core = jnp.stack([s_me, cc]).astype(jnp.int32)
    me = jnp.stack([4 * cx + 2 * cy + cc]).astype(jnp.int32)
    small = {n: (a[n][0] if a[n].ndim > 2 else a[n]) for n in SMALL}
    b_t = small["sgu_b_spatial"].T
    xs, pos, target = x[0], positions.reshape(SEQ, 1), loss_target[0]
    flipped = ("w_gate", "w_up")

    def big(name, n):
        return jnp.swapaxes(a[name], 1, 2)[0] if n in flipped else a[name][0]

    own = {"w_in": _halves(big("w_in", "w_in").astype(BF16))}

    def with_own(full, n):
        full = lax.dynamic_update_slice(full, own[n][None], (s_me, 0, 0, 0))
        return full.reshape((N_SHARD,) + big(n, n).shape)

    ffn = ("w_gate", "w_up", "w_down")
    g_in = _gather_ici([own["w_in"]])
    (s_in,), token = _split_starts("gather_in_start", [g_in], small["pre_mix_norm"])
    for n in ("w_out",) + ffn:
        own[n] = _halves((big(n, n) + token[0:1, 0:1]).astype(BF16))
    g_out, g_ffn = _gather_ici([own["w_out"]]), _gather_ici([own[n] for n in ffn])
    (s_out, s_ffn), token = _split_starts("gather_rest_start", [g_out, g_ffn], token)
    in_lands = _split_wait("gather_in_wait", g_in, *s_in, token)
    ((in_lands,),) = _comm_only("comm_pass_in", [_gather_pass(in_lands)])
    w_in_f = with_own(in_lands, "w_in")
    h, u, vs, sgu, *qkv = _inproj_fwd(xs, pos, small["pre_mix_norm"], w_in_f, small["sgu_ln_gain"],
                                      small["sgu_ln_bias"], small["sgu_w_spatial"], b_t)
    views = [tuple(qkv[3 * i:3 * i + 3]) for i in range(len(DILATIONS))]
    out_lands = _split_wait("gather_out_wait", g_out, *s_out, sgu)
    o_list, l_list = [], []
    for dil, (qv, kv, vv) in zip(DILATIONS, views):
        if dil == 1:
            (o, l), ((out_lands,),) = _attn_fwd(qv, kv, vv, dil, comms=[_gather_pass(out_lands)])
        else:
            o, l = _attn_fwd(qv, kv, vv, dil)
        o_list.append(o)
        l_list.append(l)
    w_out_f = with_own(out_lands, "w_out")
    ffn_lands = _split_wait("gather_ffn_wait", g_ffn, *s_ffn, l_list[-1])
    (attn, mixed, y, x1, *lses), (ffn_lands,) = _mix_out_fwd(
        o_list, l_list, sgu, xs, w_out_f, small["attn_out_norm"], small["sgu_out_norm"], small["post_mix_norm"],
        comms=[_gather_pass(ffn_lands)])
    w_gate_f, w_up_f, w_down_f = (with_own(f, n).reshape(FF, D_MODEL) for f, n in zip(ffn_lands, ffn))
    h2, act, dg, dup, df, dx1, loss_cols, d_pre_ffn, d_post_ffn = _ffn_fwd_bwd(
        x1, target, w_gate_f, w_up_f, w_down_f, small["pre_ffn_norm"], small["post_ffn_norm"])

    full_tok = pl.BlockSpec((SEQ, D_MODEL), lambda s: (0, 0), pipeline_mode=pl.Buffered(1))
    gw = {}
    gw["w_gate"], gw["w_up"] = _wgrad_ff([dg, dup], h2, "wgrad_gate_up")
    (gw["w_down"],), ((sib_gate,),) = _wgrad_ff([act], df, "wgrad_down", comms=[_rs_sibling([gw["w_gate"]])])
    (dy, du, dvs_sgu, d_post_mix, d_attn_norm, d_sgu_norm, d_w_sp, d_b_sp, d_ln_gain, d_ln_bias,
     *dviews), ((sib_up, sib_down),) = _outproj_bwd(
        dx1, y, attn, sgu, w_out_f, small["post_mix_norm"], small["attn_out_norm"], small["sgu_out_norm"],
        u, vs, small["sgu_ln_gain"], small["sgu_ln_bias"], small["sgu_w_spatial"], b_t,
        comms=[_rs_sibling([gw["w_up"], gw["w_down"]])])
    sib = {"w_gate": sib_gate, "w_up": sib_up, "w_down": sib_down}
    part = dict(zip(ffn, _chip_sums([gw[n] for n in ffn], [sib[n] for n in ffn], shard_core)))
    gw["w_out"] = _wgrad(mixed, dy, pl.BlockSpec((SEQ, OUT_S), lambda s: (0, s)), full_tok, (OUT_S, D_MODEL),
                         "wgrad_out")
    x_ffn = _rs_chips([part[n] for n in ffn])
    packed_early = _pack({
        "sgu_ln_gain": d_ln_gain, "sgu_ln_bias": d_ln_bias, "sgu_b_spatial": d_b_sp,
        "attn_out_norm": d_attn_norm, "sgu_out_norm": d_sgu_norm, "post_mix_norm": d_post_mix,
        "pre_ffn_norm": d_pre_ffn, "post_ffn_norm": d_post_ffn, LOSS_ROW: loss_cols}, SMALL_EARLY, 8)
    wsp_view = (SGU_GROUPS * CHUNK, CHUNK)
    packed_wsp = d_w_sp.reshape(wsp_view).astype(BF16)
    x_small, x_wsp = _small_exchange(packed_early), _small_exchange(packed_wsp)
    (s_ffn,), token = _split_starts("rs_ffn_start", [x_ffn], small["pre_mix_norm"])

    dqs, dks, dvs = [], [], []
    for i, (dil, (qv, kv, vv)) in enumerate(zip(DILATIONS, views)):
        if dil == 1:
            (dq, dk, dv), ((sib["w_out"],),) = _attn_bwd(qv, kv, vv, dviews[i], dviews[3 + i], lses[i], dil,
                                                        comms=[_rs_sibling([gw["w_out"]])], after=[token])
            (part["w_out"],) = _chip_sums([gw["w_out"]], [sib["w_out"]], shard_core)
            x_out = _rs_chips([part["w_out"]])
            (s_out, s_small, s_wsp), token = _split_starts("rs_out_small_start", [x_out, x_small, x_wsp], token)
        else:
            dq, dk, dv = _attn_bwd(qv, kv, vv, dviews[i], dviews[3 + i], lses[i], dil, after=[token])
        dqs.append(dq)
        dks.append(dk)
        dvs.append(dv)
    half, far, joined = {}, {}, {}
    far.update(zip(ffn, _split_wait("rs_ffn_wait", x_ffn, *s_ffn, dvs[-1])))
    dproj, grad_x, d_pre_mix = _inproj_bwd(dqs, dks, dvs, du, dvs_sgu, pos, xs, dx1, w_in_f, small["pre_mix_norm"])
    (far["w_out"],) = _split_wait("rs_out_wait", x_out, *s_out, grad_x)
    names = ffn + ("w_out",)
    half.update(zip(names, _final_sums([gw[n] for n in names], [sib[n] for n in names], [far[n] for n in names],
                                       shard_core)))
    packed_late = _pack({"pre_mix_norm": d_pre_mix}, SMALL_LATE, 8)
    gw["w_in"], (got, (slots_late,)) = _wgrad(
        h, dproj, full_tok, pl.BlockSpec((SEQ, IN_S), lambda s: (0, s)), (D_MODEL, IN_S), "wgrad_in",
        comms=[_rs_join([half[n] for n in names]), _small_exchange(packed_late)])
    joined.update(zip(names, got))
    ((sib["w_in"],),) = _comm_only("comm_rs_sibling_in", [_rs_sibling([gw["w_in"]])])
    (part["w_in"],) = _chip_sums([gw["w_in"]], [sib["w_in"]], shard_core)
    x_in = _rs_chips([part["w_in"]])
    (s_in,), token = _split_starts("rs_in_start", [x_in], small["pre_mix_norm"])

    grads, deltas, new_m, new_v = {}, {}, {}, {}

    def record(n, outs):
        grads[n], deltas[n], new_m[n], new_v[n] = (
            jnp.swapaxes(o[None], 1, 2) if n in flipped else o[None] for o in outs)

    def update(names, name, after):
        for n, outs in zip(names, _adamw_multi(
                [big(n, n) for n in names], [joined[n].reshape(big(n, n).shape) for n in names],
                [big("m_" + n, n) for n in names], [big("v_" + n, n) for n in names], name, after)):
            record(n, outs)

    update(names, "adamw_ffn_out", [token])
    (slots_early,) = _split_wait("small_early_wait", x_small, *s_small, new_v["w_out"])
    (slots_wsp,) = _split_wait("small_wsp_wait", x_wsp, *s_wsp, slots_early)
    (far["w_in"],) = _split_wait("rs_in_wait", x_in, *s_in, slots_wsp)
    (half["w_in"],) = _final_sums([gw["w_in"]], [sib["w_in"]], [far["w_in"]], shard_core)
    ((joined["w_in"],),) = _comm_only("comm_rs_join_in", [_rs_join([half["w_in"]])])
    update(("w_in",), "adamw_w_in", [])
    a[LOSS_ROW] = a["m_" + LOSS_ROW] = a["v_" + LOSS_ROW] = jnp.zeros((1, D_MODEL), F32)
    outs = _adamw_small(packed_wsp, slots_wsp, *[a[p + SMALL_WSP].reshape(wsp_view) for p in ("", "m_", "v_")], me)
    for dst, buf in zip((grads, deltas, new_m, new_v), outs):
        dst[SMALL_WSP] = buf.reshape(a[SMALL_WSP].shape)
    for names, rows, packed, slots in ((SMALL_EARLY, 8, packed_early, slots_early),
                                       (SMALL_LATE, 8, packed_late, slots_late)):
        outs = _adamw_small(packed, slots, _pack(a, names, rows), _pack({n: a["m_" + n] for n in names}, names, rows),
                            _pack({n: a["v_" + n] for n in names}, names, rows), me)
        for dst, buf in zip((grads, deltas, new_m, new_v), outs):
            dst.update(_unpack(buf, names, {n: a[n].shape for n in names}))
    loss = jnp.sum(grads[LOSS_ROW]) * np.float32(0.5 / D_MODEL)
    return (loss, grad_x[None], *[grads[n] for n in WEIGHTS], *[deltas[n] for n in WEIGHTS],
            *[new_m[n] for n in WEIGHTS], *[new_v[n] for n in WEIGHTS])
```

```python
import numpy as np
import jax
import jax.numpy as jnp
from jax import lax
from jax.experimental import pallas as pl
from jax.experimental.pallas import tpu as pltpu

F32 = jnp.float32
BF16 = jnp.bfloat16

SEQ = 2048
D_MODEL = 1024
HEAD_DIM = 64
ATTN_W = 512
SGU_W = 512
SGU_GROUPS = 8
CHUNK = 128
DILATIONS = (1, 4, 16)
N_SHARD = 4
IN_S = 640
OUT_S = 256
FF_S = 704
PROJ_W = N_SHARD * IN_S
FF = N_SHARD * FF_S
FF_CHUNKS = ((0, 1024), (1024, 2048), (2048, FF))
RMS_EPS = 1e-6
LN_EPS = 1e-5
ROPE_THETA = 500000.0
ATTN_SCALE = 1.0 / np.sqrt(HEAD_DIM)
NEG = -1e30
TM = 512
TM_FFN = 256
VMEM_LIMIT = 56 * 1024 * 1024

ADAM_LR = 0.001
ADAM_B1 = 0.9
ADAM_B2 = 0.999
ADAM_EPS = 1e-08
ADAM_WD = 0.01
ADAM_STEP = 10

MESH = pl.DeviceIdType.MESH
ANY = pl.BlockSpec(memory_space=pl.ANY)


def _dot(a, b):
    return jnp.dot(a, b, preferred_element_type=F32)


def _dot_nt(a, b):
    return lax.dot_general(a, b, (((1,), (1,)), ((), ())), preferred_element_type=F32)


def _dot_tn(a, b):
    return lax.dot_general(a, b, (((0,), (0,)), ((), ())), preferred_element_type=F32)


def _dot_exact(a, b):
    return jnp.dot(a, b, preferred_element_type=F32, precision=lax.Precision.HIGHEST)


def _dot_select(a, sel):
    hi = a.astype(BF16)
    lo = (a - hi.astype(F32)).astype(BF16)
    sel = sel.astype(BF16)
    return _dot(hi, sel) + _dot(lo, sel)


def _rms_stats(x):
    r = lax.rsqrt(jnp.mean(x * x, axis=-1, keepdims=True) + RMS_EPS)
    return x * r, r


def _rms_bwd(xh, r, gain, dy):
    dxh = dy * gain
    dx = r * (dxh - xh * jnp.mean(dxh * xh, axis=-1, keepdims=True))
    return dx, jnp.sum(dy * xh, axis=0, keepdims=True)


_ERF_ALPHA = (-2.72614225801306e-10, 2.77068142495902e-08, -2.10102402082508e-06, -5.69250639462346e-05,
              -7.34990630326855e-04, -2.95459980854025e-03, -1.60960333262415e-02)
_ERF_BETA = (-1.45660718464996e-05, -2.13374055278905e-04, -1.68282697438203e-03, -7.37332916720468e-03,
             -1.42647390514189e-02)


def _erf(x):
    x = jnp.clip(x, -4.0, 4.0)
    x2 = x * x
    p = jnp.full_like(x, _ERF_ALPHA[0])
    for a in _ERF_ALPHA[1:]:
        p = p * x2 + a
    q = jnp.full_like(x, _ERF_BETA[0])
    for b in _ERF_BETA[1:]:
        q = q * x2 + b
    return x * p / q


def _normal_cdf(x):
    return 0.5 * (1.0 + _erf(x * np.float32(1.0 / np.sqrt(2.0))))


def _gelu_grad(x, cdf):
    pdf = jnp.exp(-0.5 * x * x) * np.float32(1.0 / np.sqrt(2.0 * np.pi))
    return cdf + x * pdf


def _sigmoid(x):
    return 1.0 / (1.0 + jnp.exp(-x))


_INV_FREQ = tuple(float(np.float32(ROPE_THETA ** (-2.0 * j / 16.0))) for j in range(8))


def _rot_tables(pos):
    lane = lax.broadcasted_iota(jnp.int32, (1, 128), 1)
    d = lane & 63
    j = d & 7
    inv = jnp.zeros((1, 128), F32)
    for jj in range(8):
        inv = jnp.where(j == jj, _INV_FREQ[jj], inv)
    ang = pos.astype(F32) * inv
    c = jnp.cos(ang)
    s = jnp.sin(ang)
    cos_t = jnp.where(d < 16, c, 1.0)
    sin_a = jnp.where(d < 8, -s, 0.0)
    sin_b = jnp.where((d >= 8) & (d < 16), s, 0.0)
    return tuple(jnp.tile(t, (1, 4)) for t in (cos_t, sin_a, sin_b))


def _rope(x, tabs):
    cos_t, sin_a, sin_b = tabs
    return x * cos_t + pltpu.roll(x, 504, 1) * sin_a + pltpu.roll(x, 8, 1) * sin_b


def _rope_bwd(dy, tabs):
    cos_t, sin_a, sin_b = tabs
    return dy * cos_t + pltpu.roll(dy * sin_a, 8, 1) + pltpu.roll(dy * sin_b, 504, 1)


def _left_half():
    return lax.broadcasted_iota(jnp.int32, (CHUNK, CHUNK), 1) < HEAD_DIM


def _group_ones():
    lane = lax.broadcasted_iota(jnp.int32, (SGU_GROUPS, SGU_W), 1)
    row = lax.broadcasted_iota(jnp.int32, (SGU_GROUPS, SGU_W), 0)
    return ((lane >> 6) == row).astype(F32)


def _masked_spatial(w_ref):
    row = lax.broadcasted_iota(jnp.int32, (CHUNK, CHUNK), 0)
    col = lax.broadcasted_iota(jnp.int32, (CHUNK, CHUNK), 1)
    return [jnp.where(col <= row, w_ref[g], 0.0).astype(BF16) for g in range(SGU_GROUPS)]


def _sgu_core(u, vs, lg, lb, wm, bias_full):
    tm = u.shape[0]
    cdf_u, cdf_vs = _normal_cdf(u), _normal_cdf(vs)
    gu = u * cdf_u
    gv = vs * cdf_vs
    mu = jnp.mean(gv, axis=-1, keepdims=True)
    xc = gv - mu
    rstd = lax.rsqrt(jnp.mean(xc * xc, axis=-1, keepdims=True) + LN_EPS)
    xh = xc * rstd
    vnb = (xh * lg + lb).astype(BF16)
    left = _left_half()
    rows = []
    for c in range(tm // CHUNK):
        pieces = []
        for p in range(4):
            vp = vnb[c * CHUNK:(c + 1) * CHUNK, p * 128:(p + 1) * 128]
            pieces.append(jnp.where(left, _dot(wm[2 * p], vp), _dot(wm[2 * p + 1], vp)))
        rows.append(jnp.concatenate(pieces, axis=1) + bias_full)
    mixed = jnp.concatenate(rows, axis=0)
    return gu, xh, rstd, vnb, mixed, cdf_u, cdf_vs


def _resident(shape):
    n = len(shape)
    return pl.BlockSpec(shape, lambda *_: (0,) * n, pipeline_mode=pl.Buffered(1))


def _rows(ncol, tm=TM):
    return pl.BlockSpec((tm, ncol), lambda i: (i, 0))


def _acc(ncol, nrow=1):
    return pl.BlockSpec((nrow, ncol), lambda i: (0, 0))


HEAD_W = 128


def _view_rows(dil, width=ATTN_W, tm=TM):
    return pl.BlockSpec((tm // dil, dil * width), lambda i: (i, 0))


def _view_shape(dil, dtype, width=ATTN_W):
    return _sds((SEQ // dil, dil * width), dtype)


def _slab_scratch():
    return pltpu.VMEM((4, TM, 128), F32)


def _store_view(val, out_ref, slabs, dil):
    width = val.shape[1]
    for j in range(width // 128):
        slabs[j] = val[:, j * 128:(j + 1) * 128]
    for r in range(dil):
        for j in range(width // 128):
            c0 = r * width + j * 128
            out_ref[:, c0:c0 + 128] = slabs.at[j][pl.ds(r, TM // dil, stride=dil), :].astype(out_ref.dtype)


def _load_view(in_ref, slabs, dil, width=ATTN_W):
    for r in range(dil):
        for j in range(width // 128):
            c0 = r * width + j * 128
            slabs.at[j][pl.ds(r, TM // dil, stride=dil), :] = in_ref[:, c0:c0 + 128].astype(F32)
    return jnp.concatenate([slabs[j] for j in range(width // 128)], axis=1)


def _head_spread():
    m = lax.broadcasted_iota(jnp.int32, (HEAD_W, ATTN_W), 0)
    lane = lax.broadcasted_iota(jnp.int32, (HEAD_W, ATTN_W), 1)
    return (m == 16 * (lane >> 6)).astype(F32)


def _head_sum():
    lane = lax.broadcasted_iota(jnp.int32, (ATTN_W, HEAD_W), 0)
    m = lax.broadcasted_iota(jnp.int32, (ATTN_W, HEAD_W), 1)
    return ((lane >> 6) == (m >> 4)).astype(F32)


def _seq_params():
    return pltpu.CompilerParams(dimension_semantics=("arbitrary",), vmem_limit_bytes=VMEM_LIMIT)


def _sds(shape, dtype):
    return jax.ShapeDtypeStruct(shape, dtype)


class _Comm:
    def __init__(self, args, out_shape, n_sems, start, finish, aliased=False):
        self.args, self.out_shape, self.n_sems = list(args), list(out_shape), n_sems
        self.start, self.finish, self.aliased = start, finish, aliased


def _pcall(body, *, name, grid, in_specs, out_specs, out_shape, args, scratch_shapes=(), comms=(), after=()):
    single = not isinstance(out_shape, (list, tuple))
    out_specs = [out_specs] if single else list(out_specs)
    out_shape = [out_shape] if single else list(out_shape)
    n_in, n_out, n_scr = len(in_specs), len(out_shape), len(scratch_shapes)
    c_args = [a for c in comms for a in c.args]
    c_outs = [o for c in comms for o in c.out_shape]
    aliases, ai, ao = {}, n_in, n_out
    for c in comms:
        if c.aliased:
            aliases.update({ai + k: ao + k for k in range(len(c.args))})
        ai += len(c.args)
        ao += len(c.out_shape)
    sems = [pltpu.SemaphoreType.DMA((c.n_sems,)) for c in comms for _ in range(2)]
    steps = grid[0]

    def wrapped(*refs):
        o0 = n_in + len(c_args) + len(after)
        s0 = o0 + n_out + len(c_outs)
        m_in, m_out, m_sem = refs[n_in:n_in + len(c_args)], refs[o0 + n_out:s0], refs[s0 + n_scr:]

        def each(phase):
            ii = oi = 0
            for k, c in enumerate(comms):
                getattr(c, phase)(m_in[ii:ii + len(c.args)], m_out[oi:oi + len(c.out_shape)],
                                  m_sem[2 * k], m_sem[2 * k + 1])
                ii += len(c.args)
                oi += len(c.out_shape)

        if comms:
            @pl.when(pl.program_id(0) == 0)
            def _():
                each("start")

        body(*refs[:n_in], *refs[o0:o0 + n_out], *refs[s0:s0 + n_scr])

        if comms:
            @pl.when(pl.program_id(0) == steps - 1)
            def _():
                each("finish")

    res = pl.pallas_call(
        wrapped, name=name, grid=grid,
        in_specs=list(in_specs) + [ANY] * (len(c_args) + len(after)), out_specs=out_specs + [ANY] * len(c_outs),
        out_shape=out_shape + c_outs, scratch_shapes=list(scratch_shapes) + sems,
        input_output_aliases=aliases, compiler_params=_seq_params(),
    )(*args, *c_args, *after)
    mine = res[0] if single else list(res[:n_out])
    if not comms:
        return mine
    theirs, oi = [], n_out
    for c in comms:
        theirs.append(list(res[oi:oi + len(c.out_shape)]))
        oi += len(c.out_shape)
    return mine, theirs


def _in_pieces(g):
    lo, hi = 512 * g, 512 * (g + 1)
    return [(s, max(lo, IN_S * s) - IN_S * s, min(hi, IN_S * (s + 1)) - IN_S * s)
            for s in range(N_SHARD) if max(lo, IN_S * s) < min(hi, IN_S * (s + 1))]


def _inproj_fwd(x, pos, g_pre, w_in, lg, lb, w_sp, b_t, comms=()):
    def body(x_ref, pos_ref, g_ref, w_ref, lg_ref, lb_ref, wsp_ref, bt_ref, h_ref, u_ref, vs_ref, sgu_ref, *rest):
        qkv_refs, slabs = rest[:9], rest[9:]
        xh, _ = _rms_stats(x_ref[...])
        h = (xh * g_ref[...]).astype(BF16)
        h_ref[...] = h
        tabs = _rot_tables(pos_ref[...])

        def group(g):
            return jnp.concatenate([_dot(h, w_ref[s, :, a:b]) for s, a, b in _in_pieces(g)], axis=1)

        for t in range(3):
            val = group(t)
            if t < 2:
                val = _rope(val, tabs)
            if t == 0:
                val = val * np.float32(ATTN_SCALE)
            qkv_refs[t][...] = val.astype(BF16)
            for i, dil in enumerate(DILATIONS[1:]):
                _store_view(val, qkv_refs[3 * (i + 1) + t], slabs[t], dil)
        u = group(3)
        vs = group(4)
        u_ref[...] = u
        vs_ref[...] = vs
        bias_full = _dot_exact(bt_ref[...], _group_ones())
        gu, _, _, _, mixed, _, _ = _sgu_core(u, vs, lg_ref[...], lb_ref[...], _masked_spatial(wsp_ref), bias_full)
        sgu_ref[...] = gu * mixed

    return _pcall(
        body, name="inproj_sgu_fwd", grid=(SEQ // TM,),
        in_specs=[_rows(D_MODEL), _rows(1), _resident((1, D_MODEL)), _resident((N_SHARD, D_MODEL, IN_S)),
                  _resident((1, SGU_W)), _resident((1, SGU_W)), _resident((SGU_GROUPS, CHUNK, CHUNK)),
                  _resident((CHUNK, SGU_GROUPS))],
        out_specs=[_rows(D_MODEL), _rows(512), _rows(512), _rows(512)]
        + [_view_rows(dil) for dil in DILATIONS for _ in range(3)],
        out_shape=[_sds((SEQ, D_MODEL), BF16), _sds((SEQ, 512), F32), _sds((SEQ, 512), F32), _sds((SEQ, 512), F32)]
        + [_view_shape(dil, BF16) for dil in DILATIONS for _ in range(3)],
        scratch_shapes=[_slab_scratch() for _ in range(3)],
        args=(x, pos, g_pre, w_in, lg, lb, w_sp, b_t), comms=comms)


def _block_masks():
    row = lax.broadcasted_iota(jnp.int32, (CHUNK, CHUNK), 0)
    col = lax.broadcasted_iota(jnp.int32, (CHUNK, CHUNK), 1)
    return col <= row, col >= row


def _attn_fwd(qv, kv, vv, dil, comms=()):
    seg = SEQ // dil
    nblk = seg // CHUNK
    rps = 4 if nblk == 1 else 1

    def body(q_ref, k_ref, v_ref, o_ref, l_ref):
        left = _left_half()
        m_cur, m_prev = _block_masks()
        zero = jnp.zeros((CHUNK, CHUNK), BF16)
        ones = (jnp.where(left, 1.0, 0.0).astype(BF16), jnp.where(left, 0.0, 1.0).astype(BF16))

        sides = tuple(enumerate((left, ~left)))

        def rows(b):
            if isinstance(b, int):
                return b * CHUNK, max(b - 1, 0) * CHUNK
            return pl.multiple_of(b * CHUNK, CHUNK), pl.multiple_of(jnp.maximum(b - 1, 0) * CHUNK, CHUNK)

        def first(rr, b):
            r0, rp = rows(b)
            prev_ok = m_prev & (b > 0)
            tiles, scores = [], []
            for hp in range(4):
                ls = slice(rr * ATTN_W + hp * 128, rr * ATTN_W + (hp + 1) * 128)
                qp = q_ref[pl.ds(r0, CHUNK), ls]
                kc = k_ref[pl.ds(r0, CHUNK), ls]
                kp = k_ref[pl.ds(rp, CHUNK), ls] if nblk > 1 else None
                tiles.append((ls, v_ref[pl.ds(r0, CHUNK), ls], v_ref[pl.ds(rp, CHUNK), ls] if nblk > 1 else None))
                for _, hm in sides:
                    qh = jnp.where(hm, qp, zero)
                    sc = jnp.where(m_cur, _dot_nt(qh, kc), NEG)
                    sp = jnp.where(prev_ok, _dot_nt(qh, kp), NEG) if nblk > 1 else None
                    scores.append((sc, sp))
            return tiles, scores

        def second(scores):
            probs = []
            for sc, sp in scores:
                if nblk > 1:
                    m = jnp.max(jnp.maximum(sc, sp), axis=-1, keepdims=True)
                    pc = jnp.exp(sc - m)
                    pp = jnp.exp(sp - m)
                    probs.append((m, pc.astype(BF16), pp.astype(BF16), (pc + pp).astype(BF16)))
                else:
                    m = jnp.max(sc, axis=-1, keepdims=True)
                    pc = jnp.exp(sc - m).astype(BF16)
                    probs.append((m, pc, None, pc))
            return probs

        def third(rr, b, tiles, probs):
            r0, _ = rows(b)
            for hp, (ls, vc, vp) in enumerate(tiles):
                acc = jnp.zeros((CHUNK, CHUNK), F32)
                den = jnp.zeros((CHUNK, CHUNK), F32)
                for side, hm in sides:
                    _, pc, pp, psum = probs[2 * hp + side]
                    acc = acc + _dot(pc, jnp.where(hm, vc, zero))
                    if nblk > 1:
                        acc = acc + _dot(pp, jnp.where(hm, vp, zero))
                    den = den + _dot(psum, ones[side])
                o_ref[pl.ds(r0, CHUNK), ls] = (acc / den).astype(o_ref.dtype)
                lse = jnp.where(left, probs[2 * hp][0], probs[2 * hp + 1][0]) + jnp.log(den)
                l_ref[pl.ds(r0, CHUNK), rr * HEAD_W + 32 * hp:rr * HEAD_W + 32 * hp + 32] = lse[:, 48:80]

        def run(units):
            data = [first(rr, b) for rr, b in units]
            probs = [second(scores) for _, scores in data]
            for (rr, b), (tiles, _), pr in zip(units, data, probs):
                third(rr, b, tiles, pr)

        if nblk == 1:
            run([(rr, 0) for rr in range(rps)])
        else:
            def one(b, carry):
                run([(0, b)])
                return carry

            lax.fori_loop(0, nblk, one, 0)

    spec = pl.BlockSpec((seg, rps * ATTN_W), lambda r: (0, r))
    return _pcall(
        body, name=f"attn_fwd_d{dil}", grid=(dil // rps,),
        in_specs=[spec, spec, spec], out_specs=[spec, pl.BlockSpec((seg, rps * HEAD_W), lambda r: (0, r))],
        out_shape=[_sds((seg, dil * ATTN_W), BF16), _sds((seg, dil * HEAD_W), F32)],
        args=(qv, kv, vv), comms=comms)


def _lane_left(nrows):
    return lax.broadcasted_iota(jnp.int32, (nrows, CHUNK), 1) < HEAD_DIM


def _attn_rows(b):
    if isinstance(b, int):
        return b * CHUNK, max(b - 1, 0) * CHUNK
    return pl.multiple_of(b * CHUNK, CHUNK), pl.multiple_of(jnp.maximum(b - 1, 0) * CHUNK, CHUNK)


def _mix_out_fwd(o_list, l_list, sgu, x, w_out, g_attn, g_sgu, g_post, comms=()):
    def body(o1, o2, o3, l1, l2, l3, sgu_ref, x_ref, w_ref, ga_ref, gs_ref, gp_ref,
             attn_ref, mixed_ref, y_ref, x1_ref, lse1_ref, lse2_ref, lse3_ref, slabs_a, slabs_b):
        os = [o1[...], _load_view(o2, slabs_a, DILATIONS[1]), _load_view(o3, slabs_b, DILATIONS[2])]
        ls = [l1[...], _load_view(l2, slabs_a, DILATIONS[1], HEAD_W), _load_view(l3, slabs_b, DILATIONS[2], HEAD_W)]
        m = jnp.maximum(jnp.maximum(ls[0], ls[1]), ls[2])
        es = [jnp.exp(l - m) for l in ls]
        den = es[0] + es[1] + es[2]
        spread = _head_spread()
        attn = sum(_dot_select(e / den, spread) * o for e, o in zip(es, os))
        attn_ref[...] = attn
        lse = m + jnp.log(den)
        lse1_ref[...] = lse
        _store_view(lse, lse2_ref, slabs_a, DILATIONS[1])
        _store_view(lse, lse3_ref, slabs_b, DILATIONS[2])
        ah, _ = _rms_stats(attn)
        sh, _ = _rms_stats(sgu_ref[...])
        mixed = jnp.concatenate([ah * ga_ref[...], sh * gs_ref[...]], axis=1).astype(BF16)
        mixed_ref[...] = mixed
        y = _dot(mixed[:, 0:OUT_S], w_ref[0])
        for s in range(1, N_SHARD):
            y = y + _dot(mixed[:, s * OUT_S:(s + 1) * OUT_S], w_ref[s])
        y_ref[...] = y
        yh, _ = _rms_stats(y)
        x1_ref[...] = x_ref[...] + yh * gp_ref[...]

    return _pcall(
        body, name="mix_out_fwd", grid=(SEQ // TM,),
        in_specs=[_view_rows(dil) for dil in DILATIONS] + [_view_rows(dil, HEAD_W) for dil in DILATIONS]
        + [_rows(512), _rows(D_MODEL), _resident((N_SHARD, OUT_S, D_MODEL)),
           _resident((1, 512)), _resident((1, 512)), _resident((1, D_MODEL))],
        out_specs=[_rows(512), _rows(D_MODEL), _rows(D_MODEL), _rows(D_MODEL)]
        + [_view_rows(dil, HEAD_W) for dil in DILATIONS],
        out_shape=[_sds((SEQ, 512), F32), _sds((SEQ, D_MODEL), BF16), _sds((SEQ, D_MODEL), F32),
                   _sds((SEQ, D_MODEL), F32)] + [_view_shape(dil, F32, HEAD_W) for dil in DILATIONS],
        scratch_shapes=[_slab_scratch(), _slab_scratch()],
        args=(*o_list, *l_list, sgu, x, w_out, g_attn, g_sgu, g_post), comms=comms)


def _ffn_fwd_bwd(x1, target, w_gate, w_up, w_down, g_pre, g_post, comms=()):
    def body(x1_ref, t_ref, wg_ref, wu_ref, wd_ref, gpf_ref, gpo_ref,
             h2_ref, a_ref, dg_ref, dup_ref, df_ref, dx1_ref, loss_ref, dgpf_ref, dgpo_ref, g_scr, up_scr):
        @pl.when(pl.program_id(0) == 0)
        def _():
            loss_ref[...] = jnp.zeros_like(loss_ref)
            dgpf_ref[...] = jnp.zeros_like(dgpf_ref)
            dgpo_ref[...] = jnp.zeros_like(dgpo_ref)

        x1 = x1_ref[...]
        gpf = gpf_ref[...]
        gpo = gpo_ref[...]
        xh, r = _rms_stats(x1)
        h2 = (xh * gpf).astype(BF16)
        h2_ref[...] = h2
        f = jnp.zeros((TM_FFN, D_MODEL), F32)
        for c0, c1 in FF_CHUNKS:
            g = _dot_nt(h2, wg_ref[c0:c1, :])
            up = _dot_nt(h2, wu_ref[c0:c1, :])
            g_scr[:, c0:c1] = g
            up_scr[:, c0:c1] = up
            a = (g * _sigmoid(g) * up).astype(BF16)
            a_ref[:, c0:c1] = a
            f = f + _dot(a, wd_ref[c0:c1, :])
        fh, rf = _rms_stats(f)
        diff = x1 + fh * gpo - t_ref[...]
        loss_ref[...] += jnp.sum(diff * diff, axis=0, keepdims=True)
        dout = diff * np.float32(1.0 / D_MODEL)
        df, dgpo = _rms_bwd(fh, rf, gpo, dout)
        dgpo_ref[...] += dgpo
        dfb = df.astype(BF16)
        df_ref[...] = dfb
        dh2 = jnp.zeros((TM_FFN, D_MODEL), F32)
        for c0, c1 in FF_CHUNKS:
            da = _dot_nt(dfb, wd_ref[c0:c1, :])
            g = g_scr[:, c0:c1]
            up = up_scr[:, c0:c1]
            sg = _sigmoid(g)
            dup = (da * (g * sg)).astype(BF16)
            dg = (da * up * (sg * (1.0 + g * (1.0 - sg)))).astype(BF16)
            dg_ref[:, c0:c1] = dg
            dup_ref[:, c0:c1] = dup
            dh2 = dh2 + _dot(dg, wg_ref[c0:c1, :]) + _dot(dup, wu_ref[c0:c1, :])
        dx, dgpf = _rms_bwd(xh, r, gpf, dh2)
        dgpf_ref[...] += dgpf
        dx1_ref[...] = dout + dx

    return _pcall(
        body, name="ffn_fwd_bwd", grid=(SEQ // TM_FFN,),
        in_specs=[_rows(D_MODEL, TM_FFN), _rows(D_MODEL, TM_FFN), _resident((FF, D_MODEL)),
                  _resident((FF, D_MODEL)), _resident((FF, D_MODEL)),
                  _resident((1, D_MODEL)), _resident((1, D_MODEL))],
        out_specs=[_rows(D_MODEL, TM_FFN), _rows(FF, TM_FFN), _rows(FF, TM_FFN), _rows(FF, TM_FFN),
                   _rows(D_MODEL, TM_FFN), _rows(D_MODEL, TM_FFN), _acc(D_MODEL), _acc(D_MODEL), _acc(D_MODEL)],
        out_shape=[_sds((SEQ, D_MODEL), BF16), _sds((SEQ, FF), BF16), _sds((SEQ, FF), BF16),
                   _sds((SEQ, FF), BF16), _sds((SEQ, D_MODEL), BF16), _sds((SEQ, D_MODEL), F32),
                   _sds((1, D_MODEL), F32), _sds((1, D_MODEL), F32), _sds((1, D_MODEL), F32)],
        scratch_shapes=[pltpu.VMEM((TM_FFN, FF), F32), pltpu.VMEM((TM_FFN, FF), F32)],
        args=(x1, target, w_gate, w_up, w_down, g_pre, g_post), comms=comms)


def _wgrad(a, b, a_spec, b_spec, out_block, name, comms=()):
    def body(a_ref, b_ref, o_ref):
        av = a_ref[0] if len(a_ref.shape) == 3 else a_ref[...]
        bv = b_ref[0] if len(b_ref.shape) == 3 else b_ref[...]
        o_ref[0] = _dot_tn(av, bv)

    return _pcall(
        body, name=name, grid=(N_SHARD,),
        in_specs=[a_spec, b_spec],
        out_specs=pl.BlockSpec((1,) + out_block, lambda s: (s, 0, 0)),
        out_shape=_sds((N_SHARD,) + out_block, F32),
        args=(a, b), comms=comms)


def _outproj_bwd(dx1, y, attn, sgu, w_out, g_post, g_attn, g_sgu, u, vs, lg, lb, w_sp, b_t, comms=(), after=()):
    sgu_bwd = _sgu_bwd_body()

    def body(dx1_ref, y_ref, attn_ref, sgu_ref, w_ref, gp_ref, ga_ref, gs_ref, u_ref, vs_ref, lg_ref, lb_ref,
             wsp_ref, bt_ref, dy_ref, du_ref, dvs_ref, dgp_ref, dga_ref, dgs_ref, dw_ref, db_ref, dlg_ref, dlb_ref,
             *rest):
        dattn_refs, delta_refs, (slabs_a, slabs_b, dsgu_ref, dbias_scr) = rest[0:3], rest[3:6], rest[6:]

        @pl.when(pl.program_id(0) == 0)
        def _():
            dgp_ref[...] = jnp.zeros_like(dgp_ref)
            dga_ref[...] = jnp.zeros_like(dga_ref)
            dgs_ref[...] = jnp.zeros_like(dgs_ref)

        yh, ry = _rms_stats(y_ref[...])
        dy, dgp = _rms_bwd(yh, ry, gp_ref[...], dx1_ref[...])
        dgp_ref[...] += dgp
        dyb = dy.astype(BF16)
        dy_ref[...] = dyb
        dmixed = jnp.concatenate([_dot_nt(dyb, w_ref[s]) for s in range(N_SHARD)], axis=1)
        attn = attn_ref[...]
        ah, ra = _rms_stats(attn)
        dattn, dga = _rms_bwd(ah, ra, ga_ref[...], dmixed[:, 0:512])
        dga_ref[...] += dga
        sh, rs = _rms_stats(sgu_ref[...])
        dsgu, dgs = _rms_bwd(sh, rs, gs_ref[...], dmixed[:, 512:1024])
        dgs_ref[...] += dgs
        dsgu_ref[...] = dsgu
        delta = _dot_select(dattn * attn, _head_sum())
        dattn_refs[0][...] = dattn.astype(BF16)
        delta_refs[0][...] = delta
        for i, dil in enumerate(DILATIONS[1:]):
            _store_view(dattn, dattn_refs[i + 1], slabs_a, dil)
            _store_view(delta, delta_refs[i + 1], slabs_b, dil)
        sgu_bwd(u_ref, vs_ref, dsgu_ref, lg_ref, lb_ref, wsp_ref, bt_ref,
                du_ref, dvs_ref, dw_ref, db_ref, dlg_ref, dlb_ref, dbias_scr)

    return _pcall(
        body, name="outproj_sgu_bwd", grid=(SEQ // TM,),
        in_specs=[_rows(D_MODEL), _rows(D_MODEL), _rows(512), _rows(512), _resident((N_SHARD, OUT_S, D_MODEL)),
                  _resident((1, D_MODEL)), _resident((1, 512)), _resident((1, 512)),
                  _rows(SGU_W), _rows(SGU_W), _resident((1, SGU_W)), _resident((1, SGU_W)),
                  _resident((SGU_GROUPS, CHUNK, CHUNK)), _resident((CHUNK, SGU_GROUPS))],
        out_specs=[_rows(D_MODEL), _rows(SGU_W), _rows(SGU_W), _acc(D_MODEL), _acc(512), _acc(512),
                   pl.BlockSpec((SGU_GROUPS, CHUNK, CHUNK), lambda i: (0, 0, 0)), _acc(CHUNK, SGU_GROUPS),
                   _acc(SGU_W), _acc(SGU_W)]
        + [_view_rows(dil) for dil in DILATIONS] + [_view_rows(dil, HEAD_W) for dil in DILATIONS],
        out_shape=[_sds((SEQ, D_MODEL), BF16), _sds((SEQ, SGU_W), BF16), _sds((SEQ, SGU_W), BF16),
                   _sds((1, D_MODEL), F32), _sds((1, 512), F32), _sds((1, 512), F32),
                   _sds((SGU_GROUPS, CHUNK, CHUNK), F32), _sds((SGU_GROUPS, CHUNK), F32),
                   _sds((1, SGU_W), F32), _sds((1, SGU_W), F32)]
        + [_view_shape(dil, BF16) for dil in DILATIONS] + [_view_shape(dil, F32, HEAD_W) for dil in DILATIONS],
        scratch_shapes=[_slab_scratch(), _slab_scratch(), pltpu.VMEM((TM, SGU_W), F32),
                        pltpu.VMEM((CHUNK, SGU_W), F32)],
        args=(dx1, y, attn, sgu, w_out, g_post, g_attn, g_sgu, u, vs, lg, lb, w_sp, b_t), comms=comms, after=after)


def _sgu_bwd_body():
    nsteps = SEQ // TM

    def body(u_ref, vs_ref, ds_ref, lg_ref, lb_ref, w_ref, bt_ref,
             du_ref, dvs_ref, dw_ref, db_ref, dlg_ref, dlb_ref, dbias_scr):
        i = pl.program_id(0)

        @pl.when(i == 0)
        def _():
            dw_ref[...] = jnp.zeros_like(dw_ref)
            dlg_ref[...] = jnp.zeros_like(dlg_ref)
            dlb_ref[...] = jnp.zeros_like(dlb_ref)
            dbias_scr[...] = jnp.zeros_like(dbias_scr)

        wm = _masked_spatial(w_ref)
        ones_g = _group_ones()
        bias_full = _dot_exact(bt_ref[...], ones_g)
        u = u_ref[...]
        vs = vs_ref[...]
        lg = lg_ref[...]
        gu, xh, rstd, vnb, mixed, cdf_u, cdf_vs = _sgu_core(u, vs, lg, lb_ref[...], wm, bias_full)
        dsgu = ds_ref[...]
        du_ref[...] = (dsgu * mixed * _gelu_grad(u, cdf_u)).astype(BF16)
        dmixed = dsgu * gu
        left = _left_half()
        dvn_rows = []
        for c in range(TM // CHUNK):
            rs = slice(c * CHUNK, (c + 1) * CHUNK)
            dm_c = dmixed[rs, :]
            dbias_scr[...] += dm_c
            pieces = []
            for p in range(4):
                ls = slice(p * 128, (p + 1) * 128)
                dmp = dm_c[:, ls]
                vp = vnb[rs, ls]
                dmb = dmp.astype(BF16)
                zero = jnp.zeros_like(dmb)
                dw_ref[2 * p] += _dot_nt(jnp.where(left, dmb, zero), vp)
                dw_ref[2 * p + 1] += _dot_nt(jnp.where(left, zero, dmb), vp)
                pieces.append(jnp.where(left, _dot_tn(wm[2 * p], dmb), _dot_tn(wm[2 * p + 1], dmb)))
            dvn_rows.append(jnp.concatenate(pieces, axis=1))
        dvn = jnp.concatenate(dvn_rows, axis=0)
        dlg_ref[...] += jnp.sum(dvn * xh, axis=0, keepdims=True)
        dlb_ref[...] += jnp.sum(dvn, axis=0, keepdims=True)
        dxh = dvn * lg
        dgv = rstd * (dxh - jnp.mean(dxh, axis=-1, keepdims=True) - xh * jnp.mean(dxh * xh, axis=-1, keepdims=True))
        dvs_ref[...] = (dgv * _gelu_grad(vs, cdf_vs)).astype(BF16)

        @pl.when(i == nsteps - 1)
        def _():
            row = lax.broadcasted_iota(jnp.int32, (CHUNK, CHUNK), 0)
            col = lax.broadcasted_iota(jnp.int32, (CHUNK, CHUNK), 1)
            for g in range(SGU_GROUPS):
                dw_ref[g] = jnp.where(col <= row, dw_ref[g], 0.0)
            db_ref[...] = lax.dot_general(ones_g, dbias_scr[...], (((1,), (1,)), ((), ())),
                                          preferred_element_type=F32, precision=lax.Precision.HIGHEST)

    return body


def _attn_bwd(qv, kv, vv, dov, deltav, lsev, dil, comms=(), after=()):
    seg = SEQ // dil
    nblk = seg // CHUNK
    rps = 4 if nblk == 1 else 1

    def body(q_ref, k_ref, v_ref, do_ref, dl_ref, lse_ref, dq_ref, dk_ref, dv_ref, dk_wait, dv_wait):
        left = _left_half()
        m_cur, m_prev = _block_masks()
        sides = tuple(enumerate((left, ~left)))
        zero = jnp.zeros((CHUNK, CHUNK), BF16)

        def first(rr, b, both):
            r0, rp = _attn_rows(b)
            keys = pl.ds(rp, 2 * CHUNK) if both else pl.ds(r0, CHUNK)
            tiles, heads = [], []
            for hp in range(4):
                ls = slice(rr * ATTN_W + hp * 128, rr * ATTN_W + (hp + 1) * 128)
                qp = q_ref[pl.ds(r0, CHUNK), ls]
                dop = do_ref[pl.ds(r0, CHUNK), ls]
                k2 = k_ref[keys, ls]
                v2 = v_ref[keys, ls]
                tiles.append((ls, k2))
                for _, hm in sides:
                    qh = jnp.where(hm, qp, zero)
                    doh = jnp.where(hm, dop, zero)
                    heads.append((qh, doh, _dot_nt(k2, qh), _dot_nt(v2, doh)))
            return tiles, heads

        def second(rr, b, heads, both):
            r0, _ = _attn_rows(b)
            ok = jnp.concatenate([m_cur, m_prev], axis=0) if both else m_prev
            lanes = slice(rr * HEAD_W, (rr + 1) * HEAD_W)
            lse_t = lse_ref[pl.ds(r0, CHUNK), lanes].T
            dl_t = dl_ref[pl.ds(r0, CHUNK), lanes].T
            out = []
            for i, (_, _, s_t, dp_t) in enumerate(heads):
                p = jnp.exp(jnp.where(ok, s_t - lse_t[16 * i:16 * i + 1, :], NEG))
                out.append((p.astype(BF16), (p * (dp_t - dl_t[16 * i:16 * i + 1, :])).astype(BF16)))
            return out

        def third(rr, b, tiles, heads, probs, both):
            r0, rp = _attn_rows(b)
            nk = 2 * CHUNK if both else CHUNK
            left_k = _lane_left(nk)
            zero_k = jnp.zeros((nk, CHUNK), BF16)
            for hp, (ls, k2) in enumerate(tiles):
                dq = jnp.zeros((CHUNK, CHUNK), F32)
                dk2 = jnp.zeros((nk, CHUNK), F32)
                dv2 = jnp.zeros((nk, CHUNK), F32)
                for side in range(2):
                    qh, doh, _, _ = heads[2 * hp + side]
                    p, ds = probs[2 * hp + side]
                    dq = dq + _dot_tn(ds, jnp.where(left_k if side == 0 else ~left_k, k2, zero_k))
                    dk2 = dk2 + _dot(ds, qh)
                    dv2 = dv2 + _dot(p, doh)
                dq_ref[pl.ds(r0, CHUNK), ls] = dq.astype(dq_ref.dtype)
                if nblk == 1:
                    dk_ref[pl.ds(r0, CHUNK), ls] = dk2.astype(dk_ref.dtype)
                    dv_ref[pl.ds(r0, CHUNK), ls] = dv2.astype(dv_ref.dtype)
                elif both:
                    dk_ref[pl.ds(rp, CHUNK), ls] = (dk_wait[:, ls] + dk2[0:CHUNK]).astype(dk_ref.dtype)
                    dv_ref[pl.ds(rp, CHUNK), ls] = (dv_wait[:, ls] + dv2[0:CHUNK]).astype(dv_ref.dtype)
                    dk_wait[:, ls] = dk2[CHUNK:]
                    dv_wait[:, ls] = dv2[CHUNK:]
                else:
                    dk_wait[:, ls] = dk2
                    dv_wait[:, ls] = dv2

        def run(units, both):
            data = [first(rr, b, both) for rr, b in units]
            probs = [second(rr, b, heads, both) for (rr, b), (_, heads) in zip(units, data)]
            for (rr, b), (tiles, heads), pr in zip(units, data, probs):
                third(rr, b, tiles, heads, pr, both)

        run([(rr, 0) for rr in range(rps)], False)
        if nblk > 1:
            def one(b, carry):
                run([(0, b)], True)
                return carry

            lax.fori_loop(1, nblk, one, 0)
            last = (nblk - 1) * CHUNK
            dk_ref[last:last + CHUNK, :] = dk_wait[...].astype(dk_ref.dtype)
            dv_ref[last:last + CHUNK, :] = dv_wait[...].astype(dv_ref.dtype)

    spec = pl.BlockSpec((seg, rps * ATTN_W), lambda r: (0, r))
    return _pcall(
        body, name=f"attn_bwd_d{dil}", grid=(dil // rps,),
        in_specs=[spec] * 4 + [pl.BlockSpec((seg, rps * HEAD_W), lambda r: (0, r))] * 2, out_specs=[spec] * 3,
        out_shape=[_sds((seg, dil * ATTN_W), BF16)] * 3,
        scratch_shapes=[pltpu.VMEM((CHUNK, ATTN_W), F32), pltpu.VMEM((CHUNK, ATTN_W), F32)],
        args=(qv, kv, vv, dov, deltav, lsev), comms=comms, after=after)


def _inproj_bwd(dqs, dks, dvs, du, dvs_sgu, pos, x, dx1, w_in, g_pre, comms=()):
    def body(dq1, dq2, dq3, dk1, dk2, dk3, dv1, dv2, dv3, du_ref, dvs_ref, pos_ref, x_ref, dx1_ref, w_ref, g_ref,
             dproj_ref, gx_ref, dg_ref, slabs_a, slabs_b):
        @pl.when(pl.program_id(0) == 0)
        def _():
            dg_ref[...] = jnp.zeros_like(dg_ref)

        def total(r1, r2, r3):
            return r1[...] + _load_view(r2, slabs_a, DILATIONS[1]) + _load_view(r3, slabs_b, DILATIONS[2])

        tabs = _rot_tables(pos_ref[...])
        groups = {3: du_ref[...], 4: dvs_ref[...]}
        dh = jnp.zeros((TM, D_MODEL), F32)
        for g in (3, 4, 0, 1, 2):
            if g == 0:
                groups[g] = _rope_bwd(total(dq1, dq2, dq3) * np.float32(ATTN_SCALE), tabs).astype(BF16)
            elif g == 1:
                groups[g] = _rope_bwd(total(dk1, dk2, dk3), tabs).astype(BF16)
            elif g == 2:
                groups[g] = total(dv1, dv2, dv3).astype(BF16)
            dproj_ref[:, 512 * g:512 * (g + 1)] = groups[g]
            off = 0
            for s, a, b in _in_pieces(g):
                dh = dh + _dot_nt(groups[g][:, off:off + b - a], w_ref[s, :, a:b])
                off += b - a
        g = g_ref[...]
        xh, r = _rms_stats(x_ref[...])
        dx, dg = _rms_bwd(xh, r, g, dh)
        dg_ref[...] += dg
        gx_ref[...] = dx1_ref[...] + dx

    return _pcall(
        body, name="inproj_bwd", grid=(SEQ // TM,),
        in_specs=[_view_rows(dil) for dil in DILATIONS] * 3
        + [_rows(512), _rows(512), _rows(1), _rows(D_MODEL), _rows(D_MODEL),
           _resident((N_SHARD, D_MODEL, IN_S)), _resident((1, D_MODEL))],
        out_specs=[_rows(PROJ_W), _rows(D_MODEL), _acc(D_MODEL)],
        out_shape=[_sds((SEQ, PROJ_W), BF16), _sds((SEQ, D_MODEL), F32), _sds((1, D_MODEL), F32)],
        scratch_shapes=[_slab_scratch(), _slab_scratch()],
        args=(*dqs, *dks, *dvs, du, dvs_sgu, pos, x, dx1, w_in, g_pre), comms=comms)


def _coords():
    return lax.axis_index("x"), lax.axis_index("y"), lax.axis_index("c")


def _other_chips(x, y):
    return [(1 - x, y), (x, 1 - y), (1 - x, 1 - y)]


WEIGHTS = ("pre_mix_norm", "w_in", "sgu_ln_gain", "sgu_ln_bias", "sgu_w_spatial", "sgu_b_spatial", "attn_out_norm",
           "sgu_out_norm", "w_out", "post_mix_norm", "pre_ffn_norm", "w_gate", "w_up", "w_down", "post_ffn_norm")
SMALL = ("pre_mix_norm", "post_mix_norm", "pre_ffn_norm", "post_ffn_norm", "sgu_ln_gain", "sgu_ln_bias",
         "attn_out_norm", "sgu_out_norm", "sgu_w_spatial", "sgu_b_spatial")


def _remote(src, dst, send_sem, recv_sem, to):
    return pltpu.make_async_remote_copy(src_ref=src, dst_ref=dst, send_sem=send_sem, recv_sem=recv_sem,
                                        device_id=to, device_id_type=MESH)


def _halves(a):
    *lead, rows, cols = a.shape
    return a.reshape(*lead, 2, rows // 2, cols)


def _gather_ici(shards):
    n = len(shards)

    def desc(ins, outs, ss, rs, j, w, landed):
        x, y, c = _coords()
        cx, cy = _other_chips(x, y)[j]
        shard = 2 * cx + cy if landed else 2 * x + y
        return _remote(ins[w].at[c], outs[w].at[shard, c], ss.at[j * n + w], rs.at[j * n + w], (cx, cy, c))

    def start(ins, outs, ss, rs):
        for j in range(3):
            for w in range(n):
                desc(ins, outs, ss, rs, j, w, False).start()

    def finish(ins, outs, ss, rs):
        for j in range(3):
            for w in range(n):
                desc(ins, outs, ss, rs, j, w, True).wait_recv()
                desc(ins, outs, ss, rs, j, w, False).wait_send()

    return _Comm(shards, [_sds((N_SHARD,) + s.shape, s.dtype) for s in shards], 3 * n, start, finish)


def _gather_pass(fulls):
    n = len(fulls)

    def desc(bufs, ss, rs, j, w, landed):
        x, y, c = _coords()
        cx, cy = _other_chips(x, y)[j]
        shard = 2 * cx + cy
        return _remote(bufs[w].at[shard, c], bufs[w].at[shard, 1 - c if landed else c],
                       ss.at[j * n + w], rs.at[j * n + w], (x, y, 1 - c))

    def start(ins, outs, ss, rs):
        for j in range(3):
            for w in range(n):
                desc(outs, ss, rs, j, w, False).start()

    def finish(ins, outs, ss, rs):
        for j in range(3):
            for w in range(n):
                desc(outs, ss, rs, j, w, True).wait_recv()
                desc(outs, ss, rs, j, w, False).wait_send()

    return _Comm(fulls, [_sds(f.shape, f.dtype) for f in fulls], 3 * n, start, finish, aliased=True)


def _rs_sibling(gws):
    n = len(gws)

    def desc(ins, outs, ss, rs, w):
        x, y, c = _coords()
        return _remote(ins[w].at[:, 1 - c], outs[w], ss.at[w], rs.at[w], (x, y, 1 - c))

    def start(ins, outs, ss, rs):
        for w in range(n):
            desc(ins, outs, ss, rs, w).start()

    def finish(ins, outs, ss, rs):
        for w in range(n):
            desc(ins, outs, ss, rs, w).wait()

    out_shape = [_sds((N_SHARD, g.shape[1] // 2, g.shape[2]), g.dtype) for g in gws]
    return _Comm([_halves(g) for g in gws], out_shape, n, start, finish)


def _rs_chips(pbs):
    n = len(pbs)

    def desc(ins, outs, ss, rs, j, w):
        x, y, c = _coords()
        cx, cy = _other_chips(x, y)[j]
        return _remote(ins[w].at[2 * cx + cy], outs[w].at[j], ss.at[j * n + w], rs.at[j * n + w], (cx, cy, c))

    def start(ins, outs, ss, rs):
        for j in range(3):
            for w in range(n):
                desc(ins, outs, ss, rs, j, w).start()

    def finish(ins, outs, ss, rs):
        for j in range(3):
            for w in range(n):
                desc(ins, outs, ss, rs, j, w).wait()

    return _Comm(pbs, [_sds((3,) + p.shape[1:], p.dtype) for p in pbs], 3 * n, start, finish)


def _rs_join(halves):
    n = len(halves)

    def desc(bufs, ss, rs, w, landed):
        x, y, c = _coords()
        return _remote(bufs[w].at[c], bufs[w].at[1 - c if landed else c], ss.at[w], rs.at[w], (x, y, 1 - c))

    def start(ins, outs, ss, rs):
        for w in range(n):
            desc(outs, ss, rs, w, False).start()

    def finish(ins, outs, ss, rs):
        for w in range(n):
            desc(outs, ss, rs, w, True).wait_recv()
            desc(outs, ss, rs, w, False).wait_send()

    return _Comm(halves, [_sds(h.shape, h.dtype) for h in halves], n, start, finish, aliased=True)


def _small_exchange(buf):
    def desc(ins, outs, ss, rs, k, landed):
        x, y, c = _coords()
        px = 1 - x if (k >> 2) & 1 else x
        py = 1 - y if (k >> 1) & 1 else y
        pc = 1 - c if k & 1 else c
        slot = 4 * px + 2 * py + pc if landed else 4 * x + 2 * y + c
        return _remote(ins[0], outs[0].at[slot], ss.at[k - 1], rs.at[k - 1], (px, py, pc))

    def start(ins, outs, ss, rs):
        for k in range(1, 8):
            desc(ins, outs, ss, rs, k, False).start()

    def finish(ins, outs, ss, rs):
        for k in range(1, 8):
            desc(ins, outs, ss, rs, k, True).wait_recv()
            desc(ins, outs, ss, rs, k, False).wait_send()

    return _Comm([buf], [_sds((8,) + buf.shape, buf.dtype)], 7, start, finish)


HBM = pl.BlockSpec(memory_space=pltpu.HBM)
SEM = pl.BlockSpec(memory_space=pltpu.SEMAPHORE)
DATAFLOW = pltpu.SideEffectType.DATAFLOW_SIDE_EFFECTING


def _split_starts(name, comms, after):
    srcs = [[pltpu.with_memory_space_constraint(s, pltpu.HBM) for s in c.args] for c in comms]
    lands = [[pltpu.with_memory_space_constraint(lax.empty(o.shape, o.dtype), pltpu.HBM) for o in c.out_shape]
             for c in comms]
    bufs = [b for k in range(len(comms)) for b in srcs[k] + lands[k]]
    nb, nc = len(bufs), len(comms)

    def body(*refs):
        sems = refs[nb + 1:nb + 1 + 2 * nc]
        off = 0
        for k, c in enumerate(comms):
            ns, nl = len(srcs[k]), len(lands[k])
            c.start(refs[off:off + ns], refs[off + ns:off + ns + nl], sems[2 * k], sems[2 * k + 1])
            off += ns + nl
        refs[-1][...] = jnp.zeros_like(refs[-1])

    res = pl.pallas_call(
        body, name=name,
        out_shape=(*[pltpu.SemaphoreType.DMA((c.n_sems,)) for c in comms for _ in range(2)],
                   *[pltpu.HBM(b.shape, b.dtype) for b in bufs], _sds((8, 128), F32)),
        in_specs=[HBM] * nb + [ANY],
        out_specs=(*[SEM] * (2 * nc), *[HBM] * nb, pl.BlockSpec(memory_space=pltpu.VMEM)),
        input_output_aliases={i: 2 * nc + i for i in range(nb)},
        compiler_params=pltpu.CompilerParams(has_side_effects=DATAFLOW),
    )(*bufs, after)
    states, off = [], 2 * nc
    for k in range(nc):
        ns, nl = len(srcs[k]), len(lands[k])
        states.append((res[2 * k], res[2 * k + 1], list(res[off:off + ns]), list(res[off + ns:off + ns + nl])))
        off += ns + nl
    return states, res[-1]


def _split_wait(name, comm, send_sems, recv_sems, srcs, lands, after):
    ns, nb = len(srcs), len(lands)
    after = list(after) if isinstance(after, (list, tuple)) else [after]

    def body(*refs):
        comm.finish(refs[:ns], refs[ns:ns + nb], refs[ns + nb], refs[ns + nb + 1])

    res = pl.pallas_call(
        body, name=name,
        out_shape=tuple(pltpu.HBM(b.shape, b.dtype) for b in srcs + lands),
        in_specs=[HBM] * (ns + nb) + [SEM, SEM] + [ANY] * len(after), out_specs=tuple([HBM] * (ns + nb)),
        input_output_aliases={i: i for i in range(ns + nb)},
        compiler_params=pltpu.CompilerParams(has_side_effects=DATAFLOW),
    )(*srcs, *lands, send_sems, recv_sems, *after)
    return list(res[ns:])


LOSS_ROW = "loss_cols"
SMALL_EARLY = ("post_mix_norm", "pre_ffn_norm", "post_ffn_norm", "sgu_ln_gain", "sgu_ln_bias", "attn_out_norm",
               "sgu_out_norm", "sgu_b_spatial", LOSS_ROW)
SMALL_LATE = ("pre_mix_norm",)
SMALL_WSP = "sgu_w_spatial"


def _pack(d, names, rows):
    flat = [d[n].reshape(-1) for n in names]
    used = sum(f.shape[0] for f in flat)
    flat.append(jnp.zeros((rows * 1024 - used,), F32))
    return jnp.concatenate(flat).reshape(rows, 1024)


def _unpack(buf, names, shapes):
    flat = buf.reshape(-1)
    out, off = {}, 0
    for n in names:
        size = int(np.prod(shapes[n]))
        out[n] = flat[off:off + size].reshape(shapes[n])
        off += size
    return out


def _chip_sums(gws, recvs, shard_core):
    n = len(gws)
    _, rows, cols = gws[0].shape
    hw = rows // 2

    def body(sc_ref, *refs):
        for k in range(n):
            refs[2 * n + k][...] = (refs[k][...] + refs[n + k][...]).astype(BF16)

    def other(s, sc):
        return jnp.where(s >= sc[0], s + 1, s)

    mine = pl.BlockSpec((1, hw, cols), lambda s, sc: (other(s, sc), sc[1], 0))
    plain = pl.BlockSpec((1, hw, cols), lambda s, sc: (other(s, sc), 0, 0))
    return pl.pallas_call(
        body, name="rs_chip_sums",
        grid_spec=pltpu.PrefetchScalarGridSpec(num_scalar_prefetch=1, grid=(N_SHARD - 1,),
                                               in_specs=[mine] * n + [plain] * n, out_specs=[plain] * n),
        out_shape=[_sds((N_SHARD, hw, cols), BF16)] * n,
        compiler_params=_seq_params(),
    )(shard_core, *gws, *recvs)


def _final_sums(gws, recv_sibs, recv_chips, shard_core):
    n = len(gws)
    halves = [(g.shape[1] // 2, g.shape[2]) for g in gws]

    def body(sc_ref, *refs):
        for k in range(n):
            acc = refs[k][0] + refs[n + k][0]
            for j in range(3):
                acc = acc + refs[2 * n + k][j].astype(F32)
            refs[3 * n + k][0] = acc

    def specs(lead, index):
        return [pl.BlockSpec((lead, hw, cols), index) for hw, cols in halves]

    return pl.pallas_call(
        body, name="rs_final_sums",
        grid_spec=pltpu.PrefetchScalarGridSpec(
            num_scalar_prefetch=1, grid=(1,),
            in_specs=specs(1, lambda i, sc: (sc[0], sc[1], 0)) + specs(1, lambda i, sc: (sc[0], 0, 0))
            + specs(3, lambda i, sc: (0, 0, 0)),
            out_specs=specs(1, lambda i, sc: (sc[1], 0, 0))),
        out_shape=[_sds((2, hw, cols), F32) for hw, cols in halves],
        compiler_params=_seq_params(),
    )(shard_core, *gws, *recv_sibs, *recv_chips)


ADAM_BLOCKS = 4


def _adamw_multi(ws, gs, ms, vs, name, after=()):
    n = len(ws)

    def body(*refs):
        outs = refs[4 * n + len(after):]
        for k in range(n):
            g = refs[n + k][...]
            d, m, v = _adam_math(refs[k][...], g, refs[2 * n + k][...], refs[3 * n + k][...])
            outs[4 * k][...], outs[4 * k + 1][...], outs[4 * k + 2][...], outs[4 * k + 3][...] = g, d, m, v

    specs = [pl.BlockSpec((w.shape[0] // ADAM_BLOCKS, w.shape[1]), lambda i: (i, 0)) for w in ws]
    res = pl.pallas_call(
        body, name=name, grid=(ADAM_BLOCKS,), in_specs=specs * 4 + [ANY] * len(after),
        out_specs=[s for s in specs for _ in range(4)],
        out_shape=[_sds(w.shape, F32) for w in ws for _ in range(4)],
        compiler_params=_seq_params(),
    )(*ws, *gs, *ms, *vs, *after)
    return [tuple(res[4 * k:4 * k + 4]) for k in range(n)]


def _wgrad_ff(a_list, b, name, comms=()):
    n = len(a_list)
    cols = 256

    def body(*refs):
        bv = refs[n][...]
        for k in range(n):
            refs[n + 1 + k][...] = _dot_tn(refs[k][...], bv)

    res = _pcall(
        body, name=name, grid=(FF // cols,),
        in_specs=[pl.BlockSpec((SEQ, cols), lambda j: (0, j))] * n
        + [pl.BlockSpec((SEQ, D_MODEL), lambda j: (0, 0), pipeline_mode=pl.Buffered(1))],
        out_specs=[pl.BlockSpec((cols, D_MODEL), lambda j: (j, 0))] * n,
        out_shape=[_sds((FF, D_MODEL), F32)] * n, args=(*a_list, b), comms=comms)
    mine, theirs = res if comms else (res, None)
    mine = [m.reshape(N_SHARD, FF_S, D_MODEL) for m in mine]
    return (mine, theirs) if comms else mine


def _comm_only(name, comms):
    return _pcall(lambda: None, name=name, grid=(1,), in_specs=[], out_specs=[], out_shape=[], args=(),
                  comms=comms)[1]


def _adam_math(w, g, m, v):
    m = ADAM_B1 * m + (1.0 - ADAM_B1) * g
    v = ADAM_B2 * v + (1.0 - ADAM_B2) * (g * g)
    m_hat = m / (1.0 - ADAM_B1 ** ADAM_STEP)
    v_hat = v / (1.0 - ADAM_B2 ** ADAM_STEP)
    return -ADAM_LR * (m_hat / (jnp.sqrt(v_hat) + ADAM_EPS) + ADAM_WD * w), m, v


def _adamw_small(own, slots, w, m, v, me):
    rows, cols = own.shape

    def body(me_ref, own_ref, slots_ref, w_ref, m_ref, v_ref, g_ref, d_ref, nm_ref, nv_ref):
        own_v = own_ref[...].astype(F32)
        g = jnp.where(me_ref[0] == 0, own_v, slots_ref[0].astype(F32))
        for i in range(1, 8):
            g = g + jnp.where(me_ref[0] == i, own_v, slots_ref[i].astype(F32))
        g_ref[...] = g
        d_ref[...], nm_ref[...], nv_ref[...] = _adam_math(w_ref[...], g, m_ref[...], v_ref[...])

    flat = pl.BlockSpec((rows, cols), lambda i, me_ref: (0, 0))
    return pl.pallas_call(
        body, name="adamw_small",
        grid_spec=pltpu.PrefetchScalarGridSpec(
            num_scalar_prefetch=1, grid=(1,),
            in_specs=[flat, pl.BlockSpec((8, rows, cols), lambda i, me_ref: (0, 0, 0)), flat, flat, flat],
            out_specs=[flat] * 4),
        out_shape=[_sds((rows, cols), F32)] * 4,
        compiler_params=_seq_params(),
    )(me, own, slots, w, m, v)


def kernel(x, positions, pre_mix_norm, w_in, sgu_ln_gain, sgu_ln_bias, sgu_w_spatial, sgu_b_spatial, attn_out_norm, sgu_out_norm, w_out, post_mix_norm, pre_ffn_norm, w_gate, w_up, w_down, post_ffn_norm, loss_target, m_pre_mix_norm, m_w_in, m_sgu_ln_gain, m_sgu_ln_bias, m_sgu_w_spatial, m_sgu_b_spatial, m_attn_out_norm, m_sgu_out_norm, m_w_out, m_post_mix_norm, m_pre_ffn_norm, m_w_gate, m_w_up, m_w_down, m_post_ffn_norm, v_pre_mix_norm, v_w_in, v_sgu_ln_gain, v_sgu_ln_bias, v_sgu_w_spatial, v_sgu_b_spatial, v_attn_out_norm, v_sgu_out_norm, v_w_out, v_post_mix_norm, v_pre_ffn_norm, v_w_gate, v_w_up, v_w_down, v_post_ffn_norm):
    a = dict(locals())
    cx, cy, cc = _coords()
    s_me = 2 * cx + cy
    shard_core = jnp.stack([s_me, cc]).astype(jnp.int32)
    me = jnp.stack([4 * cx + 2 * cy + cc]).astype(jnp.int32)
    small = {n: (a[n][0] if a[n].ndim > 2 else a[n]) for n in SMALL}
    b_t = small["sgu_b_spatial"].T
    xs, pos, target = x[0], positions.reshape(SEQ, 1), loss_target[0]
    flipped = ("w_gate", "w_up")

    def big(name, n):
        return jnp.swapaxes(a[name], 1, 2)[0] if n in flipped else a[name][0]

    own = {"w_in": _halves(big("w_in", "w_in").astype(BF16))}

    def with_own(full, n):
        full = lax.dynamic_update_slice(full, own[n][None], (s_me, 0, 0, 0))
        return full.reshape((N_SHARD,) + big(n, n).shape)

    ffn = ("w_gate", "w_up", "w_down")
    g_in = _gather_ici([own["w_in"]])
    (s_in,), token = _split_starts("gather_in_start", [g_in], small["pre_mix_norm"])
    for n in ("w_out",) + ffn:
        own[n] = _halves((big(n, n) + token[0:1, 0:1]).astype(BF16))
    g_out, g_ffn = _gather_ici([own["w_out"]]), _gather_ici([own[n] for n in ffn])
    (s_out, s_ffn), token = _split_starts("gather_rest_start", [g_out, g_ffn], token)
    in_lands = _split_wait("gather_in_wait", g_in, *s_in, token)
    ((in_lands,),) = _comm_only("comm_pass_in", [_gather_pass(in_lands)])
    w_in_f = with_own(in_lands, "w_in")
    h, u, vs, sgu, *qkv = _inproj_fwd(xs, pos, small["pre_mix_norm"], w_in_f, small["sgu_ln_gain"],
                                      small["sgu_ln_bias"], small["sgu_w_spatial"], b_t)
    views = [tuple(qkv[3 * i:3 * i + 3]) for i in range(len(DILATIONS))]
    out_lands = _split_wait("gather_out_wait", g_out, *s_out, sgu)
    o_list, l_list = [], []
    for dil, (qv, kv, vv) in zip(DILATIONS, views):
        if dil == 1:
            (o, l), ((out_lands,),) = _attn_fwd(qv, kv, vv, dil, comms=[_gather_pass(out_lands)])
        else:
            o, l = _attn_fwd(qv, kv, vv, dil)
        o_list.append(o)
        l_list.append(l)
    w_out_f = with_own(out_lands, "w_out")
    ffn_lands = _split_wait("gather_ffn_wait", g_ffn, *s_ffn, l_list[-1])
    (attn, mixed, y, x1, *lses), (ffn_lands,) = _mix_out_fwd(
        o_list, l_list, sgu, xs, w_out_f, small["attn_out_norm"], small["sgu_out_norm"], small["post_mix_norm"],
        comms=[_gather_pass(ffn_lands)])
    w_gate_f, w_up_f, w_down_f = (with_own(f, n).reshape(FF, D_MODEL) for f, n in zip(ffn_lands, ffn))
    h2, act, dg, dup, df, dx1, loss_cols, d_pre_ffn, d_post_ffn = _ffn_fwd_bwd(
        x1, target, w_gate_f, w_up_f, w_down_f, small["pre_ffn_norm"], small["post_ffn_norm"])

    full_tok = pl.BlockSpec((SEQ, D_MODEL), lambda s: (0, 0), pipeline_mode=pl.Buffered(1))
    gw = {}
    gw["w_gate"], gw["w_up"] = _wgrad_ff([dg, dup], h2, "wgrad_gate_up")
    (gw["w_down"],) = _wgrad_ff([act], df, "wgrad_down")
    (dy, du, dvs_sgu, d_post_mix, d_attn_norm, d_sgu_norm, d_w_sp, d_b_sp, d_ln_gain, d_ln_bias,
     *dviews), (sib_ffn,) = _outproj_bwd(
        dx1, y, attn, sgu, w_out_f, small["post_mix_norm"], small["attn_out_norm"], small["sgu_out_norm"],
        u, vs, small["sgu_ln_gain"], small["sgu_ln_bias"], small["sgu_w_spatial"], b_t,
        comms=[_rs_sibling([gw[n] for n in ffn])])
    sib = dict(zip(ffn, sib_ffn))
    part = dict(zip(ffn, _chip_sums([gw[n] for n in ffn], [sib[n] for n in ffn], shard_core)))
    gw["w_out"] = _wgrad(mixed, dy, pl.BlockSpec((SEQ, OUT_S), lambda s: (0, s)), full_tok, (OUT_S, D_MODEL),
                         "wgrad_out")
    x_ffn = _rs_chips([part[n] for n in ffn])
    packed_early = _pack({
        "sgu_ln_gain": d_ln_gain, "sgu_ln_bias": d_ln_bias, "sgu_b_spatial": d_b_sp,
        "attn_out_norm": d_attn_norm, "sgu_out_norm": d_sgu_norm, "post_mix_norm": d_post_mix,
        "pre_ffn_norm": d_pre_ffn, "post_ffn_norm": d_post_ffn, LOSS_ROW: loss_cols}, SMALL_EARLY, 8)
    wsp_view = (SGU_GROUPS * CHUNK, CHUNK)
    packed_wsp = d_w_sp.reshape(wsp_view).astype(BF16)
    x_small, x_wsp = _small_exchange(packed_early), _small_exchange(packed_wsp)
    (s_ffn,), token = _split_starts("rs_ffn_start", [x_ffn], small["pre_mix_norm"])

    dqs, dks, dvs = [], [], []
    for i, (dil, (qv, kv, vv)) in enumerate(zip(DILATIONS, views)):
        if dil == 1:
            (dq, dk, dv), ((sib["w_out"],),) = _attn_bwd(qv, kv, vv, dviews[i], dviews[3 + i], lses[i], dil,
                                                        comms=[_rs_sibling([gw["w_out"]])], after=[token])
            (part["w_out"],) = _chip_sums([gw["w_out"]], [sib["w_out"]], shard_core)
            x_out = _rs_chips([part["w_out"]])
            (s_out, s_small, s_wsp), token = _split_starts("rs_out_small_start", [x_out, x_small, x_wsp], token)
        else:
            dq, dk, dv = _attn_bwd(qv, kv, vv, dviews[i], dviews[3 + i], lses[i], dil, after=[token])
        dqs.append(dq)
        dks.append(dk)
        dvs.append(dv)
    half, far, joined = {}, {}, {}
    far.update(zip(ffn, _split_wait("rs_ffn_wait", x_ffn, *s_ffn, dvs[-1])))
    dproj, grad_x, d_pre_mix = _inproj_bwd(dqs, dks, dvs, du, dvs_sgu, pos, xs, dx1, w_in_f, small["pre_mix_norm"])
    (far["w_out"],) = _split_wait("rs_out_wait", x_out, *s_out, grad_x)
    names = ffn + ("w_out",)
    half.update(zip(names, _final_sums([gw[n] for n in names], [sib[n] for n in names], [far[n] for n in names],
                                       shard_core)))
    packed_late = _pack({"pre_mix_norm": d_pre_mix}, SMALL_LATE, 8)
    gw["w_in"], (got, (slots_late,)) = _wgrad(
        h, dproj, full_tok, pl.BlockSpec((SEQ, IN_S), lambda s: (0, s)), (D_MODEL, IN_S), "wgrad_in",
        comms=[_rs_join([half[n] for n in names]), _small_exchange(packed_late)])
    joined.update(zip(names, got))
    ((sib["w_in"],),) = _comm_only("comm_rs_sibling_in", [_rs_sibling([gw["w_in"]])])
    (part["w_in"],) = _chip_sums([gw["w_in"]], [sib["w_in"]], shard_core)
    x_in = _rs_chips([part["w_in"]])
    (s_in,), token = _split_starts("rs_in_start", [x_in], small["pre_mix_norm"])

    grads, deltas, new_m, new_v = {}, {}, {}, {}

    def record(n, outs):
        grads[n], deltas[n], new_m[n], new_v[n] = (
            jnp.swapaxes(o[None], 1, 2) if n in flipped else o[None] for o in outs)

    def update(names, name, after):
        for n, outs in zip(names, _adamw_multi(
                [big(n, n) for n in names], [joined[n].reshape(big(n, n).shape) for n in names],
                [big("m_" + n, n) for n in names], [big("v_" + n, n) for n in names], name, after)):
            record(n, outs)

    update(names, "adamw_ffn_out", [token])
    (slots_early,) = _split_wait("small_early_wait", x_small, *s_small, new_v["w_out"])
    (slots_wsp,) = _split_wait("small_wsp_wait", x_wsp, *s_wsp, slots_early)
    (far["w_in"],) = _split_wait("rs_in_wait", x_in, *s_in, slots_wsp)
    (half["w_in"],) = _final_sums([gw["w_in"]], [sib["w_in"]], [far["w_in"]], shard_core)
    ((joined["w_in"],),) = _comm_only("comm_rs_join_in", [_rs_join([half["w_in"]])])
    update(("w_in",), "adamw_w_in", [])
    a[LOSS_ROW] = a["m_" + LOSS_ROW] = a["v_" + LOSS_ROW] = jnp.zeros((1, D_MODEL), F32)
    outs = _adamw_small(packed_wsp, slots_wsp, *[a[p + SMALL_WSP].reshape(wsp_view) for p in ("", "m_", "v_")], me)
    for dst, buf in zip((grads, deltas, new_m, new_v), outs):
        dst[SMALL_WSP] = buf.reshape(a[SMALL_WSP].shape)
    for names, rows, packed, slots in ((SMALL_EARLY, 8, packed_early, slots_early),
                                       (SMALL_LATE, 8, packed_late, slots_late)):
        outs = _adamw_small(packed, slots, _pack(a, names, rows), _pack({n: a["m_" + n] for n in names}, names, rows),
                            _pack({n: a["v_" + n] for n in names}, names, rows), me)
        for dst, buf in zip((grads, deltas, new_m, new_v), outs):
            dst.update(_unpack(buf, names, {n: a[n].shape for n in names}))
    loss = jnp.sum(grads[LOSS_ROW]) * np.float32(0.5 / D_MODEL)
    return (loss, grad_x[None], *[grads[n] for n in WEIGHTS], *[deltas[n] for n in WEIGHTS],
            *[new_m[n] for n in WEIGHTS], *[new_v[n] for n in WEIGHTS])
```

```python
import numpy as np
import jax
import jax.numpy as jnp
from jax import lax
from jax.experimental import pallas as pl
from jax.experimental.pallas import tpu as pltpu

F32 = jnp.float32
BF16 = jnp.bfloat16

SEQ = 2048
D_MODEL = 1024
HEAD_DIM = 64
ATTN_W = 512
SGU_W = 512
SGU_GROUPS = 8
CHUNK = 128
DILATIONS = (1, 4, 16)
N_SHARD = 4
IN_S = 640
OUT_S = 256
FF_S = 704
PROJ_W = N_SHARD * IN_S
FF = N_SHARD * FF_S
FF_CHUNKS = ((0, 1024), (1024, 2048), (2048, FF))
RMS_EPS = 1e-6
LN_EPS = 1e-5
ROPE_THETA = 500000.0
ATTN_SCALE = 1.0 / np.sqrt(HEAD_DIM)
NEG = -1e30
TM = 512
TM_FFN = 256
VMEM_LIMIT = 56 * 1024 * 1024

ADAM_LR = 0.001
ADAM_B1 = 0.9
ADAM_B2 = 0.999
ADAM_EPS = 1e-08
ADAM_WD = 0.01
ADAM_STEP = 10

MESH = pl.DeviceIdType.MESH
ANY = pl.BlockSpec(memory_space=pl.ANY)


def _dot(a, b):
    return jnp.dot(a, b, preferred_element_type=F32)


def _dot_nt(a, b):
    return lax.dot_general(a, b, (((1,), (1,)), ((), ())), preferred_element_type=F32)


def _dot_tn(a, b):
    return lax.dot_general(a, b, (((0,), (0,)), ((), ())), preferred_element_type=F32)


def _dot_exact(a, b):
    return jnp.dot(a, b, preferred_element_type=F32, precision=lax.Precision.HIGHEST)


def _dot_select(a, sel):
    hi = a.astype(BF16)
    lo = (a - hi.astype(F32)).astype(BF16)
    sel = sel.astype(BF16)
    return _dot(hi, sel) + _dot(lo, sel)


def _rms_stats(x):
    r = lax.rsqrt(jnp.mean(x * x, axis=-1, keepdims=True) + RMS_EPS)
    return x * r, r


def _rms_bwd(xh, r, gain, dy):
    dxh = dy * gain
    dx = r * (dxh - xh * jnp.mean(dxh * xh, axis=-1, keepdims=True))
    return dx, jnp.sum(dy * xh, axis=0, keepdims=True)


_ERF_ALPHA = (-2.72614225801306e-10, 2.77068142495902e-08, -2.10102402082508e-06, -5.69250639462346e-05,
              -7.34990630326855e-04, -2.95459980854025e-03, -1.60960333262415e-02)
_ERF_BETA = (-1.45660718464996e-05, -2.13374055278905e-04, -1.68282697438203e-03, -7.37332916720468e-03,
             -1.42647390514189e-02)


def _erf(x):
    x = jnp.clip(x, -4.0, 4.0)
    x2 = x * x
    p = jnp.full_like(x, _ERF_ALPHA[0])
    for a in _ERF_ALPHA[1:]:
        p = p * x2 + a
    q = jnp.full_like(x, _ERF_BETA[0])
    for b in _ERF_BETA[1:]:
        q = q * x2 + b
    return x * p / q


def _normal_cdf(x):
    return 0.5 * (1.0 + _erf(x * np.float32(1.0 / np.sqrt(2.0))))


def _gelu_grad(x, cdf):
    pdf = jnp.exp(-0.5 * x * x) * np.float32(1.0 / np.sqrt(2.0 * np.pi))
    return cdf + x * pdf


def _sigmoid(x):
    return 1.0 / (1.0 + jnp.exp(-x))


_INV_FREQ = tuple(float(np.float32(ROPE_THETA ** (-2.0 * j / 16.0))) for j in range(8))


def _rot_tables(pos):
    lane = lax.broadcasted_iota(jnp.int32, (1, 128), 1)
    d = lane & 63
    j = d & 7
    inv = jnp.zeros((1, 128), F32)
    for jj in range(8):
        inv = jnp.where(j == jj, _INV_FREQ[jj], inv)
    ang = pos.astype(F32) * inv
    c = jnp.cos(ang)
    s = jnp.sin(ang)
    cos_t = jnp.where(d < 16, c, 1.0)
    sin_a = jnp.where(d < 8, -s, 0.0)
    sin_b = jnp.where((d >= 8) & (d < 16), s, 0.0)
    return tuple(jnp.tile(t, (1, 4)) for t in (cos_t, sin_a, sin_b))


def _rope(x, tabs):
    cos_t, sin_a, sin_b = tabs
    return x * cos_t + pltpu.roll(x, 504, 1) * sin_a + pltpu.roll(x, 8, 1) * sin_b


def _rope_bwd(dy, tabs):
    cos_t, sin_a, sin_b = tabs
    return dy * cos_t + pltpu.roll(dy * sin_a, 8, 1) + pltpu.roll(dy * sin_b, 504, 1)


def _left_half():
    return lax.broadcasted_iota(jnp.int32, (CHUNK, CHUNK), 1) < HEAD_DIM


def _group_ones():
    lane = lax.broadcasted_iota(jnp.int32, (SGU_GROUPS, SGU_W), 1)
    row = lax.broadcasted_iota(jnp.int32, (SGU_GROUPS, SGU_W), 0)
    return ((lane >> 6) == row).astype(F32)


def _masked_spatial(w_ref):
    row = lax.broadcasted_iota(jnp.int32, (CHUNK, CHUNK), 0)
    col = lax.broadcasted_iota(jnp.int32, (CHUNK, CHUNK), 1)
    return [jnp.where(col <= row, w_ref[g], 0.0).astype(BF16) for g in range(SGU_GROUPS)]


def _sgu_core(u, vs, lg, lb, wm, bias_full):
    tm = u.shape[0]
    cdf_u, cdf_vs = _normal_cdf(u), _normal_cdf(vs)
    gu = u * cdf_u
    gv = vs * cdf_vs
    mu = jnp.mean(gv, axis=-1, keepdims=True)
    xc = gv - mu
    rstd = lax.rsqrt(jnp.mean(xc * xc, axis=-1, keepdims=True) + LN_EPS)
    xh = xc * rstd
    vnb = (xh * lg + lb).astype(BF16)
    left = _left_half()
    rows = []
    for c in range(tm // CHUNK):
        pieces = []
        for p in range(4):
            vp = vnb[c * CHUNK:(c + 1) * CHUNK, p * 128:(p + 1) * 128]
            pieces.append(jnp.where(left, _dot(wm[2 * p], vp), _dot(wm[2 * p + 1], vp)))
        rows.append(jnp.concatenate(pieces, axis=1) + bias_full)
    mixed = jnp.concatenate(rows, axis=0)
    return gu, xh, rstd, vnb, mixed, cdf_u, cdf_vs


def _resident(shape):
    n = len(shape)
    return pl.BlockSpec(shape, lambda *_: (0,) * n, pipeline_mode=pl.Buffered(1))


def _rows(ncol, tm=TM):
    return pl.BlockSpec((tm, ncol), lambda i: (i, 0))


def _acc(ncol, nrow=1):
    return pl.BlockSpec((nrow, ncol), lambda i: (0, 0))


HEAD_W = 128


def _view_rows(dil, width=ATTN_W, tm=TM):
    return pl.BlockSpec((tm // dil, dil * width), lambda i: (i, 0))


def _view_shape(dil, dtype, width=ATTN_W):
    return _sds((SEQ // dil, dil * width), dtype)


def _slab_scratch():
    return pltpu.VMEM((4, TM, 128), F32)


def _store_view(val, out_ref, slabs, dil):
    width = val.shape[1]
    for j in range(width // 128):
        slabs[j] = val[:, j * 128:(j + 1) * 128]
    for r in range(dil):
        for j in range(width // 128):
            c0 = r * width + j * 128
            out_ref[:, c0:c0 + 128] = slabs.at[j][pl.ds(r, TM // dil, stride=dil), :].astype(out_ref.dtype)


def _load_view(in_ref, slabs, dil, width=ATTN_W):
    for r in range(dil):
        for j in range(width // 128):
            c0 = r * width + j * 128
            slabs.at[j][pl.ds(r, TM // dil, stride=dil), :] = in_ref[:, c0:c0 + 128].astype(F32)
    return jnp.concatenate([slabs[j] for j in range(width // 128)], axis=1)


def _head_spread():
    m = lax.broadcasted_iota(jnp.int32, (HEAD_W, ATTN_W), 0)
    lane = lax.broadcasted_iota(jnp.int32, (HEAD_W, ATTN_W), 1)
    return (m == 16 * (lane >> 6)).astype(F32)


def _head_sum():
    lane = lax.broadcasted_iota(jnp.int32, (ATTN_W, HEAD_W), 0)
    m = lax.broadcasted_iota(jnp.int32, (ATTN_W, HEAD_W), 1)
    return ((lane >> 6) == (m >> 4)).astype(F32)


def _seq_params():
    return pltpu.CompilerParams(dimension_semantics=("arbitrary",), vmem_limit_bytes=VMEM_LIMIT)


def _sds(shape, dtype):
    return jax.ShapeDtypeStruct(shape, dtype)


class _Comm:
    def __init__(self, args, out_shape, n_sems, start, finish, aliased=False):
        self.args, self.out_shape, self.n_sems = list(args), list(out_shape), n_sems
        self.start, self.finish, self.aliased = start, finish, aliased


def _pcall(body, *, name, grid, in_specs, out_specs, out_shape, args, scratch_shapes=(), comms=(), after=()):
    single = not isinstance(out_shape, (list, tuple))
    out_specs = [out_specs] if single else list(out_specs)
    out_shape = [out_shape] if single else list(out_shape)
    n_in, n_out, n_scr = len(in_specs), len(out_shape), len(scratch_shapes)
    c_args = [a for c in comms for a in c.args]
    c_outs = [o for c in comms for o in c.out_shape]
    aliases, ai, ao = {}, n_in, n_out
    for c in comms:
        if c.aliased:
            aliases.update({ai + k: ao + k for k in range(len(c.args))})
        ai += len(c.args)
        ao += len(c.out_shape)
    sems = [pltpu.SemaphoreType.DMA((c.n_sems,)) for c in comms for _ in range(2)]
    steps = grid[0]

    def wrapped(*refs):
        o0 = n_in + len(c_args) + len(after)
        s0 = o0 + n_out + len(c_outs)
        m_in, m_out, m_sem = refs[n_in:n_in + len(c_args)], refs[o0 + n_out:s0], refs[s0 + n_scr:]

        def each(phase):
            ii = oi = 0
            for k, c in enumerate(comms):
                getattr(c, phase)(m_in[ii:ii + len(c.args)], m_out[oi:oi + len(c.out_shape)],
                                  m_sem[2 * k], m_sem[2 * k + 1])
                ii += len(c.args)
                oi += len(c.out_shape)

        if comms:
            @pl.when(pl.program_id(0) == 0)
            def _():
                each("start")

        body(*refs[:n_in], *refs[o0:o0 + n_out], *refs[s0:s0 + n_scr])

        if comms:
            @pl.when(pl.program_id(0) == steps - 1)
            def _():
                each("finish")

    res = pl.pallas_call(
        wrapped, name=name, grid=grid,
        in_specs=list(in_specs) + [ANY] * (len(c_args) + len(after)), out_specs=out_specs + [ANY] * len(c_outs),
        out_shape=out_shape + c_outs, scratch_shapes=list(scratch_shapes) + sems,
        input_output_aliases=aliases, compiler_params=_seq_params(),
    )(*args, *c_args, *after)
    mine = res[0] if single else list(res[:n_out])
    if not comms:
        return mine
    theirs, oi = [], n_out
    for c in comms:
        theirs.append(list(res[oi:oi + len(c.out_shape)]))
        oi += len(c.out_shape)
    return mine, theirs


def _in_pieces(g):
    lo, hi = 512 * g, 512 * (g + 1)
    return [(s, max(lo, IN_S * s) - IN_S * s, min(hi, IN_S * (s + 1)) - IN_S * s)
            for s in range(N_SHARD) if max(lo, IN_S * s) < min(hi, IN_S * (s + 1))]


def _inproj_fwd(x, pos, g_pre, w_in, lg, lb, w_sp, b_t, comms=()):
    def body(x_ref, pos_ref, g_ref, w_ref, lg_ref, lb_ref, wsp_ref, bt_ref, h_ref, u_ref, vs_ref, sgu_ref, *rest):
        qkv_refs, slabs = rest[:9], rest[9:]
        xh, _ = _rms_stats(x_ref[...])
        h = (xh * g_ref[...]).astype(BF16)
        h_ref[...] = h
        tabs = _rot_tables(pos_ref[...])

        def group(g):
            return jnp.concatenate([_dot(h, w_ref[s, :, a:b]) for s, a, b in _in_pieces(g)], axis=1)

        for t in range(3):
            val = group(t)
            if t < 2:
                val = _rope(val, tabs)
            if t == 0:
                val = val * np.float32(ATTN_SCALE)
            qkv_refs[t][...] = val.astype(BF16)
            for i, dil in enumerate(DILATIONS[1:]):
                _store_view(val, qkv_refs[3 * (i + 1) + t], slabs[t], dil)
        u = group(3)
        vs = group(4)
        u_ref[...] = u
        vs_ref[...] = vs
        bias_full = _dot_exact(bt_ref[...], _group_ones())
        gu, _, _, _, mixed, _, _ = _sgu_core(u, vs, lg_ref[...], lb_ref[...], _masked_spatial(wsp_ref), bias_full)
        sgu_ref[...] = gu * mixed

    return _pcall(
        body, name="inproj_sgu_fwd", grid=(SEQ // TM,),
        in_specs=[_rows(D_MODEL), _rows(1), _resident((1, D_MODEL)), _resident((N_SHARD, D_MODEL, IN_S)),
                  _resident((1, SGU_W)), _resident((1, SGU_W)), _resident((SGU_GROUPS, CHUNK, CHUNK)),
                  _resident((CHUNK, SGU_GROUPS))],
        out_specs=[_rows(D_MODEL), _rows(512), _rows(512), _rows(512)]
        + [_view_rows(dil) for dil in DILATIONS for _ in range(3)],
        out_shape=[_sds((SEQ, D_MODEL), BF16), _sds((SEQ, 512), F32), _sds((SEQ, 512), F32), _sds((SEQ, 512), F32)]
        + [_view_shape(dil, BF16) for dil in DILATIONS for _ in range(3)],
        scratch_shapes=[_slab_scratch() for _ in range(3)],
        args=(x, pos, g_pre, w_in, lg, lb, w_sp, b_t), comms=comms)


def _block_masks():
    row = lax.broadcasted_iota(jnp.int32, (CHUNK, CHUNK), 0)
    col = lax.broadcasted_iota(jnp.int32, (CHUNK, CHUNK), 1)
    return col <= row, col >= row


def _attn_fwd(qv, kv, vv, dil, comms=()):
    seg = SEQ // dil
    nblk = seg // CHUNK
    rps = 4 if nblk == 1 else 1

    def body(q_ref, k_ref, v_ref, o_ref, l_ref):
        left = _left_half()
        m_cur, m_prev = _block_masks()
        zero = jnp.zeros((CHUNK, CHUNK), BF16)
        ones = (jnp.where(left, 1.0, 0.0).astype(BF16), jnp.where(left, 0.0, 1.0).astype(BF16))

        sides = tuple(enumerate((left, ~left)))

        def rows(b):
            if isinstance(b, int):
                return b * CHUNK, max(b - 1, 0) * CHUNK
            return pl.multiple_of(b * CHUNK, CHUNK), pl.multiple_of(jnp.maximum(b - 1, 0) * CHUNK, CHUNK)

        def first(rr, b):
            r0, rp = rows(b)
            prev_ok = m_prev & (b > 0)
            tiles, scores = [], []
            for hp in range(4):
                ls = slice(rr * ATTN_W + hp * 128, rr * ATTN_W + (hp + 1) * 128)
                qp = q_ref[pl.ds(r0, CHUNK), ls]
                kc = k_ref[pl.ds(r0, CHUNK), ls]
                kp = k_ref[pl.ds(rp, CHUNK), ls] if nblk > 1 else None
                tiles.append((ls, v_ref[pl.ds(r0, CHUNK), ls], v_ref[pl.ds(rp, CHUNK), ls] if nblk > 1 else None))
                for _, hm in sides:
                    qh = jnp.where(hm, qp, zero)
                    sc = jnp.where(m_cur, _dot_nt(qh, kc), NEG)
                    sp = jnp.where(prev_ok, _dot_nt(qh, kp), NEG) if nblk > 1 else None
                    scores.append((sc, sp))
            return tiles, scores

        def second(scores):
            probs = []
            for sc, sp in scores:
                if nblk > 1:
                    m = jnp.max(jnp.maximum(sc, sp), axis=-1, keepdims=True)
                    pc = jnp.exp(sc - m)
                    pp = jnp.exp(sp - m)
                    probs.append((m, pc.astype(BF16), pp.astype(BF16), (pc + pp).astype(BF16)))
                else:
                    m = jnp.max(sc, axis=-1, keepdims=True)
                    pc = jnp.exp(sc - m).astype(BF16)
                    probs.append((m, pc, None, pc))
            return probs

        def third(rr, b, tiles, probs):
            r0, _ = rows(b)
            for hp, (ls, vc, vp) in enumerate(tiles):
                acc = jnp.zeros((CHUNK, CHUNK), F32)
                den = jnp.zeros((CHUNK, CHUNK), F32)
                for side, hm in sides:
                    _, pc, pp, psum = probs[2 * hp + side]
                    acc = acc + _dot(pc, jnp.where(hm, vc, zero))
                    if nblk > 1:
                        acc = acc + _dot(pp, jnp.where(hm, vp, zero))
                    den = den + _dot(psum, ones[side])
                o_ref[pl.ds(r0, CHUNK), ls] = (acc / den).astype(o_ref.dtype)
                lse = jnp.where(left, probs[2 * hp][0], probs[2 * hp + 1][0]) + jnp.log(den)
                l_ref[pl.ds(r0, CHUNK), rr * HEAD_W + 32 * hp:rr * HEAD_W + 32 * hp + 32] = lse[:, 48:80]

        def run(units):
            data = [first(rr, b) for rr, b in units]
            probs = [second(scores) for _, scores in data]
            for (rr, b), (tiles, _), pr in zip(units, data, probs):
                third(rr, b, tiles, pr)

        if nblk == 1:
            run([(rr, 0) for rr in range(rps)])
        else:
            def one(b, carry):
                run([(0, b)])
                return carry

            lax.fori_loop(0, nblk, one, 0)

    spec = pl.BlockSpec((seg, rps * ATTN_W), lambda r: (0, r))
    return _pcall(
        body, name=f"attn_fwd_d{dil}", grid=(dil // rps,),
        in_specs=[spec, spec, spec], out_specs=[spec, pl.BlockSpec((seg, rps * HEAD_W), lambda r: (0, r))],
        out_shape=[_sds((seg, dil * ATTN_W), BF16), _sds((seg, dil * HEAD_W), F32)],
        args=(qv, kv, vv), comms=comms)


def _lane_left(nrows):
    return lax.broadcasted_iota(jnp.int32, (nrows, CHUNK), 1) < HEAD_DIM


def _attn_rows(b):
    if isinstance(b, int):
        return b * CHUNK, max(b - 1, 0) * CHUNK
    return pl.multiple_of(b * CHUNK, CHUNK), pl.multiple_of(jnp.maximum(b - 1, 0) * CHUNK, CHUNK)


def _mix_out_fwd(o_list, l_list, sgu, x, w_out, g_attn, g_sgu, g_post, comms=()):
    def body(o1, o2, o3, l1, l2, l3, sgu_ref, x_ref, w_ref, ga_ref, gs_ref, gp_ref,
             attn_ref, mixed_ref, y_ref, x1_ref, lse1_ref, lse2_ref, lse3_ref, slabs_a, slabs_b):
        os = [o1[...], _load_view(o2, slabs_a, DILATIONS[1]), _load_view(o3, slabs_b, DILATIONS[2])]
        ls = [l1[...], _load_view(l2, slabs_a, DILATIONS[1], HEAD_W), _load_view(l3, slabs_b, DILATIONS[2], HEAD_W)]
        m = jnp.maximum(jnp.maximum(ls[0], ls[1]), ls[2])
        es = [jnp.exp(l - m) for l in ls]
        den = es[0] + es[1] + es[2]
        spread = _head_spread()
        attn = sum(_dot_select(e / den, spread) * o for e, o in zip(es, os))
        attn_ref[...] = attn
        lse = m + jnp.log(den)
        lse1_ref[...] = lse
        _store_view(lse, lse2_ref, slabs_a, DILATIONS[1])
        _store_view(lse, lse3_ref, slabs_b, DILATIONS[2])
        ah, _ = _rms_stats(attn)
        sh, _ = _rms_stats(sgu_ref[...])
        mixed = jnp.concatenate([ah * ga_ref[...], sh * gs_ref[...]], axis=1).astype(BF16)
        mixed_ref[...] = mixed
        y = _dot(mixed[:, 0:OUT_S], w_ref[0])
        for s in range(1, N_SHARD):
            y = y + _dot(mixed[:, s * OUT_S:(s + 1) * OUT_S], w_ref[s])
        y_ref[...] = y
        yh, _ = _rms_stats(y)
        x1_ref[...] = x_ref[...] + yh * gp_ref[...]

    return _pcall(
        body, name="mix_out_fwd", grid=(SEQ // TM,),
        in_specs=[_view_rows(dil) for dil in DILATIONS] + [_view_rows(dil, HEAD_W) for dil in DILATIONS]
        + [_rows(512), _rows(D_MODEL), _resident((N_SHARD, OUT_S, D_MODEL)),
           _resident((1, 512)), _resident((1, 512)), _resident((1, D_MODEL))],
        out_specs=[_rows(512), _rows(D_MODEL), _rows(D_MODEL), _rows(D_MODEL)]
        + [_view_rows(dil, HEAD_W) for dil in DILATIONS],
        out_shape=[_sds((SEQ, 512), F32), _sds((SEQ, D_MODEL), BF16), _sds((SEQ, D_MODEL), F32),
                   _sds((SEQ, D_MODEL), F32)] + [_view_shape(dil, F32, HEAD_W) for dil in DILATIONS],
        scratch_shapes=[_slab_scratch(), _slab_scratch()],
        args=(*o_list, *l_list, sgu, x, w_out, g_attn, g_sgu, g_post), comms=comms)


def _ffn_fwd_bwd(x1, target, w_gate, w_up, w_down, g_pre, g_post, comms=()):
    def body(x1_ref, t_ref, wg_ref, wu_ref, wd_ref, gpf_ref, gpo_ref,
             h2_ref, a_ref, dg_ref, dup_ref, df_ref, dx1_ref, loss_ref, dgpf_ref, dgpo_ref, g_scr, up_scr):
        @pl.when(pl.program_id(0) == 0)
        def _():
            loss_ref[...] = jnp.zeros_like(loss_ref)
            dgpf_ref[...] = jnp.zeros_like(dgpf_ref)
            dgpo_ref[...] = jnp.zeros_like(dgpo_ref)

        x1 = x1_ref[...]
        gpf = gpf_ref[...]
        gpo = gpo_ref[...]
        xh, r = _rms_stats(x1)
        h2 = (xh * gpf).astype(BF16)
        h2_ref[...] = h2
        f = jnp.zeros((TM_FFN, D_MODEL), F32)
        for c0, c1 in FF_CHUNKS:
            g = _dot_nt(h2, wg_ref[c0:c1, :])
            up = _dot_nt(h2, wu_ref[c0:c1, :])
            g_scr[:, c0:c1] = g
            up_scr[:, c0:c1] = up
            a = (g * _sigmoid(g) * up).astype(BF16)
            a_ref[:, c0:c1] = a
            f = f + _dot(a, wd_ref[c0:c1, :])
        fh, rf = _rms_stats(f)
        diff = x1 + fh * gpo - t_ref[...]
        loss_ref[...] += jnp.sum(diff * diff, axis=0, keepdims=True)
        dout = diff * np.float32(1.0 / D_MODEL)
        df, dgpo = _rms_bwd(fh, rf, gpo, dout)
        dgpo_ref[...] += dgpo
        dfb = df.astype(BF16)
        df_ref[...] = dfb
        dh2 = jnp.zeros((TM_FFN, D_MODEL), F32)
        for c0, c1 in FF_CHUNKS:
            da = _dot_nt(dfb, wd_ref[c0:c1, :])
            g = g_scr[:, c0:c1]
            up = up_scr[:, c0:c1]
            sg = _sigmoid(g)
            dup = (da * (g * sg)).astype(BF16)
            dg = (da * up * (sg * (1.0 + g * (1.0 - sg)))).astype(BF16)
            dg_ref[:, c0:c1] = dg
            dup_ref[:, c0:c1] = dup
            dh2 = dh2 + _dot(dg, wg_ref[c0:c1, :]) + _dot(dup, wu_ref[c0:c1, :])
        dx, dgpf = _rms_bwd(xh, r, gpf, dh2)
        dgpf_ref[...] += dgpf
        dx1_ref[...] = dout + dx

    return _pcall(
        body, name="ffn_fwd_bwd", grid=(SEQ // TM_FFN,),
        in_specs=[_rows(D_MODEL, TM_FFN), _rows(D_MODEL, TM_FFN), _resident((FF, D_MODEL)),
                  _resident((FF, D_MODEL)), _resident((FF, D_MODEL)),
                  _resident((1, D_MODEL)), _resident((1, D_MODEL))],
        out_specs=[_rows(D_MODEL, TM_FFN), _rows(FF, TM_FFN), _rows(FF, TM_FFN), _rows(FF, TM_FFN),
                   _rows(D_MODEL, TM_FFN), _rows(D_MODEL, TM_FFN), _acc(D_MODEL), _acc(D_MODEL), _acc(D_MODEL)],
        out_shape=[_sds((SEQ, D_MODEL), BF16), _sds((SEQ, FF), BF16), _sds((SEQ, FF), BF16),
                   _sds((SEQ, FF), BF16), _sds((SEQ, D_MODEL), BF16), _sds((SEQ, D_MODEL), F32),
                   _sds((1, D_MODEL), F32), _sds((1, D_MODEL), F32), _sds((1, D_MODEL), F32)],
        scratch_shapes=[pltpu.VMEM((TM_FFN, FF), F32), pltpu.VMEM((TM_FFN, FF), F32)],
        args=(x1, target, w_gate, w_up, w_down, g_pre, g_post), comms=comms)


def _wgrad(a, b, a_spec, b_spec, out_block, name, comms=()):
    def body(a_ref, b_ref, o_ref, ob_ref):
        av = a_ref[0] if len(a_ref.shape) == 3 else a_ref[...]
        bv = b_ref[0] if len(b_ref.shape) == 3 else b_ref[...]
        g = _dot_tn(av, bv)
        o_ref[0] = g
        ob_ref[0] = g.astype(BF16)

    return _pcall(
        body, name=name, grid=(N_SHARD,),
        in_specs=[a_spec, b_spec],
        out_specs=[pl.BlockSpec((1,) + out_block, lambda s: (s, 0, 0))] * 2,
        out_shape=[_sds((N_SHARD,) + out_block, F32), _sds((N_SHARD,) + out_block, BF16)],
        args=(a, b), comms=comms)


def _outproj_bwd(dx1, y, attn, sgu, w_out, g_post, g_attn, g_sgu, u, vs, lg, lb, w_sp, b_t, comms=(), after=()):
    sgu_bwd = _sgu_bwd_body()

    def body(dx1_ref, y_ref, attn_ref, sgu_ref, w_ref, gp_ref, ga_ref, gs_ref, u_ref, vs_ref, lg_ref, lb_ref,
             wsp_ref, bt_ref, dy_ref, du_ref, dvs_ref, dgp_ref, dga_ref, dgs_ref, dw_ref, db_ref, dlg_ref, dlb_ref,
             *rest):
        dattn_refs, delta_refs, (slabs_a, slabs_b, dsgu_ref, dbias_scr) = rest[0:3], rest[3:6], rest[6:]

        @pl.when(pl.program_id(0) == 0)
        def _():
            dgp_ref[...] = jnp.zeros_like(dgp_ref)
            dga_ref[...] = jnp.zeros_like(dga_ref)
            dgs_ref[...] = jnp.zeros_like(dgs_ref)

        yh, ry = _rms_stats(y_ref[...])
        dy, dgp = _rms_bwd(yh, ry, gp_ref[...], dx1_ref[...])
        dgp_ref[...] += dgp
        dyb = dy.astype(BF16)
        dy_ref[...] = dyb
        dmixed = jnp.concatenate([_dot_nt(dyb, w_ref[s]) for s in range(N_SHARD)], axis=1)
        attn = attn_ref[...]
        ah, ra = _rms_stats(attn)
        dattn, dga = _rms_bwd(ah, ra, ga_ref[...], dmixed[:, 0:512])
        dga_ref[...] += dga
        sh, rs = _rms_stats(sgu_ref[...])
        dsgu, dgs = _rms_bwd(sh, rs, gs_ref[...], dmixed[:, 512:1024])
        dgs_ref[...] += dgs
        dsgu_ref[...] = dsgu
        delta = _dot_select(dattn * attn, _head_sum())
        dattn_refs[0][...] = dattn.astype(BF16)
        delta_refs[0][...] = delta
        for i, dil in enumerate(DILATIONS[1:]):
            _store_view(dattn, dattn_refs[i + 1], slabs_a, dil)
            _store_view(delta, delta_refs[i + 1], slabs_b, dil)
        sgu_bwd(u_ref, vs_ref, dsgu_ref, lg_ref, lb_ref, wsp_ref, bt_ref,
                du_ref, dvs_ref, dw_ref, db_ref, dlg_ref, dlb_ref, dbias_scr)

    return _pcall(
        body, name="outproj_sgu_bwd", grid=(SEQ // TM,),
        in_specs=[_rows(D_MODEL), _rows(D_MODEL), _rows(512), _rows(512), _resident((N_SHARD, OUT_S, D_MODEL)),
                  _resident((1, D_MODEL)), _resident((1, 512)), _resident((1, 512)),
                  _rows(SGU_W), _rows(SGU_W), _resident((1, SGU_W)), _resident((1, SGU_W)),
                  _resident((SGU_GROUPS, CHUNK, CHUNK)), _resident((CHUNK, SGU_GROUPS))],
        out_specs=[_rows(D_MODEL), _rows(SGU_W), _rows(SGU_W), _acc(D_MODEL), _acc(512), _acc(512),
                   pl.BlockSpec((SGU_GROUPS, CHUNK, CHUNK), lambda i: (0, 0, 0)), _acc(CHUNK, SGU_GROUPS),
                   _acc(SGU_W), _acc(SGU_W)]
        + [_view_rows(dil) for dil in DILATIONS] + [_view_rows(dil, HEAD_W) for dil in DILATIONS],
        out_shape=[_sds((SEQ, D_MODEL), BF16), _sds((SEQ, SGU_W), BF16), _sds((SEQ, SGU_W), BF16),
                   _sds((1, D_MODEL), F32), _sds((1, 512), F32), _sds((1, 512), F32),
                   _sds((SGU_GROUPS, CHUNK, CHUNK), F32), _sds((SGU_GROUPS, CHUNK), F32),
                   _sds((1, SGU_W), F32), _sds((1, SGU_W), F32)]
        + [_view_shape(dil, BF16) for dil in DILATIONS] + [_view_shape(dil, F32, HEAD_W) for dil in DILATIONS],
        scratch_shapes=[_slab_scratch(), _slab_scratch(), pltpu.VMEM((TM, SGU_W), F32),
                        pltpu.VMEM((CHUNK, SGU_W), F32)],
        args=(dx1, y, attn, sgu, w_out, g_post, g_attn, g_sgu, u, vs, lg, lb, w_sp, b_t), comms=comms, after=after)


def _sgu_bwd_body():
    nsteps = SEQ // TM

    def body(u_ref, vs_ref, ds_ref, lg_ref, lb_ref, w_ref, bt_ref,
             du_ref, dvs_ref, dw_ref, db_ref, dlg_ref, dlb_ref, dbias_scr):
        i = pl.program_id(0)

        @pl.when(i == 0)
        def _():
            dw_ref[...] = jnp.zeros_like(dw_ref)
            dlg_ref[...] = jnp.zeros_like(dlg_ref)
            dlb_ref[...] = jnp.zeros_like(dlb_ref)
            dbias_scr[...] = jnp.zeros_like(dbias_scr)

        wm = _masked_spatial(w_ref)
        ones_g = _group_ones()
        bias_full = _dot_exact(bt_ref[...], ones_g)
        u = u_ref[...]
        vs = vs_ref[...]
        lg = lg_ref[...]
        gu, xh, rstd, vnb, mixed, cdf_u, cdf_vs = _sgu_core(u, vs, lg, lb_ref[...], wm, bias_full)
        dsgu = ds_ref[...]
        du_ref[...] = (dsgu * mixed * _gelu_grad(u, cdf_u)).astype(BF16)
        dmixed = dsgu * gu
        left = _left_half()
        dvn_rows = []
        for c in range(TM // CHUNK):
            rs = slice(c * CHUNK, (c + 1) * CHUNK)
            dm_c = dmixed[rs, :]
            dbias_scr[...] += dm_c
            pieces = []
            for p in range(4):
                ls = slice(p * 128, (p + 1) * 128)
                dmp = dm_c[:, ls]
                vp = vnb[rs, ls]
                dmb = dmp.astype(BF16)
                zero = jnp.zeros_like(dmb)
                dw_ref[2 * p] += _dot_nt(jnp.where(left, dmb, zero), vp)
                dw_ref[2 * p + 1] += _dot_nt(jnp.where(left, zero, dmb), vp)
                pieces.append(jnp.where(left, _dot_tn(wm[2 * p], dmb), _dot_tn(wm[2 * p + 1], dmb)))
            dvn_rows.append(jnp.concatenate(pieces, axis=1))
        dvn = jnp.concatenate(dvn_rows, axis=0)
        dlg_ref[...] += jnp.sum(dvn * xh, axis=0, keepdims=True)
        dlb_ref[...] += jnp.sum(dvn, axis=0, keepdims=True)
        dxh = dvn * lg
        dgv = rstd * (dxh - jnp.mean(dxh, axis=-1, keepdims=True) - xh * jnp.mean(dxh * xh, axis=-1, keepdims=True))
        dvs_ref[...] = (dgv * _gelu_grad(vs, cdf_vs)).astype(BF16)

        @pl.when(i == nsteps - 1)
        def _():
            row = lax.broadcasted_iota(jnp.int32, (CHUNK, CHUNK), 0)
            col = lax.broadcasted_iota(jnp.int32, (CHUNK, CHUNK), 1)
            for g in range(SGU_GROUPS):
                dw_ref[g] = jnp.where(col <= row, dw_ref[g], 0.0)
            db_ref[...] = lax.dot_general(ones_g, dbias_scr[...], (((1,), (1,)), ((), ())),
                                          preferred_element_type=F32, precision=lax.Precision.HIGHEST)

    return body


def _attn_bwd(qv, kv, vv, dov, deltav, lsev, dil, comms=(), after=()):
    seg = SEQ // dil
    nblk = seg // CHUNK
    rps = 4 if nblk == 1 else 1

    def body(q_ref, k_ref, v_ref, do_ref, dl_ref, lse_ref, dq_ref, dk_ref, dv_ref, dk_wait, dv_wait):
        left = _left_half()
        m_cur, m_prev = _block_masks()
        sides = tuple(enumerate((left, ~left)))
        zero = jnp.zeros((CHUNK, CHUNK), BF16)

        def first(rr, b, both):
            r0, rp = _attn_rows(b)
            keys = pl.ds(rp, 2 * CHUNK) if both else pl.ds(r0, CHUNK)
            tiles, heads = [], []
            for hp in range(4):
                ls = slice(rr * ATTN_W + hp * 128, rr * ATTN_W + (hp + 1) * 128)
                qp = q_ref[pl.ds(r0, CHUNK), ls]
                dop = do_ref[pl.ds(r0, CHUNK), ls]
                k2 = k_ref[keys, ls]
                v2 = v_ref[keys, ls]
                tiles.append((ls, k2))
                for _, hm in sides:
                    qh = jnp.where(hm, qp, zero)
                    doh = jnp.where(hm, dop, zero)
                    heads.append((qh, doh, _dot_nt(k2, qh), _dot_nt(v2, doh)))
            return tiles, heads

        def second(rr, b, heads, both):
            r0, _ = _attn_rows(b)
            ok = jnp.concatenate([m_cur, m_prev], axis=0) if both else m_prev
            lanes = slice(rr * HEAD_W, (rr + 1) * HEAD_W)
            lse_t = lse_ref[pl.ds(r0, CHUNK), lanes].T
            dl_t = dl_ref[pl.ds(r0, CHUNK), lanes].T
            out = []
            for i, (_, _, s_t, dp_t) in enumerate(heads):
                p = jnp.exp(jnp.where(ok, s_t - lse_t[16 * i:16 * i + 1, :], NEG))
                out.append((p.astype(BF16), (p * (dp_t - dl_t[16 * i:16 * i + 1, :])).astype(BF16)))
            return out

        def third(rr, b, tiles, heads, probs, both):
            r0, rp = _attn_rows(b)
            nk = 2 * CHUNK if both else CHUNK
            left_k = _lane_left(nk)
            zero_k = jnp.zeros((nk, CHUNK), BF16)
            for hp, (ls, k2) in enumerate(tiles):
                dq = jnp.zeros((CHUNK, CHUNK), F32)
                dk2 = jnp.zeros((nk, CHUNK), F32)
                dv2 = jnp.zeros((nk, CHUNK), F32)
                for side in range(2):
                    qh, doh, _, _ = heads[2 * hp + side]
                    p, ds = probs[2 * hp + side]
                    dq = dq + _dot_tn(ds, jnp.where(left_k if side == 0 else ~left_k, k2, zero_k))
                    dk2 = dk2 + _dot(ds, qh)
                    dv2 = dv2 + _dot(p, doh)
                dq_ref[pl.ds(r0, CHUNK), ls] = dq.astype(dq_ref.dtype)
                if nblk == 1:
                    dk_ref[pl.ds(r0, CHUNK), ls] = dk2.astype(dk_ref.dtype)
                    dv_ref[pl.ds(r0, CHUNK), ls] = dv2.astype(dv_ref.dtype)
                elif both:
                    dk_ref[pl.ds(rp, CHUNK), ls] = (dk_wait[:, ls] + dk2[0:CHUNK]).astype(dk_ref.dtype)
                    dv_ref[pl.ds(rp, CHUNK), ls] = (dv_wait[:, ls] + dv2[0:CHUNK]).astype(dv_ref.dtype)
                    dk_wait[:, ls] = dk2[CHUNK:]
                    dv_wait[:, ls] = dv2[CHUNK:]
                else:
                    dk_wait[:, ls] = dk2
                    dv_wait[:, ls] = dv2

        def run(units, both):
            data = [first(rr, b, both) for rr, b in units]
            probs = [second(rr, b, heads, both) for (rr, b), (_, heads) in zip(units, data)]
            for (rr, b), (tiles, heads), pr in zip(units, data, probs):
                third(rr, b, tiles, heads, pr, both)

        run([(rr, 0) for rr in range(rps)], False)
        if nblk > 1:
            def one(b, carry):
                run([(0, b)], True)
                return carry

            lax.fori_loop(1, nblk, one, 0)
            last = (nblk - 1) * CHUNK
            dk_ref[last:last + CHUNK, :] = dk_wait[...].astype(dk_ref.dtype)
            dv_ref[last:last + CHUNK, :] = dv_wait[...].astype(dv_ref.dtype)

    spec = pl.BlockSpec((seg, rps * ATTN_W), lambda r: (0, r))
    return _pcall(
        body, name=f"attn_bwd_d{dil}", grid=(dil // rps,),
        in_specs=[spec] * 4 + [pl.BlockSpec((seg, rps * HEAD_W), lambda r: (0, r))] * 2, out_specs=[spec] * 3,
        out_shape=[_sds((seg, dil * ATTN_W), BF16)] * 3,
        scratch_shapes=[pltpu.VMEM((CHUNK, ATTN_W), F32), pltpu.VMEM((CHUNK, ATTN_W), F32)],
        args=(qv, kv, vv, dov, deltav, lsev), comms=comms, after=after)


def _inproj_bwd(dqs, dks, dvs, du, dvs_sgu, pos, x, dx1, w_in, g_pre, comms=()):
    def body(dq1, dq2, dq3, dk1, dk2, dk3, dv1, dv2, dv3, du_ref, dvs_ref, pos_ref, x_ref, dx1_ref, w_ref, g_ref,
             dproj_ref, gx_ref, dg_ref, slabs_a, slabs_b):
        @pl.when(pl.program_id(0) == 0)
        def _():
            dg_ref[...] = jnp.zeros_like(dg_ref)

        def total(r1, r2, r3):
            return r1[...] + _load_view(r2, slabs_a, DILATIONS[1]) + _load_view(r3, slabs_b, DILATIONS[2])

        tabs = _rot_tables(pos_ref[...])
        groups = {3: du_ref[...], 4: dvs_ref[...]}
        dh = jnp.zeros((TM, D_MODEL), F32)
        for g in (3, 4, 0, 1, 2):
            if g == 0:
                groups[g] = _rope_bwd(total(dq1, dq2, dq3) * np.float32(ATTN_SCALE), tabs).astype(BF16)
            elif g == 1:
                groups[g] = _rope_bwd(total(dk1, dk2, dk3), tabs).astype(BF16)
            elif g == 2:
                groups[g] = total(dv1, dv2, dv3).astype(BF16)
            dproj_ref[:, 512 * g:512 * (g + 1)] = groups[g]
            off = 0
            for s, a, b in _in_pieces(g):
                dh = dh + _dot_nt(groups[g][:, off:off + b - a], w_ref[s, :, a:b])
                off += b - a
        g = g_ref[...]
        xh, r = _rms_stats(x_ref[...])
        dx, dg = _rms_bwd(xh, r, g, dh)
        dg_ref[...] += dg
        gx_ref[...] = dx1_ref[...] + dx

    return _pcall(
        body, name="inproj_bwd", grid=(SEQ // TM,),
        in_specs=[_view_rows(dil) for dil in DILATIONS] * 3
        + [_rows(512), _rows(512), _rows(1), _rows(D_MODEL), _rows(D_MODEL),
           _resident((N_SHARD, D_MODEL, IN_S)), _resident((1, D_MODEL))],
        out_specs=[_rows(PROJ_W), _rows(D_MODEL), _acc(D_MODEL)],
        out_shape=[_sds((SEQ, PROJ_W), BF16), _sds((SEQ, D_MODEL), F32), _sds((1, D_MODEL), F32)],
        scratch_shapes=[_slab_scratch(), _slab_scratch()],
        args=(*dqs, *dks, *dvs, du, dvs_sgu, pos, x, dx1, w_in, g_pre), comms=comms)


def _coords():
    return lax.axis_index("x"), lax.axis_index("y"), lax.axis_index("c")


def _other_chips(x, y):
    return [(1 - x, y), (x, 1 - y), (1 - x, 1 - y)]


WEIGHTS = ("pre_mix_norm", "w_in", "sgu_ln_gain", "sgu_ln_bias", "sgu_w_spatial", "sgu_b_spatial", "attn_out_norm",
           "sgu_out_norm", "w_out", "post_mix_norm", "pre_ffn_norm", "w_gate", "w_up", "w_down", "post_ffn_norm")
SMALL = ("pre_mix_norm", "post_mix_norm", "pre_ffn_norm", "post_ffn_norm", "sgu_ln_gain", "sgu_ln_bias",
         "attn_out_norm", "sgu_out_norm", "sgu_w_spatial", "sgu_b_spatial")


def _remote(src, dst, send_sem, recv_sem, to):
    return pltpu.make_async_remote_copy(src_ref=src, dst_ref=dst, send_sem=send_sem, recv_sem=recv_sem,
                                        device_id=to, device_id_type=MESH)


def _halves(a):
    *lead, rows, cols = a.shape
    return a.reshape(*lead, 2, rows // 2, cols)


def _gather_ici(shards):
    n = len(shards)

    def desc(ins, outs, ss, rs, j, w, landed):
        x, y, c = _coords()
        cx, cy = _other_chips(x, y)[j]
        shard = 2 * cx + cy if landed else 2 * x + y
        return _remote(ins[w].at[c], outs[w].at[shard, c], ss.at[j * n + w], rs.at[j * n + w], (cx, cy, c))

    def start(ins, outs, ss, rs):
        for j in range(3):
            for w in range(n):
                desc(ins, outs, ss, rs, j, w, False).start()

    def finish(ins, outs, ss, rs):
        for j in range(3):
            for w in range(n):
                desc(ins, outs, ss, rs, j, w, True).wait_recv()
                desc(ins, outs, ss, rs, j, w, False).wait_send()

    return _Comm(shards, [_sds((N_SHARD,) + s.shape, s.dtype) for s in shards], 3 * n, start, finish)


def _gather_pass(fulls):
    n = len(fulls)

    def desc(bufs, ss, rs, j, w, landed):
        x, y, c = _coords()
        cx, cy = _other_chips(x, y)[j]
        shard = 2 * cx + cy
        return _remote(bufs[w].at[shard, c], bufs[w].at[shard, 1 - c if landed else c],
                       ss.at[j * n + w], rs.at[j * n + w], (x, y, 1 - c))

    def start(ins, outs, ss, rs):
        for j in range(3):
            for w in range(n):
                desc(outs, ss, rs, j, w, False).start()

    def finish(ins, outs, ss, rs):
        for j in range(3):
            for w in range(n):
                desc(outs, ss, rs, j, w, True).wait_recv()
                desc(outs, ss, rs, j, w, False).wait_send()

    return _Comm(fulls, [_sds(f.shape, f.dtype) for f in fulls], 3 * n, start, finish, aliased=True)


def _rs_sibling(gws):
    n = len(gws)

    def desc(ins, outs, ss, rs, w):
        x, y, c = _coords()
        return _remote(ins[w].at[:, 1 - c], outs[w], ss.at[w], rs.at[w], (x, y, 1 - c))

    def start(ins, outs, ss, rs):
        for w in range(n):
            desc(ins, outs, ss, rs, w).start()

    def finish(ins, outs, ss, rs):
        for w in range(n):
            desc(ins, outs, ss, rs, w).wait()

    out_shape = [_sds((N_SHARD, g.shape[1] // 2, g.shape[2]), g.dtype) for g in gws]
    return _Comm([_halves(g) for g in gws], out_shape, n, start, finish)


def _rs_chips(pbs):
    n = len(pbs)

    def desc(ins, outs, ss, rs, j, w):
        x, y, c = _coords()
        cx, cy = _other_chips(x, y)[j]
        return _remote(ins[w].at[2 * cx + cy], outs[w].at[j], ss.at[j * n + w], rs.at[j * n + w], (cx, cy, c))

    def start(ins, outs, ss, rs):
        for j in range(3):
            for w in range(n):
                desc(ins, outs, ss, rs, j, w).start()

    def finish(ins, outs, ss, rs):
        for j in range(3):
            for w in range(n):
                desc(ins, outs, ss, rs, j, w).wait()

    return _Comm(pbs, [_sds((3,) + p.shape[1:], p.dtype) for p in pbs], 3 * n, start, finish)


def _rs_join(halves):
    n = len(halves)

    def desc(bufs, ss, rs, w, landed):
        x, y, c = _coords()
        return _remote(bufs[w].at[c], bufs[w].at[1 - c if landed else c], ss.at[w], rs.at[w], (x, y, 1 - c))

    def start(ins, outs, ss, rs):
        for w in range(n):
            desc(outs, ss, rs, w, False).start()

    def finish(ins, outs, ss, rs):
        for w in range(n):
            desc(outs, ss, rs, w, True).wait_recv()
            desc(outs, ss, rs, w, False).wait_send()

    return _Comm(halves, [_sds(h.shape, h.dtype) for h in halves], n, start, finish, aliased=True)


def _small_exchange(buf):
    def desc(ins, outs, ss, rs, k, landed):
        x, y, c = _coords()
        px = 1 - x if (k >> 2) & 1 else x
        py = 1 - y if (k >> 1) & 1 else y
        pc = 1 - c if k & 1 else c
        slot = 4 * px + 2 * py + pc if landed else 4 * x + 2 * y + c
        return _remote(ins[0], outs[0].at[slot], ss.at[k - 1], rs.at[k - 1], (px, py, pc))

    def start(ins, outs, ss, rs):
        for k in range(1, 8):
            desc(ins, outs, ss, rs, k, False).start()

    def finish(ins, outs, ss, rs):
        for k in range(1, 8):
            desc(ins, outs, ss, rs, k, True).wait_recv()
            desc(ins, outs, ss, rs, k, False).wait_send()

    return _Comm([buf], [_sds((8,) + buf.shape, buf.dtype)], 7, start, finish)


HBM = pl.BlockSpec(memory_space=pltpu.HBM)
SEM = pl.BlockSpec(memory_space=pltpu.SEMAPHORE)
DATAFLOW = pltpu.SideEffectType.DATAFLOW_SIDE_EFFECTING


def _split_starts(name, comms, after):
    srcs = [[pltpu.with_memory_space_constraint(s, pltpu.HBM) for s in c.args] for c in comms]
    lands = [[pltpu.with_memory_space_constraint(lax.empty(o.shape, o.dtype), pltpu.HBM) for o in c.out_shape]
             for c in comms]
    bufs = [b for k in range(len(comms)) for b in srcs[k] + lands[k]]
    nb, nc = len(bufs), len(comms)

    def body(*refs):
        sems = refs[nb + 1:nb + 1 + 2 * nc]
        off = 0
        for k, c in enumerate(comms):
            ns, nl = len(srcs[k]), len(lands[k])
            c.start(refs[off:off + ns], refs[off + ns:off + ns + nl], sems[2 * k], sems[2 * k + 1])
            off += ns + nl
        refs[-1][...] = jnp.zeros_like(refs[-1])

    res = pl.pallas_call(
        body, name=name,
        out_shape=(*[pltpu.SemaphoreType.DMA((c.n_sems,)) for c in comms for _ in range(2)],
                   *[pltpu.HBM(b.shape, b.dtype) for b in bufs], _sds((8, 128), F32)),
        in_specs=[HBM] * nb + [ANY],
        out_specs=(*[SEM] * (2 * nc), *[HBM] * nb, pl.BlockSpec(memory_space=pltpu.VMEM)),
        input_output_aliases={i: 2 * nc + i for i in range(nb)},
        compiler_params=pltpu.CompilerParams(has_side_effects=DATAFLOW),
    )(*bufs, after)
    states, off = [], 2 * nc
    for k in range(nc):
        ns, nl = len(srcs[k]), len(lands[k])
        states.append((res[2 * k], res[2 * k + 1], list(res[off:off + ns]), list(res[off + ns:off + ns + nl])))
        off += ns + nl
    return states, res[-1]


def _split_wait(name, comm, send_sems, recv_sems, srcs, lands, after):
    ns, nb = len(srcs), len(lands)
    after = list(after) if isinstance(after, (list, tuple)) else [after]

    def body(*refs):
        comm.finish(refs[:ns], refs[ns:ns + nb], refs[ns + nb], refs[ns + nb + 1])

    res = pl.pallas_call(
        body, name=name,
        out_shape=tuple(pltpu.HBM(b.shape, b.dtype) for b in srcs + lands),
        in_specs=[HBM] * (ns + nb) + [SEM, SEM] + [ANY] * len(after), out_specs=tuple([HBM] * (ns + nb)),
        input_output_aliases={i: i for i in range(ns + nb)},
        compiler_params=pltpu.CompilerParams(has_side_effects=DATAFLOW),
    )(*srcs, *lands, send_sems, recv_sems, *after)
    return list(res[ns:])


LOSS_ROW = "loss_cols"
SMALL_EARLY = ("post_mix_norm", "pre_ffn_norm", "post_ffn_norm", "sgu_ln_gain", "sgu_ln_bias", "attn_out_norm",
               "sgu_out_norm", "sgu_b_spatial", LOSS_ROW)
SMALL_LATE = ("pre_mix_norm",)
SMALL_WSP = "sgu_w_spatial"


def _pack(d, names, rows):
    flat = [d[n].reshape(-1) for n in names]
    used = sum(f.shape[0] for f in flat)
    flat.append(jnp.zeros((rows * 1024 - used,), F32))
    return jnp.concatenate(flat).reshape(rows, 1024)


def _unpack(buf, names, shapes):
    flat = buf.reshape(-1)
    out, off = {}, 0
    for n in names:
        size = int(np.prod(shapes[n]))
        out[n] = flat[off:off + size].reshape(shapes[n])
        off += size
    return out


def _chip_sums(gws, recvs, shard_core):
    n = len(gws)
    _, rows, cols = gws[0].shape
    hw = rows // 2

    def body(sc_ref, *refs):
        for k in range(n):
            refs[2 * n + k][...] = (refs[k][...] + refs[n + k][...]).astype(BF16)

    def other(s, sc):
        return jnp.where(s >= sc[0], s + 1, s)

    mine = pl.BlockSpec((1, hw, cols), lambda s, sc: (other(s, sc), sc[1], 0))
    plain = pl.BlockSpec((1, hw, cols), lambda s, sc: (other(s, sc), 0, 0))
    return pl.pallas_call(
        body, name="rs_chip_sums",
        grid_spec=pltpu.PrefetchScalarGridSpec(num_scalar_prefetch=1, grid=(N_SHARD - 1,),
                                               in_specs=[mine] * n + [plain] * n, out_specs=[plain] * n),
        out_shape=[_sds((N_SHARD, hw, cols), BF16)] * n,
        compiler_params=_seq_params(),
    )(shard_core, *gws, *recvs)


def _final_sums(gws, recv_sibs, recv_chips, shard_core):
    n = len(gws)
    halves = [(g.shape[1] // 2, g.shape[2]) for g in gws]

    def body(sc_ref, *refs):
        for k in range(n):
            acc = refs[k][0] + refs[n + k][0]
            for j in range(3):
                acc = acc + refs[2 * n + k][j].astype(F32)
            refs[3 * n + k][0] = acc

    def specs(lead, index):
        return [pl.BlockSpec((lead, hw, cols), index) for hw, cols in halves]

    return pl.pallas_call(
        body, name="rs_final_sums",
        grid_spec=pltpu.PrefetchScalarGridSpec(
            num_scalar_prefetch=1, grid=(1,),
            in_specs=specs(1, lambda i, sc: (sc[0], sc[1], 0)) + specs(1, lambda i, sc: (sc[0], 0, 0))
            + specs(3, lambda i, sc: (0, 0, 0)),
            out_specs=specs(1, lambda i, sc: (sc[1], 0, 0))),
        out_shape=[_sds((2, hw, cols), F32) for hw, cols in halves],
        compiler_params=_seq_params(),
    )(shard_core, *gws, *recv_sibs, *recv_chips)


ADAM_BLOCKS = 4


def _adamw_multi(ws, gs, ms, vs, name, after=()):
    n = len(ws)

    def body(*refs):
        outs = refs[4 * n + len(after):]
        for k in range(n):
            g = refs[n + k][...]
            d, m, v = _adam_math(refs[k][...], g, refs[2 * n + k][...], refs[3 * n + k][...])
            outs[4 * k][...], outs[4 * k + 1][...], outs[4 * k + 2][...], outs[4 * k + 3][...] = g, d, m, v

    specs = [pl.BlockSpec((w.shape[0] // ADAM_BLOCKS, w.shape[1]), lambda i: (i, 0)) for w in ws]
    res = pl.pallas_call(
        body, name=name, grid=(ADAM_BLOCKS,), in_specs=specs * 4 + [ANY] * len(after),
        out_specs=[s for s in specs for _ in range(4)],
        out_shape=[_sds(w.shape, F32) for w in ws for _ in range(4)],
        compiler_params=_seq_params(),
    )(*ws, *gs, *ms, *vs, *after)
    return [tuple(res[4 * k:4 * k + 4]) for k in range(n)]


def _wgrad_ff(a_list, b, name):
    n = len(a_list)
    cols = 256

    def body(*refs):
        bv = refs[n][...]
        for k in range(n):
            g = _dot_tn(refs[k][...], bv)
            refs[n + 1 + k][...] = g
            refs[2 * n + 1 + k][...] = g.astype(BF16)

    res = _pcall(
        body, name=name, grid=(FF // cols,),
        in_specs=[pl.BlockSpec((SEQ, cols), lambda j: (0, j))] * n
        + [pl.BlockSpec((SEQ, D_MODEL), lambda j: (0, 0), pipeline_mode=pl.Buffered(1))],
        out_specs=[pl.BlockSpec((cols, D_MODEL), lambda j: (j, 0))] * (2 * n),
        out_shape=[_sds((FF, D_MODEL), F32)] * n + [_sds((FF, D_MODEL), BF16)] * n, args=(*a_list, b))
    return [m.reshape(N_SHARD, FF_S, D_MODEL) for m in res]


def _comm_only(name, comms):
    return _pcall(lambda: None, name=name, grid=(1,), in_specs=[], out_specs=[], out_shape=[], args=(),
                  comms=comms)[1]


def _adam_math(w, g, m, v):
    m = ADAM_B1 * m + (1.0 - ADAM_B1) * g
    v = ADAM_B2 * v + (1.0 - ADAM_B2) * (g * g)
    m_hat = m / (1.0 - ADAM_B1 ** ADAM_STEP)
    v_hat = v / (1.0 - ADAM_B2 ** ADAM_STEP)
    return -ADAM_LR * (m_hat / (jnp.sqrt(v_hat) + ADAM_EPS) + ADAM_WD * w), m, v


def _adamw_small(own, slots, w, m, v, me):
    rows, cols = own.shape

    def body(me_ref, own_ref, slots_ref, w_ref, m_ref, v_ref, g_ref, d_ref, nm_ref, nv_ref):
        own_v = own_ref[...].astype(F32)
        g = jnp.where(me_ref[0] == 0, own_v, slots_ref[0].astype(F32))
        for i in range(1, 8):
            g = g + jnp.where(me_ref[0] == i, own_v, slots_ref[i].astype(F32))
        g_ref[...] = g
        d_ref[...], nm_ref[...], nv_ref[...] = _adam_math(w_ref[...], g, m_ref[...], v_ref[...])

    flat = pl.BlockSpec((rows, cols), lambda i, me_ref: (0, 0))
    return pl.pallas_call(
        body, name="adamw_small",
        grid_spec=pltpu.PrefetchScalarGridSpec(
            num_scalar_prefetch=1, grid=(1,),
            in_specs=[flat, pl.BlockSpec((8, rows, cols), lambda i, me_ref: (0, 0, 0)), flat, flat, flat],
            out_specs=[flat] * 4),
        out_shape=[_sds((rows, cols), F32)] * 4,
        compiler_params=_seq_params(),
    )(me, own, slots, w, m, v)


def kernel(x, positions, pre_mix_norm, w_in, sgu_ln_gain, sgu_ln_bias, sgu_w_spatial, sgu_b_spatial, attn_out_norm, sgu_out_norm, w_out, post_mix_norm, pre_ffn_norm, w_gate, w_up, w_down, post_ffn_norm, loss_target, m_pre_mix_norm, m_w_in, m_sgu_ln_gain, m_sgu_ln_bias, m_sgu_w_spatial, m_sgu_b_spatial, m_attn_out_norm, m_sgu_out_norm, m_w_out, m_post_mix_norm, m_pre_ffn_norm, m_w_gate, m_w_up, m_w_down, m_post_ffn_norm, v_pre_mix_norm, v_w_in, v_sgu_ln_gain, v_sgu_ln_bias, v_sgu_w_spatial, v_sgu_b_spatial, v_attn_out_norm, v_sgu_out_norm, v_w_out, v_post_mix_norm, v_pre_ffn_norm, v_w_gate, v_w_up, v_w_down, v_post_ffn_norm):
    a = dict(locals())
    cx, cy, cc = _coords()
    s_me = 2 * cx + cy
    shard_core = jnp.stack([s_me, cc]).astype(jnp.int32)
    me = jnp.stack([4 * cx + 2 * cy + cc]).astype(jnp.int32)
    small = {n: (a[n][0] if a[n].ndim > 2 else a[n]) for n in SMALL}
    b_t = small["sgu_b_spatial"].T
    xs, pos, target = x[0], positions.reshape(SEQ, 1), loss_target[0]
    flipped = ("w_gate", "w_up")

    def big(name, n):
        return jnp.swapaxes(a[name], 1, 2)[0] if n in flipped else a[name][0]

    own = {"w_in": _halves(big("w_in", "w_in").astype(BF16))}

    def with_own(full, n):
        full = lax.dynamic_update_slice(full, own[n][None], (s_me, 0, 0, 0))
        return full.reshape((N_SHARD,) + big(n, n).shape)

    ffn = ("w_gate", "w_up", "w_down")
    g_in = _gather_ici([own["w_in"]])
    (s_in,), token = _split_starts("gather_in_start", [g_in], small["pre_mix_norm"])
    for n in ("w_out",) + ffn:
        own[n] = _halves((big(n, n) + token[0:1, 0:1]).astype(BF16))
    g_out, g_ffn = _gather_ici([own["w_out"]]), _gather_ici([own[n] for n in ffn])
    (s_out, s_ffn), token = _split_starts("gather_rest_start", [g_out, g_ffn], token)
    in_lands = _split_wait("gather_in_wait", g_in, *s_in, token)
    ((in_lands,),) = _comm_only("comm_pass_in", [_gather_pass(in_lands)])
    w_in_f = with_own(in_lands, "w_in")
    h, u, vs, sgu, *qkv = _inproj_fwd(xs, pos, small["pre_mix_norm"], w_in_f, small["sgu_ln_gain"],
                                      small["sgu_ln_bias"], small["sgu_w_spatial"], b_t)
    views = [tuple(qkv[3 * i:3 * i + 3]) for i in range(len(DILATIONS))]
    out_lands = _split_wait("gather_out_wait", g_out, *s_out, sgu)
    o_list, l_list = [], []
    for dil, (qv, kv, vv) in zip(DILATIONS, views):
        if dil == 1:
            (o, l), ((out_lands,),) = _attn_fwd(qv, kv, vv, dil, comms=[_gather_pass(out_lands)])
        else:
            o, l = _attn_fwd(qv, kv, vv, dil)
        o_list.append(o)
        l_list.append(l)
    w_out_f = with_own(out_lands, "w_out")
    ffn_lands = _split_wait("gather_ffn_wait", g_ffn, *s_ffn, l_list[-1])
    (attn, mixed, y, x1, *lses), (ffn_lands,) = _mix_out_fwd(
        o_list, l_list, sgu, xs, w_out_f, small["attn_out_norm"], small["sgu_out_norm"], small["post_mix_norm"],
        comms=[_gather_pass(ffn_lands)])
    w_gate_f, w_up_f, w_down_f = (with_own(f, n).reshape(FF, D_MODEL) for f, n in zip(ffn_lands, ffn))
    h2, act, dg, dup, df, dx1, loss_cols, d_pre_ffn, d_post_ffn = _ffn_fwd_bwd(
        x1, target, w_gate_f, w_up_f, w_down_f, small["pre_ffn_norm"], small["post_ffn_norm"])

    full_tok = pl.BlockSpec((SEQ, D_MODEL), lambda s: (0, 0), pipeline_mode=pl.Buffered(1))
    gw, gb = {}, {}
    gw["w_gate"], gw["w_up"], gb["w_gate"], gb["w_up"] = _wgrad_ff([dg, dup], h2, "wgrad_gate_up")
    gw["w_down"], gb["w_down"] = _wgrad_ff([act], df, "wgrad_down")
    (dy, du, dvs_sgu, d_post_mix, d_attn_norm, d_sgu_norm, d_w_sp, d_b_sp, d_ln_gain, d_ln_bias,
     *dviews), (sib_ffn,) = _outproj_bwd(
        dx1, y, attn, sgu, w_out_f, small["post_mix_norm"], small["attn_out_norm"], small["sgu_out_norm"],
        u, vs, small["sgu_ln_gain"], small["sgu_ln_bias"], small["sgu_w_spatial"], b_t,
        comms=[_rs_sibling([gb[n] for n in ffn])])
    sib = dict(zip(ffn, sib_ffn))
    part = dict(zip(ffn, _chip_sums([gw[n] for n in ffn], [sib[n] for n in ffn], shard_core)))
    gw["w_out"], gb["w_out"] = _wgrad(mixed, dy, pl.BlockSpec((SEQ, OUT_S), lambda s: (0, s)), full_tok,
                                      (OUT_S, D_MODEL), "wgrad_out")
    x_ffn = _rs_chips([part[n] for n in ffn])
    packed_early = _pack({
        "sgu_ln_gain": d_ln_gain, "sgu_ln_bias": d_ln_bias, "sgu_b_spatial": d_b_sp,
        "attn_out_norm": d_attn_norm, "sgu_out_norm": d_sgu_norm, "post_mix_norm": d_post_mix,
        "pre_ffn_norm": d_pre_ffn, "post_ffn_norm": d_post_ffn, LOSS_ROW: loss_cols}, SMALL_EARLY, 8)
    wsp_view = (SGU_GROUPS * CHUNK, CHUNK)
    packed_wsp = d_w_sp.reshape(wsp_view).astype(BF16)
    x_small, x_wsp = _small_exchange(packed_early), _small_exchange(packed_wsp)
    (s_ffn,), token = _split_starts("rs_ffn_start", [x_ffn], small["pre_mix_norm"])

    dqs, dks, dvs = [], [], []
    for i, (dil, (qv, kv, vv)) in enumerate(zip(DILATIONS, views)):
        if dil == 1:
            (dq, dk, dv), ((sib["w_out"],),) = _attn_bwd(qv, kv, vv, dviews[i], dviews[3 + i], lses[i], dil,
                                                        comms=[_rs_sibling([gb["w_out"]])], after=[token])
            (part["w_out"],) = _chip_sums([gw["w_out"]], [sib["w_out"]], shard_core)
            x_out = _rs_chips([part["w_out"]])
            (s_out, s_small, s_wsp), token = _split_starts("rs_out_small_start", [x_out, x_small, x_wsp], token)
        else:
            dq, dk, dv = _attn_bwd(qv, kv, vv, dviews[i], dviews[3 + i], lses[i], dil, after=[token])
        dqs.append(dq)
        dks.append(dk)
        dvs.append(dv)
    half, far, joined = {}, {}, {}
    far.update(zip(ffn, _split_wait("rs_ffn_wait", x_ffn, *s_ffn, dvs[-1])))
    dproj, grad_x, d_pre_mix = _inproj_bwd(dqs, dks, dvs, du, dvs_sgu, pos, xs, dx1, w_in_f, small["pre_mix_norm"])
    (far["w_out"],) = _split_wait("rs_out_wait", x_out, *s_out, grad_x)
    names = ffn + ("w_out",)
    half.update(zip(names, _final_sums([gw[n] for n in names], [sib[n] for n in names], [far[n] for n in names],
                                       shard_core)))
    packed_late = _pack({"pre_mix_norm": d_pre_mix}, SMALL_LATE, 8)
    (gw["w_in"], gb["w_in"]), (got, (slots_late,)) = _wgrad(
        h, dproj, full_tok, pl.BlockSpec((SEQ, IN_S), lambda s: (0, s)), (D_MODEL, IN_S), "wgrad_in",
        comms=[_rs_join([half[n] for n in names]), _small_exchange(packed_late)])
    joined.update(zip(names, got))
    ((sib["w_in"],),) = _comm_only("comm_rs_sibling_in", [_rs_sibling([gb["w_in"]])])
    (part["w_in"],) = _chip_sums([gw["w_in"]], [sib["w_in"]], shard_core)
    x_in = _rs_chips([part["w_in"]])
    (s_in,), token = _split_starts("rs_in_start", [x_in], small["pre_mix_norm"])

    grads, deltas, new_m, new_v = {}, {}, {}, {}

    def record(n, outs):
        grads[n], deltas[n], new_m[n], new_v[n] = (
            jnp.swapaxes(o[None], 1, 2) if n in flipped else o[None] for o in outs)

    def update(names, name, after):
        for n, outs in zip(names, _adamw_multi(
                [big(n, n) for n in names], [joined[n].reshape(big(n, n).shape) for n in names],
                [big("m_" + n, n) for n in names], [big("v_" + n, n) for n in names], name, after)):
            record(n, outs)

    update(names, "adamw_ffn_out", [token])
    (slots_early,) = _split_wait("small_early_wait", x_small, *s_small, new_v["w_out"])
    (slots_wsp,) = _split_wait("small_wsp_wait", x_wsp, *s_wsp, slots_early)
    (far["w_in"],) = _split_wait("rs_in_wait", x_in, *s_in, slots_wsp)
    (half["w_in"],) = _final_sums([gw["w_in"]], [sib["w_in"]], [far["w_in"]], shard_core)
    ((joined["w_in"],),) = _comm_only("comm_rs_join_in", [_rs_join([half["w_in"]])])
    update(("w_in",), "adamw_w_in", [])
    a[LOSS_ROW] = a["m_" + LOSS_ROW] = a["v_" + LOSS_ROW] = jnp.zeros((1, D_MODEL), F32)
    outs = _adamw_small(packed_wsp, slots_wsp, *[a[p + SMALL_WSP].reshape(wsp_view) for p in ("", "m_", "v_")], me)
    for dst, buf in zip((grads, deltas, new_m, new_v), outs):
        dst[SMALL_WSP] = buf.reshape(a[SMALL_WSP].shape)
    for names, rows, packed, slots in ((SMALL_EARLY, 8, packed_early, slots_early),
                                       (SMALL_LATE, 8, packed_late, slots_late)):
        outs = _adamw_small(packed, slots, _pack(a, names, rows), _pack({n: a["m_" + n] for n in names}, names, rows),
                            _pack({n: a["v_" + n] for n in names}, names, rows), me)
        for dst, buf in zip((grads, deltas, new_m, new_v), outs):
            dst.update(_unpack(buf, names, {n: a[n].shape for n in names}))
    loss = jnp.sum(grads[LOSS_ROW]) * np.float32(0.5 / D_MODEL)
    return (loss, grad_x[None], *[grads[n] for n in WEIGHTS], *[deltas[n] for n in WEIGHTS],
            *[new_m[n] for n in WEIGHTS], *[new_v[n] for n in WEIGHTS])
```

```python
import numpy as np
import jax
import jax.numpy as jnp
from jax import lax
from jax.experimental import pallas as pl
from jax.experimental.pallas import tpu as pltpu

F32 = jnp.float32
BF16 = jnp.bfloat16

SEQ = 2048
D_MODEL = 1024
HEAD_DIM = 64
ATTN_W = 512
SGU_W = 512
SGU_GROUPS = 8
CHUNK = 128
DILATIONS = (1, 4, 16)
N_SHARD = 4
IN_S = 640
OUT_S = 256
FF_S = 704
PROJ_W = N_SHARD * IN_S
FF = N_SHARD * FF_S
FF_CHUNKS = ((0, 1024), (1024, 2048), (2048, FF))
RMS_EPS = 1e-6
LN_EPS = 1e-5
ROPE_THETA = 500000.0
ATTN_SCALE = 1.0 / np.sqrt(HEAD_DIM)
NEG = -1e30
TM = 512
TM_FFN = 256
VMEM_LIMIT = 56 * 1024 * 1024

ADAM_LR = 0.001
ADAM_B1 = 0.9
ADAM_B2 = 0.999
ADAM_EPS = 1e-08
ADAM_WD = 0.01
ADAM_STEP = 10

MESH = pl.DeviceIdType.MESH
ANY = pl.BlockSpec(memory_space=pl.ANY)


def _dot(a, b):
    return jnp.dot(a, b, preferred_element_type=F32)


def _dot_nt(a, b):
    return lax.dot_general(a, b, (((1,), (1,)), ((), ())), preferred_element_type=F32)


def _dot_tn(a, b):
    return lax.dot_general(a, b, (((0,), (0,)), ((), ())), preferred_element_type=F32)


def _dot_exact(a, b):
    return jnp.dot(a, b, preferred_element_type=F32, precision=lax.Precision.HIGHEST)


def _dot_select(a, sel):
    hi = a.astype(BF16)
    lo = (a - hi.astype(F32)).astype(BF16)
    sel = sel.astype(BF16)
    return _dot(hi, sel) + _dot(lo, sel)


def _rms_stats(x):
    r = lax.rsqrt(jnp.mean(x * x, axis=-1, keepdims=True) + RMS_EPS)
    return x * r, r


def _rms_bwd(xh, r, gain, dy):
    dxh = dy * gain
    dx = r * (dxh - xh * jnp.mean(dxh * xh, axis=-1, keepdims=True))
    return dx, jnp.sum(dy * xh, axis=0, keepdims=True)


_ERF_ALPHA = (-2.72614225801306e-10, 2.77068142495902e-08, -2.10102402082508e-06, -5.69250639462346e-05,
              -7.34990630326855e-04, -2.95459980854025e-03, -1.60960333262415e-02)
_ERF_BETA = (-1.45660718464996e-05, -2.13374055278905e-04, -1.68282697438203e-03, -7.37332916720468e-03,
             -1.42647390514189e-02)


def _erf(x):
    x = jnp.clip(x, -4.0, 4.0)
    x2 = x * x
    p = jnp.full_like(x, _ERF_ALPHA[0])
    for a in _ERF_ALPHA[1:]:
        p = p * x2 + a
    q = jnp.full_like(x, _ERF_BETA[0])
    for b in _ERF_BETA[1:]:
        q = q * x2 + b
    return x * p / q


def _normal_cdf(x):
    return 0.5 * (1.0 + _erf(x * np.float32(1.0 / np.sqrt(2.0))))


def _gelu_grad(x, cdf):
    pdf = jnp.exp(-0.5 * x * x) * np.float32(1.0 / np.sqrt(2.0 * np.pi))
    return cdf + x * pdf


def _sigmoid(x):
    return 1.0 / (1.0 + jnp.exp(-x))


_INV_FREQ = tuple(float(np.float32(ROPE_THETA ** (-2.0 * j / 16.0))) for j in range(8))


def _rot_tables(pos):
    lane = lax.broadcasted_iota(jnp.int32, (1, 128), 1)
    d = lane & 63
    j = d & 7
    inv = jnp.zeros((1, 128), F32)
    for jj in range(8):
        inv = jnp.where(j == jj, _INV_FREQ[jj], inv)
    ang = pos.astype(F32) * inv
    c = jnp.cos(ang)
    s = jnp.sin(ang)
    cos_t = jnp.where(d < 16, c, 1.0)
    sin_a = jnp.where(d < 8, -s, 0.0)
    sin_b = jnp.where((d >= 8) & (d < 16), s, 0.0)
    return tuple(jnp.tile(t, (1, 4)) for t in (cos_t, sin_a, sin_b))


def _rope(x, tabs):
    cos_t, sin_a, sin_b = tabs
    return x * cos_t + pltpu.roll(x, 504, 1) * sin_a + pltpu.roll(x, 8, 1) * sin_b


def _rope_bwd(dy, tabs):
    cos_t, sin_a, sin_b = tabs
    return dy * cos_t + pltpu.roll(dy * sin_a, 8, 1) + pltpu.roll(dy * sin_b, 504, 1)


def _left_half():
    return lax.broadcasted_iota(jnp.int32, (CHUNK, CHUNK), 1) < HEAD_DIM


def _group_ones():
    lane = lax.broadcasted_iota(jnp.int32, (SGU_GROUPS, SGU_W), 1)
    row = lax.broadcasted_iota(jnp.int32, (SGU_GROUPS, SGU_W), 0)
    return ((lane >> 6) == row).astype(F32)


def _masked_spatial(w_ref):
    row = lax.broadcasted_iota(jnp.int32, (CHUNK, CHUNK), 0)
    col = lax.broadcasted_iota(jnp.int32, (CHUNK, CHUNK), 1)
    return [jnp.where(col <= row, w_ref[g], 0.0).astype(BF16) for g in range(SGU_GROUPS)]


def _sgu_core(u, vs, lg, lb, wm, bias_full):
    tm = u.shape[0]
    cdf_u, cdf_vs = _normal_cdf(u), _normal_cdf(vs)
    gu = u * cdf_u
    gv = vs * cdf_vs
    mu = jnp.mean(gv, axis=-1, keepdims=True)
    xc = gv - mu
    rstd = lax.rsqrt(jnp.mean(xc * xc, axis=-1, keepdims=True) + LN_EPS)
    xh = xc * rstd
    vnb = (xh * lg + lb).astype(BF16)
    left = _left_half()
    rows = []
    for c in range(tm // CHUNK):
        pieces = []
        for p in range(4):
            vp = vnb[c * CHUNK:(c + 1) * CHUNK, p * 128:(p + 1) * 128]
            pieces.append(jnp.where(left, _dot(wm[2 * p], vp), _dot(wm[2 * p + 1], vp)))
        rows.append(jnp.concatenate(pieces, axis=1) + bias_full)
    mixed = jnp.concatenate(rows, axis=0)
    return gu, xh, rstd, vnb, mixed, cdf_u, cdf_vs


def _resident(shape):
    n = len(shape)
    return pl.BlockSpec(shape, lambda *_: (0,) * n, pipeline_mode=pl.Buffered(1))


def _rows(ncol, tm=TM):
    return pl.BlockSpec((tm, ncol), lambda i: (i, 0))


def _acc(ncol, nrow=1):
    return pl.BlockSpec((nrow, ncol), lambda i: (0, 0))


HEAD_W = 128


def _view_rows(dil, width=ATTN_W, tm=TM):
    return pl.BlockSpec((tm // dil, dil * width), lambda i: (i, 0))


def _view_shape(dil, dtype, width=ATTN_W):
    return _sds((SEQ // dil, dil * width), dtype)


def _slab_scratch():
    return pltpu.VMEM((4, TM, 128), F32)


def _store_view(val, out_ref, slabs, dil):
    width = val.shape[1]
    for j in range(width // 128):
        slabs[j] = val[:, j * 128:(j + 1) * 128]
    for r in range(dil):
        for j in range(width // 128):
            c0 = r * width + j * 128
            out_ref[:, c0:c0 + 128] = slabs.at[j][pl.ds(r, TM // dil, stride=dil), :].astype(out_ref.dtype)


def _load_view(in_ref, slabs, dil, width=ATTN_W):
    for r in range(dil):
        for j in range(width // 128):
            c0 = r * width + j * 128
            slabs.at[j][pl.ds(r, TM // dil, stride=dil), :] = in_ref[:, c0:c0 + 128].astype(F32)
    return jnp.concatenate([slabs[j] for j in range(width // 128)], axis=1)


def _head_spread():
    m = lax.broadcasted_iota(jnp.int32, (HEAD_W, ATTN_W), 0)
    lane = lax.broadcasted_iota(jnp.int32, (HEAD_W, ATTN_W), 1)
    return (m == 16 * (lane >> 6)).astype(F32)


def _head_sum():
    lane = lax.broadcasted_iota(jnp.int32, (ATTN_W, HEAD_W), 0)
    m = lax.broadcasted_iota(jnp.int32, (ATTN_W, HEAD_W), 1)
    return ((lane >> 6) == (m >> 4)).astype(F32)


def _seq_params():
    return pltpu.CompilerParams(dimension_semantics=("arbitrary",), vmem_limit_bytes=VMEM_LIMIT)


def _sds(shape, dtype):
    return jax.ShapeDtypeStruct(shape, dtype)


class _Comm:
    def __init__(self, args, out_shape, n_sems, start, finish, aliased=False):
        self.args, self.out_shape, self.n_sems = list(args), list(out_shape), n_sems
        self.start, self.finish, self.aliased = start, finish, aliased


def _pcall(body, *, name, grid, in_specs, out_specs, out_shape, args, scratch_shapes=(), comms=(), after=()):
    single = not isinstance(out_shape, (list, tuple))
    out_specs = [out_specs] if single else list(out_specs)
    out_shape = [out_shape] if single else list(out_shape)
    n_in, n_out, n_scr = len(in_specs), len(out_shape), len(scratch_shapes)
    c_args = [a for c in comms for a in c.args]
    c_outs = [o for c in comms for o in c.out_shape]
    aliases, ai, ao = {}, n_in, n_out
    for c in comms:
        if c.aliased:
            aliases.update({ai + k: ao + k for k in range(len(c.args))})
        ai += len(c.args)
        ao += len(c.out_shape)
    sems = [pltpu.SemaphoreType.DMA((c.n_sems,)) for c in comms for _ in range(2)]
    steps = grid[0]

    def wrapped(*refs):
        o0 = n_in + len(c_args) + len(after)
        s0 = o0 + n_out + len(c_outs)
        m_in, m_out, m_sem = refs[n_in:n_in + len(c_args)], refs[o0 + n_out:s0], refs[s0 + n_scr:]

        def each(phase):
            ii = oi = 0
            for k, c in enumerate(comms):
                getattr(c, phase)(m_in[ii:ii + len(c.args)], m_out[oi:oi + len(c.out_shape)],
                                  m_sem[2 * k], m_sem[2 * k + 1])
                ii += len(c.args)
                oi += len(c.out_shape)

        if comms:
            @pl.when(pl.program_id(0) == 0)
            def _():
                each("start")

        body(*refs[:n_in], *refs[o0:o0 + n_out], *refs[s0:s0 + n_scr])

        if comms:
            @pl.when(pl.program_id(0) == steps - 1)
            def _():
                each("finish")

    res = pl.pallas_call(
        wrapped, name=name, grid=grid,
        in_specs=list(in_specs) + [ANY] * (len(c_args) + len(after)), out_specs=out_specs + [ANY] * len(c_outs),
        out_shape=out_shape + c_outs, scratch_shapes=list(scratch_shapes) + sems,
        input_output_aliases=aliases, compiler_params=_seq_params(),
    )(*args, *c_args, *after)
    mine = res[0] if single else list(res[:n_out])
    if not comms:
        return mine
    theirs, oi = [], n_out
    for c in comms:
        theirs.append(list(res[oi:oi + len(c.out_shape)]))
        oi += len(c.out_shape)
    return mine, theirs


def _in_pieces(g):
    lo, hi = 512 * g, 512 * (g + 1)
    return [(s, max(lo, IN_S * s) - IN_S * s, min(hi, IN_S * (s + 1)) - IN_S * s)
            for s in range(N_SHARD) if max(lo, IN_S * s) < min(hi, IN_S * (s + 1))]


def _inproj_fwd(x, pos, g_pre, w_in, lg, lb, w_sp, b_t, comms=()):
    def body(x_ref, pos_ref, g_ref, w_ref, lg_ref, lb_ref, wsp_ref, bt_ref, h_ref, u_ref, vs_ref, sgu_ref, *rest):
        qkv_refs, slabs = rest[:9], rest[9:]
        xh, _ = _rms_stats(x_ref[...])
        h = (xh * g_ref[...]).astype(BF16)
        h_ref[...] = h
        tabs = _rot_tables(pos_ref[...])

        def group(g):
            return jnp.concatenate([_dot(h, w_ref[s, :, a:b]) for s, a, b in _in_pieces(g)], axis=1)

        for t in range(3):
            val = group(t)
            if t < 2:
                val = _rope(val, tabs)
            if t == 0:
                val = val * np.float32(ATTN_SCALE)
            qkv_refs[t][...] = val.astype(BF16)
            for i, dil in enumerate(DILATIONS[1:]):
                _store_view(val, qkv_refs[3 * (i + 1) + t], slabs[t], dil)
        u = group(3)
        vs = group(4)
        u_ref[...] = u
        vs_ref[...] = vs
        bias_full = _dot_exact(bt_ref[...], _group_ones())
        gu, _, _, _, mixed, _, _ = _sgu_core(u, vs, lg_ref[...], lb_ref[...], _masked_spatial(wsp_ref), bias_full)
        sgu_ref[...] = gu * mixed

    return _pcall(
        body, name="inproj_sgu_fwd", grid=(SEQ // TM,),
        in_specs=[_rows(D_MODEL), _rows(1), _resident((1, D_MODEL)), _resident((N_SHARD, D_MODEL, IN_S)),
                  _resident((1, SGU_W)), _resident((1, SGU_W)), _resident((SGU_GROUPS, CHUNK, CHUNK)),
                  _resident((CHUNK, SGU_GROUPS))],
        out_specs=[_rows(D_MODEL), _rows(512), _rows(512), _rows(512)]
        + [_view_rows(dil) for dil in DILATIONS for _ in range(3)],
        out_shape=[_sds((SEQ, D_MODEL), BF16), _sds((SEQ, 512), F32), _sds((SEQ, 512), F32), _sds((SEQ, 512), F32)]
        + [_view_shape(dil, BF16) for dil in DILATIONS for _ in range(3)],
        scratch_shapes=[_slab_scratch() for _ in range(3)],
        args=(x, pos, g_pre, w_in, lg, lb, w_sp, b_t), comms=comms)


def _block_masks():
    row = lax.broadcasted_iota(jnp.int32, (CHUNK, CHUNK), 0)
    col = lax.broadcasted_iota(jnp.int32, (CHUNK, CHUNK), 1)
    return col <= row, col >= row


def _attn_fwd(qv, kv, vv, dil, comms=()):
    seg = SEQ // dil
    nblk = seg // CHUNK
    rps = 4 if nblk == 1 else 1

    def body(q_ref, k_ref, v_ref, o_ref, l_ref):
        left = _left_half()
        m_cur, m_prev = _block_masks()
        zero = jnp.zeros((CHUNK, CHUNK), BF16)
        ones = (jnp.where(left, 1.0, 0.0).astype(BF16), jnp.where(left, 0.0, 1.0).astype(BF16))

        sides = tuple(enumerate((left, ~left)))

        def rows(b):
            if isinstance(b, int):
                return b * CHUNK, max(b - 1, 0) * CHUNK
            return pl.multiple_of(b * CHUNK, CHUNK), pl.multiple_of(jnp.maximum(b - 1, 0) * CHUNK, CHUNK)

        def first(rr, b):
            r0, rp = rows(b)
            prev_ok = m_prev & (b > 0)
            tiles, scores = [], []
            for hp in range(4):
                ls = slice(rr * ATTN_W + hp * 128, rr * ATTN_W + (hp + 1) * 128)
                qp = q_ref[pl.ds(r0, CHUNK), ls]
                kc = k_ref[pl.ds(r0, CHUNK), ls]
                kp = k_ref[pl.ds(rp, CHUNK), ls] if nblk > 1 else None
                tiles.append((ls, v_ref[pl.ds(r0, CHUNK), ls], v_ref[pl.ds(rp, CHUNK), ls] if nblk > 1 else None))
                for _, hm in sides:
                    qh = jnp.where(hm, qp, zero)
                    sc = jnp.where(m_cur, _dot_nt(qh, kc), NEG)
                    sp = jnp.where(prev_ok, _dot_nt(qh, kp), NEG) if nblk > 1 else None
                    scores.append((sc, sp))
            return tiles, scores

        def second(scores):
            probs = []
            for sc, sp in scores:
                if nblk > 1:
                    m = jnp.max(jnp.maximum(sc, sp), axis=-1, keepdims=True)
                    pc = jnp.exp(sc - m)
                    pp = jnp.exp(sp - m)
                    probs.append((m, pc.astype(BF16), pp.astype(BF16), (pc + pp).astype(BF16)))
                else:
                    m = jnp.max(sc, axis=-1, keepdims=True)
                    pc = jnp.exp(sc - m).astype(BF16)
                    probs.append((m, pc, None, pc))
            return probs

        def third(rr, b, tiles, probs):
            r0, _ = rows(b)
            for hp, (ls, vc, vp) in enumerate(tiles):
                acc = jnp.zeros((CHUNK, CHUNK), F32)
                den = jnp.zeros((CHUNK, CHUNK), F32)
                for side, hm in sides:
                    _, pc, pp, psum = probs[2 * hp + side]
                    acc = acc + _dot(pc, jnp.where(hm, vc, zero))
                    if nblk > 1:
                        acc = acc + _dot(pp, jnp.where(hm, vp, zero))
                    den = den + _dot(psum, ones[side])
                o_ref[pl.ds(r0, CHUNK), ls] = (acc / den).astype(o_ref.dtype)
                lse = jnp.where(left, probs[2 * hp][0], probs[2 * hp + 1][0]) + jnp.log(den)
                l_ref[pl.ds(r0, CHUNK), rr * HEAD_W + 32 * hp:rr * HEAD_W + 32 * hp + 32] = lse[:, 48:80]

        def run(units):
            data = [first(rr, b) for rr, b in units]
            probs = [second(scores) for _, scores in data]
            for (rr, b), (tiles, _), pr in zip(units, data, probs):
                third(rr, b, tiles, pr)

        if nblk == 1:
            run([(rr, 0) for rr in range(rps)])
        else:
            def one(b, carry):
                run([(0, b)])
                return carry

            lax.fori_loop(0, nblk, one, 0)

    spec = pl.BlockSpec((seg, rps * ATTN_W), lambda r: (0, r))
    return _pcall(
        body, name=f"attn_fwd_d{dil}", grid=(dil // rps,),
        in_specs=[spec, spec, spec], out_specs=[spec, pl.BlockSpec((seg, rps * HEAD_W), lambda r: (0, r))],
        out_shape=[_sds((seg, dil * ATTN_W), BF16), _sds((seg, dil * HEAD_W), F32)],
        args=(qv, kv, vv), comms=comms)


def _lane_left(nrows):
    return lax.broadcasted_iota(jnp.int32, (nrows, CHUNK), 1) < HEAD_DIM


def _attn_rows(b):
    if isinstance(b, int):
        return b * CHUNK, max(b - 1, 0) * CHUNK
    return pl.multiple_of(b * CHUNK, CHUNK), pl.multiple_of(jnp.maximum(b - 1, 0) * CHUNK, CHUNK)


def _mix_out_fwd(o_list, l_list, sgu, x, w_out, g_attn, g_sgu, g_post, comms=()):
    def body(o1, o2, o3, l1, l2, l3, sgu_ref, x_ref, w_ref, ga_ref, gs_ref, gp_ref,
             attn_ref, mixed_ref, y_ref, x1_ref, lse1_ref, lse2_ref, lse3_ref, slabs_a, slabs_b):
        os = [o1[...], _load_view(o2, slabs_a, DILATIONS[1]), _load_view(o3, slabs_b, DILATIONS[2])]
        ls = [l1[...], _load_view(l2, slabs_a, DILATIONS[1], HEAD_W), _load_view(l3, slabs_b, DILATIONS[2], HEAD_W)]
        m = jnp.maximum(jnp.maximum(ls[0], ls[1]), ls[2])
        es = [jnp.exp(l - m) for l in ls]
        den = es[0] + es[1] + es[2]
        spread = _head_spread()
        attn = sum(_dot_select(e / den, spread) * o for e, o in zip(es, os))
        attn_ref[...] = attn
        lse = m + jnp.log(den)
        lse1_ref[...] = lse
        _store_view(lse, lse2_ref, slabs_a, DILATIONS[1])
        _store_view(lse, lse3_ref, slabs_b, DILATIONS[2])
        ah, _ = _rms_stats(attn)
        sh, _ = _rms_stats(sgu_ref[...])
        mixed = jnp.concatenate([ah * ga_ref[...], sh * gs_ref[...]], axis=1).astype(BF16)
        mixed_ref[...] = mixed
        y = _dot(mixed[:, 0:OUT_S], w_ref[0])
        for s in range(1, N_SHARD):
            y = y + _dot(mixed[:, s * OUT_S:(s + 1) * OUT_S], w_ref[s])
        y_ref[...] = y
        yh, _ = _rms_stats(y)
        x1_ref[...] = x_ref[...] + yh * gp_ref[...]

    return _pcall(
        body, name="mix_out_fwd", grid=(SEQ // TM,),
        in_specs=[_view_rows(dil) for dil in DILATIONS] + [_view_rows(dil, HEAD_W) for dil in DILATIONS]
        + [_rows(512), _rows(D_MODEL), _resident((N_SHARD, OUT_S, D_MODEL)),
           _resident((1, 512)), _resident((1, 512)), _resident((1, D_MODEL))],
        out_specs=[_rows(512), _rows(D_MODEL), _rows(D_MODEL), _rows(D_MODEL)]
        + [_view_rows(dil, HEAD_W) for dil in DILATIONS],
        out_shape=[_sds((SEQ, 512), F32), _sds((SEQ, D_MODEL), BF16), _sds((SEQ, D_MODEL), F32),
                   _sds((SEQ, D_MODEL), F32)] + [_view_shape(dil, F32, HEAD_W) for dil in DILATIONS],
        scratch_shapes=[_slab_scratch(), _slab_scratch()],
        args=(*o_list, *l_list, sgu, x, w_out, g_attn, g_sgu, g_post), comms=comms)


def _ffn_fwd_bwd(x1, target, w_gate, w_up, w_down, g_pre, g_post, comms=()):
    def body(x1_ref, t_ref, wg_ref, wu_ref, wd_ref, gpf_ref, gpo_ref,
             h2_ref, a_ref, dg_ref, dup_ref, df_ref, dx1_ref, loss_ref, dgpf_ref, dgpo_ref, g_scr, up_scr):
        @pl.when(pl.program_id(0) == 0)
        def _():
            loss_ref[...] = jnp.zeros_like(loss_ref)
            dgpf_ref[...] = jnp.zeros_like(dgpf_ref)
            dgpo_ref[...] = jnp.zeros_like(dgpo_ref)

        x1 = x1_ref[...]
        gpf = gpf_ref[...]
        gpo = gpo_ref[...]
        xh, r = _rms_stats(x1)
        h2 = (xh * gpf).astype(BF16)
        h2_ref[...] = h2
        f = jnp.zeros((TM_FFN, D_MODEL), F32)
        for c0, c1 in FF_CHUNKS:
            g = _dot_nt(h2, wg_ref[c0:c1, :])
            up = _dot_nt(h2, wu_ref[c0:c1, :])
            g_scr[:, c0:c1] = g
            up_scr[:, c0:c1] = up
            a = (g * _sigmoid(g) * up).astype(BF16)
            a_ref[:, c0:c1] = a
            f = f + _dot(a, wd_ref[c0:c1, :])
        fh, rf = _rms_stats(f)
        diff = x1 + fh * gpo - t_ref[...]
        loss_ref[...] += jnp.sum(diff * diff, axis=0, keepdims=True)
        dout = diff * np.float32(1.0 / D_MODEL)
        df, dgpo = _rms_bwd(fh, rf, gpo, dout)
        dgpo_ref[...] += dgpo
        dfb = df.astype(BF16)
        df_ref[...] = dfb
        dh2 = jnp.zeros((TM_FFN, D_MODEL), F32)
        for c0, c1 in FF_CHUNKS:
            da = _dot_nt(dfb, wd_ref[c0:c1, :])
            g = g_scr[:, c0:c1]
            up = up_scr[:, c0:c1]
            sg = _sigmoid(g)
            dup = (da * (g * sg)).astype(BF16)
            dg = (da * up * (sg * (1.0 + g * (1.0 - sg)))).astype(BF16)
            dg_ref[:, c0:c1] = dg
            dup_ref[:, c0:c1] = dup
            dh2 = dh2 + _dot(dg, wg_ref[c0:c1, :]) + _dot(dup, wu_ref[c0:c1, :])
        dx, dgpf = _rms_bwd(xh, r, gpf, dh2)
        dgpf_ref[...] += dgpf
        dx1_ref[...] = dout + dx

    return _pcall(
        body, name="ffn_fwd_bwd", grid=(SEQ // TM_FFN,),
        in_specs=[_rows(D_MODEL, TM_FFN), _rows(D_MODEL, TM_FFN), _resident((FF, D_MODEL)),
                  _resident((FF, D_MODEL)), _resident((FF, D_MODEL)),
                  _resident((1, D_MODEL)), _resident((1, D_MODEL))],
        out_specs=[_rows(D_MODEL, TM_FFN), _rows(FF, TM_FFN), _rows(FF, TM_FFN), _rows(FF, TM_FFN),
                   _rows(D_MODEL, TM_FFN), _rows(D_MODEL, TM_FFN), _acc(D_MODEL), _acc(D_MODEL), _acc(D_MODEL)],
        out_shape=[_sds((SEQ, D_MODEL), BF16), _sds((SEQ, FF), BF16), _sds((SEQ, FF), BF16),
                   _sds((SEQ, FF), BF16), _sds((SEQ, D_MODEL), BF16), _sds((SEQ, D_MODEL), F32),
                   _sds((1, D_MODEL), F32), _sds((1, D_MODEL), F32), _sds((1, D_MODEL), F32)],
        scratch_shapes=[pltpu.VMEM((TM_FFN, FF), F32), pltpu.VMEM((TM_FFN, FF), F32)],
        args=(x1, target, w_gate, w_up, w_down, g_pre, g_post), comms=comms)


def _wgrad(a, b, a_spec, b_spec, out_block, name, comms=()):
    def body(a_ref, b_ref, o_ref, ob_ref):
        av = a_ref[0] if len(a_ref.shape) == 3 else a_ref[...]
        bv = b_ref[0] if len(b_ref.shape) == 3 else b_ref[...]
        g = _dot_tn(av, bv)
        o_ref[0] = g
        ob_ref[0] = g.astype(BF16)

    return _pcall(
        body, name=name, grid=(N_SHARD,),
        in_specs=[a_spec, b_spec],
        out_specs=[pl.BlockSpec((1,) + out_block, lambda s: (s, 0, 0))] * 2,
        out_shape=[_sds((N_SHARD,) + out_block, F32), _sds((N_SHARD,) + out_block, BF16)],
        args=(a, b), comms=comms)


def _outproj_bwd(dx1, y, attn, sgu, w_out, g_post, g_attn, g_sgu, u, vs, lg, lb, w_sp, b_t, comms=(), after=()):
    sgu_bwd = _sgu_bwd_body()

    def body(dx1_ref, y_ref, attn_ref, sgu_ref, w_ref, gp_ref, ga_ref, gs_ref, u_ref, vs_ref, lg_ref, lb_ref,
             wsp_ref, bt_ref, dy_ref, du_ref, dvs_ref, dgp_ref, dga_ref, dgs_ref, dw_ref, db_ref, dlg_ref, dlb_ref,
             *rest):
        dattn_refs, delta_refs, (slabs_a, slabs_b, dsgu_ref, dbias_scr) = rest[0:3], rest[3:6], rest[6:]

        @pl.when(pl.program_id(0) == 0)
        def _():
            dgp_ref[...] = jnp.zeros_like(dgp_ref)
            dga_ref[...] = jnp.zeros_like(dga_ref)
            dgs_ref[...] = jnp.zeros_like(dgs_ref)

        yh, ry = _rms_stats(y_ref[...])
        dy, dgp = _rms_bwd(yh, ry, gp_ref[...], dx1_ref[...])
        dgp_ref[...] += dgp
        dyb = dy.astype(BF16)
        dy_ref[...] = dyb
        dmixed = jnp.concatenate([_dot_nt(dyb, w_ref[s]) for s in range(N_SHARD)], axis=1)
        attn = attn_ref[...]
        ah, ra = _rms_stats(attn)
        dattn, dga = _rms_bwd(ah, ra, ga_ref[...], dmixed[:, 0:512])
        dga_ref[...] += dga
        sh, rs = _rms_stats(sgu_ref[...])
        dsgu, dgs = _rms_bwd(sh, rs, gs_ref[...], dmixed[:, 512:1024])
        dgs_ref[...] += dgs
        dsgu_ref[...] = dsgu
        delta = _dot_select(dattn * attn, _head_sum())
        dattn_refs[0][...] = dattn.astype(BF16)
        delta_refs[0][...] = delta
        for i, dil in enumerate(DILATIONS[1:]):
            _store_view(dattn, dattn_refs[i + 1], slabs_a, dil)
            _store_view(delta, delta_refs[i + 1], slabs_b, dil)
        sgu_bwd(u_ref, vs_ref, dsgu_ref, lg_ref, lb_ref, wsp_ref, bt_ref,
                du_ref, dvs_ref, dw_ref, db_ref, dlg_ref, dlb_ref, dbias_scr)

    return _pcall(
        body, name="outproj_sgu_bwd", grid=(SEQ // TM,),
        in_specs=[_rows(D_MODEL), _rows(D_MODEL), _rows(512), _rows(512), _resident((N_SHARD, OUT_S, D_MODEL)),
                  _resident((1, D_MODEL)), _resident((1, 512)), _resident((1, 512)),
                  _rows(SGU_W), _rows(SGU_W), _resident((1, SGU_W)), _resident((1, SGU_W)),
                  _resident((SGU_GROUPS, CHUNK, CHUNK)), _resident((CHUNK, SGU_GROUPS))],
        out_specs=[_rows(D_MODEL), _rows(SGU_W), _rows(SGU_W), _acc(D_MODEL), _acc(512), _acc(512),
                   pl.BlockSpec((SGU_GROUPS, CHUNK, CHUNK), lambda i: (0, 0, 0)), _acc(CHUNK, SGU_GROUPS),
                   _acc(SGU_W), _acc(SGU_W)]
        + [_view_rows(dil) for dil in DILATIONS] + [_view_rows(dil, HEAD_W) for dil in DILATIONS],
        out_shape=[_sds((SEQ, D_MODEL), BF16), _sds((SEQ, SGU_W), BF16), _sds((SEQ, SGU_W), BF16),
                   _sds((1, D_MODEL), F32), _sds((1, 512), F32), _sds((1, 512), F32),
                   _sds((SGU_GROUPS, CHUNK, CHUNK), F32), _sds((SGU_GROUPS, CHUNK), F32),
                   _sds((1, SGU_W), F32), _sds((1, SGU_W), F32)]
        + [_view_shape(dil, BF16) for dil in DILATIONS] + [_view_shape(dil, F32, HEAD_W) for dil in DILATIONS],
        scratch_shapes=[_slab_scratch(), _slab_scratch(), pltpu.VMEM((TM, SGU_W), F32),
                        pltpu.VMEM((CHUNK, SGU_W), F32)],
        args=(dx1, y, attn, sgu, w_out, g_post, g_attn, g_sgu, u, vs, lg, lb, w_sp, b_t), comms=comms, after=after)


def _sgu_bwd_body():
    nsteps = SEQ // TM

    def body(u_ref, vs_ref, ds_ref, lg_ref, lb_ref, w_ref, bt_ref,
             du_ref, dvs_ref, dw_ref, db_ref, dlg_ref, dlb_ref, dbias_scr):
        i = pl.program_id(0)

        @pl.when(i == 0)
        def _():
            dw_ref[...] = jnp.zeros_like(dw_ref)
            dlg_ref[...] = jnp.zeros_like(dlg_ref)
            dlb_ref[...] = jnp.zeros_like(dlb_ref)
            dbias_scr[...] = jnp.zeros_like(dbias_scr)

        wm = _masked_spatial(w_ref)
        ones_g = _group_ones()
        bias_full = _dot_exact(bt_ref[...], ones_g)
        u = u_ref[...]
        vs = vs_ref[...]
        lg = lg_ref[...]
        gu, xh, rstd, vnb, mixed, cdf_u, cdf_vs = _sgu_core(u, vs, lg, lb_ref[...], wm, bias_full)
        dsgu = ds_ref[...]
        du_ref[...] = (dsgu * mixed * _gelu_grad(u, cdf_u)).astype(BF16)
        dmixed = dsgu * gu
        left = _left_half()
        dvn_rows = []
        for c in range(TM // CHUNK):
            rs = slice(c * CHUNK, (c + 1) * CHUNK)
            dm_c = dmixed[rs, :]
            dbias_scr[...] += dm_c
            pieces = []
            for p in range(4):
                ls = slice(p * 128, (p + 1) * 128)
                dmp = dm_c[:, ls]
                vp = vnb[rs, ls]
                dmb = dmp.astype(BF16)
                zero = jnp.zeros_like(dmb)
                dw_ref[2 * p] += _dot_nt(jnp.where(left, dmb, zero), vp)
                dw_ref[2 * p + 1] += _dot_nt(jnp.where(left, zero, dmb), vp)
                pieces.append(jnp.where(left, _dot_tn(wm[2 * p], dmb), _dot_tn(wm[2 * p + 1], dmb)))
            dvn_rows.append(jnp.concatenate(pieces, axis=1))
        dvn = jnp.concatenate(dvn_rows, axis=0)
        dlg_ref[...] += jnp.sum(dvn * xh, axis=0, keepdims=True)
        dlb_ref[...] += jnp.sum(dvn, axis=0, keepdims=True)
        dxh = dvn * lg
        dgv = rstd * (dxh - jnp.mean(dxh, axis=-1, keepdims=True) - xh * jnp.mean(dxh * xh, axis=-1, keepdims=True))
        dvs_ref[...] = (dgv * _gelu_grad(vs, cdf_vs)).astype(BF16)

        @pl.when(i == nsteps - 1)
        def _():
            row = lax.broadcasted_iota(jnp.int32, (CHUNK, CHUNK), 0)
            col = lax.broadcasted_iota(jnp.int32, (CHUNK, CHUNK), 1)
            for g in range(SGU_GROUPS):
                dw_ref[g] = jnp.where(col <= row, dw_ref[g], 0.0)
            db_ref[...] = lax.dot_general(ones_g, dbias_scr[...], (((1,), (1,)), ((), ())),
                                          preferred_element_type=F32, precision=lax.Precision.HIGHEST)

    return body


def _attn_bwd(qv, kv, vv, dov, deltav, lsev, dil, comms=(), after=()):
    seg = SEQ // dil
    nblk = seg // CHUNK
    rps = 4 if nblk == 1 else 1

    def body(q_ref, k_ref, v_ref, do_ref, dl_ref, lse_ref, dq_ref, dk_ref, dv_ref, dk_wait, dv_wait):
        left = _left_half()
        m_cur, m_prev = _block_masks()
        sides = tuple(enumerate((left, ~left)))
        zero = jnp.zeros((CHUNK, CHUNK), BF16)

        def first(rr, b, both):
            r0, rp = _attn_rows(b)
            keys = pl.ds(rp, 2 * CHUNK) if both else pl.ds(r0, CHUNK)
            tiles, heads = [], []
            for hp in range(4):
                ls = slice(rr * ATTN_W + hp * 128, rr * ATTN_W + (hp + 1) * 128)
                qp = q_ref[pl.ds(r0, CHUNK), ls]
                dop = do_ref[pl.ds(r0, CHUNK), ls]
                k2 = k_ref[keys, ls]
                v2 = v_ref[keys, ls]
                tiles.append((ls, k2))
                for _, hm in sides:
                    qh = jnp.where(hm, qp, zero)
                    doh = jnp.where(hm, dop, zero)
                    heads.append((qh, doh, _dot_nt(k2, qh), _dot_nt(v2, doh)))
            return tiles, heads

        def second(rr, b, heads, both):
            r0, _ = _attn_rows(b)
            ok = jnp.concatenate([m_cur, m_prev], axis=0) if both else m_prev
            lanes = slice(rr * HEAD_W, (rr + 1) * HEAD_W)
            lse_t = lse_ref[pl.ds(r0, CHUNK), lanes].T
            dl_t = dl_ref[pl.ds(r0, CHUNK), lanes].T
            out = []
            for i, (_, _, s_t, dp_t) in enumerate(heads):
                p = jnp.exp(jnp.where(ok, s_t - lse_t[16 * i:16 * i + 1, :], NEG))
                out.append((p.astype(BF16), (p * (dp_t - dl_t[16 * i:16 * i + 1, :])).astype(BF16)))
            return out

        def third(rr, b, tiles, heads, probs, both):
            r0, rp = _attn_rows(b)
            nk = 2 * CHUNK if both else CHUNK
            left_k = _lane_left(nk)
            zero_k = jnp.zeros((nk, CHUNK), BF16)
            for hp, (ls, k2) in enumerate(tiles):
                dq = jnp.zeros((CHUNK, CHUNK), F32)
                dk2 = jnp.zeros((nk, CHUNK), F32)
                dv2 = jnp.zeros((nk, CHUNK), F32)
                for side in range(2):
                    qh, doh, _, _ = heads[2 * hp + side]
                    p, ds = probs[2 * hp + side]
                    dq = dq + _dot_tn(ds, jnp.where(left_k if side == 0 else ~left_k, k2, zero_k))
                    dk2 = dk2 + _dot(ds, qh)
                    dv2 = dv2 + _dot(p, doh)
                dq_ref[pl.ds(r0, CHUNK), ls] = dq.astype(dq_ref.dtype)
                if nblk == 1:
                    dk_ref[pl.ds(r0, CHUNK), ls] = dk2.astype(dk_ref.dtype)
                    dv_ref[pl.ds(r0, CHUNK), ls] = dv2.astype(dv_ref.dtype)
                elif both:
                    dk_ref[pl.ds(rp, CHUNK), ls] = (dk_wait[:, ls] + dk2[0:CHUNK]).astype(dk_ref.dtype)
                    dv_ref[pl.ds(rp, CHUNK), ls] = (dv_wait[:, ls] + dv2[0:CHUNK]).astype(dv_ref.dtype)
                    dk_wait[:, ls] = dk2[CHUNK:]
                    dv_wait[:, ls] = dv2[CHUNK:]
                else:
                    dk_wait[:, ls] = dk2
                    dv_wait[:, ls] = dv2

        def run(units, both):
            data = [first(rr, b, both) for rr, b in units]
            probs = [second(rr, b, heads, both) for (rr, b), (_, heads) in zip(units, data)]
            for (rr, b), (tiles, heads), pr in zip(units, data, probs):
                third(rr, b, tiles, heads, pr, both)

        run([(rr, 0) for rr in range(rps)], False)
        if nblk > 1:
            def one(b, carry):
                run([(0, b)], True)
                return carry

            lax.fori_loop(1, nblk, one, 0)
            last = (nblk - 1) * CHUNK
            dk_ref[last:last + CHUNK, :] = dk_wait[...].astype(dk_ref.dtype)
            dv_ref[last:last + CHUNK, :] = dv_wait[...].astype(dv_ref.dtype)

    spec = pl.BlockSpec((seg, rps * ATTN_W), lambda r: (0, r))
    return _pcall(
        body, name=f"attn_bwd_d{dil}", grid=(dil // rps,),
        in_specs=[spec] * 4 + [pl.BlockSpec((seg, rps * HEAD_W), lambda r: (0, r))] * 2, out_specs=[spec] * 3,
        out_shape=[_sds((seg, dil * ATTN_W), BF16)] * 3,
        scratch_shapes=[pltpu.VMEM((CHUNK, ATTN_W), F32), pltpu.VMEM((CHUNK, ATTN_W), F32)],
        args=(qv, kv, vv, dov, deltav, lsev), comms=comms, after=after)


def _inproj_bwd(dqs, dks, dvs, du, dvs_sgu, pos, x, dx1, w_in, g_pre, comms=()):
    def body(dq1, dq2, dq3, dk1, dk2, dk3, dv1, dv2, dv3, du_ref, dvs_ref, pos_ref, x_ref, dx1_ref, w_ref, g_ref,
             dproj_ref, gx_ref, dg_ref, slabs_a, slabs_b):
        @pl.when(pl.program_id(0) == 0)
        def _():
            dg_ref[...] = jnp.zeros_like(dg_ref)

        def total(r1, r2, r3):
            return r1[...] + _load_view(r2, slabs_a, DILATIONS[1]) + _load_view(r3, slabs_b, DILATIONS[2])

        tabs = _rot_tables(pos_ref[...])
        groups = {3: du_ref[...], 4: dvs_ref[...]}
        dh = jnp.zeros((TM, D_MODEL), F32)
        for g in (3, 4, 0, 1, 2):
            if g == 0:
                groups[g] = _rope_bwd(total(dq1, dq2, dq3) * np.float32(ATTN_SCALE), tabs).astype(BF16)
            elif g == 1:
                groups[g] = _rope_bwd(total(dk1, dk2, dk3), tabs).astype(BF16)
            elif g == 2:
                groups[g] = total(dv1, dv2, dv3).astype(BF16)
            dproj_ref[:, 512 * g:512 * (g + 1)] = groups[g]
            off = 0
            for s, a, b in _in_pieces(g):
                dh = dh + _dot_nt(groups[g][:, off:off + b - a], w_ref[s, :, a:b])
                off += b - a
        g = g_ref[...]
        xh, r = _rms_stats(x_ref[...])
        dx, dg = _rms_bwd(xh, r, g, dh)
        dg_ref[...] += dg
        gx_ref[...] = dx1_ref[...] + dx

    return _pcall(
        body, name="inproj_bwd", grid=(SEQ // TM,),
        in_specs=[_view_rows(dil) for dil in DILATIONS] * 3
        + [_rows(512), _rows(512), _rows(1), _rows(D_MODEL), _rows(D_MODEL),
           _resident((N_SHARD, D_MODEL, IN_S)), _resident((1, D_MODEL))],
        out_specs=[_rows(PROJ_W), _rows(D_MODEL), _acc(D_MODEL)],
        out_shape=[_sds((SEQ, PROJ_W), BF16), _sds((SEQ, D_MODEL), F32), _sds((1, D_MODEL), F32)],
        scratch_shapes=[_slab_scratch(), _slab_scratch()],
        args=(*dqs, *dks, *dvs, du, dvs_sgu, pos, x, dx1, w_in, g_pre), comms=comms)


def _coords():
    return lax.axis_index("x"), lax.axis_index("y"), lax.axis_index("c")


def _other_chips(x, y):
    return [(1 - x, y), (x, 1 - y), (1 - x, 1 - y)]


WEIGHTS = ("pre_mix_norm", "w_in", "sgu_ln_gain", "sgu_ln_bias", "sgu_w_spatial", "sgu_b_spatial", "attn_out_norm",
           "sgu_out_norm", "w_out", "post_mix_norm", "pre_ffn_norm", "w_gate", "w_up", "w_down", "post_ffn_norm")
SMALL = ("pre_mix_norm", "post_mix_norm", "pre_ffn_norm", "post_ffn_norm", "sgu_ln_gain", "sgu_ln_bias",
         "attn_out_norm", "sgu_out_norm", "sgu_w_spatial", "sgu_b_spatial")


def _remote(src, dst, send_sem, recv_sem, to):
    return pltpu.make_async_remote_copy(src_ref=src, dst_ref=dst, send_sem=send_sem, recv_sem=recv_sem,
                                        device_id=to, device_id_type=MESH)


def _halves(a):
    *lead, rows, cols = a.shape
    return a.reshape(*lead, 2, rows // 2, cols)


def _gather_ici(shards):
    n = len(shards)

    def desc(ins, outs, ss, rs, j, w, landed):
        x, y, c = _coords()
        cx, cy = _other_chips(x, y)[j]
        shard = 2 * cx + cy if landed else 2 * x + y
        return _remote(ins[w].at[c], outs[w].at[shard, c], ss.at[j * n + w], rs.at[j * n + w], (cx, cy, c))

    def start(ins, outs, ss, rs):
        for j in range(3):
            for w in range(n):
                desc(ins, outs, ss, rs, j, w, False).start()

    def finish(ins, outs, ss, rs):
        for j in range(3):
            for w in range(n):
                desc(ins, outs, ss, rs, j, w, True).wait_recv()
                desc(ins, outs, ss, rs, j, w, False).wait_send()

    return _Comm(shards, [_sds((N_SHARD,) + s.shape, s.dtype) for s in shards], 3 * n, start, finish)


def _gather_pass(fulls):
    n = len(fulls)

    def desc(bufs, ss, rs, j, w, landed):
        x, y, c = _coords()
        cx, cy = _other_chips(x, y)[j]
        shard = 2 * cx + cy
        return _remote(bufs[w].at[shard, c], bufs[w].at[shard, 1 - c if landed else c],
                       ss.at[j * n + w], rs.at[j * n + w], (x, y, 1 - c))

    def start(ins, outs, ss, rs):
        for j in range(3):
            for w in range(n):
                desc(outs, ss, rs, j, w, False).start()

    def finish(ins, outs, ss, rs):
        for j in range(3):
            for w in range(n):
                desc(outs, ss, rs, j, w, True).wait_recv()
                desc(outs, ss, rs, j, w, False).wait_send()

    return _Comm(fulls, [_sds(f.shape, f.dtype) for f in fulls], 3 * n, start, finish, aliased=True)


def _rs_sibling(gws):
    n = len(gws)

    def desc(ins, outs, ss, rs, w):
        x, y, c = _coords()
        return _remote(ins[w].at[:, 1 - c], outs[w], ss.at[w], rs.at[w], (x, y, 1 - c))

    def start(ins, outs, ss, rs):
        for w in range(n):
            desc(ins, outs, ss, rs, w).start()

    def finish(ins, outs, ss, rs):
        for w in range(n):
            desc(ins, outs, ss, rs, w).wait()

    out_shape = [_sds((N_SHARD, g.shape[1] // 2, g.shape[2]), g.dtype) for g in gws]
    return _Comm([_halves(g) for g in gws], out_shape, n, start, finish)


def _rs_chips(pbs):
    n = len(pbs)

    def desc(ins, outs, ss, rs, j, w):
        x, y, c = _coords()
        cx, cy = _other_chips(x, y)[j]
        return _remote(ins[w].at[2 * cx + cy], outs[w].at[j], ss.at[j * n + w], rs.at[j * n + w], (cx, cy, c))

    def start(ins, outs, ss, rs):
        for j in range(3):
            for w in range(n):
                desc(ins, outs, ss, rs, j, w).start()

    def finish(ins, outs, ss, rs):
        for j in range(3):
            for w in range(n):
                desc(ins, outs, ss, rs, j, w).wait()

    return _Comm(pbs, [_sds((3,) + p.shape[1:], p.dtype) for p in pbs], 3 * n, start, finish)


def _rs_join(halves):
    n = len(halves)

    def desc(bufs, ss, rs, w, landed):
        x, y, c = _coords()
        return _remote(bufs[w].at[c], bufs[w].at[1 - c if landed else c], ss.at[w], rs.at[w], (x, y, 1 - c))

    def start(ins, outs, ss, rs):
        for w in range(n):
            desc(outs, ss, rs, w, False).start()

    def finish(ins, outs, ss, rs):
        for w in range(n):
            desc(outs, ss, rs, w, True).wait_recv()
            desc(outs, ss, rs, w, False).wait_send()

    return _Comm(halves, [_sds(h.shape, h.dtype) for h in halves], n, start, finish, aliased=True)


def _small_exchange(buf):
    def desc(ins, outs, ss, rs, k, landed):
        x, y, c = _coords()
        px = 1 - x if (k >> 2) & 1 else x
        py = 1 - y if (k >> 1) & 1 else y
        pc = 1 - c if k & 1 else c
        slot = 4 * px + 2 * py + pc if landed else 4 * x + 2 * y + c
        return _remote(ins[0], outs[0].at[slot], ss.at[k - 1], rs.at[k - 1], (px, py, pc))

    def start(ins, outs, ss, rs):
        for k in range(1, 8):
            desc(ins, outs, ss, rs, k, False).start()

    def finish(ins, outs, ss, rs):
        for k in range(1, 8):
            desc(ins, outs, ss, rs, k, True).wait_recv()
            desc(ins, outs, ss, rs, k, False).wait_send()

    return _Comm([buf], [_sds((8,) + buf.shape, buf.dtype)], 7, start, finish)


HBM = pl.BlockSpec(memory_space=pltpu.HBM)
SEM = pl.BlockSpec(memory_space=pltpu.SEMAPHORE)
DATAFLOW = pltpu.SideEffectType.DATAFLOW_SIDE_EFFECTING


def _split_starts(name, comms, after):
    srcs = [[pltpu.with_memory_space_constraint(s, pltpu.HBM) for s in c.args] for c in comms]
    lands = [[pltpu.with_memory_space_constraint(lax.empty(o.shape, o.dtype), pltpu.HBM) for o in c.out_shape]
             for c in comms]
    bufs = [b for k in range(len(comms)) for b in srcs[k] + lands[k]]
    nb, nc = len(bufs), len(comms)

    def body(*refs):
        sems = refs[nb + 1:nb + 1 + 2 * nc]
        off = 0
        for k, c in enumerate(comms):
            ns, nl = len(srcs[k]), len(lands[k])
            c.start(refs[off:off + ns], refs[off + ns:off + ns + nl], sems[2 * k], sems[2 * k + 1])
            off += ns + nl
        refs[-1][...] = jnp.zeros_like(refs[-1])

    res = pl.pallas_call(
        body, name=name,
        out_shape=(*[pltpu.SemaphoreType.DMA((c.n_sems,)) for c in comms for _ in range(2)],
                   *[pltpu.HBM(b.shape, b.dtype) for b in bufs], _sds((8, 128), F32)),
        in_specs=[HBM] * nb + [ANY],
        out_specs=(*[SEM] * (2 * nc), *[HBM] * nb, pl.BlockSpec(memory_space=pltpu.VMEM)),
        input_output_aliases={i: 2 * nc + i for i in range(nb)},
        compiler_params=pltpu.CompilerParams(has_side_effects=DATAFLOW),
    )(*bufs, after)
    states, off = [], 2 * nc
    for k in range(nc):
        ns, nl = len(srcs[k]), len(lands[k])
        states.append((res[2 * k], res[2 * k + 1], list(res[off:off + ns]), list(res[off + ns:off + ns + nl])))
        off += ns + nl
    return states, res[-1]


def _split_wait(name, comm, send_sems, recv_sems, srcs, lands, after):
    ns, nb = len(srcs), len(lands)
    after = list(after) if isinstance(after, (list, tuple)) else [after]

    def body(*refs):
        comm.finish(refs[:ns], refs[ns:ns + nb], refs[ns + nb], refs[ns + nb + 1])

    res = pl.pallas_call(
        body, name=name,
        out_shape=tuple(pltpu.HBM(b.shape, b.dtype) for b in srcs + lands),
        in_specs=[HBM] * (ns + nb) + [SEM, SEM] + [ANY] * len(after), out_specs=tuple([HBM] * (ns + nb)),
        input_output_aliases={i: i for i in range(ns + nb)},
        compiler_params=pltpu.CompilerParams(has_side_effects=DATAFLOW),
    )(*srcs, *lands, send_sems, recv_sems, *after)
    return list(res[ns:])


LOSS_ROW = "loss_cols"
SMALL_EARLY = ("post_mix_norm", "pre_ffn_norm", "post_ffn_norm", "sgu_ln_gain", "sgu_ln_bias", "attn_out_norm",
               "sgu_out_norm", "sgu_b_spatial", LOSS_ROW)
SMALL_LATE = ("pre_mix_norm",)
SMALL_WSP = "sgu_w_spatial"


def _pack(d, names, rows):
    flat = [d[n].reshape(-1) for n in names]
    used = sum(f.shape[0] for f in flat)
    flat.append(jnp.zeros((rows * 1024 - used,), F32))
    return jnp.concatenate(flat).reshape(rows, 1024)


def _unpack(buf, names, shapes):
    flat = buf.reshape(-1)
    out, off = {}, 0
    for n in names:
        size = int(np.prod(shapes[n]))
        out[n] = flat[off:off + size].reshape(shapes[n])
        off += size
    return out


def _chip_sums(gbs, recvs, shard_core):
    n = len(gbs)
    _, rows, cols = gbs[0].shape
    hw = rows // 2

    def body(sc_ref, *refs):
        for k in range(n):
            refs[2 * n + k][...] = (refs[k][...].astype(F32) + refs[n + k][...].astype(F32)).astype(BF16)

    def other(s, sc):
        return jnp.where(s >= sc[0], s + 1, s)

    mine = pl.BlockSpec((1, hw, cols), lambda s, sc: (other(s, sc), sc[1], 0))
    plain = pl.BlockSpec((1, hw, cols), lambda s, sc: (other(s, sc), 0, 0))
    return pl.pallas_call(
        body, name="rs_chip_sums",
        grid_spec=pltpu.PrefetchScalarGridSpec(num_scalar_prefetch=1, grid=(N_SHARD - 1,),
                                               in_specs=[mine] * n + [plain] * n, out_specs=[plain] * n),
        out_shape=[_sds((N_SHARD, hw, cols), BF16)] * n,
        compiler_params=_seq_params(),
    )(shard_core, *gbs, *recvs)


def _final_sums(gws, recv_sibs, recv_chips, shard_core):
    n = len(gws)
    halves = [(g.shape[1] // 2, g.shape[2]) for g in gws]

    def body(sc_ref, *refs):
        for k in range(n):
            acc = refs[k][0] + refs[n + k][0]
            for j in range(3):
                acc = acc + refs[2 * n + k][j].astype(F32)
            refs[3 * n + k][0] = acc

    def specs(lead, index):
        return [pl.BlockSpec((lead, hw, cols), index) for hw, cols in halves]

    return pl.pallas_call(
        body, name="rs_final_sums",
        grid_spec=pltpu.PrefetchScalarGridSpec(
            num_scalar_prefetch=1, grid=(1,),
            in_specs=specs(1, lambda i, sc: (sc[0], sc[1], 0)) + specs(1, lambda i, sc: (sc[0], 0, 0))
            + specs(3, lambda i, sc: (0, 0, 0)),
            out_specs=specs(1, lambda i, sc: (sc[1], 0, 0))),
        out_shape=[_sds((2, hw, cols), F32) for hw, cols in halves],
        compiler_params=_seq_params(),
    )(shard_core, *gws, *recv_sibs, *recv_chips)


ADAM_BLOCKS = 4


def _adamw_multi(ws, gs, ms, vs, name, after=()):
    n = len(ws)

    def body(*refs):
        outs = refs[4 * n + len(after):]
        for k in range(n):
            g = refs[n + k][...]
            d, m, v = _adam_math(refs[k][...], g, refs[2 * n + k][...], refs[3 * n + k][...])
            outs[4 * k][...], outs[4 * k + 1][...], outs[4 * k + 2][...], outs[4 * k + 3][...] = g, d, m, v

    specs = [pl.BlockSpec((w.shape[0] // ADAM_BLOCKS, w.shape[1]), lambda i: (i, 0)) for w in ws]
    res = pl.pallas_call(
        body, name=name, grid=(ADAM_BLOCKS,), in_specs=specs * 4 + [ANY] * len(after),
        out_specs=[s for s in specs for _ in range(4)],
        out_shape=[_sds(w.shape, F32) for w in ws for _ in range(4)],
        compiler_params=_seq_params(),
    )(*ws, *gs, *ms, *vs, *after)
    return [tuple(res[4 * k:4 * k + 4]) for k in range(n)]


def _wgrad_ff(a_list, b, name):
    n = len(a_list)
    cols = 256

    def body(*refs):
        bv = refs[n][...]
        for k in range(n):
            g = _dot_tn(refs[k][...], bv)
            refs[n + 1 + k][...] = g
            refs[2 * n + 1 + k][...] = g.astype(BF16)

    res = _pcall(
        body, name=name, grid=(FF // cols,),
        in_specs=[pl.BlockSpec((SEQ, cols), lambda j: (0, j))] * n
        + [pl.BlockSpec((SEQ, D_MODEL), lambda j: (0, 0), pipeline_mode=pl.Buffered(1))],
        out_specs=[pl.BlockSpec((cols, D_MODEL), lambda j: (j, 0))] * (2 * n),
        out_shape=[_sds((FF, D_MODEL), F32)] * n + [_sds((FF, D_MODEL), BF16)] * n, args=(*a_list, b))
    return [m.reshape(N_SHARD, FF_S, D_MODEL) for m in res]


def _comm_only(name, comms):
    return _pcall(lambda: None, name=name, grid=(1,), in_specs=[], out_specs=[], out_shape=[], args=(),
                  comms=comms)[1]


def _adam_math(w, g, m, v):
    m = ADAM_B1 * m + (1.0 - ADAM_B1) * g
    v = ADAM_B2 * v + (1.0 - ADAM_B2) * (g * g)
    m_hat = m / (1.0 - ADAM_B1 ** ADAM_STEP)
    v_hat = v / (1.0 - ADAM_B2 ** ADAM_STEP)
    return -ADAM_LR * (m_hat / (jnp.sqrt(v_hat) + ADAM_EPS) + ADAM_WD * w), m, v


def _adamw_small(own, slots, w, m, v, me):
    rows, cols = own.shape

    def body(me_ref, own_ref, slots_ref, w_ref, m_ref, v_ref, g_ref, d_ref, nm_ref, nv_ref):
        own_v = own_ref[...].astype(F32)
        g = jnp.where(me_ref[0] == 0, own_v, slots_ref[0].astype(F32))
        for i in range(1, 8):
            g = g + jnp.where(me_ref[0] == i, own_v, slots_ref[i].astype(F32))
        g_ref[...] = g
        d_ref[...], nm_ref[...], nv_ref[...] = _adam_math(w_ref[...], g, m_ref[...], v_ref[...])

    flat = pl.BlockSpec((rows, cols), lambda i, me_ref: (0, 0))
    return pl.pallas_call(
        body, name="adamw_small",
        grid_spec=pltpu.PrefetchScalarGridSpec(
            num_scalar_prefetch=1, grid=(1,),
            in_specs=[flat, pl.BlockSpec((8, rows, cols), lambda i, me_ref: (0, 0, 0)), flat, flat, flat],
            out_specs=[flat] * 4),
        out_shape=[_sds((rows, cols), F32)] * 4,
        compiler_params=_seq_params(),
    )(me, own, slots, w, m, v)


def kernel(x, positions, pre_mix_norm, w_in, sgu_ln_gain, sgu_ln_bias, sgu_w_spatial, sgu_b_spatial, attn_out_norm, sgu_out_norm, w_out, post_mix_norm, pre_ffn_norm, w_gate, w_up, w_down, post_ffn_norm, loss_target, m_pre_mix_norm, m_w_in, m_sgu_ln_gain, m_sgu_ln_bias, m_sgu_w_spatial, m_sgu_b_spatial, m_attn_out_norm, m_sgu_out_norm, m_w_out, m_post_mix_norm, m_pre_ffn_norm, m_w_gate, m_w_up, m_w_down, m_post_ffn_norm, v_pre_mix_norm, v_w_in, v_sgu_ln_gain, v_sgu_ln_bias, v_sgu_w_spatial, v_sgu_b_spatial, v_attn_out_norm, v_sgu_out_norm, v_w_out, v_post_mix_norm, v_pre_ffn_norm, v_w_gate, v_w_up, v_w_down, v_post_ffn_norm):
    a = dict(locals())
    cx, cy, cc = _coords()
    s_me = 2 * cx + cy
    shard_core = jnp.stack([s_me, cc]).astype(jnp.int32)
    me = jnp.stack([4 * cx + 2 * cy + cc]).astype(jnp.int32)
    small = {n: (a[n][0] if a[n].ndim > 2 else a[n]) for n in SMALL}
    b_t = small["sgu_b_spatial"].T
    xs, pos, target = x[0], positions.reshape(SEQ, 1), loss_target[0]
    flipped = ("w_gate", "w_up")

    def big(name, n):
        return jnp.swapaxes(a[name], 1, 2)[0] if n in flipped else a[name][0]

    own = {"w_in": _halves(big("w_in", "w_in").astype(BF16))}

    def with_own(full, n):
        full = lax.dynamic_update_slice(full, own[n][None], (s_me, 0, 0, 0))
        return full.reshape((N_SHARD,) + big(n, n).shape)

    ffn = ("w_gate", "w_up", "w_down")
    g_in = _gather_ici([own["w_in"]])
    (s_in,), token = _split_starts("gather_in_start", [g_in], small["pre_mix_norm"])
    for n in ("w_out",) + ffn:
        own[n] = _halves((big(n, n) + token[0:1, 0:1]).astype(BF16))
    g_out, g_ffn = _gather_ici([own["w_out"]]), _gather_ici([own[n] for n in ffn])
    (s_out, s_ffn), token = _split_starts("gather_rest_start", [g_out, g_ffn], token)
    in_lands = _split_wait("gather_in_wait", g_in, *s_in, token)
    ((in_lands,),) = _comm_only("comm_pass_in", [_gather_pass(in_lands)])
    w_in_f = with_own(in_lands, "w_in")
    h, u, vs, sgu, *qkv = _inproj_fwd(xs, pos, small["pre_mix_norm"], w_in_f, small["sgu_ln_gain"],
                                      small["sgu_ln_bias"], small["sgu_w_spatial"], b_t)
    views = [tuple(qkv[3 * i:3 * i + 3]) for i in range(len(DILATIONS))]
    out_lands = _split_wait("gather_out_wait", g_out, *s_out, sgu)
    o_list, l_list = [], []
    for dil, (qv, kv, vv) in zip(DILATIONS, views):
        if dil == 1:
            (o, l), ((out_lands,),) = _attn_fwd(qv, kv, vv, dil, comms=[_gather_pass(out_lands)])
        else:
            o, l = _attn_fwd(qv, kv, vv, dil)
        o_list.append(o)
        l_list.append(l)
    w_out_f = with_own(out_lands, "w_out")
    ffn_lands = _split_wait("gather_ffn_wait", g_ffn, *s_ffn, l_list[-1])
    (attn, mixed, y, x1, *lses), (ffn_lands,) = _mix_out_fwd(
        o_list, l_list, sgu, xs, w_out_f, small["attn_out_norm"], small["sgu_out_norm"], small["post_mix_norm"],
        comms=[_gather_pass(ffn_lands)])
    w_gate_f, w_up_f, w_down_f = (with_own(f, n).reshape(FF, D_MODEL) for f, n in zip(ffn_lands, ffn))
    h2, act, dg, dup, df, dx1, loss_cols, d_pre_ffn, d_post_ffn = _ffn_fwd_bwd(
        x1, target, w_gate_f, w_up_f, w_down_f, small["pre_ffn_norm"], small["post_ffn_norm"])

    full_tok = pl.BlockSpec((SEQ, D_MODEL), lambda s: (0, 0), pipeline_mode=pl.Buffered(1))
    gw, gb = {}, {}
    gw["w_gate"], gw["w_up"], gb["w_gate"], gb["w_up"] = _wgrad_ff([dg, dup], h2, "wgrad_gate_up")
    gw["w_down"], gb["w_down"] = _wgrad_ff([act], df, "wgrad_down")
    (dy, du, dvs_sgu, d_post_mix, d_attn_norm, d_sgu_norm, d_w_sp, d_b_sp, d_ln_gain, d_ln_bias,
     *dviews), (sib_ffn,) = _outproj_bwd(
        dx1, y, attn, sgu, w_out_f, small["post_mix_norm"], small["attn_out_norm"], small["sgu_out_norm"],
        u, vs, small["sgu_ln_gain"], small["sgu_ln_bias"], small["sgu_w_spatial"], b_t,
        comms=[_rs_sibling([gb[n] for n in ffn])])
    sib = dict(zip(ffn, sib_ffn))
    part = dict(zip(ffn, _chip_sums([gb[n] for n in ffn], [sib[n] for n in ffn], shard_core)))
    gw["w_out"], gb["w_out"] = _wgrad(mixed, dy, pl.BlockSpec((SEQ, OUT_S), lambda s: (0, s)), full_tok,
                                      (OUT_S, D_MODEL), "wgrad_out")
    x_ffn = _rs_chips([part[n] for n in ffn])
    packed_early = _pack({
        "sgu_ln_gain": d_ln_gain, "sgu_ln_bias": d_ln_bias, "sgu_b_spatial": d_b_sp,
        "attn_out_norm": d_attn_norm, "sgu_out_norm": d_sgu_norm, "post_mix_norm": d_post_mix,
        "pre_ffn_norm": d_pre_ffn, "post_ffn_norm": d_post_ffn, LOSS_ROW: loss_cols}, SMALL_EARLY, 8)
    wsp_view = (SGU_GROUPS * CHUNK, CHUNK)
    packed_wsp = d_w_sp.reshape(wsp_view).astype(BF16)
    x_small, x_wsp = _small_exchange(packed_early), _small_exchange(packed_wsp)
    (s_ffn,), token = _split_starts("rs_ffn_start", [x_ffn], small["pre_mix_norm"])

    dqs, dks, dvs = [], [], []
    for i, (dil, (qv, kv, vv)) in enumerate(zip(DILATIONS, views)):
        if dil == 1:
            (dq, dk, dv), ((sib["w_out"],),) = _attn_bwd(qv, kv, vv, dviews[i], dviews[3 + i], lses[i], dil,
                                                        comms=[_rs_sibling([gb["w_out"]])], after=[token])
            (part["w_out"],) = _chip_sums([gb["w_out"]], [sib["w_out"]], shard_core)
            x_out = _rs_chips([part["w_out"]])
            (s_out, s_small, s_wsp), token = _split_starts("rs_out_small_start", [x_out, x_small, x_wsp], token)
        else:
            dq, dk, dv = _attn_bwd(qv, kv, vv, dviews[i], dviews[3 + i], lses[i], dil, after=[token])
        dqs.append(dq)
        dks.append(dk)
        dvs.append(dv)
    half, far, joined = {}, {}, {}
    far.update(zip(ffn, _split_wait("rs_ffn_wait", x_ffn, *s_ffn, dvs[-1])))
    dproj, grad_x, d_pre_mix = _inproj_bwd(dqs, dks, dvs, du, dvs_sgu, pos, xs, dx1, w_in_f, small["pre_mix_norm"])
    (far["w_out"],) = _split_wait("rs_out_wait", x_out, *s_out, grad_x)
    names = ffn + ("w_out",)
    half.update(zip(names, _final_sums([gw[n] for n in names], [sib[n] for n in names], [far[n] for n in names],
                                       shard_core)))
    packed_late = _pack({"pre_mix_norm": d_pre_mix}, SMALL_LATE, 8)
    (gw["w_in"], gb["w_in"]), (got, (slots_late,)) = _wgrad(
        h, dproj, full_tok, pl.BlockSpec((SEQ, IN_S), lambda s: (0, s)), (D_MODEL, IN_S), "wgrad_in",
        comms=[_rs_join([half[n] for n in names]), _small_exchange(packed_late)])
    joined.update(zip(names, got))
    ((sib["w_in"],),) = _comm_only("comm_rs_sibling_in", [_rs_sibling([gb["w_in"]])])
    (part["w_in"],) = _chip_sums([gb["w_in"]], [sib["w_in"]], shard_core)
    x_in = _rs_chips([part["w_in"]])
    (s_in,), token = _split_starts("rs_in_start", [x_in], small["pre_mix_norm"])

    grads, deltas, new_m, new_v = {}, {}, {}, {}

    def record(n, outs):
        grads[n], deltas[n], new_m[n], new_v[n] = (
            jnp.swapaxes(o[None], 1, 2) if n in flipped else o[None] for o in outs)

    def update(names, name, after):
        for n, outs in zip(names, _adamw_multi(
                [big(n, n) for n in names], [joined[n].reshape(big(n, n).shape) for n in names],
                [big("m_" + n, n) for n in names], [big("v_" + n, n) for n in names], name, after)):
            record(n, outs)

    update(names, "adamw_ffn_out", [token])
    (slots_early,) = _split_wait("small_early_wait", x_small, *s_small, new_v["w_out"])
    (slots_wsp,) = _split_wait("small_wsp_wait", x_wsp, *s_wsp, slots_early)
    (far["w_in"],) = _split_wait("rs_in_wait", x_in, *s_in, slots_wsp)
    (half["w_in"],) = _final_sums([gw["w_in"]], [sib["w_in"]], [far["w_in"]], shard_core)
    ((joined["w_in"],),) = _comm_only("comm_rs_join_in", [_rs_join([half["w_in"]])])
    update(("w_in",), "adamw_w_in", [])
    a[LOSS_ROW] = a["m_" + LOSS_ROW] = a["v_" + LOSS_ROW] = jnp.zeros((1, D_MODEL), F32)
    outs = _adamw_small(packed_wsp, slots_wsp, *[a[p + SMALL_WSP].reshape(wsp_view) for p in ("", "m_", "v_")], me)
    for dst, buf in zip((grads, deltas, new_m, new_v), outs):
        dst[SMALL_WSP] = buf.reshape(a[SMALL_WSP].shape)
    for names, rows, packed, slots in ((SMALL_EARLY, 8, packed_early, slots_early),
                                       (SMALL_LATE, 8, packed_late, slots_late)):
        outs = _adamw_small(packed, slots, _pack(a, names, rows), _pack({n: a["m_" + n] for n in names}, names, rows),
                            _pack({n: a["v_" + n] for n in names}, names, rows), me)
        for dst, buf in zip((grads, deltas, new_m, new_v), outs):
            dst.update(_unpack(buf, names, {n: a[n].shape for n in names}))
    loss = jnp.sum(grads[LOSS_ROW]) * np.float32(0.5 / D_MODEL)
    return (loss, grad_x[None], *[grads[n] for n in WEIGHTS], *[deltas[n] for n in WEIGHTS],
            *[new_m[n] for n in WEIGHTS], *[new_v[n] for n in WEIGHTS])
```

```python
import numpy as np
import jax
import jax.numpy as jnp
from jax import lax
from jax.experimental import pallas as pl
from jax.experimental.pallas import tpu as pltpu

F32 = jnp.float32
BF16 = jnp.bfloat16

SEQ = 2048
D_MODEL = 1024
HEAD_DIM = 64
ATTN_W = 512
SGU_W = 512
SGU_GROUPS = 8
CHUNK = 128
DILATIONS = (1, 4, 16)
N_SHARD = 4
IN_S = 640
OUT_S = 256
FF_S = 704
PROJ_W = N_SHARD * IN_S
FF = N_SHARD * FF_S
FF_CHUNKS = ((0, 1024), (1024, 2048), (2048, FF))
RMS_EPS = 1e-6
LN_EPS = 1e-5
ROPE_THETA = 500000.0
ATTN_SCALE = 1.0 / np.sqrt(HEAD_DIM)
NEG = -1e30
TM = 512
TM_FFN = 256
VMEM_LIMIT = 56 * 1024 * 1024

ADAM_LR = 0.001
ADAM_B1 = 0.9
ADAM_B2 = 0.999
ADAM_EPS = 1e-08
ADAM_WD = 0.01
ADAM_STEP = 10

MESH = pl.DeviceIdType.MESH
ANY = pl.BlockSpec(memory_space=pl.ANY)


def _dot(a, b):
    return jnp.dot(a, b, preferred_element_type=F32)


def _dot_nt(a, b):
    return lax.dot_general(a, b, (((1,), (1,)), ((), ())), preferred_element_type=F32)


def _dot_tn(a, b):
    return lax.dot_general(a, b, (((0,), (0,)), ((), ())), preferred_element_type=F32)


def _dot_exact(a, b):
    return jnp.dot(a, b, preferred_element_type=F32, precision=lax.Precision.HIGHEST)


def _dot_select(a, sel):
    hi = a.astype(BF16)
    lo = (a - hi.astype(F32)).astype(BF16)
    sel = sel.astype(BF16)
    return _dot(hi, sel) + _dot(lo, sel)


def _rms_stats(x):
    r = lax.rsqrt(jnp.mean(x * x, axis=-1, keepdims=True) + RMS_EPS)
    return x * r, r


def _rms_bwd(xh, r, gain, dy):
    dxh = dy * gain
    dx = r * (dxh - xh * jnp.mean(dxh * xh, axis=-1, keepdims=True))
    return dx, jnp.sum(dy * xh, axis=0, keepdims=True)


_ERF_ALPHA = (-2.72614225801306e-10, 2.77068142495902e-08, -2.10102402082508e-06, -5.69250639462346e-05,
              -7.34990630326855e-04, -2.95459980854025e-03, -1.60960333262415e-02)
_ERF_BETA = (-1.45660718464996e-05, -2.13374055278905e-04, -1.68282697438203e-03, -7.37332916720468e-03,
             -1.42647390514189e-02)


def _erf(x):
    x = jnp.clip(x, -4.0, 4.0)
    x2 = x * x
    p = jnp.full_like(x, _ERF_ALPHA[0])
    for a in _ERF_ALPHA[1:]:
        p = p * x2 + a
    q = jnp.full_like(x, _ERF_BETA[0])
    for b in _ERF_BETA[1:]:
        q = q * x2 + b
    return x * p / q


def _normal_cdf(x):
    return 0.5 * (1.0 + _erf(x * np.float32(1.0 / np.sqrt(2.0))))


def _gelu_grad(x, cdf):
    pdf = jnp.exp(-0.5 * x * x) * np.float32(1.0 / np.sqrt(2.0 * np.pi))
    return cdf + x * pdf


def _sigmoid(x):
    return 1.0 / (1.0 + jnp.exp(-x))


_INV_FREQ = tuple(float(np.float32(ROPE_THETA ** (-2.0 * j / 16.0))) for j in range(8))


def _rot_tables(pos):
    lane = lax.broadcasted_iota(jnp.int32, (1, 128), 1)
    d = lane & 63
    j = d & 7
    inv = jnp.zeros((1, 128), F32)
    for jj in range(8):
        inv = jnp.where(j == jj, _INV_FREQ[jj], inv)
    ang = pos.astype(F32) * inv
    c = jnp.cos(ang)
    s = jnp.sin(ang)
    cos_t = jnp.where(d < 16, c, 1.0)
    sin_a = jnp.where(d < 8, -s, 0.0)
    sin_b = jnp.where((d >= 8) & (d < 16), s, 0.0)
    return tuple(jnp.tile(t, (1, 4)) for t in (cos_t, sin_a, sin_b))


def _rope(x, tabs):
    cos_t, sin_a, sin_b = tabs
    return x * cos_t + pltpu.roll(x, 504, 1) * sin_a + pltpu.roll(x, 8, 1) * sin_b


def _rope_bwd(dy, tabs):
    cos_t, sin_a, sin_b = tabs
    return dy * cos_t + pltpu.roll(dy * sin_a, 8, 1) + pltpu.roll(dy * sin_b, 504, 1)


def _left_half():
    return lax.broadcasted_iota(jnp.int32, (CHUNK, CHUNK), 1) < HEAD_DIM


def _group_ones():
    lane = lax.broadcasted_iota(jnp.int32, (SGU_GROUPS, SGU_W), 1)
    row = lax.broadcasted_iota(jnp.int32, (SGU_GROUPS, SGU_W), 0)
    return ((lane >> 6) == row).astype(F32)


def _masked_spatial(w_ref):
    row = lax.broadcasted_iota(jnp.int32, (CHUNK, CHUNK), 0)
    col = lax.broadcasted_iota(jnp.int32, (CHUNK, CHUNK), 1)
    return [jnp.where(col <= row, w_ref[g], 0.0).astype(BF16) for g in range(SGU_GROUPS)]


def _sgu_core(u, vs, lg, lb, wm, bias_full):
    tm = u.shape[0]
    cdf_u, cdf_vs = _normal_cdf(u), _normal_cdf(vs)
    gu = u * cdf_u
    gv = vs * cdf_vs
    mu = jnp.mean(gv, axis=-1, keepdims=True)
    xc = gv - mu
    rstd = lax.rsqrt(jnp.mean(xc * xc, axis=-1, keepdims=True) + LN_EPS)
    xh = xc * rstd
    vnb = (xh * lg + lb).astype(BF16)
    left = _left_half()
    rows = []
    for c in range(tm // CHUNK):
        pieces = []
        for p in range(4):
            vp = vnb[c * CHUNK:(c + 1) * CHUNK, p * 128:(p + 1) * 128]
            pieces.append(jnp.where(left, _dot(wm[2 * p], vp), _dot(wm[2 * p + 1], vp)))
        rows.append(jnp.concatenate(pieces, axis=1) + bias_full)
    mixed = jnp.concatenate(rows, axis=0)
    return gu, xh, rstd, vnb, mixed, cdf_u, cdf_vs


def _resident(shape):
    n = len(shape)
    return pl.BlockSpec(shape, lambda *_: (0,) * n, pipeline_mode=pl.Buffered(1))


def _rows(ncol, tm=TM):
    return pl.BlockSpec((tm, ncol), lambda i: (i, 0))


def _acc(ncol, nrow=1):
    return pl.BlockSpec((nrow, ncol), lambda i: (0, 0))


HEAD_W = 128


def _view_rows(dil, width=ATTN_W, tm=TM):
    return pl.BlockSpec((tm // dil, dil * width), lambda i: (i, 0))


def _view_shape(dil, dtype, width=ATTN_W):
    return _sds((SEQ // dil, dil * width), dtype)


def _slab_scratch():
    return pltpu.VMEM((4, TM, 128), F32)


def _store_view(val, out_ref, slabs, dil):
    width = val.shape[1]
    for j in range(width // 128):
        slabs[j] = val[:, j * 128:(j + 1) * 128]
    for r in range(dil):
        for j in range(width // 128):
            c0 = r * width + j * 128
            out_ref[:, c0:c0 + 128] = slabs.at[j][pl.ds(r, TM // dil, stride=dil), :].astype(out_ref.dtype)


def _load_view(in_ref, slabs, dil, width=ATTN_W):
    for r in range(dil):
        for j in range(width // 128):
            c0 = r * width + j * 128
            slabs.at[j][pl.ds(r, TM // dil, stride=dil), :] = in_ref[:, c0:c0 + 128].astype(F32)
    return jnp.concatenate([slabs[j] for j in range(width // 128)], axis=1)


def _head_spread():
    m = lax.broadcasted_iota(jnp.int32, (HEAD_W, ATTN_W), 0)
    lane = lax.broadcasted_iota(jnp.int32, (HEAD_W, ATTN_W), 1)
    return (m == 16 * (lane >> 6)).astype(F32)


def _head_sum():
    lane = lax.broadcasted_iota(jnp.int32, (ATTN_W, HEAD_W), 0)
    m = lax.broadcasted_iota(jnp.int32, (ATTN_W, HEAD_W), 1)
    return ((lane >> 6) == (m >> 4)).astype(F32)


def _seq_params():
    return pltpu.CompilerParams(dimension_semantics=("arbitrary",), vmem_limit_bytes=VMEM_LIMIT)


def _sds(shape, dtype):
    return jax.ShapeDtypeStruct(shape, dtype)


class _Comm:
    def __init__(self, args, out_shape, n_sems, start, finish, aliased=False):
        self.args, self.out_shape, self.n_sems = list(args), list(out_shape), n_sems
        self.start, self.finish, self.aliased = start, finish, aliased


def _pcall(body, *, name, grid, in_specs, out_specs, out_shape, args, scratch_shapes=(), comms=(), after=()):
    single = not isinstance(out_shape, (list, tuple))
    out_specs = [out_specs] if single else list(out_specs)
    out_shape = [out_shape] if single else list(out_shape)
    n_in, n_out, n_scr = len(in_specs), len(out_shape), len(scratch_shapes)
    c_args = [a for c in comms for a in c.args]
    c_outs = [o for c in comms for o in c.out_shape]
    aliases, ai, ao = {}, n_in, n_out
    for c in comms:
        if c.aliased:
            aliases.update({ai + k: ao + k for k in range(len(c.args))})
        ai += len(c.args)
        ao += len(c.out_shape)
    sems = [pltpu.SemaphoreType.DMA((c.n_sems,)) for c in comms for _ in range(2)]
    steps = grid[0]

    def wrapped(*refs):
        o0 = n_in + len(c_args) + len(after)
        s0 = o0 + n_out + len(c_outs)
        m_in, m_out, m_sem = refs[n_in:n_in + len(c_args)], refs[o0 + n_out:s0], refs[s0 + n_scr:]

        def each(phase):
            ii = oi = 0
            for k, c in enumerate(comms):
                getattr(c, phase)(m_in[ii:ii + len(c.args)], m_out[oi:oi + len(c.out_shape)],
                                  m_sem[2 * k], m_sem[2 * k + 1])
                ii += len(c.args)
                oi += len(c.out_shape)

        if comms:
            @pl.when(pl.program_id(0) == 0)
            def _():
                each("start")

        body(*refs[:n_in], *refs[o0:o0 + n_out], *refs[s0:s0 + n_scr])

        if comms:
            @pl.when(pl.program_id(0) == steps - 1)
            def _():
                each("finish")

    res = pl.pallas_call(
        wrapped, name=name, grid=grid,
        in_specs=list(in_specs) + [ANY] * (len(c_args) + len(after)), out_specs=out_specs + [ANY] * len(c_outs),
        out_shape=out_shape + c_outs, scratch_shapes=list(scratch_shapes) + sems,
        input_output_aliases=aliases, compiler_params=_seq_params(),
    )(*args, *c_args, *after)
    mine = res[0] if single else list(res[:n_out])
    if not comms:
        return mine
    theirs, oi = [], n_out
    for c in comms:
        theirs.append(list(res[oi:oi + len(c.out_shape)]))
        oi += len(c.out_shape)
    return mine, theirs


def _in_pieces(g):
    lo, hi = 512 * g, 512 * (g + 1)
    return [(s, max(lo, IN_S * s) - IN_S * s, min(hi, IN_S * (s + 1)) - IN_S * s)
            for s in range(N_SHARD) if max(lo, IN_S * s) < min(hi, IN_S * (s + 1))]


def _inproj_fwd(x, pos, g_pre, w_in, lg, lb, w_sp, b_t, comms=()):
    def body(x_ref, pos_ref, g_ref, w_ref, lg_ref, lb_ref, wsp_ref, bt_ref, h_ref, u_ref, vs_ref, sgu_ref, *rest):
        qkv_refs, slabs = rest[:9], rest[9:]
        xh, _ = _rms_stats(x_ref[...])
        h = (xh * g_ref[...]).astype(BF16)
        h_ref[...] = h
        tabs = _rot_tables(pos_ref[...])

        def group(g):
            return jnp.concatenate([_dot(h, w_ref[s, :, a:b]) for s, a, b in _in_pieces(g)], axis=1)

        for t in range(3):
            val = group(t)
            if t < 2:
                val = _rope(val, tabs)
            if t == 0:
                val = val * np.float32(ATTN_SCALE)
            qkv_refs[t][...] = val.astype(BF16)
            for i, dil in enumerate(DILATIONS[1:]):
                _store_view(val, qkv_refs[3 * (i + 1) + t], slabs[t], dil)
        u = group(3)
        vs = group(4)
        u_ref[...] = u
        vs_ref[...] = vs
        bias_full = _dot_exact(bt_ref[...], _group_ones())
        gu, _, _, _, mixed, _, _ = _sgu_core(u, vs, lg_ref[...], lb_ref[...], _masked_spatial(wsp_ref), bias_full)
        sgu_ref[...] = gu * mixed

    return _pcall(
        body, name="inproj_sgu_fwd", grid=(SEQ // TM,),
        in_specs=[_rows(D_MODEL), _rows(1), _resident((1, D_MODEL)), _resident((N_SHARD, D_MODEL, IN_S)),
                  _resident((1, SGU_W)), _resident((1, SGU_W)), _resident((SGU_GROUPS, CHUNK, CHUNK)),
                  _resident((CHUNK, SGU_GROUPS))],
        out_specs=[_rows(D_MODEL), _rows(512), _rows(512), _rows(512)]
        + [_view_rows(dil) for dil in DILATIONS for _ in range(3)],
        out_shape=[_sds((SEQ, D_MODEL), BF16), _sds((SEQ, 512), F32), _sds((SEQ, 512), F32), _sds((SEQ, 512), F32)]
        + [_view_shape(dil, BF16) for dil in DILATIONS for _ in range(3)],
        scratch_shapes=[_slab_scratch() for _ in range(3)],
        args=(x, pos, g_pre, w_in, lg, lb, w_sp, b_t), comms=comms)


def _block_masks():
    row = lax.broadcasted_iota(jnp.int32, (CHUNK, CHUNK), 0)
    col = lax.broadcasted_iota(jnp.int32, (CHUNK, CHUNK), 1)
    return col <= row, col >= row


def _attn_fwd(qv, kv, vv, dil, comms=()):
    seg = SEQ // dil
    nblk = seg // CHUNK
    rps = 4 if nblk == 1 else 1

    def body(q_ref, k_ref, v_ref, o_ref, l_ref):
        left = _left_half()
        m_cur, m_prev = _block_masks()
        zero = jnp.zeros((CHUNK, CHUNK), BF16)
        ones = (jnp.where(left, 1.0, 0.0).astype(BF16), jnp.where(left, 0.0, 1.0).astype(BF16))

        sides = tuple(enumerate((left, ~left)))

        def rows(b):
            if isinstance(b, int):
                return b * CHUNK, max(b - 1, 0) * CHUNK
            return pl.multiple_of(b * CHUNK, CHUNK), pl.multiple_of(jnp.maximum(b - 1, 0) * CHUNK, CHUNK)

        def first(rr, b):
            r0, rp = rows(b)
            prev_ok = m_prev & (b > 0)
            tiles, scores = [], []
            for hp in range(4):
                ls = slice(rr * ATTN_W + hp * 128, rr * ATTN_W + (hp + 1) * 128)
                qp = q_ref[pl.ds(r0, CHUNK), ls]
                kc = k_ref[pl.ds(r0, CHUNK), ls]
                kp = k_ref[pl.ds(rp, CHUNK), ls] if nblk > 1 else None
                tiles.append((ls, v_ref[pl.ds(r0, CHUNK), ls], v_ref[pl.ds(rp, CHUNK), ls] if nblk > 1 else None))
                for _, hm in sides:
                    qh = jnp.where(hm, qp, zero)
                    sc = jnp.where(m_cur, _dot_nt(qh, kc), NEG)
                    sp = jnp.where(prev_ok, _dot_nt(qh, kp), NEG) if nblk > 1 else None
                    scores.append((sc, sp))
            return tiles, scores

        def second(scores):
            probs = []
            for sc, sp in scores:
                if nblk > 1:
                    m = jnp.max(jnp.maximum(sc, sp), axis=-1, keepdims=True)
                    pc = jnp.exp(sc - m)
                    pp = jnp.exp(sp - m)
                    probs.append((m, pc.astype(BF16), pp.astype(BF16), (pc + pp).astype(BF16)))
                else:
                    m = jnp.max(sc, axis=-1, keepdims=True)
                    pc = jnp.exp(sc - m).astype(BF16)
                    probs.append((m, pc, None, pc))
            return probs

        def third(rr, b, tiles, probs):
            r0, _ = rows(b)
            for hp, (ls, vc, vp) in enumerate(tiles):
                acc = jnp.zeros((CHUNK, CHUNK), F32)
                den = jnp.zeros((CHUNK, CHUNK), F32)
                for side, hm in sides:
                    _, pc, pp, psum = probs[2 * hp + side]
                    acc = acc + _dot(pc, jnp.where(hm, vc, zero))
                    if nblk > 1:
                        acc = acc + _dot(pp, jnp.where(hm, vp, zero))
                    den = den + _dot(psum, ones[side])
                o_ref[pl.ds(r0, CHUNK), ls] = (acc / den).astype(o_ref.dtype)
                lse = jnp.where(left, probs[2 * hp][0], probs[2 * hp + 1][0]) + jnp.log(den)
                l_ref[pl.ds(r0, CHUNK), rr * HEAD_W + 32 * hp:rr * HEAD_W + 32 * hp + 32] = lse[:, 48:80]

        def run(units):
            data = [first(rr, b) for rr, b in units]
            probs = [second(scores) for _, scores in data]
            for (rr, b), (tiles, _), pr in zip(units, data, probs):
                third(rr, b, tiles, pr)

        if nblk == 1:
            run([(rr, 0) for rr in range(rps)])
        else:
            def one(b, carry):
                run([(0, b)])
                return carry

            lax.fori_loop(0, nblk, one, 0)

    spec = pl.BlockSpec((seg, rps * ATTN_W), lambda r: (0, r))
    return _pcall(
        body, name=f"attn_fwd_d{dil}", grid=(dil // rps,),
        in_specs=[spec, spec, spec], out_specs=[spec, pl.BlockSpec((seg, rps * HEAD_W), lambda r: (0, r))],
        out_shape=[_sds((seg, dil * ATTN_W), BF16), _sds((seg, dil * HEAD_W), F32)],
        args=(qv, kv, vv), comms=comms)


def _lane_left(nrows):
    return lax.broadcasted_iota(jnp.int32, (nrows, CHUNK), 1) < HEAD_DIM


def _attn_rows(b):
    if isinstance(b, int):
        return b * CHUNK, max(b - 1, 0) * CHUNK
    return pl.multiple_of(b * CHUNK, CHUNK), pl.multiple_of(jnp.maximum(b - 1, 0) * CHUNK, CHUNK)


def _mix_out_fwd(o_list, l_list, sgu, x, w_out, g_attn, g_sgu, g_post, comms=()):
    def body(o1, o2, o3, l1, l2, l3, sgu_ref, x_ref, w_ref, ga_ref, gs_ref, gp_ref,
             attn_ref, mixed_ref, y_ref, x1_ref, lse1_ref, lse2_ref, lse3_ref, slabs_a, slabs_b):
        os = [o1[...], _load_view(o2, slabs_a, DILATIONS[1]), _load_view(o3, slabs_b, DILATIONS[2])]
        ls = [l1[...], _load_view(l2, slabs_a, DILATIONS[1], HEAD_W), _load_view(l3, slabs_b, DILATIONS[2], HEAD_W)]
        m = jnp.maximum(jnp.maximum(ls[0], ls[1]), ls[2])
        es = [jnp.exp(l - m) for l in ls]
        den = es[0] + es[1] + es[2]
        spread = _head_spread()
        attn = sum(_dot_select(e / den, spread) * o for e, o in zip(es, os))
        attn_ref[...] = attn
        lse = m + jnp.log(den)
        lse1_ref[...] = lse
        _store_view(lse, lse2_ref, slabs_a, DILATIONS[1])
        _store_view(lse, lse3_ref, slabs_b, DILATIONS[2])
        ah, _ = _rms_stats(attn)
        sh, _ = _rms_stats(sgu_ref[...])
        mixed = jnp.concatenate([ah * ga_ref[...], sh * gs_ref[...]], axis=1).astype(BF16)
        mixed_ref[...] = mixed
        y = _dot(mixed[:, 0:OUT_S], w_ref[0])
        for s in range(1, N_SHARD):
            y = y + _dot(mixed[:, s * OUT_S:(s + 1) * OUT_S], w_ref[s])
        y_ref[...] = y
        yh, _ = _rms_stats(y)
        x1_ref[...] = x_ref[...] + yh * gp_ref[...]

    return _pcall(
        body, name="mix_out_fwd", grid=(SEQ // TM,),
        in_specs=[_view_rows(dil) for dil in DILATIONS] + [_view_rows(dil, HEAD_W) for dil in DILATIONS]
        + [_rows(512), _rows(D_MODEL), _resident((N_SHARD, OUT_S, D_MODEL)),
           _resident((1, 512)), _resident((1, 512)), _resident((1, D_MODEL))],
        out_specs=[_rows(512), _rows(D_MODEL), _rows(D_MODEL), _rows(D_MODEL)]
        + [_view_rows(dil, HEAD_W) for dil in DILATIONS],
        out_shape=[_sds((SEQ, 512), F32), _sds((SEQ, D_MODEL), BF16), _sds((SEQ, D_MODEL), F32),
                   _sds((SEQ, D_MODEL), F32)] + [_view_shape(dil, F32, HEAD_W) for dil in DILATIONS],
        scratch_shapes=[_slab_scratch(), _slab_scratch()],
        args=(*o_list, *l_list, sgu, x, w_out, g_attn, g_sgu, g_post), comms=comms)


def _ffn_fwd_bwd(x1, target, w_gate, w_up, w_down, g_pre, g_post, comms=()):
    def body(x1_ref, t_ref, wg_ref, wu_ref, wd_ref, gpf_ref, gpo_ref,
             h2_ref, a_ref, dg_ref, dup_ref, df_ref, dx1_ref, loss_ref, dgpf_ref, dgpo_ref, g_scr, up_scr):
        @pl.when(pl.program_id(0) == 0)
        def _():
            loss_ref[...] = jnp.zeros_like(loss_ref)
            dgpf_ref[...] = jnp.zeros_like(dgpf_ref)
            dgpo_ref[...] = jnp.zeros_like(dgpo_ref)

        x1 = x1_ref[...]
        gpf = gpf_ref[...]
        gpo = gpo_ref[...]
        xh, r = _rms_stats(x1)
        h2 = (xh * gpf).astype(BF16)
        h2_ref[...] = h2
        f = jnp.zeros((TM_FFN, D_MODEL), F32)
        for c0, c1 in FF_CHUNKS:
            g = _dot_nt(h2, wg_ref[c0:c1, :])
            up = _dot_nt(h2, wu_ref[c0:c1, :])
            g_scr[:, c0:c1] = g
            up_scr[:, c0:c1] = up
            a = (g * _sigmoid(g) * up).astype(BF16)
            a_ref[:, c0:c1] = a
            f = f + _dot(a, wd_ref[c0:c1, :])
        fh, rf = _rms_stats(f)
        diff = x1 + fh * gpo - t_ref[...]
        loss_ref[...] += jnp.sum(diff * diff, axis=0, keepdims=True)
        dout = diff * np.float32(1.0 / D_MODEL)
        df, dgpo = _rms_bwd(fh, rf, gpo, dout)
        dgpo_ref[...] += dgpo
        dfb = df.astype(BF16)
        df_ref[...] = dfb
        dh2 = jnp.zeros((TM_FFN, D_MODEL), F32)
        for c0, c1 in FF_CHUNKS:
            da = _dot_nt(dfb, wd_ref[c0:c1, :])
            g = g_scr[:, c0:c1]
            up = up_scr[:, c0:c1]
            sg = _sigmoid(g)
            dup = (da * (g * sg)).astype(BF16)
            dg = (da * up * (sg * (1.0 + g * (1.0 - sg)))).astype(BF16)
            dg_ref[:, c0:c1] = dg
            dup_ref[:, c0:c1] = dup
            dh2 = dh2 + _dot(dg, wg_ref[c0:c1, :]) + _dot(dup, wu_ref[c0:c1, :])
        dx, dgpf = _rms_bwd(xh, r, gpf, dh2)
        dgpf_ref[...] += dgpf
        dx1_ref[...] = dout + dx

    return _pcall(
        body, name="ffn_fwd_bwd", grid=(SEQ // TM_FFN,),
        in_specs=[_rows(D_MODEL, TM_FFN), _rows(D_MODEL, TM_FFN), _resident((FF, D_MODEL)),
                  _resident((FF, D_MODEL)), _resident((FF, D_MODEL)),
                  _resident((1, D_MODEL)), _resident((1, D_MODEL))],
        out_specs=[_rows(D_MODEL, TM_FFN), _rows(FF, TM_FFN), _rows(FF, TM_FFN), _rows(FF, TM_FFN),
                   _rows(D_MODEL, TM_FFN), _rows(D_MODEL, TM_FFN), _acc(D_MODEL), _acc(D_MODEL), _acc(D_MODEL)],
        out_shape=[_sds((SEQ, D_MODEL), BF16), _sds((SEQ, FF), BF16), _sds((SEQ, FF), BF16),
                   _sds((SEQ, FF), BF16), _sds((SEQ, D_MODEL), BF16), _sds((SEQ, D_MODEL), F32),
                   _sds((1, D_MODEL), F32), _sds((1, D_MODEL), F32), _sds((1, D_MODEL), F32)],
        scratch_shapes=[pltpu.VMEM((TM_FFN, FF), F32), pltpu.VMEM((TM_FFN, FF), F32)],
        args=(x1, target, w_gate, w_up, w_down, g_pre, g_post), comms=comms)


def _wgrad(a, b, a_spec, b_spec, out_block, name, comms=()):
    def body(a_ref, b_ref, o_ref, ob_ref):
        av = a_ref[0] if len(a_ref.shape) == 3 else a_ref[...]
        bv = b_ref[0] if len(b_ref.shape) == 3 else b_ref[...]
        g = _dot_tn(av, bv)
        o_ref[0] = g
        ob_ref[0] = g.astype(BF16)

    return _pcall(
        body, name=name, grid=(N_SHARD,),
        in_specs=[a_spec, b_spec],
        out_specs=[pl.BlockSpec((1,) + out_block, lambda s: (s, 0, 0))] * 2,
        out_shape=[_sds((N_SHARD,) + out_block, F32), _sds((N_SHARD,) + out_block, BF16)],
        args=(a, b), comms=comms)


def _outproj_bwd(dx1, y, attn, sgu, w_out, g_post, g_attn, g_sgu, u, vs, lg, lb, w_sp, b_t, comms=(), after=()):
    sgu_bwd = _sgu_bwd_body()

    def body(dx1_ref, y_ref, attn_ref, sgu_ref, w_ref, gp_ref, ga_ref, gs_ref, u_ref, vs_ref, lg_ref, lb_ref,
             wsp_ref, bt_ref, dy_ref, du_ref, dvs_ref, dgp_ref, dga_ref, dgs_ref, dw_ref, db_ref, dlg_ref, dlb_ref,
             *rest):
        dattn_refs, delta_refs, (slabs_a, slabs_b, dsgu_ref, dbias_scr) = rest[0:3], rest[3:6], rest[6:]

        @pl.when(pl.program_id(0) == 0)
        def _():
            dgp_ref[...] = jnp.zeros_like(dgp_ref)
            dga_ref[...] = jnp.zeros_like(dga_ref)
            dgs_ref[...] = jnp.zeros_like(dgs_ref)

        yh, ry = _rms_stats(y_ref[...])
        dy, dgp = _rms_bwd(yh, ry, gp_ref[...], dx1_ref[...])
        dgp_ref[...] += dgp
        dyb = dy.astype(BF16)
        dy_ref[...] = dyb
        dmixed = jnp.concatenate([_dot_nt(dyb, w_ref[s]) for s in range(N_SHARD)], axis=1)
        attn = attn_ref[...]
        ah, ra = _rms_stats(attn)
        dattn, dga = _rms_bwd(ah, ra, ga_ref[...], dmixed[:, 0:512])
        dga_ref[...] += dga
        sh, rs = _rms_stats(sgu_ref[...])
        dsgu, dgs = _rms_bwd(sh, rs, gs_ref[...], dmixed[:, 512:1024])
        dgs_ref[...] += dgs
        dsgu_ref[...] = dsgu
        delta = _dot_select(dattn * attn, _head_sum())
        dattn_refs[0][...] = dattn.astype(BF16)
        delta_refs[0][...] = delta
        for i, dil in enumerate(DILATIONS[1:]):
            _store_view(dattn, dattn_refs[i + 1], slabs_a, dil)
            _store_view(delta, delta_refs[i + 1], slabs_b, dil)
        sgu_bwd(u_ref, vs_ref, dsgu_ref, lg_ref, lb_ref, wsp_ref, bt_ref,
                du_ref, dvs_ref, dw_ref, db_ref, dlg_ref, dlb_ref, dbias_scr)

    return _pcall(
        body, name="outproj_sgu_bwd", grid=(SEQ // TM,),
        in_specs=[_rows(D_MODEL), _rows(D_MODEL), _rows(512), _rows(512), _resident((N_SHARD, OUT_S, D_MODEL)),
                  _resident((1, D_MODEL)), _resident((1, 512)), _resident((1, 512)),
                  _rows(SGU_W), _rows(SGU_W), _resident((1, SGU_W)), _resident((1, SGU_W)),
                  _resident((SGU_GROUPS, CHUNK, CHUNK)), _resident((CHUNK, SGU_GROUPS))],
        out_specs=[_rows(D_MODEL), _rows(SGU_W), _rows(SGU_W), _acc(D_MODEL), _acc(512), _acc(512),
                   pl.BlockSpec((SGU_GROUPS, CHUNK, CHUNK), lambda i: (0, 0, 0)), _acc(CHUNK, SGU_GROUPS),
                   _acc(SGU_W), _acc(SGU_W)]
        + [_view_rows(dil) for dil in DILATIONS] + [_view_rows(dil, HEAD_W) for dil in DILATIONS],
        out_shape=[_sds((SEQ, D_MODEL), BF16), _sds((SEQ, SGU_W), BF16), _sds((SEQ, SGU_W), BF16),
                   _sds((1, D_MODEL), F32), _sds((1, 512), F32), _sds((1, 512), F32),
                   _sds((SGU_GROUPS, CHUNK, CHUNK), F32), _sds((SGU_GROUPS, CHUNK), F32),
                   _sds((1, SGU_W), F32), _sds((1, SGU_W), F32)]
        + [_view_shape(dil, BF16) for dil in DILATIONS] + [_view_shape(dil, F32, HEAD_W) for dil in DILATIONS],
        scratch_shapes=[_slab_scratch(), _slab_scratch(), pltpu.VMEM((TM, SGU_W), F32),
                        pltpu.VMEM((CHUNK, SGU_W), F32)],
        args=(dx1, y, attn, sgu, w_out, g_post, g_attn, g_sgu, u, vs, lg, lb, w_sp, b_t), comms=comms, after=after)


def _sgu_bwd_body():
    nsteps = SEQ // TM

    def body(u_ref, vs_ref, ds_ref, lg_ref, lb_ref, w_ref, bt_ref,
             du_ref, dvs_ref, dw_ref, db_ref, dlg_ref, dlb_ref, dbias_scr):
        i = pl.program_id(0)

        @pl.when(i == 0)
        def _():
            dw_ref[...] = jnp.zeros_like(dw_ref)
            dlg_ref[...] = jnp.zeros_like(dlg_ref)
            dlb_ref[...] = jnp.zeros_like(dlb_ref)
            dbias_scr[...] = jnp.zeros_like(dbias_scr)

        wm = _masked_spatial(w_ref)
        ones_g = _group_ones()
        bias_full = _dot_exact(bt_ref[...], ones_g)
        u = u_ref[...]
        vs = vs_ref[...]
        lg = lg_ref[...]
        gu, xh, rstd, vnb, mixed, cdf_u, cdf_vs = _sgu_core(u, vs, lg, lb_ref[...], wm, bias_full)
        dsgu = ds_ref[...]
        du_ref[...] = (dsgu * mixed * _gelu_grad(u, cdf_u)).astype(BF16)
        dmixed = dsgu * gu
        left = _left_half()
        dvn_rows = []
        for c in range(TM // CHUNK):
            rs = slice(c * CHUNK, (c + 1) * CHUNK)
            dm_c = dmixed[rs, :]
            dbias_scr[...] += dm_c
            pieces = []
            for p in range(4):
                ls = slice(p * 128, (p + 1) * 128)
                dmp = dm_c[:, ls]
                vp = vnb[rs, ls]
                dmb = dmp.astype(BF16)
                zero = jnp.zeros_like(dmb)
                dw_ref[2 * p] += _dot_nt(jnp.where(left, dmb, zero), vp)
                dw_ref[2 * p + 1] += _dot_nt(jnp.where(left, zero, dmb), vp)
                pieces.append(jnp.where(left, _dot_tn(wm[2 * p], dmb), _dot_tn(wm[2 * p + 1], dmb)))
            dvn_rows.append(jnp.concatenate(pieces, axis=1))
        dvn = jnp.concatenate(dvn_rows, axis=0)
        dlg_ref[...] += jnp.sum(dvn * xh, axis=0, keepdims=True)
        dlb_ref[...] += jnp.sum(dvn, axis=0, keepdims=True)
        dxh = dvn * lg
        dgv = rstd * (dxh - jnp.mean(dxh, axis=-1, keepdims=True) - xh * jnp.mean(dxh * xh, axis=-1, keepdims=True))
        dvs_ref[...] = (dgv * _gelu_grad(vs, cdf_vs)).astype(BF16)

        @pl.when(i == nsteps - 1)
        def _():
            row = lax.broadcasted_iota(jnp.int32, (CHUNK, CHUNK), 0)
            col = lax.broadcasted_iota(jnp.int32, (CHUNK, CHUNK), 1)
            for g in range(SGU_GROUPS):
                dw_ref[g] = jnp.where(col <= row, dw_ref[g], 0.0)
            db_ref[...] = lax.dot_general(ones_g, dbias_scr[...], (((1,), (1,)), ((), ())),
                                          preferred_element_type=F32, precision=lax.Precision.HIGHEST)

    return body


def _attn_bwd(qv, kv, vv, dov, deltav, lsev, dil, comms=(), after=()):
    seg = SEQ // dil
    nblk = seg // CHUNK
    rps = 4 if nblk == 1 else 1

    def body(q_ref, k_ref, v_ref, do_ref, dl_ref, lse_ref, dq_ref, dk_ref, dv_ref, dk_wait, dv_wait):
        left = _left_half()
        m_cur, m_prev = _block_masks()
        sides = tuple(enumerate((left, ~left)))
        zero = jnp.zeros((CHUNK, CHUNK), BF16)

        def first(rr, b, both):
            r0, rp = _attn_rows(b)
            keys = pl.ds(rp, 2 * CHUNK) if both else pl.ds(r0, CHUNK)
            tiles, heads = [], []
            for hp in range(4):
                ls = slice(rr * ATTN_W + hp * 128, rr * ATTN_W + (hp + 1) * 128)
                qp = q_ref[pl.ds(r0, CHUNK), ls]
                dop = do_ref[pl.ds(r0, CHUNK), ls]
                k2 = k_ref[keys, ls]
                v2 = v_ref[keys, ls]
                tiles.append((ls, k2))
                for _, hm in sides:
                    qh = jnp.where(hm, qp, zero)
                    doh = jnp.where(hm, dop, zero)
                    heads.append((qh, doh, _dot_nt(k2, qh), _dot_nt(v2, doh)))
            return tiles, heads

        def second(rr, b, heads, both):
            r0, _ = _attn_rows(b)
            ok = jnp.concatenate([m_cur, m_prev], axis=0) if both else m_prev
            lanes = slice(rr * HEAD_W, (rr + 1) * HEAD_W)
            lse_t = lse_ref[pl.ds(r0, CHUNK), lanes].T
            dl_t = dl_ref[pl.ds(r0, CHUNK), lanes].T
            out = []
            for i, (_, _, s_t, dp_t) in enumerate(heads):
                p = jnp.exp(jnp.where(ok, s_t - lse_t[16 * i:16 * i + 1, :], NEG))
                out.append((p.astype(BF16), (p * (dp_t - dl_t[16 * i:16 * i + 1, :])).astype(BF16)))
            return out

        def third(rr, b, tiles, heads, probs, both):
            r0, rp = _attn_rows(b)
            nk = 2 * CHUNK if both else CHUNK
            left_k = _lane_left(nk)
            zero_k = jnp.zeros((nk, CHUNK), BF16)
            for hp, (ls, k2) in enumerate(tiles):
                dq = jnp.zeros((CHUNK, CHUNK), F32)
                dk2 = jnp.zeros((nk, CHUNK), F32)
                dv2 = jnp.zeros((nk, CHUNK), F32)
                for side in range(2):
                    qh, doh, _, _ = heads[2 * hp + side]
                    p, ds = probs[2 * hp + side]
                    dq = dq + _dot_tn(ds, jnp.where(left_k if side == 0 else ~left_k, k2, zero_k))
                    dk2 = dk2 + _dot(ds, qh)
                    dv2 = dv2 + _dot(p, doh)
                dq_ref[pl.ds(r0, CHUNK), ls] = dq.astype(dq_ref.dtype)
                if nblk == 1:
                    dk_ref[pl.ds(r0, CHUNK), ls] = dk2.astype(dk_ref.dtype)
                    dv_ref[pl.ds(r0, CHUNK), ls] = dv2.astype(dv_ref.dtype)
                elif both:
                    dk_ref[pl.ds(rp, CHUNK), ls] = (dk_wait[:, ls] + dk2[0:CHUNK]).astype(dk_ref.dtype)
                    dv_ref[pl.ds(rp, CHUNK), ls] = (dv_wait[:, ls] + dv2[0:CHUNK]).astype(dv_ref.dtype)
                    dk_wait[:, ls] = dk2[CHUNK:]
                    dv_wait[:, ls] = dv2[CHUNK:]
                else:
                    dk_wait[:, ls] = dk2
                    dv_wait[:, ls] = dv2

        def run(units, both):
            data = [first(rr, b, both) for rr, b in units]
            probs = [second(rr, b, heads, both) for (rr, b), (_, heads) in zip(units, data)]
            for (rr, b), (tiles, heads), pr in zip(units, data, probs):
                third(rr, b, tiles, heads, pr, both)

        run([(rr, 0) for rr in range(rps)], False)
        if nblk > 1:
            def one(b, carry):
                run([(0, b)], True)
                return carry

            lax.fori_loop(1, nblk, one, 0)
            last = (nblk - 1) * CHUNK
            dk_ref[last:last + CHUNK, :] = dk_wait[...].astype(dk_ref.dtype)
            dv_ref[last:last + CHUNK, :] = dv_wait[...].astype(dv_ref.dtype)

    spec = pl.BlockSpec((seg, rps * ATTN_W), lambda r: (0, r))
    return _pcall(
        body, name=f"attn_bwd_d{dil}", grid=(dil // rps,),
        in_specs=[spec] * 4 + [pl.BlockSpec((seg, rps * HEAD_W), lambda r: (0, r))] * 2, out_specs=[spec] * 3,
        out_shape=[_sds((seg, dil * ATTN_W), BF16)] * 3,
        scratch_shapes=[pltpu.VMEM((CHUNK, ATTN_W), F32), pltpu.VMEM((CHUNK, ATTN_W), F32)],
        args=(qv, kv, vv, dov, deltav, lsev), comms=comms, after=after)


def _inproj_bwd(dqs, dks, dvs, du, dvs_sgu, pos, x, dx1, w_in, g_pre, comms=()):
    def body(dq1, dq2, dq3, dk1, dk2, dk3, dv1, dv2, dv3, du_ref, dvs_ref, pos_ref, x_ref, dx1_ref, w_ref, g_ref,
             dproj_ref, gx_ref, dg_ref, slabs_a, slabs_b):
        @pl.when(pl.program_id(0) == 0)
        def _():
            dg_ref[...] = jnp.zeros_like(dg_ref)

        def total(r1, r2, r3):
            return r1[...] + _load_view(r2, slabs_a, DILATIONS[1]) + _load_view(r3, slabs_b, DILATIONS[2])

        tabs = _rot_tables(pos_ref[...])
        groups = {3: du_ref[...], 4: dvs_ref[...]}
        dh = jnp.zeros((TM, D_MODEL), F32)
        for g in (3, 4, 0, 1, 2):
            if g == 0:
                groups[g] = _rope_bwd(total(dq1, dq2, dq3) * np.float32(ATTN_SCALE), tabs).astype(BF16)
            elif g == 1:
                groups[g] = _rope_bwd(total(dk1, dk2, dk3), tabs).astype(BF16)
            elif g == 2:
                groups[g] = total(dv1, dv2, dv3).astype(BF16)
            dproj_ref[:, 512 * g:512 * (g + 1)] = groups[g]
            off = 0
            for s, a, b in _in_pieces(g):
                dh = dh + _dot_nt(groups[g][:, off:off + b - a], w_ref[s, :, a:b])
                off += b - a
        g = g_ref[...]
        xh, r = _rms_stats(x_ref[...])
        dx, dg = _rms_bwd(xh, r, g, dh)
        dg_ref[...] += dg
        gx_ref[...] = dx1_ref[...] + dx

    return _pcall(
        body, name="inproj_bwd", grid=(SEQ // TM,),
        in_specs=[_view_rows(dil) for dil in DILATIONS] * 3
        + [_rows(512), _rows(512), _rows(1), _rows(D_MODEL), _rows(D_MODEL),
           _resident((N_SHARD, D_MODEL, IN_S)), _resident((1, D_MODEL))],
        out_specs=[_rows(PROJ_W), _rows(D_MODEL), _acc(D_MODEL)],
        out_shape=[_sds((SEQ, PROJ_W), BF16), _sds((SEQ, D_MODEL), F32), _sds((1, D_MODEL), F32)],
        scratch_shapes=[_slab_scratch(), _slab_scratch()],
        args=(*dqs, *dks, *dvs, du, dvs_sgu, pos, x, dx1, w_in, g_pre), comms=comms)


def _coords():
    return lax.axis_index("x"), lax.axis_index("y"), lax.axis_index("c")


def _other_chips(x, y):
    return [(1 - x, y), (x, 1 - y), (1 - x, 1 - y)]


WEIGHTS = ("pre_mix_norm", "w_in", "sgu_ln_gain", "sgu_ln_bias", "sgu_w_spatial", "sgu_b_spatial", "attn_out_norm",
           "sgu_out_norm", "w_out", "post_mix_norm", "pre_ffn_norm", "w_gate", "w_up", "w_down", "post_ffn_norm")
SMALL = ("pre_mix_norm", "post_mix_norm", "pre_ffn_norm", "post_ffn_norm", "sgu_ln_gain", "sgu_ln_bias",
         "attn_out_norm", "sgu_out_norm", "sgu_w_spatial", "sgu_b_spatial")


def _remote(src, dst, send_sem, recv_sem, to):
    return pltpu.make_async_remote_copy(src_ref=src, dst_ref=dst, send_sem=send_sem, recv_sem=recv_sem,
                                        device_id=to, device_id_type=MESH)


def _halves(a):
    *lead, rows, cols = a.shape
    return a.reshape(*lead, 2, rows // 2, cols)


def _gather_ici(shards):
    n = len(shards)

    def desc(ins, outs, ss, rs, j, w, landed):
        x, y, c = _coords()
        cx, cy = _other_chips(x, y)[j]
        shard = 2 * cx + cy if landed else 2 * x + y
        return _remote(ins[w].at[c], outs[w].at[shard, c], ss.at[j * n + w], rs.at[j * n + w], (cx, cy, c))

    def start(ins, outs, ss, rs):
        for j in range(3):
            for w in range(n):
                desc(ins, outs, ss, rs, j, w, False).start()

    def finish(ins, outs, ss, rs):
        for j in range(3):
            for w in range(n):
                desc(ins, outs, ss, rs, j, w, True).wait_recv()
                desc(ins, outs, ss, rs, j, w, False).wait_send()

    return _Comm(shards, [_sds((N_SHARD,) + s.shape, s.dtype) for s in shards], 3 * n, start, finish)


def _gather_pass(fulls):
    n = len(fulls)

    def desc(bufs, ss, rs, j, w, landed):
        x, y, c = _coords()
        cx, cy = _other_chips(x, y)[j]
        shard = 2 * cx + cy
        return _remote(bufs[w].at[shard, c], bufs[w].at[shard, 1 - c if landed else c],
                       ss.at[j * n + w], rs.at[j * n + w], (x, y, 1 - c))

    def start(ins, outs, ss, rs):
        for j in range(3):
            for w in range(n):
                desc(outs, ss, rs, j, w, False).start()

    def finish(ins, outs, ss, rs):
        for j in range(3):
            for w in range(n):
                desc(outs, ss, rs, j, w, True).wait_recv()
                desc(outs, ss, rs, j, w, False).wait_send()

    return _Comm(fulls, [_sds(f.shape, f.dtype) for f in fulls], 3 * n, start, finish, aliased=True)


def _rs_sibling(gws):
    n = len(gws)

    def desc(ins, outs, ss, rs, w):
        x, y, c = _coords()
        return _remote(ins[w].at[:, 1 - c], outs[w], ss.at[w], rs.at[w], (x, y, 1 - c))

    def start(ins, outs, ss, rs):
        for w in range(n):
            desc(ins, outs, ss, rs, w).start()

    def finish(ins, outs, ss, rs):
        for w in range(n):
            desc(ins, outs, ss, rs, w).wait()

    out_shape = [_sds((N_SHARD, g.shape[1] // 2, g.shape[2]), g.dtype) for g in gws]
    return _Comm([_halves(g) for g in gws], out_shape, n, start, finish)


def _rs_chips(pbs):
    n = len(pbs)

    def desc(ins, outs, ss, rs, j, w):
        x, y, c = _coords()
        cx, cy = _other_chips(x, y)[j]
        return _remote(ins[w].at[2 * cx + cy], outs[w].at[j], ss.at[j * n + w], rs.at[j * n + w], (cx, cy, c))

    def start(ins, outs, ss, rs):
        for j in range(3):
            for w in range(n):
                desc(ins, outs, ss, rs, j, w).start()

    def finish(ins, outs, ss, rs):
        for j in range(3):
            for w in range(n):
                desc(ins, outs, ss, rs, j, w).wait()

    return _Comm(pbs, [_sds((3,) + p.shape[1:], p.dtype) for p in pbs], 3 * n, start, finish)


def _rs_join(halves):
    n = len(halves)

    def desc(bufs, ss, rs, w, landed):
        x, y, c = _coords()
        return _remote(bufs[w].at[c], bufs[w].at[1 - c if landed else c], ss.at[w], rs.at[w], (x, y, 1 - c))

    def start(ins, outs, ss, rs):
        for w in range(n):
            desc(outs, ss, rs, w, False).start()

    def finish(ins, outs, ss, rs):
        for w in range(n):
            desc(outs, ss, rs, w, True).wait_recv()
            desc(outs, ss, rs, w, False).wait_send()

    return _Comm(halves, [_sds(h.shape, h.dtype) for h in halves], n, start, finish, aliased=True)


def _small_exchange(buf):
    def desc(ins, outs, ss, rs, k, landed):
        x, y, c = _coords()
        px = 1 - x if (k >> 2) & 1 else x
        py = 1 - y if (k >> 1) & 1 else y
        pc = 1 - c if k & 1 else c
        slot = 4 * px + 2 * py + pc if landed else 4 * x + 2 * y + c
        return _remote(ins[0], outs[0].at[slot], ss.at[k - 1], rs.at[k - 1], (px, py, pc))

    def start(ins, outs, ss, rs):
        for k in range(1, 8):
            desc(ins, outs, ss, rs, k, False).start()

    def finish(ins, outs, ss, rs):
        for k in range(1, 8):
            desc(ins, outs, ss, rs, k, True).wait_recv()
            desc(ins, outs, ss, rs, k, False).wait_send()

    return _Comm([buf], [_sds((8,) + buf.shape, buf.dtype)], 7, start, finish)


HBM = pl.BlockSpec(memory_space=pltpu.HBM)
SEM = pl.BlockSpec(memory_space=pltpu.SEMAPHORE)
DATAFLOW = pltpu.SideEffectType.DATAFLOW_SIDE_EFFECTING


def _split_starts(name, comms, after):
    srcs = [[pltpu.with_memory_space_constraint(s, pltpu.HBM) for s in c.args] for c in comms]
    lands = [[pltpu.with_memory_space_constraint(lax.empty(o.shape, o.dtype), pltpu.HBM) for o in c.out_shape]
             for c in comms]
    bufs = [b for k in range(len(comms)) for b in srcs[k] + lands[k]]
    nb, nc = len(bufs), len(comms)

    def body(*refs):
        sems = refs[nb + 1:nb + 1 + 2 * nc]
        off = 0
        for k, c in enumerate(comms):
            ns, nl = len(srcs[k]), len(lands[k])
            c.start(refs[off:off + ns], refs[off + ns:off + ns + nl], sems[2 * k], sems[2 * k + 1])
            off += ns + nl
        refs[-1][...] = jnp.zeros_like(refs[-1])

    res = pl.pallas_call(
        body, name=name,
        out_shape=(*[pltpu.SemaphoreType.DMA((c.n_sems,)) for c in comms for _ in range(2)],
                   *[pltpu.HBM(b.shape, b.dtype) for b in bufs], _sds((8, 128), F32)),
        in_specs=[HBM] * nb + [ANY],
        out_specs=(*[SEM] * (2 * nc), *[HBM] * nb, pl.BlockSpec(memory_space=pltpu.VMEM)),
        input_output_aliases={i: 2 * nc + i for i in range(nb)},
        compiler_params=pltpu.CompilerParams(has_side_effects=DATAFLOW),
    )(*bufs, after)
    states, off = [], 2 * nc
    for k in range(nc):
        ns, nl = len(srcs[k]), len(lands[k])
        states.append((res[2 * k], res[2 * k + 1], list(res[off:off + ns]), list(res[off + ns:off + ns + nl])))
        off += ns + nl
    return states, res[-1]


def _split_wait(name, comm, send_sems, recv_sems, srcs, lands, after):
    ns, nb = len(srcs), len(lands)
    after = list(after) if isinstance(after, (list, tuple)) else [after]

    def body(*refs):
        comm.finish(refs[:ns], refs[ns:ns + nb], refs[ns + nb], refs[ns + nb + 1])

    res = pl.pallas_call(
        body, name=name,
        out_shape=tuple(pltpu.HBM(b.shape, b.dtype) for b in srcs + lands),
        in_specs=[HBM] * (ns + nb) + [SEM, SEM] + [ANY] * len(after), out_specs=tuple([HBM] * (ns + nb)),
        input_output_aliases={i: i for i in range(ns + nb)},
        compiler_params=pltpu.CompilerParams(has_side_effects=DATAFLOW),
    )(*srcs, *lands, send_sems, recv_sems, *after)
    return list(res[ns:])


LOSS_ROW = "loss_cols"
SMALL_EARLY = ("post_mix_norm", "pre_ffn_norm", "post_ffn_norm", "sgu_ln_gain", "sgu_ln_bias", "attn_out_norm",
               "sgu_out_norm", "sgu_b_spatial", LOSS_ROW)
SMALL_LATE = ("pre_mix_norm",)
SMALL_WSP = "sgu_w_spatial"


def _pack(d, names, rows):
    flat = [d[n].reshape(-1) for n in names]
    used = sum(f.shape[0] for f in flat)
    flat.append(jnp.zeros((rows * 1024 - used,), F32))
    return jnp.concatenate(flat).reshape(rows, 1024)


def _unpack(buf, names, shapes):
    flat = buf.reshape(-1)
    out, off = {}, 0
    for n in names:
        size = int(np.prod(shapes[n]))
        out[n] = flat[off:off + size].reshape(shapes[n])
        off += size
    return out


def _chip_sums(gbs, recvs, shard_core):
    n = len(gbs)
    _, rows, cols = gbs[0].shape
    hw = rows // 2

    def body(sc_ref, *refs):
        for k in range(n):
            refs[2 * n + k][...] = (refs[k][...].astype(F32) + refs[n + k][...].astype(F32)).astype(BF16)

    def other(s, sc):
        return jnp.where(s >= sc[0], s + 1, s)

    mine = pl.BlockSpec((1, hw, cols), lambda s, sc: (other(s, sc), sc[1], 0))
    plain = pl.BlockSpec((1, hw, cols), lambda s, sc: (other(s, sc), 0, 0))
    return pl.pallas_call(
        body, name="rs_chip_sums",
        grid_spec=pltpu.PrefetchScalarGridSpec(num_scalar_prefetch=1, grid=(N_SHARD - 1,),
                                               in_specs=[mine] * n + [plain] * n, out_specs=[plain] * n),
        out_shape=[_sds((N_SHARD, hw, cols), BF16)] * n,
        compiler_params=_seq_params(),
    )(shard_core, *gbs, *recvs)


def _final_sums(gws, recv_sibs, recv_chips, shard_core):
    n = len(gws)
    halves = [(g.shape[1] // 2, g.shape[2]) for g in gws]

    def body(sc_ref, *refs):
        for k in range(n):
            acc = refs[k][0] + refs[n + k][0]
            for j in range(3):
                acc = acc + refs[2 * n + k][j].astype(F32)
            refs[3 * n + k][0] = acc

    def specs(lead, index):
        return [pl.BlockSpec((lead, hw, cols), index) for hw, cols in halves]

    return pl.pallas_call(
        body, name="rs_final_sums",
        grid_spec=pltpu.PrefetchScalarGridSpec(
            num_scalar_prefetch=1, grid=(1,),
            in_specs=specs(1, lambda i, sc: (sc[0], sc[1], 0)) + specs(1, lambda i, sc: (sc[0], 0, 0))
            + specs(3, lambda i, sc: (0, 0, 0)),
            out_specs=specs(1, lambda i, sc: (sc[1], 0, 0))),
        out_shape=[_sds((2, hw, cols), F32) for hw, cols in halves],
        compiler_params=_seq_params(),
    )(shard_core, *gws, *recv_sibs, *recv_chips)


ADAM_BLOCKS = 4


def _adamw_multi(ws, gs, ms, vs, name, after=()):
    n = len(ws)

    def body(*refs):
        outs = refs[4 * n + len(after):]
        for k in range(n):
            g = refs[n + k][...]
            d, m, v = _adam_math(refs[k][...], g, refs[2 * n + k][...], refs[3 * n + k][...])
            outs[4 * k][...], outs[4 * k + 1][...], outs[4 * k + 2][...], outs[4 * k + 3][...] = g, d, m, v

    specs = [pl.BlockSpec((w.shape[0] // ADAM_BLOCKS, w.shape[1]), lambda i: (i, 0)) for w in ws]
    res = pl.pallas_call(
        body, name=name, grid=(ADAM_BLOCKS,), in_specs=specs * 4 + [ANY] * len(after),
        out_specs=[s for s in specs for _ in range(4)],
        out_shape=[_sds(w.shape, F32) for w in ws for _ in range(4)],
        compiler_params=_seq_params(),
    )(*ws, *gs, *ms, *vs, *after)
    return [tuple(res[4 * k:4 * k + 4]) for k in range(n)]


def _wgrad_ff(a_list, b, name):
    n = len(a_list)
    cols = 256

    def body(*refs):
        bv = refs[n][...]
        for k in range(n):
            g = _dot_tn(refs[k][...], bv)
            refs[n + 1 + k][...] = g
            refs[2 * n + 1 + k][...] = g.astype(BF16)

    res = _pcall(
        body, name=name, grid=(FF // cols,),
        in_specs=[pl.BlockSpec((SEQ, cols), lambda j: (0, j))] * n
        + [pl.BlockSpec((SEQ, D_MODEL), lambda j: (0, 0), pipeline_mode=pl.Buffered(1))],
        out_specs=[pl.BlockSpec((cols, D_MODEL), lambda j: (j, 0))] * (2 * n),
        out_shape=[_sds((FF, D_MODEL), F32)] * n + [_sds((FF, D_MODEL), BF16)] * n, args=(*a_list, b))
    return [m.reshape(N_SHARD, FF_S, D_MODEL) for m in res]


def _comm_only(name, comms):
    return _pcall(lambda: None, name=name, grid=(1,), in_specs=[], out_specs=[], out_shape=[], args=(),
                  comms=comms)[1]


def _adam_math(w, g, m, v):
    m = ADAM_B1 * m + (1.0 - ADAM_B1) * g
    v = ADAM_B2 * v + (1.0 - ADAM_B2) * (g * g)
    m_hat = m / (1.0 - ADAM_B1 ** ADAM_STEP)
    v_hat = v / (1.0 - ADAM_B2 ** ADAM_STEP)
    return -ADAM_LR * (m_hat / (jnp.sqrt(v_hat) + ADAM_EPS) + ADAM_WD * w), m, v


def _adamw_small(own, slots, w, m, v, me):
    rows, cols = own.shape

    def body(me_ref, own_ref, slots_ref, w_ref, m_ref, v_ref, g_ref, d_ref, nm_ref, nv_ref):
        own_v = own_ref[...].astype(F32)
        g = jnp.where(me_ref[0] == 0, own_v, slots_ref[0].astype(F32))
        for i in range(1, 8):
            g = g + jnp.where(me_ref[0] == i, own_v, slots_ref[i].astype(F32))
        g_ref[...] = g
        d_ref[...], nm_ref[...], nv_ref[...] = _adam_math(w_ref[...], g, m_ref[...], v_ref[...])

    flat = pl.BlockSpec((rows, cols), lambda i, me_ref: (0, 0))
    return pl.pallas_call(
        body, name="adamw_small",
        grid_spec=pltpu.PrefetchScalarGridSpec(
            num_scalar_prefetch=1, grid=(1,),
            in_specs=[flat, pl.BlockSpec((8, rows, cols), lambda i, me_ref: (0, 0, 0)), flat, flat, flat],
            out_specs=[flat] * 4),
        out_shape=[_sds((rows, cols), F32)] * 4,
        compiler_params=_seq_params(),
    )(me, own, slots, w, m, v)


def kernel(x, positions, pre_mix_norm, w_in, sgu_ln_gain, sgu_ln_bias, sgu_w_spatial, sgu_b_spatial, attn_out_norm, sgu_out_norm, w_out, post_mix_norm, pre_ffn_norm, w_gate, w_up, w_down, post_ffn_norm, loss_target, m_pre_mix_norm, m_w_in, m_sgu_ln_gain, m_sgu_ln_bias, m_sgu_w_spatial, m_sgu_b_spatial, m_attn_out_norm, m_sgu_out_norm, m_w_out, m_post_mix_norm, m_pre_ffn_norm, m_w_gate, m_w_up, m_w_down, m_post_ffn_norm, v_pre_mix_norm, v_w_in, v_sgu_ln_gain, v_sgu_ln_bias, v_sgu_w_spatial, v_sgu_b_spatial, v_attn_out_norm, v_sgu_out_norm, v_w_out, v_post_mix_norm, v_pre_ffn_norm, v_w_gate, v_w_up, v_w_down, v_post_ffn_norm):
    a = dict(locals())
    cx, cy, cc = _coords()
    s_me = 2 * cx + cy
    shard_core = jnp.stack([s_me, cc]).astype(jnp.int32)
    me = jnp.stack([4 * cx + 2 * cy + cc]).astype(jnp.int32)
    small = {n: (a[n][0] if a[n].ndim > 2 else a[n]) for n in SMALL}
    b_t = small["sgu_b_spatial"].T
    xs, pos, target = x[0], positions.reshape(SEQ, 1), loss_target[0]
    flipped = ("w_gate", "w_up")

    def big(name, n):
        return jnp.swapaxes(a[name], 1, 2)[0] if n in flipped else a[name][0]

    own = {"w_in": _halves(big("w_in", "w_in").astype(BF16))}

    def with_own(full, n):
        full = lax.dynamic_update_slice(full, own[n][None], (s_me, 0, 0, 0))
        return full.reshape((N_SHARD,) + big(n, n).shape)

    ffn = ("w_gate", "w_up", "w_down")
    g_in = _gather_ici([own["w_in"]])
    (s_in,), token = _split_starts("gather_in_start", [g_in], small["pre_mix_norm"])
    for n in ("w_out",) + ffn:
        own[n] = _halves((big(n, n) + token[0:1, 0:1]).astype(BF16))
    g_out, g_ffn = _gather_ici([own["w_out"]]), _gather_ici([own[n] for n in ffn])
    (s_out, s_ffn), token = _split_starts("gather_rest_start", [g_out, g_ffn], token)
    in_lands = _split_wait("gather_in_wait", g_in, *s_in, token)
    ((in_lands,),) = _comm_only("comm_pass_in", [_gather_pass(in_lands)])
    w_in_f = with_own(in_lands, "w_in")
    h, u, vs, sgu, *qkv = _inproj_fwd(xs, pos, small["pre_mix_norm"], w_in_f, small["sgu_ln_gain"],
                                      small["sgu_ln_bias"], small["sgu_w_spatial"], b_t)
    views = [tuple(qkv[3 * i:3 * i + 3]) for i in range(len(DILATIONS))]
    out_lands = _split_wait("gather_out_wait", g_out, *s_out, sgu)
    o_list, l_list = [], []
    for dil, (qv, kv, vv) in zip(DILATIONS, views):
        if dil == 1:
            (o, l), ((out_lands,),) = _attn_fwd(qv, kv, vv, dil, comms=[_gather_pass(out_lands)])
        else:
            o, l = _attn_fwd(qv, kv, vv, dil)
        o_list.append(o)
        l_list.append(l)
    w_out_f = with_own(out_lands, "w_out")
    ffn_lands = _split_wait("gather_ffn_wait", g_ffn, *s_ffn, l_list[-1])
    (attn, mixed, y, x1, *lses), (ffn_lands,) = _mix_out_fwd(
        o_list, l_list, sgu, xs, w_out_f, small["attn_out_norm"], small["sgu_out_norm"], small["post_mix_norm"],
        comms=[_gather_pass(ffn_lands)])
    w_gate_f, w_up_f, w_down_f = (with_own(f, n).reshape(FF, D_MODEL) for f, n in zip(ffn_lands, ffn))
    h2, act, dg, dup, df, dx1, loss_cols, d_pre_ffn, d_post_ffn = _ffn_fwd_bwd(
        x1, target, w_gate_f, w_up_f, w_down_f, small["pre_ffn_norm"], small["post_ffn_norm"])

    full_tok = pl.BlockSpec((SEQ, D_MODEL), lambda s: (0, 0), pipeline_mode=pl.Buffered(1))
    gw, gb = {}, {}
    gw["w_gate"], gw["w_up"], gb["w_gate"], gb["w_up"] = _wgrad_ff([dg, dup], h2, "wgrad_gate_up")
    gw["w_down"], gb["w_down"] = _wgrad_ff([act], df, "wgrad_down")
    (dy, du, dvs_sgu, d_post_mix, d_attn_norm, d_sgu_norm, d_w_sp, d_b_sp, d_ln_gain, d_ln_bias,
     *dviews), (sib_ffn,) = _outproj_bwd(
        dx1, y, attn, sgu, w_out_f, small["post_mix_norm"], small["attn_out_norm"], small["sgu_out_norm"],
        u, vs, small["sgu_ln_gain"], small["sgu_ln_bias"], small["sgu_w_spatial"], b_t,
        comms=[_rs_sibling([gb[n] for n in ffn])])
    sib = dict(zip(ffn, sib_ffn))
    part = dict(zip(ffn, _chip_sums([gb[n] for n in ffn], [sib[n] for n in ffn], shard_core)))
    gw["w_out"], gb["w_out"] = _wgrad(mixed, dy, pl.BlockSpec((SEQ, OUT_S), lambda s: (0, s)), full_tok,
                                      (OUT_S, D_MODEL), "wgrad_out")
    x_ffn = _rs_chips([part[n] for n in ffn])
    packed_early = _pack({
        "sgu_ln_gain": d_ln_gain, "sgu_ln_bias": d_ln_bias, "sgu_b_spatial": d_b_sp,
        "attn_out_norm": d_attn_norm, "sgu_out_norm": d_sgu_norm, "post_mix_norm": d_post_mix,
        "pre_ffn_norm": d_pre_ffn, "post_ffn_norm": d_post_ffn, LOSS_ROW: loss_cols}, SMALL_EARLY, 8)
    wsp_view = (SGU_GROUPS * CHUNK, CHUNK)
    packed_wsp = d_w_sp.reshape(wsp_view).astype(BF16)
    x_small, x_wsp = _small_exchange(packed_early), _small_exchange(packed_wsp)
    (s_ffn,), token = _split_starts("rs_ffn_start", [x_ffn], small["pre_mix_norm"])

    dqs, dks, dvs = [], [], []
    for i, (dil, (qv, kv, vv)) in enumerate(zip(DILATIONS, views)):
        if dil == 1:
            (dq, dk, dv), ((sib["w_out"],),) = _attn_bwd(qv, kv, vv, dviews[i], dviews[3 + i], lses[i], dil,
                                                        comms=[_rs_sibling([gb["w_out"]])], after=[token])
            (part["w_out"],) = _chip_sums([gb["w_out"]], [sib["w_out"]], shard_core)
            x_out = _rs_chips([part["w_out"]])
            (s_out, s_small, s_wsp), token = _split_starts("rs_out_small_start", [x_out, x_small, x_wsp], token)
        else:
            dq, dk, dv = _attn_bwd(qv, kv, vv, dviews[i], dviews[3 + i], lses[i], dil, after=[token])
        dqs.append(dq)
        dks.append(dk)
        dvs.append(dv)
    half, far, joined = {}, {}, {}
    far.update(zip(ffn, _split_wait("rs_ffn_wait", x_ffn, *s_ffn, dvs[-1])))
    half.update(zip(ffn, _final_sums([gw[n] for n in ffn], [sib[n] for n in ffn], [far[n] for n in ffn],
                                     shard_core)))
    dproj, grad_x, d_pre_mix = _inproj_bwd(dqs, dks, dvs, du, dvs_sgu, pos, xs, dx1, w_in_f, small["pre_mix_norm"])
    packed_late = _pack({"pre_mix_norm": d_pre_mix}, SMALL_LATE, 8)
    (gw["w_in"], gb["w_in"]), (got, (slots_late,)) = _wgrad(
        h, dproj, full_tok, pl.BlockSpec((SEQ, IN_S), lambda s: (0, s)), (D_MODEL, IN_S), "wgrad_in",
        comms=[_rs_join([half[n] for n in ffn]), _small_exchange(packed_late)])
    joined.update(zip(ffn, got))
    ((sib["w_in"],),) = _comm_only("comm_rs_sibling_in", [_rs_sibling([gb["w_in"]])])
    (part["w_in"],) = _chip_sums([gb["w_in"]], [sib["w_in"]], shard_core)
    x_in = _rs_chips([part["w_in"]])
    (s_in,), token = _split_starts("rs_in_start", [x_in], small["pre_mix_norm"])

    grads, deltas, new_m, new_v = {}, {}, {}, {}

    def record(n, outs):
        grads[n], deltas[n], new_m[n], new_v[n] = (
            jnp.swapaxes(o[None], 1, 2) if n in flipped else o[None] for o in outs)

    def update(names, name, after):
        for n, outs in zip(names, _adamw_multi(
                [big(n, n) for n in names], [joined[n].reshape(big(n, n).shape) for n in names],
                [big("m_" + n, n) for n in names], [big("v_" + n, n) for n in names], name, after)):
            record(n, outs)

    update(ffn, "adamw_ffn", [token])
    (far["w_out"],) = _split_wait("rs_out_wait", x_out, *s_out, new_v["w_down"])
    (slots_early,) = _split_wait("small_early_wait", x_small, *s_small, far["w_out"])
    (slots_wsp,) = _split_wait("small_wsp_wait", x_wsp, *s_wsp, slots_early)
    (far["w_in"],) = _split_wait("rs_in_wait", x_in, *s_in, slots_wsp)
    last = ("w_in", "w_out")
    half.update(zip(last, _final_sums([gw[n] for n in last], [sib[n] for n in last], [far[n] for n in last],
                                      shard_core)))
    (got,) = _comm_only("comm_rs_join_in_out", [_rs_join([half[n] for n in last])])
    joined.update(zip(last, got))
    update(last, "adamw_in_out", [])
    a[LOSS_ROW] = a["m_" + LOSS_ROW] = a["v_" + LOSS_ROW] = jnp.zeros((1, D_MODEL), F32)
    outs = _adamw_small(packed_wsp, slots_wsp, *[a[p + SMALL_WSP].reshape(wsp_view) for p in ("", "m_", "v_")], me)
    for dst, buf in zip((grads, deltas, new_m, new_v), outs):
        dst[SMALL_WSP] = buf.reshape(a[SMALL_WSP].shape)
    for names, rows, packed, slots in ((SMALL_EARLY, 8, packed_early, slots_early),
                                       (SMALL_LATE, 8, packed_late, slots_late)):
        outs = _adamw_small(packed, slots, _pack(a, names, rows), _pack({n: a["m_" + n] for n in names}, names, rows),
                            _pack({n: a["v_" + n] for n in names}, names, rows), me)
        for dst, buf in zip((grads, deltas, new_m, new_v), outs):
            dst.update(_unpack(buf, names, {n: a[n].shape for n in names}))
    loss = jnp.sum(grads[LOSS_ROW]) * np.float32(0.5 / D_MODEL)
    return (loss, grad_x[None], *[grads[n] for n in WEIGHTS], *[deltas[n] for n in WEIGHTS],
            *[new_m[n] for n in WEIGHTS], *[new_v[n] for n in WEIGHTS])
```

```python
import numpy as np
import jax
import jax.numpy as jnp
from jax import lax
from jax.experimental import pallas as pl
from jax.experimental.pallas import tpu as pltpu

F32 = jnp.float32
BF16 = jnp.bfloat16

SEQ = 2048
D_MODEL = 1024
HEAD_DIM = 64
ATTN_W = 512
SGU_W = 512
SGU_GROUPS = 8
CHUNK = 128
DILATIONS = (1, 4, 16)
N_SHARD = 4
IN_S = 640
OUT_S = 256
FF_S = 704
PROJ_W = N_SHARD * IN_S
FF = N_SHARD * FF_S
FF_CHUNKS = ((0, 1024), (1024, 2048), (2048, FF))
RMS_EPS = 1e-6
LN_EPS = 1e-5
ROPE_THETA = 500000.0
ATTN_SCALE = 1.0 / np.sqrt(HEAD_DIM)
NEG = -1e30
TM = 512
TM_FFN = 256
VMEM_LIMIT = 56 * 1024 * 1024

ADAM_LR = 0.001
ADAM_B1 = 0.9
ADAM_B2 = 0.999
ADAM_EPS = 1e-08
ADAM_WD = 0.01
ADAM_STEP = 10

MESH = pl.DeviceIdType.MESH
ANY = pl.BlockSpec(memory_space=pl.ANY)


def _dot(a, b):
    return jnp.dot(a, b, preferred_element_type=F32)


def _dot_nt(a, b):
    return lax.dot_general(a, b, (((1,), (1,)), ((), ())), preferred_element_type=F32)


def _dot_tn(a, b):
    return lax.dot_general(a, b, (((0,), (0,)), ((), ())), preferred_element_type=F32)


def _dot_exact(a, b):
    return jnp.dot(a, b, preferred_element_type=F32, precision=lax.Precision.HIGHEST)


def _dot_select(a, sel):
    hi = a.astype(BF16)
    lo = (a - hi.astype(F32)).astype(BF16)
    sel = sel.astype(BF16)
    return _dot(hi, sel) + _dot(lo, sel)


def _rms_stats(x):
    r = lax.rsqrt(jnp.mean(x * x, axis=-1, keepdims=True) + RMS_EPS)
    return x * r, r


def _rms_bwd(xh, r, gain, dy):
    dxh = dy * gain
    dx = r * (dxh - xh * jnp.mean(dxh * xh, axis=-1, keepdims=True))
    return dx, jnp.sum(dy * xh, axis=0, keepdims=True)


_ERF_ALPHA = (-2.72614225801306e-10, 2.77068142495902e-08, -2.10102402082508e-06, -5.69250639462346e-05,
              -7.34990630326855e-04, -2.95459980854025e-03, -1.60960333262415e-02)
_ERF_BETA = (-1.45660718464996e-05, -2.13374055278905e-04, -1.68282697438203e-03, -7.37332916720468e-03,
             -1.42647390514189e-02)


def _erf(x):
    x = jnp.clip(x, -4.0, 4.0)
    x2 = x * x
    p = jnp.full_like(x, _ERF_ALPHA[0])
    for a in _ERF_ALPHA[1:]:
        p = p * x2 + a
    q = jnp.full_like(x, _ERF_BETA[0])
    for b in _ERF_BETA[1:]:
        q = q * x2 + b
    return x * p / q


def _normal_cdf(x):
    return 0.5 * (1.0 + _erf(x * np.float32(1.0 / np.sqrt(2.0))))


def _gelu_grad(x, cdf):
    pdf = jnp.exp(-0.5 * x * x) * np.float32(1.0 / np.sqrt(2.0 * np.pi))
    return cdf + x * pdf


def _sigmoid(x):
    return 1.0 / (1.0 + jnp.exp(-x))


_INV_FREQ = tuple(float(np.float32(ROPE_THETA ** (-2.0 * j / 16.0))) for j in range(8))


def _rot_tables(pos):
    lane = lax.broadcasted_iota(jnp.int32, (1, 128), 1)
    d = lane & 63
    j = d & 7
    inv = jnp.zeros((1, 128), F32)
    for jj in range(8):
        inv = jnp.where(j == jj, _INV_FREQ[jj], inv)
    ang = pos.astype(F32) * inv
    c = jnp.cos(ang)
    s = jnp.sin(ang)
    cos_t = jnp.where(d < 16, c, 1.0)
    sin_a = jnp.where(d < 8, -s, 0.0)
    sin_b = jnp.where((d >= 8) & (d < 16), s, 0.0)
    return tuple(jnp.tile(t, (1, 4)) for t in (cos_t, sin_a, sin_b))


def _rope(x, tabs):
    cos_t, sin_a, sin_b = tabs
    return x * cos_t + pltpu.roll(x, 504, 1) * sin_a + pltpu.roll(x, 8, 1) * sin_b


def _rope_bwd(dy, tabs):
    cos_t, sin_a, sin_b = tabs
    return dy * cos_t + pltpu.roll(dy * sin_a, 8, 1) + pltpu.roll(dy * sin_b, 504, 1)


def _left_half():
    return lax.broadcasted_iota(jnp.int32, (CHUNK, CHUNK), 1) < HEAD_DIM


def _group_ones():
    lane = lax.broadcasted_iota(jnp.int32, (SGU_GROUPS, SGU_W), 1)
    row = lax.broadcasted_iota(jnp.int32, (SGU_GROUPS, SGU_W), 0)
    return ((lane >> 6) == row).astype(F32)


def _masked_spatial(w_ref):
    row = lax.broadcasted_iota(jnp.int32, (CHUNK, CHUNK), 0)
    col = lax.broadcasted_iota(jnp.int32, (CHUNK, CHUNK), 1)
    return [jnp.where(col <= row, w_ref[g], 0.0).astype(BF16) for g in range(SGU_GROUPS)]


def _sgu_core(u, vs, lg, lb, wm, bias_full):
    tm = u.shape[0]
    cdf_u, cdf_vs = _normal_cdf(u), _normal_cdf(vs)
    gu = u * cdf_u
    gv = vs * cdf_vs
    mu = jnp.mean(gv, axis=-1, keepdims=True)
    xc = gv - mu
    rstd = lax.rsqrt(jnp.mean(xc * xc, axis=-1, keepdims=True) + LN_EPS)
    xh = xc * rstd
    vnb = (xh * lg + lb).astype(BF16)
    left = _left_half()
    rows = []
    for c in range(tm // CHUNK):
        pieces = []
        for p in range(4):
            vp = vnb[c * CHUNK:(c + 1) * CHUNK, p * 128:(p + 1) * 128]
            pieces.append(jnp.where(left, _dot(wm[2 * p], vp), _dot(wm[2 * p + 1], vp)))
        rows.append(jnp.concatenate(pieces, axis=1) + bias_full)
    mixed = jnp.concatenate(rows, axis=0)
    return gu, xh, rstd, vnb, mixed, cdf_u, cdf_vs


def _resident(shape):
    n = len(shape)
    return pl.BlockSpec(shape, lambda *_: (0,) * n, pipeline_mode=pl.Buffered(1))


def _rows(ncol, tm=TM):
    return pl.BlockSpec((tm, ncol), lambda i: (i, 0))


def _acc(ncol, nrow=1):
    return pl.BlockSpec((nrow, ncol), lambda i: (0, 0))


HEAD_W = 128


def _view_rows(dil, width=ATTN_W, tm=TM):
    return pl.BlockSpec((tm // dil, dil * width), lambda i: (i, 0))


def _view_shape(dil, dtype, width=ATTN_W):
    return _sds((SEQ // dil, dil * width), dtype)


def _slab_scratch():
    return pltpu.VMEM((4, TM, 128), F32)


def _store_view(val, out_ref, slabs, dil):
    width = val.shape[1]
    for j in range(width // 128):
        slabs[j] = val[:, j * 128:(j + 1) * 128]
    for r in range(dil):
        for j in range(width // 128):
            c0 = r * width + j * 128
            out_ref[:, c0:c0 + 128] = slabs.at[j][pl.ds(r, TM // dil, stride=dil), :].astype(out_ref.dtype)


def _load_view(in_ref, slabs, dil, width=ATTN_W):
    for r in range(dil):
        for j in range(width // 128):
            c0 = r * width + j * 128
            slabs.at[j][pl.ds(r, TM // dil, stride=dil), :] = in_ref[:, c0:c0 + 128].astype(F32)
    return jnp.concatenate([slabs[j] for j in range(width // 128)], axis=1)


def _head_spread():
    m = lax.broadcasted_iota(jnp.int32, (HEAD_W, ATTN_W), 0)
    lane = lax.broadcasted_iota(jnp.int32, (HEAD_W, ATTN_W), 1)
    return (m == 16 * (lane >> 6)).astype(F32)


def _head_sum():
    lane = lax.broadcasted_iota(jnp.int32, (ATTN_W, HEAD_W), 0)
    m = lax.broadcasted_iota(jnp.int32, (ATTN_W, HEAD_W), 1)
    return ((lane >> 6) == (m >> 4)).astype(F32)


def _seq_params():
    return pltpu.CompilerParams(dimension_semantics=("arbitrary",), vmem_limit_bytes=VMEM_LIMIT)


def _sds(shape, dtype):
    return jax.ShapeDtypeStruct(shape, dtype)


class _Comm:
    def __init__(self, args, out_shape, n_sems, start, finish, aliased=False):
        self.args, self.out_shape, self.n_sems = list(args), list(out_shape), n_sems
        self.start, self.finish, self.aliased = start, finish, aliased


def _pcall(body, *, name, grid, in_specs, out_specs, out_shape, args, scratch_shapes=(), comms=(), after=()):
    single = not isinstance(out_shape, (list, tuple))
    out_specs = [out_specs] if single else list(out_specs)
    out_shape = [out_shape] if single else list(out_shape)
    n_in, n_out, n_scr = len(in_specs), len(out_shape), len(scratch_shapes)
    c_args = [a for c in comms for a in c.args]
    c_outs = [o for c in comms for o in c.out_shape]
    aliases, ai, ao = {}, n_in, n_out
    for c in comms:
        if c.aliased:
            aliases.update({ai + k: ao + k for k in range(len(c.args))})
        ai += len(c.args)
        ao += len(c.out_shape)
    sems = [pltpu.SemaphoreType.DMA((c.n_sems,)) for c in comms for _ in range(2)]
    steps = grid[0]

    def wrapped(*refs):
        o0 = n_in + len(c_args) + len(after)
        s0 = o0 + n_out + len(c_outs)
        m_in, m_out, m_sem = refs[n_in:n_in + len(c_args)], refs[o0 + n_out:s0], refs[s0 + n_scr:]

        def each(phase):
            ii = oi = 0
            for k, c in enumerate(comms):
                getattr(c, phase)(m_in[ii:ii + len(c.args)], m_out[oi:oi + len(c.out_shape)],
                                  m_sem[2 * k], m_sem[2 * k + 1])
                ii += len(c.args)
                oi += len(c.out_shape)

        if comms:
            @pl.when(pl.program_id(0) == 0)
            def _():
                each("start")

        body(*refs[:n_in], *refs[o0:o0 + n_out], *refs[s0:s0 + n_scr])

        if comms:
            @pl.when(pl.program_id(0) == steps - 1)
            def _():
                each("finish")

    res = pl.pallas_call(
        wrapped, name=name, grid=grid,
        in_specs=list(in_specs) + [ANY] * (len(c_args) + len(after)), out_specs=out_specs + [ANY] * len(c_outs),
        out_shape=out_shape + c_outs, scratch_shapes=list(scratch_shapes) + sems,
        input_output_aliases=aliases, compiler_params=_seq_params(),
    )(*args, *c_args, *after)
    mine = res[0] if single else list(res[:n_out])
    if not comms:
        return mine
    theirs, oi = [], n_out
    for c in comms:
        theirs.append(list(res[oi:oi + len(c.out_shape)]))
        oi += len(c.out_shape)
    return mine, theirs


def _in_pieces(g):
    lo, hi = 512 * g, 512 * (g + 1)
    return [(s, max(lo, IN_S * s) - IN_S * s, min(hi, IN_S * (s + 1)) - IN_S * s)
            for s in range(N_SHARD) if max(lo, IN_S * s) < min(hi, IN_S * (s + 1))]


def _inproj_fwd(x, pos, g_pre, w_in, lg, lb, w_sp, b_t, comms=()):
    def body(x_ref, pos_ref, g_ref, w_ref, lg_ref, lb_ref, wsp_ref, bt_ref, h_ref, u_ref, vs_ref, sgu_ref, *rest):
        qkv_refs, slabs = rest[:9], rest[9:]
        xh, _ = _rms_stats(x_ref[...])
        h = (xh * g_ref[...]).astype(BF16)
        h_ref[...] = h
        tabs = _rot_tables(pos_ref[...])

        def group(g):
            return jnp.concatenate([_dot(h, w_ref[s, :, a:b]) for s, a, b in _in_pieces(g)], axis=1)

        for t in range(3):
            val = group(t)
            if t < 2:
                val = _rope(val, tabs)
            if t == 0:
                val = val * np.float32(ATTN_SCALE)
            qkv_refs[t][...] = val.astype(BF16)
            for i, dil in enumerate(DILATIONS[1:]):
                _store_view(val, qkv_refs[3 * (i + 1) + t], slabs[t], dil)
        u = group(3)
        vs = group(4)
        u_ref[...] = u
        vs_ref[...] = vs
        bias_full = _dot_exact(bt_ref[...], _group_ones())
        gu, _, _, _, mixed, _, _ = _sgu_core(u, vs, lg_ref[...], lb_ref[...], _masked_spatial(wsp_ref), bias_full)
        sgu_ref[...] = gu * mixed

    return _pcall(
        body, name="inproj_sgu_fwd", grid=(SEQ // TM,),
        in_specs=[_rows(D_MODEL), _rows(1), _resident((1, D_MODEL)), _resident((N_SHARD, D_MODEL, IN_S)),
                  _resident((1, SGU_W)), _resident((1, SGU_W)), _resident((SGU_GROUPS, CHUNK, CHUNK)),
                  _resident((CHUNK, SGU_GROUPS))],
        out_specs=[_rows(D_MODEL), _rows(512), _rows(512), _rows(512)]
        + [_view_rows(dil) for dil in DILATIONS for _ in range(3)],
        out_shape=[_sds((SEQ, D_MODEL), BF16), _sds((SEQ, 512), F32), _sds((SEQ, 512), F32), _sds((SEQ, 512), F32)]
        + [_view_shape(dil, BF16) for dil in DILATIONS for _ in range(3)],
        scratch_shapes=[_slab_scratch() for _ in range(3)],
        args=(x, pos, g_pre, w_in, lg, lb, w_sp, b_t), comms=comms)


def _block_masks():
    row = lax.broadcasted_iota(jnp.int32, (CHUNK, CHUNK), 0)
    col = lax.broadcasted_iota(jnp.int32, (CHUNK, CHUNK), 1)
    return col <= row, col >= row


def _attn_fwd(qv, kv, vv, dil, comms=()):
    seg = SEQ // dil
    nblk = seg // CHUNK
    rps = 4 if nblk == 1 else 1

    def body(q_ref, k_ref, v_ref, o_ref, l_ref):
        left = _left_half()
        m_cur, m_prev = _block_masks()
        zero = jnp.zeros((CHUNK, CHUNK), BF16)
        ones = (jnp.where(left, 1.0, 0.0).astype(BF16), jnp.where(left, 0.0, 1.0).astype(BF16))

        sides = tuple(enumerate((left, ~left)))

        def rows(b):
            if isinstance(b, int):
                return b * CHUNK, max(b - 1, 0) * CHUNK
            return pl.multiple_of(b * CHUNK, CHUNK), pl.multiple_of(jnp.maximum(b - 1, 0) * CHUNK, CHUNK)

        def first(rr, b):
            r0, rp = rows(b)
            prev_ok = m_prev & (b > 0)
            tiles, scores = [], []
            for hp in range(4):
                ls = slice(rr * ATTN_W + hp * 128, rr * ATTN_W + (hp + 1) * 128)
                qp = q_ref[pl.ds(r0, CHUNK), ls]
                kc = k_ref[pl.ds(r0, CHUNK), ls]
                kp = k_ref[pl.ds(rp, CHUNK), ls] if nblk > 1 else None
                tiles.append((ls, v_ref[pl.ds(r0, CHUNK), ls], v_ref[pl.ds(rp, CHUNK), ls] if nblk > 1 else None))
                for _, hm in sides:
                    qh = jnp.where(hm, qp, zero)
                    sc = jnp.where(m_cur, _dot_nt(qh, kc), NEG)
                    sp = jnp.where(prev_ok, _dot_nt(qh, kp), NEG) if nblk > 1 else None
                    scores.append((sc, sp))
            return tiles, scores

        def second(scores):
            probs = []
            for sc, sp in scores:
                if nblk > 1:
                    m = jnp.max(jnp.maximum(sc, sp), axis=-1, keepdims=True)
                    pc = jnp.exp(sc - m)
                    pp = jnp.exp(sp - m)
                    probs.append((m, pc.astype(BF16), pp.astype(BF16), (pc + pp).astype(BF16)))
                else:
                    m = jnp.max(sc, axis=-1, keepdims=True)
                    pc = jnp.exp(sc - m).astype(BF16)
                    probs.append((m, pc, None, pc))
            return probs

        def third(rr, b, tiles, probs):
            r0, _ = rows(b)
            for hp, (ls, vc, vp) in enumerate(tiles):
                acc = jnp.zeros((CHUNK, CHUNK), F32)
                den = jnp.zeros((CHUNK, CHUNK), F32)
                for side, hm in sides:
                    _, pc, pp, psum = probs[2 * hp + side]
                    acc = acc + _dot(pc, jnp.where(hm, vc, zero))
                    if nblk > 1:
                        acc = acc + _dot(pp, jnp.where(hm, vp, zero))
                    den = den + _dot(psum, ones[side])
                o_ref[pl.ds(r0, CHUNK), ls] = (acc / den).astype(o_ref.dtype)
                lse = jnp.where(left, probs[2 * hp][0], probs[2 * hp + 1][0]) + jnp.log(den)
                l_ref[pl.ds(r0, CHUNK), rr * HEAD_W + 32 * hp:rr * HEAD_W + 32 * hp + 32] = lse[:, 48:80]

        def run(units):
            data = [first(rr, b) for rr, b in units]
            probs = [second(scores) for _, scores in data]
            for (rr, b), (tiles, _), pr in zip(units, data, probs):
                third(rr, b, tiles, pr)

        if nblk == 1:
            run([(rr, 0) for rr in range(rps)])
        else:
            def one(b, carry):
                run([(0, b)])
                return carry

            lax.fori_loop(0, nblk, one, 0)

    spec = pl.BlockSpec((seg, rps * ATTN_W), lambda r: (0, r))
    return _pcall(
        body, name=f"attn_fwd_d{dil}", grid=(dil // rps,),
        in_specs=[spec, spec, spec], out_specs=[spec, pl.BlockSpec((seg, rps * HEAD_W), lambda r: (0, r))],
        out_shape=[_sds((seg, dil * ATTN_W), BF16), _sds((seg, dil * HEAD_W), F32)],
        args=(qv, kv, vv), comms=comms)


def _lane_left(nrows):
    return lax.broadcasted_iota(jnp.int32, (nrows, CHUNK), 1) < HEAD_DIM


def _attn_rows(b):
    if isinstance(b, int):
        return b * CHUNK, max(b - 1, 0) * CHUNK
    return pl.multiple_of(b * CHUNK, CHUNK), pl.multiple_of(jnp.maximum(b - 1, 0) * CHUNK, CHUNK)


def _mix_out_fwd(o_list, l_list, sgu, x, w_out, g_attn, g_sgu, g_post, comms=()):
    def body(o1, o2, o3, l1, l2, l3, sgu_ref, x_ref, w_ref, ga_ref, gs_ref, gp_ref,
             attn_ref, mixed_ref, y_ref, x1_ref, lse1_ref, lse2_ref, lse3_ref, slabs_a, slabs_b):
        os = [o1[...], _load_view(o2, slabs_a, DILATIONS[1]), _load_view(o3, slabs_b, DILATIONS[2])]
        ls = [l1[...], _load_view(l2, slabs_a, DILATIONS[1], HEAD_W), _load_view(l3, slabs_b, DILATIONS[2], HEAD_W)]
        m = jnp.maximum(jnp.maximum(ls[0], ls[1]), ls[2])
        es = [jnp.exp(l - m) for l in ls]
        den = es[0] + es[1] + es[2]
        spread = _head_spread()
        attn = sum(_dot_select(e / den, spread) * o for e, o in zip(es, os))
        attn_ref[...] = attn
        lse = m + jnp.log(den)
        lse1_ref[...] = lse
        _store_view(lse, lse2_ref, slabs_a, DILATIONS[1])
        _store_view(lse, lse3_ref, slabs_b, DILATIONS[2])
        ah, _ = _rms_stats(attn)
        sh, _ = _rms_stats(sgu_ref[...])
        mixed = jnp.concatenate([ah * ga_ref[...], sh * gs_ref[...]], axis=1).astype(BF16)
        mixed_ref[...] = mixed
        y = _dot(mixed[:, 0:OUT_S], w_ref[0])
        for s in range(1, N_SHARD):
            y = y + _dot(mixed[:, s * OUT_S:(s + 1) * OUT_S], w_ref[s])
        y_ref[...] = y
        yh, _ = _rms_stats(y)
        x1_ref[...] = x_ref[...] + yh * gp_ref[...]

    return _pcall(
        body, name="mix_out_fwd", grid=(SEQ // TM,),
        in_specs=[_view_rows(dil) for dil in DILATIONS] + [_view_rows(dil, HEAD_W) for dil in DILATIONS]
        + [_rows(512), _rows(D_MODEL), _resident((N_SHARD, OUT_S, D_MODEL)),
           _resident((1, 512)), _resident((1, 512)), _resident((1, D_MODEL))],
        out_specs=[_rows(512), _rows(D_MODEL), _rows(D_MODEL), _rows(D_MODEL)]
        + [_view_rows(dil, HEAD_W) for dil in DILATIONS],
        out_shape=[_sds((SEQ, 512), F32), _sds((SEQ, D_MODEL), BF16), _sds((SEQ, D_MODEL), F32),
                   _sds((SEQ, D_MODEL), F32)] + [_view_shape(dil, F32, HEAD_W) for dil in DILATIONS],
        scratch_shapes=[_slab_scratch(), _slab_scratch()],
        args=(*o_list, *l_list, sgu, x, w_out, g_attn, g_sgu, g_post), comms=comms)


def _ffn_fwd_bwd(x1, target, w_gate, w_up, w_down, g_pre, g_post, comms=()):
    def body(x1_ref, t_ref, wg_ref, wu_ref, wd_ref, gpf_ref, gpo_ref,
             h2_ref, a_ref, dg_ref, dup_ref, df_ref, dx1_ref, loss_ref, dgpf_ref, dgpo_ref, g_scr, up_scr):
        @pl.when(pl.program_id(0) == 0)
        def _():
            loss_ref[...] = jnp.zeros_like(loss_ref)
            dgpf_ref[...] = jnp.zeros_like(dgpf_ref)
            dgpo_ref[...] = jnp.zeros_like(dgpo_ref)

        x1 = x1_ref[...]
        gpf = gpf_ref[...]
        gpo = gpo_ref[...]
        xh, r = _rms_stats(x1)
        h2 = (xh * gpf).astype(BF16)
        h2_ref[...] = h2
        f = jnp.zeros((TM_FFN, D_MODEL), F32)
        for c0, c1 in FF_CHUNKS:
            g = _dot_nt(h2, wg_ref[c0:c1, :])
            up = _dot_nt(h2, wu_ref[c0:c1, :])
            g_scr[:, c0:c1] = g
            up_scr[:, c0:c1] = up
            a = (g * _sigmoid(g) * up).astype(BF16)
            a_ref[:, c0:c1] = a
            f = f + _dot(a, wd_ref[c0:c1, :])
        fh, rf = _rms_stats(f)
        diff = x1 + fh * gpo - t_ref[...]
        loss_ref[...] += jnp.sum(diff * diff, axis=0, keepdims=True)
        dout = diff * np.float32(1.0 / D_MODEL)
        df, dgpo = _rms_bwd(fh, rf, gpo, dout)
        dgpo_ref[...] += dgpo
        dfb = df.astype(BF16)
        df_ref[...] = dfb
        dh2 = jnp.zeros((TM_FFN, D_MODEL), F32)
        for c0, c1 in FF_CHUNKS:
            da = _dot_nt(dfb, wd_ref[c0:c1, :])
            g = g_scr[:, c0:c1]
            up = up_scr[:, c0:c1]
            sg = _sigmoid(g)
            dup = (da * (g * sg)).astype(BF16)
            dg = (da * up * (sg * (1.0 + g * (1.0 - sg)))).astype(BF16)
            dg_ref[:, c0:c1] = dg
            dup_ref[:, c0:c1] = dup
            dh2 = dh2 + _dot(dg, wg_ref[c0:c1, :]) + _dot(dup, wu_ref[c0:c1, :])
        dx, dgpf = _rms_bwd(xh, r, gpf, dh2)
        dgpf_ref[...] += dgpf
        dx1_ref[...] = dout + dx

    return _pcall(
        body, name="ffn_fwd_bwd", grid=(SEQ // TM_FFN,),
        in_specs=[_rows(D_MODEL, TM_FFN), _rows(D_MODEL, TM_FFN), _resident((FF, D_MODEL)),
                  _resident((FF, D_MODEL)), _resident((FF, D_MODEL)),
                  _resident((1, D_MODEL)), _resident((1, D_MODEL))],
        out_specs=[_rows(D_MODEL, TM_FFN), _rows(FF, TM_FFN), _rows(FF, TM_FFN), _rows(FF, TM_FFN),
                   _rows(D_MODEL, TM_FFN), _rows(D_MODEL, TM_FFN), _acc(D_MODEL), _acc(D_MODEL), _acc(D_MODEL)],
        out_shape=[_sds((SEQ, D_MODEL), BF16), _sds((SEQ, FF), BF16), _sds((SEQ, FF), BF16),
                   _sds((SEQ, FF), BF16), _sds((SEQ, D_MODEL), BF16), _sds((SEQ, D_MODEL), F32),
                   _sds((1, D_MODEL), F32), _sds((1, D_MODEL), F32), _sds((1, D_MODEL), F32)],
        scratch_shapes=[pltpu.VMEM((TM_FFN, FF), F32), pltpu.VMEM((TM_FFN, FF), F32)],
        args=(x1, target, w_gate, w_up, w_down, g_pre, g_post), comms=comms)


def _wgrad(a, b, a_spec, b_spec, out_block, name, comms=()):
    def body(a_ref, b_ref, o_ref, ob_ref):
        av = a_ref[0] if len(a_ref.shape) == 3 else a_ref[...]
        bv = b_ref[0] if len(b_ref.shape) == 3 else b_ref[...]
        g = _dot_tn(av, bv)
        o_ref[0] = g
        ob_ref[0] = g.astype(BF16)

    return _pcall(
        body, name=name, grid=(N_SHARD,),
        in_specs=[a_spec, b_spec],
        out_specs=[pl.BlockSpec((1,) + out_block, lambda s: (s, 0, 0))] * 2,
        out_shape=[_sds((N_SHARD,) + out_block, F32), _sds((N_SHARD,) + out_block, BF16)],
        args=(a, b), comms=comms)


def _outproj_bwd(dx1, y, attn, sgu, w_out, g_post, g_attn, g_sgu, u, vs, lg, lb, w_sp, b_t, comms=(), after=()):
    sgu_bwd = _sgu_bwd_body()

    def body(dx1_ref, y_ref, attn_ref, sgu_ref, w_ref, gp_ref, ga_ref, gs_ref, u_ref, vs_ref, lg_ref, lb_ref,
             wsp_ref, bt_ref, dy_ref, du_ref, dvs_ref, dgp_ref, dga_ref, dgs_ref, dw_ref, db_ref, dlg_ref, dlb_ref,
             *rest):
        dattn_refs, delta_refs, (slabs_a, slabs_b, dsgu_ref, dbias_scr) = rest[0:3], rest[3:6], rest[6:]

        @pl.when(pl.program_id(0) == 0)
        def _():
            dgp_ref[...] = jnp.zeros_like(dgp_ref)
            dga_ref[...] = jnp.zeros_like(dga_ref)
            dgs_ref[...] = jnp.zeros_like(dgs_ref)

        yh, ry = _rms_stats(y_ref[...])
        dy, dgp = _rms_bwd(yh, ry, gp_ref[...], dx1_ref[...])
        dgp_ref[...] += dgp
        dyb = dy.astype(BF16)
        dy_ref[...] = dyb
        dmixed = jnp.concatenate([_dot_nt(dyb, w_ref[s]) for s in range(N_SHARD)], axis=1)
        attn = attn_ref[...]
        ah, ra = _rms_stats(attn)
        dattn, dga = _rms_bwd(ah, ra, ga_ref[...], dmixed[:, 0:512])
        dga_ref[...] += dga
        sh, rs = _rms_stats(sgu_ref[...])
        dsgu, dgs = _rms_bwd(sh, rs, gs_ref[...], dmixed[:, 512:1024])
        dgs_ref[...] += dgs
        dsgu_ref[...] = dsgu
        delta = _dot_select(dattn * attn, _head_sum())
        dattn_refs[0][...] = dattn.astype(BF16)
        delta_refs[0][...] = delta
        for i, dil in enumerate(DILATIONS[1:]):
            _store_view(dattn, dattn_refs[i + 1], slabs_a, dil)
            _store_view(delta, delta_refs[i + 1], slabs_b, dil)
        sgu_bwd(u_ref, vs_ref, dsgu_ref, lg_ref, lb_ref, wsp_ref, bt_ref,
                du_ref, dvs_ref, dw_ref, db_ref, dlg_ref, dlb_ref, dbias_scr)

    return _pcall(
        body, name="outproj_sgu_bwd", grid=(SEQ // TM,),
        in_specs=[_rows(D_MODEL), _rows(D_MODEL), _rows(512), _rows(512), _resident((N_SHARD, OUT_S, D_MODEL)),
                  _resident((1, D_MODEL)), _resident((1, 512)), _resident((1, 512)),
                  _rows(SGU_W), _rows(SGU_W), _resident((1, SGU_W)), _resident((1, SGU_W)),
                  _resident((SGU_GROUPS, CHUNK, CHUNK)), _resident((CHUNK, SGU_GROUPS))],
        out_specs=[_rows(D_MODEL), _rows(SGU_W), _rows(SGU_W), _acc(D_MODEL), _acc(512), _acc(512),
                   pl.BlockSpec((SGU_GROUPS, CHUNK, CHUNK), lambda i: (0, 0, 0)), _acc(CHUNK, SGU_GROUPS),
                   _acc(SGU_W), _acc(SGU_W)]
        + [_view_rows(dil) for dil in DILATIONS] + [_view_rows(dil, HEAD_W) for dil in DILATIONS],
        out_shape=[_sds((SEQ, D_MODEL), BF16), _sds((SEQ, SGU_W), BF16), _sds((SEQ, SGU_W), BF16),
                   _sds((1, D_MODEL), F32), _sds((1, 512), F32), _sds((1, 512), F32),
                   _sds((SGU_GROUPS, CHUNK, CHUNK), F32), _sds((SGU_GROUPS, CHUNK), F32),
                   _sds((1, SGU_W), F32), _sds((1, SGU_W), F32)]
        + [_view_shape(dil, BF16) for dil in DILATIONS] + [_view_shape(dil, F32, HEAD_W) for dil in DILATIONS],
        scratch_shapes=[_slab_scratch(), _slab_scratch(), pltpu.VMEM((TM, SGU_W), F32),
                        pltpu.VMEM((CHUNK, SGU_W), F32)],
        args=(dx1, y, attn, sgu, w_out, g_post, g_attn, g_sgu, u, vs, lg, lb, w_sp, b_t), comms=comms, after=after)


def _sgu_bwd_body():
    nsteps = SEQ // TM

    def body(u_ref, vs_ref, ds_ref, lg_ref, lb_ref, w_ref, bt_ref,
             du_ref, dvs_ref, dw_ref, db_ref, dlg_ref, dlb_ref, dbias_scr):
        i = pl.program_id(0)

        @pl.when(i == 0)
        def _():
            dw_ref[...] = jnp.zeros_like(dw_ref)
            dlg_ref[...] = jnp.zeros_like(dlg_ref)
            dlb_ref[...] = jnp.zeros_like(dlb_ref)
            dbias_scr[...] = jnp.zeros_like(dbias_scr)

        wm = _masked_spatial(w_ref)
        ones_g = _group_ones()
        bias_full = _dot_exact(bt_ref[...], ones_g)
        u = u_ref[...]
        vs = vs_ref[...]
        lg = lg_ref[...]
        gu, xh, rstd, vnb, mixed, cdf_u, cdf_vs = _sgu_core(u, vs, lg, lb_ref[...], wm, bias_full)
        dsgu = ds_ref[...]
        du_ref[...] = (dsgu * mixed * _gelu_grad(u, cdf_u)).astype(BF16)
        dmixed = dsgu * gu
        left = _left_half()
        dvn_rows = []
        for c in range(TM // CHUNK):
            rs = slice(c * CHUNK, (c + 1) * CHUNK)
            dm_c = dmixed[rs, :]
            dbias_scr[...] += dm_c
            pieces = []
            for p in range(4):
                ls = slice(p * 128, (p + 1) * 128)
                dmp = dm_c[:, ls]
                vp = vnb[rs, ls]
                dmb = dmp.astype(BF16)
                zero = jnp.zeros_like(dmb)
                dw_ref[2 * p] += _dot_nt(jnp.where(left, dmb, zero), vp)
                dw_ref[2 * p + 1] += _dot_nt(jnp.where(left, zero, dmb), vp)
                pieces.append(jnp.where(left, _dot_tn(wm[2 * p], dmb), _dot_tn(wm[2 * p + 1], dmb)))
            dvn_rows.append(jnp.concatenate(pieces, axis=1))
        dvn = jnp.concatenate(dvn_rows, axis=0)
        dlg_ref[...] += jnp.sum(dvn * xh, axis=0, keepdims=True)
        dlb_ref[...] += jnp.sum(dvn, axis=0, keepdims=True)
        dxh = dvn * lg
        dgv = rstd * (dxh - jnp.mean(dxh, axis=-1, keepdims=True) - xh * jnp.mean(dxh * xh, axis=-1, keepdims=True))
        dvs_ref[...] = (dgv * _gelu_grad(vs, cdf_vs)).astype(BF16)

        @pl.when(i == nsteps - 1)
        def _():
            row = lax.broadcasted_iota(jnp.int32, (CHUNK, CHUNK), 0)
            col = lax.broadcasted_iota(jnp.int32, (CHUNK, CHUNK), 1)
            for g in range(SGU_GROUPS):
                dw_ref[g] = jnp.where(col <= row, dw_ref[g], 0.0)
            db_ref[...] = lax.dot_general(ones_g, dbias_scr[...], (((1,), (1,)), ((), ())),
                                          preferred_element_type=F32, precision=lax.Precision.HIGHEST)

    return body


def _attn_bwd(qv, kv, vv, dov, deltav, lsev, dil, comms=(), after=()):
    seg = SEQ // dil
    nblk = seg // CHUNK
    rps = 4 if nblk == 1 else 1

    def body(q_ref, k_ref, v_ref, do_ref, dl_ref, lse_ref, dq_ref, dk_ref, dv_ref, dk_wait, dv_wait):
        left = _left_half()
        m_cur, m_prev = _block_masks()
        sides = tuple(enumerate((left, ~left)))
        zero = jnp.zeros((CHUNK, CHUNK), BF16)

        def first(rr, b, both):
            r0, rp = _attn_rows(b)
            keys = pl.ds(rp, 2 * CHUNK) if both else pl.ds(r0, CHUNK)
            tiles, heads = [], []
            for hp in range(4):
                ls = slice(rr * ATTN_W + hp * 128, rr * ATTN_W + (hp + 1) * 128)
                qp = q_ref[pl.ds(r0, CHUNK), ls]
                dop = do_ref[pl.ds(r0, CHUNK), ls]
                k2 = k_ref[keys, ls]
                v2 = v_ref[keys, ls]
                tiles.append((ls, k2))
                for _, hm in sides:
                    qh = jnp.where(hm, qp, zero)
                    doh = jnp.where(hm, dop, zero)
                    heads.append((qh, doh, _dot_nt(k2, qh), _dot_nt(v2, doh)))
            return tiles, heads

        def second(rr, b, heads, both):
            r0, _ = _attn_rows(b)
            ok = jnp.concatenate([m_cur, m_prev], axis=0) if both else m_prev
            lanes = slice(rr * HEAD_W, (rr + 1) * HEAD_W)
            lse_t = lse_ref[pl.ds(r0, CHUNK), lanes].T
            dl_t = dl_ref[pl.ds(r0, CHUNK), lanes].T
            out = []
            for i, (_, _, s_t, dp_t) in enumerate(heads):
                p = jnp.exp(jnp.where(ok, s_t - lse_t[16 * i:16 * i + 1, :], NEG))
                out.append((p.astype(BF16), (p * (dp_t - dl_t[16 * i:16 * i + 1, :])).astype(BF16)))
            return out

        def third(rr, b, tiles, heads, probs, both):
            r0, rp = _attn_rows(b)
            nk = 2 * CHUNK if both else CHUNK
            left_k = _lane_left(nk)
            zero_k = jnp.zeros((nk, CHUNK), BF16)
            for hp, (ls, k2) in enumerate(tiles):
                dq = jnp.zeros((CHUNK, CHUNK), F32)
                dk2 = jnp.zeros((nk, CHUNK), F32)
                dv2 = jnp.zeros((nk, CHUNK), F32)
                for side in range(2):
                    qh, doh, _, _ = heads[2 * hp + side]
                    p, ds = probs[2 * hp + side]
                    dq = dq + _dot_tn(ds, jnp.where(left_k if side == 0 else ~left_k, k2, zero_k))
                    dk2 = dk2 + _dot(ds, qh)
                    dv2 = dv2 + _dot(p, doh)
                dq_ref[pl.ds(r0, CHUNK), ls] = dq.astype(dq_ref.dtype)
                if nblk == 1:
                    dk_ref[pl.ds(r0, CHUNK), ls] = dk2.astype(dk_ref.dtype)
                    dv_ref[pl.ds(r0, CHUNK), ls] = dv2.astype(dv_ref.dtype)
                elif both:
                    dk_ref[pl.ds(rp, CHUNK), ls] = (dk_wait[:, ls] + dk2[0:CHUNK]).astype(dk_ref.dtype)
                    dv_ref[pl.ds(rp, CHUNK), ls] = (dv_wait[:, ls] + dv2[0:CHUNK]).astype(dv_ref.dtype)
                    dk_wait[:, ls] = dk2[CHUNK:]
                    dv_wait[:, ls] = dv2[CHUNK:]
                else:
                    dk_wait[:, ls] = dk2
                    dv_wait[:, ls] = dv2

        def run(units, both):
            data = [first(rr, b, both) for rr, b in units]
            probs = [second(rr, b, heads, both) for (rr, b), (_, heads) in zip(units, data)]
            for (rr, b), (tiles, heads), pr in zip(units, data, probs):
                third(rr, b, tiles, heads, pr, both)

        run([(rr, 0) for rr in range(rps)], False)
        if nblk > 1:
            def one(b, carry):
                run([(0, b)], True)
                return carry

            lax.fori_loop(1, nblk, one, 0)
            last = (nblk - 1) * CHUNK
            dk_ref[last:last + CHUNK, :] = dk_wait[...].astype(dk_ref.dtype)
            dv_ref[last:last + CHUNK, :] = dv_wait[...].astype(dv_ref.dtype)

    spec = pl.BlockSpec((seg, rps * ATTN_W), lambda r: (0, r))
    return _pcall(
        body, name=f"attn_bwd_d{dil}", grid=(dil // rps,),
        in_specs=[spec] * 4 + [pl.BlockSpec((seg, rps * HEAD_W), lambda r: (0, r))] * 2, out_specs=[spec] * 3,
        out_shape=[_sds((seg, dil * ATTN_W), BF16)] * 3,
        scratch_shapes=[pltpu.VMEM((CHUNK, ATTN_W), F32), pltpu.VMEM((CHUNK, ATTN_W), F32)],
        args=(qv, kv, vv, dov, deltav, lsev), comms=comms, after=after)


def _inproj_bwd(dqs, dks, dvs, du, dvs_sgu, pos, x, dx1, w_in, g_pre, comms=()):
    def body(dq1, dq2, dq3, dk1, dk2, dk3, dv1, dv2, dv3, du_ref, dvs_ref, pos_ref, x_ref, dx1_ref, w_ref, g_ref,
             dproj_ref, gx_ref, dg_ref, slabs_a, slabs_b):
        @pl.when(pl.program_id(0) == 0)
        def _():
            dg_ref[...] = jnp.zeros_like(dg_ref)

        def total(r1, r2, r3):
            return r1[...] + _load_view(r2, slabs_a, DILATIONS[1]) + _load_view(r3, slabs_b, DILATIONS[2])

        tabs = _rot_tables(pos_ref[...])
        groups = {3: du_ref[...], 4: dvs_ref[...]}
        dh = jnp.zeros((TM, D_MODEL), F32)
        for g in (3, 4, 0, 1, 2):
            if g == 0:
                groups[g] = _rope_bwd(total(dq1, dq2, dq3) * np.float32(ATTN_SCALE), tabs).astype(BF16)
            elif g == 1:
                groups[g] = _rope_bwd(total(dk1, dk2, dk3), tabs).astype(BF16)
            elif g == 2:
                groups[g] = total(dv1, dv2, dv3).astype(BF16)
            dproj_ref[:, 512 * g:512 * (g + 1)] = groups[g]
            off = 0
            for s, a, b in _in_pieces(g):
                dh = dh + _dot_nt(groups[g][:, off:off + b - a], w_ref[s, :, a:b])
                off += b - a
        g = g_ref[...]
        xh, r = _rms_stats(x_ref[...])
        dx, dg = _rms_bwd(xh, r, g, dh)
        dg_ref[...] += dg
        gx_ref[...] = dx1_ref[...] + dx

    return _pcall(
        body, name="inproj_bwd", grid=(SEQ // TM,),
        in_specs=[_view_rows(dil) for dil in DILATIONS] * 3
        + [_rows(512), _rows(512), _rows(1), _rows(D_MODEL), _rows(D_MODEL),
           _resident((N_SHARD, D_MODEL, IN_S)), _resident((1, D_MODEL))],
        out_specs=[_rows(PROJ_W), _rows(D_MODEL), _acc(D_MODEL)],
        out_shape=[_sds((SEQ, PROJ_W), BF16), _sds((SEQ, D_MODEL), F32), _sds((1, D_MODEL), F32)],
        scratch_shapes=[_slab_scratch(), _slab_scratch()],
        args=(*dqs, *dks, *dvs, du, dvs_sgu, pos, x, dx1, w_in, g_pre), comms=comms)


def _coords():
    return lax.axis_index("x"), lax.axis_index("y"), lax.axis_index("c")


def _other_chips(x, y):
    return [(1 - x, y), (x, 1 - y), (1 - x, 1 - y)]


WEIGHTS = ("pre_mix_norm", "w_in", "sgu_ln_gain", "sgu_ln_bias", "sgu_w_spatial", "sgu_b_spatial", "attn_out_norm",
           "sgu_out_norm", "w_out", "post_mix_norm", "pre_ffn_norm", "w_gate", "w_up", "w_down", "post_ffn_norm")
SMALL = ("pre_mix_norm", "post_mix_norm", "pre_ffn_norm", "post_ffn_norm", "sgu_ln_gain", "sgu_ln_bias",
         "attn_out_norm", "sgu_out_norm", "sgu_w_spatial", "sgu_b_spatial")


def _remote(src, dst, send_sem, recv_sem, to):
    return pltpu.make_async_remote_copy(src_ref=src, dst_ref=dst, send_sem=send_sem, recv_sem=recv_sem,
                                        device_id=to, device_id_type=MESH)


def _halves(a):
    *lead, rows, cols = a.shape
    return a.reshape(*lead, 2, rows // 2, cols)


def _gather_ici(shards):
    n = len(shards)

    def desc(ins, outs, ss, rs, j, w, landed):
        x, y, c = _coords()
        cx, cy = _other_chips(x, y)[j]
        shard = 2 * cx + cy if landed else 2 * x + y
        return _remote(ins[w].at[c], outs[w].at[shard, c], ss.at[j * n + w], rs.at[j * n + w], (cx, cy, c))

    def start(ins, outs, ss, rs):
        for j in range(3):
            for w in range(n):
                desc(ins, outs, ss, rs, j, w, False).start()

    def finish(ins, outs, ss, rs):
        for j in range(3):
            for w in range(n):
                desc(ins, outs, ss, rs, j, w, True).wait_recv()
                desc(ins, outs, ss, rs, j, w, False).wait_send()

    return _Comm(shards, [_sds((N_SHARD,) + s.shape, s.dtype) for s in shards], 3 * n, start, finish)


def _gather_pass(fulls):
    n = len(fulls)

    def desc(bufs, ss, rs, j, w, landed):
        x, y, c = _coords()
        cx, cy = _other_chips(x, y)[j]
        shard = 2 * cx + cy
        return _remote(bufs[w].at[shard, c], bufs[w].at[shard, 1 - c if landed else c],
                       ss.at[j * n + w], rs.at[j * n + w], (x, y, 1 - c))

    def start(ins, outs, ss, rs):
        for j in range(3):
            for w in range(n):
                desc(outs, ss, rs, j, w, False).start()

    def finish(ins, outs, ss, rs):
        for j in range(3):
            for w in range(n):
                desc(outs, ss, rs, j, w, True).wait_recv()
                desc(outs, ss, rs, j, w, False).wait_send()

    return _Comm(fulls, [_sds(f.shape, f.dtype) for f in fulls], 3 * n, start, finish, aliased=True)


def _rs_sibling(gws):
    n = len(gws)

    def desc(ins, outs, ss, rs, w):
        x, y, c = _coords()
        return _remote(ins[w].at[:, 1 - c], outs[w], ss.at[w], rs.at[w], (x, y, 1 - c))

    def start(ins, outs, ss, rs):
        for w in range(n):
            desc(ins, outs, ss, rs, w).start()

    def finish(ins, outs, ss, rs):
        for w in range(n):
            desc(ins, outs, ss, rs, w).wait()

    out_shape = [_sds((N_SHARD, g.shape[1] // 2, g.shape[2]), g.dtype) for g in gws]
    return _Comm([_halves(g) for g in gws], out_shape, n, start, finish)


def _rs_chips(pbs):
    n = len(pbs)

    def desc(ins, outs, ss, rs, j, w):
        x, y, c = _coords()
        cx, cy = _other_chips(x, y)[j]
        return _remote(ins[w].at[2 * cx + cy], outs[w].at[j], ss.at[j * n + w], rs.at[j * n + w], (cx, cy, c))

    def start(ins, outs, ss, rs):
        for j in range(3):
            for w in range(n):
                desc(ins, outs, ss, rs, j, w).start()

    def finish(ins, outs, ss, rs):
        for j in range(3):
            for w in range(n):
                desc(ins, outs, ss, rs, j, w).wait()

    return _Comm(pbs, [_sds((3,) + p.shape[1:], p.dtype) for p in pbs], 3 * n, start, finish)


def _rs_join(halves):
    n = len(halves)

    def desc(bufs, ss, rs, w, landed):
        x, y, c = _coords()
        return _remote(bufs[w].at[c], bufs[w].at[1 - c if landed else c], ss.at[w], rs.at[w], (x, y, 1 - c))

    def start(ins, outs, ss, rs):
        for w in range(n):
            desc(outs, ss, rs, w, False).start()

    def finish(ins, outs, ss, rs):
        for w in range(n):
            desc(outs, ss, rs, w, True).wait_recv()
            desc(outs, ss, rs, w, False).wait_send()

    return _Comm(halves, [_sds(h.shape, h.dtype) for h in halves], n, start, finish, aliased=True)


def _small_exchange(buf):
    def desc(ins, outs, ss, rs, k, landed):
        x, y, c = _coords()
        px = 1 - x if (k >> 2) & 1 else x
        py = 1 - y if (k >> 1) & 1 else y
        pc = 1 - c if k & 1 else c
        slot = 4 * px + 2 * py + pc if landed else 4 * x + 2 * y + c
        return _remote(ins[0], outs[0].at[slot], ss.at[k - 1], rs.at[k - 1], (px, py, pc))

    def start(ins, outs, ss, rs):
        for k in range(1, 8):
            desc(ins, outs, ss, rs, k, False).start()

    def finish(ins, outs, ss, rs):
        for k in range(1, 8):
            desc(ins, outs, ss, rs, k, True).wait_recv()
            desc(ins, outs, ss, rs, k, False).wait_send()

    return _Comm([buf], [_sds((8,) + buf.shape, buf.dtype)], 7, start, finish)


HBM = pl.BlockSpec(memory_space=pltpu.HBM)
SEM = pl.BlockSpec(memory_space=pltpu.SEMAPHORE)
DATAFLOW = pltpu.SideEffectType.DATAFLOW_SIDE_EFFECTING


def _split_starts(name, comms, after):
    srcs = [[pltpu.with_memory_space_constraint(s, pltpu.HBM) for s in c.args] for c in comms]
    lands = [[pltpu.with_memory_space_constraint(lax.empty(o.shape, o.dtype), pltpu.HBM) for o in c.out_shape]
             for c in comms]
    bufs = [b for k in range(len(comms)) for b in srcs[k] + lands[k]]
    nb, nc = len(bufs), len(comms)

    def body(*refs):
        sems = refs[nb + 1:nb + 1 + 2 * nc]
        off = 0
        for k, c in enumerate(comms):
            ns, nl = len(srcs[k]), len(lands[k])
            c.start(refs[off:off + ns], refs[off + ns:off + ns + nl], sems[2 * k], sems[2 * k + 1])
            off += ns + nl
        refs[-1][...] = jnp.zeros_like(refs[-1])

    res = pl.pallas_call(
        body, name=name,
        out_shape=(*[pltpu.SemaphoreType.DMA((c.n_sems,)) for c in comms for _ in range(2)],
                   *[pltpu.HBM(b.shape, b.dtype) for b in bufs], _sds((8, 128), F32)),
        in_specs=[HBM] * nb + [ANY],
        out_specs=(*[SEM] * (2 * nc), *[HBM] * nb, pl.BlockSpec(memory_space=pltpu.VMEM)),
        input_output_aliases={i: 2 * nc + i for i in range(nb)},
        compiler_params=pltpu.CompilerParams(has_side_effects=DATAFLOW),
    )(*bufs, after)
    states, off = [], 2 * nc
    for k in range(nc):
        ns, nl = len(srcs[k]), len(lands[k])
        states.append((res[2 * k], res[2 * k + 1], list(res[off:off + ns]), list(res[off + ns:off + ns + nl])))
        off += ns + nl
    return states, res[-1]


def _split_wait(name, comm, send_sems, recv_sems, srcs, lands, after):
    ns, nb = len(srcs), len(lands)
    after = list(after) if isinstance(after, (list, tuple)) else [after]

    def body(*refs):
        comm.finish(refs[:ns], refs[ns:ns + nb], refs[ns + nb], refs[ns + nb + 1])

    res = pl.pallas_call(
        body, name=name,
        out_shape=tuple(pltpu.HBM(b.shape, b.dtype) for b in srcs + lands),
        in_specs=[HBM] * (ns + nb) + [SEM, SEM] + [ANY] * len(after), out_specs=tuple([HBM] * (ns + nb)),
        input_output_aliases={i: i for i in range(ns + nb)},
        compiler_params=pltpu.CompilerParams(has_side_effects=DATAFLOW),
    )(*srcs, *lands, send_sems, recv_sems, *after)
    return list(res[ns:])


LOSS_ROW = "loss_cols"
SMALL_EARLY = ("post_mix_norm", "pre_ffn_norm", "post_ffn_norm", "sgu_ln_gain", "sgu_ln_bias", "attn_out_norm",
               "sgu_out_norm", "sgu_b_spatial", LOSS_ROW)
SMALL_LATE = ("pre_mix_norm",)
SMALL_WSP = "sgu_w_spatial"


def _pack(d, names, rows):
    flat = [d[n].reshape(-1) for n in names]
    used = sum(f.shape[0] for f in flat)
    flat.append(jnp.zeros((rows * 1024 - used,), F32))
    return jnp.concatenate(flat).reshape(rows, 1024)


def _unpack(buf, names, shapes):
    flat = buf.reshape(-1)
    out, off = {}, 0
    for n in names:
        size = int(np.prod(shapes[n]))
        out[n] = flat[off:off + size].reshape(shapes[n])
        off += size
    return out


def _chip_sums(gbs, recvs, shard_core):
    n = len(gbs)
    _, rows, cols = gbs[0].shape
    hw = rows // 2

    def body(sc_ref, *refs):
        for k in range(n):
            refs[2 * n + k][...] = (refs[k][...].astype(F32) + refs[n + k][...].astype(F32)).astype(BF16)

    def other(s, sc):
        return jnp.where(s >= sc[0], s + 1, s)

    mine = pl.BlockSpec((1, hw, cols), lambda s, sc: (other(s, sc), sc[1], 0))
    plain = pl.BlockSpec((1, hw, cols), lambda s, sc: (other(s, sc), 0, 0))
    return pl.pallas_call(
        body, name="rs_chip_sums",
        grid_spec=pltpu.PrefetchScalarGridSpec(num_scalar_prefetch=1, grid=(N_SHARD - 1,),
                                               in_specs=[mine] * n + [plain] * n, out_specs=[plain] * n),
        out_shape=[_sds((N_SHARD, hw, cols), BF16)] * n,
        compiler_params=_seq_params(),
    )(shard_core, *gbs, *recvs)


def _final_sums(gws, recv_sibs, recv_chips, shard_core):
    n = len(gws)
    halves = [(g.shape[1] // 2, g.shape[2]) for g in gws]

    def body(sc_ref, *refs):
        for k in range(n):
            acc = refs[k][0] + refs[n + k][0]
            for j in range(3):
                acc = acc + refs[2 * n + k][j].astype(F32)
            refs[3 * n + k][0] = acc

    def specs(lead, index):
        return [pl.BlockSpec((lead, hw, cols), index) for hw, cols in halves]

    return pl.pallas_call(
        body, name="rs_final_sums",
        grid_spec=pltpu.PrefetchScalarGridSpec(
            num_scalar_prefetch=1, grid=(1,),
            in_specs=specs(1, lambda i, sc: (sc[0], sc[1], 0)) + specs(1, lambda i, sc: (sc[0], 0, 0))
            + specs(3, lambda i, sc: (0, 0, 0)),
            out_specs=specs(1, lambda i, sc: (sc[1], 0, 0))),
        out_shape=[_sds((2, hw, cols), F32) for hw, cols in halves],
        compiler_params=_seq_params(),
    )(shard_core, *gws, *recv_sibs, *recv_chips)


ADAM_BLOCKS = 4


def _adamw_multi(ws, gs, ms, vs, name, after=()):
    n = len(ws)

    def body(*refs):
        outs = refs[4 * n + len(after):]
        for k in range(n):
            g = refs[n + k][...]
            d, m, v = _adam_math(refs[k][...], g, refs[2 * n + k][...], refs[3 * n + k][...])
            outs[4 * k][...], outs[4 * k + 1][...], outs[4 * k + 2][...], outs[4 * k + 3][...] = g, d, m, v

    specs = [pl.BlockSpec((w.shape[0] // ADAM_BLOCKS, w.shape[1]), lambda i: (i, 0)) for w in ws]
    res = pl.pallas_call(
        body, name=name, grid=(ADAM_BLOCKS,), in_specs=specs * 4 + [ANY] * len(after),
        out_specs=[s for s in specs for _ in range(4)],
        out_shape=[_sds(w.shape, F32) for w in ws for _ in range(4)],
        compiler_params=_seq_params(),
    )(*ws, *gs, *ms, *vs, *after)
    return [tuple(res[4 * k:4 * k + 4]) for k in range(n)]


def _wgrad_ff(a_list, b, name):
    n = len(a_list)
    cols = 256

    def body(*refs):
        bv = refs[n][...]
        for k in range(n):
            g = _dot_tn(refs[k][...], bv)
            refs[n + 1 + k][...] = g
            refs[2 * n + 1 + k][...] = g.astype(BF16)

    res = _pcall(
        body, name=name, grid=(FF // cols,),
        in_specs=[pl.BlockSpec((SEQ, cols), lambda j: (0, j))] * n
        + [pl.BlockSpec((SEQ, D_MODEL), lambda j: (0, 0), pipeline_mode=pl.Buffered(1))],
        out_specs=[pl.BlockSpec((cols, D_MODEL), lambda j: (j, 0))] * (2 * n),
        out_shape=[_sds((FF, D_MODEL), F32)] * n + [_sds((FF, D_MODEL), BF16)] * n, args=(*a_list, b))
    return [m.reshape(N_SHARD, FF_S, D_MODEL) for m in res]


def _comm_only(name, comms):
    return _pcall(lambda: None, name=name, grid=(1,), in_specs=[], out_specs=[], out_shape=[], args=(),
                  comms=comms)[1]


def _adam_math(w, g, m, v):
    m = ADAM_B1 * m + (1.0 - ADAM_B1) * g
    v = ADAM_B2 * v + (1.0 - ADAM_B2) * (g * g)
    m_hat = m / (1.0 - ADAM_B1 ** ADAM_STEP)
    v_hat = v / (1.0 - ADAM_B2 ** ADAM_STEP)
    return -ADAM_LR * (m_hat / (jnp.sqrt(v_hat) + ADAM_EPS) + ADAM_WD * w), m, v


def _adamw_small(own, slots, w, m, v, me):
    rows, cols = own.shape

    def body(me_ref, own_ref, slots_ref, w_ref, m_ref, v_ref, g_ref, d_ref, nm_ref, nv_ref):
        own_v = own_ref[...].astype(F32)
        g = jnp.where(me_ref[0] == 0, own_v, slots_ref[0].astype(F32))
        for i in range(1, 8):
            g = g + jnp.where(me_ref[0] == i, own_v, slots_ref[i].astype(F32))
        g_ref[...] = g
        d_ref[...], nm_ref[...], nv_ref[...] = _adam_math(w_ref[...], g, m_ref[...], v_ref[...])

    flat = pl.BlockSpec((rows, cols), lambda i, me_ref: (0, 0))
    return pl.pallas_call(
        body, name="adamw_small",
        grid_spec=pltpu.PrefetchScalarGridSpec(
            num_scalar_prefetch=1, grid=(1,),
            in_specs=[flat, pl.BlockSpec((8, rows, cols), lambda i, me_ref: (0, 0, 0)), flat, flat, flat],
            out_specs=[flat] * 4),
        out_shape=[_sds((rows, cols), F32)] * 4,
        compiler_params=_seq_params(),
    )(me, own, slots, w, m, v)


def kernel(x, positions, pre_mix_norm, w_in, sgu_ln_gain, sgu_ln_bias, sgu_w_spatial, sgu_b_spatial, attn_out_norm, sgu_out_norm, w_out, post_mix_norm, pre_ffn_norm, w_gate, w_up, w_down, post_ffn_norm, loss_target, m_pre_mix_norm, m_w_in, m_sgu_ln_gain, m_sgu_ln_bias, m_sgu_w_spatial, m_sgu_b_spatial, m_attn_out_norm, m_sgu_out_norm, m_w_out, m_post_mix_norm, m_pre_ffn_norm, m_w_gate, m_w_up, m_w_down, m_post_ffn_norm, v_pre_mix_norm, v_w_in, v_sgu_ln_gain, v_sgu_ln_bias, v_sgu_w_spatial, v_sgu_b_spatial, v_attn_out_norm, v_sgu_out_norm, v_w_out, v_post_mix_norm, v_pre_ffn_norm, v_w_gate, v_w_up, v_w_down, v_post_ffn_norm):
    a = dict(locals())
    cx, cy, cc = _coords()
    s_me = 2 * cx + cy
    shard_core = jnp.stack([s_me, cc]).astype(jnp.int32)
    me = jnp.stack([4 * cx + 2 * cy + cc]).astype(jnp.int32)
    small = {n: (a[n][0] if a[n].ndim > 2 else a[n]) for n in SMALL}
    b_t = small["sgu_b_spatial"].T
    xs, pos, target = x[0], positions.reshape(SEQ, 1), loss_target[0]
    flipped = ("w_gate", "w_up")

    def big(name, n):
        return jnp.swapaxes(a[name], 1, 2)[0] if n in flipped else a[name][0]

    own = {"w_in": _halves(big("w_in", "w_in").astype(BF16))}

    def with_own(full, n):
        full = lax.dynamic_update_slice(full, own[n][None], (s_me, 0, 0, 0))
        return full.reshape((N_SHARD,) + big(n, n).shape)

    ffn = ("w_gate", "w_up", "w_down")
    g_in = _gather_ici([own["w_in"]])
    (s_in,), token = _split_starts("gather_in_start", [g_in], small["pre_mix_norm"])
    for n in ("w_out",) + ffn:
        own[n] = _halves((big(n, n) + token[0:1, 0:1]).astype(BF16))
    g_out, g_ffn = _gather_ici([own["w_out"]]), _gather_ici([own[n] for n in ffn])
    (s_out, s_ffn), token = _split_starts("gather_rest_start", [g_out, g_ffn], token)
    in_lands = _split_wait("gather_in_wait", g_in, *s_in, token)
    ((in_lands,),) = _comm_only("comm_pass_in", [_gather_pass(in_lands)])
    w_in_f = with_own(in_lands, "w_in")
    h, u, vs, sgu, *qkv = _inproj_fwd(xs, pos, small["pre_mix_norm"], w_in_f, small["sgu_ln_gain"],
                                      small["sgu_ln_bias"], small["sgu_w_spatial"], b_t)
    views = [tuple(qkv[3 * i:3 * i + 3]) for i in range(len(DILATIONS))]
    out_lands = _split_wait("gather_out_wait", g_out, *s_out, sgu)
    o_list, l_list = [], []
    for dil, (qv, kv, vv) in zip(DILATIONS, views):
        if dil == 1:
            (o, l), ((out_lands,),) = _attn_fwd(qv, kv, vv, dil, comms=[_gather_pass(out_lands)])
        else:
            o, l = _attn_fwd(qv, kv, vv, dil)
        o_list.append(o)
        l_list.append(l)
    w_out_f = with_own(out_lands, "w_out")
    ffn_lands = _split_wait("gather_ffn_wait", g_ffn, *s_ffn, l_list[-1])
    (attn, mixed, y, x1, *lses), (ffn_lands,) = _mix_out_fwd(
        o_list, l_list, sgu, xs, w_out_f, small["attn_out_norm"], small["sgu_out_norm"], small["post_mix_norm"],
        comms=[_gather_pass(ffn_lands)])
    w_gate_f, w_up_f, w_down_f = (with_own(f, n).reshape(FF, D_MODEL) for f, n in zip(ffn_lands, ffn))
    h2, act, dg, dup, df, dx1, loss_cols, d_pre_ffn, d_post_ffn = _ffn_fwd_bwd(
        x1, target, w_gate_f, w_up_f, w_down_f, small["pre_ffn_norm"], small["post_ffn_norm"])

    full_tok = pl.BlockSpec((SEQ, D_MODEL), lambda s: (0, 0), pipeline_mode=pl.Buffered(1))
    gw, gb = {}, {}
    gw["w_gate"], gw["w_up"], gb["w_gate"], gb["w_up"] = _wgrad_ff([dg, dup], h2, "wgrad_gate_up")
    gw["w_down"], gb["w_down"] = _wgrad_ff([act], df, "wgrad_down")
    (dy, du, dvs_sgu, d_post_mix, d_attn_norm, d_sgu_norm, d_w_sp, d_b_sp, d_ln_gain, d_ln_bias,
     *dviews), (sib_ffn,) = _outproj_bwd(
        dx1, y, attn, sgu, w_out_f, small["post_mix_norm"], small["attn_out_norm"], small["sgu_out_norm"],
        u, vs, small["sgu_ln_gain"], small["sgu_ln_bias"], small["sgu_w_spatial"], b_t,
        comms=[_rs_sibling([gb[n] for n in ffn])])
    sib = dict(zip(ffn, sib_ffn))
    part = dict(zip(ffn, _chip_sums([gb[n] for n in ffn], [sib[n] for n in ffn], shard_core)))
    gw["w_out"], gb["w_out"] = _wgrad(mixed, dy, pl.BlockSpec((SEQ, OUT_S), lambda s: (0, s)), full_tok,
                                      (OUT_S, D_MODEL), "wgrad_out")
    x_ffn = _rs_chips([part[n] for n in ffn])
    packed_early = _pack({
        "sgu_ln_gain": d_ln_gain, "sgu_ln_bias": d_ln_bias, "sgu_b_spatial": d_b_sp,
        "attn_out_norm": d_attn_norm, "sgu_out_norm": d_sgu_norm, "post_mix_norm": d_post_mix,
        "pre_ffn_norm": d_pre_ffn, "post_ffn_norm": d_post_ffn, LOSS_ROW: loss_cols}, SMALL_EARLY, 8)
    wsp_view = (SGU_GROUPS * CHUNK, CHUNK)
    packed_wsp = d_w_sp.reshape(wsp_view).astype(BF16)
    x_small, x_wsp = _small_exchange(packed_early), _small_exchange(packed_wsp)
    (s_ffn,), token = _split_starts("rs_ffn_start", [x_ffn], small["pre_mix_norm"])

    dqs, dks, dvs = [], [], []
    for i, (dil, (qv, kv, vv)) in enumerate(zip(DILATIONS, views)):
        if dil == 1:
            (dq, dk, dv), ((sib["w_out"],),) = _attn_bwd(qv, kv, vv, dviews[i], dviews[3 + i], lses[i], dil,
                                                        comms=[_rs_sibling([gb["w_out"]])], after=[token])
            (part["w_out"],) = _chip_sums([gb["w_out"]], [sib["w_out"]], shard_core)
            x_out = _rs_chips([part["w_out"]])
            (s_out, s_small, s_wsp), token = _split_starts("rs_out_small_start", [x_out, x_small, x_wsp], token)
        else:
            dq, dk, dv = _attn_bwd(qv, kv, vv, dviews[i], dviews[3 + i], lses[i], dil, after=[token])
        dqs.append(dq)
        dks.append(dk)
        dvs.append(dv)
    half, far, joined = {}, {}, {}
    far.update(zip(ffn, _split_wait("rs_ffn_wait", x_ffn, *s_ffn, dvs[-1])))
    half.update(zip(ffn, _final_sums([gw[n] for n in ffn], [sib[n] for n in ffn], [far[n] for n in ffn],
                                     shard_core)))
    dproj, grad_x, d_pre_mix = _inproj_bwd(dqs, dks, dvs, du, dvs_sgu, pos, xs, dx1, w_in_f, small["pre_mix_norm"])
    packed_late = _pack({"pre_mix_norm": d_pre_mix}, SMALL_LATE, 8)
    (gw["w_in"], gb["w_in"]), (got,) = _wgrad(
        h, dproj, full_tok, pl.BlockSpec((SEQ, IN_S), lambda s: (0, s)), (D_MODEL, IN_S), "wgrad_in",
        comms=[_rs_join([half[n] for n in ffn])])
    joined.update(zip(ffn, got))
    ((sib["w_in"],),) = _comm_only("comm_rs_sibling_in", [_rs_sibling([gb["w_in"]])])
    (part["w_in"],) = _chip_sums([gb["w_in"]], [sib["w_in"]], shard_core)
    x_in, x_late = _rs_chips([part["w_in"]]), _small_exchange(packed_late)
    (s_in, s_late), token = _split_starts("rs_in_start", [x_in, x_late], small["pre_mix_norm"])

    grads, deltas, new_m, new_v = {}, {}, {}, {}

    def record(n, outs):
        grads[n], deltas[n], new_m[n], new_v[n] = (
            jnp.swapaxes(o[None], 1, 2) if n in flipped else o[None] for o in outs)

    def update(names, name, after):
        for n, outs in zip(names, _adamw_multi(
                [big(n, n) for n in names], [joined[n].reshape(big(n, n).shape) for n in names],
                [big("m_" + n, n) for n in names], [big("v_" + n, n) for n in names], name, after)):
            record(n, outs)

    update(ffn, "adamw_ffn", [token])
    (far["w_out"],) = _split_wait("rs_out_wait", x_out, *s_out, new_v["w_down"])
    (slots_early,) = _split_wait("small_early_wait", x_small, *s_small, far["w_out"])
    (slots_wsp,) = _split_wait("small_wsp_wait", x_wsp, *s_wsp, slots_early)
    (far["w_in"],) = _split_wait("rs_in_wait", x_in, *s_in, slots_wsp)
    (slots_late,) = _split_wait("small_late_wait", x_late, *s_late, far["w_in"])
    last = ("w_in", "w_out")
    half.update(zip(last, _final_sums([gw[n] for n in last], [sib[n] for n in last], [far[n] for n in last],
                                      shard_core)))
    (got,) = _comm_only("comm_rs_join_in_out", [_rs_join([half[n] for n in last])])
    joined.update(zip(last, got))
    update(last, "adamw_in_out", [])
    a[LOSS_ROW] = a["m_" + LOSS_ROW] = a["v_" + LOSS_ROW] = jnp.zeros((1, D_MODEL), F32)
    outs = _adamw_small(packed_wsp, slots_wsp, *[a[p + SMALL_WSP].reshape(wsp_view) for p in ("", "m_", "v_")], me)
    for dst, buf in zip((grads, deltas, new_m, new_v), outs):
        dst[SMALL_WSP] = buf.reshape(a[SMALL_WSP].shape)
    for names, rows, packed, slots in ((SMALL_EARLY, 8, packed_early, slots_early),
                                       (SMALL_LATE, 8, packed_late, slots_late)):
        outs = _adamw_small(packed, slots, _pack(a, names, rows), _pack({n: a["m_" + n] for n in names}, names, rows),
                            _pack({n: a["v_" + n] for n in names}, names, rows), me)
        for dst, buf in zip((grads, deltas, new_m, new_v), outs):
            dst.update(_unpack(buf, names, {n: a[n].shape for n in names}))
    loss = jnp.sum(grads[LOSS_ROW]) * np.float32(0.5 / D_MODEL)
    return (loss, grad_x[None], *[grads[n] for n in WEIGHTS], *[deltas[n] for n in WEIGHTS],
            *[new_m[n] for n in WEIGHTS], *[new_v[n] for n in WEIGHTS])
```

```python
import numpy as np
import jax
import jax.numpy as jnp
from jax import lax
from jax.experimental import pallas as pl
from jax.experimental.pallas import tpu as pltpu

F32 = jnp.float32
BF16 = jnp.bfloat16

SEQ = 2048
D_MODEL = 1024
HEAD_DIM = 64
ATTN_W = 512
SGU_W = 512
SGU_GROUPS = 8
CHUNK = 128
DILATIONS = (1, 4, 16)
N_SHARD = 4
IN_S = 640
OUT_S = 256
FF_S = 704
PROJ_W = N_SHARD * IN_S
FF = N_SHARD * FF_S
FF_CHUNKS = ((0, 1024), (1024, 2048), (2048, FF))
RMS_EPS = 1e-6
LN_EPS = 1e-5
ROPE_THETA = 500000.0
ATTN_SCALE = 1.0 / np.sqrt(HEAD_DIM)
NEG = -1e30
TM = 512
TM_FFN = 256
VMEM_LIMIT = 56 * 1024 * 1024

ADAM_LR = 0.001
ADAM_B1 = 0.9
ADAM_B2 = 0.999
ADAM_EPS = 1e-08
ADAM_WD = 0.01
ADAM_STEP = 10

MESH = pl.DeviceIdType.MESH
ANY = pl.BlockSpec(memory_space=pl.ANY)


def _dot(a, b):
    return jnp.dot(a, b, preferred_element_type=F32)


def _dot_nt(a, b):
    return lax.dot_general(a, b, (((1,), (1,)), ((), ())), preferred_element_type=F32)


def _dot_tn(a, b):
    return lax.dot_general(a, b, (((0,), (0,)), ((), ())), preferred_element_type=F32)


def _dot_exact(a, b):
    return jnp.dot(a, b, preferred_element_type=F32, precision=lax.Precision.HIGHEST)


def _dot_select(a, sel):
    hi = a.astype(BF16)
    lo = (a - hi.astype(F32)).astype(BF16)
    sel = sel.astype(BF16)
    return _dot(hi, sel) + _dot(lo, sel)


def _rms_stats(x):
    r = lax.rsqrt(jnp.mean(x * x, axis=-1, keepdims=True) + RMS_EPS)
    return x * r, r


def _rms_bwd(xh, r, gain, dy):
    dxh = dy * gain
    dx = r * (dxh - xh * jnp.mean(dxh * xh, axis=-1, keepdims=True))
    return dx, jnp.sum(dy * xh, axis=0, keepdims=True)


_ERF_ALPHA = (-2.72614225801306e-10, 2.77068142495902e-08, -2.10102402082508e-06, -5.69250639462346e-05,
              -7.34990630326855e-04, -2.95459980854025e-03, -1.60960333262415e-02)
_ERF_BETA = (-1.45660718464996e-05, -2.13374055278905e-04, -1.68282697438203e-03, -7.37332916720468e-03,
             -1.42647390514189e-02)


def _erf(x):
    x = jnp.clip(x, -4.0, 4.0)
    x2 = x * x
    p = jnp.full_like(x, _ERF_ALPHA[0])
    for a in _ERF_ALPHA[1:]:
        p = p * x2 + a
    q = jnp.full_like(x, _ERF_BETA[0])
    for b in _ERF_BETA[1:]:
        q = q * x2 + b
    return x * p / q


def _normal_cdf(x):
    return 0.5 * (1.0 + _erf(x * np.float32(1.0 / np.sqrt(2.0))))


def _gelu_grad(x, cdf):
    pdf = jnp.exp(-0.5 * x * x) * np.float32(1.0 / np.sqrt(2.0 * np.pi))
    return cdf + x * pdf


def _sigmoid(x):
    return 1.0 / (1.0 + jnp.exp(-x))


_INV_FREQ = tuple(float(np.float32(ROPE_THETA ** (-2.0 * j / 16.0))) for j in range(8))


def _rot_tables(pos):
    lane = lax.broadcasted_iota(jnp.int32, (1, 128), 1)
    d = lane & 63
    j = d & 7
    inv = jnp.zeros((1, 128), F32)
    for jj in range(8):
        inv = jnp.where(j == jj, _INV_FREQ[jj], inv)
    ang = pos.astype(F32) * inv
    c = jnp.cos(ang)
    s = jnp.sin(ang)
    cos_t = jnp.where(d < 16, c, 1.0)
    sin_a = jnp.where(d < 8, -s, 0.0)
    sin_b = jnp.where((d >= 8) & (d < 16), s, 0.0)
    return tuple(jnp.tile(t, (1, 4)) for t in (cos_t, sin_a, sin_b))


def _rope(x, tabs):
    cos_t, sin_a, sin_b = tabs
    return x * cos_t + pltpu.roll(x, 504, 1) * sin_a + pltpu.roll(x, 8, 1) * sin_b


def _rope_bwd(dy, tabs):
    cos_t, sin_a, sin_b = tabs
    return dy * cos_t + pltpu.roll(dy * sin_a, 8, 1) + pltpu.roll(dy * sin_b, 504, 1)


def _left_half():
    return lax.broadcasted_iota(jnp.int32, (CHUNK, CHUNK), 1) < HEAD_DIM


def _group_ones():
    lane = lax.broadcasted_iota(jnp.int32, (SGU_GROUPS, SGU_W), 1)
    row = lax.broadcasted_iota(jnp.int32, (SGU_GROUPS, SGU_W), 0)
    return ((lane >> 6) == row).astype(F32)


def _masked_spatial(w_ref):
    row = lax.broadcasted_iota(jnp.int32, (CHUNK, CHUNK), 0)
    col = lax.broadcasted_iota(jnp.int32, (CHUNK, CHUNK), 1)
    return [jnp.where(col <= row, w_ref[g], 0.0).astype(BF16) for g in range(SGU_GROUPS)]


def _sgu_core(u, vs, lg, lb, wm, bias_full):
    tm = u.shape[0]
    cdf_u, cdf_vs = _normal_cdf(u), _normal_cdf(vs)
    gu = u * cdf_u
    gv = vs * cdf_vs
    mu = jnp.mean(gv, axis=-1, keepdims=True)
    xc = gv - mu
    rstd = lax.rsqrt(jnp.mean(xc * xc, axis=-1, keepdims=True) + LN_EPS)
    xh = xc * rstd
    vnb = (xh * lg + lb).astype(BF16)
    left = _left_half()
    rows = []
    for c in range(tm // CHUNK):
        pieces = []
        for p in range(4):
            vp = vnb[c * CHUNK:(c + 1) * CHUNK, p * 128:(p + 1) * 128]
            pieces.append(jnp.where(left, _dot(wm[2 * p], vp), _dot(wm[2 * p + 1], vp)))
        rows.append(jnp.concatenate(pieces, axis=1) + bias_full)
    mixed = jnp.concatenate(rows, axis=0)
    return gu, xh, rstd, vnb, mixed, cdf_u, cdf_vs


def _resident(shape):
    n = len(shape)
    return pl.BlockSpec(shape, lambda *_: (0,) * n, pipeline_mode=pl.Buffered(1))


def _rows(ncol, tm=TM):
    return pl.BlockSpec((tm, ncol), lambda i: (i, 0))


def _acc(ncol, nrow=1):
    return pl.BlockSpec((nrow, ncol), lambda i: (0, 0))


HEAD_W = 128


def _view_rows(dil, width=ATTN_W, tm=TM):
    return pl.BlockSpec((tm // dil, dil * width), lambda i: (i, 0))


def _view_shape(dil, dtype, width=ATTN_W):
    return _sds((SEQ // dil, dil * width), dtype)


def _slab_scratch():
    return pltpu.VMEM((4, TM, 128), F32)


def _store_view(val, out_ref, slabs, dil):
    width = val.shape[1]
    for j in range(width // 128):
        slabs[j] = val[:, j * 128:(j + 1) * 128]
    for r in range(dil):
        for j in range(width // 128):
            c0 = r * width + j * 128
            out_ref[:, c0:c0 + 128] = slabs.at[j][pl.ds(r, TM // dil, stride=dil), :].astype(out_ref.dtype)


def _load_view(in_ref, slabs, dil, width=ATTN_W):
    for r in range(dil):
        for j in range(width // 128):
            c0 = r * width + j * 128
            slabs.at[j][pl.ds(r, TM // dil, stride=dil), :] = in_ref[:, c0:c0 + 128].astype(F32)
    return jnp.concatenate([slabs[j] for j in range(width // 128)], axis=1)


def _head_spread():
    m = lax.broadcasted_iota(jnp.int32, (HEAD_W, ATTN_W), 0)
    lane = lax.broadcasted_iota(jnp.int32, (HEAD_W, ATTN_W), 1)
    return (m == 16 * (lane >> 6)).astype(F32)


def _head_sum():
    lane = lax.broadcasted_iota(jnp.int32, (ATTN_W, HEAD_W), 0)
    m = lax.broadcasted_iota(jnp.int32, (ATTN_W, HEAD_W), 1)
    return ((lane >> 6) == (m >> 4)).astype(F32)


def _seq_params():
    return pltpu.CompilerParams(dimension_semantics=("arbitrary",), vmem_limit_bytes=VMEM_LIMIT)


def _sds(shape, dtype):
    return jax.ShapeDtypeStruct(shape, dtype)


class _Comm:
    def __init__(self, args, out_shape, n_sems, start, finish, aliased=False, landing=None):
        self.args, self.out_shape, self.n_sems = list(args), list(out_shape), n_sems
        self.start, self.finish, self.aliased, self.landing = start, finish, aliased, landing


def _pcall(body, *, name, grid, in_specs, out_specs, out_shape, args, scratch_shapes=(), comms=(), after=()):
    single = not isinstance(out_shape, (list, tuple))
    out_specs = [out_specs] if single else list(out_specs)
    out_shape = [out_shape] if single else list(out_shape)
    n_in, n_out, n_scr = len(in_specs), len(out_shape), len(scratch_shapes)
    c_args = [a for c in comms for a in c.args]
    c_outs = [o for c in comms for o in c.out_shape]
    aliases, ai, ao = {}, n_in, n_out
    for c in comms:
        if c.aliased:
            aliases.update({ai + k: ao + k for k in range(len(c.args))})
        ai += len(c.args)
        ao += len(c.out_shape)
    sems = [pltpu.SemaphoreType.DMA((c.n_sems,)) for c in comms for _ in range(2)]
    steps = grid[0]

    def wrapped(*refs):
        o0 = n_in + len(c_args) + len(after)
        s0 = o0 + n_out + len(c_outs)
        m_in, m_out, m_sem = refs[n_in:n_in + len(c_args)], refs[o0 + n_out:s0], refs[s0 + n_scr:]

        def each(phase):
            ii = oi = 0
            for k, c in enumerate(comms):
                getattr(c, phase)(m_in[ii:ii + len(c.args)], m_out[oi:oi + len(c.out_shape)],
                                  m_sem[2 * k], m_sem[2 * k + 1])
                ii += len(c.args)
                oi += len(c.out_shape)

        if comms:
            @pl.when(pl.program_id(0) == 0)
            def _():
                each("start")

        body(*refs[:n_in], *refs[o0:o0 + n_out], *refs[s0:s0 + n_scr])

        if comms:
            @pl.when(pl.program_id(0) == steps - 1)
            def _():
                each("finish")

    res = pl.pallas_call(
        wrapped, name=name, grid=grid,
        in_specs=list(in_specs) + [ANY] * (len(c_args) + len(after)), out_specs=out_specs + [ANY] * len(c_outs),
        out_shape=out_shape + c_outs, scratch_shapes=list(scratch_shapes) + sems,
        input_output_aliases=aliases, compiler_params=_seq_params(),
    )(*args, *c_args, *after)
    mine = res[0] if single else list(res[:n_out])
    if not comms:
        return mine
    theirs, oi = [], n_out
    for c in comms:
        theirs.append(list(res[oi:oi + len(c.out_shape)]))
        oi += len(c.out_shape)
    return mine, theirs


def _in_pieces(g):
    lo, hi = 512 * g, 512 * (g + 1)
    return [(s, max(lo, IN_S * s) - IN_S * s, min(hi, IN_S * (s + 1)) - IN_S * s)
            for s in range(N_SHARD) if max(lo, IN_S * s) < min(hi, IN_S * (s + 1))]


def _inproj_fwd(x, pos, g_pre, w_in, lg, lb, w_sp, b_t, comms=()):
    def body(x_ref, pos_ref, g_ref, w_ref, lg_ref, lb_ref, wsp_ref, bt_ref, h_ref, u_ref, vs_ref, sgu_ref, *rest):
        qkv_refs, slabs = rest[:9], rest[9:]
        xh, _ = _rms_stats(x_ref[...])
        h = (xh * g_ref[...]).astype(BF16)
        h_ref[...] = h
        tabs = _rot_tables(pos_ref[...])

        def group(g):
            return jnp.concatenate([_dot(h, w_ref[s, :, a:b]) for s, a, b in _in_pieces(g)], axis=1)

        for t in range(3):
            val = group(t)
            if t < 2:
                val = _rope(val, tabs)
            if t == 0:
                val = val * np.float32(ATTN_SCALE)
            qkv_refs[t][...] = val.astype(BF16)
            for i, dil in enumerate(DILATIONS[1:]):
                _store_view(val, qkv_refs[3 * (i + 1) + t], slabs[t], dil)
        u = group(3)
        vs = group(4)
        u_ref[...] = u
        vs_ref[...] = vs
        bias_full = _dot_exact(bt_ref[...], _group_ones())
        gu, _, _, _, mixed, _, _ = _sgu_core(u, vs, lg_ref[...], lb_ref[...], _masked_spatial(wsp_ref), bias_full)
        sgu_ref[...] = gu * mixed

    return _pcall(
        body, name="inproj_sgu_fwd", grid=(SEQ // TM,),
        in_specs=[_rows(D_MODEL), _rows(1), _resident((1, D_MODEL)), _resident((N_SHARD, D_MODEL, IN_S)),
                  _resident((1, SGU_W)), _resident((1, SGU_W)), _resident((SGU_GROUPS, CHUNK, CHUNK)),
                  _resident((CHUNK, SGU_GROUPS))],
        out_specs=[_rows(D_MODEL), _rows(512), _rows(512), _rows(512)]
        + [_view_rows(dil) for dil in DILATIONS for _ in range(3)],
        out_shape=[_sds((SEQ, D_MODEL), BF16), _sds((SEQ, 512), F32), _sds((SEQ, 512), F32), _sds((SEQ, 512), F32)]
        + [_view_shape(dil, BF16) for dil in DILATIONS for _ in range(3)],
        scratch_shapes=[_slab_scratch() for _ in range(3)],
        args=(x, pos, g_pre, w_in, lg, lb, w_sp, b_t), comms=comms)


def _block_masks():
    row = lax.broadcasted_iota(jnp.int32, (CHUNK, CHUNK), 0)
    col = lax.broadcasted_iota(jnp.int32, (CHUNK, CHUNK), 1)
    return col <= row, col >= row


def _attn_fwd(qv, kv, vv, dil, comms=()):
    seg = SEQ // dil
    nblk = seg // CHUNK
    rps = 4 if nblk == 1 else 1

    def body(q_ref, k_ref, v_ref, o_ref, l_ref):
        left = _left_half()
        m_cur, m_prev = _block_masks()
        zero = jnp.zeros((CHUNK, CHUNK), BF16)
        ones = (jnp.where(left, 1.0, 0.0).astype(BF16), jnp.where(left, 0.0, 1.0).astype(BF16))

        sides = tuple(enumerate((left, ~left)))

        def rows(b):
            if isinstance(b, int):
                return b * CHUNK, max(b - 1, 0) * CHUNK
            return pl.multiple_of(b * CHUNK, CHUNK), pl.multiple_of(jnp.maximum(b - 1, 0) * CHUNK, CHUNK)

        def first(rr, b):
            r0, rp = rows(b)
            prev_ok = m_prev & (b > 0)
            tiles, scores = [], []
            for hp in range(4):
                ls = slice(rr * ATTN_W + hp * 128, rr * ATTN_W + (hp + 1) * 128)
                qp = q_ref[pl.ds(r0, CHUNK), ls]
                kc = k_ref[pl.ds(r0, CHUNK), ls]
                kp = k_ref[pl.ds(rp, CHUNK), ls] if nblk > 1 else None
                tiles.append((ls, v_ref[pl.ds(r0, CHUNK), ls], v_ref[pl.ds(rp, CHUNK), ls] if nblk > 1 else None))
                for _, hm in sides:
                    qh = jnp.where(hm, qp, zero)
                    sc = jnp.where(m_cur, _dot_nt(qh, kc), NEG)
                    sp = jnp.where(prev_ok, _dot_nt(qh, kp), NEG) if nblk > 1 else None
                    scores.append((sc, sp))
            return tiles, scores

        def second(scores):
            probs = []
            for sc, sp in scores:
                if nblk > 1:
                    m = jnp.max(jnp.maximum(sc, sp), axis=-1, keepdims=True)
                    pc = jnp.exp(sc - m)
                    pp = jnp.exp(sp - m)
                    probs.append((m, pc.astype(BF16), pp.astype(BF16), (pc + pp).astype(BF16)))
                else:
                    m = jnp.max(sc, axis=-1, keepdims=True)
                    pc = jnp.exp(sc - m).astype(BF16)
                    probs.append((m, pc, None, pc))
            return probs

        def third(rr, b, tiles, probs):
            r0, _ = rows(b)
            for hp, (ls, vc, vp) in enumerate(tiles):
                acc = jnp.zeros((CHUNK, CHUNK), F32)
                den = jnp.zeros((CHUNK, CHUNK), F32)
                for side, hm in sides:
                    _, pc, pp, psum = probs[2 * hp + side]
                    acc = acc + _dot(pc, jnp.where(hm, vc, zero))
                    if nblk > 1:
                        acc = acc + _dot(pp, jnp.where(hm, vp, zero))
                    den = den + _dot(psum, ones[side])
                o_ref[pl.ds(r0, CHUNK), ls] = (acc / den).astype(o_ref.dtype)
                lse = jnp.where(left, probs[2 * hp][0], probs[2 * hp + 1][0]) + jnp.log(den)
                l_ref[pl.ds(r0, CHUNK), rr * HEAD_W + 32 * hp:rr * HEAD_W + 32 * hp + 32] = lse[:, 48:80]

        def run(units):
            data = [first(rr, b) for rr, b in units]
            probs = [second(scores) for _, scores in data]
            for (rr, b), (tiles, _), pr in zip(units, data, probs):
                third(rr, b, tiles, pr)

        if nblk == 1:
            run([(rr, 0) for rr in range(rps)])
        else:
            def one(b, carry):
                run([(0, b)])
                return carry

            lax.fori_loop(0, nblk, one, 0)

    spec = pl.BlockSpec((seg, rps * ATTN_W), lambda r: (0, r))
    return _pcall(
        body, name=f"attn_fwd_d{dil}", grid=(dil // rps,),
        in_specs=[spec, spec, spec], out_specs=[spec, pl.BlockSpec((seg, rps * HEAD_W), lambda r: (0, r))],
        out_shape=[_sds((seg, dil * ATTN_W), BF16), _sds((seg, dil * HEAD_W), F32)],
        args=(qv, kv, vv), comms=comms)


def _lane_left(nrows):
    return lax.broadcasted_iota(jnp.int32, (nrows, CHUNK), 1) < HEAD_DIM


def _attn_rows(b):
    if isinstance(b, int):
        return b * CHUNK, max(b - 1, 0) * CHUNK
    return pl.multiple_of(b * CHUNK, CHUNK), pl.multiple_of(jnp.maximum(b - 1, 0) * CHUNK, CHUNK)


def _mix_out_fwd(o_list, l_list, sgu, x, w_out, g_attn, g_sgu, g_post, comms=()):
    def body(o1, o2, o3, l1, l2, l3, sgu_ref, x_ref, w_ref, ga_ref, gs_ref, gp_ref,
             attn_ref, mixed_ref, y_ref, x1_ref, lse1_ref, lse2_ref, lse3_ref, slabs_a, slabs_b):
        os = [o1[...], _load_view(o2, slabs_a, DILATIONS[1]), _load_view(o3, slabs_b, DILATIONS[2])]
        ls = [l1[...], _load_view(l2, slabs_a, DILATIONS[1], HEAD_W), _load_view(l3, slabs_b, DILATIONS[2], HEAD_W)]
        m = jnp.maximum(jnp.maximum(ls[0], ls[1]), ls[2])
        es = [jnp.exp(l - m) for l in ls]
        den = es[0] + es[1] + es[2]
        spread = _head_spread()
        attn = sum(_dot_select(e / den, spread) * o for e, o in zip(es, os))
        attn_ref[...] = attn
        lse = m + jnp.log(den)
        lse1_ref[...] = lse
        _store_view(lse, lse2_ref, slabs_a, DILATIONS[1])
        _store_view(lse, lse3_ref, slabs_b, DILATIONS[2])
        ah, _ = _rms_stats(attn)
        sh, _ = _rms_stats(sgu_ref[...])
        mixed = jnp.concatenate([ah * ga_ref[...], sh * gs_ref[...]], axis=1).astype(BF16)
        mixed_ref[...] = mixed
        y = _dot(mixed[:, 0:OUT_S], w_ref[0])
        for s in range(1, N_SHARD):
            y = y + _dot(mixed[:, s * OUT_S:(s + 1) * OUT_S], w_ref[s])
        y_ref[...] = y
        yh, _ = _rms_stats(y)
        x1_ref[...] = x_ref[...] + yh * gp_ref[...]

    return _pcall(
        body, name="mix_out_fwd", grid=(SEQ // TM,),
        in_specs=[_view_rows(dil) for dil in DILATIONS] + [_view_rows(dil, HEAD_W) for dil in DILATIONS]
        + [_rows(512), _rows(D_MODEL), _resident((N_SHARD, OUT_S, D_MODEL)),
           _resident((1, 512)), _resident((1, 512)), _resident((1, D_MODEL))],
        out_specs=[_rows(512), _rows(D_MODEL), _rows(D_MODEL), _rows(D_MODEL)]
        + [_view_rows(dil, HEAD_W) for dil in DILATIONS],
        out_shape=[_sds((SEQ, 512), F32), _sds((SEQ, D_MODEL), BF16), _sds((SEQ, D_MODEL), F32),
                   _sds((SEQ, D_MODEL), F32)] + [_view_shape(dil, F32, HEAD_W) for dil in DILATIONS],
        scratch_shapes=[_slab_scratch(), _slab_scratch()],
        args=(*o_list, *l_list, sgu, x, w_out, g_attn, g_sgu, g_post), comms=comms)


def _ffn_fwd_bwd(x1, target, w_gate, w_up, w_down, g_pre, g_post, comms=()):
    def body(x1_ref, t_ref, wg_ref, wu_ref, wd_ref, gpf_ref, gpo_ref,
             h2_ref, a_ref, dg_ref, dup_ref, df_ref, dx1_ref, loss_ref, dgpf_ref, dgpo_ref, g_scr, up_scr):
        @pl.when(pl.program_id(0) == 0)
        def _():
            loss_ref[...] = jnp.zeros_like(loss_ref)
            dgpf_ref[...] = jnp.zeros_like(dgpf_ref)
            dgpo_ref[...] = jnp.zeros_like(dgpo_ref)

        x1 = x1_ref[...]
        gpf = gpf_ref[...]
        gpo = gpo_ref[...]
        xh, r = _rms_stats(x1)
        h2 = (xh * gpf).astype(BF16)
        h2_ref[...] = h2
        f = jnp.zeros((TM_FFN, D_MODEL), F32)
        for c0, c1 in FF_CHUNKS:
            g = _dot_nt(h2, wg_ref[c0:c1, :])
            up = _dot_nt(h2, wu_ref[c0:c1, :])
            g_scr[:, c0:c1] = g
            up_scr[:, c0:c1] = up
            a = (g * _sigmoid(g) * up).astype(BF16)
            a_ref[:, c0:c1] = a
            f = f + _dot(a, wd_ref[c0:c1, :])
        fh, rf = _rms_stats(f)
        diff = x1 + fh * gpo - t_ref[...]
        loss_ref[...] += jnp.sum(diff * diff, axis=0, keepdims=True)
        dout = diff * np.float32(1.0 / D_MODEL)
        df, dgpo = _rms_bwd(fh, rf, gpo, dout)
        dgpo_ref[...] += dgpo
        dfb = df.astype(BF16)
        df_ref[...] = dfb
        dh2 = jnp.zeros((TM_FFN, D_MODEL), F32)
        for c0, c1 in FF_CHUNKS:
            da = _dot_nt(dfb, wd_ref[c0:c1, :])
            g = g_scr[:, c0:c1]
            up = up_scr[:, c0:c1]
            sg = _sigmoid(g)
            dup = (da * (g * sg)).astype(BF16)
            dg = (da * up * (sg * (1.0 + g * (1.0 - sg)))).astype(BF16)
            dg_ref[:, c0:c1] = dg
            dup_ref[:, c0:c1] = dup
            dh2 = dh2 + _dot(dg, wg_ref[c0:c1, :]) + _dot(dup, wu_ref[c0:c1, :])
        dx, dgpf = _rms_bwd(xh, r, gpf, dh2)
        dgpf_ref[...] += dgpf
        dx1_ref[...] = dout + dx

    return _pcall(
        body, name="ffn_fwd_bwd", grid=(SEQ // TM_FFN,),
        in_specs=[_rows(D_MODEL, TM_FFN), _rows(D_MODEL, TM_FFN), _resident((FF, D_MODEL)),
                  _resident((FF, D_MODEL)), _resident((FF, D_MODEL)),
                  _resident((1, D_MODEL)), _resident((1, D_MODEL))],
        out_specs=[_rows(D_MODEL, TM_FFN), _rows(FF, TM_FFN), _rows(FF, TM_FFN), _rows(FF, TM_FFN),
                   _rows(D_MODEL, TM_FFN), _rows(D_MODEL, TM_FFN), _acc(D_MODEL), _acc(D_MODEL), _acc(D_MODEL)],
        out_shape=[_sds((SEQ, D_MODEL), BF16), _sds((SEQ, FF), BF16), _sds((SEQ, FF), BF16),
                   _sds((SEQ, FF), BF16), _sds((SEQ, D_MODEL), BF16), _sds((SEQ, D_MODEL), F32),
                   _sds((1, D_MODEL), F32), _sds((1, D_MODEL), F32), _sds((1, D_MODEL), F32)],
        scratch_shapes=[pltpu.VMEM((TM_FFN, FF), F32), pltpu.VMEM((TM_FFN, FF), F32)],
        args=(x1, target, w_gate, w_up, w_down, g_pre, g_post), comms=comms)


def _wgrad(a, b, a_spec, b_spec, out_block, name, comms=()):
    def body(a_ref, b_ref, o_ref, ob_ref):
        av = a_ref[0] if len(a_ref.shape) == 3 else a_ref[...]
        bv = b_ref[0] if len(b_ref.shape) == 3 else b_ref[...]
        g = _dot_tn(av, bv)
        o_ref[0] = g
        ob_ref[0] = g.astype(BF16)

    return _pcall(
        body, name=name, grid=(N_SHARD,),
        in_specs=[a_spec, b_spec],
        out_specs=[pl.BlockSpec((1,) + out_block, lambda s: (s, 0, 0))] * 2,
        out_shape=[_sds((N_SHARD,) + out_block, F32), _sds((N_SHARD,) + out_block, BF16)],
        args=(a, b), comms=comms)


def _outproj_bwd(dx1, y, attn, sgu, w_out, g_post, g_attn, g_sgu, u, vs, lg, lb, w_sp, b_t, comms=(), after=()):
    sgu_bwd = _sgu_bwd_body()

    def body(dx1_ref, y_ref, attn_ref, sgu_ref, w_ref, gp_ref, ga_ref, gs_ref, u_ref, vs_ref, lg_ref, lb_ref,
             wsp_ref, bt_ref, dy_ref, du_ref, dvs_ref, dgp_ref, dga_ref, dgs_ref, dw_ref, db_ref, dlg_ref, dlb_ref,
             *rest):
        dattn_refs, delta_refs, (slabs_a, slabs_b, dsgu_ref, dbias_scr) = rest[0:3], rest[3:6], rest[6:]

        @pl.when(pl.program_id(0) == 0)
        def _():
            dgp_ref[...] = jnp.zeros_like(dgp_ref)
            dga_ref[...] = jnp.zeros_like(dga_ref)
            dgs_ref[...] = jnp.zeros_like(dgs_ref)

        yh, ry = _rms_stats(y_ref[...])
        dy, dgp = _rms_bwd(yh, ry, gp_ref[...], dx1_ref[...])
        dgp_ref[...] += dgp
        dyb = dy.astype(BF16)
        dy_ref[...] = dyb
        dmixed = jnp.concatenate([_dot_nt(dyb, w_ref[s]) for s in range(N_SHARD)], axis=1)
        attn = attn_ref[...]
        ah, ra = _rms_stats(attn)
        dattn, dga = _rms_bwd(ah, ra, ga_ref[...], dmixed[:, 0:512])
        dga_ref[...] += dga
        sh, rs = _rms_stats(sgu_ref[...])
        dsgu, dgs = _rms_bwd(sh, rs, gs_ref[...], dmixed[:, 512:1024])
        dgs_ref[...] += dgs
        dsgu_ref[...] = dsgu
        delta = _dot_select(dattn * attn, _head_sum())
        dattn_refs[0][...] = dattn.astype(BF16)
        delta_refs[0][...] = delta
        for i, dil in enumerate(DILATIONS[1:]):
            _store_view(dattn, dattn_refs[i + 1], slabs_a, dil)
            _store_view(delta, delta_refs[i + 1], slabs_b, dil)
        sgu_bwd(u_ref, vs_ref, dsgu_ref, lg_ref, lb_ref, wsp_ref, bt_ref,
                du_ref, dvs_ref, dw_ref, db_ref, dlg_ref, dlb_ref, dbias_scr)

    return _pcall(
        body, name="outproj_sgu_bwd", grid=(SEQ // TM,),
        in_specs=[_rows(D_MODEL), _rows(D_MODEL), _rows(512), _rows(512), _resident((N_SHARD, OUT_S, D_MODEL)),
                  _resident((1, D_MODEL)), _resident((1, 512)), _resident((1, 512)),
                  _rows(SGU_W), _rows(SGU_W), _resident((1, SGU_W)), _resident((1, SGU_W)),
                  _resident((SGU_GROUPS, CHUNK, CHUNK)), _resident((CHUNK, SGU_GROUPS))],
        out_specs=[_rows(D_MODEL), _rows(SGU_W), _rows(SGU_W), _acc(D_MODEL), _acc(512), _acc(512),
                   pl.BlockSpec((SGU_GROUPS, CHUNK, CHUNK), lambda i: (0, 0, 0)), _acc(CHUNK, SGU_GROUPS),
                   _acc(SGU_W), _acc(SGU_W)]
        + [_view_rows(dil) for dil in DILATIONS] + [_view_rows(dil, HEAD_W) for dil in DILATIONS],
        out_shape=[_sds((SEQ, D_MODEL), BF16), _sds((SEQ, SGU_W), BF16), _sds((SEQ, SGU_W), BF16),
                   _sds((1, D_MODEL), F32), _sds((1, 512), F32), _sds((1, 512), F32),
                   _sds((SGU_GROUPS, CHUNK, CHUNK), F32), _sds((SGU_GROUPS, CHUNK), F32),
                   _sds((1, SGU_W), F32), _sds((1, SGU_W), F32)]
        + [_view_shape(dil, BF16) for dil in DILATIONS] + [_view_shape(dil, F32, HEAD_W) for dil in DILATIONS],
        scratch_shapes=[_slab_scratch(), _slab_scratch(), pltpu.VMEM((TM, SGU_W), F32),
                        pltpu.VMEM((CHUNK, SGU_W), F32)],
        args=(dx1, y, attn, sgu, w_out, g_post, g_attn, g_sgu, u, vs, lg, lb, w_sp, b_t), comms=comms, after=after)


def _sgu_bwd_body():
    nsteps = SEQ // TM

    def body(u_ref, vs_ref, ds_ref, lg_ref, lb_ref, w_ref, bt_ref,
             du_ref, dvs_ref, dw_ref, db_ref, dlg_ref, dlb_ref, dbias_scr):
        i = pl.program_id(0)

        @pl.when(i == 0)
        def _():
            dw_ref[...] = jnp.zeros_like(dw_ref)
            dlg_ref[...] = jnp.zeros_like(dlg_ref)
            dlb_ref[...] = jnp.zeros_like(dlb_ref)
            dbias_scr[...] = jnp.zeros_like(dbias_scr)

        wm = _masked_spatial(w_ref)
        ones_g = _group_ones()
        bias_full = _dot_exact(bt_ref[...], ones_g)
        u = u_ref[...]
        vs = vs_ref[...]
        lg = lg_ref[...]
        gu, xh, rstd, vnb, mixed, cdf_u, cdf_vs = _sgu_core(u, vs, lg, lb_ref[...], wm, bias_full)
        dsgu = ds_ref[...]
        du_ref[...] = (dsgu * mixed * _gelu_grad(u, cdf_u)).astype(BF16)
        dmixed = dsgu * gu
        left = _left_half()
        dvn_rows = []
        for c in range(TM // CHUNK):
            rs = slice(c * CHUNK, (c + 1) * CHUNK)
            dm_c = dmixed[rs, :]
            dbias_scr[...] += dm_c
            pieces = []
            for p in range(4):
                ls = slice(p * 128, (p + 1) * 128)
                dmp = dm_c[:, ls]
                vp = vnb[rs, ls]
                dmb = dmp.astype(BF16)
                zero = jnp.zeros_like(dmb)
                dw_ref[2 * p] += _dot_nt(jnp.where(left, dmb, zero), vp)
                dw_ref[2 * p + 1] += _dot_nt(jnp.where(left, zero, dmb), vp)
                pieces.append(jnp.where(left, _dot_tn(wm[2 * p], dmb), _dot_tn(wm[2 * p + 1], dmb)))
            dvn_rows.append(jnp.concatenate(pieces, axis=1))
        dvn = jnp.concatenate(dvn_rows, axis=0)
        dlg_ref[...] += jnp.sum(dvn * xh, axis=0, keepdims=True)
        dlb_ref[...] += jnp.sum(dvn, axis=0, keepdims=True)
        dxh = dvn * lg
        dgv = rstd * (dxh - jnp.mean(dxh, axis=-1, keepdims=True) - xh * jnp.mean(dxh * xh, axis=-1, keepdims=True))
        dvs_ref[...] = (dgv * _gelu_grad(vs, cdf_vs)).astype(BF16)

        @pl.when(i == nsteps - 1)
        def _():
            row = lax.broadcasted_iota(jnp.int32, (CHUNK, CHUNK), 0)
            col = lax.broadcasted_iota(jnp.int32, (CHUNK, CHUNK), 1)
            for g in range(SGU_GROUPS):
                dw_ref[g] = jnp.where(col <= row, dw_ref[g], 0.0)
            db_ref[...] = lax.dot_general(ones_g, dbias_scr[...], (((1,), (1,)), ((), ())),
                                          preferred_element_type=F32, precision=lax.Precision.HIGHEST)

    return body


def _attn_bwd(qv, kv, vv, dov, deltav, lsev, dil, comms=(), after=()):
    seg = SEQ // dil
    nblk = seg // CHUNK
    rps = 4 if nblk == 1 else 1

    def body(q_ref, k_ref, v_ref, do_ref, dl_ref, lse_ref, dq_ref, dk_ref, dv_ref, dk_wait, dv_wait):
        left = _left_half()
        m_cur, m_prev = _block_masks()
        sides = tuple(enumerate((left, ~left)))
        zero = jnp.zeros((CHUNK, CHUNK), BF16)

        def first(rr, b, both):
            r0, rp = _attn_rows(b)
            keys = pl.ds(rp, 2 * CHUNK) if both else pl.ds(r0, CHUNK)
            tiles, heads = [], []
            for hp in range(4):
                ls = slice(rr * ATTN_W + hp * 128, rr * ATTN_W + (hp + 1) * 128)
                qp = q_ref[pl.ds(r0, CHUNK), ls]
                dop = do_ref[pl.ds(r0, CHUNK), ls]
                k2 = k_ref[keys, ls]
                v2 = v_ref[keys, ls]
                tiles.append((ls, k2))
                for _, hm in sides:
                    qh = jnp.where(hm, qp, zero)
                    doh = jnp.where(hm, dop, zero)
                    heads.append((qh, doh, _dot_nt(k2, qh), _dot_nt(v2, doh)))
            return tiles, heads

        def second(rr, b, heads, both):
            r0, _ = _attn_rows(b)
            ok = jnp.concatenate([m_cur, m_prev], axis=0) if both else m_prev
            lanes = slice(rr * HEAD_W, (rr + 1) * HEAD_W)
            lse_t = lse_ref[pl.ds(r0, CHUNK), lanes].T
            dl_t = dl_ref[pl.ds(r0, CHUNK), lanes].T
            out = []
            for i, (_, _, s_t, dp_t) in enumerate(heads):
                p = jnp.exp(jnp.where(ok, s_t - lse_t[16 * i:16 * i + 1, :], NEG))
                out.append((p.astype(BF16), (p * (dp_t - dl_t[16 * i:16 * i + 1, :])).astype(BF16)))
            return out

        def third(rr, b, tiles, heads, probs, both):
            r0, rp = _attn_rows(b)
            nk = 2 * CHUNK if both else CHUNK
            left_k = _lane_left(nk)
            zero_k = jnp.zeros((nk, CHUNK), BF16)
            for hp, (ls, k2) in enumerate(tiles):
                dq = jnp.zeros((CHUNK, CHUNK), F32)
                dk2 = jnp.zeros((nk, CHUNK), F32)
                dv2 = jnp.zeros((nk, CHUNK), F32)
                for side in range(2):
                    qh, doh, _, _ = heads[2 * hp + side]
                    p, ds = probs[2 * hp + side]
                    dq = dq + _dot_tn(ds, jnp.where(left_k if side == 0 else ~left_k, k2, zero_k))
                    dk2 = dk2 + _dot(ds, qh)
                    dv2 = dv2 + _dot(p, doh)
                dq_ref[pl.ds(r0, CHUNK), ls] = dq.astype(dq_ref.dtype)
                if nblk == 1:
                    dk_ref[pl.ds(r0, CHUNK), ls] = dk2.astype(dk_ref.dtype)
                    dv_ref[pl.ds(r0, CHUNK), ls] = dv2.astype(dv_ref.dtype)
                elif both:
                    dk_ref[pl.ds(rp, CHUNK), ls] = (dk_wait[:, ls] + dk2[0:CHUNK]).astype(dk_ref.dtype)
                    dv_ref[pl.ds(rp, CHUNK), ls] = (dv_wait[:, ls] + dv2[0:CHUNK]).astype(dv_ref.dtype)
                    dk_wait[:, ls] = dk2[CHUNK:]
                    dv_wait[:, ls] = dv2[CHUNK:]
                else:
                    dk_wait[:, ls] = dk2
                    dv_wait[:, ls] = dv2

        def run(units, both):
            data = [first(rr, b, both) for rr, b in units]
            probs = [second(rr, b, heads, both) for (rr, b), (_, heads) in zip(units, data)]
            for (rr, b), (tiles, heads), pr in zip(units, data, probs):
                third(rr, b, tiles, heads, pr, both)

        run([(rr, 0) for rr in range(rps)], False)
        if nblk > 1:
            def one(b, carry):
                run([(0, b)], True)
                return carry

            lax.fori_loop(1, nblk, one, 0)
            last = (nblk - 1) * CHUNK
            dk_ref[last:last + CHUNK, :] = dk_wait[...].astype(dk_ref.dtype)
            dv_ref[last:last + CHUNK, :] = dv_wait[...].astype(dv_ref.dtype)

    spec = pl.BlockSpec((seg, rps * ATTN_W), lambda r: (0, r))
    return _pcall(
        body, name=f"attn_bwd_d{dil}", grid=(dil // rps,),
        in_specs=[spec] * 4 + [pl.BlockSpec((seg, rps * HEAD_W), lambda r: (0, r))] * 2, out_specs=[spec] * 3,
        out_shape=[_sds((seg, dil * ATTN_W), BF16)] * 3,
        scratch_shapes=[pltpu.VMEM((CHUNK, ATTN_W), F32), pltpu.VMEM((CHUNK, ATTN_W), F32)],
        args=(qv, kv, vv, dov, deltav, lsev), comms=comms, after=after)


def _inproj_bwd(dqs, dks, dvs, du, dvs_sgu, pos, x, dx1, w_in, g_pre, comms=()):
    def body(dq1, dq2, dq3, dk1, dk2, dk3, dv1, dv2, dv3, du_ref, dvs_ref, pos_ref, x_ref, dx1_ref, w_ref, g_ref,
             dproj_ref, gx_ref, dg_ref, slabs_a, slabs_b):
        @pl.when(pl.program_id(0) == 0)
        def _():
            dg_ref[...] = jnp.zeros_like(dg_ref)

        def total(r1, r2, r3):
            return r1[...] + _load_view(r2, slabs_a, DILATIONS[1]) + _load_view(r3, slabs_b, DILATIONS[2])

        tabs = _rot_tables(pos_ref[...])
        groups = {3: du_ref[...], 4: dvs_ref[...]}
        dh = jnp.zeros((TM, D_MODEL), F32)
        for g in (3, 4, 0, 1, 2):
            if g == 0:
                groups[g] = _rope_bwd(total(dq1, dq2, dq3) * np.float32(ATTN_SCALE), tabs).astype(BF16)
            elif g == 1:
                groups[g] = _rope_bwd(total(dk1, dk2, dk3), tabs).astype(BF16)
            elif g == 2:
                groups[g] = total(dv1, dv2, dv3).astype(BF16)
            dproj_ref[:, 512 * g:512 * (g + 1)] = groups[g]
            off = 0
            for s, a, b in _in_pieces(g):
                dh = dh + _dot_nt(groups[g][:, off:off + b - a], w_ref[s, :, a:b])
                off += b - a
        g = g_ref[...]
        xh, r = _rms_stats(x_ref[...])
        dx, dg = _rms_bwd(xh, r, g, dh)
        dg_ref[...] += dg
        gx_ref[...] = dx1_ref[...] + dx

    return _pcall(
        body, name="inproj_bwd", grid=(SEQ // TM,),
        in_specs=[_view_rows(dil) for dil in DILATIONS] * 3
        + [_rows(512), _rows(512), _rows(1), _rows(D_MODEL), _rows(D_MODEL),
           _resident((N_SHARD, D_MODEL, IN_S)), _resident((1, D_MODEL))],
        out_specs=[_rows(PROJ_W), _rows(D_MODEL), _acc(D_MODEL)],
        out_shape=[_sds((SEQ, PROJ_W), BF16), _sds((SEQ, D_MODEL), F32), _sds((1, D_MODEL), F32)],
        scratch_shapes=[_slab_scratch(), _slab_scratch()],
        args=(*dqs, *dks, *dvs, du, dvs_sgu, pos, x, dx1, w_in, g_pre), comms=comms)


def _coords():
    return lax.axis_index("x"), lax.axis_index("y"), lax.axis_index("c")


def _other_chips(x, y):
    return [(1 - x, y), (x, 1 - y), (1 - x, 1 - y)]


WEIGHTS = ("pre_mix_norm", "w_in", "sgu_ln_gain", "sgu_ln_bias", "sgu_w_spatial", "sgu_b_spatial", "attn_out_norm",
           "sgu_out_norm", "w_out", "post_mix_norm", "pre_ffn_norm", "w_gate", "w_up", "w_down", "post_ffn_norm")
SMALL = ("pre_mix_norm", "post_mix_norm", "pre_ffn_norm", "post_ffn_norm", "sgu_ln_gain", "sgu_ln_bias",
         "attn_out_norm", "sgu_out_norm", "sgu_w_spatial", "sgu_b_spatial")


def _remote(src, dst, send_sem, recv_sem, to):
    return pltpu.make_async_remote_copy(src_ref=src, dst_ref=dst, send_sem=send_sem, recv_sem=recv_sem,
                                        device_id=to, device_id_type=MESH)


def _halves(a):
    *lead, rows, cols = a.shape
    return a.reshape(*lead, 2, rows // 2, cols)


def _gather_ici(shards, landing=None):
    n = len(shards)

    def desc(ins, outs, ss, rs, j, w, landed):
        x, y, c = _coords()
        cx, cy = _other_chips(x, y)[j]
        shard = 2 * cx + cy if landed else 2 * x + y
        return _remote(ins[w].at[c], outs[w].at[shard, c], ss.at[j * n + w], rs.at[j * n + w], (cx, cy, c))

    def start(ins, outs, ss, rs):
        for j in range(3):
            for w in range(n):
                desc(ins, outs, ss, rs, j, w, False).start()

    def finish(ins, outs, ss, rs):
        for j in range(3):
            for w in range(n):
                desc(ins, outs, ss, rs, j, w, True).wait_recv()
                desc(ins, outs, ss, rs, j, w, False).wait_send()

    return _Comm(shards, [_sds((N_SHARD,) + s.shape, s.dtype) for s in shards], 3 * n, start, finish,
                 landing=landing)


def _gather_pass(fulls):
    n = len(fulls)

    def desc(bufs, ss, rs, j, w, landed):
        x, y, c = _coords()
        cx, cy = _other_chips(x, y)[j]
        shard = 2 * cx + cy
        return _remote(bufs[w].at[shard, c], bufs[w].at[shard, 1 - c if landed else c],
                       ss.at[j * n + w], rs.at[j * n + w], (x, y, 1 - c))

    def start(ins, outs, ss, rs):
        for j in range(3):
            for w in range(n):
                desc(outs, ss, rs, j, w, False).start()

    def finish(ins, outs, ss, rs):
        for j in range(3):
            for w in range(n):
                desc(outs, ss, rs, j, w, True).wait_recv()
                desc(outs, ss, rs, j, w, False).wait_send()

    return _Comm(fulls, [_sds(f.shape, f.dtype) for f in fulls], 3 * n, start, finish, aliased=True)


def _rs_sibling(gws):
    n = len(gws)

    def desc(ins, outs, ss, rs, w):
        x, y, c = _coords()
        return _remote(ins[w].at[:, 1 - c], outs[w], ss.at[w], rs.at[w], (x, y, 1 - c))

    def start(ins, outs, ss, rs):
        for w in range(n):
            desc(ins, outs, ss, rs, w).start()

    def finish(ins, outs, ss, rs):
        for w in range(n):
            desc(ins, outs, ss, rs, w).wait()

    out_shape = [_sds((N_SHARD, g.shape[1] // 2, g.shape[2]), g.dtype) for g in gws]
    return _Comm([_halves(g) for g in gws], out_shape, n, start, finish)


def _rs_chips(pbs):
    n = len(pbs)

    def desc(ins, outs, ss, rs, j, w):
        x, y, c = _coords()
        cx, cy = _other_chips(x, y)[j]
        return _remote(ins[w].at[2 * cx + cy], outs[w].at[j], ss.at[j * n + w], rs.at[j * n + w], (cx, cy, c))

    def start(ins, outs, ss, rs):
        for j in range(3):
            for w in range(n):
                desc(ins, outs, ss, rs, j, w).start()

    def finish(ins, outs, ss, rs):
        for j in range(3):
            for w in range(n):
                desc(ins, outs, ss, rs, j, w).wait()

    return _Comm(pbs, [_sds((3,) + p.shape[1:], p.dtype) for p in pbs], 3 * n, start, finish)


def _rs_join(halves):
    n = len(halves)

    def desc(bufs, ss, rs, w, landed):
        x, y, c = _coords()
        return _remote(bufs[w].at[c], bufs[w].at[1 - c if landed else c], ss.at[w], rs.at[w], (x, y, 1 - c))

    def start(ins, outs, ss, rs):
        for w in range(n):
            desc(outs, ss, rs, w, False).start()

    def finish(ins, outs, ss, rs):
        for w in range(n):
            desc(outs, ss, rs, w, True).wait_recv()
            desc(outs, ss, rs, w, False).wait_send()

    return _Comm(halves, [_sds(h.shape, h.dtype) for h in halves], n, start, finish, aliased=True)


def _small_exchange(buf):
    def desc(ins, outs, ss, rs, k, landed):
        x, y, c = _coords()
        px = 1 - x if (k >> 2) & 1 else x
        py = 1 - y if (k >> 1) & 1 else y
        pc = 1 - c if k & 1 else c
        slot = 4 * px + 2 * py + pc if landed else 4 * x + 2 * y + c
        return _remote(ins[0], outs[0].at[slot], ss.at[k - 1], rs.at[k - 1], (px, py, pc))

    def start(ins, outs, ss, rs):
        for k in range(1, 8):
            desc(ins, outs, ss, rs, k, False).start()

    def finish(ins, outs, ss, rs):
        for k in range(1, 8):
            desc(ins, outs, ss, rs, k, True).wait_recv()
            desc(ins, outs, ss, rs, k, False).wait_send()

    return _Comm([buf], [_sds((8,) + buf.shape, buf.dtype)], 7, start, finish)


HBM = pl.BlockSpec(memory_space=pltpu.HBM)
SEM = pl.BlockSpec(memory_space=pltpu.SEMAPHORE)
DATAFLOW = pltpu.SideEffectType.DATAFLOW_SIDE_EFFECTING


def _split_starts(name, comms, after):
    srcs = [[pltpu.with_memory_space_constraint(s, pltpu.HBM) for s in c.args] for c in comms]
    lands = [[pltpu.with_memory_space_constraint(b, pltpu.HBM)
              for b in c.landing or [lax.empty(o.shape, o.dtype) for o in c.out_shape]] for c in comms]
    bufs = [b for k in range(len(comms)) for b in srcs[k] + lands[k]]
    nb, nc = len(bufs), len(comms)

    def body(*refs):
        sems = refs[nb + 1:nb + 1 + 2 * nc]
        off = 0
        for k, c in enumerate(comms):
            ns, nl = len(srcs[k]), len(lands[k])
            c.start(refs[off:off + ns], refs[off + ns:off + ns + nl], sems[2 * k], sems[2 * k + 1])
            off += ns + nl
        refs[-1][...] = jnp.zeros_like(refs[-1])

    res = pl.pallas_call(
        body, name=name,
        out_shape=(*[pltpu.SemaphoreType.DMA((c.n_sems,)) for c in comms for _ in range(2)],
                   *[pltpu.HBM(b.shape, b.dtype) for b in bufs], _sds((8, 128), F32)),
        in_specs=[HBM] * nb + [ANY],
        out_specs=(*[SEM] * (2 * nc), *[HBM] * nb, pl.BlockSpec(memory_space=pltpu.VMEM)),
        input_output_aliases={i: 2 * nc + i for i in range(nb)},
        compiler_params=pltpu.CompilerParams(has_side_effects=DATAFLOW),
    )(*bufs, after)
    states, off = [], 2 * nc
    for k in range(nc):
        ns, nl = len(srcs[k]), len(lands[k])
        states.append((res[2 * k], res[2 * k + 1], list(res[off:off + ns]), list(res[off + ns:off + ns + nl])))
        off += ns + nl
    return states, res[-1]


def _split_wait(name, comm, send_sems, recv_sems, srcs, lands, after):
    ns, nb = len(srcs), len(lands)
    after = list(after) if isinstance(after, (list, tuple)) else [after]

    def body(*refs):
        comm.finish(refs[:ns], refs[ns:ns + nb], refs[ns + nb], refs[ns + nb + 1])

    res = pl.pallas_call(
        body, name=name,
        out_shape=tuple(pltpu.HBM(b.shape, b.dtype) for b in srcs + lands),
        in_specs=[HBM] * (ns + nb) + [SEM, SEM] + [ANY] * len(after), out_specs=tuple([HBM] * (ns + nb)),
        input_output_aliases={i: i for i in range(ns + nb)},
        compiler_params=pltpu.CompilerParams(has_side_effects=DATAFLOW),
    )(*srcs, *lands, send_sems, recv_sems, *after)
    return list(res[ns:])


LOSS_ROW = "loss_cols"
SMALL_EARLY = ("post_mix_norm", "pre_ffn_norm", "post_ffn_norm", "sgu_ln_gain", "sgu_ln_bias", "attn_out_norm",
               "sgu_out_norm", "sgu_b_spatial", LOSS_ROW)
SMALL_LATE = ("pre_mix_norm",)
SMALL_WSP = "sgu_w_spatial"


def _pack(d, names, rows):
    flat = [d[n].reshape(-1) for n in names]
    used = sum(f.shape[0] for f in flat)
    flat.append(jnp.zeros((rows * 1024 - used,), F32))
    return jnp.concatenate(flat).reshape(rows, 1024)


def _unpack(buf, names, shapes):
    flat = buf.reshape(-1)
    out, off = {}, 0
    for n in names:
        size = int(np.prod(shapes[n]))
        out[n] = flat[off:off + size].reshape(shapes[n])
        off += size
    return out


def _chip_sums(gbs, recvs, shard_core):
    n = len(gbs)
    _, rows, cols = gbs[0].shape
    hw = rows // 2

    def body(sc_ref, *refs):
        for k in range(n):
            refs[2 * n + k][...] = (refs[k][...].astype(F32) + refs[n + k][...].astype(F32)).astype(BF16)

    def other(s, sc):
        return jnp.where(s >= sc[0], s + 1, s)

    mine = pl.BlockSpec((1, hw, cols), lambda s, sc: (other(s, sc), sc[1], 0))
    plain = pl.BlockSpec((1, hw, cols), lambda s, sc: (other(s, sc), 0, 0))
    return pl.pallas_call(
        body, name="rs_chip_sums",
        grid_spec=pltpu.PrefetchScalarGridSpec(num_scalar_prefetch=1, grid=(N_SHARD - 1,),
                                               in_specs=[mine] * n + [plain] * n, out_specs=[plain] * n),
        out_shape=[_sds((N_SHARD, hw, cols), BF16)] * n,
        compiler_params=_seq_params(),
    )(shard_core, *gbs, *recvs)


def _final_sums(gws, recv_sibs, recv_chips, shard_core):
    n = len(gws)
    halves = [(g.shape[1] // 2, g.shape[2]) for g in gws]

    def body(sc_ref, *refs):
        for k in range(n):
            acc = refs[k][0] + refs[n + k][0]
            for j in range(3):
                acc = acc + refs[2 * n + k][j].astype(F32)
            refs[3 * n + k][0] = acc

    def specs(lead, index):
        return [pl.BlockSpec((lead, hw, cols), index) for hw, cols in halves]

    return pl.pallas_call(
        body, name="rs_final_sums",
        grid_spec=pltpu.PrefetchScalarGridSpec(
            num_scalar_prefetch=1, grid=(1,),
            in_specs=specs(1, lambda i, sc: (sc[0], sc[1], 0)) + specs(1, lambda i, sc: (sc[0], 0, 0))
            + specs(3, lambda i, sc: (0, 0, 0)),
            out_specs=specs(1, lambda i, sc: (sc[1], 0, 0))),
        out_shape=[_sds((2, hw, cols), F32) for hw, cols in halves],
        compiler_params=_seq_params(),
    )(shard_core, *gws, *recv_sibs, *recv_chips)


ADAM_BLOCKS = 4


def _adamw_multi(ws, gs, ms, vs, name, after=()):
    n = len(ws)

    def body(*refs):
        outs = refs[4 * n + len(after):]
        for k in range(n):
            g = refs[n + k][...]
            d, m, v = _adam_math(refs[k][...], g, refs[2 * n + k][...], refs[3 * n + k][...])
            outs[4 * k][...], outs[4 * k + 1][...], outs[4 * k + 2][...], outs[4 * k + 3][...] = g, d, m, v

    specs = [pl.BlockSpec((w.shape[0] // ADAM_BLOCKS, w.shape[1]), lambda i: (i, 0)) for w in ws]
    res = pl.pallas_call(
        body, name=name, grid=(ADAM_BLOCKS,), in_specs=specs * 4 + [ANY] * len(after),
        out_specs=[s for s in specs for _ in range(4)],
        out_shape=[_sds(w.shape, F32) for w in ws for _ in range(4)],
        compiler_params=_seq_params(),
    )(*ws, *gs, *ms, *vs, *after)
    return [tuple(res[4 * k:4 * k + 4]) for k in range(n)]


def _wgrad_ff(a_list, b, name):
    n = len(a_list)
    cols = 256

    def body(*refs):
        bv = refs[n][...]
        for k in range(n):
            g = _dot_tn(refs[k][...], bv)
            refs[n + 1 + k][...] = g
            refs[2 * n + 1 + k][...] = g.astype(BF16)

    res = _pcall(
        body, name=name, grid=(FF // cols,),
        in_specs=[pl.BlockSpec((SEQ, cols), lambda j: (0, j))] * n
        + [pl.BlockSpec((SEQ, D_MODEL), lambda j: (0, 0), pipeline_mode=pl.Buffered(1))],
        out_specs=[pl.BlockSpec((cols, D_MODEL), lambda j: (j, 0))] * (2 * n),
        out_shape=[_sds((FF, D_MODEL), F32)] * n + [_sds((FF, D_MODEL), BF16)] * n, args=(*a_list, b))
    return [m.reshape(N_SHARD, FF_S, D_MODEL) for m in res]


def _comm_only(name, comms):
    return _pcall(lambda: None, name=name, grid=(1,), in_specs=[], out_specs=[], out_shape=[], args=(),
                  comms=comms)[1]


def _adam_math(w, g, m, v):
    m = ADAM_B1 * m + (1.0 - ADAM_B1) * g
    v = ADAM_B2 * v + (1.0 - ADAM_B2) * (g * g)
    m_hat = m / (1.0 - ADAM_B1 ** ADAM_STEP)
    v_hat = v / (1.0 - ADAM_B2 ** ADAM_STEP)
    return -ADAM_LR * (m_hat / (jnp.sqrt(v_hat) + ADAM_EPS) + ADAM_WD * w), m, v


def _adamw_small(own, slots, w, m, v, me):
    rows, cols = own.shape

    def body(me_ref, own_ref, slots_ref, w_ref, m_ref, v_ref, g_ref, d_ref, nm_ref, nv_ref):
        own_v = own_ref[...].astype(F32)
        g = jnp.where(me_ref[0] == 0, own_v, slots_ref[0].astype(F32))
        for i in range(1, 8):
            g = g + jnp.where(me_ref[0] == i, own_v, slots_ref[i].astype(F32))
        g_ref[...] = g
        d_ref[...], nm_ref[...], nv_ref[...] = _adam_math(w_ref[...], g, m_ref[...], v_ref[...])

    flat = pl.BlockSpec((rows, cols), lambda i, me_ref: (0, 0))
    return pl.pallas_call(
        body, name="adamw_small",
        grid_spec=pltpu.PrefetchScalarGridSpec(
            num_scalar_prefetch=1, grid=(1,),
            in_specs=[flat, pl.BlockSpec((8, rows, cols), lambda i, me_ref: (0, 0, 0)), flat, flat, flat],
            out_specs=[flat] * 4),
        out_shape=[_sds((rows, cols), F32)] * 4,
        compiler_params=_seq_params(),
    )(me, own, slots, w, m, v)


def kernel(x, positions, pre_mix_norm, w_in, sgu_ln_gain, sgu_ln_bias, sgu_w_spatial, sgu_b_spatial, attn_out_norm, sgu_out_norm, w_out, post_mix_norm, pre_ffn_norm, w_gate, w_up, w_down, post_ffn_norm, loss_target, m_pre_mix_norm, m_w_in, m_sgu_ln_gain, m_sgu_ln_bias, m_sgu_w_spatial, m_sgu_b_spatial, m_attn_out_norm, m_sgu_out_norm, m_w_out, m_post_mix_norm, m_pre_ffn_norm, m_w_gate, m_w_up, m_w_down, m_post_ffn_norm, v_pre_mix_norm, v_w_in, v_sgu_ln_gain, v_sgu_ln_bias, v_sgu_w_spatial, v_sgu_b_spatial, v_attn_out_norm, v_sgu_out_norm, v_w_out, v_post_mix_norm, v_pre_ffn_norm, v_w_gate, v_w_up, v_w_down, v_post_ffn_norm):
    a = dict(locals())
    cx, cy, cc = _coords()
    s_me = 2 * cx + cy
    shard_core = jnp.stack([s_me, cc]).astype(jnp.int32)
    me = jnp.stack([4 * cx + 2 * cy + cc]).astype(jnp.int32)
    small = {n: (a[n][0] if a[n].ndim > 2 else a[n]) for n in SMALL}
    b_t = small["sgu_b_spatial"].T
    xs, pos, target = x[0], positions.reshape(SEQ, 1), loss_target[0]
    flipped = ("w_gate", "w_up")

    def big(name, n):
        return jnp.swapaxes(a[name], 1, 2)[0] if n in flipped else a[name][0]

    own = {"w_in": _halves(big("w_in", "w_in").astype(BF16))}

    def with_own(full, n):
        return lax.dynamic_update_slice(full, own[n][None], (s_me, 0, 0, 0))

    def whole(full, n):
        return full.reshape((N_SHARD,) + big(n, n).shape)

    ffn = ("w_gate", "w_up", "w_down")
    g_in = _gather_ici([own["w_in"]])
    (s_in,), token = _split_starts("gather_in_start", [g_in], small["pre_mix_norm"])
    rest = ("w_out",) + ffn
    for n in rest:
        own[n] = _halves((big(n, n) + token[0:1, 0:1]).astype(BF16))
    room = {n: with_own(lax.empty((N_SHARD,) + own[n].shape, BF16), n) for n in rest}
    g_out, g_ffn = _gather_ici([own["w_out"]], [room["w_out"]]), _gather_ici([own[n] for n in ffn],
                                                                             [room[n] for n in ffn])
    (s_out, s_ffn), token = _split_starts("gather_rest_start", [g_out, g_ffn], token)
    in_lands = _split_wait("gather_in_wait", g_in, *s_in, token)
    ((in_lands,),) = _comm_only("comm_pass_in", [_gather_pass(in_lands)])
    w_in_f = whole(with_own(in_lands, "w_in"), "w_in")
    h, u, vs, sgu, *qkv = _inproj_fwd(xs, pos, small["pre_mix_norm"], w_in_f, small["sgu_ln_gain"],
                                      small["sgu_ln_bias"], small["sgu_w_spatial"], b_t)
    views = [tuple(qkv[3 * i:3 * i + 3]) for i in range(len(DILATIONS))]
    out_lands = _split_wait("gather_out_wait", g_out, *s_out, sgu)
    o_list, l_list = [], []
    for dil, (qv, kv, vv) in zip(DILATIONS, views):
        if dil == 1:
            (o, l), ((out_lands,),) = _attn_fwd(qv, kv, vv, dil, comms=[_gather_pass(out_lands)])
        else:
            o, l = _attn_fwd(qv, kv, vv, dil)
        o_list.append(o)
        l_list.append(l)
    w_out_f = whole(out_lands, "w_out")
    ffn_lands = _split_wait("gather_ffn_wait", g_ffn, *s_ffn, l_list[-1])
    (attn, mixed, y, x1, *lses), (ffn_lands,) = _mix_out_fwd(
        o_list, l_list, sgu, xs, w_out_f, small["attn_out_norm"], small["sgu_out_norm"], small["post_mix_norm"],
        comms=[_gather_pass(ffn_lands)])
    w_gate_f, w_up_f, w_down_f = (f.reshape(FF, D_MODEL) for f in ffn_lands)
    h2, act, dg, dup, df, dx1, loss_cols, d_pre_ffn, d_post_ffn = _ffn_fwd_bwd(
        x1, target, w_gate_f, w_up_f, w_down_f, small["pre_ffn_norm"], small["post_ffn_norm"])

    full_tok = pl.BlockSpec((SEQ, D_MODEL), lambda s: (0, 0), pipeline_mode=pl.Buffered(1))
    gw, gb = {}, {}
    gw["w_gate"], gw["w_up"], gb["w_gate"], gb["w_up"] = _wgrad_ff([dg, dup], h2, "wgrad_gate_up")
    gw["w_down"], gb["w_down"] = _wgrad_ff([act], df, "wgrad_down")
    (dy, du, dvs_sgu, d_post_mix, d_attn_norm, d_sgu_norm, d_w_sp, d_b_sp, d_ln_gain, d_ln_bias,
     *dviews), (sib_ffn,) = _outproj_bwd(
        dx1, y, attn, sgu, w_out_f, small["post_mix_norm"], small["attn_out_norm"], small["sgu_out_norm"],
        u, vs, small["sgu_ln_gain"], small["sgu_ln_bias"], small["sgu_w_spatial"], b_t,
        comms=[_rs_sibling([gb[n] for n in ffn])])
    sib = dict(zip(ffn, sib_ffn))
    part = dict(zip(ffn, _chip_sums([gb[n] for n in ffn], [sib[n] for n in ffn], shard_core)))
    gw["w_out"], gb["w_out"] = _wgrad(mixed, dy, pl.BlockSpec((SEQ, OUT_S), lambda s: (0, s)), full_tok,
                                      (OUT_S, D_MODEL), "wgrad_out")
    x_ffn = _rs_chips([part[n] for n in ffn])
    packed_early = _pack({
        "sgu_ln_gain": d_ln_gain, "sgu_ln_bias": d_ln_bias, "sgu_b_spatial": d_b_sp,
        "attn_out_norm": d_attn_norm, "sgu_out_norm": d_sgu_norm, "post_mix_norm": d_post_mix,
        "pre_ffn_norm": d_pre_ffn, "post_ffn_norm": d_post_ffn, LOSS_ROW: loss_cols}, SMALL_EARLY, 8)
    wsp_view = (SGU_GROUPS * CHUNK, CHUNK)
    packed_wsp = d_w_sp.reshape(wsp_view).astype(BF16)
    x_small, x_wsp = _small_exchange(packed_early), _small_exchange(packed_wsp)
    (s_ffn,), token = _split_starts("rs_ffn_start", [x_ffn], small["pre_mix_norm"])

    dqs, dks, dvs = [], [], []
    for i, (dil, (qv, kv, vv)) in enumerate(zip(DILATIONS, views)):
        if dil == 1:
            (dq, dk, dv), ((sib["w_out"],),) = _attn_bwd(qv, kv, vv, dviews[i], dviews[3 + i], lses[i], dil,
                                                        comms=[_rs_sibling([gb["w_out"]])], after=[token])
            (part["w_out"],) = _chip_sums([gb["w_out"]], [sib["w_out"]], shard_core)
            x_out = _rs_chips([part["w_out"]])
            (s_out, s_small, s_wsp), token = _split_starts("rs_out_small_start", [x_out, x_small, x_wsp], token)
        else:
            dq, dk, dv = _attn_bwd(qv, kv, vv, dviews[i], dviews[3 + i], lses[i], dil, after=[token])
        dqs.append(dq)
        dks.append(dk)
        dvs.append(dv)
    half, far, joined = {}, {}, {}
    far.update(zip(ffn, _split_wait("rs_ffn_wait", x_ffn, *s_ffn, dvs[-1])))
    half.update(zip(ffn, _final_sums([gw[n] for n in ffn], [sib[n] for n in ffn], [far[n] for n in ffn],
                                     shard_core)))
    dproj, grad_x, d_pre_mix = _inproj_bwd(dqs, dks, dvs, du, dvs_sgu, pos, xs, dx1, w_in_f, small["pre_mix_norm"])
    packed_late = _pack({"pre_mix_norm": d_pre_mix}, SMALL_LATE, 8)
    (gw["w_in"], gb["w_in"]), (got, (slots_late,)) = _wgrad(
        h, dproj, full_tok, pl.BlockSpec((SEQ, IN_S), lambda s: (0, s)), (D_MODEL, IN_S), "wgrad_in",
        comms=[_rs_join([half[n] for n in ffn]), _small_exchange(packed_late)])
    joined.update(zip(ffn, got))
    ((sib["w_in"],),) = _comm_only("comm_rs_sibling_in", [_rs_sibling([gb["w_in"]])])
    (part["w_in"],) = _chip_sums([gb["w_in"]], [sib["w_in"]], shard_core)
    x_in = _rs_chips([part["w_in"]])
    (s_in,), token = _split_starts("rs_in_start", [x_in], small["pre_mix_norm"])

    grads, deltas, new_m, new_v = {}, {}, {}, {}

    def record(n, outs):
        grads[n], deltas[n], new_m[n], new_v[n] = (
            jnp.swapaxes(o[None], 1, 2) if n in flipped else o[None] for o in outs)

    def update(names, name, after):
        for n, outs in zip(names, _adamw_multi(
                [big(n, n) for n in names], [joined[n].reshape(big(n, n).shape) for n in names],
                [big("m_" + n, n) for n in names], [big("v_" + n, n) for n in names], name, after)):
            record(n, outs)

    update(ffn, "adamw_ffn", [token])
    (far["w_out"],) = _split_wait("rs_out_wait", x_out, *s_out, new_v["w_down"])
    (slots_early,) = _split_wait("small_early_wait", x_small, *s_small, far["w_out"])
    (slots_wsp,) = _split_wait("small_wsp_wait", x_wsp, *s_wsp, slots_early)
    (far["w_in"],) = _split_wait("rs_in_wait", x_in, *s_in, slots_wsp)
    last = ("w_in", "w_out")
    half.update(zip(last, _final_sums([gw[n] for n in last], [sib[n] for n in last], [far[n] for n in last],
                                      shard_core)))
    (got,) = _comm_only("comm_rs_join_in_out", [_rs_join([half[n] for n in last])])
    joined.update(zip(last, got))
    update(last, "adamw_in_out", [])
    a[LOSS_ROW] = a["m_" + LOSS_ROW] = a["v_" + LOSS_ROW] = jnp.zeros((1, D_MODEL), F32)
    outs = _adamw_small(packed_wsp, slots_wsp, *[a[p + SMALL_WSP].reshape(wsp_view) for p in ("", "m_", "v_")], me)
    for dst, buf in zip((grads, deltas, new_m, new_v), outs):
        dst[SMALL_WSP] = buf.reshape(a[SMALL_WSP].shape)
    for names, rows, packed, slots in ((SMALL_EARLY, 8, packed_early, slots_early),
                                       (SMALL_LATE, 8, packed_late, slots_late)):
        outs = _adamw_small(packed, slots, _pack(a, names, rows), _pack({n: a["m_" + n] for n in names}, names, rows),
                            _pack({n: a["v_" + n] for n in names}, names, rows), me)
        for dst, buf in zip((grads, deltas, new_m, new_v), outs):
            dst.update(_unpack(buf, names, {n: a[n].shape for n in names}))
    loss = jnp.sum(grads[LOSS_ROW]) * np.float32(0.5 / D_MODEL)
    return (loss, grad_x[None], *[grads[n] for n in WEIGHTS], *[deltas[n] for n in WEIGHTS],
            *[new_m[n] for n in WEIGHTS], *[new_v[n] for n in WEIGHTS])
```

```python
import numpy as np
import jax
import jax.numpy as jnp
from jax import lax
from jax.experimental import pallas as pl
from jax.experimental.pallas import tpu as pltpu

F32 = jnp.float32
BF16 = jnp.bfloat16

SEQ = 2048
D_MODEL = 1024
HEAD_DIM = 64
ATTN_W = 512
SGU_W = 512
SGU_GROUPS = 8
CHUNK = 128
DILATIONS = (1, 4, 16)
N_SHARD = 4
IN_S = 640
OUT_S = 256
FF_S = 704
PROJ_W = N_SHARD * IN_S
FF = N_SHARD * FF_S
FF_CHUNKS = ((0, 1024), (1024, 2048), (2048, FF))
FF_BLOCK = 256
RMS_EPS = 1e-6
LN_EPS = 1e-5
ROPE_THETA = 500000.0
ATTN_SCALE = 1.0 / np.sqrt(HEAD_DIM)
NEG = -1e30
TM = 512
TM_FFN = 256
VMEM_LIMIT = 56 * 1024 * 1024

ADAM_LR = 0.001
ADAM_B1 = 0.9
ADAM_B2 = 0.999
ADAM_EPS = 1e-08
ADAM_WD = 0.01
ADAM_STEP = 10

MESH = pl.DeviceIdType.MESH
ANY = pl.BlockSpec(memory_space=pl.ANY)


def _dot(a, b):
    return jnp.dot(a, b, preferred_element_type=F32)


def _dot_nt(a, b):
    return lax.dot_general(a, b, (((1,), (1,)), ((), ())), preferred_element_type=F32)


def _dot_tn(a, b):
    return lax.dot_general(a, b, (((0,), (0,)), ((), ())), preferred_element_type=F32)


def _dot_exact(a, b):
    return jnp.dot(a, b, preferred_element_type=F32, precision=lax.Precision.HIGHEST)


def _dot_select(a, sel):
    hi = a.astype(BF16)
    lo = (a - hi.astype(F32)).astype(BF16)
    sel = sel.astype(BF16)
    return _dot(hi, sel) + _dot(lo, sel)


def _rms_stats(x):
    r = lax.rsqrt(jnp.mean(x * x, axis=-1, keepdims=True) + RMS_EPS)
    return x * r, r


def _rms_bwd(xh, r, gain, dy):
    dxh = dy * gain
    dx = r * (dxh - xh * jnp.mean(dxh * xh, axis=-1, keepdims=True))
    return dx, jnp.sum(dy * xh, axis=0, keepdims=True)


_ERF_ALPHA = (-2.72614225801306e-10, 2.77068142495902e-08, -2.10102402082508e-06, -5.69250639462346e-05,
              -7.34990630326855e-04, -2.95459980854025e-03, -1.60960333262415e-02)
_ERF_BETA = (-1.45660718464996e-05, -2.13374055278905e-04, -1.68282697438203e-03, -7.37332916720468e-03,
             -1.42647390514189e-02)


def _erf(x):
    x = jnp.clip(x, -4.0, 4.0)
    x2 = x * x
    p = jnp.full_like(x, _ERF_ALPHA[0])
    for a in _ERF_ALPHA[1:]:
        p = p * x2 + a
    q = jnp.full_like(x, _ERF_BETA[0])
    for b in _ERF_BETA[1:]:
        q = q * x2 + b
    return x * p / q


def _normal_cdf(x):
    return 0.5 * (1.0 + _erf(x * np.float32(1.0 / np.sqrt(2.0))))


def _gelu_grad(x, cdf):
    pdf = jnp.exp(-0.5 * x * x) * np.float32(1.0 / np.sqrt(2.0 * np.pi))
    return cdf + x * pdf


def _sigmoid(x):
    return 1.0 / (1.0 + jnp.exp(-x))


_INV_FREQ = tuple(float(np.float32(ROPE_THETA ** (-2.0 * j / 16.0))) for j in range(8))


def _rot_tables(pos):
    lane = lax.broadcasted_iota(jnp.int32, (1, 128), 1)
    d = lane & 63
    j = d & 7
    inv = jnp.zeros((1, 128), F32)
    for jj in range(8):
        inv = jnp.where(j == jj, _INV_FREQ[jj], inv)
    ang = pos.astype(F32) * inv
    c = jnp.cos(ang)
    s = jnp.sin(ang)
    cos_t = jnp.where(d < 16, c, 1.0)
    sin_a = jnp.where(d < 8, -s, 0.0)
    sin_b = jnp.where((d >= 8) & (d < 16), s, 0.0)
    return tuple(jnp.tile(t, (1, 4)) for t in (cos_t, sin_a, sin_b))


def _rope(x, tabs):
    cos_t, sin_a, sin_b = tabs
    return x * cos_t + pltpu.roll(x, 504, 1) * sin_a + pltpu.roll(x, 8, 1) * sin_b


def _rope_bwd(dy, tabs):
    cos_t, sin_a, sin_b = tabs
    return dy * cos_t + pltpu.roll(dy * sin_a, 8, 1) + pltpu.roll(dy * sin_b, 504, 1)


def _left_half():
    return lax.broadcasted_iota(jnp.int32, (CHUNK, CHUNK), 1) < HEAD_DIM


def _group_ones():
    lane = lax.broadcasted_iota(jnp.int32, (SGU_GROUPS, SGU_W), 1)
    row = lax.broadcasted_iota(jnp.int32, (SGU_GROUPS, SGU_W), 0)
    return ((lane >> 6) == row).astype(F32)


def _masked_spatial(w_ref):
    row = lax.broadcasted_iota(jnp.int32, (CHUNK, CHUNK), 0)
    col = lax.broadcasted_iota(jnp.int32, (CHUNK, CHUNK), 1)
    return [jnp.where(col <= row, w_ref[g], 0.0).astype(BF16) for g in range(SGU_GROUPS)]


def _sgu_core(u, vs, lg, lb, wm, bias_full):
    tm = u.shape[0]
    cdf_u, cdf_vs = _normal_cdf(u), _normal_cdf(vs)
    gu = u * cdf_u
    gv = vs * cdf_vs
    mu = jnp.mean(gv, axis=-1, keepdims=True)
    xc = gv - mu
    rstd = lax.rsqrt(jnp.mean(xc * xc, axis=-1, keepdims=True) + LN_EPS)
    xh = xc * rstd
    vnb = (xh * lg + lb).astype(BF16)
    left = _left_half()
    rows = []
    for c in range(tm // CHUNK):
        pieces = []
        for p in range(4):
            vp = vnb[c * CHUNK:(c + 1) * CHUNK, p * 128:(p + 1) * 128]
            pieces.append(jnp.where(left, _dot(wm[2 * p], vp), _dot(wm[2 * p + 1], vp)))
        rows.append(jnp.concatenate(pieces, axis=1) + bias_full)
    mixed = jnp.concatenate(rows, axis=0)
    return gu, xh, rstd, vnb, mixed, cdf_u, cdf_vs


def _resident(shape):
    n = len(shape)
    return pl.BlockSpec(shape, lambda *_: (0,) * n, pipeline_mode=pl.Buffered(1))


def _rows(ncol, tm=TM):
    return pl.BlockSpec((tm, ncol), lambda i: (i, 0))


def _acc(ncol, nrow=1):
    return pl.BlockSpec((nrow, ncol), lambda i: (0, 0))


HEAD_W = 128


def _view_rows(dil, width=ATTN_W, tm=TM):
    return pl.BlockSpec((tm // dil, dil * width), lambda i: (i, 0))


def _view_shape(dil, dtype, width=ATTN_W):
    return _sds((SEQ // dil, dil * width), dtype)


def _slab_scratch():
    return pltpu.VMEM((4, TM, 128), F32)


def _store_view(val, out_ref, slabs, dil):
    width = val.shape[1]
    for j in range(width // 128):
        slabs[j] = val[:, j * 128:(j + 1) * 128]
    for r in range(dil):
        for j in range(width // 128):
            c0 = r * width + j * 128
            out_ref[:, c0:c0 + 128] = slabs.at[j][pl.ds(r, TM // dil, stride=dil), :].astype(out_ref.dtype)


def _load_view(in_ref, slabs, dil, width=ATTN_W):
    for r in range(dil):
        for j in range(width // 128):
            c0 = r * width + j * 128
            slabs.at[j][pl.ds(r, TM // dil, stride=dil), :] = in_ref[:, c0:c0 + 128].astype(F32)
    return jnp.concatenate([slabs[j] for j in range(width // 128)], axis=1)


def _head_spread():
    m = lax.broadcasted_iota(jnp.int32, (HEAD_W, ATTN_W), 0)
    lane = lax.broadcasted_iota(jnp.int32, (HEAD_W, ATTN_W), 1)
    return (m == 16 * (lane >> 6)).astype(F32)


def _head_sum():
    lane = lax.broadcasted_iota(jnp.int32, (ATTN_W, HEAD_W), 0)
    m = lax.broadcasted_iota(jnp.int32, (ATTN_W, HEAD_W), 1)
    return ((lane >> 6) == (m >> 4)).astype(F32)


def _seq_params():
    return pltpu.CompilerParams(dimension_semantics=("arbitrary",), vmem_limit_bytes=VMEM_LIMIT)


def _sds(shape, dtype):
    return jax.ShapeDtypeStruct(shape, dtype)


class _Comm:
    def __init__(self, args, out_shape, n_sems, start, finish, aliased=False, landing=None):
        self.args, self.out_shape, self.n_sems = list(args), list(out_shape), n_sems
        self.start, self.finish, self.aliased, self.landing = start, finish, aliased, landing


def _pcall(body, *, name, grid, in_specs, out_specs, out_shape, args, scratch_shapes=(), comms=(), after=()):
    single = not isinstance(out_shape, (list, tuple))
    out_specs = [out_specs] if single else list(out_specs)
    out_shape = [out_shape] if single else list(out_shape)
    n_in, n_out, n_scr = len(in_specs), len(out_shape), len(scratch_shapes)
    c_args = [a for c in comms for a in c.args]
    c_outs = [o for c in comms for o in c.out_shape]
    aliases, ai, ao = {}, n_in, n_out
    for c in comms:
        if c.aliased:
            aliases.update({ai + k: ao + k for k in range(len(c.args))})
        ai += len(c.args)
        ao += len(c.out_shape)
    sems = [pltpu.SemaphoreType.DMA((c.n_sems,)) for c in comms for _ in range(2)]
    steps = grid[0]

    def wrapped(*refs):
        o0 = n_in + len(c_args) + len(after)
        s0 = o0 + n_out + len(c_outs)
        m_in, m_out, m_sem = refs[n_in:n_in + len(c_args)], refs[o0 + n_out:s0], refs[s0 + n_scr:]

        def each(phase):
            ii = oi = 0
            for k, c in enumerate(comms):
                getattr(c, phase)(m_in[ii:ii + len(c.args)], m_out[oi:oi + len(c.out_shape)],
                                  m_sem[2 * k], m_sem[2 * k + 1])
                ii += len(c.args)
                oi += len(c.out_shape)

        if comms:
            @pl.when(pl.program_id(0) == 0)
            def _():
                each("start")

        body(*refs[:n_in], *refs[o0:o0 + n_out], *refs[s0:s0 + n_scr])

        if comms:
            @pl.when(pl.program_id(0) == steps - 1)
            def _():
                each("finish")

    res = pl.pallas_call(
        wrapped, name=name, grid=grid,
        in_specs=list(in_specs) + [ANY] * (len(c_args) + len(after)), out_specs=out_specs + [ANY] * len(c_outs),
        out_shape=out_shape + c_outs, scratch_shapes=list(scratch_shapes) + sems,
        input_output_aliases=aliases, compiler_params=_seq_params(),
    )(*args, *c_args, *after)
    mine = res[0] if single else list(res[:n_out])
    if not comms:
        return mine
    theirs, oi = [], n_out
    for c in comms:
        theirs.append(list(res[oi:oi + len(c.out_shape)]))
        oi += len(c.out_shape)
    return mine, theirs


def _in_pieces(g):
    lo, hi = 512 * g, 512 * (g + 1)
    return [(s, max(lo, IN_S * s) - IN_S * s, min(hi, IN_S * (s + 1)) - IN_S * s)
            for s in range(N_SHARD) if max(lo, IN_S * s) < min(hi, IN_S * (s + 1))]


def _inproj_fwd(x, pos, g_pre, w_in, lg, lb, w_sp, b_t, comms=()):
    def body(x_ref, pos_ref, g_ref, w_ref, lg_ref, lb_ref, wsp_ref, bt_ref, h_ref, u_ref, vs_ref, sgu_ref, *rest):
        qkv_refs, slabs = rest[:9], rest[9:]
        xh, _ = _rms_stats(x_ref[...])
        h = (xh * g_ref[...]).astype(BF16)
        h_ref[...] = h
        tabs = _rot_tables(pos_ref[...])

        def group(g):
            return jnp.concatenate([_dot(h, w_ref[s, :, a:b]) for s, a, b in _in_pieces(g)], axis=1)

        for t in range(3):
            val = group(t)
            if t < 2:
                val = _rope(val, tabs)
            if t == 0:
                val = val * np.float32(ATTN_SCALE)
            qkv_refs[t][...] = val.astype(BF16)
            for i, dil in enumerate(DILATIONS[1:]):
                _store_view(val, qkv_refs[3 * (i + 1) + t], slabs[t], dil)
        u = group(3)
        vs = group(4)
        u_ref[...] = u
        vs_ref[...] = vs
        bias_full = _dot_exact(bt_ref[...], _group_ones())
        gu, _, _, _, mixed, _, _ = _sgu_core(u, vs, lg_ref[...], lb_ref[...], _masked_spatial(wsp_ref), bias_full)
        sgu_ref[...] = gu * mixed

    return _pcall(
        body, name="inproj_sgu_fwd", grid=(SEQ // TM,),
        in_specs=[_rows(D_MODEL), _rows(1), _resident((1, D_MODEL)), _resident((N_SHARD, D_MODEL, IN_S)),
                  _resident((1, SGU_W)), _resident((1, SGU_W)), _resident((SGU_GROUPS, CHUNK, CHUNK)),
                  _resident((CHUNK, SGU_GROUPS))],
        out_specs=[_rows(D_MODEL), _rows(512), _rows(512), _rows(512)]
        + [_view_rows(dil) for dil in DILATIONS for _ in range(3)],
        out_shape=[_sds((SEQ, D_MODEL), BF16), _sds((SEQ, 512), F32), _sds((SEQ, 512), F32), _sds((SEQ, 512), F32)]
        + [_view_shape(dil, BF16) for dil in DILATIONS for _ in range(3)],
        scratch_shapes=[_slab_scratch() for _ in range(3)],
        args=(x, pos, g_pre, w_in, lg, lb, w_sp, b_t), comms=comms)


def _block_masks():
    row = lax.broadcasted_iota(jnp.int32, (CHUNK, CHUNK), 0)
    col = lax.broadcasted_iota(jnp.int32, (CHUNK, CHUNK), 1)
    return col <= row, col >= row


def _attn_fwd(qv, kv, vv, dil, comms=()):
    seg = SEQ // dil
    nblk = seg // CHUNK
    rps = 4 if nblk == 1 else 1

    def body(q_ref, k_ref, v_ref, o_ref, l_ref):
        left = _left_half()
        m_cur, m_prev = _block_masks()
        zero = jnp.zeros((CHUNK, CHUNK), BF16)
        ones = (jnp.where(left, 1.0, 0.0).astype(BF16), jnp.where(left, 0.0, 1.0).astype(BF16))

        sides = tuple(enumerate((left, ~left)))

        def rows(b):
            if isinstance(b, int):
                return b * CHUNK, max(b - 1, 0) * CHUNK
            return pl.multiple_of(b * CHUNK, CHUNK), pl.multiple_of(jnp.maximum(b - 1, 0) * CHUNK, CHUNK)

        def first(rr, b):
            r0, rp = rows(b)
            prev_ok = m_prev & (b > 0)
            tiles, scores = [], []
            for hp in range(4):
                ls = slice(rr * ATTN_W + hp * 128, rr * ATTN_W + (hp + 1) * 128)
                qp = q_ref[pl.ds(r0, CHUNK), ls]
                kc = k_ref[pl.ds(r0, CHUNK), ls]
                kp = k_ref[pl.ds(rp, CHUNK), ls] if nblk > 1 else None
                tiles.append((ls, v_ref[pl.ds(r0, CHUNK), ls], v_ref[pl.ds(rp, CHUNK), ls] if nblk > 1 else None))
                for _, hm in sides:
                    qh = jnp.where(hm, qp, zero)
                    sc = jnp.where(m_cur, _dot_nt(qh, kc), NEG)
                    sp = jnp.where(prev_ok, _dot_nt(qh, kp), NEG) if nblk > 1 else None
                    scores.append((sc, sp))
            return tiles, scores

        def second(scores):
            probs = []
            for sc, sp in scores:
                if nblk > 1:
                    m = jnp.max(jnp.maximum(sc, sp), axis=-1, keepdims=True)
                    pc = jnp.exp(sc - m)
                    pp = jnp.exp(sp - m)
                    probs.append((m, pc.astype(BF16), pp.astype(BF16), (pc + pp).astype(BF16)))
                else:
                    m = jnp.max(sc, axis=-1, keepdims=True)
                    pc = jnp.exp(sc - m).astype(BF16)
                    probs.append((m, pc, None, pc))
            return probs

        def third(rr, b, tiles, probs):
            r0, _ = rows(b)
            for hp, (ls, vc, vp) in enumerate(tiles):
                acc = jnp.zeros((CHUNK, CHUNK), F32)
                den = jnp.zeros((CHUNK, CHUNK), F32)
                for side, hm in sides:
                    _, pc, pp, psum = probs[2 * hp + side]
                    acc = acc + _dot(pc, jnp.where(hm, vc, zero))
                    if nblk > 1:
                        acc = acc + _dot(pp, jnp.where(hm, vp, zero))
                    den = den + _dot(psum, ones[side])
                o_ref[pl.ds(r0, CHUNK), ls] = (acc / den).astype(o_ref.dtype)
                lse = jnp.where(left, probs[2 * hp][0], probs[2 * hp + 1][0]) + jnp.log(den)
                l_ref[pl.ds(r0, CHUNK), rr * HEAD_W + 32 * hp:rr * HEAD_W + 32 * hp + 32] = lse[:, 48:80]

        def run(units):
            data = [first(rr, b) for rr, b in units]
            probs = [second(scores) for _, scores in data]
            for (rr, b), (tiles, _), pr in zip(units, data, probs):
                third(rr, b, tiles, pr)

        if nblk == 1:
            run([(rr, 0) for rr in range(rps)])
        else:
            def one(b, carry):
                run([(0, b)])
                return carry

            lax.fori_loop(0, nblk, one, 0)

    spec = pl.BlockSpec((seg, rps * ATTN_W), lambda r: (0, r))
    return _pcall(
        body, name=f"attn_fwd_d{dil}", grid=(dil // rps,),
        in_specs=[spec, spec, spec], out_specs=[spec, pl.BlockSpec((seg, rps * HEAD_W), lambda r: (0, r))],
        out_shape=[_sds((seg, dil * ATTN_W), BF16), _sds((seg, dil * HEAD_W), F32)],
        args=(qv, kv, vv), comms=comms)


def _lane_left(nrows):
    return lax.broadcasted_iota(jnp.int32, (nrows, CHUNK), 1) < HEAD_DIM


def _attn_rows(b):
    if isinstance(b, int):
        return b * CHUNK, max(b - 1, 0) * CHUNK
    return pl.multiple_of(b * CHUNK, CHUNK), pl.multiple_of(jnp.maximum(b - 1, 0) * CHUNK, CHUNK)


def _mix_out_fwd(o_list, l_list, sgu, x, w_out, g_attn, g_sgu, g_post, comms=()):
    def body(o1, o2, o3, l1, l2, l3, sgu_ref, x_ref, w_ref, ga_ref, gs_ref, gp_ref,
             attn_ref, mixed_ref, y_ref, x1_ref, lse1_ref, lse2_ref, lse3_ref, slabs_a, slabs_b):
        os = [o1[...], _load_view(o2, slabs_a, DILATIONS[1]), _load_view(o3, slabs_b, DILATIONS[2])]
        ls = [l1[...], _load_view(l2, slabs_a, DILATIONS[1], HEAD_W), _load_view(l3, slabs_b, DILATIONS[2], HEAD_W)]
        m = jnp.maximum(jnp.maximum(ls[0], ls[1]), ls[2])
        es = [jnp.exp(l - m) for l in ls]
        den = es[0] + es[1] + es[2]
        spread = _head_spread()
        attn = sum(_dot_select(e / den, spread) * o for e, o in zip(es, os))
        attn_ref[...] = attn
        lse = m + jnp.log(den)
        lse1_ref[...] = lse
        _store_view(lse, lse2_ref, slabs_a, DILATIONS[1])
        _store_view(lse, lse3_ref, slabs_b, DILATIONS[2])
        ah, _ = _rms_stats(attn)
        sh, _ = _rms_stats(sgu_ref[...])
        mixed = jnp.concatenate([ah * ga_ref[...], sh * gs_ref[...]], axis=1).astype(BF16)
        mixed_ref[...] = mixed
        y = _dot(mixed[:, 0:OUT_S], w_ref[0])
        for s in range(1, N_SHARD):
            y = y + _dot(mixed[:, s * OUT_S:(s + 1) * OUT_S], w_ref[s])
        y_ref[...] = y
        yh, _ = _rms_stats(y)
        x1_ref[...] = x_ref[...] + yh * gp_ref[...]

    return _pcall(
        body, name="mix_out_fwd", grid=(SEQ // TM,),
        in_specs=[_view_rows(dil) for dil in DILATIONS] + [_view_rows(dil, HEAD_W) for dil in DILATIONS]
        + [_rows(512), _rows(D_MODEL), _resident((N_SHARD, OUT_S, D_MODEL)),
           _resident((1, 512)), _resident((1, 512)), _resident((1, D_MODEL))],
        out_specs=[_rows(512), _rows(D_MODEL), _rows(D_MODEL), _rows(D_MODEL)]
        + [_view_rows(dil, HEAD_W) for dil in DILATIONS],
        out_shape=[_sds((SEQ, 512), F32), _sds((SEQ, D_MODEL), BF16), _sds((SEQ, D_MODEL), F32),
                   _sds((SEQ, D_MODEL), F32)] + [_view_shape(dil, F32, HEAD_W) for dil in DILATIONS],
        scratch_shapes=[_slab_scratch(), _slab_scratch()],
        args=(*o_list, *l_list, sgu, x, w_out, g_attn, g_sgu, g_post), comms=comms)


def _ffn_fwd_bwd(x1, target, w_gate, w_up, w_down, g_pre, g_post, comms=()):
    def body(x1_ref, t_ref, wg_ref, wu_ref, wd_ref, gpf_ref, gpo_ref,
             h2_ref, a_ref, dg_ref, dup_ref, df_ref, dx1_ref, loss_ref, dgpf_ref, dgpo_ref, g_scr, up_scr):
        @pl.when(pl.program_id(0) == 0)
        def _():
            loss_ref[...] = jnp.zeros_like(loss_ref)
            dgpf_ref[...] = jnp.zeros_like(dgpf_ref)
            dgpo_ref[...] = jnp.zeros_like(dgpo_ref)

        def store_blocks(ref, c0, val):
            for j in range(val.shape[1] // FF_BLOCK):
                ref[c0 // FF_BLOCK + j] = val[:, j * FF_BLOCK:(j + 1) * FF_BLOCK]

        x1 = x1_ref[...]
        gpf = gpf_ref[...]
        gpo = gpo_ref[...]
        xh, r = _rms_stats(x1)
        h2 = (xh * gpf).astype(BF16)
        h2_ref[...] = h2
        f = jnp.zeros((TM_FFN, D_MODEL), F32)
        for c0, c1 in FF_CHUNKS:
            g = _dot_nt(h2, wg_ref[c0:c1, :])
            up = _dot_nt(h2, wu_ref[c0:c1, :])
            g_scr[:, c0:c1] = g
            up_scr[:, c0:c1] = up
            a = (g * _sigmoid(g) * up).astype(BF16)
            store_blocks(a_ref, c0, a)
            f = f + _dot(a, wd_ref[c0:c1, :])
        fh, rf = _rms_stats(f)
        diff = x1 + fh * gpo - t_ref[...]
        loss_ref[...] += jnp.sum(diff * diff, axis=0, keepdims=True)
        dout = diff * np.float32(1.0 / D_MODEL)
        df, dgpo = _rms_bwd(fh, rf, gpo, dout)
        dgpo_ref[...] += dgpo
        dfb = df.astype(BF16)
        df_ref[...] = dfb
        dh2 = jnp.zeros((TM_FFN, D_MODEL), F32)
        for c0, c1 in FF_CHUNKS:
            da = _dot_nt(dfb, wd_ref[c0:c1, :])
            g = g_scr[:, c0:c1]
            up = up_scr[:, c0:c1]
            sg = _sigmoid(g)
            dup = (da * (g * sg)).astype(BF16)
            dg = (da * up * (sg * (1.0 + g * (1.0 - sg)))).astype(BF16)
            store_blocks(dg_ref, c0, dg)
            store_blocks(dup_ref, c0, dup)
            dh2 = dh2 + _dot(dg, wg_ref[c0:c1, :]) + _dot(dup, wu_ref[c0:c1, :])
        dx, dgpf = _rms_bwd(xh, r, gpf, dh2)
        dgpf_ref[...] += dgpf
        dx1_ref[...] = dout + dx

    blocked = pl.BlockSpec((FF // FF_BLOCK, TM_FFN, FF_BLOCK), lambda i: (0, i, 0))
    return _pcall(
        body, name="ffn_fwd_bwd", grid=(SEQ // TM_FFN,),
        in_specs=[_rows(D_MODEL, TM_FFN), _rows(D_MODEL, TM_FFN), _resident((FF, D_MODEL)),
                  _resident((FF, D_MODEL)), _resident((FF, D_MODEL)),
                  _resident((1, D_MODEL)), _resident((1, D_MODEL))],
        out_specs=[_rows(D_MODEL, TM_FFN), blocked, blocked, blocked,
                   _rows(D_MODEL, TM_FFN), _rows(D_MODEL, TM_FFN), _acc(D_MODEL), _acc(D_MODEL), _acc(D_MODEL)],
        out_shape=[_sds((SEQ, D_MODEL), BF16)] + [_sds((FF // FF_BLOCK, SEQ, FF_BLOCK), BF16)] * 3
        + [_sds((SEQ, D_MODEL), BF16), _sds((SEQ, D_MODEL), F32)] + [_sds((1, D_MODEL), F32)] * 3,
        scratch_shapes=[pltpu.VMEM((TM_FFN, FF), F32), pltpu.VMEM((TM_FFN, FF), F32)],
        args=(x1, target, w_gate, w_up, w_down, g_pre, g_post), comms=comms)


def _wgrad(a, b, a_spec, b_spec, out_block, name, comms=()):
    def body(a_ref, b_ref, o_ref, ob_ref):
        av = a_ref[0] if len(a_ref.shape) == 3 else a_ref[...]
        bv = b_ref[0] if len(b_ref.shape) == 3 else b_ref[...]
        g = _dot_tn(av, bv)
        o_ref[0] = g
        ob_ref[0] = g.astype(BF16)

    return _pcall(
        body, name=name, grid=(N_SHARD,),
        in_specs=[a_spec, b_spec],
        out_specs=[pl.BlockSpec((1,) + out_block, lambda s: (s, 0, 0))] * 2,
        out_shape=[_sds((N_SHARD,) + out_block, F32), _sds((N_SHARD,) + out_block, BF16)],
        args=(a, b), comms=comms)


def _outproj_bwd(dx1, y, attn, sgu, w_out, g_post, g_attn, g_sgu, u, vs, lg, lb, w_sp, b_t, comms=(), after=()):
    sgu_bwd = _sgu_bwd_body()

    def body(dx1_ref, y_ref, attn_ref, sgu_ref, w_ref, gp_ref, ga_ref, gs_ref, u_ref, vs_ref, lg_ref, lb_ref,
             wsp_ref, bt_ref, dy_ref, du_ref, dvs_ref, dgp_ref, dga_ref, dgs_ref, dw_ref, db_ref, dlg_ref, dlb_ref,
             *rest):
        dattn_refs, delta_refs, (slabs_a, slabs_b, dsgu_ref, dbias_scr) = rest[0:3], rest[3:6], rest[6:]

        @pl.when(pl.program_id(0) == 0)
        def _():
            dgp_ref[...] = jnp.zeros_like(dgp_ref)
            dga_ref[...] = jnp.zeros_like(dga_ref)
            dgs_ref[...] = jnp.zeros_like(dgs_ref)

        yh, ry = _rms_stats(y_ref[...])
        dy, dgp = _rms_bwd(yh, ry, gp_ref[...], dx1_ref[...])
        dgp_ref[...] += dgp
        dyb = dy.astype(BF16)
        dy_ref[...] = dyb
        dmixed = jnp.concatenate([_dot_nt(dyb, w_ref[s]) for s in range(N_SHARD)], axis=1)
        attn = attn_ref[...]
        ah, ra = _rms_stats(attn)
        dattn, dga = _rms_bwd(ah, ra, ga_ref[...], dmixed[:, 0:512])
        dga_ref[...] += dga
        sh, rs = _rms_stats(sgu_ref[...])
        dsgu, dgs = _rms_bwd(sh, rs, gs_ref[...], dmixed[:, 512:1024])
        dgs_ref[...] += dgs
        dsgu_ref[...] = dsgu
        delta = _dot_select(dattn * attn, _head_sum())
        dattn_refs[0][...] = dattn.astype(BF16)
        delta_refs[0][...] = delta
        for i, dil in enumerate(DILATIONS[1:]):
            _store_view(dattn, dattn_refs[i + 1], slabs_a, dil)
            _store_view(delta, delta_refs[i + 1], slabs_b, dil)
        sgu_bwd(u_ref, vs_ref, dsgu_ref, lg_ref, lb_ref, wsp_ref, bt_ref,
                du_ref, dvs_ref, dw_ref, db_ref, dlg_ref, dlb_ref, dbias_scr)

    return _pcall(
        body, name="outproj_sgu_bwd", grid=(SEQ // TM,),
        in_specs=[_rows(D_MODEL), _rows(D_MODEL), _rows(512), _rows(512), _resident((N_SHARD, OUT_S, D_MODEL)),
                  _resident((1, D_MODEL)), _resident((1, 512)), _resident((1, 512)),
                  _rows(SGU_W), _rows(SGU_W), _resident((1, SGU_W)), _resident((1, SGU_W)),
                  _resident((SGU_GROUPS, CHUNK, CHUNK)), _resident((CHUNK, SGU_GROUPS))],
        out_specs=[_rows(D_MODEL), _rows(SGU_W), _rows(SGU_W), _acc(D_MODEL), _acc(512), _acc(512),
                   pl.BlockSpec((SGU_GROUPS, CHUNK, CHUNK), lambda i: (0, 0, 0)), _acc(CHUNK, SGU_GROUPS),
                   _acc(SGU_W), _acc(SGU_W)]
        + [_view_rows(dil) for dil in DILATIONS] + [_view_rows(dil, HEAD_W) for dil in DILATIONS],
        out_shape=[_sds((SEQ, D_MODEL), BF16), _sds((SEQ, SGU_W), BF16), _sds((SEQ, SGU_W), BF16),
                   _sds((1, D_MODEL), F32), _sds((1, 512), F32), _sds((1, 512), F32),
                   _sds((SGU_GROUPS, CHUNK, CHUNK), F32), _sds((SGU_GROUPS, CHUNK), F32),
                   _sds((1, SGU_W), F32), _sds((1, SGU_W), F32)]
        + [_view_shape(dil, BF16) for dil in DILATIONS] + [_view_shape(dil, F32, HEAD_W) for dil in DILATIONS],
        scratch_shapes=[_slab_scratch(), _slab_scratch(), pltpu.VMEM((TM, SGU_W), F32),
                        pltpu.VMEM((CHUNK, SGU_W), F32)],
        args=(dx1, y, attn, sgu, w_out, g_post, g_attn, g_sgu, u, vs, lg, lb, w_sp, b_t), comms=comms, after=after)


def _sgu_bwd_body():
    nsteps = SEQ // TM

    def body(u_ref, vs_ref, ds_ref, lg_ref, lb_ref, w_ref, bt_ref,
             du_ref, dvs_ref, dw_ref, db_ref, dlg_ref, dlb_ref, dbias_scr):
        i = pl.program_id(0)

        @pl.when(i == 0)
        def _():
            dw_ref[...] = jnp.zeros_like(dw_ref)
            dlg_ref[...] = jnp.zeros_like(dlg_ref)
            dlb_ref[...] = jnp.zeros_like(dlb_ref)
            dbias_scr[...] = jnp.zeros_like(dbias_scr)

        wm = _masked_spatial(w_ref)
        ones_g = _group_ones()
        bias_full = _dot_exact(bt_ref[...], ones_g)
        u = u_ref[...]
        vs = vs_ref[...]
        lg = lg_ref[...]
        gu, xh, rstd, vnb, mixed, cdf_u, cdf_vs = _sgu_core(u, vs, lg, lb_ref[...], wm, bias_full)
        dsgu = ds_ref[...]
        du_ref[...] = (dsgu * mixed * _gelu_grad(u, cdf_u)).astype(BF16)
        dmixed = dsgu * gu
        left = _left_half()
        dvn_rows = []
        for c in range(TM // CHUNK):
            rs = slice(c * CHUNK, (c + 1) * CHUNK)
            dm_c = dmixed[rs, :]
            dbias_scr[...] += dm_c
            pieces = []
            for p in range(4):
                ls = slice(p * 128, (p + 1) * 128)
                dmp = dm_c[:, ls]
                vp = vnb[rs, ls]
                dmb = dmp.astype(BF16)
                zero = jnp.zeros_like(dmb)
                dw_ref[2 * p] += _dot_nt(jnp.where(left, dmb, zero), vp)
                dw_ref[2 * p + 1] += _dot_nt(jnp.where(left, zero, dmb), vp)
                pieces.append(jnp.where(left, _dot_tn(wm[2 * p], dmb), _dot_tn(wm[2 * p + 1], dmb)))
            dvn_rows.append(jnp.concatenate(pieces, axis=1))
        dvn = jnp.concatenate(dvn_rows, axis=0)
        dlg_ref[...] += jnp.sum(dvn * xh, axis=0, keepdims=True)
        dlb_ref[...] += jnp.sum(dvn, axis=0, keepdims=True)
        dxh = dvn * lg
        dgv = rstd * (dxh - jnp.mean(dxh, axis=-1, keepdims=True) - xh * jnp.mean(dxh * xh, axis=-1, keepdims=True))
        dvs_ref[...] = (dgv * _gelu_grad(vs, cdf_vs)).astype(BF16)

        @pl.when(i == nsteps - 1)
        def _():
            row = lax.broadcasted_iota(jnp.int32, (CHUNK, CHUNK), 0)
            col = lax.broadcasted_iota(jnp.int32, (CHUNK, CHUNK), 1)
            for g in range(SGU_GROUPS):
                dw_ref[g] = jnp.where(col <= row, dw_ref[g], 0.0)
            db_ref[...] = lax.dot_general(ones_g, dbias_scr[...], (((1,), (1,)), ((), ())),
                                          preferred_element_type=F32, precision=lax.Precision.HIGHEST)

    return body


def _attn_bwd(qv, kv, vv, dov, deltav, lsev, dil, comms=(), after=()):
    seg = SEQ // dil
    nblk = seg // CHUNK
    rps = 4 if nblk == 1 else 1

    def body(q_ref, k_ref, v_ref, do_ref, dl_ref, lse_ref, dq_ref, dk_ref, dv_ref, dk_wait, dv_wait):
        left = _left_half()
        m_cur, m_prev = _block_masks()
        sides = tuple(enumerate((left, ~left)))
        zero = jnp.zeros((CHUNK, CHUNK), BF16)

        def first(rr, b, both):
            r0, rp = _attn_rows(b)
            keys = pl.ds(rp, 2 * CHUNK) if both else pl.ds(r0, CHUNK)
            tiles, heads = [], []
            for hp in range(4):
                ls = slice(rr * ATTN_W + hp * 128, rr * ATTN_W + (hp + 1) * 128)
                qp = q_ref[pl.ds(r0, CHUNK), ls]
                dop = do_ref[pl.ds(r0, CHUNK), ls]
                k2 = k_ref[keys, ls]
                v2 = v_ref[keys, ls]
                tiles.append((ls, k2))
                for _, hm in sides:
                    qh = jnp.where(hm, qp, zero)
                    doh = jnp.where(hm, dop, zero)
                    heads.append((qh, doh, _dot_nt(k2, qh), _dot_nt(v2, doh)))
            return tiles, heads

        def second(rr, b, heads, both):
            r0, _ = _attn_rows(b)
            ok = jnp.concatenate([m_cur, m_prev], axis=0) if both else m_prev
            lanes = slice(rr * HEAD_W, (rr + 1) * HEAD_W)
            lse_t = lse_ref[pl.ds(r0, CHUNK), lanes].T
            dl_t = dl_ref[pl.ds(r0, CHUNK), lanes].T
            out = []
            for i, (_, _, s_t, dp_t) in enumerate(heads):
                p = jnp.exp(jnp.where(ok, s_t - lse_t[16 * i:16 * i + 1, :], NEG))
                out.append((p.astype(BF16), (p * (dp_t - dl_t[16 * i:16 * i + 1, :])).astype(BF16)))
            return out

        def third(rr, b, tiles, heads, probs, both):
            r0, rp = _attn_rows(b)
            nk = 2 * CHUNK if both else CHUNK
            left_k = _lane_left(nk)
            zero_k = jnp.zeros((nk, CHUNK), BF16)
            for hp, (ls, k2) in enumerate(tiles):
                dq = jnp.zeros((CHUNK, CHUNK), F32)
                dk2 = jnp.zeros((nk, CHUNK), F32)
                dv2 = jnp.zeros((nk, CHUNK), F32)
                for side in range(2):
                    qh, doh, _, _ = heads[2 * hp + side]
                    p, ds = probs[2 * hp + side]
                    dq = dq + _dot_tn(ds, jnp.where(left_k if side == 0 else ~left_k, k2, zero_k))
                    dk2 = dk2 + _dot(ds, qh)
                    dv2 = dv2 + _dot(p, doh)
                dq_ref[pl.ds(r0, CHUNK), ls] = dq.astype(dq_ref.dtype)
                if nblk == 1:
                    dk_ref[pl.ds(r0, CHUNK), ls] = dk2.astype(dk_ref.dtype)
                    dv_ref[pl.ds(r0, CHUNK), ls] = dv2.astype(dv_ref.dtype)
                elif both:
                    dk_ref[pl.ds(rp, CHUNK), ls] = (dk_wait[:, ls] + dk2[0:CHUNK]).astype(dk_ref.dtype)
                    dv_ref[pl.ds(rp, CHUNK), ls] = (dv_wait[:, ls] + dv2[0:CHUNK]).astype(dv_ref.dtype)
                    dk_wait[:, ls] = dk2[CHUNK:]
                    dv_wait[:, ls] = dv2[CHUNK:]
                else:
                    dk_wait[:, ls] = dk2
                    dv_wait[:, ls] = dv2

        def run(units, both):
            data = [first(rr, b, both) for rr, b in units]
            probs = [second(rr, b, heads, both) for (rr, b), (_, heads) in zip(units, data)]
            for (rr, b), (tiles, heads), pr in zip(units, data, probs):
                third(rr, b, tiles, heads, pr, both)

        run([(rr, 0) for rr in range(rps)], False)
        if nblk > 1:
            def one(b, carry):
                run([(0, b)], True)
                return carry

            lax.fori_loop(1, nblk, one, 0)
            last = (nblk - 1) * CHUNK
            dk_ref[last:last + CHUNK, :] = dk_wait[...].astype(dk_ref.dtype)
            dv_ref[last:last + CHUNK, :] = dv_wait[...].astype(dv_ref.dtype)

    spec = pl.BlockSpec((seg, rps * ATTN_W), lambda r: (0, r))
    return _pcall(
        body, name=f"attn_bwd_d{dil}", grid=(dil // rps,),
        in_specs=[spec] * 4 + [pl.BlockSpec((seg, rps * HEAD_W), lambda r: (0, r))] * 2, out_specs=[spec] * 3,
        out_shape=[_sds((seg, dil * ATTN_W), BF16)] * 3,
        scratch_shapes=[pltpu.VMEM((CHUNK, ATTN_W), F32), pltpu.VMEM((CHUNK, ATTN_W), F32)],
        args=(qv, kv, vv, dov, deltav, lsev), comms=comms, after=after)


def _inproj_bwd(dqs, dks, dvs, du, dvs_sgu, pos, x, dx1, w_in, g_pre, comms=()):
    def body(dq1, dq2, dq3, dk1, dk2, dk3, dv1, dv2, dv3, du_ref, dvs_ref, pos_ref, x_ref, dx1_ref, w_ref, g_ref,
             dproj_ref, gx_ref, dg_ref, slabs_a, slabs_b):
        @pl.when(pl.program_id(0) == 0)
        def _():
            dg_ref[...] = jnp.zeros_like(dg_ref)

        def total(r1, r2, r3):
            return r1[...] + _load_view(r2, slabs_a, DILATIONS[1]) + _load_view(r3, slabs_b, DILATIONS[2])

        tabs = _rot_tables(pos_ref[...])
        groups = {3: du_ref[...], 4: dvs_ref[...]}
        dh = jnp.zeros((TM, D_MODEL), F32)
        for g in (3, 4, 0, 1, 2):
            if g == 0:
                groups[g] = _rope_bwd(total(dq1, dq2, dq3) * np.float32(ATTN_SCALE), tabs).astype(BF16)
            elif g == 1:
                groups[g] = _rope_bwd(total(dk1, dk2, dk3), tabs).astype(BF16)
            elif g == 2:
                groups[g] = total(dv1, dv2, dv3).astype(BF16)
            dproj_ref[:, 512 * g:512 * (g + 1)] = groups[g]
            off = 0
            for s, a, b in _in_pieces(g):
                dh = dh + _dot_nt(groups[g][:, off:off + b - a], w_ref[s, :, a:b])
                off += b - a
        g = g_ref[...]
        xh, r = _rms_stats(x_ref[...])
        dx, dg = _rms_bwd(xh, r, g, dh)
        dg_ref[...] += dg
        gx_ref[...] = dx1_ref[...] + dx

    return _pcall(
        body, name="inproj_bwd", grid=(SEQ // TM,),
        in_specs=[_view_rows(dil) for dil in DILATIONS] * 3
        + [_rows(512), _rows(512), _rows(1), _rows(D_MODEL), _rows(D_MODEL),
           _resident((N_SHARD, D_MODEL, IN_S)), _resident((1, D_MODEL))],
        out_specs=[_rows(PROJ_W), _rows(D_MODEL), _acc(D_MODEL)],
        out_shape=[_sds((SEQ, PROJ_W), BF16), _sds((SEQ, D_MODEL), F32), _sds((1, D_MODEL), F32)],
        scratch_shapes=[_slab_scratch(), _slab_scratch()],
        args=(*dqs, *dks, *dvs, du, dvs_sgu, pos, x, dx1, w_in, g_pre), comms=comms)


def _coords():
    return lax.axis_index("x"), lax.axis_index("y"), lax.axis_index("c")


def _other_chips(x, y):
    return [(1 - x, y), (x, 1 - y), (1 - x, 1 - y)]


WEIGHTS = ("pre_mix_norm", "w_in", "sgu_ln_gain", "sgu_ln_bias", "sgu_w_spatial", "sgu_b_spatial", "attn_out_norm",
           "sgu_out_norm", "w_out", "post_mix_norm", "pre_ffn_norm", "w_gate", "w_up", "w_down", "post_ffn_norm")
SMALL = ("pre_mix_norm", "post_mix_norm", "pre_ffn_norm", "post_ffn_norm", "sgu_ln_gain", "sgu_ln_bias",
         "attn_out_norm", "sgu_out_norm", "sgu_w_spatial", "sgu_b_spatial")


def _remote(src, dst, send_sem, recv_sem, to):
    return pltpu.make_async_remote_copy(src_ref=src, dst_ref=dst, send_sem=send_sem, recv_sem=recv_sem,
                                        device_id=to, device_id_type=MESH)


def _halves(a):
    *lead, rows, cols = a.shape
    return a.reshape(*lead, 2, rows // 2, cols)


def _gather_ici(shards, landing=None):
    n = len(shards)

    def desc(ins, outs, ss, rs, j, w, landed):
        x, y, c = _coords()
        cx, cy = _other_chips(x, y)[j]
        shard = 2 * cx + cy if landed else 2 * x + y
        return _remote(ins[w].at[c], outs[w].at[shard, c], ss.at[j * n + w], rs.at[j * n + w], (cx, cy, c))

    def start(ins, outs, ss, rs):
        for j in range(3):
            for w in range(n):
                desc(ins, outs, ss, rs, j, w, False).start()

    def finish(ins, outs, ss, rs):
        for j in range(3):
            for w in range(n):
                desc(ins, outs, ss, rs, j, w, True).wait_recv()
                desc(ins, outs, ss, rs, j, w, False).wait_send()

    return _Comm(shards, [_sds((N_SHARD,) + s.shape, s.dtype) for s in shards], 3 * n, start, finish,
                 landing=landing)


def _gather_pass(fulls):
    n = len(fulls)

    def desc(bufs, ss, rs, j, w, landed):
        x, y, c = _coords()
        cx, cy = _other_chips(x, y)[j]
        shard = 2 * cx + cy
        return _remote(bufs[w].at[shard, c], bufs[w].at[shard, 1 - c if landed else c],
                       ss.at[j * n + w], rs.at[j * n + w], (x, y, 1 - c))

    def start(ins, outs, ss, rs):
        for j in range(3):
            for w in range(n):
                desc(outs, ss, rs, j, w, False).start()

    def finish(ins, outs, ss, rs):
        for j in range(3):
            for w in range(n):
                desc(outs, ss, rs, j, w, True).wait_recv()
                desc(outs, ss, rs, j, w, False).wait_send()

    return _Comm(fulls, [_sds(f.shape, f.dtype) for f in fulls], 3 * n, start, finish, aliased=True)


def _rs_sibling(gws):
    n = len(gws)

    def desc(ins, outs, ss, rs, w):
        x, y, c = _coords()
        return _remote(ins[w].at[:, 1 - c], outs[w], ss.at[w], rs.at[w], (x, y, 1 - c))

    def start(ins, outs, ss, rs):
        for w in range(n):
            desc(ins, outs, ss, rs, w).start()

    def finish(ins, outs, ss, rs):
        for w in range(n):
            desc(ins, outs, ss, rs, w).wait()

    out_shape = [_sds((N_SHARD, g.shape[1] // 2, g.shape[2]), g.dtype) for g in gws]
    return _Comm([_halves(g) for g in gws], out_shape, n, start, finish)


def _rs_chips(pbs):
    n = len(pbs)

    def desc(ins, outs, ss, rs, j, w):
        x, y, c = _coords()
        cx, cy = _other_chips(x, y)[j]
        return _remote(ins[w].at[2 * cx + cy], outs[w].at[j], ss.at[j * n + w], rs.at[j * n + w], (cx, cy, c))

    def start(ins, outs, ss, rs):
        for j in range(3):
            for w in range(n):
                desc(ins, outs, ss, rs, j, w).start()

    def finish(ins, outs, ss, rs):
        for j in range(3):
            for w in range(n):
                desc(ins, outs, ss, rs, j, w).wait()

    return _Comm(pbs, [_sds((3,) + p.shape[1:], p.dtype) for p in pbs], 3 * n, start, finish)


def _rs_join(halves):
    n = len(halves)

    def desc(bufs, ss, rs, w, landed):
        x, y, c = _coords()
        return _remote(bufs[w].at[c], bufs[w].at[1 - c if landed else c], ss.at[w], rs.at[w], (x, y, 1 - c))

    def start(ins, outs, ss, rs):
        for w in range(n):
            desc(outs, ss, rs, w, False).start()

    def finish(ins, outs, ss, rs):
        for w in range(n):
            desc(outs, ss, rs, w, True).wait_recv()
            desc(outs, ss, rs, w, False).wait_send()

    return _Comm(halves, [_sds(h.shape, h.dtype) for h in halves], n, start, finish, aliased=True)


def _small_exchange(buf):
    def desc(ins, outs, ss, rs, k, landed):
        x, y, c = _coords()
        px = 1 - x if (k >> 2) & 1 else x
        py = 1 - y if (k >> 1) & 1 else y
        pc = 1 - c if k & 1 else c
        slot = 4 * px + 2 * py + pc if landed else 4 * x + 2 * y + c
        return _remote(ins[0], outs[0].at[slot], ss.at[k - 1], rs.at[k - 1], (px, py, pc))

    def start(ins, outs, ss, rs):
        for k in range(1, 8):
            desc(ins, outs, ss, rs, k, False).start()

    def finish(ins, outs, ss, rs):
        for k in range(1, 8):
            desc(ins, outs, ss, rs, k, True).wait_recv()
            desc(ins, outs, ss, rs, k, False).wait_send()

    return _Comm([buf], [_sds((8,) + buf.shape, buf.dtype)], 7, start, finish)


HBM = pl.BlockSpec(memory_space=pltpu.HBM)
SEM = pl.BlockSpec(memory_space=pltpu.SEMAPHORE)
DATAFLOW = pltpu.SideEffectType.DATAFLOW_SIDE_EFFECTING


def _split_starts(name, comms, after):
    srcs = [[pltpu.with_memory_space_constraint(s, pltpu.HBM) for s in c.args] for c in comms]
    lands = [[pltpu.with_memory_space_constraint(b, pltpu.HBM)
              for b in c.landing or [lax.empty(o.shape, o.dtype) for o in c.out_shape]] for c in comms]
    bufs = [b for k in range(len(comms)) for b in srcs[k] + lands[k]]
    nb, nc = len(bufs), len(comms)

    def body(*refs):
        sems = refs[nb + 1:nb + 1 + 2 * nc]
        off = 0
        for k, c in enumerate(comms):
            ns, nl = len(srcs[k]), len(lands[k])
            c.start(refs[off:off + ns], refs[off + ns:off + ns + nl], sems[2 * k], sems[2 * k + 1])
            off += ns + nl
        refs[-1][...] = jnp.zeros_like(refs[-1])

    res = pl.pallas_call(
        body, name=name,
        out_shape=(*[pltpu.SemaphoreType.DMA((c.n_sems,)) for c in comms for _ in range(2)],
                   *[pltpu.HBM(b.shape, b.dtype) for b in bufs], _sds((8, 128), F32)),
        in_specs=[HBM] * nb + [ANY],
        out_specs=(*[SEM] * (2 * nc), *[HBM] * nb, pl.BlockSpec(memory_space=pltpu.VMEM)),
        input_output_aliases={i: 2 * nc + i for i in range(nb)},
        compiler_params=pltpu.CompilerParams(has_side_effects=DATAFLOW),
    )(*bufs, after)
    states, off = [], 2 * nc
    for k in range(nc):
        ns, nl = len(srcs[k]), len(lands[k])
        states.append((res[2 * k], res[2 * k + 1], list(res[off:off + ns]), list(res[off + ns:off + ns + nl])))
        off += ns + nl
    return states, res[-1]


def _split_wait(name, comm, send_sems, recv_sems, srcs, lands, after):
    ns, nb = len(srcs), len(lands)
    after = list(after) if isinstance(after, (list, tuple)) else [after]

    def body(*refs):
        comm.finish(refs[:ns], refs[ns:ns + nb], refs[ns + nb], refs[ns + nb + 1])

    res = pl.pallas_call(
        body, name=name,
        out_shape=tuple(pltpu.HBM(b.shape, b.dtype) for b in srcs + lands),
        in_specs=[HBM] * (ns + nb) + [SEM, SEM] + [ANY] * len(after), out_specs=tuple([HBM] * (ns + nb)),
        input_output_aliases={i: i for i in range(ns + nb)},
        compiler_params=pltpu.CompilerParams(has_side_effects=DATAFLOW),
    )(*srcs, *lands, send_sems, recv_sems, *after)
    return list(res[ns:])


LOSS_ROW = "loss_cols"
SMALL_EARLY = ("post_mix_norm", "pre_ffn_norm", "post_ffn_norm", "sgu_ln_gain", "sgu_ln_bias", "attn_out_norm",
               "sgu_out_norm", "sgu_b_spatial", LOSS_ROW)
SMALL_LATE = ("pre_mix_norm",)
SMALL_WSP = "sgu_w_spatial"


def _pack(d, names, rows):
    flat = [d[n].reshape(-1) for n in names]
    used = sum(f.shape[0] for f in flat)
    flat.append(jnp.zeros((rows * 1024 - used,), F32))
    return jnp.concatenate(flat).reshape(rows, 1024)


def _unpack(buf, names, shapes):
    flat = buf.reshape(-1)
    out, off = {}, 0
    for n in names:
        size = int(np.prod(shapes[n]))
        out[n] = flat[off:off + size].reshape(shapes[n])
        off += size
    return out


def _chip_sums(gbs, recvs, shard_core):
    n = len(gbs)
    _, rows, cols = gbs[0].shape
    hw = rows // 2

    def body(sc_ref, *refs):
        for k in range(n):
            refs[2 * n + k][...] = (refs[k][...].astype(F32) + refs[n + k][...].astype(F32)).astype(BF16)

    def other(s, sc):
        return jnp.where(s >= sc[0], s + 1, s)

    mine = pl.BlockSpec((1, hw, cols), lambda s, sc: (other(s, sc), sc[1], 0))
    plain = pl.BlockSpec((1, hw, cols), lambda s, sc: (other(s, sc), 0, 0))
    return pl.pallas_call(
        body, name="rs_chip_sums",
        grid_spec=pltpu.PrefetchScalarGridSpec(num_scalar_prefetch=1, grid=(N_SHARD - 1,),
                                               in_specs=[mine] * n + [plain] * n, out_specs=[plain] * n),
        out_shape=[_sds((N_SHARD, hw, cols), BF16)] * n,
        compiler_params=_seq_params(),
    )(shard_core, *gbs, *recvs)


def _final_sums(gws, recv_sibs, recv_chips, shard_core):
    n = len(gws)
    halves = [(g.shape[1] // 2, g.shape[2]) for g in gws]

    def body(sc_ref, *refs):
        for k in range(n):
            acc = refs[k][0] + refs[n + k][0]
            for j in range(3):
                acc = acc + refs[2 * n + k][j].astype(F32)
            refs[3 * n + k][0] = acc

    def specs(lead, index):
        return [pl.BlockSpec((lead, hw, cols), index) for hw, cols in halves]

    return pl.pallas_call(
        body, name="rs_final_sums",
        grid_spec=pltpu.PrefetchScalarGridSpec(
            num_scalar_prefetch=1, grid=(1,),
            in_specs=specs(1, lambda i, sc: (sc[0], sc[1], 0)) + specs(1, lambda i, sc: (sc[0], 0, 0))
            + specs(3, lambda i, sc: (0, 0, 0)),
            out_specs=specs(1, lambda i, sc: (sc[1], 0, 0))),
        out_shape=[_sds((2, hw, cols), F32) for hw, cols in halves],
        compiler_params=_seq_params(),
    )(shard_core, *gws, *recv_sibs, *recv_chips)


ADAM_BLOCKS = 4


def _adamw_multi(ws, gs, ms, vs, name, after=()):
    n = len(ws)

    def body(*refs):
        outs = refs[4 * n + len(after):]
        for k in range(n):
            g = refs[n + k][...]
            d, m, v = _adam_math(refs[k][...], g, refs[2 * n + k][...], refs[3 * n + k][...])
            outs[4 * k][...], outs[4 * k + 1][...], outs[4 * k + 2][...], outs[4 * k + 3][...] = g, d, m, v

    specs = [pl.BlockSpec((w.shape[0] // ADAM_BLOCKS, w.shape[1]), lambda i: (i, 0)) for w in ws]
    res = pl.pallas_call(
        body, name=name, grid=(ADAM_BLOCKS,), in_specs=specs * 4 + [ANY] * len(after),
        out_specs=[s for s in specs for _ in range(4)],
        out_shape=[_sds(w.shape, F32) for w in ws for _ in range(4)],
        compiler_params=_seq_params(),
    )(*ws, *gs, *ms, *vs, *after)
    return [tuple(res[4 * k:4 * k + 4]) for k in range(n)]


def _wgrad_ff(a_list, b, name):
    n = len(a_list)

    def body(*refs):
        bv = refs[n][...]
        for k in range(n):
            g = _dot_tn(refs[k][0], bv)
            refs[n + 1 + k][...] = g
            refs[2 * n + 1 + k][...] = g.astype(BF16)

    res = _pcall(
        body, name=name, grid=(FF // FF_BLOCK,),
        in_specs=[pl.BlockSpec((1, SEQ, FF_BLOCK), lambda j: (j, 0, 0))] * n
        + [pl.BlockSpec((SEQ, D_MODEL), lambda j: (0, 0), pipeline_mode=pl.Buffered(1))],
        out_specs=[pl.BlockSpec((FF_BLOCK, D_MODEL), lambda j: (j, 0))] * (2 * n),
        out_shape=[_sds((FF, D_MODEL), F32)] * n + [_sds((FF, D_MODEL), BF16)] * n, args=(*a_list, b))
    return [m.reshape(N_SHARD, FF_S, D_MODEL) for m in res]


def _comm_only(name, comms):
    return _pcall(lambda: None, name=name, grid=(1,), in_specs=[], out_specs=[], out_shape=[], args=(),
                  comms=comms)[1]


def _adam_math(w, g, m, v):
    m = ADAM_B1 * m + (1.0 - ADAM_B1) * g
    v = ADAM_B2 * v + (1.0 - ADAM_B2) * (g * g)
    m_hat = m / (1.0 - ADAM_B1 ** ADAM_STEP)
    v_hat = v / (1.0 - ADAM_B2 ** ADAM_STEP)
    return -ADAM_LR * (m_hat / (jnp.sqrt(v_hat) + ADAM_EPS) + ADAM_WD * w), m, v


def _adamw_small(own, slots, w, m, v, me):
    rows, cols = own.shape

    def body(me_ref, own_ref, slots_ref, w_ref, m_ref, v_ref, g_ref, d_ref, nm_ref, nv_ref):
        own_v = own_ref[...].astype(F32)
        g = jnp.where(me_ref[0] == 0, own_v, slots_ref[0].astype(F32))
        for i in range(1, 8):
            g = g + jnp.where(me_ref[0] == i, own_v, slots_ref[i].astype(F32))
        g_ref[...] = g
        d_ref[...], nm_ref[...], nv_ref[...] = _adam_math(w_ref[...], g, m_ref[...], v_ref[...])

    flat = pl.BlockSpec((rows, cols), lambda i, me_ref: (0, 0))
    return pl.pallas_call(
        body, name="adamw_small",
        grid_spec=pltpu.PrefetchScalarGridSpec(
            num_scalar_prefetch=1, grid=(1,),
            in_specs=[flat, pl.BlockSpec((8, rows, cols), lambda i, me_ref: (0, 0, 0)), flat, flat, flat],
            out_specs=[flat] * 4),
        out_shape=[_sds((rows, cols), F32)] * 4,
        compiler_params=_seq_params(),
    )(me, own, slots, w, m, v)


def kernel(x, positions, pre_mix_norm, w_in, sgu_ln_gain, sgu_ln_bias, sgu_w_spatial, sgu_b_spatial, attn_out_norm, sgu_out_norm, w_out, post_mix_norm, pre_ffn_norm, w_gate, w_up, w_down, post_ffn_norm, loss_target, m_pre_mix_norm, m_w_in, m_sgu_ln_gain, m_sgu_ln_bias, m_sgu_w_spatial, m_sgu_b_spatial, m_attn_out_norm, m_sgu_out_norm, m_w_out, m_post_mix_norm, m_pre_ffn_norm, m_w_gate, m_w_up, m_w_down, m_post_ffn_norm, v_pre_mix_norm, v_w_in, v_sgu_ln_gain, v_sgu_ln_bias, v_sgu_w_spatial, v_sgu_b_spatial, v_attn_out_norm, v_sgu_out_norm, v_w_out, v_post_mix_norm, v_pre_ffn_norm, v_w_gate, v_w_up, v_w_down, v_post_ffn_norm):
    a = dict(locals())
    cx, cy, cc = _coords()
    s_me = 2 * cx + cy
    shard_core = jnp.stack([s_me, cc]).astype(jnp.int32)
    me = jnp.stack([4 * cx + 2 * cy + cc]).astype(jnp.int32)
    small = {n: (a[n][0] if a[n].ndim > 2 else a[n]) for n in SMALL}
    b_t = small["sgu_b_spatial"].T
    xs, pos, target = x[0], positions.reshape(SEQ, 1), loss_target[0]
    flipped = ("w_gate", "w_up")

    def big(name, n):
        return jnp.swapaxes(a[name], 1, 2)[0] if n in flipped else a[name][0]

    own = {"w_in": _halves(big("w_in", "w_in").astype(BF16))}

    def with_own(full, n):
        return lax.dynamic_update_slice(full, own[n][None], (s_me, 0, 0, 0))

    def whole(full, n):
        return full.reshape((N_SHARD,) + big(n, n).shape)

    ffn = ("w_gate", "w_up", "w_down")
    g_in = _gather_ici([own["w_in"]])
    (s_in,), token = _split_starts("gather_in_start", [g_in], small["pre_mix_norm"])
    rest = ("w_out",) + ffn
    for n in rest:
        own[n] = _halves((big(n, n) + token[0:1, 0:1]).astype(BF16))
    room = {n: with_own(lax.empty((N_SHARD,) + own[n].shape, BF16), n) for n in rest}
    g_out, g_ffn = _gather_ici([own["w_out"]], [room["w_out"]]), _gather_ici([own[n] for n in ffn],
                                                                             [room[n] for n in ffn])
    (s_out, s_ffn), token = _split_starts("gather_rest_start", [g_out, g_ffn], token)
    in_lands = _split_wait("gather_in_wait", g_in, *s_in, token)
    ((in_lands,),) = _comm_only("comm_pass_in", [_gather_pass(in_lands)])
    w_in_f = whole(with_own(in_lands, "w_in"), "w_in")
    h, u, vs, sgu, *qkv = _inproj_fwd(xs, pos, small["pre_mix_norm"], w_in_f, small["sgu_ln_gain"],
                                      small["sgu_ln_bias"], small["sgu_w_spatial"], b_t)
    views = [tuple(qkv[3 * i:3 * i + 3]) for i in range(len(DILATIONS))]
    out_lands = _split_wait("gather_out_wait", g_out, *s_out, sgu)
    o_list, l_list = [], []
    for dil, (qv, kv, vv) in zip(DILATIONS, views):
        if dil == 1:
            (o, l), ((out_lands,),) = _attn_fwd(qv, kv, vv, dil, comms=[_gather_pass(out_lands)])
        else:
            o, l = _attn_fwd(qv, kv, vv, dil)
        o_list.append(o)
        l_list.append(l)
    w_out_f = whole(out_lands, "w_out")
    ffn_lands = _split_wait("gather_ffn_wait", g_ffn, *s_ffn, l_list[-1])
    (attn, mixed, y, x1, *lses), (ffn_lands,) = _mix_out_fwd(
        o_list, l_list, sgu, xs, w_out_f, small["attn_out_norm"], small["sgu_out_norm"], small["post_mix_norm"],
        comms=[_gather_pass(ffn_lands)])
    w_gate_f, w_up_f, w_down_f = (f.reshape(FF, D_MODEL) for f in ffn_lands)
    h2, act, dg, dup, df, dx1, loss_cols, d_pre_ffn, d_post_ffn = _ffn_fwd_bwd(
        x1, target, w_gate_f, w_up_f, w_down_f, small["pre_ffn_norm"], small["post_ffn_norm"])

    full_tok = pl.BlockSpec((SEQ, D_MODEL), lambda s: (0, 0), pipeline_mode=pl.Buffered(1))
    gw, gb = {}, {}
    gw["w_gate"], gw["w_up"], gb["w_gate"], gb["w_up"] = _wgrad_ff([dg, dup], h2, "wgrad_gate_up")
    gw["w_down"], gb["w_down"] = _wgrad_ff([act], df, "wgrad_down")
    (dy, du, dvs_sgu, d_post_mix, d_attn_norm, d_sgu_norm, d_w_sp, d_b_sp, d_ln_gain, d_ln_bias,
     *dviews), (sib_ffn,) = _outproj_bwd(
        dx1, y, attn, sgu, w_out_f, small["post_mix_norm"], small["attn_out_norm"], small["sgu_out_norm"],
        u, vs, small["sgu_ln_gain"], small["sgu_ln_bias"], small["sgu_w_spatial"], b_t,
        comms=[_rs_sibling([gb[n] for n in ffn])])
    sib = dict(zip(ffn, sib_ffn))
    part = dict(zip(ffn, _chip_sums([gb[n] for n in ffn], [sib[n] for n in ffn], shard_core)))
    gw["w_out"], gb["w_out"] = _wgrad(mixed, dy, pl.BlockSpec((SEQ, OUT_S), lambda s: (0, s)), full_tok,
                                      (OUT_S, D_MODEL), "wgrad_out")
    x_ffn = _rs_chips([part[n] for n in ffn])
    packed_early = _pack({
        "sgu_ln_gain": d_ln_gain, "sgu_ln_bias": d_ln_bias, "sgu_b_spatial": d_b_sp,
        "attn_out_norm": d_attn_norm, "sgu_out_norm": d_sgu_norm, "post_mix_norm": d_post_mix,
        "pre_ffn_norm": d_pre_ffn, "post_ffn_norm": d_post_ffn, LOSS_ROW: loss_cols}, SMALL_EARLY, 8)
    wsp_view = (SGU_GROUPS * CHUNK, CHUNK)
    packed_wsp = d_w_sp.reshape(wsp_view).astype(BF16)
    x_small, x_wsp = _small_exchange(packed_early), _small_exchange(packed_wsp)
    (s_ffn,), token = _split_starts("rs_ffn_start", [x_ffn], small["pre_mix_norm"])

    dqs, dks, dvs = [], [], []
    for i, (dil, (qv, kv, vv)) in enumerate(zip(DILATIONS, views)):
        if dil == 1:
            (dq, dk, dv), ((sib["w_out"],),) = _attn_bwd(qv, kv, vv, dviews[i], dviews[3 + i], lses[i], dil,
                                                        comms=[_rs_sibling([gb["w_out"]])], after=[token])
            (part["w_out"],) = _chip_sums([gb["w_out"]], [sib["w_out"]], shard_core)
            x_out = _rs_chips([part["w_out"]])
            (s_out, s_small, s_wsp), token = _split_starts("rs_out_small_start", [x_out, x_small, x_wsp], token)
        else:
            dq, dk, dv = _attn_bwd(qv, kv, vv, dviews[i], dviews[3 + i], lses[i], dil, after=[token])
        dqs.append(dq)
        dks.append(dk)
        dvs.append(dv)
    half, far, joined = {}, {}, {}
    far.update(zip(ffn, _split_wait("rs_ffn_wait", x_ffn, *s_ffn, dvs[-1])))
    half.update(zip(ffn, _final_sums([gw[n] for n in ffn], [sib[n] for n in ffn], [far[n] for n in ffn],
                                     shard_core)))
    dproj, grad_x, d_pre_mix = _inproj_bwd(dqs, dks, dvs, du, dvs_sgu, pos, xs, dx1, w_in_f, small["pre_mix_norm"])
    packed_late = _pack({"pre_mix_norm": d_pre_mix}, SMALL_LATE, 8)
    (gw["w_in"], gb["w_in"]), (got, (slots_late,)) = _wgrad(
        h, dproj, full_tok, pl.BlockSpec((SEQ, IN_S), lambda s: (0, s)), (D_MODEL, IN_S), "wgrad_in",
        comms=[_rs_join([half[n] for n in ffn]), _small_exchange(packed_late)])
    joined.update(zip(ffn, got))
    ((sib["w_in"],),) = _comm_only("comm_rs_sibling_in", [_rs_sibling([gb["w_in"]])])
    (part["w_in"],) = _chip_sums([gb["w_in"]], [sib["w_in"]], shard_core)
    x_in = _rs_chips([part["w_in"]])
    (s_in,), token = _split_starts("rs_in_start", [x_in], small["pre_mix_norm"])

    grads, deltas, new_m, new_v = {}, {}, {}, {}

    def record(n, outs):
        grads[n], deltas[n], new_m[n], new_v[n] = (
            jnp.swapaxes(o[None], 1, 2) if n in flipped else o[None] for o in outs)

    def update(names, name, after):
        for n, outs in zip(names, _adamw_multi(
                [big(n, n) for n in names], [joined[n].reshape(big(n, n).shape) for n in names],
                [big("m_" + n, n) for n in names], [big("v_" + n, n) for n in names], name, after)):
            record(n, outs)

    update(ffn, "adamw_ffn", [token])
    (far["w_out"],) = _split_wait("rs_out_wait", x_out, *s_out, new_v["w_down"])
    (slots_early,) = _split_wait("small_early_wait", x_small, *s_small, far["w_out"])
    (slots_wsp,) = _split_wait("small_wsp_wait", x_wsp, *s_wsp, slots_early)
    (far["w_in"],) = _split_wait("rs_in_wait", x_in, *s_in, slots_wsp)
    last = ("w_in", "w_out")
    half.update(zip(last, _final_sums([gw[n] for n in last], [sib[n] for n in last], [far[n] for n in last],
                                      shard_core)))
    (got,) = _comm_only("comm_rs_join_in_out", [_rs_join([half[n] for n in last])])
    joined.update(zip(last, got))
    update(last, "adamw_in_out", [])
    a[LOSS_ROW] = a["m_" + LOSS_ROW] = a["v_" + LOSS_ROW] = jnp.zeros((1, D_MODEL), F32)
    outs = _adamw_small(packed_wsp, slots_wsp, *[a[p + SMALL_WSP].reshape(wsp_view) for p in ("", "m_", "v_")], me)
    for dst, buf in zip((grads, deltas, new_m, new_v), outs):
        dst[SMALL_WSP] = buf.reshape(a[SMALL_WSP].shape)
    for names, rows, packed, slots in ((SMALL_EARLY, 8, packed_early, slots_early),
                                       (SMALL_LATE, 8, packed_late, slots_late)):
        outs = _adamw_small(packed, slots, _pack(a, names, rows), _pack({n: a["m_" + n] for n in names}, names, rows),
                            _pack({n: a["v_" + n] for n in names}, names, rows), me)
        for dst, buf in zip((grads, deltas, new_m, new_v), outs):
            dst.update(_unpack(buf, names, {n: a[n].shape for n in names}))
    loss = jnp.sum(grads[LOSS_ROW]) * np.float32(0.5 / D_MODEL)
    return (loss, grad_x[None], *[grads[n] for n in WEIGHTS], *[deltas[n] for n in WEIGHTS],
            *[new_m[n] for n in WEIGHTS], *[new_v[n] for n in WEIGHTS])
```

```python
import numpy as np
import jax
import jax.numpy as jnp
from jax import lax
from jax.experimental import pallas as pl
from jax.experimental.pallas import tpu as pltpu

F32 = jnp.float32
BF16 = jnp.bfloat16

SEQ = 2048
D_MODEL = 1024
HEAD_DIM = 64
ATTN_W = 512
SGU_W = 512
SGU_GROUPS = 8
CHUNK = 128
DILATIONS = (1, 4, 16)
N_SHARD = 4
IN_S = 640
OUT_S = 256
FF_S = 704
PROJ_W = N_SHARD * IN_S
FF = N_SHARD * FF_S
FF_CHUNKS = ((0, 1024), (1024, 2048), (2048, FF))
RMS_EPS = 1e-6
LN_EPS = 1e-5
ROPE_THETA = 500000.0
ATTN_SCALE = 1.0 / np.sqrt(HEAD_DIM)
NEG = -1e30
TM = 512
TM_FFN = 256
VMEM_LIMIT = 56 * 1024 * 1024

ADAM_LR = 0.001
ADAM_B1 = 0.9
ADAM_B2 = 0.999
ADAM_EPS = 1e-08
ADAM_WD = 0.01
ADAM_STEP = 10

MESH = pl.DeviceIdType.MESH
ANY = pl.BlockSpec(memory_space=pl.ANY)


def _dot(a, b):
    return jnp.dot(a, b, preferred_element_type=F32)


def _dot_nt(a, b):
    return lax.dot_general(a, b, (((1,), (1,)), ((), ())), preferred_element_type=F32)


def _dot_tn(a, b):
    return lax.dot_general(a, b, (((0,), (0,)), ((), ())), preferred_element_type=F32)


def _dot_exact(a, b):
    return jnp.dot(a, b, preferred_element_type=F32, precision=lax.Precision.HIGHEST)


def _dot_select(a, sel):
    hi = a.astype(BF16)
    lo = (a - hi.astype(F32)).astype(BF16)
    sel = sel.astype(BF16)
    return _dot(hi, sel) + _dot(lo, sel)


def _rms_stats(x):
    r = lax.rsqrt(jnp.mean(x * x, axis=-1, keepdims=True) + RMS_EPS)
    return x * r, r


def _rms_bwd(xh, r, gain, dy):
    dxh = dy * gain
    dx = r * (dxh - xh * jnp.mean(dxh * xh, axis=-1, keepdims=True))
    return dx, jnp.sum(dy * xh, axis=0, keepdims=True)


_ERF_ALPHA = (-2.72614225801306e-10, 2.77068142495902e-08, -2.10102402082508e-06, -5.69250639462346e-05,
              -7.34990630326855e-04, -2.95459980854025e-03, -1.60960333262415e-02)
_ERF_BETA = (-1.45660718464996e-05, -2.13374055278905e-04, -1.68282697438203e-03, -7.37332916720468e-03,
             -1.42647390514189e-02)


def _erf(x):
    x = jnp.clip(x, -4.0, 4.0)
    x2 = x * x
    p = jnp.full_like(x, _ERF_ALPHA[0])
    for a in _ERF_ALPHA[1:]:
        p = p * x2 + a
    q = jnp.full_like(x, _ERF_BETA[0])
    for b in _ERF_BETA[1:]:
        q = q * x2 + b
    return x * p / q


def _normal_cdf(x):
    return 0.5 * (1.0 + _erf(x * np.float32(1.0 / np.sqrt(2.0))))


def _gelu_grad(x, cdf):
    pdf = jnp.exp(-0.5 * x * x) * np.float32(1.0 / np.sqrt(2.0 * np.pi))
    return cdf + x * pdf


def _sigmoid(x):
    return 1.0 / (1.0 + jnp.exp(-x))


_INV_FREQ = tuple(float(np.float32(ROPE_THETA ** (-2.0 * j / 16.0))) for j in range(8))


def _rot_tables(pos):
    lane = lax.broadcasted_iota(jnp.int32, (1, 128), 1)
    d = lane & 63
    j = d & 7
    inv = jnp.zeros((1, 128), F32)
    for jj in range(8):
        inv = jnp.where(j == jj, _INV_FREQ[jj], inv)
    ang = pos.astype(F32) * inv
    c = jnp.cos(ang)
    s = jnp.sin(ang)
    cos_t = jnp.where(d < 16, c, 1.0)
    sin_a = jnp.where(d < 8, -s, 0.0)
    sin_b = jnp.where((d >= 8) & (d < 16), s, 0.0)
    return tuple(jnp.tile(t, (1, 4)) for t in (cos_t, sin_a, sin_b))


def _rope(x, tabs):
    cos_t, sin_a, sin_b = tabs
    return x * cos_t + pltpu.roll(x, 504, 1) * sin_a + pltpu.roll(x, 8, 1) * sin_b


def _rope_bwd(dy, tabs):
    cos_t, sin_a, sin_b = tabs
    return dy * cos_t + pltpu.roll(dy * sin_a, 8, 1) + pltpu.roll(dy * sin_b, 504, 1)


def _left_half():
    return lax.broadcasted_iota(jnp.int32, (CHUNK, CHUNK), 1) < HEAD_DIM


def _group_ones():
    lane = lax.broadcasted_iota(jnp.int32, (SGU_GROUPS, SGU_W), 1)
    row = lax.broadcasted_iota(jnp.int32, (SGU_GROUPS, SGU_W), 0)
    return ((lane >> 6) == row).astype(F32)


def _masked_spatial(w_ref):
    row = lax.broadcasted_iota(jnp.int32, (CHUNK, CHUNK), 0)
    col = lax.broadcasted_iota(jnp.int32, (CHUNK, CHUNK), 1)
    return [jnp.where(col <= row, w_ref[g], 0.0).astype(BF16) for g in range(SGU_GROUPS)]


def _sgu_core(u, vs, lg, lb, wm, bias_full):
    tm = u.shape[0]
    cdf_u, cdf_vs = _normal_cdf(u), _normal_cdf(vs)
    gu = u * cdf_u
    gv = vs * cdf_vs
    mu = jnp.mean(gv, axis=-1, keepdims=True)
    xc = gv - mu
    rstd = lax.rsqrt(jnp.mean(xc * xc, axis=-1, keepdims=True) + LN_EPS)
    xh = xc * rstd
    vnb = (xh * lg + lb).astype(BF16)
    left = _left_half()
    rows = []
    for c in range(tm // CHUNK):
        pieces = []
        for p in range(4):
            vp = vnb[c * CHUNK:(c + 1) * CHUNK, p * 128:(p + 1) * 128]
            pieces.append(jnp.where(left, _dot(wm[2 * p], vp), _dot(wm[2 * p + 1], vp)))
        rows.append(jnp.concatenate(pieces, axis=1) + bias_full)
    mixed = jnp.concatenate(rows, axis=0)
    return gu, xh, rstd, vnb, mixed, cdf_u, cdf_vs


def _resident(shape):
    n = len(shape)
    return pl.BlockSpec(shape, lambda *_: (0,) * n, pipeline_mode=pl.Buffered(1))


def _rows(ncol, tm=TM):
    return pl.BlockSpec((tm, ncol), lambda i: (i, 0))


def _acc(ncol, nrow=1):
    return pl.BlockSpec((nrow, ncol), lambda i: (0, 0))


HEAD_W = 128


def _view_rows(dil, width=ATTN_W, tm=TM):
    return pl.BlockSpec((tm // dil, dil * width), lambda i: (i, 0))


def _view_shape(dil, dtype, width=ATTN_W):
    return _sds((SEQ // dil, dil * width), dtype)


def _slab_scratch():
    return pltpu.VMEM((4, TM, 128), F32)


def _store_view(val, out_ref, slabs, dil):
    width = val.shape[1]
    for j in range(width // 128):
        slabs[j] = val[:, j * 128:(j + 1) * 128]
    for r in range(dil):
        for j in range(width // 128):
            c0 = r * width + j * 128
            out_ref[:, c0:c0 + 128] = slabs.at[j][pl.ds(r, TM // dil, stride=dil), :].astype(out_ref.dtype)


def _load_view(in_ref, slabs, dil, width=ATTN_W):
    for r in range(dil):
        for j in range(width // 128):
            c0 = r * width + j * 128
            slabs.at[j][pl.ds(r, TM // dil, stride=dil), :] = in_ref[:, c0:c0 + 128].astype(F32)
    return jnp.concatenate([slabs[j] for j in range(width // 128)], axis=1)


def _head_spread():
    m = lax.broadcasted_iota(jnp.int32, (HEAD_W, ATTN_W), 0)
    lane = lax.broadcasted_iota(jnp.int32, (HEAD_W, ATTN_W), 1)
    return (m == 16 * (lane >> 6)).astype(F32)


def _head_sum():
    lane = lax.broadcasted_iota(jnp.int32, (ATTN_W, HEAD_W), 0)
    m = lax.broadcasted_iota(jnp.int32, (ATTN_W, HEAD_W), 1)
    return ((lane >> 6) == (m >> 4)).astype(F32)


def _seq_params():
    return pltpu.CompilerParams(dimension_semantics=("arbitrary",), vmem_limit_bytes=VMEM_LIMIT)


def _sds(shape, dtype):
    return jax.ShapeDtypeStruct(shape, dtype)


class _Comm:
    def __init__(self, args, out_shape, n_sems, start, finish, aliased=False):
        self.args, self.out_shape, self.n_sems = list(args), list(out_shape), n_sems
        self.start, self.finish, self.aliased = start, finish, aliased


def _pcall(body, *, name, grid, in_specs, out_specs, out_shape, args, scratch_shapes=(), comms=(), after=()):
    single = not isinstance(out_shape, (list, tuple))
    out_specs = [out_specs] if single else list(out_specs)
    out_shape = [out_shape] if single else list(out_shape)
    n_in, n_out, n_scr = len(in_specs), len(out_shape), len(scratch_shapes)
    c_args = [a for c in comms for a in c.args]
    c_outs = [o for c in comms for o in c.out_shape]
    aliases, ai, ao = {}, n_in, n_out
    for c in comms:
        if c.aliased:
            aliases.update({ai + k: ao + k for k in range(len(c.args))})
        ai += len(c.args)
        ao += len(c.out_shape)
    sems = [pltpu.SemaphoreType.DMA((c.n_sems,)) for c in comms for _ in range(2)]
    steps = grid[0]

    def wrapped(*refs):
        o0 = n_in + len(c_args) + len(after)
        s0 = o0 + n_out + len(c_outs)
        m_in, m_out, m_sem = refs[n_in:n_in + len(c_args)], refs[o0 + n_out:s0], refs[s0 + n_scr:]

        def each(phase):
            ii = oi = 0
            for k, c in enumerate(comms):
                getattr(c, phase)(m_in[ii:ii + len(c.args)], m_out[oi:oi + len(c.out_shape)],
                                  m_sem[2 * k], m_sem[2 * k + 1])
                ii += len(c.args)
                oi += len(c.out_shape)

        if comms:
            @pl.when(pl.program_id(0) == 0)
            def _():
                each("start")

        body(*refs[:n_in], *refs[o0:o0 + n_out], *refs[s0:s0 + n_scr])

        if comms:
            @pl.when(pl.program_id(0) == steps - 1)
            def _():
                each("finish")

    res = pl.pallas_call(
        wrapped, name=name, grid=grid,
        in_specs=list(in_specs) + [ANY] * (len(c_args) + len(after)), out_specs=out_specs + [ANY] * len(c_outs),
        out_shape=out_shape + c_outs, scratch_shapes=list(scratch_shapes) + sems,
        input_output_aliases=aliases, compiler_params=_seq_params(),
    )(*args, *c_args, *after)
    mine = res[0] if single else list(res[:n_out])
    if not comms:
        return mine
    theirs, oi = [], n_out
    for c in comms:
        theirs.append(list(res[oi:oi + len(c.out_shape)]))
        oi += len(c.out_shape)
    return mine, theirs


def _in_pieces(g):
    lo, hi = 512 * g, 512 * (g + 1)
    return [(s, max(lo, IN_S * s) - IN_S * s, min(hi, IN_S * (s + 1)) - IN_S * s)
            for s in range(N_SHARD) if max(lo, IN_S * s) < min(hi, IN_S * (s + 1))]


def _inproj_fwd(x, pos, g_pre, w_in, lg, lb, w_sp, b_t, comms=()):
    def body(x_ref, pos_ref, g_ref, w_ref, lg_ref, lb_ref, wsp_ref, bt_ref, h_ref, u_ref, vs_ref, sgu_ref, *rest):
        qkv_refs, slabs = rest[:9], rest[9:]
        xh, _ = _rms_stats(x_ref[...])
        h = (xh * g_ref[...]).astype(BF16)
        h_ref[...] = h
        tabs = _rot_tables(pos_ref[...])

        def group(g):
            return jnp.concatenate([_dot(h, w_ref[s, :, a:b]) for s, a, b in _in_pieces(g)], axis=1)

        for t in range(3):
            val = group(t)
            if t < 2:
                val = _rope(val, tabs)
            if t == 0:
                val = val * np.float32(ATTN_SCALE)
            qkv_refs[t][...] = val.astype(BF16)
            for i, dil in enumerate(DILATIONS[1:]):
                _store_view(val, qkv_refs[3 * (i + 1) + t], slabs[t], dil)
        u = group(3)
        vs = group(4)
        u_ref[...] = u
        vs_ref[...] = vs
        bias_full = _dot_exact(bt_ref[...], _group_ones())
        gu, _, _, _, mixed, _, _ = _sgu_core(u, vs, lg_ref[...], lb_ref[...], _masked_spatial(wsp_ref), bias_full)
        sgu_ref[...] = gu * mixed

    return _pcall(
        body, name="inproj_sgu_fwd", grid=(SEQ // TM,),
        in_specs=[_rows(D_MODEL), _rows(1), _resident((1, D_MODEL)), _resident((N_SHARD, D_MODEL, IN_S)),
                  _resident((1, SGU_W)), _resident((1, SGU_W)), _resident((SGU_GROUPS, CHUNK, CHUNK)),
                  _resident((CHUNK, SGU_GROUPS))],
        out_specs=[_rows(D_MODEL), _rows(512), _rows(512), _rows(512)]
        + [_view_rows(dil) for dil in DILATIONS for _ in range(3)],
        out_shape=[_sds((SEQ, D_MODEL), BF16), _sds((SEQ, 512), F32), _sds((SEQ, 512), F32), _sds((SEQ, 512), F32)]
        + [_view_shape(dil, BF16) for dil in DILATIONS for _ in range(3)],
        scratch_shapes=[_slab_scratch() for _ in range(3)],
        args=(x, pos, g_pre, w_in, lg, lb, w_sp, b_t), comms=comms)


def _block_masks():
    row = lax.broadcasted_iota(jnp.int32, (CHUNK, CHUNK), 0)
    col = lax.broadcasted_iota(jnp.int32, (CHUNK, CHUNK), 1)
    return col <= row, col >= row


def _attn_fwd(qv, kv, vv, dil, comms=()):
    seg = SEQ // dil
    nblk = seg // CHUNK
    rps = 4 if nblk == 1 else 1

    def body(q_ref, k_ref, v_ref, o_ref, l_ref):
        left = _left_half()
        m_cur, m_prev = _block_masks()
        zero = jnp.zeros((CHUNK, CHUNK), BF16)
        ones = (jnp.where(left, 1.0, 0.0).astype(BF16), jnp.where(left, 0.0, 1.0).astype(BF16))

        sides = tuple(enumerate((left, ~left)))

        def rows(b):
            if isinstance(b, int):
                return b * CHUNK, max(b - 1, 0) * CHUNK
            return pl.multiple_of(b * CHUNK, CHUNK), pl.multiple_of(jnp.maximum(b - 1, 0) * CHUNK, CHUNK)

        def first(rr, b):
            r0, rp = rows(b)
            prev_ok = m_prev & (b > 0)
            tiles, scores = [], []
            for hp in range(4):
                ls = slice(rr * ATTN_W + hp * 128, rr * ATTN_W + (hp + 1) * 128)
                qp = q_ref[pl.ds(r0, CHUNK), ls]
                kc = k_ref[pl.ds(r0, CHUNK), ls]
                kp = k_ref[pl.ds(rp, CHUNK), ls] if nblk > 1 else None
                tiles.append((ls, v_ref[pl.ds(r0, CHUNK), ls], v_ref[pl.ds(rp, CHUNK), ls] if nblk > 1 else None))
                for _, hm in sides:
                    qh = jnp.where(hm, qp, zero)
                    sc = jnp.where(m_cur, _dot_nt(qh, kc), NEG)
                    sp = jnp.where(prev_ok, _dot_nt(qh, kp), NEG) if nblk > 1 else None
                    scores.append((sc, sp))
            return tiles, scores

        def second(scores):
            probs = []
            for sc, sp in scores:
                if nblk > 1:
                    m = jnp.max(jnp.maximum(sc, sp), axis=-1, keepdims=True)
                    pc = jnp.exp(sc - m)
                    pp = jnp.exp(sp - m)
                    probs.append((m, pc.astype(BF16), pp.astype(BF16), (pc + pp).astype(BF16)))
                else:
                    m = jnp.max(sc, axis=-1, keepdims=True)
                    pc = jnp.exp(sc - m).astype(BF16)
                    probs.append((m, pc, None, pc))
            return probs

        def third(rr, b, tiles, probs):
            r0, _ = rows(b)
            for hp, (ls, vc, vp) in enumerate(tiles):
                acc = jnp.zeros((CHUNK, CHUNK), F32)
                den = jnp.zeros((CHUNK, CHUNK), F32)
                for side, hm in sides:
                    _, pc, pp, psum = probs[2 * hp + side]
                    acc = acc + _dot(pc, jnp.where(hm, vc, zero))
                    if nblk > 1:
                        acc = acc + _dot(pp, jnp.where(hm, vp, zero))
                    den = den + _dot(psum, ones[side])
                o_ref[pl.ds(r0, CHUNK), ls] = (acc / den).astype(o_ref.dtype)
                lse = jnp.where(left, probs[2 * hp][0], probs[2 * hp + 1][0]) + jnp.log(den)
                l_ref[pl.ds(r0, CHUNK), rr * HEAD_W + 32 * hp:rr * HEAD_W + 32 * hp + 32] = lse[:, 48:80]

        def run(units):
            data = [first(rr, b) for rr, b in units]
            probs = [second(scores) for _, scores in data]
            for (rr, b), (tiles, _), pr in zip(units, data, probs):
                third(rr, b, tiles, pr)

        if nblk == 1:
            run([(rr, 0) for rr in range(rps)])
        else:
            def one(b, carry):
                run([(0, b)])
                return carry

            lax.fori_loop(0, nblk, one, 0)

    spec = pl.BlockSpec((seg, rps * ATTN_W), lambda r: (0, r))
    return _pcall(
        body, name=f"attn_fwd_d{dil}", grid=(dil // rps,),
        in_specs=[spec, spec, spec], out_specs=[spec, pl.BlockSpec((seg, rps * HEAD_W), lambda r: (0, r))],
        out_shape=[_sds((seg, dil * ATTN_W), BF16), _sds((seg, dil * HEAD_W), F32)],
        args=(qv, kv, vv), comms=comms)


def _lane_left(nrows):
    return lax.broadcasted_iota(jnp.int32, (nrows, CHUNK), 1) < HEAD_DIM


def _attn_rows(b):
    if isinstance(b, int):
        return b * CHUNK, max(b - 1, 0) * CHUNK
    return pl.multiple_of(b * CHUNK, CHUNK), pl.multiple_of(jnp.maximum(b - 1, 0) * CHUNK, CHUNK)


def _mix_out_fwd(o_list, l_list, sgu, x, w_out, g_attn, g_sgu, g_post, comms=()):
    def body(o1, o2, o3, l1, l2, l3, sgu_ref, x_ref, w_ref, ga_ref, gs_ref, gp_ref,
             attn_ref, mixed_ref, y_ref, x1_ref, lse1_ref, lse2_ref, lse3_ref, slabs_a, slabs_b):
        os = [o1[...], _load_view(o2, slabs_a, DILATIONS[1]), _load_view(o3, slabs_b, DILATIONS[2])]
        ls = [l1[...], _load_view(l2, slabs_a, DILATIONS[1], HEAD_W), _load_view(l3, slabs_b, DILATIONS[2], HEAD_W)]
        m = jnp.maximum(jnp.maximum(ls[0], ls[1]), ls[2])
        es = [jnp.exp(l - m) for l in ls]
        den = es[0] + es[1] + es[2]
        spread = _head_spread()
        attn = sum(_dot_select(e / den, spread) * o for e, o in zip(es, os))
        attn_ref[...] = attn
        lse = m + jnp.log(den)
        lse1_ref[...] = lse
        _store_view(lse, lse2_ref, slabs_a, DILATIONS[1])
        _store_view(lse, lse3_ref, slabs_b, DILATIONS[2])
        ah, _ = _rms_stats(attn)
        sh, _ = _rms_stats(sgu_ref[...])
        mixed = jnp.concatenate([ah * ga_ref[...], sh * gs_ref[...]], axis=1).astype(BF16)
        mixed_ref[...] = mixed
        y = _dot(mixed[:, 0:OUT_S], w_ref[0])
        for s in range(1, N_SHARD):
            y = y + _dot(mixed[:, s * OUT_S:(s + 1) * OUT_S], w_ref[s])
        y_ref[...] = y
        yh, _ = _rms_stats(y)
        x1_ref[...] = x_ref[...] + yh * gp_ref[...]

    return _pcall(
        body, name="mix_out_fwd", grid=(SEQ // TM,),
        in_specs=[_view_rows(dil) for dil in DILATIONS] + [_view_rows(dil, HEAD_W) for dil in DILATIONS]
        + [_rows(512), _rows(D_MODEL), _resident((N_SHARD, OUT_S, D_MODEL)),
           _resident((1, 512)), _resident((1, 512)), _resident((1, D_MODEL))],
        out_specs=[_rows(512), _rows(D_MODEL), _rows(D_MODEL), _rows(D_MODEL)]
        + [_view_rows(dil, HEAD_W) for dil in DILATIONS],
        out_shape=[_sds((SEQ, 512), F32), _sds((SEQ, D_MODEL), BF16), _sds((SEQ, D_MODEL), F32),
                   _sds((SEQ, D_MODEL), F32)] + [_view_shape(dil, F32, HEAD_W) for dil in DILATIONS],
        scratch_shapes=[_slab_scratch(), _slab_scratch()],
        args=(*o_list, *l_list, sgu, x, w_out, g_attn, g_sgu, g_post), comms=comms)


def _ffn_fwd_bwd(x1, target, w_gate, w_up, w_down, g_pre, g_post, comms=()):
    def body(x1_ref, t_ref, wg_ref, wu_ref, wd_ref, gpf_ref, gpo_ref,
             h2_ref, a_ref, dg_ref, dup_ref, df_ref, dx1_ref, loss_ref, dgpf_ref, dgpo_ref, g_scr, up_scr):
        @pl.when(pl.program_id(0) == 0)
        def _():
            loss_ref[...] = jnp.zeros_like(loss_ref)
            dgpf_ref[...] = jnp.zeros_like(dgpf_ref)
            dgpo_ref[...] = jnp.zeros_like(dgpo_ref)

        x1 = x1_ref[...]
        gpf = gpf_ref[...]
        gpo = gpo_ref[...]
        xh, r = _rms_stats(x1)
        h2 = (xh * gpf).astype(BF16)
        h2_ref[...] = h2
        f = jnp.zeros((TM_FFN, D_MODEL), F32)
        for c0, c1 in FF_CHUNKS:
            g = _dot_nt(h2, wg_ref[c0:c1, :])
            up = _dot_nt(h2, wu_ref[c0:c1, :])
            g_scr[:, c0:c1] = g
            up_scr[:, c0:c1] = up
            a = (g * _sigmoid(g) * up).astype(BF16)
            a_ref[:, c0:c1] = a
            f = f + _dot(a, wd_ref[c0:c1, :])
        fh, rf = _rms_stats(f)
        diff = x1 + fh * gpo - t_ref[...]
        loss_ref[...] += jnp.sum(diff * diff, axis=0, keepdims=True)
        dout = diff * np.float32(1.0 / D_MODEL)
        df, dgpo = _rms_bwd(fh, rf, gpo, dout)
        dgpo_ref[...] += dgpo
        dfb = df.astype(BF16)
        df_ref[...] = dfb
        dh2 = jnp.zeros((TM_FFN, D_MODEL), F32)
        for c0, c1 in FF_CHUNKS:
            da = _dot_nt(dfb, wd_ref[c0:c1, :])
            g = g_scr[:, c0:c1]
            up = up_scr[:, c0:c1]
            sg = _sigmoid(g)
            dup = (da * (g * sg)).astype(BF16)
            dg = (da * up * (sg * (1.0 + g * (1.0 - sg)))).astype(BF16)
            dg_ref[:, c0:c1] = dg
            dup_ref[:, c0:c1] = dup
            dh2 = dh2 + _dot(dg, wg_ref[c0:c1, :]) + _dot(dup, wu_ref[c0:c1, :])
        dx, dgpf = _rms_bwd(xh, r, gpf, dh2)
        dgpf_ref[...] += dgpf
        dx1_ref[...] = dout + dx

    return _pcall(
        body, name="ffn_fwd_bwd", grid=(SEQ // TM_FFN,),
        in_specs=[_rows(D_MODEL, TM_FFN), _rows(D_MODEL, TM_FFN), _resident((FF, D_MODEL)),
                  _resident((FF, D_MODEL)), _resident((FF, D_MODEL)),
                  _resident((1, D_MODEL)), _resident((1, D_MODEL))],
        out_specs=[_rows(D_MODEL, TM_FFN), _rows(FF, TM_FFN), _rows(FF, TM_FFN), _rows(FF, TM_FFN),
                   _rows(D_MODEL, TM_FFN), _rows(D_MODEL, TM_FFN), _acc(D_MODEL), _acc(D_MODEL), _acc(D_MODEL)],
        out_shape=[_sds((SEQ, D_MODEL), BF16), _sds((SEQ, FF), BF16), _sds((SEQ, FF), BF16),
                   _sds((SEQ, FF), BF16), _sds((SEQ, D_MODEL), BF16), _sds((SEQ, D_MODEL), F32),
                   _sds((1, D_MODEL), F32), _sds((1, D_MODEL), F32), _sds((1, D_MODEL), F32)],
        scratch_shapes=[pltpu.VMEM((TM_FFN, FF), F32), pltpu.VMEM((TM_FFN, FF), F32)],
        args=(x1, target, w_gate, w_up, w_down, g_pre, g_post), comms=comms)


def _wgrad(a, b, a_spec, b_spec, out_block, name, comms=()):
    def body(a_ref, b_ref, o_ref, ob_ref):
        av = a_ref[0] if len(a_ref.shape) == 3 else a_ref[...]
        bv = b_ref[0] if len(b_ref.shape) == 3 else b_ref[...]
        g = _dot_tn(av, bv)
        o_ref[0] = g
        ob_ref[0] = g.astype(BF16)

    return _pcall(
        body, name=name, grid=(N_SHARD,),
        in_specs=[a_spec, b_spec],
        out_specs=[pl.BlockSpec((1,) + out_block, lambda s: (s, 0, 0))] * 2,
        out_shape=[_sds((N_SHARD,) + out_block, F32), _sds((N_SHARD,) + out_block, BF16)],
        args=(a, b), comms=comms)


def _outproj_bwd(dx1, y, attn, sgu, w_out, g_post, g_attn, g_sgu, u, vs, lg, lb, w_sp, b_t, comms=(), after=()):
    sgu_bwd = _sgu_bwd_body()

    def body(dx1_ref, y_ref, attn_ref, sgu_ref, w_ref, gp_ref, ga_ref, gs_ref, u_ref, vs_ref, lg_ref, lb_ref,
             wsp_ref, bt_ref, dy_ref, du_ref, dvs_ref, dgp_ref, dga_ref, dgs_ref, dw_ref, db_ref, dlg_ref, dlb_ref,
             *rest):
        dattn_refs, delta_refs, (slabs_a, slabs_b, dsgu_ref, dbias_scr) = rest[0:3], rest[3:6], rest[6:]

        @pl.when(pl.program_id(0) == 0)
        def _():
            dgp_ref[...] = jnp.zeros_like(dgp_ref)
            dga_ref[...] = jnp.zeros_like(dga_ref)
            dgs_ref[...] = jnp.zeros_like(dgs_ref)

        yh, ry = _rms_stats(y_ref[...])
        dy, dgp = _rms_bwd(yh, ry, gp_ref[...], dx1_ref[...])
        dgp_ref[...] += dgp
        dyb = dy.astype(BF16)
        dy_ref[...] = dyb
        dmixed = jnp.concatenate([_dot_nt(dyb, w_ref[s]) for s in range(N_SHARD)], axis=1)
        attn = attn_ref[...]
        ah, ra = _rms_stats(attn)
        dattn, dga = _rms_bwd(ah, ra, ga_ref[...], dmixed[:, 0:512])
        dga_ref[...] += dga
        sh, rs = _rms_stats(sgu_ref[...])
        dsgu, dgs = _rms_bwd(sh, rs, gs_ref[...], dmixed[:, 512:1024])
        dgs_ref[...] += dgs
        dsgu_ref[...] = dsgu
        delta = _dot_select(dattn * attn, _head_sum())
        dattn_refs[0][...] = dattn.astype(BF16)
        delta_refs[0][...] = delta
        for i, dil in enumerate(DILATIONS[1:]):
            _store_view(dattn, dattn_refs[i + 1], slabs_a, dil)
            _store_view(delta, delta_refs[i + 1], slabs_b, dil)
        sgu_bwd(u_ref, vs_ref, dsgu_ref, lg_ref, lb_ref, wsp_ref, bt_ref,
                du_ref, dvs_ref, dw_ref, db_ref, dlg_ref, dlb_ref, dbias_scr)

    return _pcall(
        body, name="outproj_sgu_bwd", grid=(SEQ // TM,),
        in_specs=[_rows(D_MODEL), _rows(D_MODEL), _rows(512), _rows(512), _resident((N_SHARD, OUT_S, D_MODEL)),
                  _resident((1, D_MODEL)), _resident((1, 512)), _resident((1, 512)),
                  _rows(SGU_W), _rows(SGU_W), _resident((1, SGU_W)), _resident((1, SGU_W)),
                  _resident((SGU_GROUPS, CHUNK, CHUNK)), _resident((CHUNK, SGU_GROUPS))],
        out_specs=[_rows(D_MODEL), _rows(SGU_W), _rows(SGU_W), _acc(D_MODEL), _acc(512), _acc(512),
                   pl.BlockSpec((SGU_GROUPS, CHUNK, CHUNK), lambda i: (0, 0, 0)), _acc(CHUNK, SGU_GROUPS),
                   _acc(SGU_W), _acc(SGU_W)]
        + [_view_rows(dil) for dil in DILATIONS] + [_view_rows(dil, HEAD_W) for dil in DILATIONS],
        out_shape=[_sds((SEQ, D_MODEL), BF16), _sds((SEQ, SGU_W), BF16), _sds((SEQ, SGU_W), BF16),
                   _sds((1, D_MODEL), F32), _sds((1, 512), F32), _sds((1, 512), F32),
                   _sds((SGU_GROUPS, CHUNK, CHUNK), F32), _sds((SGU_GROUPS, CHUNK), F32),
                   _sds((1, SGU_W), F32), _sds((1, SGU_W), F32)]
        + [_view_shape(dil, BF16) for dil in DILATIONS] + [_view_shape(dil, F32, HEAD_W) for dil in DILATIONS],
        scratch_shapes=[_slab_scratch(), _slab_scratch(), pltpu.VMEM((TM, SGU_W), F32),
                        pltpu.VMEM((CHUNK, SGU_W), F32)],
        args=(dx1, y, attn, sgu, w_out, g_post, g_attn, g_sgu, u, vs, lg, lb, w_sp, b_t), comms=comms, after=after)


def _sgu_bwd_body():
    nsteps = SEQ // TM

    def body(u_ref, vs_ref, ds_ref, lg_ref, lb_ref, w_ref, bt_ref,
             du_ref, dvs_ref, dw_ref, db_ref, dlg_ref, dlb_ref, dbias_scr):
        i = pl.program_id(0)

        @pl.when(i == 0)
        def _():
            dw_ref[...] = jnp.zeros_like(dw_ref)
            dlg_ref[...] = jnp.zeros_like(dlg_ref)
            dlb_ref[...] = jnp.zeros_like(dlb_ref)
            dbias_scr[...] = jnp.zeros_like(dbias_scr)

        wm = _masked_spatial(w_ref)
        ones_g = _group_ones()
        bias_full = _dot_exact(bt_ref[...], ones_g)
        u = u_ref[...]
        vs = vs_ref[...]
        lg = lg_ref[...]
        gu, xh, rstd, vnb, mixed, cdf_u, cdf_vs = _sgu_core(u, vs, lg, lb_ref[...], wm, bias_full)
        dsgu = ds_ref[...]
        du_ref[...] = (dsgu * mixed * _gelu_grad(u, cdf_u)).astype(BF16)
        dmixed = dsgu * gu
        left = _left_half()
        dvn_rows = []
        for c in range(TM // CHUNK):
            rs = slice(c * CHUNK, (c + 1) * CHUNK)
            dm_c = dmixed[rs, :]
            dbias_scr[...] += dm_c
            pieces = []
            for p in range(4):
                ls = slice(p * 128, (p + 1) * 128)
                dmp = dm_c[:, ls]
                vp = vnb[rs, ls]
                dmb = dmp.astype(BF16)
                zero = jnp.zeros_like(dmb)
                dw_ref[2 * p] += _dot_nt(jnp.where(left, dmb, zero), vp)
                dw_ref[2 * p + 1] += _dot_nt(jnp.where(left, zero, dmb), vp)
                pieces.append(jnp.where(left, _dot_tn(wm[2 * p], dmb), _dot_tn(wm[2 * p + 1], dmb)))
            dvn_rows.append(jnp.concatenate(pieces, axis=1))
        dvn = jnp.concatenate(dvn_rows, axis=0)
        dlg_ref[...] += jnp.sum(dvn * xh, axis=0, keepdims=True)
        dlb_ref[...] += jnp.sum(dvn, axis=0, keepdims=True)
        dxh = dvn * lg
        dgv = rstd * (dxh - jnp.mean(dxh, axis=-1, keepdims=True) - xh * jnp.mean(dxh * xh, axis=-1, keepdims=True))
        dvs_ref[...] = (dgv * _gelu_grad(vs, cdf_vs)).astype(BF16)

        @pl.when(i == nsteps - 1)
        def _():
            row = lax.broadcasted_iota(jnp.int32, (CHUNK, CHUNK), 0)
            col = lax.broadcasted_iota(jnp.int32, (CHUNK, CHUNK), 1)
            for g in range(SGU_GROUPS):
                dw_ref[g] = jnp.where(col <= row, dw_ref[g], 0.0)
            db_ref[...] = lax.dot_general(ones_g, dbias_scr[...], (((1,), (1,)), ((), ())),
                                          preferred_element_type=F32, precision=lax.Precision.HIGHEST)

    return body


def _attn_bwd(qv, kv, vv, dov, deltav, lsev, dil, comms=(), after=()):
    seg = SEQ // dil
    nblk = seg // CHUNK
    rps = 4 if nblk == 1 else 1

    def body(q_ref, k_ref, v_ref, do_ref, dl_ref, lse_ref, dq_ref, dk_ref, dv_ref, dk_wait, dv_wait):
        left = _left_half()
        m_cur, m_prev = _block_masks()
        sides = tuple(enumerate((left, ~left)))
        zero = jnp.zeros((CHUNK, CHUNK), BF16)

        def first(rr, b, both):
            r0, rp = _attn_rows(b)
            keys = pl.ds(rp, 2 * CHUNK) if both else pl.ds(r0, CHUNK)
            tiles, heads = [], []
            for hp in range(4):
                ls = slice(rr * ATTN_W + hp * 128, rr * ATTN_W + (hp + 1) * 128)
                qp = q_ref[pl.ds(r0, CHUNK), ls]
                dop = do_ref[pl.ds(r0, CHUNK), ls]
                k2 = k_ref[keys, ls]
                v2 = v_ref[keys, ls]
                tiles.append((ls, k2))
                for _, hm in sides:
                    qh = jnp.where(hm, qp, zero)
                    doh = jnp.where(hm, dop, zero)
                    heads.append((qh, doh, _dot_nt(k2, qh), _dot_nt(v2, doh)))
            return tiles, heads

        def second(rr, b, heads, both):
            r0, _ = _attn_rows(b)
            ok = jnp.concatenate([m_cur, m_prev], axis=0) if both else m_prev
            lanes = slice(rr * HEAD_W, (rr + 1) * HEAD_W)
            lse_t = lse_ref[pl.ds(r0, CHUNK), lanes].T
            dl_t = dl_ref[pl.ds(r0, CHUNK), lanes].T
            out = []
            for i, (_, _, s_t, dp_t) in enumerate(heads):
                p = jnp.exp(jnp.where(ok, s_t - lse_t[16 * i:16 * i + 1, :], NEG))
                out.append((p.astype(BF16), (p * (dp_t - dl_t[16 * i:16 * i + 1, :])).astype(BF16)))
            return out

        def third(rr, b, tiles, heads, probs, both):
            r0, rp = _attn_rows(b)
            nk = 2 * CHUNK if both else CHUNK
            left_k = _lane_left(nk)
            zero_k = jnp.zeros((nk, CHUNK), BF16)
            for hp, (ls, k2) in enumerate(tiles):
                dq = jnp.zeros((CHUNK, CHUNK), F32)
                dk2 = jnp.zeros((nk, CHUNK), F32)
                dv2 = jnp.zeros((nk, CHUNK), F32)
                for side in range(2):
                    qh, doh, _, _ = heads[2 * hp + side]
                    p, ds = probs[2 * hp + side]
                    dq = dq + _dot_tn(ds, jnp.where(left_k if side == 0 else ~left_k, k2, zero_k))
                    dk2 = dk2 + _dot(ds, qh)
                    dv2 = dv2 + _dot(p, doh)
                dq_ref[pl.ds(r0, CHUNK), ls] = dq.astype(dq_ref.dtype)
                if nblk == 1:
                    dk_ref[pl.ds(r0, CHUNK), ls] = dk2.astype(dk_ref.dtype)
                    dv_ref[pl.ds(r0, CHUNK), ls] = dv2.astype(dv_ref.dtype)
                elif both:
                    dk_ref[pl.ds(rp, CHUNK), ls] = (dk_wait[:, ls] + dk2[0:CHUNK]).astype(dk_ref.dtype)
                    dv_ref[pl.ds(rp, CHUNK), ls] = (dv_wait[:, ls] + dv2[0:CHUNK]).astype(dv_ref.dtype)
                    dk_wait[:, ls] = dk2[CHUNK:]
                    dv_wait[:, ls] = dv2[CHUNK:]
                else:
                    dk_wait[:, ls] = dk2
                    dv_wait[:, ls] = dv2

        def run(units, both):
            data = [first(rr, b, both) for rr, b in units]
            probs = [second(rr, b, heads, both) for (rr, b), (_, heads) in zip(units, data)]
            for (rr, b), (tiles, heads), pr in zip(units, data, probs):
                third(rr, b, tiles, heads, pr, both)

        run([(rr, 0) for rr in range(rps)], False)
        if nblk > 1:
            def one(b, carry):
                run([(0, b)], True)
                return carry

            lax.fori_loop(1, nblk, one, 0)
            last = (nblk - 1) * CHUNK
            dk_ref[last:last + CHUNK, :] = dk_wait[...].astype(dk_ref.dtype)
            dv_ref[last:last + CHUNK, :] = dv_wait[...].astype(dv_ref.dtype)

    spec = pl.BlockSpec((seg, rps * ATTN_W), lambda r: (0, r))
    return _pcall(
        body, name=f"attn_bwd_d{dil}", grid=(dil // rps,),
        in_specs=[spec] * 4 + [pl.BlockSpec((seg, rps * HEAD_W), lambda r: (0, r))] * 2, out_specs=[spec] * 3,
        out_shape=[_sds((seg, dil * ATTN_W), BF16)] * 3,
        scratch_shapes=[pltpu.VMEM((CHUNK, ATTN_W), F32), pltpu.VMEM((CHUNK, ATTN_W), F32)],
        args=(qv, kv, vv, dov, deltav, lsev), comms=comms, after=after)


def _inproj_bwd(dqs, dks, dvs, du, dvs_sgu, pos, x, dx1, w_in, g_pre, comms=()):
    def body(dq1, dq2, dq3, dk1, dk2, dk3, dv1, dv2, dv3, du_ref, dvs_ref, pos_ref, x_ref, dx1_ref, w_ref, g_ref,
             dproj_ref, gx_ref, dg_ref, slabs_a, slabs_b):
        @pl.when(pl.program_id(0) == 0)
        def _():
            dg_ref[...] = jnp.zeros_like(dg_ref)

        def total(r1, r2, r3):
            return r1[...] + _load_view(r2, slabs_a, DILATIONS[1]) + _load_view(r3, slabs_b, DILATIONS[2])

        tabs = _rot_tables(pos_ref[...])
        groups = {3: du_ref[...], 4: dvs_ref[...]}
        dh = jnp.zeros((TM, D_MODEL), F32)
        for g in (3, 4, 0, 1, 2):
            if g == 0:
                groups[g] = _rope_bwd(total(dq1, dq2, dq3) * np.float32(ATTN_SCALE), tabs).astype(BF16)
            elif g == 1:
                groups[g] = _rope_bwd(total(dk1, dk2, dk3), tabs).astype(BF16)
            elif g == 2:
                groups[g] = total(dv1, dv2, dv3).astype(BF16)
            dproj_ref[:, 512 * g:512 * (g + 1)] = groups[g]
            off = 0
            for s, a, b in _in_pieces(g):
                dh = dh + _dot_nt(groups[g][:, off:off + b - a], w_ref[s, :, a:b])
                off += b - a
        g = g_ref[...]
        xh, r = _rms_stats(x_ref[...])
        dx, dg = _rms_bwd(xh, r, g, dh)
        dg_ref[...] += dg
        gx_ref[...] = dx1_ref[...] + dx

    return _pcall(
        body, name="inproj_bwd", grid=(SEQ // TM,),
        in_specs=[_view_rows(dil) for dil in DILATIONS] * 3
        + [_rows(512), _rows(512), _rows(1), _rows(D_MODEL), _rows(D_MODEL),
           _resident((N_SHARD, D_MODEL, IN_S)), _resident((1, D_MODEL))],
        out_specs=[_rows(PROJ_W), _rows(D_MODEL), _acc(D_MODEL)],
        out_shape=[_sds((SEQ, PROJ_W), BF16), _sds((SEQ, D_MODEL), F32), _sds((1, D_MODEL), F32)],
        scratch_shapes=[_slab_scratch(), _slab_scratch()],
        args=(*dqs, *dks, *dvs, du, dvs_sgu, pos, x, dx1, w_in, g_pre), comms=comms)


def _coords():
    return lax.axis_index("x"), lax.axis_index("y"), lax.axis_index("c")


def _other_chips(x, y):
    return [(1 - x, y), (x, 1 - y), (1 - x, 1 - y)]


WEIGHTS = ("pre_mix_norm", "w_in", "sgu_ln_gain", "sgu_ln_bias", "sgu_w_spatial", "sgu_b_spatial", "attn_out_norm",
           "sgu_out_norm", "w_out", "post_mix_norm", "pre_ffn_norm", "w_gate", "w_up", "w_down", "post_ffn_norm")
SMALL = ("pre_mix_norm", "post_mix_norm", "pre_ffn_norm", "post_ffn_norm", "sgu_ln_gain", "sgu_ln_bias",
         "attn_out_norm", "sgu_out_norm", "sgu_w_spatial", "sgu_b_spatial")


def _remote(src, dst, send_sem, recv_sem, to):
    return pltpu.make_async_remote_copy(src_ref=src, dst_ref=dst, send_sem=send_sem, recv_sem=recv_sem,
                                        device_id=to, device_id_type=MESH)


def _halves(a):
    *lead, rows, cols = a.shape
    return a.reshape(*lead, 2, rows // 2, cols)


def _gather_ici(shards):
    n = len(shards)

    def desc(ins, outs, ss, rs, j, w, landed):
        x, y, c = _coords()
        cx, cy = _other_chips(x, y)[j]
        shard = 2 * cx + cy if landed else 2 * x + y
        return _remote(ins[w].at[c], outs[w].at[shard, c], ss.at[j * n + w], rs.at[j * n + w], (cx, cy, c))

    def start(ins, outs, ss, rs):
        for j in range(3):
            for w in range(n):
                desc(ins, outs, ss, rs, j, w, False).start()

    def finish(ins, outs, ss, rs):
        for j in range(3):
            for w in range(n):
                desc(ins, outs, ss, rs, j, w, True).wait_recv()
                desc(ins, outs, ss, rs, j, w, False).wait_send()

    return _Comm(shards, [_sds((N_SHARD,) + s.shape, s.dtype) for s in shards], 3 * n, start, finish)


def _gather_pass(fulls):
    n = len(fulls)

    def desc(bufs, ss, rs, j, w, landed):
        x, y, c = _coords()
        cx, cy = _other_chips(x, y)[j]
        shard = 2 * cx + cy
        return _remote(bufs[w].at[shard, c], bufs[w].at[shard, 1 - c if landed else c],
                       ss.at[j * n + w], rs.at[j * n + w], (x, y, 1 - c))

    def start(ins, outs, ss, rs):
        for j in range(3):
            for w in range(n):
                desc(outs, ss, rs, j, w, False).start()

    def finish(ins, outs, ss, rs):
        for j in range(3):
            for w in range(n):
                desc(outs, ss, rs, j, w, True).wait_recv()
                desc(outs, ss, rs, j, w, False).wait_send()

    return _Comm(fulls, [_sds(f.shape, f.dtype) for f in fulls], 3 * n, start, finish, aliased=True)


def _rs_sibling(gws):
    n = len(gws)

    def desc(ins, outs, ss, rs, w):
        x, y, c = _coords()
        return _remote(ins[w].at[:, 1 - c], outs[w], ss.at[w], rs.at[w], (x, y, 1 - c))

    def start(ins, outs, ss, rs):
        for w in range(n):
            desc(ins, outs, ss, rs, w).start()

    def finish(ins, outs, ss, rs):
        for w in range(n):
            desc(ins, outs, ss, rs, w).wait()

    out_shape = [_sds((N_SHARD, g.shape[1] // 2, g.shape[2]), g.dtype) for g in gws]
    return _Comm([_halves(g) for g in gws], out_shape, n, start, finish)


def _rs_chips(pbs):
    n = len(pbs)

    def desc(ins, outs, ss, rs, j, w):
        x, y, c = _coords()
        cx, cy = _other_chips(x, y)[j]
        return _remote(ins[w].at[2 * cx + cy], outs[w].at[j], ss.at[j * n + w], rs.at[j * n + w], (cx, cy, c))

    def start(ins, outs, ss, rs):
        for j in range(3):
            for w in range(n):
                desc(ins, outs, ss, rs, j, w).start()

    def finish(ins, outs, ss, rs):
        for j in range(3):
            for w in range(n):
                desc(ins, outs, ss, rs, j, w).wait()

    return _Comm(pbs, [_sds((3,) + p.shape[1:], p.dtype) for p in pbs], 3 * n, start, finish)


def _rs_join(halves):
    n = len(halves)

    def desc(bufs, ss, rs, w, landed):
        x, y, c = _coords()
        return _remote(bufs[w].at[c], bufs[w].at[1 - c if landed else c], ss.at[w], rs.at[w], (x, y, 1 - c))

    def start(ins, outs, ss, rs):
        for w in range(n):
            desc(outs, ss, rs, w, False).start()

    def finish(ins, outs, ss, rs):
        for w in range(n):
            desc(outs, ss, rs, w, True).wait_recv()
            desc(outs, ss, rs, w, False).wait_send()

    return _Comm(halves, [_sds(h.shape, h.dtype) for h in halves], n, start, finish, aliased=True)


def _small_exchange(buf):
    def desc(ins, outs, ss, rs, k, landed):
        x, y, c = _coords()
        px = 1 - x if (k >> 2) & 1 else x
        py = 1 - y if (k >> 1) & 1 else y
        pc = 1 - c if k & 1 else c
        slot = 4 * px + 2 * py + pc if landed else 4 * x + 2 * y + c
        return _remote(ins[0], outs[0].at[slot], ss.at[k - 1], rs.at[k - 1], (px, py, pc))

    def start(ins, outs, ss, rs):
        for k in range(1, 8):
            desc(ins, outs, ss, rs, k, False).start()

    def finish(ins, outs, ss, rs):
        for k in range(1, 8):
            desc(ins, outs, ss, rs, k, True).wait_recv()
            desc(ins, outs, ss, rs, k, False).wait_send()

    return _Comm([buf], [_sds((8,) + buf.shape, buf.dtype)], 7, start, finish)


HBM = pl.BlockSpec(memory_space=pltpu.HBM)
SEM = pl.BlockSpec(memory_space=pltpu.SEMAPHORE)
DATAFLOW = pltpu.SideEffectType.DATAFLOW_SIDE_EFFECTING


def _split_starts(name, comms, after):
    srcs = [[pltpu.with_memory_space_constraint(s, pltpu.HBM) for s in c.args] for c in comms]
    lands = [[pltpu.with_memory_space_constraint(lax.empty(o.shape, o.dtype), pltpu.HBM) for o in c.out_shape]
             for c in comms]
    bufs = [b for k in range(len(comms)) for b in srcs[k] + lands[k]]
    nb, nc = len(bufs), len(comms)

    def body(*refs):
        sems = refs[nb + 1:nb + 1 + 2 * nc]
        off = 0
        for k, c in enumerate(comms):
            ns, nl = len(srcs[k]), len(lands[k])
            c.start(refs[off:off + ns], refs[off + ns:off + ns + nl], sems[2 * k], sems[2 * k + 1])
            off += ns + nl
        refs[-1][...] = jnp.zeros_like(refs[-1])

    res = pl.pallas_call(
        body, name=name,
        out_shape=(*[pltpu.SemaphoreType.DMA((c.n_sems,)) for c in comms for _ in range(2)],
                   *[pltpu.HBM(b.shape, b.dtype) for b in bufs], _sds((8, 128), F32)),
        in_specs=[HBM] * nb + [ANY],
        out_specs=(*[SEM] * (2 * nc), *[HBM] * nb, pl.BlockSpec(memory_space=pltpu.VMEM)),
        input_output_aliases={i: 2 * nc + i for i in range(nb)},
        compiler_params=pltpu.CompilerParams(has_side_effects=DATAFLOW),
    )(*bufs, after)
    states, off = [], 2 * nc
    for k in range(nc):
        ns, nl = len(srcs[k]), len(lands[k])
        states.append((res[2 * k], res[2 * k + 1], list(res[off:off + ns]), list(res[off + ns:off + ns + nl])))
        off += ns + nl
    return states, res[-1]


def _split_wait(name, comm, send_sems, recv_sems, srcs, lands, after):
    ns, nb = len(srcs), len(lands)
    after = list(after) if isinstance(after, (list, tuple)) else [after]

    def body(*refs):
        comm.finish(refs[:ns], refs[ns:ns + nb], refs[ns + nb], refs[ns + nb + 1])

    res = pl.pallas_call(
        body, name=name,
        out_shape=tuple(pltpu.HBM(b.shape, b.dtype) for b in srcs + lands),
        in_specs=[HBM] * (ns + nb) + [SEM, SEM] + [ANY] * len(after), out_specs=tuple([HBM] * (ns + nb)),
        input_output_aliases={i: i for i in range(ns + nb)},
        compiler_params=pltpu.CompilerParams(has_side_effects=DATAFLOW),
    )(*srcs, *lands, send_sems, recv_sems, *after)
    return list(res[ns:])


LOSS_ROW = "loss_cols"
SMALL_EARLY = ("post_mix_norm", "pre_ffn_norm", "post_ffn_norm", "sgu_ln_gain", "sgu_ln_bias", "attn_out_norm",
               "sgu_out_norm", "sgu_b_spatial", LOSS_ROW)
SMALL_LATE = ("pre_mix_norm",)
SMALL_WSP = "sgu_w_spatial"


def _pack(d, names, rows):
    flat = [d[n].reshape(-1) for n in names]
    used = sum(f.shape[0] for f in flat)
    flat.append(jnp.zeros((rows * 1024 - used,), F32))
    return jnp.concatenate(flat).reshape(rows, 1024)


def _unpack(buf, names, shapes):
    flat = buf.reshape(-1)
    out, off = {}, 0
    for n in names:
        size = int(np.prod(shapes[n]))
        out[n] = flat[off:off + size].reshape(shapes[n])
        off += size
    return out


SUM_SPLIT = 2


def _chip_sums(gbs, recvs, shard_core):
    n = len(gbs)
    _, rows, cols = gbs[0].shape
    hw = rows // 2
    hb = hw // SUM_SPLIT

    def body(sc_ref, *refs):
        for k in range(n):
            refs[2 * n + k][...] = (refs[k][...].astype(F32) + refs[n + k][...].astype(F32)).astype(BF16)

    def other(i, sc):
        s = i // SUM_SPLIT
        return jnp.where(s >= sc[0], s + 1, s)

    mine = pl.BlockSpec((1, hb, cols), lambda i, sc: (other(i, sc), SUM_SPLIT * sc[1] + i % SUM_SPLIT, 0))
    plain = pl.BlockSpec((1, hb, cols), lambda i, sc: (other(i, sc), i % SUM_SPLIT, 0))
    return pl.pallas_call(
        body, name="rs_chip_sums",
        grid_spec=pltpu.PrefetchScalarGridSpec(num_scalar_prefetch=1, grid=((N_SHARD - 1) * SUM_SPLIT,),
                                               in_specs=[mine] * n + [plain] * n, out_specs=[plain] * n),
        out_shape=[_sds((N_SHARD, hw, cols), BF16)] * n,
        compiler_params=_seq_params(),
    )(shard_core, *gbs, *recvs)


def _final_sums(gws, recv_sibs, recv_chips, shard_core):
    n = len(gws)
    halves = [(g.shape[1] // 2, g.shape[2]) for g in gws]

    def body(sc_ref, *refs):
        for k in range(n):
            acc = refs[k][0] + refs[n + k][0]
            for j in range(3):
                acc = acc + refs[2 * n + k][j].astype(F32)
            refs[3 * n + k][0] = acc

    def specs(lead, index):
        return [pl.BlockSpec((lead, hw // SUM_SPLIT, cols), index) for hw, cols in halves]

    return pl.pallas_call(
        body, name="rs_final_sums",
        grid_spec=pltpu.PrefetchScalarGridSpec(
            num_scalar_prefetch=1, grid=(SUM_SPLIT,),
            in_specs=specs(1, lambda i, sc: (sc[0], SUM_SPLIT * sc[1] + i, 0))
            + specs(1, lambda i, sc: (sc[0], i, 0)) + specs(3, lambda i, sc: (0, i, 0)),
            out_specs=specs(1, lambda i, sc: (sc[1], i, 0))),
        out_shape=[_sds((2, hw, cols), F32) for hw, cols in halves],
        compiler_params=_seq_params(),
    )(shard_core, *gws, *recv_sibs, *recv_chips)


ADAM_BLOCKS = 8


def _adamw_multi(ws, gs, ms, vs, name, after=()):
    n = len(ws)

    def body(*refs):
        outs = refs[4 * n + len(after):]
        for k in range(n):
            g = refs[n + k][...]
            d, m, v = _adam_math(refs[k][...], g, refs[2 * n + k][...], refs[3 * n + k][...])
            outs[4 * k][...], outs[4 * k + 1][...], outs[4 * k + 2][...], outs[4 * k + 3][...] = g, d, m, v

    specs = [pl.BlockSpec((w.shape[0] // ADAM_BLOCKS, w.shape[1]), lambda i: (i, 0)) for w in ws]
    res = pl.pallas_call(
        body, name=name, grid=(ADAM_BLOCKS,), in_specs=specs * 4 + [ANY] * len(after),
        out_specs=[s for s in specs for _ in range(4)],
        out_shape=[_sds(w.shape, F32) for w in ws for _ in range(4)],
        compiler_params=_seq_params(),
    )(*ws, *gs, *ms, *vs, *after)
    return [tuple(res[4 * k:4 * k + 4]) for k in range(n)]


def _wgrad_ff(a_list, b, name):
    n = len(a_list)
    cols = 256

    def body(*refs):
        bv = refs[n][...]
        for k in range(n):
            g = _dot_tn(refs[k][...], bv)
            refs[n + 1 + k][...] = g
            refs[2 * n + 1 + k][...] = g.astype(BF16)

    res = _pcall(
        body, name=name, grid=(FF // cols,),
        in_specs=[pl.BlockSpec((SEQ, cols), lambda j: (0, j))] * n
        + [pl.BlockSpec((SEQ, D_MODEL), lambda j: (0, 0), pipeline_mode=pl.Buffered(1))],
        out_specs=[pl.BlockSpec((cols, D_MODEL), lambda j: (j, 0))] * (2 * n),
        out_shape=[_sds((FF, D_MODEL), F32)] * n + [_sds((FF, D_MODEL), BF16)] * n, args=(*a_list, b))
    return [m.reshape(N_SHARD, FF_S, D_MODEL) for m in res]


def _comm_only(name, comms):
    return _pcall(lambda: None, name=name, grid=(1,), in_specs=[], out_specs=[], out_shape=[], args=(),
                  comms=comms)[1]


def _adam_math(w, g, m, v):
    m = ADAM_B1 * m + (1.0 - ADAM_B1) * g
    v = ADAM_B2 * v + (1.0 - ADAM_B2) * (g * g)
    m_hat = m / (1.0 - ADAM_B1 ** ADAM_STEP)
    v_hat = v / (1.0 - ADAM_B2 ** ADAM_STEP)
    return -ADAM_LR * (m_hat / (jnp.sqrt(v_hat) + ADAM_EPS) + ADAM_WD * w), m, v


def _adamw_small(own, slots, w, m, v, me):
    rows, cols = own.shape

    def body(me_ref, own_ref, slots_ref, w_ref, m_ref, v_ref, g_ref, d_ref, nm_ref, nv_ref):
        own_v = own_ref[...].astype(F32)
        g = jnp.where(me_ref[0] == 0, own_v, slots_ref[0].astype(F32))
        for i in range(1, 8):
            g = g + jnp.where(me_ref[0] == i, own_v, slots_ref[i].astype(F32))
        g_ref[...] = g
        d_ref[...], nm_ref[...], nv_ref[...] = _adam_math(w_ref[...], g, m_ref[...], v_ref[...])

    flat = pl.BlockSpec((rows, cols), lambda i, me_ref: (0, 0))
    return pl.pallas_call(
        body, name="adamw_small",
        grid_spec=pltpu.PrefetchScalarGridSpec(
            num_scalar_prefetch=1, grid=(1,),
            in_specs=[flat, pl.BlockSpec((8, rows, cols), lambda i, me_ref: (0, 0, 0)), flat, flat, flat],
            out_specs=[flat] * 4),
        out_shape=[_sds((rows, cols), F32)] * 4,
        compiler_params=_seq_params(),
    )(me, own, slots, w, m, v)


def kernel(x, positions, pre_mix_norm, w_in, sgu_ln_gain, sgu_ln_bias, sgu_w_spatial, sgu_b_spatial, attn_out_norm, sgu_out_norm, w_out, post_mix_norm, pre_ffn_norm, w_gate, w_up, w_down, post_ffn_norm, loss_target, m_pre_mix_norm, m_w_in, m_sgu_ln_gain, m_sgu_ln_bias, m_sgu_w_spatial, m_sgu_b_spatial, m_attn_out_norm, m_sgu_out_norm, m_w_out, m_post_mix_norm, m_pre_ffn_norm, m_w_gate, m_w_up, m_w_down, m_post_ffn_norm, v_pre_mix_norm, v_w_in, v_sgu_ln_gain, v_sgu_ln_bias, v_sgu_w_spatial, v_sgu_b_spatial, v_attn_out_norm, v_sgu_out_norm, v_w_out, v_post_mix_norm, v_pre_ffn_norm, v_w_gate, v_w_up, v_w_down, v_post_ffn_norm):
    a = dict(locals())
    cx, cy, cc = _coords()
    s_me = 2 * cx + cy
    shard_core = jnp.stack([s_me, cc]).astype(jnp.int32)
    me = jnp.stack([4 * cx + 2 * cy + cc]).astype(jnp.int32)
    small = {n: (a[n][0] if a[n].ndim > 2 else a[n]) for n in SMALL}
    b_t = small["sgu_b_spatial"].T
    xs, pos, target = x[0], positions.reshape(SEQ, 1), loss_target[0]
    flipped = ("w_gate", "w_up")

    def big(name, n):
        return jnp.swapaxes(a[name], 1, 2)[0] if n in flipped else a[name][0]

    own = {"w_in": _halves(big("w_in", "w_in").astype(BF16))}

    def with_own(full, n):
        full = lax.dynamic_update_slice(full, own[n][None], (s_me, 0, 0, 0))
        return full.reshape((N_SHARD,) + big(n, n).shape)

    ffn = ("w_gate", "w_up", "w_down")
    g_in = _gather_ici([own["w_in"]])
    (s_in,), token = _split_starts("gather_in_start", [g_in], small["pre_mix_norm"])
    for n in ("w_out",) + ffn:
        own[n] = _halves((big(n, n) + token[0:1, 0:1]).astype(BF16))
    g_out, g_ffn = _gather_ici([own["w_out"]]), _gather_ici([own[n] for n in ffn])
    (s_out, s_ffn), token = _split_starts("gather_rest_start", [g_out, g_ffn], token)
    in_lands = _split_wait("gather_in_wait", g_in, *s_in, token)
    ((in_lands,),) = _comm_only("comm_pass_in", [_gather_pass(in_lands)])
    w_in_f = with_own(in_lands, "w_in")
    h, u, vs, sgu, *qkv = _inproj_fwd(xs, pos, small["pre_mix_norm"], w_in_f, small["sgu_ln_gain"],
                                      small["sgu_ln_bias"], small["sgu_w_spatial"], b_t)
    views = [tuple(qkv[3 * i:3 * i + 3]) for i in range(len(DILATIONS))]
    out_lands = _split_wait("gather_out_wait", g_out, *s_out, sgu)
    o_list, l_list = [], []
    for dil, (qv, kv, vv) in zip(DILATIONS, views):
        if dil == 1:
            (o, l), ((out_lands,),) = _attn_fwd(qv, kv, vv, dil, comms=[_gather_pass(out_lands)])
        else:
            o, l = _attn_fwd(qv, kv, vv, dil)
        o_list.append(o)
        l_list.append(l)
    w_out_f = with_own(out_lands, "w_out")
    ffn_lands = _split_wait("gather_ffn_wait", g_ffn, *s_ffn, l_list[-1])
    (attn, mixed, y, x1, *lses), (ffn_lands,) = _mix_out_fwd(
        o_list, l_list, sgu, xs, w_out_f, small["attn_out_norm"], small["sgu_out_norm"], small["post_mix_norm"],
        comms=[_gather_pass(ffn_lands)])
    w_gate_f, w_up_f, w_down_f = (with_own(f, n).reshape(FF, D_MODEL) for f, n in zip(ffn_lands, ffn))
    h2, act, dg, dup, df, dx1, loss_cols, d_pre_ffn, d_post_ffn = _ffn_fwd_bwd(
        x1, target, w_gate_f, w_up_f, w_down_f, small["pre_ffn_norm"], small["post_ffn_norm"])

    full_tok = pl.BlockSpec((SEQ, D_MODEL), lambda s: (0, 0), pipeline_mode=pl.Buffered(1))
    gw, gb = {}, {}
    gw["w_gate"], gw["w_up"], gb["w_gate"], gb["w_up"] = _wgrad_ff([dg, dup], h2, "wgrad_gate_up")
    gw["w_down"], gb["w_down"] = _wgrad_ff([act], df, "wgrad_down")
    (dy, du, dvs_sgu, d_post_mix, d_attn_norm, d_sgu_norm, d_w_sp, d_b_sp, d_ln_gain, d_ln_bias,
     *dviews), (sib_ffn,) = _outproj_bwd(
        dx1, y, attn, sgu, w_out_f, small["post_mix_norm"], small["attn_out_norm"], small["sgu_out_norm"],
        u, vs, small["sgu_ln_gain"], small["sgu_ln_bias"], small["sgu_w_spatial"], b_t,
        comms=[_rs_sibling([gb[n] for n in ffn])])
    sib = dict(zip(ffn, sib_ffn))
    part = dict(zip(ffn, _chip_sums([gb[n] for n in ffn], [sib[n] for n in ffn], shard_core)))
    gw["w_out"], gb["w_out"] = _wgrad(mixed, dy, pl.BlockSpec((SEQ, OUT_S), lambda s: (0, s)), full_tok,
                                      (OUT_S, D_MODEL), "wgrad_out")
    x_ffn = _rs_chips([part[n] for n in ffn])
    packed_early = _pack({
        "sgu_ln_gain": d_ln_gain, "sgu_ln_bias": d_ln_bias, "sgu_b_spatial": d_b_sp,
        "attn_out_norm": d_attn_norm, "sgu_out_norm": d_sgu_norm, "post_mix_norm": d_post_mix,
        "pre_ffn_norm": d_pre_ffn, "post_ffn_norm": d_post_ffn, LOSS_ROW: loss_cols}, SMALL_EARLY, 8)
    wsp_view = (SGU_GROUPS * CHUNK, CHUNK)
    packed_wsp = d_w_sp.reshape(wsp_view).astype(BF16)
    x_small, x_wsp = _small_exchange(packed_early), _small_exchange(packed_wsp)
    (s_ffn,), token = _split_starts("rs_ffn_start", [x_ffn], small["pre_mix_norm"])

    dqs, dks, dvs = [], [], []
    for i, (dil, (qv, kv, vv)) in enumerate(zip(DILATIONS, views)):
        if dil == 1:
            (dq, dk, dv), ((sib["w_out"],),) = _attn_bwd(qv, kv, vv, dviews[i], dviews[3 + i], lses[i], dil,
                                                        comms=[_rs_sibling([gb["w_out"]])], after=[token])
            (part["w_out"],) = _chip_sums([gb["w_out"]], [sib["w_out"]], shard_core)
            x_out = _rs_chips([part["w_out"]])
            (s_out, s_small, s_wsp), token = _split_starts("rs_out_small_start", [x_out, x_small, x_wsp], token)
        else:
            dq, dk, dv = _attn_bwd(qv, kv, vv, dviews[i], dviews[3 + i], lses[i], dil, after=[token])
        dqs.append(dq)
        dks.append(dk)
        dvs.append(dv)
    half, far, joined = {}, {}, {}
    far.update(zip(ffn, _split_wait("rs_ffn_wait", x_ffn, *s_ffn, dvs[-1])))
    half.update(zip(ffn, _final_sums([gw[n] for n in ffn], [sib[n] for n in ffn], [far[n] for n in ffn],
                                     shard_core)))
    dproj, grad_x, d_pre_mix = _inproj_bwd(dqs, dks, dvs, du, dvs_sgu, pos, xs, dx1, w_in_f, small["pre_mix_norm"])
    packed_late = _pack({"pre_mix_norm": d_pre_mix}, SMALL_LATE, 8)
    (gw["w_in"], gb["w_in"]), (got, (slots_late,)) = _wgrad(
        h, dproj, full_tok, pl.BlockSpec((SEQ, IN_S), lambda s: (0, s)), (D_MODEL, IN_S), "wgrad_in",
        comms=[_rs_join([half[n] for n in ffn]), _small_exchange(packed_late)])
    joined.update(zip(ffn, got))
    ((sib["w_in"],),) = _comm_only("comm_rs_sibling_in", [_rs_sibling([gb["w_in"]])])
    (part["w_in"],) = _chip_sums([gb["w_in"]], [sib["w_in"]], shard_core)
    x_in = _rs_chips([part["w_in"]])
    (s_in,), token = _split_starts("rs_in_start", [x_in], small["pre_mix_norm"])

    grads, deltas, new_m, new_v = {}, {}, {}, {}

    def record(n, outs):
        grads[n], deltas[n], new_m[n], new_v[n] = (
            jnp.swapaxes(o[None], 1, 2) if n in flipped else o[None] for o in outs)

    def update(names, name, after):
        for n, outs in zip(names, _adamw_multi(
                [big(n, n) for n in names], [joined[n].reshape(big(n, n).shape) for n in names],
                [big("m_" + n, n) for n in names], [big("v_" + n, n) for n in names], name, after)):
            record(n, outs)

    update(ffn, "adamw_ffn", [token])
    (far["w_out"],) = _split_wait("rs_out_wait", x_out, *s_out, new_v["w_down"])
    (slots_early,) = _split_wait("small_early_wait", x_small, *s_small, far["w_out"])
    (slots_wsp,) = _split_wait("small_wsp_wait", x_wsp, *s_wsp, slots_early)
    (far["w_in"],) = _split_wait("rs_in_wait", x_in, *s_in, slots_wsp)
    last = ("w_in", "w_out")
    half.update(zip(last, _final_sums([gw[n] for n in last], [sib[n] for n in last], [far[n] for n in last],
                                      shard_core)))
    (got,) = _comm_only("comm_rs_join_in_out", [_rs_join([half[n] for n in last])])
    joined.update(zip(last, got))
    update(last, "adamw_in_out", [])
    a[LOSS_ROW] = a["m_" + LOSS_ROW] = a["v_" + LOSS_ROW] = jnp.zeros((1, D_MODEL), F32)
    outs = _adamw_small(packed_wsp, slots_wsp, *[a[p + SMALL_WSP].reshape(wsp_view) for p in ("", "m_", "v_")], me)
    for dst, buf in zip((grads, deltas, new_m, new_v), outs):
        dst[SMALL_WSP] = buf.reshape(a[SMALL_WSP].shape)
    for names, rows, packed, slots in ((SMALL_EARLY, 8, packed_early, slots_early),
                                       (SMALL_LATE, 8, packed_late, slots_late)):
        outs = _adamw_small(packed, slots, _pack(a, names, rows), _pack({n: a["m_" + n] for n in names}, names, rows),
                            _pack({n: a["v_" + n] for n in names}, names, rows), me)
        for dst, buf in zip((grads, deltas, new_m, new_v), outs):
            dst.update(_unpack(buf, names, {n: a[n].shape for n in names}))
    loss = jnp.sum(grads[LOSS_ROW]) * np.float32(0.5 / D_MODEL)
    return (loss, grad_x[None], *[grads[n] for n in WEIGHTS], *[deltas[n] for n in WEIGHTS],
            *[new_m[n] for n in WEIGHTS], *[new_v[n] for n in WEIGHTS])
```

```python
import numpy as np
import jax
import jax.numpy as jnp
from jax import lax
from jax.experimental import pallas as pl
from jax.experimental.pallas import tpu as pltpu

F32 = jnp.float32
BF16 = jnp.bfloat16

SEQ = 2048
D_MODEL = 1024
HEAD_DIM = 64
ATTN_W = 512
SGU_W = 512
SGU_GROUPS = 8
CHUNK = 128
DILATIONS = (1, 4, 16)
N_SHARD = 4
IN_S = 640
OUT_S = 256
FF_S = 704
PROJ_W = N_SHARD * IN_S
FF = N_SHARD * FF_S
FF_CHUNKS = ((0, 1024), (1024, 2048), (2048, FF))
RMS_EPS = 1e-6
LN_EPS = 1e-5
ROPE_THETA = 500000.0
ATTN_SCALE = 1.0 / np.sqrt(HEAD_DIM)
NEG = -1e30
TM = 512
TM_FFN = 256
VMEM_LIMIT = 56 * 1024 * 1024

ADAM_LR = 0.001
ADAM_B1 = 0.9
ADAM_B2 = 0.999
ADAM_EPS = 1e-08
ADAM_WD = 0.01
ADAM_STEP = 10

MESH = pl.DeviceIdType.MESH
ANY = pl.BlockSpec(memory_space=pl.ANY)


def _dot(a, b):
    return jnp.dot(a, b, preferred_element_type=F32)


def _dot_nt(a, b):
    return lax.dot_general(a, b, (((1,), (1,)), ((), ())), preferred_element_type=F32)


def _dot_tn(a, b):
    return lax.dot_general(a, b, (((0,), (0,)), ((), ())), preferred_element_type=F32)


def _dot_exact(a, b):
    return jnp.dot(a, b, preferred_element_type=F32, precision=lax.Precision.HIGHEST)


def _dot_select(a, sel):
    hi = a.astype(BF16)
    lo = (a - hi.astype(F32)).astype(BF16)
    sel = sel.astype(BF16)
    return _dot(hi, sel) + _dot(lo, sel)


def _rms_stats(x):
    r = lax.rsqrt(jnp.mean(x * x, axis=-1, keepdims=True) + RMS_EPS)
    return x * r, r


def _rms_bwd(xh, r, gain, dy):
    dxh = dy * gain
    dx = r * (dxh - xh * jnp.mean(dxh * xh, axis=-1, keepdims=True))
    return dx, jnp.sum(dy * xh, axis=0, keepdims=True)


_ERF_ALPHA = (-2.72614225801306e-10, 2.77068142495902e-08, -2.10102402082508e-06, -5.69250639462346e-05,
              -7.34990630326855e-04, -2.95459980854025e-03, -1.60960333262415e-02)
_ERF_BETA = (-1.45660718464996e-05, -2.13374055278905e-04, -1.68282697438203e-03, -7.37332916720468e-03,
             -1.42647390514189e-02)


def _erf(x):
    x = jnp.clip(x, -4.0, 4.0)
    x2 = x * x
    p = jnp.full_like(x, _ERF_ALPHA[0])
    for a in _ERF_ALPHA[1:]:
        p = p * x2 + a
    q = jnp.full_like(x, _ERF_BETA[0])
    for b in _ERF_BETA[1:]:
        q = q * x2 + b
    return x * p / q


def _normal_cdf(x):
    return 0.5 * (1.0 + _erf(x * np.float32(1.0 / np.sqrt(2.0))))


def _gelu_grad(x, cdf):
    pdf = jnp.exp(-0.5 * x * x) * np.float32(1.0 / np.sqrt(2.0 * np.pi))
    return cdf + x * pdf


def _sigmoid(x):
    return 1.0 / (1.0 + jnp.exp(-x))


_INV_FREQ = tuple(float(np.float32(ROPE_THETA ** (-2.0 * j / 16.0))) for j in range(8))


def _rot_tables(pos):
    lane = lax.broadcasted_iota(jnp.int32, (1, 128), 1)
    d = lane & 63
    j = d & 7
    inv = jnp.zeros((1, 128), F32)
    for jj in range(8):
        inv = jnp.where(j == jj, _INV_FREQ[jj], inv)
    ang = pos.astype(F32) * inv
    c = jnp.cos(ang)
    s = jnp.sin(ang)
    cos_t = jnp.where(d < 16, c, 1.0)
    sin_a = jnp.where(d < 8, -s, 0.0)
    sin_b = jnp.where((d >= 8) & (d < 16), s, 0.0)
    return tuple(jnp.tile(t, (1, 4)) for t in (cos_t, sin_a, sin_b))


def _rope(x, tabs):
    cos_t, sin_a, sin_b = tabs
    return x * cos_t + pltpu.roll(x, 504, 1) * sin_a + pltpu.roll(x, 8, 1) * sin_b


def _rope_bwd(dy, tabs):
    cos_t, sin_a, sin_b = tabs
    return dy * cos_t + pltpu.roll(dy * sin_a, 8, 1) + pltpu.roll(dy * sin_b, 504, 1)


def _left_half():
    return lax.broadcasted_iota(jnp.int32, (CHUNK, CHUNK), 1) < HEAD_DIM


def _group_ones():
    lane = lax.broadcasted_iota(jnp.int32, (SGU_GROUPS, SGU_W), 1)
    row = lax.broadcasted_iota(jnp.int32, (SGU_GROUPS, SGU_W), 0)
    return ((lane >> 6) == row).astype(F32)


def _masked_spatial(w_ref):
    row = lax.broadcasted_iota(jnp.int32, (CHUNK, CHUNK), 0)
    col = lax.broadcasted_iota(jnp.int32, (CHUNK, CHUNK), 1)
    return [jnp.where(col <= row, w_ref[g], 0.0).astype(BF16) for g in range(SGU_GROUPS)]


def _sgu_core(u, vs, lg, lb, wm, bias_full):
    tm = u.shape[0]
    cdf_u, cdf_vs = _normal_cdf(u), _normal_cdf(vs)
    gu = u * cdf_u
    gv = vs * cdf_vs
    mu = jnp.mean(gv, axis=-1, keepdims=True)
    xc = gv - mu
    rstd = lax.rsqrt(jnp.mean(xc * xc, axis=-1, keepdims=True) + LN_EPS)
    xh = xc * rstd
    vnb = (xh * lg + lb).astype(BF16)
    left = _left_half()
    rows = []
    for c in range(tm // CHUNK):
        pieces = []
        for p in range(4):
            vp = vnb[c * CHUNK:(c + 1) * CHUNK, p * 128:(p + 1) * 128]
            pieces.append(jnp.where(left, _dot(wm[2 * p], vp), _dot(wm[2 * p + 1], vp)))
        rows.append(jnp.concatenate(pieces, axis=1) + bias_full)
    mixed = jnp.concatenate(rows, axis=0)
    return gu, xh, rstd, vnb, mixed, cdf_u, cdf_vs


def _resident(shape):
    n = len(shape)
    return pl.BlockSpec(shape, lambda *_: (0,) * n, pipeline_mode=pl.Buffered(1))


def _rows(ncol, tm=TM):
    return pl.BlockSpec((tm, ncol), lambda i: (i, 0))


def _acc(ncol, nrow=1):
    return pl.BlockSpec((nrow, ncol), lambda i: (0, 0))


HEAD_W = 128


def _view_rows(dil, width=ATTN_W, tm=TM):
    return pl.BlockSpec((tm // dil, dil * width), lambda i: (i, 0))


def _view_shape(dil, dtype, width=ATTN_W):
    return _sds((SEQ // dil, dil * width), dtype)


def _slab_scratch():
    return pltpu.VMEM((4, TM, 128), F32)


def _store_view(val, out_ref, slabs, dil):
    width = val.shape[1]
    for j in range(width // 128):
        slabs[j] = val[:, j * 128:(j + 1) * 128]
    for r in range(dil):
        for j in range(width // 128):
            c0 = r * width + j * 128
            out_ref[:, c0:c0 + 128] = slabs.at[j][pl.ds(r, TM // dil, stride=dil), :].astype(out_ref.dtype)


def _load_view(in_ref, slabs, dil, width=ATTN_W):
    for r in range(dil):
        for j in range(width // 128):
            c0 = r * width + j * 128
            slabs.at[j][pl.ds(r, TM // dil, stride=dil), :] = in_ref[:, c0:c0 + 128].astype(F32)
    return jnp.concatenate([slabs[j] for j in range(width // 128)], axis=1)


def _head_spread():
    m = lax.broadcasted_iota(jnp.int32, (HEAD_W, ATTN_W), 0)
    lane = lax.broadcasted_iota(jnp.int32, (HEAD_W, ATTN_W), 1)
    return (m == 16 * (lane >> 6)).astype(F32)


def _head_sum():
    lane = lax.broadcasted_iota(jnp.int32, (ATTN_W, HEAD_W), 0)
    m = lax.broadcasted_iota(jnp.int32, (ATTN_W, HEAD_W), 1)
    return ((lane >> 6) == (m >> 4)).astype(F32)


def _seq_params():
    return pltpu.CompilerParams(dimension_semantics=("arbitrary",), vmem_limit_bytes=VMEM_LIMIT)


def _sds(shape, dtype):
    return jax.ShapeDtypeStruct(shape, dtype)


class _Comm:
    def __init__(self, args, out_shape, n_sems, start, finish, aliased=False):
        self.args, self.out_shape, self.n_sems = list(args), list(out_shape), n_sems
        self.start, self.finish, self.aliased = start, finish, aliased


def _pcall(body, *, name, grid, in_specs, out_specs, out_shape, args, scratch_shapes=(), comms=(), after=()):
    single = not isinstance(out_shape, (list, tuple))
    out_specs = [out_specs] if single else list(out_specs)
    out_shape = [out_shape] if single else list(out_shape)
    n_in, n_out, n_scr = len(in_specs), len(out_shape), len(scratch_shapes)
    c_args = [a for c in comms for a in c.args]
    c_outs = [o for c in comms for o in c.out_shape]
    aliases, ai, ao = {}, n_in, n_out
    for c in comms:
        if c.aliased:
            aliases.update({ai + k: ao + k for k in range(len(c.args))})
        ai += len(c.args)
        ao += len(c.out_shape)
    sems = [pltpu.SemaphoreType.DMA((c.n_sems,)) for c in comms for _ in range(2)]
    steps = grid[0]

    def wrapped(*refs):
        o0 = n_in + len(c_args) + len(after)
        s0 = o0 + n_out + len(c_outs)
        m_in, m_out, m_sem = refs[n_in:n_in + len(c_args)], refs[o0 + n_out:s0], refs[s0 + n_scr:]

        def each(phase):
            ii = oi = 0
            for k, c in enumerate(comms):
                getattr(c, phase)(m_in[ii:ii + len(c.args)], m_out[oi:oi + len(c.out_shape)],
                                  m_sem[2 * k], m_sem[2 * k + 1])
                ii += len(c.args)
                oi += len(c.out_shape)

        if comms:
            @pl.when(pl.program_id(0) == 0)
            def _():
                each("start")

        body(*refs[:n_in], *refs[o0:o0 + n_out], *refs[s0:s0 + n_scr])

        if comms:
            @pl.when(pl.program_id(0) == steps - 1)
            def _():
                each("finish")

    res = pl.pallas_call(
        wrapped, name=name, grid=grid,
        in_specs=list(in_specs) + [ANY] * (len(c_args) + len(after)), out_specs=out_specs + [ANY] * len(c_outs),
        out_shape=out_shape + c_outs, scratch_shapes=list(scratch_shapes) + sems,
        input_output_aliases=aliases, compiler_params=_seq_params(),
    )(*args, *c_args, *after)
    mine = res[0] if single else list(res[:n_out])
    if not comms:
        return mine
    theirs, oi = [], n_out
    for c in comms:
        theirs.append(list(res[oi:oi + len(c.out_shape)]))
        oi += len(c.out_shape)
    return mine, theirs


def _in_pieces(g):
    lo, hi = 512 * g, 512 * (g + 1)
    return [(s, max(lo, IN_S * s) - IN_S * s, min(hi, IN_S * (s + 1)) - IN_S * s)
            for s in range(N_SHARD) if max(lo, IN_S * s) < min(hi, IN_S * (s + 1))]


def _inproj_fwd(x, pos, g_pre, w_in, lg, lb, w_sp, b_t, comms=()):
    def body(x_ref, pos_ref, g_ref, w_ref, lg_ref, lb_ref, wsp_ref, bt_ref, h_ref, u_ref, vs_ref, sgu_ref, *rest):
        qkv_refs, slabs = rest[:9], rest[9:]
        xh, _ = _rms_stats(x_ref[...])
        h = (xh * g_ref[...]).astype(BF16)
        h_ref[...] = h
        tabs = _rot_tables(pos_ref[...])

        def group(g):
            return jnp.concatenate([_dot(h, w_ref[s, :, a:b]) for s, a, b in _in_pieces(g)], axis=1)

        for t in range(3):
            val = group(t)
            if t < 2:
                val = _rope(val, tabs)
            if t == 0:
                val = val * np.float32(ATTN_SCALE)
            qkv_refs[t][...] = val.astype(BF16)
            for i, dil in enumerate(DILATIONS[1:]):
                _store_view(val, qkv_refs[3 * (i + 1) + t], slabs[t], dil)
        u = group(3)
        vs = group(4)
        u_ref[...] = u
        vs_ref[...] = vs
        bias_full = _dot_exact(bt_ref[...], _group_ones())
        gu, _, _, _, mixed, _, _ = _sgu_core(u, vs, lg_ref[...], lb_ref[...], _masked_spatial(wsp_ref), bias_full)
        sgu_ref[...] = gu * mixed

    return _pcall(
        body, name="inproj_sgu_fwd", grid=(SEQ // TM,),
        in_specs=[_rows(D_MODEL), _rows(1), _resident((1, D_MODEL)), _resident((N_SHARD, D_MODEL, IN_S)),
                  _resident((1, SGU_W)), _resident((1, SGU_W)), _resident((SGU_GROUPS, CHUNK, CHUNK)),
                  _resident((CHUNK, SGU_GROUPS))],
        out_specs=[_rows(D_MODEL), _rows(512), _rows(512), _rows(512)]
        + [_view_rows(dil) for dil in DILATIONS for _ in range(3)],
        out_shape=[_sds((SEQ, D_MODEL), BF16), _sds((SEQ, 512), F32), _sds((SEQ, 512), F32), _sds((SEQ, 512), F32)]
        + [_view_shape(dil, BF16) for dil in DILATIONS for _ in range(3)],
        scratch_shapes=[_slab_scratch() for _ in range(3)],
        args=(x, pos, g_pre, w_in, lg, lb, w_sp, b_t), comms=comms)


def _block_masks():
    row = lax.broadcasted_iota(jnp.int32, (CHUNK, CHUNK), 0)
    col = lax.broadcasted_iota(jnp.int32, (CHUNK, CHUNK), 1)
    return col <= row, col >= row


def _attn_fwd(qv, kv, vv, dil, comms=()):
    seg = SEQ // dil
    nblk = seg // CHUNK
    rps = 4 if nblk == 1 else 1

    def body(q_ref, k_ref, v_ref, o_ref, l_ref):
        left = _left_half()
        m_cur, m_prev = _block_masks()
        zero = jnp.zeros((CHUNK, CHUNK), BF16)
        ones = (jnp.where(left, 1.0, 0.0).astype(BF16), jnp.where(left, 0.0, 1.0).astype(BF16))

        sides = tuple(enumerate((left, ~left)))

        def rows(b):
            if isinstance(b, int):
                return b * CHUNK, max(b - 1, 0) * CHUNK
            return pl.multiple_of(b * CHUNK, CHUNK), pl.multiple_of(jnp.maximum(b - 1, 0) * CHUNK, CHUNK)

        def first(rr, b):
            r0, rp = rows(b)
            prev_ok = m_prev & (b > 0)
            tiles, scores = [], []
            for hp in range(4):
                ls = slice(rr * ATTN_W + hp * 128, rr * ATTN_W + (hp + 1) * 128)
                qp = q_ref[pl.ds(r0, CHUNK), ls]
                kc = k_ref[pl.ds(r0, CHUNK), ls]
                kp = k_ref[pl.ds(rp, CHUNK), ls] if nblk > 1 else None
                tiles.append((ls, v_ref[pl.ds(r0, CHUNK), ls], v_ref[pl.ds(rp, CHUNK), ls] if nblk > 1 else None))
                for _, hm in sides:
                    qh = jnp.where(hm, qp, zero)
                    sc = jnp.where(m_cur, _dot_nt(qh, kc), NEG)
                    sp = jnp.where(prev_ok, _dot_nt(qh, kp), NEG) if nblk > 1 else None
                    scores.append((sc, sp))
            return tiles, scores

        def second(scores):
            probs = []
            for sc, sp in scores:
                if nblk > 1:
                    m = jnp.max(jnp.maximum(sc, sp), axis=-1, keepdims=True)
                    pc = jnp.exp(sc - m)
                    pp = jnp.exp(sp - m)
                    probs.append((m, pc.astype(BF16), pp.astype(BF16), (pc + pp).astype(BF16)))
                else:
                    m = jnp.max(sc, axis=-1, keepdims=True)
                    pc = jnp.exp(sc - m).astype(BF16)
                    probs.append((m, pc, None, pc))
            return probs

        def third(rr, b, tiles, probs):
            r0, _ = rows(b)
            for hp, (ls, vc, vp) in enumerate(tiles):
                acc = jnp.zeros((CHUNK, CHUNK), F32)
                den = jnp.zeros((CHUNK, CHUNK), F32)
                for side, hm in sides:
                    _, pc, pp, psum = probs[2 * hp + side]
                    acc = acc + _dot(pc, jnp.where(hm, vc, zero))
                    if nblk > 1:
                        acc = acc + _dot(pp, jnp.where(hm, vp, zero))
                    den = den + _dot(psum, ones[side])
                o_ref[pl.ds(r0, CHUNK), ls] = (acc / den).astype(o_ref.dtype)
                lse = jnp.where(left, probs[2 * hp][0], probs[2 * hp + 1][0]) + jnp.log(den)
                l_ref[pl.ds(r0, CHUNK), rr * HEAD_W + 32 * hp:rr * HEAD_W + 32 * hp + 32] = lse[:, 48:80]

        def run(units):
            data = [first(rr, b) for rr, b in units]
            probs = [second(scores) for _, scores in data]
            for (rr, b), (tiles, _), pr in zip(units, data, probs):
                third(rr, b, tiles, pr)

        if nblk == 1:
            run([(rr, 0) for rr in range(rps)])
        else:
            def one(b, carry):
                run([(0, b)])
                return carry

            lax.fori_loop(0, nblk, one, 0)

    spec = pl.BlockSpec((seg, rps * ATTN_W), lambda r: (0, r))
    return _pcall(
        body, name=f"attn_fwd_d{dil}", grid=(dil // rps,),
        in_specs=[spec, spec, spec], out_specs=[spec, pl.BlockSpec((seg, rps * HEAD_W), lambda r: (0, r))],
        out_shape=[_sds((seg, dil * ATTN_W), BF16), _sds((seg, dil * HEAD_W), F32)],
        args=(qv, kv, vv), comms=comms)


def _lane_left(nrows):
    return lax.broadcasted_iota(jnp.int32, (nrows, CHUNK), 1) < HEAD_DIM


def _attn_rows(b):
    if isinstance(b, int):
        return b * CHUNK, max(b - 1, 0) * CHUNK
    return pl.multiple_of(b * CHUNK, CHUNK), pl.multiple_of(jnp.maximum(b - 1, 0) * CHUNK, CHUNK)


def _mix_out_fwd(o_list, l_list, sgu, x, w_out, g_attn, g_sgu, g_post, comms=()):
    def body(o1, o2, o3, l1, l2, l3, sgu_ref, x_ref, w_ref, ga_ref, gs_ref, gp_ref,
             attn_ref, mixed_ref, y_ref, x1_ref, lse1_ref, lse2_ref, lse3_ref, slabs_a, slabs_b):
        os = [o1[...], _load_view(o2, slabs_a, DILATIONS[1]), _load_view(o3, slabs_b, DILATIONS[2])]
        ls = [l1[...], _load_view(l2, slabs_a, DILATIONS[1], HEAD_W), _load_view(l3, slabs_b, DILATIONS[2], HEAD_W)]
        m = jnp.maximum(jnp.maximum(ls[0], ls[1]), ls[2])
        es = [jnp.exp(l - m) for l in ls]
        den = es[0] + es[1] + es[2]
        spread = _head_spread()
        attn = sum(_dot_select(e / den, spread) * o for e, o in zip(es, os))
        attn_ref[...] = attn
        lse = m + jnp.log(den)
        lse1_ref[...] = lse
        _store_view(lse, lse2_ref, slabs_a, DILATIONS[1])
        _store_view(lse, lse3_ref, slabs_b, DILATIONS[2])
        ah, _ = _rms_stats(attn)
        sh, _ = _rms_stats(sgu_ref[...])
        mixed = jnp.concatenate([ah * ga_ref[...], sh * gs_ref[...]], axis=1).astype(BF16)
        mixed_ref[...] = mixed
        y = _dot(mixed[:, 0:OUT_S], w_ref[0])
        for s in range(1, N_SHARD):
            y = y + _dot(mixed[:, s * OUT_S:(s + 1) * OUT_S], w_ref[s])
        y_ref[...] = y
        yh, _ = _rms_stats(y)
        x1_ref[...] = x_ref[...] + yh * gp_ref[...]

    return _pcall(
        body, name="mix_out_fwd", grid=(SEQ // TM,),
        in_specs=[_view_rows(dil) for dil in DILATIONS] + [_view_rows(dil, HEAD_W) for dil in DILATIONS]
        + [_rows(512), _rows(D_MODEL), _resident((N_SHARD, OUT_S, D_MODEL)),
           _resident((1, 512)), _resident((1, 512)), _resident((1, D_MODEL))],
        out_specs=[_rows(512), _rows(D_MODEL), _rows(D_MODEL), _rows(D_MODEL)]
        + [_view_rows(dil, HEAD_W) for dil in DILATIONS],
        out_shape=[_sds((SEQ, 512), F32), _sds((SEQ, D_MODEL), BF16), _sds((SEQ, D_MODEL), F32),
                   _sds((SEQ, D_MODEL), F32)] + [_view_shape(dil, F32, HEAD_W) for dil in DILATIONS],
        scratch_shapes=[_slab_scratch(), _slab_scratch()],
        args=(*o_list, *l_list, sgu, x, w_out, g_attn, g_sgu, g_post), comms=comms)


def _ffn_fwd_bwd(x1, target, w_gate, w_up, w_down, g_pre, g_post, comms=()):
    def body(x1_ref, t_ref, wg_ref, wu_ref, wd_ref, gpf_ref, gpo_ref,
             h2_ref, a_ref, dg_ref, dup_ref, df_ref, dx1_ref, loss_ref, dgpf_ref, dgpo_ref, g_scr, up_scr):
        @pl.when(pl.program_id(0) == 0)
        def _():
            loss_ref[...] = jnp.zeros_like(loss_ref)
            dgpf_ref[...] = jnp.zeros_like(dgpf_ref)
            dgpo_ref[...] = jnp.zeros_like(dgpo_ref)

        x1 = x1_ref[...]
        gpf = gpf_ref[...]
        gpo = gpo_ref[...]
        xh, r = _rms_stats(x1)
        h2 = (xh * gpf).astype(BF16)
        h2_ref[...] = h2
        f = jnp.zeros((TM_FFN, D_MODEL), F32)
        for c0, c1 in FF_CHUNKS:
            g = _dot_nt(h2, wg_ref[c0:c1, :])
            up = _dot_nt(h2, wu_ref[c0:c1, :])
            g_scr[:, c0:c1] = g
            up_scr[:, c0:c1] = up
            a = (g * _sigmoid(g) * up).astype(BF16)
            a_ref[:, c0:c1] = a
            f = f + _dot(a, wd_ref[c0:c1, :])
        fh, rf = _rms_stats(f)
        diff = x1 + fh * gpo - t_ref[...]
        loss_ref[...] += jnp.sum(diff * diff, axis=0, keepdims=True)
        dout = diff * np.float32(1.0 / D_MODEL)
        df, dgpo = _rms_bwd(fh, rf, gpo, dout)
        dgpo_ref[...] += dgpo
        dfb = df.astype(BF16)
        df_ref[...] = dfb
        dh2 = jnp.zeros((TM_FFN, D_MODEL), F32)
        for c0, c1 in FF_CHUNKS:
            da = _dot_nt(dfb, wd_ref[c0:c1, :])
            g = g_scr[:, c0:c1]
            up = up_scr[:, c0:c1]
            sg = _sigmoid(g)
            dup = (da * (g * sg)).astype(BF16)
            dg = (da * up * (sg * (1.0 + g * (1.0 - sg)))).astype(BF16)
            dg_ref[:, c0:c1] = dg
            dup_ref[:, c0:c1] = dup
            dh2 = dh2 + _dot(dg, wg_ref[c0:c1, :]) + _dot(dup, wu_ref[c0:c1, :])
        dx, dgpf = _rms_bwd(xh, r, gpf, dh2)
        dgpf_ref[...] += dgpf
        dx1_ref[...] = dout + dx

    return _pcall(
        body, name="ffn_fwd_bwd", grid=(SEQ // TM_FFN,),
        in_specs=[_rows(D_MODEL, TM_FFN), _rows(D_MODEL, TM_FFN), _resident((FF, D_MODEL)),
                  _resident((FF, D_MODEL)), _resident((FF, D_MODEL)),
                  _resident((1, D_MODEL)), _resident((1, D_MODEL))],
        out_specs=[_rows(D_MODEL, TM_FFN), _rows(FF, TM_FFN), _rows(FF, TM_FFN), _rows(FF, TM_FFN),
                   _rows(D_MODEL, TM_FFN), _rows(D_MODEL, TM_FFN), _acc(D_MODEL), _acc(D_MODEL), _acc(D_MODEL)],
        out_shape=[_sds((SEQ, D_MODEL), BF16), _sds((SEQ, FF), BF16), _sds((SEQ, FF), BF16),
                   _sds((SEQ, FF), BF16), _sds((SEQ, D_MODEL), BF16), _sds((SEQ, D_MODEL), F32),
                   _sds((1, D_MODEL), F32), _sds((1, D_MODEL), F32), _sds((1, D_MODEL), F32)],
        scratch_shapes=[pltpu.VMEM((TM_FFN, FF), F32), pltpu.VMEM((TM_FFN, FF), F32)],
        args=(x1, target, w_gate, w_up, w_down, g_pre, g_post), comms=comms)


def _wgrad(a, b, a_spec, b_spec, out_block, name, comms=(), after=()):
    def body(a_ref, b_ref, o_ref, ob_ref):
        av = a_ref[0] if len(a_ref.shape) == 3 else a_ref[...]
        bv = b_ref[0] if len(b_ref.shape) == 3 else b_ref[...]
        g = _dot_tn(av, bv)
        o_ref[0] = g
        ob_ref[0] = g.astype(BF16)

    return _pcall(
        body, name=name, grid=(N_SHARD,),
        in_specs=[a_spec, b_spec],
        out_specs=[pl.BlockSpec((1,) + out_block, lambda s: (s, 0, 0))] * 2,
        out_shape=[_sds((N_SHARD,) + out_block, F32), _sds((N_SHARD,) + out_block, BF16)],
        args=(a, b), comms=comms, after=after)


def _outproj_bwd(dx1, y, attn, sgu, w_out, g_post, g_attn, g_sgu, u, vs, lg, lb, w_sp, b_t, comms=(), after=()):
    sgu_bwd = _sgu_bwd_body()

    def body(dx1_ref, y_ref, attn_ref, sgu_ref, w_ref, gp_ref, ga_ref, gs_ref, u_ref, vs_ref, lg_ref, lb_ref,
             wsp_ref, bt_ref, dy_ref, du_ref, dvs_ref, dgp_ref, dga_ref, dgs_ref, dw_ref, db_ref, dlg_ref, dlb_ref,
             *rest):
        dattn_refs, delta_refs, (slabs_a, slabs_b, dsgu_ref, dbias_scr) = rest[0:3], rest[3:6], rest[6:]

        @pl.when(pl.program_id(0) == 0)
        def _():
            dgp_ref[...] = jnp.zeros_like(dgp_ref)
            dga_ref[...] = jnp.zeros_like(dga_ref)
            dgs_ref[...] = jnp.zeros_like(dgs_ref)

        yh, ry = _rms_stats(y_ref[...])
        dy, dgp = _rms_bwd(yh, ry, gp_ref[...], dx1_ref[...])
        dgp_ref[...] += dgp
        dyb = dy.astype(BF16)
        dy_ref[...] = dyb
        dmixed = jnp.concatenate([_dot_nt(dyb, w_ref[s]) for s in range(N_SHARD)], axis=1)
        attn = attn_ref[...]
        ah, ra = _rms_stats(attn)
        dattn, dga = _rms_bwd(ah, ra, ga_ref[...], dmixed[:, 0:512])
        dga_ref[...] += dga
        sh, rs = _rms_stats(sgu_ref[...])
        dsgu, dgs = _rms_bwd(sh, rs, gs_ref[...], dmixed[:, 512:1024])
        dgs_ref[...] += dgs
        dsgu_ref[...] = dsgu
        delta = _dot_select(dattn * attn, _head_sum())
        dattn_refs[0][...] = dattn.astype(BF16)
        delta_refs[0][...] = delta
        for i, dil in enumerate(DILATIONS[1:]):
            _store_view(dattn, dattn_refs[i + 1], slabs_a, dil)
            _store_view(delta, delta_refs[i + 1], slabs_b, dil)
        sgu_bwd(u_ref, vs_ref, dsgu_ref, lg_ref, lb_ref, wsp_ref, bt_ref,
                du_ref, dvs_ref, dw_ref, db_ref, dlg_ref, dlb_ref, dbias_scr)

    return _pcall(
        body, name="outproj_sgu_bwd", grid=(SEQ // TM,),
        in_specs=[_rows(D_MODEL), _rows(D_MODEL), _rows(512), _rows(512), _resident((N_SHARD, OUT_S, D_MODEL)),
                  _resident((1, D_MODEL)), _resident((1, 512)), _resident((1, 512)),
                  _rows(SGU_W), _rows(SGU_W), _resident((1, SGU_W)), _resident((1, SGU_W)),
                  _resident((SGU_GROUPS, CHUNK, CHUNK)), _resident((CHUNK, SGU_GROUPS))],
        out_specs=[_rows(D_MODEL), _rows(SGU_W), _rows(SGU_W), _acc(D_MODEL), _acc(512), _acc(512),
                   pl.BlockSpec((SGU_GROUPS, CHUNK, CHUNK), lambda i: (0, 0, 0)), _acc(CHUNK, SGU_GROUPS),
                   _acc(SGU_W), _acc(SGU_W)]
        + [_view_rows(dil) for dil in DILATIONS] + [_view_rows(dil, HEAD_W) for dil in DILATIONS],
        out_shape=[_sds((SEQ, D_MODEL), BF16), _sds((SEQ, SGU_W), BF16), _sds((SEQ, SGU_W), BF16),
                   _sds((1, D_MODEL), F32), _sds((1, 512), F32), _sds((1, 512), F32),
                   _sds((SGU_GROUPS, CHUNK, CHUNK), F32), _sds((SGU_GROUPS, CHUNK), F32),
                   _sds((1, SGU_W), F32), _sds((1, SGU_W), F32)]
        + [_view_shape(dil, BF16) for dil in DILATIONS] + [_view_shape(dil, F32, HEAD_W) for dil in DILATIONS],
        scratch_shapes=[_slab_scratch(), _slab_scratch(), pltpu.VMEM((TM, SGU_W), F32),
                        pltpu.VMEM((CHUNK, SGU_W), F32)],
        args=(dx1, y, attn, sgu, w_out, g_post, g_attn, g_sgu, u, vs, lg, lb, w_sp, b_t), comms=comms, after=after)


def _sgu_bwd_body():
    nsteps = SEQ // TM

    def body(u_ref, vs_ref, ds_ref, lg_ref, lb_ref, w_ref, bt_ref,
             du_ref, dvs_ref, dw_ref, db_ref, dlg_ref, dlb_ref, dbias_scr):
        i = pl.program_id(0)

        @pl.when(i == 0)
        def _():
            dw_ref[...] = jnp.zeros_like(dw_ref)
            dlg_ref[...] = jnp.zeros_like(dlg_ref)
            dlb_ref[...] = jnp.zeros_like(dlb_ref)
            dbias_scr[...] = jnp.zeros_like(dbias_scr)

        wm = _masked_spatial(w_ref)
        ones_g = _group_ones()
        bias_full = _dot_exact(bt_ref[...], ones_g)
        u = u_ref[...]
        vs = vs_ref[...]
        lg = lg_ref[...]
        gu, xh, rstd, vnb, mixed, cdf_u, cdf_vs = _sgu_core(u, vs, lg, lb_ref[...], wm, bias_full)
        dsgu = ds_ref[...]
        du_ref[...] = (dsgu * mixed * _gelu_grad(u, cdf_u)).astype(BF16)
        dmixed = dsgu * gu
        left = _left_half()
        dvn_rows = []
        for c in range(TM // CHUNK):
            rs = slice(c * CHUNK, (c + 1) * CHUNK)
            dm_c = dmixed[rs, :]
            dbias_scr[...] += dm_c
            pieces = []
            for p in range(4):
                ls = slice(p * 128, (p + 1) * 128)
                dmp = dm_c[:, ls]
                vp = vnb[rs, ls]
                dmb = dmp.astype(BF16)
                zero = jnp.zeros_like(dmb)
                dw_ref[2 * p] += _dot_nt(jnp.where(left, dmb, zero), vp)
                dw_ref[2 * p + 1] += _dot_nt(jnp.where(left, zero, dmb), vp)
                pieces.append(jnp.where(left, _dot_tn(wm[2 * p], dmb), _dot_tn(wm[2 * p + 1], dmb)))
            dvn_rows.append(jnp.concatenate(pieces, axis=1))
        dvn = jnp.concatenate(dvn_rows, axis=0)
        dlg_ref[...] += jnp.sum(dvn * xh, axis=0, keepdims=True)
        dlb_ref[...] += jnp.sum(dvn, axis=0, keepdims=True)
        dxh = dvn * lg
        dgv = rstd * (dxh - jnp.mean(dxh, axis=-1, keepdims=True) - xh * jnp.mean(dxh * xh, axis=-1, keepdims=True))
        dvs_ref[...] = (dgv * _gelu_grad(vs, cdf_vs)).astype(BF16)

        @pl.when(i == nsteps - 1)
        def _():
            row = lax.broadcasted_iota(jnp.int32, (CHUNK, CHUNK), 0)
            col = lax.broadcasted_iota(jnp.int32, (CHUNK, CHUNK), 1)
            for g in range(SGU_GROUPS):
                dw_ref[g] = jnp.where(col <= row, dw_ref[g], 0.0)
            db_ref[...] = lax.dot_general(ones_g, dbias_scr[...], (((1,), (1,)), ((), ())),
                                          preferred_element_type=F32, precision=lax.Precision.HIGHEST)

    return body


def _attn_bwd(qv, kv, vv, dov, deltav, lsev, dil, comms=(), after=()):
    seg = SEQ // dil
    nblk = seg // CHUNK
    rps = 4 if nblk == 1 else 1

    def body(q_ref, k_ref, v_ref, do_ref, dl_ref, lse_ref, dq_ref, dk_ref, dv_ref, dk_wait, dv_wait):
        left = _left_half()
        m_cur, m_prev = _block_masks()
        sides = tuple(enumerate((left, ~left)))
        zero = jnp.zeros((CHUNK, CHUNK), BF16)

        def first(rr, b, both):
            r0, rp = _attn_rows(b)
            keys = pl.ds(rp, 2 * CHUNK) if both else pl.ds(r0, CHUNK)
            tiles, heads = [], []
            for hp in range(4):
                ls = slice(rr * ATTN_W + hp * 128, rr * ATTN_W + (hp + 1) * 128)
                qp = q_ref[pl.ds(r0, CHUNK), ls]
                dop = do_ref[pl.ds(r0, CHUNK), ls]
                k2 = k_ref[keys, ls]
                v2 = v_ref[keys, ls]
                tiles.append((ls, k2))
                for _, hm in sides:
                    qh = jnp.where(hm, qp, zero)
                    doh = jnp.where(hm, dop, zero)
                    heads.append((qh, doh, _dot_nt(k2, qh), _dot_nt(v2, doh)))
            return tiles, heads

        def second(rr, b, heads, both):
            r0, _ = _attn_rows(b)
            ok = jnp.concatenate([m_cur, m_prev], axis=0) if both else m_prev
            lanes = slice(rr * HEAD_W, (rr + 1) * HEAD_W)
            lse_t = lse_ref[pl.ds(r0, CHUNK), lanes].T
            dl_t = dl_ref[pl.ds(r0, CHUNK), lanes].T
            out = []
            for i, (_, _, s_t, dp_t) in enumerate(heads):
                p = jnp.exp(jnp.where(ok, s_t - lse_t[16 * i:16 * i + 1, :], NEG))
                out.append((p.astype(BF16), (p * (dp_t - dl_t[16 * i:16 * i + 1, :])).astype(BF16)))
            return out

        def third(rr, b, tiles, heads, probs, both):
            r0, rp = _attn_rows(b)
            nk = 2 * CHUNK if both else CHUNK
            left_k = _lane_left(nk)
            zero_k = jnp.zeros((nk, CHUNK), BF16)
            for hp, (ls, k2) in enumerate(tiles):
                dq = jnp.zeros((CHUNK, CHUNK), F32)
                dk2 = jnp.zeros((nk, CHUNK), F32)
                dv2 = jnp.zeros((nk, CHUNK), F32)
                for side in range(2):
                    qh, doh, _, _ = heads[2 * hp + side]
                    p, ds = probs[2 * hp + side]
                    dq = dq + _dot_tn(ds, jnp.where(left_k if side == 0 else ~left_k, k2, zero_k))
                    dk2 = dk2 + _dot(ds, qh)
                    dv2 = dv2 + _dot(p, doh)
                dq_ref[pl.ds(r0, CHUNK), ls] = dq.astype(dq_ref.dtype)
                if nblk == 1:
                    dk_ref[pl.ds(r0, CHUNK), ls] = dk2.astype(dk_ref.dtype)
                    dv_ref[pl.ds(r0, CHUNK), ls] = dv2.astype(dv_ref.dtype)
                elif both:
                    dk_ref[pl.ds(rp, CHUNK), ls] = (dk_wait[:, ls] + dk2[0:CHUNK]).astype(dk_ref.dtype)
                    dv_ref[pl.ds(rp, CHUNK), ls] = (dv_wait[:, ls] + dv2[0:CHUNK]).astype(dv_ref.dtype)
                    dk_wait[:, ls] = dk2[CHUNK:]
                    dv_wait[:, ls] = dv2[CHUNK:]
                else:
                    dk_wait[:, ls] = dk2
                    dv_wait[:, ls] = dv2

        def run(units, both):
            data = [first(rr, b, both) for rr, b in units]
            probs = [second(rr, b, heads, both) for (rr, b), (_, heads) in zip(units, data)]
            for (rr, b), (tiles, heads), pr in zip(units, data, probs):
                third(rr, b, tiles, heads, pr, both)

        run([(rr, 0) for rr in range(rps)], False)
        if nblk > 1:
            def one(b, carry):
                run([(0, b)], True)
                return carry

            lax.fori_loop(1, nblk, one, 0)
            last = (nblk - 1) * CHUNK
            dk_ref[last:last + CHUNK, :] = dk_wait[...].astype(dk_ref.dtype)
            dv_ref[last:last + CHUNK, :] = dv_wait[...].astype(dv_ref.dtype)

    spec = pl.BlockSpec((seg, rps * ATTN_W), lambda r: (0, r))
    return _pcall(
        body, name=f"attn_bwd_d{dil}", grid=(dil // rps,),
        in_specs=[spec] * 4 + [pl.BlockSpec((seg, rps * HEAD_W), lambda r: (0, r))] * 2, out_specs=[spec] * 3,
        out_shape=[_sds((seg, dil * ATTN_W), BF16)] * 3,
        scratch_shapes=[pltpu.VMEM((CHUNK, ATTN_W), F32), pltpu.VMEM((CHUNK, ATTN_W), F32)],
        args=(qv, kv, vv, dov, deltav, lsev), comms=comms, after=after)


def _inproj_bwd(dqs, dks, dvs, du, dvs_sgu, pos, x, dx1, w_in, g_pre, comms=()):
    def body(dq1, dq2, dq3, dk1, dk2, dk3, dv1, dv2, dv3, du_ref, dvs_ref, pos_ref, x_ref, dx1_ref, w_ref, g_ref,
             dproj_ref, gx_ref, dg_ref, slabs_a, slabs_b):
        @pl.when(pl.program_id(0) == 0)
        def _():
            dg_ref[...] = jnp.zeros_like(dg_ref)

        def total(r1, r2, r3):
            return r1[...] + _load_view(r2, slabs_a, DILATIONS[1]) + _load_view(r3, slabs_b, DILATIONS[2])

        tabs = _rot_tables(pos_ref[...])
        groups = {3: du_ref[...], 4: dvs_ref[...]}
        dh = jnp.zeros((TM, D_MODEL), F32)
        for g in (3, 4, 0, 1, 2):
            if g == 0:
                groups[g] = _rope_bwd(total(dq1, dq2, dq3) * np.float32(ATTN_SCALE), tabs).astype(BF16)
            elif g == 1:
                groups[g] = _rope_bwd(total(dk1, dk2, dk3), tabs).astype(BF16)
            elif g == 2:
                groups[g] = total(dv1, dv2, dv3).astype(BF16)
            dproj_ref[:, 512 * g:512 * (g + 1)] = groups[g]
            off = 0
            for s, a, b in _in_pieces(g):
                dh = dh + _dot_nt(groups[g][:, off:off + b - a], w_ref[s, :, a:b])
                off += b - a
        g = g_ref[...]
        xh, r = _rms_stats(x_ref[...])
        dx, dg = _rms_bwd(xh, r, g, dh)
        dg_ref[...] += dg
        gx_ref[...] = dx1_ref[...] + dx

    return _pcall(
        body, name="inproj_bwd", grid=(SEQ // TM,),
        in_specs=[_view_rows(dil) for dil in DILATIONS] * 3
        + [_rows(512), _rows(512), _rows(1), _rows(D_MODEL), _rows(D_MODEL),
           _resident((N_SHARD, D_MODEL, IN_S)), _resident((1, D_MODEL))],
        out_specs=[_rows(PROJ_W), _rows(D_MODEL), _acc(D_MODEL)],
        out_shape=[_sds((SEQ, PROJ_W), BF16), _sds((SEQ, D_MODEL), F32), _sds((1, D_MODEL), F32)],
        scratch_shapes=[_slab_scratch(), _slab_scratch()],
        args=(*dqs, *dks, *dvs, du, dvs_sgu, pos, x, dx1, w_in, g_pre), comms=comms)


def _coords():
    return lax.axis_index("x"), lax.axis_index("y"), lax.axis_index("c")


def _other_chips(x, y):
    return [(1 - x, y), (x, 1 - y), (1 - x, 1 - y)]


WEIGHTS = ("pre_mix_norm", "w_in", "sgu_ln_gain", "sgu_ln_bias", "sgu_w_spatial", "sgu_b_spatial", "attn_out_norm",
           "sgu_out_norm", "w_out", "post_mix_norm", "pre_ffn_norm", "w_gate", "w_up", "w_down", "post_ffn_norm")
SMALL = ("pre_mix_norm", "post_mix_norm", "pre_ffn_norm", "post_ffn_norm", "sgu_ln_gain", "sgu_ln_bias",
         "attn_out_norm", "sgu_out_norm", "sgu_w_spatial", "sgu_b_spatial")


def _remote(src, dst, send_sem, recv_sem, to):
    return pltpu.make_async_remote_copy(src_ref=src, dst_ref=dst, send_sem=send_sem, recv_sem=recv_sem,
                                        device_id=to, device_id_type=MESH)


def _halves(a):
    *lead, rows, cols = a.shape
    return a.reshape(*lead, 2, rows // 2, cols)


def _gather_ici(shards):
    n = len(shards)

    def desc(ins, outs, ss, rs, j, w, landed):
        x, y, c = _coords()
        cx, cy = _other_chips(x, y)[j]
        shard = 2 * cx + cy if landed else 2 * x + y
        return _remote(ins[w].at[c], outs[w].at[shard, c], ss.at[j * n + w], rs.at[j * n + w], (cx, cy, c))

    def start(ins, outs, ss, rs):
        for j in range(3):
            for w in range(n):
                desc(ins, outs, ss, rs, j, w, False).start()

    def finish(ins, outs, ss, rs):
        for j in range(3):
            for w in range(n):
                desc(ins, outs, ss, rs, j, w, True).wait_recv()
                desc(ins, outs, ss, rs, j, w, False).wait_send()

    return _Comm(shards, [_sds((N_SHARD,) + s.shape, s.dtype) for s in shards], 3 * n, start, finish)


def _gather_pass(fulls):
    n = len(fulls)

    def desc(bufs, ss, rs, j, w, landed):
        x, y, c = _coords()
        cx, cy = _other_chips(x, y)[j]
        shard = 2 * cx + cy
        return _remote(bufs[w].at[shard, c], bufs[w].at[shard, 1 - c if landed else c],
                       ss.at[j * n + w], rs.at[j * n + w], (x, y, 1 - c))

    def start(ins, outs, ss, rs):
        for j in range(3):
            for w in range(n):
                desc(outs, ss, rs, j, w, False).start()

    def finish(ins, outs, ss, rs):
        for j in range(3):
            for w in range(n):
                desc(outs, ss, rs, j, w, True).wait_recv()
                desc(outs, ss, rs, j, w, False).wait_send()

    return _Comm(fulls, [_sds(f.shape, f.dtype) for f in fulls], 3 * n, start, finish, aliased=True)


def _rs_sibling(gws):
    n = len(gws)

    def desc(ins, outs, ss, rs, w):
        x, y, c = _coords()
        return _remote(ins[w].at[:, 1 - c], outs[w], ss.at[w], rs.at[w], (x, y, 1 - c))

    def start(ins, outs, ss, rs):
        for w in range(n):
            desc(ins, outs, ss, rs, w).start()

    def finish(ins, outs, ss, rs):
        for w in range(n):
            desc(ins, outs, ss, rs, w).wait()

    out_shape = [_sds((N_SHARD, g.shape[1] // 2, g.shape[2]), g.dtype) for g in gws]
    return _Comm([_halves(g) for g in gws], out_shape, n, start, finish)


def _rs_chips(pbs):
    n = len(pbs)

    def desc(ins, outs, ss, rs, j, w):
        x, y, c = _coords()
        cx, cy = _other_chips(x, y)[j]
        return _remote(ins[w].at[2 * cx + cy], outs[w].at[j], ss.at[j * n + w], rs.at[j * n + w], (cx, cy, c))

    def start(ins, outs, ss, rs):
        for j in range(3):
            for w in range(n):
                desc(ins, outs, ss, rs, j, w).start()

    def finish(ins, outs, ss, rs):
        for j in range(3):
            for w in range(n):
                desc(ins, outs, ss, rs, j, w).wait()

    return _Comm(pbs, [_sds((3,) + p.shape[1:], p.dtype) for p in pbs], 3 * n, start, finish)


def _rs_join(halves):
    n = len(halves)

    def desc(bufs, ss, rs, w, landed):
        x, y, c = _coords()
        return _remote(bufs[w].at[c], bufs[w].at[1 - c if landed else c], ss.at[w], rs.at[w], (x, y, 1 - c))

    def start(ins, outs, ss, rs):
        for w in range(n):
            desc(outs, ss, rs, w, False).start()

    def finish(ins, outs, ss, rs):
        for w in range(n):
            desc(outs, ss, rs, w, True).wait_recv()
            desc(outs, ss, rs, w, False).wait_send()

    return _Comm(halves, [_sds(h.shape, h.dtype) for h in halves], n, start, finish, aliased=True)


def _small_exchange(buf):
    def desc(ins, outs, ss, rs, k, landed):
        x, y, c = _coords()
        px = 1 - x if (k >> 2) & 1 else x
        py = 1 - y if (k >> 1) & 1 else y
        pc = 1 - c if k & 1 else c
        slot = 4 * px + 2 * py + pc if landed else 4 * x + 2 * y + c
        return _remote(ins[0], outs[0].at[slot], ss.at[k - 1], rs.at[k - 1], (px, py, pc))

    def start(ins, outs, ss, rs):
        for k in range(1, 8):
            desc(ins, outs, ss, rs, k, False).start()

    def finish(ins, outs, ss, rs):
        for k in range(1, 8):
            desc(ins, outs, ss, rs, k, True).wait_recv()
            desc(ins, outs, ss, rs, k, False).wait_send()

    return _Comm([buf], [_sds((8,) + buf.shape, buf.dtype)], 7, start, finish)


HBM = pl.BlockSpec(memory_space=pltpu.HBM)
SEM = pl.BlockSpec(memory_space=pltpu.SEMAPHORE)
DATAFLOW = pltpu.SideEffectType.DATAFLOW_SIDE_EFFECTING


def _split_starts(name, comms, after):
    srcs = [[pltpu.with_memory_space_constraint(s, pltpu.HBM) for s in c.args] for c in comms]
    lands = [[pltpu.with_memory_space_constraint(lax.empty(o.shape, o.dtype), pltpu.HBM) for o in c.out_shape]
             for c in comms]
    bufs = [b for k in range(len(comms)) for b in srcs[k] + lands[k]]
    nb, nc = len(bufs), len(comms)

    def body(*refs):
        sems = refs[nb + 1:nb + 1 + 2 * nc]
        off = 0
        for k, c in enumerate(comms):
            ns, nl = len(srcs[k]), len(lands[k])
            c.start(refs[off:off + ns], refs[off + ns:off + ns + nl], sems[2 * k], sems[2 * k + 1])
            off += ns + nl
        refs[-1][...] = jnp.zeros_like(refs[-1])

    res = pl.pallas_call(
        body, name=name,
        out_shape=(*[pltpu.SemaphoreType.DMA((c.n_sems,)) for c in comms for _ in range(2)],
                   *[pltpu.HBM(b.shape, b.dtype) for b in bufs], _sds((8, 128), F32)),
        in_specs=[HBM] * nb + [ANY],
        out_specs=(*[SEM] * (2 * nc), *[HBM] * nb, pl.BlockSpec(memory_space=pltpu.VMEM)),
        input_output_aliases={i: 2 * nc + i for i in range(nb)},
        compiler_params=pltpu.CompilerParams(has_side_effects=DATAFLOW),
    )(*bufs, after)
    states, off = [], 2 * nc
    for k in range(nc):
        ns, nl = len(srcs[k]), len(lands[k])
        states.append((res[2 * k], res[2 * k + 1], list(res[off:off + ns]), list(res[off + ns:off + ns + nl])))
        off += ns + nl
    return states, res[-1]


def _split_wait(name, comm, send_sems, recv_sems, srcs, lands, after):
    ns, nb = len(srcs), len(lands)
    after = list(after) if isinstance(after, (list, tuple)) else [after]

    def body(*refs):
        comm.finish(refs[:ns], refs[ns:ns + nb], refs[ns + nb], refs[ns + nb + 1])

    res = pl.pallas_call(
        body, name=name,
        out_shape=tuple(pltpu.HBM(b.shape, b.dtype) for b in srcs + lands),
        in_specs=[HBM] * (ns + nb) + [SEM, SEM] + [ANY] * len(after), out_specs=tuple([HBM] * (ns + nb)),
        input_output_aliases={i: i for i in range(ns + nb)},
        compiler_params=pltpu.CompilerParams(has_side_effects=DATAFLOW),
    )(*srcs, *lands, send_sems, recv_sems, *after)
    return list(res[ns:])


LOSS_ROW = "loss_cols"
SMALL_EARLY = ("post_mix_norm", "pre_ffn_norm", "post_ffn_norm", "sgu_ln_gain", "sgu_ln_bias", "attn_out_norm",
               "sgu_out_norm", "sgu_b_spatial", LOSS_ROW)
SMALL_LATE = ("pre_mix_norm",)
SMALL_WSP = "sgu_w_spatial"


def _pack(d, names, rows):
    flat = [d[n].reshape(-1) for n in names]
    used = sum(f.shape[0] for f in flat)
    flat.append(jnp.zeros((rows * 1024 - used,), F32))
    return jnp.concatenate(flat).reshape(rows, 1024)


def _unpack(buf, names, shapes):
    flat = buf.reshape(-1)
    out, off = {}, 0
    for n in names:
        size = int(np.prod(shapes[n]))
        out[n] = flat[off:off + size].reshape(shapes[n])
        off += size
    return out


def _chip_sums(gbs, recvs, shard_core):
    n = len(gbs)
    _, rows, cols = gbs[0].shape
    hw = rows // 2

    def body(sc_ref, *refs):
        for k in range(n):
            refs[2 * n + k][...] = (refs[k][...].astype(F32) + refs[n + k][...].astype(F32)).astype(BF16)

    def other(s, sc):
        return jnp.where(s >= sc[0], s + 1, s)

    mine = pl.BlockSpec((1, hw, cols), lambda s, sc: (other(s, sc), sc[1], 0))
    plain = pl.BlockSpec((1, hw, cols), lambda s, sc: (other(s, sc), 0, 0))
    return pl.pallas_call(
        body, name="rs_chip_sums",
        grid_spec=pltpu.PrefetchScalarGridSpec(num_scalar_prefetch=1, grid=(N_SHARD - 1,),
                                               in_specs=[mine] * n + [plain] * n, out_specs=[plain] * n),
        out_shape=[_sds((N_SHARD, hw, cols), BF16)] * n,
        compiler_params=_seq_params(),
    )(shard_core, *gbs, *recvs)


def _final_sums(gws, recv_sibs, recv_chips, shard_core):
    n = len(gws)
    halves = [(g.shape[1] // 2, g.shape[2]) for g in gws]

    def body(sc_ref, *refs):
        for k in range(n):
            acc = refs[k][0] + refs[n + k][0]
            for j in range(3):
                acc = acc + refs[2 * n + k][j].astype(F32)
            refs[3 * n + k][0] = acc

    def specs(lead, index):
        return [pl.BlockSpec((lead, hw, cols), index) for hw, cols in halves]

    return pl.pallas_call(
        body, name="rs_final_sums",
        grid_spec=pltpu.PrefetchScalarGridSpec(
            num_scalar_prefetch=1, grid=(1,),
            in_specs=specs(1, lambda i, sc: (sc[0], sc[1], 0)) + specs(1, lambda i, sc: (sc[0], 0, 0))
            + specs(3, lambda i, sc: (0, 0, 0)),
            out_specs=specs(1, lambda i, sc: (sc[1], 0, 0))),
        out_shape=[_sds((2, hw, cols), F32) for hw, cols in halves],
        compiler_params=_seq_params(),
    )(shard_core, *gws, *recv_sibs, *recv_chips)


ADAM_BLOCKS = 4


def _adamw_multi(ws, gs, ms, vs, name, after=()):
    n = len(ws)

    def body(*refs):
        outs = refs[4 * n + len(after):]
        for k in range(n):
            g = refs[n + k][...]
            d, m, v = _adam_math(refs[k][...], g, refs[2 * n + k][...], refs[3 * n + k][...])
            outs[4 * k][...], outs[4 * k + 1][...], outs[4 * k + 2][...], outs[4 * k + 3][...] = g, d, m, v

    specs = [pl.BlockSpec((w.shape[0] // ADAM_BLOCKS, w.shape[1]), lambda i: (i, 0)) for w in ws]
    res = pl.pallas_call(
        body, name=name, grid=(ADAM_BLOCKS,), in_specs=specs * 4 + [ANY] * len(after),
        out_specs=[s for s in specs for _ in range(4)],
        out_shape=[_sds(w.shape, F32) for w in ws for _ in range(4)],
        compiler_params=_seq_params(),
    )(*ws, *gs, *ms, *vs, *after)
    return [tuple(res[4 * k:4 * k + 4]) for k in range(n)]


def _wgrad_ff(a_list, b, name):
    n = len(a_list)
    cols = 256

    def body(*refs):
        bv = refs[n][...]
        for k in range(n):
            g = _dot_tn(refs[k][...], bv)
            refs[n + 1 + k][...] = g
            refs[2 * n + 1 + k][...] = g.astype(BF16)

    res = _pcall(
        body, name=name, grid=(FF // cols,),
        in_specs=[pl.BlockSpec((SEQ, cols), lambda j: (0, j))] * n
        + [pl.BlockSpec((SEQ, D_MODEL), lambda j: (0, 0), pipeline_mode=pl.Buffered(1))],
        out_specs=[pl.BlockSpec((cols, D_MODEL), lambda j: (j, 0))] * (2 * n),
        out_shape=[_sds((FF, D_MODEL), F32)] * n + [_sds((FF, D_MODEL), BF16)] * n, args=(*a_list, b))
    return [m.reshape(N_SHARD, FF_S, D_MODEL) for m in res]


def _comm_only(name, comms):
    return _pcall(lambda: None, name=name, grid=(1,), in_specs=[], out_specs=[], out_shape=[], args=(),
                  comms=comms)[1]


def _adam_math(w, g, m, v):
    m = ADAM_B1 * m + (1.0 - ADAM_B1) * g
    v = ADAM_B2 * v + (1.0 - ADAM_B2) * (g * g)
    m_hat = m / (1.0 - ADAM_B1 ** ADAM_STEP)
    v_hat = v / (1.0 - ADAM_B2 ** ADAM_STEP)
    return -ADAM_LR * (m_hat / (jnp.sqrt(v_hat) + ADAM_EPS) + ADAM_WD * w), m, v


def _adamw_small(own, slots, w, m, v, me):
    rows, cols = own.shape

    def body(me_ref, own_ref, slots_ref, w_ref, m_ref, v_ref, g_ref, d_ref, nm_ref, nv_ref):
        own_v = own_ref[...].astype(F32)
        g = jnp.where(me_ref[0] == 0, own_v, slots_ref[0].astype(F32))
        for i in range(1, 8):
            g = g + jnp.where(me_ref[0] == i, own_v, slots_ref[i].astype(F32))
        g_ref[...] = g
        d_ref[...], nm_ref[...], nv_ref[...] = _adam_math(w_ref[...], g, m_ref[...], v_ref[...])

    flat = pl.BlockSpec((rows, cols), lambda i, me_ref: (0, 0))
    return pl.pallas_call(
        body, name="adamw_small",
        grid_spec=pltpu.PrefetchScalarGridSpec(
            num_scalar_prefetch=1, grid=(1,),
            in_specs=[flat, pl.BlockSpec((8, rows, cols), lambda i, me_ref: (0, 0, 0)), flat, flat, flat],
            out_specs=[flat] * 4),
        out_shape=[_sds((rows, cols), F32)] * 4,
        compiler_params=_seq_params(),
    )(me, own, slots, w, m, v)


def kernel(x, positions, pre_mix_norm, w_in, sgu_ln_gain, sgu_ln_bias, sgu_w_spatial, sgu_b_spatial, attn_out_norm, sgu_out_norm, w_out, post_mix_norm, pre_ffn_norm, w_gate, w_up, w_down, post_ffn_norm, loss_target, m_pre_mix_norm, m_w_in, m_sgu_ln_gain, m_sgu_ln_bias, m_sgu_w_spatial, m_sgu_b_spatial, m_attn_out_norm, m_sgu_out_norm, m_w_out, m_post_mix_norm, m_pre_ffn_norm, m_w_gate, m_w_up, m_w_down, m_post_ffn_norm, v_pre_mix_norm, v_w_in, v_sgu_ln_gain, v_sgu_ln_bias, v_sgu_w_spatial, v_sgu_b_spatial, v_attn_out_norm, v_sgu_out_norm, v_w_out, v_post_mix_norm, v_pre_ffn_norm, v_w_gate, v_w_up, v_w_down, v_post_ffn_norm):
    a = dict(locals())
    cx, cy, cc = _coords()
    s_me = 2 * cx + cy
    shard_core = jnp.stack([s_me, cc]).astype(jnp.int32)
    me = jnp.stack([4 * cx + 2 * cy + cc]).astype(jnp.int32)
    small = {n: (a[n][0] if a[n].ndim > 2 else a[n]) for n in SMALL}
    b_t = small["sgu_b_spatial"].T
    xs, pos, target = x[0], positions.reshape(SEQ, 1), loss_target[0]
    flipped = ("w_gate", "w_up")

    def big(name, n):
        return jnp.swapaxes(a[name], 1, 2)[0] if n in flipped else a[name][0]

    own = {"w_in": _halves(big("w_in", "w_in").astype(BF16))}

    def with_own(full, n):
        full = lax.dynamic_update_slice(full, own[n][None], (s_me, 0, 0, 0))
        return full.reshape((N_SHARD,) + big(n, n).shape)

    ffn = ("w_gate", "w_up", "w_down")
    g_in = _gather_ici([own["w_in"]])
    (s_in,), token = _split_starts("gather_in_start", [g_in], small["pre_mix_norm"])
    for n in ("w_out",) + ffn:
        own[n] = _halves((big(n, n) + token[0:1, 0:1]).astype(BF16))
    g_out, g_ffn = _gather_ici([own["w_out"]]), _gather_ici([own[n] for n in ffn])
    (s_out, s_ffn), token = _split_starts("gather_rest_start", [g_out, g_ffn], token)
    in_lands = _split_wait("gather_in_wait", g_in, *s_in, token)
    ((in_lands,),) = _comm_only("comm_pass_in", [_gather_pass(in_lands)])
    w_in_f = with_own(in_lands, "w_in")
    h, u, vs, sgu, *qkv = _inproj_fwd(xs, pos, small["pre_mix_norm"], w_in_f, small["sgu_ln_gain"],
                                      small["sgu_ln_bias"], small["sgu_w_spatial"], b_t)
    views = [tuple(qkv[3 * i:3 * i + 3]) for i in range(len(DILATIONS))]
    out_lands = _split_wait("gather_out_wait", g_out, *s_out, sgu)
    o_list, l_list = [], []
    for dil, (qv, kv, vv) in zip(DILATIONS, views):
        if dil == 1:
            (o, l), ((out_lands,),) = _attn_fwd(qv, kv, vv, dil, comms=[_gather_pass(out_lands)])
        else:
            o, l = _attn_fwd(qv, kv, vv, dil)
        o_list.append(o)
        l_list.append(l)
    w_out_f = with_own(out_lands, "w_out")
    ffn_lands = _split_wait("gather_ffn_wait", g_ffn, *s_ffn, l_list[-1])
    (attn, mixed, y, x1, *lses), (ffn_lands,) = _mix_out_fwd(
        o_list, l_list, sgu, xs, w_out_f, small["attn_out_norm"], small["sgu_out_norm"], small["post_mix_norm"],
        comms=[_gather_pass(ffn_lands)])
    w_gate_f, w_up_f, w_down_f = (with_own(f, n).reshape(FF, D_MODEL) for f, n in zip(ffn_lands, ffn))
    h2, act, dg, dup, df, dx1, loss_cols, d_pre_ffn, d_post_ffn = _ffn_fwd_bwd(
        x1, target, w_gate_f, w_up_f, w_down_f, small["pre_ffn_norm"], small["post_ffn_norm"])

    full_tok = pl.BlockSpec((SEQ, D_MODEL), lambda s: (0, 0), pipeline_mode=pl.Buffered(1))
    gw, gb = {}, {}
    gw["w_gate"], gw["w_up"], gb["w_gate"], gb["w_up"] = _wgrad_ff([dg, dup], h2, "wgrad_gate_up")
    gw["w_down"], gb["w_down"] = _wgrad_ff([act], df, "wgrad_down")
    (dy, du, dvs_sgu, d_post_mix, d_attn_norm, d_sgu_norm, d_w_sp, d_b_sp, d_ln_gain, d_ln_bias,
     *dviews), (sib_ffn,) = _outproj_bwd(
        dx1, y, attn, sgu, w_out_f, small["post_mix_norm"], small["attn_out_norm"], small["sgu_out_norm"],
        u, vs, small["sgu_ln_gain"], small["sgu_ln_bias"], small["sgu_w_spatial"], b_t,
        comms=[_rs_sibling([gb[n] for n in ffn])])
    sib = dict(zip(ffn, sib_ffn))
    part = dict(zip(ffn, _chip_sums([gb[n] for n in ffn], [sib[n] for n in ffn], shard_core)))
    x_ffn = _rs_chips([part[n] for n in ffn])
    packed_early = _pack({
        "sgu_ln_gain": d_ln_gain, "sgu_ln_bias": d_ln_bias, "sgu_b_spatial": d_b_sp,
        "attn_out_norm": d_attn_norm, "sgu_out_norm": d_sgu_norm, "post_mix_norm": d_post_mix,
        "pre_ffn_norm": d_pre_ffn, "post_ffn_norm": d_post_ffn, LOSS_ROW: loss_cols}, SMALL_EARLY, 8)
    wsp_view = (SGU_GROUPS * CHUNK, CHUNK)
    packed_wsp = d_w_sp.reshape(wsp_view).astype(BF16)
    x_small, x_wsp = _small_exchange(packed_early), _small_exchange(packed_wsp)
    (s_ffn,), token = _split_starts("rs_ffn_start", [x_ffn], small["pre_mix_norm"])
    gw["w_out"], gb["w_out"] = _wgrad(mixed, dy, pl.BlockSpec((SEQ, OUT_S), lambda s: (0, s)), full_tok,
                                      (OUT_S, D_MODEL), "wgrad_out", after=[token])

    dqs, dks, dvs = [], [], []
    for i, (dil, (qv, kv, vv)) in enumerate(zip(DILATIONS, views)):
        if dil == 1:
            (dq, dk, dv), ((sib["w_out"],),) = _attn_bwd(qv, kv, vv, dviews[i], dviews[3 + i], lses[i], dil,
                                                        comms=[_rs_sibling([gb["w_out"]])], after=[token])
            (part["w_out"],) = _chip_sums([gb["w_out"]], [sib["w_out"]], shard_core)
            x_out = _rs_chips([part["w_out"]])
            (s_out, s_small, s_wsp), token = _split_starts("rs_out_small_start", [x_out, x_small, x_wsp], token)
        else:
            dq, dk, dv = _attn_bwd(qv, kv, vv, dviews[i], dviews[3 + i], lses[i], dil, after=[token])
        dqs.append(dq)
        dks.append(dk)
        dvs.append(dv)
    half, far, joined = {}, {}, {}
    far.update(zip(ffn, _split_wait("rs_ffn_wait", x_ffn, *s_ffn, dvs[-1])))
    half.update(zip(ffn, _final_sums([gw[n] for n in ffn], [sib[n] for n in ffn], [far[n] for n in ffn],
                                     shard_core)))
    dproj, grad_x, d_pre_mix = _inproj_bwd(dqs, dks, dvs, du, dvs_sgu, pos, xs, dx1, w_in_f, small["pre_mix_norm"])
    packed_late = _pack({"pre_mix_norm": d_pre_mix}, SMALL_LATE, 8)
    (gw["w_in"], gb["w_in"]), (got, (slots_late,)) = _wgrad(
        h, dproj, full_tok, pl.BlockSpec((SEQ, IN_S), lambda s: (0, s)), (D_MODEL, IN_S), "wgrad_in",
        comms=[_rs_join([half[n] for n in ffn]), _small_exchange(packed_late)])
    joined.update(zip(ffn, got))
    ((sib["w_in"],),) = _comm_only("comm_rs_sibling_in", [_rs_sibling([gb["w_in"]])])
    (part["w_in"],) = _chip_sums([gb["w_in"]], [sib["w_in"]], shard_core)
    x_in = _rs_chips([part["w_in"]])
    (s_in,), token = _split_starts("rs_in_start", [x_in], small["pre_mix_norm"])

    grads, deltas, new_m, new_v = {}, {}, {}, {}

    def record(n, outs):
        grads[n], deltas[n], new_m[n], new_v[n] = (
            jnp.swapaxes(o[None], 1, 2) if n in flipped else o[None] for o in outs)

    def update(names, name, after):
        for n, outs in zip(names, _adamw_multi(
                [big(n, n) for n in names], [joined[n].reshape(big(n, n).shape) for n in names],
                [big("m_" + n, n) for n in names], [big("v_" + n, n) for n in names], name, after)):
            record(n, outs)

    update(ffn, "adamw_ffn", [token])
    (far["w_out"],) = _split_wait("rs_out_wait", x_out, *s_out, new_v["w_down"])
    (slots_early,) = _split_wait("small_early_wait", x_small, *s_small, far["w_out"])
    (slots_wsp,) = _split_wait("small_wsp_wait", x_wsp, *s_wsp, slots_early)
    (far["w_in"],) = _split_wait("rs_in_wait", x_in, *s_in, slots_wsp)
    last = ("w_in", "w_out")
    half.update(zip(last, _final_sums([gw[n] for n in last], [sib[n] for n in last], [far[n] for n in last],
                                      shard_core)))
    (got,) = _comm_only("comm_rs_join_in_out", [_rs_join([half[n] for n in last])])
    joined.update(zip(last, got))
    update(last, "adamw_in_out", [])
    a[LOSS_ROW] = a["m_" + LOSS_ROW] = a["v_" + LOSS_ROW] = jnp.zeros((1, D_MODEL), F32)
    outs = _adamw_small(packed_wsp, slots_wsp, *[a[p + SMALL_WSP].reshape(wsp_view) for p in ("", "m_", "v_")], me)
    for dst, buf in zip((grads, deltas, new_m, new_v), outs):
        dst[SMALL_WSP] = buf.reshape(a[SMALL_WSP].shape)
    for names, rows, packed, slots in ((SMALL_EARLY, 8, packed_early, slots_early),
                                       (SMALL_LATE, 8, packed_late, slots_late)):
        outs = _adamw_small(packed, slots, _pack(a, names, rows), _pack({n: a["m_" + n] for n in names}, names, rows),
                            _pack({n: a["v_" + n] for n in names}, names, rows), me)
        for dst, buf in zip((grads, deltas, new_m, new_v), outs):
            dst.update(_unpack(buf, names, {n: a[n].shape for n in names}))
    loss = jnp.sum(grads[LOSS_ROW]) * np.float32(0.5 / D_MODEL)
    return (loss, grad_x[None], *[grads[n] for n in WEIGHTS], *[deltas[n] for n in WEIGHTS],
            *[new_m[n] for n in WEIGHTS], *[new_v[n] for n in WEIGHTS])
```

```python
import numpy as np
import jax
import jax.numpy as jnp
from jax import lax
from jax.experimental import pallas as pl
from jax.experimental.pallas import tpu as pltpu

F32 = jnp.float32
BF16 = jnp.bfloat16

SEQ = 2048
D_MODEL = 1024
HEAD_DIM = 64
ATTN_W = 512
SGU_W = 512
SGU_GROUPS = 8
CHUNK = 128
DILATIONS = (1, 4, 16)
N_SHARD = 4
IN_S = 640
OUT_S = 256
FF_S = 704
PROJ_W = N_SHARD * IN_S
FF = N_SHARD * FF_S
FF_CHUNKS = ((0, 1024), (1024, 2048), (2048, FF))
RMS_EPS = 1e-6
LN_EPS = 1e-5
ROPE_THETA = 500000.0
ATTN_SCALE = 1.0 / np.sqrt(HEAD_DIM)
NEG = -1e30
TM = 512
TM_FFN = 256
VMEM_LIMIT = 56 * 1024 * 1024

ADAM_LR = 0.001
ADAM_B1 = 0.9
ADAM_B2 = 0.999
ADAM_EPS = 1e-08
ADAM_WD = 0.01
ADAM_STEP = 10

MESH = pl.DeviceIdType.MESH
ANY = pl.BlockSpec(memory_space=pl.ANY)


def _dot(a, b):
    return jnp.dot(a, b, preferred_element_type=F32)


def _dot_nt(a, b):
    return lax.dot_general(a, b, (((1,), (1,)), ((), ())), preferred_element_type=F32)


def _dot_tn(a, b):
    return lax.dot_general(a, b, (((0,), (0,)), ((), ())), preferred_element_type=F32)


def _dot_exact(a, b):
    return jnp.dot(a, b, preferred_element_type=F32, precision=lax.Precision.HIGHEST)


def _dot_select(a, sel):
    hi = a.astype(BF16)
    lo = (a - hi.astype(F32)).astype(BF16)
    sel = sel.astype(BF16)
    return _dot(hi, sel) + _dot(lo, sel)


def _rms_stats(x):
    r = lax.rsqrt(jnp.mean(x * x, axis=-1, keepdims=True) + RMS_EPS)
    return x * r, r


def _rms_bwd(xh, r, gain, dy):
    dxh = dy * gain
    dx = r * (dxh - xh * jnp.mean(dxh * xh, axis=-1, keepdims=True))
    return dx, jnp.sum(dy * xh, axis=0, keepdims=True)


_ERF_ALPHA = (-2.72614225801306e-10, 2.77068142495902e-08, -2.10102402082508e-06, -5.69250639462346e-05,
              -7.34990630326855e-04, -2.95459980854025e-03, -1.60960333262415e-02)
_ERF_BETA = (-1.45660718464996e-05, -2.13374055278905e-04, -1.68282697438203e-03, -7.37332916720468e-03,
             -1.42647390514189e-02)


def _erf(x):
    x = jnp.clip(x, -4.0, 4.0)
    x2 = x * x
    p = jnp.full_like(x, _ERF_ALPHA[0])
    for a in _ERF_ALPHA[1:]:
        p = p * x2 + a
    q = jnp.full_like(x, _ERF_BETA[0])
    for b in _ERF_BETA[1:]:
        q = q * x2 + b
    return x * p / q


def _normal_cdf(x):
    return 0.5 * (1.0 + _erf(x * np.float32(1.0 / np.sqrt(2.0))))


def _gelu_grad(x, cdf):
    pdf = jnp.exp(-0.5 * x * x) * np.float32(1.0 / np.sqrt(2.0 * np.pi))
    return cdf + x * pdf


def _sigmoid(x):
    return 1.0 / (1.0 + jnp.exp(-x))


_INV_FREQ = tuple(float(np.float32(ROPE_THETA ** (-2.0 * j / 16.0))) for j in range(8))


def _rot_tables(pos):
    lane = lax.broadcasted_iota(jnp.int32, (1, 128), 1)
    d = lane & 63
    j = d & 7
    inv = jnp.zeros((1, 128), F32)
    for jj in range(8):
        inv = jnp.where(j == jj, _INV_FREQ[jj], inv)
    ang = pos.astype(F32) * inv
    c = jnp.cos(ang)
    s = jnp.sin(ang)
    cos_t = jnp.where(d < 16, c, 1.0)
    sin_a = jnp.where(d < 8, -s, 0.0)
    sin_b = jnp.where((d >= 8) & (d < 16), s, 0.0)
    return tuple(jnp.tile(t, (1, 4)) for t in (cos_t, sin_a, sin_b))


def _rope(x, tabs):
    cos_t, sin_a, sin_b = tabs
    return x * cos_t + pltpu.roll(x, 504, 1) * sin_a + pltpu.roll(x, 8, 1) * sin_b


def _rope_bwd(dy, tabs):
    cos_t, sin_a, sin_b = tabs
    return dy * cos_t + pltpu.roll(dy * sin_a, 8, 1) + pltpu.roll(dy * sin_b, 504, 1)


def _left_half():
    return lax.broadcasted_iota(jnp.int32, (CHUNK, CHUNK), 1) < HEAD_DIM


def _group_ones():
    lane = lax.broadcasted_iota(jnp.int32, (SGU_GROUPS, SGU_W), 1)
    row = lax.broadcasted_iota(jnp.int32, (SGU_GROUPS, SGU_W), 0)
    return ((lane >> 6) == row).astype(F32)


def _masked_spatial(w_ref):
    row = lax.broadcasted_iota(jnp.int32, (CHUNK, CHUNK), 0)
    col = lax.broadcasted_iota(jnp.int32, (CHUNK, CHUNK), 1)
    return [jnp.where(col <= row, w_ref[g], 0.0).astype(BF16) for g in range(SGU_GROUPS)]


def _sgu_core(u, vs, lg, lb, wm, bias_full):
    tm = u.shape[0]
    cdf_u, cdf_vs = _normal_cdf(u), _normal_cdf(vs)
    gu = u * cdf_u
    gv = vs * cdf_vs
    mu = jnp.mean(gv, axis=-1, keepdims=True)
    xc = gv - mu
    rstd = lax.rsqrt(jnp.mean(xc * xc, axis=-1, keepdims=True) + LN_EPS)
    xh = xc * rstd
    vnb = (xh * lg + lb).astype(BF16)
    left = _left_half()
    rows = []
    for c in range(tm // CHUNK):
        pieces = []
        for p in range(4):
            vp = vnb[c * CHUNK:(c + 1) * CHUNK, p * 128:(p + 1) * 128]
            pieces.append(jnp.where(left, _dot(wm[2 * p], vp), _dot(wm[2 * p + 1], vp)))
        rows.append(jnp.concatenate(pieces, axis=1) + bias_full)
    mixed = jnp.concatenate(rows, axis=0)
    return gu, xh, rstd, vnb, mixed, cdf_u, cdf_vs


def _resident(shape):
    n = len(shape)
    return pl.BlockSpec(shape, lambda *_: (0,) * n, pipeline_mode=pl.Buffered(1))


def _rows(ncol, tm=TM):
    return pl.BlockSpec((tm, ncol), lambda i: (i, 0))


def _acc(ncol, nrow=1):
    return pl.BlockSpec((nrow, ncol), lambda i: (0, 0))


HEAD_W = 128


def _view_rows(dil, width=ATTN_W, tm=TM):
    return pl.BlockSpec((tm // dil, dil * width), lambda i: (i, 0))


def _view_shape(dil, dtype, width=ATTN_W):
    return _sds((SEQ // dil, dil * width), dtype)


def _slab_scratch():
    return pltpu.VMEM((4, TM, 128), F32)


def _store_view(val, out_ref, slabs, dil):
    width = val.shape[1]
    for j in range(width // 128):
        slabs[j] = val[:, j * 128:(j + 1) * 128]
    for r in range(dil):
        for j in range(width // 128):
            c0 = r * width + j * 128
            out_ref[:, c0:c0 + 128] = slabs.at[j][pl.ds(r, TM // dil, stride=dil), :].astype(out_ref.dtype)


def _load_view(in_ref, slabs, dil, width=ATTN_W):
    for r in range(dil):
        for j in range(width // 128):
            c0 = r * width + j * 128
            slabs.at[j][pl.ds(r, TM // dil, stride=dil), :] = in_ref[:, c0:c0 + 128].astype(F32)
    return jnp.concatenate([slabs[j] for j in range(width // 128)], axis=1)


def _head_spread():
    m = lax.broadcasted_iota(jnp.int32, (HEAD_W, ATTN_W), 0)
    lane = lax.broadcasted_iota(jnp.int32, (HEAD_W, ATTN_W), 1)
    return (m == 16 * (lane >> 6)).astype(F32)


def _head_sum():
    lane = lax.broadcasted_iota(jnp.int32, (ATTN_W, HEAD_W), 0)
    m = lax.broadcasted_iota(jnp.int32, (ATTN_W, HEAD_W), 1)
    return ((lane >> 6) == (m >> 4)).astype(F32)


def _seq_params():
    return pltpu.CompilerParams(dimension_semantics=("arbitrary",), vmem_limit_bytes=VMEM_LIMIT)


def _sds(shape, dtype):
    return jax.ShapeDtypeStruct(shape, dtype)


class _Comm:
    def __init__(self, args, out_shape, n_sems, start, finish, aliased=False):
        self.args, self.out_shape, self.n_sems = list(args), list(out_shape), n_sems
        self.start, self.finish, self.aliased = start, finish, aliased


def _pcall(body, *, name, grid, in_specs, out_specs, out_shape, args, scratch_shapes=(), comms=(), after=()):
    single = not isinstance(out_shape, (list, tuple))
    out_specs = [out_specs] if single else list(out_specs)
    out_shape = [out_shape] if single else list(out_shape)
    n_in, n_out, n_scr = len(in_specs), len(out_shape), len(scratch_shapes)
    c_args = [a for c in comms for a in c.args]
    c_outs = [o for c in comms for o in c.out_shape]
    aliases, ai, ao = {}, n_in, n_out
    for c in comms:
        if c.aliased:
            aliases.update({ai + k: ao + k for k in range(len(c.args))})
        ai += len(c.args)
        ao += len(c.out_shape)
    sems = [pltpu.SemaphoreType.DMA((c.n_sems,)) for c in comms for _ in range(2)]
    steps = grid[0]

    def wrapped(*refs):
        o0 = n_in + len(c_args) + len(after)
        s0 = o0 + n_out + len(c_outs)
        m_in, m_out, m_sem = refs[n_in:n_in + len(c_args)], refs[o0 + n_out:s0], refs[s0 + n_scr:]

        def each(phase):
            ii = oi = 0
            for k, c in enumerate(comms):
                getattr(c, phase)(m_in[ii:ii + len(c.args)], m_out[oi:oi + len(c.out_shape)],
                                  m_sem[2 * k], m_sem[2 * k + 1])
                ii += len(c.args)
                oi += len(c.out_shape)

        if comms:
            @pl.when(pl.program_id(0) == 0)
            def _():
                each("start")

        body(*refs[:n_in], *refs[o0:o0 + n_out], *refs[s0:s0 + n_scr])

        if comms:
            @pl.when(pl.program_id(0) == steps - 1)
            def _():
                each("finish")

    res = pl.pallas_call(
        wrapped, name=name, grid=grid,
        in_specs=list(in_specs) + [ANY] * (len(c_args) + len(after)), out_specs=out_specs + [ANY] * len(c_outs),
        out_shape=out_shape + c_outs, scratch_shapes=list(scratch_shapes) + sems,
        input_output_aliases=aliases, compiler_params=_seq_params(),
    )(*args, *c_args, *after)
    mine = res[0] if single else list(res[:n_out])
    if not comms:
        return mine
    theirs, oi = [], n_out
    for c in comms:
        theirs.append(list(res[oi:oi + len(c.out_shape)]))
        oi += len(c.out_shape)
    return mine, theirs


def _in_pieces(g):
    lo, hi = 512 * g, 512 * (g + 1)
    return [(s, max(lo, IN_S * s) - IN_S * s, min(hi, IN_S * (s + 1)) - IN_S * s)
            for s in range(N_SHARD) if max(lo, IN_S * s) < min(hi, IN_S * (s + 1))]


def _inproj_fwd(x, pos, g_pre, w_in, lg, lb, w_sp, b_t, comms=()):
    def body(x_ref, pos_ref, g_ref, w_ref, lg_ref, lb_ref, wsp_ref, bt_ref, h_ref, u_ref, vs_ref, sgu_ref, *rest):
        qkv_refs, slabs = rest[:9], rest[9:]
        xh, _ = _rms_stats(x_ref[...])
        h = (xh * g_ref[...]).astype(BF16)
        h_ref[...] = h
        tabs = _rot_tables(pos_ref[...])

        def group(g):
            return jnp.concatenate([_dot(h, w_ref[s, :, a:b]) for s, a, b in _in_pieces(g)], axis=1)

        for t in range(3):
            val = group(t)
            if t < 2:
                val = _rope(val, tabs)
            if t == 0:
                val = val * np.float32(ATTN_SCALE)
            qkv_refs[t][...] = val.astype(BF16)
            for i, dil in enumerate(DILATIONS[1:]):
                _store_view(val, qkv_refs[3 * (i + 1) + t], slabs[t], dil)
        u = group(3)
        vs = group(4)
        u_ref[...] = u
        vs_ref[...] = vs
        bias_full = _dot_exact(bt_ref[...], _group_ones())
        gu, _, _, _, mixed, _, _ = _sgu_core(u, vs, lg_ref[...], lb_ref[...], _masked_spatial(wsp_ref), bias_full)
        sgu_ref[...] = gu * mixed

    return _pcall(
        body, name="inproj_sgu_fwd", grid=(SEQ // TM,),
        in_specs=[_rows(D_MODEL), _rows(1), _resident((1, D_MODEL)), _resident((N_SHARD, D_MODEL, IN_S)),
                  _resident((1, SGU_W)), _resident((1, SGU_W)), _resident((SGU_GROUPS, CHUNK, CHUNK)),
                  _resident((CHUNK, SGU_GROUPS))],
        out_specs=[_rows(D_MODEL), _rows(512), _rows(512), _rows(512)]
        + [_view_rows(dil) for dil in DILATIONS for _ in range(3)],
        out_shape=[_sds((SEQ, D_MODEL), BF16), _sds((SEQ, 512), F32), _sds((SEQ, 512), F32), _sds((SEQ, 512), F32)]
        + [_view_shape(dil, BF16) for dil in DILATIONS for _ in range(3)],
        scratch_shapes=[_slab_scratch() for _ in range(3)],
        args=(x, pos, g_pre, w_in, lg, lb, w_sp, b_t), comms=comms)


def _block_masks():
    row = lax.broadcasted_iota(jnp.int32, (CHUNK, CHUNK), 0)
    col = lax.broadcasted_iota(jnp.int32, (CHUNK, CHUNK), 1)
    return col <= row, col >= row


def _attn_fwd(qv, kv, vv, dil, comms=()):
    seg = SEQ // dil
    nblk = seg // CHUNK
    rps = 4 if nblk == 1 else 1

    def body(q_ref, k_ref, v_ref, o_ref, l_ref):
        left = _left_half()
        m_cur, m_prev = _block_masks()
        zero = jnp.zeros((CHUNK, CHUNK), BF16)
        ones = (jnp.where(left, 1.0, 0.0).astype(BF16), jnp.where(left, 0.0, 1.0).astype(BF16))

        sides = tuple(enumerate((left, ~left)))

        def rows(b):
            if isinstance(b, int):
                return b * CHUNK, max(b - 1, 0) * CHUNK
            return pl.multiple_of(b * CHUNK, CHUNK), pl.multiple_of(jnp.maximum(b - 1, 0) * CHUNK, CHUNK)

        def first(rr, b):
            r0, rp = rows(b)
            prev_ok = m_prev & (b > 0)
            tiles, scores = [], []
            for hp in range(4):
                ls = slice(rr * ATTN_W + hp * 128, rr * ATTN_W + (hp + 1) * 128)
                qp = q_ref[pl.ds(r0, CHUNK), ls]
                kc = k_ref[pl.ds(r0, CHUNK), ls]
                kp = k_ref[pl.ds(rp, CHUNK), ls] if nblk > 1 else None
                tiles.append((ls, v_ref[pl.ds(r0, CHUNK), ls], v_ref[pl.ds(rp, CHUNK), ls] if nblk > 1 else None))
                for _, hm in sides:
                    qh = jnp.where(hm, qp, zero)
                    sc = jnp.where(m_cur, _dot_nt(qh, kc), NEG)
                    sp = jnp.where(prev_ok, _dot_nt(qh, kp), NEG) if nblk > 1 else None
                    scores.append((sc, sp))
            return tiles, scores

        def second(scores):
            probs = []
            for sc, sp in scores:
                if nblk > 1:
                    m = jnp.max(jnp.maximum(sc, sp), axis=-1, keepdims=True)
                    pc = jnp.exp(sc - m)
                    pp = jnp.exp(sp - m)
                    probs.append((m, pc.astype(BF16), pp.astype(BF16), (pc + pp).astype(BF16)))
                else:
                    m = jnp.max(sc, axis=-1, keepdims=True)
                    pc = jnp.exp(sc - m).astype(BF16)
                    probs.append((m, pc, None, pc))
            return probs

        def third(rr, b, tiles, probs):
            r0, _ = rows(b)
            for hp, (ls, vc, vp) in enumerate(tiles):
                acc = jnp.zeros((CHUNK, CHUNK), F32)
                den = jnp.zeros((CHUNK, CHUNK), F32)
                for side, hm in sides:
                    _, pc, pp, psum = probs[2 * hp + side]
                    acc = acc + _dot(pc, jnp.where(hm, vc, zero))
                    if nblk > 1:
                        acc = acc + _dot(pp, jnp.where(hm, vp, zero))
                    den = den + _dot(psum, ones[side])
                o_ref[pl.ds(r0, CHUNK), ls] = (acc / den).astype(o_ref.dtype)
                lse = jnp.where(left, probs[2 * hp][0], probs[2 * hp + 1][0]) + jnp.log(den)
                l_ref[pl.ds(r0, CHUNK), rr * HEAD_W + 32 * hp:rr * HEAD_W + 32 * hp + 32] = lse[:, 48:80]

        def run(units):
            data = [first(rr, b) for rr, b in units]
            probs = [second(scores) for _, scores in data]
            for (rr, b), (tiles, _), pr in zip(units, data, probs):
                third(rr, b, tiles, pr)

        if nblk == 1:
            run([(rr, 0) for rr in range(rps)])
        else:
            def one(b, carry):
                run([(0, b)])
                return carry

            lax.fori_loop(0, nblk, one, 0)

    spec = pl.BlockSpec((seg, rps * ATTN_W), lambda r: (0, r))
    return _pcall(
        body, name=f"attn_fwd_d{dil}", grid=(dil // rps,),
        in_specs=[spec, spec, spec], out_specs=[spec, pl.BlockSpec((seg, rps * HEAD_W), lambda r: (0, r))],
        out_shape=[_sds((seg, dil * ATTN_W), BF16), _sds((seg, dil * HEAD_W), F32)],
        args=(qv, kv, vv), comms=comms)


def _lane_left(nrows):
    return lax.broadcasted_iota(jnp.int32, (nrows, CHUNK), 1) < HEAD_DIM


def _attn_rows(b):
    if isinstance(b, int):
        return b * CHUNK, max(b - 1, 0) * CHUNK
    return pl.multiple_of(b * CHUNK, CHUNK), pl.multiple_of(jnp.maximum(b - 1, 0) * CHUNK, CHUNK)


def _mix_out_fwd(o_list, l_list, sgu, x, w_out, g_attn, g_sgu, g_post, comms=()):
    def body(o1, o2, o3, l1, l2, l3, sgu_ref, x_ref, w_ref, ga_ref, gs_ref, gp_ref,
             attn_ref, mixed_ref, y_ref, x1_ref, lse1_ref, lse2_ref, lse3_ref, slabs_a, slabs_b):
        os = [o1[...], _load_view(o2, slabs_a, DILATIONS[1]), _load_view(o3, slabs_b, DILATIONS[2])]
        ls = [l1[...], _load_view(l2, slabs_a, DILATIONS[1], HEAD_W), _load_view(l3, slabs_b, DILATIONS[2], HEAD_W)]
        m = jnp.maximum(jnp.maximum(ls[0], ls[1]), ls[2])
        es = [jnp.exp(l - m) for l in ls]
        den = es[0] + es[1] + es[2]
        spread = _head_spread()
        attn = sum(_dot_select(e / den, spread) * o for e, o in zip(es, os))
        attn_ref[...] = attn
        lse = m + jnp.log(den)
        lse1_ref[...] = lse
        _store_view(lse, lse2_ref, slabs_a, DILATIONS[1])
        _store_view(lse, lse3_ref, slabs_b, DILATIONS[2])
        ah, _ = _rms_stats(attn)
        sh, _ = _rms_stats(sgu_ref[...])
        mixed = jnp.concatenate([ah * ga_ref[...], sh * gs_ref[...]], axis=1).astype(BF16)
        mixed_ref[...] = mixed
        y = _dot(mixed[:, 0:OUT_S], w_ref[0])
        for s in range(1, N_SHARD):
            y = y + _dot(mixed[:, s * OUT_S:(s + 1) * OUT_S], w_ref[s])
        y_ref[...] = y
        yh, _ = _rms_stats(y)
        x1_ref[...] = x_ref[...] + yh * gp_ref[...]

    return _pcall(
        body, name="mix_out_fwd", grid=(SEQ // TM,),
        in_specs=[_view_rows(dil) for dil in DILATIONS] + [_view_rows(dil, HEAD_W) for dil in DILATIONS]
        + [_rows(512), _rows(D_MODEL), _resident((N_SHARD, OUT_S, D_MODEL)),
           _resident((1, 512)), _resident((1, 512)), _resident((1, D_MODEL))],
        out_specs=[_rows(512), _rows(D_MODEL), _rows(D_MODEL), _rows(D_MODEL)]
        + [_view_rows(dil, HEAD_W) for dil in DILATIONS],
        out_shape=[_sds((SEQ, 512), F32), _sds((SEQ, D_MODEL), BF16), _sds((SEQ, D_MODEL), F32),
                   _sds((SEQ, D_MODEL), F32)] + [_view_shape(dil, F32, HEAD_W) for dil in DILATIONS],
        scratch_shapes=[_slab_scratch(), _slab_scratch()],
        args=(*o_list, *l_list, sgu, x, w_out, g_attn, g_sgu, g_post), comms=comms)


def _ffn_fwd_bwd(x1, target, w_gate, w_up, w_down, g_pre, g_post, comms=()):
    def body(x1_ref, t_ref, wg_ref, wu_ref, wd_ref, gpf_ref, gpo_ref,
             h2_ref, a_ref, dg_ref, dup_ref, df_ref, dx1_ref, loss_ref, dgpf_ref, dgpo_ref, g_scr, up_scr):
        @pl.when(pl.program_id(0) == 0)
        def _():
            loss_ref[...] = jnp.zeros_like(loss_ref)
            dgpf_ref[...] = jnp.zeros_like(dgpf_ref)
            dgpo_ref[...] = jnp.zeros_like(dgpo_ref)

        x1 = x1_ref[...]
        gpf = gpf_ref[...]
        gpo = gpo_ref[...]
        xh, r = _rms_stats(x1)
        h2 = (xh * gpf).astype(BF16)
        h2_ref[...] = h2
        f = jnp.zeros((TM_FFN, D_MODEL), F32)
        for c0, c1 in FF_CHUNKS:
            g = _dot_nt(h2, wg_ref[c0:c1, :])
            up = _dot_nt(h2, wu_ref[c0:c1, :])
            g_scr[:, c0:c1] = g
            up_scr[:, c0:c1] = up
            a = (g * _sigmoid(g) * up).astype(BF16)
            a_ref[:, c0:c1] = a
            f = f + _dot(a, wd_ref[c0:c1, :])
        fh, rf = _rms_stats(f)
        diff = x1 + fh * gpo - t_ref[...]
        loss_ref[...] += jnp.sum(diff * diff, axis=0, keepdims=True)
        dout = diff * np.float32(1.0 / D_MODEL)
        df, dgpo = _rms_bwd(fh, rf, gpo, dout)
        dgpo_ref[...] += dgpo
        dfb = df.astype(BF16)
        df_ref[...] = dfb
        dh2 = jnp.zeros((TM_FFN, D_MODEL), F32)
        for c0, c1 in FF_CHUNKS:
            da = _dot_nt(dfb, wd_ref[c0:c1, :])
            g = g_scr[:, c0:c1]
            up = up_scr[:, c0:c1]
            sg = _sigmoid(g)
            dup = (da * (g * sg)).astype(BF16)
            dg = (da * up * (sg * (1.0 + g * (1.0 - sg)))).astype(BF16)
            dg_ref[:, c0:c1] = dg
            dup_ref[:, c0:c1] = dup
            dh2 = dh2 + _dot(dg, wg_ref[c0:c1, :]) + _dot(dup, wu_ref[c0:c1, :])
        dx, dgpf = _rms_bwd(xh, r, gpf, dh2)
        dgpf_ref[...] += dgpf
        dx1_ref[...] = dout + dx

    return _pcall(
        body, name="ffn_fwd_bwd", grid=(SEQ // TM_FFN,),
        in_specs=[_rows(D_MODEL, TM_FFN), _rows(D_MODEL, TM_FFN), _resident((FF, D_MODEL)),
                  _resident((FF, D_MODEL)), _resident((FF, D_MODEL)),
                  _resident((1, D_MODEL)), _resident((1, D_MODEL))],
        out_specs=[_rows(D_MODEL, TM_FFN), _rows(FF, TM_FFN), _rows(FF, TM_FFN), _rows(FF, TM_FFN),
                   _rows(D_MODEL, TM_FFN), _rows(D_MODEL, TM_FFN), _acc(D_MODEL), _acc(D_MODEL), _acc(D_MODEL)],
        out_shape=[_sds((SEQ, D_MODEL), BF16), _sds((SEQ, FF), BF16), _sds((SEQ, FF), BF16),
                   _sds((SEQ, FF), BF16), _sds((SEQ, D_MODEL), BF16), _sds((SEQ, D_MODEL), F32),
                   _sds((1, D_MODEL), F32), _sds((1, D_MODEL), F32), _sds((1, D_MODEL), F32)],
        scratch_shapes=[pltpu.VMEM((TM_FFN, FF), F32), pltpu.VMEM((TM_FFN, FF), F32)],
        args=(x1, target, w_gate, w_up, w_down, g_pre, g_post), comms=comms)


def _wgrad(a, b, a_spec, b_spec, out_block, name, comms=()):
    def body(a_ref, b_ref, o_ref, ob_ref):
        av = a_ref[0] if len(a_ref.shape) == 3 else a_ref[...]
        bv = b_ref[0] if len(b_ref.shape) == 3 else b_ref[...]
        g = _dot_tn(av, bv)
        o_ref[0] = g
        ob_ref[0] = g.astype(BF16)

    return _pcall(
        body, name=name, grid=(N_SHARD,),
        in_specs=[a_spec, b_spec],
        out_specs=[pl.BlockSpec((1,) + out_block, lambda s: (s, 0, 0))] * 2,
        out_shape=[_sds((N_SHARD,) + out_block, F32), _sds((N_SHARD,) + out_block, BF16)],
        args=(a, b), comms=comms)


def _outproj_bwd(dx1, y, attn, sgu, w_out, g_post, g_attn, g_sgu, u, vs, lg, lb, w_sp, b_t, comms=(), after=()):
    sgu_bwd = _sgu_bwd_body()

    def body(dx1_ref, y_ref, attn_ref, sgu_ref, w_ref, gp_ref, ga_ref, gs_ref, u_ref, vs_ref, lg_ref, lb_ref,
             wsp_ref, bt_ref, dy_ref, du_ref, dvs_ref, dgp_ref, dga_ref, dgs_ref, dw_ref, db_ref, dlg_ref, dlb_ref,
             *rest):
        dattn_refs, delta_refs, (slabs_a, slabs_b, dsgu_ref, dbias_scr) = rest[0:3], rest[3:6], rest[6:]

        @pl.when(pl.program_id(0) == 0)
        def _():
            dgp_ref[...] = jnp.zeros_like(dgp_ref)
            dga_ref[...] = jnp.zeros_like(dga_ref)
            dgs_ref[...] = jnp.zeros_like(dgs_ref)

        yh, ry = _rms_stats(y_ref[...])
        dy, dgp = _rms_bwd(yh, ry, gp_ref[...], dx1_ref[...])
        dgp_ref[...] += dgp
        dyb = dy.astype(BF16)
        dy_ref[...] = dyb
        dmixed = jnp.concatenate([_dot_nt(dyb, w_ref[s]) for s in range(N_SHARD)], axis=1)
        attn = attn_ref[...]
        ah, ra = _rms_stats(attn)
        dattn, dga = _rms_bwd(ah, ra, ga_ref[...], dmixed[:, 0:512])
        dga_ref[...] += dga
        sh, rs = _rms_stats(sgu_ref[...])
        dsgu, dgs = _rms_bwd(sh, rs, gs_ref[...], dmixed[:, 512:1024])
        dgs_ref[...] += dgs
        dsgu_ref[...] = dsgu
        delta = _dot_select(dattn * attn, _head_sum())
        dattn_refs[0][...] = dattn.astype(BF16)
        delta_refs[0][...] = delta
        for i, dil in enumerate(DILATIONS[1:]):
            _store_view(dattn, dattn_refs[i + 1], slabs_a, dil)
            _store_view(delta, delta_refs[i + 1], slabs_b, dil)
        sgu_bwd(u_ref, vs_ref, dsgu_ref, lg_ref, lb_ref, wsp_ref, bt_ref,
                du_ref, dvs_ref, dw_ref, db_ref, dlg_ref, dlb_ref, dbias_scr)

    return _pcall(
        body, name="outproj_sgu_bwd", grid=(SEQ // TM,),
        in_specs=[_rows(D_MODEL), _rows(D_MODEL), _rows(512), _rows(512), _resident((N_SHARD, OUT_S, D_MODEL)),
                  _resident((1, D_MODEL)), _resident((1, 512)), _resident((1, 512)),
                  _rows(SGU_W), _rows(SGU_W), _resident((1, SGU_W)), _resident((1, SGU_W)),
                  _resident((SGU_GROUPS, CHUNK, CHUNK)), _resident((CHUNK, SGU_GROUPS))],
        out_specs=[_rows(D_MODEL), _rows(SGU_W), _rows(SGU_W), _acc(D_MODEL), _acc(512), _acc(512),
                   pl.BlockSpec((SGU_GROUPS, CHUNK, CHUNK), lambda i: (0, 0, 0)), _acc(CHUNK, SGU_GROUPS),
                   _acc(SGU_W), _acc(SGU_W)]
        + [_view_rows(dil) for dil in DILATIONS] + [_view_rows(dil, HEAD_W) for dil in DILATIONS],
        out_shape=[_sds((SEQ, D_MODEL), BF16), _sds((SEQ, SGU_W), BF16), _sds((SEQ, SGU_W), BF16),
                   _sds((1, D_MODEL), F32), _sds((1, 512), F32), _sds((1, 512), F32),
                   _sds((SGU_GROUPS, CHUNK, CHUNK), F32), _sds((SGU_GROUPS, CHUNK), F32),
                   _sds((1, SGU_W), F32), _sds((1, SGU_W), F32)]
        + [_view_shape(dil, BF16) for dil in DILATIONS] + [_view_shape(dil, F32, HEAD_W) for dil in DILATIONS],
        scratch_shapes=[_slab_scratch(), _slab_scratch(), pltpu.VMEM((TM, SGU_W), F32),
                        pltpu.VMEM((CHUNK, SGU_W), F32)],
        args=(dx1, y, attn, sgu, w_out, g_post, g_attn, g_sgu, u, vs, lg, lb, w_sp, b_t), comms=comms, after=after)


def _sgu_bwd_body():
    nsteps = SEQ // TM

    def body(u_ref, vs_ref, ds_ref, lg_ref, lb_ref, w_ref, bt_ref,
             du_ref, dvs_ref, dw_ref, db_ref, dlg_ref, dlb_ref, dbias_scr):
        i = pl.program_id(0)

        @pl.when(i == 0)
        def _():
            dw_ref[...] = jnp.zeros_like(dw_ref)
            dlg_ref[...] = jnp.zeros_like(dlg_ref)
            dlb_ref[...] = jnp.zeros_like(dlb_ref)
            dbias_scr[...] = jnp.zeros_like(dbias_scr)

        wm = _masked_spatial(w_ref)
        ones_g = _group_ones()
        bias_full = _dot_exact(bt_ref[...], ones_g)
        u = u_ref[...]
        vs = vs_ref[...]
        lg = lg_ref[...]
        gu, xh, rstd, vnb, mixed, cdf_u, cdf_vs = _sgu_core(u, vs, lg, lb_ref[...], wm, bias_full)
        dsgu = ds_ref[...]
        du_ref[...] = (dsgu * mixed * _gelu_grad(u, cdf_u)).astype(BF16)
        dmixed = dsgu * gu
        left = _left_half()
        dvn_rows = []
        for c in range(TM // CHUNK):
            rs = slice(c * CHUNK, (c + 1) * CHUNK)
            dm_c = dmixed[rs, :]
            dbias_scr[...] += dm_c
            pieces = []
            for p in range(4):
                ls = slice(p * 128, (p + 1) * 128)
                dmp = dm_c[:, ls]
                vp = vnb[rs, ls]
                dmb = dmp.astype(BF16)
                zero = jnp.zeros_like(dmb)
                dw_ref[2 * p] += _dot_nt(jnp.where(left, dmb, zero), vp)
                dw_ref[2 * p + 1] += _dot_nt(jnp.where(left, zero, dmb), vp)
                pieces.append(jnp.where(left, _dot_tn(wm[2 * p], dmb), _dot_tn(wm[2 * p + 1], dmb)))
            dvn_rows.append(jnp.concatenate(pieces, axis=1))
        dvn = jnp.concatenate(dvn_rows, axis=0)
        dlg_ref[...] += jnp.sum(dvn * xh, axis=0, keepdims=True)
        dlb_ref[...] += jnp.sum(dvn, axis=0, keepdims=True)
        dxh = dvn * lg
        dgv = rstd * (dxh - jnp.mean(dxh, axis=-1, keepdims=True) - xh * jnp.mean(dxh * xh, axis=-1, keepdims=True))
        dvs_ref[...] = (dgv * _gelu_grad(vs, cdf_vs)).astype(BF16)

        @pl.when(i == nsteps - 1)
        def _():
            row = lax.broadcasted_iota(jnp.int32, (CHUNK, CHUNK), 0)
            col = lax.broadcasted_iota(jnp.int32, (CHUNK, CHUNK), 1)
            for g in range(SGU_GROUPS):
                dw_ref[g] = jnp.where(col <= row, dw_ref[g], 0.0)
            db_ref[...] = lax.dot_general(ones_g, dbias_scr[...], (((1,), (1,)), ((), ())),
                                          preferred_element_type=F32, precision=lax.Precision.HIGHEST)

    return body


def _attn_bwd(qv, kv, vv, dov, deltav, lsev, dil, comms=(), after=()):
    seg = SEQ // dil
    nblk = seg // CHUNK
    rps = 4 if nblk == 1 else 1

    def body(q_ref, k_ref, v_ref, do_ref, dl_ref, lse_ref, dq_ref, dk_ref, dv_ref, dk_wait, dv_wait):
        left = _left_half()
        m_cur, m_prev = _block_masks()
        sides = tuple(enumerate((left, ~left)))
        zero = jnp.zeros((CHUNK, CHUNK), BF16)

        def first(rr, b, both):
            r0, rp = _attn_rows(b)
            keys = pl.ds(rp, 2 * CHUNK) if both else pl.ds(r0, CHUNK)
            tiles, heads = [], []
            for hp in range(4):
                ls = slice(rr * ATTN_W + hp * 128, rr * ATTN_W + (hp + 1) * 128)
                qp = q_ref[pl.ds(r0, CHUNK), ls]
                dop = do_ref[pl.ds(r0, CHUNK), ls]
                k2 = k_ref[keys, ls]
                v2 = v_ref[keys, ls]
                tiles.append((ls, k2))
                for _, hm in sides:
                    qh = jnp.where(hm, qp, zero)
                    doh = jnp.where(hm, dop, zero)
                    heads.append((qh, doh, _dot_nt(k2, qh), _dot_nt(v2, doh)))
            return tiles, heads

        def second(rr, b, heads, both):
            r0, _ = _attn_rows(b)
            ok = jnp.concatenate([m_cur, m_prev], axis=0) if both else m_prev
            lanes = slice(rr * HEAD_W, (rr + 1) * HEAD_W)
            lse_t = lse_ref[pl.ds(r0, CHUNK), lanes].T
            dl_t = dl_ref[pl.ds(r0, CHUNK), lanes].T
            out = []
            for i, (_, _, s_t, dp_t) in enumerate(heads):
                p = jnp.exp(jnp.where(ok, s_t - lse_t[16 * i:16 * i + 1, :], NEG))
                out.append((p.astype(BF16), (p * (dp_t - dl_t[16 * i:16 * i + 1, :])).astype(BF16)))
            return out

        def third(rr, b, tiles, heads, probs, both):
            r0, rp = _attn_rows(b)
            nk = 2 * CHUNK if both else CHUNK
            left_k = _lane_left(nk)
            zero_k = jnp.zeros((nk, CHUNK), BF16)
            for hp, (ls, k2) in enumerate(tiles):
                dq = jnp.zeros((CHUNK, CHUNK), F32)
                dk2 = jnp.zeros((nk, CHUNK), F32)
                dv2 = jnp.zeros((nk, CHUNK), F32)
                for side in range(2):
                    qh, doh, _, _ = heads[2 * hp + side]
                    p, ds = probs[2 * hp + side]
                    dq = dq + _dot_tn(ds, jnp.where(left_k if side == 0 else ~left_k, k2, zero_k))
                    dk2 = dk2 + _dot(ds, qh)
                    dv2 = dv2 + _dot(p, doh)
                dq_ref[pl.ds(r0, CHUNK), ls] = dq.astype(dq_ref.dtype)
                if nblk == 1:
                    dk_ref[pl.ds(r0, CHUNK), ls] = dk2.astype(dk_ref.dtype)
                    dv_ref[pl.ds(r0, CHUNK), ls] = dv2.astype(dv_ref.dtype)
                elif both:
                    dk_ref[pl.ds(rp, CHUNK), ls] = (dk_wait[:, ls] + dk2[0:CHUNK]).astype(dk_ref.dtype)
                    dv_ref[pl.ds(rp, CHUNK), ls] = (dv_wait[:, ls] + dv2[0:CHUNK]).astype(dv_ref.dtype)
                    dk_wait[:, ls] = dk2[CHUNK:]
                    dv_wait[:, ls] = dv2[CHUNK:]
                else:
                    dk_wait[:, ls] = dk2
                    dv_wait[:, ls] = dv2

        def run(units, both):
            data = [first(rr, b, both) for rr, b in units]
            probs = [second(rr, b, heads, both) for (rr, b), (_, heads) in zip(units, data)]
            for (rr, b), (tiles, heads), pr in zip(units, data, probs):
                third(rr, b, tiles, heads, pr, both)

        run([(rr, 0) for rr in range(rps)], False)
        if nblk > 1:
            def one(b, carry):
                run([(0, b)], True)
                return carry

            lax.fori_loop(1, nblk, one, 0)
            last = (nblk - 1) * CHUNK
            dk_ref[last:last + CHUNK, :] = dk_wait[...].astype(dk_ref.dtype)
            dv_ref[last:last + CHUNK, :] = dv_wait[...].astype(dv_ref.dtype)

    spec = pl.BlockSpec((seg, rps * ATTN_W), lambda r: (0, r))
    return _pcall(
        body, name=f"attn_bwd_d{dil}", grid=(dil // rps,),
        in_specs=[spec] * 4 + [pl.BlockSpec((seg, rps * HEAD_W), lambda r: (0, r))] * 2, out_specs=[spec] * 3,
        out_shape=[_sds((seg, dil * ATTN_W), BF16)] * 3,
        scratch_shapes=[pltpu.VMEM((CHUNK, ATTN_W), F32), pltpu.VMEM((CHUNK, ATTN_W), F32)],
        args=(qv, kv, vv, dov, deltav, lsev), comms=comms, after=after)


def _inproj_bwd(dqs, dks, dvs, du, dvs_sgu, pos, x, dx1, w_in, g_pre, comms=()):
    def body(dq1, dq2, dq3, dk1, dk2, dk3, dv1, dv2, dv3, du_ref, dvs_ref, pos_ref, x_ref, dx1_ref, w_ref, g_ref,
             dproj_ref, gx_ref, dg_ref, slabs_a, slabs_b):
        @pl.when(pl.program_id(0) == 0)
        def _():
            dg_ref[...] = jnp.zeros_like(dg_ref)

        def total(r1, r2, r3):
            return r1[...] + _load_view(r2, slabs_a, DILATIONS[1]) + _load_view(r3, slabs_b, DILATIONS[2])

        tabs = _rot_tables(pos_ref[...])
        groups = {3: du_ref[...], 4: dvs_ref[...]}
        dh = jnp.zeros((TM, D_MODEL), F32)
        for g in (3, 4, 0, 1, 2):
            if g == 0:
                groups[g] = _rope_bwd(total(dq1, dq2, dq3) * np.float32(ATTN_SCALE), tabs).astype(BF16)
            elif g == 1:
                groups[g] = _rope_bwd(total(dk1, dk2, dk3), tabs).astype(BF16)
            elif g == 2:
                groups[g] = total(dv1, dv2, dv3).astype(BF16)
            dproj_ref[:, 512 * g:512 * (g + 1)] = groups[g]
            off = 0
            for s, a, b in _in_pieces(g):
                dh = dh + _dot_nt(groups[g][:, off:off + b - a], w_ref[s, :, a:b])
                off += b - a
        g = g_ref[...]
        xh, r = _rms_stats(x_ref[...])
        dx, dg = _rms_bwd(xh, r, g, dh)
        dg_ref[...] += dg
        gx_ref[...] = dx1_ref[...] + dx

    return _pcall(
        body, name="inproj_bwd", grid=(SEQ // TM,),
        in_specs=[_view_rows(dil) for dil in DILATIONS] * 3
        + [_rows(512), _rows(512), _rows(1), _rows(D_MODEL), _rows(D_MODEL),
           _resident((N_SHARD, D_MODEL, IN_S)), _resident((1, D_MODEL))],
        out_specs=[_rows(PROJ_W), _rows(D_MODEL), _acc(D_MODEL)],
        out_shape=[_sds((SEQ, PROJ_W), BF16), _sds((SEQ, D_MODEL), F32), _sds((1, D_MODEL), F32)],
        scratch_shapes=[_slab_scratch(), _slab_scratch()],
        args=(*dqs, *dks, *dvs, du, dvs_sgu, pos, x, dx1, w_in, g_pre), comms=comms)


def _coords():
    return lax.axis_index("x"), lax.axis_index("y"), lax.axis_index("c")


def _other_chips(x, y):
    return [(1 - x, y), (x, 1 - y), (1 - x, 1 - y)]


WEIGHTS = ("pre_mix_norm", "w_in", "sgu_ln_gain", "sgu_ln_bias", "sgu_w_spatial", "sgu_b_spatial", "attn_out_norm",
           "sgu_out_norm", "w_out", "post_mix_norm", "pre_ffn_norm", "w_gate", "w_up", "w_down", "post_ffn_norm")
SMALL = ("pre_mix_norm", "post_mix_norm", "pre_ffn_norm", "post_ffn_norm", "sgu_ln_gain", "sgu_ln_bias",
         "attn_out_norm", "sgu_out_norm", "sgu_w_spatial", "sgu_b_spatial")


def _remote(src, dst, send_sem, recv_sem, to):
    return pltpu.make_async_remote_copy(src_ref=src, dst_ref=dst, send_sem=send_sem, recv_sem=recv_sem,
                                        device_id=to, device_id_type=MESH)


def _halves(a):
    *lead, rows, cols = a.shape
    return a.reshape(*lead, 2, rows // 2, cols)


def _gather_ici(shards):
    n = len(shards)

    def desc(ins, outs, ss, rs, j, w, landed):
        x, y, c = _coords()
        cx, cy = _other_chips(x, y)[j]
        shard = 2 * cx + cy if landed else 2 * x + y
        return _remote(ins[w].at[c], outs[w].at[shard, c], ss.at[j * n + w], rs.at[j * n + w], (cx, cy, c))

    def start(ins, outs, ss, rs):
        for j in range(3):
            for w in range(n):
                desc(ins, outs, ss, rs, j, w, False).start()

    def finish(ins, outs, ss, rs):
        for j in range(3):
            for w in range(n):
                desc(ins, outs, ss, rs, j, w, True).wait_recv()
                desc(ins, outs, ss, rs, j, w, False).wait_send()

    return _Comm(shards, [_sds((N_SHARD,) + s.shape, s.dtype) for s in shards], 3 * n, start, finish)


def _gather_pass(fulls):
    n = len(fulls)

    def desc(bufs, ss, rs, j, w, landed):
        x, y, c = _coords()
        cx, cy = _other_chips(x, y)[j]
        shard = 2 * cx + cy
        return _remote(bufs[w].at[shard, c], bufs[w].at[shard, 1 - c if landed else c],
                       ss.at[j * n + w], rs.at[j * n + w], (x, y, 1 - c))

    def start(ins, outs, ss, rs):
        for j in range(3):
            for w in range(n):
                desc(outs, ss, rs, j, w, False).start()

    def finish(ins, outs, ss, rs):
        for j in range(3):
            for w in range(n):
                desc(outs, ss, rs, j, w, True).wait_recv()
                desc(outs, ss, rs, j, w, False).wait_send()

    return _Comm(fulls, [_sds(f.shape, f.dtype) for f in fulls], 3 * n, start, finish, aliased=True)


def _rs_sibling(gws):
    n = len(gws)

    def desc(ins, outs, ss, rs, w):
        x, y, c = _coords()
        return _remote(ins[w].at[:, 1 - c], outs[w], ss.at[w], rs.at[w], (x, y, 1 - c))

    def start(ins, outs, ss, rs):
        for w in range(n):
            desc(ins, outs, ss, rs, w).start()

    def finish(ins, outs, ss, rs):
        for w in range(n):
            desc(ins, outs, ss, rs, w).wait()

    out_shape = [_sds((N_SHARD, g.shape[1] // 2, g.shape[2]), g.dtype) for g in gws]
    return _Comm([_halves(g) for g in gws], out_shape, n, start, finish)


def _rs_chips(pbs):
    n = len(pbs)

    def desc(ins, outs, ss, rs, j, w):
        x, y, c = _coords()
        cx, cy = _other_chips(x, y)[j]
        return _remote(ins[w].at[2 * cx + cy], outs[w].at[j], ss.at[j * n + w], rs.at[j * n + w], (cx, cy, c))

    def start(ins, outs, ss, rs):
        for j in range(3):
            for w in range(n):
                desc(ins, outs, ss, rs, j, w).start()

    def finish(ins, outs, ss, rs):
        for j in range(3):
            for w in range(n):
                desc(ins, outs, ss, rs, j, w).wait()

    return _Comm(pbs, [_sds((3,) + p.shape[1:], p.dtype) for p in pbs], 3 * n, start, finish)


def _rs_join(halves):
    n = len(halves)

    def desc(bufs, ss, rs, w, landed):
        x, y, c = _coords()
        return _remote(bufs[w].at[c], bufs[w].at[1 - c if landed else c], ss.at[w], rs.at[w], (x, y, 1 - c))

    def start(ins, outs, ss, rs):
        for w in range(n):
            desc(outs, ss, rs, w, False).start()

    def finish(ins, outs, ss, rs):
        for w in range(n):
            desc(outs, ss, rs, w, True).wait_recv()
            desc(outs, ss, rs, w, False).wait_send()

    return _Comm(halves, [_sds(h.shape, h.dtype) for h in halves], n, start, finish, aliased=True)


def _small_exchange(buf):
    def desc(ins, outs, ss, rs, k, landed):
        x, y, c = _coords()
        px = 1 - x if (k >> 2) & 1 else x
        py = 1 - y if (k >> 1) & 1 else y
        pc = 1 - c if k & 1 else c
        slot = 4 * px + 2 * py + pc if landed else 4 * x + 2 * y + c
        return _remote(ins[0], outs[0].at[slot], ss.at[k - 1], rs.at[k - 1], (px, py, pc))

    def start(ins, outs, ss, rs):
        for k in range(1, 8):
            desc(ins, outs, ss, rs, k, False).start()

    def finish(ins, outs, ss, rs):
        for k in range(1, 8):
            desc(ins, outs, ss, rs, k, True).wait_recv()
            desc(ins, outs, ss, rs, k, False).wait_send()

    return _Comm([buf], [_sds((8,) + buf.shape, buf.dtype)], 7, start, finish)


HBM = pl.BlockSpec(memory_space=pltpu.HBM)
SEM = pl.BlockSpec(memory_space=pltpu.SEMAPHORE)
DATAFLOW = pltpu.SideEffectType.DATAFLOW_SIDE_EFFECTING


def _split_starts(name, comms, after):
    srcs = [[pltpu.with_memory_space_constraint(s, pltpu.HBM) for s in c.args] for c in comms]
    lands = [[pltpu.with_memory_space_constraint(lax.empty(o.shape, o.dtype), pltpu.HBM) for o in c.out_shape]
             for c in comms]
    bufs = [b for k in range(len(comms)) for b in srcs[k] + lands[k]]
    nb, nc = len(bufs), len(comms)

    def body(*refs):
        sems = refs[nb + 1:nb + 1 + 2 * nc]
        off = 0
        for k, c in enumerate(comms):
            ns, nl = len(srcs[k]), len(lands[k])
            c.start(refs[off:off + ns], refs[off + ns:off + ns + nl], sems[2 * k], sems[2 * k + 1])
            off += ns + nl
        refs[-1][...] = jnp.zeros_like(refs[-1])

    res = pl.pallas_call(
        body, name=name,
        out_shape=(*[pltpu.SemaphoreType.DMA((c.n_sems,)) for c in comms for _ in range(2)],
                   *[pltpu.HBM(b.shape, b.dtype) for b in bufs], _sds((8, 128), F32)),
        in_specs=[HBM] * nb + [ANY],
        out_specs=(*[SEM] * (2 * nc), *[HBM] * nb, pl.BlockSpec(memory_space=pltpu.VMEM)),
        input_output_aliases={i: 2 * nc + i for i in range(nb)},
        compiler_params=pltpu.CompilerParams(has_side_effects=DATAFLOW),
    )(*bufs, after)
    states, off = [], 2 * nc
    for k in range(nc):
        ns, nl = len(srcs[k]), len(lands[k])
        states.append((res[2 * k], res[2 * k + 1], list(res[off:off + ns]), list(res[off + ns:off + ns + nl])))
        off += ns + nl
    return states, res[-1]


def _split_wait(name, comm, send_sems, recv_sems, srcs, lands, after):
    ns, nb = len(srcs), len(lands)
    after = list(after) if isinstance(after, (list, tuple)) else [after]

    def body(*refs):
        comm.finish(refs[:ns], refs[ns:ns + nb], refs[ns + nb], refs[ns + nb + 1])

    res = pl.pallas_call(
        body, name=name,
        out_shape=tuple(pltpu.HBM(b.shape, b.dtype) for b in srcs + lands),
        in_specs=[HBM] * (ns + nb) + [SEM, SEM] + [ANY] * len(after), out_specs=tuple([HBM] * (ns + nb)),
        input_output_aliases={i: i for i in range(ns + nb)},
        compiler_params=pltpu.CompilerParams(has_side_effects=DATAFLOW),
    )(*srcs, *lands, send_sems, recv_sems, *after)
    return list(res[ns:])


LOSS_ROW = "loss_cols"
SMALL_EARLY = ("post_mix_norm", "pre_ffn_norm", "post_ffn_norm", "sgu_ln_gain", "sgu_ln_bias", "attn_out_norm",
               "sgu_out_norm", "sgu_b_spatial", LOSS_ROW)
SMALL_LATE = ("pre_mix_norm",)
SMALL_WSP = "sgu_w_spatial"


def _pack(d, names, rows):
    flat = [d[n].reshape(-1) for n in names]
    used = sum(f.shape[0] for f in flat)
    flat.append(jnp.zeros((rows * 1024 - used,), F32))
    return jnp.concatenate(flat).reshape(rows, 1024)


def _unpack(buf, names, shapes):
    flat = buf.reshape(-1)
    out, off = {}, 0
    for n in names:
        size = int(np.prod(shapes[n]))
        out[n] = flat[off:off + size].reshape(shapes[n])
        off += size
    return out


def _chip_sums(gbs, recvs, shard_core):
    n = len(gbs)
    _, rows, cols = gbs[0].shape
    hw = rows // 2

    def body(sc_ref, *refs):
        for k in range(n):
            refs[2 * n + k][...] = (refs[k][...].astype(F32) + refs[n + k][...].astype(F32)).astype(BF16)

    def other(s, sc):
        return jnp.where(s >= sc[0], s + 1, s)

    mine = pl.BlockSpec((1, hw, cols), lambda s, sc: (other(s, sc), sc[1], 0))
    plain = pl.BlockSpec((1, hw, cols), lambda s, sc: (other(s, sc), 0, 0))
    return pl.pallas_call(
        body, name="rs_chip_sums",
        grid_spec=pltpu.PrefetchScalarGridSpec(num_scalar_prefetch=1, grid=(N_SHARD - 1,),
                                               in_specs=[mine] * n + [plain] * n, out_specs=[plain] * n),
        out_shape=[_sds((N_SHARD, hw, cols), BF16)] * n,
        compiler_params=_seq_params(),
    )(shard_core, *gbs, *recvs)


def _final_sums(gws, recv_sibs, recv_chips, shard_core):
    n = len(gws)
    halves = [(g.shape[1] // 2, g.shape[2]) for g in gws]

    def body(sc_ref, *refs):
        for k in range(n):
            acc = refs[k][0] + refs[n + k][0]
            for j in range(3):
                acc = acc + refs[2 * n + k][j].astype(F32)
            refs[3 * n + k][0] = acc

    def specs(lead, index):
        return [pl.BlockSpec((lead, hw, cols), index) for hw, cols in halves]

    return pl.pallas_call(
        body, name="rs_final_sums",
        grid_spec=pltpu.PrefetchScalarGridSpec(
            num_scalar_prefetch=1, grid=(1,),
            in_specs=specs(1, lambda i, sc: (sc[0], sc[1], 0)) + specs(1, lambda i, sc: (sc[0], 0, 0))
            + specs(3, lambda i, sc: (0, 0, 0)),
            out_specs=specs(1, lambda i, sc: (sc[1], 0, 0))),
        out_shape=[_sds((2, hw, cols), F32) for hw, cols in halves],
        compiler_params=_seq_params(),
    )(shard_core, *gws, *recv_sibs, *recv_chips)


ADAM_BLOCKS = 4


def _adamw_multi(ws, gs, ms, vs, name, after=()):
    n = len(ws)

    def body(*refs):
        outs = refs[4 * n + len(after):]
        for k in range(n):
            g = refs[n + k][...]
            d, m, v = _adam_math(refs[k][...], g, refs[2 * n + k][...], refs[3 * n + k][...])
            outs[4 * k][...], outs[4 * k + 1][...], outs[4 * k + 2][...], outs[4 * k + 3][...] = g, d, m, v

    specs = [pl.BlockSpec((w.shape[0] // ADAM_BLOCKS, w.shape[1]), lambda i: (i, 0)) for w in ws]
    res = pl.pallas_call(
        body, name=name, grid=(ADAM_BLOCKS,), in_specs=specs * 4 + [ANY] * len(after),
        out_specs=[s for s in specs for _ in range(4)],
        out_shape=[_sds(w.shape, F32) for w in ws for _ in range(4)],
        compiler_params=_seq_params(),
    )(*ws, *gs, *ms, *vs, *after)
    return [tuple(res[4 * k:4 * k + 4]) for k in range(n)]


def _wgrad_ff(a_list, b, name):
    n = len(a_list)
    cols = 256

    def body(*refs):
        bv = refs[n][...]
        for k in range(n):
            g = _dot_tn(refs[k][...], bv)
            refs[n + 1 + k][...] = g
            refs[2 * n + 1 + k][...] = g.astype(BF16)

    res = _pcall(
        body, name=name, grid=(FF // cols,),
        in_specs=[pl.BlockSpec((SEQ, cols), lambda j: (0, j))] * n
        + [pl.BlockSpec((SEQ, D_MODEL), lambda j: (0, 0), pipeline_mode=pl.Buffered(1))],
        out_specs=[pl.BlockSpec((cols, D_MODEL), lambda j: (j, 0))] * (2 * n),
        out_shape=[_sds((FF, D_MODEL), F32)] * n + [_sds((FF, D_MODEL), BF16)] * n, args=(*a_list, b))
    return [m.reshape(N_SHARD, FF_S, D_MODEL) for m in res]


def _comm_only(name, comms):
    return _pcall(lambda: None, name=name, grid=(1,), in_specs=[], out_specs=[], out_shape=[], args=(),
                  comms=comms)[1]


def _adam_math(w, g, m, v):
    m = ADAM_B1 * m + (1.0 - ADAM_B1) * g
    v = ADAM_B2 * v + (1.0 - ADAM_B2) * (g * g)
    m_hat = m / (1.0 - ADAM_B1 ** ADAM_STEP)
    v_hat = v / (1.0 - ADAM_B2 ** ADAM_STEP)
    return -ADAM_LR * (m_hat / (jnp.sqrt(v_hat) + ADAM_EPS) + ADAM_WD * w), m, v


def _adamw_small(own, slots, w, m, v, me):
    rows, cols = own.shape

    def body(me_ref, own_ref, slots_ref, w_ref, m_ref, v_ref, g_ref, d_ref, nm_ref, nv_ref):
        own_v = own_ref[...].astype(F32)
        g = jnp.where(me_ref[0] == 0, own_v, slots_ref[0].astype(F32))
        for i in range(1, 8):
            g = g + jnp.where(me_ref[0] == i, own_v, slots_ref[i].astype(F32))
        g_ref[...] = g
        d_ref[...], nm_ref[...], nv_ref[...] = _adam_math(w_ref[...], g, m_ref[...], v_ref[...])

    flat = pl.BlockSpec((rows, cols), lambda i, me_ref: (0, 0))
    return pl.pallas_call(
        body, name="adamw_small",
        grid_spec=pltpu.PrefetchScalarGridSpec(
            num_scalar_prefetch=1, grid=(1,),
            in_specs=[flat, pl.BlockSpec((8, rows, cols), lambda i, me_ref: (0, 0, 0)), flat, flat, flat],
            out_specs=[flat] * 4),
        out_shape=[_sds((rows, cols), F32)] * 4,
        compiler_params=_seq_params(),
    )(me, own, slots, w, m, v)


def kernel(x, positions, pre_mix_norm, w_in, sgu_ln_gain, sgu_ln_bias, sgu_w_spatial, sgu_b_spatial, attn_out_norm, sgu_out_norm, w_out, post_mix_norm, pre_ffn_norm, w_gate, w_up, w_down, post_ffn_norm, loss_target, m_pre_mix_norm, m_w_in, m_sgu_ln_gain, m_sgu_ln_bias, m_sgu_w_spatial, m_sgu_b_spatial, m_attn_out_norm, m_sgu_out_norm, m_w_out, m_post_mix_norm, m_pre_ffn_norm, m_w_gate, m_w_up, m_w_down, m_post_ffn_norm, v_pre_mix_norm, v_w_in, v_sgu_ln_gain, v_sgu_ln_bias, v_sgu_w_spatial, v_sgu_b_spatial, v_attn_out_norm, v_sgu_out_norm, v_w_out, v_post_mix_norm, v_pre_ffn_norm, v_w_gate, v_w_up, v_w_down, v_post_ffn_norm):
    a = dict(locals())
    cx, cy, cc = _coords()
    s_me = 2 * cx + cy
    shard_core = jnp.stack([s_me, cc]).astype(jnp.int32)
    me = jnp.stack([4 * cx + 2 * cy + cc]).astype(jnp.int32)
    small = {n: (a[n][0] if a[n].ndim > 2 else a[n]) for n in SMALL}
    b_t = small["sgu_b_spatial"].T
    xs, pos, target = x[0], positions.reshape(SEQ, 1), loss_target[0]
    flipped = ("w_gate", "w_up")

    def big(name, n):
        return jnp.swapaxes(a[name], 1, 2)[0] if n in flipped else a[name][0]

    own = {"w_in": _halves(big("w_in", "w_in").astype(BF16))}

    def with_own(full, n):
        full = lax.dynamic_update_slice(full, own[n][None], (s_me, 0, 0, 0))
        return full.reshape((N_SHARD,) + big(n, n).shape)

    ffn = ("w_gate", "w_up", "w_down")
    g_in = _gather_ici([own["w_in"]])
    (s_in,), token = _split_starts("gather_in_start", [g_in], small["pre_mix_norm"])
    for n in ("w_out",) + ffn:
        own[n] = _halves((big(n, n) + token[0:1, 0:1]).astype(BF16))
    g_out, g_ffn = _gather_ici([own["w_out"]]), _gather_ici([own[n] for n in ffn])
    (s_out, s_ffn), token = _split_starts("gather_rest_start", [g_out, g_ffn], token)
    in_lands = _split_wait("gather_in_wait", g_in, *s_in, token)
    ((in_lands,),) = _comm_only("comm_pass_in", [_gather_pass(in_lands)])
    w_in_f = with_own(in_lands, "w_in")
    h, u, vs, sgu, *qkv = _inproj_fwd(xs, pos, small["pre_mix_norm"], w_in_f, small["sgu_ln_gain"],
                                      small["sgu_ln_bias"], small["sgu_w_spatial"], b_t)
    views = [tuple(qkv[3 * i:3 * i + 3]) for i in range(len(DILATIONS))]
    out_lands = _split_wait("gather_out_wait", g_out, *s_out, sgu)
    o_list, l_list = [], []
    for dil, (qv, kv, vv) in zip(DILATIONS, views):
        if dil == 1:
            (o, l), ((out_lands,),) = _attn_fwd(qv, kv, vv, dil, comms=[_gather_pass(out_lands)])
        else:
            o, l = _attn_fwd(qv, kv, vv, dil)
        o_list.append(o)
        l_list.append(l)
    w_out_f = with_own(out_lands, "w_out")
    ffn_lands = _split_wait("gather_ffn_wait", g_ffn, *s_ffn, l_list[-1])
    (attn, mixed, y, x1, *lses), (ffn_lands,) = _mix_out_fwd(
        o_list, l_list, sgu, xs, w_out_f, small["attn_out_norm"], small["sgu_out_norm"], small["post_mix_norm"],
        comms=[_gather_pass(ffn_lands)])
    w_gate_f, w_up_f, w_down_f = (with_own(f, n).reshape(FF, D_MODEL) for f, n in zip(ffn_lands, ffn))
    h2, act, dg, dup, df, dx1, loss_cols, d_pre_ffn, d_post_ffn = _ffn_fwd_bwd(
        x1, target, w_gate_f, w_up_f, w_down_f, small["pre_ffn_norm"], small["post_ffn_norm"])

    full_tok = pl.BlockSpec((SEQ, D_MODEL), lambda s: (0, 0), pipeline_mode=pl.Buffered(1))
    gw, gb = {}, {}
    gw["w_gate"], gw["w_up"], gb["w_gate"], gb["w_up"] = _wgrad_ff([dg, dup], h2, "wgrad_gate_up")
    gw["w_down"], gb["w_down"] = _wgrad_ff([act], df, "wgrad_down")
    (dy, du, dvs_sgu, d_post_mix, d_attn_norm, d_sgu_norm, d_w_sp, d_b_sp, d_ln_gain, d_ln_bias,
     *dviews), (sib_ffn,) = _outproj_bwd(
        dx1, y, attn, sgu, w_out_f, small["post_mix_norm"], small["attn_out_norm"], small["sgu_out_norm"],
        u, vs, small["sgu_ln_gain"], small["sgu_ln_bias"], small["sgu_w_spatial"], b_t,
        comms=[_rs_sibling([gb[n] for n in ffn])])
    sib = dict(zip(ffn, sib_ffn))
    part = dict(zip(ffn, _chip_sums([gb[n] for n in ffn], [sib[n] for n in ffn], shard_core)))
    gw["w_out"], gb["w_out"] = _wgrad(mixed, dy, pl.BlockSpec((SEQ, OUT_S), lambda s: (0, s)), full_tok,
                                      (OUT_S, D_MODEL), "wgrad_out")
    x_ffn = _rs_chips([part[n] for n in ffn])
    packed_early = _pack({
        "sgu_ln_gain": d_ln_gain, "sgu_ln_bias": d_ln_bias, "sgu_b_spatial": d_b_sp,
        "attn_out_norm": d_attn_norm, "sgu_out_norm": d_sgu_norm, "post_mix_norm": d_post_mix,
        "pre_ffn_norm": d_pre_ffn, "post_ffn_norm": d_post_ffn, LOSS_ROW: loss_cols}, SMALL_EARLY, 8)
    wsp_view = (SGU_GROUPS * CHUNK, CHUNK)
    packed_wsp = d_w_sp.reshape(wsp_view).astype(BF16)
    x_small, x_wsp = _small_exchange(packed_early), _small_exchange(packed_wsp)
    (s_ffn,), token = _split_starts("rs_ffn_start", [x_ffn], small["pre_mix_norm"])

    dqs, dks, dvs = [], [], []
    for i, (dil, (qv, kv, vv)) in enumerate(zip(DILATIONS, views)):
        if dil == 1:
            (dq, dk, dv), ((sib["w_out"],),) = _attn_bwd(qv, kv, vv, dviews[i], dviews[3 + i], lses[i], dil,
                                                        comms=[_rs_sibling([gb["w_out"]])], after=[token])
            (part["w_out"],) = _chip_sums([gb["w_out"]], [sib["w_out"]], shard_core)
            x_out = _rs_chips([part["w_out"]])
            (s_out, s_small, s_wsp), token = _split_starts("rs_out_small_start", [x_out, x_small, x_wsp], token)
        else:
            dq, dk, dv = _attn_bwd(qv, kv, vv, dviews[i], dviews[3 + i], lses[i], dil, after=[token])
        dqs.append(dq)
        dks.append(dk)
        dvs.append(dv)
    half, far, joined = {}, {}, {}
    far.update(zip(ffn, _split_wait("rs_ffn_wait", x_ffn, *s_ffn, dvs[-1])))
    half.update(zip(ffn, _final_sums([gw[n] for n in ffn], [sib[n] for n in ffn], [far[n] for n in ffn],
                                     shard_core)))
    dproj, grad_x, d_pre_mix = _inproj_bwd(dqs, dks, dvs, du, dvs_sgu, pos, xs, dx1, w_in_f, small["pre_mix_norm"])
    packed_late = _pack({"pre_mix_norm": d_pre_mix}, SMALL_LATE, 1)
    (gw["w_in"], gb["w_in"]), (got, (slots_late,)) = _wgrad(
        h, dproj, full_tok, pl.BlockSpec((SEQ, IN_S), lambda s: (0, s)), (D_MODEL, IN_S), "wgrad_in",
        comms=[_rs_join([half[n] for n in ffn]), _small_exchange(packed_late)])
    joined.update(zip(ffn, got))
    ((sib["w_in"],),) = _comm_only("comm_rs_sibling_in", [_rs_sibling([gb["w_in"]])])
    (part["w_in"],) = _chip_sums([gb["w_in"]], [sib["w_in"]], shard_core)
    x_in = _rs_chips([part["w_in"]])
    (s_in,), token = _split_starts("rs_in_start", [x_in], small["pre_mix_norm"])

    grads, deltas, new_m, new_v = {}, {}, {}, {}

    def record(n, outs):
        grads[n], deltas[n], new_m[n], new_v[n] = (
            jnp.swapaxes(o[None], 1, 2) if n in flipped else o[None] for o in outs)

    def update(names, name, after):
        for n, outs in zip(names, _adamw_multi(
                [big(n, n) for n in names], [joined[n].reshape(big(n, n).shape) for n in names],
                [big("m_" + n, n) for n in names], [big("v_" + n, n) for n in names], name, after)):
            record(n, outs)

    update(ffn, "adamw_ffn", [token])
    (far["w_out"],) = _split_wait("rs_out_wait", x_out, *s_out, new_v["w_down"])
    (slots_early,) = _split_wait("small_early_wait", x_small, *s_small, far["w_out"])
    (slots_wsp,) = _split_wait("small_wsp_wait", x_wsp, *s_wsp, slots_early)
    (far["w_in"],) = _split_wait("rs_in_wait", x_in, *s_in, slots_wsp)
    last = ("w_in", "w_out")
    half.update(zip(last, _final_sums([gw[n] for n in last], [sib[n] for n in last], [far[n] for n in last],
                                      shard_core)))
    (got,) = _comm_only("comm_rs_join_in_out", [_rs_join([half[n] for n in last])])
    joined.update(zip(last, got))
    update(last, "adamw_in_out", [])
    a[LOSS_ROW] = a["m_" + LOSS_ROW] = a["v_" + LOSS_ROW] = jnp.zeros((1, D_MODEL), F32)
    outs = _adamw_small(packed_wsp, slots_wsp, *[a[p + SMALL_WSP].reshape(wsp_view) for p in ("", "m_", "v_")], me)
    for dst, buf in zip((grads, deltas, new_m, new_v), outs):
        dst[SMALL_WSP] = buf.reshape(a[SMALL_WSP].shape)
    for names, rows, packed, slots in ((SMALL_EARLY, 8, packed_early, slots_early),
                                       (SMALL_LATE, 1, packed_late, slots_late)):
        outs = _adamw_small(packed, slots, _pack(a, names, rows), _pack({n: a["m_" + n] for n in names}, names, rows),
                            _pack({n: a["v_" + n] for n in names}, names, rows), me)
        for dst, buf in zip((grads, deltas, new_m, new_v), outs):
            dst.update(_unpack(buf, names, {n: a[n].shape for n in names}))
    loss = jnp.sum(grads[LOSS_ROW]) * np.float32(0.5 / D_MODEL)
    return (loss, grad_x[None], *[grads[n] for n in WEIGHTS], *[deltas[n] for n in WEIGHTS],
            *[new_m[n] for n in WEIGHTS], *[new_v[n] for n in WEIGHTS])
```

```python
import numpy as np
import jax
import jax.numpy as jnp
from jax import lax
from jax.experimental import pallas as pl
from jax.experimental.pallas import tpu as pltpu

F32 = jnp.float32
BF16 = jnp.bfloat16

SEQ = 2048
D_MODEL = 1024
HEAD_DIM = 64
ATTN_W = 512
SGU_W = 512
SGU_GROUPS = 8
CHUNK = 128
DILATIONS = (1, 4, 16)
N_SHARD = 4
IN_S = 640
OUT_S = 256
FF_S = 704
PROJ_W = N_SHARD * IN_S
FF = N_SHARD * FF_S
FF_CHUNKS = ((0, 1024), (1024, 2048), (2048, FF))
RMS_EPS = 1e-6
LN_EPS = 1e-5
ROPE_THETA = 500000.0
ATTN_SCALE = 1.0 / np.sqrt(HEAD_DIM)
NEG = -1e30
TM = 512
TM_FFN = 256
VMEM_LIMIT = 56 * 1024 * 1024

ADAM_LR = 0.001
ADAM_B1 = 0.9
ADAM_B2 = 0.999
ADAM_EPS = 1e-08
ADAM_WD = 0.01
ADAM_STEP = 10

MESH = pl.DeviceIdType.MESH
ANY = pl.BlockSpec(memory_space=pl.ANY)


def _dot(a, b):
    return jnp.dot(a, b, preferred_element_type=F32)


def _dot_nt(a, b):
    return lax.dot_general(a, b, (((1,), (1,)), ((), ())), preferred_element_type=F32)


def _dot_tn(a, b):
    return lax.dot_general(a, b, (((0,), (0,)), ((), ())), preferred_element_type=F32)


def _dot_exact(a, b):
    return jnp.dot(a, b, preferred_element_type=F32, precision=lax.Precision.HIGHEST)


def _dot_select(a, sel):
    hi = a.astype(BF16)
    lo = (a - hi.astype(F32)).astype(BF16)
    sel = sel.astype(BF16)
    return _dot(hi, sel) + _dot(lo, sel)


def _rms_stats(x):
    r = lax.rsqrt(jnp.mean(x * x, axis=-1, keepdims=True) + RMS_EPS)
    return x * r, r


def _rms_bwd(xh, r, gain, dy):
    dxh = dy * gain
    dx = r * (dxh - xh * jnp.mean(dxh * xh, axis=-1, keepdims=True))
    return dx, jnp.sum(dy * xh, axis=0, keepdims=True)


_ERF_ALPHA = (-2.72614225801306e-10, 2.77068142495902e-08, -2.10102402082508e-06, -5.69250639462346e-05,
              -7.34990630326855e-04, -2.95459980854025e-03, -1.60960333262415e-02)
_ERF_BETA = (-1.45660718464996e-05, -2.13374055278905e-04, -1.68282697438203e-03, -7.37332916720468e-03,
             -1.42647390514189e-02)


def _erf(x):
    x = jnp.clip(x, -4.0, 4.0)
    x2 = x * x
    p = jnp.full_like(x, _ERF_ALPHA[0])
    for a in _ERF_ALPHA[1:]:
        p = p * x2 + a
    q = jnp.full_like(x, _ERF_BETA[0])
    for b in _ERF_BETA[1:]:
        q = q * x2 + b
    return x * p / q


def _normal_cdf(x):
    return 0.5 * (1.0 + _erf(x * np.float32(1.0 / np.sqrt(2.0))))


def _gelu_grad(x, cdf):
    pdf = jnp.exp(-0.5 * x * x) * np.float32(1.0 / np.sqrt(2.0 * np.pi))
    return cdf + x * pdf


def _sigmoid(x):
    return 1.0 / (1.0 + jnp.exp(-x))


_INV_FREQ = tuple(float(np.float32(ROPE_THETA ** (-2.0 * j / 16.0))) for j in range(8))


def _rot_tables(pos):
    lane = lax.broadcasted_iota(jnp.int32, (1, 128), 1)
    d = lane & 63
    j = d & 7
    inv = jnp.zeros((1, 128), F32)
    for jj in range(8):
        inv = jnp.where(j == jj, _INV_FREQ[jj], inv)
    ang = pos.astype(F32) * inv
    c = jnp.cos(ang)
    s = jnp.sin(ang)
    cos_t = jnp.where(d < 16, c, 1.0)
    sin_a = jnp.where(d < 8, -s, 0.0)
    sin_b = jnp.where((d >= 8) & (d < 16), s, 0.0)
    return tuple(jnp.tile(t, (1, 4)) for t in (cos_t, sin_a, sin_b))


def _rope(x, tabs):
    cos_t, sin_a, sin_b = tabs
    return x * cos_t + pltpu.roll(x, 504, 1) * sin_a + pltpu.roll(x, 8, 1) * sin_b


def _rope_bwd(dy, tabs):
    cos_t, sin_a, sin_b = tabs
    return dy * cos_t + pltpu.roll(dy * sin_a, 8, 1) + pltpu.roll(dy * sin_b, 504, 1)


def _left_half():
    return lax.broadcasted_iota(jnp.int32, (CHUNK, CHUNK), 1) < HEAD_DIM


def _group_ones():
    lane = lax.broadcasted_iota(jnp.int32, (SGU_GROUPS, SGU_W), 1)
    row = lax.broadcasted_iota(jnp.int32, (SGU_GROUPS, SGU_W), 0)
    return ((lane >> 6) == row).astype(F32)


def _masked_spatial(w_ref):
    row = lax.broadcasted_iota(jnp.int32, (CHUNK, CHUNK), 0)
    col = lax.broadcasted_iota(jnp.int32, (CHUNK, CHUNK), 1)
    return [jnp.where(col <= row, w_ref[g], 0.0).astype(BF16) for g in range(SGU_GROUPS)]


def _sgu_core(u, vs, lg, lb, wm, bias_full):
    tm = u.shape[0]
    cdf_u, cdf_vs = _normal_cdf(u), _normal_cdf(vs)
    gu = u * cdf_u
    gv = vs * cdf_vs
    mu = jnp.mean(gv, axis=-1, keepdims=True)
    xc = gv - mu
    rstd = lax.rsqrt(jnp.mean(xc * xc, axis=-1, keepdims=True) + LN_EPS)
    xh = xc * rstd
    vnb = (xh * lg + lb).astype(BF16)
    left = _left_half()
    rows = []
    for c in range(tm // CHUNK):
        pieces = []
        for p in range(4):
            vp = vnb[c * CHUNK:(c + 1) * CHUNK, p * 128:(p + 1) * 128]
            pieces.append(jnp.where(left, _dot(wm[2 * p], vp), _dot(wm[2 * p + 1], vp)))
        rows.append(jnp.concatenate(pieces, axis=1) + bias_full)
    mixed = jnp.concatenate(rows, axis=0)
    return gu, xh, rstd, vnb, mixed, cdf_u, cdf_vs


def _resident(shape):
    n = len(shape)
    return pl.BlockSpec(shape, lambda *_: (0,) * n, pipeline_mode=pl.Buffered(1))


def _rows(ncol, tm=TM):
    return pl.BlockSpec((tm, ncol), lambda i: (i, 0))


def _acc(ncol, nrow=1):
    return pl.BlockSpec((nrow, ncol), lambda i: (0, 0))


HEAD_W = 128


def _view_rows(dil, width=ATTN_W, tm=TM):
    return pl.BlockSpec((tm // dil, dil * width), lambda i: (i, 0))


def _view_shape(dil, dtype, width=ATTN_W):
    return _sds((SEQ // dil, dil * width), dtype)


def _slab_scratch():
    return pltpu.VMEM((4, TM, 128), F32)


def _store_view(val, out_ref, slabs, dil):
    width = val.shape[1]
    for j in range(width // 128):
        slabs[j] = val[:, j * 128:(j + 1) * 128]
    for r in range(dil):
        for j in range(width // 128):
            c0 = r * width + j * 128
            out_ref[:, c0:c0 + 128] = slabs.at[j][pl.ds(r, TM // dil, stride=dil), :].astype(out_ref.dtype)


def _load_view(in_ref, slabs, dil, width=ATTN_W):
    for r in range(dil):
        for j in range(width // 128):
            c0 = r * width + j * 128
            slabs.at[j][pl.ds(r, TM // dil, stride=dil), :] = in_ref[:, c0:c0 + 128].astype(F32)
    return jnp.concatenate([slabs[j] for j in range(width // 128)], axis=1)


def _head_spread():
    m = lax.broadcasted_iota(jnp.int32, (HEAD_W, ATTN_W), 0)
    lane = lax.broadcasted_iota(jnp.int32, (HEAD_W, ATTN_W), 1)
    return (m == 16 * (lane >> 6)).astype(F32)


def _head_sum():
    lane = lax.broadcasted_iota(jnp.int32, (ATTN_W, HEAD_W), 0)
    m = lax.broadcasted_iota(jnp.int32, (ATTN_W, HEAD_W), 1)
    return ((lane >> 6) == (m >> 4)).astype(F32)


def _seq_params():
    return pltpu.CompilerParams(dimension_semantics=("arbitrary",), vmem_limit_bytes=VMEM_LIMIT)


def _sds(shape, dtype):
    return jax.ShapeDtypeStruct(shape, dtype)


class _Comm:
    def __init__(self, args, out_shape, n_sems, start, finish, aliased=False):
        self.args, self.out_shape, self.n_sems = list(args), list(out_shape), n_sems
        self.start, self.finish, self.aliased = start, finish, aliased


def _pcall(body, *, name, grid, in_specs, out_specs, out_shape, args, scratch_shapes=(), comms=(), after=()):
    single = not isinstance(out_shape, (list, tuple))
    out_specs = [out_specs] if single else list(out_specs)
    out_shape = [out_shape] if single else list(out_shape)
    n_in, n_out, n_scr = len(in_specs), len(out_shape), len(scratch_shapes)
    c_args = [a for c in comms for a in c.args]
    c_outs = [o for c in comms for o in c.out_shape]
    aliases, ai, ao = {}, n_in, n_out
    for c in comms:
        if c.aliased:
            aliases.update({ai + k: ao + k for k in range(len(c.args))})
        ai += len(c.args)
        ao += len(c.out_shape)
    sems = [pltpu.SemaphoreType.DMA((c.n_sems,)) for c in comms for _ in range(2)]
    steps = grid[0]

    def wrapped(*refs):
        o0 = n_in + len(c_args) + len(after)
        s0 = o0 + n_out + len(c_outs)
        m_in, m_out, m_sem = refs[n_in:n_in + len(c_args)], refs[o0 + n_out:s0], refs[s0 + n_scr:]

        def each(phase):
            ii = oi = 0
            for k, c in enumerate(comms):
                getattr(c, phase)(m_in[ii:ii + len(c.args)], m_out[oi:oi + len(c.out_shape)],
                                  m_sem[2 * k], m_sem[2 * k + 1])
                ii += len(c.args)
                oi += len(c.out_shape)

        if comms:
            @pl.when(pl.program_id(0) == 0)
            def _():
                each("start")

        body(*refs[:n_in], *refs[o0:o0 + n_out], *refs[s0:s0 + n_scr])

        if comms:
            @pl.when(pl.program_id(0) == steps - 1)
            def _():
                each("finish")

    res = pl.pallas_call(
        wrapped, name=name, grid=grid,
        in_specs=list(in_specs) + [ANY] * (len(c_args) + len(after)), out_specs=out_specs + [ANY] * len(c_outs),
        out_shape=out_shape + c_outs, scratch_shapes=list(scratch_shapes) + sems,
        input_output_aliases=aliases, compiler_params=_seq_params(),
    )(*args, *c_args, *after)
    mine = res[0] if single else list(res[:n_out])
    if not comms:
        return mine
    theirs, oi = [], n_out
    for c in comms:
        theirs.append(list(res[oi:oi + len(c.out_shape)]))
        oi += len(c.out_shape)
    return mine, theirs


def _in_pieces(g):
    lo, hi = 512 * g, 512 * (g + 1)
    return [(s, max(lo, IN_S * s) - IN_S * s, min(hi, IN_S * (s + 1)) - IN_S * s)
            for s in range(N_SHARD) if max(lo, IN_S * s) < min(hi, IN_S * (s + 1))]


def _inproj_fwd(x, pos, g_pre, w_in, lg, lb, w_sp, b_t, comms=()):
    def body(x_ref, pos_ref, g_ref, w_ref, lg_ref, lb_ref, wsp_ref, bt_ref, h_ref, u_ref, vs_ref, sgu_ref, *rest):
        qkv_refs, slabs = rest[:9], rest[9:]
        xh, _ = _rms_stats(x_ref[...])
        h = (xh * g_ref[...]).astype(BF16)
        h_ref[...] = h
        tabs = _rot_tables(pos_ref[...])

        def group(g):
            return jnp.concatenate([_dot(h, w_ref[s, :, a:b]) for s, a, b in _in_pieces(g)], axis=1)

        for t in range(3):
            val = group(t)
            if t < 2:
                val = _rope(val, tabs)
            if t == 0:
                val = val * np.float32(ATTN_SCALE)
            qkv_refs[t][...] = val.astype(BF16)
            for i, dil in enumerate(DILATIONS[1:]):
                _store_view(val, qkv_refs[3 * (i + 1) + t], slabs[t], dil)
        u = group(3)
        vs = group(4)
        u_ref[...] = u
        vs_ref[...] = vs
        bias_full = _dot_exact(bt_ref[...], _group_ones())
        gu, _, _, _, mixed, _, _ = _sgu_core(u, vs, lg_ref[...], lb_ref[...], _masked_spatial(wsp_ref), bias_full)
        sgu_ref[...] = gu * mixed

    return _pcall(
        body, name="inproj_sgu_fwd", grid=(SEQ // TM,),
        in_specs=[_rows(D_MODEL), _rows(1), _resident((1, D_MODEL)), _resident((N_SHARD, D_MODEL, IN_S)),
                  _resident((1, SGU_W)), _resident((1, SGU_W)), _resident((SGU_GROUPS, CHUNK, CHUNK)),
                  _resident((CHUNK, SGU_GROUPS))],
        out_specs=[_rows(D_MODEL), _rows(512), _rows(512), _rows(512)]
        + [_view_rows(dil) for dil in DILATIONS for _ in range(3)],
        out_shape=[_sds((SEQ, D_MODEL), BF16), _sds((SEQ, 512), F32), _sds((SEQ, 512), F32), _sds((SEQ, 512), F32)]
        + [_view_shape(dil, BF16) for dil in DILATIONS for _ in range(3)],
        scratch_shapes=[_slab_scratch() for _ in range(3)],
        args=(x, pos, g_pre, w_in, lg, lb, w_sp, b_t), comms=comms)


def _block_masks():
    row = lax.broadcasted_iota(jnp.int32, (CHUNK, CHUNK), 0)
    col = lax.broadcasted_iota(jnp.int32, (CHUNK, CHUNK), 1)
    return col <= row, col >= row


def _attn_fwd(qv, kv, vv, dil, comms=()):
    seg = SEQ // dil
    nblk = seg // CHUNK
    rps = 4 if nblk == 1 else 1

    def body(q_ref, k_ref, v_ref, o_ref, l_ref):
        left = _left_half()
        m_cur, m_prev = _block_masks()
        zero = jnp.zeros((CHUNK, CHUNK), BF16)
        ones = (jnp.where(left, 1.0, 0.0).astype(BF16), jnp.where(left, 0.0, 1.0).astype(BF16))

        sides = tuple(enumerate((left, ~left)))

        def rows(b):
            if isinstance(b, int):
                return b * CHUNK, max(b - 1, 0) * CHUNK
            return pl.multiple_of(b * CHUNK, CHUNK), pl.multiple_of(jnp.maximum(b - 1, 0) * CHUNK, CHUNK)

        def first(rr, b):
            r0, rp = rows(b)
            prev_ok = m_prev & (b > 0)
            tiles, scores = [], []
            for hp in range(4):
                ls = slice(rr * ATTN_W + hp * 128, rr * ATTN_W + (hp + 1) * 128)
                qp = q_ref[pl.ds(r0, CHUNK), ls]
                kc = k_ref[pl.ds(r0, CHUNK), ls]
                kp = k_ref[pl.ds(rp, CHUNK), ls] if nblk > 1 else None
                tiles.append((ls, v_ref[pl.ds(r0, CHUNK), ls], v_ref[pl.ds(rp, CHUNK), ls] if nblk > 1 else None))
                for _, hm in sides:
                    qh = jnp.where(hm, qp, zero)
                    sc = jnp.where(m_cur, _dot_nt(qh, kc), NEG)
                    sp = jnp.where(prev_ok, _dot_nt(qh, kp), NEG) if nblk > 1 else None
                    scores.append((sc, sp))
            return tiles, scores

        def second(scores):
            probs = []
            for sc, sp in scores:
                if nblk > 1:
                    m = jnp.max(jnp.maximum(sc, sp), axis=-1, keepdims=True)
                    pc = jnp.exp(sc - m)
                    pp = jnp.exp(sp - m)
                    probs.append((m, pc.astype(BF16), pp.astype(BF16), (pc + pp).astype(BF16)))
                else:
                    m = jnp.max(sc, axis=-1, keepdims=True)
                    pc = jnp.exp(sc - m).astype(BF16)
                    probs.append((m, pc, None, pc))
            return probs

        def third(rr, b, tiles, probs):
            r0, _ = rows(b)
            for hp, (ls, vc, vp) in enumerate(tiles):
                acc = jnp.zeros((CHUNK, CHUNK), F32)
                den = jnp.zeros((CHUNK, CHUNK), F32)
                for side, hm in sides:
                    _, pc, pp, psum = probs[2 * hp + side]
                    acc = acc + _dot(pc, jnp.where(hm, vc, zero))
                    if nblk > 1:
                        acc = acc + _dot(pp, jnp.where(hm, vp, zero))
                    den = den + _dot(psum, ones[side])
                o_ref[pl.ds(r0, CHUNK), ls] = (acc / den).astype(o_ref.dtype)
                lse = jnp.where(left, probs[2 * hp][0], probs[2 * hp + 1][0]) + jnp.log(den)
                l_ref[pl.ds(r0, CHUNK), rr * HEAD_W + 32 * hp:rr * HEAD_W + 32 * hp + 32] = lse[:, 48:80]

        def run(units):
            data = [first(rr, b) for rr, b in units]
            probs = [second(scores) for _, scores in data]
            for (rr, b), (tiles, _), pr in zip(units, data, probs):
                third(rr, b, tiles, pr)

        if nblk == 1:
            run([(rr, 0) for rr in range(rps)])
        else:
            def one(b, carry):
                run([(0, b)])
                return carry

            lax.fori_loop(0, nblk, one, 0)

    spec = pl.BlockSpec((seg, rps * ATTN_W), lambda r: (0, r))
    return _pcall(
        body, name=f"attn_fwd_d{dil}", grid=(dil // rps,),
        in_specs=[spec, spec, spec], out_specs=[spec, pl.BlockSpec((seg, rps * HEAD_W), lambda r: (0, r))],
        out_shape=[_sds((seg, dil * ATTN_W), BF16), _sds((seg, dil * HEAD_W), F32)],
        args=(qv, kv, vv), comms=comms)


def _lane_left(nrows):
    return lax.broadcasted_iota(jnp.int32, (nrows, CHUNK), 1) < HEAD_DIM


def _attn_rows(b):
    if isinstance(b, int):
        return b * CHUNK, max(b - 1, 0) * CHUNK
    return pl.multiple_of(b * CHUNK, CHUNK), pl.multiple_of(jnp.maximum(b - 1, 0) * CHUNK, CHUNK)


def _mix_out_fwd(o_list, l_list, sgu, x, w_out, g_attn, g_sgu, g_post, comms=()):
    def body(o1, o2, o3, l1, l2, l3, sgu_ref, x_ref, w_ref, ga_ref, gs_ref, gp_ref,
             attn_ref, mixed_ref, y_ref, x1_ref, lse1_ref, lse2_ref, lse3_ref, slabs_a, slabs_b):
        os = [o1[...], _load_view(o2, slabs_a, DILATIONS[1]), _load_view(o3, slabs_b, DILATIONS[2])]
        ls = [l1[...], _load_view(l2, slabs_a, DILATIONS[1], HEAD_W), _load_view(l3, slabs_b, DILATIONS[2], HEAD_W)]
        m = jnp.maximum(jnp.maximum(ls[0], ls[1]), ls[2])
        es = [jnp.exp(l - m) for l in ls]
        den = es[0] + es[1] + es[2]
        spread = _head_spread()
        attn = sum(_dot_select(e / den, spread) * o for e, o in zip(es, os))
        attn_ref[...] = attn
        lse = m + jnp.log(den)
        lse1_ref[...] = lse
        _store_view(lse, lse2_ref, slabs_a, DILATIONS[1])
        _store_view(lse, lse3_ref, slabs_b, DILATIONS[2])
        ah, _ = _rms_stats(attn)
        sh, _ = _rms_stats(sgu_ref[...])
        mixed = jnp.concatenate([ah * ga_ref[...], sh * gs_ref[...]], axis=1).astype(BF16)
        mixed_ref[...] = mixed
        y = _dot(mixed[:, 0:OUT_S], w_ref[0])
        for s in range(1, N_SHARD):
            y = y + _dot(mixed[:, s * OUT_S:(s + 1) * OUT_S], w_ref[s])
        y_ref[...] = y
        yh, _ = _rms_stats(y)
        x1_ref[...] = x_ref[...] + yh * gp_ref[...]

    return _pcall(
        body, name="mix_out_fwd", grid=(SEQ // TM,),
        in_specs=[_view_rows(dil) for dil in DILATIONS] + [_view_rows(dil, HEAD_W) for dil in DILATIONS]
        + [_rows(512), _rows(D_MODEL), _resident((N_SHARD, OUT_S, D_MODEL)),
           _resident((1, 512)), _resident((1, 512)), _resident((1, D_MODEL))],
        out_specs=[_rows(512), _rows(D_MODEL), _rows(D_MODEL), _rows(D_MODEL)]
        + [_view_rows(dil, HEAD_W) for dil in DILATIONS],
        out_shape=[_sds((SEQ, 512), F32), _sds((SEQ, D_MODEL), BF16), _sds((SEQ, D_MODEL), F32),
                   _sds((SEQ, D_MODEL), F32)] + [_view_shape(dil, F32, HEAD_W) for dil in DILATIONS],
        scratch_shapes=[_slab_scratch(), _slab_scratch()],
        args=(*o_list, *l_list, sgu, x, w_out, g_attn, g_sgu, g_post), comms=comms)


def _ffn_fwd_bwd(x1, target, w_gate, w_up, w_down, g_pre, g_post, comms=()):
    def body(x1_ref, t_ref, wg_ref, wu_ref, wd_ref, gpf_ref, gpo_ref,
             h2_ref, a_ref, dg_ref, dup_ref, df_ref, dx1_ref, loss_ref, dgpf_ref, dgpo_ref, g_scr, up_scr):
        @pl.when(pl.program_id(0) == 0)
        def _():
            loss_ref[...] = jnp.zeros_like(loss_ref)
            dgpf_ref[...] = jnp.zeros_like(dgpf_ref)
            dgpo_ref[...] = jnp.zeros_like(dgpo_ref)

        x1 = x1_ref[...]
        gpf = gpf_ref[...]
        gpo = gpo_ref[...]
        xh, r = _rms_stats(x1)
        h2 = (xh * gpf).astype(BF16)
        h2_ref[...] = h2
        f = jnp.zeros((TM_FFN, D_MODEL), F32)
        for c0, c1 in FF_CHUNKS:
            g = _dot_nt(h2, wg_ref[c0:c1, :])
            up = _dot_nt(h2, wu_ref[c0:c1, :])
            g_scr[:, c0:c1] = g
            up_scr[:, c0:c1] = up
            a = (g * _sigmoid(g) * up).astype(BF16)
            a_ref[:, c0:c1] = a
            f = f + _dot(a, wd_ref[c0:c1, :])
        fh, rf = _rms_stats(f)
        diff = x1 + fh * gpo - t_ref[...]
        loss_ref[...] += jnp.sum(diff * diff, axis=0, keepdims=True)
        dout = diff * np.float32(1.0 / D_MODEL)
        df, dgpo = _rms_bwd(fh, rf, gpo, dout)
        dgpo_ref[...] += dgpo
        dfb = df.astype(BF16)
        df_ref[...] = dfb
        dh2 = jnp.zeros((TM_FFN, D_MODEL), F32)
        for c0, c1 in FF_CHUNKS:
            da = _dot_nt(dfb, wd_ref[c0:c1, :])
            g = g_scr[:, c0:c1]
            up = up_scr[:, c0:c1]
            sg = _sigmoid(g)
            dup = (da * (g * sg)).astype(BF16)
            dg = (da * up * (sg * (1.0 + g * (1.0 - sg)))).astype(BF16)
            dg_ref[:, c0:c1] = dg
            dup_ref[:, c0:c1] = dup
            dh2 = dh2 + _dot(dg, wg_ref[c0:c1, :]) + _dot(dup, wu_ref[c0:c1, :])
        dx, dgpf = _rms_bwd(xh, r, gpf, dh2)
        dgpf_ref[...] += dgpf
        dx1_ref[...] = dout + dx

    return _pcall(
        body, name="ffn_fwd_bwd", grid=(SEQ // TM_FFN,),
        in_specs=[_rows(D_MODEL, TM_FFN), _rows(D_MODEL, TM_FFN), _resident((FF, D_MODEL)),
                  _resident((FF, D_MODEL)), _resident((FF, D_MODEL)),
                  _resident((1, D_MODEL)), _resident((1, D_MODEL))],
        out_specs=[_rows(D_MODEL, TM_FFN), _rows(FF, TM_FFN), _rows(FF, TM_FFN), _rows(FF, TM_FFN),
                   _rows(D_MODEL, TM_FFN), _rows(D_MODEL, TM_FFN), _acc(D_MODEL), _acc(D_MODEL), _acc(D_MODEL)],
        out_shape=[_sds((SEQ, D_MODEL), BF16), _sds((SEQ, FF), BF16), _sds((SEQ, FF), BF16),
                   _sds((SEQ, FF), BF16), _sds((SEQ, D_MODEL), BF16), _sds((SEQ, D_MODEL), F32),
                   _sds((1, D_MODEL), F32), _sds((1, D_MODEL), F32), _sds((1, D_MODEL), F32)],
        scratch_shapes=[pltpu.VMEM((TM_FFN, FF), F32), pltpu.VMEM((TM_FFN, FF), F32)],
        args=(x1, target, w_gate, w_up, w_down, g_pre, g_post), comms=comms)


def _wgrad(a, b, a_spec, b_spec, out_block, name, comms=()):
    def body(a_ref, b_ref, o_ref, ob_ref):
        av = a_ref[0] if len(a_ref.shape) == 3 else a_ref[...]
        bv = b_ref[0] if len(b_ref.shape) == 3 else b_ref[...]
        g = _dot_tn(av, bv)
        o_ref[0] = g
        ob_ref[0] = g.astype(BF16)

    return _pcall(
        body, name=name, grid=(N_SHARD,),
        in_specs=[a_spec, b_spec],
        out_specs=[pl.BlockSpec((1,) + out_block, lambda s: (s, 0, 0))] * 2,
        out_shape=[_sds((N_SHARD,) + out_block, F32), _sds((N_SHARD,) + out_block, BF16)],
        args=(a, b), comms=comms)


def _outproj_bwd(dx1, y, attn, sgu, w_out, g_post, g_attn, g_sgu, u, vs, lg, lb, w_sp, b_t, comms=(), after=()):
    sgu_bwd = _sgu_bwd_body()

    def body(dx1_ref, y_ref, attn_ref, sgu_ref, w_ref, gp_ref, ga_ref, gs_ref, u_ref, vs_ref, lg_ref, lb_ref,
             wsp_ref, bt_ref, dy_ref, du_ref, dvs_ref, dgp_ref, dga_ref, dgs_ref, dw_ref, db_ref, dlg_ref, dlb_ref,
             *rest):
        dattn_refs, delta_refs, (slabs_a, slabs_b, dsgu_ref, dbias_scr) = rest[0:3], rest[3:6], rest[6:]

        @pl.when(pl.program_id(0) == 0)
        def _():
            dgp_ref[...] = jnp.zeros_like(dgp_ref)
            dga_ref[...] = jnp.zeros_like(dga_ref)
            dgs_ref[...] = jnp.zeros_like(dgs_ref)

        yh, ry = _rms_stats(y_ref[...])
        dy, dgp = _rms_bwd(yh, ry, gp_ref[...], dx1_ref[...])
        dgp_ref[...] += dgp
        dyb = dy.astype(BF16)
        dy_ref[...] = dyb
        dmixed = jnp.concatenate([_dot_nt(dyb, w_ref[s]) for s in range(N_SHARD)], axis=1)
        attn = attn_ref[...]
        ah, ra = _rms_stats(attn)
        dattn, dga = _rms_bwd(ah, ra, ga_ref[...], dmixed[:, 0:512])
        dga_ref[...] += dga
        sh, rs = _rms_stats(sgu_ref[...])
        dsgu, dgs = _rms_bwd(sh, rs, gs_ref[...], dmixed[:, 512:1024])
        dgs_ref[...] += dgs
        dsgu_ref[...] = dsgu
        delta = _dot_select(dattn * attn, _head_sum())
        dattn_refs[0][...] = dattn.astype(BF16)
        delta_refs[0][...] = delta
        for i, dil in enumerate(DILATIONS[1:]):
            _store_view(dattn, dattn_refs[i + 1], slabs_a, dil)
            _store_view(delta, delta_refs[i + 1], slabs_b, dil)
        sgu_bwd(u_ref, vs_ref, dsgu_ref, lg_ref, lb_ref, wsp_ref, bt_ref,
                du_ref, dvs_ref, dw_ref, db_ref, dlg_ref, dlb_ref, dbias_scr)

    return _pcall(
        body, name="outproj_sgu_bwd", grid=(SEQ // TM,),
        in_specs=[_rows(D_MODEL), _rows(D_MODEL), _rows(512), _rows(512), _resident((N_SHARD, OUT_S, D_MODEL)),
                  _resident((1, D_MODEL)), _resident((1, 512)), _resident((1, 512)),
                  _rows(SGU_W), _rows(SGU_W), _resident((1, SGU_W)), _resident((1, SGU_W)),
                  _resident((SGU_GROUPS, CHUNK, CHUNK)), _resident((CHUNK, SGU_GROUPS))],
        out_specs=[_rows(D_MODEL), _rows(SGU_W), _rows(SGU_W), _acc(D_MODEL), _acc(512), _acc(512),
                   pl.BlockSpec((SGU_GROUPS, CHUNK, CHUNK), lambda i: (0, 0, 0)), _acc(CHUNK, SGU_GROUPS),
                   _acc(SGU_W), _acc(SGU_W)]
        + [_view_rows(dil) for dil in DILATIONS] + [_view_rows(dil, HEAD_W) for dil in DILATIONS],
        out_shape=[_sds((SEQ, D_MODEL), BF16), _sds((SEQ, SGU_W), BF16), _sds((SEQ, SGU_W), BF16),
                   _sds((1, D_MODEL), F32), _sds((1, 512), F32), _sds((1, 512), F32),
                   _sds((SGU_GROUPS, CHUNK, CHUNK), F32), _sds((SGU_GROUPS, CHUNK), F32),
                   _sds((1, SGU_W), F32), _sds((1, SGU_W), F32)]
        + [_view_shape(dil, BF16) for dil in DILATIONS] + [_view_shape(dil, F32, HEAD_W) for dil in DILATIONS],
        scratch_shapes=[_slab_scratch(), _slab_scratch(), pltpu.VMEM((TM, SGU_W), F32),
                        pltpu.VMEM((CHUNK, SGU_W), F32)],
        args=(dx1, y, attn, sgu, w_out, g_post, g_attn, g_sgu, u, vs, lg, lb, w_sp, b_t), comms=comms, after=after)


def _sgu_bwd_body():
    nsteps = SEQ // TM

    def body(u_ref, vs_ref, ds_ref, lg_ref, lb_ref, w_ref, bt_ref,
             du_ref, dvs_ref, dw_ref, db_ref, dlg_ref, dlb_ref, dbias_scr):
        i = pl.program_id(0)

        @pl.when(i == 0)
        def _():
            dw_ref[...] = jnp.zeros_like(dw_ref)
            dlg_ref[...] = jnp.zeros_like(dlg_ref)
            dlb_ref[...] = jnp.zeros_like(dlb_ref)
            dbias_scr[...] = jnp.zeros_like(dbias_scr)

        wm = _masked_spatial(w_ref)
        ones_g = _group_ones()
        bias_full = _dot_exact(bt_ref[...], ones_g)
        u = u_ref[...]
        vs = vs_ref[...]
        lg = lg_ref[...]
        gu, xh, rstd, vnb, mixed, cdf_u, cdf_vs = _sgu_core(u, vs, lg, lb_ref[...], wm, bias_full)
        dsgu = ds_ref[...]
        du_ref[...] = (dsgu * mixed * _gelu_grad(u, cdf_u)).astype(BF16)
        dmixed = dsgu * gu
        left = _left_half()
        dvn_rows = []
        for c in range(TM // CHUNK):
            rs = slice(c * CHUNK, (c + 1) * CHUNK)
            dm_c = dmixed[rs, :]
            dbias_scr[...] += dm_c
            pieces = []
            for p in range(4):
                ls = slice(p * 128, (p + 1) * 128)
                dmp = dm_c[:, ls]
                vp = vnb[rs, ls]
                dmb = dmp.astype(BF16)
                zero = jnp.zeros_like(dmb)
                dw_ref[2 * p] += _dot_nt(jnp.where(left, dmb, zero), vp)
                dw_ref[2 * p + 1] += _dot_nt(jnp.where(left, zero, dmb), vp)
                pieces.append(jnp.where(left, _dot_tn(wm[2 * p], dmb), _dot_tn(wm[2 * p + 1], dmb)))
            dvn_rows.append(jnp.concatenate(pieces, axis=1))
        dvn = jnp.concatenate(dvn_rows, axis=0)
        dlg_ref[...] += jnp.sum(dvn * xh, axis=0, keepdims=True)
        dlb_ref[...] += jnp.sum(dvn, axis=0, keepdims=True)
        dxh = dvn * lg
        dgv = rstd * (dxh - jnp.mean(dxh, axis=-1, keepdims=True) - xh * jnp.mean(dxh * xh, axis=-1, keepdims=True))
        dvs_ref[...] = (dgv * _gelu_grad(vs, cdf_vs)).astype(BF16)

        @pl.when(i == nsteps - 1)
        def _():
            row = lax.broadcasted_iota(jnp.int32, (CHUNK, CHUNK), 0)
            col = lax.broadcasted_iota(jnp.int32, (CHUNK, CHUNK), 1)
            for g in range(SGU_GROUPS):
                dw_ref[g] = jnp.where(col <= row, dw_ref[g], 0.0)
            db_ref[...] = lax.dot_general(ones_g, dbias_scr[...], (((1,), (1,)), ((), ())),
                                          preferred_element_type=F32, precision=lax.Precision.HIGHEST)

    return body


def _attn_bwd(qv, kv, vv, dov, deltav, lsev, dil, comms=(), after=()):
    seg = SEQ // dil
    nblk = seg // CHUNK
    rps = 4 if nblk == 1 else 1

    def body(q_ref, k_ref, v_ref, do_ref, dl_ref, lse_ref, dq_ref, dk_ref, dv_ref, dk_wait, dv_wait):
        left = _left_half()
        m_cur, m_prev = _block_masks()
        sides = tuple(enumerate((left, ~left)))
        zero = jnp.zeros((CHUNK, CHUNK), BF16)

        def first(rr, b, both):
            r0, rp = _attn_rows(b)
            keys = pl.ds(rp, 2 * CHUNK) if both else pl.ds(r0, CHUNK)
            tiles, heads = [], []
            for hp in range(4):
                ls = slice(rr * ATTN_W + hp * 128, rr * ATTN_W + (hp + 1) * 128)
                qp = q_ref[pl.ds(r0, CHUNK), ls]
                dop = do_ref[pl.ds(r0, CHUNK), ls]
                k2 = k_ref[keys, ls]
                v2 = v_ref[keys, ls]
                tiles.append((ls, k2))
                for _, hm in sides:
                    qh = jnp.where(hm, qp, zero)
                    doh = jnp.where(hm, dop, zero)
                    heads.append((qh, doh, _dot_nt(k2, qh), _dot_nt(v2, doh)))
            return tiles, heads

        def second(rr, b, heads, both):
            r0, _ = _attn_rows(b)
            ok = jnp.concatenate([m_cur, m_prev], axis=0) if both else m_prev
            lanes = slice(rr * HEAD_W, (rr + 1) * HEAD_W)
            lse_t = lse_ref[pl.ds(r0, CHUNK), lanes].T
            dl_t = dl_ref[pl.ds(r0, CHUNK), lanes].T
            out = []
            for i, (_, _, s_t, dp_t) in enumerate(heads):
                p = jnp.exp(jnp.where(ok, s_t - lse_t[16 * i:16 * i + 1, :], NEG))
                out.append((p.astype(BF16), (p * (dp_t - dl_t[16 * i:16 * i + 1, :])).astype(BF16)))
            return out

        def third(rr, b, tiles, heads, probs, both):
            r0, rp = _attn_rows(b)
            nk = 2 * CHUNK if both else CHUNK
            left_k = _lane_left(nk)
            zero_k = jnp.zeros((nk, CHUNK), BF16)
            for hp, (ls, k2) in enumerate(tiles):
                dq = jnp.zeros((CHUNK, CHUNK), F32)
                dk2 = jnp.zeros((nk, CHUNK), F32)
                dv2 = jnp.zeros((nk, CHUNK), F32)
                for side in range(2):
                    qh, doh, _, _ = heads[2 * hp + side]
                    p, ds = probs[2 * hp + side]
                    dq = dq + _dot_tn(ds, jnp.where(left_k if side == 0 else ~left_k, k2, zero_k))
                    dk2 = dk2 + _dot(ds, qh)
                    dv2 = dv2 + _dot(p, doh)
                dq_ref[pl.ds(r0, CHUNK), ls] = dq.astype(dq_ref.dtype)
                if nblk == 1:
                    dk_ref[pl.ds(r0, CHUNK), ls] = dk2.astype(dk_ref.dtype)
                    dv_ref[pl.ds(r0, CHUNK), ls] = dv2.astype(dv_ref.dtype)
                elif both:
                    dk_ref[pl.ds(rp, CHUNK), ls] = (dk_wait[:, ls] + dk2[0:CHUNK]).astype(dk_ref.dtype)
                    dv_ref[pl.ds(rp, CHUNK), ls] = (dv_wait[:, ls] + dv2[0:CHUNK]).astype(dv_ref.dtype)
                    dk_wait[:, ls] = dk2[CHUNK:]
                    dv_wait[:, ls] = dv2[CHUNK:]
                else:
                    dk_wait[:, ls] = dk2
                    dv_wait[:, ls] = dv2

        def run(units, both):
            data = [first(rr, b, both) for rr, b in units]
            probs = [second(rr, b, heads, both) for (rr, b), (_, heads) in zip(units, data)]
            for (rr, b), (tiles, heads), pr in zip(units, data, probs):
                third(rr, b, tiles, heads, pr, both)

        run([(rr, 0) for rr in range(rps)], False)
        if nblk > 1:
            def one(b, carry):
                run([(0, b)], True)
                return carry

            lax.fori_loop(1, nblk, one, 0)
            last = (nblk - 1) * CHUNK
            dk_ref[last:last + CHUNK, :] = dk_wait[...].astype(dk_ref.dtype)
            dv_ref[last:last + CHUNK, :] = dv_wait[...].astype(dv_ref.dtype)

    spec = pl.BlockSpec((seg, rps * ATTN_W), lambda r: (0, r))
    return _pcall(
        body, name=f"attn_bwd_d{dil}", grid=(dil // rps,),
        in_specs=[spec] * 4 + [pl.BlockSpec((seg, rps * HEAD_W), lambda r: (0, r))] * 2, out_specs=[spec] * 3,
        out_shape=[_sds((seg, dil * ATTN_W), BF16)] * 3,
        scratch_shapes=[pltpu.VMEM((CHUNK, ATTN_W), F32), pltpu.VMEM((CHUNK, ATTN_W), F32)],
        args=(qv, kv, vv, dov, deltav, lsev), comms=comms, after=after)


def _inproj_bwd(dqs, dks, dvs, du, dvs_sgu, pos, x, dx1, w_in, g_pre, comms=()):
    def body(dq1, dq2, dq3, dk1, dk2, dk3, dv1, dv2, dv3, du_ref, dvs_ref, pos_ref, x_ref, dx1_ref, w_ref, g_ref,
             dproj_ref, gx_ref, dg_ref, slabs_a, slabs_b):
        @pl.when(pl.program_id(0) == 0)
        def _():
            dg_ref[...] = jnp.zeros_like(dg_ref)

        def total(r1, r2, r3):
            return r1[...] + _load_view(r2, slabs_a, DILATIONS[1]) + _load_view(r3, slabs_b, DILATIONS[2])

        tabs = _rot_tables(pos_ref[...])
        groups = {3: du_ref[...], 4: dvs_ref[...]}
        dh = jnp.zeros((TM, D_MODEL), F32)
        for g in (3, 4, 0, 1, 2):
            if g == 0:
                groups[g] = _rope_bwd(total(dq1, dq2, dq3) * np.float32(ATTN_SCALE), tabs).astype(BF16)
            elif g == 1:
                groups[g] = _rope_bwd(total(dk1, dk2, dk3), tabs).astype(BF16)
            elif g == 2:
                groups[g] = total(dv1, dv2, dv3).astype(BF16)
            dproj_ref[:, 512 * g:512 * (g + 1)] = groups[g]
            off = 0
            for s, a, b in _in_pieces(g):
                dh = dh + _dot_nt(groups[g][:, off:off + b - a], w_ref[s, :, a:b])
                off += b - a
        g = g_ref[...]
        xh, r = _rms_stats(x_ref[...])
        dx, dg = _rms_bwd(xh, r, g, dh)
        dg_ref[...] += dg
        gx_ref[...] = dx1_ref[...] + dx

    return _pcall(
        body, name="inproj_bwd", grid=(SEQ // TM,),
        in_specs=[_view_rows(dil) for dil in DILATIONS] * 3
        + [_rows(512), _rows(512), _rows(1), _rows(D_MODEL), _rows(D_MODEL),
           _resident((N_SHARD, D_MODEL, IN_S)), _resident((1, D_MODEL))],
        out_specs=[_rows(PROJ_W), _rows(D_MODEL), _acc(D_MODEL)],
        out_shape=[_sds((SEQ, PROJ_W), BF16), _sds((SEQ, D_MODEL), F32), _sds((1, D_MODEL), F32)],
        scratch_shapes=[_slab_scratch(), _slab_scratch()],
        args=(*dqs, *dks, *dvs, du, dvs_sgu, pos, x, dx1, w_in, g_pre), comms=comms)


def _coords():
    return lax.axis_index("x"), lax.axis_index("y"), lax.axis_index("c")


def _other_chips(x, y):
    return [(1 - x, y), (x, 1 - y), (1 - x, 1 - y)]


WEIGHTS = ("pre_mix_norm", "w_in", "sgu_ln_gain", "sgu_ln_bias", "sgu_w_spatial", "sgu_b_spatial", "attn_out_norm",
           "sgu_out_norm", "w_out", "post_mix_norm", "pre_ffn_norm", "w_gate", "w_up", "w_down", "post_ffn_norm")
SMALL = ("pre_mix_norm", "post_mix_norm", "pre_ffn_norm", "post_ffn_norm", "sgu_ln_gain", "sgu_ln_bias",
         "attn_out_norm", "sgu_out_norm", "sgu_w_spatial", "sgu_b_spatial")


def _remote(src, dst, send_sem, recv_sem, to):
    return pltpu.make_async_remote_copy(src_ref=src, dst_ref=dst, send_sem=send_sem, recv_sem=recv_sem,
                                        device_id=to, device_id_type=MESH)


def _halves(a):
    *lead, rows, cols = a.shape
    return a.reshape(*lead, 2, rows // 2, cols)


def _gather_ici(shards):
    n = len(shards)

    def desc(ins, outs, ss, rs, j, w, landed):
        x, y, c = _coords()
        cx, cy = _other_chips(x, y)[j]
        shard = 2 * cx + cy if landed else 2 * x + y
        return _remote(ins[w].at[c], outs[w].at[shard, c], ss.at[j * n + w], rs.at[j * n + w], (cx, cy, c))

    def start(ins, outs, ss, rs):
        for j in range(3):
            for w in range(n):
                desc(ins, outs, ss, rs, j, w, False).start()

    def finish(ins, outs, ss, rs):
        for j in range(3):
            for w in range(n):
                desc(ins, outs, ss, rs, j, w, True).wait_recv()
                desc(ins, outs, ss, rs, j, w, False).wait_send()

    return _Comm(shards, [_sds((N_SHARD,) + s.shape, s.dtype) for s in shards], 3 * n, start, finish)


def _gather_pass(fulls):
    n = len(fulls)

    def desc(bufs, ss, rs, j, w, landed):
        x, y, c = _coords()
        cx, cy = _other_chips(x, y)[j]
        shard = 2 * cx + cy
        return _remote(bufs[w].at[shard, c], bufs[w].at[shard, 1 - c if landed else c],
                       ss.at[j * n + w], rs.at[j * n + w], (x, y, 1 - c))

    def start(ins, outs, ss, rs):
        for j in range(3):
            for w in range(n):
                desc(outs, ss, rs, j, w, False).start()

    def finish(ins, outs, ss, rs):
        for j in range(3):
            for w in range(n):
                desc(outs, ss, rs, j, w, True).wait_recv()
                desc(outs, ss, rs, j, w, False).wait_send()

    return _Comm(fulls, [_sds(f.shape, f.dtype) for f in fulls], 3 * n, start, finish, aliased=True)


def _rs_sibling(gws):
    n = len(gws)

    def desc(ins, outs, ss, rs, w):
        x, y, c = _coords()
        return _remote(ins[w].at[:, 1 - c], outs[w], ss.at[w], rs.at[w], (x, y, 1 - c))

    def start(ins, outs, ss, rs):
        for w in range(n):
            desc(ins, outs, ss, rs, w).start()

    def finish(ins, outs, ss, rs):
        for w in range(n):
            desc(ins, outs, ss, rs, w).wait()

    out_shape = [_sds((N_SHARD, g.shape[1] // 2, g.shape[2]), g.dtype) for g in gws]
    return _Comm([_halves(g) for g in gws], out_shape, n, start, finish)


def _rs_chips(pbs):
    n = len(pbs)

    def desc(ins, outs, ss, rs, j, w):
        x, y, c = _coords()
        cx, cy = _other_chips(x, y)[j]
        return _remote(ins[w].at[2 * cx + cy], outs[w].at[j], ss.at[j * n + w], rs.at[j * n + w], (cx, cy, c))

    def start(ins, outs, ss, rs):
        for j in range(3):
            for w in range(n):
                desc(ins, outs, ss, rs, j, w).start()

    def finish(ins, outs, ss, rs):
        for j in range(3):
            for w in range(n):
                desc(ins, outs, ss, rs, j, w).wait()

    return _Comm(pbs, [_sds((3,) + p.shape[1:], p.dtype) for p in pbs], 3 * n, start, finish)


def _rs_join(halves):
    n = len(halves)

    def desc(bufs, ss, rs, w, landed):
        x, y, c = _coords()
        return _remote(bufs[w].at[c], bufs[w].at[1 - c if landed else c], ss.at[w], rs.at[w], (x, y, 1 - c))

    def start(ins, outs, ss, rs):
        for w in range(n):
            desc(outs, ss, rs, w, False).start()

    def finish(ins, outs, ss, rs):
        for w in range(n):
            desc(outs, ss, rs, w, True).wait_recv()
            desc(outs, ss, rs, w, False).wait_send()

    return _Comm(halves, [_sds(h.shape, h.dtype) for h in halves], n, start, finish, aliased=True)


def _small_exchange(buf):
    def desc(ins, outs, ss, rs, k, landed):
        x, y, c = _coords()
        px = 1 - x if (k >> 2) & 1 else x
        py = 1 - y if (k >> 1) & 1 else y
        pc = 1 - c if k & 1 else c
        slot = 4 * px + 2 * py + pc if landed else 4 * x + 2 * y + c
        return _remote(ins[0], outs[0].at[slot], ss.at[k - 1], rs.at[k - 1], (px, py, pc))

    def start(ins, outs, ss, rs):
        for k in range(1, 8):
            desc(ins, outs, ss, rs, k, False).start()

    def finish(ins, outs, ss, rs):
        for k in range(1, 8):
            desc(ins, outs, ss, rs, k, True).wait_recv()
            desc(ins, outs, ss, rs, k, False).wait_send()

    return _Comm([buf], [_sds((8,) + buf.shape, buf.dtype)], 7, start, finish)


HBM = pl.BlockSpec(memory_space=pltpu.HBM)
SEM = pl.BlockSpec(memory_space=pltpu.SEMAPHORE)
DATAFLOW = pltpu.SideEffectType.DATAFLOW_SIDE_EFFECTING


def _split_starts(name, comms, after):
    srcs = [[pltpu.with_memory_space_constraint(s, pltpu.HBM) for s in c.args] for c in comms]
    lands = [[pltpu.with_memory_space_constraint(lax.empty(o.shape, o.dtype), pltpu.HBM) for o in c.out_shape]
             for c in comms]
    bufs = [b for k in range(len(comms)) for b in srcs[k] + lands[k]]
    nb, nc = len(bufs), len(comms)

    def body(*refs):
        sems = refs[nb + 1:nb + 1 + 2 * nc]
        off = 0
        for k, c in enumerate(comms):
            ns, nl = len(srcs[k]), len(lands[k])
            c.start(refs[off:off + ns], refs[off + ns:off + ns + nl], sems[2 * k], sems[2 * k + 1])
            off += ns + nl
        refs[-1][...] = jnp.zeros_like(refs[-1])

    res = pl.pallas_call(
        body, name=name,
        out_shape=(*[pltpu.SemaphoreType.DMA((c.n_sems,)) for c in comms for _ in range(2)],
                   *[pltpu.HBM(b.shape, b.dtype) for b in bufs], _sds((8, 128), F32)),
        in_specs=[HBM] * nb + [ANY],
        out_specs=(*[SEM] * (2 * nc), *[HBM] * nb, pl.BlockSpec(memory_space=pltpu.VMEM)),
        input_output_aliases={i: 2 * nc + i for i in range(nb)},
        compiler_params=pltpu.CompilerParams(has_side_effects=DATAFLOW),
    )(*bufs, after)
    states, off = [], 2 * nc
    for k in range(nc):
        ns, nl = len(srcs[k]), len(lands[k])
        states.append((res[2 * k], res[2 * k + 1], list(res[off:off + ns]), list(res[off + ns:off + ns + nl])))
        off += ns + nl
    return states, res[-1]


def _split_wait(name, comm, send_sems, recv_sems, srcs, lands, after):
    ns, nb = len(srcs), len(lands)
    after = list(after) if isinstance(after, (list, tuple)) else [after]

    def body(*refs):
        comm.finish(refs[:ns], refs[ns:ns + nb], refs[ns + nb], refs[ns + nb + 1])

    res = pl.pallas_call(
        body, name=name,
        out_shape=tuple(pltpu.HBM(b.shape, b.dtype) for b in srcs + lands),
        in_specs=[HBM] * (ns + nb) + [SEM, SEM] + [ANY] * len(after), out_specs=tuple([HBM] * (ns + nb)),
        input_output_aliases={i: i for i in range(ns + nb)},
        compiler_params=pltpu.CompilerParams(has_side_effects=DATAFLOW),
    )(*srcs, *lands, send_sems, recv_sems, *after)
    return list(res[ns:])


LOSS_ROW = "loss_cols"
SMALL_EARLY = ("post_mix_norm", "pre_ffn_norm", "post_ffn_norm", "sgu_ln_gain", "sgu_ln_bias", "attn_out_norm",
               "sgu_out_norm", "sgu_b_spatial", LOSS_ROW)
SMALL_LATE = ("pre_mix_norm",)
SMALL_WSP = "sgu_w_spatial"


def _pack(d, names, rows):
    flat = [d[n].reshape(-1) for n in names]
    used = sum(f.shape[0] for f in flat)
    flat.append(jnp.zeros((rows * 1024 - used,), F32))
    return jnp.concatenate(flat).reshape(rows, 1024)


def _unpack(buf, names, shapes):
    flat = buf.reshape(-1)
    out, off = {}, 0
    for n in names:
        size = int(np.prod(shapes[n]))
        out[n] = flat[off:off + size].reshape(shapes[n])
        off += size
    return out


def _chip_sums(gbs, recvs, shard_core):
    n = len(gbs)
    _, rows, cols = gbs[0].shape
    hw = rows // 2

    def body(sc_ref, *refs):
        for k in range(n):
            refs[2 * n + k][...] = (refs[k][...].astype(F32) + refs[n + k][...].astype(F32)).astype(BF16)

    def other(s, sc):
        return jnp.where(s >= sc[0], s + 1, s)

    mine = pl.BlockSpec((1, hw, cols), lambda s, sc: (other(s, sc), sc[1], 0))
    plain = pl.BlockSpec((1, hw, cols), lambda s, sc: (other(s, sc), 0, 0))
    return pl.pallas_call(
        body, name="rs_chip_sums",
        grid_spec=pltpu.PrefetchScalarGridSpec(num_scalar_prefetch=1, grid=(N_SHARD - 1,),
                                               in_specs=[mine] * n + [plain] * n, out_specs=[plain] * n),
        out_shape=[_sds((N_SHARD, hw, cols), BF16)] * n,
        compiler_params=_seq_params(),
    )(shard_core, *gbs, *recvs)


def _final_sums(gws, recv_sibs, recv_chips, shard_core):
    n = len(gws)
    halves = [(g.shape[1] // 2, g.shape[2]) for g in gws]

    def body(sc_ref, *refs):
        for k in range(n):
            acc = refs[k][0] + refs[n + k][0]
            for j in range(3):
                acc = acc + refs[2 * n + k][j].astype(F32)
            refs[3 * n + k][0] = acc

    def specs(lead, index):
        return [pl.BlockSpec((lead, hw, cols), index) for hw, cols in halves]

    return pl.pallas_call(
        body, name="rs_final_sums",
        grid_spec=pltpu.PrefetchScalarGridSpec(
            num_scalar_prefetch=1, grid=(1,),
            in_specs=specs(1, lambda i, sc: (sc[0], sc[1], 0)) + specs(1, lambda i, sc: (sc[0], 0, 0))
            + specs(3, lambda i, sc: (0, 0, 0)),
            out_specs=specs(1, lambda i, sc: (sc[1], 0, 0))),
        out_shape=[_sds((2, hw, cols), F32) for hw, cols in halves],
        compiler_params=_seq_params(),
    )(shard_core, *gws, *recv_sibs, *recv_chips)


ADAM_BLOCKS = 4


def _adamw_multi(ws, gs, ms, vs, name, after=()):
    n = len(ws)

    def body(*refs):
        outs = refs[4 * n + len(after):]
        for k in range(n):
            g = refs[n + k][...]
            d, m, v = _adam_math(refs[k][...], g, refs[2 * n + k][...], refs[3 * n + k][...])
            outs[4 * k][...], outs[4 * k + 1][...], outs[4 * k + 2][...], outs[4 * k + 3][...] = g, d, m, v

    specs = [pl.BlockSpec((w.shape[0] // ADAM_BLOCKS, w.shape[1]), lambda i: (i, 0)) for w in ws]
    res = pl.pallas_call(
        body, name=name, grid=(ADAM_BLOCKS,), in_specs=specs * 4 + [ANY] * len(after),
        out_specs=[s for s in specs for _ in range(4)],
        out_shape=[_sds(w.shape, F32) for w in ws for _ in range(4)],
        compiler_params=_seq_params(),
    )(*ws, *gs, *ms, *vs, *after)
    return [tuple(res[4 * k:4 * k + 4]) for k in range(n)]


def _wgrad_ff(a_list, b, name):
    n = len(a_list)
    cols = 256

    def body(*refs):
        bv = refs[n][...]
        for k in range(n):
            g = _dot_tn(refs[k][...], bv)
            refs[n + 1 + k][...] = g
            refs[2 * n + 1 + k][...] = g.astype(BF16)

    res = _pcall(
        body, name=name, grid=(FF // cols,),
        in_specs=[pl.BlockSpec((SEQ, cols), lambda j: (0, j))] * n
        + [pl.BlockSpec((SEQ, D_MODEL), lambda j: (0, 0), pipeline_mode=pl.Buffered(1))],
        out_specs=[pl.BlockSpec((cols, D_MODEL), lambda j: (j, 0))] * (2 * n),
        out_shape=[_sds((FF, D_MODEL), F32)] * n + [_sds((FF, D_MODEL), BF16)] * n, args=(*a_list, b))
    return [m.reshape(N_SHARD, FF_S, D_MODEL) for m in res]


def _comm_only(name, comms):
    return _pcall(lambda: None, name=name, grid=(1,), in_specs=[], out_specs=[], out_shape=[], args=(),
                  comms=comms)[1]


def _adam_math(w, g, m, v):
    m = ADAM_B1 * m + (1.0 - ADAM_B1) * g
    v = ADAM_B2 * v + (1.0 - ADAM_B2) * (g * g)
    m_hat = m / (1.0 - ADAM_B1 ** ADAM_STEP)
    v_hat = v / (1.0 - ADAM_B2 ** ADAM_STEP)
    return -ADAM_LR * (m_hat / (jnp.sqrt(v_hat) + ADAM_EPS) + ADAM_WD * w), m, v


def _adamw_small(own, slots, w, m, v, me):
    rows, cols = own.shape

    def body(me_ref, own_ref, slots_ref, w_ref, m_ref, v_ref, g_ref, d_ref, nm_ref, nv_ref):
        own_v = own_ref[...].astype(F32)
        g = jnp.where(me_ref[0] == 0, own_v, slots_ref[0].astype(F32))
        for i in range(1, 8):
            g = g + jnp.where(me_ref[0] == i, own_v, slots_ref[i].astype(F32))
        g_ref[...] = g
        d_ref[...], nm_ref[...], nv_ref[...] = _adam_math(w_ref[...], g, m_ref[...], v_ref[...])

    flat = pl.BlockSpec((rows, cols), lambda i, me_ref: (0, 0))
    return pl.pallas_call(
        body, name="adamw_small",
        grid_spec=pltpu.PrefetchScalarGridSpec(
            num_scalar_prefetch=1, grid=(1,),
            in_specs=[flat, pl.BlockSpec((8, rows, cols), lambda i, me_ref: (0, 0, 0)), flat, flat, flat],
            out_specs=[flat] * 4),
        out_shape=[_sds((rows, cols), F32)] * 4,
        compiler_params=_seq_params(),
    )(me, own, slots, w, m, v)


def kernel(x, positions, pre_mix_norm, w_in, sgu_ln_gain, sgu_ln_bias, sgu_w_spatial, sgu_b_spatial, attn_out_norm, sgu_out_norm, w_out, post_mix_norm, pre_ffn_norm, w_gate, w_up, w_down, post_ffn_norm, loss_target, m_pre_mix_norm, m_w_in, m_sgu_ln_gain, m_sgu_ln_bias, m_sgu_w_spatial, m_sgu_b_spatial, m_attn_out_norm, m_sgu_out_norm, m_w_out, m_post_mix_norm, m_pre_ffn_norm, m_w_gate, m_w_up, m_w_down, m_post_ffn_norm, v_pre_mix_norm, v_w_in, v_sgu_ln_gain, v_sgu_ln_bias, v_sgu_w_spatial, v_sgu_b_spatial, v_attn_out_norm, v_sgu_out_norm, v_w_out, v_post_mix_norm, v_pre_ffn_norm, v_w_gate, v_w_up, v_w_down, v_post_ffn_norm):
    a = dict(locals())
    cx, cy, cc = _coords()
    s_me = 2 * cx + cy
    shard_core = jnp.stack([s_me, cc]).astype(jnp.int32)
    me = jnp.stack([4 * cx + 2 * cy + cc]).astype(jnp.int32)
    small = {n: (a[n][0] if a[n].ndim > 2 else a[n]) for n in SMALL}
    b_t = small["sgu_b_spatial"].T
    xs, target = x[0], loss_target[0]
    flipped = ("w_gate", "w_up")

    def big(name, n):
        return jnp.swapaxes(a[name], 1, 2)[0] if n in flipped else a[name][0]

    own = {"w_in": _halves(big("w_in", "w_in").astype(BF16))}

    def with_own(full, n):
        full = lax.dynamic_update_slice(full, own[n][None], (s_me, 0, 0, 0))
        return full.reshape((N_SHARD,) + big(n, n).shape)

    ffn = ("w_gate", "w_up", "w_down")
    g_in = _gather_ici([own["w_in"]])
    (s_in,), token = _split_starts("gather_in_start", [g_in], small["pre_mix_norm"])
    pos = positions.astype(F32).reshape(SEQ, 1) + token[0:1, 0:1]
    for n in ("w_out",) + ffn:
        own[n] = _halves((big(n, n) + token[0:1, 0:1]).astype(BF16))
    g_out, g_ffn = _gather_ici([own["w_out"]]), _gather_ici([own[n] for n in ffn])
    (s_out, s_ffn), token = _split_starts("gather_rest_start", [g_out, g_ffn], token)
    in_lands = _split_wait("gather_in_wait", g_in, *s_in, token)
    ((in_lands,),) = _comm_only("comm_pass_in", [_gather_pass(in_lands)])
    w_in_f = with_own(in_lands, "w_in")
    h, u, vs, sgu, *qkv = _inproj_fwd(xs, pos, small["pre_mix_norm"], w_in_f, small["sgu_ln_gain"],
                                      small["sgu_ln_bias"], small["sgu_w_spatial"], b_t)
    views = [tuple(qkv[3 * i:3 * i + 3]) for i in range(len(DILATIONS))]
    out_lands = _split_wait("gather_out_wait", g_out, *s_out, sgu)
    o_list, l_list = [], []
    for dil, (qv, kv, vv) in zip(DILATIONS, views):
        if dil == 1:
            (o, l), ((out_lands,),) = _attn_fwd(qv, kv, vv, dil, comms=[_gather_pass(out_lands)])
        else:
            o, l = _attn_fwd(qv, kv, vv, dil)
        o_list.append(o)
        l_list.append(l)
    w_out_f = with_own(out_lands, "w_out")
    ffn_lands = _split_wait("gather_ffn_wait", g_ffn, *s_ffn, l_list[-1])
    (attn, mixed, y, x1, *lses), (ffn_lands,) = _mix_out_fwd(
        o_list, l_list, sgu, xs, w_out_f, small["attn_out_norm"], small["sgu_out_norm"], small["post_mix_norm"],
        comms=[_gather_pass(ffn_lands)])
    w_gate_f, w_up_f, w_down_f = (with_own(f, n).reshape(FF, D_MODEL) for f, n in zip(ffn_lands, ffn))
    h2, act, dg, dup, df, dx1, loss_cols, d_pre_ffn, d_post_ffn = _ffn_fwd_bwd(
        x1, target, w_gate_f, w_up_f, w_down_f, small["pre_ffn_norm"], small["post_ffn_norm"])

    full_tok = pl.BlockSpec((SEQ, D_MODEL), lambda s: (0, 0), pipeline_mode=pl.Buffered(1))
    gw, gb = {}, {}
    gw["w_gate"], gw["w_up"], gb["w_gate"], gb["w_up"] = _wgrad_ff([dg, dup], h2, "wgrad_gate_up")
    gw["w_down"], gb["w_down"] = _wgrad_ff([act], df, "wgrad_down")
    (dy, du, dvs_sgu, d_post_mix, d_attn_norm, d_sgu_norm, d_w_sp, d_b_sp, d_ln_gain, d_ln_bias,
     *dviews), (sib_ffn,) = _outproj_bwd(
        dx1, y, attn, sgu, w_out_f, small["post_mix_norm"], small["attn_out_norm"], small["sgu_out_norm"],
        u, vs, small["sgu_ln_gain"], small["sgu_ln_bias"], small["sgu_w_spatial"], b_t,
        comms=[_rs_sibling([gb[n] for n in ffn])])
    sib = dict(zip(ffn, sib_ffn))
    part = dict(zip(ffn, _chip_sums([gb[n] for n in ffn], [sib[n] for n in ffn], shard_core)))
    gw["w_out"], gb["w_out"] = _wgrad(mixed, dy, pl.BlockSpec((SEQ, OUT_S), lambda s: (0, s)), full_tok,
                                      (OUT_S, D_MODEL), "wgrad_out")
    x_ffn = _rs_chips([part[n] for n in ffn])
    packed_early = _pack({
        "sgu_ln_gain": d_ln_gain, "sgu_ln_bias": d_ln_bias, "sgu_b_spatial": d_b_sp,
        "attn_out_norm": d_attn_norm, "sgu_out_norm": d_sgu_norm, "post_mix_norm": d_post_mix,
        "pre_ffn_norm": d_pre_ffn, "post_ffn_norm": d_post_ffn, LOSS_ROW: loss_cols}, SMALL_EARLY, 8)
    wsp_view = (SGU_GROUPS * CHUNK, CHUNK)
    packed_wsp = d_w_sp.reshape(wsp_view).astype(BF16)
    x_small, x_wsp = _small_exchange(packed_early), _small_exchange(packed_wsp)
    (s_ffn,), token = _split_starts("rs_ffn_start", [x_ffn], small["pre_mix_norm"])

    dqs, dks, dvs = [], [], []
    for i, (dil, (qv, kv, vv)) in enumerate(zip(DILATIONS, views)):
        if dil == 1:
            (dq, dk, dv), ((sib["w_out"],),) = _attn_bwd(qv, kv, vv, dviews[i], dviews[3 + i], lses[i], dil,
                                                        comms=[_rs_sibling([gb["w_out"]])], after=[token])
            (part["w_out"],) = _chip_sums([gb["w_out"]], [sib["w_out"]], shard_core)
            x_out = _rs_chips([part["w_out"]])
            (s_out, s_small, s_wsp), token = _split_starts("rs_out_small_start", [x_out, x_small, x_wsp], token)
        else:
            dq, dk, dv = _attn_bwd(qv, kv, vv, dviews[i], dviews[3 + i], lses[i], dil, after=[token])
        dqs.append(dq)
        dks.append(dk)
        dvs.append(dv)
    half, far, joined = {}, {}, {}
    far.update(zip(ffn, _split_wait("rs_ffn_wait", x_ffn, *s_ffn, dvs[-1])))
    half.update(zip(ffn, _final_sums([gw[n] for n in ffn], [sib[n] for n in ffn], [far[n] for n in ffn],
                                     shard_core)))
    dproj, grad_x, d_pre_mix = _inproj_bwd(dqs, dks, dvs, du, dvs_sgu, pos, xs, dx1, w_in_f, small["pre_mix_norm"])
    packed_late = _pack({"pre_mix_norm": d_pre_mix}, SMALL_LATE, 1)
    (gw["w_in"], gb["w_in"]), (got, (slots_late,)) = _wgrad(
        h, dproj, full_tok, pl.BlockSpec((SEQ, IN_S), lambda s: (0, s)), (D_MODEL, IN_S), "wgrad_in",
        comms=[_rs_join([half[n] for n in ffn]), _small_exchange(packed_late)])
    joined.update(zip(ffn, got))
    ((sib["w_in"],),) = _comm_only("comm_rs_sibling_in", [_rs_sibling([gb["w_in"]])])
    (part["w_in"],) = _chip_sums([gb["w_in"]], [sib["w_in"]], shard_core)
    x_in = _rs_chips([part["w_in"]])
    (s_in,), token = _split_starts("rs_in_start", [x_in], small["pre_mix_norm"])

    grads, deltas, new_m, new_v = {}, {}, {}, {}

    def record(n, outs):
        grads[n], deltas[n], new_m[n], new_v[n] = (
            jnp.swapaxes(o[None], 1, 2) if n in flipped else o[None] for o in outs)

    def update(names, name, after):
        for n, outs in zip(names, _adamw_multi(
                [big(n, n) for n in names], [joined[n].reshape(big(n, n).shape) for n in names],
                [big("m_" + n, n) for n in names], [big("v_" + n, n) for n in names], name, after)):
            record(n, outs)

    update(ffn, "adamw_ffn", [token])
    (far["w_out"],) = _split_wait("rs_out_wait", x_out, *s_out, new_v["w_down"])
    (slots_early,) = _split_wait("small_early_wait", x_small, *s_small, far["w_out"])
    (slots_wsp,) = _split_wait("small_wsp_wait", x_wsp, *s_wsp, slots_early)
    (far["w_in"],) = _split_wait("rs_in_wait", x_in, *s_in, slots_wsp)
    last = ("w_in", "w_out")
    half.update(zip(last, _final_sums([gw[n] for n in last], [sib[n] for n in last], [far[n] for n in last],
                                      shard_core)))
    (got,) = _comm_only("comm_rs_join_in_out", [_rs_join([half[n] for n in last])])
    joined.update(zip(last, got))
    update(last, "adamw_in_out", [])
    a[LOSS_ROW] = a["m_" + LOSS_ROW] = a["v_" + LOSS_ROW] = jnp.zeros((1, D_MODEL), F32)
    outs = _adamw_small(packed_wsp, slots_wsp, *[a[p + SMALL_WSP].reshape(wsp_view) for p in ("", "m_", "v_")], me)
    for dst, buf in zip((grads, deltas, new_m, new_v), outs):
        dst[SMALL_WSP] = buf.reshape(a[SMALL_WSP].shape)
    for names, rows, packed, slots in ((SMALL_EARLY, 8, packed_early, slots_early),
                                       (SMALL_LATE, 1, packed_late, slots_late)):
        outs = _adamw_small(packed, slots, _pack(a, names, rows), _pack({n: a["m_" + n] for n in names}, names, rows),
                            _pack({n: a["v_" + n] for n in names}, names, rows), me)
        for dst, buf in zip((grads, deltas, new_m, new_v), outs):
            dst.update(_unpack(buf, names, {n: a[n].shape for n in names}))
    loss = jnp.sum(grads[LOSS_ROW]) * np.float32(0.5 / D_MODEL)
    return (loss, grad_x[None], *[grads[n] for n in WEIGHTS], *[deltas[n] for n in WEIGHTS],
            *[new_m[n] for n in WEIGHTS], *[new_v[n] for n in WEIGHTS])
```

```python
import numpy as np
import jax
import jax.numpy as jnp
from jax import lax
from jax.experimental import pallas as pl
from jax.experimental.pallas import tpu as pltpu

F32 = jnp.float32
BF16 = jnp.bfloat16

SEQ = 2048
D_MODEL = 1024
HEAD_DIM = 64
ATTN_W = 512
SGU_W = 512
SGU_GROUPS = 8
CHUNK = 128
DILATIONS = (1, 4, 16)
N_SHARD = 4
IN_S = 640
OUT_S = 256
FF_S = 704
PROJ_W = N_SHARD * IN_S
FF = N_SHARD * FF_S
FF_CHUNKS = ((0, 1024), (1024, 2048), (2048, FF))
RMS_EPS = 1e-6
LN_EPS = 1e-5
ROPE_THETA = 500000.0
ATTN_SCALE = 1.0 / np.sqrt(HEAD_DIM)
NEG = -1e30
TM = 512
TM_FFN = 256
VMEM_LIMIT = 56 * 1024 * 1024

ADAM_LR = 0.001
ADAM_B1 = 0.9
ADAM_B2 = 0.999
ADAM_EPS = 1e-08
ADAM_WD = 0.01
ADAM_STEP = 10

MESH = pl.DeviceIdType.MESH
ANY = pl.BlockSpec(memory_space=pl.ANY)


def _dot(a, b):
    return jnp.dot(a, b, preferred_element_type=F32)


def _dot_nt(a, b):
    return lax.dot_general(a, b, (((1,), (1,)), ((), ())), preferred_element_type=F32)


def _dot_tn(a, b):
    return lax.dot_general(a, b, (((0,), (0,)), ((), ())), preferred_element_type=F32)


def _dot_exact(a, b):
    return jnp.dot(a, b, preferred_element_type=F32, precision=lax.Precision.HIGHEST)


def _dot_select(a, sel):
    hi = a.astype(BF16)
    lo = (a - hi.astype(F32)).astype(BF16)
    sel = sel.astype(BF16)
    return _dot(hi, sel) + _dot(lo, sel)


def _rms_stats(x):
    r = lax.rsqrt(jnp.mean(x * x, axis=-1, keepdims=True) + RMS_EPS)
    return x * r, r


def _rms_bwd(xh, r, gain, dy):
    dxh = dy * gain
    dx = r * (dxh - xh * jnp.mean(dxh * xh, axis=-1, keepdims=True))
    return dx, jnp.sum(dy * xh, axis=0, keepdims=True)


_ERF_ALPHA = (-2.72614225801306e-10, 2.77068142495902e-08, -2.10102402082508e-06, -5.69250639462346e-05,
              -7.34990630326855e-04, -2.95459980854025e-03, -1.60960333262415e-02)
_ERF_BETA = (-1.45660718464996e-05, -2.13374055278905e-04, -1.68282697438203e-03, -7.37332916720468e-03,
             -1.42647390514189e-02)


def _erf(x):
    x = jnp.clip(x, -4.0, 4.0)
    x2 = x * x
    p = jnp.full_like(x, _ERF_ALPHA[0])
    for a in _ERF_ALPHA[1:]:
        p = p * x2 + a
    q = jnp.full_like(x, _ERF_BETA[0])
    for b in _ERF_BETA[1:]:
        q = q * x2 + b
    return x * p / q


def _normal_cdf(x):
    return 0.5 * (1.0 + _erf(x * np.float32(1.0 / np.sqrt(2.0))))


def _gelu_grad(x, cdf):
    pdf = jnp.exp(-0.5 * x * x) * np.float32(1.0 / np.sqrt(2.0 * np.pi))
    return cdf + x * pdf


def _sigmoid(x):
    return 1.0 / (1.0 + jnp.exp(-x))


_INV_FREQ = tuple(float(np.float32(ROPE_THETA ** (-2.0 * j / 16.0))) for j in range(8))


def _rot_tables(pos):
    lane = lax.broadcasted_iota(jnp.int32, (1, 128), 1)
    d = lane & 63
    j = d & 7
    inv = jnp.zeros((1, 128), F32)
    for jj in range(8):
        inv = jnp.where(j == jj, _INV_FREQ[jj], inv)
    ang = pos.astype(F32) * inv
    c = jnp.cos(ang)
    s = jnp.sin(ang)
    cos_t = jnp.where(d < 16, c, 1.0)
    sin_a = jnp.where(d < 8, -s, 0.0)
    sin_b = jnp.where((d >= 8) & (d < 16), s, 0.0)
    return tuple(jnp.tile(t, (1, 4)) for t in (cos_t, sin_a, sin_b))


def _rope(x, tabs):
    cos_t, sin_a, sin_b = tabs
    return x * cos_t + pltpu.roll(x, 504, 1) * sin_a + pltpu.roll(x, 8, 1) * sin_b


def _rope_bwd(dy, tabs):
    cos_t, sin_a, sin_b = tabs
    return dy * cos_t + pltpu.roll(dy * sin_a, 8, 1) + pltpu.roll(dy * sin_b, 504, 1)


def _left_half():
    return lax.broadcasted_iota(jnp.int32, (CHUNK, CHUNK), 1) < HEAD_DIM


def _group_ones():
    lane = lax.broadcasted_iota(jnp.int32, (SGU_GROUPS, SGU_W), 1)
    row = lax.broadcasted_iota(jnp.int32, (SGU_GROUPS, SGU_W), 0)
    return ((lane >> 6) == row).astype(F32)


def _masked_spatial(w_ref):
    row = lax.broadcasted_iota(jnp.int32, (CHUNK, CHUNK), 0)
    col = lax.broadcasted_iota(jnp.int32, (CHUNK, CHUNK), 1)
    return [jnp.where(col <= row, w_ref[g], 0.0).astype(BF16) for g in range(SGU_GROUPS)]


def _sgu_core(u, vs, lg, lb, wm, bias_full):
    tm = u.shape[0]
    cdf_u, cdf_vs = _normal_cdf(u), _normal_cdf(vs)
    gu = u * cdf_u
    gv = vs * cdf_vs
    mu = jnp.mean(gv, axis=-1, keepdims=True)
    xc = gv - mu
    rstd = lax.rsqrt(jnp.mean(xc * xc, axis=-1, keepdims=True) + LN_EPS)
    xh = xc * rstd
    vnb = (xh * lg + lb).astype(BF16)
    left = _left_half()
    rows = []
    for c in range(tm // CHUNK):
        pieces = []
        for p in range(4):
            vp = vnb[c * CHUNK:(c + 1) * CHUNK, p * 128:(p + 1) * 128]
            pieces.append(jnp.where(left, _dot(wm[2 * p], vp), _dot(wm[2 * p + 1], vp)))
        rows.append(jnp.concatenate(pieces, axis=1) + bias_full)
    mixed = jnp.concatenate(rows, axis=0)
    return gu, xh, rstd, vnb, mixed, cdf_u, cdf_vs


def _resident(shape):
    n = len(shape)
    return pl.BlockSpec(shape, lambda *_: (0,) * n, pipeline_mode=pl.Buffered(1))


def _rows(ncol, tm=TM):
    return pl.BlockSpec((tm, ncol), lambda i: (i, 0))


def _acc(ncol, nrow=1):
    return pl.BlockSpec((nrow, ncol), lambda i: (0, 0))


HEAD_W = 128


def _view_rows(dil, width=ATTN_W, tm=TM):
    return pl.BlockSpec((tm // dil, dil * width), lambda i: (i, 0))


def _view_shape(dil, dtype, width=ATTN_W):
    return _sds((SEQ // dil, dil * width), dtype)


def _slab_scratch():
    return pltpu.VMEM((4, TM, 128), F32)


def _store_view(val, out_ref, slabs, dil):
    width = val.shape[1]
    for j in range(width // 128):
        slabs[j] = val[:, j * 128:(j + 1) * 128]
    for r in range(dil):
        for j in range(width // 128):
            c0 = r * width + j * 128
            out_ref[:, c0:c0 + 128] = slabs.at[j][pl.ds(r, TM // dil, stride=dil), :].astype(out_ref.dtype)


def _load_view(in_ref, slabs, dil, width=ATTN_W):
    for r in range(dil):
        for j in range(width // 128):
            c0 = r * width + j * 128
            slabs.at[j][pl.ds(r, TM // dil, stride=dil), :] = in_ref[:, c0:c0 + 128].astype(F32)
    return jnp.concatenate([slabs[j] for j in range(width // 128)], axis=1)


def _head_spread():
    m = lax.broadcasted_iota(jnp.int32, (HEAD_W, ATTN_W), 0)
    lane = lax.broadcasted_iota(jnp.int32, (HEAD_W, ATTN_W), 1)
    return (m == 16 * (lane >> 6)).astype(F32)


def _head_sum():
    lane = lax.broadcasted_iota(jnp.int32, (ATTN_W, HEAD_W), 0)
    m = lax.broadcasted_iota(jnp.int32, (ATTN_W, HEAD_W), 1)
    return ((lane >> 6) == (m >> 4)).astype(F32)


def _seq_params():
    return pltpu.CompilerParams(dimension_semantics=("arbitrary",), vmem_limit_bytes=VMEM_LIMIT)


def _sds(shape, dtype):
    return jax.ShapeDtypeStruct(shape, dtype)


class _Comm:
    def __init__(self, args, out_shape, n_sems, start, finish, aliased=False):
        self.args, self.out_shape, self.n_sems = list(args), list(out_shape), n_sems
        self.start, self.finish, self.aliased = start, finish, aliased


def _pcall(body, *, name, grid, in_specs, out_specs, out_shape, args, scratch_shapes=(), comms=(), after=()):
    single = not isinstance(out_shape, (list, tuple))
    out_specs = [out_specs] if single else list(out_specs)
    out_shape = [out_shape] if single else list(out_shape)
    n_in, n_out, n_scr = len(in_specs), len(out_shape), len(scratch_shapes)
    c_args = [a for c in comms for a in c.args]
    c_outs = [o for c in comms for o in c.out_shape]
    aliases, ai, ao = {}, n_in, n_out
    for c in comms:
        if c.aliased:
            aliases.update({ai + k: ao + k for k in range(len(c.args))})
        ai += len(c.args)
        ao += len(c.out_shape)
    sems = [pltpu.SemaphoreType.DMA((c.n_sems,)) for c in comms for _ in range(2)]
    steps = grid[0]

    def wrapped(*refs):
        o0 = n_in + len(c_args) + len(after)
        s0 = o0 + n_out + len(c_outs)
        m_in, m_out, m_sem = refs[n_in:n_in + len(c_args)], refs[o0 + n_out:s0], refs[s0 + n_scr:]

        def each(phase):
            ii = oi = 0
            for k, c in enumerate(comms):
                getattr(c, phase)(m_in[ii:ii + len(c.args)], m_out[oi:oi + len(c.out_shape)],
                                  m_sem[2 * k], m_sem[2 * k + 1])
                ii += len(c.args)
                oi += len(c.out_shape)

        if comms:
            @pl.when(pl.program_id(0) == 0)
            def _():
                each("start")

        body(*refs[:n_in], *refs[o0:o0 + n_out], *refs[s0:s0 + n_scr])

        if comms:
            @pl.when(pl.program_id(0) == steps - 1)
            def _():
                each("finish")

    res = pl.pallas_call(
        wrapped, name=name, grid=grid,
        in_specs=list(in_specs) + [ANY] * (len(c_args) + len(after)), out_specs=out_specs + [ANY] * len(c_outs),
        out_shape=out_shape + c_outs, scratch_shapes=list(scratch_shapes) + sems,
        input_output_aliases=aliases, compiler_params=_seq_params(),
    )(*args, *c_args, *after)
    mine = res[0] if single else list(res[:n_out])
    if not comms:
        return mine
    theirs, oi = [], n_out
    for c in comms:
        theirs.append(list(res[oi:oi + len(c.out_shape)]))
        oi += len(c.out_shape)
    return mine, theirs


def _in_pieces(g):
    lo, hi = 512 * g, 512 * (g + 1)
    return [(s, max(lo, IN_S * s) - IN_S * s, min(hi, IN_S * (s + 1)) - IN_S * s)
            for s in range(N_SHARD) if max(lo, IN_S * s) < min(hi, IN_S * (s + 1))]


def _inproj_fwd(x, pos, g_pre, w_in, lg, lb, w_sp, b_t, comms=()):
    def body(x_ref, pos_ref, g_ref, w_ref, lg_ref, lb_ref, wsp_ref, bt_ref, h_ref, u_ref, vs_ref, sgu_ref, *rest):
        qkv_refs, slabs = rest[:9], rest[9:]
        xh, _ = _rms_stats(x_ref[...])
        h = (xh * g_ref[...]).astype(BF16)
        h_ref[...] = h
        tabs = _rot_tables(pos_ref[...])

        def group(g):
            return jnp.concatenate([_dot(h, w_ref[s, :, a:b]) for s, a, b in _in_pieces(g)], axis=1)

        for t in range(3):
            val = group(t)
            if t < 2:
                val = _rope(val, tabs)
            if t == 0:
                val = val * np.float32(ATTN_SCALE)
            qkv_refs[t][...] = val.astype(BF16)
            for i, dil in enumerate(DILATIONS[1:]):
                _store_view(val, qkv_refs[3 * (i + 1) + t], slabs[t], dil)
        u = group(3)
        vs = group(4)
        u_ref[...] = u
        vs_ref[...] = vs
        bias_full = _dot_exact(bt_ref[...], _group_ones())
        gu, _, _, _, mixed, _, _ = _sgu_core(u, vs, lg_ref[...], lb_ref[...], _masked_spatial(wsp_ref), bias_full)
        sgu_ref[...] = gu * mixed

    return _pcall(
        body, name="inproj_sgu_fwd", grid=(SEQ // TM,),
        in_specs=[_rows(D_MODEL), _rows(1), _resident((1, D_MODEL)), _resident((N_SHARD, D_MODEL, IN_S)),
                  _resident((1, SGU_W)), _resident((1, SGU_W)), _resident((SGU_GROUPS, CHUNK, CHUNK)),
                  _resident((CHUNK, SGU_GROUPS))],
        out_specs=[_rows(D_MODEL), _rows(512), _rows(512), _rows(512)]
        + [_view_rows(dil) for dil in DILATIONS for _ in range(3)],
        out_shape=[_sds((SEQ, D_MODEL), BF16), _sds((SEQ, 512), F32), _sds((SEQ, 512), F32), _sds((SEQ, 512), F32)]
        + [_view_shape(dil, BF16) for dil in DILATIONS for _ in range(3)],
        scratch_shapes=[_slab_scratch() for _ in range(3)],
        args=(x, pos, g_pre, w_in, lg, lb, w_sp, b_t), comms=comms)


def _block_masks():
    row = lax.broadcasted_iota(jnp.int32, (CHUNK, CHUNK), 0)
    col = lax.broadcasted_iota(jnp.int32, (CHUNK, CHUNK), 1)
    return col <= row, col >= row


def _attn_fwd(qv, kv, vv, dil, comms=()):
    seg = SEQ // dil
    nblk = seg // CHUNK
    rps = 4 if nblk == 1 else 1

    def body(q_ref, k_ref, v_ref, o_ref, l_ref):
        left = _left_half()
        m_cur, m_prev = _block_masks()
        zero = jnp.zeros((CHUNK, CHUNK), BF16)
        ones = (jnp.where(left, 1.0, 0.0).astype(BF16), jnp.where(left, 0.0, 1.0).astype(BF16))

        sides = tuple(enumerate((left, ~left)))

        def rows(b):
            if isinstance(b, int):
                return b * CHUNK, max(b - 1, 0) * CHUNK
            return pl.multiple_of(b * CHUNK, CHUNK), pl.multiple_of(jnp.maximum(b - 1, 0) * CHUNK, CHUNK)

        def first(rr, b):
            r0, rp = rows(b)
            prev_ok = m_prev & (b > 0)
            tiles, scores = [], []
            for hp in range(4):
                ls = slice(rr * ATTN_W + hp * 128, rr * ATTN_W + (hp + 1) * 128)
                qp = q_ref[pl.ds(r0, CHUNK), ls]
                kc = k_ref[pl.ds(r0, CHUNK), ls]
                kp = k_ref[pl.ds(rp, CHUNK), ls] if nblk > 1 else None
                tiles.append((ls, v_ref[pl.ds(r0, CHUNK), ls], v_ref[pl.ds(rp, CHUNK), ls] if nblk > 1 else None))
                for _, hm in sides:
                    qh = jnp.where(hm, qp, zero)
                    sc = jnp.where(m_cur, _dot_nt(qh, kc), NEG)
                    sp = jnp.where(prev_ok, _dot_nt(qh, kp), NEG) if nblk > 1 else None
                    scores.append((sc, sp))
            return tiles, scores

        def second(scores):
            probs = []
            for sc, sp in scores:
                if nblk > 1:
                    m = jnp.max(jnp.maximum(sc, sp), axis=-1, keepdims=True)
                    pc = jnp.exp(sc - m)
                    pp = jnp.exp(sp - m)
                    probs.append((m, pc.astype(BF16), pp.astype(BF16), (pc + pp).astype(BF16)))
                else:
                    m = jnp.max(sc, axis=-1, keepdims=True)
                    pc = jnp.exp(sc - m).astype(BF16)
                    probs.append((m, pc, None, pc))
            return probs

        def third(rr, b, tiles, probs):
            r0, _ = rows(b)
            for hp, (ls, vc, vp) in enumerate(tiles):
                acc = jnp.zeros((CHUNK, CHUNK), F32)
                den = jnp.zeros((CHUNK, CHUNK), F32)
                for side, hm in sides:
                    _, pc, pp, psum = probs[2 * hp + side]
                    acc = acc + _dot(pc, jnp.where(hm, vc, zero))
                    if nblk > 1:
                        acc = acc + _dot(pp, jnp.where(hm, vp, zero))
                    den = den + _dot(psum, ones[side])
                o_ref[pl.ds(r0, CHUNK), ls] = (acc / den).astype(o_ref.dtype)
                lse = jnp.where(left, probs[2 * hp][0], probs[2 * hp + 1][0]) + jnp.log(den)
                l_ref[pl.ds(r0, CHUNK), rr * HEAD_W + 32 * hp:rr * HEAD_W + 32 * hp + 32] = lse[:, 48:80]

        def run(units):
            data = [first(rr, b) for rr, b in units]
            probs = [second(scores) for _, scores in data]
            for (rr, b), (tiles, _), pr in zip(units, data, probs):
                third(rr, b, tiles, pr)

        if nblk == 1:
            run([(rr, 0) for rr in range(rps)])
        else:
            def one(b, carry):
                run([(0, b)])
                return carry

            lax.fori_loop(0, nblk, one, 0)

    spec = pl.BlockSpec((seg, rps * ATTN_W), lambda r: (0, r))
    return _pcall(
        body, name=f"attn_fwd_d{dil}", grid=(dil // rps,),
        in_specs=[spec, spec, spec], out_specs=[spec, pl.BlockSpec((seg, rps * HEAD_W), lambda r: (0, r))],
        out_shape=[_sds((seg, dil * ATTN_W), BF16), _sds((seg, dil * HEAD_W), F32)],
        args=(qv, kv, vv), comms=comms)


def _lane_left(nrows):
    return lax.broadcasted_iota(jnp.int32, (nrows, CHUNK), 1) < HEAD_DIM


def _attn_rows(b):
    if isinstance(b, int):
        return b * CHUNK, max(b - 1, 0) * CHUNK
    return pl.multiple_of(b * CHUNK, CHUNK), pl.multiple_of(jnp.maximum(b - 1, 0) * CHUNK, CHUNK)


def _mix_out_fwd(o_list, l_list, sgu, x, w_out, g_attn, g_sgu, g_post, comms=()):
    def body(o1, o2, o3, l1, l2, l3, sgu_ref, x_ref, w_ref, ga_ref, gs_ref, gp_ref,
             attn_ref, mixed_ref, y_ref, x1_ref, lse1_ref, lse2_ref, lse3_ref, slabs_a, slabs_b):
        os = [o1[...], _load_view(o2, slabs_a, DILATIONS[1]), _load_view(o3, slabs_b, DILATIONS[2])]
        ls = [l1[...], _load_view(l2, slabs_a, DILATIONS[1], HEAD_W), _load_view(l3, slabs_b, DILATIONS[2], HEAD_W)]
        m = jnp.maximum(jnp.maximum(ls[0], ls[1]), ls[2])
        es = [jnp.exp(l - m) for l in ls]
        den = es[0] + es[1] + es[2]
        spread = _head_spread()
        attn = sum(_dot_select(e / den, spread) * o for e, o in zip(es, os))
        attn_ref[...] = attn
        lse = m + jnp.log(den)
        lse1_ref[...] = lse
        _store_view(lse, lse2_ref, slabs_a, DILATIONS[1])
        _store_view(lse, lse3_ref, slabs_b, DILATIONS[2])
        ah, _ = _rms_stats(attn)
        sh, _ = _rms_stats(sgu_ref[...])
        mixed = jnp.concatenate([ah * ga_ref[...], sh * gs_ref[...]], axis=1).astype(BF16)
        mixed_ref[...] = mixed
        y = _dot(mixed[:, 0:OUT_S], w_ref[0])
        for s in range(1, N_SHARD):
            y = y + _dot(mixed[:, s * OUT_S:(s + 1) * OUT_S], w_ref[s])
        y_ref[...] = y
        yh, _ = _rms_stats(y)
        x1_ref[...] = x_ref[...] + yh * gp_ref[...]

    return _pcall(
        body, name="mix_out_fwd", grid=(SEQ // TM,),
        in_specs=[_view_rows(dil) for dil in DILATIONS] + [_view_rows(dil, HEAD_W) for dil in DILATIONS]
        + [_rows(512), _rows(D_MODEL), _resident((N_SHARD, OUT_S, D_MODEL)),
           _resident((1, 512)), _resident((1, 512)), _resident((1, D_MODEL))],
        out_specs=[_rows(512), _rows(D_MODEL), _rows(D_MODEL), _rows(D_MODEL)]
        + [_view_rows(dil, HEAD_W) for dil in DILATIONS],
        out_shape=[_sds((SEQ, 512), F32), _sds((SEQ, D_MODEL), BF16), _sds((SEQ, D_MODEL), F32),
                   _sds((SEQ, D_MODEL), F32)] + [_view_shape(dil, F32, HEAD_W) for dil in DILATIONS],
        scratch_shapes=[_slab_scratch(), _slab_scratch()],
        args=(*o_list, *l_list, sgu, x, w_out, g_attn, g_sgu, g_post), comms=comms)


def _ffn_fwd_bwd(x1, target, w_gate, w_up, w_down, g_pre, g_post, comms=()):
    def body(x1_ref, t_ref, wg_ref, wu_ref, wd_ref, gpf_ref, gpo_ref,
             h2_ref, a_ref, dg_ref, dup_ref, df_ref, dx1_ref, loss_ref, dgpf_ref, dgpo_ref, g_scr, up_scr):
        @pl.when(pl.program_id(0) == 0)
        def _():
            loss_ref[...] = jnp.zeros_like(loss_ref)
            dgpf_ref[...] = jnp.zeros_like(dgpf_ref)
            dgpo_ref[...] = jnp.zeros_like(dgpo_ref)

        x1 = x1_ref[...]
        gpf = gpf_ref[...]
        gpo = gpo_ref[...]
        xh, r = _rms_stats(x1)
        h2 = (xh * gpf).astype(BF16)
        h2_ref[...] = h2
        f = jnp.zeros((TM_FFN, D_MODEL), F32)
        for c0, c1 in FF_CHUNKS:
            g = _dot_nt(h2, wg_ref[c0:c1, :])
            up = _dot_nt(h2, wu_ref[c0:c1, :])
            g_scr[:, c0:c1] = g
            up_scr[:, c0:c1] = up
            a = (g * _sigmoid(g) * up).astype(BF16)
            a_ref[:, c0:c1] = a
            f = f + _dot(a, wd_ref[c0:c1, :])
        fh, rf = _rms_stats(f)
        diff = x1 + fh * gpo - t_ref[...]
        loss_ref[...] += jnp.sum(diff * diff, axis=0, keepdims=True)
        dout = diff * np.float32(1.0 / D_MODEL)
        df, dgpo = _rms_bwd(fh, rf, gpo, dout)
        dgpo_ref[...] += dgpo
        dfb = df.astype(BF16)
        df_ref[...] = dfb
        dh2 = jnp.zeros((TM_FFN, D_MODEL), F32)
        for c0, c1 in FF_CHUNKS:
            da = _dot_nt(dfb, wd_ref[c0:c1, :])
            g = g_scr[:, c0:c1]
            up = up_scr[:, c0:c1]
            sg = _sigmoid(g)
            dup = (da * (g * sg)).astype(BF16)
            dg = (da * up * (sg * (1.0 + g * (1.0 - sg)))).astype(BF16)
            dg_ref[:, c0:c1] = dg
            dup_ref[:, c0:c1] = dup
            dh2 = dh2 + _dot(dg, wg_ref[c0:c1, :]) + _dot(dup, wu_ref[c0:c1, :])
        dx, dgpf = _rms_bwd(xh, r, gpf, dh2)
        dgpf_ref[...] += dgpf
        dx1_ref[...] = dout + dx

    return _pcall(
        body, name="ffn_fwd_bwd", grid=(SEQ // TM_FFN,),
        in_specs=[_rows(D_MODEL, TM_FFN), _rows(D_MODEL, TM_FFN), _resident((FF, D_MODEL)),
                  _resident((FF, D_MODEL)), _resident((FF, D_MODEL)),
                  _resident((1, D_MODEL)), _resident((1, D_MODEL))],
        out_specs=[_rows(D_MODEL, TM_FFN), _rows(FF, TM_FFN), _rows(FF, TM_FFN), _rows(FF, TM_FFN),
                   _rows(D_MODEL, TM_FFN), _rows(D_MODEL, TM_FFN), _acc(D_MODEL), _acc(D_MODEL), _acc(D_MODEL)],
        out_shape=[_sds((SEQ, D_MODEL), BF16), _sds((SEQ, FF), BF16), _sds((SEQ, FF), BF16),
                   _sds((SEQ, FF), BF16), _sds((SEQ, D_MODEL), BF16), _sds((SEQ, D_MODEL), F32),
                   _sds((1, D_MODEL), F32), _sds((1, D_MODEL), F32), _sds((1, D_MODEL), F32)],
        scratch_shapes=[pltpu.VMEM((TM_FFN, FF), F32), pltpu.VMEM((TM_FFN, FF), F32)],
        args=(x1, target, w_gate, w_up, w_down, g_pre, g_post), comms=comms)


def _wgrad(a, b, a_spec, b_spec, out_block, name, comms=()):
    def body(a_ref, b_ref, o_ref, ob_ref):
        av = a_ref[0] if len(a_ref.shape) == 3 else a_ref[...]
        bv = b_ref[0] if len(b_ref.shape) == 3 else b_ref[...]
        g = _dot_tn(av, bv)
        o_ref[0] = g
        ob_ref[0] = g.astype(BF16)

    return _pcall(
        body, name=name, grid=(N_SHARD,),
        in_specs=[a_spec, b_spec],
        out_specs=[pl.BlockSpec((1,) + out_block, lambda s: (s, 0, 0))] * 2,
        out_shape=[_sds((N_SHARD,) + out_block, F32), _sds((N_SHARD,) + out_block, BF16)],
        args=(a, b), comms=comms)


def _outproj_bwd(dx1, y, attn, sgu, w_out, g_post, g_attn, g_sgu, u, vs, lg, lb, w_sp, b_t, comms=(), after=()):
    sgu_bwd = _sgu_bwd_body()

    def body(dx1_ref, y_ref, attn_ref, sgu_ref, w_ref, gp_ref, ga_ref, gs_ref, u_ref, vs_ref, lg_ref, lb_ref,
             wsp_ref, bt_ref, dy_ref, du_ref, dvs_ref, dgp_ref, dga_ref, dgs_ref, dw_ref, db_ref, dlg_ref, dlb_ref,
             *rest):
        dattn_refs, delta_refs, (slabs_a, slabs_b, dsgu_ref, dbias_scr) = rest[0:3], rest[3:6], rest[6:]

        @pl.when(pl.program_id(0) == 0)
        def _():
            dgp_ref[...] = jnp.zeros_like(dgp_ref)
            dga_ref[...] = jnp.zeros_like(dga_ref)
            dgs_ref[...] = jnp.zeros_like(dgs_ref)

        yh, ry = _rms_stats(y_ref[...])
        dy, dgp = _rms_bwd(yh, ry, gp_ref[...], dx1_ref[...])
        dgp_ref[...] += dgp
        dyb = dy.astype(BF16)
        dy_ref[...] = dyb
        dmixed = jnp.concatenate([_dot_nt(dyb, w_ref[s]) for s in range(N_SHARD)], axis=1)
        attn = attn_ref[...]
        ah, ra = _rms_stats(attn)
        dattn, dga = _rms_bwd(ah, ra, ga_ref[...], dmixed[:, 0:512])
        dga_ref[...] += dga
        sh, rs = _rms_stats(sgu_ref[...])
        dsgu, dgs = _rms_bwd(sh, rs, gs_ref[...], dmixed[:, 512:1024])
        dgs_ref[...] += dgs
        dsgu_ref[...] = dsgu
        delta = _dot_select(dattn * attn, _head_sum())
        dattn_refs[0][...] = dattn.astype(BF16)
        delta_refs[0][...] = delta
        for i, dil in enumerate(DILATIONS[1:]):
            _store_view(dattn, dattn_refs[i + 1], slabs_a, dil)
            _store_view(delta, delta_refs[i + 1], slabs_b, dil)
        sgu_bwd(u_ref, vs_ref, dsgu_ref, lg_ref, lb_ref, wsp_ref, bt_ref,
                du_ref, dvs_ref, dw_ref, db_ref, dlg_ref, dlb_ref, dbias_scr)

    return _pcall(
        body, name="outproj_sgu_bwd", grid=(SEQ // TM,),
        in_specs=[_rows(D_MODEL), _rows(D_MODEL), _rows(512), _rows(512), _resident((N_SHARD, OUT_S, D_MODEL)),
                  _resident((1, D_MODEL)), _resident((1, 512)), _resident((1, 512)),
                  _rows(SGU_W), _rows(SGU_W), _resident((1, SGU_W)), _resident((1, SGU_W)),
                  _resident((SGU_GROUPS, CHUNK, CHUNK)), _resident((CHUNK, SGU_GROUPS))],
        out_specs=[_rows(D_MODEL), _rows(SGU_W), _rows(SGU_W), _acc(D_MODEL), _acc(512), _acc(512),
                   pl.BlockSpec((SGU_GROUPS, CHUNK, CHUNK), lambda i: (0, 0, 0)), _acc(CHUNK, SGU_GROUPS),
                   _acc(SGU_W), _acc(SGU_W)]
        + [_view_rows(dil) for dil in DILATIONS] + [_view_rows(dil, HEAD_W) for dil in DILATIONS],
        out_shape=[_sds((SEQ, D_MODEL), BF16), _sds((SEQ, SGU_W), BF16), _sds((SEQ, SGU_W), BF16),
                   _sds((1, D_MODEL), F32), _sds((1, 512), F32), _sds((1, 512), F32),
                   _sds((SGU_GROUPS, CHUNK, CHUNK), F32), _sds((SGU_GROUPS, CHUNK), F32),
                   _sds((1, SGU_W), F32), _sds((1, SGU_W), F32)]
        + [_view_shape(dil, BF16) for dil in DILATIONS] + [_view_shape(dil, F32, HEAD_W) for dil in DILATIONS],
        scratch_shapes=[_slab_scratch(), _slab_scratch(), pltpu.VMEM((TM, SGU_W), F32),
                        pltpu.VMEM((CHUNK, SGU_W), F32)],
        args=(dx1, y, attn, sgu, w_out, g_post, g_attn, g_sgu, u, vs, lg, lb, w_sp, b_t), comms=comms, after=after)


def _sgu_bwd_body():
    nsteps = SEQ // TM

    def body(u_ref, vs_ref, ds_ref, lg_ref, lb_ref, w_ref, bt_ref,
             du_ref, dvs_ref, dw_ref, db_ref, dlg_ref, dlb_ref, dbias_scr):
        i = pl.program_id(0)

        @pl.when(i == 0)
        def _():
            dw_ref[...] = jnp.zeros_like(dw_ref)
            dlg_ref[...] = jnp.zeros_like(dlg_ref)
            dlb_ref[...] = jnp.zeros_like(dlb_ref)
            dbias_scr[...] = jnp.zeros_like(dbias_scr)

        wm = _masked_spatial(w_ref)
        ones_g = _group_ones()
        bias_full = _dot_exact(bt_ref[...], ones_g)
        u = u_ref[...]
        vs = vs_ref[...]
        lg = lg_ref[...]
        gu, xh, rstd, vnb, mixed, cdf_u, cdf_vs = _sgu_core(u, vs, lg, lb_ref[...], wm, bias_full)
        dsgu = ds_ref[...]
        du_ref[...] = (dsgu * mixed * _gelu_grad(u, cdf_u)).astype(BF16)
        dmixed = dsgu * gu
        left = _left_half()
        dvn_rows = []
        for c in range(TM // CHUNK):
            rs = slice(c * CHUNK, (c + 1) * CHUNK)
            dm_c = dmixed[rs, :]
            dbias_scr[...] += dm_c
            pieces = []
            for p in range(4):
                ls = slice(p * 128, (p + 1) * 128)
                dmp = dm_c[:, ls]
                vp = vnb[rs, ls]
                dmb = dmp.astype(BF16)
                zero = jnp.zeros_like(dmb)
                dw_ref[2 * p] += _dot_nt(jnp.where(left, dmb, zero), vp)
                dw_ref[2 * p + 1] += _dot_nt(jnp.where(left, zero, dmb), vp)
                pieces.append(jnp.where(left, _dot_tn(wm[2 * p], dmb), _dot_tn(wm[2 * p + 1], dmb)))
            dvn_rows.append(jnp.concatenate(pieces, axis=1))
        dvn = jnp.concatenate(dvn_rows, axis=0)
        dlg_ref[...] += jnp.sum(dvn * xh, axis=0, keepdims=True)
        dlb_ref[...] += jnp.sum(dvn, axis=0, keepdims=True)
        dxh = dvn * lg
        dgv = rstd * (dxh - jnp.mean(dxh, axis=-1, keepdims=True) - xh * jnp.mean(dxh * xh, axis=-1, keepdims=True))
        dvs_ref[...] = (dgv * _gelu_grad(vs, cdf_vs)).astype(BF16)

        @pl.when(i == nsteps - 1)
        def _():
            row = lax.broadcasted_iota(jnp.int32, (CHUNK, CHUNK), 0)
            col = lax.broadcasted_iota(jnp.int32, (CHUNK, CHUNK), 1)
            for g in range(SGU_GROUPS):
                dw_ref[g] = jnp.where(col <= row, dw_ref[g], 0.0)
            db_ref[...] = lax.dot_general(ones_g, dbias_scr[...], (((1,), (1,)), ((), ())),
                                          preferred_element_type=F32, precision=lax.Precision.HIGHEST)

    return body


def _attn_bwd(qv, kv, vv, dov, deltav, lsev, dil, comms=(), after=()):
    seg = SEQ // dil
    nblk = seg // CHUNK
    rps = 4 if nblk == 1 else 1

    def body(q_ref, k_ref, v_ref, do_ref, dl_ref, lse_ref, dq_ref, dk_ref, dv_ref, dk_wait, dv_wait):
        left = _left_half()
        m_cur, m_prev = _block_masks()
        sides = tuple(enumerate((left, ~left)))
        zero = jnp.zeros((CHUNK, CHUNK), BF16)

        def first(rr, b, both):
            r0, rp = _attn_rows(b)
            keys = pl.ds(rp, 2 * CHUNK) if both else pl.ds(r0, CHUNK)
            tiles, heads = [], []
            for hp in range(4):
                ls = slice(rr * ATTN_W + hp * 128, rr * ATTN_W + (hp + 1) * 128)
                qp = q_ref[pl.ds(r0, CHUNK), ls]
                dop = do_ref[pl.ds(r0, CHUNK), ls]
                k2 = k_ref[keys, ls]
                v2 = v_ref[keys, ls]
                tiles.append((ls, k2))
                for _, hm in sides:
                    qh = jnp.where(hm, qp, zero)
                    doh = jnp.where(hm, dop, zero)
                    heads.append((qh, doh, _dot_nt(k2, qh), _dot_nt(v2, doh)))
            return tiles, heads

        def second(rr, b, heads, both):
            r0, _ = _attn_rows(b)
            ok = jnp.concatenate([m_cur, m_prev], axis=0) if both else m_prev
            lanes = slice(rr * HEAD_W, (rr + 1) * HEAD_W)
            lse_t = lse_ref[pl.ds(r0, CHUNK), lanes].T
            dl_t = dl_ref[pl.ds(r0, CHUNK), lanes].T
            out = []
            for i, (_, _, s_t, dp_t) in enumerate(heads):
                p = jnp.exp(jnp.where(ok, s_t - lse_t[16 * i:16 * i + 1, :], NEG))
                out.append((p.astype(BF16), (p * (dp_t - dl_t[16 * i:16 * i + 1, :])).astype(BF16)))
            return out

        def third(rr, b, tiles, heads, probs, both):
            r0, rp = _attn_rows(b)
            nk = 2 * CHUNK if both else CHUNK
            left_k = _lane_left(nk)
            zero_k = jnp.zeros((nk, CHUNK), BF16)
            for hp, (ls, k2) in enumerate(tiles):
                dq = jnp.zeros((CHUNK, CHUNK), F32)
                dk2 = jnp.zeros((nk, CHUNK), F32)
                dv2 = jnp.zeros((nk, CHUNK), F32)
                for side in range(2):
                    qh, doh, _, _ = heads[2 * hp + side]
                    p, ds = probs[2 * hp + side]
                    dq = dq + _dot_tn(ds, jnp.where(left_k if side == 0 else ~left_k, k2, zero_k))
                    dk2 = dk2 + _dot(ds, qh)
                    dv2 = dv2 + _dot(p, doh)
                dq_ref[pl.ds(r0, CHUNK), ls] = dq.astype(dq_ref.dtype)
                if nblk == 1:
                    dk_ref[pl.ds(r0, CHUNK), ls] = dk2.astype(dk_ref.dtype)
                    dv_ref[pl.ds(r0, CHUNK), ls] = dv2.astype(dv_ref.dtype)
                elif both:
                    dk_ref[pl.ds(rp, CHUNK), ls] = (dk_wait[:, ls] + dk2[0:CHUNK]).astype(dk_ref.dtype)
                    dv_ref[pl.ds(rp, CHUNK), ls] = (dv_wait[:, ls] + dv2[0:CHUNK]).astype(dv_ref.dtype)
                    dk_wait[:, ls] = dk2[CHUNK:]
                    dv_wait[:, ls] = dv2[CHUNK:]
                else:
                    dk_wait[:, ls] = dk2
                    dv_wait[:, ls] = dv2

        def run(units, both):
            data = [first(rr, b, both) for rr, b in units]
            probs = [second(rr, b, heads, both) for (rr, b), (_, heads) in zip(units, data)]
            for (rr, b), (tiles, heads), pr in zip(units, data, probs):
                third(rr, b, tiles, heads, pr, both)

        run([(rr, 0) for rr in range(rps)], False)
        if nblk > 1:
            def one(b, carry):
                run([(0, b)], True)
                return carry

            lax.fori_loop(1, nblk, one, 0)
            last = (nblk - 1) * CHUNK
            dk_ref[last:last + CHUNK, :] = dk_wait[...].astype(dk_ref.dtype)
            dv_ref[last:last + CHUNK, :] = dv_wait[...].astype(dv_ref.dtype)

    spec = pl.BlockSpec((seg, rps * ATTN_W), lambda r: (0, r))
    return _pcall(
        body, name=f"attn_bwd_d{dil}", grid=(dil // rps,),
        in_specs=[spec] * 4 + [pl.BlockSpec((seg, rps * HEAD_W), lambda r: (0, r))] * 2, out_specs=[spec] * 3,
        out_shape=[_sds((seg, dil * ATTN_W), BF16)] * 3,
        scratch_shapes=[pltpu.VMEM((CHUNK, ATTN_W), F32), pltpu.VMEM((CHUNK, ATTN_W), F32)],
        args=(qv, kv, vv, dov, deltav, lsev), comms=comms, after=after)


def _inproj_bwd(dqs, dks, dvs, du, dvs_sgu, pos, x, dx1, w_in, g_pre, comms=()):
    def body(dq1, dq2, dq3, dk1, dk2, dk3, dv1, dv2, dv3, du_ref, dvs_ref, pos_ref, x_ref, dx1_ref, w_ref, g_ref,
             dproj_ref, gx_ref, dg_ref, slabs_a, slabs_b):
        @pl.when(pl.program_id(0) == 0)
        def _():
            dg_ref[...] = jnp.zeros_like(dg_ref)

        def total(r1, r2, r3):
            return r1[...] + _load_view(r2, slabs_a, DILATIONS[1]) + _load_view(r3, slabs_b, DILATIONS[2])

        tabs = _rot_tables(pos_ref[...])
        groups = {3: du_ref[...], 4: dvs_ref[...]}
        dh = jnp.zeros((TM, D_MODEL), F32)
        for g in (3, 4, 0, 1, 2):
            if g == 0:
                groups[g] = _rope_bwd(total(dq1, dq2, dq3) * np.float32(ATTN_SCALE), tabs).astype(BF16)
            elif g == 1:
                groups[g] = _rope_bwd(total(dk1, dk2, dk3), tabs).astype(BF16)
            elif g == 2:
                groups[g] = total(dv1, dv2, dv3).astype(BF16)
            dproj_ref[:, 512 * g:512 * (g + 1)] = groups[g]
            off = 0
            for s, a, b in _in_pieces(g):
                dh = dh + _dot_nt(groups[g][:, off:off + b - a], w_ref[s, :, a:b])
                off += b - a
        g = g_ref[...]
        xh, r = _rms_stats(x_ref[...])
        dx, dg = _rms_bwd(xh, r, g, dh)
        dg_ref[...] += dg
        gx_ref[...] = dx1_ref[...] + dx

    return _pcall(
        body, name="inproj_bwd", grid=(SEQ // TM,),
        in_specs=[_view_rows(dil) for dil in DILATIONS] * 3
        + [_rows(512), _rows(512), _rows(1), _rows(D_MODEL), _rows(D_MODEL),
           _resident((N_SHARD, D_MODEL, IN_S)), _resident((1, D_MODEL))],
        out_specs=[_rows(PROJ_W), _rows(D_MODEL), _acc(D_MODEL)],
        out_shape=[_sds((SEQ, PROJ_W), BF16), _sds((SEQ, D_MODEL), F32), _sds((1, D_MODEL), F32)],
        scratch_shapes=[_slab_scratch(), _slab_scratch()],
        args=(*dqs, *dks, *dvs, du, dvs_sgu, pos, x, dx1, w_in, g_pre), comms=comms)


def _coords():
    return lax.axis_index("x"), lax.axis_index("y"), lax.axis_index("c")


def _other_chips(x, y):
    return [(1 - x, y), (x, 1 - y), (1 - x, 1 - y)]


WEIGHTS = ("pre_mix_norm", "w_in", "sgu_ln_gain", "sgu_ln_bias", "sgu_w_spatial", "sgu_b_spatial", "attn_out_norm",
           "sgu_out_norm", "w_out", "post_mix_norm", "pre_ffn_norm", "w_gate", "w_up", "w_down", "post_ffn_norm")
SMALL = ("pre_mix_norm", "post_mix_norm", "pre_ffn_norm", "post_ffn_norm", "sgu_ln_gain", "sgu_ln_bias",
         "attn_out_norm", "sgu_out_norm", "sgu_w_spatial", "sgu_b_spatial")


def _remote(src, dst, send_sem, recv_sem, to):
    return pltpu.make_async_remote_copy(src_ref=src, dst_ref=dst, send_sem=send_sem, recv_sem=recv_sem,
                                        device_id=to, device_id_type=MESH)


def _halves(a):
    *lead, rows, cols = a.shape
    return a.reshape(*lead, 2, rows // 2, cols)


def _gather_ici(shards):
    n = len(shards)

    def desc(ins, outs, ss, rs, j, w, landed):
        x, y, c = _coords()
        cx, cy = _other_chips(x, y)[j]
        shard = 2 * cx + cy if landed else 2 * x + y
        return _remote(ins[w].at[c], outs[w].at[shard, c], ss.at[j * n + w], rs.at[j * n + w], (cx, cy, c))

    def start(ins, outs, ss, rs):
        for j in range(3):
            for w in range(n):
                desc(ins, outs, ss, rs, j, w, False).start()

    def finish(ins, outs, ss, rs):
        for j in range(3):
            for w in range(n):
                desc(ins, outs, ss, rs, j, w, True).wait_recv()
                desc(ins, outs, ss, rs, j, w, False).wait_send()

    return _Comm(shards, [_sds((N_SHARD,) + s.shape, s.dtype) for s in shards], 3 * n, start, finish)


def _gather_pass(fulls):
    n = len(fulls)

    def desc(bufs, ss, rs, j, w, landed):
        x, y, c = _coords()
        cx, cy = _other_chips(x, y)[j]
        shard = 2 * cx + cy
        return _remote(bufs[w].at[shard, c], bufs[w].at[shard, 1 - c if landed else c],
                       ss.at[j * n + w], rs.at[j * n + w], (x, y, 1 - c))

    def start(ins, outs, ss, rs):
        for j in range(3):
            for w in range(n):
                desc(outs, ss, rs, j, w, False).start()

    def finish(ins, outs, ss, rs):
        for j in range(3):
            for w in range(n):
                desc(outs, ss, rs, j, w, True).wait_recv()
                desc(outs, ss, rs, j, w, False).wait_send()

    return _Comm(fulls, [_sds(f.shape, f.dtype) for f in fulls], 3 * n, start, finish, aliased=True)


def _rs_sibling(gws):
    n = len(gws)

    def desc(ins, outs, ss, rs, w):
        x, y, c = _coords()
        return _remote(ins[w].at[:, 1 - c], outs[w], ss.at[w], rs.at[w], (x, y, 1 - c))

    def start(ins, outs, ss, rs):
        for w in range(n):
            desc(ins, outs, ss, rs, w).start()

    def finish(ins, outs, ss, rs):
        for w in range(n):
            desc(ins, outs, ss, rs, w).wait()

    out_shape = [_sds((N_SHARD, g.shape[1] // 2, g.shape[2]), g.dtype) for g in gws]
    return _Comm([_halves(g) for g in gws], out_shape, n, start, finish)


def _rs_chips(pbs):
    n = len(pbs)

    def desc(ins, outs, ss, rs, j, w):
        x, y, c = _coords()
        cx, cy = _other_chips(x, y)[j]
        return _remote(ins[w].at[2 * cx + cy], outs[w].at[j], ss.at[j * n + w], rs.at[j * n + w], (cx, cy, c))

    def start(ins, outs, ss, rs):
        for j in range(3):
            for w in range(n):
                desc(ins, outs, ss, rs, j, w).start()

    def finish(ins, outs, ss, rs):
        for j in range(3):
            for w in range(n):
                desc(ins, outs, ss, rs, j, w).wait()

    return _Comm(pbs, [_sds((3,) + p.shape[1:], p.dtype) for p in pbs], 3 * n, start, finish)


def _rs_join(halves):
    n = len(halves)

    def desc(bufs, ss, rs, w, landed):
        x, y, c = _coords()
        return _remote(bufs[w].at[c], bufs[w].at[1 - c if landed else c], ss.at[w], rs.at[w], (x, y, 1 - c))

    def start(ins, outs, ss, rs):
        for w in range(n):
            desc(outs, ss, rs, w, False).start()

    def finish(ins, outs, ss, rs):
        for w in range(n):
            desc(outs, ss, rs, w, True).wait_recv()
            desc(outs, ss, rs, w, False).wait_send()

    return _Comm(halves, [_sds(h.shape, h.dtype) for h in halves], n, start, finish, aliased=True)


def _small_exchange(buf):
    def desc(ins, outs, ss, rs, k, landed):
        x, y, c = _coords()
        px = 1 - x if (k >> 2) & 1 else x
        py = 1 - y if (k >> 1) & 1 else y
        pc = 1 - c if k & 1 else c
        slot = 4 * px + 2 * py + pc if landed else 4 * x + 2 * y + c
        return _remote(ins[0], outs[0].at[slot], ss.at[k - 1], rs.at[k - 1], (px, py, pc))

    def start(ins, outs, ss, rs):
        for k in range(1, 8):
            desc(ins, outs, ss, rs, k, False).start()

    def finish(ins, outs, ss, rs):
        for k in range(1, 8):
            desc(ins, outs, ss, rs, k, True).wait_recv()
            desc(ins, outs, ss, rs, k, False).wait_send()

    return _Comm([buf], [_sds((8,) + buf.shape, buf.dtype)], 7, start, finish)


HBM = pl.BlockSpec(memory_space=pltpu.HBM)
SEM = pl.BlockSpec(memory_space=pltpu.SEMAPHORE)
DATAFLOW = pltpu.SideEffectType.DATAFLOW_SIDE_EFFECTING


def _split_starts(name, comms, after):
    srcs = [[pltpu.with_memory_space_constraint(s, pltpu.HBM) for s in c.args] for c in comms]
    lands = [[pltpu.with_memory_space_constraint(lax.empty(o.shape, o.dtype), pltpu.HBM) for o in c.out_shape]
             for c in comms]
    bufs = [b for k in range(len(comms)) for b in srcs[k] + lands[k]]
    nb, nc = len(bufs), len(comms)

    def body(*refs):
        sems = refs[nb + 1:nb + 1 + 2 * nc]
        off = 0
        for k, c in enumerate(comms):
            ns, nl = len(srcs[k]), len(lands[k])
            c.start(refs[off:off + ns], refs[off + ns:off + ns + nl], sems[2 * k], sems[2 * k + 1])
            off += ns + nl
        refs[-1][...] = jnp.zeros_like(refs[-1])

    res = pl.pallas_call(
        body, name=name,
        out_shape=(*[pltpu.SemaphoreType.DMA((c.n_sems,)) for c in comms for _ in range(2)],
                   *[pltpu.HBM(b.shape, b.dtype) for b in bufs], _sds((8, 128), F32)),
        in_specs=[HBM] * nb + [ANY],
        out_specs=(*[SEM] * (2 * nc), *[HBM] * nb, pl.BlockSpec(memory_space=pltpu.VMEM)),
        input_output_aliases={i: 2 * nc + i for i in range(nb)},
        compiler_params=pltpu.CompilerParams(has_side_effects=DATAFLOW),
    )(*bufs, after)
    states, off = [], 2 * nc
    for k in range(nc):
        ns, nl = len(srcs[k]), len(lands[k])
        states.append((res[2 * k], res[2 * k + 1], list(res[off:off + ns]), list(res[off + ns:off + ns + nl])))
        off += ns + nl
    return states, res[-1]


def _split_wait(name, comm, send_sems, recv_sems, srcs, lands, after):
    ns, nb = len(srcs), len(lands)
    after = list(after) if isinstance(after, (list, tuple)) else [after]

    def body(*refs):
        comm.finish(refs[:ns], refs[ns:ns + nb], refs[ns + nb], refs[ns + nb + 1])

    res = pl.pallas_call(
        body, name=name,
        out_shape=tuple(pltpu.HBM(b.shape, b.dtype) for b in srcs + lands),
        in_specs=[HBM] * (ns + nb) + [SEM, SEM] + [ANY] * len(after), out_specs=tuple([HBM] * (ns + nb)),
        input_output_aliases={i: i for i in range(ns + nb)},
        compiler_params=pltpu.CompilerParams(has_side_effects=DATAFLOW),
    )(*srcs, *lands, send_sems, recv_sems, *after)
    return list(res[ns:])


LOSS_ROW = "loss_cols"
SMALL_EARLY = ("post_mix_norm", "pre_ffn_norm", "post_ffn_norm", "sgu_ln_gain", "sgu_ln_bias", "attn_out_norm",
               "sgu_out_norm", "sgu_b_spatial", LOSS_ROW)
SMALL_LATE = ("pre_mix_norm",)
SMALL_WSP = "sgu_w_spatial"


def _pack(d, names, rows):
    flat = [d[n].reshape(-1) for n in names]
    used = sum(f.shape[0] for f in flat)
    flat.append(jnp.zeros((rows * 1024 - used,), F32))
    return jnp.concatenate(flat).reshape(rows, 1024)


def _unpack(buf, names, shapes):
    flat = buf.reshape(-1)
    out, off = {}, 0
    for n in names:
        size = int(np.prod(shapes[n]))
        out[n] = flat[off:off + size].reshape(shapes[n])
        off += size
    return out


def _chip_sums(gbs, recvs, shard_core):
    n = len(gbs)
    _, rows, cols = gbs[0].shape
    hw = rows // 2

    def body(sc_ref, *refs):
        for k in range(n):
            refs[2 * n + k][...] = (refs[k][...].astype(F32) + refs[n + k][...].astype(F32)).astype(BF16)

    def other(s, sc):
        return jnp.where(s >= sc[0], s + 1, s)

    mine = pl.BlockSpec((1, hw, cols), lambda s, sc: (other(s, sc), sc[1], 0))
    plain = pl.BlockSpec((1, hw, cols), lambda s, sc: (other(s, sc), 0, 0))
    return pl.pallas_call(
        body, name="rs_chip_sums",
        grid_spec=pltpu.PrefetchScalarGridSpec(num_scalar_prefetch=1, grid=(N_SHARD - 1,),
                                               in_specs=[mine] * n + [plain] * n, out_specs=[plain] * n),
        out_shape=[_sds((N_SHARD, hw, cols), BF16)] * n,
        compiler_params=_seq_params(),
    )(shard_core, *gbs, *recvs)


def _final_sums(gws, recv_sibs, recv_chips, shard_core):
    n = len(gws)
    halves = [(g.shape[1] // 2, g.shape[2]) for g in gws]

    def body(sc_ref, *refs):
        for k in range(n):
            acc = refs[k][0] + refs[n + k][0]
            for j in range(3):
                acc = acc + refs[2 * n + k][j].astype(F32)
            refs[3 * n + k][0] = acc

    def specs(lead, index):
        return [pl.BlockSpec((lead, hw, cols), index) for hw, cols in halves]

    return pl.pallas_call(
        body, name="rs_final_sums",
        grid_spec=pltpu.PrefetchScalarGridSpec(
            num_scalar_prefetch=1, grid=(1,),
            in_specs=specs(1, lambda i, sc: (sc[0], sc[1], 0)) + specs(1, lambda i, sc: (sc[0], 0, 0))
            + specs(3, lambda i, sc: (0, 0, 0)),
            out_specs=specs(1, lambda i, sc: (sc[1], 0, 0))),
        out_shape=[_sds((2, hw, cols), F32) for hw, cols in halves],
        compiler_params=_seq_params(),
    )(shard_core, *gws, *recv_sibs, *recv_chips)


ADAM_BLOCKS = 4


def _adamw_multi(ws, gs, ms, vs, name, after=()):
    n = len(ws)

    def body(*refs):
        outs = refs[4 * n + len(after):]
        for k in range(n):
            g = refs[n + k][...]
            d, m, v = _adam_math(refs[k][...], g, refs[2 * n + k][...], refs[3 * n + k][...])
            outs[4 * k][...], outs[4 * k + 1][...], outs[4 * k + 2][...], outs[4 * k + 3][...] = g, d, m, v

    specs = [pl.BlockSpec((w.shape[0] // ADAM_BLOCKS, w.shape[1]), lambda i: (i, 0)) for w in ws]
    res = pl.pallas_call(
        body, name=name, grid=(ADAM_BLOCKS,), in_specs=specs * 4 + [ANY] * len(after),
        out_specs=[s for s in specs for _ in range(4)],
        out_shape=[_sds(w.shape, F32) for w in ws for _ in range(4)],
        compiler_params=_seq_params(),
    )(*ws, *gs, *ms, *vs, *after)
    return [tuple(res[4 * k:4 * k + 4]) for k in range(n)]


def _wgrad_ff(a_list, b, name):
    n = len(a_list)
    cols = 256

    def body(*refs):
        bv = refs[n][...]
        for k in range(n):
            g = _dot_tn(refs[k][...], bv)
            refs[n + 1 + k][...] = g
            refs[2 * n + 1 + k][...] = g.astype(BF16)

    res = _pcall(
        body, name=name, grid=(FF // cols,),
        in_specs=[pl.BlockSpec((SEQ, cols), lambda j: (0, j))] * n
        + [pl.BlockSpec((SEQ, D_MODEL), lambda j: (0, 0), pipeline_mode=pl.Buffered(1))],
        out_specs=[pl.BlockSpec((cols, D_MODEL), lambda j: (j, 0))] * (2 * n),
        out_shape=[_sds((FF, D_MODEL), F32)] * n + [_sds((FF, D_MODEL), BF16)] * n, args=(*a_list, b))
    return [m.reshape(N_SHARD, FF_S, D_MODEL) for m in res]


def _comm_only(name, comms):
    return _pcall(lambda: None, name=name, grid=(1,), in_specs=[], out_specs=[], out_shape=[], args=(),
                  comms=comms)[1]


def _adam_math(w, g, m, v):
    m = ADAM_B1 * m + (1.0 - ADAM_B1) * g
    v = ADAM_B2 * v + (1.0 - ADAM_B2) * (g * g)
    m_hat = m / (1.0 - ADAM_B1 ** ADAM_STEP)
    v_hat = v / (1.0 - ADAM_B2 ** ADAM_STEP)
    return -ADAM_LR * (m_hat / (jnp.sqrt(v_hat) + ADAM_EPS) + ADAM_WD * w), m, v


def _adamw_small(own, slots, w, m, v, me, stacked=False):
    rows, cols = own.shape

    def body(me_ref, own_ref, slots_ref, w_ref, m_ref, v_ref, *o_refs):
        own_v = own_ref[...].astype(F32)
        g = jnp.where(me_ref[0] == 0, own_v, slots_ref[0].astype(F32))
        for i in range(1, 8):
            g = g + jnp.where(me_ref[0] == i, own_v, slots_ref[i].astype(F32))
        for k, r in enumerate((g, *_adam_math(w_ref[...], g, m_ref[...], v_ref[...]))):
            if stacked:
                o_refs[0][k] = r
            else:
                o_refs[k][...] = r

    flat = pl.BlockSpec((rows, cols), lambda i, me_ref: (0, 0))
    res = pl.pallas_call(
        body, name="adamw_small",
        grid_spec=pltpu.PrefetchScalarGridSpec(
            num_scalar_prefetch=1, grid=(1,),
            in_specs=[flat, pl.BlockSpec((8, rows, cols), lambda i, me_ref: (0, 0, 0)), flat, flat, flat],
            out_specs=[pl.BlockSpec((4, rows, cols), lambda i, me_ref: (0, 0, 0))] if stacked else [flat] * 4),
        out_shape=[_sds((4, rows, cols), F32)] if stacked else [_sds((rows, cols), F32)] * 4,
        compiler_params=_seq_params(),
    )(me, own, slots, w, m, v)
    return list(res[0]) if stacked else res


def kernel(x, positions, pre_mix_norm, w_in, sgu_ln_gain, sgu_ln_bias, sgu_w_spatial, sgu_b_spatial, attn_out_norm, sgu_out_norm, w_out, post_mix_norm, pre_ffn_norm, w_gate, w_up, w_down, post_ffn_norm, loss_target, m_pre_mix_norm, m_w_in, m_sgu_ln_gain, m_sgu_ln_bias, m_sgu_w_spatial, m_sgu_b_spatial, m_attn_out_norm, m_sgu_out_norm, m_w_out, m_post_mix_norm, m_pre_ffn_norm, m_w_gate, m_w_up, m_w_down, m_post_ffn_norm, v_pre_mix_norm, v_w_in, v_sgu_ln_gain, v_sgu_ln_bias, v_sgu_w_spatial, v_sgu_b_spatial, v_attn_out_norm, v_sgu_out_norm, v_w_out, v_post_mix_norm, v_pre_ffn_norm, v_w_gate, v_w_up, v_w_down, v_post_ffn_norm):
    a = dict(locals())
    cx, cy, cc = _coords()
    s_me = 2 * cx + cy
    shard_core = jnp.stack([s_me, cc]).astype(jnp.int32)
    me = jnp.stack([4 * cx + 2 * cy + cc]).astype(jnp.int32)
    small = {n: (a[n][0] if a[n].ndim > 2 else a[n]) for n in SMALL}
    b_t = small["sgu_b_spatial"].T
    xs, target = x[0], loss_target[0]
    flipped = ("w_gate", "w_up")

    def big(name, n):
        return jnp.swapaxes(a[name], 1, 2)[0] if n in flipped else a[name][0]

    own = {"w_in": _halves(big("w_in", "w_in").astype(BF16))}

    def with_own(full, n):
        full = lax.dynamic_update_slice(full, own[n][None], (s_me, 0, 0, 0))
        return full.reshape((N_SHARD,) + big(n, n).shape)

    ffn = ("w_gate", "w_up", "w_down")
    g_in = _gather_ici([own["w_in"]])
    (s_in,), token = _split_starts("gather_in_start", [g_in], small["pre_mix_norm"])
    pos = positions.astype(F32).reshape(SEQ, 1) + token[0:1, 0:1]
    for n in ("w_out",) + ffn:
        own[n] = _halves((big(n, n) + token[0:1, 0:1]).astype(BF16))
    g_out, g_ffn = _gather_ici([own["w_out"]]), _gather_ici([own[n] for n in ffn])
    (s_out, s_ffn), token = _split_starts("gather_rest_start", [g_out, g_ffn], token)
    in_lands = _split_wait("gather_in_wait", g_in, *s_in, token)
    ((in_lands,),) = _comm_only("comm_pass_in", [_gather_pass(in_lands)])
    w_in_f = with_own(in_lands, "w_in")
    h, u, vs, sgu, *qkv = _inproj_fwd(xs, pos, small["pre_mix_norm"], w_in_f, small["sgu_ln_gain"],
                                      small["sgu_ln_bias"], small["sgu_w_spatial"], b_t)
    views = [tuple(qkv[3 * i:3 * i + 3]) for i in range(len(DILATIONS))]
    out_lands = _split_wait("gather_out_wait", g_out, *s_out, sgu)
    o_list, l_list = [], []
    for dil, (qv, kv, vv) in zip(DILATIONS, views):
        if dil == 1:
            (o, l), ((out_lands,),) = _attn_fwd(qv, kv, vv, dil, comms=[_gather_pass(out_lands)])
        else:
            o, l = _attn_fwd(qv, kv, vv, dil)
        o_list.append(o)
        l_list.append(l)
    w_out_f = with_own(out_lands, "w_out")
    ffn_lands = _split_wait("gather_ffn_wait", g_ffn, *s_ffn, l_list[-1])
    (attn, mixed, y, x1, *lses), (ffn_lands,) = _mix_out_fwd(
        o_list, l_list, sgu, xs, w_out_f, small["attn_out_norm"], small["sgu_out_norm"], small["post_mix_norm"],
        comms=[_gather_pass(ffn_lands)])
    w_gate_f, w_up_f, w_down_f = (with_own(f, n).reshape(FF, D_MODEL) for f, n in zip(ffn_lands, ffn))
    h2, act, dg, dup, df, dx1, loss_cols, d_pre_ffn, d_post_ffn = _ffn_fwd_bwd(
        x1, target, w_gate_f, w_up_f, w_down_f, small["pre_ffn_norm"], small["post_ffn_norm"])

    full_tok = pl.BlockSpec((SEQ, D_MODEL), lambda s: (0, 0), pipeline_mode=pl.Buffered(1))
    gw, gb = {}, {}
    gw["w_gate"], gw["w_up"], gb["w_gate"], gb["w_up"] = _wgrad_ff([dg, dup], h2, "wgrad_gate_up")
    gw["w_down"], gb["w_down"] = _wgrad_ff([act], df, "wgrad_down")
    (dy, du, dvs_sgu, d_post_mix, d_attn_norm, d_sgu_norm, d_w_sp, d_b_sp, d_ln_gain, d_ln_bias,
     *dviews), (sib_ffn,) = _outproj_bwd(
        dx1, y, attn, sgu, w_out_f, small["post_mix_norm"], small["attn_out_norm"], small["sgu_out_norm"],
        u, vs, small["sgu_ln_gain"], small["sgu_ln_bias"], small["sgu_w_spatial"], b_t,
        comms=[_rs_sibling([gb[n] for n in ffn])])
    sib = dict(zip(ffn, sib_ffn))
    part = dict(zip(ffn, _chip_sums([gb[n] for n in ffn], [sib[n] for n in ffn], shard_core)))
    gw["w_out"], gb["w_out"] = _wgrad(mixed, dy, pl.BlockSpec((SEQ, OUT_S), lambda s: (0, s)), full_tok,
                                      (OUT_S, D_MODEL), "wgrad_out")
    x_ffn = _rs_chips([part[n] for n in ffn])
    packed_early = _pack({
        "sgu_ln_gain": d_ln_gain, "sgu_ln_bias": d_ln_bias, "sgu_b_spatial": d_b_sp,
        "attn_out_norm": d_attn_norm, "sgu_out_norm": d_sgu_norm, "post_mix_norm": d_post_mix,
        "pre_ffn_norm": d_pre_ffn, "post_ffn_norm": d_post_ffn, LOSS_ROW: loss_cols}, SMALL_EARLY, 8)
    wsp_view = (SGU_GROUPS * CHUNK, CHUNK)
    packed_wsp = d_w_sp.reshape(wsp_view).astype(BF16)
    x_small, x_wsp = _small_exchange(packed_early), _small_exchange(packed_wsp)
    (s_ffn,), token = _split_starts("rs_ffn_start", [x_ffn], small["pre_mix_norm"])

    dqs, dks, dvs = [], [], []
    for i, (dil, (qv, kv, vv)) in enumerate(zip(DILATIONS, views)):
        if dil == 1:
            (dq, dk, dv), ((sib["w_out"],),) = _attn_bwd(qv, kv, vv, dviews[i], dviews[3 + i], lses[i], dil,
                                                        comms=[_rs_sibling([gb["w_out"]])], after=[token])
            (part["w_out"],) = _chip_sums([gb["w_out"]], [sib["w_out"]], shard_core)
            x_out = _rs_chips([part["w_out"]])
            (s_out, s_small, s_wsp), token = _split_starts("rs_out_small_start", [x_out, x_small, x_wsp], token)
        else:
            dq, dk, dv = _attn_bwd(qv, kv, vv, dviews[i], dviews[3 + i], lses[i], dil, after=[token])
        dqs.append(dq)
        dks.append(dk)
        dvs.append(dv)
    half, far, joined = {}, {}, {}
    far.update(zip(ffn, _split_wait("rs_ffn_wait", x_ffn, *s_ffn, dvs[-1])))
    half.update(zip(ffn, _final_sums([gw[n] for n in ffn], [sib[n] for n in ffn], [far[n] for n in ffn],
                                     shard_core)))
    dproj, grad_x, d_pre_mix = _inproj_bwd(dqs, dks, dvs, du, dvs_sgu, pos, xs, dx1, w_in_f, small["pre_mix_norm"])
    packed_late = _pack({"pre_mix_norm": d_pre_mix}, SMALL_LATE, 1)
    (gw["w_in"], gb["w_in"]), (got, (slots_late,)) = _wgrad(
        h, dproj, full_tok, pl.BlockSpec((SEQ, IN_S), lambda s: (0, s)), (D_MODEL, IN_S), "wgrad_in",
        comms=[_rs_join([half[n] for n in ffn]), _small_exchange(packed_late)])
    joined.update(zip(ffn, got))
    ((sib["w_in"],),) = _comm_only("comm_rs_sibling_in", [_rs_sibling([gb["w_in"]])])
    (part["w_in"],) = _chip_sums([gb["w_in"]], [sib["w_in"]], shard_core)
    x_in = _rs_chips([part["w_in"]])
    (s_in,), token = _split_starts("rs_in_start", [x_in], small["pre_mix_norm"])

    grads, deltas, new_m, new_v = {}, {}, {}, {}

    def record(n, outs):
        grads[n], deltas[n], new_m[n], new_v[n] = (
            jnp.swapaxes(o[None], 1, 2) if n in flipped else o[None] for o in outs)

    def update(names, name, after):
        for n, outs in zip(names, _adamw_multi(
                [big(n, n) for n in names], [joined[n].reshape(big(n, n).shape) for n in names],
                [big("m_" + n, n) for n in names], [big("v_" + n, n) for n in names], name, after)):
            record(n, outs)

    update(ffn, "adamw_ffn", [token])
    (far["w_out"],) = _split_wait("rs_out_wait", x_out, *s_out, new_v["w_down"])
    (slots_early,) = _split_wait("small_early_wait", x_small, *s_small, far["w_out"])
    (slots_wsp,) = _split_wait("small_wsp_wait", x_wsp, *s_wsp, slots_early)
    (far["w_in"],) = _split_wait("rs_in_wait", x_in, *s_in, slots_wsp)
    last = ("w_in", "w_out")
    half.update(zip(last, _final_sums([gw[n] for n in last], [sib[n] for n in last], [far[n] for n in last],
                                      shard_core)))
    (got,) = _comm_only("comm_rs_join_in_out", [_rs_join([half[n] for n in last])])
    joined.update(zip(last, got))
    update(last, "adamw_in_out", [])
    a[LOSS_ROW] = a["m_" + LOSS_ROW] = a["v_" + LOSS_ROW] = jnp.zeros((1, D_MODEL), F32)
    outs = _adamw_small(packed_wsp, slots_wsp, *[a[p + SMALL_WSP].reshape(wsp_view) for p in ("", "m_", "v_")], me)
    for dst, buf in zip((grads, deltas, new_m, new_v), outs):
        dst[SMALL_WSP] = buf.reshape(a[SMALL_WSP].shape)
    for names, rows, packed, slots in ((SMALL_EARLY, 8, packed_early, slots_early),
                                       (SMALL_LATE, 1, packed_late, slots_late)):
        outs = _adamw_small(packed, slots, _pack(a, names, rows), _pack({n: a["m_" + n] for n in names}, names, rows),
                            _pack({n: a["v_" + n] for n in names}, names, rows), me, stacked=len(names) > 1)
        for dst, buf in zip((grads, deltas, new_m, new_v), outs):
            dst.update(_unpack(buf, names, {n: a[n].shape for n in names}))
    loss = jnp.sum(grads[LOSS_ROW]) * np.float32(0.5 / D_MODEL)
    return (loss, grad_x[None], *[grads[n] for n in WEIGHTS], *[deltas[n] for n in WEIGHTS],
            *[new_m[n] for n in WEIGHTS], *[new_v[n] for n in WEIGHTS])
```

```python
import numpy as np
import jax
import jax.numpy as jnp
from jax import lax
from jax.experimental import pallas as pl
from jax.experimental.pallas import tpu as pltpu

F32 = jnp.float32
BF16 = jnp.bfloat16

SEQ = 2048
D_MODEL = 1024
HEAD_DIM = 64
ATTN_W = 512
SGU_W = 512
SGU_GROUPS = 8
CHUNK = 128
DILATIONS = (1, 4, 16)
N_SHARD = 4
IN_S = 640
OUT_S = 256
FF_S = 704
PROJ_W = N_SHARD * IN_S
FF = N_SHARD * FF_S
FF_CHUNKS = ((0, 1024), (1024, 2048), (2048, FF))
RMS_EPS = 1e-6
LN_EPS = 1e-5
ROPE_THETA = 500000.0
ATTN_SCALE = 1.0 / np.sqrt(HEAD_DIM)
NEG = -1e30
TM = 512
TM_FFN = 256
VMEM_LIMIT = 56 * 1024 * 1024

ADAM_LR = 0.001
ADAM_B1 = 0.9
ADAM_B2 = 0.999
ADAM_EPS = 1e-08
ADAM_WD = 0.01
ADAM_STEP = 10

MESH = pl.DeviceIdType.MESH
ANY = pl.BlockSpec(memory_space=pl.ANY)


def _dot(a, b):
    return jnp.dot(a, b, preferred_element_type=F32)


def _dot_nt(a, b):
    return lax.dot_general(a, b, (((1,), (1,)), ((), ())), preferred_element_type=F32)


def _dot_tn(a, b):
    return lax.dot_general(a, b, (((0,), (0,)), ((), ())), preferred_element_type=F32)


def _dot_exact(a, b):
    return jnp.dot(a, b, preferred_element_type=F32, precision=lax.Precision.HIGHEST)


def _dot_select(a, sel):
    hi = a.astype(BF16)
    lo = (a - hi.astype(F32)).astype(BF16)
    sel = sel.astype(BF16)
    return _dot(hi, sel) + _dot(lo, sel)


def _rms_stats(x):
    r = lax.rsqrt(jnp.mean(x * x, axis=-1, keepdims=True) + RMS_EPS)
    return x * r, r


def _rms_bwd(xh, r, gain, dy):
    dxh = dy * gain
    dx = r * (dxh - xh * jnp.mean(dxh * xh, axis=-1, keepdims=True))
    return dx, jnp.sum(dy * xh, axis=0, keepdims=True)


_ERF_ALPHA = (-2.72614225801306e-10, 2.77068142495902e-08, -2.10102402082508e-06, -5.69250639462346e-05,
              -7.34990630326855e-04, -2.95459980854025e-03, -1.60960333262415e-02)
_ERF_BETA = (-1.45660718464996e-05, -2.13374055278905e-04, -1.68282697438203e-03, -7.37332916720468e-03,
             -1.42647390514189e-02)


def _erf(x):
    x = jnp.clip(x, -4.0, 4.0)
    x2 = x * x
    p = jnp.full_like(x, _ERF_ALPHA[0])
    for a in _ERF_ALPHA[1:]:
        p = p * x2 + a
    q = jnp.full_like(x, _ERF_BETA[0])
    for b in _ERF_BETA[1:]:
        q = q * x2 + b
    return x * p / q


def _normal_cdf(x):
    return 0.5 * (1.0 + _erf(x * np.float32(1.0 / np.sqrt(2.0))))


def _gelu_grad(x, cdf):
    pdf = jnp.exp(-0.5 * x * x) * np.float32(1.0 / np.sqrt(2.0 * np.pi))
    return cdf + x * pdf


def _sigmoid(x):
    return 1.0 / (1.0 + jnp.exp(-x))


_INV_FREQ = tuple(float(np.float32(ROPE_THETA ** (-2.0 * j / 16.0))) for j in range(8))


def _rot_tables(pos):
    lane = lax.broadcasted_iota(jnp.int32, (1, 128), 1)
    d = lane & 63
    j = d & 7
    inv = jnp.zeros((1, 128), F32)
    for jj in range(8):
        inv = jnp.where(j == jj, _INV_FREQ[jj], inv)
    ang = pos.astype(F32) * inv
    c = jnp.cos(ang)
    s = jnp.sin(ang)
    cos_t = jnp.where(d < 16, c, 1.0)
    sin_a = jnp.where(d < 8, -s, 0.0)
    sin_b = jnp.where((d >= 8) & (d < 16), s, 0.0)
    return tuple(jnp.tile(t, (1, 4)) for t in (cos_t, sin_a, sin_b))


def _rope(x, tabs):
    cos_t, sin_a, sin_b = tabs
    return x * cos_t + pltpu.roll(x, 504, 1) * sin_a + pltpu.roll(x, 8, 1) * sin_b


def _rope_bwd(dy, tabs):
    cos_t, sin_a, sin_b = tabs
    return dy * cos_t + pltpu.roll(dy * sin_a, 8, 1) + pltpu.roll(dy * sin_b, 504, 1)


def _left_half():
    return lax.broadcasted_iota(jnp.int32, (CHUNK, CHUNK), 1) < HEAD_DIM


def _group_ones():
    lane = lax.broadcasted_iota(jnp.int32, (SGU_GROUPS, SGU_W), 1)
    row = lax.broadcasted_iota(jnp.int32, (SGU_GROUPS, SGU_W), 0)
    return ((lane >> 6) == row).astype(F32)


def _masked_spatial(w_ref):
    row = lax.broadcasted_iota(jnp.int32, (CHUNK, CHUNK), 0)
    col = lax.broadcasted_iota(jnp.int32, (CHUNK, CHUNK), 1)
    return [jnp.where(col <= row, w_ref[g], 0.0).astype(BF16) for g in range(SGU_GROUPS)]


def _sgu_core(u, vs, lg, lb, wm, bias_full):
    tm = u.shape[0]
    cdf_u, cdf_vs = _normal_cdf(u), _normal_cdf(vs)
    gu = u * cdf_u
    gv = vs * cdf_vs
    mu = jnp.mean(gv, axis=-1, keepdims=True)
    xc = gv - mu
    rstd = lax.rsqrt(jnp.mean(xc * xc, axis=-1, keepdims=True) + LN_EPS)
    xh = xc * rstd
    vnb = (xh * lg + lb).astype(BF16)
    left = _left_half()
    rows = []
    for c in range(tm // CHUNK):
        pieces = []
        for p in range(4):
            vp = vnb[c * CHUNK:(c + 1) * CHUNK, p * 128:(p + 1) * 128]
            pieces.append(jnp.where(left, _dot(wm[2 * p], vp), _dot(wm[2 * p + 1], vp)))
        rows.append(jnp.concatenate(pieces, axis=1) + bias_full)
    mixed = jnp.concatenate(rows, axis=0)
    return gu, xh, rstd, vnb, mixed, cdf_u, cdf_vs


def _resident(shape):
    n = len(shape)
    return pl.BlockSpec(shape, lambda *_: (0,) * n, pipeline_mode=pl.Buffered(1))


def _rows(ncol, tm=TM):
    return pl.BlockSpec((tm, ncol), lambda i: (i, 0))


def _acc(ncol, nrow=1):
    return pl.BlockSpec((nrow, ncol), lambda i: (0, 0))


HEAD_W = 128


def _view_rows(dil, width=ATTN_W, tm=TM):
    return pl.BlockSpec((tm // dil, dil * width), lambda i: (i, 0))


def _view_shape(dil, dtype, width=ATTN_W):
    return _sds((SEQ // dil, dil * width), dtype)


def _slab_scratch():
    return pltpu.VMEM((4, TM, 128), F32)


def _store_view(val, out_ref, slabs, dil):
    width = val.shape[1]
    for j in range(width // 128):
        slabs[j] = val[:, j * 128:(j + 1) * 128]
    for r in range(dil):
        for j in range(width // 128):
            c0 = r * width + j * 128
            out_ref[:, c0:c0 + 128] = slabs.at[j][pl.ds(r, TM // dil, stride=dil), :].astype(out_ref.dtype)


def _load_view(in_ref, slabs, dil, width=ATTN_W):
    for r in range(dil):
        for j in range(width // 128):
            c0 = r * width + j * 128
            slabs.at[j][pl.ds(r, TM // dil, stride=dil), :] = in_ref[:, c0:c0 + 128].astype(F32)
    return jnp.concatenate([slabs[j] for j in range(width // 128)], axis=1)


def _head_spread():
    m = lax.broadcasted_iota(jnp.int32, (HEAD_W, ATTN_W), 0)
    lane = lax.broadcasted_iota(jnp.int32, (HEAD_W, ATTN_W), 1)
    return (m == 16 * (lane >> 6)).astype(F32)


def _head_sum():
    lane = lax.broadcasted_iota(jnp.int32, (ATTN_W, HEAD_W), 0)
    m = lax.broadcasted_iota(jnp.int32, (ATTN_W, HEAD_W), 1)
    return ((lane >> 6) == (m >> 4)).astype(F32)


def _seq_params():
    return pltpu.CompilerParams(dimension_semantics=("arbitrary",), vmem_limit_bytes=VMEM_LIMIT)


def _sds(shape, dtype):
    return jax.ShapeDtypeStruct(shape, dtype)


class _Comm:
    def __init__(self, args, out_shape, n_sems, start, finish, aliased=False):
        self.args, self.out_shape, self.n_sems = list(args), list(out_shape), n_sems
        self.start, self.finish, self.aliased = start, finish, aliased


def _pcall(body, *, name, grid, in_specs, out_specs, out_shape, args, scratch_shapes=(), comms=(), after=()):
    single = not isinstance(out_shape, (list, tuple))
    out_specs = [out_specs] if single else list(out_specs)
    out_shape = [out_shape] if single else list(out_shape)
    n_in, n_out, n_scr = len(in_specs), len(out_shape), len(scratch_shapes)
    c_args = [a for c in comms for a in c.args]
    c_outs = [o for c in comms for o in c.out_shape]
    aliases, ai, ao = {}, n_in, n_out
    for c in comms:
        if c.aliased:
            aliases.update({ai + k: ao + k for k in range(len(c.args))})
        ai += len(c.args)
        ao += len(c.out_shape)
    sems = [pltpu.SemaphoreType.DMA((c.n_sems,)) for c in comms for _ in range(2)]
    steps = grid[0]

    def wrapped(*refs):
        o0 = n_in + len(c_args) + len(after)
        s0 = o0 + n_out + len(c_outs)
        m_in, m_out, m_sem = refs[n_in:n_in + len(c_args)], refs[o0 + n_out:s0], refs[s0 + n_scr:]

        def each(phase):
            ii = oi = 0
            for k, c in enumerate(comms):
                getattr(c, phase)(m_in[ii:ii + len(c.args)], m_out[oi:oi + len(c.out_shape)],
                                  m_sem[2 * k], m_sem[2 * k + 1])
                ii += len(c.args)
                oi += len(c.out_shape)

        if comms:
            @pl.when(pl.program_id(0) == 0)
            def _():
                each("start")

        body(*refs[:n_in], *refs[o0:o0 + n_out], *refs[s0:s0 + n_scr])

        if comms:
            @pl.when(pl.program_id(0) == steps - 1)
            def _():
                each("finish")

    res = pl.pallas_call(
        wrapped, name=name, grid=grid,
        in_specs=list(in_specs) + [ANY] * (len(c_args) + len(after)), out_specs=out_specs + [ANY] * len(c_outs),
        out_shape=out_shape + c_outs, scratch_shapes=list(scratch_shapes) + sems,
        input_output_aliases=aliases, compiler_params=_seq_params(),
    )(*args, *c_args, *after)
    mine = res[0] if single else list(res[:n_out])
    if not comms:
        return mine
    theirs, oi = [], n_out
    for c in comms:
        theirs.append(list(res[oi:oi + len(c.out_shape)]))
        oi += len(c.out_shape)
    return mine, theirs


def _in_pieces(g):
    lo, hi = 512 * g, 512 * (g + 1)
    return [(s, max(lo, IN_S * s) - IN_S * s, min(hi, IN_S * (s + 1)) - IN_S * s)
            for s in range(N_SHARD) if max(lo, IN_S * s) < min(hi, IN_S * (s + 1))]


def _inproj_fwd(x, pos, g_pre, w_in, lg, lb, w_sp, b_t, comms=()):
    def body(x_ref, pos_ref, g_ref, w_ref, lg_ref, lb_ref, wsp_ref, bt_ref, h_ref, u_ref, vs_ref, sgu_ref, *rest):
        qkv_refs, slabs = rest[:9], rest[9:]
        xh, _ = _rms_stats(x_ref[...])
        h = (xh * g_ref[...]).astype(BF16)
        h_ref[...] = h
        tabs = _rot_tables(pos_ref[...])

        def group(g):
            return jnp.concatenate([_dot(h, w_ref[s, :, a:b]) for s, a, b in _in_pieces(g)], axis=1)

        for t in range(3):
            val = group(t)
            if t < 2:
                val = _rope(val, tabs)
            if t == 0:
                val = val * np.float32(ATTN_SCALE)
            qkv_refs[t][...] = val.astype(BF16)
            for i, dil in enumerate(DILATIONS[1:]):
                _store_view(val, qkv_refs[3 * (i + 1) + t], slabs[t], dil)
        u = group(3)
        vs = group(4)
        u_ref[...] = u
        vs_ref[...] = vs
        bias_full = _dot_exact(bt_ref[...], _group_ones())
        gu, _, _, _, mixed, _, _ = _sgu_core(u, vs, lg_ref[...], lb_ref[...], _masked_spatial(wsp_ref), bias_full)
        sgu_ref[...] = gu * mixed

    return _pcall(
        body, name="inproj_sgu_fwd", grid=(SEQ // TM,),
        in_specs=[_rows(D_MODEL), _rows(1), _resident((1, D_MODEL)), _resident((N_SHARD, D_MODEL, IN_S)),
                  _resident((1, SGU_W)), _resident((1, SGU_W)), _resident((SGU_GROUPS, CHUNK, CHUNK)),
                  _resident((CHUNK, SGU_GROUPS))],
        out_specs=[_rows(D_MODEL), _rows(512), _rows(512), _rows(512)]
        + [_view_rows(dil) for dil in DILATIONS for _ in range(3)],
        out_shape=[_sds((SEQ, D_MODEL), BF16), _sds((SEQ, 512), F32), _sds((SEQ, 512), F32), _sds((SEQ, 512), F32)]
        + [_view_shape(dil, BF16) for dil in DILATIONS for _ in range(3)],
        scratch_shapes=[_slab_scratch() for _ in range(3)],
        args=(x, pos, g_pre, w_in, lg, lb, w_sp, b_t), comms=comms)


def _block_masks():
    row = lax.broadcasted_iota(jnp.int32, (CHUNK, CHUNK), 0)
    col = lax.broadcasted_iota(jnp.int32, (CHUNK, CHUNK), 1)
    return col <= row, col >= row


def _attn_fwd(qv, kv, vv, dil, comms=()):
    seg = SEQ // dil
    nblk = seg // CHUNK
    rps = 4 if nblk == 1 else 1

    def body(q_ref, k_ref, v_ref, o_ref, l_ref):
        left = _left_half()
        m_cur, m_prev = _block_masks()
        zero = jnp.zeros((CHUNK, CHUNK), BF16)
        ones = (jnp.where(left, 1.0, 0.0).astype(BF16), jnp.where(left, 0.0, 1.0).astype(BF16))

        sides = tuple(enumerate((left, ~left)))

        def rows(b):
            if isinstance(b, int):
                return b * CHUNK, max(b - 1, 0) * CHUNK
            return pl.multiple_of(b * CHUNK, CHUNK), pl.multiple_of(jnp.maximum(b - 1, 0) * CHUNK, CHUNK)

        def first(rr, b):
            r0, rp = rows(b)
            prev_ok = m_prev & (b > 0)
            tiles, scores = [], []
            for hp in range(4):
                ls = slice(rr * ATTN_W + hp * 128, rr * ATTN_W + (hp + 1) * 128)
                qp = q_ref[pl.ds(r0, CHUNK), ls]
                kc = k_ref[pl.ds(r0, CHUNK), ls]
                kp = k_ref[pl.ds(rp, CHUNK), ls] if nblk > 1 else None
                tiles.append((ls, v_ref[pl.ds(r0, CHUNK), ls], v_ref[pl.ds(rp, CHUNK), ls] if nblk > 1 else None))
                for _, hm in sides:
                    qh = jnp.where(hm, qp, zero)
                    sc = jnp.where(m_cur, _dot_nt(qh, kc), NEG)
                    sp = jnp.where(prev_ok, _dot_nt(qh, kp), NEG) if nblk > 1 else None
                    scores.append((sc, sp))
            return tiles, scores

        def second(scores):
            probs = []
            for sc, sp in scores:
                if nblk > 1:
                    m = jnp.max(jnp.maximum(sc, sp), axis=-1, keepdims=True)
                    pc = jnp.exp(sc - m)
                    pp = jnp.exp(sp - m)
                    probs.append((m, pc.astype(BF16), pp.astype(BF16), (pc + pp).astype(BF16)))
                else:
                    m = jnp.max(sc, axis=-1, keepdims=True)
                    pc = jnp.exp(sc - m).astype(BF16)
                    probs.append((m, pc, None, pc))
            return probs

        def third(rr, b, tiles, probs):
            r0, _ = rows(b)
            for hp, (ls, vc, vp) in enumerate(tiles):
                acc = jnp.zeros((CHUNK, CHUNK), F32)
                den = jnp.zeros((CHUNK, CHUNK), F32)
                for side, hm in sides:
                    _, pc, pp, psum = probs[2 * hp + side]
                    acc = acc + _dot(pc, jnp.where(hm, vc, zero))
                    if nblk > 1:
                        acc = acc + _dot(pp, jnp.where(hm, vp, zero))
                    den = den + _dot(psum, ones[side])
                o_ref[pl.ds(r0, CHUNK), ls] = (acc / den).astype(o_ref.dtype)
                lse = jnp.where(left, probs[2 * hp][0], probs[2 * hp + 1][0]) + jnp.log(den)
                l_ref[pl.ds(r0, CHUNK), rr * HEAD_W + 32 * hp:rr * HEAD_W + 32 * hp + 32] = lse[:, 48:80]

        def run(units):
            data = [first(rr, b) for rr, b in units]
            probs = [second(scores) for _, scores in data]
            for (rr, b), (tiles, _), pr in zip(units, data, probs):
                third(rr, b, tiles, pr)

        if nblk == 1:
            run([(rr, 0) for rr in range(rps)])
        else:
            def one(b, carry):
                run([(0, b)])
                return carry

            lax.fori_loop(0, nblk, one, 0)

    spec = pl.BlockSpec((seg, rps * ATTN_W), lambda r: (0, r))
    return _pcall(
        body, name=f"attn_fwd_d{dil}", grid=(dil // rps,),
        in_specs=[spec, spec, spec], out_specs=[spec, pl.BlockSpec((seg, rps * HEAD_W), lambda r: (0, r))],
        out_shape=[_sds((seg, dil * ATTN_W), BF16), _sds((seg, dil * HEAD_W), F32)],
        args=(qv, kv, vv), comms=comms)


def _lane_left(nrows):
    return lax.broadcasted_iota(jnp.int32, (nrows, CHUNK), 1) < HEAD_DIM


def _attn_rows(b):
    if isinstance(b, int):
        return b * CHUNK, max(b - 1, 0) * CHUNK
    return pl.multiple_of(b * CHUNK, CHUNK), pl.multiple_of(jnp.maximum(b - 1, 0) * CHUNK, CHUNK)


def _mix_out_fwd(o_list, l_list, sgu, x, w_out, g_attn, g_sgu, g_post, comms=()):
    def body(o1, o2, o3, l1, l2, l3, sgu_ref, x_ref, w_ref, ga_ref, gs_ref, gp_ref,
             attn_ref, mixed_ref, y_ref, x1_ref, lse1_ref, lse2_ref, lse3_ref, slabs_a, slabs_b):
        os = [o1[...], _load_view(o2, slabs_a, DILATIONS[1]), _load_view(o3, slabs_b, DILATIONS[2])]
        ls = [l1[...], _load_view(l2, slabs_a, DILATIONS[1], HEAD_W), _load_view(l3, slabs_b, DILATIONS[2], HEAD_W)]
        m = jnp.maximum(jnp.maximum(ls[0], ls[1]), ls[2])
        es = [jnp.exp(l - m) for l in ls]
        den = es[0] + es[1] + es[2]
        spread = _head_spread()
        attn = sum(_dot_select(e / den, spread) * o for e, o in zip(es, os))
        attn_ref[...] = attn
        lse = m + jnp.log(den)
        lse1_ref[...] = lse
        _store_view(lse, lse2_ref, slabs_a, DILATIONS[1])
        _store_view(lse, lse3_ref, slabs_b, DILATIONS[2])
        ah, _ = _rms_stats(attn)
        sh, _ = _rms_stats(sgu_ref[...])
        mixed = jnp.concatenate([ah * ga_ref[...], sh * gs_ref[...]], axis=1).astype(BF16)
        mixed_ref[...] = mixed
        y = _dot(mixed[:, 0:OUT_S], w_ref[0])
        for s in range(1, N_SHARD):
            y = y + _dot(mixed[:, s * OUT_S:(s + 1) * OUT_S], w_ref[s])
        y_ref[...] = y
        yh, _ = _rms_stats(y)
        x1_ref[...] = x_ref[...] + yh * gp_ref[...]

    return _pcall(
        body, name="mix_out_fwd", grid=(SEQ // TM,),
        in_specs=[_view_rows(dil) for dil in DILATIONS] + [_view_rows(dil, HEAD_W) for dil in DILATIONS]
        + [_rows(512), _rows(D_MODEL), _resident((N_SHARD, OUT_S, D_MODEL)),
           _resident((1, 512)), _resident((1, 512)), _resident((1, D_MODEL))],
        out_specs=[_rows(512), _rows(D_MODEL), _rows(D_MODEL), _rows(D_MODEL)]
        + [_view_rows(dil, HEAD_W) for dil in DILATIONS],
        out_shape=[_sds((SEQ, 512), F32), _sds((SEQ, D_MODEL), BF16), _sds((SEQ, D_MODEL), F32),
                   _sds((SEQ, D_MODEL), F32)] + [_view_shape(dil, F32, HEAD_W) for dil in DILATIONS],
        scratch_shapes=[_slab_scratch(), _slab_scratch()],
        args=(*o_list, *l_list, sgu, x, w_out, g_attn, g_sgu, g_post), comms=comms)


def _ffn_fwd_bwd(x1, target, w_gate, w_up, w_down, g_pre, g_post, comms=()):
    def body(x1_ref, t_ref, wg_ref, wu_ref, wd_ref, gpf_ref, gpo_ref,
             h2_ref, a_ref, dg_ref, dup_ref, df_ref, dx1_ref, loss_ref, dgpf_ref, dgpo_ref, g_scr, up_scr):
        @pl.when(pl.program_id(0) == 0)
        def _():
            loss_ref[...] = jnp.zeros_like(loss_ref)
            dgpf_ref[...] = jnp.zeros_like(dgpf_ref)
            dgpo_ref[...] = jnp.zeros_like(dgpo_ref)

        x1 = x1_ref[...]
        gpf = gpf_ref[...]
        gpo = gpo_ref[...]
        xh, r = _rms_stats(x1)
        h2 = (xh * gpf).astype(BF16)
        h2_ref[...] = h2
        f = jnp.zeros((TM_FFN, D_MODEL), F32)
        for c0, c1 in FF_CHUNKS:
            g = _dot_nt(h2, wg_ref[c0:c1, :])
            up = _dot_nt(h2, wu_ref[c0:c1, :])
            g_scr[:, c0:c1] = g
            up_scr[:, c0:c1] = up
            a = (g * _sigmoid(g) * up).astype(BF16)
            a_ref[:, c0:c1] = a
            f = f + _dot(a, wd_ref[c0:c1, :])
        fh, rf = _rms_stats(f)
        diff = x1 + fh * gpo - t_ref[...]
        loss_ref[...] += jnp.sum(diff * diff, axis=0, keepdims=True)
        dout = diff * np.float32(1.0 / D_MODEL)
        df, dgpo = _rms_bwd(fh, rf, gpo, dout)
        dgpo_ref[...] += dgpo
        dfb = df.astype(BF16)
        df_ref[...] = dfb
        dh2 = jnp.zeros((TM_FFN, D_MODEL), F32)
        for c0, c1 in FF_CHUNKS:
            da = _dot_nt(dfb, wd_ref[c0:c1, :])
            g = g_scr[:, c0:c1]
            up = up_scr[:, c0:c1]
            sg = _sigmoid(g)
            dup = (da * (g * sg)).astype(BF16)
            dg = (da * up * (sg * (1.0 + g * (1.0 - sg)))).astype(BF16)
            dg_ref[:, c0:c1] = dg
            dup_ref[:, c0:c1] = dup
            dh2 = dh2 + _dot(dg, wg_ref[c0:c1, :]) + _dot(dup, wu_ref[c0:c1, :])
        dx, dgpf = _rms_bwd(xh, r, gpf, dh2)
        dgpf_ref[...] += dgpf
        dx1_ref[...] = dout + dx

    return _pcall(
        body, name="ffn_fwd_bwd", grid=(SEQ // TM_FFN,),
        in_specs=[_rows(D_MODEL, TM_FFN), _rows(D_MODEL, TM_FFN), _resident((FF, D_MODEL)),
                  _resident((FF, D_MODEL)), _resident((FF, D_MODEL)),
                  _resident((1, D_MODEL)), _resident((1, D_MODEL))],
        out_specs=[_rows(D_MODEL, TM_FFN), _rows(FF, TM_FFN), _rows(FF, TM_FFN), _rows(FF, TM_FFN),
                   _rows(D_MODEL, TM_FFN), _rows(D_MODEL, TM_FFN), _acc(D_MODEL), _acc(D_MODEL), _acc(D_MODEL)],
        out_shape=[_sds((SEQ, D_MODEL), BF16), _sds((SEQ, FF), BF16), _sds((SEQ, FF), BF16),
                   _sds((SEQ, FF), BF16), _sds((SEQ, D_MODEL), BF16), _sds((SEQ, D_MODEL), F32),
                   _sds((1, D_MODEL), F32), _sds((1, D_MODEL), F32), _sds((1, D_MODEL), F32)],
        scratch_shapes=[pltpu.VMEM((TM_FFN, FF), F32), pltpu.VMEM((TM_FFN, FF), F32)],
        args=(x1, target, w_gate, w_up, w_down, g_pre, g_post), comms=comms)


def _wgrad(a, b, a_spec, b_spec, out_block, name, comms=()):
    def body(a_ref, b_ref, o_ref, ob_ref):
        av = a_ref[0] if len(a_ref.shape) == 3 else a_ref[...]
        bv = b_ref[0] if len(b_ref.shape) == 3 else b_ref[...]
        g = _dot_tn(av, bv)
        o_ref[0] = g
        ob_ref[0] = g.astype(BF16)

    return _pcall(
        body, name=name, grid=(N_SHARD,),
        in_specs=[a_spec, b_spec],
        out_specs=[pl.BlockSpec((1,) + out_block, lambda s: (s, 0, 0))] * 2,
        out_shape=[_sds((N_SHARD,) + out_block, F32), _sds((N_SHARD,) + out_block, BF16)],
        args=(a, b), comms=comms)


def _outproj_bwd(dx1, y, attn, sgu, w_out, g_post, g_attn, g_sgu, u, vs, lg, lb, w_sp, b_t, comms=(), after=()):
    sgu_bwd = _sgu_bwd_body()

    def body(dx1_ref, y_ref, attn_ref, sgu_ref, w_ref, gp_ref, ga_ref, gs_ref, u_ref, vs_ref, lg_ref, lb_ref,
             wsp_ref, bt_ref, dy_ref, du_ref, dvs_ref, dgp_ref, dga_ref, dgs_ref, dw_ref, db_ref, dlg_ref, dlb_ref,
             *rest):
        dattn_refs, delta_refs, (slabs_a, slabs_b, dsgu_ref, dbias_scr) = rest[0:3], rest[3:6], rest[6:]

        @pl.when(pl.program_id(0) == 0)
        def _():
            dgp_ref[...] = jnp.zeros_like(dgp_ref)
            dga_ref[...] = jnp.zeros_like(dga_ref)
            dgs_ref[...] = jnp.zeros_like(dgs_ref)

        yh, ry = _rms_stats(y_ref[...])
        dy, dgp = _rms_bwd(yh, ry, gp_ref[...], dx1_ref[...])
        dgp_ref[...] += dgp
        dyb = dy.astype(BF16)
        dy_ref[...] = dyb
        dmixed = jnp.concatenate([_dot_nt(dyb, w_ref[s]) for s in range(N_SHARD)], axis=1)
        attn = attn_ref[...]
        ah, ra = _rms_stats(attn)
        dattn, dga = _rms_bwd(ah, ra, ga_ref[...], dmixed[:, 0:512])
        dga_ref[...] += dga
        sh, rs = _rms_stats(sgu_ref[...])
        dsgu, dgs = _rms_bwd(sh, rs, gs_ref[...], dmixed[:, 512:1024])
        dgs_ref[...] += dgs
        dsgu_ref[...] = dsgu
        delta = _dot_select(dattn * attn, _head_sum())
        dattn_refs[0][...] = dattn.astype(BF16)
        delta_refs[0][...] = delta
        for i, dil in enumerate(DILATIONS[1:]):
            _store_view(dattn, dattn_refs[i + 1], slabs_a, dil)
            _store_view(delta, delta_refs[i + 1], slabs_b, dil)
        sgu_bwd(u_ref, vs_ref, dsgu_ref, lg_ref, lb_ref, wsp_ref, bt_ref,
                du_ref, dvs_ref, dw_ref, db_ref, dlg_ref, dlb_ref, dbias_scr)

    return _pcall(
        body, name="outproj_sgu_bwd", grid=(SEQ // TM,),
        in_specs=[_rows(D_MODEL), _rows(D_MODEL), _rows(512), _rows(512), _resident((N_SHARD, OUT_S, D_MODEL)),
                  _resident((1, D_MODEL)), _resident((1, 512)), _resident((1, 512)),
                  _rows(SGU_W), _rows(SGU_W), _resident((1, SGU_W)), _resident((1, SGU_W)),
                  _resident((SGU_GROUPS, CHUNK, CHUNK)), _resident((CHUNK, SGU_GROUPS))],
        out_specs=[_rows(D_MODEL), _rows(SGU_W), _rows(SGU_W), _acc(D_MODEL), _acc(512), _acc(512),
                   pl.BlockSpec((SGU_GROUPS, CHUNK, CHUNK), lambda i: (0, 0, 0)), _acc(CHUNK, SGU_GROUPS),
                   _acc(SGU_W), _acc(SGU_W)]
        + [_view_rows(dil) for dil in DILATIONS] + [_view_rows(dil, HEAD_W) for dil in DILATIONS],
        out_shape=[_sds((SEQ, D_MODEL), BF16), _sds((SEQ, SGU_W), BF16), _sds((SEQ, SGU_W), BF16),
                   _sds((1, D_MODEL), F32), _sds((1, 512), F32), _sds((1, 512), F32),
                   _sds((SGU_GROUPS, CHUNK, CHUNK), F32), _sds((SGU_GROUPS, CHUNK), F32),
                   _sds((1, SGU_W), F32), _sds((1, SGU_W), F32)]
        + [_view_shape(dil, BF16) for dil in DILATIONS] + [_view_shape(dil, F32, HEAD_W) for dil in DILATIONS],
        scratch_shapes=[_slab_scratch(), _slab_scratch(), pltpu.VMEM((TM, SGU_W), F32),
                        pltpu.VMEM((CHUNK, SGU_W), F32)],
        args=(dx1, y, attn, sgu, w_out, g_post, g_attn, g_sgu, u, vs, lg, lb, w_sp, b_t), comms=comms, after=after)


def _sgu_bwd_body():
    nsteps = SEQ // TM

    def body(u_ref, vs_ref, ds_ref, lg_ref, lb_ref, w_ref, bt_ref,
             du_ref, dvs_ref, dw_ref, db_ref, dlg_ref, dlb_ref, dbias_scr):
        i = pl.program_id(0)

        @pl.when(i == 0)
        def _():
            dw_ref[...] = jnp.zeros_like(dw_ref)
            dlg_ref[...] = jnp.zeros_like(dlg_ref)
            dlb_ref[...] = jnp.zeros_like(dlb_ref)
            dbias_scr[...] = jnp.zeros_like(dbias_scr)

        wm = _masked_spatial(w_ref)
        ones_g = _group_ones()
        bias_full = _dot_exact(bt_ref[...], ones_g)
        u = u_ref[...]
        vs = vs_ref[...]
        lg = lg_ref[...]
        gu, xh, rstd, vnb, mixed, cdf_u, cdf_vs = _sgu_core(u, vs, lg, lb_ref[...], wm, bias_full)
        dsgu = ds_ref[...]
        du_ref[...] = (dsgu * mixed * _gelu_grad(u, cdf_u)).astype(BF16)
        dmixed = dsgu * gu
        left = _left_half()
        dvn_rows = []
        for c in range(TM // CHUNK):
            rs = slice(c * CHUNK, (c + 1) * CHUNK)
            dm_c = dmixed[rs, :]
            dbias_scr[...] += dm_c
            pieces = []
            for p in range(4):
                ls = slice(p * 128, (p + 1) * 128)
                dmp = dm_c[:, ls]
                vp = vnb[rs, ls]
                dmb = dmp.astype(BF16)
                zero = jnp.zeros_like(dmb)
                dw_ref[2 * p] += _dot_nt(jnp.where(left, dmb, zero), vp)
                dw_ref[2 * p + 1] += _dot_nt(jnp.where(left, zero, dmb), vp)
                pieces.append(jnp.where(left, _dot_tn(wm[2 * p], dmb), _dot_tn(wm[2 * p + 1], dmb)))
            dvn_rows.append(jnp.concatenate(pieces, axis=1))
        dvn = jnp.concatenate(dvn_rows, axis=0)
        dlg_ref[...] += jnp.sum(dvn * xh, axis=0, keepdims=True)
        dlb_ref[...] += jnp.sum(dvn, axis=0, keepdims=True)
        dxh = dvn * lg
        dgv = rstd * (dxh - jnp.mean(dxh, axis=-1, keepdims=True) - xh * jnp.mean(dxh * xh, axis=-1, keepdims=True))
        dvs_ref[...] = (dgv * _gelu_grad(vs, cdf_vs)).astype(BF16)

        @pl.when(i == nsteps - 1)
        def _():
            row = lax.broadcasted_iota(jnp.int32, (CHUNK, CHUNK), 0)
            col = lax.broadcasted_iota(jnp.int32, (CHUNK, CHUNK), 1)
            for g in range(SGU_GROUPS):
                dw_ref[g] = jnp.where(col <= row, dw_ref[g], 0.0)
            db_ref[...] = lax.dot_general(ones_g, dbias_scr[...], (((1,), (1,)), ((), ())),
                                          preferred_element_type=F32, precision=lax.Precision.HIGHEST)

    return body


def _attn_bwd(qv, kv, vv, dov, deltav, lsev, dil, comms=(), after=()):
    seg = SEQ // dil
    nblk = seg // CHUNK
    rps = 4 if nblk == 1 else 1

    def body(q_ref, k_ref, v_ref, do_ref, dl_ref, lse_ref, dq_ref, dk_ref, dv_ref, dk_wait, dv_wait):
        left = _left_half()
        m_cur, m_prev = _block_masks()
        sides = tuple(enumerate((left, ~left)))
        zero = jnp.zeros((CHUNK, CHUNK), BF16)

        def first(rr, b, both):
            r0, rp = _attn_rows(b)
            keys = pl.ds(rp, 2 * CHUNK) if both else pl.ds(r0, CHUNK)
            tiles, heads = [], []
            for hp in range(4):
                ls = slice(rr * ATTN_W + hp * 128, rr * ATTN_W + (hp + 1) * 128)
                qp = q_ref[pl.ds(r0, CHUNK), ls]
                dop = do_ref[pl.ds(r0, CHUNK), ls]
                k2 = k_ref[keys, ls]
                v2 = v_ref[keys, ls]
                tiles.append((ls, k2))
                for _, hm in sides:
                    qh = jnp.where(hm, qp, zero)
                    doh = jnp.where(hm, dop, zero)
                    heads.append((qh, doh, _dot_nt(k2, qh), _dot_nt(v2, doh)))
            return tiles, heads

        def second(rr, b, heads, both):
            r0, _ = _attn_rows(b)
            ok = jnp.concatenate([m_cur, m_prev], axis=0) if both else m_prev
            lanes = slice(rr * HEAD_W, (rr + 1) * HEAD_W)
            lse_t = lse_ref[pl.ds(r0, CHUNK), lanes].T
            dl_t = dl_ref[pl.ds(r0, CHUNK), lanes].T
            out = []
            for i, (_, _, s_t, dp_t) in enumerate(heads):
                p = jnp.exp(jnp.where(ok, s_t - lse_t[16 * i:16 * i + 1, :], NEG))
                out.append((p.astype(BF16), (p * (dp_t - dl_t[16 * i:16 * i + 1, :])).astype(BF16)))
            return out

        def third(rr, b, tiles, heads, probs, both):
            r0, rp = _attn_rows(b)
            nk = 2 * CHUNK if both else CHUNK
            left_k = _lane_left(nk)
            zero_k = jnp.zeros((nk, CHUNK), BF16)
            for hp, (ls, k2) in enumerate(tiles):
                dq = jnp.zeros((CHUNK, CHUNK), F32)
                dk2 = jnp.zeros((nk, CHUNK), F32)
                dv2 = jnp.zeros((nk, CHUNK), F32)
                for side in range(2):
                    qh, doh, _, _ = heads[2 * hp + side]
                    p, ds = probs[2 * hp + side]
                    dq = dq + _dot_tn(ds, jnp.where(left_k if side == 0 else ~left_k, k2, zero_k))
                    dk2 = dk2 + _dot(ds, qh)
                    dv2 = dv2 + _dot(p, doh)
                dq_ref[pl.ds(r0, CHUNK), ls] = dq.astype(dq_ref.dtype)
                if nblk == 1:
                    dk_ref[pl.ds(r0, CHUNK), ls] = dk2.astype(dk_ref.dtype)
                    dv_ref[pl.ds(r0, CHUNK), ls] = dv2.astype(dv_ref.dtype)
                elif both:
                    dk_ref[pl.ds(rp, CHUNK), ls] = (dk_wait[:, ls] + dk2[0:CHUNK]).astype(dk_ref.dtype)
                    dv_ref[pl.ds(rp, CHUNK), ls] = (dv_wait[:, ls] + dv2[0:CHUNK]).astype(dv_ref.dtype)
                    dk_wait[:, ls] = dk2[CHUNK:]
                    dv_wait[:, ls] = dv2[CHUNK:]
                else:
                    dk_wait[:, ls] = dk2
                    dv_wait[:, ls] = dv2

        def run(units, both):
            data = [first(rr, b, both) for rr, b in units]
            probs = [second(rr, b, heads, both) for (rr, b), (_, heads) in zip(units, data)]
            for (rr, b), (tiles, heads), pr in zip(units, data, probs):
                third(rr, b, tiles, heads, pr, both)

        run([(rr, 0) for rr in range(rps)], False)
        if nblk > 1:
            def one(b, carry):
                run([(0, b)], True)
                return carry

            lax.fori_loop(1, nblk, one, 0)
            last = (nblk - 1) * CHUNK
            dk_ref[last:last + CHUNK, :] = dk_wait[...].astype(dk_ref.dtype)
            dv_ref[last:last + CHUNK, :] = dv_wait[...].astype(dv_ref.dtype)

    spec = pl.BlockSpec((seg, rps * ATTN_W), lambda r: (0, r))
    return _pcall(
        body, name=f"attn_bwd_d{dil}", grid=(dil // rps,),
        in_specs=[spec] * 4 + [pl.BlockSpec((seg, rps * HEAD_W), lambda r: (0, r))] * 2, out_specs=[spec] * 3,
        out_shape=[_sds((seg, dil * ATTN_W), BF16)] * 3,
        scratch_shapes=[pltpu.VMEM((CHUNK, ATTN_W), F32), pltpu.VMEM((CHUNK, ATTN_W), F32)],
        args=(qv, kv, vv, dov, deltav, lsev), comms=comms, after=after)


def _inproj_bwd(dqs, dks, dvs, du, dvs_sgu, pos, x, dx1, w_in, g_pre, comms=()):
    def body(dq1, dq2, dq3, dk1, dk2, dk3, dv1, dv2, dv3, du_ref, dvs_ref, pos_ref, x_ref, dx1_ref, w_ref, g_ref,
             dproj_ref, gx_ref, dg_ref, slabs_a, slabs_b):
        @pl.when(pl.program_id(0) == 0)
        def _():
            dg_ref[...] = jnp.zeros_like(dg_ref)

        def total(r1, r2, r3):
            return r1[...] + _load_view(r2, slabs_a, DILATIONS[1]) + _load_view(r3, slabs_b, DILATIONS[2])

        tabs = _rot_tables(pos_ref[...])
        groups = {3: du_ref[...], 4: dvs_ref[...]}
        dh = jnp.zeros((TM, D_MODEL), F32)
        for g in (3, 4, 0, 1, 2):
            if g == 0:
                groups[g] = _rope_bwd(total(dq1, dq2, dq3) * np.float32(ATTN_SCALE), tabs).astype(BF16)
            elif g == 1:
                groups[g] = _rope_bwd(total(dk1, dk2, dk3), tabs).astype(BF16)
            elif g == 2:
                groups[g] = total(dv1, dv2, dv3).astype(BF16)
            dproj_ref[:, 512 * g:512 * (g + 1)] = groups[g]
            off = 0
            for s, a, b in _in_pieces(g):
                dh = dh + _dot_nt(groups[g][:, off:off + b - a], w_ref[s, :, a:b])
                off += b - a
        g = g_ref[...]
        xh, r = _rms_stats(x_ref[...])
        dx, dg = _rms_bwd(xh, r, g, dh)
        dg_ref[...] += dg
        gx_ref[...] = dx1_ref[...] + dx

    return _pcall(
        body, name="inproj_bwd", grid=(SEQ // TM,),
        in_specs=[_view_rows(dil) for dil in DILATIONS] * 3
        + [_rows(512), _rows(512), _rows(1), _rows(D_MODEL), _rows(D_MODEL),
           _resident((N_SHARD, D_MODEL, IN_S)), _resident((1, D_MODEL))],
        out_specs=[_rows(PROJ_W), _rows(D_MODEL), _acc(D_MODEL)],
        out_shape=[_sds((SEQ, PROJ_W), BF16), _sds((SEQ, D_MODEL), F32), _sds((1, D_MODEL), F32)],
        scratch_shapes=[_slab_scratch(), _slab_scratch()],
        args=(*dqs, *dks, *dvs, du, dvs_sgu, pos, x, dx1, w_in, g_pre), comms=comms)


def _coords():
    return lax.axis_index("x"), lax.axis_index("y"), lax.axis_index("c")


def _other_chips(x, y):
    return [(1 - x, y), (x, 1 - y), (1 - x, 1 - y)]


WEIGHTS = ("pre_mix_norm", "w_in", "sgu_ln_gain", "sgu_ln_bias", "sgu_w_spatial", "sgu_b_spatial", "attn_out_norm",
           "sgu_out_norm", "w_out", "post_mix_norm", "pre_ffn_norm", "w_gate", "w_up", "w_down", "post_ffn_norm")
SMALL = ("pre_mix_norm", "post_mix_norm", "pre_ffn_norm", "post_ffn_norm", "sgu_ln_gain", "sgu_ln_bias",
         "attn_out_norm", "sgu_out_norm", "sgu_w_spatial", "sgu_b_spatial")


def _remote(src, dst, send_sem, recv_sem, to):
    return pltpu.make_async_remote_copy(src_ref=src, dst_ref=dst, send_sem=send_sem, recv_sem=recv_sem,
                                        device_id=to, device_id_type=MESH)


def _halves(a):
    *lead, rows, cols = a.shape
    return a.reshape(*lead, 2, rows // 2, cols)


def _gather_ici(shards):
    n = len(shards)

    def desc(ins, outs, ss, rs, j, w, landed):
        x, y, c = _coords()
        cx, cy = _other_chips(x, y)[j]
        shard = 2 * cx + cy if landed else 2 * x + y
        return _remote(ins[w].at[c], outs[w].at[shard, c], ss.at[j * n + w], rs.at[j * n + w], (cx, cy, c))

    def start(ins, outs, ss, rs):
        for j in range(3):
            for w in range(n):
                desc(ins, outs, ss, rs, j, w, False).start()

    def finish(ins, outs, ss, rs):
        for j in range(3):
            for w in range(n):
                desc(ins, outs, ss, rs, j, w, True).wait_recv()
                desc(ins, outs, ss, rs, j, w, False).wait_send()

    return _Comm(shards, [_sds((N_SHARD,) + s.shape, s.dtype) for s in shards], 3 * n, start, finish)


def _gather_pass(fulls):
    n = len(fulls)

    def desc(bufs, ss, rs, j, w, landed):
        x, y, c = _coords()
        cx, cy = _other_chips(x, y)[j]
        shard = 2 * cx + cy
        return _remote(bufs[w].at[shard, c], bufs[w].at[shard, 1 - c if landed else c],
                       ss.at[j * n + w], rs.at[j * n + w], (x, y, 1 - c))

    def start(ins, outs, ss, rs):
        for j in range(3):
            for w in range(n):
                desc(outs, ss, rs, j, w, False).start()

    def finish(ins, outs, ss, rs):
        for j in range(3):
            for w in range(n):
                desc(outs, ss, rs, j, w, True).wait_recv()
                desc(outs, ss, rs, j, w, False).wait_send()

    return _Comm(fulls, [_sds(f.shape, f.dtype) for f in fulls], 3 * n, start, finish, aliased=True)


def _rs_sibling(gws):
    n = len(gws)

    def desc(ins, outs, ss, rs, w):
        x, y, c = _coords()
        return _remote(ins[w].at[:, 1 - c], outs[w], ss.at[w], rs.at[w], (x, y, 1 - c))

    def start(ins, outs, ss, rs):
        for w in range(n):
            desc(ins, outs, ss, rs, w).start()

    def finish(ins, outs, ss, rs):
        for w in range(n):
            desc(ins, outs, ss, rs, w).wait()

    out_shape = [_sds((N_SHARD, g.shape[1] // 2, g.shape[2]), g.dtype) for g in gws]
    return _Comm([_halves(g) for g in gws], out_shape, n, start, finish)


def _rs_chips(pbs):
    n = len(pbs)

    def desc(ins, outs, ss, rs, j, w):
        x, y, c = _coords()
        cx, cy = _other_chips(x, y)[j]
        return _remote(ins[w].at[2 * cx + cy], outs[w].at[j], ss.at[j * n + w], rs.at[j * n + w], (cx, cy, c))

    def start(ins, outs, ss, rs):
        for j in range(3):
            for w in range(n):
                desc(ins, outs, ss, rs, j, w).start()

    def finish(ins, outs, ss, rs):
        for j in range(3):
            for w in range(n):
                desc(ins, outs, ss, rs, j, w).wait()

    return _Comm(pbs, [_sds((3,) + p.shape[1:], p.dtype) for p in pbs], 3 * n, start, finish)


def _rs_join(halves, split=False):
    n = len(halves)

    def desc(bufs, ss, rs, w, landed):
        x, y, c = _coords()
        return _remote(bufs[w].at[c], bufs[w].at[1 - c if landed else c], ss.at[w], rs.at[w], (x, y, 1 - c))

    def start(ins, outs, ss, rs):
        for w in range(n):
            desc(ins if split else outs, ss, rs, w, False).start()

    def finish(ins, outs, ss, rs):
        for w in range(n):
            desc(ins if split else outs, ss, rs, w, True).wait_recv()
            desc(ins if split else outs, ss, rs, w, False).wait_send()

    if split:
        return _Comm(halves, [], n, start, finish)
    return _Comm(halves, [_sds(h.shape, h.dtype) for h in halves], n, start, finish, aliased=True)


def _small_exchange(buf):
    def desc(ins, outs, ss, rs, k, landed):
        x, y, c = _coords()
        px = 1 - x if (k >> 2) & 1 else x
        py = 1 - y if (k >> 1) & 1 else y
        pc = 1 - c if k & 1 else c
        slot = 4 * px + 2 * py + pc if landed else 4 * x + 2 * y + c
        return _remote(ins[0], outs[0].at[slot], ss.at[k - 1], rs.at[k - 1], (px, py, pc))

    def start(ins, outs, ss, rs):
        for k in range(1, 8):
            desc(ins, outs, ss, rs, k, False).start()

    def finish(ins, outs, ss, rs):
        for k in range(1, 8):
            desc(ins, outs, ss, rs, k, True).wait_recv()
            desc(ins, outs, ss, rs, k, False).wait_send()

    return _Comm([buf], [_sds((8,) + buf.shape, buf.dtype)], 7, start, finish)


HBM = pl.BlockSpec(memory_space=pltpu.HBM)
SEM = pl.BlockSpec(memory_space=pltpu.SEMAPHORE)
DATAFLOW = pltpu.SideEffectType.DATAFLOW_SIDE_EFFECTING


def _split_starts(name, comms, after):
    srcs = [[pltpu.with_memory_space_constraint(s, pltpu.HBM) for s in c.args] for c in comms]
    lands = [[pltpu.with_memory_space_constraint(lax.empty(o.shape, o.dtype), pltpu.HBM) for o in c.out_shape]
             for c in comms]
    bufs = [b for k in range(len(comms)) for b in srcs[k] + lands[k]]
    nb, nc = len(bufs), len(comms)

    def body(*refs):
        sems = refs[nb + 1:nb + 1 + 2 * nc]
        off = 0
        for k, c in enumerate(comms):
            ns, nl = len(srcs[k]), len(lands[k])
            c.start(refs[off:off + ns], refs[off + ns:off + ns + nl], sems[2 * k], sems[2 * k + 1])
            off += ns + nl
        refs[-1][...] = jnp.zeros_like(refs[-1])

    res = pl.pallas_call(
        body, name=name,
        out_shape=(*[pltpu.SemaphoreType.DMA((c.n_sems,)) for c in comms for _ in range(2)],
                   *[pltpu.HBM(b.shape, b.dtype) for b in bufs], _sds((8, 128), F32)),
        in_specs=[HBM] * nb + [ANY],
        out_specs=(*[SEM] * (2 * nc), *[HBM] * nb, pl.BlockSpec(memory_space=pltpu.VMEM)),
        input_output_aliases={i: 2 * nc + i for i in range(nb)},
        compiler_params=pltpu.CompilerParams(has_side_effects=DATAFLOW),
    )(*bufs, after)
    states, off = [], 2 * nc
    for k in range(nc):
        ns, nl = len(srcs[k]), len(lands[k])
        states.append((res[2 * k], res[2 * k + 1], list(res[off:off + ns]), list(res[off + ns:off + ns + nl])))
        off += ns + nl
    return states, res[-1]


def _split_wait(name, comm, send_sems, recv_sems, srcs, lands, after):
    ns, nb = len(srcs), len(lands)
    after = list(after) if isinstance(after, (list, tuple)) else [after]

    def body(*refs):
        comm.finish(refs[:ns], refs[ns:ns + nb], refs[ns + nb], refs[ns + nb + 1])

    res = pl.pallas_call(
        body, name=name,
        out_shape=tuple(pltpu.HBM(b.shape, b.dtype) for b in srcs + lands),
        in_specs=[HBM] * (ns + nb) + [SEM, SEM] + [ANY] * len(after), out_specs=tuple([HBM] * (ns + nb)),
        input_output_aliases={i: i for i in range(ns + nb)},
        compiler_params=pltpu.CompilerParams(has_side_effects=DATAFLOW),
    )(*srcs, *lands, send_sems, recv_sems, *after)
    return list(res[ns:]) or list(res)


LOSS_ROW = "loss_cols"
SMALL_EARLY = ("post_mix_norm", "pre_ffn_norm", "post_ffn_norm", "sgu_ln_gain", "sgu_ln_bias", "attn_out_norm",
               "sgu_out_norm", "sgu_b_spatial", LOSS_ROW)
SMALL_LATE = ("pre_mix_norm",)
SMALL_WSP = "sgu_w_spatial"


def _pack(d, names, rows):
    flat = [d[n].reshape(-1) for n in names]
    used = sum(f.shape[0] for f in flat)
    flat.append(jnp.zeros((rows * 1024 - used,), F32))
    return jnp.concatenate(flat).reshape(rows, 1024)


def _unpack(buf, names, shapes):
    flat = buf.reshape(-1)
    out, off = {}, 0
    for n in names:
        size = int(np.prod(shapes[n]))
        out[n] = flat[off:off + size].reshape(shapes[n])
        off += size
    return out


def _chip_sums(gbs, recvs, shard_core):
    n = len(gbs)
    _, rows, cols = gbs[0].shape
    hw = rows // 2

    def body(sc_ref, *refs):
        for k in range(n):
            refs[2 * n + k][...] = (refs[k][...].astype(F32) + refs[n + k][...].astype(F32)).astype(BF16)

    def other(s, sc):
        return jnp.where(s >= sc[0], s + 1, s)

    mine = pl.BlockSpec((1, hw, cols), lambda s, sc: (other(s, sc), sc[1], 0))
    plain = pl.BlockSpec((1, hw, cols), lambda s, sc: (other(s, sc), 0, 0))
    return pl.pallas_call(
        body, name="rs_chip_sums",
        grid_spec=pltpu.PrefetchScalarGridSpec(num_scalar_prefetch=1, grid=(N_SHARD - 1,),
                                               in_specs=[mine] * n + [plain] * n, out_specs=[plain] * n),
        out_shape=[_sds((N_SHARD, hw, cols), BF16)] * n,
        compiler_params=_seq_params(),
    )(shard_core, *gbs, *recvs)


def _final_sums(gws, recv_sibs, recv_chips, shard_core):
    n = len(gws)
    halves = [(g.shape[1] // 2, g.shape[2]) for g in gws]

    def body(sc_ref, *refs):
        for k in range(n):
            acc = refs[k][0] + refs[n + k][0]
            for j in range(3):
                acc = acc + refs[2 * n + k][j].astype(F32)
            refs[3 * n + k][0] = acc

    def specs(lead, index):
        return [pl.BlockSpec((lead, hw, cols), index) for hw, cols in halves]

    return pl.pallas_call(
        body, name="rs_final_sums",
        grid_spec=pltpu.PrefetchScalarGridSpec(
            num_scalar_prefetch=1, grid=(1,),
            in_specs=specs(1, lambda i, sc: (sc[0], sc[1], 0)) + specs(1, lambda i, sc: (sc[0], 0, 0))
            + specs(3, lambda i, sc: (0, 0, 0)),
            out_specs=specs(1, lambda i, sc: (sc[1], 0, 0))),
        out_shape=[_sds((2, hw, cols), F32) for hw, cols in halves],
        compiler_params=_seq_params(),
    )(shard_core, *gws, *recv_sibs, *recv_chips)


ADAM_BLOCKS = 4


def _adamw_multi(ws, gs, ms, vs, name, after=()):
    n = len(ws)

    def body(*refs):
        outs = refs[4 * n + len(after):]
        for k in range(n):
            g = refs[n + k][...]
            d, m, v = _adam_math(refs[k][...], g, refs[2 * n + k][...], refs[3 * n + k][...])
            outs[4 * k][...], outs[4 * k + 1][...], outs[4 * k + 2][...], outs[4 * k + 3][...] = g, d, m, v

    specs = [pl.BlockSpec((w.shape[0] // ADAM_BLOCKS, w.shape[1]), lambda i: (i, 0)) for w in ws]
    res = pl.pallas_call(
        body, name=name, grid=(ADAM_BLOCKS,), in_specs=specs * 4 + [ANY] * len(after),
        out_specs=[s for s in specs for _ in range(4)],
        out_shape=[_sds(w.shape, F32) for w in ws for _ in range(4)],
        compiler_params=_seq_params(),
    )(*ws, *gs, *ms, *vs, *after)
    return [tuple(res[4 * k:4 * k + 4]) for k in range(n)]


def _wgrad_ff(a_list, b, name):
    n = len(a_list)
    cols = 256

    def body(*refs):
        bv = refs[n][...]
        for k in range(n):
            g = _dot_tn(refs[k][...], bv)
            refs[n + 1 + k][...] = g
            refs[2 * n + 1 + k][...] = g.astype(BF16)

    res = _pcall(
        body, name=name, grid=(FF // cols,),
        in_specs=[pl.BlockSpec((SEQ, cols), lambda j: (0, j))] * n
        + [pl.BlockSpec((SEQ, D_MODEL), lambda j: (0, 0), pipeline_mode=pl.Buffered(1))],
        out_specs=[pl.BlockSpec((cols, D_MODEL), lambda j: (j, 0))] * (2 * n),
        out_shape=[_sds((FF, D_MODEL), F32)] * n + [_sds((FF, D_MODEL), BF16)] * n, args=(*a_list, b))
    return [m.reshape(N_SHARD, FF_S, D_MODEL) for m in res]


def _comm_only(name, comms):
    return _pcall(lambda: None, name=name, grid=(1,), in_specs=[], out_specs=[], out_shape=[], args=(),
                  comms=comms)[1]


def _adam_math(w, g, m, v):
    m = ADAM_B1 * m + (1.0 - ADAM_B1) * g
    v = ADAM_B2 * v + (1.0 - ADAM_B2) * (g * g)
    m_hat = m / (1.0 - ADAM_B1 ** ADAM_STEP)
    v_hat = v / (1.0 - ADAM_B2 ** ADAM_STEP)
    return -ADAM_LR * (m_hat / (jnp.sqrt(v_hat) + ADAM_EPS) + ADAM_WD * w), m, v


def _adamw_small(own, slots, w, m, v, me, after=()):
    rows, cols = own.shape

    def body(me_ref, own_ref, slots_ref, w_ref, m_ref, v_ref, *refs):
        g_ref, d_ref, nm_ref, nv_ref = refs[len(after):]
        own_v = own_ref[...].astype(F32)
        g = jnp.where(me_ref[0] == 0, own_v, slots_ref[0].astype(F32))
        for i in range(1, 8):
            g = g + jnp.where(me_ref[0] == i, own_v, slots_ref[i].astype(F32))
        g_ref[...] = g
        d_ref[...], nm_ref[...], nv_ref[...] = _adam_math(w_ref[...], g, m_ref[...], v_ref[...])

    flat = pl.BlockSpec((rows, cols), lambda i, me_ref: (0, 0))
    return pl.pallas_call(
        body, name="adamw_small",
        grid_spec=pltpu.PrefetchScalarGridSpec(
            num_scalar_prefetch=1, grid=(1,),
            in_specs=[flat, pl.BlockSpec((8, rows, cols), lambda i, me_ref: (0, 0, 0)), flat, flat, flat]
            + [ANY] * len(after),
            out_specs=[flat] * 4),
        out_shape=[_sds((rows, cols), F32)] * 4,
        compiler_params=_seq_params(),
    )(me, own, slots, w, m, v, *after)


def kernel(x, positions, pre_mix_norm, w_in, sgu_ln_gain, sgu_ln_bias, sgu_w_spatial, sgu_b_spatial, attn_out_norm, sgu_out_norm, w_out, post_mix_norm, pre_ffn_norm, w_gate, w_up, w_down, post_ffn_norm, loss_target, m_pre_mix_norm, m_w_in, m_sgu_ln_gain, m_sgu_ln_bias, m_sgu_w_spatial, m_sgu_b_spatial, m_attn_out_norm, m_sgu_out_norm, m_w_out, m_post_mix_norm, m_pre_ffn_norm, m_w_gate, m_w_up, m_w_down, m_post_ffn_norm, v_pre_mix_norm, v_w_in, v_sgu_ln_gain, v_sgu_ln_bias, v_sgu_w_spatial, v_sgu_b_spatial, v_attn_out_norm, v_sgu_out_norm, v_w_out, v_post_mix_norm, v_pre_ffn_norm, v_w_gate, v_w_up, v_w_down, v_post_ffn_norm):
    a = dict(locals())
    cx, cy, cc = _coords()
    s_me = 2 * cx + cy
    shard_core = jnp.stack([s_me, cc]).astype(jnp.int32)
    me = jnp.stack([4 * cx + 2 * cy + cc]).astype(jnp.int32)
    small = {n: (a[n][0] if a[n].ndim > 2 else a[n]) for n in SMALL}
    b_t = small["sgu_b_spatial"].T
    xs, target = x[0], loss_target[0]
    flipped = ("w_gate", "w_up")

    def big(name, n):
        return jnp.swapaxes(a[name], 1, 2)[0] if n in flipped else a[name][0]

    own = {"w_in": _halves(big("w_in", "w_in").astype(BF16))}

    def with_own(full, n):
        full = lax.dynamic_update_slice(full, own[n][None], (s_me, 0, 0, 0))
        return full.reshape((N_SHARD,) + big(n, n).shape)

    ffn = ("w_gate", "w_up", "w_down")
    g_in = _gather_ici([own["w_in"]])
    (s_in,), token = _split_starts("gather_in_start", [g_in], small["pre_mix_norm"])
    pos = positions.astype(F32).reshape(SEQ, 1) + token[0:1, 0:1]
    for n in ("w_out",) + ffn:
        own[n] = _halves((big(n, n) + token[0:1, 0:1]).astype(BF16))
    g_out, g_ffn = _gather_ici([own["w_out"]]), _gather_ici([own[n] for n in ffn])
    (s_out, s_ffn), token = _split_starts("gather_rest_start", [g_out, g_ffn], token)
    in_lands = _split_wait("gather_in_wait", g_in, *s_in, token)
    ((in_lands,),) = _comm_only("comm_pass_in", [_gather_pass(in_lands)])
    w_in_f = with_own(in_lands, "w_in")
    h, u, vs, sgu, *qkv = _inproj_fwd(xs, pos, small["pre_mix_norm"], w_in_f, small["sgu_ln_gain"],
                                      small["sgu_ln_bias"], small["sgu_w_spatial"], b_t)
    views = [tuple(qkv[3 * i:3 * i + 3]) for i in range(len(DILATIONS))]
    out_lands = _split_wait("gather_out_wait", g_out, *s_out, sgu)
    o_list, l_list = [], []
    for dil, (qv, kv, vv) in zip(DILATIONS, views):
        if dil == 1:
            (o, l), ((out_lands,),) = _attn_fwd(qv, kv, vv, dil, comms=[_gather_pass(out_lands)])
        else:
            o, l = _attn_fwd(qv, kv, vv, dil)
        o_list.append(o)
        l_list.append(l)
    w_out_f = with_own(out_lands, "w_out")
    ffn_lands = _split_wait("gather_ffn_wait", g_ffn, *s_ffn, l_list[-1])
    (attn, mixed, y, x1, *lses), (ffn_lands,) = _mix_out_fwd(
        o_list, l_list, sgu, xs, w_out_f, small["attn_out_norm"], small["sgu_out_norm"], small["post_mix_norm"],
        comms=[_gather_pass(ffn_lands)])
    w_gate_f, w_up_f, w_down_f = (with_own(f, n).reshape(FF, D_MODEL) for f, n in zip(ffn_lands, ffn))
    h2, act, dg, dup, df, dx1, loss_cols, d_pre_ffn, d_post_ffn = _ffn_fwd_bwd(
        x1, target, w_gate_f, w_up_f, w_down_f, small["pre_ffn_norm"], small["post_ffn_norm"])

    full_tok = pl.BlockSpec((SEQ, D_MODEL), lambda s: (0, 0), pipeline_mode=pl.Buffered(1))
    gw, gb = {}, {}
    gw["w_gate"], gw["w_up"], gb["w_gate"], gb["w_up"] = _wgrad_ff([dg, dup], h2, "wgrad_gate_up")
    gw["w_down"], gb["w_down"] = _wgrad_ff([act], df, "wgrad_down")
    (dy, du, dvs_sgu, d_post_mix, d_attn_norm, d_sgu_norm, d_w_sp, d_b_sp, d_ln_gain, d_ln_bias,
     *dviews), (sib_ffn,) = _outproj_bwd(
        dx1, y, attn, sgu, w_out_f, small["post_mix_norm"], small["attn_out_norm"], small["sgu_out_norm"],
        u, vs, small["sgu_ln_gain"], small["sgu_ln_bias"], small["sgu_w_spatial"], b_t,
        comms=[_rs_sibling([gb[n] for n in ffn])])
    sib = dict(zip(ffn, sib_ffn))
    part = dict(zip(ffn, _chip_sums([gb[n] for n in ffn], [sib[n] for n in ffn], shard_core)))
    gw["w_out"], gb["w_out"] = _wgrad(mixed, dy, pl.BlockSpec((SEQ, OUT_S), lambda s: (0, s)), full_tok,
                                      (OUT_S, D_MODEL), "wgrad_out")
    x_ffn = _rs_chips([part[n] for n in ffn])
    packed_early = _pack({
        "sgu_ln_gain": d_ln_gain, "sgu_ln_bias": d_ln_bias, "sgu_b_spatial": d_b_sp,
        "attn_out_norm": d_attn_norm, "sgu_out_norm": d_sgu_norm, "post_mix_norm": d_post_mix,
        "pre_ffn_norm": d_pre_ffn, "post_ffn_norm": d_post_ffn, LOSS_ROW: loss_cols}, SMALL_EARLY, 8)
    wsp_view = (SGU_GROUPS * CHUNK, CHUNK)
    packed_wsp = d_w_sp.reshape(wsp_view).astype(BF16)
    x_small, x_wsp = _small_exchange(packed_early), _small_exchange(packed_wsp)
    (s_ffn,), token = _split_starts("rs_ffn_start", [x_ffn], small["pre_mix_norm"])

    dqs, dks, dvs = [], [], []
    for i, (dil, (qv, kv, vv)) in enumerate(zip(DILATIONS, views)):
        if dil == 1:
            (dq, dk, dv), ((sib["w_out"],),) = _attn_bwd(qv, kv, vv, dviews[i], dviews[3 + i], lses[i], dil,
                                                        comms=[_rs_sibling([gb["w_out"]])], after=[token])
            (part["w_out"],) = _chip_sums([gb["w_out"]], [sib["w_out"]], shard_core)
            x_out = _rs_chips([part["w_out"]])
            (s_out, s_small, s_wsp), token = _split_starts("rs_out_small_start", [x_out, x_small, x_wsp], token)
        else:
            dq, dk, dv = _attn_bwd(qv, kv, vv, dviews[i], dviews[3 + i], lses[i], dil, after=[token])
        dqs.append(dq)
        dks.append(dk)
        dvs.append(dv)
    half, far, joined = {}, {}, {}
    far.update(zip(ffn, _split_wait("rs_ffn_wait", x_ffn, *s_ffn, dvs[-1])))
    half.update(zip(ffn, _final_sums([gw[n] for n in ffn], [sib[n] for n in ffn], [far[n] for n in ffn],
                                     shard_core)))
    dproj, grad_x, d_pre_mix = _inproj_bwd(dqs, dks, dvs, du, dvs_sgu, pos, xs, dx1, w_in_f, small["pre_mix_norm"])
    packed_late = _pack({"pre_mix_norm": d_pre_mix}, SMALL_LATE, 1)
    (gw["w_in"], gb["w_in"]), (got, (slots_late,)) = _wgrad(
        h, dproj, full_tok, pl.BlockSpec((SEQ, IN_S), lambda s: (0, s)), (D_MODEL, IN_S), "wgrad_in",
        comms=[_rs_join([half[n] for n in ffn]), _small_exchange(packed_late)])
    joined.update(zip(ffn, got))
    ((sib["w_in"],),) = _comm_only("comm_rs_sibling_in", [_rs_sibling([gb["w_in"]])])
    (part["w_in"],) = _chip_sums([gb["w_in"]], [sib["w_in"]], shard_core)
    x_in = _rs_chips([part["w_in"]])
    (s_in,), token = _split_starts("rs_in_start", [x_in], small["pre_mix_norm"])

    grads, deltas, new_m, new_v = {}, {}, {}, {}

    def record(n, outs):
        grads[n], deltas[n], new_m[n], new_v[n] = (
            jnp.swapaxes(o[None], 1, 2) if n in flipped else o[None] for o in outs)

    def update(names, name, after):
        for n, outs in zip(names, _adamw_multi(
                [big(n, n) for n in names], [joined[n].reshape(big(n, n).shape) for n in names],
                [big("m_" + n, n) for n in names], [big("v_" + n, n) for n in names], name, after)):
            record(n, outs)

    update(ffn, "adamw_ffn", [token])
    (far["w_out"],) = _split_wait("rs_out_wait", x_out, *s_out, new_v["w_down"])
    (slots_early,) = _split_wait("small_early_wait", x_small, *s_small, far["w_out"])
    (slots_wsp,) = _split_wait("small_wsp_wait", x_wsp, *s_wsp, slots_early)
    (far["w_in"],) = _split_wait("rs_in_wait", x_in, *s_in, slots_wsp)
    last = ("w_in", "w_out")
    half.update(zip(last, _final_sums([gw[n] for n in last], [sib[n] for n in last], [far[n] for n in last],
                                      shard_core)))
    x_join = _rs_join([half[n] for n in last], split=True)
    (s_join,), token = _split_starts("rs_join_start", [x_join], slots_wsp)
    outs = _adamw_small(packed_wsp, slots_wsp, *[a[p + SMALL_WSP].reshape(wsp_view) for p in ("", "m_", "v_")], me,
                        after=[token])
    for dst, buf in zip((grads, deltas, new_m, new_v), outs):
        dst[SMALL_WSP] = buf.reshape(a[SMALL_WSP].shape)
    joined.update(zip(last, _split_wait("rs_join_wait", x_join, *s_join, list(outs))))
    update(last, "adamw_in_out", [])
    a[LOSS_ROW] = a["m_" + LOSS_ROW] = a["v_" + LOSS_ROW] = jnp.zeros((1, D_MODEL), F32)
    for names, rows, packed, slots in ((SMALL_EARLY, 8, packed_early, slots_early),
                                       (SMALL_LATE, 1, packed_late, slots_late)):
        outs = _adamw_small(packed, slots, _pack(a, names, rows), _pack({n: a["m_" + n] for n in names}, names, rows),
                            _pack({n: a["v_" + n] for n in names}, names, rows), me)
        for dst, buf in zip((grads, deltas, new_m, new_v), outs):
            dst.update(_unpack(buf, names, {n: a[n].shape for n in names}))
    loss = jnp.sum(grads[LOSS_ROW]) * np.float32(0.5 / D_MODEL)
    return (loss, grad_x[None], *[grads[n] for n in WEIGHTS], *[deltas[n] for n in WEIGHTS],
            *[new_m[n] for n in WEIGHTS], *[new_v[n] for n in WEIGHTS])
```

```python
import numpy as np
import jax
import jax.numpy as jnp
from jax import lax
from jax.experimental import pallas as pl
from jax.experimental.pallas import tpu as pltpu

F32 = jnp.float32
BF16 = jnp.bfloat16

SEQ = 2048
D_MODEL = 1024
HEAD_DIM = 64
ATTN_W = 512
SGU_W = 512
SGU_GROUPS = 8
CHUNK = 128
DILATIONS = (1, 4, 16)
N_SHARD = 4
IN_S = 640
OUT_S = 256
FF_S = 704
PROJ_W = N_SHARD * IN_S
FF = N_SHARD * FF_S
FF_CHUNKS = ((0, 1024), (1024, 2048), (2048, FF))
RMS_EPS = 1e-6
LN_EPS = 1e-5
ROPE_THETA = 500000.0
ATTN_SCALE = 1.0 / np.sqrt(HEAD_DIM)
NEG = -1e30
TM = 512
TM_FFN = 256
VMEM_LIMIT = 56 * 1024 * 1024

ADAM_LR = 0.001
ADAM_B1 = 0.9
ADAM_B2 = 0.999
ADAM_EPS = 1e-08
ADAM_WD = 0.01
ADAM_STEP = 10

MESH = pl.DeviceIdType.MESH
ANY = pl.BlockSpec(memory_space=pl.ANY)


def _dot(a, b):
    return jnp.dot(a, b, preferred_element_type=F32)


def _dot_nt(a, b):
    return lax.dot_general(a, b, (((1,), (1,)), ((), ())), preferred_element_type=F32)


def _dot_tn(a, b):
    return lax.dot_general(a, b, (((0,), (0,)), ((), ())), preferred_element_type=F32)


def _dot_exact(a, b):
    return jnp.dot(a, b, preferred_element_type=F32, precision=lax.Precision.HIGHEST)


def _dot_select(a, sel):
    hi = a.astype(BF16)
    lo = (a - hi.astype(F32)).astype(BF16)
    sel = sel.astype(BF16)
    return _dot(hi, sel) + _dot(lo, sel)


def _rms_stats(x):
    r = lax.rsqrt(jnp.mean(x * x, axis=-1, keepdims=True) + RMS_EPS)
    return x * r, r


def _rms_bwd(xh, r, gain, dy):
    dxh = dy * gain
    dx = r * (dxh - xh * jnp.mean(dxh * xh, axis=-1, keepdims=True))
    return dx, jnp.sum(dy * xh, axis=0, keepdims=True)


_ERF_ALPHA = (-2.72614225801306e-10, 2.77068142495902e-08, -2.10102402082508e-06, -5.69250639462346e-05,
              -7.34990630326855e-04, -2.95459980854025e-03, -1.60960333262415e-02)
_ERF_BETA = (-1.45660718464996e-05, -2.13374055278905e-04, -1.68282697438203e-03, -7.37332916720468e-03,
             -1.42647390514189e-02)


def _erf(x):
    x = jnp.clip(x, -4.0, 4.0)
    x2 = x * x
    p = jnp.full_like(x, _ERF_ALPHA[0])
    for a in _ERF_ALPHA[1:]:
        p = p * x2 + a
    q = jnp.full_like(x, _ERF_BETA[0])
    for b in _ERF_BETA[1:]:
        q = q * x2 + b
    return x * p / q


def _normal_cdf(x):
    return 0.5 * (1.0 + _erf(x * np.float32(1.0 / np.sqrt(2.0))))


def _gelu_grad(x, cdf):
    pdf = jnp.exp(-0.5 * x * x) * np.float32(1.0 / np.sqrt(2.0 * np.pi))
    return cdf + x * pdf


def _sigmoid(x):
    return 1.0 / (1.0 + jnp.exp(-x))


_INV_FREQ = tuple(float(np.float32(ROPE_THETA ** (-2.0 * j / 16.0))) for j in range(8))


def _rot_tables(pos):
    lane = lax.broadcasted_iota(jnp.int32, (1, 128), 1)
    d = lane & 63
    j = d & 7
    inv = jnp.zeros((1, 128), F32)
    for jj in range(8):
        inv = jnp.where(j == jj, _INV_FREQ[jj], inv)
    ang = pos.astype(F32) * inv
    c = jnp.cos(ang)
    s = jnp.sin(ang)
    cos_t = jnp.where(d < 16, c, 1.0)
    sin_a = jnp.where(d < 8, -s, 0.0)
    sin_b = jnp.where((d >= 8) & (d < 16), s, 0.0)
    return tuple(jnp.tile(t, (1, 4)) for t in (cos_t, sin_a, sin_b))


def _rope(x, tabs):
    cos_t, sin_a, sin_b = tabs
    return x * cos_t + pltpu.roll(x, 504, 1) * sin_a + pltpu.roll(x, 8, 1) * sin_b


def _rope_bwd(dy, tabs):
    cos_t, sin_a, sin_b = tabs
    return dy * cos_t + pltpu.roll(dy * sin_a, 8, 1) + pltpu.roll(dy * sin_b, 504, 1)


def _left_half():
    return lax.broadcasted_iota(jnp.int32, (CHUNK, CHUNK), 1) < HEAD_DIM


def _group_ones():
    lane = lax.broadcasted_iota(jnp.int32, (SGU_GROUPS, SGU_W), 1)
    row = lax.broadcasted_iota(jnp.int32, (SGU_GROUPS, SGU_W), 0)
    return ((lane >> 6) == row).astype(F32)


def _masked_spatial(w_ref):
    row = lax.broadcasted_iota(jnp.int32, (CHUNK, CHUNK), 0)
    col = lax.broadcasted_iota(jnp.int32, (CHUNK, CHUNK), 1)
    return [jnp.where(col <= row, w_ref[g], 0.0).astype(BF16) for g in range(SGU_GROUPS)]


def _sgu_core(u, vs, lg, lb, wm, bias_full):
    tm = u.shape[0]
    cdf_u, cdf_vs = _normal_cdf(u), _normal_cdf(vs)
    gu = u * cdf_u
    gv = vs * cdf_vs
    mu = jnp.mean(gv, axis=-1, keepdims=True)
    xc = gv - mu
    rstd = lax.rsqrt(jnp.mean(xc * xc, axis=-1, keepdims=True) + LN_EPS)
    xh = xc * rstd
    vnb = (xh * lg + lb).astype(BF16)
    left = _left_half()
    rows = []
    for c in range(tm // CHUNK):
        pieces = []
        for p in range(4):
            vp = vnb[c * CHUNK:(c + 1) * CHUNK, p * 128:(p + 1) * 128]
            pieces.append(jnp.where(left, _dot(wm[2 * p], vp), _dot(wm[2 * p + 1], vp)))
        rows.append(jnp.concatenate(pieces, axis=1) + bias_full)
    mixed = jnp.concatenate(rows, axis=0)
    return gu, xh, rstd, vnb, mixed, cdf_u, cdf_vs


def _resident(shape):
    n = len(shape)
    return pl.BlockSpec(shape, lambda *_: (0,) * n, pipeline_mode=pl.Buffered(1))


def _rows(ncol, tm=TM):
    return pl.BlockSpec((tm, ncol), lambda i: (i, 0))


def _acc(ncol, nrow=1):
    return pl.BlockSpec((nrow, ncol), lambda i: (0, 0))


HEAD_W = 128


def _view_rows(dil, width=ATTN_W, tm=TM):
    return pl.BlockSpec((tm // dil, dil * width), lambda i: (i, 0))


def _view_shape(dil, dtype, width=ATTN_W):
    return _sds((SEQ // dil, dil * width), dtype)


def _slab_scratch():
    return pltpu.VMEM((4, TM, 128), F32)


def _store_view(val, out_ref, slabs, dil):
    width = val.shape[1]
    for j in range(width // 128):
        slabs[j] = val[:, j * 128:(j + 1) * 128]
    for r in range(dil):
        for j in range(width // 128):
            c0 = r * width + j * 128
            out_ref[:, c0:c0 + 128] = slabs.at[j][pl.ds(r, TM // dil, stride=dil), :].astype(out_ref.dtype)


def _load_view(in_ref, slabs, dil, width=ATTN_W):
    for r in range(dil):
        for j in range(width // 128):
            c0 = r * width + j * 128
            slabs.at[j][pl.ds(r, TM // dil, stride=dil), :] = in_ref[:, c0:c0 + 128].astype(F32)
    return jnp.concatenate([slabs[j] for j in range(width // 128)], axis=1)


def _head_spread():
    m = lax.broadcasted_iota(jnp.int32, (HEAD_W, ATTN_W), 0)
    lane = lax.broadcasted_iota(jnp.int32, (HEAD_W, ATTN_W), 1)
    return (m == 16 * (lane >> 6)).astype(F32)


def _head_sum():
    lane = lax.broadcasted_iota(jnp.int32, (ATTN_W, HEAD_W), 0)
    m = lax.broadcasted_iota(jnp.int32, (ATTN_W, HEAD_W), 1)
    return ((lane >> 6) == (m >> 4)).astype(F32)


def _seq_params():
    return pltpu.CompilerParams(dimension_semantics=("arbitrary",), vmem_limit_bytes=VMEM_LIMIT)


def _sds(shape, dtype):
    return jax.ShapeDtypeStruct(shape, dtype)


class _Comm:
    def __init__(self, args, out_shape, n_sems, start, finish, aliased=False):
        self.args, self.out_shape, self.n_sems = list(args), list(out_shape), n_sems
        self.start, self.finish, self.aliased = start, finish, aliased


def _pcall(body, *, name, grid, in_specs, out_specs, out_shape, args, scratch_shapes=(), comms=(), after=()):
    single = not isinstance(out_shape, (list, tuple))
    out_specs = [out_specs] if single else list(out_specs)
    out_shape = [out_shape] if single else list(out_shape)
    n_in, n_out, n_scr = len(in_specs), len(out_shape), len(scratch_shapes)
    c_args = [a for c in comms for a in c.args]
    c_outs = [o for c in comms for o in c.out_shape]
    aliases, ai, ao = {}, n_in, n_out
    for c in comms:
        if c.aliased:
            aliases.update({ai + k: ao + k for k in range(len(c.args))})
        ai += len(c.args)
        ao += len(c.out_shape)
    sems = [pltpu.SemaphoreType.DMA((c.n_sems,)) for c in comms for _ in range(2)]
    steps = grid[0]

    def wrapped(*refs):
        o0 = n_in + len(c_args) + len(after)
        s0 = o0 + n_out + len(c_outs)
        m_in, m_out, m_sem = refs[n_in:n_in + len(c_args)], refs[o0 + n_out:s0], refs[s0 + n_scr:]

        def each(phase):
            ii = oi = 0
            for k, c in enumerate(comms):
                getattr(c, phase)(m_in[ii:ii + len(c.args)], m_out[oi:oi + len(c.out_shape)],
                                  m_sem[2 * k], m_sem[2 * k + 1])
                ii += len(c.args)
                oi += len(c.out_shape)

        if comms:
            @pl.when(pl.program_id(0) == 0)
            def _():
                each("start")

        body(*refs[:n_in], *refs[o0:o0 + n_out], *refs[s0:s0 + n_scr])

        if comms:
            @pl.when(pl.program_id(0) == steps - 1)
            def _():
                each("finish")

    res = pl.pallas_call(
        wrapped, name=name, grid=grid,
        in_specs=list(in_specs) + [ANY] * (len(c_args) + len(after)), out_specs=out_specs + [ANY] * len(c_outs),
        out_shape=out_shape + c_outs, scratch_shapes=list(scratch_shapes) + sems,
        input_output_aliases=aliases, compiler_params=_seq_params(),
    )(*args, *c_args, *after)
    mine = res[0] if single else list(res[:n_out])
    if not comms:
        return mine
    theirs, oi = [], n_out
    for c in comms:
        theirs.append(list(res[oi:oi + len(c.out_shape)]))
        oi += len(c.out_shape)
    return mine, theirs


def _in_pieces(g):
    lo, hi = 512 * g, 512 * (g + 1)
    return [(s, max(lo, IN_S * s) - IN_S * s, min(hi, IN_S * (s + 1)) - IN_S * s)
            for s in range(N_SHARD) if max(lo, IN_S * s) < min(hi, IN_S * (s + 1))]


def _inproj_fwd(x, pos, g_pre, w_in, lg, lb, w_sp, b_t, comms=()):
    def body(x_ref, pos_ref, g_ref, w_ref, lg_ref, lb_ref, wsp_ref, bt_ref, h_ref, u_ref, vs_ref, sgu_ref, *rest):
        qkv_refs, slabs = rest[:9], rest[9:]
        xh, _ = _rms_stats(x_ref[...])
        h = (xh * g_ref[...]).astype(BF16)
        h_ref[...] = h
        tabs = _rot_tables(pos_ref[...])

        def group(g):
            return jnp.concatenate([_dot(h, w_ref[s, :, a:b]) for s, a, b in _in_pieces(g)], axis=1)

        for t in range(3):
            val = group(t)
            if t < 2:
                val = _rope(val, tabs)
            if t == 0:
                val = val * np.float32(ATTN_SCALE)
            qkv_refs[t][...] = val.astype(BF16)
            for i, dil in enumerate(DILATIONS[1:]):
                _store_view(val, qkv_refs[3 * (i + 1) + t], slabs[t], dil)
        u = group(3)
        vs = group(4)
        u_ref[...] = u
        vs_ref[...] = vs
        bias_full = _dot_exact(bt_ref[...], _group_ones())
        gu, _, _, _, mixed, _, _ = _sgu_core(u, vs, lg_ref[...], lb_ref[...], _masked_spatial(wsp_ref), bias_full)
        sgu_ref[...] = gu * mixed

    return _pcall(
        body, name="inproj_sgu_fwd", grid=(SEQ // TM,),
        in_specs=[_rows(D_MODEL), _rows(1), _resident((1, D_MODEL)), _resident((N_SHARD, D_MODEL, IN_S)),
                  _resident((1, SGU_W)), _resident((1, SGU_W)), _resident((SGU_GROUPS, CHUNK, CHUNK)),
                  _resident((CHUNK, SGU_GROUPS))],
        out_specs=[_rows(D_MODEL), _rows(512), _rows(512), _rows(512)]
        + [_view_rows(dil) for dil in DILATIONS for _ in range(3)],
        out_shape=[_sds((SEQ, D_MODEL), BF16), _sds((SEQ, 512), F32), _sds((SEQ, 512), F32), _sds((SEQ, 512), F32)]
        + [_view_shape(dil, BF16) for dil in DILATIONS for _ in range(3)],
        scratch_shapes=[_slab_scratch() for _ in range(3)],
        args=(x, pos, g_pre, w_in, lg, lb, w_sp, b_t), comms=comms)


def _block_masks():
    row = lax.broadcasted_iota(jnp.int32, (CHUNK, CHUNK), 0)
    col = lax.broadcasted_iota(jnp.int32, (CHUNK, CHUNK), 1)
    return col <= row, col >= row


def _attn_fwd(qv, kv, vv, dil, comms=()):
    seg = SEQ // dil
    nblk = seg // CHUNK
    rps = 4 if nblk == 1 else 1

    def body(q_ref, k_ref, v_ref, o_ref, l_ref):
        left = _left_half()
        m_cur, m_prev = _block_masks()
        zero = jnp.zeros((CHUNK, CHUNK), BF16)
        ones = (jnp.where(left, 1.0, 0.0).astype(BF16), jnp.where(left, 0.0, 1.0).astype(BF16))

        sides = tuple(enumerate((left, ~left)))

        def rows(b):
            if isinstance(b, int):
                return b * CHUNK, max(b - 1, 0) * CHUNK
            return pl.multiple_of(b * CHUNK, CHUNK), pl.multiple_of(jnp.maximum(b - 1, 0) * CHUNK, CHUNK)

        def first(rr, b):
            r0, rp = rows(b)
            prev_ok = m_prev & (b > 0)
            tiles, scores = [], []
            for hp in range(4):
                ls = slice(rr * ATTN_W + hp * 128, rr * ATTN_W + (hp + 1) * 128)
                qp = q_ref[pl.ds(r0, CHUNK), ls]
                kc = k_ref[pl.ds(r0, CHUNK), ls]
                kp = k_ref[pl.ds(rp, CHUNK), ls] if nblk > 1 else None
                tiles.append((ls, v_ref[pl.ds(r0, CHUNK), ls], v_ref[pl.ds(rp, CHUNK), ls] if nblk > 1 else None))
                for _, hm in sides:
                    qh = jnp.where(hm, qp, zero)
                    sc = jnp.where(m_cur, _dot_nt(qh, kc), NEG)
                    sp = jnp.where(prev_ok, _dot_nt(qh, kp), NEG) if nblk > 1 else None
                    scores.append((sc, sp))
            return tiles, scores

        def second(scores):
            probs = []
            for sc, sp in scores:
                if nblk > 1:
                    m = jnp.max(jnp.maximum(sc, sp), axis=-1, keepdims=True)
                    pc = jnp.exp(sc - m)
                    pp = jnp.exp(sp - m)
                    probs.append((m, pc.astype(BF16), pp.astype(BF16), (pc + pp).astype(BF16)))
                else:
                    m = jnp.max(sc, axis=-1, keepdims=True)
                    pc = jnp.exp(sc - m).astype(BF16)
                    probs.append((m, pc, None, pc))
            return probs

        def third(rr, b, tiles, probs):
            r0, _ = rows(b)
            for hp, (ls, vc, vp) in enumerate(tiles):
                acc = jnp.zeros((CHUNK, CHUNK), F32)
                den = jnp.zeros((CHUNK, CHUNK), F32)
                for side, hm in sides:
                    _, pc, pp, psum = probs[2 * hp + side]
                    acc = acc + _dot(pc, jnp.where(hm, vc, zero))
                    if nblk > 1:
                        acc = acc + _dot(pp, jnp.where(hm, vp, zero))
                    den = den + _dot(psum, ones[side])
                o_ref[pl.ds(r0, CHUNK), ls] = (acc / den).astype(o_ref.dtype)
                lse = jnp.where(left, probs[2 * hp][0], probs[2 * hp + 1][0]) + jnp.log(den)
                l_ref[pl.ds(r0, CHUNK), rr * HEAD_W + 32 * hp:rr * HEAD_W + 32 * hp + 32] = lse[:, 48:80]

        def run(units):
            data = [first(rr, b) for rr, b in units]
            probs = [second(scores) for _, scores in data]
            for (rr, b), (tiles, _), pr in zip(units, data, probs):
                third(rr, b, tiles, pr)

        if nblk == 1:
            run([(rr, 0) for rr in range(rps)])
        else:
            def one(b, carry):
                run([(0, b)])
                return carry

            lax.fori_loop(0, nblk, one, 0)

    spec = pl.BlockSpec((seg, rps * ATTN_W), lambda r: (0, r))
    return _pcall(
        body, name=f"attn_fwd_d{dil}", grid=(dil // rps,),
        in_specs=[spec, spec, spec], out_specs=[spec, pl.BlockSpec((seg, rps * HEAD_W), lambda r: (0, r))],
        out_shape=[_sds((seg, dil * ATTN_W), BF16), _sds((seg, dil * HEAD_W), F32)],
        args=(qv, kv, vv), comms=comms)


def _lane_left(nrows):
    return lax.broadcasted_iota(jnp.int32, (nrows, CHUNK), 1) < HEAD_DIM


def _attn_rows(b):
    if isinstance(b, int):
        return b * CHUNK, max(b - 1, 0) * CHUNK
    return pl.multiple_of(b * CHUNK, CHUNK), pl.multiple_of(jnp.maximum(b - 1, 0) * CHUNK, CHUNK)


def _mix_out_fwd(o_list, l_list, sgu, x, w_out, g_attn, g_sgu, g_post, comms=()):
    def body(o1, o2, o3, l1, l2, l3, sgu_ref, x_ref, w_ref, ga_ref, gs_ref, gp_ref,
             attn_ref, mixed_ref, y_ref, x1_ref, lse1_ref, lse2_ref, lse3_ref, slabs_a, slabs_b):
        os = [o1[...], _load_view(o2, slabs_a, DILATIONS[1]), _load_view(o3, slabs_b, DILATIONS[2])]
        ls = [l1[...], _load_view(l2, slabs_a, DILATIONS[1], HEAD_W), _load_view(l3, slabs_b, DILATIONS[2], HEAD_W)]
        m = jnp.maximum(jnp.maximum(ls[0], ls[1]), ls[2])
        es = [jnp.exp(l - m) for l in ls]
        den = es[0] + es[1] + es[2]
        spread = _head_spread()
        attn = sum(_dot_select(e / den, spread) * o for e, o in zip(es, os))
        attn_ref[...] = attn
        lse = m + jnp.log(den)
        lse1_ref[...] = lse
        _store_view(lse, lse2_ref, slabs_a, DILATIONS[1])
        _store_view(lse, lse3_ref, slabs_b, DILATIONS[2])
        ah, _ = _rms_stats(attn)
        sh, _ = _rms_stats(sgu_ref[...])
        mixed = jnp.concatenate([ah * ga_ref[...], sh * gs_ref[...]], axis=1).astype(BF16)
        mixed_ref[...] = mixed
        y = _dot(mixed[:, 0:OUT_S], w_ref[0])
        for s in range(1, N_SHARD):
            y = y + _dot(mixed[:, s * OUT_S:(s + 1) * OUT_S], w_ref[s])
        y_ref[...] = y
        yh, _ = _rms_stats(y)
        x1_ref[...] = x_ref[...] + yh * gp_ref[...]

    return _pcall(
        body, name="mix_out_fwd", grid=(SEQ // TM,),
        in_specs=[_view_rows(dil) for dil in DILATIONS] + [_view_rows(dil, HEAD_W) for dil in DILATIONS]
        + [_rows(512), _rows(D_MODEL), _resident((N_SHARD, OUT_S, D_MODEL)),
           _resident((1, 512)), _resident((1, 512)), _resident((1, D_MODEL))],
        out_specs=[_rows(512), _rows(D_MODEL), _rows(D_MODEL), _rows(D_MODEL)]
        + [_view_rows(dil, HEAD_W) for dil in DILATIONS],
        out_shape=[_sds((SEQ, 512), F32), _sds((SEQ, D_MODEL), BF16), _sds((SEQ, D_MODEL), F32),
                   _sds((SEQ, D_MODEL), F32)] + [_view_shape(dil, F32, HEAD_W) for dil in DILATIONS],
        scratch_shapes=[_slab_scratch(), _slab_scratch()],
        args=(*o_list, *l_list, sgu, x, w_out, g_attn, g_sgu, g_post), comms=comms)


def _ffn_fwd_bwd(x1, target, w_gate, w_up, w_down, g_pre, g_post, comms=()):
    def body(x1_ref, t_ref, wg_ref, wu_ref, wd_ref, gpf_ref, gpo_ref,
             h2_ref, a_ref, dg_ref, dup_ref, df_ref, dx1_ref, loss_ref, dgpf_ref, dgpo_ref, g_scr, up_scr):
        @pl.when(pl.program_id(0) == 0)
        def _():
            loss_ref[...] = jnp.zeros_like(loss_ref)
            dgpf_ref[...] = jnp.zeros_like(dgpf_ref)
            dgpo_ref[...] = jnp.zeros_like(dgpo_ref)

        x1 = x1_ref[...]
        gpf = gpf_ref[...]
        gpo = gpo_ref[...]
        xh, r = _rms_stats(x1)
        h2 = (xh * gpf).astype(BF16)
        h2_ref[...] = h2
        f = jnp.zeros((TM_FFN, D_MODEL), F32)
        for c0, c1 in FF_CHUNKS:
            g = _dot_nt(h2, wg_ref[c0:c1, :])
            up = _dot_nt(h2, wu_ref[c0:c1, :])
            g_scr[:, c0:c1] = g
            up_scr[:, c0:c1] = up
            a = (g * _sigmoid(g) * up).astype(BF16)
            a_ref[:, c0:c1] = a
            f = f + _dot(a, wd_ref[c0:c1, :])
        fh, rf = _rms_stats(f)
        diff = x1 + fh * gpo - t_ref[...]
        loss_ref[...] += jnp.sum(diff * diff, axis=0, keepdims=True)
        dout = diff * np.float32(1.0 / D_MODEL)
        df, dgpo = _rms_bwd(fh, rf, gpo, dout)
        dgpo_ref[...] += dgpo
        dfb = df.astype(BF16)
        df_ref[...] = dfb
        dh2 = jnp.zeros((TM_FFN, D_MODEL), F32)
        for c0, c1 in FF_CHUNKS:
            da = _dot_nt(dfb, wd_ref[c0:c1, :])
            g = g_scr[:, c0:c1]
            up = up_scr[:, c0:c1]
            sg = _sigmoid(g)
            dup = (da * (g * sg)).astype(BF16)
            dg = (da * up * (sg * (1.0 + g * (1.0 - sg)))).astype(BF16)
            dg_ref[:, c0:c1] = dg
            dup_ref[:, c0:c1] = dup
            dh2 = dh2 + _dot(dg, wg_ref[c0:c1, :]) + _dot(dup, wu_ref[c0:c1, :])
        dx, dgpf = _rms_bwd(xh, r, gpf, dh2)
        dgpf_ref[...] += dgpf
        dx1_ref[...] = dout + dx

    return _pcall(
        body, name="ffn_fwd_bwd", grid=(SEQ // TM_FFN,),
        in_specs=[_rows(D_MODEL, TM_FFN), _rows(D_MODEL, TM_FFN), _resident((FF, D_MODEL)),
                  _resident((FF, D_MODEL)), _resident((FF, D_MODEL)),
                  _resident((1, D_MODEL)), _resident((1, D_MODEL))],
        out_specs=[_rows(D_MODEL, TM_FFN), _rows(FF, TM_FFN), _rows(FF, TM_FFN), _rows(FF, TM_FFN),
                   _rows(D_MODEL, TM_FFN), _rows(D_MODEL, TM_FFN), _acc(D_MODEL), _acc(D_MODEL), _acc(D_MODEL)],
        out_shape=[_sds((SEQ, D_MODEL), BF16), _sds((SEQ, FF), BF16), _sds((SEQ, FF), BF16),
                   _sds((SEQ, FF), BF16), _sds((SEQ, D_MODEL), BF16), _sds((SEQ, D_MODEL), F32),
                   _sds((1, D_MODEL), F32), _sds((1, D_MODEL), F32), _sds((1, D_MODEL), F32)],
        scratch_shapes=[pltpu.VMEM((TM_FFN, FF), F32), pltpu.VMEM((TM_FFN, FF), F32)],
        args=(x1, target, w_gate, w_up, w_down, g_pre, g_post), comms=comms)


def _wgrad(a, b, a_spec, b_spec, out_block, name, comms=()):
    def body(a_ref, b_ref, o_ref, ob_ref):
        av = a_ref[0] if len(a_ref.shape) == 3 else a_ref[...]
        bv = b_ref[0] if len(b_ref.shape) == 3 else b_ref[...]
        g = _dot_tn(av, bv)
        o_ref[0] = g
        ob_ref[0] = g.astype(BF16)

    return _pcall(
        body, name=name, grid=(N_SHARD,),
        in_specs=[a_spec, b_spec],
        out_specs=[pl.BlockSpec((1,) + out_block, lambda s: (s, 0, 0))] * 2,
        out_shape=[_sds((N_SHARD,) + out_block, F32), _sds((N_SHARD,) + out_block, BF16)],
        args=(a, b), comms=comms)


def _outproj_bwd(dx1, y, attn, sgu, w_out, g_post, g_attn, g_sgu, u, vs, lg, lb, w_sp, b_t, comms=(), after=()):
    sgu_bwd = _sgu_bwd_body()

    def body(dx1_ref, y_ref, attn_ref, sgu_ref, w_ref, gp_ref, ga_ref, gs_ref, u_ref, vs_ref, lg_ref, lb_ref,
             wsp_ref, bt_ref, dy_ref, du_ref, dvs_ref, dgp_ref, dga_ref, dgs_ref, dw_ref, db_ref, dlg_ref, dlb_ref,
             *rest):
        dattn_refs, delta_refs, (slabs_a, slabs_b, dsgu_ref, dbias_scr) = rest[0:3], rest[3:6], rest[6:]

        @pl.when(pl.program_id(0) == 0)
        def _():
            dgp_ref[...] = jnp.zeros_like(dgp_ref)
            dga_ref[...] = jnp.zeros_like(dga_ref)
            dgs_ref[...] = jnp.zeros_like(dgs_ref)

        yh, ry = _rms_stats(y_ref[...])
        dy, dgp = _rms_bwd(yh, ry, gp_ref[...], dx1_ref[...])
        dgp_ref[...] += dgp
        dyb = dy.astype(BF16)
        dy_ref[...] = dyb
        dmixed = jnp.concatenate([_dot_nt(dyb, w_ref[s]) for s in range(N_SHARD)], axis=1)
        attn = attn_ref[...]
        ah, ra = _rms_stats(attn)
        dattn, dga = _rms_bwd(ah, ra, ga_ref[...], dmixed[:, 0:512])
        dga_ref[...] += dga
        sh, rs = _rms_stats(sgu_ref[...])
        dsgu, dgs = _rms_bwd(sh, rs, gs_ref[...], dmixed[:, 512:1024])
        dgs_ref[...] += dgs
        dsgu_ref[...] = dsgu
        delta = _dot_select(dattn * attn, _head_sum())
        dattn_refs[0][...] = dattn.astype(BF16)
        delta_refs[0][...] = delta
        for i, dil in enumerate(DILATIONS[1:]):
            _store_view(dattn, dattn_refs[i + 1], slabs_a, dil)
            _store_view(delta, delta_refs[i + 1], slabs_b, dil)
        sgu_bwd(u_ref, vs_ref, dsgu_ref, lg_ref, lb_ref, wsp_ref, bt_ref,
                du_ref, dvs_ref, dw_ref, db_ref, dlg_ref, dlb_ref, dbias_scr)

    return _pcall(
        body, name="outproj_sgu_bwd", grid=(SEQ // TM,),
        in_specs=[_rows(D_MODEL), _rows(D_MODEL), _rows(512), _rows(512), _resident((N_SHARD, OUT_S, D_MODEL)),
                  _resident((1, D_MODEL)), _resident((1, 512)), _resident((1, 512)),
                  _rows(SGU_W), _rows(SGU_W), _resident((1, SGU_W)), _resident((1, SGU_W)),
                  _resident((SGU_GROUPS, CHUNK, CHUNK)), _resident((CHUNK, SGU_GROUPS))],
        out_specs=[_rows(D_MODEL), _rows(SGU_W), _rows(SGU_W), _acc(D_MODEL), _acc(512), _acc(512),
                   pl.BlockSpec((SGU_GROUPS, CHUNK, CHUNK), lambda i: (0, 0, 0)), _acc(CHUNK, SGU_GROUPS),
                   _acc(SGU_W), _acc(SGU_W)]
        + [_view_rows(dil) for dil in DILATIONS] + [_view_rows(dil, HEAD_W) for dil in DILATIONS],
        out_shape=[_sds((SEQ, D_MODEL), BF16), _sds((SEQ, SGU_W), BF16), _sds((SEQ, SGU_W), BF16),
                   _sds((1, D_MODEL), F32), _sds((1, 512), F32), _sds((1, 512), F32),
                   _sds((SGU_GROUPS, CHUNK, CHUNK), F32), _sds((SGU_GROUPS, CHUNK), F32),
                   _sds((1, SGU_W), F32), _sds((1, SGU_W), F32)]
        + [_view_shape(dil, BF16) for dil in DILATIONS] + [_view_shape(dil, F32, HEAD_W) for dil in DILATIONS],
        scratch_shapes=[_slab_scratch(), _slab_scratch(), pltpu.VMEM((TM, SGU_W), F32),
                        pltpu.VMEM((CHUNK, SGU_W), F32)],
        args=(dx1, y, attn, sgu, w_out, g_post, g_attn, g_sgu, u, vs, lg, lb, w_sp, b_t), comms=comms, after=after)


def _sgu_bwd_body():
    nsteps = SEQ // TM

    def body(u_ref, vs_ref, ds_ref, lg_ref, lb_ref, w_ref, bt_ref,
             du_ref, dvs_ref, dw_ref, db_ref, dlg_ref, dlb_ref, dbias_scr):
        i = pl.program_id(0)

        @pl.when(i == 0)
        def _():
            dw_ref[...] = jnp.zeros_like(dw_ref)
            dlg_ref[...] = jnp.zeros_like(dlg_ref)
            dlb_ref[...] = jnp.zeros_like(dlb_ref)
            dbias_scr[...] = jnp.zeros_like(dbias_scr)

        wm = _masked_spatial(w_ref)
        ones_g = _group_ones()
        bias_full = _dot_exact(bt_ref[...], ones_g)
        u = u_ref[...]
        vs = vs_ref[...]
        lg = lg_ref[...]
        gu, xh, rstd, vnb, mixed, cdf_u, cdf_vs = _sgu_core(u, vs, lg, lb_ref[...], wm, bias_full)
        dsgu = ds_ref[...]
        du_ref[...] = (dsgu * mixed * _gelu_grad(u, cdf_u)).astype(BF16)
        dmixed = dsgu * gu
        left = _left_half()
        dvn_rows = []
        for c in range(TM // CHUNK):
            rs = slice(c * CHUNK, (c + 1) * CHUNK)
            dm_c = dmixed[rs, :]
            dbias_scr[...] += dm_c
            pieces = []
            for p in range(4):
                ls = slice(p * 128, (p + 1) * 128)
                dmp = dm_c[:, ls]
                vp = vnb[rs, ls]
                dmb = dmp.astype(BF16)
                zero = jnp.zeros_like(dmb)
                dw_ref[2 * p] += _dot_nt(jnp.where(left, dmb, zero), vp)
                dw_ref[2 * p + 1] += _dot_nt(jnp.where(left, zero, dmb), vp)
                pieces.append(jnp.where(left, _dot_tn(wm[2 * p], dmb), _dot_tn(wm[2 * p + 1], dmb)))
            dvn_rows.append(jnp.concatenate(pieces, axis=1))
        dvn = jnp.concatenate(dvn_rows, axis=0)
        dlg_ref[...] += jnp.sum(dvn * xh, axis=0, keepdims=True)
        dlb_ref[...] += jnp.sum(dvn, axis=0, keepdims=True)
        dxh = dvn * lg
        dgv = rstd * (dxh - jnp.mean(dxh, axis=-1, keepdims=True) - xh * jnp.mean(dxh * xh, axis=-1, keepdims=True))
        dvs_ref[...] = (dgv * _gelu_grad(vs, cdf_vs)).astype(BF16)

        @pl.when(i == nsteps - 1)
        def _():
            row = lax.broadcasted_iota(jnp.int32, (CHUNK, CHUNK), 0)
            col = lax.broadcasted_iota(jnp.int32, (CHUNK, CHUNK), 1)
            for g in range(SGU_GROUPS):
                dw_ref[g] = jnp.where(col <= row, dw_ref[g], 0.0)
            db_ref[...] = lax.dot_general(ones_g, dbias_scr[...], (((1,), (1,)), ((), ())),
                                          preferred_element_type=F32, precision=lax.Precision.HIGHEST)

    return body


def _attn_bwd(qv, kv, vv, dov, deltav, lsev, dil, comms=(), after=()):
    seg = SEQ // dil
    nblk = seg // CHUNK
    rps = 4 if nblk == 1 else 1

    def body(q_ref, k_ref, v_ref, do_ref, dl_ref, lse_ref, dq_ref, dk_ref, dv_ref, dk_wait, dv_wait):
        left = _left_half()
        m_cur, m_prev = _block_masks()
        sides = tuple(enumerate((left, ~left)))
        zero = jnp.zeros((CHUNK, CHUNK), BF16)

        def first(rr, b, both):
            r0, rp = _attn_rows(b)
            keys = pl.ds(rp, 2 * CHUNK) if both else pl.ds(r0, CHUNK)
            tiles, heads = [], []
            for hp in range(4):
                ls = slice(rr * ATTN_W + hp * 128, rr * ATTN_W + (hp + 1) * 128)
                qp = q_ref[pl.ds(r0, CHUNK), ls]
                dop = do_ref[pl.ds(r0, CHUNK), ls]
                k2 = k_ref[keys, ls]
                v2 = v_ref[keys, ls]
                tiles.append((ls, k2))
                for _, hm in sides:
                    qh = jnp.where(hm, qp, zero)
                    doh = jnp.where(hm, dop, zero)
                    heads.append((qh, doh, _dot_nt(k2, qh), _dot_nt(v2, doh)))
            return tiles, heads

        def second(rr, b, heads, both):
            r0, _ = _attn_rows(b)
            ok = jnp.concatenate([m_cur, m_prev], axis=0) if both else m_prev
            lanes = slice(rr * HEAD_W, (rr + 1) * HEAD_W)
            lse_t = lse_ref[pl.ds(r0, CHUNK), lanes].T
            dl_t = dl_ref[pl.ds(r0, CHUNK), lanes].T
            out = []
            for i, (_, _, s_t, dp_t) in enumerate(heads):
                p = jnp.exp(jnp.where(ok, s_t - lse_t[16 * i:16 * i + 1, :], NEG))
                out.append((p.astype(BF16), (p * (dp_t - dl_t[16 * i:16 * i + 1, :])).astype(BF16)))
            return out

        def third(rr, b, tiles, heads, probs, both):
            r0, rp = _attn_rows(b)
            nk = 2 * CHUNK if both else CHUNK
            left_k = _lane_left(nk)
            zero_k = jnp.zeros((nk, CHUNK), BF16)
            for hp, (ls, k2) in enumerate(tiles):
                dq = jnp.zeros((CHUNK, CHUNK), F32)
                dk2 = jnp.zeros((nk, CHUNK), F32)
                dv2 = jnp.zeros((nk, CHUNK), F32)
                for side in range(2):
                    qh, doh, _, _ = heads[2 * hp + side]
                    p, ds = probs[2 * hp + side]
                    dq = dq + _dot_tn(ds, jnp.where(left_k if side == 0 else ~left_k, k2, zero_k))
                    dk2 = dk2 + _dot(ds, qh)
                    dv2 = dv2 + _dot(p, doh)
                dq_ref[pl.ds(r0, CHUNK), ls] = dq.astype(dq_ref.dtype)
                if nblk == 1:
                    dk_ref[pl.ds(r0, CHUNK), ls] = dk2.astype(dk_ref.dtype)
                    dv_ref[pl.ds(r0, CHUNK), ls] = dv2.astype(dv_ref.dtype)
                elif both:
                    dk_ref[pl.ds(rp, CHUNK), ls] = (dk_wait[:, ls] + dk2[0:CHUNK]).astype(dk_ref.dtype)
                    dv_ref[pl.ds(rp, CHUNK), ls] = (dv_wait[:, ls] + dv2[0:CHUNK]).astype(dv_ref.dtype)
                    dk_wait[:, ls] = dk2[CHUNK:]
                    dv_wait[:, ls] = dv2[CHUNK:]
                else:
                    dk_wait[:, ls] = dk2
                    dv_wait[:, ls] = dv2

        def run(units, both):
            data = [first(rr, b, both) for rr, b in units]
            probs = [second(rr, b, heads, both) for (rr, b), (_, heads) in zip(units, data)]
            for (rr, b), (tiles, heads), pr in zip(units, data, probs):
                third(rr, b, tiles, heads, pr, both)

        run([(rr, 0) for rr in range(rps)], False)
        if nblk > 1:
            def one(b, carry):
                run([(0, b)], True)
                return carry

            lax.fori_loop(1, nblk, one, 0)
            last = (nblk - 1) * CHUNK
            dk_ref[last:last + CHUNK, :] = dk_wait[...].astype(dk_ref.dtype)
            dv_ref[last:last + CHUNK, :] = dv_wait[...].astype(dv_ref.dtype)

    spec = pl.BlockSpec((seg, rps * ATTN_W), lambda r: (0, r))
    return _pcall(
        body, name=f"attn_bwd_d{dil}", grid=(dil // rps,),
        in_specs=[spec] * 4 + [pl.BlockSpec((seg, rps * HEAD_W), lambda r: (0, r))] * 2, out_specs=[spec] * 3,
        out_shape=[_sds((seg, dil * ATTN_W), BF16)] * 3,
        scratch_shapes=[pltpu.VMEM((CHUNK, ATTN_W), F32), pltpu.VMEM((CHUNK, ATTN_W), F32)],
        args=(qv, kv, vv, dov, deltav, lsev), comms=comms, after=after)


def _inproj_bwd(dqs, dks, dvs, du, dvs_sgu, pos, x, dx1, w_in, g_pre, comms=()):
    def body(dq1, dq2, dq3, dk1, dk2, dk3, dv1, dv2, dv3, du_ref, dvs_ref, pos_ref, x_ref, dx1_ref, w_ref, g_ref,
             dproj_ref, gx_ref, dg_ref, slabs_a, slabs_b):
        @pl.when(pl.program_id(0) == 0)
        def _():
            dg_ref[...] = jnp.zeros_like(dg_ref)

        def total(r1, r2, r3):
            return r1[...] + _load_view(r2, slabs_a, DILATIONS[1]) + _load_view(r3, slabs_b, DILATIONS[2])

        tabs = _rot_tables(pos_ref[...])
        groups = {3: du_ref[...], 4: dvs_ref[...]}
        dh = jnp.zeros((TM, D_MODEL), F32)
        for g in (3, 4, 0, 1, 2):
            if g == 0:
                groups[g] = _rope_bwd(total(dq1, dq2, dq3) * np.float32(ATTN_SCALE), tabs).astype(BF16)
            elif g == 1:
                groups[g] = _rope_bwd(total(dk1, dk2, dk3), tabs).astype(BF16)
            elif g == 2:
                groups[g] = total(dv1, dv2, dv3).astype(BF16)
            dproj_ref[:, 512 * g:512 * (g + 1)] = groups[g]
            off = 0
            for s, a, b in _in_pieces(g):
                dh = dh + _dot_nt(groups[g][:, off:off + b - a], w_ref[s, :, a:b])
                off += b - a
        g = g_ref[...]
        xh, r = _rms_stats(x_ref[...])
        dx, dg = _rms_bwd(xh, r, g, dh)
        dg_ref[...] += dg
        gx_ref[...] = dx1_ref[...] + dx

    return _pcall(
        body, name="inproj_bwd", grid=(SEQ // TM,),
        in_specs=[_view_rows(dil) for dil in DILATIONS] * 3
        + [_rows(512), _rows(512), _rows(1), _rows(D_MODEL), _rows(D_MODEL),
           _resident((N_SHARD, D_MODEL, IN_S)), _resident((1, D_MODEL))],
        out_specs=[_rows(PROJ_W), _rows(D_MODEL), _acc(D_MODEL)],
        out_shape=[_sds((SEQ, PROJ_W), BF16), _sds((SEQ, D_MODEL), F32), _sds((1, D_MODEL), F32)],
        scratch_shapes=[_slab_scratch(), _slab_scratch()],
        args=(*dqs, *dks, *dvs, du, dvs_sgu, pos, x, dx1, w_in, g_pre), comms=comms)


def _coords():
    return lax.axis_index("x"), lax.axis_index("y"), lax.axis_index("c")


def _other_chips(x, y):
    return [(1 - x, y), (x, 1 - y), (1 - x, 1 - y)]


WEIGHTS = ("pre_mix_norm", "w_in", "sgu_ln_gain", "sgu_ln_bias", "sgu_w_spatial", "sgu_b_spatial", "attn_out_norm",
           "sgu_out_norm", "w_out", "post_mix_norm", "pre_ffn_norm", "w_gate", "w_up", "w_down", "post_ffn_norm")
SMALL = ("pre_mix_norm", "post_mix_norm", "pre_ffn_norm", "post_ffn_norm", "sgu_ln_gain", "sgu_ln_bias",
         "attn_out_norm", "sgu_out_norm", "sgu_w_spatial", "sgu_b_spatial")


def _remote(src, dst, send_sem, recv_sem, to):
    return pltpu.make_async_remote_copy(src_ref=src, dst_ref=dst, send_sem=send_sem, recv_sem=recv_sem,
                                        device_id=to, device_id_type=MESH)


def _halves(a):
    *lead, rows, cols = a.shape
    return a.reshape(*lead, 2, rows // 2, cols)


def _gather_ici(shards):
    n = len(shards)

    def desc(ins, outs, ss, rs, j, w, landed):
        x, y, c = _coords()
        cx, cy = _other_chips(x, y)[j]
        shard = 2 * cx + cy if landed else 2 * x + y
        return _remote(ins[w].at[c], outs[w].at[shard, c], ss.at[j * n + w], rs.at[j * n + w], (cx, cy, c))

    def start(ins, outs, ss, rs):
        for j in range(3):
            for w in range(n):
                desc(ins, outs, ss, rs, j, w, False).start()

    def finish(ins, outs, ss, rs):
        for j in range(3):
            for w in range(n):
                desc(ins, outs, ss, rs, j, w, True).wait_recv()
                desc(ins, outs, ss, rs, j, w, False).wait_send()

    return _Comm(shards, [_sds((N_SHARD,) + s.shape, s.dtype) for s in shards], 3 * n, start, finish)


def _gather_pass(fulls):
    n = len(fulls)

    def desc(bufs, ss, rs, j, w, landed):
        x, y, c = _coords()
        cx, cy = _other_chips(x, y)[j]
        shard = 2 * cx + cy
        return _remote(bufs[w].at[shard, c], bufs[w].at[shard, 1 - c if landed else c],
                       ss.at[j * n + w], rs.at[j * n + w], (x, y, 1 - c))

    def start(ins, outs, ss, rs):
        for j in range(3):
            for w in range(n):
                desc(outs, ss, rs, j, w, False).start()

    def finish(ins, outs, ss, rs):
        for j in range(3):
            for w in range(n):
                desc(outs, ss, rs, j, w, True).wait_recv()
                desc(outs, ss, rs, j, w, False).wait_send()

    return _Comm(fulls, [_sds(f.shape, f.dtype) for f in fulls], 3 * n, start, finish, aliased=True)


def _rs_sibling(gws):
    n = len(gws)

    def desc(ins, outs, ss, rs, w):
        x, y, c = _coords()
        return _remote(ins[w].at[:, 1 - c], outs[w], ss.at[w], rs.at[w], (x, y, 1 - c))

    def start(ins, outs, ss, rs):
        for w in range(n):
            desc(ins, outs, ss, rs, w).start()

    def finish(ins, outs, ss, rs):
        for w in range(n):
            desc(ins, outs, ss, rs, w).wait()

    out_shape = [_sds((N_SHARD, g.shape[1] // 2, g.shape[2]), g.dtype) for g in gws]
    return _Comm([_halves(g) for g in gws], out_shape, n, start, finish)


def _rs_chips(pbs):
    n = len(pbs)

    def desc(ins, outs, ss, rs, j, w):
        x, y, c = _coords()
        cx, cy = _other_chips(x, y)[j]
        return _remote(ins[w].at[2 * cx + cy], outs[w].at[j], ss.at[j * n + w], rs.at[j * n + w], (cx, cy, c))

    def start(ins, outs, ss, rs):
        for j in range(3):
            for w in range(n):
                desc(ins, outs, ss, rs, j, w).start()

    def finish(ins, outs, ss, rs):
        for j in range(3):
            for w in range(n):
                desc(ins, outs, ss, rs, j, w).wait()

    return _Comm(pbs, [_sds((3,) + p.shape[1:], p.dtype) for p in pbs], 3 * n, start, finish)


def _rs_join(halves, split=False):
    n = len(halves)

    def desc(bufs, ss, rs, w, landed):
        x, y, c = _coords()
        return _remote(bufs[w].at[c], bufs[w].at[1 - c if landed else c], ss.at[w], rs.at[w], (x, y, 1 - c))

    def start(ins, outs, ss, rs):
        for w in range(n):
            desc(ins if split else outs, ss, rs, w, False).start()

    def finish(ins, outs, ss, rs):
        for w in range(n):
            desc(ins if split else outs, ss, rs, w, True).wait_recv()
            desc(ins if split else outs, ss, rs, w, False).wait_send()

    if split:
        return _Comm(halves, [], n, start, finish)
    return _Comm(halves, [_sds(h.shape, h.dtype) for h in halves], n, start, finish, aliased=True)


def _small_exchange(buf):
    def desc(ins, outs, ss, rs, k, landed):
        x, y, c = _coords()
        px = 1 - x if (k >> 2) & 1 else x
        py = 1 - y if (k >> 1) & 1 else y
        pc = 1 - c if k & 1 else c
        slot = 4 * px + 2 * py + pc if landed else 4 * x + 2 * y + c
        return _remote(ins[0], outs[0].at[slot], ss.at[k - 1], rs.at[k - 1], (px, py, pc))

    def start(ins, outs, ss, rs):
        for k in range(1, 8):
            desc(ins, outs, ss, rs, k, False).start()

    def finish(ins, outs, ss, rs):
        for k in range(1, 8):
            desc(ins, outs, ss, rs, k, True).wait_recv()
            desc(ins, outs, ss, rs, k, False).wait_send()

    return _Comm([buf], [_sds((8,) + buf.shape, buf.dtype)], 7, start, finish)


HBM = pl.BlockSpec(memory_space=pltpu.HBM)
SEM = pl.BlockSpec(memory_space=pltpu.SEMAPHORE)
DATAFLOW = pltpu.SideEffectType.DATAFLOW_SIDE_EFFECTING


def _split_starts(name, comms, after):
    srcs = [[pltpu.with_memory_space_constraint(s, pltpu.HBM) for s in c.args] for c in comms]
    lands = [[pltpu.with_memory_space_constraint(lax.empty(o.shape, o.dtype), pltpu.HBM) for o in c.out_shape]
             for c in comms]
    bufs = [b for k in range(len(comms)) for b in srcs[k] + lands[k]]
    nb, nc = len(bufs), len(comms)

    def body(*refs):
        sems = refs[nb + 1:nb + 1 + 2 * nc]
        off = 0
        for k, c in enumerate(comms):
            ns, nl = len(srcs[k]), len(lands[k])
            c.start(refs[off:off + ns], refs[off + ns:off + ns + nl], sems[2 * k], sems[2 * k + 1])
            off += ns + nl
        refs[-1][...] = jnp.zeros_like(refs[-1])

    res = pl.pallas_call(
        body, name=name,
        out_shape=(*[pltpu.SemaphoreType.DMA((c.n_sems,)) for c in comms for _ in range(2)],
                   *[pltpu.HBM(b.shape, b.dtype) for b in bufs], _sds((8, 128), F32)),
        in_specs=[HBM] * nb + [ANY],
        out_specs=(*[SEM] * (2 * nc), *[HBM] * nb, pl.BlockSpec(memory_space=pltpu.VMEM)),
        input_output_aliases={i: 2 * nc + i for i in range(nb)},
        compiler_params=pltpu.CompilerParams(has_side_effects=DATAFLOW),
    )(*bufs, after)
    states, off = [], 2 * nc
    for k in range(nc):
        ns, nl = len(srcs[k]), len(lands[k])
        states.append((res[2 * k], res[2 * k + 1], list(res[off:off + ns]), list(res[off + ns:off + ns + nl])))
        off += ns + nl
    return states, res[-1]


def _split_wait(name, comm, send_sems, recv_sems, srcs, lands, after):
    ns, nb = len(srcs), len(lands)
    after = list(after) if isinstance(after, (list, tuple)) else [after]

    def body(*refs):
        comm.finish(refs[:ns], refs[ns:ns + nb], refs[ns + nb], refs[ns + nb + 1])

    res = pl.pallas_call(
        body, name=name,
        out_shape=tuple(pltpu.HBM(b.shape, b.dtype) for b in srcs + lands),
        in_specs=[HBM] * (ns + nb) + [SEM, SEM] + [ANY] * len(after), out_specs=tuple([HBM] * (ns + nb)),
        input_output_aliases={i: i for i in range(ns + nb)},
        compiler_params=pltpu.CompilerParams(has_side_effects=DATAFLOW),
    )(*srcs, *lands, send_sems, recv_sems, *after)
    return list(res[ns:]) or list(res)


LOSS_ROW = "loss_cols"
SMALL_EARLY = ("post_mix_norm", "pre_ffn_norm", "post_ffn_norm", "sgu_ln_gain", "sgu_ln_bias", "attn_out_norm",
               "sgu_out_norm", "sgu_b_spatial", LOSS_ROW)
SMALL_LATE = ("pre_mix_norm",)
SMALL_WSP = "sgu_w_spatial"


def _pack(d, names, rows):
    flat = [d[n].reshape(-1) for n in names]
    used = sum(f.shape[0] for f in flat)
    flat.append(jnp.zeros((rows * 1024 - used,), F32))
    return jnp.concatenate(flat).reshape(rows, 1024)


def _unpack(buf, names, shapes):
    flat = buf.reshape(-1)
    out, off = {}, 0
    for n in names:
        size = int(np.prod(shapes[n]))
        out[n] = flat[off:off + size].reshape(shapes[n])
        off += size
    return out


def _chip_sums(gbs, recvs, shard_core):
    n = len(gbs)
    _, rows, cols = gbs[0].shape
    hw = rows // 2

    def body(sc_ref, *refs):
        for k in range(n):
            refs[2 * n + k][...] = (refs[k][...].astype(F32) + refs[n + k][...].astype(F32)).astype(BF16)

    def other(s, sc):
        return jnp.where(s >= sc[0], s + 1, s)

    mine = pl.BlockSpec((1, hw, cols), lambda s, sc: (other(s, sc), sc[1], 0))
    plain = pl.BlockSpec((1, hw, cols), lambda s, sc: (other(s, sc), 0, 0))
    return pl.pallas_call(
        body, name="rs_chip_sums",
        grid_spec=pltpu.PrefetchScalarGridSpec(num_scalar_prefetch=1, grid=(N_SHARD - 1,),
                                               in_specs=[mine] * n + [plain] * n, out_specs=[plain] * n),
        out_shape=[_sds((N_SHARD, hw, cols), BF16)] * n,
        compiler_params=_seq_params(),
    )(shard_core, *gbs, *recvs)


def _final_sums(gws, recv_sibs, recv_chips, shard_core):
    n = len(gws)
    halves = [(g.shape[1] // 2, g.shape[2]) for g in gws]

    def body(sc_ref, *refs):
        for k in range(n):
            acc = refs[k][0] + refs[n + k][0]
            for j in range(3):
                acc = acc + refs[2 * n + k][j].astype(F32)
            refs[3 * n + k][0] = acc

    def specs(lead, index):
        return [pl.BlockSpec((lead, hw, cols), index) for hw, cols in halves]

    return pl.pallas_call(
        body, name="rs_final_sums",
        grid_spec=pltpu.PrefetchScalarGridSpec(
            num_scalar_prefetch=1, grid=(1,),
            in_specs=specs(1, lambda i, sc: (sc[0], sc[1], 0)) + specs(1, lambda i, sc: (sc[0], 0, 0))
            + specs(3, lambda i, sc: (0, 0, 0)),
            out_specs=specs(1, lambda i, sc: (sc[1], 0, 0))),
        out_shape=[_sds((2, hw, cols), F32) for hw, cols in halves],
        compiler_params=_seq_params(),
    )(shard_core, *gws, *recv_sibs, *recv_chips)


ADAM_BLOCKS = 4


def _adamw_multi(ws, gs, ms, vs, name, after=()):
    n = len(ws)

    def body(*refs):
        outs = refs[4 * n + len(after):]
        for k in range(n):
            g = refs[n + k][...]
            d, m, v = _adam_math(refs[k][...], g, refs[2 * n + k][...], refs[3 * n + k][...])
            outs[4 * k][...], outs[4 * k + 1][...], outs[4 * k + 2][...], outs[4 * k + 3][...] = g, d, m, v

    specs = [pl.BlockSpec((w.shape[0] // ADAM_BLOCKS, w.shape[1]), lambda i: (i, 0)) for w in ws]
    res = pl.pallas_call(
        body, name=name, grid=(ADAM_BLOCKS,), in_specs=specs * 4 + [ANY] * len(after),
        out_specs=[s for s in specs for _ in range(4)],
        out_shape=[_sds(w.shape, F32) for w in ws for _ in range(4)],
        compiler_params=_seq_params(),
    )(*ws, *gs, *ms, *vs, *after)
    return [tuple(res[4 * k:4 * k + 4]) for k in range(n)]


def _wgrad_ff(a_list, b, name):
    n = len(a_list)
    cols = 256

    def body(*refs):
        bv = refs[n][...]
        for k in range(n):
            g = _dot_tn(refs[k][...], bv)
            refs[n + 1 + k][...] = g
            refs[2 * n + 1 + k][...] = g.astype(BF16)

    res = _pcall(
        body, name=name, grid=(FF // cols,),
        in_specs=[pl.BlockSpec((SEQ, cols), lambda j: (0, j))] * n
        + [pl.BlockSpec((SEQ, D_MODEL), lambda j: (0, 0), pipeline_mode=pl.Buffered(1))],
        out_specs=[pl.BlockSpec((cols, D_MODEL), lambda j: (j, 0))] * (2 * n),
        out_shape=[_sds((FF, D_MODEL), F32)] * n + [_sds((FF, D_MODEL), BF16)] * n, args=(*a_list, b))
    return [m.reshape(N_SHARD, FF_S, D_MODEL) for m in res]


def _comm_only(name, comms):
    return _pcall(lambda: None, name=name, grid=(1,), in_specs=[], out_specs=[], out_shape=[], args=(),
                  comms=comms)[1]


def _adam_math(w, g, m, v):
    m = ADAM_B1 * m + (1.0 - ADAM_B1) * g
    v = ADAM_B2 * v + (1.0 - ADAM_B2) * (g * g)
    m_hat = m / (1.0 - ADAM_B1 ** ADAM_STEP)
    v_hat = v / (1.0 - ADAM_B2 ** ADAM_STEP)
    return -ADAM_LR * (m_hat / (jnp.sqrt(v_hat) + ADAM_EPS) + ADAM_WD * w), m, v


def _adamw_small(own, slots, w, m, v, me, after=()):
    rows, cols = own.shape

    def body(me_ref, own_ref, slots_ref, w_ref, m_ref, v_ref, *refs):
        g_ref, d_ref, nm_ref, nv_ref = refs[len(after):]
        own_v = own_ref[...].astype(F32)
        g = jnp.where(me_ref[0] == 0, own_v, slots_ref[0].astype(F32))
        for i in range(1, 8):
            g = g + jnp.where(me_ref[0] == i, own_v, slots_ref[i].astype(F32))
        g_ref[...] = g
        d_ref[...], nm_ref[...], nv_ref[...] = _adam_math(w_ref[...], g, m_ref[...], v_ref[...])

    flat = pl.BlockSpec((rows, cols), lambda i, me_ref: (0, 0))
    return pl.pallas_call(
        body, name="adamw_small",
        grid_spec=pltpu.PrefetchScalarGridSpec(
            num_scalar_prefetch=1, grid=(1,),
            in_specs=[flat, pl.BlockSpec((8, rows, cols), lambda i, me_ref: (0, 0, 0)), flat, flat, flat]
            + [ANY] * len(after),
            out_specs=[flat] * 4),
        out_shape=[_sds((rows, cols), F32)] * 4,
        compiler_params=_seq_params(),
    )(me, own, slots, w, m, v, *after)


def kernel(x, positions, pre_mix_norm, w_in, sgu_ln_gain, sgu_ln_bias, sgu_w_spatial, sgu_b_spatial, attn_out_norm, sgu_out_norm, w_out, post_mix_norm, pre_ffn_norm, w_gate, w_up, w_down, post_ffn_norm, loss_target, m_pre_mix_norm, m_w_in, m_sgu_ln_gain, m_sgu_ln_bias, m_sgu_w_spatial, m_sgu_b_spatial, m_attn_out_norm, m_sgu_out_norm, m_w_out, m_post_mix_norm, m_pre_ffn_norm, m_w_gate, m_w_up, m_w_down, m_post_ffn_norm, v_pre_mix_norm, v_w_in, v_sgu_ln_gain, v_sgu_ln_bias, v_sgu_w_spatial, v_sgu_b_spatial, v_attn_out_norm, v_sgu_out_norm, v_w_out, v_post_mix_norm, v_pre_ffn_norm, v_w_gate, v_w_up, v_w_down, v_post_ffn_norm):
    a = dict(locals())
    cx, cy, cc = _coords()
    s_me = 2 * cx + cy
    shard_core = jnp.stack([s_me, cc]).astype(jnp.int32)
    me = jnp.stack([4 * cx + 2 * cy + cc]).astype(jnp.int32)
    small = {n: (a[n][0] if a[n].ndim > 2 else a[n]) for n in SMALL}
    b_t = small["sgu_b_spatial"].T
    xs, target = x[0], loss_target[0]
    flipped = ("w_gate", "w_up")

    def big(name, n):
        return jnp.swapaxes(a[name], 1, 2)[0] if n in flipped else a[name][0]

    own = {"w_in": _halves(big("w_in", "w_in").astype(BF16))}

    def with_own(full, n):
        full = lax.dynamic_update_slice(full, own[n][None], (s_me, 0, 0, 0))
        return full.reshape((N_SHARD,) + big(n, n).shape)

    ffn = ("w_gate", "w_up", "w_down")
    g_in = _gather_ici([own["w_in"]])
    (s_in,), token = _split_starts("gather_in_start", [g_in], small["pre_mix_norm"])
    pos = positions.astype(F32).reshape(SEQ, 1) + token[0:1, 0:1]
    for n in ("w_out",) + ffn:
        own[n] = _halves((big(n, n) + token[0:1, 0:1]).astype(BF16))
    g_out, g_ffn = _gather_ici([own["w_out"]]), _gather_ici([own[n] for n in ffn])
    (s_out, s_ffn), token = _split_starts("gather_rest_start", [g_out, g_ffn], token)
    in_lands = _split_wait("gather_in_wait", g_in, *s_in, token)
    ((in_lands,),) = _comm_only("comm_pass_in", [_gather_pass(in_lands)])
    w_in_f = with_own(in_lands, "w_in")
    h, u, vs, sgu, *qkv = _inproj_fwd(xs, pos, small["pre_mix_norm"], w_in_f, small["sgu_ln_gain"],
                                      small["sgu_ln_bias"], small["sgu_w_spatial"], b_t)
    views = [tuple(qkv[3 * i:3 * i + 3]) for i in range(len(DILATIONS))]
    out_lands = _split_wait("gather_out_wait", g_out, *s_out, sgu)
    o_list, l_list = [], []
    for dil, (qv, kv, vv) in zip(DILATIONS, views):
        if dil == 1:
            (o, l), ((out_lands,),) = _attn_fwd(qv, kv, vv, dil, comms=[_gather_pass(out_lands)])
        else:
            o, l = _attn_fwd(qv, kv, vv, dil)
        o_list.append(o)
        l_list.append(l)
    w_out_f = with_own(out_lands, "w_out")
    ffn_lands = _split_wait("gather_ffn_wait", g_ffn, *s_ffn, l_list[-1])
    (attn, mixed, y, x1, *lses), (ffn_lands,) = _mix_out_fwd(
        o_list, l_list, sgu, xs, w_out_f, small["attn_out_norm"], small["sgu_out_norm"], small["post_mix_norm"],
        comms=[_gather_pass(ffn_lands)])
    w_gate_f, w_up_f, w_down_f = (with_own(f, n).reshape(FF, D_MODEL) for f, n in zip(ffn_lands, ffn))
    h2, act, dg, dup, df, dx1, loss_cols, d_pre_ffn, d_post_ffn = _ffn_fwd_bwd(
        x1, target, w_gate_f, w_up_f, w_down_f, small["pre_ffn_norm"], small["post_ffn_norm"])

    full_tok = pl.BlockSpec((SEQ, D_MODEL), lambda s: (0, 0), pipeline_mode=pl.Buffered(1))
    gw, gb = {}, {}
    gw["w_gate"], gw["w_up"], gb["w_gate"], gb["w_up"] = _wgrad_ff([dg, dup], h2, "wgrad_gate_up")
    gw["w_down"], gb["w_down"] = _wgrad_ff([act], df, "wgrad_down")
    (dy, du, dvs_sgu, d_post_mix, d_attn_norm, d_sgu_norm, d_w_sp, d_b_sp, d_ln_gain, d_ln_bias,
     *dviews), (sib_ffn,) = _outproj_bwd(
        dx1, y, attn, sgu, w_out_f, small["post_mix_norm"], small["attn_out_norm"], small["sgu_out_norm"],
        u, vs, small["sgu_ln_gain"], small["sgu_ln_bias"], small["sgu_w_spatial"], b_t,
        comms=[_rs_sibling([gb[n] for n in ffn])])
    sib = dict(zip(ffn, sib_ffn))
    part = dict(zip(ffn, _chip_sums([gb[n] for n in ffn], [sib[n] for n in ffn], shard_core)))
    gw["w_out"], gb["w_out"] = _wgrad(mixed, dy, pl.BlockSpec((SEQ, OUT_S), lambda s: (0, s)), full_tok,
                                      (OUT_S, D_MODEL), "wgrad_out")
    x_ffn = _rs_chips([part[n] for n in ffn])
    packed_early = _pack({
        "sgu_ln_gain": d_ln_gain, "sgu_ln_bias": d_ln_bias, "sgu_b_spatial": d_b_sp,
        "attn_out_norm": d_attn_norm, "sgu_out_norm": d_sgu_norm, "post_mix_norm": d_post_mix,
        "pre_ffn_norm": d_pre_ffn, "post_ffn_norm": d_post_ffn, LOSS_ROW: loss_cols}, SMALL_EARLY, 8)
    wsp_view = (SGU_GROUPS * CHUNK, CHUNK)
    packed_wsp = d_w_sp.reshape(wsp_view).astype(BF16)
    x_small, x_wsp = _small_exchange(packed_early), _small_exchange(packed_wsp)
    (s_ffn,), token = _split_starts("rs_ffn_start", [x_ffn], small["pre_mix_norm"])

    dqs, dks, dvs = [], [], []
    for i, (dil, (qv, kv, vv)) in enumerate(zip(DILATIONS, views)):
        if dil == 1:
            (dq, dk, dv), ((sib["w_out"],),) = _attn_bwd(qv, kv, vv, dviews[i], dviews[3 + i], lses[i], dil,
                                                        comms=[_rs_sibling([gb["w_out"]])], after=[token])
            (part["w_out"],) = _chip_sums([gb["w_out"]], [sib["w_out"]], shard_core)
            x_out = _rs_chips([part["w_out"]])
            (s_out, s_small, s_wsp), token = _split_starts("rs_out_small_start", [x_out, x_small, x_wsp], token)
        else:
            dq, dk, dv = _attn_bwd(qv, kv, vv, dviews[i], dviews[3 + i], lses[i], dil, after=[token])
        dqs.append(dq)
        dks.append(dk)
        dvs.append(dv)
    half, far, joined = {}, {}, {}
    far.update(zip(ffn, _split_wait("rs_ffn_wait", x_ffn, *s_ffn, dvs[-1])))
    half.update(zip(ffn, _final_sums([gw[n] for n in ffn], [sib[n] for n in ffn], [far[n] for n in ffn],
                                     shard_core)))
    dproj, grad_x, d_pre_mix = _inproj_bwd(dqs, dks, dvs, du, dvs_sgu, pos, xs, dx1, w_in_f, small["pre_mix_norm"])
    packed_late = _pack({"pre_mix_norm": d_pre_mix}, SMALL_LATE, 1)
    (gw["w_in"], gb["w_in"]), (got, (slots_late,)) = _wgrad(
        h, dproj, full_tok, pl.BlockSpec((SEQ, IN_S), lambda s: (0, s)), (D_MODEL, IN_S), "wgrad_in",
        comms=[_rs_join([half[n] for n in ffn]), _small_exchange(packed_late)])
    joined.update(zip(ffn, got))
    x_sib = _rs_sibling([gb["w_in"]])
    (s_sib,), token = _split_starts("rs_sibling_in_start", [x_sib], slots_late)
    (slots_early,) = _split_wait("small_early_wait", x_small, *s_small, token)
    grads, deltas, new_m, new_v = {}, {}, {}, {}
    a[LOSS_ROW] = a["m_" + LOSS_ROW] = a["v_" + LOSS_ROW] = jnp.zeros((1, D_MODEL), F32)
    small_outs = []
    for names, rows, packed, slots in ((SMALL_EARLY, 8, packed_early, slots_early),
                                       (SMALL_LATE, 1, packed_late, slots_late)):
        outs = _adamw_small(packed, slots, _pack(a, names, rows), _pack({n: a["m_" + n] for n in names}, names, rows),
                            _pack({n: a["v_" + n] for n in names}, names, rows), me, after=[token])
        small_outs += list(outs)
        for dst, buf in zip((grads, deltas, new_m, new_v), outs):
            dst.update(_unpack(buf, names, {n: a[n].shape for n in names}))
    (sib["w_in"],) = _split_wait("rs_sibling_in_wait", x_sib, *s_sib, small_outs)
    (part["w_in"],) = _chip_sums([gb["w_in"]], [sib["w_in"]], shard_core)
    x_in = _rs_chips([part["w_in"]])
    (s_in,), token = _split_starts("rs_in_start", [x_in], small["pre_mix_norm"])

    def record(n, outs):
        grads[n], deltas[n], new_m[n], new_v[n] = (
            jnp.swapaxes(o[None], 1, 2) if n in flipped else o[None] for o in outs)

    def update(names, name, after):
        for n, outs in zip(names, _adamw_multi(
                [big(n, n) for n in names], [joined[n].reshape(big(n, n).shape) for n in names],
                [big("m_" + n, n) for n in names], [big("v_" + n, n) for n in names], name, after)):
            record(n, outs)

    update(ffn, "adamw_ffn", [token])
    (far["w_out"],) = _split_wait("rs_out_wait", x_out, *s_out, new_v["w_down"])
    (slots_wsp,) = _split_wait("small_wsp_wait", x_wsp, *s_wsp, far["w_out"])
    (far["w_in"],) = _split_wait("rs_in_wait", x_in, *s_in, slots_wsp)
    last = ("w_in", "w_out")
    half.update(zip(last, _final_sums([gw[n] for n in last], [sib[n] for n in last], [far[n] for n in last],
                                      shard_core)))
    x_join = _rs_join([half[n] for n in last], split=True)
    (s_join,), token = _split_starts("rs_join_start", [x_join], slots_wsp)
    outs = _adamw_small(packed_wsp, slots_wsp, *[a[p + SMALL_WSP].reshape(wsp_view) for p in ("", "m_", "v_")], me,
                        after=[token])
    for dst, buf in zip((grads, deltas, new_m, new_v), outs):
        dst[SMALL_WSP] = buf.reshape(a[SMALL_WSP].shape)
    joined.update(zip(last, _split_wait("rs_join_wait", x_join, *s_join, list(outs))))
    update(last, "adamw_in_out", [])
    loss = jnp.sum(grads[LOSS_ROW]) * np.float32(0.5 / D_MODEL)
    return (loss, grad_x[None], *[grads[n] for n in WEIGHTS], *[deltas[n] for n in WEIGHTS],
            *[new_m[n] for n in WEIGHTS], *[new_v[n] for n in WEIGHTS])
```
